```python
import jax, jax.numpy as jnp
from jax import lax
import numpy as np

D_MODEL = 1024
BATCH = 16
SEQ = 2048
DEPTH = 4

N_MIXERS = 2
CONV_CH = D_MODEL
CONV_KERNEL = 31
GMLP_CHUNK = 128
GMLP_GROUPS = 8
GMLP_DFF = 2 * D_MODEL
GMLP_HALF = GMLP_DFF // 2
GMLP_GROUP_CH = GMLP_HALF // GMLP_GROUPS
FFN_HIDDEN = 2816
FFN_CONV = 3
DEEPNORM_ALPHA = (2.0 * DEPTH) ** 0.25
DEEPNORM_BETA = (8.0 * DEPTH) ** -0.25
LN_EPS = 1e-5
N_CONV_LAYERS = (DEPTH + 1) // 2
N_GMLP_LAYERS = DEPTH // 2

kernel_name = "hybrid_conformer_gmlp_deepnorm"


def layer_norm(x, g, b):
    xf = x.astype(jnp.float32)
    mu = jnp.mean(xf, axis=-1, keepdims=True)
    xc = xf - mu
    var = jnp.mean(xc * xc, axis=-1, keepdims=True)
    y = xc * lax.rsqrt(var + LN_EPS)
    return (y * g.astype(jnp.float32) + b.astype(jnp.float32)).astype(x.dtype)


def causal_depthwise_conv(x, w, b):
    k, c = w.shape
    y = lax.conv_general_dilated(
        x, w[:, None, :].astype(x.dtype), window_strides=(1,), padding=[(k - 1, 0)],
        dimension_numbers=("NWC", "WIO", "NWC"), feature_group_count=c)
    return y + b


def conformer_conv_module(x, w_in, b_in, w_dw, b_dw, ln_g, ln_b, w_out, b_out):
    h = x @ w_in + b_in
    a, gate = jnp.split(h, 2, axis=-1)
    h = a * jax.nn.sigmoid(gate)
    h = causal_depthwise_conv(h, w_dw, b_dw)
    h = jax.nn.silu(layer_norm(h, ln_g, ln_b))
    return h @ w_out + b_out


def chunked_spatial_gating(x, w_in, b_in, ln_g, ln_b, w_s, b_s, w_out, b_out):
    bsz, t, _ = x.shape
    z = jax.nn.gelu(x @ w_in + b_in, approximate=False)
    u, v = jnp.split(z, 2, axis=-1)
    v = layer_norm(v, ln_g, ln_b)
    n_chunks = t // GMLP_CHUNK
    v = v.reshape(bsz, n_chunks, GMLP_CHUNK, GMLP_GROUPS, GMLP_GROUP_CH)
    mask = jnp.tril(jnp.ones((GMLP_CHUNK, GMLP_CHUNK), dtype=bool))
    w_causal = jnp.where(mask[None], w_s, jnp.zeros((), w_s.dtype))
    s = jnp.einsum("gts,bnsgc->bntgc", w_causal, v) + b_s.T[:, :, None]
    s = s.reshape(bsz, t, GMLP_HALF)
    return (u * s) @ w_out + b_out


def conv_ffn(x, w_up, b_up, w_dw, b_dw, w_down, b_down):
    h = x @ w_up + b_up
    h = causal_depthwise_conv(h, w_dw, b_dw)
    g, val = jnp.split(h, 2, axis=-1)
    return (jax.nn.silu(g) * val) @ w_down + b_down


def _fwd_setup_inputs(seed: int = 0) -> dict:
    key = jax.random.key(seed)
    ks = iter(jax.random.split(key, 32))

    def nrm(shape, std):
        return jax.random.normal(next(ks), shape, jnp.float32) * std

    def xavier_out(shape):
        fan_in, fan_out = shape[-2], shape[-1]
        return nrm(shape, DEEPNORM_BETA * (2.0 / (fan_in + fan_out)) ** 0.5)

    na, nb, d = N_CONV_LAYERS, N_GMLP_LAYERS, D_MODEL
    inp = {}
    inp["x"] = nrm((BATCH, SEQ, d), 1.0)
    inp["conv_w_in"] = nrm((na, d, 2 * CONV_CH), d ** -0.5)
    inp["conv_b_in"] = nrm((na, 2 * CONV_CH), 0.02)
    inp["conv_w_dw"] = nrm((na, CONV_KERNEL, CONV_CH), CONV_KERNEL ** -0.5)
    inp["conv_b_dw"] = nrm((na, CONV_CH), 0.02)
    inp["conv_ln_g"] = 1.0 + nrm((na, CONV_CH), 0.05)
    inp["conv_ln_b"] = nrm((na, CONV_CH), 0.02)
    inp["conv_w_out"] = xavier_out((na, CONV_CH, d))
    inp["conv_b_out"] = nrm((na, d), 0.02)
    inp["gmlp_w_in"] = nrm((nb, d, GMLP_DFF), d ** -0.5)
    inp["gmlp_b_in"] = nrm((nb, GMLP_DFF), 0.02)
    inp["gmlp_ln_g"] = 1.0 + nrm((nb, GMLP_HALF), 0.05)
    inp["gmlp_ln_b"] = nrm((nb, GMLP_HALF), 0.02)
    inp["gmlp_w_s"] = nrm((nb, GMLP_GROUPS, GMLP_CHUNK, GMLP_CHUNK), 0.5 * GMLP_CHUNK ** -0.5)
    inp["gmlp_b_s"] = 1.0 + nrm((nb, GMLP_GROUPS, GMLP_CHUNK), 0.1)
    inp["gmlp_w_out"] = xavier_out((nb, GMLP_HALF, d))
    inp["gmlp_b_out"] = nrm((nb, d), 0.02)
    inp["ffn_w_up"] = nrm((DEPTH, d, 2 * FFN_HIDDEN), d ** -0.5)
    inp["ffn_b_up"] = nrm((DEPTH, 2 * FFN_HIDDEN), 0.02)
    inp["ffn_w_dw"] = nrm((DEPTH, FFN_CONV, 2 * FFN_HIDDEN), FFN_CONV ** -0.5)
    inp["ffn_b_dw"] = nrm((DEPTH, 2 * FFN_HIDDEN), 0.02)
    inp["ffn_w_down"] = xavier_out((DEPTH, FFN_HIDDEN, d))
    inp["ffn_b_down"] = nrm((DEPTH, d), 0.02)
    inp["norm1_g"] = 1.0 + nrm((DEPTH, d), 0.05)
    inp["norm1_b"] = nrm((DEPTH, d), 0.02)
    inp["norm2_g"] = 1.0 + nrm((DEPTH, d), 0.05)
    inp["norm2_b"] = nrm((DEPTH, d), 0.02)
    return inp


def _fwd_reference(x, conv_w_in, conv_b_in, conv_w_dw, conv_b_dw, conv_ln_g, conv_ln_b,
              conv_w_out, conv_b_out, gmlp_w_in, gmlp_b_in, gmlp_ln_g, gmlp_ln_b,
              gmlp_w_s, gmlp_b_s, gmlp_w_out, gmlp_b_out, ffn_w_up, ffn_b_up,
              ffn_w_dw, ffn_b_dw, ffn_w_down, ffn_b_down, norm1_g, norm1_b,
              norm2_g, norm2_b):
    for i in range(DEPTH):
        j = i // N_MIXERS
        if i % N_MIXERS == 0:
            y = conformer_conv_module(x, conv_w_in[j], conv_b_in[j], conv_w_dw[j], conv_b_dw[j],
                                      conv_ln_g[j], conv_ln_b[j], conv_w_out[j], conv_b_out[j])
        else:
            y = chunked_spatial_gating(x, gmlp_w_in[j], gmlp_b_in[j], gmlp_ln_g[j], gmlp_ln_b[j],
                                       gmlp_w_s[j], gmlp_b_s[j], gmlp_w_out[j], gmlp_b_out[j])
        x = layer_norm(DEEPNORM_ALPHA * x + y, norm1_g[i], norm1_b[i])
        y = conv_ffn(x, ffn_w_up[i], ffn_b_up[i], ffn_w_dw[i], ffn_b_dw[i],
                     ffn_w_down[i], ffn_b_down[i])
        x = layer_norm(DEEPNORM_ALPHA * x + y, norm2_g[i], norm2_b[i])
    return x


import jax as _jax
import jax.numpy as _jnp

TWIN_FORMAT = 'train_step'
FWD_PARAMS = ['x', 'conv_w_in', 'conv_b_in', 'conv_w_dw', 'conv_b_dw', 'conv_ln_g', 'conv_ln_b', 'conv_w_out', 'conv_b_out', 'gmlp_w_in', 'gmlp_b_in', 'gmlp_ln_g', 'gmlp_ln_b', 'gmlp_w_s', 'gmlp_b_s', 'gmlp_w_out', 'gmlp_b_out', 'ffn_w_up', 'ffn_b_up', 'ffn_w_dw', 'ffn_b_dw', 'ffn_w_down', 'ffn_b_down', 'norm1_g', 'norm1_b', 'norm2_g', 'norm2_b']
TWIN_WEIGHTS = ['conv_w_in', 'conv_b_in', 'conv_w_dw', 'conv_b_dw', 'conv_ln_g', 'conv_ln_b', 'conv_w_out', 'conv_b_out', 'gmlp_w_in', 'gmlp_b_in', 'gmlp_ln_g', 'gmlp_ln_b', 'gmlp_w_s', 'gmlp_b_s', 'gmlp_w_out', 'gmlp_b_out', 'ffn_w_up', 'ffn_b_up', 'ffn_w_dw', 'ffn_b_dw', 'ffn_w_down', 'ffn_b_down', 'norm1_g', 'norm1_b', 'norm2_g', 'norm2_b']
TWIN_DIFF_INPUT = 'x'
TWIN_INPUTS = ['x', 'conv_w_in', 'conv_b_in', 'conv_w_dw', 'conv_b_dw', 'conv_ln_g', 'conv_ln_b', 'conv_w_out', 'conv_b_out', 'gmlp_w_in', 'gmlp_b_in', 'gmlp_ln_g', 'gmlp_ln_b', 'gmlp_w_s', 'gmlp_b_s', 'gmlp_w_out', 'gmlp_b_out', 'ffn_w_up', 'ffn_b_up', 'ffn_w_dw', 'ffn_b_dw', 'ffn_w_down', 'ffn_b_down', 'norm1_g', 'norm1_b', 'norm2_g', 'norm2_b', 'loss_target', 'm_conv_w_in', 'm_conv_b_in', 'm_conv_w_dw', 'm_conv_b_dw', 'm_conv_ln_g', 'm_conv_ln_b', 'm_conv_w_out', 'm_conv_b_out', 'm_gmlp_w_in', 'm_gmlp_b_in', 'm_gmlp_ln_g', 'm_gmlp_ln_b', 'm_gmlp_w_s', 'm_gmlp_b_s', 'm_gmlp_w_out', 'm_gmlp_b_out', 'm_ffn_w_up', 'm_ffn_b_up', 'm_ffn_w_dw', 'm_ffn_b_dw', 'm_ffn_w_down', 'm_ffn_b_down', 'm_norm1_g', 'm_norm1_b', 'm_norm2_g', 'm_norm2_b', 'v_conv_w_in', 'v_conv_b_in', 'v_conv_w_dw', 'v_conv_b_dw', 'v_conv_ln_g', 'v_conv_ln_b', 'v_conv_w_out', 'v_conv_b_out', 'v_gmlp_w_in', 'v_gmlp_b_in', 'v_gmlp_ln_g', 'v_gmlp_ln_b', 'v_gmlp_w_s', 'v_gmlp_b_s', 'v_gmlp_w_out', 'v_gmlp_b_out', 'v_ffn_w_up', 'v_ffn_b_up', 'v_ffn_w_dw', 'v_ffn_b_dw', 'v_ffn_w_down', 'v_ffn_b_down', 'v_norm1_g', 'v_norm1_b', 'v_norm2_g', 'v_norm2_b']
TWIN_OUTPUTS = ['loss', 'grad_x', 'grad_conv_w_in', 'grad_conv_b_in', 'grad_conv_w_dw', 'grad_conv_b_dw', 'grad_conv_ln_g', 'grad_conv_ln_b', 'grad_conv_w_out', 'grad_conv_b_out', 'grad_gmlp_w_in', 'grad_gmlp_b_in', 'grad_gmlp_ln_g', 'grad_gmlp_ln_b', 'grad_gmlp_w_s', 'grad_gmlp_b_s', 'grad_gmlp_w_out', 'grad_gmlp_b_out', 'grad_ffn_w_up', 'grad_ffn_b_up', 'grad_ffn_w_dw', 'grad_ffn_b_dw', 'grad_ffn_w_down', 'grad_ffn_b_down', 'grad_norm1_g', 'grad_norm1_b', 'grad_norm2_g', 'grad_norm2_b', 'delta_conv_w_in', 'delta_conv_b_in', 'delta_conv_w_dw', 'delta_conv_b_dw', 'delta_conv_ln_g', 'delta_conv_ln_b', 'delta_conv_w_out', 'delta_conv_b_out', 'delta_gmlp_w_in', 'delta_gmlp_b_in', 'delta_gmlp_ln_g', 'delta_gmlp_ln_b', 'delta_gmlp_w_s', 'delta_gmlp_b_s', 'delta_gmlp_w_out', 'delta_gmlp_b_out', 'delta_ffn_w_up', 'delta_ffn_b_up', 'delta_ffn_w_dw', 'delta_ffn_b_dw', 'delta_ffn_w_down', 'delta_ffn_b_down', 'delta_norm1_g', 'delta_norm1_b', 'delta_norm2_g', 'delta_norm2_b', 'new_m_conv_w_in', 'new_m_conv_b_in', 'new_m_conv_w_dw', 'new_m_conv_b_dw', 'new_m_conv_ln_g', 'new_m_conv_ln_b', 'new_m_conv_w_out', 'new_m_conv_b_out', 'new_m_gmlp_w_in', 'new_m_gmlp_b_in', 'new_m_gmlp_ln_g', 'new_m_gmlp_ln_b', 'new_m_gmlp_w_s', 'new_m_gmlp_b_s', 'new_m_gmlp_w_out', 'new_m_gmlp_b_out', 'new_m_ffn_w_up', 'new_m_ffn_b_up', 'new_m_ffn_w_dw', 'new_m_ffn_b_dw', 'new_m_ffn_w_down', 'new_m_ffn_b_down', 'new_m_norm1_g', 'new_m_norm1_b', 'new_m_norm2_g', 'new_m_norm2_b', 'new_v_conv_w_in', 'new_v_conv_b_in', 'new_v_conv_w_dw', 'new_v_conv_b_dw', 'new_v_conv_ln_g', 'new_v_conv_ln_b', 'new_v_conv_w_out', 'new_v_conv_b_out', 'new_v_gmlp_w_in', 'new_v_gmlp_b_in', 'new_v_gmlp_ln_g', 'new_v_gmlp_ln_b', 'new_v_gmlp_w_s', 'new_v_gmlp_b_s', 'new_v_gmlp_w_out', 'new_v_gmlp_b_out', 'new_v_ffn_w_up', 'new_v_ffn_b_up', 'new_v_ffn_w_dw', 'new_v_ffn_b_dw', 'new_v_ffn_w_down', 'new_v_ffn_b_down', 'new_v_norm1_g', 'new_v_norm1_b', 'new_v_norm2_g', 'new_v_norm2_b']
TWIN_LEAF_KINDS = {'loss': 'loss', 'grad_x': 'grad_x', 'grad_conv_w_in': 'grad_w', 'grad_conv_b_in': 'grad_w', 'grad_conv_w_dw': 'grad_w', 'grad_conv_b_dw': 'grad_w', 'grad_conv_ln_g': 'grad_w', 'grad_conv_ln_b': 'grad_w', 'grad_conv_w_out': 'grad_w', 'grad_conv_b_out': 'grad_w', 'grad_gmlp_w_in': 'grad_w', 'grad_gmlp_b_in': 'grad_w', 'grad_gmlp_ln_g': 'grad_w', 'grad_gmlp_ln_b': 'grad_w', 'grad_gmlp_w_s': 'grad_w', 'grad_gmlp_b_s': 'grad_w', 'grad_gmlp_w_out': 'grad_w', 'grad_gmlp_b_out': 'grad_w', 'grad_ffn_w_up': 'grad_w', 'grad_ffn_b_up': 'grad_w', 'grad_ffn_w_dw': 'grad_w', 'grad_ffn_b_dw': 'grad_w', 'grad_ffn_w_down': 'grad_w', 'grad_ffn_b_down': 'grad_w', 'grad_norm1_g': 'grad_w', 'grad_norm1_b': 'grad_w', 'grad_norm2_g': 'grad_w', 'grad_norm2_b': 'grad_w', 'delta_conv_w_in': 'delta_w', 'delta_conv_b_in': 'delta_w', 'delta_conv_w_dw': 'delta_w', 'delta_conv_b_dw': 'delta_w', 'delta_conv_ln_g': 'delta_w', 'delta_conv_ln_b': 'delta_w', 'delta_conv_w_out': 'delta_w', 'delta_conv_b_out': 'delta_w', 'delta_gmlp_w_in': 'delta_w', 'delta_gmlp_b_in': 'delta_w', 'delta_gmlp_ln_g': 'delta_w', 'delta_gmlp_ln_b': 'delta_w', 'delta_gmlp_w_s': 'delta_w', 'delta_gmlp_b_s': 'delta_w', 'delta_gmlp_w_out': 'delta_w', 'delta_gmlp_b_out': 'delta_w', 'delta_ffn_w_up': 'delta_w', 'delta_ffn_b_up': 'delta_w', 'delta_ffn_w_dw': 'delta_w', 'delta_ffn_b_dw': 'delta_w', 'delta_ffn_w_down': 'delta_w', 'delta_ffn_b_down': 'delta_w', 'delta_norm1_g': 'delta_w', 'delta_norm1_b': 'delta_w', 'delta_norm2_g': 'delta_w', 'delta_norm2_b': 'delta_w', 'new_m_conv_w_in': 'new_m', 'new_m_conv_b_in': 'new_m', 'new_m_conv_w_dw': 'new_m', 'new_m_conv_b_dw': 'new_m', 'new_m_conv_ln_g': 'new_m', 'new_m_conv_ln_b': 'new_m', 'new_m_conv_w_out': 'new_m', 'new_m_conv_b_out': 'new_m', 'new_m_gmlp_w_in': 'new_m', 'new_m_gmlp_b_in': 'new_m', 'new_m_gmlp_ln_g': 'new_m', 'new_m_gmlp_ln_b': 'new_m', 'new_m_gmlp_w_s': 'new_m', 'new_m_gmlp_b_s': 'new_m', 'new_m_gmlp_w_out': 'new_m', 'new_m_gmlp_b_out': 'new_m', 'new_m_ffn_w_up': 'new_m', 'new_m_ffn_b_up': 'new_m', 'new_m_ffn_w_dw': 'new_m', 'new_m_ffn_b_dw': 'new_m', 'new_m_ffn_w_down': 'new_m', 'new_m_ffn_b_down': 'new_m', 'new_m_norm1_g': 'new_m', 'new_m_norm1_b': 'new_m', 'new_m_norm2_g': 'new_m', 'new_m_norm2_b': 'new_m', 'new_v_conv_w_in': 'new_v', 'new_v_conv_b_in': 'new_v', 'new_v_conv_w_dw': 'new_v', 'new_v_conv_b_dw': 'new_v', 'new_v_conv_ln_g': 'new_v', 'new_v_conv_ln_b': 'new_v', 'new_v_conv_w_out': 'new_v', 'new_v_conv_b_out': 'new_v', 'new_v_gmlp_w_in': 'new_v', 'new_v_gmlp_b_in': 'new_v', 'new_v_gmlp_ln_g': 'new_v', 'new_v_gmlp_ln_b': 'new_v', 'new_v_gmlp_w_s': 'new_v', 'new_v_gmlp_b_s': 'new_v', 'new_v_gmlp_w_out': 'new_v', 'new_v_gmlp_b_out': 'new_v', 'new_v_ffn_w_up': 'new_v', 'new_v_ffn_b_up': 'new_v', 'new_v_ffn_w_dw': 'new_v', 'new_v_ffn_b_dw': 'new_v', 'new_v_ffn_w_down': 'new_v', 'new_v_ffn_b_down': 'new_v', 'new_v_norm1_g': 'new_v', 'new_v_norm1_b': 'new_v', 'new_v_norm2_g': 'new_v', 'new_v_norm2_b': 'new_v'}


def _forward(args):
    return _fwd_reference(*[args[k] for k in FWD_PARAMS])


def _output_shape():
    out = _jax.eval_shape(lambda: _forward(_fwd_setup_inputs(0)))
    return out.shape, out.dtype

N_MICROBATCH = 1
ADAM_LR = 0.001
ADAM_B1 = 0.9
ADAM_B2 = 0.999
ADAM_EPS = 1e-08
ADAM_WD = 0.01
ADAM_STEP = 10
PER_EXAMPLE_BATCH_AXIS = {'x': 0, 'loss_target': 0}
SHARED_INPUTS = []
_WEIGHT_DTYPES = {'conv_w_in': _jnp.float32, 'conv_b_in': _jnp.float32, 'conv_w_dw': _jnp.float32, 'conv_b_dw': _jnp.float32, 'conv_ln_g': _jnp.float32, 'conv_ln_b': _jnp.float32, 'conv_w_out': _jnp.float32, 'conv_b_out': _jnp.float32, 'gmlp_w_in': _jnp.float32, 'gmlp_b_in': _jnp.float32, 'gmlp_ln_g': _jnp.float32, 'gmlp_ln_b': _jnp.float32, 'gmlp_w_s': _jnp.float32, 'gmlp_b_s': _jnp.float32, 'gmlp_w_out': _jnp.float32, 'gmlp_b_out': _jnp.float32, 'ffn_w_up': _jnp.float32, 'ffn_b_up': _jnp.float32, 'ffn_w_dw': _jnp.float32, 'ffn_b_dw': _jnp.float32, 'ffn_w_down': _jnp.float32, 'ffn_b_down': _jnp.float32, 'norm1_g': _jnp.float32, 'norm1_b': _jnp.float32, 'norm2_g': _jnp.float32, 'norm2_b': _jnp.float32}
MOMENT_SCALE = {'conv_w_in': 2.204896e-02, 'conv_b_in': 4.354850e-02, 'conv_w_dw': 2.932682e-02, 'conv_b_dw': 9.901254e-02, 'conv_ln_g': 4.814773e-02, 'conv_ln_b': 6.033674e-02, 'conv_w_out': 8.270490e-02, 'conv_b_out': 3.023477e-01, 'gmlp_w_in': 2.528527e-02, 'gmlp_b_in': 4.930909e-02, 'gmlp_ln_g': 1.071940e-02, 'gmlp_ln_b': 1.046300e-02, 'gmlp_w_s': 2.108026e-02, 'gmlp_b_s': 2.967530e-02, 'gmlp_w_out': 1.026919e-01, 'gmlp_b_out': 3.067038e-01, 'ffn_w_up': 2.058652e-02, 'ffn_b_up': 2.351271e-02, 'ffn_w_dw': 2.046600e-02, 'ffn_b_dw': 2.276344e-02, 'ffn_w_down': 6.629022e-02, 'ffn_b_down': 3.106191e-01, 'norm1_g': 2.731213e+00, 'norm1_b': 5.189196e-01, 'norm2_g': 1.656532e+01, 'norm2_b': 1.996229e+00}


def _to_microbatches(a, axis):
    t = _jnp.moveaxis(a, axis, 0)
    t = t.reshape((N_MICROBATCH, t.shape[0] // N_MICROBATCH) + t.shape[1:])
    return _jnp.moveaxis(t, 1, axis + 1)


def setup_inputs(seed: int = 0) -> dict:
    inp = _fwd_setup_inputs(seed)
    key = _jax.random.fold_in(_jax.random.key(seed), 7919)
    shape, _ = _output_shape()
    out = dict(inp)
    out["loss_target"] = _jax.random.normal(_jax.random.fold_in(key, 0), shape, _jnp.float32)
    for i, name in enumerate(TWIN_WEIGHTS):
        w = inp[name].astype(_jnp.float32)
        if MOMENT_SCALE is None:
            s = _jnp.sqrt(_jnp.mean(_jnp.square(w)) + 1e-30)
        else:
            s = MOMENT_SCALE[name]
        km, kv = _jax.random.split(_jax.random.fold_in(key, i + 1))
        out[name] = w
        out["m_" + name] = s * _jax.random.normal(km, w.shape, _jnp.float32)
        out["v_" + name] = (s * s) * _jax.random.uniform(kv, w.shape, _jnp.float32, 0.5, 1.5)
    if N_MICROBATCH > 1:
        for name, axis in PER_EXAMPLE_BATCH_AXIS.items():
            out[name] = _to_microbatches(out[name], axis)
    return {'x': out['x'], 'conv_w_in': out['conv_w_in'], 'conv_b_in': out['conv_b_in'], 'conv_w_dw': out['conv_w_dw'], 'conv_b_dw': out['conv_b_dw'], 'conv_ln_g': out['conv_ln_g'], 'conv_ln_b': out['conv_ln_b'], 'conv_w_out': out['conv_w_out'], 'conv_b_out': out['conv_b_out'], 'gmlp_w_in': out['gmlp_w_in'], 'gmlp_b_in': out['gmlp_b_in'], 'gmlp_ln_g': out['gmlp_ln_g'], 'gmlp_ln_b': out['gmlp_ln_b'], 'gmlp_w_s': out['gmlp_w_s'], 'gmlp_b_s': out['gmlp_b_s'], 'gmlp_w_out': out['gmlp_w_out'], 'gmlp_b_out': out['gmlp_b_out'], 'ffn_w_up': out['ffn_w_up'], 'ffn_b_up': out['ffn_b_up'], 'ffn_w_dw': out['ffn_w_dw'], 'ffn_b_dw': out['ffn_b_dw'], 'ffn_w_down': out['ffn_w_down'], 'ffn_b_down': out['ffn_b_down'], 'norm1_g': out['norm1_g'], 'norm1_b': out['norm1_b'], 'norm2_g': out['norm2_g'], 'norm2_b': out['norm2_b'], 'loss_target': out['loss_target'], 'm_conv_w_in': out['m_conv_w_in'], 'm_conv_b_in': out['m_conv_b_in'], 'm_conv_w_dw': out['m_conv_w_dw'], 'm_conv_b_dw': out['m_conv_b_dw'], 'm_conv_ln_g': out['m_conv_ln_g'], 'm_conv_ln_b': out['m_conv_ln_b'], 'm_conv_w_out': out['m_conv_w_out'], 'm_conv_b_out': out['m_conv_b_out'], 'm_gmlp_w_in': out['m_gmlp_w_in'], 'm_gmlp_b_in': out['m_gmlp_b_in'], 'm_gmlp_ln_g': out['m_gmlp_ln_g'], 'm_gmlp_ln_b': out['m_gmlp_ln_b'], 'm_gmlp_w_s': out['m_gmlp_w_s'], 'm_gmlp_b_s': out['m_gmlp_b_s'], 'm_gmlp_w_out': out['m_gmlp_w_out'], 'm_gmlp_b_out': out['m_gmlp_b_out'], 'm_ffn_w_up': out['m_ffn_w_up'], 'm_ffn_b_up': out['m_ffn_b_up'], 'm_ffn_w_dw': out['m_ffn_w_dw'], 'm_ffn_b_dw': out['m_ffn_b_dw'], 'm_ffn_w_down': out['m_ffn_w_down'], 'm_ffn_b_down': out['m_ffn_b_down'], 'm_norm1_g': out['m_norm1_g'], 'm_norm1_b': out['m_norm1_b'], 'm_norm2_g': out['m_norm2_g'], 'm_norm2_b': out['m_norm2_b'], 'v_conv_w_in': out['v_conv_w_in'], 'v_conv_b_in': out['v_conv_b_in'], 'v_conv_w_dw': out['v_conv_w_dw'], 'v_conv_b_dw': out['v_conv_b_dw'], 'v_conv_ln_g': out['v_conv_ln_g'], 'v_conv_ln_b': out['v_conv_ln_b'], 'v_conv_w_out': out['v_conv_w_out'], 'v_conv_b_out': out['v_conv_b_out'], 'v_gmlp_w_in': out['v_gmlp_w_in'], 'v_gmlp_b_in': out['v_gmlp_b_in'], 'v_gmlp_ln_g': out['v_gmlp_ln_g'], 'v_gmlp_ln_b': out['v_gmlp_ln_b'], 'v_gmlp_w_s': out['v_gmlp_w_s'], 'v_gmlp_b_s': out['v_gmlp_b_s'], 'v_gmlp_w_out': out['v_gmlp_w_out'], 'v_gmlp_b_out': out['v_gmlp_b_out'], 'v_ffn_w_up': out['v_ffn_w_up'], 'v_ffn_b_up': out['v_ffn_b_up'], 'v_ffn_w_dw': out['v_ffn_w_dw'], 'v_ffn_b_dw': out['v_ffn_b_dw'], 'v_ffn_w_down': out['v_ffn_w_down'], 'v_ffn_b_down': out['v_ffn_b_down'], 'v_norm1_g': out['v_norm1_g'], 'v_norm1_b': out['v_norm1_b'], 'v_norm2_g': out['v_norm2_g'], 'v_norm2_b': out['v_norm2_b']}


def _loss(weights, diff, rest, loss_target):
    with _jax.named_scope("forward"):
        args = {**rest, TWIN_DIFF_INPUT: diff, **{k: w.astype(_WEIGHT_DTYPES[k]) for k, w in weights.items()}}
        y = _forward(args)
    with _jax.named_scope("loss_head"):
        err = _jnp.square(y.astype(_jnp.float32) - loss_target)
        return 0.5 * _jnp.sum(_jnp.mean(err, axis=-1)) if err.ndim else 0.5 * err


def _adamw(w, g, m, v):
    m = ADAM_B1 * m + (1.0 - ADAM_B1) * g
    v = ADAM_B2 * v + (1.0 - ADAM_B2) * _jnp.square(g)
    m_hat = m / (1.0 - ADAM_B1 ** ADAM_STEP)
    v_hat = v / (1.0 - ADAM_B2 ** ADAM_STEP)
    delta = -ADAM_LR * (m_hat / (_jnp.sqrt(v_hat) + ADAM_EPS) + ADAM_WD * w)
    return delta, m, v


def reference(x, conv_w_in, conv_b_in, conv_w_dw, conv_b_dw, conv_ln_g, conv_ln_b, conv_w_out, conv_b_out, gmlp_w_in, gmlp_b_in, gmlp_ln_g, gmlp_ln_b, gmlp_w_s, gmlp_b_s, gmlp_w_out, gmlp_b_out, ffn_w_up, ffn_b_up, ffn_w_dw, ffn_b_dw, ffn_w_down, ffn_b_down, norm1_g, norm1_b, norm2_g, norm2_b, loss_target, m_conv_w_in, m_conv_b_in, m_conv_w_dw, m_conv_b_dw, m_conv_ln_g, m_conv_ln_b, m_conv_w_out, m_conv_b_out, m_gmlp_w_in, m_gmlp_b_in, m_gmlp_ln_g, m_gmlp_ln_b, m_gmlp_w_s, m_gmlp_b_s, m_gmlp_w_out, m_gmlp_b_out, m_ffn_w_up, m_ffn_b_up, m_ffn_w_dw, m_ffn_b_dw, m_ffn_w_down, m_ffn_b_down, m_norm1_g, m_norm1_b, m_norm2_g, m_norm2_b, v_conv_w_in, v_conv_b_in, v_conv_w_dw, v_conv_b_dw, v_conv_ln_g, v_conv_ln_b, v_conv_w_out, v_conv_b_out, v_gmlp_w_in, v_gmlp_b_in, v_gmlp_ln_g, v_gmlp_ln_b, v_gmlp_w_s, v_gmlp_b_s, v_gmlp_w_out, v_gmlp_b_out, v_ffn_w_up, v_ffn_b_up, v_ffn_w_dw, v_ffn_b_dw, v_ffn_w_down, v_ffn_b_down, v_norm1_g, v_norm1_b, v_norm2_g, v_norm2_b):
    given = dict(x=x, conv_w_in=conv_w_in, conv_b_in=conv_b_in, conv_w_dw=conv_w_dw, conv_b_dw=conv_b_dw, conv_ln_g=conv_ln_g, conv_ln_b=conv_ln_b, conv_w_out=conv_w_out, conv_b_out=conv_b_out, gmlp_w_in=gmlp_w_in, gmlp_b_in=gmlp_b_in, gmlp_ln_g=gmlp_ln_g, gmlp_ln_b=gmlp_ln_b, gmlp_w_s=gmlp_w_s, gmlp_b_s=gmlp_b_s, gmlp_w_out=gmlp_w_out, gmlp_b_out=gmlp_b_out, ffn_w_up=ffn_w_up, ffn_b_up=ffn_b_up, ffn_w_dw=ffn_w_dw, ffn_b_dw=ffn_b_dw, ffn_w_down=ffn_w_down, ffn_b_down=ffn_b_down, norm1_g=norm1_g, norm1_b=norm1_b, norm2_g=norm2_g, norm2_b=norm2_b, loss_target=loss_target, m_conv_w_in=m_conv_w_in, m_conv_b_in=m_conv_b_in, m_conv_w_dw=m_conv_w_dw, m_conv_b_dw=m_conv_b_dw, m_conv_ln_g=m_conv_ln_g, m_conv_ln_b=m_conv_ln_b, m_conv_w_out=m_conv_w_out, m_conv_b_out=m_conv_b_out, m_gmlp_w_in=m_gmlp_w_in, m_gmlp_b_in=m_gmlp_b_in, m_gmlp_ln_g=m_gmlp_ln_g, m_gmlp_ln_b=m_gmlp_ln_b, m_gmlp_w_s=m_gmlp_w_s, m_gmlp_b_s=m_gmlp_b_s, m_gmlp_w_out=m_gmlp_w_out, m_gmlp_b_out=m_gmlp_b_out, m_ffn_w_up=m_ffn_w_up, m_ffn_b_up=m_ffn_b_up, m_ffn_w_dw=m_ffn_w_dw, m_ffn_b_dw=m_ffn_b_dw, m_ffn_w_down=m_ffn_w_down, m_ffn_b_down=m_ffn_b_down, m_norm1_g=m_norm1_g, m_norm1_b=m_norm1_b, m_norm2_g=m_norm2_g, m_norm2_b=m_norm2_b, v_conv_w_in=v_conv_w_in, v_conv_b_in=v_conv_b_in, v_conv_w_dw=v_conv_w_dw, v_conv_b_dw=v_conv_b_dw, v_conv_ln_g=v_conv_ln_g, v_conv_ln_b=v_conv_ln_b, v_conv_w_out=v_conv_w_out, v_conv_b_out=v_conv_b_out, v_gmlp_w_in=v_gmlp_w_in, v_gmlp_b_in=v_gmlp_b_in, v_gmlp_ln_g=v_gmlp_ln_g, v_gmlp_ln_b=v_gmlp_ln_b, v_gmlp_w_s=v_gmlp_w_s, v_gmlp_b_s=v_gmlp_b_s, v_gmlp_w_out=v_gmlp_w_out, v_gmlp_b_out=v_gmlp_b_out, v_ffn_w_up=v_ffn_w_up, v_ffn_b_up=v_ffn_b_up, v_ffn_w_dw=v_ffn_w_dw, v_ffn_b_dw=v_ffn_b_dw, v_ffn_w_down=v_ffn_w_down, v_ffn_b_down=v_ffn_b_down, v_norm1_g=v_norm1_g, v_norm1_b=v_norm1_b, v_norm2_g=v_norm2_g, v_norm2_b=v_norm2_b)
    weights = {n: given[n] for n in TWIN_WEIGHTS}
    shared = {n: given[n] for n in SHARED_INPUTS}
    per_example = {n: given[n] for n in ['x']}
    grad_fn = _jax.value_and_grad(_loss, argnums=(0, 1))

    def one_microbatch(ex, loss_target):
        ex = dict(ex)
        diff = ex.pop(TWIN_DIFF_INPUT)
        return grad_fn(weights, diff, {**shared, **ex}, loss_target)

    if N_MICROBATCH == 1:
        loss, (grad_w, grad_x) = one_microbatch(per_example, given["loss_target"])
    else:
        def body(carry, xs):
            loss_sum, grad_sum = carry
            l_k, (gw_k, gx_k) = one_microbatch(xs[0], xs[1])
            with _jax.named_scope("update"):
                return (loss_sum + l_k, _jax.tree.map(_jnp.add, grad_sum, gw_k)), gx_k

        init = (_jnp.zeros((), _jnp.float32), _jax.tree.map(_jnp.zeros_like, weights))
        (loss, grad_w), grad_x = _jax.lax.scan(body, init, (per_example, given["loss_target"]))
    with _jax.named_scope("update"):
        delta_w, new_m, new_v = {}, {}, {}
        for n in TWIN_WEIGHTS:
            delta_w[n], new_m[n], new_v[n] = _adamw(weights[n], grad_w[n], given["m_" + n], given["v_" + n])
    return (loss, grad_x, *[grad_w[n] for n in TWIN_WEIGHTS], *[delta_w[n] for n in TWIN_WEIGHTS],
            *[new_m[n] for n in TWIN_WEIGHTS], *[new_v[n] for n in TWIN_WEIGHTS])
```

```python
import functools

import jax
import jax.numpy as jnp
from jax import lax
from jax.experimental import pallas as pl
from jax.experimental.pallas import tpu as pltpu

F32 = jnp.float32
_MXU = jnp.bfloat16
_WIRE = jnp.bfloat16
_HDT = jnp.bfloat16
LN_EPS = 1e-5
ADAM_LR, ADAM_B1, ADAM_B2, ADAM_EPS, ADAM_WD, ADAM_STEP = 0.001, 0.9, 0.999, 1e-08, 0.01, 10
N_CHIPS = 4
N_DEV = 8
LANES = 128
SUBLANES = 8
CONV_TAPS_PAD = 32
VMEM_LIMIT = 56 << 20
MESH = pl.DeviceIdType.MESH
ANY = pl.BlockSpec(memory_space=pl.ANY)
PERM = (0, 2, 1, 3)


def _cp(sem=None):
    return pltpu.CompilerParams(dimension_semantics=sem, vmem_limit_bytes=VMEM_LIMIT)


def _tile(dim, pref, mult=SUBLANES):
    if dim <= pref:
        return dim
    t = (pref // mult) * mult
    while t > mult and dim % t:
        t -= mult
    assert dim % t == 0, (dim, pref, mult)
    return t


def _perm_idx(q):
    return (q % 2) * 2 + q // 2


def _fold8(t):
    r, n = t.shape
    return t.reshape(r // SUBLANES, SUBLANES, n).sum(axis=0)


def _ln_rows(z, g, b):
    mu = jnp.mean(z, axis=-1, keepdims=True)
    xc = z - mu
    var = jnp.mean(xc * xc, axis=-1, keepdims=True)
    rstd = lax.rsqrt(var + LN_EPS)
    xh = xc * rstd
    return xh * g + b, xh, rstd


def _ln_bwd_rows(dy, xh, rstd, g):
    dxh = dy * g
    m1 = jnp.mean(dxh, axis=-1, keepdims=True)
    m2 = jnp.mean(dxh * xh, axis=-1, keepdims=True)
    return rstd * (dxh - m1 - xh * m2)


def _sigmoid(v):
    return 1.0 / (1.0 + jnp.exp(-v))


def _gelu_parts(p):
    cdf = 0.5 * (1.0 + lax.erf(p * 0.7071067811865476))
    pdf = jnp.exp(-0.5 * p * p) * 0.3989422804014327
    return p * cdf, cdf + p * pdf


def _shift_down(prev8, t, s):
    ext = jnp.concatenate([prev8, t], axis=0)
    return pltpu.roll(ext, s, 0)[SUBLANES:]


def _shift_up(t, next8, s):
    n = t.shape[0]
    ext = jnp.concatenate([t, next8], axis=0)
    return pltpu.roll(ext, n + SUBLANES - s, 0)[:n]


def _mm(a, b, *, ta=False, tb=False, bl=None, bias=None, res=None, res_scale=1.0, out_dtype=F32,
        tm, tn, tk, name, pieces=None):
    M, K = (a.shape[1], a.shape[0]) if ta else a.shape
    bs = b.shape[1:] if bl is not None else b.shape
    N, Kb = (bs[0], bs[1]) if tb else (bs[1], bs[0])
    assert K == Kb and M % tm == 0 and N % tn == 0 and K % tk == 0, (a.shape, b.shape, tm, tn, tk)
    gm, gn, gk = M // tm, N // tn, K // tk
    a_spec = pl.BlockSpec((tk, tm), lambda i, j, k: (k, i)) if ta else pl.BlockSpec((tm, tk), lambda i, j, k: (i, k))
    bblk = (tn, tk) if tb else (tk, tn)
    bmap = (lambda i, j, k: (j, k)) if tb else (lambda i, j, k: (k, j))
    if bl is not None:
        b_spec = pl.BlockSpec((None,) + bblk, lambda i, j, k: (bl,) + bmap(i, j, k))
    else:
        b_spec = pl.BlockSpec(bblk, bmap)
    in_specs, operands = [a_spec, b_spec], [a, b]
    if bias is not None:
        in_specs.append(pl.BlockSpec((1, tn), lambda i, j, k: (0, j)))
        operands.append(bias)
    if res is not None:
        in_specs.append(pl.BlockSpec((tm, tn), lambda i, j, k: (i, j)))
        operands.append(res)
    if pieces is None:
        out_shape = jax.ShapeDtypeStruct((M, N), out_dtype)
        out_spec = pl.BlockSpec((tm, tn), lambda i, j, k: (i, j))
        ppb = pr = None
    elif pieces[0] == 'col':
        pr, pc = M // 2, N // N_CHIPS
        assert tm == pr and pc % tn == 0
        ppb, per = 1, pc // tn
        perm = pieces[1]
        out_shape = jax.ShapeDtypeStruct((N_DEV, pr, pc), out_dtype)
        out_spec = pl.BlockSpec(
            (1, pr, tn), lambda i, j, k: (2 * (_perm_idx(j // per) if perm else j // per) + i, 0, j % per))
    else:
        pr = M // N_DEV
        assert tm % pr == 0
        ppb = tm // pr
        out_shape = jax.ShapeDtypeStruct((N_DEV, pr, N), out_dtype)
        out_spec = pl.BlockSpec((ppb, pr, tn), lambda i, j, k: (i, 0, j))
    dims = (((0 if ta else 1,), (1 if tb else 0,)), ((), ()))

    def body(*refs):
        a_ref, b_ref = refs[0], refs[1]
        pos = 2
        bias_ref = res_ref = None
        if bias is not None:
            bias_ref = refs[pos]
            pos += 1
        if res is not None:
            res_ref = refs[pos]
            pos += 1
        o_ref, acc_ref = refs[pos], refs[pos + 1]
        k = pl.program_id(2)

        @pl.when(k == 0)
        def _():
            acc_ref[...] = jnp.zeros_like(acc_ref)

        acc_ref[...] += lax.dot_general(a_ref[...].astype(_MXU), b_ref[...].astype(_MXU), dims,
                                        preferred_element_type=F32)

        @pl.when(k == gk - 1)
        def _():
            r = acc_ref[...]
            if bias_ref is not None:
                r = r + bias_ref[...]
            if res_ref is not None:
                r = r + res_scale * res_ref[...]
            if pieces is not None:
                r = r.reshape(ppb, pr, tn)
            o_ref[...] = r.astype(out_dtype)

    return pl.pallas_call(
        body, name=name, grid=(gm, gn, gk), in_specs=in_specs, out_specs=out_spec, out_shape=out_shape,
        scratch_shapes=[pltpu.VMEM((tm, tn), F32)],
        compiler_params=_cp(("parallel", "parallel", "arbitrary")),
    )(*operands)


def _mm_res_ln(a, w, wl, bias, res, alpha, g, b, *, name):
    T, K = a.shape
    D = w.shape[-1]
    tm = _tile(T, 256)

    def body(a_ref, w_ref, bias_ref, res_ref, g_ref, b_ref, y_ref, yb_ref, xh_ref, rs_ref):
        z = jnp.dot(a_ref[...].astype(_MXU), w_ref[...].astype(_MXU), preferred_element_type=F32)
        z = z + bias_ref[...] + alpha * res_ref[...]
        y, xh, rstd = _ln_rows(z, g_ref[...], b_ref[...])
        y_ref[...] = y
        yb_ref[...] = y.astype(_MXU)
        xh_ref[...] = xh
        rs_ref[...] = rstd

    row = lambda i: (i, 0)
    vec = pl.BlockSpec((1, D), lambda i: (0, 0))
    return pl.pallas_call(
        body, name=name, grid=(T // tm,),
        in_specs=[pl.BlockSpec((tm, K), row), pl.BlockSpec((None, K, D), lambda i: (wl, 0, 0)), vec,
                  pl.BlockSpec((tm, D), row), vec, vec],
        out_specs=[pl.BlockSpec((tm, D), row), pl.BlockSpec((tm, D), row), pl.BlockSpec((tm, D), row),
                   pl.BlockSpec((tm, 1), row)],
        out_shape=[jax.ShapeDtypeStruct((T, D), F32), jax.ShapeDtypeStruct((T, D), _MXU),
                   jax.ShapeDtypeStruct((T, D), F32), jax.ShapeDtypeStruct((T, 1), F32)],
        compiler_params=_cp(("parallel",)),
    )(a, w, bias, res, g, b)


def _ln_bwd(dy, xh, rstd, g, *, name, target=None):
    T, D = dy.shape
    tm = _tile(T, 256)
    head = target is not None

    def body(*refs):
        if head:
            dy_ref, t_ref, xh_ref, rs_ref, g_ref, dz_ref, dzb_ref, dg_ref, db_ref, cs_ref, ls_ref = refs
        else:
            dy_ref, xh_ref, rs_ref, g_ref, dz_ref, dzb_ref, dg_ref, db_ref, cs_ref = refs
        i = pl.program_id(0)

        @pl.when(i == 0)
        def _():
            dg_ref[...] = jnp.zeros_like(dg_ref)
            db_ref[...] = jnp.zeros_like(db_ref)
            cs_ref[...] = jnp.zeros_like(cs_ref)
            if head:
                ls_ref[...] = jnp.zeros_like(ls_ref)

        d = dy_ref[...]
        if head:
            err = d - t_ref[...]
            ls_ref[...] += _fold8(err * err)
            d = err * (1.0 / D)
        xh = xh_ref[...]
        dz = _ln_bwd_rows(d, xh, rs_ref[...], g_ref[...])
        dz_ref[...] = dz
        dzb_ref[...] = dz.astype(_MXU)
        dg_ref[...] += _fold8(d * xh)
        db_ref[...] += _fold8(d)
        cs_ref[...] += _fold8(dz)

    row = lambda i: (i, 0)
    fixed = lambda i: (0, 0)
    tile = pl.BlockSpec((tm, D), row)
    part = pl.BlockSpec((SUBLANES, D), fixed)
    in_specs = [tile] + ([tile] if head else []) + [tile, pl.BlockSpec((tm, 1), row), pl.BlockSpec((1, D), fixed)]
    n_part = 4 if head else 3
    operands = [dy] + ([target] if head else []) + [xh, rstd, g]
    return pl.pallas_call(
        body, name=name, grid=(T // tm,), in_specs=in_specs,
        out_specs=[tile, tile] + [part] * n_part,
        out_shape=[jax.ShapeDtypeStruct((T, D), F32), jax.ShapeDtypeStruct((T, D), _MXU)]
        + [jax.ShapeDtypeStruct((SUBLANES, D), F32)] * n_part,
        compiler_params=_cp(("arbitrary",)),
    )(*operands)


def _conv_cols(C, tc):
    per = (C // 2) // tc
    return per, (lambda j: (j // per) * (2 * per) + j % per)


def _glu_shifted(a_ref, g_ref, p_ref, S):
    u = a_ref[...] * _sigmoid(g_ref[...])
    rows = lax.broadcasted_iota(jnp.int32, u.shape, 0)
    for r in range(SUBLANES):
        p_ref[r, 0:CONV_TAPS_PAD, :] = jnp.zeros((CONV_TAPS_PAD, u.shape[1]), F32)
        p_ref[r, CONV_TAPS_PAD:CONV_TAPS_PAD + S, :] = u if r == 0 else jnp.where(rows >= r, pltpu.roll(u, r, 0), 0.0)


def _conv_fwd(h1, w_dw, b_dw, *, B, S, name):
    C = w_dw.shape[1]
    taps = CONV_TAPS_PAD - 1
    tc = LANES
    ch = _tile(S, 128)
    per, col_a = _conv_cols(C, tc)

    def body(a_ref, g_ref, w_ref, b_ref, o_ref, p_ref):
        _glu_shifted(a_ref, g_ref, p_ref, S)

        def chunk(ci, carry):
            base = pl.multiple_of(ci * ch, ch)
            acc = jnp.zeros((ch, tc), F32) + b_ref[...]
            for k in range(taps):
                q, r = divmod(taps - 1 - k, SUBLANES)
                start = pl.multiple_of(base + (CONV_TAPS_PAD - SUBLANES * q), SUBLANES)
                acc = acc + w_ref[pl.ds(k, 1), :] * p_ref[r, pl.ds(start, ch), :]
            o_ref[pl.ds(base, ch), :] = acc
            return carry

        lax.fori_loop(0, S // ch, chunk, 0)

    return pl.pallas_call(
        body, name=name, grid=(B, C // tc),
        in_specs=[pl.BlockSpec((S, tc), lambda b, j: (b, col_a(j))),
                  pl.BlockSpec((S, tc), lambda b, j: (b, col_a(j) + per)),
                  pl.BlockSpec((CONV_TAPS_PAD, tc), lambda b, j: (0, j)),
                  pl.BlockSpec((1, tc), lambda b, j: (0, j))],
        out_specs=pl.BlockSpec((S, tc), lambda b, j: (b, j)),
        out_shape=jax.ShapeDtypeStruct((B * S, C), F32),
        scratch_shapes=[pltpu.VMEM((SUBLANES, S + CONV_TAPS_PAD, tc), F32)],
        compiler_params=_cp(("parallel", "parallel")),
    )(h1, h1, w_dw, b_dw)


def _conv_bwd(dd, h1, w_dw, *, B, S, name):
    C = w_dw.shape[1]
    taps = CONV_TAPS_PAD - 1
    tc = LANES
    ch = _tile(S, 128)
    per, col_a = _conv_cols(C, tc)

    def body(d_ref, a_ref, g_ref, w_ref, du_ref, dw_ref, db_ref, p_ref, q_ref):
        b = pl.program_id(1)

        @pl.when(b == 0)
        def _():
            dw_ref[...] = jnp.zeros_like(dw_ref)
            db_ref[...] = jnp.zeros_like(db_ref)

        _glu_shifted(a_ref, g_ref, p_ref, S)
        d = d_ref[...]
        rows = lax.broadcasted_iota(jnp.int32, d.shape, 0)
        for r in range(SUBLANES):
            q_ref[r, S:S + CONV_TAPS_PAD, :] = jnp.zeros((CONV_TAPS_PAD, tc), F32)
            q_ref[r, 0:S, :] = d if r == 0 else jnp.where(rows < S - r, pltpu.roll(d, S - r, 0), 0.0)
        db_ref[...] += _fold8(d)

        def chunk(ci, carry):
            base = pl.multiple_of(ci * ch, ch)
            dch = d_ref[pl.ds(base, ch), :]
            acc = jnp.zeros((ch, tc), F32)
            for k in range(taps):
                q, r = divmod(taps - 1 - k, SUBLANES)
                up = pl.multiple_of(base + SUBLANES * q, SUBLANES)
                acc = acc + w_ref[pl.ds(k, 1), :] * q_ref[r, pl.ds(up, ch), :]
                down = pl.multiple_of(base + (CONV_TAPS_PAD - SUBLANES * q), SUBLANES)
                dw_ref[k] += _fold8(dch * p_ref[r, pl.ds(down, ch), :])
            du_ref[pl.ds(base, ch), :] = acc
            return carry

        lax.fori_loop(0, S // ch, chunk, 0)

    return pl.pallas_call(
        body, name=name, grid=(C // tc, B),
        in_specs=[pl.BlockSpec((S, tc), lambda j, b: (b, j)),
                  pl.BlockSpec((S, tc), lambda j, b: (b, col_a(j))),
                  pl.BlockSpec((S, tc), lambda j, b: (b, col_a(j) + per)),
                  pl.BlockSpec((CONV_TAPS_PAD, tc), lambda j, b: (0, j))],
        out_specs=[pl.BlockSpec((S, tc), lambda j, b: (b, j)),
                   pl.BlockSpec((CONV_TAPS_PAD, SUBLANES, tc), lambda j, b: (0, 0, j)),
                   pl.BlockSpec((SUBLANES, tc), lambda j, b: (0, j))],
        out_shape=[jax.ShapeDtypeStruct((B * S, C), F32),
                   jax.ShapeDtypeStruct((CONV_TAPS_PAD, SUBLANES, C), F32),
                   jax.ShapeDtypeStruct((SUBLANES, C), F32)],
        scratch_shapes=[pltpu.VMEM((SUBLANES, S + CONV_TAPS_PAD, tc), F32),
                        pltpu.VMEM((SUBLANES, S + CONV_TAPS_PAD, tc), F32)],
        compiler_params=_cp(("parallel", "arbitrary")),
    )(dd, h1, h1, w_dw)


def _ln_silu_fwd(v, g, b, *, name):
    T, C = v.shape
    tm = _tile(T, 512)

    def body(v_ref, g_ref, b_ref, s_ref, xh_ref, rs_ref):
        y, xh, rstd = _ln_rows(v_ref[...], g_ref[...], b_ref[...])
        s_ref[...] = (y * _sigmoid(y)).astype(_MXU)
        xh_ref[...] = xh
        rs_ref[...] = rstd

    row = lambda i: (i, 0)
    vec = pl.BlockSpec((1, C), lambda i: (0, 0))
    return pl.pallas_call(
        body, name=name, grid=(T // tm,),
        in_specs=[pl.BlockSpec((tm, C), row), vec, vec],
        out_specs=[pl.BlockSpec((tm, C), row), pl.BlockSpec((tm, C), row), pl.BlockSpec((tm, 1), row)],
        out_shape=[jax.ShapeDtypeStruct((T, C), _MXU), jax.ShapeDtypeStruct((T, C), F32),
                   jax.ShapeDtypeStruct((T, 1), F32)],
        compiler_params=_cp(("parallel",)),
    )(v, g, b)


def _ln_silu_bwd(ds, xh, rstd, g, b, *, name):
    T, C = ds.shape
    tm = _tile(T, 256)

    def body(ds_ref, xh_ref, rs_ref, g_ref, b_ref, dv_ref, dg_ref, db_ref):
        @pl.when(pl.program_id(0) == 0)
        def _():
            dg_ref[...] = jnp.zeros_like(dg_ref)
            db_ref[...] = jnp.zeros_like(db_ref)

        xh = xh_ref[...]
        gam = g_ref[...]
        y = xh * gam + b_ref[...]
        sig = _sigmoid(y)
        dln = ds_ref[...] * (sig * (1.0 + y * (1.0 - sig)))
        dv_ref[...] = _ln_bwd_rows(dln, xh, rs_ref[...], gam)
        dg_ref[...] += _fold8(dln * xh)
        db_ref[...] += _fold8(dln)

    row = lambda i: (i, 0)
    fixed = lambda i: (0, 0)
    vec = pl.BlockSpec((1, C), fixed)
    part = pl.BlockSpec((SUBLANES, C), fixed)
    return pl.pallas_call(
        body, name=name, grid=(T // tm,),
        in_specs=[pl.BlockSpec((tm, C), row), pl.BlockSpec((tm, C), row), pl.BlockSpec((tm, 1), row), vec, vec],
        out_specs=[pl.BlockSpec((tm, C), row), part, part],
        out_shape=[jax.ShapeDtypeStruct((T, C), F32)] + [jax.ShapeDtypeStruct((SUBLANES, C), F32)] * 2,
        compiler_params=_cp(("arbitrary",)),
    )(ds, xh, rstd, g, b)


def _glu_bwd(du, h1, *, name):
    T, C = du.shape
    il = C // 2
    tm = _tile(T, 256)

    def body(du_ref, h_ref, dh_ref, cs_ref):
        @pl.when(pl.program_id(0) == 0)
        def _():
            cs_ref[...] = jnp.zeros_like(cs_ref)

        for hb in range(2):
            a = h_ref[:, 2 * hb * il:(2 * hb + 1) * il]
            gate = h_ref[:, (2 * hb + 1) * il:(2 * hb + 2) * il]
            d = du_ref[:, hb * il:(hb + 1) * il]
            sig = _sigmoid(gate)
            da = d * sig
            dgate = d * a * sig * (1.0 - sig)
            dh_ref[:, 2 * hb * il:(2 * hb + 1) * il] = da.astype(_MXU)
            dh_ref[:, (2 * hb + 1) * il:(2 * hb + 2) * il] = dgate.astype(_MXU)
            cs_ref[:, 2 * hb * il:(2 * hb + 1) * il] += _fold8(da)
            cs_ref[:, (2 * hb + 1) * il:(2 * hb + 2) * il] += _fold8(dgate)

    row = lambda i: (i, 0)
    return pl.pallas_call(
        body, name=name, grid=(T // tm,),
        in_specs=[pl.BlockSpec((tm, C), row), pl.BlockSpec((tm, 2 * C), row)],
        out_specs=[pl.BlockSpec((tm, 2 * C), row), pl.BlockSpec((SUBLANES, 2 * C), lambda i: (0, 0))],
        out_shape=[jax.ShapeDtypeStruct((T, 2 * C), _MXU), jax.ShapeDtypeStruct((SUBLANES, 2 * C), F32)],
        compiler_params=_cp(("arbitrary",)),
    )(du, h1)


def _tril_mask(n):
    return lax.broadcasted_iota(jnp.int32, (n, n), 0) >= lax.broadcasted_iota(jnp.int32, (n, n), 1)


def _split_uv(t, il):
    u = jnp.concatenate([t[:, 0:il], t[:, 2 * il:3 * il]], axis=1)
    v = jnp.concatenate([t[:, il:2 * il], t[:, 3 * il:4 * il]], axis=1)
    return u, v


def _gmlp_gate_fwd(p, g, b, w_s, bsb, *, name):
    T, C2 = p.shape
    C = C2 // 2
    il = C // 2
    G, L, _ = w_s.shape
    assert G * L == C
    tm = _tile(T, 2 * L, L)

    def body(p_ref, g_ref, b_ref, ws_ref, bs_ref, us_ref, xh_ref, rs_ref, vn_ref, u_ref):
        z, _ = _gelu_parts(p_ref[...])
        u, v = _split_uv(z, il)
        vn, xh, rstd = _ln_rows(v, g_ref[...], b_ref[...])
        xh_ref[...] = xh
        rs_ref[...] = rstd
        vn_ref[...] = vn.astype(_MXU)
        u_ref[...] = u
        mask = _tril_mask(L)
        for gi in range(G):
            wc = jnp.where(mask, ws_ref[gi], 0.0).astype(_MXU)
            cols = slice(gi * L, (gi + 1) * L)
            for c in range(tm // L):
                rows = slice(c * L, (c + 1) * L)
                s = jnp.dot(wc, vn_ref[rows, cols], preferred_element_type=F32) + bs_ref[:, cols]
                us_ref[rows, cols] = (u_ref[rows, cols] * s).astype(_MXU)

    row = lambda i: (i, 0)
    fixed = lambda i: (0, 0)
    return pl.pallas_call(
        body, name=name, grid=(T // tm,),
        in_specs=[pl.BlockSpec((tm, C2), row), pl.BlockSpec((1, C), fixed), pl.BlockSpec((1, C), fixed),
                  pl.BlockSpec((G, L, L), lambda i: (0, 0, 0)), pl.BlockSpec((L, C), fixed)],
        out_specs=[pl.BlockSpec((tm, C), row), pl.BlockSpec((tm, C), row), pl.BlockSpec((tm, 1), row)],
        out_shape=[jax.ShapeDtypeStruct((T, C), _MXU), jax.ShapeDtypeStruct((T, C), F32),
                   jax.ShapeDtypeStruct((T, 1), F32)],
        scratch_shapes=[pltpu.VMEM((tm, C), _MXU), pltpu.VMEM((tm, C), F32)],
        compiler_params=_cp(("parallel",)),
    )(p, g, b, w_s, bsb)


def _gmlp_gate_bwd(dus, p, xh, rstd, g, b, w_s, bsb, *, name):
    T, C2 = p.shape
    C = C2 // 2
    il = C // 2
    G, L, _ = w_s.shape
    tm = _tile(T, 2 * L, L)

    def body(dus_ref, p_ref, xh_ref, rs_ref, g_ref, b_ref, ws_ref, bs_ref,
             dp_ref, dg_ref, db_ref, cs_ref, dws_ref, dbs_ref, vn_ref, u_ref, dvn_ref, du_ref):
        @pl.when(pl.program_id(0) == 0)
        def _():
            dg_ref[...] = jnp.zeros_like(dg_ref)
            db_ref[...] = jnp.zeros_like(db_ref)
            cs_ref[...] = jnp.zeros_like(cs_ref)
            dws_ref[...] = jnp.zeros_like(dws_ref)
            dbs_ref[...] = jnp.zeros_like(dbs_ref)

        z, gp = _gelu_parts(p_ref[...])
        u, _ = _split_uv(z, il)
        xh = xh_ref[...]
        gam = g_ref[...]
        vn_ref[...] = (xh * gam + b_ref[...]).astype(_MXU)
        u_ref[...] = u
        mask = _tril_mask(L)
        for gi in range(G):
            wc = jnp.where(mask, ws_ref[gi], 0.0).astype(_MXU)
            cols = slice(gi * L, (gi + 1) * L)
            for c in range(tm // L):
                rows = slice(c * L, (c + 1) * L)
                vnb = vn_ref[rows, cols]
                s = jnp.dot(wc, vnb, preferred_element_type=F32) + bs_ref[:, cols]
                d = dus_ref[rows, cols]
                du_ref[rows, cols] = d * s
                ds = d * u_ref[rows, cols]
                dbs_ref[:, cols] += ds
                dsb = ds.astype(_MXU)
                dw = lax.dot_general(dsb, vnb, (((1,), (1,)), ((), ())), preferred_element_type=F32)
                dws_ref[gi] += jnp.where(mask, dw, 0.0)
                dvn_ref[rows, cols] = lax.dot_general(wc, dsb, (((0,), (0,)), ((), ())), preferred_element_type=F32)
        dvn = dvn_ref[...]
        dg_ref[...] += _fold8(dvn * xh)
        db_ref[...] += _fold8(dvn)
        dv = _ln_bwd_rows(dvn, xh, rs_ref[...], gam)
        du = du_ref[...]
        for hb in range(2):
            for part, src in ((0, du), (1, dv)):
                lo = (2 * hb + part) * il
                dp = src[:, hb * il:(hb + 1) * il] * gp[:, lo:lo + il]
                dp_ref[:, lo:lo + il] = dp.astype(_MXU)
                cs_ref[:, lo:lo + il] += _fold8(dp)

    row = lambda i: (i, 0)
    fixed = lambda i: (0, 0)
    part_c = pl.BlockSpec((SUBLANES, C), fixed)
    return pl.pallas_call(
        body, name=name, grid=(T // tm,),
        in_specs=[pl.BlockSpec((tm, C), row), pl.BlockSpec((tm, C2), row), pl.BlockSpec((tm, C), row),
                  pl.BlockSpec((tm, 1), row), pl.BlockSpec((1, C), fixed), pl.BlockSpec((1, C), fixed),
                  pl.BlockSpec((G, L, L), lambda i: (0, 0, 0)), pl.BlockSpec((L, C), fixed)],
        out_specs=[pl.BlockSpec((tm, C2), row), part_c, part_c, pl.BlockSpec((SUBLANES, C2), fixed),
                   pl.BlockSpec((G, L, L), lambda i: (0, 0, 0)), pl.BlockSpec((L, C), fixed)],
        out_shape=[jax.ShapeDtypeStruct((T, C2), _MXU), jax.ShapeDtypeStruct((SUBLANES, C), F32),
                   jax.ShapeDtypeStruct((SUBLANES, C), F32), jax.ShapeDtypeStruct((SUBLANES, C2), F32),
                   jax.ShapeDtypeStruct((G, L, L), F32), jax.ShapeDtypeStruct((L, C), F32)],
        scratch_shapes=[pltpu.VMEM((tm, C), _MXU), pltpu.VMEM((tm, C), F32), pltpu.VMEM((tm, C), F32),
                        pltpu.VMEM((tm, C), F32)],
        compiler_params=_cp(("arbitrary",)),
    )(dus, p, xh, rstd, g, b, w_s, bsb)


def _ffn_conv(h, prev8, w_ref, b_ref):
    h1 = _shift_down(prev8, h, 1)
    h2 = _shift_down(prev8, h, 2)
    hc = w_ref[pl.ds(2, 1), :] * h + w_ref[pl.ds(1, 1), :] * h1 + w_ref[pl.ds(0, 1), :] * h2 + b_ref[...]
    return hc, h1, h2


def _ffn_up_fwd(xb, w, wl, b_up, w_dw, b_dw, *, S, name):
    T, D = xb.shape
    N = w.shape[-1]
    tn = N // N_CHIPS
    tm = _tile(S, 256)
    spt = S // tm

    def body(x_ref, w_ref, bu_ref, wd_ref, bd_ref, h_ref, f_ref, carry_ref):
        i = pl.program_id(1)

        @pl.when(i % spt == 0)
        def _():
            carry_ref[...] = jnp.zeros_like(carry_ref)

        h = jnp.dot(x_ref[...].astype(_MXU), w_ref[...].astype(_MXU), preferred_element_type=F32) + bu_ref[...]
        hq = h.astype(_HDT)
        h_ref[...] = hq
        h = hq.astype(F32)
        hc, _, _ = _ffn_conv(h, carry_ref[...], wd_ref, bd_ref)
        carry_ref[...] = h[tm - SUBLANES:tm]
        gte = hc[:, :tn]
        f_ref[...] = (gte * _sigmoid(gte) * hc[:, tn:]).astype(_MXU)

    pair = lambda j, i: (0, j)
    return pl.pallas_call(
        body, name=name, grid=(2, T // tm),
        in_specs=[pl.BlockSpec((tm, D), lambda j, i: (i, 0)),
                  pl.BlockSpec((None, D, 2 * tn), lambda j, i: (wl, 0, j)),
                  pl.BlockSpec((1, 2 * tn), pair), pl.BlockSpec((SUBLANES, 2 * tn), pair),
                  pl.BlockSpec((1, 2 * tn), pair)],
        out_specs=[pl.BlockSpec((tm, 2 * tn), lambda j, i: (i, j)), pl.BlockSpec((tm, tn), lambda j, i: (i, j))],
        out_shape=[jax.ShapeDtypeStruct((T, N), _HDT), jax.ShapeDtypeStruct((T, N // 2), _MXU)],
        scratch_shapes=[pltpu.VMEM((SUBLANES, 2 * tn), F32)],
        compiler_params=_cp(("parallel", "arbitrary")),
    )(xb, w, b_up, w_dw, b_dw)


def _ffn_bwd(dzb, w_down, wl, hs, w_dw, b_dw, *, S, name):
    T, D = dzb.shape
    N = hs.shape[1]
    tn = N // N_CHIPS
    tm = _tile(S, 256)
    spt = S // tm
    nt = T // tm
    hal = 16

    def body(dz_ref, wd_ref, h_ref, halo_ref, wc_ref, bc_ref, dh_ref, cs_ref, dw_ref, db_ref, carry_ref):
        i = pl.program_id(1)
        ii = nt - 1 - i

        @pl.when(i == 0)
        def _():
            cs_ref[...] = jnp.zeros_like(cs_ref)
            dw_ref[...] = jnp.zeros_like(dw_ref)
            db_ref[...] = jnp.zeros_like(db_ref)

        df = lax.dot_general(dz_ref[...].astype(_MXU), wd_ref[...].astype(_MXU), (((1,), (1,)), ((), ())),
                             preferred_element_type=F32)
        h = h_ref[...].astype(F32)
        prev8 = halo_ref[...].astype(F32)[hal - SUBLANES:hal]
        prev8 = jnp.where(ii % spt == 0, 0.0, prev8)
        hc, h1, h2 = _ffn_conv(h, prev8, wc_ref, bc_ref)
        gte, val = hc[:, :tn], hc[:, tn:]
        sig = _sigmoid(gte)
        dval = df * (gte * sig)
        dg = df * val * (sig * (1.0 + gte * (1.0 - sig)))
        dhc = jnp.concatenate([dg, dval], axis=1)
        db_ref[...] += _fold8(dhc)
        dw_ref[2] += _fold8(dhc * h)
        dw_ref[1] += _fold8(dhc * h1)
        dw_ref[0] += _fold8(dhc * h2)
        nxt = jnp.where((ii + 1) % spt == 0, 0.0, carry_ref[...])
        dh = (wc_ref[pl.ds(2, 1), :] * dhc + wc_ref[pl.ds(1, 1), :] * _shift_up(dhc, nxt, 1)
              + wc_ref[pl.ds(0, 1), :] * _shift_up(dhc, nxt, 2))
        carry_ref[...] = dhc[0:SUBLANES]
        cs_ref[...] += _fold8(dh)
        dh_ref[...] = dh.astype(_MXU)

    pair = lambda j, i: (0, j)
    rev = lambda j, i: (nt - 1 - i, j)
    return pl.pallas_call(
        body, name=name, grid=(2, nt),
        in_specs=[pl.BlockSpec((tm, D), lambda j, i: (nt - 1 - i, 0)),
                  pl.BlockSpec((None, tn, D), lambda j, i: (wl, j, 0)),
                  pl.BlockSpec((tm, 2 * tn), rev),
                  pl.BlockSpec((hal, 2 * tn), lambda j, i: (jnp.maximum((nt - 1 - i) * (tm // hal) - 1, 0), j)),
                  pl.BlockSpec((SUBLANES, 2 * tn), pair), pl.BlockSpec((1, 2 * tn), pair)],
        out_specs=[pl.BlockSpec((tm, 2 * tn), rev), pl.BlockSpec((SUBLANES, 2 * tn), pair),
                   pl.BlockSpec((3, SUBLANES, 2 * tn), lambda j, i: (0, 0, j)),
                   pl.BlockSpec((SUBLANES, 2 * tn), pair)],
        out_shape=[jax.ShapeDtypeStruct((T, N), _MXU), jax.ShapeDtypeStruct((SUBLANES, N), F32),
                   jax.ShapeDtypeStruct((3, SUBLANES, N), F32), jax.ShapeDtypeStruct((SUBLANES, N), F32)],
        scratch_shapes=[pltpu.VMEM((SUBLANES, 2 * tn), F32)],
        compiler_params=_cp(("parallel", "arbitrary")),
    )(dzb, w_down, hs, hs, w_dw, b_dw)


def _sum8(r, *, name):
    _, pr, pc = r.shape
    tr = _tile(pr, 128)

    def body(r_ref, o_ref):
        acc = r_ref[0].astype(F32)
        for s in range(1, N_DEV):
            acc = acc + r_ref[s].astype(F32)
        o_ref[...] = acc

    return pl.pallas_call(
        body, name=name, grid=(pr // tr,),
        in_specs=[pl.BlockSpec((N_DEV, tr, pc), lambda i: (0, i, 0))],
        out_specs=pl.BlockSpec((tr, pc), lambda i: (i, 0)),
        out_shape=jax.ShapeDtypeStruct((pr, pc), F32),
        compiler_params=_cp(("parallel",)),
    )(r)


def _adam(w, g, m, v, *, name):
    R, C = w.shape
    tr = _tile(R, 256)
    bc1 = 1.0 - ADAM_B1 ** ADAM_STEP
    bc2 = 1.0 - ADAM_B2 ** ADAM_STEP

    def body(w_ref, g_ref, m_ref, v_ref, go_ref, d_ref, mo_ref, vo_ref):
        g_ = g_ref[...]
        m_ = ADAM_B1 * m_ref[...] + (1.0 - ADAM_B1) * g_
        v_ = ADAM_B2 * v_ref[...] + (1.0 - ADAM_B2) * (g_ * g_)
        d_ref[...] = -ADAM_LR * ((m_ / bc1) / (jnp.sqrt(v_ / bc2) + ADAM_EPS) + ADAM_WD * w_ref[...])
        go_ref[...] = g_
        mo_ref[...] = m_
        vo_ref[...] = v_

    spec = pl.BlockSpec((tr, C), lambda i: (i, 0))
    return pl.pallas_call(
        body, name=name, grid=(R // tr,), in_specs=[spec] * 4, out_specs=[spec] * 4,
        out_shape=[jax.ShapeDtypeStruct((R, C), F32)] * 4,
        compiler_params=_cp(("parallel",)),
    )(w, g, m, v)


def _remote(src, dst, send, recv, dev):
    return pltpu.make_async_remote_copy(src_ref=src, dst_ref=dst, send_sem=send, recv_sem=recv,
                                        device_id=dev, device_id_type=MESH)


def _allgather_w(shard, *, axis, perm, name):
    L, R, C = shard.shape
    rh = R // 2
    out_shape = (L, R, N_CHIPS * C) if axis == 2 else (L, N_CHIPS * R, C)

    def body(s_ref, o_ref, send, recv, lsem):
        x, y, c = lax.axis_index("x"), lax.axis_index("y"), lax.axis_index("c")

        def win(px, py, h):
            q = 2 * px + py
            if perm:
                q = _perm_idx(q)
            if axis == 2:
                cols = pl.ds(pl.multiple_of(q * C, LANES), C)
                if h is None:
                    return o_ref.at[:, :, cols]
                return o_ref.at[:, pl.ds(pl.multiple_of(h * rh, 16), rh), cols]
            if h is None:
                return o_ref.at[:, pl.ds(pl.multiple_of(q * R, 16), R), :]
            return o_ref.at[:, pl.ds(pl.multiple_of(q * R + h * rh, 16), rh), :]

        half = s_ref.at[:, pl.ds(pl.multiple_of(c * rh, 16), rh), :]
        chips = [(1 - x, y), (x, 1 - y), (1 - x, 1 - y)]
        mine = pltpu.make_async_copy(s_ref, win(x, y, None), lsem)
        mine.start()
        first = [_remote(half, win(x, y, c), send.at[i], recv.at[i], (px, py, c)) for i, (px, py) in enumerate(chips)]
        for cp in first:
            cp.start()
        passed = []
        for i, (px, py) in enumerate(chips):
            _remote(half, win(px, py, c), send.at[i], recv.at[i], (px, py, c)).wait_recv()
            fw = _remote(win(px, py, c), win(px, py, c), send.at[3 + i], recv.at[3 + i], (x, y, 1 - c))
            fw.start()
            passed.append(fw)
        for i, (px, py) in enumerate(chips):
            _remote(half, win(px, py, 1 - c), send.at[3 + i], recv.at[3 + i], (x, y, 1 - c)).wait_recv()
        for cp in first + passed:
            cp.wait_send()
        mine.wait()

    return pl.pallas_call(
        body, name=name, in_specs=[ANY], out_specs=ANY,
        out_shape=jax.ShapeDtypeStruct(out_shape, shard.dtype),
        scratch_shapes=[pltpu.SemaphoreType.DMA((6,)), pltpu.SemaphoreType.DMA((6,)), pltpu.SemaphoreType.DMA],
    )(shard)


def _flip(x, y, c, f):
    return ((1 - x) if f & 4 else x, (1 - y) if f & 2 else y, (1 - c) if f & 1 else c)


def _rs_exchange(g, *, name):
    def body(g_ref, r_ref, send, recv, lsem):
        x, y, c = lax.axis_index("x"), lax.axis_index("y"), lax.axis_index("c")
        me = 4 * x + 2 * y + c
        loc = pltpu.make_async_copy(g_ref.at[me], r_ref.at[me], lsem)
        loc.start()
        sends, recvs = [], []
        for f in range(1, N_DEV):
            tx, ty, tcx = _flip(x, y, c, f)
            t = 4 * tx + 2 * ty + tcx
            cp = _remote(g_ref.at[t], r_ref.at[me], send.at[f - 1], recv.at[f - 1], (tx, ty, tcx))
            cp.start()
            sends.append(cp)
            recvs.append(_remote(g_ref.at[me], r_ref.at[t], send.at[f - 1], recv.at[f - 1], (tx, ty, tcx)))
        for cp in recvs:
            cp.wait_recv()
        for cp in sends:
            cp.wait_send()
        loc.wait()

    return pl.pallas_call(
        body, name=name, in_specs=[ANY], out_specs=ANY, out_shape=jax.ShapeDtypeStruct(g.shape, g.dtype),
        scratch_shapes=[pltpu.SemaphoreType.DMA((N_DEV - 1,)), pltpu.SemaphoreType.DMA((N_DEV - 1,)),
                        pltpu.SemaphoreType.DMA],
    )(g)


def _pair_exchange(halves, *, name):
    L = len(halves)
    pr, pc = halves[0].shape

    def body(*refs):
        h_refs, o_ref, send, recv, lsem = refs[:L], refs[L], refs[L + 1], refs[L + 2], refs[L + 3]
        x, y, c = lax.axis_index("x"), lax.axis_index("y"), lax.axis_index("c")
        sib = (x, y, 1 - c)
        cps = []
        for l in range(L):
            loc = pltpu.make_async_copy(h_refs[l], o_ref.at[l, c], lsem.at[l])
            rem = _remote(h_refs[l], o_ref.at[l, c], send.at[l], recv.at[l], sib)
            loc.start()
            rem.start()
            cps.append((loc, rem))
        for l in range(L):
            _remote(h_refs[l], o_ref.at[l, 1 - c], send.at[l], recv.at[l], sib).wait_recv()
        for loc, rem in cps:
            rem.wait_send()
            loc.wait()

    return pl.pallas_call(
        body, name=name, in_specs=[ANY] * L, out_specs=ANY,
        out_shape=jax.ShapeDtypeStruct((L, 2, pr, pc), halves[0].dtype),
        scratch_shapes=[pltpu.SemaphoreType.DMA((L,)), pltpu.SemaphoreType.DMA((L,)), pltpu.SemaphoreType.DMA((L,))],
    )(*halves)


def _allreduce_flat(vec, *, name):
    n = vec.shape[0]
    unit = N_DEV * SUBLANES * LANES
    npad = -(-n // unit) * unit
    rows = npad // (N_DEV * LANES)
    xin = jnp.pad(vec, (0, npad - n)).reshape(N_DEV, rows, LANES)

    def body(x_ref, y_ref, a_ref, send_a, recv_a, send_b, recv_b):
        x, y, c = lax.axis_index("x"), lax.axis_index("y"), lax.axis_index("c")
        me = 4 * x + 2 * y + c
        a_ref[me] = x_ref[me]
        sends, recvs = [], []
        for f in range(1, N_DEV):
            dev = _flip(x, y, c, f)
            t = 4 * dev[0] + 2 * dev[1] + dev[2]
            cp = _remote(x_ref.at[t], a_ref.at[me], send_a.at[f - 1], recv_a.at[f - 1], dev)
            cp.start()
            sends.append(cp)
            recvs.append(_remote(x_ref.at[me], a_ref.at[t], send_a.at[f - 1], recv_a.at[f - 1], dev))
        for cp in recvs:
            cp.wait_recv()
        for cp in sends:
            cp.wait_send()
        acc = a_ref[0]
        for s in range(1, N_DEV):
            acc = acc + a_ref[s]
        y_ref[me] = acc
        sends, recvs = [], []
        for f in range(1, N_DEV):
            dev = _flip(x, y, c, f)
            t = 4 * dev[0] + 2 * dev[1] + dev[2]
            cp = _remote(y_ref.at[me], y_ref.at[me], send_b.at[f - 1], recv_b.at[f - 1], dev)
            cp.start()
            sends.append(cp)
            recvs.append(_remote(y_ref.at[me], y_ref.at[t], send_b.at[f - 1], recv_b.at[f - 1], dev))
        for cp in recvs:
            cp.wait_recv()
        for cp in sends:
            cp.wait_send()

    vm = pl.BlockSpec(memory_space=pltpu.VMEM)
    out = pl.pallas_call(
        body, name=name, in_specs=[vm], out_specs=vm,
        out_shape=jax.ShapeDtypeStruct((N_DEV, rows, LANES), F32),
        scratch_shapes=[pltpu.VMEM((N_DEV, rows, LANES), F32)] + [pltpu.SemaphoreType.DMA((N_DEV - 1,))] * 4,
        compiler_params=_cp(),
    )(xin)
    return out.reshape(npad)[:n]


def _perm_cols(v, blocks=N_CHIPS):
    lead, n = v.shape[:-1], v.shape[-1]
    return v.reshape(lead + (blocks, n // blocks))[..., PERM, :].reshape(lead + (n,))


def _pack(arrs):
    return jnp.concatenate([a.reshape(-1).astype(F32) for a in arrs])


def _unpack(flat, shapes):
    out, pos = [], 0
    for s in shapes:
        n = 1
        for d in s:
            n *= d
        out.append(flat[pos:pos + n].reshape(s))
        pos += n
    return out


def kernel(x, conv_w_in, conv_b_in, conv_w_dw, conv_b_dw, conv_ln_g, conv_ln_b, conv_w_out, conv_b_out, gmlp_w_in, gmlp_b_in, gmlp_ln_g, gmlp_ln_b, gmlp_w_s, gmlp_b_s, gmlp_w_out, gmlp_b_out, ffn_w_up, ffn_b_up, ffn_w_dw, ffn_b_dw, ffn_w_down, ffn_b_down, norm1_g, norm1_b, norm2_g, norm2_b, loss_target, m_conv_w_in, m_conv_b_in, m_conv_w_dw, m_conv_b_dw, m_conv_ln_g, m_conv_ln_b, m_conv_w_out, m_conv_b_out, m_gmlp_w_in, m_gmlp_b_in, m_gmlp_ln_g, m_gmlp_ln_b, m_gmlp_w_s, m_gmlp_b_s, m_gmlp_w_out, m_gmlp_b_out, m_ffn_w_up, m_ffn_b_up, m_ffn_w_dw, m_ffn_b_dw, m_ffn_w_down, m_ffn_b_down, m_norm1_g, m_norm1_b, m_norm2_g, m_norm2_b, v_conv_w_in, v_conv_b_in, v_conv_w_dw, v_conv_b_dw, v_conv_ln_g, v_conv_ln_b, v_conv_w_out, v_conv_b_out, v_gmlp_w_in, v_gmlp_b_in, v_gmlp_ln_g, v_gmlp_ln_b, v_gmlp_w_s, v_gmlp_b_s, v_gmlp_w_out, v_gmlp_b_out, v_ffn_w_up, v_ffn_b_up, v_ffn_w_dw, v_ffn_b_dw, v_ffn_w_down, v_ffn_b_down, v_norm1_g, v_norm1_b, v_norm2_g, v_norm2_b):
    P = dict(locals())
    WEIGHTS = ['conv_w_in', 'conv_b_in', 'conv_w_dw', 'conv_b_dw', 'conv_ln_g', 'conv_ln_b', 'conv_w_out',
               'conv_b_out', 'gmlp_w_in', 'gmlp_b_in', 'gmlp_ln_g', 'gmlp_ln_b', 'gmlp_w_s', 'gmlp_b_s',
               'gmlp_w_out', 'gmlp_b_out', 'ffn_w_up', 'ffn_b_up', 'ffn_w_dw', 'ffn_b_dw', 'ffn_w_down',
               'ffn_b_down', 'norm1_g', 'norm1_b', 'norm2_g', 'norm2_b']
    BIG = ['conv_w_in', 'conv_w_out', 'gmlp_w_in', 'gmlp_w_out', 'ffn_w_up', 'ffn_w_down']
    SMALL_SHARDED = {'conv_w_dw': 2, 'gmlp_b_in': 1, 'gmlp_ln_g': 1, 'gmlp_ln_b': 1, 'gmlp_b_out': 1, 'ffn_w_dw': 2}

    B, S, D = x.shape
    T = B * S
    depth = norm1_g.shape[0]
    alpha = (2.0 * depth) ** 0.25
    C = conv_w_out.shape[-1]
    F2 = ffn_b_up.shape[-1]
    G, L = gmlp_w_s.shape[1], gmlp_w_s.shape[2]
    xi, yi, ci = lax.axis_index("x"), lax.axis_index("y"), lax.axis_index("c")
    shard = 2 * xi + yi

    cw_in = _allgather_w(conv_w_in.astype(_WIRE), axis=2, perm=True, name="ag_conv_w_in")
    cw_out = _allgather_w(conv_w_out.astype(_WIRE), axis=1, perm=False, name="ag_conv_w_out")
    gw_in = _allgather_w(gmlp_w_in.astype(_WIRE), axis=2, perm=True, name="ag_gmlp_w_in")
    gw_out = _allgather_w(gmlp_w_out.astype(_WIRE), axis=1, perm=False, name="ag_gmlp_w_out")
    fw_up = _allgather_w(ffn_w_up.astype(_WIRE), axis=2, perm=True, name="ag_ffn_w_up")
    fw_down = _allgather_w(ffn_w_down.astype(_WIRE), axis=1, perm=False, name="ag_ffn_w_down")

    sm_names = list(SMALL_SHARDED)
    sm_shapes = [P[n].shape for n in sm_names]
    mine = _pack([P[n] for n in sm_names]) * (ci == 0).astype(F32)
    buf = jnp.zeros((N_CHIPS, mine.shape[0]), F32)
    buf = lax.dynamic_update_slice(buf, mine[None], (shard, 0))
    gathered = _allreduce_flat(buf.reshape(-1), name="ag_small").reshape(N_CHIPS, -1)
    full = {}
    for n, parts in zip(sm_names, zip(*[_unpack(gathered[k], sm_shapes) for k in range(N_CHIPS)])):
        full[n] = jnp.concatenate(parts, axis=SMALL_SHARDED[n])
    for n in WEIGHTS:
        if n not in BIG and n not in full:
            full[n] = P[n]

    assert G * L == C, "a gMLP group must be as wide as a chunk is long"

    def row(v):
        return v.reshape(1, -1)

    def pad_rows(v, r):
        return jnp.pad(v, ((0, r - v.shape[0]), (0, 0)))

    xf = x.reshape(T, D)
    saved = []
    cur, cur_b = xf, xf
    for i in range(depth):
        j = i // 2
        sv = {'x': cur, 'xb': cur_b}
        if i % 2 == 0:
            b_in = row(_perm_cols(full['conv_b_in'][j]))
            h1 = _mm(cur_b, cw_in, bl=j, bias=b_in, tm=_tile(T, 512), tn=_tile(2 * C, 1024, LANES), tk=D,
                     name=f"conv_in_{j}")
            wdw = pad_rows(full['conv_w_dw'][j], CONV_TAPS_PAD)
            dwo = _conv_fwd(h1, wdw, row(full['conv_b_dw'][j]), B=B, S=S, name=f"conv_dw_{j}")
            s_act, xhc, rsc = _ln_silu_fwd(dwo, row(full['conv_ln_g'][j]), row(full['conv_ln_b'][j]),
                                           name=f"conv_ln_{j}")
            sv.update(h1=h1, wdw=wdw, act=s_act, xhc=xhc, rsc=rsc)
            y1 = _mm_res_ln(s_act, cw_out, j, row(full['conv_b_out'][j]), cur, alpha, row(norm1_g[i]),
                            row(norm1_b[i]), name=f"conv_out_ln_{j}")
        else:
            b_in = row(_perm_cols(full['gmlp_b_in'][j]))
            pre = _mm(cur_b, gw_in, bl=j, bias=b_in, tm=_tile(T, 512), tn=_tile(2 * C, 1024, LANES), tk=D,
                      name=f"gmlp_in_{j}")
            bsb = jnp.repeat(gmlp_b_s[j].T, L, axis=1)
            us, xhv, rsv = _gmlp_gate_fwd(pre, row(full['gmlp_ln_g'][j]), row(full['gmlp_ln_b'][j]), gmlp_w_s[j],
                                          bsb, name=f"gmlp_gate_{j}")
            sv.update(pre=pre, bsb=bsb, act=us, xhv=xhv, rsv=rsv)
            y1 = _mm_res_ln(us, gw_out, j, row(full['gmlp_b_out'][j]), cur, alpha, row(norm1_g[i]),
                            row(norm1_b[i]), name=f"gmlp_out_ln_{j}")
        x1, x1b, xh1, rs1 = y1
        wdw3 = pad_rows(_perm_cols(full['ffn_w_dw'][i]), SUBLANES)
        bdw3 = row(_perm_cols(ffn_b_dw[i]))
        hs, f_act = _ffn_up_fwd(x1b, fw_up, i, row(_perm_cols(ffn_b_up[i])), wdw3, bdw3, S=S, name=f"ffn_up_{i}")
        x2, x2b, xh2, rs2 = _mm_res_ln(f_act, fw_down, i, row(ffn_b_down[i]), x1, alpha, row(norm2_g[i]),
                                       row(norm2_b[i]), name=f"ffn_down_ln_{i}")
        sv.update(x1=x1, x1b=x1b, xh1=xh1, rs1=rs1, hs=hs, f=f_act, wdw3=wdw3, bdw3=bdw3, xh2=xh2, rs2=rs2)
        saved.append(sv)
        cur, cur_b = x2, x2b

    sg = {n: [None] * full[n].shape[0] for n in WEIGHTS if n not in BIG}
    pieces = {n: [None] * P[n].shape[0] for n in BIG}
    tgt = loss_target.reshape(T, D)
    dcur = None
    loss_part = None
    tk_t = _tile(T, 512)
    for i in reversed(range(depth)):
        j = i // 2
        sv = saved[i]
        if dcur is None:
            dz2, dz2b, dg, db, cs, ls = _ln_bwd(cur, sv['xh2'], sv['rs2'], row(norm2_g[i]), target=tgt,
                                                name=f"ln2_bwd_head_{i}")
            loss_part = ls
        else:
            dz2, dz2b, dg, db, cs = _ln_bwd(dcur, sv['xh2'], sv['rs2'], row(norm2_g[i]), name=f"ln2_bwd_{i}")
        sg['norm2_g'][i], sg['norm2_b'][i], sg['ffn_b_down'][i] = dg.sum(0), db.sum(0), cs.sum(0)
        Fh = F2 // 2
        pieces['ffn_w_down'][i] = _mm(sv['f'], dz2b, ta=True, out_dtype=_WIRE, tm=Fh // 2, tn=_tile(D, 1024, LANES),
                                      tk=tk_t, name=f"ffn_dw_down_{i}", pieces=('row',))
        dh, csu, dwd, dbd = _ffn_bwd(dz2b, fw_down, i, sv['hs'], sv['wdw3'], sv['bdw3'], S=S, name=f"ffn_bwd_{i}")
        sg['ffn_b_up'][i] = _perm_cols(csu.sum(0))
        sg['ffn_w_dw'][i] = _perm_cols(dwd.sum(1))
        sg['ffn_b_dw'][i] = _perm_cols(dbd.sum(0))
        pieces['ffn_w_up'][i] = _mm(sv['x1b'], dh, ta=True, out_dtype=_WIRE, tm=D // 2, tn=F2 // N_CHIPS, tk=tk_t,
                                    name=f"ffn_dw_up_{i}", pieces=('col', True))
        dx1 = _mm(dh, fw_up, bl=i, tb=True, res=dz2, res_scale=alpha, tm=_tile(T, 512), tn=_tile(D, 1024, LANES),
                  tk=F2 // N_CHIPS, name=f"ffn_dx_{i}")
        dz1, dz1b, dg, db, cs = _ln_bwd(dx1, sv['xh1'], sv['rs1'], row(norm1_g[i]), name=f"ln1_bwd_{i}")
        sg['norm1_g'][i], sg['norm1_b'][i] = dg.sum(0), db.sum(0)
        if i % 2 == 0:
            sg['conv_b_out'][j] = cs.sum(0)
            pieces['conv_w_out'][j] = _mm(sv['act'], dz1b, ta=True, out_dtype=_WIRE, tm=_tile(C, 512),
                                          tn=_tile(D, 1024, LANES), tk=tk_t, name=f"conv_dw_out_{j}", pieces=('row',))
            ds = _mm(dz1b, cw_out, bl=j, tb=True, tm=_tile(T, 512), tn=_tile(C, 1024, LANES), tk=_tile(D, 1024, LANES),
                     name=f"conv_ds_{j}")
            ddw, dg, db = _ln_silu_bwd(ds, sv['xhc'], sv['rsc'], row(full['conv_ln_g'][j]),
                                       row(full['conv_ln_b'][j]), name=f"conv_ln_bwd_{j}")
            sg['conv_ln_g'][j], sg['conv_ln_b'][j] = dg.sum(0), db.sum(0)
            dglu, dwk, dbk = _conv_bwd(ddw, sv['h1'], sv['wdw'], B=B, S=S, name=f"conv_dw_bwd_{j}")
            sg['conv_w_dw'][j] = dwk.sum(1)[:conv_w_dw.shape[1]]
            sg['conv_b_dw'][j] = dbk.sum(0)
            dh1, csi = _glu_bwd(dglu, sv['h1'], name=f"conv_glu_bwd_{j}")
            sg['conv_b_in'][j] = _perm_cols(csi.sum(0))
            w_in_full, fam = cw_in, 'conv_w_in'
        else:
            sg['gmlp_b_out'][j] = cs.sum(0)
            pieces['gmlp_w_out'][j] = _mm(sv['act'], dz1b, ta=True, out_dtype=_WIRE, tm=_tile(C, 512),
                                          tn=_tile(D, 1024, LANES), tk=tk_t, name=f"gmlp_dw_out_{j}", pieces=('row',))
            dus = _mm(dz1b, gw_out, bl=j, tb=True, tm=_tile(T, 512), tn=_tile(C, 1024, LANES),
                      tk=_tile(D, 1024, LANES), name=f"gmlp_dus_{j}")
            dh1, dg, db, csi, dws, dbs = _gmlp_gate_bwd(dus, sv['pre'], sv['xhv'], sv['rsv'], row(full['gmlp_ln_g'][j]),
                                                        row(full['gmlp_ln_b'][j]), gmlp_w_s[j], sv['bsb'],
                                                        name=f"gmlp_gate_bwd_{j}")
            sg['gmlp_ln_g'][j], sg['gmlp_ln_b'][j] = dg.sum(0), db.sum(0)
            sg['gmlp_b_in'][j] = _perm_cols(csi.sum(0))
            sg['gmlp_w_s'][j] = dws
            sg['gmlp_b_s'][j] = dbs.reshape(L, G, L).sum(-1).T
            w_in_full, fam = gw_in, 'gmlp_w_in'
        pieces[fam][j] = _mm(sv['xb'], dh1, ta=True, out_dtype=_WIRE, tm=D // 2, tn=(2 * C) // N_CHIPS, tk=tk_t,
                             name=f"{fam}_dw_{j}", pieces=('col', True))
        dcur = _mm(dh1, w_in_full, bl=j, tb=True, res=dz1, res_scale=alpha, tm=_tile(T, 512),
                   tn=_tile(D, 1024, LANES), tk=_tile(2 * C, 1024, LANES), name=f"{fam}_dx_{j}")
    grad_x = dcur.reshape(B, S, D)

    small_names = [n for n in WEIGHTS if n not in BIG]
    small_full = [jnp.stack(sg[n]) for n in small_names]
    flat = _pack(small_full + [loss_part])
    red = _allreduce_flat(flat, name="ar_small")
    red_parts = _unpack(red, [a.shape for a in small_full] + [loss_part.shape])
    loss = (0.5 / D) * jnp.sum(red_parts[-1])
    grads = {}
    for n, g in zip(small_names, red_parts[:-1]):
        if n in SMALL_SHARDED:
            ax = SMALL_SHARDED[n]
            width = P[n].shape[ax]
            g = lax.dynamic_slice_in_dim(g, shard * width, width, axis=ax)
        grads[n] = g

    big_out = {}
    for n in BIG:
        halves = []
        for l, pc_ in enumerate(pieces[n]):
            r = _rs_exchange(pc_, name=f"rs_{n}_{l}")
            halves.append(_sum8(r, name=f"sum_{n}_{l}"))
        gfull = _pair_exchange(halves, name=f"px_{n}").reshape(P[n].shape)
        rows2d = (-1, P[n].shape[-1])
        go, dl, mo, vo = _adam(P[n].reshape(rows2d), gfull.reshape(rows2d), P['m_' + n].reshape(rows2d),
                               P['v_' + n].reshape(rows2d), name=f"adam_{n}")
        big_out[n] = tuple(t.reshape(P[n].shape) for t in (go, dl, mo, vo))

    shapes = [P[n].shape for n in small_names]
    n_small = sum(functools.reduce(lambda p_, d_: p_ * d_, s_, 1) for s_ in shapes)
    unit = SUBLANES * LANES
    npad = -(-n_small // unit) * unit

    def flat2d(arrs, fill=0.0):
        v = _pack(arrs)
        return jnp.pad(v, (0, npad - n_small), constant_values=fill).reshape(-1, LANES)

    go, dl, mo, vo = _adam(flat2d([P[n] for n in small_names]), flat2d([grads[n] for n in small_names]),
                           flat2d([P['m_' + n] for n in small_names]),
                           flat2d([P['v_' + n] for n in small_names], fill=1.0), name="adam_small")
    small_out = {}
    for k, t in enumerate((go, dl, mo, vo)):
        for n, a in zip(small_names, _unpack(t.reshape(-1), shapes)):
            small_out.setdefault(n, [None] * 4)[k] = a

    outs = [loss, grad_x]
    for k in range(4):
        for n in WEIGHTS:
            outs.append(big_out[n][k] if n in BIG else small_out[n][k])
    return tuple(outs)
```

```python
import functools

import jax
import jax.numpy as jnp
from jax import lax
from jax.experimental import pallas as pl
from jax.experimental.pallas import tpu as pltpu

F32 = jnp.float32
_MXU = jnp.bfloat16
_WIRE = jnp.bfloat16
_HDT = jnp.bfloat16
LN_EPS = 1e-5
ADAM_LR, ADAM_B1, ADAM_B2, ADAM_EPS, ADAM_WD, ADAM_STEP = 0.001, 0.9, 0.999, 1e-08, 0.01, 10
N_CHIPS = 4
N_DEV = 8
LANES = 128
SUBLANES = 8
CONV_TAPS_PAD = 32
VMEM_LIMIT = 56 << 20
MESH = pl.DeviceIdType.MESH
ANY = pl.BlockSpec(memory_space=pl.ANY)
PERM = (0, 2, 1, 3)


def _cp(sem=None):
    return pltpu.CompilerParams(dimension_semantics=sem, vmem_limit_bytes=VMEM_LIMIT)


def _tile(dim, pref, mult=SUBLANES):
    if dim <= pref:
        return dim
    t = (pref // mult) * mult
    while t > mult and dim % t:
        t -= mult
    assert dim % t == 0, (dim, pref, mult)
    return t


def _perm_idx(q):
    return (q % 2) * 2 + q // 2


def _fold8(t):
    r, n = t.shape
    return t.reshape(r // SUBLANES, SUBLANES, n).sum(axis=0)


def _ln_rows(z, g, b):
    mu = jnp.mean(z, axis=-1, keepdims=True)
    xc = z - mu
    var = jnp.mean(xc * xc, axis=-1, keepdims=True)
    rstd = lax.rsqrt(var + LN_EPS)
    xh = xc * rstd
    return xh * g + b, xh, rstd


def _ln_bwd_rows(dy, xh, rstd, g):
    dxh = dy * g
    m1 = jnp.mean(dxh, axis=-1, keepdims=True)
    m2 = jnp.mean(dxh * xh, axis=-1, keepdims=True)
    return rstd * (dxh - m1 - xh * m2)


def _sigmoid(v):
    return 1.0 / (1.0 + jnp.exp(-v))


def _gelu_parts(p):
    cdf = 0.5 * (1.0 + lax.erf(p * 0.7071067811865476))
    pdf = jnp.exp(-0.5 * p * p) * 0.3989422804014327
    return p * cdf, cdf + p * pdf


def _shift_down(prev8, t, s):
    ext = jnp.concatenate([prev8, t], axis=0)
    return pltpu.roll(ext, s, 0)[SUBLANES:]


def _shift_up(t, next8, s):
    n = t.shape[0]
    ext = jnp.concatenate([t, next8], axis=0)
    return pltpu.roll(ext, n + SUBLANES - s, 0)[:n]


def _mm(a, b, *, ta=False, tb=False, bl=None, bias=None, res=None, res_scale=1.0, out_dtype=F32,
        tm, tn, tk, name, pieces=None):
    M, K = (a.shape[1], a.shape[0]) if ta else a.shape
    bs = b.shape[1:] if bl is not None else b.shape
    N, Kb = (bs[0], bs[1]) if tb else (bs[1], bs[0])
    assert K == Kb and M % tm == 0 and N % tn == 0 and K % tk == 0, (a.shape, b.shape, tm, tn, tk)
    gm, gn, gk = M // tm, N // tn, K // tk
    a_spec = pl.BlockSpec((tk, tm), lambda i, j, k: (k, i)) if ta else pl.BlockSpec((tm, tk), lambda i, j, k: (i, k))
    bblk = (tn, tk) if tb else (tk, tn)
    bmap = (lambda i, j, k: (j, k)) if tb else (lambda i, j, k: (k, j))
    if bl is not None:
        b_spec = pl.BlockSpec((None,) + bblk, lambda i, j, k: (bl,) + bmap(i, j, k))
    else:
        b_spec = pl.BlockSpec(bblk, bmap)
    in_specs, operands = [a_spec, b_spec], [a, b]
    if bias is not None:
        in_specs.append(pl.BlockSpec((1, tn), lambda i, j, k: (0, j)))
        operands.append(bias)
    if res is not None:
        in_specs.append(pl.BlockSpec((tm, tn), lambda i, j, k: (i, j)))
        operands.append(res)
    if pieces is None:
        out_shape = jax.ShapeDtypeStruct((M, N), out_dtype)
        out_spec = pl.BlockSpec((tm, tn), lambda i, j, k: (i, j))
        ppb = pr = None
    elif pieces[0] == 'col':
        pr, pc = M // 2, N // N_CHIPS
        assert tm == pr and pc % tn == 0
        ppb, per = 1, pc // tn
        perm = pieces[1]
        out_shape = jax.ShapeDtypeStruct((N_DEV, pr, pc), out_dtype)
        out_spec = pl.BlockSpec(
            (1, pr, tn), lambda i, j, k: (2 * (_perm_idx(j // per) if perm else j // per) + i, 0, j % per))
    else:
        pr = M // N_DEV
        assert tm % pr == 0
        ppb = tm // pr
        out_shape = jax.ShapeDtypeStruct((N_DEV, pr, N), out_dtype)
        out_spec = pl.BlockSpec((ppb, pr, tn), lambda i, j, k: (i, 0, j))
    dims = (((0 if ta else 1,), (1 if tb else 0,)), ((), ()))

    def body(*refs):
        a_ref, b_ref = refs[0], refs[1]
        pos = 2
        bias_ref = res_ref = None
        if bias is not None:
            bias_ref = refs[pos]
            pos += 1
        if res is not None:
            res_ref = refs[pos]
            pos += 1
        o_ref, acc_ref = refs[pos], refs[pos + 1]
        k = pl.program_id(2)

        @pl.when(k == 0)
        def _():
            acc_ref[...] = jnp.zeros_like(acc_ref)

        acc_ref[...] += lax.dot_general(a_ref[...].astype(_MXU), b_ref[...].astype(_MXU), dims,
                                        preferred_element_type=F32)

        @pl.when(k == gk - 1)
        def _():
            r = acc_ref[...]
            if bias_ref is not None:
                r = r + bias_ref[...]
            if res_ref is not None:
                r = r + res_scale * res_ref[...]
            if pieces is not None:
                r = r.reshape(ppb, pr, tn)
            o_ref[...] = r.astype(out_dtype)

    return pl.pallas_call(
        body, name=name, grid=(gm, gn, gk), in_specs=in_specs, out_specs=out_spec, out_shape=out_shape,
        scratch_shapes=[pltpu.VMEM((tm, tn), F32)],
        compiler_params=_cp(("parallel", "parallel", "arbitrary")),
    )(*operands)


def _mm_res_ln(a, w, wl, bias, res, alpha, g, b, *, name):
    T, K = a.shape
    D = w.shape[-1]
    tm = _tile(T, 256)

    def body(a_ref, w_ref, bias_ref, res_ref, g_ref, b_ref, y_ref, yb_ref, xh_ref, rs_ref):
        z = jnp.dot(a_ref[...].astype(_MXU), w_ref[...].astype(_MXU), preferred_element_type=F32)
        z = z + bias_ref[...] + alpha * res_ref[...]
        y, xh, rstd = _ln_rows(z, g_ref[...], b_ref[...])
        y_ref[...] = y
        yb_ref[...] = y.astype(_MXU)
        xh_ref[...] = xh
        rs_ref[...] = rstd

    row = lambda i: (i, 0)
    vec = pl.BlockSpec((1, D), lambda i: (0, 0))
    return pl.pallas_call(
        body, name=name, grid=(T // tm,),
        in_specs=[pl.BlockSpec((tm, K), row), pl.BlockSpec((None, K, D), lambda i: (wl, 0, 0)), vec,
                  pl.BlockSpec((tm, D), row), vec, vec],
        out_specs=[pl.BlockSpec((tm, D), row), pl.BlockSpec((tm, D), row), pl.BlockSpec((tm, D), row),
                   pl.BlockSpec((tm, 1), row)],
        out_shape=[jax.ShapeDtypeStruct((T, D), F32), jax.ShapeDtypeStruct((T, D), _MXU),
                   jax.ShapeDtypeStruct((T, D), F32), jax.ShapeDtypeStruct((T, 1), F32)],
        compiler_params=_cp(("parallel",)),
    )(a, w, bias, res, g, b)


def _ln_bwd(dy, xh, rstd, g, *, name, target=None):
    T, D = dy.shape
    tm = _tile(T, 256)
    head = target is not None

    def body(*refs):
        if head:
            dy_ref, t_ref, xh_ref, rs_ref, g_ref, dz_ref, dzb_ref, dg_ref, db_ref, cs_ref, ls_ref = refs
        else:
            dy_ref, xh_ref, rs_ref, g_ref, dz_ref, dzb_ref, dg_ref, db_ref, cs_ref = refs
        i = pl.program_id(0)

        @pl.when(i == 0)
        def _():
            dg_ref[...] = jnp.zeros_like(dg_ref)
            db_ref[...] = jnp.zeros_like(db_ref)
            cs_ref[...] = jnp.zeros_like(cs_ref)
            if head:
                ls_ref[...] = jnp.zeros_like(ls_ref)

        d = dy_ref[...]
        if head:
            err = d - t_ref[...]
            ls_ref[...] += _fold8(err * err)
            d = err * (1.0 / D)
        xh = xh_ref[...]
        dz = _ln_bwd_rows(d, xh, rs_ref[...], g_ref[...])
        dz_ref[...] = dz
        dzb_ref[...] = dz.astype(_MXU)
        dg_ref[...] += _fold8(d * xh)
        db_ref[...] += _fold8(d)
        cs_ref[...] += _fold8(dz)

    row = lambda i: (i, 0)
    fixed = lambda i: (0, 0)
    tile = pl.BlockSpec((tm, D), row)
    part = pl.BlockSpec((SUBLANES, D), fixed)
    in_specs = [tile] + ([tile] if head else []) + [tile, pl.BlockSpec((tm, 1), row), pl.BlockSpec((1, D), fixed)]
    n_part = 4 if head else 3
    operands = [dy] + ([target] if head else []) + [xh, rstd, g]
    return pl.pallas_call(
        body, name=name, grid=(T // tm,), in_specs=in_specs,
        out_specs=[tile, tile] + [part] * n_part,
        out_shape=[jax.ShapeDtypeStruct((T, D), F32), jax.ShapeDtypeStruct((T, D), _MXU)]
        + [jax.ShapeDtypeStruct((SUBLANES, D), F32)] * n_part,
        compiler_params=_cp(("arbitrary",)),
    )(*operands)


def _conv_cols(C, tc):
    per = (C // 2) // tc
    return per, (lambda j: (j // per) * (2 * per) + j % per)


def _glu_shifted(a_ref, g_ref, p_ref, S):
    u = a_ref[...] * _sigmoid(g_ref[...])
    rows = lax.broadcasted_iota(jnp.int32, u.shape, 0)
    for r in range(SUBLANES):
        p_ref[r, 0:CONV_TAPS_PAD, :] = jnp.zeros((CONV_TAPS_PAD, u.shape[1]), F32)
        p_ref[r, CONV_TAPS_PAD:CONV_TAPS_PAD + S, :] = u if r == 0 else jnp.where(rows >= r, pltpu.roll(u, r, 0), 0.0)


def _conv_fwd(h1, w_dw, b_dw, *, B, S, name):
    C = w_dw.shape[1]
    taps = CONV_TAPS_PAD - 1
    tc = LANES
    ch = _tile(S, 128)
    per, col_a = _conv_cols(C, tc)

    def body(a_ref, g_ref, w_ref, b_ref, o_ref, p_ref):
        _glu_shifted(a_ref, g_ref, p_ref, S)

        def chunk(ci, carry):
            base = pl.multiple_of(ci * ch, ch)
            acc = jnp.zeros((ch, tc), F32) + b_ref[...]
            for k in range(taps):
                q, r = divmod(taps - 1 - k, SUBLANES)
                start = pl.multiple_of(base + (CONV_TAPS_PAD - SUBLANES * q), SUBLANES)
                acc = acc + w_ref[pl.ds(k, 1), :] * p_ref[r, pl.ds(start, ch), :]
            o_ref[pl.ds(base, ch), :] = acc
            return carry

        lax.fori_loop(0, S // ch, chunk, 0)

    return pl.pallas_call(
        body, name=name, grid=(B, C // tc),
        in_specs=[pl.BlockSpec((S, tc), lambda b, j: (b, col_a(j))),
                  pl.BlockSpec((S, tc), lambda b, j: (b, col_a(j) + per)),
                  pl.BlockSpec((CONV_TAPS_PAD, tc), lambda b, j: (0, j)),
                  pl.BlockSpec((1, tc), lambda b, j: (0, j))],
        out_specs=pl.BlockSpec((S, tc), lambda b, j: (b, j)),
        out_shape=jax.ShapeDtypeStruct((B * S, C), F32),
        scratch_shapes=[pltpu.VMEM((SUBLANES, S + CONV_TAPS_PAD, tc), F32)],
        compiler_params=_cp(("parallel", "parallel")),
    )(h1, h1, w_dw, b_dw)


def _conv_bwd(dd, h1, w_dw, *, B, S, name):
    C = w_dw.shape[1]
    taps = CONV_TAPS_PAD - 1
    tc = LANES
    ch = _tile(S, 128)
    per, col_a = _conv_cols(C, tc)

    def body(d_ref, a_ref, g_ref, w_ref, du_ref, dw_ref, db_ref, p_ref, q_ref):
        b = pl.program_id(1)

        @pl.when(b == 0)
        def _():
            dw_ref[...] = jnp.zeros_like(dw_ref)
            db_ref[...] = jnp.zeros_like(db_ref)

        _glu_shifted(a_ref, g_ref, p_ref, S)
        d = d_ref[...]
        rows = lax.broadcasted_iota(jnp.int32, d.shape, 0)
        for r in range(SUBLANES):
            q_ref[r, S:S + CONV_TAPS_PAD, :] = jnp.zeros((CONV_TAPS_PAD, tc), F32)
            q_ref[r, 0:S, :] = d if r == 0 else jnp.where(rows < S - r, pltpu.roll(d, S - r, 0), 0.0)
        db_ref[...] += _fold8(d)

        def chunk(ci, carry):
            base = pl.multiple_of(ci * ch, ch)
            dch = d_ref[pl.ds(base, ch), :]
            acc = jnp.zeros((ch, tc), F32)
            for k in range(taps):
                q, r = divmod(taps - 1 - k, SUBLANES)
                up = pl.multiple_of(base + SUBLANES * q, SUBLANES)
                acc = acc + w_ref[pl.ds(k, 1), :] * q_ref[r, pl.ds(up, ch), :]
                down = pl.multiple_of(base + (CONV_TAPS_PAD - SUBLANES * q), SUBLANES)
                dw_ref[k] += _fold8(dch * p_ref[r, pl.ds(down, ch), :])
            du_ref[pl.ds(base, ch), :] = acc
            return carry

        lax.fori_loop(0, S // ch, chunk, 0)

    return pl.pallas_call(
        body, name=name, grid=(C // tc, B),
        in_specs=[pl.BlockSpec((S, tc), lambda j, b: (b, j)),
                  pl.BlockSpec((S, tc), lambda j, b: (b, col_a(j))),
                  pl.BlockSpec((S, tc), lambda j, b: (b, col_a(j) + per)),
                  pl.BlockSpec((CONV_TAPS_PAD, tc), lambda j, b: (0, j))],
        out_specs=[pl.BlockSpec((S, tc), lambda j, b: (b, j)),
                   pl.BlockSpec((CONV_TAPS_PAD, SUBLANES, tc), lambda j, b: (0, 0, j)),
                   pl.BlockSpec((SUBLANES, tc), lambda j, b: (0, j))],
        out_shape=[jax.ShapeDtypeStruct((B * S, C), F32),
                   jax.ShapeDtypeStruct((CONV_TAPS_PAD, SUBLANES, C), F32),
                   jax.ShapeDtypeStruct((SUBLANES, C), F32)],
        scratch_shapes=[pltpu.VMEM((SUBLANES, S + CONV_TAPS_PAD, tc), F32),
                        pltpu.VMEM((SUBLANES, S + CONV_TAPS_PAD, tc), F32)],
        compiler_params=_cp(("parallel", "arbitrary")),
    )(dd, h1, h1, w_dw)


def _ln_silu_fwd(v, g, b, *, name):
    T, C = v.shape
    tm = _tile(T, 512)

    def body(v_ref, g_ref, b_ref, s_ref, xh_ref, rs_ref):
        y, xh, rstd = _ln_rows(v_ref[...], g_ref[...], b_ref[...])
        s_ref[...] = (y * _sigmoid(y)).astype(_MXU)
        xh_ref[...] = xh
        rs_ref[...] = rstd

    row = lambda i: (i, 0)
    vec = pl.BlockSpec((1, C), lambda i: (0, 0))
    return pl.pallas_call(
        body, name=name, grid=(T // tm,),
        in_specs=[pl.BlockSpec((tm, C), row), vec, vec],
        out_specs=[pl.BlockSpec((tm, C), row), pl.BlockSpec((tm, C), row), pl.BlockSpec((tm, 1), row)],
        out_shape=[jax.ShapeDtypeStruct((T, C), _MXU), jax.ShapeDtypeStruct((T, C), F32),
                   jax.ShapeDtypeStruct((T, 1), F32)],
        compiler_params=_cp(("parallel",)),
    )(v, g, b)


def _ln_silu_bwd(ds, xh, rstd, g, b, *, name):
    T, C = ds.shape
    tm = _tile(T, 256)

    def body(ds_ref, xh_ref, rs_ref, g_ref, b_ref, dv_ref, dg_ref, db_ref):
        @pl.when(pl.program_id(0) == 0)
        def _():
            dg_ref[...] = jnp.zeros_like(dg_ref)
            db_ref[...] = jnp.zeros_like(db_ref)

        xh = xh_ref[...]
        gam = g_ref[...]
        y = xh * gam + b_ref[...]
        sig = _sigmoid(y)
        dln = ds_ref[...] * (sig * (1.0 + y * (1.0 - sig)))
        dv_ref[...] = _ln_bwd_rows(dln, xh, rs_ref[...], gam)
        dg_ref[...] += _fold8(dln * xh)
        db_ref[...] += _fold8(dln)

    row = lambda i: (i, 0)
    fixed = lambda i: (0, 0)
    vec = pl.BlockSpec((1, C), fixed)
    part = pl.BlockSpec((SUBLANES, C), fixed)
    return pl.pallas_call(
        body, name=name, grid=(T // tm,),
        in_specs=[pl.BlockSpec((tm, C), row), pl.BlockSpec((tm, C), row), pl.BlockSpec((tm, 1), row), vec, vec],
        out_specs=[pl.BlockSpec((tm, C), row), part, part],
        out_shape=[jax.ShapeDtypeStruct((T, C), F32)] + [jax.ShapeDtypeStruct((SUBLANES, C), F32)] * 2,
        compiler_params=_cp(("arbitrary",)),
    )(ds, xh, rstd, g, b)


def _glu_bwd(du, h1, *, name):
    T, C = du.shape
    il = C // 2
    tm = _tile(T, 256)

    def body(du_ref, h_ref, dh_ref, cs_ref):
        @pl.when(pl.program_id(0) == 0)
        def _():
            cs_ref[...] = jnp.zeros_like(cs_ref)

        for hb in range(2):
            a = h_ref[:, 2 * hb * il:(2 * hb + 1) * il]
            gate = h_ref[:, (2 * hb + 1) * il:(2 * hb + 2) * il]
            d = du_ref[:, hb * il:(hb + 1) * il]
            sig = _sigmoid(gate)
            da = d * sig
            dgate = d * a * sig * (1.0 - sig)
            dh_ref[:, 2 * hb * il:(2 * hb + 1) * il] = da.astype(_MXU)
            dh_ref[:, (2 * hb + 1) * il:(2 * hb + 2) * il] = dgate.astype(_MXU)
            cs_ref[:, 2 * hb * il:(2 * hb + 1) * il] += _fold8(da)
            cs_ref[:, (2 * hb + 1) * il:(2 * hb + 2) * il] += _fold8(dgate)

    row = lambda i: (i, 0)
    return pl.pallas_call(
        body, name=name, grid=(T // tm,),
        in_specs=[pl.BlockSpec((tm, C), row), pl.BlockSpec((tm, 2 * C), row)],
        out_specs=[pl.BlockSpec((tm, 2 * C), row), pl.BlockSpec((SUBLANES, 2 * C), lambda i: (0, 0))],
        out_shape=[jax.ShapeDtypeStruct((T, 2 * C), _MXU), jax.ShapeDtypeStruct((SUBLANES, 2 * C), F32)],
        compiler_params=_cp(("arbitrary",)),
    )(du, h1)


def _tril_mask(n):
    return lax.broadcasted_iota(jnp.int32, (n, n), 0) >= lax.broadcasted_iota(jnp.int32, (n, n), 1)


def _split_uv(t, il):
    u = jnp.concatenate([t[:, 0:il], t[:, 2 * il:3 * il]], axis=1)
    v = jnp.concatenate([t[:, il:2 * il], t[:, 3 * il:4 * il]], axis=1)
    return u, v


def _gmlp_gate_fwd(p, g, b, w_s, bsb, *, name):
    T, C2 = p.shape
    C = C2 // 2
    il = C // 2
    G, L, _ = w_s.shape
    assert G * L == C
    tm = _tile(T, 2 * L, L)

    def body(p_ref, g_ref, b_ref, ws_ref, bs_ref, us_ref, xh_ref, rs_ref, vn_ref, u_ref):
        z, _ = _gelu_parts(p_ref[...])
        u, v = _split_uv(z, il)
        vn, xh, rstd = _ln_rows(v, g_ref[...], b_ref[...])
        xh_ref[...] = xh
        rs_ref[...] = rstd
        vn_ref[...] = vn.astype(_MXU)
        u_ref[...] = u
        mask = _tril_mask(L)
        for gi in range(G):
            wc = jnp.where(mask, ws_ref[gi], 0.0).astype(_MXU)
            cols = slice(gi * L, (gi + 1) * L)
            for c in range(tm // L):
                rows = slice(c * L, (c + 1) * L)
                s = jnp.dot(wc, vn_ref[rows, cols], preferred_element_type=F32) + bs_ref[:, cols]
                us_ref[rows, cols] = (u_ref[rows, cols] * s).astype(_MXU)

    row = lambda i: (i, 0)
    fixed = lambda i: (0, 0)
    return pl.pallas_call(
        body, name=name, grid=(T // tm,),
        in_specs=[pl.BlockSpec((tm, C2), row), pl.BlockSpec((1, C), fixed), pl.BlockSpec((1, C), fixed),
                  pl.BlockSpec((G, L, L), lambda i: (0, 0, 0)), pl.BlockSpec((L, C), fixed)],
        out_specs=[pl.BlockSpec((tm, C), row), pl.BlockSpec((tm, C), row), pl.BlockSpec((tm, 1), row)],
        out_shape=[jax.ShapeDtypeStruct((T, C), _MXU), jax.ShapeDtypeStruct((T, C), F32),
                   jax.ShapeDtypeStruct((T, 1), F32)],
        scratch_shapes=[pltpu.VMEM((tm, C), _MXU), pltpu.VMEM((tm, C), F32)],
        compiler_params=_cp(("parallel",)),
    )(p, g, b, w_s, bsb)


def _gmlp_gate_bwd(dus, p, xh, rstd, g, b, w_s, bsb, *, name):
    T, C2 = p.shape
    C = C2 // 2
    il = C // 2
    G, L, _ = w_s.shape
    tm = _tile(T, 2 * L, L)

    def body(dus_ref, p_ref, xh_ref, rs_ref, g_ref, b_ref, ws_ref, bs_ref,
             dp_ref, dg_ref, db_ref, cs_ref, dws_ref, dbs_ref, vn_ref, u_ref, dvn_ref, du_ref):
        @pl.when(pl.program_id(0) == 0)
        def _():
            dg_ref[...] = jnp.zeros_like(dg_ref)
            db_ref[...] = jnp.zeros_like(db_ref)
            cs_ref[...] = jnp.zeros_like(cs_ref)
            dws_ref[...] = jnp.zeros_like(dws_ref)
            dbs_ref[...] = jnp.zeros_like(dbs_ref)

        z, gp = _gelu_parts(p_ref[...])
        u, _ = _split_uv(z, il)
        xh = xh_ref[...]
        gam = g_ref[...]
        vn_ref[...] = (xh * gam + b_ref[...]).astype(_MXU)
        u_ref[...] = u
        mask = _tril_mask(L)
        for gi in range(G):
            wc = jnp.where(mask, ws_ref[gi], 0.0).astype(_MXU)
            cols = slice(gi * L, (gi + 1) * L)
            for c in range(tm // L):
                rows = slice(c * L, (c + 1) * L)
                vnb = vn_ref[rows, cols]
                s = jnp.dot(wc, vnb, preferred_element_type=F32) + bs_ref[:, cols]
                d = dus_ref[rows, cols]
                du_ref[rows, cols] = d * s
                ds = d * u_ref[rows, cols]
                dbs_ref[:, cols] += ds
                dsb = ds.astype(_MXU)
                dw = lax.dot_general(dsb, vnb, (((1,), (1,)), ((), ())), preferred_element_type=F32)
                dws_ref[gi] += jnp.where(mask, dw, 0.0)
                dvn_ref[rows, cols] = lax.dot_general(wc, dsb, (((0,), (0,)), ((), ())), preferred_element_type=F32)
        dvn = dvn_ref[...]
        dg_ref[...] += _fold8(dvn * xh)
        db_ref[...] += _fold8(dvn)
        dv = _ln_bwd_rows(dvn, xh, rs_ref[...], gam)
        du = du_ref[...]
        for hb in range(2):
            for part, src in ((0, du), (1, dv)):
                lo = (2 * hb + part) * il
                dp = src[:, hb * il:(hb + 1) * il] * gp[:, lo:lo + il]
                dp_ref[:, lo:lo + il] = dp.astype(_MXU)
                cs_ref[:, lo:lo + il] += _fold8(dp)

    row = lambda i: (i, 0)
    fixed = lambda i: (0, 0)
    part_c = pl.BlockSpec((SUBLANES, C), fixed)
    return pl.pallas_call(
        body, name=name, grid=(T // tm,),
        in_specs=[pl.BlockSpec((tm, C), row), pl.BlockSpec((tm, C2), row), pl.BlockSpec((tm, C), row),
                  pl.BlockSpec((tm, 1), row), pl.BlockSpec((1, C), fixed), pl.BlockSpec((1, C), fixed),
                  pl.BlockSpec((G, L, L), lambda i: (0, 0, 0)), pl.BlockSpec((L, C), fixed)],
        out_specs=[pl.BlockSpec((tm, C2), row), part_c, part_c, pl.BlockSpec((SUBLANES, C2), fixed),
                   pl.BlockSpec((G, L, L), lambda i: (0, 0, 0)), pl.BlockSpec((L, C), fixed)],
        out_shape=[jax.ShapeDtypeStruct((T, C2), _MXU), jax.ShapeDtypeStruct((SUBLANES, C), F32),
                   jax.ShapeDtypeStruct((SUBLANES, C), F32), jax.ShapeDtypeStruct((SUBLANES, C2), F32),
                   jax.ShapeDtypeStruct((G, L, L), F32), jax.ShapeDtypeStruct((L, C), F32)],
        scratch_shapes=[pltpu.VMEM((tm, C), _MXU), pltpu.VMEM((tm, C), F32), pltpu.VMEM((tm, C), F32),
                        pltpu.VMEM((tm, C), F32)],
        compiler_params=_cp(("arbitrary",)),
    )(dus, p, xh, rstd, g, b, w_s, bsb)


def _ffn_conv(h, prev8, w_ref, b_ref):
    h1 = _shift_down(prev8, h, 1)
    h2 = _shift_down(prev8, h, 2)
    hc = w_ref[pl.ds(2, 1), :] * h + w_ref[pl.ds(1, 1), :] * h1 + w_ref[pl.ds(0, 1), :] * h2 + b_ref[...]
    return hc, h1, h2


def _ffn_up_fwd(xb, w, wl, b_up, w_dw, b_dw, *, S, name):
    T, D = xb.shape
    N = w.shape[-1]
    tn = N // N_CHIPS
    tm = _tile(S, 256)
    spt = S // tm

    def body(x_ref, w_ref, bu_ref, wd_ref, bd_ref, h_ref, f_ref, carry_ref):
        i = pl.program_id(1)

        @pl.when(i % spt == 0)
        def _():
            carry_ref[...] = jnp.zeros_like(carry_ref)

        h = jnp.dot(x_ref[...].astype(_MXU), w_ref[...].astype(_MXU), preferred_element_type=F32) + bu_ref[...]
        hq = h.astype(_HDT)
        h_ref[...] = hq
        h = hq.astype(F32)
        hc, _, _ = _ffn_conv(h, carry_ref[...], wd_ref, bd_ref)
        carry_ref[...] = h[tm - SUBLANES:tm]
        gte = hc[:, :tn]
        f_ref[...] = (gte * _sigmoid(gte) * hc[:, tn:]).astype(_MXU)

    pair = lambda j, i: (0, j)
    return pl.pallas_call(
        body, name=name, grid=(2, T // tm),
        in_specs=[pl.BlockSpec((tm, D), lambda j, i: (i, 0)),
                  pl.BlockSpec((None, D, 2 * tn), lambda j, i: (wl, 0, j)),
                  pl.BlockSpec((1, 2 * tn), pair), pl.BlockSpec((SUBLANES, 2 * tn), pair),
                  pl.BlockSpec((1, 2 * tn), pair)],
        out_specs=[pl.BlockSpec((tm, 2 * tn), lambda j, i: (i, j)), pl.BlockSpec((tm, tn), lambda j, i: (i, j))],
        out_shape=[jax.ShapeDtypeStruct((T, N), _HDT), jax.ShapeDtypeStruct((T, N // 2), _MXU)],
        scratch_shapes=[pltpu.VMEM((SUBLANES, 2 * tn), F32)],
        compiler_params=_cp(("parallel", "arbitrary")),
    )(xb, w, b_up, w_dw, b_dw)


def _ffn_bwd(dzb, w_down, wl, hs, w_dw, b_dw, *, S, name):
    T, D = dzb.shape
    N = hs.shape[1]
    tn = N // N_CHIPS
    tm = _tile(S, 256)
    spt = S // tm
    nt = T // tm
    hal = 16

    def body(dz_ref, wd_ref, h_ref, halo_ref, wc_ref, bc_ref, dh_ref, cs_ref, dw_ref, db_ref, carry_ref):
        i = pl.program_id(1)
        ii = nt - 1 - i

        @pl.when(i == 0)
        def _():
            cs_ref[...] = jnp.zeros_like(cs_ref)
            dw_ref[...] = jnp.zeros_like(dw_ref)
            db_ref[...] = jnp.zeros_like(db_ref)

        df = lax.dot_general(dz_ref[...].astype(_MXU), wd_ref[...].astype(_MXU), (((1,), (1,)), ((), ())),
                             preferred_element_type=F32)
        h = h_ref[...].astype(F32)
        prev8 = halo_ref[...].astype(F32)[hal - SUBLANES:hal]
        prev8 = jnp.where(ii % spt == 0, 0.0, prev8)
        hc, h1, h2 = _ffn_conv(h, prev8, wc_ref, bc_ref)
        gte, val = hc[:, :tn], hc[:, tn:]
        sig = _sigmoid(gte)
        dval = df * (gte * sig)
        dg = df * val * (sig * (1.0 + gte * (1.0 - sig)))
        dhc = jnp.concatenate([dg, dval], axis=1)
        db_ref[...] += _fold8(dhc)
        dw_ref[2] += _fold8(dhc * h)
        dw_ref[1] += _fold8(dhc * h1)
        dw_ref[0] += _fold8(dhc * h2)
        nxt = jnp.where((ii + 1) % spt == 0, 0.0, carry_ref[...])
        dh = (wc_ref[pl.ds(2, 1), :] * dhc + wc_ref[pl.ds(1, 1), :] * _shift_up(dhc, nxt, 1)
              + wc_ref[pl.ds(0, 1), :] * _shift_up(dhc, nxt, 2))
        carry_ref[...] = dhc[0:SUBLANES]
        cs_ref[...] += _fold8(dh)
        dh_ref[...] = dh.astype(_MXU)

    pair = lambda j, i: (0, j)
    rev = lambda j, i: (nt - 1 - i, j)
    return pl.pallas_call(
        body, name=name, grid=(2, nt),
        in_specs=[pl.BlockSpec((tm, D), lambda j, i: (nt - 1 - i, 0)),
                  pl.BlockSpec((None, tn, D), lambda j, i: (wl, j, 0)),
                  pl.BlockSpec((tm, 2 * tn), rev),
                  pl.BlockSpec((hal, 2 * tn), lambda j, i: (jnp.maximum((nt - 1 - i) * (tm // hal) - 1, 0), j)),
                  pl.BlockSpec((SUBLANES, 2 * tn), pair), pl.BlockSpec((1, 2 * tn), pair)],
        out_specs=[pl.BlockSpec((tm, 2 * tn), rev), pl.BlockSpec((SUBLANES, 2 * tn), pair),
                   pl.BlockSpec((3, SUBLANES, 2 * tn), lambda j, i: (0, 0, j)),
                   pl.BlockSpec((SUBLANES, 2 * tn), pair)],
        out_shape=[jax.ShapeDtypeStruct((T, N), _MXU), jax.ShapeDtypeStruct((SUBLANES, N), F32),
                   jax.ShapeDtypeStruct((3, SUBLANES, N), F32), jax.ShapeDtypeStruct((SUBLANES, N), F32)],
        scratch_shapes=[pltpu.VMEM((SUBLANES, 2 * tn), F32)],
        compiler_params=_cp(("parallel", "arbitrary")),
    )(dzb, w_down, hs, hs, w_dw, b_dw)


def _sum_pieces(g, r, me, layer, acc, n_layers, *, name):
    _, pr, pc = g.shape
    tr = _tile(pr, 128)

    def body(me_ref, g_ref, r_ref, *rest):
        o_ref = rest[-1]
        total = g_ref[...].astype(F32)
        for s in range(N_DEV - 1):
            total = total + r_ref[s].astype(F32)
        o_ref[...] = total

    in_specs = [pl.BlockSpec((None, tr, pc), lambda i, me_ref: (me_ref[0], i, 0)),
                pl.BlockSpec((N_DEV - 1, tr, pc), lambda i, me_ref: (0, i, 0))]
    operands = [me, g, r]
    aliases = {}
    if acc is not None:
        in_specs.append(ANY)
        operands.append(acc)
        aliases = {3: 0}
    return pl.pallas_call(
        body, name=name,
        grid_spec=pltpu.PrefetchScalarGridSpec(
            num_scalar_prefetch=1, grid=(pr // tr,), in_specs=in_specs,
            out_specs=pl.BlockSpec((None, tr, pc), lambda i, me_ref: (layer, i, 0))),
        out_shape=jax.ShapeDtypeStruct((n_layers, pr, pc), F32),
        input_output_aliases=aliases,
        compiler_params=_cp(("parallel",)),
    )(*operands)


def _adam_math(w, g, m, v):
    bc1 = 1.0 - ADAM_B1 ** ADAM_STEP
    bc2 = 1.0 - ADAM_B2 ** ADAM_STEP
    m = ADAM_B1 * m + (1.0 - ADAM_B1) * g
    v = ADAM_B2 * v + (1.0 - ADAM_B2) * (g * g)
    return -ADAM_LR * ((m / bc1) / (jnp.sqrt(v / bc2) + ADAM_EPS) + ADAM_WD * w), m, v


def _adam(w, g, m, v, *, name):
    R, C = w.shape
    tr = _tile(R, 256)

    def body(w_ref, g_ref, m_ref, v_ref, d_ref, mo_ref, vo_ref):
        d_ref[...], mo_ref[...], vo_ref[...] = _adam_math(w_ref[...], g_ref[...], m_ref[...], v_ref[...])

    spec = pl.BlockSpec((tr, C), lambda i: (i, 0))
    return pl.pallas_call(
        body, name=name, grid=(R // tr,), in_specs=[spec] * 4, out_specs=[spec] * 3,
        out_shape=[jax.ShapeDtypeStruct((R, C), F32)] * 3,
        compiler_params=_cp(("parallel",)),
    )(w, g, m, v)


def _adam_halves(w, own, got, m, v, core, *, name):
    L, R, C = w.shape
    rh = R // 2
    tr = _tile(rh, 256)
    nt = rh // tr

    def body(c_ref, w_ref, own_ref, got_ref, m_ref, v_ref, g_ref, d_ref, mo_ref, vo_ref):
        g = jnp.where(pl.program_id(1) == c_ref[0], own_ref[...], got_ref[...])
        g_ref[...] = g
        d_ref[...], mo_ref[...], vo_ref[...] = _adam_math(w_ref[...], g, m_ref[...], v_ref[...])

    full = pl.BlockSpec((None, tr, C), lambda l, h, t, c_ref: (l, h * nt + t, 0))
    half = pl.BlockSpec((None, tr, C), lambda l, h, t, c_ref: (l, t, 0))
    return pl.pallas_call(
        body, name=name,
        grid_spec=pltpu.PrefetchScalarGridSpec(
            num_scalar_prefetch=1, grid=(L, 2, nt), in_specs=[full, half, half, full, full], out_specs=[full] * 4),
        out_shape=[jax.ShapeDtypeStruct((L, R, C), F32)] * 4,
        compiler_params=_cp(("parallel", "parallel", "parallel")),
    )(core, w, own, got, m, v)


def _remote(src, dst, send, recv, dev):
    return pltpu.make_async_remote_copy(src_ref=src, dst_ref=dst, send_sem=send, recv_sem=recv,
                                        device_id=dev, device_id_type=MESH)


def _place_w(shard, pos, *, axis, name):
    L, R, C = shard.shape
    tr = _tile(R, 512, 16)
    nt = R // tr
    if axis == 2:
        out_shape = (L, R, N_CHIPS * C)
        out_map = lambda l, t, q: (l, t, q[0])
    else:
        out_shape = (L, N_CHIPS * R, C)
        out_map = lambda l, t, q: (l, q[0] * nt + t, 0)

    def body(q_ref, s_ref, o_ref):
        o_ref[...] = s_ref[...].astype(_WIRE)

    return pl.pallas_call(
        body, name=name,
        grid_spec=pltpu.PrefetchScalarGridSpec(
            num_scalar_prefetch=1, grid=(L, nt),
            in_specs=[pl.BlockSpec((None, tr, C), lambda l, t, q: (l, t, 0))],
            out_specs=pl.BlockSpec((None, tr, C), out_map)),
        out_shape=jax.ShapeDtypeStruct(out_shape, _WIRE),
        compiler_params=_cp(("parallel", "parallel")),
    )(pos, shard)


def _allgather_w(placed, *, axis, perm, name):
    if axis == 2:
        L, R, C = placed.shape[0], placed.shape[1], placed.shape[2] // N_CHIPS
    else:
        L, R, C = placed.shape[0], placed.shape[1] // N_CHIPS, placed.shape[2]
    rh = R // 2

    def body(in_ref, o_ref, send, recv):
        del in_ref
        x, y, c = lax.axis_index("x"), lax.axis_index("y"), lax.axis_index("c")

        def win(px, py, h):
            q = 2 * px + py
            if perm:
                q = _perm_idx(q)
            if axis == 2:
                return o_ref.at[:, pl.ds(pl.multiple_of(h * rh, 16), rh), pl.ds(pl.multiple_of(q * C, LANES), C)]
            return o_ref.at[:, pl.ds(pl.multiple_of(q * R + h * rh, 16), rh), :]

        chips = [(1 - x, y), (x, 1 - y), (1 - x, 1 - y)]
        first = [_remote(win(x, y, c), win(x, y, c), send.at[i], recv.at[i], (px, py, c))
                 for i, (px, py) in enumerate(chips)]
        for cp in first:
            cp.start()
        passed = []
        for i, (px, py) in enumerate(chips):
            _remote(win(x, y, c), win(px, py, c), send.at[i], recv.at[i], (px, py, c)).wait_recv()
            fw = _remote(win(px, py, c), win(px, py, c), send.at[3 + i], recv.at[3 + i], (x, y, 1 - c))
            fw.start()
            passed.append(fw)
        for i, (px, py) in enumerate(chips):
            _remote(win(x, y, c), win(px, py, 1 - c), send.at[3 + i], recv.at[3 + i], (x, y, 1 - c)).wait_recv()
        for cp in first + passed:
            cp.wait_send()

    return pl.pallas_call(
        body, name=name, in_specs=[ANY], out_specs=ANY,
        out_shape=jax.ShapeDtypeStruct(placed.shape, placed.dtype), input_output_aliases={0: 0},
        scratch_shapes=[pltpu.SemaphoreType.DMA((6,)), pltpu.SemaphoreType.DMA((6,))],
    )(placed)


def _flip(x, y, c, f):
    return ((1 - x) if f & 4 else x, (1 - y) if f & 2 else y, (1 - c) if f & 1 else c)


def _rs_exchange(g, *, name):
    _, pr, pc = g.shape

    def body(g_ref, r_ref, send, recv):
        x, y, c = lax.axis_index("x"), lax.axis_index("y"), lax.axis_index("c")
        cps = []
        for f in range(1, N_DEV):
            tx, ty, tcx = _flip(x, y, c, f)
            cp = _remote(g_ref.at[4 * tx + 2 * ty + tcx], r_ref.at[f - 1], send.at[f - 1], recv.at[f - 1], (tx, ty, tcx))
            cp.start()
            cps.append(cp)
        for cp in cps:
            cp.wait_recv()
        for cp in cps:
            cp.wait_send()

    return pl.pallas_call(
        body, name=name, in_specs=[ANY], out_specs=ANY,
        out_shape=jax.ShapeDtypeStruct((N_DEV - 1, pr, pc), g.dtype),
        scratch_shapes=[pltpu.SemaphoreType.DMA((N_DEV - 1,)), pltpu.SemaphoreType.DMA((N_DEV - 1,))],
    )(g)


def _pair_exchange(own, *, name):
    def body(own_ref, got_ref, send, recv):
        x, y, c = lax.axis_index("x"), lax.axis_index("y"), lax.axis_index("c")
        cp = _remote(own_ref, got_ref, send, recv, (x, y, 1 - c))
        cp.start()
        cp.wait_recv()
        cp.wait_send()

    return pl.pallas_call(
        body, name=name, in_specs=[ANY], out_specs=ANY, out_shape=jax.ShapeDtypeStruct(own.shape, own.dtype),
        scratch_shapes=[pltpu.SemaphoreType.DMA, pltpu.SemaphoreType.DMA],
    )(own)


def _allreduce_flat(vec, *, name):
    n = vec.shape[0]
    unit = N_DEV * SUBLANES * LANES
    npad = -(-n // unit) * unit
    rows = npad // (N_DEV * LANES)
    xin = jnp.pad(vec, (0, npad - n)).reshape(N_DEV, rows, LANES)

    def body(x_ref, y_ref, a_ref, send_a, recv_a, send_b, recv_b):
        x, y, c = lax.axis_index("x"), lax.axis_index("y"), lax.axis_index("c")
        me = 4 * x + 2 * y + c
        a_ref[me] = x_ref[me]
        sends, recvs = [], []
        for f in range(1, N_DEV):
            dev = _flip(x, y, c, f)
            t = 4 * dev[0] + 2 * dev[1] + dev[2]
            cp = _remote(x_ref.at[t], a_ref.at[me], send_a.at[f - 1], recv_a.at[f - 1], dev)
            cp.start()
            sends.append(cp)
            recvs.append(_remote(x_ref.at[me], a_ref.at[t], send_a.at[f - 1], recv_a.at[f - 1], dev))
        for cp in recvs:
            cp.wait_recv()
        for cp in sends:
            cp.wait_send()
        acc = a_ref[0]
        for s in range(1, N_DEV):
            acc = acc + a_ref[s]
        y_ref[me] = acc
        sends, recvs = [], []
        for f in range(1, N_DEV):
            dev = _flip(x, y, c, f)
            t = 4 * dev[0] + 2 * dev[1] + dev[2]
            cp = _remote(y_ref.at[me], y_ref.at[me], send_b.at[f - 1], recv_b.at[f - 1], dev)
            cp.start()
            sends.append(cp)
            recvs.append(_remote(y_ref.at[me], y_ref.at[t], send_b.at[f - 1], recv_b.at[f - 1], dev))
        for cp in recvs:
            cp.wait_recv()
        for cp in sends:
            cp.wait_send()

    vm = pl.BlockSpec(memory_space=pltpu.VMEM)
    out = pl.pallas_call(
        body, name=name, in_specs=[vm], out_specs=vm,
        out_shape=jax.ShapeDtypeStruct((N_DEV, rows, LANES), F32),
        scratch_shapes=[pltpu.VMEM((N_DEV, rows, LANES), F32)] + [pltpu.SemaphoreType.DMA((N_DEV - 1,))] * 4,
        compiler_params=_cp(),
    )(xin)
    return out.reshape(npad)[:n]


def _perm_cols(v, blocks=N_CHIPS):
    lead, n = v.shape[:-1], v.shape[-1]
    return v.reshape(lead + (blocks, n // blocks))[..., PERM, :].reshape(lead + (n,))


def _pack(arrs):
    return jnp.concatenate([a.reshape(-1).astype(F32) for a in arrs])


def _unpack(flat, shapes):
    out, pos = [], 0
    for s in shapes:
        n = 1
        for d in s:
            n *= d
        out.append(flat[pos:pos + n].reshape(s))
        pos += n
    return out


def kernel(x, conv_w_in, conv_b_in, conv_w_dw, conv_b_dw, conv_ln_g, conv_ln_b, conv_w_out, conv_b_out, gmlp_w_in, gmlp_b_in, gmlp_ln_g, gmlp_ln_b, gmlp_w_s, gmlp_b_s, gmlp_w_out, gmlp_b_out, ffn_w_up, ffn_b_up, ffn_w_dw, ffn_b_dw, ffn_w_down, ffn_b_down, norm1_g, norm1_b, norm2_g, norm2_b, loss_target, m_conv_w_in, m_conv_b_in, m_conv_w_dw, m_conv_b_dw, m_conv_ln_g, m_conv_ln_b, m_conv_w_out, m_conv_b_out, m_gmlp_w_in, m_gmlp_b_in, m_gmlp_ln_g, m_gmlp_ln_b, m_gmlp_w_s, m_gmlp_b_s, m_gmlp_w_out, m_gmlp_b_out, m_ffn_w_up, m_ffn_b_up, m_ffn_w_dw, m_ffn_b_dw, m_ffn_w_down, m_ffn_b_down, m_norm1_g, m_norm1_b, m_norm2_g, m_norm2_b, v_conv_w_in, v_conv_b_in, v_conv_w_dw, v_conv_b_dw, v_conv_ln_g, v_conv_ln_b, v_conv_w_out, v_conv_b_out, v_gmlp_w_in, v_gmlp_b_in, v_gmlp_ln_g, v_gmlp_ln_b, v_gmlp_w_s, v_gmlp_b_s, v_gmlp_w_out, v_gmlp_b_out, v_ffn_w_up, v_ffn_b_up, v_ffn_w_dw, v_ffn_b_dw, v_ffn_w_down, v_ffn_b_down, v_norm1_g, v_norm1_b, v_norm2_g, v_norm2_b):
    P = dict(locals())
    WEIGHTS = ['conv_w_in', 'conv_b_in', 'conv_w_dw', 'conv_b_dw', 'conv_ln_g', 'conv_ln_b', 'conv_w_out',
               'conv_b_out', 'gmlp_w_in', 'gmlp_b_in', 'gmlp_ln_g', 'gmlp_ln_b', 'gmlp_w_s', 'gmlp_b_s',
               'gmlp_w_out', 'gmlp_b_out', 'ffn_w_up', 'ffn_b_up', 'ffn_w_dw', 'ffn_b_dw', 'ffn_w_down',
               'ffn_b_down', 'norm1_g', 'norm1_b', 'norm2_g', 'norm2_b']
    BIG = ['conv_w_in', 'conv_w_out', 'gmlp_w_in', 'gmlp_w_out', 'ffn_w_up', 'ffn_w_down']
    SMALL_SHARDED = {'conv_w_dw': 2, 'gmlp_b_in': 1, 'gmlp_ln_g': 1, 'gmlp_ln_b': 1, 'gmlp_b_out': 1, 'ffn_w_dw': 2}

    B, S, D = x.shape
    T = B * S
    depth = norm1_g.shape[0]
    alpha = (2.0 * depth) ** 0.25
    C = conv_w_out.shape[-1]
    F2 = ffn_b_up.shape[-1]
    G, L = gmlp_w_s.shape[1], gmlp_w_s.shape[2]
    xi, yi, ci = lax.axis_index("x"), lax.axis_index("y"), lax.axis_index("c")
    shard = 2 * xi + yi

    i32 = lambda v: jnp.reshape(v, (1,)).astype(jnp.int32)
    pos_plain, pos_perm = i32(shard), i32(_perm_idx(shard))
    me_id, core_id = i32(4 * xi + 2 * yi + ci), i32(ci)

    def gather(w, axis, perm, name):
        placed = _place_w(w, pos_perm if perm else pos_plain, axis=axis, name="place_" + name)
        return _allgather_w(placed, axis=axis, perm=perm, name="ag_" + name)

    cw_in = gather(conv_w_in, 2, True, "conv_w_in")
    cw_out = gather(conv_w_out, 1, False, "conv_w_out")
    gw_in = gather(gmlp_w_in, 2, True, "gmlp_w_in")
    gw_out = gather(gmlp_w_out, 1, False, "gmlp_w_out")
    fw_up = gather(ffn_w_up, 2, True, "ffn_w_up")
    fw_down = gather(ffn_w_down, 1, False, "ffn_w_down")

    sm_names = list(SMALL_SHARDED)
    sm_shapes = [P[n].shape for n in sm_names]
    mine = _pack([P[n] for n in sm_names]) * (ci == 0).astype(F32)
    buf = jnp.zeros((N_CHIPS, mine.shape[0]), F32)
    buf = lax.dynamic_update_slice(buf, mine[None], (shard, 0))
    gathered = _allreduce_flat(buf.reshape(-1), name="ag_small").reshape(N_CHIPS, -1)
    full = {}
    for n, parts in zip(sm_names, zip(*[_unpack(gathered[k], sm_shapes) for k in range(N_CHIPS)])):
        full[n] = jnp.concatenate(parts, axis=SMALL_SHARDED[n])
    for n in WEIGHTS:
        if n not in BIG and n not in full:
            full[n] = P[n]

    assert G * L == C, "a gMLP group must be as wide as a chunk is long"

    def row(v):
        return v.reshape(1, -1)

    def pad_rows(v, r):
        return jnp.pad(v, ((0, r - v.shape[0]), (0, 0)))

    xf = x.reshape(T, D)
    saved = []
    cur, cur_b = xf, xf
    for i in range(depth):
        j = i // 2
        sv = {'x': cur, 'xb': cur_b}
        if i % 2 == 0:
            b_in = row(_perm_cols(full['conv_b_in'][j]))
            h1 = _mm(cur_b, cw_in, bl=j, bias=b_in, tm=_tile(T, 512), tn=_tile(2 * C, 1024, LANES), tk=D,
                     name=f"conv_in_{j}")
            wdw = pad_rows(full['conv_w_dw'][j], CONV_TAPS_PAD)
            dwo = _conv_fwd(h1, wdw, row(full['conv_b_dw'][j]), B=B, S=S, name=f"conv_dw_{j}")
            s_act, xhc, rsc = _ln_silu_fwd(dwo, row(full['conv_ln_g'][j]), row(full['conv_ln_b'][j]),
                                           name=f"conv_ln_{j}")
            sv.update(h1=h1, wdw=wdw, act=s_act, xhc=xhc, rsc=rsc)
            y1 = _mm_res_ln(s_act, cw_out, j, row(full['conv_b_out'][j]), cur, alpha, row(norm1_g[i]),
                            row(norm1_b[i]), name=f"conv_out_ln_{j}")
        else:
            b_in = row(_perm_cols(full['gmlp_b_in'][j]))
            pre = _mm(cur_b, gw_in, bl=j, bias=b_in, tm=_tile(T, 512), tn=_tile(2 * C, 1024, LANES), tk=D,
                      name=f"gmlp_in_{j}")
            bsb = jnp.repeat(gmlp_b_s[j].T, L, axis=1)
            us, xhv, rsv = _gmlp_gate_fwd(pre, row(full['gmlp_ln_g'][j]), row(full['gmlp_ln_b'][j]), gmlp_w_s[j],
                                          bsb, name=f"gmlp_gate_{j}")
            sv.update(pre=pre, bsb=bsb, act=us, xhv=xhv, rsv=rsv)
            y1 = _mm_res_ln(us, gw_out, j, row(full['gmlp_b_out'][j]), cur, alpha, row(norm1_g[i]),
                            row(norm1_b[i]), name=f"gmlp_out_ln_{j}")
        x1, x1b, xh1, rs1 = y1
        wdw3 = pad_rows(_perm_cols(full['ffn_w_dw'][i]), SUBLANES)
        bdw3 = row(_perm_cols(ffn_b_dw[i]))
        hs, f_act = _ffn_up_fwd(x1b, fw_up, i, row(_perm_cols(ffn_b_up[i])), wdw3, bdw3, S=S, name=f"ffn_up_{i}")
        x2, x2b, xh2, rs2 = _mm_res_ln(f_act, fw_down, i, row(ffn_b_down[i]), x1, alpha, row(norm2_g[i]),
                                       row(norm2_b[i]), name=f"ffn_down_ln_{i}")
        sv.update(x1=x1, x1b=x1b, xh1=xh1, rs1=rs1, hs=hs, f=f_act, wdw3=wdw3, bdw3=bdw3, xh2=xh2, rs2=rs2)
        saved.append(sv)
        cur, cur_b = x2, x2b

    sg = {n: [None] * full[n].shape[0] for n in WEIGHTS if n not in BIG}
    pieces = {n: [None] * P[n].shape[0] for n in BIG}
    tgt = loss_target.reshape(T, D)
    dcur = None
    loss_part = None
    tk_t = _tile(T, 512)
    for i in reversed(range(depth)):
        j = i // 2
        sv = saved[i]
        if dcur is None:
            dz2, dz2b, dg, db, cs, ls = _ln_bwd(cur, sv['xh2'], sv['rs2'], row(norm2_g[i]), target=tgt,
                                                name=f"ln2_bwd_head_{i}")
            loss_part = ls
        else:
            dz2, dz2b, dg, db, cs = _ln_bwd(dcur, sv['xh2'], sv['rs2'], row(norm2_g[i]), name=f"ln2_bwd_{i}")
        sg['norm2_g'][i], sg['norm2_b'][i], sg['ffn_b_down'][i] = dg.sum(0), db.sum(0), cs.sum(0)
        Fh = F2 // 2
        pieces['ffn_w_down'][i] = _mm(sv['f'], dz2b, ta=True, out_dtype=_WIRE, tm=Fh // 2, tn=_tile(D, 1024, LANES),
                                      tk=tk_t, name=f"ffn_dw_down_{i}", pieces=('row',))
        dh, csu, dwd, dbd = _ffn_bwd(dz2b, fw_down, i, sv['hs'], sv['wdw3'], sv['bdw3'], S=S, name=f"ffn_bwd_{i}")
        sg['ffn_b_up'][i] = _perm_cols(csu.sum(0))
        sg['ffn_w_dw'][i] = _perm_cols(dwd.sum(1))
        sg['ffn_b_dw'][i] = _perm_cols(dbd.sum(0))
        pieces['ffn_w_up'][i] = _mm(sv['x1b'], dh, ta=True, out_dtype=_WIRE, tm=D // 2, tn=F2 // N_CHIPS, tk=tk_t,
                                    name=f"ffn_dw_up_{i}", pieces=('col', True))
        dx1 = _mm(dh, fw_up, bl=i, tb=True, res=dz2, res_scale=alpha, tm=_tile(T, 512), tn=_tile(D, 1024, LANES),
                  tk=F2 // N_CHIPS, name=f"ffn_dx_{i}")
        dz1, dz1b, dg, db, cs = _ln_bwd(dx1, sv['xh1'], sv['rs1'], row(norm1_g[i]), name=f"ln1_bwd_{i}")
        sg['norm1_g'][i], sg['norm1_b'][i] = dg.sum(0), db.sum(0)
        if i % 2 == 0:
            sg['conv_b_out'][j] = cs.sum(0)
            pieces['conv_w_out'][j] = _mm(sv['act'], dz1b, ta=True, out_dtype=_WIRE, tm=_tile(C, 512),
                                          tn=_tile(D, 1024, LANES), tk=tk_t, name=f"conv_dw_out_{j}", pieces=('row',))
            ds = _mm(dz1b, cw_out, bl=j, tb=True, tm=_tile(T, 512), tn=_tile(C, 1024, LANES), tk=_tile(D, 1024, LANES),
                     name=f"conv_ds_{j}")
            ddw, dg, db = _ln_silu_bwd(ds, sv['xhc'], sv['rsc'], row(full['conv_ln_g'][j]),
                                       row(full['conv_ln_b'][j]), name=f"conv_ln_bwd_{j}")
            sg['conv_ln_g'][j], sg['conv_ln_b'][j] = dg.sum(0), db.sum(0)
            dglu, dwk, dbk = _conv_bwd(ddw, sv['h1'], sv['wdw'], B=B, S=S, name=f"conv_dw_bwd_{j}")
            sg['conv_w_dw'][j] = dwk.sum(1)[:conv_w_dw.shape[1]]
            sg['conv_b_dw'][j] = dbk.sum(0)
            dh1, csi = _glu_bwd(dglu, sv['h1'], name=f"conv_glu_bwd_{j}")
            sg['conv_b_in'][j] = _perm_cols(csi.sum(0))
            w_in_full, fam = cw_in, 'conv_w_in'
        else:
            sg['gmlp_b_out'][j] = cs.sum(0)
            pieces['gmlp_w_out'][j] = _mm(sv['act'], dz1b, ta=True, out_dtype=_WIRE, tm=_tile(C, 512),
                                          tn=_tile(D, 1024, LANES), tk=tk_t, name=f"gmlp_dw_out_{j}", pieces=('row',))
            dus = _mm(dz1b, gw_out, bl=j, tb=True, tm=_tile(T, 512), tn=_tile(C, 1024, LANES),
                      tk=_tile(D, 1024, LANES), name=f"gmlp_dus_{j}")
            dh1, dg, db, csi, dws, dbs = _gmlp_gate_bwd(dus, sv['pre'], sv['xhv'], sv['rsv'], row(full['gmlp_ln_g'][j]),
                                                        row(full['gmlp_ln_b'][j]), gmlp_w_s[j], sv['bsb'],
                                                        name=f"gmlp_gate_bwd_{j}")
            sg['gmlp_ln_g'][j], sg['gmlp_ln_b'][j] = dg.sum(0), db.sum(0)
            sg['gmlp_b_in'][j] = _perm_cols(csi.sum(0))
            sg['gmlp_w_s'][j] = dws
            sg['gmlp_b_s'][j] = dbs.reshape(L, G, L).sum(-1).T
            w_in_full, fam = gw_in, 'gmlp_w_in'
        pieces[fam][j] = _mm(sv['xb'], dh1, ta=True, out_dtype=_WIRE, tm=D // 2, tn=(2 * C) // N_CHIPS, tk=tk_t,
                             name=f"{fam}_dw_{j}", pieces=('col', True))
        dcur = _mm(dh1, w_in_full, bl=j, tb=True, res=dz1, res_scale=alpha, tm=_tile(T, 512),
                   tn=_tile(D, 1024, LANES), tk=_tile(2 * C, 1024, LANES), name=f"{fam}_dx_{j}")
    grad_x = dcur.reshape(B, S, D)

    small_names = [n for n in WEIGHTS if n not in BIG]
    small_full = [jnp.stack(sg[n]) for n in small_names]
    flat = _pack(small_full + [loss_part])
    red = _allreduce_flat(flat, name="ar_small")
    red_parts = _unpack(red, [a.shape for a in small_full] + [loss_part.shape])
    loss = (0.5 / D) * jnp.sum(red_parts[-1])
    grads = {}
    for n, g in zip(small_names, red_parts[:-1]):
        if n in SMALL_SHARDED:
            ax = SMALL_SHARDED[n]
            width = P[n].shape[ax]
            g = lax.dynamic_slice_in_dim(g, shard * width, width, axis=ax)
        grads[n] = g

    big_out = {}
    for n in BIG:
        own = None
        for l, pc_ in enumerate(pieces[n]):
            r = _rs_exchange(pc_, name=f"rs_{n}_{l}")
            own = _sum_pieces(pc_, r, me_id, l, own, len(pieces[n]), name=f"sum_{n}_{l}")
        got = _pair_exchange(own, name=f"px_{n}")
        big_out[n] = _adam_halves(P[n], own, got, P['m_' + n], P['v_' + n], core_id, name=f"adam_{n}")

    shapes = [P[n].shape for n in small_names]
    n_small = sum(functools.reduce(lambda p_, d_: p_ * d_, s_, 1) for s_ in shapes)
    unit = SUBLANES * LANES
    npad = -(-n_small // unit) * unit

    def flat2d(arrs, fill=0.0):
        v = _pack(arrs)
        return jnp.pad(v, (0, npad - n_small), constant_values=fill).reshape(-1, LANES)

    dl, mo, vo = _adam(flat2d([P[n] for n in small_names]), flat2d([grads[n] for n in small_names]),
                       flat2d([P['m_' + n] for n in small_names]),
                       flat2d([P['v_' + n] for n in small_names], fill=1.0), name="adam_small")
    small_out = {n: [grads[n], None, None, None] for n in small_names}
    for k, t in enumerate((dl, mo, vo)):
        for n, a in zip(small_names, _unpack(t.reshape(-1), shapes)):
            small_out[n][k + 1] = a

    outs = [loss, grad_x]
    for k in range(4):
        for n in WEIGHTS:
            outs.append(big_out[n][k] if n in BIG else small_out[n][k])
    return tuple(outs)
```

```python
import functools

import jax
import jax.numpy as jnp
from jax import lax
from jax.experimental import pallas as pl
from jax.experimental.pallas import tpu as pltpu

F32 = jnp.float32
_MXU = jnp.bfloat16
_WIRE = jnp.bfloat16
_HDT = jnp.bfloat16
LN_EPS = 1e-5
ADAM_LR, ADAM_B1, ADAM_B2, ADAM_EPS, ADAM_WD, ADAM_STEP = 0.001, 0.9, 0.999, 1e-08, 0.01, 10
N_CHIPS = 4
N_DEV = 8
LANES = 128
SUBLANES = 8
CONV_TAPS_PAD = 32
VMEM_LIMIT = 56 << 20
MESH = pl.DeviceIdType.MESH
ANY = pl.BlockSpec(memory_space=pl.ANY)
HBM = pl.BlockSpec(memory_space=pltpu.HBM)
SEMS = pl.BlockSpec(memory_space=pltpu.SEMAPHORE)
EFFECT = pltpu.SideEffectType.DATAFLOW_SIDE_EFFECTING
PERM = (0, 2, 1, 3)


def _cp(sem=None):
    return pltpu.CompilerParams(dimension_semantics=sem, vmem_limit_bytes=VMEM_LIMIT)


def _tile(dim, pref, mult=SUBLANES):
    if dim <= pref:
        return dim
    t = (pref // mult) * mult
    while t > mult and dim % t:
        t -= mult
    assert dim % t == 0, (dim, pref, mult)
    return t


def _perm_idx(q):
    return (q % 2) * 2 + q // 2


def _fold8(t):
    r, n = t.shape
    return t.reshape(r // SUBLANES, SUBLANES, n).sum(axis=0)


def _ln_rows(z, g, b):
    mu = jnp.mean(z, axis=-1, keepdims=True)
    xc = z - mu
    var = jnp.mean(xc * xc, axis=-1, keepdims=True)
    rstd = lax.rsqrt(var + LN_EPS)
    xh = xc * rstd
    return xh * g + b, xh, rstd


def _ln_bwd_rows(dy, xh, rstd, g):
    dxh = dy * g
    m1 = jnp.mean(dxh, axis=-1, keepdims=True)
    m2 = jnp.mean(dxh * xh, axis=-1, keepdims=True)
    return rstd * (dxh - m1 - xh * m2)


def _sigmoid(v):
    return 1.0 / (1.0 + jnp.exp(-v))


def _gelu_parts(p):
    cdf = 0.5 * (1.0 + lax.erf(p * 0.7071067811865476))
    pdf = jnp.exp(-0.5 * p * p) * 0.3989422804014327
    return p * cdf, cdf + p * pdf


def _shift_down(prev8, t, s):
    ext = jnp.concatenate([prev8, t], axis=0)
    return pltpu.roll(ext, s, 0)[SUBLANES:]


def _shift_up(t, next8, s):
    n = t.shape[0]
    ext = jnp.concatenate([t, next8], axis=0)
    return pltpu.roll(ext, n + SUBLANES - s, 0)[:n]


def _mm(a, b, *, ta=False, tb=False, bl=None, bias=None, res=None, res_scale=1.0, out_dtype=F32,
        tm, tn, tk, name, pieces=None, deps=None):
    M, K = (a.shape[1], a.shape[0]) if ta else a.shape
    bs = b.shape[1:] if bl is not None else b.shape
    N, Kb = (bs[0], bs[1]) if tb else (bs[1], bs[0])
    assert K == Kb and M % tm == 0 and N % tn == 0 and K % tk == 0, (a.shape, b.shape, tm, tn, tk)
    gm, gn, gk = M // tm, N // tn, K // tk
    a_spec = pl.BlockSpec((tk, tm), lambda i, j, k: (k, i)) if ta else pl.BlockSpec((tm, tk), lambda i, j, k: (i, k))
    bblk = (tn, tk) if tb else (tk, tn)
    bmap = (lambda i, j, k: (j, k)) if tb else (lambda i, j, k: (k, j))
    if bl is not None:
        b_spec = pl.BlockSpec((None,) + bblk, lambda i, j, k: (bl,) + bmap(i, j, k))
    else:
        b_spec = pl.BlockSpec(bblk, bmap)
    in_specs, operands = [a_spec, b_spec], [a, b]
    if bias is not None:
        in_specs.append(pl.BlockSpec((1, tn), lambda i, j, k: (0, j)))
        operands.append(bias)
    if res is not None:
        in_specs.append(pl.BlockSpec((tm, tn), lambda i, j, k: (i, j)))
        operands.append(res)
    n_dep = len(deps) if deps else 0
    if n_dep:
        in_specs += [ANY] * n_dep
        operands += deps
        del deps[:]
    if pieces is None:
        out_shape = jax.ShapeDtypeStruct((M, N), out_dtype)
        out_spec = pl.BlockSpec((tm, tn), lambda i, j, k: (i, j))
        ppb = pr = None
    elif pieces[0] == 'col':
        pr, pc = M // 2, N // N_CHIPS
        assert tm == pr and pc % tn == 0
        ppb, per = 1, pc // tn
        perm = pieces[1]
        out_shape = jax.ShapeDtypeStruct((N_DEV, pr, pc), out_dtype)
        out_spec = pl.BlockSpec(
            (1, pr, tn), lambda i, j, k: (2 * (_perm_idx(j // per) if perm else j // per) + i, 0, j % per))
    else:
        pr = M // N_DEV
        assert tm % pr == 0
        ppb = tm // pr
        out_shape = jax.ShapeDtypeStruct((N_DEV, pr, N), out_dtype)
        out_spec = pl.BlockSpec((ppb, pr, tn), lambda i, j, k: (i, 0, j))
    dims = (((0 if ta else 1,), (1 if tb else 0,)), ((), ()))

    def body(*refs):
        a_ref, b_ref = refs[0], refs[1]
        pos = 2
        bias_ref = res_ref = None
        if bias is not None:
            bias_ref = refs[pos]
            pos += 1
        if res is not None:
            res_ref = refs[pos]
            pos += 1
        pos += n_dep
        o_ref, acc_ref = refs[pos], refs[pos + 1]
        k = pl.program_id(2)

        @pl.when(k == 0)
        def _():
            acc_ref[...] = jnp.zeros_like(acc_ref)

        acc_ref[...] += lax.dot_general(a_ref[...].astype(_MXU), b_ref[...].astype(_MXU), dims,
                                        preferred_element_type=F32)

        @pl.when(k == gk - 1)
        def _():
            r = acc_ref[...]
            if bias_ref is not None:
                r = r + bias_ref[...]
            if res_ref is not None:
                r = r + res_scale * res_ref[...]
            if pieces is not None:
                r = r.reshape(ppb, pr, tn)
            o_ref[...] = r.astype(out_dtype)

    return pl.pallas_call(
        body, name=name, grid=(gm, gn, gk), in_specs=in_specs, out_specs=out_spec, out_shape=out_shape,
        scratch_shapes=[pltpu.VMEM((tm, tn), F32)],
        compiler_params=_cp(("parallel", "parallel", "arbitrary")),
    )(*operands)


def _mm_res_ln(a, w, wl, bias, res, alpha, g, b, *, name):
    T, K = a.shape
    D = w.shape[-1]
    tm = _tile(T, 256)

    def body(a_ref, w_ref, bias_ref, res_ref, g_ref, b_ref, y_ref, yb_ref, xh_ref, rs_ref):
        z = jnp.dot(a_ref[...].astype(_MXU), w_ref[...].astype(_MXU), preferred_element_type=F32)
        z = z + bias_ref[...] + alpha * res_ref[...]
        y, xh, rstd = _ln_rows(z, g_ref[...], b_ref[...])
        y_ref[...] = y
        yb_ref[...] = y.astype(_MXU)
        xh_ref[...] = xh
        rs_ref[...] = rstd

    row = lambda i: (i, 0)
    vec = pl.BlockSpec((1, D), lambda i: (0, 0))
    return pl.pallas_call(
        body, name=name, grid=(T // tm,),
        in_specs=[pl.BlockSpec((tm, K), row), pl.BlockSpec((None, K, D), lambda i: (wl, 0, 0)), vec,
                  pl.BlockSpec((tm, D), row), vec, vec],
        out_specs=[pl.BlockSpec((tm, D), row), pl.BlockSpec((tm, D), row), pl.BlockSpec((tm, D), row),
                   pl.BlockSpec((tm, 1), row)],
        out_shape=[jax.ShapeDtypeStruct((T, D), F32), jax.ShapeDtypeStruct((T, D), _MXU),
                   jax.ShapeDtypeStruct((T, D), F32), jax.ShapeDtypeStruct((T, 1), F32)],
        compiler_params=_cp(("parallel",)),
    )(a, w, bias, res, g, b)


def _ln_bwd(dy, xh, rstd, g, *, name, target=None):
    T, D = dy.shape
    tm = _tile(T, 256)
    head = target is not None

    def body(*refs):
        if head:
            dy_ref, t_ref, xh_ref, rs_ref, g_ref, dz_ref, dzb_ref, dg_ref, db_ref, cs_ref, ls_ref = refs
        else:
            dy_ref, xh_ref, rs_ref, g_ref, dz_ref, dzb_ref, dg_ref, db_ref, cs_ref = refs
        i = pl.program_id(0)

        @pl.when(i == 0)
        def _():
            dg_ref[...] = jnp.zeros_like(dg_ref)
            db_ref[...] = jnp.zeros_like(db_ref)
            cs_ref[...] = jnp.zeros_like(cs_ref)
            if head:
                ls_ref[...] = jnp.zeros_like(ls_ref)

        d = dy_ref[...]
        if head:
            err = d - t_ref[...]
            ls_ref[...] += _fold8(err * err)
            d = err * (1.0 / D)
        xh = xh_ref[...]
        dz = _ln_bwd_rows(d, xh, rs_ref[...], g_ref[...])
        dz_ref[...] = dz
        dzb_ref[...] = dz.astype(_MXU)
        dg_ref[...] += _fold8(d * xh)
        db_ref[...] += _fold8(d)
        cs_ref[...] += _fold8(dz)

    row = lambda i: (i, 0)
    fixed = lambda i: (0, 0)
    tile = pl.BlockSpec((tm, D), row)
    part = pl.BlockSpec((SUBLANES, D), fixed)
    in_specs = [tile] + ([tile] if head else []) + [tile, pl.BlockSpec((tm, 1), row), pl.BlockSpec((1, D), fixed)]
    n_part = 4 if head else 3
    operands = [dy] + ([target] if head else []) + [xh, rstd, g]
    return pl.pallas_call(
        body, name=name, grid=(T // tm,), in_specs=in_specs,
        out_specs=[tile, tile] + [part] * n_part,
        out_shape=[jax.ShapeDtypeStruct((T, D), F32), jax.ShapeDtypeStruct((T, D), _MXU)]
        + [jax.ShapeDtypeStruct((SUBLANES, D), F32)] * n_part,
        compiler_params=_cp(("arbitrary",)),
    )(*operands)


def _conv_cols(C, tc):
    per = (C // 2) // tc
    return per, (lambda j: (j // per) * (2 * per) + j % per)


def _glu_shifted(a_ref, g_ref, p_ref, S):
    u = a_ref[...] * _sigmoid(g_ref[...])
    rows = lax.broadcasted_iota(jnp.int32, u.shape, 0)
    for r in range(SUBLANES):
        p_ref[r, 0:CONV_TAPS_PAD, :] = jnp.zeros((CONV_TAPS_PAD, u.shape[1]), F32)
        p_ref[r, CONV_TAPS_PAD:CONV_TAPS_PAD + S, :] = u if r == 0 else jnp.where(rows >= r, pltpu.roll(u, r, 0), 0.0)


def _conv_fwd(h1, w_dw, b_dw, *, B, S, name):
    C = w_dw.shape[1]
    taps = CONV_TAPS_PAD - 1
    tc = LANES
    ch = _tile(S, 128)
    per, col_a = _conv_cols(C, tc)

    def body(a_ref, g_ref, w_ref, b_ref, o_ref, p_ref):
        _glu_shifted(a_ref, g_ref, p_ref, S)

        def chunk(ci, carry):
            base = pl.multiple_of(ci * ch, ch)
            acc = jnp.zeros((ch, tc), F32) + b_ref[...]
            for k in range(taps):
                q, r = divmod(taps - 1 - k, SUBLANES)
                start = pl.multiple_of(base + (CONV_TAPS_PAD - SUBLANES * q), SUBLANES)
                acc = acc + w_ref[pl.ds(k, 1), :] * p_ref[r, pl.ds(start, ch), :]
            o_ref[pl.ds(base, ch), :] = acc
            return carry

        lax.fori_loop(0, S // ch, chunk, 0)

    return pl.pallas_call(
        body, name=name, grid=(B, C // tc),
        in_specs=[pl.BlockSpec((S, tc), lambda b, j: (b, col_a(j))),
                  pl.BlockSpec((S, tc), lambda b, j: (b, col_a(j) + per)),
                  pl.BlockSpec((CONV_TAPS_PAD, tc), lambda b, j: (0, j)),
                  pl.BlockSpec((1, tc), lambda b, j: (0, j))],
        out_specs=pl.BlockSpec((S, tc), lambda b, j: (b, j)),
        out_shape=jax.ShapeDtypeStruct((B * S, C), F32),
        scratch_shapes=[pltpu.VMEM((SUBLANES, S + CONV_TAPS_PAD, tc), F32)],
        compiler_params=_cp(("parallel", "parallel")),
    )(h1, h1, w_dw, b_dw)


def _conv_bwd(dd, h1, w_dw, *, B, S, name):
    C = w_dw.shape[1]
    taps = CONV_TAPS_PAD - 1
    tc = LANES
    ch = _tile(S, 128)
    per, col_a = _conv_cols(C, tc)

    def body(d_ref, a_ref, g_ref, w_ref, du_ref, dw_ref, db_ref, p_ref, q_ref):
        b = pl.program_id(1)

        @pl.when(b == 0)
        def _():
            dw_ref[...] = jnp.zeros_like(dw_ref)
            db_ref[...] = jnp.zeros_like(db_ref)

        _glu_shifted(a_ref, g_ref, p_ref, S)
        d = d_ref[...]
        rows = lax.broadcasted_iota(jnp.int32, d.shape, 0)
        for r in range(SUBLANES):
            q_ref[r, S:S + CONV_TAPS_PAD, :] = jnp.zeros((CONV_TAPS_PAD, tc), F32)
            q_ref[r, 0:S, :] = d if r == 0 else jnp.where(rows < S - r, pltpu.roll(d, S - r, 0), 0.0)
        db_ref[...] += _fold8(d)

        def chunk(ci, carry):
            base = pl.multiple_of(ci * ch, ch)
            dch = d_ref[pl.ds(base, ch), :]
            acc = jnp.zeros((ch, tc), F32)
            for k in range(taps):
                q, r = divmod(taps - 1 - k, SUBLANES)
                up = pl.multiple_of(base + SUBLANES * q, SUBLANES)
                acc = acc + w_ref[pl.ds(k, 1), :] * q_ref[r, pl.ds(up, ch), :]
                down = pl.multiple_of(base + (CONV_TAPS_PAD - SUBLANES * q), SUBLANES)
                dw_ref[k] += _fold8(dch * p_ref[r, pl.ds(down, ch), :])
            du_ref[pl.ds(base, ch), :] = acc
            return carry

        lax.fori_loop(0, S // ch, chunk, 0)

    return pl.pallas_call(
        body, name=name, grid=(C // tc, B),
        in_specs=[pl.BlockSpec((S, tc), lambda j, b: (b, j)),
                  pl.BlockSpec((S, tc), lambda j, b: (b, col_a(j))),
                  pl.BlockSpec((S, tc), lambda j, b: (b, col_a(j) + per)),
                  pl.BlockSpec((CONV_TAPS_PAD, tc), lambda j, b: (0, j))],
        out_specs=[pl.BlockSpec((S, tc), lambda j, b: (b, j)),
                   pl.BlockSpec((CONV_TAPS_PAD, SUBLANES, tc), lambda j, b: (0, 0, j)),
                   pl.BlockSpec((SUBLANES, tc), lambda j, b: (0, j))],
        out_shape=[jax.ShapeDtypeStruct((B * S, C), F32),
                   jax.ShapeDtypeStruct((CONV_TAPS_PAD, SUBLANES, C), F32),
                   jax.ShapeDtypeStruct((SUBLANES, C), F32)],
        scratch_shapes=[pltpu.VMEM((SUBLANES, S + CONV_TAPS_PAD, tc), F32),
                        pltpu.VMEM((SUBLANES, S + CONV_TAPS_PAD, tc), F32)],
        compiler_params=_cp(("parallel", "arbitrary")),
    )(dd, h1, h1, w_dw)


def _ln_silu_fwd(v, g, b, *, name):
    T, C = v.shape
    tm = _tile(T, 512)

    def body(v_ref, g_ref, b_ref, s_ref, xh_ref, rs_ref):
        y, xh, rstd = _ln_rows(v_ref[...], g_ref[...], b_ref[...])
        s_ref[...] = (y * _sigmoid(y)).astype(_MXU)
        xh_ref[...] = xh
        rs_ref[...] = rstd

    row = lambda i: (i, 0)
    vec = pl.BlockSpec((1, C), lambda i: (0, 0))
    return pl.pallas_call(
        body, name=name, grid=(T // tm,),
        in_specs=[pl.BlockSpec((tm, C), row), vec, vec],
        out_specs=[pl.BlockSpec((tm, C), row), pl.BlockSpec((tm, C), row), pl.BlockSpec((tm, 1), row)],
        out_shape=[jax.ShapeDtypeStruct((T, C), _MXU), jax.ShapeDtypeStruct((T, C), F32),
                   jax.ShapeDtypeStruct((T, 1), F32)],
        compiler_params=_cp(("parallel",)),
    )(v, g, b)


def _ln_silu_bwd(ds, xh, rstd, g, b, *, name):
    T, C = ds.shape
    tm = _tile(T, 256)

    def body(ds_ref, xh_ref, rs_ref, g_ref, b_ref, dv_ref, dg_ref, db_ref):
        @pl.when(pl.program_id(0) == 0)
        def _():
            dg_ref[...] = jnp.zeros_like(dg_ref)
            db_ref[...] = jnp.zeros_like(db_ref)

        xh = xh_ref[...]
        gam = g_ref[...]
        y = xh * gam + b_ref[...]
        sig = _sigmoid(y)
        dln = ds_ref[...] * (sig * (1.0 + y * (1.0 - sig)))
        dv_ref[...] = _ln_bwd_rows(dln, xh, rs_ref[...], gam)
        dg_ref[...] += _fold8(dln * xh)
        db_ref[...] += _fold8(dln)

    row = lambda i: (i, 0)
    fixed = lambda i: (0, 0)
    vec = pl.BlockSpec((1, C), fixed)
    part = pl.BlockSpec((SUBLANES, C), fixed)
    return pl.pallas_call(
        body, name=name, grid=(T // tm,),
        in_specs=[pl.BlockSpec((tm, C), row), pl.BlockSpec((tm, C), row), pl.BlockSpec((tm, 1), row), vec, vec],
        out_specs=[pl.BlockSpec((tm, C), row), part, part],
        out_shape=[jax.ShapeDtypeStruct((T, C), F32)] + [jax.ShapeDtypeStruct((SUBLANES, C), F32)] * 2,
        compiler_params=_cp(("arbitrary",)),
    )(ds, xh, rstd, g, b)


def _glu_bwd(du, h1, *, name):
    T, C = du.shape
    il = C // 2
    tm = _tile(T, 256)

    def body(du_ref, h_ref, dh_ref, cs_ref):
        @pl.when(pl.program_id(0) == 0)
        def _():
            cs_ref[...] = jnp.zeros_like(cs_ref)

        for hb in range(2):
            a = h_ref[:, 2 * hb * il:(2 * hb + 1) * il]
            gate = h_ref[:, (2 * hb + 1) * il:(2 * hb + 2) * il]
            d = du_ref[:, hb * il:(hb + 1) * il]
            sig = _sigmoid(gate)
            da = d * sig
            dgate = d * a * sig * (1.0 - sig)
            dh_ref[:, 2 * hb * il:(2 * hb + 1) * il] = da.astype(_MXU)
            dh_ref[:, (2 * hb + 1) * il:(2 * hb + 2) * il] = dgate.astype(_MXU)
            cs_ref[:, 2 * hb * il:(2 * hb + 1) * il] += _fold8(da)
            cs_ref[:, (2 * hb + 1) * il:(2 * hb + 2) * il] += _fold8(dgate)

    row = lambda i: (i, 0)
    return pl.pallas_call(
        body, name=name, grid=(T // tm,),
        in_specs=[pl.BlockSpec((tm, C), row), pl.BlockSpec((tm, 2 * C), row)],
        out_specs=[pl.BlockSpec((tm, 2 * C), row), pl.BlockSpec((SUBLANES, 2 * C), lambda i: (0, 0))],
        out_shape=[jax.ShapeDtypeStruct((T, 2 * C), _MXU), jax.ShapeDtypeStruct((SUBLANES, 2 * C), F32)],
        compiler_params=_cp(("arbitrary",)),
    )(du, h1)


def _tril_mask(n):
    return lax.broadcasted_iota(jnp.int32, (n, n), 0) >= lax.broadcasted_iota(jnp.int32, (n, n), 1)


def _split_uv(t, il):
    u = jnp.concatenate([t[:, 0:il], t[:, 2 * il:3 * il]], axis=1)
    v = jnp.concatenate([t[:, il:2 * il], t[:, 3 * il:4 * il]], axis=1)
    return u, v


def _gmlp_gate_fwd(p, g, b, w_s, bsb, *, name):
    T, C2 = p.shape
    C = C2 // 2
    il = C // 2
    G, L, _ = w_s.shape
    assert G * L == C
    tm = _tile(T, 2 * L, L)

    def body(p_ref, g_ref, b_ref, ws_ref, bs_ref, us_ref, xh_ref, rs_ref, vn_ref, u_ref):
        z, _ = _gelu_parts(p_ref[...])
        u, v = _split_uv(z, il)
        vn, xh, rstd = _ln_rows(v, g_ref[...], b_ref[...])
        xh_ref[...] = xh
        rs_ref[...] = rstd
        vn_ref[...] = vn.astype(_MXU)
        u_ref[...] = u
        mask = _tril_mask(L)
        for gi in range(G):
            wc = jnp.where(mask, ws_ref[gi], 0.0).astype(_MXU)
            cols = slice(gi * L, (gi + 1) * L)
            for c in range(tm // L):
                rows = slice(c * L, (c + 1) * L)
                s = jnp.dot(wc, vn_ref[rows, cols], preferred_element_type=F32) + bs_ref[:, cols]
                us_ref[rows, cols] = (u_ref[rows, cols] * s).astype(_MXU)

    row = lambda i: (i, 0)
    fixed = lambda i: (0, 0)
    return pl.pallas_call(
        body, name=name, grid=(T // tm,),
        in_specs=[pl.BlockSpec((tm, C2), row), pl.BlockSpec((1, C), fixed), pl.BlockSpec((1, C), fixed),
                  pl.BlockSpec((G, L, L), lambda i: (0, 0, 0)), pl.BlockSpec((L, C), fixed)],
        out_specs=[pl.BlockSpec((tm, C), row), pl.BlockSpec((tm, C), row), pl.BlockSpec((tm, 1), row)],
        out_shape=[jax.ShapeDtypeStruct((T, C), _MXU), jax.ShapeDtypeStruct((T, C), F32),
                   jax.ShapeDtypeStruct((T, 1), F32)],
        scratch_shapes=[pltpu.VMEM((tm, C), _MXU), pltpu.VMEM((tm, C), F32)],
        compiler_params=_cp(("parallel",)),
    )(p, g, b, w_s, bsb)


def _gmlp_gate_bwd(dus, p, xh, rstd, g, b, w_s, bsb, *, name):
    T, C2 = p.shape
    C = C2 // 2
    il = C // 2
    G, L, _ = w_s.shape
    tm = _tile(T, 2 * L, L)

    def body(dus_ref, p_ref, xh_ref, rs_ref, g_ref, b_ref, ws_ref, bs_ref,
             dp_ref, dg_ref, db_ref, cs_ref, dws_ref, dbs_ref, vn_ref, u_ref, dvn_ref, du_ref):
        @pl.when(pl.program_id(0) == 0)
        def _():
            dg_ref[...] = jnp.zeros_like(dg_ref)
            db_ref[...] = jnp.zeros_like(db_ref)
            cs_ref[...] = jnp.zeros_like(cs_ref)
            dws_ref[...] = jnp.zeros_like(dws_ref)
            dbs_ref[...] = jnp.zeros_like(dbs_ref)

        z, gp = _gelu_parts(p_ref[...])
        u, _ = _split_uv(z, il)
        xh = xh_ref[...]
        gam = g_ref[...]
        vn_ref[...] = (xh * gam + b_ref[...]).astype(_MXU)
        u_ref[...] = u
        mask = _tril_mask(L)
        for gi in range(G):
            wc = jnp.where(mask, ws_ref[gi], 0.0).astype(_MXU)
            cols = slice(gi * L, (gi + 1) * L)
            for c in range(tm // L):
                rows = slice(c * L, (c + 1) * L)
                vnb = vn_ref[rows, cols]
                s = jnp.dot(wc, vnb, preferred_element_type=F32) + bs_ref[:, cols]
                d = dus_ref[rows, cols]
                du_ref[rows, cols] = d * s
                ds = d * u_ref[rows, cols]
                dbs_ref[:, cols] += ds
                dsb = ds.astype(_MXU)
                dw = lax.dot_general(dsb, vnb, (((1,), (1,)), ((), ())), preferred_element_type=F32)
                dws_ref[gi] += jnp.where(mask, dw, 0.0)
                dvn_ref[rows, cols] = lax.dot_general(wc, dsb, (((0,), (0,)), ((), ())), preferred_element_type=F32)
        dvn = dvn_ref[...]
        dg_ref[...] += _fold8(dvn * xh)
        db_ref[...] += _fold8(dvn)
        dv = _ln_bwd_rows(dvn, xh, rs_ref[...], gam)
        du = du_ref[...]
        for hb in range(2):
            for part, src in ((0, du), (1, dv)):
                lo = (2 * hb + part) * il
                dp = src[:, hb * il:(hb + 1) * il] * gp[:, lo:lo + il]
                dp_ref[:, lo:lo + il] = dp.astype(_MXU)
                cs_ref[:, lo:lo + il] += _fold8(dp)

    row = lambda i: (i, 0)
    fixed = lambda i: (0, 0)
    part_c = pl.BlockSpec((SUBLANES, C), fixed)
    return pl.pallas_call(
        body, name=name, grid=(T // tm,),
        in_specs=[pl.BlockSpec((tm, C), row), pl.BlockSpec((tm, C2), row), pl.BlockSpec((tm, C), row),
                  pl.BlockSpec((tm, 1), row), pl.BlockSpec((1, C), fixed), pl.BlockSpec((1, C), fixed),
                  pl.BlockSpec((G, L, L), lambda i: (0, 0, 0)), pl.BlockSpec((L, C), fixed)],
        out_specs=[pl.BlockSpec((tm, C2), row), part_c, part_c, pl.BlockSpec((SUBLANES, C2), fixed),
                   pl.BlockSpec((G, L, L), lambda i: (0, 0, 0)), pl.BlockSpec((L, C), fixed)],
        out_shape=[jax.ShapeDtypeStruct((T, C2), _MXU), jax.ShapeDtypeStruct((SUBLANES, C), F32),
                   jax.ShapeDtypeStruct((SUBLANES, C), F32), jax.ShapeDtypeStruct((SUBLANES, C2), F32),
                   jax.ShapeDtypeStruct((G, L, L), F32), jax.ShapeDtypeStruct((L, C), F32)],
        scratch_shapes=[pltpu.VMEM((tm, C), _MXU), pltpu.VMEM((tm, C), F32), pltpu.VMEM((tm, C), F32),
                        pltpu.VMEM((tm, C), F32)],
        compiler_params=_cp(("arbitrary",)),
    )(dus, p, xh, rstd, g, b, w_s, bsb)


def _ffn_conv(h, prev8, w_ref, b_ref):
    h1 = _shift_down(prev8, h, 1)
    h2 = _shift_down(prev8, h, 2)
    hc = w_ref[pl.ds(2, 1), :] * h + w_ref[pl.ds(1, 1), :] * h1 + w_ref[pl.ds(0, 1), :] * h2 + b_ref[...]
    return hc, h1, h2


def _ffn_up_fwd(xb, w, wl, b_up, w_dw, b_dw, *, S, name):
    T, D = xb.shape
    N = w.shape[-1]
    tn = N // N_CHIPS
    tm = _tile(S, 256)
    spt = S // tm

    def body(x_ref, w_ref, bu_ref, wd_ref, bd_ref, h_ref, f_ref, carry_ref):
        i = pl.program_id(1)

        @pl.when(i % spt == 0)
        def _():
            carry_ref[...] = jnp.zeros_like(carry_ref)

        h = jnp.dot(x_ref[...].astype(_MXU), w_ref[...].astype(_MXU), preferred_element_type=F32) + bu_ref[...]
        hq = h.astype(_HDT)
        h_ref[...] = hq
        h = hq.astype(F32)
        hc, _, _ = _ffn_conv(h, carry_ref[...], wd_ref, bd_ref)
        carry_ref[...] = h[tm - SUBLANES:tm]
        gte = hc[:, :tn]
        f_ref[...] = (gte * _sigmoid(gte) * hc[:, tn:]).astype(_MXU)

    pair = lambda j, i: (0, j)
    return pl.pallas_call(
        body, name=name, grid=(2, T // tm),
        in_specs=[pl.BlockSpec((tm, D), lambda j, i: (i, 0)),
                  pl.BlockSpec((None, D, 2 * tn), lambda j, i: (wl, 0, j)),
                  pl.BlockSpec((1, 2 * tn), pair), pl.BlockSpec((SUBLANES, 2 * tn), pair),
                  pl.BlockSpec((1, 2 * tn), pair)],
        out_specs=[pl.BlockSpec((tm, 2 * tn), lambda j, i: (i, j)), pl.BlockSpec((tm, tn), lambda j, i: (i, j))],
        out_shape=[jax.ShapeDtypeStruct((T, N), _HDT), jax.ShapeDtypeStruct((T, N // 2), _MXU)],
        scratch_shapes=[pltpu.VMEM((SUBLANES, 2 * tn), F32)],
        compiler_params=_cp(("parallel", "arbitrary")),
    )(xb, w, b_up, w_dw, b_dw)


def _ffn_bwd(dzb, w_down, wl, hs, w_dw, b_dw, *, S, name):
    T, D = dzb.shape
    N = hs.shape[1]
    tn = N // N_CHIPS
    tm = _tile(S, 256)
    spt = S // tm
    nt = T // tm
    hal = 16

    def body(dz_ref, wd_ref, h_ref, halo_ref, wc_ref, bc_ref, dh_ref, cs_ref, dw_ref, db_ref, carry_ref):
        i = pl.program_id(1)
        ii = nt - 1 - i

        @pl.when(i == 0)
        def _():
            cs_ref[...] = jnp.zeros_like(cs_ref)
            dw_ref[...] = jnp.zeros_like(dw_ref)
            db_ref[...] = jnp.zeros_like(db_ref)

        df = lax.dot_general(dz_ref[...].astype(_MXU), wd_ref[...].astype(_MXU), (((1,), (1,)), ((), ())),
                             preferred_element_type=F32)
        h = h_ref[...].astype(F32)
        prev8 = halo_ref[...].astype(F32)[hal - SUBLANES:hal]
        prev8 = jnp.where(ii % spt == 0, 0.0, prev8)
        hc, h1, h2 = _ffn_conv(h, prev8, wc_ref, bc_ref)
        gte, val = hc[:, :tn], hc[:, tn:]
        sig = _sigmoid(gte)
        dval = df * (gte * sig)
        dg = df * val * (sig * (1.0 + gte * (1.0 - sig)))
        dhc = jnp.concatenate([dg, dval], axis=1)
        db_ref[...] += _fold8(dhc)
        dw_ref[2] += _fold8(dhc * h)
        dw_ref[1] += _fold8(dhc * h1)
        dw_ref[0] += _fold8(dhc * h2)
        nxt = jnp.where((ii + 1) % spt == 0, 0.0, carry_ref[...])
        dh = (wc_ref[pl.ds(2, 1), :] * dhc + wc_ref[pl.ds(1, 1), :] * _shift_up(dhc, nxt, 1)
              + wc_ref[pl.ds(0, 1), :] * _shift_up(dhc, nxt, 2))
        carry_ref[...] = dhc[0:SUBLANES]
        cs_ref[...] += _fold8(dh)
        dh_ref[...] = dh.astype(_MXU)

    pair = lambda j, i: (0, j)
    rev = lambda j, i: (nt - 1 - i, j)
    return pl.pallas_call(
        body, name=name, grid=(2, nt),
        in_specs=[pl.BlockSpec((tm, D), lambda j, i: (nt - 1 - i, 0)),
                  pl.BlockSpec((None, tn, D), lambda j, i: (wl, j, 0)),
                  pl.BlockSpec((tm, 2 * tn), rev),
                  pl.BlockSpec((hal, 2 * tn), lambda j, i: (jnp.maximum((nt - 1 - i) * (tm // hal) - 1, 0), j)),
                  pl.BlockSpec((SUBLANES, 2 * tn), pair), pl.BlockSpec((1, 2 * tn), pair)],
        out_specs=[pl.BlockSpec((tm, 2 * tn), rev), pl.BlockSpec((SUBLANES, 2 * tn), pair),
                   pl.BlockSpec((3, SUBLANES, 2 * tn), lambda j, i: (0, 0, j)),
                   pl.BlockSpec((SUBLANES, 2 * tn), pair)],
        out_shape=[jax.ShapeDtypeStruct((T, N), _MXU), jax.ShapeDtypeStruct((SUBLANES, N), F32),
                   jax.ShapeDtypeStruct((3, SUBLANES, N), F32), jax.ShapeDtypeStruct((SUBLANES, N), F32)],
        scratch_shapes=[pltpu.VMEM((SUBLANES, 2 * tn), F32)],
        compiler_params=_cp(("parallel", "arbitrary")),
    )(dzb, w_down, hs, hs, w_dw, b_dw)


def _sum_pieces(g, r, me, layer, acc, n_layers, *, name):
    _, pr, pc = g.shape
    tr = _tile(pr, 128)

    def body(me_ref, g_ref, r_ref, *rest):
        o_ref = rest[-1]
        total = g_ref[...].astype(F32)
        for s in range(N_DEV - 1):
            total = total + r_ref[s].astype(F32)
        o_ref[...] = total

    in_specs = [pl.BlockSpec((None, tr, pc), lambda i, me_ref: (me_ref[0], i, 0)),
                pl.BlockSpec((N_DEV - 1, tr, pc), lambda i, me_ref: (0, i, 0))]
    operands = [me, g, r]
    aliases = {}
    if acc is not None:
        in_specs.append(ANY)
        operands.append(acc)
        aliases = {3: 0}
    return pl.pallas_call(
        body, name=name,
        grid_spec=pltpu.PrefetchScalarGridSpec(
            num_scalar_prefetch=1, grid=(pr // tr,), in_specs=in_specs,
            out_specs=pl.BlockSpec((None, tr, pc), lambda i, me_ref: (layer, i, 0))),
        out_shape=jax.ShapeDtypeStruct((n_layers, pr, pc), F32),
        input_output_aliases=aliases,
        compiler_params=_cp(("parallel",)),
    )(*operands)


def _adam_math(w, g, m, v):
    bc1 = 1.0 - ADAM_B1 ** ADAM_STEP
    bc2 = 1.0 - ADAM_B2 ** ADAM_STEP
    m = ADAM_B1 * m + (1.0 - ADAM_B1) * g
    v = ADAM_B2 * v + (1.0 - ADAM_B2) * (g * g)
    return -ADAM_LR * ((m / bc1) / (jnp.sqrt(v / bc2) + ADAM_EPS) + ADAM_WD * w), m, v


def _adam(w, g, m, v, *, name):
    R, C = w.shape
    tr = _tile(R, 256)

    def body(w_ref, g_ref, m_ref, v_ref, d_ref, mo_ref, vo_ref):
        d_ref[...], mo_ref[...], vo_ref[...] = _adam_math(w_ref[...], g_ref[...], m_ref[...], v_ref[...])

    spec = pl.BlockSpec((tr, C), lambda i: (i, 0))
    return pl.pallas_call(
        body, name=name, grid=(R // tr,), in_specs=[spec] * 4, out_specs=[spec] * 3,
        out_shape=[jax.ShapeDtypeStruct((R, C), F32)] * 3,
        compiler_params=_cp(("parallel",)),
    )(w, g, m, v)


def _adam_halves(w, own, got, m, v, core, *, name):
    L, R, C = w.shape
    rh = R // 2
    tr = _tile(rh, 256)
    nt = rh // tr

    def body(c_ref, w_ref, own_ref, got_ref, m_ref, v_ref, g_ref, d_ref, mo_ref, vo_ref):
        g = jnp.where(pl.program_id(1) == c_ref[0], own_ref[...], got_ref[...])
        g_ref[...] = g
        d_ref[...], mo_ref[...], vo_ref[...] = _adam_math(w_ref[...], g, m_ref[...], v_ref[...])

    full = pl.BlockSpec((None, tr, C), lambda l, h, t, c_ref: (l, h * nt + t, 0))
    half = pl.BlockSpec((None, tr, C), lambda l, h, t, c_ref: (l, t, 0))
    return pl.pallas_call(
        body, name=name,
        grid_spec=pltpu.PrefetchScalarGridSpec(
            num_scalar_prefetch=1, grid=(L, 2, nt), in_specs=[full, half, half, full, full], out_specs=[full] * 4),
        out_shape=[jax.ShapeDtypeStruct((L, R, C), F32)] * 4,
        compiler_params=_cp(("parallel", "parallel", "parallel")),
    )(core, w, own, got, m, v)


def _remote(src, dst, send, recv, dev):
    return pltpu.make_async_remote_copy(src_ref=src, dst_ref=dst, send_sem=send, recv_sem=recv,
                                        device_id=dev, device_id_type=MESH)


def _place_w(shard, pos, *, axis, name):
    L, R, C = shard.shape
    tr = _tile(R, 512, 16)
    nt = R // tr
    if axis == 2:
        out_shape = (L, R, N_CHIPS * C)
        out_map = lambda l, t, q: (l, t, q[0])
    else:
        out_shape = (L, N_CHIPS * R, C)
        out_map = lambda l, t, q: (l, q[0] * nt + t, 0)

    def body(q_ref, s_ref, o_ref):
        o_ref[...] = s_ref[...].astype(_WIRE)

    return pl.pallas_call(
        body, name=name,
        grid_spec=pltpu.PrefetchScalarGridSpec(
            num_scalar_prefetch=1, grid=(L, nt),
            in_specs=[pl.BlockSpec((None, tr, C), lambda l, t, q: (l, t, 0))],
            out_specs=pl.BlockSpec((None, tr, C), out_map)),
        out_shape=jax.ShapeDtypeStruct(out_shape, _WIRE),
        compiler_params=_cp(("parallel", "parallel")),
    )(pos, shard)


def _allgather_w(placed, *, axis, perm, name):
    if axis == 2:
        L, R, C = placed.shape[0], placed.shape[1], placed.shape[2] // N_CHIPS
    else:
        L, R, C = placed.shape[0], placed.shape[1] // N_CHIPS, placed.shape[2]
    rh = R // 2

    def body(in_ref, o_ref, send, recv):
        del in_ref
        x, y, c = lax.axis_index("x"), lax.axis_index("y"), lax.axis_index("c")

        def win(px, py, h):
            q = 2 * px + py
            if perm:
                q = _perm_idx(q)
            if axis == 2:
                return o_ref.at[:, pl.ds(pl.multiple_of(h * rh, 16), rh), pl.ds(pl.multiple_of(q * C, LANES), C)]
            return o_ref.at[:, pl.ds(pl.multiple_of(q * R + h * rh, 16), rh), :]

        chips = [(1 - x, y), (x, 1 - y), (1 - x, 1 - y)]
        first = [_remote(win(x, y, c), win(x, y, c), send.at[i], recv.at[i], (px, py, c))
                 for i, (px, py) in enumerate(chips)]
        for cp in first:
            cp.start()
        passed = []
        for i, (px, py) in enumerate(chips):
            _remote(win(x, y, c), win(px, py, c), send.at[i], recv.at[i], (px, py, c)).wait_recv()
            fw = _remote(win(px, py, c), win(px, py, c), send.at[3 + i], recv.at[3 + i], (x, y, 1 - c))
            fw.start()
            passed.append(fw)
        for i, (px, py) in enumerate(chips):
            _remote(win(x, y, c), win(px, py, 1 - c), send.at[3 + i], recv.at[3 + i], (x, y, 1 - c)).wait_recv()
        for cp in first + passed:
            cp.wait_send()

    return pl.pallas_call(
        body, name=name, in_specs=[ANY], out_specs=ANY,
        out_shape=jax.ShapeDtypeStruct(placed.shape, placed.dtype), input_output_aliases={0: 0},
        scratch_shapes=[pltpu.SemaphoreType.DMA((6,)), pltpu.SemaphoreType.DMA((6,))],
    )(placed)


def _flip(x, y, c, f):
    return ((1 - x) if f & 4 else x, (1 - y) if f & 2 else y, (1 - c) if f & 1 else c)


def _rs_copies(g_ref, land_ref, send, recv):
    x, y, c = lax.axis_index("x"), lax.axis_index("y"), lax.axis_index("c")
    cps = []
    for f in range(1, N_DEV):
        tx, ty, tcx = _flip(x, y, c, f)
        cps.append(_remote(g_ref.at[4 * tx + 2 * ty + tcx], land_ref.at[f - 1], send.at[f - 1], recv.at[f - 1],
                           (tx, ty, tcx)))
    return cps


def _rs_start(g, *, name):
    _, pr, pc = g.shape
    land_shape = (N_DEV - 1, pr, pc)

    def body(g_ref, land_ref, send, recv, g_thru, land_thru, token):
        for cp in _rs_copies(g_ref, land_ref, send, recv):
            cp.start()
        token[...] = jnp.zeros_like(token)

    sems = pltpu.SemaphoreType.DMA((N_DEV - 1,))
    return pl.pallas_call(
        body, name=name,
        out_shape=(sems, sems, pltpu.HBM(g.shape, g.dtype), pltpu.HBM(land_shape, g.dtype),
                   jax.ShapeDtypeStruct((SUBLANES, LANES), F32)),
        in_specs=(HBM, HBM), out_specs=(SEMS, SEMS, HBM, HBM, pl.BlockSpec(memory_space=pltpu.VMEM)),
        input_output_aliases={0: 2, 1: 3},
        compiler_params=pltpu.CompilerParams(has_side_effects=EFFECT),
    )(pltpu.with_memory_space_constraint(g, pltpu.HBM),
      pltpu.with_memory_space_constraint(lax.empty(land_shape, g.dtype), pltpu.HBM))


def _rs_wait(send, recv, g_thru, land_thru, after, *, name):
    def body(g_ref, land_ref, send, recv, after_ref, g_out, land_out):
        cps = _rs_copies(g_ref, land_ref, send, recv)
        for cp in cps:
            cp.wait_send()
        for cp in cps:
            cp.wait_recv()

    return pl.pallas_call(
        body, name=name,
        out_shape=(pltpu.HBM(g_thru.shape, g_thru.dtype), pltpu.HBM(land_thru.shape, land_thru.dtype)),
        in_specs=(HBM, HBM, SEMS, SEMS, ANY), out_specs=(HBM, HBM), input_output_aliases={0: 0, 1: 1},
        compiler_params=pltpu.CompilerParams(has_side_effects=EFFECT),
    )(g_thru, land_thru, send, recv, after)


def _pair_exchange(own, *, name):
    def body(own_ref, got_ref, send, recv):
        x, y, c = lax.axis_index("x"), lax.axis_index("y"), lax.axis_index("c")
        cp = _remote(own_ref, got_ref, send, recv, (x, y, 1 - c))
        cp.start()
        cp.wait_recv()
        cp.wait_send()

    return pl.pallas_call(
        body, name=name, in_specs=[ANY], out_specs=ANY, out_shape=jax.ShapeDtypeStruct(own.shape, own.dtype),
        scratch_shapes=[pltpu.SemaphoreType.DMA, pltpu.SemaphoreType.DMA],
    )(own)


def _allreduce_flat(vec, *, name):
    n = vec.shape[0]
    unit = N_DEV * SUBLANES * LANES
    npad = -(-n // unit) * unit
    rows = npad // (N_DEV * LANES)
    xin = jnp.pad(vec, (0, npad - n)).reshape(N_DEV, rows, LANES)

    def body(x_ref, y_ref, a_ref, send_a, recv_a, send_b, recv_b):
        x, y, c = lax.axis_index("x"), lax.axis_index("y"), lax.axis_index("c")
        me = 4 * x + 2 * y + c
        a_ref[me] = x_ref[me]
        sends, recvs = [], []
        for f in range(1, N_DEV):
            dev = _flip(x, y, c, f)
            t = 4 * dev[0] + 2 * dev[1] + dev[2]
            cp = _remote(x_ref.at[t], a_ref.at[me], send_a.at[f - 1], recv_a.at[f - 1], dev)
            cp.start()
            sends.append(cp)
            recvs.append(_remote(x_ref.at[me], a_ref.at[t], send_a.at[f - 1], recv_a.at[f - 1], dev))
        for cp in recvs:
            cp.wait_recv()
        for cp in sends:
            cp.wait_send()
        acc = a_ref[0]
        for s in range(1, N_DEV):
            acc = acc + a_ref[s]
        y_ref[me] = acc
        sends, recvs = [], []
        for f in range(1, N_DEV):
            dev = _flip(x, y, c, f)
            t = 4 * dev[0] + 2 * dev[1] + dev[2]
            cp = _remote(y_ref.at[me], y_ref.at[me], send_b.at[f - 1], recv_b.at[f - 1], dev)
            cp.start()
            sends.append(cp)
            recvs.append(_remote(y_ref.at[me], y_ref.at[t], send_b.at[f - 1], recv_b.at[f - 1], dev))
        for cp in recvs:
            cp.wait_recv()
        for cp in sends:
            cp.wait_send()

    vm = pl.BlockSpec(memory_space=pltpu.VMEM)
    out = pl.pallas_call(
        body, name=name, in_specs=[vm], out_specs=vm,
        out_shape=jax.ShapeDtypeStruct((N_DEV, rows, LANES), F32),
        scratch_shapes=[pltpu.VMEM((N_DEV, rows, LANES), F32)] + [pltpu.SemaphoreType.DMA((N_DEV - 1,))] * 4,
        compiler_params=_cp(),
    )(xin)
    return out.reshape(npad)[:n]


def _perm_cols(v, blocks=N_CHIPS):
    lead, n = v.shape[:-1], v.shape[-1]
    return v.reshape(lead + (blocks, n // blocks))[..., PERM, :].reshape(lead + (n,))


def _pack(arrs):
    return jnp.concatenate([a.reshape(-1).astype(F32) for a in arrs])


def _unpack(flat, shapes):
    out, pos = [], 0
    for s in shapes:
        n = 1
        for d in s:
            n *= d
        out.append(flat[pos:pos + n].reshape(s))
        pos += n
    return out


def kernel(x, conv_w_in, conv_b_in, conv_w_dw, conv_b_dw, conv_ln_g, conv_ln_b, conv_w_out, conv_b_out, gmlp_w_in, gmlp_b_in, gmlp_ln_g, gmlp_ln_b, gmlp_w_s, gmlp_b_s, gmlp_w_out, gmlp_b_out, ffn_w_up, ffn_b_up, ffn_w_dw, ffn_b_dw, ffn_w_down, ffn_b_down, norm1_g, norm1_b, norm2_g, norm2_b, loss_target, m_conv_w_in, m_conv_b_in, m_conv_w_dw, m_conv_b_dw, m_conv_ln_g, m_conv_ln_b, m_conv_w_out, m_conv_b_out, m_gmlp_w_in, m_gmlp_b_in, m_gmlp_ln_g, m_gmlp_ln_b, m_gmlp_w_s, m_gmlp_b_s, m_gmlp_w_out, m_gmlp_b_out, m_ffn_w_up, m_ffn_b_up, m_ffn_w_dw, m_ffn_b_dw, m_ffn_w_down, m_ffn_b_down, m_norm1_g, m_norm1_b, m_norm2_g, m_norm2_b, v_conv_w_in, v_conv_b_in, v_conv_w_dw, v_conv_b_dw, v_conv_ln_g, v_conv_ln_b, v_conv_w_out, v_conv_b_out, v_gmlp_w_in, v_gmlp_b_in, v_gmlp_ln_g, v_gmlp_ln_b, v_gmlp_w_s, v_gmlp_b_s, v_gmlp_w_out, v_gmlp_b_out, v_ffn_w_up, v_ffn_b_up, v_ffn_w_dw, v_ffn_b_dw, v_ffn_w_down, v_ffn_b_down, v_norm1_g, v_norm1_b, v_norm2_g, v_norm2_b):
    P = dict(locals())
    WEIGHTS = ['conv_w_in', 'conv_b_in', 'conv_w_dw', 'conv_b_dw', 'conv_ln_g', 'conv_ln_b', 'conv_w_out',
               'conv_b_out', 'gmlp_w_in', 'gmlp_b_in', 'gmlp_ln_g', 'gmlp_ln_b', 'gmlp_w_s', 'gmlp_b_s',
               'gmlp_w_out', 'gmlp_b_out', 'ffn_w_up', 'ffn_b_up', 'ffn_w_dw', 'ffn_b_dw', 'ffn_w_down',
               'ffn_b_down', 'norm1_g', 'norm1_b', 'norm2_g', 'norm2_b']
    BIG = ['conv_w_in', 'conv_w_out', 'gmlp_w_in', 'gmlp_w_out', 'ffn_w_up', 'ffn_w_down']
    SMALL_SHARDED = {'conv_w_dw': 2, 'gmlp_b_in': 1, 'gmlp_ln_g': 1, 'gmlp_ln_b': 1, 'gmlp_b_out': 1, 'ffn_w_dw': 2}

    B, S, D = x.shape
    T = B * S
    depth = norm1_g.shape[0]
    alpha = (2.0 * depth) ** 0.25
    C = conv_w_out.shape[-1]
    F2 = ffn_b_up.shape[-1]
    G, L = gmlp_w_s.shape[1], gmlp_w_s.shape[2]
    xi, yi, ci = lax.axis_index("x"), lax.axis_index("y"), lax.axis_index("c")
    shard = 2 * xi + yi

    i32 = lambda v: jnp.reshape(v, (1,)).astype(jnp.int32)
    pos_plain, pos_perm = i32(shard), i32(_perm_idx(shard))
    me_id, core_id = i32(4 * xi + 2 * yi + ci), i32(ci)

    def gather(w, axis, perm, name):
        placed = _place_w(w, pos_perm if perm else pos_plain, axis=axis, name="place_" + name)
        return _allgather_w(placed, axis=axis, perm=perm, name="ag_" + name)

    cw_in = gather(conv_w_in, 2, True, "conv_w_in")
    cw_out = gather(conv_w_out, 1, False, "conv_w_out")
    gw_in = gather(gmlp_w_in, 2, True, "gmlp_w_in")
    gw_out = gather(gmlp_w_out, 1, False, "gmlp_w_out")
    fw_up = gather(ffn_w_up, 2, True, "ffn_w_up")
    fw_down = gather(ffn_w_down, 1, False, "ffn_w_down")

    sm_names = list(SMALL_SHARDED)
    sm_shapes = [P[n].shape for n in sm_names]
    mine = _pack([P[n] for n in sm_names]) * (ci == 0).astype(F32)
    buf = jnp.zeros((N_CHIPS, mine.shape[0]), F32)
    buf = lax.dynamic_update_slice(buf, mine[None], (shard, 0))
    gathered = _allreduce_flat(buf.reshape(-1), name="ag_small").reshape(N_CHIPS, -1)
    full = {}
    for n, parts in zip(sm_names, zip(*[_unpack(gathered[k], sm_shapes) for k in range(N_CHIPS)])):
        full[n] = jnp.concatenate(parts, axis=SMALL_SHARDED[n])
    for n in WEIGHTS:
        if n not in BIG and n not in full:
            full[n] = P[n]

    assert G * L == C, "a gMLP group must be as wide as a chunk is long"

    def row(v):
        return v.reshape(1, -1)

    def pad_rows(v, r):
        return jnp.pad(v, ((0, r - v.shape[0]), (0, 0)))

    xf = x.reshape(T, D)
    saved = []
    cur, cur_b = xf, xf
    for i in range(depth):
        j = i // 2
        sv = {'x': cur, 'xb': cur_b}
        if i % 2 == 0:
            b_in = row(_perm_cols(full['conv_b_in'][j]))
            h1 = _mm(cur_b, cw_in, bl=j, bias=b_in, tm=_tile(T, 512), tn=_tile(2 * C, 1024, LANES), tk=D,
                     name=f"conv_in_{j}")
            wdw = pad_rows(full['conv_w_dw'][j], CONV_TAPS_PAD)
            dwo = _conv_fwd(h1, wdw, row(full['conv_b_dw'][j]), B=B, S=S, name=f"conv_dw_{j}")
            s_act, xhc, rsc = _ln_silu_fwd(dwo, row(full['conv_ln_g'][j]), row(full['conv_ln_b'][j]),
                                           name=f"conv_ln_{j}")
            sv.update(h1=h1, wdw=wdw, act=s_act, xhc=xhc, rsc=rsc)
            y1 = _mm_res_ln(s_act, cw_out, j, row(full['conv_b_out'][j]), cur, alpha, row(norm1_g[i]),
                            row(norm1_b[i]), name=f"conv_out_ln_{j}")
        else:
            b_in = row(_perm_cols(full['gmlp_b_in'][j]))
            pre = _mm(cur_b, gw_in, bl=j, bias=b_in, tm=_tile(T, 512), tn=_tile(2 * C, 1024, LANES), tk=D,
                      name=f"gmlp_in_{j}")
            bsb = jnp.repeat(gmlp_b_s[j].T, L, axis=1)
            us, xhv, rsv = _gmlp_gate_fwd(pre, row(full['gmlp_ln_g'][j]), row(full['gmlp_ln_b'][j]), gmlp_w_s[j],
                                          bsb, name=f"gmlp_gate_{j}")
            sv.update(pre=pre, bsb=bsb, act=us, xhv=xhv, rsv=rsv)
            y1 = _mm_res_ln(us, gw_out, j, row(full['gmlp_b_out'][j]), cur, alpha, row(norm1_g[i]),
                            row(norm1_b[i]), name=f"gmlp_out_ln_{j}")
        x1, x1b, xh1, rs1 = y1
        wdw3 = pad_rows(_perm_cols(full['ffn_w_dw'][i]), SUBLANES)
        bdw3 = row(_perm_cols(ffn_b_dw[i]))
        hs, f_act = _ffn_up_fwd(x1b, fw_up, i, row(_perm_cols(ffn_b_up[i])), wdw3, bdw3, S=S, name=f"ffn_up_{i}")
        x2, x2b, xh2, rs2 = _mm_res_ln(f_act, fw_down, i, row(ffn_b_down[i]), x1, alpha, row(norm2_g[i]),
                                       row(norm2_b[i]), name=f"ffn_down_ln_{i}")
        sv.update(x1=x1, x1b=x1b, xh1=xh1, rs1=rs1, hs=hs, f=f_act, wdw3=wdw3, bdw3=bdw3, xh2=xh2, rs2=rs2)
        saved.append(sv)
        cur, cur_b = x2, x2b

    sg = {n: [None] * full[n].shape[0] for n in WEIGHTS if n not in BIG}
    inflight = {n: [None] * P[n].shape[0] for n in BIG}
    deps = []
    tgt = loss_target.reshape(T, D)
    dcur = None
    loss_part = None
    tk_t = _tile(T, 512)

    def wgrad(n, l, a_, b_, **kw):
        g = _mm(a_, b_, ta=True, out_dtype=_WIRE, tk=tk_t, name=f"{n}_dw_{l}", deps=deps, **kw)
        send, recv, g_thru, land, token = _rs_start(g, name=f"rs_start_{n}_{l}")
        inflight[n][l] = (send, recv, g_thru, land)
        deps.append(token)

    for i in reversed(range(depth)):
        j = i // 2
        sv = saved[i]
        if dcur is None:
            dz2, dz2b, dg, db, cs, ls = _ln_bwd(cur, sv['xh2'], sv['rs2'], row(norm2_g[i]), target=tgt,
                                                name=f"ln2_bwd_head_{i}")
            loss_part = ls
        else:
            dz2, dz2b, dg, db, cs = _ln_bwd(dcur, sv['xh2'], sv['rs2'], row(norm2_g[i]), name=f"ln2_bwd_{i}")
        sg['norm2_g'][i], sg['norm2_b'][i], sg['ffn_b_down'][i] = dg.sum(0), db.sum(0), cs.sum(0)
        Fh = F2 // 2
        wgrad('ffn_w_down', i, sv['f'], dz2b, tm=Fh // 2, tn=_tile(D, 1024, LANES), pieces=('row',))
        dh, csu, dwd, dbd = _ffn_bwd(dz2b, fw_down, i, sv['hs'], sv['wdw3'], sv['bdw3'], S=S, name=f"ffn_bwd_{i}")
        sg['ffn_b_up'][i] = _perm_cols(csu.sum(0))
        sg['ffn_w_dw'][i] = _perm_cols(dwd.sum(1))
        sg['ffn_b_dw'][i] = _perm_cols(dbd.sum(0))
        wgrad('ffn_w_up', i, sv['x1b'], dh, tm=D // 2, tn=F2 // N_CHIPS, pieces=('col', True))
        dx1 = _mm(dh, fw_up, bl=i, tb=True, res=dz2, res_scale=alpha, tm=_tile(T, 512), tn=_tile(D, 1024, LANES),
                  tk=F2 // N_CHIPS, name=f"ffn_dx_{i}", deps=deps)
        dz1, dz1b, dg, db, cs = _ln_bwd(dx1, sv['xh1'], sv['rs1'], row(norm1_g[i]), name=f"ln1_bwd_{i}")
        sg['norm1_g'][i], sg['norm1_b'][i] = dg.sum(0), db.sum(0)
        if i % 2 == 0:
            sg['conv_b_out'][j] = cs.sum(0)
            wgrad('conv_w_out', j, sv['act'], dz1b, tm=_tile(C, 512), tn=_tile(D, 1024, LANES), pieces=('row',))
            ds = _mm(dz1b, cw_out, bl=j, tb=True, tm=_tile(T, 512), tn=_tile(C, 1024, LANES), tk=_tile(D, 1024, LANES),
                     name=f"conv_ds_{j}", deps=deps)
            ddw, dg, db = _ln_silu_bwd(ds, sv['xhc'], sv['rsc'], row(full['conv_ln_g'][j]),
                                       row(full['conv_ln_b'][j]), name=f"conv_ln_bwd_{j}")
            sg['conv_ln_g'][j], sg['conv_ln_b'][j] = dg.sum(0), db.sum(0)
            dglu, dwk, dbk = _conv_bwd(ddw, sv['h1'], sv['wdw'], B=B, S=S, name=f"conv_dw_bwd_{j}")
            sg['conv_w_dw'][j] = dwk.sum(1)[:conv_w_dw.shape[1]]
            sg['conv_b_dw'][j] = dbk.sum(0)
            dh1, csi = _glu_bwd(dglu, sv['h1'], name=f"conv_glu_bwd_{j}")
            sg['conv_b_in'][j] = _perm_cols(csi.sum(0))
            w_in_full, fam = cw_in, 'conv_w_in'
        else:
            sg['gmlp_b_out'][j] = cs.sum(0)
            wgrad('gmlp_w_out', j, sv['act'], dz1b, tm=_tile(C, 512), tn=_tile(D, 1024, LANES), pieces=('row',))
            dus = _mm(dz1b, gw_out, bl=j, tb=True, tm=_tile(T, 512), tn=_tile(C, 1024, LANES),
                      tk=_tile(D, 1024, LANES), name=f"gmlp_dus_{j}", deps=deps)
            dh1, dg, db, csi, dws, dbs = _gmlp_gate_bwd(dus, sv['pre'], sv['xhv'], sv['rsv'], row(full['gmlp_ln_g'][j]),
                                                        row(full['gmlp_ln_b'][j]), gmlp_w_s[j], sv['bsb'],
                                                        name=f"gmlp_gate_bwd_{j}")
            sg['gmlp_ln_g'][j], sg['gmlp_ln_b'][j] = dg.sum(0), db.sum(0)
            sg['gmlp_b_in'][j] = _perm_cols(csi.sum(0))
            sg['gmlp_w_s'][j] = dws
            sg['gmlp_b_s'][j] = dbs.reshape(L, G, L).sum(-1).T
            w_in_full, fam = gw_in, 'gmlp_w_in'
        wgrad(fam, j, sv['xb'], dh1, tm=D // 2, tn=(2 * C) // N_CHIPS, pieces=('col', True))
        dcur = _mm(dh1, w_in_full, bl=j, tb=True, res=dz1, res_scale=alpha, tm=_tile(T, 512),
                   tn=_tile(D, 1024, LANES), tk=_tile(2 * C, 1024, LANES), name=f"{fam}_dx_{j}", deps=deps)
    grad_x = dcur.reshape(B, S, D)

    small_names = [n for n in WEIGHTS if n not in BIG]
    small_full = [jnp.stack(sg[n]) for n in small_names]
    flat = _pack(small_full + [loss_part])
    red = _allreduce_flat(flat, name="ar_small")
    red_parts = _unpack(red, [a.shape for a in small_full] + [loss_part.shape])
    loss = (0.5 / D) * jnp.sum(red_parts[-1])
    grads = {}
    for n, g in zip(small_names, red_parts[:-1]):
        if n in SMALL_SHARDED:
            ax = SMALL_SHARDED[n]
            width = P[n].shape[ax]
            g = lax.dynamic_slice_in_dim(g, shard * width, width, axis=ax)
        grads[n] = g

    big_out = {}
    for n in ['ffn_w_down', 'ffn_w_up', 'gmlp_w_out', 'gmlp_w_in', 'conv_w_out', 'conv_w_in']:
        own = None
        n_layers = len(inflight[n])
        for l in reversed(range(n_layers)):
            send, recv, g_thru, land = inflight[n][l]
            pc_, r = _rs_wait(send, recv, g_thru, land, dcur, name=f"rs_wait_{n}_{l}")
            own = _sum_pieces(pc_, r, me_id, l, own, n_layers, name=f"sum_{n}_{l}")
        got = _pair_exchange(own, name=f"px_{n}")
        big_out[n] = _adam_halves(P[n], own, got, P['m_' + n], P['v_' + n], core_id, name=f"adam_{n}")

    shapes = [P[n].shape for n in small_names]
    n_small = sum(functools.reduce(lambda p_, d_: p_ * d_, s_, 1) for s_ in shapes)
    unit = SUBLANES * LANES
    npad = -(-n_small // unit) * unit

    def flat2d(arrs, fill=0.0):
        v = _pack(arrs)
        return jnp.pad(v, (0, npad - n_small), constant_values=fill).reshape(-1, LANES)

    dl, mo, vo = _adam(flat2d([P[n] for n in small_names]), flat2d([grads[n] for n in small_names]),
                       flat2d([P['m_' + n] for n in small_names]),
                       flat2d([P['v_' + n] for n in small_names], fill=1.0), name="adam_small")
    small_out = {n: [grads[n], None, None, None] for n in small_names}
    for k, t in enumerate((dl, mo, vo)):
        for n, a in zip(small_names, _unpack(t.reshape(-1), shapes)):
            small_out[n][k + 1] = a

    outs = [loss, grad_x]
    for k in range(4):
        for n in WEIGHTS:
            outs.append(big_out[n][k] if n in BIG else small_out[n][k])
    return tuple(outs)
```

```python
import functools

import jax
import jax.numpy as jnp
from jax import lax
from jax.experimental import pallas as pl
from jax.experimental.pallas import tpu as pltpu

F32 = jnp.float32
_MXU = jnp.bfloat16
_WIRE = jnp.bfloat16
_HDT = jnp.bfloat16
LN_EPS = 1e-5
ADAM_LR, ADAM_B1, ADAM_B2, ADAM_EPS, ADAM_WD, ADAM_STEP = 0.001, 0.9, 0.999, 1e-08, 0.01, 10
N_CHIPS = 4
N_DEV = 8
LANES = 128
SUBLANES = 8
CONV_TAPS_PAD = 32
VMEM_LIMIT = 56 << 20
MESH = pl.DeviceIdType.MESH
ANY = pl.BlockSpec(memory_space=pl.ANY)
HBM = pl.BlockSpec(memory_space=pltpu.HBM)
SEMS = pl.BlockSpec(memory_space=pltpu.SEMAPHORE)
EFFECT = pltpu.SideEffectType.DATAFLOW_SIDE_EFFECTING
PERM = (0, 2, 1, 3)


def _cp(sem=None):
    return pltpu.CompilerParams(dimension_semantics=sem, vmem_limit_bytes=VMEM_LIMIT)


def _tile(dim, pref, mult=SUBLANES):
    if dim <= pref:
        return dim
    t = (pref // mult) * mult
    while t > mult and dim % t:
        t -= mult
    assert dim % t == 0, (dim, pref, mult)
    return t


def _perm_idx(q):
    return (q % 2) * 2 + q // 2


def _fold8(t):
    r, n = t.shape
    return t.reshape(r // SUBLANES, SUBLANES, n).sum(axis=0)


def _ln_rows(z, g, b):
    mu = jnp.mean(z, axis=-1, keepdims=True)
    xc = z - mu
    var = jnp.mean(xc * xc, axis=-1, keepdims=True)
    rstd = lax.rsqrt(var + LN_EPS)
    xh = xc * rstd
    return xh * g + b, xh, rstd


def _ln_bwd_rows(dy, xh, rstd, g):
    dxh = dy * g
    m1 = jnp.mean(dxh, axis=-1, keepdims=True)
    m2 = jnp.mean(dxh * xh, axis=-1, keepdims=True)
    return rstd * (dxh - m1 - xh * m2)


def _sigmoid(v):
    return 1.0 / (1.0 + jnp.exp(-v))


def _gelu_parts(p):
    cdf = 0.5 * (1.0 + lax.erf(p * 0.7071067811865476))
    pdf = jnp.exp(-0.5 * p * p) * 0.3989422804014327
    return p * cdf, cdf + p * pdf


def _shift_down(prev8, t, s):
    ext = jnp.concatenate([prev8, t], axis=0)
    return pltpu.roll(ext, s, 0)[SUBLANES:]


def _shift_up(t, next8, s):
    n = t.shape[0]
    ext = jnp.concatenate([t, next8], axis=0)
    return pltpu.roll(ext, n + SUBLANES - s, 0)[:n]


def _mm(a, b, *, ta=False, tb=False, bl=None, bias=None, res=None, res_scale=1.0, out_dtype=F32,
        tm, tn, tk, name, pieces=None, deps=None):
    M, K = (a.shape[1], a.shape[0]) if ta else a.shape
    bs = b.shape[1:] if bl is not None else b.shape
    N, Kb = (bs[0], bs[1]) if tb else (bs[1], bs[0])
    assert K == Kb and M % tm == 0 and N % tn == 0 and K % tk == 0, (a.shape, b.shape, tm, tn, tk)
    gm, gn, gk = M // tm, N // tn, K // tk
    a_spec = pl.BlockSpec((tk, tm), lambda i, j, k: (k, i)) if ta else pl.BlockSpec((tm, tk), lambda i, j, k: (i, k))
    bblk = (tn, tk) if tb else (tk, tn)
    bmap = (lambda i, j, k: (j, k)) if tb else (lambda i, j, k: (k, j))
    if bl is not None:
        b_spec = pl.BlockSpec((None,) + bblk, lambda i, j, k: (bl,) + bmap(i, j, k))
    else:
        b_spec = pl.BlockSpec(bblk, bmap)
    in_specs, operands = [a_spec, b_spec], [a, b]
    if bias is not None:
        in_specs.append(pl.BlockSpec((1, tn), lambda i, j, k: (0, j)))
        operands.append(bias)
    if res is not None:
        in_specs.append(pl.BlockSpec((tm, tn), lambda i, j, k: (i, j)))
        operands.append(res)
    n_dep = len(deps) if deps else 0
    if n_dep:
        in_specs += [ANY] * n_dep
        operands += deps
        del deps[:]
    if pieces is None:
        out_shape = jax.ShapeDtypeStruct((M, N), out_dtype)
        out_spec = pl.BlockSpec((tm, tn), lambda i, j, k: (i, j))
        ppb = pr = None
    elif pieces[0] == 'col':
        pr, pc = M // 2, N // N_CHIPS
        assert tm == pr and pc % tn == 0
        ppb, per = 1, pc // tn
        perm = pieces[1]
        out_shape = jax.ShapeDtypeStruct((N_DEV, pr, pc), out_dtype)
        out_spec = pl.BlockSpec(
            (1, pr, tn), lambda i, j, k: (2 * (_perm_idx(j // per) if perm else j // per) + i, 0, j % per))
    else:
        pr = M // N_DEV
        assert tm % pr == 0
        ppb = tm // pr
        out_shape = jax.ShapeDtypeStruct((N_DEV, pr, N), out_dtype)
        out_spec = pl.BlockSpec((ppb, pr, tn), lambda i, j, k: (i, 0, j))
    dims = (((0 if ta else 1,), (1 if tb else 0,)), ((), ()))

    def body(*refs):
        a_ref, b_ref = refs[0], refs[1]
        pos = 2
        bias_ref = res_ref = None
        if bias is not None:
            bias_ref = refs[pos]
            pos += 1
        if res is not None:
            res_ref = refs[pos]
            pos += 1
        pos += n_dep
        o_ref, acc_ref = refs[pos], refs[pos + 1]
        k = pl.program_id(2)

        @pl.when(k == 0)
        def _():
            acc_ref[...] = jnp.zeros_like(acc_ref)

        acc_ref[...] += lax.dot_general(a_ref[...].astype(_MXU), b_ref[...].astype(_MXU), dims,
                                        preferred_element_type=F32)

        @pl.when(k == gk - 1)
        def _():
            r = acc_ref[...]
            if bias_ref is not None:
                r = r + bias_ref[...]
            if res_ref is not None:
                r = r + res_scale * res_ref[...]
            if pieces is not None:
                r = r.reshape(ppb, pr, tn)
            o_ref[...] = r.astype(out_dtype)

    return pl.pallas_call(
        body, name=name, grid=(gm, gn, gk), in_specs=in_specs, out_specs=out_spec, out_shape=out_shape,
        scratch_shapes=[pltpu.VMEM((tm, tn), F32)],
        compiler_params=_cp(("parallel", "parallel", "arbitrary")),
    )(*operands)


def _mm_res_ln(a, w, wl, bias, res, alpha, g, b, *, name):
    T, K = a.shape
    D = w.shape[-1]
    tm = _tile(T, 256)

    def body(a_ref, w_ref, bias_ref, res_ref, g_ref, b_ref, y_ref, yb_ref, xh_ref, rs_ref):
        z = jnp.dot(a_ref[...].astype(_MXU), w_ref[...].astype(_MXU), preferred_element_type=F32)
        z = z + bias_ref[...] + alpha * res_ref[...]
        y, xh, rstd = _ln_rows(z, g_ref[...], b_ref[...])
        y_ref[...] = y
        yb_ref[...] = y.astype(_MXU)
        xh_ref[...] = xh
        rs_ref[...] = rstd

    row = lambda i: (i, 0)
    vec = pl.BlockSpec((1, D), lambda i: (0, 0))
    return pl.pallas_call(
        body, name=name, grid=(T // tm,),
        in_specs=[pl.BlockSpec((tm, K), row), pl.BlockSpec((None, K, D), lambda i: (wl, 0, 0)), vec,
                  pl.BlockSpec((tm, D), row), vec, vec],
        out_specs=[pl.BlockSpec((tm, D), row), pl.BlockSpec((tm, D), row), pl.BlockSpec((tm, D), row),
                   pl.BlockSpec((tm, 1), row)],
        out_shape=[jax.ShapeDtypeStruct((T, D), F32), jax.ShapeDtypeStruct((T, D), _MXU),
                   jax.ShapeDtypeStruct((T, D), F32), jax.ShapeDtypeStruct((T, 1), F32)],
        compiler_params=_cp(("parallel",)),
    )(a, w, bias, res, g, b)


def _ln_bwd(dy, xh, rstd, g, *, name, target=None):
    T, D = dy.shape
    tm = _tile(T, 256)
    head = target is not None

    def body(*refs):
        if head:
            dy_ref, t_ref, xh_ref, rs_ref, g_ref, dz_ref, dzb_ref, dg_ref, db_ref, cs_ref, ls_ref = refs
        else:
            dy_ref, xh_ref, rs_ref, g_ref, dz_ref, dzb_ref, dg_ref, db_ref, cs_ref = refs
        i = pl.program_id(0)

        @pl.when(i == 0)
        def _():
            dg_ref[...] = jnp.zeros_like(dg_ref)
            db_ref[...] = jnp.zeros_like(db_ref)
            cs_ref[...] = jnp.zeros_like(cs_ref)
            if head:
                ls_ref[...] = jnp.zeros_like(ls_ref)

        d = dy_ref[...]
        if head:
            err = d - t_ref[...]
            ls_ref[...] += _fold8(err * err)
            d = err * (1.0 / D)
        xh = xh_ref[...]
        dz = _ln_bwd_rows(d, xh, rs_ref[...], g_ref[...])
        dz_ref[...] = dz
        dzb_ref[...] = dz.astype(_MXU)
        dg_ref[...] += _fold8(d * xh)
        db_ref[...] += _fold8(d)
        cs_ref[...] += _fold8(dz)

    row = lambda i: (i, 0)
    fixed = lambda i: (0, 0)
    tile = pl.BlockSpec((tm, D), row)
    part = pl.BlockSpec((SUBLANES, D), fixed)
    in_specs = [tile] + ([tile] if head else []) + [tile, pl.BlockSpec((tm, 1), row), pl.BlockSpec((1, D), fixed)]
    n_part = 4 if head else 3
    operands = [dy] + ([target] if head else []) + [xh, rstd, g]
    return pl.pallas_call(
        body, name=name, grid=(T // tm,), in_specs=in_specs,
        out_specs=[tile, tile] + [part] * n_part,
        out_shape=[jax.ShapeDtypeStruct((T, D), F32), jax.ShapeDtypeStruct((T, D), _MXU)]
        + [jax.ShapeDtypeStruct((SUBLANES, D), F32)] * n_part,
        compiler_params=_cp(("arbitrary",)),
    )(*operands)


def _conv_cols(C, tc):
    per = (C // 2) // tc
    return per, (lambda j: (j // per) * (2 * per) + j % per)


def _glu_shifted(a_ref, g_ref, p_ref, S):
    u = a_ref[...] * _sigmoid(g_ref[...])
    rows = lax.broadcasted_iota(jnp.int32, u.shape, 0)
    for r in range(SUBLANES):
        p_ref[r, 0:CONV_TAPS_PAD, :] = jnp.zeros((CONV_TAPS_PAD, u.shape[1]), F32)
        p_ref[r, CONV_TAPS_PAD:CONV_TAPS_PAD + S, :] = u if r == 0 else jnp.where(rows >= r, pltpu.roll(u, r, 0), 0.0)


def _conv_fwd(h1, w_dw, b_dw, *, B, S, name):
    C = w_dw.shape[1]
    taps = CONV_TAPS_PAD - 1
    tc = LANES
    ch = _tile(S, 128)
    per, col_a = _conv_cols(C, tc)

    def body(a_ref, g_ref, w_ref, b_ref, o_ref, p_ref):
        _glu_shifted(a_ref, g_ref, p_ref, S)

        def chunk(ci, carry):
            base = pl.multiple_of(ci * ch, ch)
            acc = jnp.zeros((ch, tc), F32) + b_ref[...]
            for k in range(taps):
                q, r = divmod(taps - 1 - k, SUBLANES)
                start = pl.multiple_of(base + (CONV_TAPS_PAD - SUBLANES * q), SUBLANES)
                acc = acc + w_ref[pl.ds(k, 1), :] * p_ref[r, pl.ds(start, ch), :]
            o_ref[pl.ds(base, ch), :] = acc
            return carry

        lax.fori_loop(0, S // ch, chunk, 0)

    return pl.pallas_call(
        body, name=name, grid=(B, C // tc),
        in_specs=[pl.BlockSpec((S, tc), lambda b, j: (b, col_a(j))),
                  pl.BlockSpec((S, tc), lambda b, j: (b, col_a(j) + per)),
                  pl.BlockSpec((CONV_TAPS_PAD, tc), lambda b, j: (0, j)),
                  pl.BlockSpec((1, tc), lambda b, j: (0, j))],
        out_specs=pl.BlockSpec((S, tc), lambda b, j: (b, j)),
        out_shape=jax.ShapeDtypeStruct((B * S, C), F32),
        scratch_shapes=[pltpu.VMEM((SUBLANES, S + CONV_TAPS_PAD, tc), F32)],
        compiler_params=_cp(("parallel", "parallel")),
    )(h1, h1, w_dw, b_dw)


def _conv_bwd(dd, h1, w_dw, *, B, S, name):
    C = w_dw.shape[1]
    taps = CONV_TAPS_PAD - 1
    tc = LANES
    ch = _tile(S, 128)
    per, col_a = _conv_cols(C, tc)

    def body(d_ref, a_ref, g_ref, w_ref, du_ref, dw_ref, db_ref, p_ref, q_ref):
        b = pl.program_id(1)

        @pl.when(b == 0)
        def _():
            dw_ref[...] = jnp.zeros_like(dw_ref)
            db_ref[...] = jnp.zeros_like(db_ref)

        _glu_shifted(a_ref, g_ref, p_ref, S)
        d = d_ref[...]
        rows = lax.broadcasted_iota(jnp.int32, d.shape, 0)
        for r in range(SUBLANES):
            q_ref[r, S:S + CONV_TAPS_PAD, :] = jnp.zeros((CONV_TAPS_PAD, tc), F32)
            q_ref[r, 0:S, :] = d if r == 0 else jnp.where(rows < S - r, pltpu.roll(d, S - r, 0), 0.0)
        db_ref[...] += _fold8(d)

        def chunk(ci, carry):
            base = pl.multiple_of(ci * ch, ch)
            dch = d_ref[pl.ds(base, ch), :]
            acc = jnp.zeros((ch, tc), F32)
            for k in range(taps):
                q, r = divmod(taps - 1 - k, SUBLANES)
                up = pl.multiple_of(base + SUBLANES * q, SUBLANES)
                acc = acc + w_ref[pl.ds(k, 1), :] * q_ref[r, pl.ds(up, ch), :]
                down = pl.multiple_of(base + (CONV_TAPS_PAD - SUBLANES * q), SUBLANES)
                dw_ref[k] += _fold8(dch * p_ref[r, pl.ds(down, ch), :])
            du_ref[pl.ds(base, ch), :] = acc
            return carry

        lax.fori_loop(0, S // ch, chunk, 0)

    return pl.pallas_call(
        body, name=name, grid=(C // tc, B),
        in_specs=[pl.BlockSpec((S, tc), lambda j, b: (b, j)),
                  pl.BlockSpec((S, tc), lambda j, b: (b, col_a(j))),
                  pl.BlockSpec((S, tc), lambda j, b: (b, col_a(j) + per)),
                  pl.BlockSpec((CONV_TAPS_PAD, tc), lambda j, b: (0, j))],
        out_specs=[pl.BlockSpec((S, tc), lambda j, b: (b, j)),
                   pl.BlockSpec((CONV_TAPS_PAD, SUBLANES, tc), lambda j, b: (0, 0, j)),
                   pl.BlockSpec((SUBLANES, tc), lambda j, b: (0, j))],
        out_shape=[jax.ShapeDtypeStruct((B * S, C), F32),
                   jax.ShapeDtypeStruct((CONV_TAPS_PAD, SUBLANES, C), F32),
                   jax.ShapeDtypeStruct((SUBLANES, C), F32)],
        scratch_shapes=[pltpu.VMEM((SUBLANES, S + CONV_TAPS_PAD, tc), F32),
                        pltpu.VMEM((SUBLANES, S + CONV_TAPS_PAD, tc), F32)],
        compiler_params=_cp(("parallel", "arbitrary")),
    )(dd, h1, h1, w_dw)


def _ln_silu_fwd(v, g, b, *, name):
    T, C = v.shape
    tm = _tile(T, 512)

    def body(v_ref, g_ref, b_ref, s_ref, xh_ref, rs_ref):
        y, xh, rstd = _ln_rows(v_ref[...], g_ref[...], b_ref[...])
        s_ref[...] = (y * _sigmoid(y)).astype(_MXU)
        xh_ref[...] = xh
        rs_ref[...] = rstd

    row = lambda i: (i, 0)
    vec = pl.BlockSpec((1, C), lambda i: (0, 0))
    return pl.pallas_call(
        body, name=name, grid=(T // tm,),
        in_specs=[pl.BlockSpec((tm, C), row), vec, vec],
        out_specs=[pl.BlockSpec((tm, C), row), pl.BlockSpec((tm, C), row), pl.BlockSpec((tm, 1), row)],
        out_shape=[jax.ShapeDtypeStruct((T, C), _MXU), jax.ShapeDtypeStruct((T, C), F32),
                   jax.ShapeDtypeStruct((T, 1), F32)],
        compiler_params=_cp(("parallel",)),
    )(v, g, b)


def _ln_silu_bwd(ds, xh, rstd, g, b, *, name):
    T, C = ds.shape
    tm = _tile(T, 256)

    def body(ds_ref, xh_ref, rs_ref, g_ref, b_ref, dv_ref, dg_ref, db_ref):
        @pl.when(pl.program_id(0) == 0)
        def _():
            dg_ref[...] = jnp.zeros_like(dg_ref)
            db_ref[...] = jnp.zeros_like(db_ref)

        xh = xh_ref[...]
        gam = g_ref[...]
        y = xh * gam + b_ref[...]
        sig = _sigmoid(y)
        dln = ds_ref[...] * (sig * (1.0 + y * (1.0 - sig)))
        dv_ref[...] = _ln_bwd_rows(dln, xh, rs_ref[...], gam)
        dg_ref[...] += _fold8(dln * xh)
        db_ref[...] += _fold8(dln)

    row = lambda i: (i, 0)
    fixed = lambda i: (0, 0)
    vec = pl.BlockSpec((1, C), fixed)
    part = pl.BlockSpec((SUBLANES, C), fixed)
    return pl.pallas_call(
        body, name=name, grid=(T // tm,),
        in_specs=[pl.BlockSpec((tm, C), row), pl.BlockSpec((tm, C), row), pl.BlockSpec((tm, 1), row), vec, vec],
        out_specs=[pl.BlockSpec((tm, C), row), part, part],
        out_shape=[jax.ShapeDtypeStruct((T, C), F32)] + [jax.ShapeDtypeStruct((SUBLANES, C), F32)] * 2,
        compiler_params=_cp(("arbitrary",)),
    )(ds, xh, rstd, g, b)


def _glu_bwd(du, h1, *, name):
    T, C = du.shape
    il = C // 2
    tm = _tile(T, 256)

    def body(du_ref, h_ref, dh_ref, cs_ref):
        @pl.when(pl.program_id(0) == 0)
        def _():
            cs_ref[...] = jnp.zeros_like(cs_ref)

        for hb in range(2):
            a = h_ref[:, 2 * hb * il:(2 * hb + 1) * il]
            gate = h_ref[:, (2 * hb + 1) * il:(2 * hb + 2) * il]
            d = du_ref[:, hb * il:(hb + 1) * il]
            sig = _sigmoid(gate)
            da = d * sig
            dgate = d * a * sig * (1.0 - sig)
            dh_ref[:, 2 * hb * il:(2 * hb + 1) * il] = da.astype(_MXU)
            dh_ref[:, (2 * hb + 1) * il:(2 * hb + 2) * il] = dgate.astype(_MXU)
            cs_ref[:, 2 * hb * il:(2 * hb + 1) * il] += _fold8(da)
            cs_ref[:, (2 * hb + 1) * il:(2 * hb + 2) * il] += _fold8(dgate)

    row = lambda i: (i, 0)
    return pl.pallas_call(
        body, name=name, grid=(T // tm,),
        in_specs=[pl.BlockSpec((tm, C), row), pl.BlockSpec((tm, 2 * C), row)],
        out_specs=[pl.BlockSpec((tm, 2 * C), row), pl.BlockSpec((SUBLANES, 2 * C), lambda i: (0, 0))],
        out_shape=[jax.ShapeDtypeStruct((T, 2 * C), _MXU), jax.ShapeDtypeStruct((SUBLANES, 2 * C), F32)],
        compiler_params=_cp(("arbitrary",)),
    )(du, h1)


def _tril_mask(n):
    return lax.broadcasted_iota(jnp.int32, (n, n), 0) >= lax.broadcasted_iota(jnp.int32, (n, n), 1)


def _split_uv(t, il):
    u = jnp.concatenate([t[:, 0:il], t[:, 2 * il:3 * il]], axis=1)
    v = jnp.concatenate([t[:, il:2 * il], t[:, 3 * il:4 * il]], axis=1)
    return u, v


def _gmlp_gate_fwd(p, g, b, w_s, bsb, *, name):
    T, C2 = p.shape
    C = C2 // 2
    il = C // 2
    G, L, _ = w_s.shape
    assert G * L == C
    tm = _tile(T, 2 * L, L)

    def body(p_ref, g_ref, b_ref, ws_ref, bs_ref, us_ref, xh_ref, rs_ref, vn_ref, u_ref):
        z, _ = _gelu_parts(p_ref[...])
        u, v = _split_uv(z, il)
        vn, xh, rstd = _ln_rows(v, g_ref[...], b_ref[...])
        xh_ref[...] = xh
        rs_ref[...] = rstd
        vn_ref[...] = vn.astype(_MXU)
        u_ref[...] = u
        mask = _tril_mask(L)
        for gi in range(G):
            wc = jnp.where(mask, ws_ref[gi], 0.0).astype(_MXU)
            cols = slice(gi * L, (gi + 1) * L)
            for c in range(tm // L):
                rows = slice(c * L, (c + 1) * L)
                s = jnp.dot(wc, vn_ref[rows, cols], preferred_element_type=F32) + bs_ref[:, cols]
                us_ref[rows, cols] = (u_ref[rows, cols] * s).astype(_MXU)

    row = lambda i: (i, 0)
    fixed = lambda i: (0, 0)
    return pl.pallas_call(
        body, name=name, grid=(T // tm,),
        in_specs=[pl.BlockSpec((tm, C2), row), pl.BlockSpec((1, C), fixed), pl.BlockSpec((1, C), fixed),
                  pl.BlockSpec((G, L, L), lambda i: (0, 0, 0)), pl.BlockSpec((L, C), fixed)],
        out_specs=[pl.BlockSpec((tm, C), row), pl.BlockSpec((tm, C), row), pl.BlockSpec((tm, 1), row)],
        out_shape=[jax.ShapeDtypeStruct((T, C), _MXU), jax.ShapeDtypeStruct((T, C), F32),
                   jax.ShapeDtypeStruct((T, 1), F32)],
        scratch_shapes=[pltpu.VMEM((tm, C), _MXU), pltpu.VMEM((tm, C), F32)],
        compiler_params=_cp(("parallel",)),
    )(p, g, b, w_s, bsb)


def _gmlp_gate_bwd(dus, p, xh, rstd, g, b, w_s, bsb, *, name):
    T, C2 = p.shape
    C = C2 // 2
    il = C // 2
    G, L, _ = w_s.shape
    tm = _tile(T, 2 * L, L)

    def body(dus_ref, p_ref, xh_ref, rs_ref, g_ref, b_ref, ws_ref, bs_ref,
             dp_ref, dg_ref, db_ref, cs_ref, dws_ref, dbs_ref, vn_ref, u_ref, dvn_ref, du_ref):
        @pl.when(pl.program_id(0) == 0)
        def _():
            dg_ref[...] = jnp.zeros_like(dg_ref)
            db_ref[...] = jnp.zeros_like(db_ref)
            cs_ref[...] = jnp.zeros_like(cs_ref)
            dws_ref[...] = jnp.zeros_like(dws_ref)
            dbs_ref[...] = jnp.zeros_like(dbs_ref)

        z, gp = _gelu_parts(p_ref[...])
        u, _ = _split_uv(z, il)
        xh = xh_ref[...]
        gam = g_ref[...]
        vn_ref[...] = (xh * gam + b_ref[...]).astype(_MXU)
        u_ref[...] = u
        mask = _tril_mask(L)
        for gi in range(G):
            wc = jnp.where(mask, ws_ref[gi], 0.0).astype(_MXU)
            cols = slice(gi * L, (gi + 1) * L)
            for c in range(tm // L):
                rows = slice(c * L, (c + 1) * L)
                vnb = vn_ref[rows, cols]
                s = jnp.dot(wc, vnb, preferred_element_type=F32) + bs_ref[:, cols]
                d = dus_ref[rows, cols]
                du_ref[rows, cols] = d * s
                ds = d * u_ref[rows, cols]
                dbs_ref[:, cols] += ds
                dsb = ds.astype(_MXU)
                dw = lax.dot_general(dsb, vnb, (((1,), (1,)), ((), ())), preferred_element_type=F32)
                dws_ref[gi] += jnp.where(mask, dw, 0.0)
                dvn_ref[rows, cols] = lax.dot_general(wc, dsb, (((0,), (0,)), ((), ())), preferred_element_type=F32)
        dvn = dvn_ref[...]
        dg_ref[...] += _fold8(dvn * xh)
        db_ref[...] += _fold8(dvn)
        dv = _ln_bwd_rows(dvn, xh, rs_ref[...], gam)
        du = du_ref[...]
        for hb in range(2):
            for part, src in ((0, du), (1, dv)):
                lo = (2 * hb + part) * il
                dp = src[:, hb * il:(hb + 1) * il] * gp[:, lo:lo + il]
                dp_ref[:, lo:lo + il] = dp.astype(_MXU)
                cs_ref[:, lo:lo + il] += _fold8(dp)

    row = lambda i: (i, 0)
    fixed = lambda i: (0, 0)
    part_c = pl.BlockSpec((SUBLANES, C), fixed)
    return pl.pallas_call(
        body, name=name, grid=(T // tm,),
        in_specs=[pl.BlockSpec((tm, C), row), pl.BlockSpec((tm, C2), row), pl.BlockSpec((tm, C), row),
                  pl.BlockSpec((tm, 1), row), pl.BlockSpec((1, C), fixed), pl.BlockSpec((1, C), fixed),
                  pl.BlockSpec((G, L, L), lambda i: (0, 0, 0)), pl.BlockSpec((L, C), fixed)],
        out_specs=[pl.BlockSpec((tm, C2), row), part_c, part_c, pl.BlockSpec((SUBLANES, C2), fixed),
                   pl.BlockSpec((G, L, L), lambda i: (0, 0, 0)), pl.BlockSpec((L, C), fixed)],
        out_shape=[jax.ShapeDtypeStruct((T, C2), _MXU), jax.ShapeDtypeStruct((SUBLANES, C), F32),
                   jax.ShapeDtypeStruct((SUBLANES, C), F32), jax.ShapeDtypeStruct((SUBLANES, C2), F32),
                   jax.ShapeDtypeStruct((G, L, L), F32), jax.ShapeDtypeStruct((L, C), F32)],
        scratch_shapes=[pltpu.VMEM((tm, C), _MXU), pltpu.VMEM((tm, C), F32), pltpu.VMEM((tm, C), F32),
                        pltpu.VMEM((tm, C), F32)],
        compiler_params=_cp(("arbitrary",)),
    )(dus, p, xh, rstd, g, b, w_s, bsb)


def _ffn_conv(h, prev8, w_ref, b_ref):
    h1 = _shift_down(prev8, h, 1)
    h2 = _shift_down(prev8, h, 2)
    hc = w_ref[pl.ds(2, 1), :] * h + w_ref[pl.ds(1, 1), :] * h1 + w_ref[pl.ds(0, 1), :] * h2 + b_ref[...]
    return hc, h1, h2


def _ffn_up_fwd(xb, w, wl, b_up, w_dw, b_dw, *, S, name):
    T, D = xb.shape
    N = w.shape[-1]
    tn = N // N_CHIPS
    tm = _tile(S, 256)
    spt = S // tm

    def body(x_ref, w_ref, bu_ref, wd_ref, bd_ref, h_ref, f_ref, carry_ref):
        i = pl.program_id(1)

        @pl.when(i % spt == 0)
        def _():
            carry_ref[...] = jnp.zeros_like(carry_ref)

        h = jnp.dot(x_ref[...].astype(_MXU), w_ref[...].astype(_MXU), preferred_element_type=F32) + bu_ref[...]
        hq = h.astype(_HDT)
        h_ref[...] = hq
        h = hq.astype(F32)
        hc, _, _ = _ffn_conv(h, carry_ref[...], wd_ref, bd_ref)
        carry_ref[...] = h[tm - SUBLANES:tm]
        gte = hc[:, :tn]
        f_ref[...] = (gte * _sigmoid(gte) * hc[:, tn:]).astype(_MXU)

    pair = lambda j, i: (0, j)
    return pl.pallas_call(
        body, name=name, grid=(2, T // tm),
        in_specs=[pl.BlockSpec((tm, D), lambda j, i: (i, 0)),
                  pl.BlockSpec((None, D, 2 * tn), lambda j, i: (wl, 0, j)),
                  pl.BlockSpec((1, 2 * tn), pair), pl.BlockSpec((SUBLANES, 2 * tn), pair),
                  pl.BlockSpec((1, 2 * tn), pair)],
        out_specs=[pl.BlockSpec((tm, 2 * tn), lambda j, i: (i, j)), pl.BlockSpec((tm, tn), lambda j, i: (i, j))],
        out_shape=[jax.ShapeDtypeStruct((T, N), _HDT), jax.ShapeDtypeStruct((T, N // 2), _MXU)],
        scratch_shapes=[pltpu.VMEM((SUBLANES, 2 * tn), F32)],
        compiler_params=_cp(("parallel", "arbitrary")),
    )(xb, w, b_up, w_dw, b_dw)


def _ffn_bwd(dzb, w_down, wl, hs, w_dw, b_dw, *, S, name):
    T, D = dzb.shape
    N = hs.shape[1]
    tn = N // N_CHIPS
    tm = _tile(S, 256)
    spt = S // tm
    nt = T // tm
    hal = 16

    def body(dz_ref, wd_ref, h_ref, halo_ref, wc_ref, bc_ref, dh_ref, cs_ref, dw_ref, db_ref, carry_ref):
        i = pl.program_id(1)
        ii = nt - 1 - i

        @pl.when(i == 0)
        def _():
            cs_ref[...] = jnp.zeros_like(cs_ref)
            dw_ref[...] = jnp.zeros_like(dw_ref)
            db_ref[...] = jnp.zeros_like(db_ref)

        df = lax.dot_general(dz_ref[...].astype(_MXU), wd_ref[...].astype(_MXU), (((1,), (1,)), ((), ())),
                             preferred_element_type=F32)
        h = h_ref[...].astype(F32)
        prev8 = halo_ref[...].astype(F32)[hal - SUBLANES:hal]
        prev8 = jnp.where(ii % spt == 0, 0.0, prev8)
        hc, h1, h2 = _ffn_conv(h, prev8, wc_ref, bc_ref)
        gte, val = hc[:, :tn], hc[:, tn:]
        sig = _sigmoid(gte)
        dval = df * (gte * sig)
        dg = df * val * (sig * (1.0 + gte * (1.0 - sig)))
        dhc = jnp.concatenate([dg, dval], axis=1)
        db_ref[...] += _fold8(dhc)
        dw_ref[2] += _fold8(dhc * h)
        dw_ref[1] += _fold8(dhc * h1)
        dw_ref[0] += _fold8(dhc * h2)
        nxt = jnp.where((ii + 1) % spt == 0, 0.0, carry_ref[...])
        dh = (wc_ref[pl.ds(2, 1), :] * dhc + wc_ref[pl.ds(1, 1), :] * _shift_up(dhc, nxt, 1)
              + wc_ref[pl.ds(0, 1), :] * _shift_up(dhc, nxt, 2))
        carry_ref[...] = dhc[0:SUBLANES]
        cs_ref[...] += _fold8(dh)
        dh_ref[...] = dh.astype(_MXU)

    pair = lambda j, i: (0, j)
    rev = lambda j, i: (nt - 1 - i, j)
    return pl.pallas_call(
        body, name=name, grid=(2, nt),
        in_specs=[pl.BlockSpec((tm, D), lambda j, i: (nt - 1 - i, 0)),
                  pl.BlockSpec((None, tn, D), lambda j, i: (wl, j, 0)),
                  pl.BlockSpec((tm, 2 * tn), rev),
                  pl.BlockSpec((hal, 2 * tn), lambda j, i: (jnp.maximum((nt - 1 - i) * (tm // hal) - 1, 0), j)),
                  pl.BlockSpec((SUBLANES, 2 * tn), pair), pl.BlockSpec((1, 2 * tn), pair)],
        out_specs=[pl.BlockSpec((tm, 2 * tn), rev), pl.BlockSpec((SUBLANES, 2 * tn), pair),
                   pl.BlockSpec((3, SUBLANES, 2 * tn), lambda j, i: (0, 0, j)),
                   pl.BlockSpec((SUBLANES, 2 * tn), pair)],
        out_shape=[jax.ShapeDtypeStruct((T, N), _MXU), jax.ShapeDtypeStruct((SUBLANES, N), F32),
                   jax.ShapeDtypeStruct((3, SUBLANES, N), F32), jax.ShapeDtypeStruct((SUBLANES, N), F32)],
        scratch_shapes=[pltpu.VMEM((SUBLANES, 2 * tn), F32)],
        compiler_params=_cp(("parallel", "arbitrary")),
    )(dzb, w_down, hs, hs, w_dw, b_dw)


def _sum_pieces(g, r, me, layer, acc, n_layers, *, name):
    _, pr, pc = g.shape
    tr = _tile(pr, 128)

    def body(me_ref, g_ref, r_ref, *rest):
        o_ref = rest[-1]
        total = g_ref[...].astype(F32)
        for s in range(N_DEV - 1):
            total = total + r_ref[s].astype(F32)
        o_ref[...] = total

    in_specs = [pl.BlockSpec((None, tr, pc), lambda i, me_ref: (me_ref[0], i, 0)),
                pl.BlockSpec((N_DEV - 1, tr, pc), lambda i, me_ref: (0, i, 0))]
    operands = [me, g, r]
    aliases = {}
    if acc is not None:
        in_specs.append(ANY)
        operands.append(acc)
        aliases = {3: 0}
    return pl.pallas_call(
        body, name=name,
        grid_spec=pltpu.PrefetchScalarGridSpec(
            num_scalar_prefetch=1, grid=(pr // tr,), in_specs=in_specs,
            out_specs=pl.BlockSpec((None, tr, pc), lambda i, me_ref: (layer, i, 0))),
        out_shape=jax.ShapeDtypeStruct((n_layers, pr, pc), F32),
        input_output_aliases=aliases,
        compiler_params=_cp(("parallel",)),
    )(*operands)


def _adam_math(w, g, m, v):
    bc1 = 1.0 - ADAM_B1 ** ADAM_STEP
    bc2 = 1.0 - ADAM_B2 ** ADAM_STEP
    m = ADAM_B1 * m + (1.0 - ADAM_B1) * g
    v = ADAM_B2 * v + (1.0 - ADAM_B2) * (g * g)
    return -ADAM_LR * ((m / bc1) / (jnp.sqrt(v / bc2) + ADAM_EPS) + ADAM_WD * w), m, v


def _adam(w, g, m, v, *, name):
    R, C = w.shape
    tr = _tile(R, 256)

    def body(w_ref, g_ref, m_ref, v_ref, d_ref, mo_ref, vo_ref):
        d_ref[...], mo_ref[...], vo_ref[...] = _adam_math(w_ref[...], g_ref[...], m_ref[...], v_ref[...])

    spec = pl.BlockSpec((tr, C), lambda i: (i, 0))
    return pl.pallas_call(
        body, name=name, grid=(R // tr,), in_specs=[spec] * 4, out_specs=[spec] * 3,
        out_shape=[jax.ShapeDtypeStruct((R, C), F32)] * 3,
        compiler_params=_cp(("parallel",)),
    )(w, g, m, v)


def _adam_halves(w, own, got, m, v, core, *, name):
    L, R, C = w.shape
    rh = R // 2
    tr = _tile(rh, 256)
    nt = rh // tr

    def body(c_ref, w_ref, own_ref, got_ref, m_ref, v_ref, g_ref, d_ref, mo_ref, vo_ref):
        g = jnp.where(pl.program_id(1) == c_ref[0], own_ref[...], got_ref[...])
        g_ref[...] = g
        d_ref[...], mo_ref[...], vo_ref[...] = _adam_math(w_ref[...], g, m_ref[...], v_ref[...])

    full = pl.BlockSpec((None, tr, C), lambda l, h, t, c_ref: (l, h * nt + t, 0))
    half = pl.BlockSpec((None, tr, C), lambda l, h, t, c_ref: (l, t, 0))
    return pl.pallas_call(
        body, name=name,
        grid_spec=pltpu.PrefetchScalarGridSpec(
            num_scalar_prefetch=1, grid=(L, 2, nt), in_specs=[full, half, half, full, full], out_specs=[full] * 4),
        out_shape=[jax.ShapeDtypeStruct((L, R, C), F32)] * 4,
        compiler_params=_cp(("parallel", "parallel", "parallel")),
    )(core, w, own, got, m, v)


def _remote(src, dst, send, recv, dev):
    return pltpu.make_async_remote_copy(src_ref=src, dst_ref=dst, send_sem=send, recv_sem=recv,
                                        device_id=dev, device_id_type=MESH)


def _place_w(shard, pos, layer, *, axis, name):
    _, R, C = shard.shape
    tr = _tile(R, 512, 16)
    nt = R // tr
    if axis == 2:
        out_shape = (1, R, N_CHIPS * C)
        out_map = lambda t, q: (0, t, q[0])
    else:
        out_shape = (1, N_CHIPS * R, C)
        out_map = lambda t, q: (0, q[0] * nt + t, 0)

    def body(q_ref, s_ref, o_ref):
        o_ref[...] = s_ref[...].astype(_WIRE)

    return pl.pallas_call(
        body, name=name,
        grid_spec=pltpu.PrefetchScalarGridSpec(
            num_scalar_prefetch=1, grid=(nt,),
            in_specs=[pl.BlockSpec((None, tr, C), lambda t, q: (layer, t, 0))],
            out_specs=pl.BlockSpec((None, tr, C), out_map)),
        out_shape=jax.ShapeDtypeStruct(out_shape, _WIRE),
        compiler_params=_cp(("parallel",)),
    )(pos, shard)


def _ag_window(ref, kind, px, py, h):
    axis, perm = kind
    q = 2 * px + py
    if perm:
        q = _perm_idx(q)
    if axis == 2:
        R, C = ref.shape[1], ref.shape[2] // N_CHIPS
        rh = R // 2
        return ref.at[:, pl.ds(pl.multiple_of(h * rh, 16), rh), pl.ds(pl.multiple_of(q * C, LANES), C)]
    R = ref.shape[1] // N_CHIPS
    rh = R // 2
    return ref.at[:, pl.ds(pl.multiple_of(q * R + h * rh, 16), rh), :]


def _ag_ici_copies(refs, kinds, send, recv):
    x, y, c = lax.axis_index("x"), lax.axis_index("y"), lax.axis_index("c")
    chips = [(1 - x, y), (x, 1 - y), (1 - x, 1 - y)]
    sends, recvs = [], []
    for a, (ref, kind) in enumerate(zip(refs, kinds)):
        own = _ag_window(ref, kind, x, y, c)
        for i, (px, py) in enumerate(chips):
            k = 3 * a + i
            sends.append(_remote(own, own, send.at[k], recv.at[k], (px, py, c)))
            recvs.append(_remote(own, _ag_window(ref, kind, px, py, c), send.at[k], recv.at[k], (px, py, c)))
    return sends, recvs


def _ag_start(arrs, kinds, after, *, name):
    n = len(arrs)

    def body(*refs):
        in_refs = refs[:n]
        send, recv = refs[n + len(after)], refs[n + len(after) + 1]
        token = refs[-1]
        sends, _ = _ag_ici_copies(in_refs, kinds, send, recv)
        for cp in sends:
            cp.start()
        token[...] = jnp.zeros_like(token)

    sems = pltpu.SemaphoreType.DMA((3 * n,))
    out = pl.pallas_call(
        body, name=name,
        out_shape=(sems, sems) + tuple(pltpu.HBM(a.shape, a.dtype) for a in arrs)
        + (jax.ShapeDtypeStruct((SUBLANES, LANES), F32),),
        in_specs=(HBM,) * n + (ANY,) * len(after),
        out_specs=(SEMS, SEMS) + (HBM,) * n + (pl.BlockSpec(memory_space=pltpu.VMEM),),
        input_output_aliases={a: 2 + a for a in range(n)},
        compiler_params=pltpu.CompilerParams(has_side_effects=EFFECT),
    )(*[pltpu.with_memory_space_constraint(a, pltpu.HBM) for a in arrs], *after)
    return out[0], out[1], list(out[2:2 + n]), out[-1]


def _ag_wait(send, recv, arrs, kinds, after, *, name):
    n = len(arrs)

    def body(*refs):
        in_refs = refs[:n]
        send, recv = refs[n], refs[n + 1]
        sends, recvs = _ag_ici_copies(in_refs, kinds, send, recv)
        for cp in sends:
            cp.wait_send()
        for cp in recvs:
            cp.wait_recv()

    out = pl.pallas_call(
        body, name=name,
        out_shape=tuple(pltpu.HBM(a.shape, a.dtype) for a in arrs),
        in_specs=(HBM,) * n + (SEMS, SEMS) + (ANY,) * len(after), out_specs=(HBM,) * n,
        input_output_aliases={a: a for a in range(n)},
        compiler_params=pltpu.CompilerParams(has_side_effects=EFFECT),
    )(*arrs, send, recv, *after)
    return list(out)


def _ag_forward(arrs, kinds, *, name):
    n = len(arrs)

    def body(*refs):
        o_refs, send, recv = refs[n:2 * n], refs[2 * n], refs[2 * n + 1]
        x, y, c = lax.axis_index("x"), lax.axis_index("y"), lax.axis_index("c")
        chips = [(1 - x, y), (x, 1 - y), (1 - x, 1 - y)]
        sib = (x, y, 1 - c)
        sends, recvs = [], []
        for a, (ref, kind) in enumerate(zip(o_refs, kinds)):
            for i, (px, py) in enumerate(chips):
                k = 3 * a + i
                got = _ag_window(ref, kind, px, py, c)
                cp = _remote(got, got, send.at[k], recv.at[k], sib)
                cp.start()
                sends.append(cp)
                recvs.append(_remote(got, _ag_window(ref, kind, px, py, 1 - c), send.at[k], recv.at[k], sib))
        for cp in recvs:
            cp.wait_recv()
        for cp in sends:
            cp.wait_send()

    out = pl.pallas_call(
        body, name=name, in_specs=[ANY] * n, out_specs=[ANY] * n,
        out_shape=[jax.ShapeDtypeStruct(a.shape, a.dtype) for a in arrs],
        input_output_aliases={a: a for a in range(n)},
        scratch_shapes=[pltpu.SemaphoreType.DMA((3 * n,)), pltpu.SemaphoreType.DMA((3 * n,))],
    )(*arrs)
    return list(out)


def _flip(x, y, c, f):
    return ((1 - x) if f & 4 else x, (1 - y) if f & 2 else y, (1 - c) if f & 1 else c)


def _rs_copies(g_ref, land_ref, send, recv):
    x, y, c = lax.axis_index("x"), lax.axis_index("y"), lax.axis_index("c")
    cps = []
    for f in range(1, N_DEV):
        tx, ty, tcx = _flip(x, y, c, f)
        cps.append(_remote(g_ref.at[4 * tx + 2 * ty + tcx], land_ref.at[f - 1], send.at[f - 1], recv.at[f - 1],
                           (tx, ty, tcx)))
    return cps


def _rs_start(g, *, name):
    _, pr, pc = g.shape
    land_shape = (N_DEV - 1, pr, pc)

    def body(g_ref, land_ref, send, recv, g_thru, land_thru, token):
        for cp in _rs_copies(g_ref, land_ref, send, recv):
            cp.start()
        token[...] = jnp.zeros_like(token)

    sems = pltpu.SemaphoreType.DMA((N_DEV - 1,))
    return pl.pallas_call(
        body, name=name,
        out_shape=(sems, sems, pltpu.HBM(g.shape, g.dtype), pltpu.HBM(land_shape, g.dtype),
                   jax.ShapeDtypeStruct((SUBLANES, LANES), F32)),
        in_specs=(HBM, HBM), out_specs=(SEMS, SEMS, HBM, HBM, pl.BlockSpec(memory_space=pltpu.VMEM)),
        input_output_aliases={0: 2, 1: 3},
        compiler_params=pltpu.CompilerParams(has_side_effects=EFFECT),
    )(pltpu.with_memory_space_constraint(g, pltpu.HBM),
      pltpu.with_memory_space_constraint(lax.empty(land_shape, g.dtype), pltpu.HBM))


def _rs_wait(send, recv, g_thru, land_thru, after, *, name):
    def body(g_ref, land_ref, send, recv, after_ref, g_out, land_out):
        cps = _rs_copies(g_ref, land_ref, send, recv)
        for cp in cps:
            cp.wait_send()
        for cp in cps:
            cp.wait_recv()

    return pl.pallas_call(
        body, name=name,
        out_shape=(pltpu.HBM(g_thru.shape, g_thru.dtype), pltpu.HBM(land_thru.shape, land_thru.dtype)),
        in_specs=(HBM, HBM, SEMS, SEMS, ANY), out_specs=(HBM, HBM), input_output_aliases={0: 0, 1: 1},
        compiler_params=pltpu.CompilerParams(has_side_effects=EFFECT),
    )(g_thru, land_thru, send, recv, after)


def _pair_exchange(own, *, name):
    def body(own_ref, got_ref, send, recv):
        x, y, c = lax.axis_index("x"), lax.axis_index("y"), lax.axis_index("c")
        cp = _remote(own_ref, got_ref, send, recv, (x, y, 1 - c))
        cp.start()
        cp.wait_recv()
        cp.wait_send()

    return pl.pallas_call(
        body, name=name, in_specs=[ANY], out_specs=ANY, out_shape=jax.ShapeDtypeStruct(own.shape, own.dtype),
        scratch_shapes=[pltpu.SemaphoreType.DMA, pltpu.SemaphoreType.DMA],
    )(own)


def _allreduce_flat(vec, *, name):
    n = vec.shape[0]
    unit = N_DEV * SUBLANES * LANES
    npad = -(-n // unit) * unit
    rows = npad // (N_DEV * LANES)
    xin = jnp.pad(vec, (0, npad - n)).reshape(N_DEV, rows, LANES)

    def body(x_ref, y_ref, a_ref, send_a, recv_a, send_b, recv_b):
        x, y, c = lax.axis_index("x"), lax.axis_index("y"), lax.axis_index("c")
        me = 4 * x + 2 * y + c
        a_ref[me] = x_ref[me]
        sends, recvs = [], []
        for f in range(1, N_DEV):
            dev = _flip(x, y, c, f)
            t = 4 * dev[0] + 2 * dev[1] + dev[2]
            cp = _remote(x_ref.at[t], a_ref.at[me], send_a.at[f - 1], recv_a.at[f - 1], dev)
            cp.start()
            sends.append(cp)
            recvs.append(_remote(x_ref.at[me], a_ref.at[t], send_a.at[f - 1], recv_a.at[f - 1], dev))
        for cp in recvs:
            cp.wait_recv()
        for cp in sends:
            cp.wait_send()
        acc = a_ref[0]
        for s in range(1, N_DEV):
            acc = acc + a_ref[s]
        y_ref[me] = acc
        sends, recvs = [], []
        for f in range(1, N_DEV):
            dev = _flip(x, y, c, f)
            t = 4 * dev[0] + 2 * dev[1] + dev[2]
            cp = _remote(y_ref.at[me], y_ref.at[me], send_b.at[f - 1], recv_b.at[f - 1], dev)
            cp.start()
            sends.append(cp)
            recvs.append(_remote(y_ref.at[me], y_ref.at[t], send_b.at[f - 1], recv_b.at[f - 1], dev))
        for cp in recvs:
            cp.wait_recv()
        for cp in sends:
            cp.wait_send()

    vm = pl.BlockSpec(memory_space=pltpu.VMEM)
    out = pl.pallas_call(
        body, name=name, in_specs=[vm], out_specs=vm,
        out_shape=jax.ShapeDtypeStruct((N_DEV, rows, LANES), F32),
        scratch_shapes=[pltpu.VMEM((N_DEV, rows, LANES), F32)] + [pltpu.SemaphoreType.DMA((N_DEV - 1,))] * 4,
        compiler_params=_cp(),
    )(xin)
    return out.reshape(npad)[:n]


def _perm_cols(v, blocks=N_CHIPS):
    lead, n = v.shape[:-1], v.shape[-1]
    return v.reshape(lead + (blocks, n // blocks))[..., PERM, :].reshape(lead + (n,))


def _pack(arrs):
    return jnp.concatenate([a.reshape(-1).astype(F32) for a in arrs])


def _unpack(flat, shapes):
    out, pos = [], 0
    for s in shapes:
        n = 1
        for d in s:
            n *= d
        out.append(flat[pos:pos + n].reshape(s))
        pos += n
    return out


def kernel(x, conv_w_in, conv_b_in, conv_w_dw, conv_b_dw, conv_ln_g, conv_ln_b, conv_w_out, conv_b_out, gmlp_w_in, gmlp_b_in, gmlp_ln_g, gmlp_ln_b, gmlp_w_s, gmlp_b_s, gmlp_w_out, gmlp_b_out, ffn_w_up, ffn_b_up, ffn_w_dw, ffn_b_dw, ffn_w_down, ffn_b_down, norm1_g, norm1_b, norm2_g, norm2_b, loss_target, m_conv_w_in, m_conv_b_in, m_conv_w_dw, m_conv_b_dw, m_conv_ln_g, m_conv_ln_b, m_conv_w_out, m_conv_b_out, m_gmlp_w_in, m_gmlp_b_in, m_gmlp_ln_g, m_gmlp_ln_b, m_gmlp_w_s, m_gmlp_b_s, m_gmlp_w_out, m_gmlp_b_out, m_ffn_w_up, m_ffn_b_up, m_ffn_w_dw, m_ffn_b_dw, m_ffn_w_down, m_ffn_b_down, m_norm1_g, m_norm1_b, m_norm2_g, m_norm2_b, v_conv_w_in, v_conv_b_in, v_conv_w_dw, v_conv_b_dw, v_conv_ln_g, v_conv_ln_b, v_conv_w_out, v_conv_b_out, v_gmlp_w_in, v_gmlp_b_in, v_gmlp_ln_g, v_gmlp_ln_b, v_gmlp_w_s, v_gmlp_b_s, v_gmlp_w_out, v_gmlp_b_out, v_ffn_w_up, v_ffn_b_up, v_ffn_w_dw, v_ffn_b_dw, v_ffn_w_down, v_ffn_b_down, v_norm1_g, v_norm1_b, v_norm2_g, v_norm2_b):
    P = dict(locals())
    WEIGHTS = ['conv_w_in', 'conv_b_in', 'conv_w_dw', 'conv_b_dw', 'conv_ln_g', 'conv_ln_b', 'conv_w_out',
               'conv_b_out', 'gmlp_w_in', 'gmlp_b_in', 'gmlp_ln_g', 'gmlp_ln_b', 'gmlp_w_s', 'gmlp_b_s',
               'gmlp_w_out', 'gmlp_b_out', 'ffn_w_up', 'ffn_b_up', 'ffn_w_dw', 'ffn_b_dw', 'ffn_w_down',
               'ffn_b_down', 'norm1_g', 'norm1_b', 'norm2_g', 'norm2_b']
    BIG = ['conv_w_in', 'conv_w_out', 'gmlp_w_in', 'gmlp_w_out', 'ffn_w_up', 'ffn_w_down']
    SMALL_SHARDED = {'conv_w_dw': 2, 'gmlp_b_in': 1, 'gmlp_ln_g': 1, 'gmlp_ln_b': 1, 'gmlp_b_out': 1, 'ffn_w_dw': 2}

    B, S, D = x.shape
    T = B * S
    depth = norm1_g.shape[0]
    alpha = (2.0 * depth) ** 0.25
    C = conv_w_out.shape[-1]
    F2 = ffn_b_up.shape[-1]
    G, L = gmlp_w_s.shape[1], gmlp_w_s.shape[2]
    xi, yi, ci = lax.axis_index("x"), lax.axis_index("y"), lax.axis_index("c")
    shard = 2 * xi + yi

    i32 = lambda v: jnp.reshape(v, (1,)).astype(jnp.int32)
    pos_plain, pos_perm = i32(shard), i32(_perm_idx(shard))
    me_id, core_id = i32(4 * xi + 2 * yi + ci), i32(ci)

    groups = []
    for i in range(depth):
        mix = 'conv' if i % 2 == 0 else 'gmlp'
        groups.append((f"{mix}{i // 2}", [(mix + '_w_in', i // 2, 2, True), (mix + '_w_out', i // 2, 1, False)]))
        groups.append((f"ffn{i}", [('ffn_w_up', i, 2, True), ('ffn_w_down', i, 1, False)]))
    started, order = {}, []
    for gname, members in groups:
        placed = [_place_w(P[n], pos_perm if perm else pos_plain, l, axis=axis, name=f"place_{n}_{l}")
                  for n, l, axis, perm in members]
        kinds = [(axis, perm) for _, _, axis, perm in members]
        send, recv, arrs, token = _ag_start(placed, kinds, order, name=f"ag_start_{gname}")
        order = [token]
        started[gname] = (send, recv, arrs, kinds, [(n, l) for n, l, _, _ in members])
    wts = {}

    def arrive(gname, after):
        send, recv, arrs, kinds, keys = started[gname]
        arrs = _ag_wait(send, recv, arrs, kinds, after, name=f"ag_wait_{gname}")
        arrs = _ag_forward(arrs, kinds, name=f"ag_fwd_{gname}")
        wts.update(zip(keys, arrs))

    sm_names = list(SMALL_SHARDED)
    sm_shapes = [P[n].shape for n in sm_names]
    mine = _pack([P[n] for n in sm_names]) * (ci == 0).astype(F32)
    buf = jnp.zeros((N_CHIPS, mine.shape[0]), F32)
    buf = lax.dynamic_update_slice(buf, mine[None], (shard, 0))
    gathered = _allreduce_flat(buf.reshape(-1), name="ag_small").reshape(N_CHIPS, -1)
    full = {}
    for n, parts in zip(sm_names, zip(*[_unpack(gathered[k], sm_shapes) for k in range(N_CHIPS)])):
        full[n] = jnp.concatenate(parts, axis=SMALL_SHARDED[n])
    for n in WEIGHTS:
        if n not in BIG and n not in full:
            full[n] = P[n]

    assert G * L == C, "a gMLP group must be as wide as a chunk is long"

    def row(v):
        return v.reshape(1, -1)

    def pad_rows(v, r):
        return jnp.pad(v, ((0, r - v.shape[0]), (0, 0)))

    xf = x.reshape(T, D)
    saved = []
    cur, cur_b = xf, xf
    for i in range(depth):
        j = i // 2
        sv = {'x': cur, 'xb': cur_b}
        arrive(groups[2 * i][0], order if i == 0 else [cur_b])
        if i % 2 == 0:
            b_in = row(_perm_cols(full['conv_b_in'][j]))
            h1 = _mm(cur_b, wts['conv_w_in', j], bl=0, bias=b_in, tm=_tile(T, 512), tn=_tile(2 * C, 1024, LANES),
                     tk=D, name=f"conv_in_{j}")
            wdw = pad_rows(full['conv_w_dw'][j], CONV_TAPS_PAD)
            dwo = _conv_fwd(h1, wdw, row(full['conv_b_dw'][j]), B=B, S=S, name=f"conv_dw_{j}")
            s_act, xhc, rsc = _ln_silu_fwd(dwo, row(full['conv_ln_g'][j]), row(full['conv_ln_b'][j]),
                                           name=f"conv_ln_{j}")
            sv.update(h1=h1, wdw=wdw, act=s_act, xhc=xhc, rsc=rsc)
            y1 = _mm_res_ln(s_act, wts['conv_w_out', j], 0, row(full['conv_b_out'][j]), cur, alpha, row(norm1_g[i]),
                            row(norm1_b[i]), name=f"conv_out_ln_{j}")
        else:
            b_in = row(_perm_cols(full['gmlp_b_in'][j]))
            pre = _mm(cur_b, wts['gmlp_w_in', j], bl=0, bias=b_in, tm=_tile(T, 512), tn=_tile(2 * C, 1024, LANES),
                      tk=D, name=f"gmlp_in_{j}")
            bsb = jnp.repeat(gmlp_b_s[j].T, L, axis=1)
            us, xhv, rsv = _gmlp_gate_fwd(pre, row(full['gmlp_ln_g'][j]), row(full['gmlp_ln_b'][j]), gmlp_w_s[j],
                                          bsb, name=f"gmlp_gate_{j}")
            sv.update(pre=pre, bsb=bsb, act=us, xhv=xhv, rsv=rsv)
            y1 = _mm_res_ln(us, wts['gmlp_w_out', j], 0, row(full['gmlp_b_out'][j]), cur, alpha, row(norm1_g[i]),
                            row(norm1_b[i]), name=f"gmlp_out_ln_{j}")
        x1, x1b, xh1, rs1 = y1
        arrive(groups[2 * i + 1][0], [x1b])
        wdw3 = pad_rows(_perm_cols(full['ffn_w_dw'][i]), SUBLANES)
        bdw3 = row(_perm_cols(ffn_b_dw[i]))
        hs, f_act = _ffn_up_fwd(x1b, wts['ffn_w_up', i], 0, row(_perm_cols(ffn_b_up[i])), wdw3, bdw3, S=S,
                                name=f"ffn_up_{i}")
        x2, x2b, xh2, rs2 = _mm_res_ln(f_act, wts['ffn_w_down', i], 0, row(ffn_b_down[i]), x1, alpha, row(norm2_g[i]),
                                       row(norm2_b[i]), name=f"ffn_down_ln_{i}")
        sv.update(x1=x1, x1b=x1b, xh1=xh1, rs1=rs1, hs=hs, f=f_act, wdw3=wdw3, bdw3=bdw3, xh2=xh2, rs2=rs2)
        saved.append(sv)
        cur, cur_b = x2, x2b

    sg = {n: [None] * full[n].shape[0] for n in WEIGHTS if n not in BIG}
    inflight = {n: [None] * P[n].shape[0] for n in BIG}
    deps = []
    tgt = loss_target.reshape(T, D)
    dcur = None
    loss_part = None
    tk_t = _tile(T, 512)

    def wgrad(n, l, a_, b_, **kw):
        g = _mm(a_, b_, ta=True, out_dtype=_WIRE, tk=tk_t, name=f"{n}_dw_{l}", deps=deps, **kw)
        send, recv, g_thru, land, token = _rs_start(g, name=f"rs_start_{n}_{l}")
        inflight[n][l] = (send, recv, g_thru, land)
        deps.append(token)

    for i in reversed(range(depth)):
        j = i // 2
        sv = saved[i]
        if dcur is None:
            dz2, dz2b, dg, db, cs, ls = _ln_bwd(cur, sv['xh2'], sv['rs2'], row(norm2_g[i]), target=tgt,
                                                name=f"ln2_bwd_head_{i}")
            loss_part = ls
        else:
            dz2, dz2b, dg, db, cs = _ln_bwd(dcur, sv['xh2'], sv['rs2'], row(norm2_g[i]), name=f"ln2_bwd_{i}")
        sg['norm2_g'][i], sg['norm2_b'][i], sg['ffn_b_down'][i] = dg.sum(0), db.sum(0), cs.sum(0)
        Fh = F2 // 2
        wgrad('ffn_w_down', i, sv['f'], dz2b, tm=Fh // 2, tn=_tile(D, 1024, LANES), pieces=('row',))
        dh, csu, dwd, dbd = _ffn_bwd(dz2b, wts['ffn_w_down', i], 0, sv['hs'], sv['wdw3'], sv['bdw3'], S=S,
                                     name=f"ffn_bwd_{i}")
        sg['ffn_b_up'][i] = _perm_cols(csu.sum(0))
        sg['ffn_w_dw'][i] = _perm_cols(dwd.sum(1))
        sg['ffn_b_dw'][i] = _perm_cols(dbd.sum(0))
        wgrad('ffn_w_up', i, sv['x1b'], dh, tm=D // 2, tn=F2 // N_CHIPS, pieces=('col', True))
        dx1 = _mm(dh, wts['ffn_w_up', i], bl=0, tb=True, res=dz2, res_scale=alpha, tm=_tile(T, 512), tn=_tile(D, 1024, LANES),
                  tk=F2 // N_CHIPS, name=f"ffn_dx_{i}", deps=deps)
        dz1, dz1b, dg, db, cs = _ln_bwd(dx1, sv['xh1'], sv['rs1'], row(norm1_g[i]), name=f"ln1_bwd_{i}")
        sg['norm1_g'][i], sg['norm1_b'][i] = dg.sum(0), db.sum(0)
        if i % 2 == 0:
            sg['conv_b_out'][j] = cs.sum(0)
            wgrad('conv_w_out', j, sv['act'], dz1b, tm=_tile(C, 512), tn=_tile(D, 1024, LANES), pieces=('row',))
            ds = _mm(dz1b, wts['conv_w_out', j], bl=0, tb=True, tm=_tile(T, 512), tn=_tile(C, 1024, LANES), tk=_tile(D, 1024, LANES),
                     name=f"conv_ds_{j}", deps=deps)
            ddw, dg, db = _ln_silu_bwd(ds, sv['xhc'], sv['rsc'], row(full['conv_ln_g'][j]),
                                       row(full['conv_ln_b'][j]), name=f"conv_ln_bwd_{j}")
            sg['conv_ln_g'][j], sg['conv_ln_b'][j] = dg.sum(0), db.sum(0)
            dglu, dwk, dbk = _conv_bwd(ddw, sv['h1'], sv['wdw'], B=B, S=S, name=f"conv_dw_bwd_{j}")
            sg['conv_w_dw'][j] = dwk.sum(1)[:conv_w_dw.shape[1]]
            sg['conv_b_dw'][j] = dbk.sum(0)
            dh1, csi = _glu_bwd(dglu, sv['h1'], name=f"conv_glu_bwd_{j}")
            sg['conv_b_in'][j] = _perm_cols(csi.sum(0))
            fam = 'conv_w_in'
        else:
            sg['gmlp_b_out'][j] = cs.sum(0)
            wgrad('gmlp_w_out', j, sv['act'], dz1b, tm=_tile(C, 512), tn=_tile(D, 1024, LANES), pieces=('row',))
            dus = _mm(dz1b, wts['gmlp_w_out', j], bl=0, tb=True, tm=_tile(T, 512), tn=_tile(C, 1024, LANES),
                      tk=_tile(D, 1024, LANES), name=f"gmlp_dus_{j}", deps=deps)
            dh1, dg, db, csi, dws, dbs = _gmlp_gate_bwd(dus, sv['pre'], sv['xhv'], sv['rsv'], row(full['gmlp_ln_g'][j]),
                                                        row(full['gmlp_ln_b'][j]), gmlp_w_s[j], sv['bsb'],
                                                        name=f"gmlp_gate_bwd_{j}")
            sg['gmlp_ln_g'][j], sg['gmlp_ln_b'][j] = dg.sum(0), db.sum(0)
            sg['gmlp_b_in'][j] = _perm_cols(csi.sum(0))
            sg['gmlp_w_s'][j] = dws
            sg['gmlp_b_s'][j] = dbs.reshape(L, G, L).sum(-1).T
            fam = 'gmlp_w_in'
        wgrad(fam, j, sv['xb'], dh1, tm=D // 2, tn=(2 * C) // N_CHIPS, pieces=('col', True))
        dcur = _mm(dh1, wts[fam, j], bl=0, tb=True, res=dz1, res_scale=alpha, tm=_tile(T, 512),
                   tn=_tile(D, 1024, LANES), tk=_tile(2 * C, 1024, LANES), name=f"{fam}_dx_{j}", deps=deps)
    grad_x = dcur.reshape(B, S, D)

    small_names = [n for n in WEIGHTS if n not in BIG]
    small_full = [jnp.stack(sg[n]) for n in small_names]
    flat = _pack(small_full + [loss_part])
    red = _allreduce_flat(flat, name="ar_small")
    red_parts = _unpack(red, [a.shape for a in small_full] + [loss_part.shape])
    loss = (0.5 / D) * jnp.sum(red_parts[-1])
    grads = {}
    for n, g in zip(small_names, red_parts[:-1]):
        if n in SMALL_SHARDED:
            ax = SMALL_SHARDED[n]
            width = P[n].shape[ax]
            g = lax.dynamic_slice_in_dim(g, shard * width, width, axis=ax)
        grads[n] = g

    big_out = {}
    for n in ['ffn_w_down', 'ffn_w_up', 'gmlp_w_out', 'gmlp_w_in', 'conv_w_out', 'conv_w_in']:
        own = None
        n_layers = len(inflight[n])
        for l in reversed(range(n_layers)):
            send, recv, g_thru, land = inflight[n][l]
            pc_, r = _rs_wait(send, recv, g_thru, land, dcur, name=f"rs_wait_{n}_{l}")
            own = _sum_pieces(pc_, r, me_id, l, own, n_layers, name=f"sum_{n}_{l}")
        got = _pair_exchange(own, name=f"px_{n}")
        big_out[n] = _adam_halves(P[n], own, got, P['m_' + n], P['v_' + n], core_id, name=f"adam_{n}")

    shapes = [P[n].shape for n in small_names]
    n_small = sum(functools.reduce(lambda p_, d_: p_ * d_, s_, 1) for s_ in shapes)
    unit = SUBLANES * LANES
    npad = -(-n_small // unit) * unit

    def flat2d(arrs, fill=0.0):
        v = _pack(arrs)
        return jnp.pad(v, (0, npad - n_small), constant_values=fill).reshape(-1, LANES)

    dl, mo, vo = _adam(flat2d([P[n] for n in small_names]), flat2d([grads[n] for n in small_names]),
                       flat2d([P['m_' + n] for n in small_names]),
                       flat2d([P['v_' + n] for n in small_names], fill=1.0), name="adam_small")
    small_out = {n: [grads[n], None, None, None] for n in small_names}
    for k, t in enumerate((dl, mo, vo)):
        for n, a in zip(small_names, _unpack(t.reshape(-1), shapes)):
            small_out[n][k + 1] = a

    outs = [loss, grad_x]
    for k in range(4):
        for n in WEIGHTS:
            outs.append(big_out[n][k] if n in BIG else small_out[n][k])
    return tuple(outs)
```

```python
import functools

import jax
import jax.numpy as jnp
from jax import lax
from jax.experimental import pallas as pl
from jax.experimental.pallas import tpu as pltpu

F32 = jnp.float32
_MXU = jnp.bfloat16
_WIRE = jnp.bfloat16
_HDT = jnp.bfloat16
LN_EPS = 1e-5
ADAM_LR, ADAM_B1, ADAM_B2, ADAM_EPS, ADAM_WD, ADAM_STEP = 0.001, 0.9, 0.999, 1e-08, 0.01, 10
N_CHIPS = 4
N_DEV = 8
LANES = 128
SUBLANES = 8
CONV_TAPS_PAD = 32
VMEM_LIMIT = 56 << 20
MESH = pl.DeviceIdType.MESH
ANY = pl.BlockSpec(memory_space=pl.ANY)
HBM = pl.BlockSpec(memory_space=pltpu.HBM)
SEMS = pl.BlockSpec(memory_space=pltpu.SEMAPHORE)
EFFECT = pltpu.SideEffectType.DATAFLOW_SIDE_EFFECTING
PERM = (0, 2, 1, 3)


def _cp(sem=None):
    return pltpu.CompilerParams(dimension_semantics=sem, vmem_limit_bytes=VMEM_LIMIT)


def _tile(dim, pref, mult=SUBLANES):
    if dim <= pref:
        return dim
    t = (pref // mult) * mult
    while t > mult and dim % t:
        t -= mult
    assert dim % t == 0, (dim, pref, mult)
    return t


def _perm_idx(q):
    return (q % 2) * 2 + q // 2


def _fold8(t):
    r, n = t.shape
    return t.reshape(r // SUBLANES, SUBLANES, n).sum(axis=0)


def _ln_rows(z, g, b):
    mu = jnp.mean(z, axis=-1, keepdims=True)
    xc = z - mu
    var = jnp.mean(xc * xc, axis=-1, keepdims=True)
    rstd = lax.rsqrt(var + LN_EPS)
    xh = xc * rstd
    return xh * g + b, xh, rstd


def _ln_bwd_rows(dy, xh, rstd, g):
    dxh = dy * g
    m1 = jnp.mean(dxh, axis=-1, keepdims=True)
    m2 = jnp.mean(dxh * xh, axis=-1, keepdims=True)
    return rstd * (dxh - m1 - xh * m2)


def _sigmoid(v):
    return 1.0 / (1.0 + jnp.exp(-v))


def _gelu_parts(p):
    cdf = 0.5 * (1.0 + lax.erf(p * 0.7071067811865476))
    pdf = jnp.exp(-0.5 * p * p) * 0.3989422804014327
    return p * cdf, cdf + p * pdf


def _shift_down(prev8, t, s):
    ext = jnp.concatenate([prev8, t], axis=0)
    return pltpu.roll(ext, s, 0)[SUBLANES:]


def _shift_up(t, next8, s):
    n = t.shape[0]
    ext = jnp.concatenate([t, next8], axis=0)
    return pltpu.roll(ext, n + SUBLANES - s, 0)[:n]


def _mm(a, b, *, ta=False, tb=False, bl=None, bias=None, res=None, res_scale=1.0, out_dtype=F32,
        tm, tn, tk, name, pieces=None, deps=None):
    M, K = (a.shape[1], a.shape[0]) if ta else a.shape
    bs = b.shape[1:] if bl is not None else b.shape
    N, Kb = (bs[0], bs[1]) if tb else (bs[1], bs[0])
    assert K == Kb and M % tm == 0 and N % tn == 0 and K % tk == 0, (a.shape, b.shape, tm, tn, tk)
    gm, gn, gk = M // tm, N // tn, K // tk
    a_spec = pl.BlockSpec((tk, tm), lambda i, j, k: (k, i)) if ta else pl.BlockSpec((tm, tk), lambda i, j, k: (i, k))
    bblk = (tn, tk) if tb else (tk, tn)
    bmap = (lambda i, j, k: (j, k)) if tb else (lambda i, j, k: (k, j))
    if bl is not None:
        b_spec = pl.BlockSpec((None,) + bblk, lambda i, j, k: (bl,) + bmap(i, j, k))
    else:
        b_spec = pl.BlockSpec(bblk, bmap)
    in_specs, operands = [a_spec, b_spec], [a, b]
    if bias is not None:
        in_specs.append(pl.BlockSpec((1, tn), lambda i, j, k: (0, j)))
        operands.append(bias)
    if res is not None:
        in_specs.append(pl.BlockSpec((tm, tn), lambda i, j, k: (i, j)))
        operands.append(res)
    n_dep = len(deps) if deps else 0
    if n_dep:
        in_specs += [ANY] * n_dep
        operands += deps
        del deps[:]
    if pieces is None:
        out_shape = jax.ShapeDtypeStruct((M, N), out_dtype)
        out_spec = pl.BlockSpec((tm, tn), lambda i, j, k: (i, j))
        ppb = pr = None
    elif pieces[0] == 'col':
        pr, pc = M // 2, N // N_CHIPS
        assert tm == pr and pc % tn == 0
        ppb, per = 1, pc // tn
        perm = pieces[1]
        out_shape = jax.ShapeDtypeStruct((N_DEV, pr, pc), out_dtype)
        out_spec = pl.BlockSpec(
            (1, pr, tn), lambda i, j, k: (2 * (_perm_idx(j // per) if perm else j // per) + i, 0, j % per))
    else:
        pr = M // N_DEV
        assert tm % pr == 0
        ppb = tm // pr
        out_shape = jax.ShapeDtypeStruct((N_DEV, pr, N), out_dtype)
        out_spec = pl.BlockSpec((ppb, pr, tn), lambda i, j, k: (i, 0, j))
    dims = (((0 if ta else 1,), (1 if tb else 0,)), ((), ()))

    def body(*refs):
        a_ref, b_ref = refs[0], refs[1]
        pos = 2
        bias_ref = res_ref = None
        if bias is not None:
            bias_ref = refs[pos]
            pos += 1
        if res is not None:
            res_ref = refs[pos]
            pos += 1
        pos += n_dep
        o_ref, acc_ref = refs[pos], refs[pos + 1]
        k = pl.program_id(2)

        @pl.when(k == 0)
        def _():
            acc_ref[...] = jnp.zeros_like(acc_ref)

        acc_ref[...] += lax.dot_general(a_ref[...].astype(_MXU), b_ref[...].astype(_MXU), dims,
                                        preferred_element_type=F32)

        @pl.when(k == gk - 1)
        def _():
            r = acc_ref[...]
            if bias_ref is not None:
                r = r + bias_ref[...]
            if res_ref is not None:
                r = r + res_scale * res_ref[...]
            if pieces is not None:
                r = r.reshape(ppb, pr, tn)
            o_ref[...] = r.astype(out_dtype)

    return pl.pallas_call(
        body, name=name, grid=(gm, gn, gk), in_specs=in_specs, out_specs=out_spec, out_shape=out_shape,
        scratch_shapes=[pltpu.VMEM((tm, tn), F32)],
        compiler_params=_cp(("parallel", "parallel", "arbitrary")),
    )(*operands)


def _mm_res_ln(a, w, wl, bias, res, alpha, g, b, *, name):
    T, K = a.shape
    D = w.shape[-1]
    tm = _tile(T, 256)

    def body(a_ref, w_ref, bias_ref, res_ref, g_ref, b_ref, y_ref, yb_ref, xh_ref, rs_ref):
        z = jnp.dot(a_ref[...].astype(_MXU), w_ref[...].astype(_MXU), preferred_element_type=F32)
        z = z + bias_ref[...] + alpha * res_ref[...]
        y, xh, rstd = _ln_rows(z, g_ref[...], b_ref[...])
        y_ref[...] = y
        yb_ref[...] = y.astype(_MXU)
        xh_ref[...] = xh
        rs_ref[...] = rstd

    row = lambda i: (i, 0)
    vec = pl.BlockSpec((1, D), lambda i: (0, 0))
    return pl.pallas_call(
        body, name=name, grid=(T // tm,),
        in_specs=[pl.BlockSpec((tm, K), row), pl.BlockSpec((None, K, D), lambda i: (wl, 0, 0)), vec,
                  pl.BlockSpec((tm, D), row), vec, vec],
        out_specs=[pl.BlockSpec((tm, D), row), pl.BlockSpec((tm, D), row), pl.BlockSpec((tm, D), row),
                   pl.BlockSpec((tm, 1), row)],
        out_shape=[jax.ShapeDtypeStruct((T, D), F32), jax.ShapeDtypeStruct((T, D), _MXU),
                   jax.ShapeDtypeStruct((T, D), F32), jax.ShapeDtypeStruct((T, 1), F32)],
        compiler_params=_cp(("parallel",)),
    )(a, w, bias, res, g, b)


def _ln_bwd(dy, xh, rstd, g, *, name, target=None):
    T, D = dy.shape
    tm = _tile(T, 256)
    head = target is not None

    def body(*refs):
        if head:
            dy_ref, t_ref, xh_ref, rs_ref, g_ref, dz_ref, dzb_ref, dg_ref, db_ref, cs_ref, ls_ref = refs
        else:
            dy_ref, xh_ref, rs_ref, g_ref, dz_ref, dzb_ref, dg_ref, db_ref, cs_ref = refs
        i = pl.program_id(0)

        @pl.when(i == 0)
        def _():
            dg_ref[...] = jnp.zeros_like(dg_ref)
            db_ref[...] = jnp.zeros_like(db_ref)
            cs_ref[...] = jnp.zeros_like(cs_ref)
            if head:
                ls_ref[...] = jnp.zeros_like(ls_ref)

        d = dy_ref[...]
        if head:
            err = d - t_ref[...]
            ls_ref[...] += _fold8(err * err)
            d = err * (1.0 / D)
        xh = xh_ref[...]
        dz = _ln_bwd_rows(d, xh, rs_ref[...], g_ref[...])
        dz_ref[...] = dz
        dzb_ref[...] = dz.astype(_MXU)
        dg_ref[...] += _fold8(d * xh)
        db_ref[...] += _fold8(d)
        cs_ref[...] += _fold8(dz)

    row = lambda i: (i, 0)
    fixed = lambda i: (0, 0)
    tile = pl.BlockSpec((tm, D), row)
    part = pl.BlockSpec((SUBLANES, D), fixed)
    in_specs = [tile] + ([tile] if head else []) + [tile, pl.BlockSpec((tm, 1), row), pl.BlockSpec((1, D), fixed)]
    n_part = 4 if head else 3
    operands = [dy] + ([target] if head else []) + [xh, rstd, g]
    return pl.pallas_call(
        body, name=name, grid=(T // tm,), in_specs=in_specs,
        out_specs=[tile, tile] + [part] * n_part,
        out_shape=[jax.ShapeDtypeStruct((T, D), F32), jax.ShapeDtypeStruct((T, D), _MXU)]
        + [jax.ShapeDtypeStruct((SUBLANES, D), F32)] * n_part,
        compiler_params=_cp(("arbitrary",)),
    )(*operands)


def _conv_cols(C, tc):
    per = (C // 2) // tc
    return per, (lambda j: (j // per) * (2 * per) + j % per)


def _glu_shifted(a_ref, g_ref, p_ref, S):
    u = a_ref[...] * _sigmoid(g_ref[...])
    rows = lax.broadcasted_iota(jnp.int32, u.shape, 0)
    for r in range(SUBLANES):
        p_ref[r, 0:CONV_TAPS_PAD, :] = jnp.zeros((CONV_TAPS_PAD, u.shape[1]), F32)
        p_ref[r, CONV_TAPS_PAD:CONV_TAPS_PAD + S, :] = u if r == 0 else jnp.where(rows >= r, pltpu.roll(u, r, 0), 0.0)


def _conv_fwd(h1, w_dw, b_dw, *, B, S, name):
    C = w_dw.shape[1]
    taps = CONV_TAPS_PAD - 1
    tc = LANES
    ch = _tile(S, 128)
    per, col_a = _conv_cols(C, tc)

    def body(a_ref, g_ref, w_ref, b_ref, o_ref, p_ref):
        _glu_shifted(a_ref, g_ref, p_ref, S)

        def chunk(ci, carry):
            base = pl.multiple_of(ci * ch, ch)
            acc = jnp.zeros((ch, tc), F32) + b_ref[...]
            for k in range(taps):
                q, r = divmod(taps - 1 - k, SUBLANES)
                start = pl.multiple_of(base + (CONV_TAPS_PAD - SUBLANES * q), SUBLANES)
                acc = acc + w_ref[pl.ds(k, 1), :] * p_ref[r, pl.ds(start, ch), :]
            o_ref[pl.ds(base, ch), :] = acc
            return carry

        lax.fori_loop(0, S // ch, chunk, 0)

    return pl.pallas_call(
        body, name=name, grid=(B, C // tc),
        in_specs=[pl.BlockSpec((S, tc), lambda b, j: (b, col_a(j))),
                  pl.BlockSpec((S, tc), lambda b, j: (b, col_a(j) + per)),
                  pl.BlockSpec((CONV_TAPS_PAD, tc), lambda b, j: (0, j)),
                  pl.BlockSpec((1, tc), lambda b, j: (0, j))],
        out_specs=pl.BlockSpec((S, tc), lambda b, j: (b, j)),
        out_shape=jax.ShapeDtypeStruct((B * S, C), F32),
        scratch_shapes=[pltpu.VMEM((SUBLANES, S + CONV_TAPS_PAD, tc), F32)],
        compiler_params=_cp(("parallel", "parallel")),
    )(h1, h1, w_dw, b_dw)


def _conv_bwd(dd, h1, w_dw, *, B, S, name):
    C = w_dw.shape[1]
    taps = CONV_TAPS_PAD - 1
    tc = LANES
    ch = _tile(S, 128)
    per, col_a = _conv_cols(C, tc)

    def body(d_ref, a_ref, g_ref, w_ref, du_ref, dw_ref, db_ref, p_ref, q_ref):
        b = pl.program_id(1)

        @pl.when(b == 0)
        def _():
            dw_ref[...] = jnp.zeros_like(dw_ref)
            db_ref[...] = jnp.zeros_like(db_ref)

        _glu_shifted(a_ref, g_ref, p_ref, S)
        d = d_ref[...]
        rows = lax.broadcasted_iota(jnp.int32, d.shape, 0)
        for r in range(SUBLANES):
            q_ref[r, S:S + CONV_TAPS_PAD, :] = jnp.zeros((CONV_TAPS_PAD, tc), F32)
            q_ref[r, 0:S, :] = d if r == 0 else jnp.where(rows < S - r, pltpu.roll(d, S - r, 0), 0.0)
        db_ref[...] += _fold8(d)

        def chunk(ci, carry):
            base = pl.multiple_of(ci * ch, ch)
            dch = d_ref[pl.ds(base, ch), :]
            acc = jnp.zeros((ch, tc), F32)
            for k in range(taps):
                q, r = divmod(taps - 1 - k, SUBLANES)
                up = pl.multiple_of(base + SUBLANES * q, SUBLANES)
                acc = acc + w_ref[pl.ds(k, 1), :] * q_ref[r, pl.ds(up, ch), :]
                down = pl.multiple_of(base + (CONV_TAPS_PAD - SUBLANES * q), SUBLANES)
                dw_ref[k] += _fold8(dch * p_ref[r, pl.ds(down, ch), :])
            du_ref[pl.ds(base, ch), :] = acc
            return carry

        lax.fori_loop(0, S // ch, chunk, 0)

    return pl.pallas_call(
        body, name=name, grid=(C // tc, B),
        in_specs=[pl.BlockSpec((S, tc), lambda j, b: (b, j)),
                  pl.BlockSpec((S, tc), lambda j, b: (b, col_a(j))),
                  pl.BlockSpec((S, tc), lambda j, b: (b, col_a(j) + per)),
                  pl.BlockSpec((CONV_TAPS_PAD, tc), lambda j, b: (0, j))],
        out_specs=[pl.BlockSpec((S, tc), lambda j, b: (b, j)),
                   pl.BlockSpec((CONV_TAPS_PAD, SUBLANES, tc), lambda j, b: (0, 0, j)),
                   pl.BlockSpec((SUBLANES, tc), lambda j, b: (0, j))],
        out_shape=[jax.ShapeDtypeStruct((B * S, C), F32),
                   jax.ShapeDtypeStruct((CONV_TAPS_PAD, SUBLANES, C), F32),
                   jax.ShapeDtypeStruct((SUBLANES, C), F32)],
        scratch_shapes=[pltpu.VMEM((SUBLANES, S + CONV_TAPS_PAD, tc), F32),
                        pltpu.VMEM((SUBLANES, S + CONV_TAPS_PAD, tc), F32)],
        compiler_params=_cp(("parallel", "arbitrary")),
    )(dd, h1, h1, w_dw)


def _ln_silu_fwd(v, g, b, *, name):
    T, C = v.shape
    tm = _tile(T, 512)

    def body(v_ref, g_ref, b_ref, s_ref, xh_ref, rs_ref):
        y, xh, rstd = _ln_rows(v_ref[...], g_ref[...], b_ref[...])
        s_ref[...] = (y * _sigmoid(y)).astype(_MXU)
        xh_ref[...] = xh
        rs_ref[...] = rstd

    row = lambda i: (i, 0)
    vec = pl.BlockSpec((1, C), lambda i: (0, 0))
    return pl.pallas_call(
        body, name=name, grid=(T // tm,),
        in_specs=[pl.BlockSpec((tm, C), row), vec, vec],
        out_specs=[pl.BlockSpec((tm, C), row), pl.BlockSpec((tm, C), row), pl.BlockSpec((tm, 1), row)],
        out_shape=[jax.ShapeDtypeStruct((T, C), _MXU), jax.ShapeDtypeStruct((T, C), F32),
                   jax.ShapeDtypeStruct((T, 1), F32)],
        compiler_params=_cp(("parallel",)),
    )(v, g, b)


def _ln_silu_bwd(ds, xh, rstd, g, b, *, name):
    T, C = ds.shape
    tm = _tile(T, 256)

    def body(ds_ref, xh_ref, rs_ref, g_ref, b_ref, dv_ref, dg_ref, db_ref):
        @pl.when(pl.program_id(0) == 0)
        def _():
            dg_ref[...] = jnp.zeros_like(dg_ref)
            db_ref[...] = jnp.zeros_like(db_ref)

        xh = xh_ref[...]
        gam = g_ref[...]
        y = xh * gam + b_ref[...]
        sig = _sigmoid(y)
        dln = ds_ref[...] * (sig * (1.0 + y * (1.0 - sig)))
        dv_ref[...] = _ln_bwd_rows(dln, xh, rs_ref[...], gam)
        dg_ref[...] += _fold8(dln * xh)
        db_ref[...] += _fold8(dln)

    row = lambda i: (i, 0)
    fixed = lambda i: (0, 0)
    vec = pl.BlockSpec((1, C), fixed)
    part = pl.BlockSpec((SUBLANES, C), fixed)
    return pl.pallas_call(
        body, name=name, grid=(T // tm,),
        in_specs=[pl.BlockSpec((tm, C), row), pl.BlockSpec((tm, C), row), pl.BlockSpec((tm, 1), row), vec, vec],
        out_specs=[pl.BlockSpec((tm, C), row), part, part],
        out_shape=[jax.ShapeDtypeStruct((T, C), F32)] + [jax.ShapeDtypeStruct((SUBLANES, C), F32)] * 2,
        compiler_params=_cp(("arbitrary",)),
    )(ds, xh, rstd, g, b)


def _glu_bwd(du, h1, *, name):
    T, C = du.shape
    il = C // 2
    tm = _tile(T, 256)

    def body(du_ref, h_ref, dh_ref, cs_ref):
        @pl.when(pl.program_id(0) == 0)
        def _():
            cs_ref[...] = jnp.zeros_like(cs_ref)

        for hb in range(2):
            a = h_ref[:, 2 * hb * il:(2 * hb + 1) * il]
            gate = h_ref[:, (2 * hb + 1) * il:(2 * hb + 2) * il]
            d = du_ref[:, hb * il:(hb + 1) * il]
            sig = _sigmoid(gate)
            da = d * sig
            dgate = d * a * sig * (1.0 - sig)
            dh_ref[:, 2 * hb * il:(2 * hb + 1) * il] = da.astype(_MXU)
            dh_ref[:, (2 * hb + 1) * il:(2 * hb + 2) * il] = dgate.astype(_MXU)
            cs_ref[:, 2 * hb * il:(2 * hb + 1) * il] += _fold8(da)
            cs_ref[:, (2 * hb + 1) * il:(2 * hb + 2) * il] += _fold8(dgate)

    row = lambda i: (i, 0)
    return pl.pallas_call(
        body, name=name, grid=(T // tm,),
        in_specs=[pl.BlockSpec((tm, C), row), pl.BlockSpec((tm, 2 * C), row)],
        out_specs=[pl.BlockSpec((tm, 2 * C), row), pl.BlockSpec((SUBLANES, 2 * C), lambda i: (0, 0))],
        out_shape=[jax.ShapeDtypeStruct((T, 2 * C), _MXU), jax.ShapeDtypeStruct((SUBLANES, 2 * C), F32)],
        compiler_params=_cp(("arbitrary",)),
    )(du, h1)


def _tril_mask(n):
    return lax.broadcasted_iota(jnp.int32, (n, n), 0) >= lax.broadcasted_iota(jnp.int32, (n, n), 1)


def _split_uv(t, il):
    u = jnp.concatenate([t[:, 0:il], t[:, 2 * il:3 * il]], axis=1)
    v = jnp.concatenate([t[:, il:2 * il], t[:, 3 * il:4 * il]], axis=1)
    return u, v


def _gmlp_gate_fwd(p, g, b, w_s, bsb, *, name):
    T, C2 = p.shape
    C = C2 // 2
    il = C // 2
    G, L, _ = w_s.shape
    assert G * L == C
    tm = _tile(T, 2 * L, L)

    def body(p_ref, g_ref, b_ref, ws_ref, bs_ref, us_ref, xh_ref, rs_ref, vn_ref, u_ref):
        z, _ = _gelu_parts(p_ref[...])
        u, v = _split_uv(z, il)
        vn, xh, rstd = _ln_rows(v, g_ref[...], b_ref[...])
        xh_ref[...] = xh
        rs_ref[...] = rstd
        vn_ref[...] = vn.astype(_MXU)
        u_ref[...] = u
        mask = _tril_mask(L)
        for gi in range(G):
            wc = jnp.where(mask, ws_ref[gi], 0.0).astype(_MXU)
            cols = slice(gi * L, (gi + 1) * L)
            for c in range(tm // L):
                rows = slice(c * L, (c + 1) * L)
                s = jnp.dot(wc, vn_ref[rows, cols], preferred_element_type=F32) + bs_ref[:, cols]
                us_ref[rows, cols] = (u_ref[rows, cols] * s).astype(_MXU)

    row = lambda i: (i, 0)
    fixed = lambda i: (0, 0)
    return pl.pallas_call(
        body, name=name, grid=(T // tm,),
        in_specs=[pl.BlockSpec((tm, C2), row), pl.BlockSpec((1, C), fixed), pl.BlockSpec((1, C), fixed),
                  pl.BlockSpec((G, L, L), lambda i: (0, 0, 0)), pl.BlockSpec((L, C), fixed)],
        out_specs=[pl.BlockSpec((tm, C), row), pl.BlockSpec((tm, C), row), pl.BlockSpec((tm, 1), row)],
        out_shape=[jax.ShapeDtypeStruct((T, C), _MXU), jax.ShapeDtypeStruct((T, C), F32),
                   jax.ShapeDtypeStruct((T, 1), F32)],
        scratch_shapes=[pltpu.VMEM((tm, C), _MXU), pltpu.VMEM((tm, C), F32)],
        compiler_params=_cp(("parallel",)),
    )(p, g, b, w_s, bsb)


def _gmlp_gate_bwd(dus, p, xh, rstd, g, b, w_s, bsb, *, name):
    T, C2 = p.shape
    C = C2 // 2
    il = C // 2
    G, L, _ = w_s.shape
    tm = _tile(T, 2 * L, L)

    def body(dus_ref, p_ref, xh_ref, rs_ref, g_ref, b_ref, ws_ref, bs_ref,
             dp_ref, dg_ref, db_ref, cs_ref, dws_ref, dbs_ref, vn_ref, u_ref, dvn_ref, du_ref):
        @pl.when(pl.program_id(0) == 0)
        def _():
            dg_ref[...] = jnp.zeros_like(dg_ref)
            db_ref[...] = jnp.zeros_like(db_ref)
            cs_ref[...] = jnp.zeros_like(cs_ref)
            dws_ref[...] = jnp.zeros_like(dws_ref)
            dbs_ref[...] = jnp.zeros_like(dbs_ref)

        z, gp = _gelu_parts(p_ref[...])
        u, _ = _split_uv(z, il)
        xh = xh_ref[...]
        gam = g_ref[...]
        vn_ref[...] = (xh * gam + b_ref[...]).astype(_MXU)
        u_ref[...] = u
        mask = _tril_mask(L)
        for gi in range(G):
            wc = jnp.where(mask, ws_ref[gi], 0.0).astype(_MXU)
            cols = slice(gi * L, (gi + 1) * L)
            for c in range(tm // L):
                rows = slice(c * L, (c + 1) * L)
                vnb = vn_ref[rows, cols]
                s = jnp.dot(wc, vnb, preferred_element_type=F32) + bs_ref[:, cols]
                d = dus_ref[rows, cols]
                du_ref[rows, cols] = d * s
                ds = d * u_ref[rows, cols]
                dbs_ref[:, cols] += ds
                dsb = ds.astype(_MXU)
                dw = lax.dot_general(dsb, vnb, (((1,), (1,)), ((), ())), preferred_element_type=F32)
                dws_ref[gi] += jnp.where(mask, dw, 0.0)
                dvn_ref[rows, cols] = lax.dot_general(wc, dsb, (((0,), (0,)), ((), ())), preferred_element_type=F32)
        dvn = dvn_ref[...]
        dg_ref[...] += _fold8(dvn * xh)
        db_ref[...] += _fold8(dvn)
        dv = _ln_bwd_rows(dvn, xh, rs_ref[...], gam)
        du = du_ref[...]
        for hb in range(2):
            for part, src in ((0, du), (1, dv)):
                lo = (2 * hb + part) * il
                dp = src[:, hb * il:(hb + 1) * il] * gp[:, lo:lo + il]
                dp_ref[:, lo:lo + il] = dp.astype(_MXU)
                cs_ref[:, lo:lo + il] += _fold8(dp)

    row = lambda i: (i, 0)
    fixed = lambda i: (0, 0)
    part_c = pl.BlockSpec((SUBLANES, C), fixed)
    return pl.pallas_call(
        body, name=name, grid=(T // tm,),
        in_specs=[pl.BlockSpec((tm, C), row), pl.BlockSpec((tm, C2), row), pl.BlockSpec((tm, C), row),
                  pl.BlockSpec((tm, 1), row), pl.BlockSpec((1, C), fixed), pl.BlockSpec((1, C), fixed),
                  pl.BlockSpec((G, L, L), lambda i: (0, 0, 0)), pl.BlockSpec((L, C), fixed)],
        out_specs=[pl.BlockSpec((tm, C2), row), part_c, part_c, pl.BlockSpec((SUBLANES, C2), fixed),
                   pl.BlockSpec((G, L, L), lambda i: (0, 0, 0)), pl.BlockSpec((L, C), fixed)],
        out_shape=[jax.ShapeDtypeStruct((T, C2), _MXU), jax.ShapeDtypeStruct((SUBLANES, C), F32),
                   jax.ShapeDtypeStruct((SUBLANES, C), F32), jax.ShapeDtypeStruct((SUBLANES, C2), F32),
                   jax.ShapeDtypeStruct((G, L, L), F32), jax.ShapeDtypeStruct((L, C), F32)],
        scratch_shapes=[pltpu.VMEM((tm, C), _MXU), pltpu.VMEM((tm, C), F32), pltpu.VMEM((tm, C), F32),
                        pltpu.VMEM((tm, C), F32)],
        compiler_params=_cp(("arbitrary",)),
    )(dus, p, xh, rstd, g, b, w_s, bsb)


def _ffn_conv(h, prev8, w_ref, b_ref):
    h1 = _shift_down(prev8, h, 1)
    h2 = _shift_down(prev8, h, 2)
    hc = w_ref[pl.ds(2, 1), :] * h + w_ref[pl.ds(1, 1), :] * h1 + w_ref[pl.ds(0, 1), :] * h2 + b_ref[...]
    return hc, h1, h2


def _ffn_up_fwd(xb, w, wl, b_up, w_dw, b_dw, *, S, name):
    T, D = xb.shape
    N = w.shape[-1]
    tn = N // N_CHIPS
    tm = _tile(S, 256)
    spt = S // tm

    def body(x_ref, w_ref, bu_ref, wd_ref, bd_ref, h_ref, f_ref, carry_ref):
        i = pl.program_id(1)

        @pl.when(i % spt == 0)
        def _():
            carry_ref[...] = jnp.zeros_like(carry_ref)

        h = jnp.dot(x_ref[...].astype(_MXU), w_ref[...].astype(_MXU), preferred_element_type=F32) + bu_ref[...]
        hq = h.astype(_HDT)
        h_ref[...] = hq
        h = hq.astype(F32)
        hc, _, _ = _ffn_conv(h, carry_ref[...], wd_ref, bd_ref)
        carry_ref[...] = h[tm - SUBLANES:tm]
        gte = hc[:, :tn]
        f_ref[...] = (gte * _sigmoid(gte) * hc[:, tn:]).astype(_MXU)

    pair = lambda j, i: (0, j)
    return pl.pallas_call(
        body, name=name, grid=(2, T // tm),
        in_specs=[pl.BlockSpec((tm, D), lambda j, i: (i, 0)),
                  pl.BlockSpec((None, D, 2 * tn), lambda j, i: (wl, 0, j)),
                  pl.BlockSpec((1, 2 * tn), pair), pl.BlockSpec((SUBLANES, 2 * tn), pair),
                  pl.BlockSpec((1, 2 * tn), pair)],
        out_specs=[pl.BlockSpec((tm, 2 * tn), lambda j, i: (i, j)), pl.BlockSpec((tm, tn), lambda j, i: (i, j))],
        out_shape=[jax.ShapeDtypeStruct((T, N), _HDT), jax.ShapeDtypeStruct((T, N // 2), _MXU)],
        scratch_shapes=[pltpu.VMEM((SUBLANES, 2 * tn), F32)],
        compiler_params=_cp(("parallel", "arbitrary")),
    )(xb, w, b_up, w_dw, b_dw)


def _ffn_bwd(dzb, w_down, wl, hs, w_dw, b_dw, *, S, name):
    T, D = dzb.shape
    N = hs.shape[1]
    tn = N // N_CHIPS
    tm = _tile(S, 256)
    spt = S // tm
    nt = T // tm
    hal = 16

    def body(dz_ref, wd_ref, h_ref, halo_ref, wc_ref, bc_ref, dh_ref, cs_ref, dw_ref, db_ref, carry_ref):
        i = pl.program_id(1)
        ii = nt - 1 - i

        @pl.when(i == 0)
        def _():
            cs_ref[...] = jnp.zeros_like(cs_ref)
            dw_ref[...] = jnp.zeros_like(dw_ref)
            db_ref[...] = jnp.zeros_like(db_ref)

        df = lax.dot_general(dz_ref[...].astype(_MXU), wd_ref[...].astype(_MXU), (((1,), (1,)), ((), ())),
                             preferred_element_type=F32)
        h = h_ref[...].astype(F32)
        prev8 = halo_ref[...].astype(F32)[hal - SUBLANES:hal]
        prev8 = jnp.where(ii % spt == 0, 0.0, prev8)
        hc, h1, h2 = _ffn_conv(h, prev8, wc_ref, bc_ref)
        gte, val = hc[:, :tn], hc[:, tn:]
        sig = _sigmoid(gte)
        dval = df * (gte * sig)
        dg = df * val * (sig * (1.0 + gte * (1.0 - sig)))
        dhc = jnp.concatenate([dg, dval], axis=1)
        db_ref[...] += _fold8(dhc)
        dw_ref[2] += _fold8(dhc * h)
        dw_ref[1] += _fold8(dhc * h1)
        dw_ref[0] += _fold8(dhc * h2)
        nxt = jnp.where((ii + 1) % spt == 0, 0.0, carry_ref[...])
        dh = (wc_ref[pl.ds(2, 1), :] * dhc + wc_ref[pl.ds(1, 1), :] * _shift_up(dhc, nxt, 1)
              + wc_ref[pl.ds(0, 1), :] * _shift_up(dhc, nxt, 2))
        carry_ref[...] = dhc[0:SUBLANES]
        cs_ref[...] += _fold8(dh)
        dh_ref[...] = dh.astype(_MXU)

    pair = lambda j, i: (0, j)
    rev = lambda j, i: (nt - 1 - i, j)
    return pl.pallas_call(
        body, name=name, grid=(2, nt),
        in_specs=[pl.BlockSpec((tm, D), lambda j, i: (nt - 1 - i, 0)),
                  pl.BlockSpec((None, tn, D), lambda j, i: (wl, j, 0)),
                  pl.BlockSpec((tm, 2 * tn), rev),
                  pl.BlockSpec((hal, 2 * tn), lambda j, i: (jnp.maximum((nt - 1 - i) * (tm // hal) - 1, 0), j)),
                  pl.BlockSpec((SUBLANES, 2 * tn), pair), pl.BlockSpec((1, 2 * tn), pair)],
        out_specs=[pl.BlockSpec((tm, 2 * tn), rev), pl.BlockSpec((SUBLANES, 2 * tn), pair),
                   pl.BlockSpec((3, SUBLANES, 2 * tn), lambda j, i: (0, 0, j)),
                   pl.BlockSpec((SUBLANES, 2 * tn), pair)],
        out_shape=[jax.ShapeDtypeStruct((T, N), _MXU), jax.ShapeDtypeStruct((SUBLANES, N), F32),
                   jax.ShapeDtypeStruct((3, SUBLANES, N), F32), jax.ShapeDtypeStruct((SUBLANES, N), F32)],
        scratch_shapes=[pltpu.VMEM((SUBLANES, 2 * tn), F32)],
        compiler_params=_cp(("parallel", "arbitrary")),
    )(dzb, w_down, hs, hs, w_dw, b_dw)


def _sum_pieces(g, r, me, layer, acc, n_layers, *, name):
    _, pr, pc = g.shape
    tr = _tile(pr, 128)

    def body(me_ref, g_ref, r_ref, *rest):
        o_ref = rest[-1]
        total = g_ref[...].astype(F32)
        for s in range(N_DEV - 1):
            total = total + r_ref[s].astype(F32)
        o_ref[...] = total

    in_specs = [pl.BlockSpec((None, tr, pc), lambda i, me_ref: (me_ref[0], i, 0)),
                pl.BlockSpec((N_DEV - 1, tr, pc), lambda i, me_ref: (0, i, 0))]
    operands = [me, g, r]
    aliases = {}
    if acc is not None:
        in_specs.append(ANY)
        operands.append(acc)
        aliases = {3: 0}
    return pl.pallas_call(
        body, name=name,
        grid_spec=pltpu.PrefetchScalarGridSpec(
            num_scalar_prefetch=1, grid=(pr // tr,), in_specs=in_specs,
            out_specs=pl.BlockSpec((None, tr, pc), lambda i, me_ref: (layer, i, 0))),
        out_shape=jax.ShapeDtypeStruct((n_layers, pr, pc), F32),
        input_output_aliases=aliases,
        compiler_params=_cp(("parallel",)),
    )(*operands)


def _adam_math(w, g, m, v):
    bc1 = 1.0 - ADAM_B1 ** ADAM_STEP
    bc2 = 1.0 - ADAM_B2 ** ADAM_STEP
    m = ADAM_B1 * m + (1.0 - ADAM_B1) * g
    v = ADAM_B2 * v + (1.0 - ADAM_B2) * (g * g)
    return -ADAM_LR * ((m / bc1) / (jnp.sqrt(v / bc2) + ADAM_EPS) + ADAM_WD * w), m, v


def _adam(w, g, m, v, *, name):
    R, C = w.shape
    tr = _tile(R, 256)

    def body(w_ref, g_ref, m_ref, v_ref, d_ref, mo_ref, vo_ref):
        d_ref[...], mo_ref[...], vo_ref[...] = _adam_math(w_ref[...], g_ref[...], m_ref[...], v_ref[...])

    spec = pl.BlockSpec((tr, C), lambda i: (i, 0))
    return pl.pallas_call(
        body, name=name, grid=(R // tr,), in_specs=[spec] * 4, out_specs=[spec] * 3,
        out_shape=[jax.ShapeDtypeStruct((R, C), F32)] * 3,
        compiler_params=_cp(("parallel",)),
    )(w, g, m, v)


def _adam_halves(w, own, got, m, v, core, *, name):
    L, R, C = w.shape
    rh = R // 2
    tr = _tile(rh, 256)
    nt = rh // tr

    def body(c_ref, w_ref, own_ref, got_ref, m_ref, v_ref, g_ref, d_ref, mo_ref, vo_ref):
        g = jnp.where(pl.program_id(1) == c_ref[0], own_ref[...], got_ref[...])
        g_ref[...] = g
        d_ref[...], mo_ref[...], vo_ref[...] = _adam_math(w_ref[...], g, m_ref[...], v_ref[...])

    full = pl.BlockSpec((None, tr, C), lambda l, h, t, c_ref: (l, h * nt + t, 0))
    half = pl.BlockSpec((None, tr, C), lambda l, h, t, c_ref: (l, t, 0))
    return pl.pallas_call(
        body, name=name,
        grid_spec=pltpu.PrefetchScalarGridSpec(
            num_scalar_prefetch=1, grid=(L, 2, nt), in_specs=[full, half, half, full, full], out_specs=[full] * 4),
        out_shape=[jax.ShapeDtypeStruct((L, R, C), F32)] * 4,
        compiler_params=_cp(("parallel", "parallel", "parallel")),
    )(core, w, own, got, m, v)


def _remote(src, dst, send, recv, dev):
    return pltpu.make_async_remote_copy(src_ref=src, dst_ref=dst, send_sem=send, recv_sem=recv,
                                        device_id=dev, device_id_type=MESH)


def _place_w(shard, pos, layer, *, axis, name):
    _, R, C = shard.shape
    tr = _tile(R, 512, 16)
    nt = R // tr
    if axis == 2:
        out_shape = (1, R, N_CHIPS * C)
        out_map = lambda t, q: (0, t, q[0])
    else:
        out_shape = (1, N_CHIPS * R, C)
        out_map = lambda t, q: (0, q[0] * nt + t, 0)

    def body(q_ref, s_ref, o_ref):
        o_ref[...] = s_ref[...].astype(_WIRE)

    return pl.pallas_call(
        body, name=name,
        grid_spec=pltpu.PrefetchScalarGridSpec(
            num_scalar_prefetch=1, grid=(nt,),
            in_specs=[pl.BlockSpec((None, tr, C), lambda t, q: (layer, t, 0))],
            out_specs=pl.BlockSpec((None, tr, C), out_map)),
        out_shape=jax.ShapeDtypeStruct(out_shape, _WIRE),
        compiler_params=_cp(("parallel",)),
    )(pos, shard)


def _ag_window(ref, kind, px, py, h):
    axis, perm = kind
    q = 2 * px + py
    if perm:
        q = _perm_idx(q)
    if axis == 2:
        R, C = ref.shape[1], ref.shape[2] // N_CHIPS
        rh = R // 2
        return ref.at[:, pl.ds(pl.multiple_of(h * rh, 16), rh), pl.ds(pl.multiple_of(q * C, LANES), C)]
    R = ref.shape[1] // N_CHIPS
    rh = R // 2
    return ref.at[:, pl.ds(pl.multiple_of(q * R + h * rh, 16), rh), :]


def _ag_ici_copies(refs, kinds, send, recv):
    x, y, c = lax.axis_index("x"), lax.axis_index("y"), lax.axis_index("c")
    chips = [(1 - x, y), (x, 1 - y), (1 - x, 1 - y)]
    sends, recvs = [], []
    for a, (ref, kind) in enumerate(zip(refs, kinds)):
        own = _ag_window(ref, kind, x, y, c)
        for i, (px, py) in enumerate(chips):
            k = 3 * a + i
            sends.append(_remote(own, own, send.at[k], recv.at[k], (px, py, c)))
            recvs.append(_remote(own, _ag_window(ref, kind, px, py, c), send.at[k], recv.at[k], (px, py, c)))
    return sends, recvs


def _ag_start(arrs, kinds, after, *, name):
    n = len(arrs)

    def body(*refs):
        in_refs = refs[:n]
        send, recv = refs[n + len(after)], refs[n + len(after) + 1]
        token = refs[-1]
        sends, _ = _ag_ici_copies(in_refs, kinds, send, recv)
        for cp in sends:
            cp.start()
        token[...] = jnp.zeros_like(token)

    sems = pltpu.SemaphoreType.DMA((3 * n,))
    out = pl.pallas_call(
        body, name=name,
        out_shape=(sems, sems) + tuple(pltpu.HBM(a.shape, a.dtype) for a in arrs)
        + (jax.ShapeDtypeStruct((SUBLANES, LANES), F32),),
        in_specs=(HBM,) * n + (ANY,) * len(after),
        out_specs=(SEMS, SEMS) + (HBM,) * n + (pl.BlockSpec(memory_space=pltpu.VMEM),),
        input_output_aliases={a: 2 + a for a in range(n)},
        compiler_params=pltpu.CompilerParams(has_side_effects=EFFECT),
    )(*[pltpu.with_memory_space_constraint(a, pltpu.HBM) for a in arrs], *after)
    return out[0], out[1], list(out[2:2 + n]), out[-1]


def _ag_wait(send, recv, arrs, kinds, after, *, name):
    n = len(arrs)

    def body(*refs):
        in_refs = refs[:n]
        send, recv = refs[n], refs[n + 1]
        sends, recvs = _ag_ici_copies(in_refs, kinds, send, recv)
        for cp in sends:
            cp.wait_send()
        for cp in recvs:
            cp.wait_recv()

    out = pl.pallas_call(
        body, name=name,
        out_shape=tuple(pltpu.HBM(a.shape, a.dtype) for a in arrs),
        in_specs=(HBM,) * n + (SEMS, SEMS) + (ANY,) * len(after), out_specs=(HBM,) * n,
        input_output_aliases={a: a for a in range(n)},
        compiler_params=pltpu.CompilerParams(has_side_effects=EFFECT),
    )(*arrs, send, recv, *after)
    return list(out)


def _ag_forward(arrs, kinds, *, name):
    n = len(arrs)

    def body(*refs):
        o_refs, send, recv = refs[n:2 * n], refs[2 * n], refs[2 * n + 1]
        x, y, c = lax.axis_index("x"), lax.axis_index("y"), lax.axis_index("c")
        chips = [(1 - x, y), (x, 1 - y), (1 - x, 1 - y)]
        sib = (x, y, 1 - c)
        sends, recvs = [], []
        for a, (ref, kind) in enumerate(zip(o_refs, kinds)):
            for i, (px, py) in enumerate(chips):
                k = 3 * a + i
                got = _ag_window(ref, kind, px, py, c)
                cp = _remote(got, got, send.at[k], recv.at[k], sib)
                cp.start()
                sends.append(cp)
                recvs.append(_remote(got, _ag_window(ref, kind, px, py, 1 - c), send.at[k], recv.at[k], sib))
        for cp in recvs:
            cp.wait_recv()
        for cp in sends:
            cp.wait_send()

    out = pl.pallas_call(
        body, name=name, in_specs=[ANY] * n, out_specs=[ANY] * n,
        out_shape=[jax.ShapeDtypeStruct(a.shape, a.dtype) for a in arrs],
        input_output_aliases={a: a for a in range(n)},
        scratch_shapes=[pltpu.SemaphoreType.DMA((3 * n,)), pltpu.SemaphoreType.DMA((3 * n,))],
    )(*arrs)
    return list(out)


def _flip(x, y, c, f):
    return ((1 - x) if f & 4 else x, (1 - y) if f & 2 else y, (1 - c) if f & 1 else c)


def _rs_copies(g_ref, land_ref, send, recv):
    x, y, c = lax.axis_index("x"), lax.axis_index("y"), lax.axis_index("c")
    cps = []
    for f in range(1, N_DEV):
        tx, ty, tcx = _flip(x, y, c, f)
        cps.append(_remote(g_ref.at[4 * tx + 2 * ty + tcx], land_ref.at[f - 1], send.at[f - 1], recv.at[f - 1],
                           (tx, ty, tcx)))
    return cps


def _rs_start(g, *, name):
    _, pr, pc = g.shape
    land_shape = (N_DEV - 1, pr, pc)

    def body(g_ref, land_ref, send, recv, g_thru, land_thru, token):
        for cp in _rs_copies(g_ref, land_ref, send, recv):
            cp.start()
        token[...] = jnp.zeros_like(token)

    sems = pltpu.SemaphoreType.DMA((N_DEV - 1,))
    return pl.pallas_call(
        body, name=name,
        out_shape=(sems, sems, pltpu.HBM(g.shape, g.dtype), pltpu.HBM(land_shape, g.dtype),
                   jax.ShapeDtypeStruct((SUBLANES, LANES), F32)),
        in_specs=(HBM, HBM), out_specs=(SEMS, SEMS, HBM, HBM, pl.BlockSpec(memory_space=pltpu.VMEM)),
        input_output_aliases={0: 2, 1: 3},
        compiler_params=pltpu.CompilerParams(has_side_effects=EFFECT),
    )(pltpu.with_memory_space_constraint(g, pltpu.HBM),
      pltpu.with_memory_space_constraint(lax.empty(land_shape, g.dtype), pltpu.HBM))


def _rs_wait(send, recv, g_thru, land_thru, after, *, name):
    def body(g_ref, land_ref, send, recv, after_ref, g_out, land_out):
        cps = _rs_copies(g_ref, land_ref, send, recv)
        for cp in cps:
            cp.wait_send()
        for cp in cps:
            cp.wait_recv()

    return pl.pallas_call(
        body, name=name,
        out_shape=(pltpu.HBM(g_thru.shape, g_thru.dtype), pltpu.HBM(land_thru.shape, land_thru.dtype)),
        in_specs=(HBM, HBM, SEMS, SEMS, ANY), out_specs=(HBM, HBM), input_output_aliases={0: 0, 1: 1},
        compiler_params=pltpu.CompilerParams(has_side_effects=EFFECT),
    )(g_thru, land_thru, send, recv, after)


def _pair_exchange(own, *, name):
    def body(own_ref, got_ref, send, recv):
        x, y, c = lax.axis_index("x"), lax.axis_index("y"), lax.axis_index("c")
        cp = _remote(own_ref, got_ref, send, recv, (x, y, 1 - c))
        cp.start()
        cp.wait_recv()
        cp.wait_send()

    return pl.pallas_call(
        body, name=name, in_specs=[ANY], out_specs=ANY, out_shape=jax.ShapeDtypeStruct(own.shape, own.dtype),
        scratch_shapes=[pltpu.SemaphoreType.DMA, pltpu.SemaphoreType.DMA],
    )(own)


def _allreduce_flat(vec, *, name):
    n = vec.shape[0]
    unit = N_DEV * SUBLANES * LANES
    npad = -(-n // unit) * unit
    rows = npad // (N_DEV * LANES)
    xin = jnp.pad(vec, (0, npad - n)).reshape(N_DEV, rows, LANES)

    def body(x_ref, y_ref, a_ref, send_a, recv_a, send_b, recv_b):
        x, y, c = lax.axis_index("x"), lax.axis_index("y"), lax.axis_index("c")
        me = 4 * x + 2 * y + c
        a_ref[me] = x_ref[me]
        sends, recvs = [], []
        for f in range(1, N_DEV):
            dev = _flip(x, y, c, f)
            t = 4 * dev[0] + 2 * dev[1] + dev[2]
            cp = _remote(x_ref.at[t], a_ref.at[me], send_a.at[f - 1], recv_a.at[f - 1], dev)
            cp.start()
            sends.append(cp)
            recvs.append(_remote(x_ref.at[me], a_ref.at[t], send_a.at[f - 1], recv_a.at[f - 1], dev))
        for cp in recvs:
            cp.wait_recv()
        for cp in sends:
            cp.wait_send()
        acc = a_ref[0]
        for s in range(1, N_DEV):
            acc = acc + a_ref[s]
        y_ref[me] = acc
        sends, recvs = [], []
        for f in range(1, N_DEV):
            dev = _flip(x, y, c, f)
            t = 4 * dev[0] + 2 * dev[1] + dev[2]
            cp = _remote(y_ref.at[me], y_ref.at[me], send_b.at[f - 1], recv_b.at[f - 1], dev)
            cp.start()
            sends.append(cp)
            recvs.append(_remote(y_ref.at[me], y_ref.at[t], send_b.at[f - 1], recv_b.at[f - 1], dev))
        for cp in recvs:
            cp.wait_recv()
        for cp in sends:
            cp.wait_send()

    vm = pl.BlockSpec(memory_space=pltpu.VMEM)
    out = pl.pallas_call(
        body, name=name, in_specs=[vm], out_specs=vm,
        out_shape=jax.ShapeDtypeStruct((N_DEV, rows, LANES), F32),
        scratch_shapes=[pltpu.VMEM((N_DEV, rows, LANES), F32)] + [pltpu.SemaphoreType.DMA((N_DEV - 1,))] * 4,
        compiler_params=_cp(),
    )(xin)
    return out.reshape(npad)[:n]


def _perm_cols(v, blocks=N_CHIPS):
    lead, n = v.shape[:-1], v.shape[-1]
    return v.reshape(lead + (blocks, n // blocks))[..., PERM, :].reshape(lead + (n,))


def _pack(arrs):
    return jnp.concatenate([a.reshape(-1).astype(F32) for a in arrs])


def _unpack(flat, shapes):
    out, pos = [], 0
    for s in shapes:
        n = 1
        for d in s:
            n *= d
        out.append(flat[pos:pos + n].reshape(s))
        pos += n
    return out


def kernel(x, conv_w_in, conv_b_in, conv_w_dw, conv_b_dw, conv_ln_g, conv_ln_b, conv_w_out, conv_b_out, gmlp_w_in, gmlp_b_in, gmlp_ln_g, gmlp_ln_b, gmlp_w_s, gmlp_b_s, gmlp_w_out, gmlp_b_out, ffn_w_up, ffn_b_up, ffn_w_dw, ffn_b_dw, ffn_w_down, ffn_b_down, norm1_g, norm1_b, norm2_g, norm2_b, loss_target, m_conv_w_in, m_conv_b_in, m_conv_w_dw, m_conv_b_dw, m_conv_ln_g, m_conv_ln_b, m_conv_w_out, m_conv_b_out, m_gmlp_w_in, m_gmlp_b_in, m_gmlp_ln_g, m_gmlp_ln_b, m_gmlp_w_s, m_gmlp_b_s, m_gmlp_w_out, m_gmlp_b_out, m_ffn_w_up, m_ffn_b_up, m_ffn_w_dw, m_ffn_b_dw, m_ffn_w_down, m_ffn_b_down, m_norm1_g, m_norm1_b, m_norm2_g, m_norm2_b, v_conv_w_in, v_conv_b_in, v_conv_w_dw, v_conv_b_dw, v_conv_ln_g, v_conv_ln_b, v_conv_w_out, v_conv_b_out, v_gmlp_w_in, v_gmlp_b_in, v_gmlp_ln_g, v_gmlp_ln_b, v_gmlp_w_s, v_gmlp_b_s, v_gmlp_w_out, v_gmlp_b_out, v_ffn_w_up, v_ffn_b_up, v_ffn_w_dw, v_ffn_b_dw, v_ffn_w_down, v_ffn_b_down, v_norm1_g, v_norm1_b, v_norm2_g, v_norm2_b):
    P = dict(locals())
    WEIGHTS = ['conv_w_in', 'conv_b_in', 'conv_w_dw', 'conv_b_dw', 'conv_ln_g', 'conv_ln_b', 'conv_w_out',
               'conv_b_out', 'gmlp_w_in', 'gmlp_b_in', 'gmlp_ln_g', 'gmlp_ln_b', 'gmlp_w_s', 'gmlp_b_s',
               'gmlp_w_out', 'gmlp_b_out', 'ffn_w_up', 'ffn_b_up', 'ffn_w_dw', 'ffn_b_dw', 'ffn_w_down',
               'ffn_b_down', 'norm1_g', 'norm1_b', 'norm2_g', 'norm2_b']
    BIG = ['conv_w_in', 'conv_w_out', 'gmlp_w_in', 'gmlp_w_out', 'ffn_w_up', 'ffn_w_down']
    SMALL_SHARDED = {'conv_w_dw': 2, 'gmlp_b_in': 1, 'gmlp_ln_g': 1, 'gmlp_ln_b': 1, 'gmlp_b_out': 1, 'ffn_w_dw': 2}

    B, S, D = x.shape
    T = B * S
    depth = norm1_g.shape[0]
    alpha = (2.0 * depth) ** 0.25
    C = conv_w_out.shape[-1]
    F2 = ffn_b_up.shape[-1]
    G, L = gmlp_w_s.shape[1], gmlp_w_s.shape[2]
    xi, yi, ci = lax.axis_index("x"), lax.axis_index("y"), lax.axis_index("c")
    shard = 2 * xi + yi

    i32 = lambda v: jnp.reshape(v, (1,)).astype(jnp.int32)
    pos_plain, pos_perm = i32(shard), i32(_perm_idx(shard))
    me_id, core_id = i32(4 * xi + 2 * yi + ci), i32(ci)

    groups = []
    for i in range(depth):
        mix = 'conv' if i % 2 == 0 else 'gmlp'
        groups.append((f"{mix}{i // 2}", [(mix + '_w_in', i // 2, 2, True), (mix + '_w_out', i // 2, 1, False)]))
        groups.append((f"ffn{i}", [('ffn_w_up', i, 2, True), ('ffn_w_down', i, 1, False)]))
    sm_names = list(SMALL_SHARDED)
    sm_shapes = [P[n].shape for n in sm_names]
    mine = _pack([P[n] for n in sm_names]) * (ci == 0).astype(F32)
    buf = jnp.zeros((N_CHIPS, mine.shape[0]), F32)
    buf = lax.dynamic_update_slice(buf, mine[None], (shard, 0))
    gathered = _allreduce_flat(buf.reshape(-1), name="ag_small").reshape(N_CHIPS, -1)

    started, order = {}, [gathered]
    for gname, members in groups:
        placed = [_place_w(P[n], pos_perm if perm else pos_plain, l, axis=axis, name=f"place_{n}_{l}")
                  for n, l, axis, perm in members]
        kinds = [(axis, perm) for _, _, axis, perm in members]
        send, recv, arrs, token = _ag_start(placed, kinds, order, name=f"ag_start_{gname}")
        order = [token]
        started[gname] = (send, recv, arrs, kinds, [(n, l) for n, l, _, _ in members])
    wts = {}

    def arrive(gname, after):
        send, recv, arrs, kinds, keys = started[gname]
        arrs = _ag_wait(send, recv, arrs, kinds, after, name=f"ag_wait_{gname}")
        arrs = _ag_forward(arrs, kinds, name=f"ag_fwd_{gname}")
        wts.update(zip(keys, arrs))

    full = {}
    for n, parts in zip(sm_names, zip(*[_unpack(gathered[k], sm_shapes) for k in range(N_CHIPS)])):
        full[n] = jnp.concatenate(parts, axis=SMALL_SHARDED[n])
    for n in WEIGHTS:
        if n not in BIG and n not in full:
            full[n] = P[n]

    assert G * L == C, "a gMLP group must be as wide as a chunk is long"

    def row(v):
        return v.reshape(1, -1)

    def pad_rows(v, r):
        return jnp.pad(v, ((0, r - v.shape[0]), (0, 0)))

    xf = x.reshape(T, D)
    saved = []
    cur, cur_b = xf, xf
    for i in range(depth):
        j = i // 2
        sv = {'x': cur, 'xb': cur_b}
        arrive(groups[2 * i][0], order if i == 0 else [cur_b])
        if i % 2 == 0:
            b_in = row(_perm_cols(full['conv_b_in'][j]))
            h1 = _mm(cur_b, wts['conv_w_in', j], bl=0, bias=b_in, tm=_tile(T, 512), tn=_tile(2 * C, 1024, LANES),
                     tk=D, name=f"conv_in_{j}")
            wdw = pad_rows(full['conv_w_dw'][j], CONV_TAPS_PAD)
            dwo = _conv_fwd(h1, wdw, row(full['conv_b_dw'][j]), B=B, S=S, name=f"conv_dw_{j}")
            s_act, xhc, rsc = _ln_silu_fwd(dwo, row(full['conv_ln_g'][j]), row(full['conv_ln_b'][j]),
                                           name=f"conv_ln_{j}")
            sv.update(h1=h1, wdw=wdw, act=s_act, xhc=xhc, rsc=rsc)
            y1 = _mm_res_ln(s_act, wts['conv_w_out', j], 0, row(full['conv_b_out'][j]), cur, alpha, row(norm1_g[i]),
                            row(norm1_b[i]), name=f"conv_out_ln_{j}")
        else:
            b_in = row(_perm_cols(full['gmlp_b_in'][j]))
            pre = _mm(cur_b, wts['gmlp_w_in', j], bl=0, bias=b_in, tm=_tile(T, 512), tn=_tile(2 * C, 1024, LANES),
                      tk=D, name=f"gmlp_in_{j}")
            bsb = jnp.repeat(gmlp_b_s[j].T, L, axis=1)
            us, xhv, rsv = _gmlp_gate_fwd(pre, row(full['gmlp_ln_g'][j]), row(full['gmlp_ln_b'][j]), gmlp_w_s[j],
                                          bsb, name=f"gmlp_gate_{j}")
            sv.update(pre=pre, bsb=bsb, act=us, xhv=xhv, rsv=rsv)
            y1 = _mm_res_ln(us, wts['gmlp_w_out', j], 0, row(full['gmlp_b_out'][j]), cur, alpha, row(norm1_g[i]),
                            row(norm1_b[i]), name=f"gmlp_out_ln_{j}")
        x1, x1b, xh1, rs1 = y1
        arrive(groups[2 * i + 1][0], [x1b])
        wdw3 = pad_rows(_perm_cols(full['ffn_w_dw'][i]), SUBLANES)
        bdw3 = row(_perm_cols(ffn_b_dw[i]))
        hs, f_act = _ffn_up_fwd(x1b, wts['ffn_w_up', i], 0, row(_perm_cols(ffn_b_up[i])), wdw3, bdw3, S=S,
                                name=f"ffn_up_{i}")
        x2, x2b, xh2, rs2 = _mm_res_ln(f_act, wts['ffn_w_down', i], 0, row(ffn_b_down[i]), x1, alpha, row(norm2_g[i]),
                                       row(norm2_b[i]), name=f"ffn_down_ln_{i}")
        sv.update(x1=x1, x1b=x1b, xh1=xh1, rs1=rs1, hs=hs, f=f_act, wdw3=wdw3, bdw3=bdw3, xh2=xh2, rs2=rs2)
        saved.append(sv)
        cur, cur_b = x2, x2b

    sg = {n: [None] * full[n].shape[0] for n in WEIGHTS if n not in BIG}
    inflight = {n: [None] * P[n].shape[0] for n in BIG}
    deps = []
    tgt = loss_target.reshape(T, D)
    dcur = None
    loss_part = None
    tk_t = _tile(T, 512)

    def wgrad(n, l, a_, b_, **kw):
        g = _mm(a_, b_, ta=True, out_dtype=_WIRE, tk=tk_t, name=f"{n}_dw_{l}", deps=deps, **kw)
        send, recv, g_thru, land, token = _rs_start(g, name=f"rs_start_{n}_{l}")
        inflight[n][l] = (send, recv, g_thru, land)
        deps.append(token)

    for i in reversed(range(depth)):
        j = i // 2
        sv = saved[i]
        if dcur is None:
            dz2, dz2b, dg, db, cs, ls = _ln_bwd(cur, sv['xh2'], sv['rs2'], row(norm2_g[i]), target=tgt,
                                                name=f"ln2_bwd_head_{i}")
            loss_part = ls
        else:
            dz2, dz2b, dg, db, cs = _ln_bwd(dcur, sv['xh2'], sv['rs2'], row(norm2_g[i]), name=f"ln2_bwd_{i}")
        sg['norm2_g'][i], sg['norm2_b'][i], sg['ffn_b_down'][i] = dg.sum(0), db.sum(0), cs.sum(0)
        Fh = F2 // 2
        wgrad('ffn_w_down', i, sv['f'], dz2b, tm=Fh // 2, tn=_tile(D, 1024, LANES), pieces=('row',))
        dh, csu, dwd, dbd = _ffn_bwd(dz2b, wts['ffn_w_down', i], 0, sv['hs'], sv['wdw3'], sv['bdw3'], S=S,
                                     name=f"ffn_bwd_{i}")
        sg['ffn_b_up'][i] = _perm_cols(csu.sum(0))
        sg['ffn_w_dw'][i] = _perm_cols(dwd.sum(1))
        sg['ffn_b_dw'][i] = _perm_cols(dbd.sum(0))
        wgrad('ffn_w_up', i, sv['x1b'], dh, tm=D // 2, tn=F2 // N_CHIPS, pieces=('col', True))
        dx1 = _mm(dh, wts['ffn_w_up', i], bl=0, tb=True, res=dz2, res_scale=alpha, tm=_tile(T, 512), tn=_tile(D, 1024, LANES),
                  tk=F2 // N_CHIPS, name=f"ffn_dx_{i}", deps=deps)
        dz1, dz1b, dg, db, cs = _ln_bwd(dx1, sv['xh1'], sv['rs1'], row(norm1_g[i]), name=f"ln1_bwd_{i}")
        sg['norm1_g'][i], sg['norm1_b'][i] = dg.sum(0), db.sum(0)
        if i % 2 == 0:
            sg['conv_b_out'][j] = cs.sum(0)
            wgrad('conv_w_out', j, sv['act'], dz1b, tm=_tile(C, 512), tn=_tile(D, 1024, LANES), pieces=('row',))
            ds = _mm(dz1b, wts['conv_w_out', j], bl=0, tb=True, tm=_tile(T, 512), tn=_tile(C, 1024, LANES), tk=_tile(D, 1024, LANES),
                     name=f"conv_ds_{j}", deps=deps)
            ddw, dg, db = _ln_silu_bwd(ds, sv['xhc'], sv['rsc'], row(full['conv_ln_g'][j]),
                                       row(full['conv_ln_b'][j]), name=f"conv_ln_bwd_{j}")
            sg['conv_ln_g'][j], sg['conv_ln_b'][j] = dg.sum(0), db.sum(0)
            dglu, dwk, dbk = _conv_bwd(ddw, sv['h1'], sv['wdw'], B=B, S=S, name=f"conv_dw_bwd_{j}")
            sg['conv_w_dw'][j] = dwk.sum(1)[:conv_w_dw.shape[1]]
            sg['conv_b_dw'][j] = dbk.sum(0)
            dh1, csi = _glu_bwd(dglu, sv['h1'], name=f"conv_glu_bwd_{j}")
            sg['conv_b_in'][j] = _perm_cols(csi.sum(0))
            fam = 'conv_w_in'
        else:
            sg['gmlp_b_out'][j] = cs.sum(0)
            wgrad('gmlp_w_out', j, sv['act'], dz1b, tm=_tile(C, 512), tn=_tile(D, 1024, LANES), pieces=('row',))
            dus = _mm(dz1b, wts['gmlp_w_out', j], bl=0, tb=True, tm=_tile(T, 512), tn=_tile(C, 1024, LANES),
                      tk=_tile(D, 1024, LANES), name=f"gmlp_dus_{j}", deps=deps)
            dh1, dg, db, csi, dws, dbs = _gmlp_gate_bwd(dus, sv['pre'], sv['xhv'], sv['rsv'], row(full['gmlp_ln_g'][j]),
                                                        row(full['gmlp_ln_b'][j]), gmlp_w_s[j], sv['bsb'],
                                                        name=f"gmlp_gate_bwd_{j}")
            sg['gmlp_ln_g'][j], sg['gmlp_ln_b'][j] = dg.sum(0), db.sum(0)
            sg['gmlp_b_in'][j] = _perm_cols(csi.sum(0))
            sg['gmlp_w_s'][j] = dws
            sg['gmlp_b_s'][j] = dbs.reshape(L, G, L).sum(-1).T
            fam = 'gmlp_w_in'
        wgrad(fam, j, sv['xb'], dh1, tm=D // 2, tn=(2 * C) // N_CHIPS, pieces=('col', True))
        dcur = _mm(dh1, wts[fam, j], bl=0, tb=True, res=dz1, res_scale=alpha, tm=_tile(T, 512),
                   tn=_tile(D, 1024, LANES), tk=_tile(2 * C, 1024, LANES), name=f"{fam}_dx_{j}", deps=deps)
    grad_x = dcur.reshape(B, S, D)

    small_names = [n for n in WEIGHTS if n not in BIG]
    small_full = [jnp.stack(sg[n]) for n in small_names]
    flat = _pack(small_full + [loss_part])
    red = _allreduce_flat(flat, name="ar_small")
    red_parts = _unpack(red, [a.shape for a in small_full] + [loss_part.shape])
    loss = (0.5 / D) * jnp.sum(red_parts[-1])
    grads = {}
    for n, g in zip(small_names, red_parts[:-1]):
        if n in SMALL_SHARDED:
            ax = SMALL_SHARDED[n]
            width = P[n].shape[ax]
            g = lax.dynamic_slice_in_dim(g, shard * width, width, axis=ax)
        grads[n] = g

    big_out = {}
    for n in ['ffn_w_down', 'ffn_w_up', 'gmlp_w_out', 'gmlp_w_in', 'conv_w_out', 'conv_w_in']:
        own = None
        n_layers = len(inflight[n])
        for l in reversed(range(n_layers)):
            send, recv, g_thru, land = inflight[n][l]
            pc_, r = _rs_wait(send, recv, g_thru, land, dcur, name=f"rs_wait_{n}_{l}")
            own = _sum_pieces(pc_, r, me_id, l, own, n_layers, name=f"sum_{n}_{l}")
        got = _pair_exchange(own, name=f"px_{n}")
        big_out[n] = _adam_halves(P[n], own, got, P['m_' + n], P['v_' + n], core_id, name=f"adam_{n}")

    shapes = [P[n].shape for n in small_names]
    n_small = sum(functools.reduce(lambda p_, d_: p_ * d_, s_, 1) for s_ in shapes)
    unit = SUBLANES * LANES
    npad = -(-n_small // unit) * unit

    def flat2d(arrs, fill=0.0):
        v = _pack(arrs)
        return jnp.pad(v, (0, npad - n_small), constant_values=fill).reshape(-1, LANES)

    dl, mo, vo = _adam(flat2d([P[n] for n in small_names]), flat2d([grads[n] for n in small_names]),
                       flat2d([P['m_' + n] for n in small_names]),
                       flat2d([P['v_' + n] for n in small_names], fill=1.0), name="adam_small")
    small_out = {n: [grads[n], None, None, None] for n in small_names}
    for k, t in enumerate((dl, mo, vo)):
        for n, a in zip(small_names, _unpack(t.reshape(-1), shapes)):
            small_out[n][k + 1] = a

    outs = [loss, grad_x]
    for k in range(4):
        for n in WEIGHTS:
            outs.append(big_out[n][k] if n in BIG else small_out[n][k])
    return tuple(outs)
```

```python
import functools

import jax
import jax.numpy as jnp
from jax import lax
from jax.experimental import pallas as pl
from jax.experimental.pallas import tpu as pltpu

F32 = jnp.float32
_MXU = jnp.bfloat16
_WIRE = jnp.bfloat16
_HDT = jnp.bfloat16
LN_EPS = 1e-5
ADAM_LR, ADAM_B1, ADAM_B2, ADAM_EPS, ADAM_WD, ADAM_STEP = 0.001, 0.9, 0.999, 1e-08, 0.01, 10
N_CHIPS = 4
N_DEV = 8
LANES = 128
SUBLANES = 8
CONV_TAPS_PAD = 32
VMEM_LIMIT = 56 << 20
MESH = pl.DeviceIdType.MESH
ANY = pl.BlockSpec(memory_space=pl.ANY)
HBM = pl.BlockSpec(memory_space=pltpu.HBM)
SEMS = pl.BlockSpec(memory_space=pltpu.SEMAPHORE)
EFFECT = pltpu.SideEffectType.DATAFLOW_SIDE_EFFECTING
PERM = (0, 2, 1, 3)


def _cp(sem=None):
    return pltpu.CompilerParams(dimension_semantics=sem, vmem_limit_bytes=VMEM_LIMIT)


def _tile(dim, pref, mult=SUBLANES):
    if dim <= pref:
        return dim
    t = (pref // mult) * mult
    while t > mult and dim % t:
        t -= mult
    assert dim % t == 0, (dim, pref, mult)
    return t


def _perm_idx(q):
    return (q % 2) * 2 + q // 2


def _fold8(t):
    r, n = t.shape
    return t.reshape(r // SUBLANES, SUBLANES, n).sum(axis=0)


def _ln_rows(z, g, b):
    mu = jnp.mean(z, axis=-1, keepdims=True)
    xc = z - mu
    var = jnp.mean(xc * xc, axis=-1, keepdims=True)
    rstd = lax.rsqrt(var + LN_EPS)
    xh = xc * rstd
    return xh * g + b, xh, rstd


def _ln_bwd_rows(dy, xh, rstd, g):
    dxh = dy * g
    m1 = jnp.mean(dxh, axis=-1, keepdims=True)
    m2 = jnp.mean(dxh * xh, axis=-1, keepdims=True)
    return rstd * (dxh - m1 - xh * m2)


def _sigmoid(v):
    return 1.0 / (1.0 + jnp.exp(-v))


def _gelu_parts(p):
    cdf = 0.5 * (1.0 + lax.erf(p * 0.7071067811865476))
    pdf = jnp.exp(-0.5 * p * p) * 0.3989422804014327
    return p * cdf, cdf + p * pdf


def _shift_down(prev8, t, s):
    ext = jnp.concatenate([prev8, t], axis=0)
    return pltpu.roll(ext, s, 0)[SUBLANES:]


def _shift_up(t, next8, s):
    n = t.shape[0]
    ext = jnp.concatenate([t, next8], axis=0)
    return pltpu.roll(ext, n + SUBLANES - s, 0)[:n]


def _mm(a, b, *, ta=False, tb=False, bl=None, bias=None, res=None, res_scale=1.0, out_dtype=F32,
        tm, tn, tk, name, pieces=None, deps=None, n_outer=False):
    M, K = (a.shape[1], a.shape[0]) if ta else a.shape
    bs = b.shape[1:] if bl is not None else b.shape
    N, Kb = (bs[0], bs[1]) if tb else (bs[1], bs[0])
    assert K == Kb and M % tm == 0 and N % tn == 0 and K % tk == 0, (a.shape, b.shape, tm, tn, tk)
    gm, gn, gk = M // tm, N // tn, K // tk

    def spec(block, imap):
        if n_outer:
            return pl.BlockSpec(block, lambda j, i, k: imap(i, j, k))
        return pl.BlockSpec(block, imap)

    a_spec = spec((tk, tm), lambda i, j, k: (k, i)) if ta else spec((tm, tk), lambda i, j, k: (i, k))
    bblk = (tn, tk) if tb else (tk, tn)
    bmap = (lambda i, j, k: (j, k)) if tb else (lambda i, j, k: (k, j))
    if bl is not None:
        b_spec = spec((None,) + bblk, lambda i, j, k: (bl,) + bmap(i, j, k))
    else:
        b_spec = spec(bblk, bmap)
    in_specs, operands = [a_spec, b_spec], [a, b]
    if bias is not None:
        in_specs.append(spec((1, tn), lambda i, j, k: (0, j)))
        operands.append(bias)
    if res is not None:
        in_specs.append(spec((tm, tn), lambda i, j, k: (i, j)))
        operands.append(res)
    n_dep = len(deps) if deps else 0
    if n_dep:
        in_specs += [ANY] * n_dep
        operands += deps
        del deps[:]
    if pieces is None:
        out_shape = jax.ShapeDtypeStruct((M, N), out_dtype)
        out_spec = spec((tm, tn), lambda i, j, k: (i, j))
        ppb = pr = None
    elif pieces[0] == 'col':
        pr, pc = M // 2, N // N_CHIPS
        assert tm % pr == 0 and pc % tn == 0
        ppb, per = tm // pr, pc // tn
        perm = pieces[1]
        out_shape = jax.ShapeDtypeStruct((N_DEV, pr, pc), out_dtype)
        out_spec = spec(
            (ppb, pr, tn),
            lambda i, j, k: ((2 * (_perm_idx(j // per) if perm else j // per)) // ppb + i, 0, j % per))
    else:
        pr = M // N_DEV
        assert tm % pr == 0
        ppb = tm // pr
        out_shape = jax.ShapeDtypeStruct((N_DEV, pr, N), out_dtype)
        out_spec = spec((ppb, pr, tn), lambda i, j, k: (i, 0, j))
    dims = (((0 if ta else 1,), (1 if tb else 0,)), ((), ()))

    def body(*refs):
        a_ref, b_ref = refs[0], refs[1]
        pos = 2
        bias_ref = res_ref = None
        if bias is not None:
            bias_ref = refs[pos]
            pos += 1
        if res is not None:
            res_ref = refs[pos]
            pos += 1
        pos += n_dep
        o_ref = refs[pos]

        def finish(r):
            if bias_ref is not None:
                r = r + bias_ref[...]
            if res_ref is not None:
                r = r + res_scale * res_ref[...]
            if pieces is not None:
                r = r.reshape(ppb, pr, tn)
            o_ref[...] = r.astype(out_dtype)

        part = lax.dot_general(a_ref[...].astype(_MXU), b_ref[...].astype(_MXU), dims, preferred_element_type=F32)
        if gk == 1:
            finish(part)
            return
        acc_ref = refs[pos + 1]
        k = pl.program_id(2)

        @pl.when(k == 0)
        def _():
            acc_ref[...] = part

        @pl.when((k > 0) & (k < gk - 1))
        def _():
            acc_ref[...] += part

        @pl.when(k == gk - 1)
        def _():
            finish(acc_ref[...] + part)

    return pl.pallas_call(
        body, name=name, grid=(gn, gm, gk) if n_outer else (gm, gn, gk), in_specs=in_specs, out_specs=out_spec,
        out_shape=out_shape, scratch_shapes=[pltpu.VMEM((tm, tn), F32)] if gk > 1 else [],
        compiler_params=_cp(("parallel", "parallel", "arbitrary")),
    )(*operands)


def _mm_res_ln(a, w, wl, bias, res, alpha, g, b, *, name):
    T, K = a.shape
    D = w.shape[-1]
    tm = _tile(T, 256)

    def body(a_ref, w_ref, bias_ref, res_ref, g_ref, b_ref, y_ref, yb_ref, xh_ref, rs_ref):
        z = jnp.dot(a_ref[...].astype(_MXU), w_ref[...].astype(_MXU), preferred_element_type=F32)
        z = z + bias_ref[...] + alpha * res_ref[...]
        y, xh, rstd = _ln_rows(z, g_ref[...], b_ref[...])
        y_ref[...] = y
        yb_ref[...] = y.astype(_MXU)
        xh_ref[...] = xh
        rs_ref[...] = rstd

    row = lambda i: (i, 0)
    vec = pl.BlockSpec((1, D), lambda i: (0, 0))
    return pl.pallas_call(
        body, name=name, grid=(T // tm,),
        in_specs=[pl.BlockSpec((tm, K), row), pl.BlockSpec((None, K, D), lambda i: (wl, 0, 0)), vec,
                  pl.BlockSpec((tm, D), row), vec, vec],
        out_specs=[pl.BlockSpec((tm, D), row), pl.BlockSpec((tm, D), row), pl.BlockSpec((tm, D), row),
                   pl.BlockSpec((tm, 1), row)],
        out_shape=[jax.ShapeDtypeStruct((T, D), F32), jax.ShapeDtypeStruct((T, D), _MXU),
                   jax.ShapeDtypeStruct((T, D), F32), jax.ShapeDtypeStruct((T, 1), F32)],
        compiler_params=_cp(("parallel",)),
    )(a, w, bias, res, g, b)


def _ln_bwd(dy, xh, rstd, g, *, name, target=None):
    T, D = dy.shape
    tm = _tile(T, 256)
    head = target is not None

    def body(*refs):
        if head:
            dy_ref, t_ref, xh_ref, rs_ref, g_ref, dz_ref, dzb_ref, dg_ref, db_ref, cs_ref, ls_ref = refs
        else:
            dy_ref, xh_ref, rs_ref, g_ref, dz_ref, dzb_ref, dg_ref, db_ref, cs_ref = refs
        i = pl.program_id(0)

        @pl.when(i == 0)
        def _():
            dg_ref[...] = jnp.zeros_like(dg_ref)
            db_ref[...] = jnp.zeros_like(db_ref)
            cs_ref[...] = jnp.zeros_like(cs_ref)
            if head:
                ls_ref[...] = jnp.zeros_like(ls_ref)

        d = dy_ref[...]
        if head:
            err = d - t_ref[...]
            ls_ref[...] += _fold8(err * err)
            d = err * (1.0 / D)
        xh = xh_ref[...]
        dz = _ln_bwd_rows(d, xh, rs_ref[...], g_ref[...])
        dz_ref[...] = dz
        dzb_ref[...] = dz.astype(_MXU)
        dg_ref[...] += _fold8(d * xh)
        db_ref[...] += _fold8(d)
        cs_ref[...] += _fold8(dz)

    row = lambda i: (i, 0)
    fixed = lambda i: (0, 0)
    tile = pl.BlockSpec((tm, D), row)
    part = pl.BlockSpec((SUBLANES, D), fixed)
    in_specs = [tile] + ([tile] if head else []) + [tile, pl.BlockSpec((tm, 1), row), pl.BlockSpec((1, D), fixed)]
    n_part = 4 if head else 3
    operands = [dy] + ([target] if head else []) + [xh, rstd, g]
    return pl.pallas_call(
        body, name=name, grid=(T // tm,), in_specs=in_specs,
        out_specs=[tile, tile] + [part] * n_part,
        out_shape=[jax.ShapeDtypeStruct((T, D), F32), jax.ShapeDtypeStruct((T, D), _MXU)]
        + [jax.ShapeDtypeStruct((SUBLANES, D), F32)] * n_part,
        compiler_params=_cp(("arbitrary",)),
    )(*operands)


def _conv_cols(C, tc):
    per = (C // 2) // tc
    return per, (lambda j: (j // per) * (2 * per) + j % per)


def _glu_shifted(a_ref, g_ref, p_ref, S):
    u = a_ref[...] * _sigmoid(g_ref[...])
    rows = lax.broadcasted_iota(jnp.int32, u.shape, 0)
    for r in range(SUBLANES):
        p_ref[r, 0:CONV_TAPS_PAD, :] = jnp.zeros((CONV_TAPS_PAD, u.shape[1]), F32)
        p_ref[r, CONV_TAPS_PAD:CONV_TAPS_PAD + S, :] = u if r == 0 else jnp.where(rows >= r, pltpu.roll(u, r, 0), 0.0)


def _conv_fwd(h1, w_dw, b_dw, *, B, S, name):
    C = w_dw.shape[1]
    taps = CONV_TAPS_PAD - 1
    tc = LANES
    ch = _tile(S, 128)
    per, col_a = _conv_cols(C, tc)

    def body(a_ref, g_ref, w_ref, b_ref, o_ref, p_ref):
        _glu_shifted(a_ref, g_ref, p_ref, S)

        def chunk(ci, carry):
            base = pl.multiple_of(ci * ch, ch)
            acc = jnp.zeros((ch, tc), F32) + b_ref[...]
            for k in range(taps):
                q, r = divmod(taps - 1 - k, SUBLANES)
                start = pl.multiple_of(base + (CONV_TAPS_PAD - SUBLANES * q), SUBLANES)
                acc = acc + w_ref[pl.ds(k, 1), :] * p_ref[r, pl.ds(start, ch), :]
            o_ref[pl.ds(base, ch), :] = acc
            return carry

        lax.fori_loop(0, S // ch, chunk, 0)

    return pl.pallas_call(
        body, name=name, grid=(B, C // tc),
        in_specs=[pl.BlockSpec((S, tc), lambda b, j: (b, col_a(j))),
                  pl.BlockSpec((S, tc), lambda b, j: (b, col_a(j) + per)),
                  pl.BlockSpec((CONV_TAPS_PAD, tc), lambda b, j: (0, j)),
                  pl.BlockSpec((1, tc), lambda b, j: (0, j))],
        out_specs=pl.BlockSpec((S, tc), lambda b, j: (b, j)),
        out_shape=jax.ShapeDtypeStruct((B * S, C), F32),
        scratch_shapes=[pltpu.VMEM((SUBLANES, S + CONV_TAPS_PAD, tc), F32)],
        compiler_params=_cp(("parallel", "parallel")),
    )(h1, h1, w_dw, b_dw)


def _conv_bwd(dd, h1, w_dw, *, B, S, name):
    C = w_dw.shape[1]
    taps = CONV_TAPS_PAD - 1
    tc = LANES
    ch = _tile(S, 128)
    per, col_a = _conv_cols(C, tc)

    def body(d_ref, a_ref, g_ref, w_ref, du_ref, dw_ref, db_ref, p_ref, q_ref):
        b = pl.program_id(1)

        @pl.when(b == 0)
        def _():
            dw_ref[...] = jnp.zeros_like(dw_ref)
            db_ref[...] = jnp.zeros_like(db_ref)

        _glu_shifted(a_ref, g_ref, p_ref, S)
        d = d_ref[...]
        rows = lax.broadcasted_iota(jnp.int32, d.shape, 0)
        for r in range(SUBLANES):
            q_ref[r, S:S + CONV_TAPS_PAD, :] = jnp.zeros((CONV_TAPS_PAD, tc), F32)
            q_ref[r, 0:S, :] = d if r == 0 else jnp.where(rows < S - r, pltpu.roll(d, S - r, 0), 0.0)
        db_ref[...] += _fold8(d)

        def chunk(ci, carry):
            base = pl.multiple_of(ci * ch, ch)
            dch = d_ref[pl.ds(base, ch), :]
            acc = jnp.zeros((ch, tc), F32)
            for k in range(taps):
                q, r = divmod(taps - 1 - k, SUBLANES)
                up = pl.multiple_of(base + SUBLANES * q, SUBLANES)
                acc = acc + w_ref[pl.ds(k, 1), :] * q_ref[r, pl.ds(up, ch), :]
                down = pl.multiple_of(base + (CONV_TAPS_PAD - SUBLANES * q), SUBLANES)
                dw_ref[k] += _fold8(dch * p_ref[r, pl.ds(down, ch), :])
            du_ref[pl.ds(base, ch), :] = acc
            return carry

        lax.fori_loop(0, S // ch, chunk, 0)

    return pl.pallas_call(
        body, name=name, grid=(C // tc, B),
        in_specs=[pl.BlockSpec((S, tc), lambda j, b: (b, j)),
                  pl.BlockSpec((S, tc), lambda j, b: (b, col_a(j))),
                  pl.BlockSpec((S, tc), lambda j, b: (b, col_a(j) + per)),
                  pl.BlockSpec((CONV_TAPS_PAD, tc), lambda j, b: (0, j))],
        out_specs=[pl.BlockSpec((S, tc), lambda j, b: (b, j)),
                   pl.BlockSpec((CONV_TAPS_PAD, SUBLANES, tc), lambda j, b: (0, 0, j)),
                   pl.BlockSpec((SUBLANES, tc), lambda j, b: (0, j))],
        out_shape=[jax.ShapeDtypeStruct((B * S, C), F32),
                   jax.ShapeDtypeStruct((CONV_TAPS_PAD, SUBLANES, C), F32),
                   jax.ShapeDtypeStruct((SUBLANES, C), F32)],
        scratch_shapes=[pltpu.VMEM((SUBLANES, S + CONV_TAPS_PAD, tc), F32),
                        pltpu.VMEM((SUBLANES, S + CONV_TAPS_PAD, tc), F32)],
        compiler_params=_cp(("parallel", "arbitrary")),
    )(dd, h1, h1, w_dw)


def _ln_silu_fwd(v, g, b, *, name):
    T, C = v.shape
    tm = _tile(T, 512)

    def body(v_ref, g_ref, b_ref, s_ref, xh_ref, rs_ref):
        y, xh, rstd = _ln_rows(v_ref[...], g_ref[...], b_ref[...])
        s_ref[...] = (y * _sigmoid(y)).astype(_MXU)
        xh_ref[...] = xh
        rs_ref[...] = rstd

    row = lambda i: (i, 0)
    vec = pl.BlockSpec((1, C), lambda i: (0, 0))
    return pl.pallas_call(
        body, name=name, grid=(T // tm,),
        in_specs=[pl.BlockSpec((tm, C), row), vec, vec],
        out_specs=[pl.BlockSpec((tm, C), row), pl.BlockSpec((tm, C), row), pl.BlockSpec((tm, 1), row)],
        out_shape=[jax.ShapeDtypeStruct((T, C), _MXU), jax.ShapeDtypeStruct((T, C), F32),
                   jax.ShapeDtypeStruct((T, 1), F32)],
        compiler_params=_cp(("parallel",)),
    )(v, g, b)


def _ln_silu_bwd(ds, xh, rstd, g, b, *, name):
    T, C = ds.shape
    tm = _tile(T, 256)

    def body(ds_ref, xh_ref, rs_ref, g_ref, b_ref, dv_ref, dg_ref, db_ref):
        @pl.when(pl.program_id(0) == 0)
        def _():
            dg_ref[...] = jnp.zeros_like(dg_ref)
            db_ref[...] = jnp.zeros_like(db_ref)

        xh = xh_ref[...]
        gam = g_ref[...]
        y = xh * gam + b_ref[...]
        sig = _sigmoid(y)
        dln = ds_ref[...] * (sig * (1.0 + y * (1.0 - sig)))
        dv_ref[...] = _ln_bwd_rows(dln, xh, rs_ref[...], gam)
        dg_ref[...] += _fold8(dln * xh)
        db_ref[...] += _fold8(dln)

    row = lambda i: (i, 0)
    fixed = lambda i: (0, 0)
    vec = pl.BlockSpec((1, C), fixed)
    part = pl.BlockSpec((SUBLANES, C), fixed)
    return pl.pallas_call(
        body, name=name, grid=(T // tm,),
        in_specs=[pl.BlockSpec((tm, C), row), pl.BlockSpec((tm, C), row), pl.BlockSpec((tm, 1), row), vec, vec],
        out_specs=[pl.BlockSpec((tm, C), row), part, part],
        out_shape=[jax.ShapeDtypeStruct((T, C), F32)] + [jax.ShapeDtypeStruct((SUBLANES, C), F32)] * 2,
        compiler_params=_cp(("arbitrary",)),
    )(ds, xh, rstd, g, b)


def _glu_bwd(du, h1, *, name):
    T, C = du.shape
    il = C // 2
    tm = _tile(T, 256)

    def body(du_ref, h_ref, dh_ref, cs_ref):
        @pl.when(pl.program_id(0) == 0)
        def _():
            cs_ref[...] = jnp.zeros_like(cs_ref)

        for hb in range(2):
            a = h_ref[:, 2 * hb * il:(2 * hb + 1) * il]
            gate = h_ref[:, (2 * hb + 1) * il:(2 * hb + 2) * il]
            d = du_ref[:, hb * il:(hb + 1) * il]
            sig = _sigmoid(gate)
            da = d * sig
            dgate = d * a * sig * (1.0 - sig)
            dh_ref[:, 2 * hb * il:(2 * hb + 1) * il] = da.astype(_MXU)
            dh_ref[:, (2 * hb + 1) * il:(2 * hb + 2) * il] = dgate.astype(_MXU)
            cs_ref[:, 2 * hb * il:(2 * hb + 1) * il] += _fold8(da)
            cs_ref[:, (2 * hb + 1) * il:(2 * hb + 2) * il] += _fold8(dgate)

    row = lambda i: (i, 0)
    return pl.pallas_call(
        body, name=name, grid=(T // tm,),
        in_specs=[pl.BlockSpec((tm, C), row), pl.BlockSpec((tm, 2 * C), row)],
        out_specs=[pl.BlockSpec((tm, 2 * C), row), pl.BlockSpec((SUBLANES, 2 * C), lambda i: (0, 0))],
        out_shape=[jax.ShapeDtypeStruct((T, 2 * C), _MXU), jax.ShapeDtypeStruct((SUBLANES, 2 * C), F32)],
        compiler_params=_cp(("arbitrary",)),
    )(du, h1)


def _tril_mask(n):
    return lax.broadcasted_iota(jnp.int32, (n, n), 0) >= lax.broadcasted_iota(jnp.int32, (n, n), 1)


def _split_uv(t, il):
    u = jnp.concatenate([t[:, 0:il], t[:, 2 * il:3 * il]], axis=1)
    v = jnp.concatenate([t[:, il:2 * il], t[:, 3 * il:4 * il]], axis=1)
    return u, v


def _gmlp_gate_fwd(p, g, b, w_s, bsb, *, name):
    T, C2 = p.shape
    C = C2 // 2
    il = C // 2
    G, L, _ = w_s.shape
    assert G * L == C
    tm = _tile(T, 2 * L, L)

    def body(p_ref, g_ref, b_ref, ws_ref, bs_ref, us_ref, xh_ref, rs_ref, vn_ref, u_ref):
        z, _ = _gelu_parts(p_ref[...])
        u, v = _split_uv(z, il)
        vn, xh, rstd = _ln_rows(v, g_ref[...], b_ref[...])
        xh_ref[...] = xh
        rs_ref[...] = rstd
        vn_ref[...] = vn.astype(_MXU)
        u_ref[...] = u
        mask = _tril_mask(L)
        for gi in range(G):
            wc = jnp.where(mask, ws_ref[gi], 0.0).astype(_MXU)
            cols = slice(gi * L, (gi + 1) * L)
            for c in range(tm // L):
                rows = slice(c * L, (c + 1) * L)
                s = jnp.dot(wc, vn_ref[rows, cols], preferred_element_type=F32) + bs_ref[:, cols]
                us_ref[rows, cols] = (u_ref[rows, cols] * s).astype(_MXU)

    row = lambda i: (i, 0)
    fixed = lambda i: (0, 0)
    return pl.pallas_call(
        body, name=name, grid=(T // tm,),
        in_specs=[pl.BlockSpec((tm, C2), row), pl.BlockSpec((1, C), fixed), pl.BlockSpec((1, C), fixed),
                  pl.BlockSpec((G, L, L), lambda i: (0, 0, 0)), pl.BlockSpec((L, C), fixed)],
        out_specs=[pl.BlockSpec((tm, C), row), pl.BlockSpec((tm, C), row), pl.BlockSpec((tm, 1), row)],
        out_shape=[jax.ShapeDtypeStruct((T, C), _MXU), jax.ShapeDtypeStruct((T, C), F32),
                   jax.ShapeDtypeStruct((T, 1), F32)],
        scratch_shapes=[pltpu.VMEM((tm, C), _MXU), pltpu.VMEM((tm, C), F32)],
        compiler_params=_cp(("parallel",)),
    )(p, g, b, w_s, bsb)


def _gmlp_gate_bwd(dus, p, xh, rstd, g, b, w_s, bsb, *, name):
    T, C2 = p.shape
    C = C2 // 2
    il = C // 2
    G, L, _ = w_s.shape
    tm = _tile(T, 2 * L, L)

    def body(dus_ref, p_ref, xh_ref, rs_ref, g_ref, b_ref, ws_ref, bs_ref,
             dp_ref, dg_ref, db_ref, cs_ref, dws_ref, dbs_ref, vn_ref, u_ref, dvn_ref, du_ref):
        @pl.when(pl.program_id(0) == 0)
        def _():
            dg_ref[...] = jnp.zeros_like(dg_ref)
            db_ref[...] = jnp.zeros_like(db_ref)
            cs_ref[...] = jnp.zeros_like(cs_ref)
            dws_ref[...] = jnp.zeros_like(dws_ref)
            dbs_ref[...] = jnp.zeros_like(dbs_ref)

        z, gp = _gelu_parts(p_ref[...])
        u, _ = _split_uv(z, il)
        xh = xh_ref[...]
        gam = g_ref[...]
        vn_ref[...] = (xh * gam + b_ref[...]).astype(_MXU)
        u_ref[...] = u
        mask = _tril_mask(L)
        for gi in range(G):
            wc = jnp.where(mask, ws_ref[gi], 0.0).astype(_MXU)
            cols = slice(gi * L, (gi + 1) * L)
            for c in range(tm // L):
                rows = slice(c * L, (c + 1) * L)
                vnb = vn_ref[rows, cols]
                s = jnp.dot(wc, vnb, preferred_element_type=F32) + bs_ref[:, cols]
                d = dus_ref[rows, cols]
                du_ref[rows, cols] = d * s
                ds = d * u_ref[rows, cols]
                dbs_ref[:, cols] += ds
                dsb = ds.astype(_MXU)
                dw = lax.dot_general(dsb, vnb, (((1,), (1,)), ((), ())), preferred_element_type=F32)
                dws_ref[gi] += jnp.where(mask, dw, 0.0)
                dvn_ref[rows, cols] = lax.dot_general(wc, dsb, (((0,), (0,)), ((), ())), preferred_element_type=F32)
        dvn = dvn_ref[...]
        dg_ref[...] += _fold8(dvn * xh)
        db_ref[...] += _fold8(dvn)
        dv = _ln_bwd_rows(dvn, xh, rs_ref[...], gam)
        du = du_ref[...]
        for hb in range(2):
            for part, src in ((0, du), (1, dv)):
                lo = (2 * hb + part) * il
                dp = src[:, hb * il:(hb + 1) * il] * gp[:, lo:lo + il]
                dp_ref[:, lo:lo + il] = dp.astype(_MXU)
                cs_ref[:, lo:lo + il] += _fold8(dp)

    row = lambda i: (i, 0)
    fixed = lambda i: (0, 0)
    part_c = pl.BlockSpec((SUBLANES, C), fixed)
    return pl.pallas_call(
        body, name=name, grid=(T // tm,),
        in_specs=[pl.BlockSpec((tm, C), row), pl.BlockSpec((tm, C2), row), pl.BlockSpec((tm, C), row),
                  pl.BlockSpec((tm, 1), row), pl.BlockSpec((1, C), fixed), pl.BlockSpec((1, C), fixed),
                  pl.BlockSpec((G, L, L), lambda i: (0, 0, 0)), pl.BlockSpec((L, C), fixed)],
        out_specs=[pl.BlockSpec((tm, C2), row), part_c, part_c, pl.BlockSpec((SUBLANES, C2), fixed),
                   pl.BlockSpec((G, L, L), lambda i: (0, 0, 0)), pl.BlockSpec((L, C), fixed)],
        out_shape=[jax.ShapeDtypeStruct((T, C2), _MXU), jax.ShapeDtypeStruct((SUBLANES, C), F32),
                   jax.ShapeDtypeStruct((SUBLANES, C), F32), jax.ShapeDtypeStruct((SUBLANES, C2), F32),
                   jax.ShapeDtypeStruct((G, L, L), F32), jax.ShapeDtypeStruct((L, C), F32)],
        scratch_shapes=[pltpu.VMEM((tm, C), _MXU), pltpu.VMEM((tm, C), F32), pltpu.VMEM((tm, C), F32),
                        pltpu.VMEM((tm, C), F32)],
        compiler_params=_cp(("arbitrary",)),
    )(dus, p, xh, rstd, g, b, w_s, bsb)


def _ffn_conv(h, prev8, w_ref, b_ref):
    h1 = _shift_down(prev8, h, 1)
    h2 = _shift_down(prev8, h, 2)
    hc = w_ref[pl.ds(2, 1), :] * h + w_ref[pl.ds(1, 1), :] * h1 + w_ref[pl.ds(0, 1), :] * h2 + b_ref[...]
    return hc, h1, h2


def _ffn_up_fwd(xb, w, wl, b_up, w_dw, b_dw, *, S, name):
    T, D = xb.shape
    N = w.shape[-1]
    tn = N // N_CHIPS
    tm = _tile(S, 256)
    spt = S // tm

    def body(x_ref, w_ref, bu_ref, wd_ref, bd_ref, h_ref, f_ref, carry_ref):
        i = pl.program_id(1)

        @pl.when(i % spt == 0)
        def _():
            carry_ref[...] = jnp.zeros_like(carry_ref)

        h = jnp.dot(x_ref[...].astype(_MXU), w_ref[...].astype(_MXU), preferred_element_type=F32) + bu_ref[...]
        hq = h.astype(_HDT)
        h_ref[...] = hq
        h = hq.astype(F32)
        hc, _, _ = _ffn_conv(h, carry_ref[...], wd_ref, bd_ref)
        carry_ref[...] = h[tm - SUBLANES:tm]
        gte = hc[:, :tn]
        f_ref[...] = (gte * _sigmoid(gte) * hc[:, tn:]).astype(_MXU)

    pair = lambda j, i: (0, j)
    return pl.pallas_call(
        body, name=name, grid=(2, T // tm),
        in_specs=[pl.BlockSpec((tm, D), lambda j, i: (i, 0)),
                  pl.BlockSpec((None, D, 2 * tn), lambda j, i: (wl, 0, j)),
                  pl.BlockSpec((1, 2 * tn), pair), pl.BlockSpec((SUBLANES, 2 * tn), pair),
                  pl.BlockSpec((1, 2 * tn), pair)],
        out_specs=[pl.BlockSpec((tm, 2 * tn), lambda j, i: (i, j)), pl.BlockSpec((tm, tn), lambda j, i: (i, j))],
        out_shape=[jax.ShapeDtypeStruct((T, N), _HDT), jax.ShapeDtypeStruct((T, N // 2), _MXU)],
        scratch_shapes=[pltpu.VMEM((SUBLANES, 2 * tn), F32)],
        compiler_params=_cp(("parallel", "arbitrary")),
    )(xb, w, b_up, w_dw, b_dw)


def _ffn_bwd(dzb, w_down, wl, hs, w_dw, b_dw, *, S, name):
    T, D = dzb.shape
    N = hs.shape[1]
    tn = N // N_CHIPS
    tm = _tile(S, 256)
    spt = S // tm
    nt = T // tm
    hal = 16

    def body(dz_ref, wd_ref, h_ref, halo_ref, wc_ref, bc_ref, dh_ref, cs_ref, dw_ref, db_ref, carry_ref):
        i = pl.program_id(1)
        ii = nt - 1 - i

        @pl.when(i == 0)
        def _():
            cs_ref[...] = jnp.zeros_like(cs_ref)
            dw_ref[...] = jnp.zeros_like(dw_ref)
            db_ref[...] = jnp.zeros_like(db_ref)

        df = lax.dot_general(dz_ref[...].astype(_MXU), wd_ref[...].astype(_MXU), (((1,), (1,)), ((), ())),
                             preferred_element_type=F32)
        h = h_ref[...].astype(F32)
        prev8 = halo_ref[...].astype(F32)[hal - SUBLANES:hal]
        prev8 = jnp.where(ii % spt == 0, 0.0, prev8)
        hc, h1, h2 = _ffn_conv(h, prev8, wc_ref, bc_ref)
        gte, val = hc[:, :tn], hc[:, tn:]
        sig = _sigmoid(gte)
        dval = df * (gte * sig)
        dg = df * val * (sig * (1.0 + gte * (1.0 - sig)))
        dhc = jnp.concatenate([dg, dval], axis=1)
        db_ref[...] += _fold8(dhc)
        dw_ref[2] += _fold8(dhc * h)
        dw_ref[1] += _fold8(dhc * h1)
        dw_ref[0] += _fold8(dhc * h2)
        nxt = jnp.where((ii + 1) % spt == 0, 0.0, carry_ref[...])
        dh = (wc_ref[pl.ds(2, 1), :] * dhc + wc_ref[pl.ds(1, 1), :] * _shift_up(dhc, nxt, 1)
              + wc_ref[pl.ds(0, 1), :] * _shift_up(dhc, nxt, 2))
        carry_ref[...] = dhc[0:SUBLANES]
        cs_ref[...] += _fold8(dh)
        dh_ref[...] = dh.astype(_MXU)

    pair = lambda j, i: (0, j)
    rev = lambda j, i: (nt - 1 - i, j)
    return pl.pallas_call(
        body, name=name, grid=(2, nt),
        in_specs=[pl.BlockSpec((tm, D), lambda j, i: (nt - 1 - i, 0)),
                  pl.BlockSpec((None, tn, D), lambda j, i: (wl, j, 0)),
                  pl.BlockSpec((tm, 2 * tn), rev),
                  pl.BlockSpec((hal, 2 * tn), lambda j, i: (jnp.maximum((nt - 1 - i) * (tm // hal) - 1, 0), j)),
                  pl.BlockSpec((SUBLANES, 2 * tn), pair), pl.BlockSpec((1, 2 * tn), pair)],
        out_specs=[pl.BlockSpec((tm, 2 * tn), rev), pl.BlockSpec((SUBLANES, 2 * tn), pair),
                   pl.BlockSpec((3, SUBLANES, 2 * tn), lambda j, i: (0, 0, j)),
                   pl.BlockSpec((SUBLANES, 2 * tn), pair)],
        out_shape=[jax.ShapeDtypeStruct((T, N), _MXU), jax.ShapeDtypeStruct((SUBLANES, N), F32),
                   jax.ShapeDtypeStruct((3, SUBLANES, N), F32), jax.ShapeDtypeStruct((SUBLANES, N), F32)],
        scratch_shapes=[pltpu.VMEM((SUBLANES, 2 * tn), F32)],
        compiler_params=_cp(("parallel", "arbitrary")),
    )(dzb, w_down, hs, hs, w_dw, b_dw)


def _sum_pieces(g, r, me, layer, acc, n_layers, *, name):
    _, pr, pc = g.shape
    tr = _tile(pr, 128)

    def body(me_ref, g_ref, r_ref, *rest):
        o_ref = rest[-1]
        total = g_ref[...].astype(F32)
        for s in range(N_DEV - 1):
            total = total + r_ref[s].astype(F32)
        o_ref[...] = total

    in_specs = [pl.BlockSpec((None, tr, pc), lambda i, me_ref: (me_ref[0], i, 0)),
                pl.BlockSpec((N_DEV - 1, tr, pc), lambda i, me_ref: (0, i, 0))]
    operands = [me, g, r]
    aliases = {}
    if acc is not None:
        in_specs.append(ANY)
        operands.append(acc)
        aliases = {3: 0}
    return pl.pallas_call(
        body, name=name,
        grid_spec=pltpu.PrefetchScalarGridSpec(
            num_scalar_prefetch=1, grid=(pr // tr,), in_specs=in_specs,
            out_specs=pl.BlockSpec((None, tr, pc), lambda i, me_ref: (layer, i, 0))),
        out_shape=jax.ShapeDtypeStruct((n_layers, pr, pc), F32),
        input_output_aliases=aliases,
        compiler_params=_cp(("parallel",)),
    )(*operands)


def _adam_math(w, g, m, v):
    bc1 = 1.0 - ADAM_B1 ** ADAM_STEP
    bc2 = 1.0 - ADAM_B2 ** ADAM_STEP
    m = ADAM_B1 * m + (1.0 - ADAM_B1) * g
    v = ADAM_B2 * v + (1.0 - ADAM_B2) * (g * g)
    return -ADAM_LR * ((m / bc1) / (jnp.sqrt(v / bc2) + ADAM_EPS) + ADAM_WD * w), m, v


def _adam(w, g, m, v, *, name):
    R, C = w.shape
    tr = _tile(R, 256)

    def body(w_ref, g_ref, m_ref, v_ref, d_ref, mo_ref, vo_ref):
        d_ref[...], mo_ref[...], vo_ref[...] = _adam_math(w_ref[...], g_ref[...], m_ref[...], v_ref[...])

    spec = pl.BlockSpec((tr, C), lambda i: (i, 0))
    return pl.pallas_call(
        body, name=name, grid=(R // tr,), in_specs=[spec] * 4, out_specs=[spec] * 3,
        out_shape=[jax.ShapeDtypeStruct((R, C), F32)] * 3,
        compiler_params=_cp(("parallel",)),
    )(w, g, m, v)


def _adam_halves(w, own, got, m, v, core, *, name):
    L, R, C = w.shape
    rh = R // 2
    tr = _tile(rh, 256)
    nt = rh // tr

    def body(c_ref, w_ref, own_ref, got_ref, m_ref, v_ref, g_ref, d_ref, mo_ref, vo_ref):
        g = jnp.where(pl.program_id(1) == c_ref[0], own_ref[...], got_ref[...])
        g_ref[...] = g
        d_ref[...], mo_ref[...], vo_ref[...] = _adam_math(w_ref[...], g, m_ref[...], v_ref[...])

    full = pl.BlockSpec((None, tr, C), lambda l, h, t, c_ref: (l, h * nt + t, 0))
    half = pl.BlockSpec((None, tr, C), lambda l, h, t, c_ref: (l, t, 0))
    return pl.pallas_call(
        body, name=name,
        grid_spec=pltpu.PrefetchScalarGridSpec(
            num_scalar_prefetch=1, grid=(L, 2, nt), in_specs=[full, half, half, full, full], out_specs=[full] * 4),
        out_shape=[jax.ShapeDtypeStruct((L, R, C), F32)] * 4,
        compiler_params=_cp(("parallel", "parallel", "parallel")),
    )(core, w, own, got, m, v)


def _remote(src, dst, send, recv, dev):
    return pltpu.make_async_remote_copy(src_ref=src, dst_ref=dst, send_sem=send, recv_sem=recv,
                                        device_id=dev, device_id_type=MESH)


def _place_w(shard, pos, layer, *, axis, name):
    _, R, C = shard.shape
    tr = _tile(R, 512, 16)
    nt = R // tr
    if axis == 2:
        out_shape = (1, R, N_CHIPS * C)
        out_map = lambda t, q: (0, t, q[0])
    else:
        out_shape = (1, N_CHIPS * R, C)
        out_map = lambda t, q: (0, q[0] * nt + t, 0)

    def body(q_ref, s_ref, o_ref):
        o_ref[...] = s_ref[...].astype(_WIRE)

    return pl.pallas_call(
        body, name=name,
        grid_spec=pltpu.PrefetchScalarGridSpec(
            num_scalar_prefetch=1, grid=(nt,),
            in_specs=[pl.BlockSpec((None, tr, C), lambda t, q: (layer, t, 0))],
            out_specs=pl.BlockSpec((None, tr, C), out_map)),
        out_shape=jax.ShapeDtypeStruct(out_shape, _WIRE),
        compiler_params=_cp(("parallel",)),
    )(pos, shard)


def _ag_window(ref, kind, px, py, h):
    axis, perm = kind
    q = 2 * px + py
    if perm:
        q = _perm_idx(q)
    if axis == 2:
        R, C = ref.shape[1], ref.shape[2] // N_CHIPS
        rh = R // 2
        return ref.at[:, pl.ds(pl.multiple_of(h * rh, 16), rh), pl.ds(pl.multiple_of(q * C, LANES), C)]
    R = ref.shape[1] // N_CHIPS
    rh = R // 2
    return ref.at[:, pl.ds(pl.multiple_of(q * R + h * rh, 16), rh), :]


def _ag_ici_copies(refs, kinds, send, recv):
    x, y, c = lax.axis_index("x"), lax.axis_index("y"), lax.axis_index("c")
    chips = [(1 - x, y), (x, 1 - y), (1 - x, 1 - y)]
    sends, recvs = [], []
    for a, (ref, kind) in enumerate(zip(refs, kinds)):
        own = _ag_window(ref, kind, x, y, c)
        for i, (px, py) in enumerate(chips):
            k = 3 * a + i
            sends.append(_remote(own, own, send.at[k], recv.at[k], (px, py, c)))
            recvs.append(_remote(own, _ag_window(ref, kind, px, py, c), send.at[k], recv.at[k], (px, py, c)))
    return sends, recvs


def _ag_start(arrs, kinds, after, *, name):
    n = len(arrs)

    def body(*refs):
        in_refs = refs[:n]
        send, recv = refs[n + len(after)], refs[n + len(after) + 1]
        token = refs[-1]
        sends, _ = _ag_ici_copies(in_refs, kinds, send, recv)
        for cp in sends:
            cp.start()
        token[...] = jnp.zeros_like(token)

    sems = pltpu.SemaphoreType.DMA((3 * n,))
    out = pl.pallas_call(
        body, name=name,
        out_shape=(sems, sems) + tuple(pltpu.HBM(a.shape, a.dtype) for a in arrs)
        + (jax.ShapeDtypeStruct((SUBLANES, LANES), F32),),
        in_specs=(HBM,) * n + (ANY,) * len(after),
        out_specs=(SEMS, SEMS) + (HBM,) * n + (pl.BlockSpec(memory_space=pltpu.VMEM),),
        input_output_aliases={a: 2 + a for a in range(n)},
        compiler_params=pltpu.CompilerParams(has_side_effects=EFFECT),
    )(*[pltpu.with_memory_space_constraint(a, pltpu.HBM) for a in arrs], *after)
    return out[0], out[1], list(out[2:2 + n]), out[-1]


def _ag_wait(send, recv, arrs, kinds, after, *, name):
    n = len(arrs)

    def body(*refs):
        in_refs = refs[:n]
        send, recv = refs[n], refs[n + 1]
        sends, recvs = _ag_ici_copies(in_refs, kinds, send, recv)
        for cp in sends:
            cp.wait_send()
        for cp in recvs:
            cp.wait_recv()

    out = pl.pallas_call(
        body, name=name,
        out_shape=tuple(pltpu.HBM(a.shape, a.dtype) for a in arrs),
        in_specs=(HBM,) * n + (SEMS, SEMS) + (ANY,) * len(after), out_specs=(HBM,) * n,
        input_output_aliases={a: a for a in range(n)},
        compiler_params=pltpu.CompilerParams(has_side_effects=EFFECT),
    )(*arrs, send, recv, *after)
    return list(out)


def _ag_forward(arrs, kinds, *, name):
    n = len(arrs)

    def body(*refs):
        o_refs, send, recv = refs[n:2 * n], refs[2 * n], refs[2 * n + 1]
        x, y, c = lax.axis_index("x"), lax.axis_index("y"), lax.axis_index("c")
        chips = [(1 - x, y), (x, 1 - y), (1 - x, 1 - y)]
        sib = (x, y, 1 - c)
        sends, recvs = [], []
        for a, (ref, kind) in enumerate(zip(o_refs, kinds)):
            for i, (px, py) in enumerate(chips):
                k = 3 * a + i
                got = _ag_window(ref, kind, px, py, c)
                cp = _remote(got, got, send.at[k], recv.at[k], sib)
                cp.start()
                sends.append(cp)
                recvs.append(_remote(got, _ag_window(ref, kind, px, py, 1 - c), send.at[k], recv.at[k], sib))
        for cp in recvs:
            cp.wait_recv()
        for cp in sends:
            cp.wait_send()

    out = pl.pallas_call(
        body, name=name, in_specs=[ANY] * n, out_specs=[ANY] * n,
        out_shape=[jax.ShapeDtypeStruct(a.shape, a.dtype) for a in arrs],
        input_output_aliases={a: a for a in range(n)},
        scratch_shapes=[pltpu.SemaphoreType.DMA((3 * n,)), pltpu.SemaphoreType.DMA((3 * n,))],
    )(*arrs)
    return list(out)


def _flip(x, y, c, f):
    return ((1 - x) if f & 4 else x, (1 - y) if f & 2 else y, (1 - c) if f & 1 else c)


def _rs_copies(g_ref, land_ref, send, recv):
    x, y, c = lax.axis_index("x"), lax.axis_index("y"), lax.axis_index("c")
    cps = []
    for f in range(1, N_DEV):
        tx, ty, tcx = _flip(x, y, c, f)
        cps.append(_remote(g_ref.at[4 * tx + 2 * ty + tcx], land_ref.at[f - 1], send.at[f - 1], recv.at[f - 1],
                           (tx, ty, tcx)))
    return cps


def _rs_start(g, *, name):
    _, pr, pc = g.shape
    land_shape = (N_DEV - 1, pr, pc)

    def body(g_ref, land_ref, send, recv, g_thru, land_thru, token):
        for cp in _rs_copies(g_ref, land_ref, send, recv):
            cp.start()
        token[...] = jnp.zeros_like(token)

    sems = pltpu.SemaphoreType.DMA((N_DEV - 1,))
    return pl.pallas_call(
        body, name=name,
        out_shape=(sems, sems, pltpu.HBM(g.shape, g.dtype), pltpu.HBM(land_shape, g.dtype),
                   jax.ShapeDtypeStruct((SUBLANES, LANES), F32)),
        in_specs=(HBM, HBM), out_specs=(SEMS, SEMS, HBM, HBM, pl.BlockSpec(memory_space=pltpu.VMEM)),
        input_output_aliases={0: 2, 1: 3},
        compiler_params=pltpu.CompilerParams(has_side_effects=EFFECT),
    )(pltpu.with_memory_space_constraint(g, pltpu.HBM),
      pltpu.with_memory_space_constraint(lax.empty(land_shape, g.dtype), pltpu.HBM))


def _rs_wait(send, recv, g_thru, land_thru, after, *, name):
    def body(g_ref, land_ref, send, recv, after_ref, g_out, land_out):
        cps = _rs_copies(g_ref, land_ref, send, recv)
        for cp in cps:
            cp.wait_send()
        for cp in cps:
            cp.wait_recv()

    return pl.pallas_call(
        body, name=name,
        out_shape=(pltpu.HBM(g_thru.shape, g_thru.dtype), pltpu.HBM(land_thru.shape, land_thru.dtype)),
        in_specs=(HBM, HBM, SEMS, SEMS, ANY), out_specs=(HBM, HBM), input_output_aliases={0: 0, 1: 1},
        compiler_params=pltpu.CompilerParams(has_side_effects=EFFECT),
    )(g_thru, land_thru, send, recv, after)


def _pair_exchange(own, *, name):
    def body(own_ref, got_ref, send, recv):
        x, y, c = lax.axis_index("x"), lax.axis_index("y"), lax.axis_index("c")
        cp = _remote(own_ref, got_ref, send, recv, (x, y, 1 - c))
        cp.start()
        cp.wait_recv()
        cp.wait_send()

    return pl.pallas_call(
        body, name=name, in_specs=[ANY], out_specs=ANY, out_shape=jax.ShapeDtypeStruct(own.shape, own.dtype),
        scratch_shapes=[pltpu.SemaphoreType.DMA, pltpu.SemaphoreType.DMA],
    )(own)


def _allreduce_flat(vec, *, name):
    n = vec.shape[0]
    unit = N_DEV * SUBLANES * LANES
    npad = -(-n // unit) * unit
    rows = npad // (N_DEV * LANES)
    xin = jnp.pad(vec, (0, npad - n)).reshape(N_DEV, rows, LANES)

    def body(x_ref, y_ref, a_ref, send_a, recv_a, send_b, recv_b):
        x, y, c = lax.axis_index("x"), lax.axis_index("y"), lax.axis_index("c")
        me = 4 * x + 2 * y + c
        a_ref[me] = x_ref[me]
        sends, recvs = [], []
        for f in range(1, N_DEV):
            dev = _flip(x, y, c, f)
            t = 4 * dev[0] + 2 * dev[1] + dev[2]
            cp = _remote(x_ref.at[t], a_ref.at[me], send_a.at[f - 1], recv_a.at[f - 1], dev)
            cp.start()
            sends.append(cp)
            recvs.append(_remote(x_ref.at[me], a_ref.at[t], send_a.at[f - 1], recv_a.at[f - 1], dev))
        for cp in recvs:
            cp.wait_recv()
        for cp in sends:
            cp.wait_send()
        acc = a_ref[0]
        for s in range(1, N_DEV):
            acc = acc + a_ref[s]
        y_ref[me] = acc
        sends, recvs = [], []
        for f in range(1, N_DEV):
            dev = _flip(x, y, c, f)
            t = 4 * dev[0] + 2 * dev[1] + dev[2]
            cp = _remote(y_ref.at[me], y_ref.at[me], send_b.at[f - 1], recv_b.at[f - 1], dev)
            cp.start()
            sends.append(cp)
            recvs.append(_remote(y_ref.at[me], y_ref.at[t], send_b.at[f - 1], recv_b.at[f - 1], dev))
        for cp in recvs:
            cp.wait_recv()
        for cp in sends:
            cp.wait_send()

    vm = pl.BlockSpec(memory_space=pltpu.VMEM)
    out = pl.pallas_call(
        body, name=name, in_specs=[vm], out_specs=vm,
        out_shape=jax.ShapeDtypeStruct((N_DEV, rows, LANES), F32),
        scratch_shapes=[pltpu.VMEM((N_DEV, rows, LANES), F32)] + [pltpu.SemaphoreType.DMA((N_DEV - 1,))] * 4,
        compiler_params=_cp(),
    )(xin)
    return out.reshape(npad)[:n]


def _perm_cols(v, blocks=N_CHIPS):
    lead, n = v.shape[:-1], v.shape[-1]
    return v.reshape(lead + (blocks, n // blocks))[..., PERM, :].reshape(lead + (n,))


def _pack(arrs):
    return jnp.concatenate([a.reshape(-1).astype(F32) for a in arrs])


def _unpack(flat, shapes):
    out, pos = [], 0
    for s in shapes:
        n = 1
        for d in s:
            n *= d
        out.append(flat[pos:pos + n].reshape(s))
        pos += n
    return out


def kernel(x, conv_w_in, conv_b_in, conv_w_dw, conv_b_dw, conv_ln_g, conv_ln_b, conv_w_out, conv_b_out, gmlp_w_in, gmlp_b_in, gmlp_ln_g, gmlp_ln_b, gmlp_w_s, gmlp_b_s, gmlp_w_out, gmlp_b_out, ffn_w_up, ffn_b_up, ffn_w_dw, ffn_b_dw, ffn_w_down, ffn_b_down, norm1_g, norm1_b, norm2_g, norm2_b, loss_target, m_conv_w_in, m_conv_b_in, m_conv_w_dw, m_conv_b_dw, m_conv_ln_g, m_conv_ln_b, m_conv_w_out, m_conv_b_out, m_gmlp_w_in, m_gmlp_b_in, m_gmlp_ln_g, m_gmlp_ln_b, m_gmlp_w_s, m_gmlp_b_s, m_gmlp_w_out, m_gmlp_b_out, m_ffn_w_up, m_ffn_b_up, m_ffn_w_dw, m_ffn_b_dw, m_ffn_w_down, m_ffn_b_down, m_norm1_g, m_norm1_b, m_norm2_g, m_norm2_b, v_conv_w_in, v_conv_b_in, v_conv_w_dw, v_conv_b_dw, v_conv_ln_g, v_conv_ln_b, v_conv_w_out, v_conv_b_out, v_gmlp_w_in, v_gmlp_b_in, v_gmlp_ln_g, v_gmlp_ln_b, v_gmlp_w_s, v_gmlp_b_s, v_gmlp_w_out, v_gmlp_b_out, v_ffn_w_up, v_ffn_b_up, v_ffn_w_dw, v_ffn_b_dw, v_ffn_w_down, v_ffn_b_down, v_norm1_g, v_norm1_b, v_norm2_g, v_norm2_b):
    P = dict(locals())
    WEIGHTS = ['conv_w_in', 'conv_b_in', 'conv_w_dw', 'conv_b_dw', 'conv_ln_g', 'conv_ln_b', 'conv_w_out',
               'conv_b_out', 'gmlp_w_in', 'gmlp_b_in', 'gmlp_ln_g', 'gmlp_ln_b', 'gmlp_w_s', 'gmlp_b_s',
               'gmlp_w_out', 'gmlp_b_out', 'ffn_w_up', 'ffn_b_up', 'ffn_w_dw', 'ffn_b_dw', 'ffn_w_down',
               'ffn_b_down', 'norm1_g', 'norm1_b', 'norm2_g', 'norm2_b']
    BIG = ['conv_w_in', 'conv_w_out', 'gmlp_w_in', 'gmlp_w_out', 'ffn_w_up', 'ffn_w_down']
    SMALL_SHARDED = {'conv_w_dw': 2, 'gmlp_b_in': 1, 'gmlp_ln_g': 1, 'gmlp_ln_b': 1, 'gmlp_b_out': 1, 'ffn_w_dw': 2}

    B, S, D = x.shape
    T = B * S
    depth = norm1_g.shape[0]
    alpha = (2.0 * depth) ** 0.25
    C = conv_w_out.shape[-1]
    F2 = ffn_b_up.shape[-1]
    G, L = gmlp_w_s.shape[1], gmlp_w_s.shape[2]
    xi, yi, ci = lax.axis_index("x"), lax.axis_index("y"), lax.axis_index("c")
    shard = 2 * xi + yi

    i32 = lambda v: jnp.reshape(v, (1,)).astype(jnp.int32)
    pos_plain, pos_perm = i32(shard), i32(_perm_idx(shard))
    me_id, core_id = i32(4 * xi + 2 * yi + ci), i32(ci)

    groups = []
    for i in range(depth):
        mix = 'conv' if i % 2 == 0 else 'gmlp'
        groups.append((f"{mix}{i // 2}", [(mix + '_w_in', i // 2, 2, True), (mix + '_w_out', i // 2, 1, False)]))
        groups.append((f"ffn{i}", [('ffn_w_up', i, 2, True), ('ffn_w_down', i, 1, False)]))
    sm_names = list(SMALL_SHARDED)
    sm_shapes = [P[n].shape for n in sm_names]
    mine = _pack([P[n] for n in sm_names]) * (ci == 0).astype(F32)
    buf = jnp.zeros((N_CHIPS, mine.shape[0]), F32)
    buf = lax.dynamic_update_slice(buf, mine[None], (shard, 0))
    gathered = _allreduce_flat(buf.reshape(-1), name="ag_small").reshape(N_CHIPS, -1)

    started, order = {}, [gathered]
    for gname, members in groups:
        placed = [_place_w(P[n], pos_perm if perm else pos_plain, l, axis=axis, name=f"place_{n}_{l}")
                  for n, l, axis, perm in members]
        kinds = [(axis, perm) for _, _, axis, perm in members]
        send, recv, arrs, token = _ag_start(placed, kinds, order, name=f"ag_start_{gname}")
        order = [token]
        started[gname] = (send, recv, arrs, kinds, [(n, l) for n, l, _, _ in members])
    wts = {}

    def arrive(gname, after):
        send, recv, arrs, kinds, keys = started[gname]
        arrs = _ag_wait(send, recv, arrs, kinds, after, name=f"ag_wait_{gname}")
        arrs = _ag_forward(arrs, kinds, name=f"ag_fwd_{gname}")
        wts.update(zip(keys, arrs))

    full = {}
    for n, parts in zip(sm_names, zip(*[_unpack(gathered[k], sm_shapes) for k in range(N_CHIPS)])):
        full[n] = jnp.concatenate(parts, axis=SMALL_SHARDED[n])
    for n in WEIGHTS:
        if n not in BIG and n not in full:
            full[n] = P[n]

    assert G * L == C, "a gMLP group must be as wide as a chunk is long"

    def row(v):
        return v.reshape(1, -1)

    def pad_rows(v, r):
        return jnp.pad(v, ((0, r - v.shape[0]), (0, 0)))

    xf = x.reshape(T, D)
    saved = []
    cur, cur_b = xf, xf.astype(_MXU)
    for i in range(depth):
        j = i // 2
        sv = {'x': cur, 'xb': cur_b}
        arrive(groups[2 * i][0], order if i == 0 else [cur_b])
        if i % 2 == 0:
            b_in = row(_perm_cols(full['conv_b_in'][j]))
            h1 = _mm(cur_b, wts['conv_w_in', j], bl=0, bias=b_in, tm=_tile(T, 512), tn=_tile(2 * C, 1024, LANES),
                     tk=D, name=f"conv_in_{j}", n_outer=True)
            wdw = pad_rows(full['conv_w_dw'][j], CONV_TAPS_PAD)
            dwo = _conv_fwd(h1, wdw, row(full['conv_b_dw'][j]), B=B, S=S, name=f"conv_dw_{j}")
            s_act, xhc, rsc = _ln_silu_fwd(dwo, row(full['conv_ln_g'][j]), row(full['conv_ln_b'][j]),
                                           name=f"conv_ln_{j}")
            sv.update(h1=h1, wdw=wdw, act=s_act, xhc=xhc, rsc=rsc)
            y1 = _mm_res_ln(s_act, wts['conv_w_out', j], 0, row(full['conv_b_out'][j]), cur, alpha, row(norm1_g[i]),
                            row(norm1_b[i]), name=f"conv_out_ln_{j}")
        else:
            b_in = row(_perm_cols(full['gmlp_b_in'][j]))
            pre = _mm(cur_b, wts['gmlp_w_in', j], bl=0, bias=b_in, tm=_tile(T, 512), tn=_tile(2 * C, 1024, LANES),
                      tk=D, name=f"gmlp_in_{j}", n_outer=True)
            bsb = jnp.repeat(gmlp_b_s[j].T, L, axis=1)
            us, xhv, rsv = _gmlp_gate_fwd(pre, row(full['gmlp_ln_g'][j]), row(full['gmlp_ln_b'][j]), gmlp_w_s[j],
                                          bsb, name=f"gmlp_gate_{j}")
            sv.update(pre=pre, bsb=bsb, act=us, xhv=xhv, rsv=rsv)
            y1 = _mm_res_ln(us, wts['gmlp_w_out', j], 0, row(full['gmlp_b_out'][j]), cur, alpha, row(norm1_g[i]),
                            row(norm1_b[i]), name=f"gmlp_out_ln_{j}")
        x1, x1b, xh1, rs1 = y1
        arrive(groups[2 * i + 1][0], [x1b])
        wdw3 = pad_rows(_perm_cols(full['ffn_w_dw'][i]), SUBLANES)
        bdw3 = row(_perm_cols(ffn_b_dw[i]))
        hs, f_act = _ffn_up_fwd(x1b, wts['ffn_w_up', i], 0, row(_perm_cols(ffn_b_up[i])), wdw3, bdw3, S=S,
                                name=f"ffn_up_{i}")
        x2, x2b, xh2, rs2 = _mm_res_ln(f_act, wts['ffn_w_down', i], 0, row(ffn_b_down[i]), x1, alpha, row(norm2_g[i]),
                                       row(norm2_b[i]), name=f"ffn_down_ln_{i}")
        sv.update(x1=x1, x1b=x1b, xh1=xh1, rs1=rs1, hs=hs, f=f_act, wdw3=wdw3, bdw3=bdw3, xh2=xh2, rs2=rs2)
        saved.append(sv)
        cur, cur_b = x2, x2b

    sg = {n: [None] * full[n].shape[0] for n in WEIGHTS if n not in BIG}
    inflight = {n: [None] * P[n].shape[0] for n in BIG}
    deps = []
    tgt = loss_target.reshape(T, D)
    dcur = None
    loss_part = None
    tk_t = _tile(T, 2048)

    def wgrad(n, l, a_, b_, **kw):
        g = _mm(a_, b_, ta=True, out_dtype=_WIRE, tk=tk_t, name=f"{n}_dw_{l}", deps=deps, **kw)
        send, recv, g_thru, land, token = _rs_start(g, name=f"rs_start_{n}_{l}")
        inflight[n][l] = (send, recv, g_thru, land)
        deps.append(token)

    for i in reversed(range(depth)):
        j = i // 2
        sv = saved[i]
        if dcur is None:
            dz2, dz2b, dg, db, cs, ls = _ln_bwd(cur, sv['xh2'], sv['rs2'], row(norm2_g[i]), target=tgt,
                                                name=f"ln2_bwd_head_{i}")
            loss_part = ls
        else:
            dz2, dz2b, dg, db, cs = _ln_bwd(dcur, sv['xh2'], sv['rs2'], row(norm2_g[i]), name=f"ln2_bwd_{i}")
        sg['norm2_g'][i], sg['norm2_b'][i], sg['ffn_b_down'][i] = dg.sum(0), db.sum(0), cs.sum(0)
        Fh = F2 // 2
        wgrad('ffn_w_down', i, sv['f'], dz2b, tm=Fh // 2, tn=_tile(D, 1024, LANES), pieces=('row',))
        dh, csu, dwd, dbd = _ffn_bwd(dz2b, wts['ffn_w_down', i], 0, sv['hs'], sv['wdw3'], sv['bdw3'], S=S,
                                     name=f"ffn_bwd_{i}")
        sg['ffn_b_up'][i] = _perm_cols(csu.sum(0))
        sg['ffn_w_dw'][i] = _perm_cols(dwd.sum(1))
        sg['ffn_b_dw'][i] = _perm_cols(dbd.sum(0))
        wgrad('ffn_w_up', i, sv['x1b'], dh, tm=D, tn=F2 // N_CHIPS, pieces=('col', True))
        dx1 = _mm(dh, wts['ffn_w_up', i], bl=0, tb=True, res=dz2, res_scale=alpha, tm=_tile(T, 512),
                  tn=_tile(D, 1024, LANES), tk=F2, name=f"ffn_dx_{i}", deps=deps)
        dz1, dz1b, dg, db, cs = _ln_bwd(dx1, sv['xh1'], sv['rs1'], row(norm1_g[i]), name=f"ln1_bwd_{i}")
        sg['norm1_g'][i], sg['norm1_b'][i] = dg.sum(0), db.sum(0)
        if i % 2 == 0:
            sg['conv_b_out'][j] = cs.sum(0)
            wgrad('conv_w_out', j, sv['act'], dz1b, tm=_tile(C, 1024), tn=_tile(D, 1024, LANES), pieces=('row',))
            ds = _mm(dz1b, wts['conv_w_out', j], bl=0, tb=True, tm=_tile(T, 512), tn=_tile(C, 1024, LANES), tk=_tile(D, 1024, LANES),
                     name=f"conv_ds_{j}", deps=deps)
            ddw, dg, db = _ln_silu_bwd(ds, sv['xhc'], sv['rsc'], row(full['conv_ln_g'][j]),
                                       row(full['conv_ln_b'][j]), name=f"conv_ln_bwd_{j}")
            sg['conv_ln_g'][j], sg['conv_ln_b'][j] = dg.sum(0), db.sum(0)
            dglu, dwk, dbk = _conv_bwd(ddw, sv['h1'], sv['wdw'], B=B, S=S, name=f"conv_dw_bwd_{j}")
            sg['conv_w_dw'][j] = dwk.sum(1)[:conv_w_dw.shape[1]]
            sg['conv_b_dw'][j] = dbk.sum(0)
            dh1, csi = _glu_bwd(dglu, sv['h1'], name=f"conv_glu_bwd_{j}")
            sg['conv_b_in'][j] = _perm_cols(csi.sum(0))
            fam = 'conv_w_in'
        else:
            sg['gmlp_b_out'][j] = cs.sum(0)
            wgrad('gmlp_w_out', j, sv['act'], dz1b, tm=_tile(C, 1024), tn=_tile(D, 1024, LANES), pieces=('row',))
            dus = _mm(dz1b, wts['gmlp_w_out', j], bl=0, tb=True, tm=_tile(T, 512), tn=_tile(C, 1024, LANES),
                      tk=_tile(D, 1024, LANES), name=f"gmlp_dus_{j}", deps=deps)
            dh1, dg, db, csi, dws, dbs = _gmlp_gate_bwd(dus, sv['pre'], sv['xhv'], sv['rsv'], row(full['gmlp_ln_g'][j]),
                                                        row(full['gmlp_ln_b'][j]), gmlp_w_s[j], sv['bsb'],
                                                        name=f"gmlp_gate_bwd_{j}")
            sg['gmlp_ln_g'][j], sg['gmlp_ln_b'][j] = dg.sum(0), db.sum(0)
            sg['gmlp_b_in'][j] = _perm_cols(csi.sum(0))
            sg['gmlp_w_s'][j] = dws
            sg['gmlp_b_s'][j] = dbs.reshape(L, G, L).sum(-1).T
            fam = 'gmlp_w_in'
        wgrad(fam, j, sv['xb'], dh1, tm=D, tn=(2 * C) // N_CHIPS, pieces=('col', True))
        dcur = _mm(dh1, wts[fam, j], bl=0, tb=True, res=dz1, res_scale=alpha, tm=_tile(T, 512),
                   tn=_tile(D, 1024, LANES), tk=2 * C, name=f"{fam}_dx_{j}", deps=deps)
    grad_x = dcur.reshape(B, S, D)

    small_names = [n for n in WEIGHTS if n not in BIG]
    small_full = [jnp.stack(sg[n]) for n in small_names]
    flat = _pack(small_full + [loss_part])
    red = _allreduce_flat(flat, name="ar_small")
    red_parts = _unpack(red, [a.shape for a in small_full] + [loss_part.shape])
    loss = (0.5 / D) * jnp.sum(red_parts[-1])
    grads = {}
    for n, g in zip(small_names, red_parts[:-1]):
        if n in SMALL_SHARDED:
            ax = SMALL_SHARDED[n]
            width = P[n].shape[ax]
            g = lax.dynamic_slice_in_dim(g, shard * width, width, axis=ax)
        grads[n] = g

    big_out = {}
    for n in ['ffn_w_down', 'ffn_w_up', 'gmlp_w_out', 'gmlp_w_in', 'conv_w_out', 'conv_w_in']:
        own = None
        n_layers = len(inflight[n])
        for l in reversed(range(n_layers)):
            send, recv, g_thru, land = inflight[n][l]
            pc_, r = _rs_wait(send, recv, g_thru, land, dcur, name=f"rs_wait_{n}_{l}")
            own = _sum_pieces(pc_, r, me_id, l, own, n_layers, name=f"sum_{n}_{l}")
        got = _pair_exchange(own, name=f"px_{n}")
        big_out[n] = _adam_halves(P[n], own, got, P['m_' + n], P['v_' + n], core_id, name=f"adam_{n}")

    shapes = [P[n].shape for n in small_names]
    n_small = sum(functools.reduce(lambda p_, d_: p_ * d_, s_, 1) for s_ in shapes)
    unit = SUBLANES * LANES
    npad = -(-n_small // unit) * unit

    def flat2d(arrs, fill=0.0):
        v = _pack(arrs)
        return jnp.pad(v, (0, npad - n_small), constant_values=fill).reshape(-1, LANES)

    dl, mo, vo = _adam(flat2d([P[n] for n in small_names]), flat2d([grads[n] for n in small_names]),
                       flat2d([P['m_' + n] for n in small_names]),
                       flat2d([P['v_' + n] for n in small_names], fill=1.0), name="adam_small")
    small_out = {n: [grads[n], None, None, None] for n in small_names}
    for k, t in enumerate((dl, mo, vo)):
        for n, a in zip(small_names, _unpack(t.reshape(-1), shapes)):
            small_out[n][k + 1] = a

    outs = [loss, grad_x]
    for k in range(4):
        for n in WEIGHTS:
            outs.append(big_out[n][k] if n in BIG else small_out[n][k])
    return tuple(outs)
```

```python
import functools

import jax
import jax.numpy as jnp
from jax import lax
from jax.experimental import pallas as pl
from jax.experimental.pallas import tpu as pltpu

F32 = jnp.float32
_MXU = jnp.bfloat16
_WIRE = jnp.bfloat16
_HDT = jnp.bfloat16
LN_EPS = 1e-5
ADAM_LR, ADAM_B1, ADAM_B2, ADAM_EPS, ADAM_WD, ADAM_STEP = 0.001, 0.9, 0.999, 1e-08, 0.01, 10
N_CHIPS = 4
N_DEV = 8
LANES = 128
SUBLANES = 8
CONV_TAPS_PAD = 32
VMEM_LIMIT = 56 << 20
MESH = pl.DeviceIdType.MESH
ANY = pl.BlockSpec(memory_space=pl.ANY)
HBM = pl.BlockSpec(memory_space=pltpu.HBM)
SEMS = pl.BlockSpec(memory_space=pltpu.SEMAPHORE)
EFFECT = pltpu.SideEffectType.DATAFLOW_SIDE_EFFECTING
PERM = (0, 2, 1, 3)


def _cp(sem=None):
    return pltpu.CompilerParams(dimension_semantics=sem, vmem_limit_bytes=VMEM_LIMIT)


def _tile(dim, pref, mult=SUBLANES):
    if dim <= pref:
        return dim
    t = (pref // mult) * mult
    while t > mult and dim % t:
        t -= mult
    assert dim % t == 0, (dim, pref, mult)
    return t


def _perm_idx(q):
    return (q % 2) * 2 + q // 2


def _fold8(t):
    r, n = t.shape
    return t.reshape(r // SUBLANES, SUBLANES, n).sum(axis=0)


def _ln_rows(z, g, b):
    mu = jnp.mean(z, axis=-1, keepdims=True)
    xc = z - mu
    var = jnp.mean(xc * xc, axis=-1, keepdims=True)
    rstd = lax.rsqrt(var + LN_EPS)
    xh = xc * rstd
    return xh * g + b, xh, rstd


def _ln_bwd_rows(dy, xh, rstd, g):
    dxh = dy * g
    m1 = jnp.mean(dxh, axis=-1, keepdims=True)
    m2 = jnp.mean(dxh * xh, axis=-1, keepdims=True)
    return rstd * (dxh - m1 - xh * m2)


def _sigmoid(v):
    return 1.0 / (1.0 + jnp.exp(-v))


def _gelu_parts(p):
    cdf = 0.5 * (1.0 + lax.erf(p * 0.7071067811865476))
    pdf = jnp.exp(-0.5 * p * p) * 0.3989422804014327
    return p * cdf, cdf + p * pdf


def _shift_down(prev8, t, s):
    ext = jnp.concatenate([prev8, t], axis=0)
    return pltpu.roll(ext, s, 0)[SUBLANES:]


def _shift_up(t, next8, s):
    n = t.shape[0]
    ext = jnp.concatenate([t, next8], axis=0)
    return pltpu.roll(ext, n + SUBLANES - s, 0)[:n]


def _mm(a, b, *, ta=False, tb=False, bl=None, bias=None, res=None, res_scale=1.0, out_dtype=F32,
        tm, tn, tk, name, pieces=None, deps=None, n_outer=False):
    M, K = (a.shape[1], a.shape[0]) if ta else a.shape
    bs = b.shape[1:] if bl is not None else b.shape
    N, Kb = (bs[0], bs[1]) if tb else (bs[1], bs[0])
    assert K == Kb and M % tm == 0 and N % tn == 0 and K % tk == 0, (a.shape, b.shape, tm, tn, tk)
    gm, gn, gk = M // tm, N // tn, K // tk

    def spec(block, imap):
        if n_outer:
            return pl.BlockSpec(block, lambda j, i, k: imap(i, j, k))
        return pl.BlockSpec(block, imap)

    a_spec = spec((tk, tm), lambda i, j, k: (k, i)) if ta else spec((tm, tk), lambda i, j, k: (i, k))
    bblk = (tn, tk) if tb else (tk, tn)
    bmap = (lambda i, j, k: (j, k)) if tb else (lambda i, j, k: (k, j))
    if bl is not None:
        b_spec = spec((None,) + bblk, lambda i, j, k: (bl,) + bmap(i, j, k))
    else:
        b_spec = spec(bblk, bmap)
    in_specs, operands = [a_spec, b_spec], [a, b]
    if bias is not None:
        in_specs.append(spec((1, tn), lambda i, j, k: (0, j)))
        operands.append(bias)
    if res is not None:
        in_specs.append(spec((tm, tn), lambda i, j, k: (i, j)))
        operands.append(res)
    n_dep = len(deps) if deps else 0
    if n_dep:
        in_specs += [ANY] * n_dep
        operands += deps
        del deps[:]
    if pieces is None:
        out_shape = jax.ShapeDtypeStruct((M, N), out_dtype)
        out_spec = spec((tm, tn), lambda i, j, k: (i, j))
        ppb = pr = None
    elif pieces[0] == 'col':
        pr, pc = M // 2, N // N_CHIPS
        assert tm % pr == 0 and pc % tn == 0
        ppb, per = tm // pr, pc // tn
        perm = pieces[1]
        out_shape = jax.ShapeDtypeStruct((N_DEV, pr, pc), out_dtype)
        out_spec = spec(
            (ppb, pr, tn),
            lambda i, j, k: ((2 * (_perm_idx(j // per) if perm else j // per)) // ppb + i, 0, j % per))
    else:
        pr = M // N_DEV
        assert tm % pr == 0
        ppb = tm // pr
        out_shape = jax.ShapeDtypeStruct((N_DEV, pr, N), out_dtype)
        out_spec = spec((ppb, pr, tn), lambda i, j, k: (i, 0, j))
    dims = (((0 if ta else 1,), (1 if tb else 0,)), ((), ()))

    def body(*refs):
        a_ref, b_ref = refs[0], refs[1]
        pos = 2
        bias_ref = res_ref = None
        if bias is not None:
            bias_ref = refs[pos]
            pos += 1
        if res is not None:
            res_ref = refs[pos]
            pos += 1
        pos += n_dep
        o_ref = refs[pos]

        def finish(r):
            if bias_ref is not None:
                r = r + bias_ref[...]
            if res_ref is not None:
                r = r + res_scale * res_ref[...]
            if pieces is not None:
                r = r.reshape(ppb, pr, tn)
            o_ref[...] = r.astype(out_dtype)

        part = lax.dot_general(a_ref[...].astype(_MXU), b_ref[...].astype(_MXU), dims, preferred_element_type=F32)
        if gk == 1:
            finish(part)
            return
        acc_ref = refs[pos + 1]
        k = pl.program_id(2)

        @pl.when(k == 0)
        def _():
            acc_ref[...] = part

        @pl.when((k > 0) & (k < gk - 1))
        def _():
            acc_ref[...] += part

        @pl.when(k == gk - 1)
        def _():
            finish(acc_ref[...] + part)

    return pl.pallas_call(
        body, name=name, grid=(gn, gm, gk) if n_outer else (gm, gn, gk), in_specs=in_specs, out_specs=out_spec,
        out_shape=out_shape, scratch_shapes=[pltpu.VMEM((tm, tn), F32)] if gk > 1 else [],
        compiler_params=_cp(("parallel", "parallel", "arbitrary")),
    )(*operands)


def _mm_res_ln(a, w, wl, bias, res, alpha, g, b, *, name):
    T, K = a.shape
    D = w.shape[-1]
    tm = _tile(T, 256)

    def body(a_ref, w_ref, bias_ref, res_ref, g_ref, b_ref, y_ref, yb_ref, xh_ref, rs_ref):
        z = jnp.dot(a_ref[...].astype(_MXU), w_ref[...].astype(_MXU), preferred_element_type=F32)
        z = z + bias_ref[...] + alpha * res_ref[...]
        y, xh, rstd = _ln_rows(z, g_ref[...], b_ref[...])
        y_ref[...] = y
        yb_ref[...] = y.astype(_MXU)
        xh_ref[...] = xh
        rs_ref[...] = rstd

    row = lambda i: (i, 0)
    vec = pl.BlockSpec((1, D), lambda i: (0, 0))
    return pl.pallas_call(
        body, name=name, grid=(T // tm,),
        in_specs=[pl.BlockSpec((tm, K), row), pl.BlockSpec((None, K, D), lambda i: (wl, 0, 0)), vec,
                  pl.BlockSpec((tm, D), row), vec, vec],
        out_specs=[pl.BlockSpec((tm, D), row), pl.BlockSpec((tm, D), row), pl.BlockSpec((tm, D), row),
                   pl.BlockSpec((tm, 1), row)],
        out_shape=[jax.ShapeDtypeStruct((T, D), F32), jax.ShapeDtypeStruct((T, D), _MXU),
                   jax.ShapeDtypeStruct((T, D), F32), jax.ShapeDtypeStruct((T, 1), F32)],
        compiler_params=_cp(("parallel",)),
    )(a, w, bias, res, g, b)


def _ln_bwd(dy, xh, rstd, g, *, name, target=None):
    T, D = dy.shape
    tm = _tile(T, 256)
    head = target is not None

    def body(*refs):
        if head:
            dy_ref, t_ref, xh_ref, rs_ref, g_ref, dz_ref, dzb_ref, dg_ref, db_ref, cs_ref, ls_ref = refs
        else:
            dy_ref, xh_ref, rs_ref, g_ref, dz_ref, dzb_ref, dg_ref, db_ref, cs_ref = refs
        i = pl.program_id(0)

        @pl.when(i == 0)
        def _():
            dg_ref[...] = jnp.zeros_like(dg_ref)
            db_ref[...] = jnp.zeros_like(db_ref)
            cs_ref[...] = jnp.zeros_like(cs_ref)
            if head:
                ls_ref[...] = jnp.zeros_like(ls_ref)

        d = dy_ref[...]
        if head:
            err = d - t_ref[...]
            ls_ref[...] += _fold8(err * err)
            d = err * (1.0 / D)
        xh = xh_ref[...]
        dz = _ln_bwd_rows(d, xh, rs_ref[...], g_ref[...])
        dz_ref[...] = dz
        dzb_ref[...] = dz.astype(_MXU)
        dg_ref[...] += _fold8(d * xh)
        db_ref[...] += _fold8(d)
        cs_ref[...] += _fold8(dz)

    row = lambda i: (i, 0)
    fixed = lambda i: (0, 0)
    tile = pl.BlockSpec((tm, D), row)
    part = pl.BlockSpec((SUBLANES, D), fixed)
    in_specs = [tile] + ([tile] if head else []) + [tile, pl.BlockSpec((tm, 1), row), pl.BlockSpec((1, D), fixed)]
    n_part = 4 if head else 3
    operands = [dy] + ([target] if head else []) + [xh, rstd, g]
    return pl.pallas_call(
        body, name=name, grid=(T // tm,), in_specs=in_specs,
        out_specs=[tile, tile] + [part] * n_part,
        out_shape=[jax.ShapeDtypeStruct((T, D), F32), jax.ShapeDtypeStruct((T, D), _MXU)]
        + [jax.ShapeDtypeStruct((SUBLANES, D), F32)] * n_part,
        compiler_params=_cp(("arbitrary",)),
    )(*operands)


def _conv_cols(C, tc):
    per = (C // 2) // tc
    return per, (lambda j: (j // per) * (2 * per) + j % per)


def _glu_shifted(a_ref, g_ref, p_ref, S):
    u = a_ref[...] * _sigmoid(g_ref[...])
    rows = lax.broadcasted_iota(jnp.int32, u.shape, 0)
    for r in range(SUBLANES):
        p_ref[r, 0:CONV_TAPS_PAD, :] = jnp.zeros((CONV_TAPS_PAD, u.shape[1]), F32)
        p_ref[r, CONV_TAPS_PAD:CONV_TAPS_PAD + S, :] = u if r == 0 else jnp.where(rows >= r, pltpu.roll(u, r, 0), 0.0)


def _conv_fwd(h1, w_dw, b_dw, *, B, S, name):
    C = w_dw.shape[1]
    taps = CONV_TAPS_PAD - 1
    tc = LANES
    ch = _tile(S, 128)
    per, col_a = _conv_cols(C, tc)

    def body(a_ref, g_ref, w_ref, b_ref, o_ref, p_ref):
        _glu_shifted(a_ref, g_ref, p_ref, S)

        def chunk(ci, carry):
            base = pl.multiple_of(ci * ch, ch)
            acc = jnp.zeros((ch, tc), F32) + b_ref[...]
            for k in range(taps):
                q, r = divmod(taps - 1 - k, SUBLANES)
                start = pl.multiple_of(base + (CONV_TAPS_PAD - SUBLANES * q), SUBLANES)
                acc = acc + w_ref[pl.ds(k, 1), :] * p_ref[r, pl.ds(start, ch), :]
            o_ref[pl.ds(base, ch), :] = acc
            return carry

        lax.fori_loop(0, S // ch, chunk, 0)

    return pl.pallas_call(
        body, name=name, grid=(B, C // tc),
        in_specs=[pl.BlockSpec((S, tc), lambda b, j: (b, col_a(j))),
                  pl.BlockSpec((S, tc), lambda b, j: (b, col_a(j) + per)),
                  pl.BlockSpec((CONV_TAPS_PAD, tc), lambda b, j: (0, j)),
                  pl.BlockSpec((1, tc), lambda b, j: (0, j))],
        out_specs=pl.BlockSpec((S, tc), lambda b, j: (b, j)),
        out_shape=jax.ShapeDtypeStruct((B * S, C), F32),
        scratch_shapes=[pltpu.VMEM((SUBLANES, S + CONV_TAPS_PAD, tc), F32)],
        compiler_params=_cp(("parallel", "parallel")),
    )(h1, h1, w_dw, b_dw)


def _conv_bwd(dd, h1, w_dw, *, B, S, name):
    C = w_dw.shape[1]
    taps = CONV_TAPS_PAD - 1
    tc = LANES
    ch = _tile(S, 128)
    per, col_a = _conv_cols(C, tc)

    def body(d_ref, a_ref, g_ref, w_ref, du_ref, dw_ref, db_ref, p_ref, q_ref):
        b = pl.program_id(1)

        @pl.when(b == 0)
        def _():
            dw_ref[...] = jnp.zeros_like(dw_ref)
            db_ref[...] = jnp.zeros_like(db_ref)

        _glu_shifted(a_ref, g_ref, p_ref, S)
        d = d_ref[...]
        rows = lax.broadcasted_iota(jnp.int32, d.shape, 0)
        for r in range(SUBLANES):
            q_ref[r, S:S + CONV_TAPS_PAD, :] = jnp.zeros((CONV_TAPS_PAD, tc), F32)
            q_ref[r, 0:S, :] = d if r == 0 else jnp.where(rows < S - r, pltpu.roll(d, S - r, 0), 0.0)
        db_ref[...] += _fold8(d)

        def chunk(ci, carry):
            base = pl.multiple_of(ci * ch, ch)
            dch = d_ref[pl.ds(base, ch), :]
            acc = jnp.zeros((ch, tc), F32)
            for k in range(taps):
                q, r = divmod(taps - 1 - k, SUBLANES)
                up = pl.multiple_of(base + SUBLANES * q, SUBLANES)
                acc = acc + w_ref[pl.ds(k, 1), :] * q_ref[r, pl.ds(up, ch), :]
                down = pl.multiple_of(base + (CONV_TAPS_PAD - SUBLANES * q), SUBLANES)
                dw_ref[k] += _fold8(dch * p_ref[r, pl.ds(down, ch), :])
            du_ref[pl.ds(base, ch), :] = acc
            return carry

        lax.fori_loop(0, S // ch, chunk, 0)

    return pl.pallas_call(
        body, name=name, grid=(C // tc, B),
        in_specs=[pl.BlockSpec((S, tc), lambda j, b: (b, j)),
                  pl.BlockSpec((S, tc), lambda j, b: (b, col_a(j))),
                  pl.BlockSpec((S, tc), lambda j, b: (b, col_a(j) + per)),
                  pl.BlockSpec((CONV_TAPS_PAD, tc), lambda j, b: (0, j))],
        out_specs=[pl.BlockSpec((S, tc), lambda j, b: (b, j)),
                   pl.BlockSpec((CONV_TAPS_PAD, SUBLANES, tc), lambda j, b: (0, 0, j)),
                   pl.BlockSpec((SUBLANES, tc), lambda j, b: (0, j))],
        out_shape=[jax.ShapeDtypeStruct((B * S, C), F32),
                   jax.ShapeDtypeStruct((CONV_TAPS_PAD, SUBLANES, C), F32),
                   jax.ShapeDtypeStruct((SUBLANES, C), F32)],
        scratch_shapes=[pltpu.VMEM((SUBLANES, S + CONV_TAPS_PAD, tc), F32),
                        pltpu.VMEM((SUBLANES, S + CONV_TAPS_PAD, tc), F32)],
        compiler_params=_cp(("parallel", "arbitrary")),
    )(dd, h1, h1, w_dw)


def _ln_silu_fwd(v, g, b, *, name):
    T, C = v.shape
    tm = _tile(T, 512)

    def body(v_ref, g_ref, b_ref, s_ref, xh_ref, rs_ref):
        y, xh, rstd = _ln_rows(v_ref[...], g_ref[...], b_ref[...])
        s_ref[...] = (y * _sigmoid(y)).astype(_MXU)
        xh_ref[...] = xh
        rs_ref[...] = rstd

    row = lambda i: (i, 0)
    vec = pl.BlockSpec((1, C), lambda i: (0, 0))
    return pl.pallas_call(
        body, name=name, grid=(T // tm,),
        in_specs=[pl.BlockSpec((tm, C), row), vec, vec],
        out_specs=[pl.BlockSpec((tm, C), row), pl.BlockSpec((tm, C), row), pl.BlockSpec((tm, 1), row)],
        out_shape=[jax.ShapeDtypeStruct((T, C), _MXU), jax.ShapeDtypeStruct((T, C), F32),
                   jax.ShapeDtypeStruct((T, 1), F32)],
        compiler_params=_cp(("parallel",)),
    )(v, g, b)


def _ln_silu_bwd(ds, xh, rstd, g, b, *, name):
    T, C = ds.shape
    tm = _tile(T, 256)

    def body(ds_ref, xh_ref, rs_ref, g_ref, b_ref, dv_ref, dg_ref, db_ref):
        @pl.when(pl.program_id(0) == 0)
        def _():
            dg_ref[...] = jnp.zeros_like(dg_ref)
            db_ref[...] = jnp.zeros_like(db_ref)

        xh = xh_ref[...]
        gam = g_ref[...]
        y = xh * gam + b_ref[...]
        sig = _sigmoid(y)
        dln = ds_ref[...] * (sig * (1.0 + y * (1.0 - sig)))
        dv_ref[...] = _ln_bwd_rows(dln, xh, rs_ref[...], gam)
        dg_ref[...] += _fold8(dln * xh)
        db_ref[...] += _fold8(dln)

    row = lambda i: (i, 0)
    fixed = lambda i: (0, 0)
    vec = pl.BlockSpec((1, C), fixed)
    part = pl.BlockSpec((SUBLANES, C), fixed)
    return pl.pallas_call(
        body, name=name, grid=(T // tm,),
        in_specs=[pl.BlockSpec((tm, C), row), pl.BlockSpec((tm, C), row), pl.BlockSpec((tm, 1), row), vec, vec],
        out_specs=[pl.BlockSpec((tm, C), row), part, part],
        out_shape=[jax.ShapeDtypeStruct((T, C), F32)] + [jax.ShapeDtypeStruct((SUBLANES, C), F32)] * 2,
        compiler_params=_cp(("arbitrary",)),
    )(ds, xh, rstd, g, b)


def _glu_bwd(du, h1, *, name):
    T, C = du.shape
    il = C // 2
    tm = _tile(T, 256)

    def body(du_ref, h_ref, dh_ref, cs_ref):
        @pl.when(pl.program_id(0) == 0)
        def _():
            cs_ref[...] = jnp.zeros_like(cs_ref)

        for hb in range(2):
            a = h_ref[:, 2 * hb * il:(2 * hb + 1) * il]
            gate = h_ref[:, (2 * hb + 1) * il:(2 * hb + 2) * il]
            d = du_ref[:, hb * il:(hb + 1) * il]
            sig = _sigmoid(gate)
            da = d * sig
            dgate = d * a * sig * (1.0 - sig)
            dh_ref[:, 2 * hb * il:(2 * hb + 1) * il] = da.astype(_MXU)
            dh_ref[:, (2 * hb + 1) * il:(2 * hb + 2) * il] = dgate.astype(_MXU)
            cs_ref[:, 2 * hb * il:(2 * hb + 1) * il] += _fold8(da)
            cs_ref[:, (2 * hb + 1) * il:(2 * hb + 2) * il] += _fold8(dgate)

    row = lambda i: (i, 0)
    return pl.pallas_call(
        body, name=name, grid=(T // tm,),
        in_specs=[pl.BlockSpec((tm, C), row), pl.BlockSpec((tm, 2 * C), row)],
        out_specs=[pl.BlockSpec((tm, 2 * C), row), pl.BlockSpec((SUBLANES, 2 * C), lambda i: (0, 0))],
        out_shape=[jax.ShapeDtypeStruct((T, 2 * C), _MXU), jax.ShapeDtypeStruct((SUBLANES, 2 * C), F32)],
        compiler_params=_cp(("arbitrary",)),
    )(du, h1)


def _tril_mask(n):
    return lax.broadcasted_iota(jnp.int32, (n, n), 0) >= lax.broadcasted_iota(jnp.int32, (n, n), 1)


def _split_uv(t, il):
    u = jnp.concatenate([t[:, 0:il], t[:, 2 * il:3 * il]], axis=1)
    v = jnp.concatenate([t[:, il:2 * il], t[:, 3 * il:4 * il]], axis=1)
    return u, v


def _gmlp_gate_fwd(p, g, b, w_s, bsb, *, name):
    T, C2 = p.shape
    C = C2 // 2
    il = C // 2
    G, L, _ = w_s.shape
    assert G * L == C
    tm = _tile(T, 2 * L, L)

    def body(p_ref, g_ref, b_ref, ws_ref, bs_ref, us_ref, xh_ref, rs_ref, vn_ref, u_ref):
        z, _ = _gelu_parts(p_ref[...])
        u, v = _split_uv(z, il)
        vn, xh, rstd = _ln_rows(v, g_ref[...], b_ref[...])
        xh_ref[...] = xh
        rs_ref[...] = rstd
        vn_ref[...] = vn.astype(_MXU)
        u_ref[...] = u
        mask = _tril_mask(L)
        for gi in range(G):
            wc = jnp.where(mask, ws_ref[gi], 0.0).astype(_MXU)
            cols = slice(gi * L, (gi + 1) * L)
            for c in range(tm // L):
                rows = slice(c * L, (c + 1) * L)
                s = jnp.dot(wc, vn_ref[rows, cols], preferred_element_type=F32) + bs_ref[:, cols]
                us_ref[rows, cols] = (u_ref[rows, cols] * s).astype(_MXU)

    row = lambda i: (i, 0)
    fixed = lambda i: (0, 0)
    return pl.pallas_call(
        body, name=name, grid=(T // tm,),
        in_specs=[pl.BlockSpec((tm, C2), row), pl.BlockSpec((1, C), fixed), pl.BlockSpec((1, C), fixed),
                  pl.BlockSpec((G, L, L), lambda i: (0, 0, 0)), pl.BlockSpec((L, C), fixed)],
        out_specs=[pl.BlockSpec((tm, C), row), pl.BlockSpec((tm, C), row), pl.BlockSpec((tm, 1), row)],
        out_shape=[jax.ShapeDtypeStruct((T, C), _MXU), jax.ShapeDtypeStruct((T, C), F32),
                   jax.ShapeDtypeStruct((T, 1), F32)],
        scratch_shapes=[pltpu.VMEM((tm, C), _MXU), pltpu.VMEM((tm, C), F32)],
        compiler_params=_cp(("parallel",)),
    )(p, g, b, w_s, bsb)


def _gmlp_gate_bwd(dus, p, xh, rstd, g, b, w_s, bsb, *, name):
    T, C2 = p.shape
    C = C2 // 2
    il = C // 2
    G, L, _ = w_s.shape
    tm = _tile(T, 2 * L, L)

    def body(dus_ref, p_ref, xh_ref, rs_ref, g_ref, b_ref, ws_ref, bs_ref,
             dp_ref, dg_ref, db_ref, cs_ref, dws_ref, dbs_ref, vn_ref, u_ref, dvn_ref, du_ref):
        @pl.when(pl.program_id(0) == 0)
        def _():
            dg_ref[...] = jnp.zeros_like(dg_ref)
            db_ref[...] = jnp.zeros_like(db_ref)
            cs_ref[...] = jnp.zeros_like(cs_ref)
            dws_ref[...] = jnp.zeros_like(dws_ref)
            dbs_ref[...] = jnp.zeros_like(dbs_ref)

        z, gp = _gelu_parts(p_ref[...])
        u, _ = _split_uv(z, il)
        xh = xh_ref[...]
        gam = g_ref[...]
        vn_ref[...] = (xh * gam + b_ref[...]).astype(_MXU)
        u_ref[...] = u
        mask = _tril_mask(L)
        for gi in range(G):
            wc = jnp.where(mask, ws_ref[gi], 0.0).astype(_MXU)
            cols = slice(gi * L, (gi + 1) * L)
            for c in range(tm // L):
                rows = slice(c * L, (c + 1) * L)
                vnb = vn_ref[rows, cols]
                s = jnp.dot(wc, vnb, preferred_element_type=F32) + bs_ref[:, cols]
                d = dus_ref[rows, cols]
                du_ref[rows, cols] = d * s
                ds = d * u_ref[rows, cols]
                dbs_ref[:, cols] += ds
                dsb = ds.astype(_MXU)
                dw = lax.dot_general(dsb, vnb, (((1,), (1,)), ((), ())), preferred_element_type=F32)
                dws_ref[gi] += jnp.where(mask, dw, 0.0)
                dvn_ref[rows, cols] = lax.dot_general(wc, dsb, (((0,), (0,)), ((), ())), preferred_element_type=F32)
        dvn = dvn_ref[...]
        dg_ref[...] += _fold8(dvn * xh)
        db_ref[...] += _fold8(dvn)
        dv = _ln_bwd_rows(dvn, xh, rs_ref[...], gam)
        du = du_ref[...]
        for hb in range(2):
            for part, src in ((0, du), (1, dv)):
                lo = (2 * hb + part) * il
                dp = src[:, hb * il:(hb + 1) * il] * gp[:, lo:lo + il]
                dp_ref[:, lo:lo + il] = dp.astype(_MXU)
                cs_ref[:, lo:lo + il] += _fold8(dp)

    row = lambda i: (i, 0)
    fixed = lambda i: (0, 0)
    part_c = pl.BlockSpec((SUBLANES, C), fixed)
    return pl.pallas_call(
        body, name=name, grid=(T // tm,),
        in_specs=[pl.BlockSpec((tm, C), row), pl.BlockSpec((tm, C2), row), pl.BlockSpec((tm, C), row),
                  pl.BlockSpec((tm, 1), row), pl.BlockSpec((1, C), fixed), pl.BlockSpec((1, C), fixed),
                  pl.BlockSpec((G, L, L), lambda i: (0, 0, 0)), pl.BlockSpec((L, C), fixed)],
        out_specs=[pl.BlockSpec((tm, C2), row), part_c, part_c, pl.BlockSpec((SUBLANES, C2), fixed),
                   pl.BlockSpec((G, L, L), lambda i: (0, 0, 0)), pl.BlockSpec((L, C), fixed)],
        out_shape=[jax.ShapeDtypeStruct((T, C2), _MXU), jax.ShapeDtypeStruct((SUBLANES, C), F32),
                   jax.ShapeDtypeStruct((SUBLANES, C), F32), jax.ShapeDtypeStruct((SUBLANES, C2), F32),
                   jax.ShapeDtypeStruct((G, L, L), F32), jax.ShapeDtypeStruct((L, C), F32)],
        scratch_shapes=[pltpu.VMEM((tm, C), _MXU), pltpu.VMEM((tm, C), F32), pltpu.VMEM((tm, C), F32),
                        pltpu.VMEM((tm, C), F32)],
        compiler_params=_cp(("arbitrary",)),
    )(dus, p, xh, rstd, g, b, w_s, bsb)


def _ffn_conv(h, prev8, w_ref, b_ref):
    h1 = _shift_down(prev8, h, 1)
    h2 = _shift_down(prev8, h, 2)
    return w_ref[pl.ds(2, 1), :] * h + w_ref[pl.ds(1, 1), :] * h1 + w_ref[pl.ds(0, 1), :] * h2 + b_ref[...]


def _ffn_up_fwd(xb, w, wl, b_up, w_dw, b_dw, *, S, name):
    T, D = xb.shape
    N = w.shape[-1]
    tn = N // N_CHIPS
    tm = _tile(S, 256)
    spt = S // tm

    def body(x_ref, w_ref, bu_ref, wd_ref, bd_ref, h_ref, hc_ref, f_ref, carry_ref):
        i = pl.program_id(1)

        @pl.when(i % spt == 0)
        def _():
            carry_ref[...] = jnp.zeros_like(carry_ref)

        h = jnp.dot(x_ref[...].astype(_MXU), w_ref[...].astype(_MXU), preferred_element_type=F32) + bu_ref[...]
        hq = h.astype(_HDT)
        h_ref[...] = hq
        h = hq.astype(F32)
        hc = _ffn_conv(h, carry_ref[...], wd_ref, bd_ref)
        hc_ref[...] = hc.astype(_HDT)
        carry_ref[...] = h[tm - SUBLANES:tm]
        gte = hc[:, :tn]
        f_ref[...] = (gte * _sigmoid(gte) * hc[:, tn:]).astype(_MXU)

    pair = lambda j, i: (0, j)
    wide = pl.BlockSpec((tm, 2 * tn), lambda j, i: (i, j))
    return pl.pallas_call(
        body, name=name, grid=(2, T // tm),
        in_specs=[pl.BlockSpec((tm, D), lambda j, i: (i, 0)),
                  pl.BlockSpec((None, D, 2 * tn), lambda j, i: (wl, 0, j)),
                  pl.BlockSpec((1, 2 * tn), pair), pl.BlockSpec((SUBLANES, 2 * tn), pair),
                  pl.BlockSpec((1, 2 * tn), pair)],
        out_specs=[wide, wide, pl.BlockSpec((tm, tn), lambda j, i: (i, j))],
        out_shape=[jax.ShapeDtypeStruct((T, N), _HDT), jax.ShapeDtypeStruct((T, N), _HDT),
                   jax.ShapeDtypeStruct((T, N // 2), _MXU)],
        scratch_shapes=[pltpu.VMEM((SUBLANES, 2 * tn), F32)],
        compiler_params=_cp(("parallel", "arbitrary")),
    )(xb, w, b_up, w_dw, b_dw)


def _ffn_bwd(dzb, w_down, wl, hs, hcs, w_dw, *, S, name):
    T, D = dzb.shape
    N = hs.shape[1]
    tn = N // N_CHIPS
    tm = _tile(S, 256)
    spt = S // tm
    nt = T // tm

    def body(dz_ref, wd_ref, h_ref, hc_ref, wc_ref, dh_ref, cs_ref, dw_ref, db_ref, carry_ref):
        i = pl.program_id(1)
        ii = nt - 1 - i

        @pl.when(i == 0)
        def _():
            cs_ref[...] = jnp.zeros_like(cs_ref)
            dw_ref[...] = jnp.zeros_like(dw_ref)
            db_ref[...] = jnp.zeros_like(db_ref)

        df = lax.dot_general(dz_ref[...].astype(_MXU), wd_ref[...].astype(_MXU), (((1,), (1,)), ((), ())),
                             preferred_element_type=F32)
        h = h_ref[...].astype(F32)
        gte, val = hc_ref[:, :tn].astype(F32), hc_ref[:, tn:].astype(F32)
        sig = _sigmoid(gte)
        dval = df * (gte * sig)
        dg = df * val * (sig * (1.0 + gte * (1.0 - sig)))
        dhc = jnp.concatenate([dg, dval], axis=1)
        nxt = jnp.where((ii + 1) % spt == 0, 0.0, carry_ref[...])
        d1 = _shift_up(dhc, nxt, 1)
        d2 = _shift_up(dhc, nxt, 2)
        carry_ref[...] = dhc[0:SUBLANES]
        db_ref[...] += _fold8(dhc)
        dw_ref[2] += _fold8(dhc * h)
        dw_ref[1] += _fold8(d1 * h)
        dw_ref[0] += _fold8(d2 * h)
        dh = wc_ref[pl.ds(2, 1), :] * dhc + wc_ref[pl.ds(1, 1), :] * d1 + wc_ref[pl.ds(0, 1), :] * d2
        cs_ref[...] += _fold8(dh)
        dh_ref[...] = dh.astype(_MXU)

    pair = lambda j, i: (0, j)
    rev = lambda j, i: (nt - 1 - i, j)
    return pl.pallas_call(
        body, name=name, grid=(2, nt),
        in_specs=[pl.BlockSpec((tm, D), lambda j, i: (nt - 1 - i, 0)),
                  pl.BlockSpec((None, tn, D), lambda j, i: (wl, j, 0)),
                  pl.BlockSpec((tm, 2 * tn), rev), pl.BlockSpec((tm, 2 * tn), rev),
                  pl.BlockSpec((SUBLANES, 2 * tn), pair)],
        out_specs=[pl.BlockSpec((tm, 2 * tn), rev), pl.BlockSpec((SUBLANES, 2 * tn), pair),
                   pl.BlockSpec((3, SUBLANES, 2 * tn), lambda j, i: (0, 0, j)),
                   pl.BlockSpec((SUBLANES, 2 * tn), pair)],
        out_shape=[jax.ShapeDtypeStruct((T, N), _MXU), jax.ShapeDtypeStruct((SUBLANES, N), F32),
                   jax.ShapeDtypeStruct((3, SUBLANES, N), F32), jax.ShapeDtypeStruct((SUBLANES, N), F32)],
        scratch_shapes=[pltpu.VMEM((SUBLANES, 2 * tn), F32)],
        compiler_params=_cp(("parallel", "arbitrary")),
    )(dzb, w_down, hs, hcs, w_dw)


def _sum_pieces(g, r, me, layer, acc, n_layers, *, name):
    _, pr, pc = g.shape
    tr = _tile(pr, 128)

    def body(me_ref, g_ref, r_ref, *rest):
        o_ref = rest[-1]
        total = g_ref[...].astype(F32)
        for s in range(N_DEV - 1):
            total = total + r_ref[s].astype(F32)
        o_ref[...] = total

    in_specs = [pl.BlockSpec((None, tr, pc), lambda i, me_ref: (me_ref[0], i, 0)),
                pl.BlockSpec((N_DEV - 1, tr, pc), lambda i, me_ref: (0, i, 0))]
    operands = [me, g, r]
    aliases = {}
    if acc is not None:
        in_specs.append(ANY)
        operands.append(acc)
        aliases = {3: 0}
    return pl.pallas_call(
        body, name=name,
        grid_spec=pltpu.PrefetchScalarGridSpec(
            num_scalar_prefetch=1, grid=(pr // tr,), in_specs=in_specs,
            out_specs=pl.BlockSpec((None, tr, pc), lambda i, me_ref: (layer, i, 0))),
        out_shape=jax.ShapeDtypeStruct((n_layers, pr, pc), F32),
        input_output_aliases=aliases,
        compiler_params=_cp(("parallel",)),
    )(*operands)


def _adam_math(w, g, m, v):
    bc1 = 1.0 - ADAM_B1 ** ADAM_STEP
    bc2 = 1.0 - ADAM_B2 ** ADAM_STEP
    m = ADAM_B1 * m + (1.0 - ADAM_B1) * g
    v = ADAM_B2 * v + (1.0 - ADAM_B2) * (g * g)
    return -ADAM_LR * ((m / bc1) / (jnp.sqrt(v / bc2) + ADAM_EPS) + ADAM_WD * w), m, v


def _adam(w, g, m, v, *, name):
    R, C = w.shape
    tr = _tile(R, 256)

    def body(w_ref, g_ref, m_ref, v_ref, d_ref, mo_ref, vo_ref):
        d_ref[...], mo_ref[...], vo_ref[...] = _adam_math(w_ref[...], g_ref[...], m_ref[...], v_ref[...])

    spec = pl.BlockSpec((tr, C), lambda i: (i, 0))
    return pl.pallas_call(
        body, name=name, grid=(R // tr,), in_specs=[spec] * 4, out_specs=[spec] * 3,
        out_shape=[jax.ShapeDtypeStruct((R, C), F32)] * 3,
        compiler_params=_cp(("parallel",)),
    )(w, g, m, v)


def _adam_halves(w, own, got, m, v, core, *, name):
    L, R, C = w.shape
    rh = R // 2
    tr = _tile(rh, 256)
    nt = rh // tr

    def body(c_ref, w_ref, own_ref, got_ref, m_ref, v_ref, g_ref, d_ref, mo_ref, vo_ref):
        g = jnp.where(pl.program_id(1) == c_ref[0], own_ref[...], got_ref[...])
        g_ref[...] = g
        d_ref[...], mo_ref[...], vo_ref[...] = _adam_math(w_ref[...], g, m_ref[...], v_ref[...])

    full = pl.BlockSpec((None, tr, C), lambda l, h, t, c_ref: (l, h * nt + t, 0))
    half = pl.BlockSpec((None, tr, C), lambda l, h, t, c_ref: (l, t, 0))
    return pl.pallas_call(
        body, name=name,
        grid_spec=pltpu.PrefetchScalarGridSpec(
            num_scalar_prefetch=1, grid=(L, 2, nt), in_specs=[full, half, half, full, full], out_specs=[full] * 4),
        out_shape=[jax.ShapeDtypeStruct((L, R, C), F32)] * 4,
        compiler_params=_cp(("parallel", "parallel", "parallel")),
    )(core, w, own, got, m, v)


def _remote(src, dst, send, recv, dev):
    return pltpu.make_async_remote_copy(src_ref=src, dst_ref=dst, send_sem=send, recv_sem=recv,
                                        device_id=dev, device_id_type=MESH)


def _place_w(shard, pos, layer, *, axis, name):
    _, R, C = shard.shape
    tr = _tile(R, 512, 16)
    nt = R // tr
    if axis == 2:
        out_shape = (1, R, N_CHIPS * C)
        out_map = lambda t, q: (0, t, q[0])
    else:
        out_shape = (1, N_CHIPS * R, C)
        out_map = lambda t, q: (0, q[0] * nt + t, 0)

    def body(q_ref, s_ref, o_ref):
        o_ref[...] = s_ref[...].astype(_WIRE)

    return pl.pallas_call(
        body, name=name,
        grid_spec=pltpu.PrefetchScalarGridSpec(
            num_scalar_prefetch=1, grid=(nt,),
            in_specs=[pl.BlockSpec((None, tr, C), lambda t, q: (layer, t, 0))],
            out_specs=pl.BlockSpec((None, tr, C), out_map)),
        out_shape=jax.ShapeDtypeStruct(out_shape, _WIRE),
        compiler_params=_cp(("parallel",)),
    )(pos, shard)


def _ag_window(ref, kind, px, py, h):
    axis, perm = kind
    q = 2 * px + py
    if perm:
        q = _perm_idx(q)
    if axis == 2:
        R, C = ref.shape[1], ref.shape[2] // N_CHIPS
        rh = R // 2
        return ref.at[:, pl.ds(pl.multiple_of(h * rh, 16), rh), pl.ds(pl.multiple_of(q * C, LANES), C)]
    R = ref.shape[1] // N_CHIPS
    rh = R // 2
    return ref.at[:, pl.ds(pl.multiple_of(q * R + h * rh, 16), rh), :]


def _ag_ici_copies(refs, kinds, send, recv):
    x, y, c = lax.axis_index("x"), lax.axis_index("y"), lax.axis_index("c")
    chips = [(1 - x, y), (x, 1 - y), (1 - x, 1 - y)]
    sends, recvs = [], []
    for a, (ref, kind) in enumerate(zip(refs, kinds)):
        own = _ag_window(ref, kind, x, y, c)
        for i, (px, py) in enumerate(chips):
            k = 3 * a + i
            sends.append(_remote(own, own, send.at[k], recv.at[k], (px, py, c)))
            recvs.append(_remote(own, _ag_window(ref, kind, px, py, c), send.at[k], recv.at[k], (px, py, c)))
    return sends, recvs


def _ag_start(arrs, kinds, after, *, name):
    n = len(arrs)

    def body(*refs):
        in_refs = refs[:n]
        send, recv = refs[n + len(after)], refs[n + len(after) + 1]
        token = refs[-1]
        sends, _ = _ag_ici_copies(in_refs, kinds, send, recv)
        for cp in sends:
            cp.start()
        token[...] = jnp.zeros_like(token)

    sems = pltpu.SemaphoreType.DMA((3 * n,))
    out = pl.pallas_call(
        body, name=name,
        out_shape=(sems, sems) + tuple(pltpu.HBM(a.shape, a.dtype) for a in arrs)
        + (jax.ShapeDtypeStruct((SUBLANES, LANES), F32),),
        in_specs=(HBM,) * n + (ANY,) * len(after),
        out_specs=(SEMS, SEMS) + (HBM,) * n + (pl.BlockSpec(memory_space=pltpu.VMEM),),
        input_output_aliases={a: 2 + a for a in range(n)},
        compiler_params=pltpu.CompilerParams(has_side_effects=EFFECT),
    )(*[pltpu.with_memory_space_constraint(a, pltpu.HBM) for a in arrs], *after)
    return out[0], out[1], list(out[2:2 + n]), out[-1]


def _ag_wait(send, recv, arrs, kinds, after, *, name):
    n = len(arrs)

    def body(*refs):
        in_refs = refs[:n]
        send, recv = refs[n], refs[n + 1]
        sends, recvs = _ag_ici_copies(in_refs, kinds, send, recv)
        for cp in sends:
            cp.wait_send()
        for cp in recvs:
            cp.wait_recv()

    out = pl.pallas_call(
        body, name=name,
        out_shape=tuple(pltpu.HBM(a.shape, a.dtype) for a in arrs),
        in_specs=(HBM,) * n + (SEMS, SEMS) + (ANY,) * len(after), out_specs=(HBM,) * n,
        input_output_aliases={a: a for a in range(n)},
        compiler_params=pltpu.CompilerParams(has_side_effects=EFFECT),
    )(*arrs, send, recv, *after)
    return list(out)


def _ag_forward(arrs, kinds, *, name):
    n = len(arrs)

    def body(*refs):
        o_refs, send, recv = refs[n:2 * n], refs[2 * n], refs[2 * n + 1]
        x, y, c = lax.axis_index("x"), lax.axis_index("y"), lax.axis_index("c")
        chips = [(1 - x, y), (x, 1 - y), (1 - x, 1 - y)]
        sib = (x, y, 1 - c)
        sends, recvs = [], []
        for a, (ref, kind) in enumerate(zip(o_refs, kinds)):
            for i, (px, py) in enumerate(chips):
                k = 3 * a + i
                got = _ag_window(ref, kind, px, py, c)
                cp = _remote(got, got, send.at[k], recv.at[k], sib)
                cp.start()
                sends.append(cp)
                recvs.append(_remote(got, _ag_window(ref, kind, px, py, 1 - c), send.at[k], recv.at[k], sib))
        for cp in recvs:
            cp.wait_recv()
        for cp in sends:
            cp.wait_send()

    out = pl.pallas_call(
        body, name=name, in_specs=[ANY] * n, out_specs=[ANY] * n,
        out_shape=[jax.ShapeDtypeStruct(a.shape, a.dtype) for a in arrs],
        input_output_aliases={a: a for a in range(n)},
        scratch_shapes=[pltpu.SemaphoreType.DMA((3 * n,)), pltpu.SemaphoreType.DMA((3 * n,))],
    )(*arrs)
    return list(out)


def _flip(x, y, c, f):
    return ((1 - x) if f & 4 else x, (1 - y) if f & 2 else y, (1 - c) if f & 1 else c)


def _rs_copies(g_ref, land_ref, send, recv):
    x, y, c = lax.axis_index("x"), lax.axis_index("y"), lax.axis_index("c")
    cps = []
    for f in range(1, N_DEV):
        tx, ty, tcx = _flip(x, y, c, f)
        cps.append(_remote(g_ref.at[4 * tx + 2 * ty + tcx], land_ref.at[f - 1], send.at[f - 1], recv.at[f - 1],
                           (tx, ty, tcx)))
    return cps


def _rs_start(g, *, name):
    _, pr, pc = g.shape
    land_shape = (N_DEV - 1, pr, pc)

    def body(g_ref, land_ref, send, recv, g_thru, land_thru, token):
        for cp in _rs_copies(g_ref, land_ref, send, recv):
            cp.start()
        token[...] = jnp.zeros_like(token)

    sems = pltpu.SemaphoreType.DMA((N_DEV - 1,))
    return pl.pallas_call(
        body, name=name,
        out_shape=(sems, sems, pltpu.HBM(g.shape, g.dtype), pltpu.HBM(land_shape, g.dtype),
                   jax.ShapeDtypeStruct((SUBLANES, LANES), F32)),
        in_specs=(HBM, HBM), out_specs=(SEMS, SEMS, HBM, HBM, pl.BlockSpec(memory_space=pltpu.VMEM)),
        input_output_aliases={0: 2, 1: 3},
        compiler_params=pltpu.CompilerParams(has_side_effects=EFFECT),
    )(pltpu.with_memory_space_constraint(g, pltpu.HBM),
      pltpu.with_memory_space_constraint(lax.empty(land_shape, g.dtype), pltpu.HBM))


def _rs_wait(send, recv, g_thru, land_thru, after, *, name):
    def body(g_ref, land_ref, send, recv, after_ref, g_out, land_out):
        cps = _rs_copies(g_ref, land_ref, send, recv)
        for cp in cps:
            cp.wait_send()
        for cp in cps:
            cp.wait_recv()

    return pl.pallas_call(
        body, name=name,
        out_shape=(pltpu.HBM(g_thru.shape, g_thru.dtype), pltpu.HBM(land_thru.shape, land_thru.dtype)),
        in_specs=(HBM, HBM, SEMS, SEMS, ANY), out_specs=(HBM, HBM), input_output_aliases={0: 0, 1: 1},
        compiler_params=pltpu.CompilerParams(has_side_effects=EFFECT),
    )(g_thru, land_thru, send, recv, after)


def _pair_exchange(own, *, name):
    def body(own_ref, got_ref, send, recv):
        x, y, c = lax.axis_index("x"), lax.axis_index("y"), lax.axis_index("c")
        cp = _remote(own_ref, got_ref, send, recv, (x, y, 1 - c))
        cp.start()
        cp.wait_recv()
        cp.wait_send()

    return pl.pallas_call(
        body, name=name, in_specs=[ANY], out_specs=ANY, out_shape=jax.ShapeDtypeStruct(own.shape, own.dtype),
        scratch_shapes=[pltpu.SemaphoreType.DMA, pltpu.SemaphoreType.DMA],
    )(own)


def _allreduce_flat(vec, *, name):
    n = vec.shape[0]
    unit = N_DEV * SUBLANES * LANES
    npad = -(-n // unit) * unit
    rows = npad // (N_DEV * LANES)
    xin = jnp.pad(vec, (0, npad - n)).reshape(N_DEV, rows, LANES)

    def body(x_ref, y_ref, a_ref, send_a, recv_a, send_b, recv_b):
        x, y, c = lax.axis_index("x"), lax.axis_index("y"), lax.axis_index("c")
        me = 4 * x + 2 * y + c
        a_ref[me] = x_ref[me]
        sends, recvs = [], []
        for f in range(1, N_DEV):
            dev = _flip(x, y, c, f)
            t = 4 * dev[0] + 2 * dev[1] + dev[2]
            cp = _remote(x_ref.at[t], a_ref.at[me], send_a.at[f - 1], recv_a.at[f - 1], dev)
            cp.start()
            sends.append(cp)
            recvs.append(_remote(x_ref.at[me], a_ref.at[t], send_a.at[f - 1], recv_a.at[f - 1], dev))
        for cp in recvs:
            cp.wait_recv()
        for cp in sends:
            cp.wait_send()
        acc = a_ref[0]
        for s in range(1, N_DEV):
            acc = acc + a_ref[s]
        y_ref[me] = acc
        sends, recvs = [], []
        for f in range(1, N_DEV):
            dev = _flip(x, y, c, f)
            t = 4 * dev[0] + 2 * dev[1] + dev[2]
            cp = _remote(y_ref.at[me], y_ref.at[me], send_b.at[f - 1], recv_b.at[f - 1], dev)
            cp.start()
            sends.append(cp)
            recvs.append(_remote(y_ref.at[me], y_ref.at[t], send_b.at[f - 1], recv_b.at[f - 1], dev))
        for cp in recvs:
            cp.wait_recv()
        for cp in sends:
            cp.wait_send()

    vm = pl.BlockSpec(memory_space=pltpu.VMEM)
    out = pl.pallas_call(
        body, name=name, in_specs=[vm], out_specs=vm,
        out_shape=jax.ShapeDtypeStruct((N_DEV, rows, LANES), F32),
        scratch_shapes=[pltpu.VMEM((N_DEV, rows, LANES), F32)] + [pltpu.SemaphoreType.DMA((N_DEV - 1,))] * 4,
        compiler_params=_cp(),
    )(xin)
    return out.reshape(npad)[:n]


def _perm_cols(v, blocks=N_CHIPS):
    lead, n = v.shape[:-1], v.shape[-1]
    return v.reshape(lead + (blocks, n // blocks))[..., PERM, :].reshape(lead + (n,))


def _pack(arrs):
    return jnp.concatenate([a.reshape(-1).astype(F32) for a in arrs])


def _unpack(flat, shapes):
    out, pos = [], 0
    for s in shapes:
        n = 1
        for d in s:
            n *= d
        out.append(flat[pos:pos + n].reshape(s))
        pos += n
    return out


def kernel(x, conv_w_in, conv_b_in, conv_w_dw, conv_b_dw, conv_ln_g, conv_ln_b, conv_w_out, conv_b_out, gmlp_w_in, gmlp_b_in, gmlp_ln_g, gmlp_ln_b, gmlp_w_s, gmlp_b_s, gmlp_w_out, gmlp_b_out, ffn_w_up, ffn_b_up, ffn_w_dw, ffn_b_dw, ffn_w_down, ffn_b_down, norm1_g, norm1_b, norm2_g, norm2_b, loss_target, m_conv_w_in, m_conv_b_in, m_conv_w_dw, m_conv_b_dw, m_conv_ln_g, m_conv_ln_b, m_conv_w_out, m_conv_b_out, m_gmlp_w_in, m_gmlp_b_in, m_gmlp_ln_g, m_gmlp_ln_b, m_gmlp_w_s, m_gmlp_b_s, m_gmlp_w_out, m_gmlp_b_out, m_ffn_w_up, m_ffn_b_up, m_ffn_w_dw, m_ffn_b_dw, m_ffn_w_down, m_ffn_b_down, m_norm1_g, m_norm1_b, m_norm2_g, m_norm2_b, v_conv_w_in, v_conv_b_in, v_conv_w_dw, v_conv_b_dw, v_conv_ln_g, v_conv_ln_b, v_conv_w_out, v_conv_b_out, v_gmlp_w_in, v_gmlp_b_in, v_gmlp_ln_g, v_gmlp_ln_b, v_gmlp_w_s, v_gmlp_b_s, v_gmlp_w_out, v_gmlp_b_out, v_ffn_w_up, v_ffn_b_up, v_ffn_w_dw, v_ffn_b_dw, v_ffn_w_down, v_ffn_b_down, v_norm1_g, v_norm1_b, v_norm2_g, v_norm2_b):
    P = dict(locals())
    WEIGHTS = ['conv_w_in', 'conv_b_in', 'conv_w_dw', 'conv_b_dw', 'conv_ln_g', 'conv_ln_b', 'conv_w_out',
               'conv_b_out', 'gmlp_w_in', 'gmlp_b_in', 'gmlp_ln_g', 'gmlp_ln_b', 'gmlp_w_s', 'gmlp_b_s',
               'gmlp_w_out', 'gmlp_b_out', 'ffn_w_up', 'ffn_b_up', 'ffn_w_dw', 'ffn_b_dw', 'ffn_w_down',
               'ffn_b_down', 'norm1_g', 'norm1_b', 'norm2_g', 'norm2_b']
    BIG = ['conv_w_in', 'conv_w_out', 'gmlp_w_in', 'gmlp_w_out', 'ffn_w_up', 'ffn_w_down']
    SMALL_SHARDED = {'conv_w_dw': 2, 'gmlp_b_in': 1, 'gmlp_ln_g': 1, 'gmlp_ln_b': 1, 'gmlp_b_out': 1, 'ffn_w_dw': 2}

    B, S, D = x.shape
    T = B * S
    depth = norm1_g.shape[0]
    alpha = (2.0 * depth) ** 0.25
    C = conv_w_out.shape[-1]
    F2 = ffn_b_up.shape[-1]
    G, L = gmlp_w_s.shape[1], gmlp_w_s.shape[2]
    xi, yi, ci = lax.axis_index("x"), lax.axis_index("y"), lax.axis_index("c")
    shard = 2 * xi + yi

    i32 = lambda v: jnp.reshape(v, (1,)).astype(jnp.int32)
    pos_plain, pos_perm = i32(shard), i32(_perm_idx(shard))
    me_id, core_id = i32(4 * xi + 2 * yi + ci), i32(ci)

    groups = []
    for i in range(depth):
        mix = 'conv' if i % 2 == 0 else 'gmlp'
        groups.append((f"{mix}{i // 2}", [(mix + '_w_in', i // 2, 2, True), (mix + '_w_out', i // 2, 1, False)]))
        groups.append((f"ffn{i}", [('ffn_w_up', i, 2, True), ('ffn_w_down', i, 1, False)]))
    sm_names = list(SMALL_SHARDED)
    sm_shapes = [P[n].shape for n in sm_names]
    mine = _pack([P[n] for n in sm_names]) * (ci == 0).astype(F32)
    buf = jnp.zeros((N_CHIPS, mine.shape[0]), F32)
    buf = lax.dynamic_update_slice(buf, mine[None], (shard, 0))
    gathered = _allreduce_flat(buf.reshape(-1), name="ag_small").reshape(N_CHIPS, -1)

    started, order = {}, [gathered]
    for gname, members in groups:
        placed = [_place_w(P[n], pos_perm if perm else pos_plain, l, axis=axis, name=f"place_{n}_{l}")
                  for n, l, axis, perm in members]
        kinds = [(axis, perm) for _, _, axis, perm in members]
        send, recv, arrs, token = _ag_start(placed, kinds, order, name=f"ag_start_{gname}")
        order = [token]
        started[gname] = (send, recv, arrs, kinds, [(n, l) for n, l, _, _ in members])
    wts = {}

    def arrive(gname, after):
        send, recv, arrs, kinds, keys = started[gname]
        arrs = _ag_wait(send, recv, arrs, kinds, after, name=f"ag_wait_{gname}")
        arrs = _ag_forward(arrs, kinds, name=f"ag_fwd_{gname}")
        wts.update(zip(keys, arrs))

    full = {}
    for n, parts in zip(sm_names, zip(*[_unpack(gathered[k], sm_shapes) for k in range(N_CHIPS)])):
        full[n] = jnp.concatenate(parts, axis=SMALL_SHARDED[n])
    for n in WEIGHTS:
        if n not in BIG and n not in full:
            full[n] = P[n]

    assert G * L == C, "a gMLP group must be as wide as a chunk is long"

    def row(v):
        return v.reshape(1, -1)

    def pad_rows(v, r):
        return jnp.pad(v, ((0, r - v.shape[0]), (0, 0)))

    xf = x.reshape(T, D)
    saved = []
    cur, cur_b = xf, xf.astype(_MXU)
    for i in range(depth):
        j = i // 2
        sv = {'x': cur, 'xb': cur_b}
        arrive(groups[2 * i][0], order if i == 0 else [cur_b])
        if i % 2 == 0:
            b_in = row(_perm_cols(full['conv_b_in'][j]))
            h1 = _mm(cur_b, wts['conv_w_in', j], bl=0, bias=b_in, tm=_tile(T, 512), tn=_tile(2 * C, 1024, LANES),
                     tk=D, name=f"conv_in_{j}", n_outer=True)
            wdw = pad_rows(full['conv_w_dw'][j], CONV_TAPS_PAD)
            dwo = _conv_fwd(h1, wdw, row(full['conv_b_dw'][j]), B=B, S=S, name=f"conv_dw_{j}")
            s_act, xhc, rsc = _ln_silu_fwd(dwo, row(full['conv_ln_g'][j]), row(full['conv_ln_b'][j]),
                                           name=f"conv_ln_{j}")
            sv.update(h1=h1, wdw=wdw, act=s_act, xhc=xhc, rsc=rsc)
            y1 = _mm_res_ln(s_act, wts['conv_w_out', j], 0, row(full['conv_b_out'][j]), cur, alpha, row(norm1_g[i]),
                            row(norm1_b[i]), name=f"conv_out_ln_{j}")
        else:
            b_in = row(_perm_cols(full['gmlp_b_in'][j]))
            pre = _mm(cur_b, wts['gmlp_w_in', j], bl=0, bias=b_in, tm=_tile(T, 512), tn=_tile(2 * C, 1024, LANES),
                      tk=D, name=f"gmlp_in_{j}", n_outer=True)
            bsb = jnp.repeat(gmlp_b_s[j].T, L, axis=1)
            us, xhv, rsv = _gmlp_gate_fwd(pre, row(full['gmlp_ln_g'][j]), row(full['gmlp_ln_b'][j]), gmlp_w_s[j],
                                          bsb, name=f"gmlp_gate_{j}")
            sv.update(pre=pre, bsb=bsb, act=us, xhv=xhv, rsv=rsv)
            y1 = _mm_res_ln(us, wts['gmlp_w_out', j], 0, row(full['gmlp_b_out'][j]), cur, alpha, row(norm1_g[i]),
                            row(norm1_b[i]), name=f"gmlp_out_ln_{j}")
        x1, x1b, xh1, rs1 = y1
        arrive(groups[2 * i + 1][0], [x1b])
        wdw3 = pad_rows(_perm_cols(full['ffn_w_dw'][i]), SUBLANES)
        bdw3 = row(_perm_cols(ffn_b_dw[i]))
        hs, hcs, f_act = _ffn_up_fwd(x1b, wts['ffn_w_up', i], 0, row(_perm_cols(ffn_b_up[i])), wdw3, bdw3, S=S,
                                     name=f"ffn_up_{i}")
        x2, x2b, xh2, rs2 = _mm_res_ln(f_act, wts['ffn_w_down', i], 0, row(ffn_b_down[i]), x1, alpha, row(norm2_g[i]),
                                       row(norm2_b[i]), name=f"ffn_down_ln_{i}")
        sv.update(x1=x1, x1b=x1b, xh1=xh1, rs1=rs1, hs=hs, hcs=hcs, f=f_act, wdw3=wdw3, xh2=xh2, rs2=rs2)
        saved.append(sv)
        cur, cur_b = x2, x2b

    sg = {n: [None] * full[n].shape[0] for n in WEIGHTS if n not in BIG}
    inflight = {n: [None] * P[n].shape[0] for n in BIG}
    deps = []
    tgt = loss_target.reshape(T, D)
    dcur = None
    loss_part = None
    tk_t = _tile(T, 2048)

    def wgrad(n, l, a_, b_, **kw):
        g = _mm(a_, b_, ta=True, out_dtype=_WIRE, tk=tk_t, name=f"{n}_dw_{l}", deps=deps, **kw)
        send, recv, g_thru, land, token = _rs_start(g, name=f"rs_start_{n}_{l}")
        inflight[n][l] = (send, recv, g_thru, land)
        deps.append(token)

    for i in reversed(range(depth)):
        j = i // 2
        sv = saved[i]
        if dcur is None:
            dz2, dz2b, dg, db, cs, ls = _ln_bwd(cur, sv['xh2'], sv['rs2'], row(norm2_g[i]), target=tgt,
                                                name=f"ln2_bwd_head_{i}")
            loss_part = ls
        else:
            dz2, dz2b, dg, db, cs = _ln_bwd(dcur, sv['xh2'], sv['rs2'], row(norm2_g[i]), name=f"ln2_bwd_{i}")
        sg['norm2_g'][i], sg['norm2_b'][i], sg['ffn_b_down'][i] = dg.sum(0), db.sum(0), cs.sum(0)
        Fh = F2 // 2
        wgrad('ffn_w_down', i, sv['f'], dz2b, tm=Fh // 2, tn=_tile(D, 1024, LANES), pieces=('row',))
        dh, csu, dwd, dbd = _ffn_bwd(dz2b, wts['ffn_w_down', i], 0, sv['hs'], sv['hcs'], sv['wdw3'], S=S,
                                     name=f"ffn_bwd_{i}")
        sg['ffn_b_up'][i] = _perm_cols(csu.sum(0))
        sg['ffn_w_dw'][i] = _perm_cols(dwd.sum(1))
        sg['ffn_b_dw'][i] = _perm_cols(dbd.sum(0))
        wgrad('ffn_w_up', i, sv['x1b'], dh, tm=D, tn=F2 // N_CHIPS, pieces=('col', True))
        dx1 = _mm(dh, wts['ffn_w_up', i], bl=0, tb=True, res=dz2, res_scale=alpha, tm=_tile(T, 512),
                  tn=_tile(D, 1024, LANES), tk=F2, name=f"ffn_dx_{i}", deps=deps)
        dz1, dz1b, dg, db, cs = _ln_bwd(dx1, sv['xh1'], sv['rs1'], row(norm1_g[i]), name=f"ln1_bwd_{i}")
        sg['norm1_g'][i], sg['norm1_b'][i] = dg.sum(0), db.sum(0)
        if i % 2 == 0:
            sg['conv_b_out'][j] = cs.sum(0)
            wgrad('conv_w_out', j, sv['act'], dz1b, tm=_tile(C, 1024), tn=_tile(D, 1024, LANES), pieces=('row',))
            ds = _mm(dz1b, wts['conv_w_out', j], bl=0, tb=True, tm=_tile(T, 512), tn=_tile(C, 1024, LANES), tk=_tile(D, 1024, LANES),
                     name=f"conv_ds_{j}", deps=deps)
            ddw, dg, db = _ln_silu_bwd(ds, sv['xhc'], sv['rsc'], row(full['conv_ln_g'][j]),
                                       row(full['conv_ln_b'][j]), name=f"conv_ln_bwd_{j}")
            sg['conv_ln_g'][j], sg['conv_ln_b'][j] = dg.sum(0), db.sum(0)
            dglu, dwk, dbk = _conv_bwd(ddw, sv['h1'], sv['wdw'], B=B, S=S, name=f"conv_dw_bwd_{j}")
            sg['conv_w_dw'][j] = dwk.sum(1)[:conv_w_dw.shape[1]]
            sg['conv_b_dw'][j] = dbk.sum(0)
            dh1, csi = _glu_bwd(dglu, sv['h1'], name=f"conv_glu_bwd_{j}")
            sg['conv_b_in'][j] = _perm_cols(csi.sum(0))
            fam = 'conv_w_in'
        else:
            sg['gmlp_b_out'][j] = cs.sum(0)
            wgrad('gmlp_w_out', j, sv['act'], dz1b, tm=_tile(C, 1024), tn=_tile(D, 1024, LANES), pieces=('row',))
            dus = _mm(dz1b, wts['gmlp_w_out', j], bl=0, tb=True, tm=_tile(T, 512), tn=_tile(C, 1024, LANES),
                      tk=_tile(D, 1024, LANES), name=f"gmlp_dus_{j}", deps=deps)
            dh1, dg, db, csi, dws, dbs = _gmlp_gate_bwd(dus, sv['pre'], sv['xhv'], sv['rsv'], row(full['gmlp_ln_g'][j]),
                                                        row(full['gmlp_ln_b'][j]), gmlp_w_s[j], sv['bsb'],
                                                        name=f"gmlp_gate_bwd_{j}")
            sg['gmlp_ln_g'][j], sg['gmlp_ln_b'][j] = dg.sum(0), db.sum(0)
            sg['gmlp_b_in'][j] = _perm_cols(csi.sum(0))
            sg['gmlp_w_s'][j] = dws
            sg['gmlp_b_s'][j] = dbs.reshape(L, G, L).sum(-1).T
            fam = 'gmlp_w_in'
        wgrad(fam, j, sv['xb'], dh1, tm=D, tn=(2 * C) // N_CHIPS, pieces=('col', True))
        dcur = _mm(dh1, wts[fam, j], bl=0, tb=True, res=dz1, res_scale=alpha, tm=_tile(T, 512),
                   tn=_tile(D, 1024, LANES), tk=2 * C, name=f"{fam}_dx_{j}", deps=deps)
    grad_x = dcur.reshape(B, S, D)

    small_names = [n for n in WEIGHTS if n not in BIG]
    small_full = [jnp.stack(sg[n]) for n in small_names]
    flat = _pack(small_full + [loss_part])
    red = _allreduce_flat(flat, name="ar_small")
    red_parts = _unpack(red, [a.shape for a in small_full] + [loss_part.shape])
    loss = (0.5 / D) * jnp.sum(red_parts[-1])
    grads = {}
    for n, g in zip(small_names, red_parts[:-1]):
        if n in SMALL_SHARDED:
            ax = SMALL_SHARDED[n]
            width = P[n].shape[ax]
            g = lax.dynamic_slice_in_dim(g, shard * width, width, axis=ax)
        grads[n] = g

    big_out = {}
    for n in ['ffn_w_down', 'ffn_w_up', 'gmlp_w_out', 'gmlp_w_in', 'conv_w_out', 'conv_w_in']:
        own = None
        n_layers = len(inflight[n])
        for l in reversed(range(n_layers)):
            send, recv, g_thru, land = inflight[n][l]
            pc_, r = _rs_wait(send, recv, g_thru, land, dcur, name=f"rs_wait_{n}_{l}")
            own = _sum_pieces(pc_, r, me_id, l, own, n_layers, name=f"sum_{n}_{l}")
        got = _pair_exchange(own, name=f"px_{n}")
        big_out[n] = _adam_halves(P[n], own, got, P['m_' + n], P['v_' + n], core_id, name=f"adam_{n}")

    shapes = [P[n].shape for n in small_names]
    n_small = sum(functools.reduce(lambda p_, d_: p_ * d_, s_, 1) for s_ in shapes)
    unit = SUBLANES * LANES
    npad = -(-n_small // unit) * unit

    def flat2d(arrs, fill=0.0):
        v = _pack(arrs)
        return jnp.pad(v, (0, npad - n_small), constant_values=fill).reshape(-1, LANES)

    dl, mo, vo = _adam(flat2d([P[n] for n in small_names]), flat2d([grads[n] for n in small_names]),
                       flat2d([P['m_' + n] for n in small_names]),
                       flat2d([P['v_' + n] for n in small_names], fill=1.0), name="adam_small")
    small_out = {n: [grads[n], None, None, None] for n in small_names}
    for k, t in enumerate((dl, mo, vo)):
        for n, a in zip(small_names, _unpack(t.reshape(-1), shapes)):
            small_out[n][k + 1] = a

    outs = [loss, grad_x]
    for k in range(4):
        for n in WEIGHTS:
            outs.append(big_out[n][k] if n in BIG else small_out[n][k])
    return tuple(outs)
```

```python
import functools

import jax
import jax.numpy as jnp
from jax import lax
from jax.experimental import pallas as pl
from jax.experimental.pallas import tpu as pltpu

F32 = jnp.float32
_MXU = jnp.bfloat16
_WIRE = jnp.bfloat16
_HDT = jnp.bfloat16
LN_EPS = 1e-5
ADAM_LR, ADAM_B1, ADAM_B2, ADAM_EPS, ADAM_WD, ADAM_STEP = 0.001, 0.9, 0.999, 1e-08, 0.01, 10
N_CHIPS = 4
N_DEV = 8
LANES = 128
SUBLANES = 8
CONV_TAPS_PAD = 32
VMEM_LIMIT = 56 << 20
MESH = pl.DeviceIdType.MESH
ANY = pl.BlockSpec(memory_space=pl.ANY)
HBM = pl.BlockSpec(memory_space=pltpu.HBM)
SEMS = pl.BlockSpec(memory_space=pltpu.SEMAPHORE)
EFFECT = pltpu.SideEffectType.DATAFLOW_SIDE_EFFECTING
PERM = (0, 2, 1, 3)


def _cp(sem=None):
    return pltpu.CompilerParams(dimension_semantics=sem, vmem_limit_bytes=VMEM_LIMIT)


def _tile(dim, pref, mult=SUBLANES):
    if dim <= pref:
        return dim
    t = (pref // mult) * mult
    while t > mult and dim % t:
        t -= mult
    assert dim % t == 0, (dim, pref, mult)
    return t


def _perm_idx(q):
    return (q % 2) * 2 + q // 2


def _fold8(t):
    r, n = t.shape
    return t.reshape(r // SUBLANES, SUBLANES, n).sum(axis=0)


def _ln_rows(z, g, b):
    mu = jnp.mean(z, axis=-1, keepdims=True)
    xc = z - mu
    var = jnp.mean(xc * xc, axis=-1, keepdims=True)
    rstd = lax.rsqrt(var + LN_EPS)
    xh = xc * rstd
    return xh * g + b, xh, rstd


def _ln_bwd_rows(dy, xh, rstd, g):
    dxh = dy * g
    m1 = jnp.mean(dxh, axis=-1, keepdims=True)
    m2 = jnp.mean(dxh * xh, axis=-1, keepdims=True)
    return rstd * (dxh - m1 - xh * m2)


def _sigmoid(v):
    return 1.0 / (1.0 + jnp.exp(-v))


def _gelu_parts(p):
    cdf = 0.5 * (1.0 + lax.erf(p * 0.7071067811865476))
    pdf = jnp.exp(-0.5 * p * p) * 0.3989422804014327
    return p * cdf, cdf + p * pdf


def _shift_down(prev8, t, s):
    ext = jnp.concatenate([prev8, t], axis=0)
    return pltpu.roll(ext, s, 0)[SUBLANES:]


def _shift_up(t, next8, s):
    n = t.shape[0]
    ext = jnp.concatenate([t, next8], axis=0)
    return pltpu.roll(ext, n + SUBLANES - s, 0)[:n]


def _mm(a, b, *, ta=False, tb=False, bl=None, bias=None, res=None, res_scale=1.0, out_dtype=F32,
        tm, tn, tk, name, pieces=None, deps=None, n_outer=False):
    M, K = (a.shape[1], a.shape[0]) if ta else a.shape
    bs = b.shape[1:] if bl is not None else b.shape
    N, Kb = (bs[0], bs[1]) if tb else (bs[1], bs[0])
    assert K == Kb and M % tm == 0 and N % tn == 0 and K % tk == 0, (a.shape, b.shape, tm, tn, tk)
    gm, gn, gk = M // tm, N // tn, K // tk

    def spec(block, imap):
        if n_outer:
            return pl.BlockSpec(block, lambda j, i, k: imap(i, j, k))
        return pl.BlockSpec(block, imap)

    a_spec = spec((tk, tm), lambda i, j, k: (k, i)) if ta else spec((tm, tk), lambda i, j, k: (i, k))
    bblk = (tn, tk) if tb else (tk, tn)
    bmap = (lambda i, j, k: (j, k)) if tb else (lambda i, j, k: (k, j))
    if bl is not None:
        b_spec = spec((None,) + bblk, lambda i, j, k: (bl,) + bmap(i, j, k))
    else:
        b_spec = spec(bblk, bmap)
    in_specs, operands = [a_spec, b_spec], [a, b]
    if bias is not None:
        in_specs.append(spec((1, tn), lambda i, j, k: (0, j)))
        operands.append(bias)
    if res is not None:
        in_specs.append(spec((tm, tn), lambda i, j, k: (i, j)))
        operands.append(res)
    n_dep = len(deps) if deps else 0
    if n_dep:
        in_specs += [ANY] * n_dep
        operands += deps
        del deps[:]
    if pieces is None:
        out_shape = jax.ShapeDtypeStruct((M, N), out_dtype)
        out_spec = spec((tm, tn), lambda i, j, k: (i, j))
        ppb = pr = None
    elif pieces[0] == 'col':
        pr, pc = M // 2, N // N_CHIPS
        assert tm % pr == 0 and pc % tn == 0
        ppb, per = tm // pr, pc // tn
        perm = pieces[1]
        out_shape = jax.ShapeDtypeStruct((N_DEV, pr, pc), out_dtype)
        out_spec = spec(
            (ppb, pr, tn),
            lambda i, j, k: ((2 * (_perm_idx(j // per) if perm else j // per)) // ppb + i, 0, j % per))
    else:
        pr = M // N_DEV
        assert tm % pr == 0
        ppb = tm // pr
        out_shape = jax.ShapeDtypeStruct((N_DEV, pr, N), out_dtype)
        out_spec = spec((ppb, pr, tn), lambda i, j, k: (i, 0, j))
    dims = (((0 if ta else 1,), (1 if tb else 0,)), ((), ()))

    def body(*refs):
        a_ref, b_ref = refs[0], refs[1]
        pos = 2
        bias_ref = res_ref = None
        if bias is not None:
            bias_ref = refs[pos]
            pos += 1
        if res is not None:
            res_ref = refs[pos]
            pos += 1
        pos += n_dep
        o_ref = refs[pos]

        def finish(r):
            if bias_ref is not None:
                r = r + bias_ref[...]
            if res_ref is not None:
                r = r + res_scale * res_ref[...]
            if pieces is not None:
                r = r.reshape(ppb, pr, tn)
            o_ref[...] = r.astype(out_dtype)

        part = lax.dot_general(a_ref[...].astype(_MXU), b_ref[...].astype(_MXU), dims, preferred_element_type=F32)
        if gk == 1:
            finish(part)
            return
        acc_ref = refs[pos + 1]
        k = pl.program_id(2)

        @pl.when(k == 0)
        def _():
            acc_ref[...] = part

        @pl.when((k > 0) & (k < gk - 1))
        def _():
            acc_ref[...] += part

        @pl.when(k == gk - 1)
        def _():
            finish(acc_ref[...] + part)

    return pl.pallas_call(
        body, name=name, grid=(gn, gm, gk) if n_outer else (gm, gn, gk), in_specs=in_specs, out_specs=out_spec,
        out_shape=out_shape, scratch_shapes=[pltpu.VMEM((tm, tn), F32)] if gk > 1 else [],
        compiler_params=_cp(("parallel", "parallel", "arbitrary")),
    )(*operands)


def _mm_ln_bwd(a, w, res, res_scale, xh, rstd, g, *, name, deps=None):
    T, K = a.shape
    D = w.shape[1]
    tm = _tile(T, 512)
    n_dep = len(deps) if deps else 0

    def body(a_ref, w_ref, res_ref, xh_ref, rs_ref, g_ref, *rest):
        dz_ref, dzb_ref, dg_ref, db_ref, cs_ref = rest[n_dep:]

        @pl.when(pl.program_id(0) == 0)
        def _():
            dg_ref[...] = jnp.zeros_like(dg_ref)
            db_ref[...] = jnp.zeros_like(db_ref)
            cs_ref[...] = jnp.zeros_like(cs_ref)

        d = lax.dot_general(a_ref[...].astype(_MXU), w_ref[...].astype(_MXU), (((1,), (1,)), ((), ())),
                            preferred_element_type=F32) + res_scale * res_ref[...]
        xh = xh_ref[...]
        dz = _ln_bwd_rows(d, xh, rs_ref[...], g_ref[...])
        dz_ref[...] = dz
        dzb_ref[...] = dz.astype(_MXU)
        dg_ref[...] += _fold8(d * xh)
        db_ref[...] += _fold8(d)
        cs_ref[...] += _fold8(dz)

    row = lambda i: (i, 0)
    fixed = lambda i: (0, 0)
    tile = pl.BlockSpec((tm, D), row)
    part = pl.BlockSpec((SUBLANES, D), fixed)
    operands = [a, w, res, xh, rstd, g] + (list(deps) if deps else [])
    if deps:
        del deps[:]
    return pl.pallas_call(
        body, name=name, grid=(T // tm,),
        in_specs=[pl.BlockSpec((tm, K), row),
                  pl.BlockSpec((None, D, K), lambda i: (0, 0, 0), pipeline_mode=pl.Buffered(1)),
                  tile, tile, pl.BlockSpec((tm, 1), row), pl.BlockSpec((1, D), fixed)] + [ANY] * n_dep,
        out_specs=[tile, tile, part, part, part],
        out_shape=[jax.ShapeDtypeStruct((T, D), F32), jax.ShapeDtypeStruct((T, D), _MXU)]
        + [jax.ShapeDtypeStruct((SUBLANES, D), F32)] * 3,
        compiler_params=_cp(("arbitrary",)),
    )(*operands)


def _mm_res_ln(a, w, wl, bias, res, alpha, g, b, *, name):
    T, K = a.shape
    D = w.shape[-1]
    tm = _tile(T, 256)

    def body(a_ref, w_ref, bias_ref, res_ref, g_ref, b_ref, y_ref, yb_ref, xh_ref, rs_ref):
        z = jnp.dot(a_ref[...].astype(_MXU), w_ref[...].astype(_MXU), preferred_element_type=F32)
        z = z + bias_ref[...] + alpha * res_ref[...]
        y, xh, rstd = _ln_rows(z, g_ref[...], b_ref[...])
        y_ref[...] = y
        yb_ref[...] = y.astype(_MXU)
        xh_ref[...] = xh
        rs_ref[...] = rstd

    row = lambda i: (i, 0)
    vec = pl.BlockSpec((1, D), lambda i: (0, 0))
    return pl.pallas_call(
        body, name=name, grid=(T // tm,),
        in_specs=[pl.BlockSpec((tm, K), row), pl.BlockSpec((None, K, D), lambda i: (wl, 0, 0)), vec,
                  pl.BlockSpec((tm, D), row), vec, vec],
        out_specs=[pl.BlockSpec((tm, D), row), pl.BlockSpec((tm, D), row), pl.BlockSpec((tm, D), row),
                   pl.BlockSpec((tm, 1), row)],
        out_shape=[jax.ShapeDtypeStruct((T, D), F32), jax.ShapeDtypeStruct((T, D), _MXU),
                   jax.ShapeDtypeStruct((T, D), F32), jax.ShapeDtypeStruct((T, 1), F32)],
        compiler_params=_cp(("parallel",)),
    )(a, w, bias, res, g, b)


def _ln_bwd(dy, xh, rstd, g, *, name, target=None):
    T, D = dy.shape
    tm = _tile(T, 256)
    head = target is not None

    def body(*refs):
        if head:
            dy_ref, t_ref, xh_ref, rs_ref, g_ref, dz_ref, dzb_ref, dg_ref, db_ref, cs_ref, ls_ref = refs
        else:
            dy_ref, xh_ref, rs_ref, g_ref, dz_ref, dzb_ref, dg_ref, db_ref, cs_ref = refs
        i = pl.program_id(0)

        @pl.when(i == 0)
        def _():
            dg_ref[...] = jnp.zeros_like(dg_ref)
            db_ref[...] = jnp.zeros_like(db_ref)
            cs_ref[...] = jnp.zeros_like(cs_ref)
            if head:
                ls_ref[...] = jnp.zeros_like(ls_ref)

        d = dy_ref[...]
        if head:
            err = d - t_ref[...]
            ls_ref[...] += _fold8(err * err)
            d = err * (1.0 / D)
        xh = xh_ref[...]
        dz = _ln_bwd_rows(d, xh, rs_ref[...], g_ref[...])
        dz_ref[...] = dz
        dzb_ref[...] = dz.astype(_MXU)
        dg_ref[...] += _fold8(d * xh)
        db_ref[...] += _fold8(d)
        cs_ref[...] += _fold8(dz)

    row = lambda i: (i, 0)
    fixed = lambda i: (0, 0)
    tile = pl.BlockSpec((tm, D), row)
    part = pl.BlockSpec((SUBLANES, D), fixed)
    in_specs = [tile] + ([tile] if head else []) + [tile, pl.BlockSpec((tm, 1), row), pl.BlockSpec((1, D), fixed)]
    n_part = 4 if head else 3
    operands = [dy] + ([target] if head else []) + [xh, rstd, g]
    return pl.pallas_call(
        body, name=name, grid=(T // tm,), in_specs=in_specs,
        out_specs=[tile, tile] + [part] * n_part,
        out_shape=[jax.ShapeDtypeStruct((T, D), F32), jax.ShapeDtypeStruct((T, D), _MXU)]
        + [jax.ShapeDtypeStruct((SUBLANES, D), F32)] * n_part,
        compiler_params=_cp(("arbitrary",)),
    )(*operands)


def _conv_cols(C, tc):
    per = (C // 2) // tc
    return per, (lambda j: (j // per) * (2 * per) + j % per)


def _glu_shifted(a_ref, g_ref, p_ref, S):
    u = a_ref[...] * _sigmoid(g_ref[...])
    rows = lax.broadcasted_iota(jnp.int32, u.shape, 0)
    for r in range(SUBLANES):
        p_ref[r, 0:CONV_TAPS_PAD, :] = jnp.zeros((CONV_TAPS_PAD, u.shape[1]), F32)
        p_ref[r, CONV_TAPS_PAD:CONV_TAPS_PAD + S, :] = u if r == 0 else jnp.where(rows >= r, pltpu.roll(u, r, 0), 0.0)


def _conv_fwd(h1, w_dw, b_dw, *, B, S, name):
    C = w_dw.shape[1]
    taps = CONV_TAPS_PAD - 1
    tc = LANES
    ch = _tile(S, 128)
    per, col_a = _conv_cols(C, tc)

    def body(a_ref, g_ref, w_ref, b_ref, o_ref, p_ref):
        _glu_shifted(a_ref, g_ref, p_ref, S)

        def chunk(ci, carry):
            base = pl.multiple_of(ci * ch, ch)
            acc = jnp.zeros((ch, tc), F32) + b_ref[...]
            for k in range(taps):
                q, r = divmod(taps - 1 - k, SUBLANES)
                start = pl.multiple_of(base + (CONV_TAPS_PAD - SUBLANES * q), SUBLANES)
                acc = acc + w_ref[pl.ds(k, 1), :] * p_ref[r, pl.ds(start, ch), :]
            o_ref[pl.ds(base, ch), :] = acc
            return carry

        lax.fori_loop(0, S // ch, chunk, 0)

    return pl.pallas_call(
        body, name=name, grid=(B, C // tc),
        in_specs=[pl.BlockSpec((S, tc), lambda b, j: (b, col_a(j))),
                  pl.BlockSpec((S, tc), lambda b, j: (b, col_a(j) + per)),
                  pl.BlockSpec((CONV_TAPS_PAD, tc), lambda b, j: (0, j)),
                  pl.BlockSpec((1, tc), lambda b, j: (0, j))],
        out_specs=pl.BlockSpec((S, tc), lambda b, j: (b, j)),
        out_shape=jax.ShapeDtypeStruct((B * S, C), F32),
        scratch_shapes=[pltpu.VMEM((SUBLANES, S + CONV_TAPS_PAD, tc), F32)],
        compiler_params=_cp(("parallel", "parallel")),
    )(h1, h1, w_dw, b_dw)


def _conv_bwd(dd, h1, w_dw, *, B, S, name):
    C = w_dw.shape[1]
    taps = CONV_TAPS_PAD - 1
    tc = LANES
    ch = _tile(S, 128)
    per, col_a = _conv_cols(C, tc)

    def body(d_ref, a_ref, g_ref, w_ref, du_ref, dw_ref, db_ref, p_ref, q_ref):
        b = pl.program_id(1)

        @pl.when(b == 0)
        def _():
            dw_ref[...] = jnp.zeros_like(dw_ref)
            db_ref[...] = jnp.zeros_like(db_ref)

        _glu_shifted(a_ref, g_ref, p_ref, S)
        d = d_ref[...]
        rows = lax.broadcasted_iota(jnp.int32, d.shape, 0)
        for r in range(SUBLANES):
            q_ref[r, S:S + CONV_TAPS_PAD, :] = jnp.zeros((CONV_TAPS_PAD, tc), F32)
            q_ref[r, 0:S, :] = d if r == 0 else jnp.where(rows < S - r, pltpu.roll(d, S - r, 0), 0.0)
        db_ref[...] += _fold8(d)

        def chunk(ci, carry):
            base = pl.multiple_of(ci * ch, ch)
            dch = d_ref[pl.ds(base, ch), :]
            acc = jnp.zeros((ch, tc), F32)
            for k in range(taps):
                q, r = divmod(taps - 1 - k, SUBLANES)
                up = pl.multiple_of(base + SUBLANES * q, SUBLANES)
                acc = acc + w_ref[pl.ds(k, 1), :] * q_ref[r, pl.ds(up, ch), :]
                down = pl.multiple_of(base + (CONV_TAPS_PAD - SUBLANES * q), SUBLANES)
                dw_ref[k] += _fold8(dch * p_ref[r, pl.ds(down, ch), :])
            du_ref[pl.ds(base, ch), :] = acc
            return carry

        lax.fori_loop(0, S // ch, chunk, 0)

    return pl.pallas_call(
        body, name=name, grid=(C // tc, B),
        in_specs=[pl.BlockSpec((S, tc), lambda j, b: (b, j)),
                  pl.BlockSpec((S, tc), lambda j, b: (b, col_a(j))),
                  pl.BlockSpec((S, tc), lambda j, b: (b, col_a(j) + per)),
                  pl.BlockSpec((CONV_TAPS_PAD, tc), lambda j, b: (0, j))],
        out_specs=[pl.BlockSpec((S, tc), lambda j, b: (b, j)),
                   pl.BlockSpec((CONV_TAPS_PAD, SUBLANES, tc), lambda j, b: (0, 0, j)),
                   pl.BlockSpec((SUBLANES, tc), lambda j, b: (0, j))],
        out_shape=[jax.ShapeDtypeStruct((B * S, C), F32),
                   jax.ShapeDtypeStruct((CONV_TAPS_PAD, SUBLANES, C), F32),
                   jax.ShapeDtypeStruct((SUBLANES, C), F32)],
        scratch_shapes=[pltpu.VMEM((SUBLANES, S + CONV_TAPS_PAD, tc), F32),
                        pltpu.VMEM((SUBLANES, S + CONV_TAPS_PAD, tc), F32)],
        compiler_params=_cp(("parallel", "arbitrary")),
    )(dd, h1, h1, w_dw)


def _ln_silu_fwd(v, g, b, *, name):
    T, C = v.shape
    tm = _tile(T, 512)

    def body(v_ref, g_ref, b_ref, s_ref, xh_ref, rs_ref):
        y, xh, rstd = _ln_rows(v_ref[...], g_ref[...], b_ref[...])
        s_ref[...] = (y * _sigmoid(y)).astype(_MXU)
        xh_ref[...] = xh
        rs_ref[...] = rstd

    row = lambda i: (i, 0)
    vec = pl.BlockSpec((1, C), lambda i: (0, 0))
    return pl.pallas_call(
        body, name=name, grid=(T // tm,),
        in_specs=[pl.BlockSpec((tm, C), row), vec, vec],
        out_specs=[pl.BlockSpec((tm, C), row), pl.BlockSpec((tm, C), row), pl.BlockSpec((tm, 1), row)],
        out_shape=[jax.ShapeDtypeStruct((T, C), _MXU), jax.ShapeDtypeStruct((T, C), F32),
                   jax.ShapeDtypeStruct((T, 1), F32)],
        compiler_params=_cp(("parallel",)),
    )(v, g, b)


def _ln_silu_bwd(ds, xh, rstd, g, b, *, name):
    T, C = ds.shape
    tm = _tile(T, 256)

    def body(ds_ref, xh_ref, rs_ref, g_ref, b_ref, dv_ref, dg_ref, db_ref):
        @pl.when(pl.program_id(0) == 0)
        def _():
            dg_ref[...] = jnp.zeros_like(dg_ref)
            db_ref[...] = jnp.zeros_like(db_ref)

        xh = xh_ref[...]
        gam = g_ref[...]
        y = xh * gam + b_ref[...]
        sig = _sigmoid(y)
        dln = ds_ref[...] * (sig * (1.0 + y * (1.0 - sig)))
        dv_ref[...] = _ln_bwd_rows(dln, xh, rs_ref[...], gam)
        dg_ref[...] += _fold8(dln * xh)
        db_ref[...] += _fold8(dln)

    row = lambda i: (i, 0)
    fixed = lambda i: (0, 0)
    vec = pl.BlockSpec((1, C), fixed)
    part = pl.BlockSpec((SUBLANES, C), fixed)
    return pl.pallas_call(
        body, name=name, grid=(T // tm,),
        in_specs=[pl.BlockSpec((tm, C), row), pl.BlockSpec((tm, C), row), pl.BlockSpec((tm, 1), row), vec, vec],
        out_specs=[pl.BlockSpec((tm, C), row), part, part],
        out_shape=[jax.ShapeDtypeStruct((T, C), F32)] + [jax.ShapeDtypeStruct((SUBLANES, C), F32)] * 2,
        compiler_params=_cp(("arbitrary",)),
    )(ds, xh, rstd, g, b)


def _glu_bwd(du, h1, *, name):
    T, C = du.shape
    il = C // 2
    tm = _tile(T, 256)

    def body(du_ref, h_ref, dh_ref, cs_ref):
        @pl.when(pl.program_id(0) == 0)
        def _():
            cs_ref[...] = jnp.zeros_like(cs_ref)

        for hb in range(2):
            a = h_ref[:, 2 * hb * il:(2 * hb + 1) * il]
            gate = h_ref[:, (2 * hb + 1) * il:(2 * hb + 2) * il]
            d = du_ref[:, hb * il:(hb + 1) * il]
            sig = _sigmoid(gate)
            da = d * sig
            dgate = d * a * sig * (1.0 - sig)
            dh_ref[:, 2 * hb * il:(2 * hb + 1) * il] = da.astype(_MXU)
            dh_ref[:, (2 * hb + 1) * il:(2 * hb + 2) * il] = dgate.astype(_MXU)
            cs_ref[:, 2 * hb * il:(2 * hb + 1) * il] += _fold8(da)
            cs_ref[:, (2 * hb + 1) * il:(2 * hb + 2) * il] += _fold8(dgate)

    row = lambda i: (i, 0)
    return pl.pallas_call(
        body, name=name, grid=(T // tm,),
        in_specs=[pl.BlockSpec((tm, C), row), pl.BlockSpec((tm, 2 * C), row)],
        out_specs=[pl.BlockSpec((tm, 2 * C), row), pl.BlockSpec((SUBLANES, 2 * C), lambda i: (0, 0))],
        out_shape=[jax.ShapeDtypeStruct((T, 2 * C), _MXU), jax.ShapeDtypeStruct((SUBLANES, 2 * C), F32)],
        compiler_params=_cp(("arbitrary",)),
    )(du, h1)


def _tril_mask(n):
    return lax.broadcasted_iota(jnp.int32, (n, n), 0) >= lax.broadcasted_iota(jnp.int32, (n, n), 1)


def _split_uv(t, il):
    u = jnp.concatenate([t[:, 0:il], t[:, 2 * il:3 * il]], axis=1)
    v = jnp.concatenate([t[:, il:2 * il], t[:, 3 * il:4 * il]], axis=1)
    return u, v


def _gmlp_gate_fwd(p, g, b, w_s, bsb, *, name):
    T, C2 = p.shape
    C = C2 // 2
    il = C // 2
    G, L, _ = w_s.shape
    assert G * L == C
    tm = _tile(T, 2 * L, L)

    def body(p_ref, g_ref, b_ref, ws_ref, bs_ref, us_ref, xh_ref, rs_ref, vn_ref, u_ref):
        z, _ = _gelu_parts(p_ref[...])
        u, v = _split_uv(z, il)
        vn, xh, rstd = _ln_rows(v, g_ref[...], b_ref[...])
        xh_ref[...] = xh
        rs_ref[...] = rstd
        vn_ref[...] = vn.astype(_MXU)
        u_ref[...] = u
        mask = _tril_mask(L)
        for gi in range(G):
            wc = jnp.where(mask, ws_ref[gi], 0.0).astype(_MXU)
            cols = slice(gi * L, (gi + 1) * L)
            for c in range(tm // L):
                rows = slice(c * L, (c + 1) * L)
                s = jnp.dot(wc, vn_ref[rows, cols], preferred_element_type=F32) + bs_ref[:, cols]
                us_ref[rows, cols] = (u_ref[rows, cols] * s).astype(_MXU)

    row = lambda i: (i, 0)
    fixed = lambda i: (0, 0)
    return pl.pallas_call(
        body, name=name, grid=(T // tm,),
        in_specs=[pl.BlockSpec((tm, C2), row), pl.BlockSpec((1, C), fixed), pl.BlockSpec((1, C), fixed),
                  pl.BlockSpec((G, L, L), lambda i: (0, 0, 0)), pl.BlockSpec((L, C), fixed)],
        out_specs=[pl.BlockSpec((tm, C), row), pl.BlockSpec((tm, C), row), pl.BlockSpec((tm, 1), row)],
        out_shape=[jax.ShapeDtypeStruct((T, C), _MXU), jax.ShapeDtypeStruct((T, C), F32),
                   jax.ShapeDtypeStruct((T, 1), F32)],
        scratch_shapes=[pltpu.VMEM((tm, C), _MXU), pltpu.VMEM((tm, C), F32)],
        compiler_params=_cp(("parallel",)),
    )(p, g, b, w_s, bsb)


def _gmlp_gate_bwd(dus, p, xh, rstd, g, b, w_s, bsb, *, name):
    T, C2 = p.shape
    C = C2 // 2
    il = C // 2
    G, L, _ = w_s.shape
    tm = _tile(T, 2 * L, L)

    def body(dus_ref, p_ref, xh_ref, rs_ref, g_ref, b_ref, ws_ref, bs_ref,
             dp_ref, dg_ref, db_ref, cs_ref, dws_ref, dbs_ref, vn_ref, u_ref, dvn_ref, du_ref):
        @pl.when(pl.program_id(0) == 0)
        def _():
            dg_ref[...] = jnp.zeros_like(dg_ref)
            db_ref[...] = jnp.zeros_like(db_ref)
            cs_ref[...] = jnp.zeros_like(cs_ref)
            dws_ref[...] = jnp.zeros_like(dws_ref)
            dbs_ref[...] = jnp.zeros_like(dbs_ref)

        z, gp = _gelu_parts(p_ref[...])
        u, _ = _split_uv(z, il)
        xh = xh_ref[...]
        gam = g_ref[...]
        vn_ref[...] = (xh * gam + b_ref[...]).astype(_MXU)
        u_ref[...] = u
        mask = _tril_mask(L)
        for gi in range(G):
            wc = jnp.where(mask, ws_ref[gi], 0.0).astype(_MXU)
            cols = slice(gi * L, (gi + 1) * L)
            for c in range(tm // L):
                rows = slice(c * L, (c + 1) * L)
                vnb = vn_ref[rows, cols]
                s = jnp.dot(wc, vnb, preferred_element_type=F32) + bs_ref[:, cols]
                d = dus_ref[rows, cols]
                du_ref[rows, cols] = d * s
                ds = d * u_ref[rows, cols]
                dbs_ref[:, cols] += ds
                dsb = ds.astype(_MXU)
                dw = lax.dot_general(dsb, vnb, (((1,), (1,)), ((), ())), preferred_element_type=F32)
                dws_ref[gi] += jnp.where(mask, dw, 0.0)
                dvn_ref[rows, cols] = lax.dot_general(wc, dsb, (((0,), (0,)), ((), ())), preferred_element_type=F32)
        dvn = dvn_ref[...]
        dg_ref[...] += _fold8(dvn * xh)
        db_ref[...] += _fold8(dvn)
        dv = _ln_bwd_rows(dvn, xh, rs_ref[...], gam)
        du = du_ref[...]
        for hb in range(2):
            for part, src in ((0, du), (1, dv)):
                lo = (2 * hb + part) * il
                dp = src[:, hb * il:(hb + 1) * il] * gp[:, lo:lo + il]
                dp_ref[:, lo:lo + il] = dp.astype(_MXU)
                cs_ref[:, lo:lo + il] += _fold8(dp)

    row = lambda i: (i, 0)
    fixed = lambda i: (0, 0)
    part_c = pl.BlockSpec((SUBLANES, C), fixed)
    return pl.pallas_call(
        body, name=name, grid=(T // tm,),
        in_specs=[pl.BlockSpec((tm, C), row), pl.BlockSpec((tm, C2), row), pl.BlockSpec((tm, C), row),
                  pl.BlockSpec((tm, 1), row), pl.BlockSpec((1, C), fixed), pl.BlockSpec((1, C), fixed),
                  pl.BlockSpec((G, L, L), lambda i: (0, 0, 0)), pl.BlockSpec((L, C), fixed)],
        out_specs=[pl.BlockSpec((tm, C2), row), part_c, part_c, pl.BlockSpec((SUBLANES, C2), fixed),
                   pl.BlockSpec((G, L, L), lambda i: (0, 0, 0)), pl.BlockSpec((L, C), fixed)],
        out_shape=[jax.ShapeDtypeStruct((T, C2), _MXU), jax.ShapeDtypeStruct((SUBLANES, C), F32),
                   jax.ShapeDtypeStruct((SUBLANES, C), F32), jax.ShapeDtypeStruct((SUBLANES, C2), F32),
                   jax.ShapeDtypeStruct((G, L, L), F32), jax.ShapeDtypeStruct((L, C), F32)],
        scratch_shapes=[pltpu.VMEM((tm, C), _MXU), pltpu.VMEM((tm, C), F32), pltpu.VMEM((tm, C), F32),
                        pltpu.VMEM((tm, C), F32)],
        compiler_params=_cp(("arbitrary",)),
    )(dus, p, xh, rstd, g, b, w_s, bsb)


def _ffn_conv(h, prev8, w_ref, b_ref):
    h1 = _shift_down(prev8, h, 1)
    h2 = _shift_down(prev8, h, 2)
    return w_ref[pl.ds(2, 1), :] * h + w_ref[pl.ds(1, 1), :] * h1 + w_ref[pl.ds(0, 1), :] * h2 + b_ref[...]


def _ffn_up_fwd(xb, w, wl, b_up, w_dw, b_dw, *, S, name):
    T, D = xb.shape
    N = w.shape[-1]
    tn = N // N_CHIPS
    tm = _tile(S, 256)
    spt = S // tm

    def body(x_ref, w_ref, bu_ref, wd_ref, bd_ref, h_ref, hc_ref, f_ref, carry_ref):
        i = pl.program_id(1)

        @pl.when(i % spt == 0)
        def _():
            carry_ref[...] = jnp.zeros_like(carry_ref)

        h = jnp.dot(x_ref[...].astype(_MXU), w_ref[...].astype(_MXU), preferred_element_type=F32) + bu_ref[...]
        hq = h.astype(_HDT)
        h_ref[...] = hq
        h = hq.astype(F32)
        hc = _ffn_conv(h, carry_ref[...], wd_ref, bd_ref)
        hc_ref[...] = hc.astype(_HDT)
        carry_ref[...] = h[tm - SUBLANES:tm]
        gte = hc[:, :tn]
        f_ref[...] = (gte * _sigmoid(gte) * hc[:, tn:]).astype(_MXU)

    pair = lambda j, i: (0, j)
    wide = pl.BlockSpec((tm, 2 * tn), lambda j, i: (i, j))
    return pl.pallas_call(
        body, name=name, grid=(2, T // tm),
        in_specs=[pl.BlockSpec((tm, D), lambda j, i: (i, 0)),
                  pl.BlockSpec((None, D, 2 * tn), lambda j, i: (wl, 0, j)),
                  pl.BlockSpec((1, 2 * tn), pair), pl.BlockSpec((SUBLANES, 2 * tn), pair),
                  pl.BlockSpec((1, 2 * tn), pair)],
        out_specs=[wide, wide, pl.BlockSpec((tm, tn), lambda j, i: (i, j))],
        out_shape=[jax.ShapeDtypeStruct((T, N), _HDT), jax.ShapeDtypeStruct((T, N), _HDT),
                   jax.ShapeDtypeStruct((T, N // 2), _MXU)],
        scratch_shapes=[pltpu.VMEM((SUBLANES, 2 * tn), F32)],
        compiler_params=_cp(("parallel", "arbitrary")),
    )(xb, w, b_up, w_dw, b_dw)


def _ffn_bwd(dzb, w_down, wl, hs, hcs, w_dw, *, S, name):
    T, D = dzb.shape
    N = hs.shape[1]
    tn = N // N_CHIPS
    tm = _tile(S, 256)
    spt = S // tm
    nt = T // tm

    def body(dz_ref, wd_ref, h_ref, hc_ref, wc_ref, dh_ref, cs_ref, dw_ref, db_ref, carry_ref):
        i = pl.program_id(1)
        ii = nt - 1 - i

        @pl.when(i == 0)
        def _():
            cs_ref[...] = jnp.zeros_like(cs_ref)
            dw_ref[...] = jnp.zeros_like(dw_ref)
            db_ref[...] = jnp.zeros_like(db_ref)

        df = lax.dot_general(dz_ref[...].astype(_MXU), wd_ref[...].astype(_MXU), (((1,), (1,)), ((), ())),
                             preferred_element_type=F32)
        h = h_ref[...].astype(F32)
        gte, val = hc_ref[:, :tn].astype(F32), hc_ref[:, tn:].astype(F32)
        sig = _sigmoid(gte)
        dval = df * (gte * sig)
        dg = df * val * (sig * (1.0 + gte * (1.0 - sig)))
        dhc = jnp.concatenate([dg, dval], axis=1)
        nxt = jnp.where((ii + 1) % spt == 0, 0.0, carry_ref[...])
        d1 = _shift_up(dhc, nxt, 1)
        d2 = _shift_up(dhc, nxt, 2)
        carry_ref[...] = dhc[0:SUBLANES]
        db_ref[...] += _fold8(dhc)
        dw_ref[2] += _fold8(dhc * h)
        dw_ref[1] += _fold8(d1 * h)
        dw_ref[0] += _fold8(d2 * h)
        dh = wc_ref[pl.ds(2, 1), :] * dhc + wc_ref[pl.ds(1, 1), :] * d1 + wc_ref[pl.ds(0, 1), :] * d2
        cs_ref[...] += _fold8(dh)
        dh_ref[...] = dh.astype(_MXU)

    pair = lambda j, i: (0, j)
    rev = lambda j, i: (nt - 1 - i, j)
    return pl.pallas_call(
        body, name=name, grid=(2, nt),
        in_specs=[pl.BlockSpec((tm, D), lambda j, i: (nt - 1 - i, 0)),
                  pl.BlockSpec((None, tn, D), lambda j, i: (wl, j, 0)),
                  pl.BlockSpec((tm, 2 * tn), rev), pl.BlockSpec((tm, 2 * tn), rev),
                  pl.BlockSpec((SUBLANES, 2 * tn), pair)],
        out_specs=[pl.BlockSpec((tm, 2 * tn), rev), pl.BlockSpec((SUBLANES, 2 * tn), pair),
                   pl.BlockSpec((3, SUBLANES, 2 * tn), lambda j, i: (0, 0, j)),
                   pl.BlockSpec((SUBLANES, 2 * tn), pair)],
        out_shape=[jax.ShapeDtypeStruct((T, N), _MXU), jax.ShapeDtypeStruct((SUBLANES, N), F32),
                   jax.ShapeDtypeStruct((3, SUBLANES, N), F32), jax.ShapeDtypeStruct((SUBLANES, N), F32)],
        scratch_shapes=[pltpu.VMEM((SUBLANES, 2 * tn), F32)],
        compiler_params=_cp(("parallel", "arbitrary")),
    )(dzb, w_down, hs, hcs, w_dw)


def _sum_pieces(g, r, me, layer, acc, n_layers, *, name):
    _, pr, pc = g.shape
    tr = _tile(pr, 128)

    def body(me_ref, g_ref, r_ref, *rest):
        o_ref = rest[-1]
        total = g_ref[...].astype(F32)
        for s in range(N_DEV - 1):
            total = total + r_ref[s].astype(F32)
        o_ref[...] = total

    in_specs = [pl.BlockSpec((None, tr, pc), lambda i, me_ref: (me_ref[0], i, 0)),
                pl.BlockSpec((N_DEV - 1, tr, pc), lambda i, me_ref: (0, i, 0))]
    operands = [me, g, r]
    aliases = {}
    if acc is not None:
        in_specs.append(ANY)
        operands.append(acc)
        aliases = {3: 0}
    return pl.pallas_call(
        body, name=name,
        grid_spec=pltpu.PrefetchScalarGridSpec(
            num_scalar_prefetch=1, grid=(pr // tr,), in_specs=in_specs,
            out_specs=pl.BlockSpec((None, tr, pc), lambda i, me_ref: (layer, i, 0))),
        out_shape=jax.ShapeDtypeStruct((n_layers, pr, pc), F32),
        input_output_aliases=aliases,
        compiler_params=_cp(("parallel",)),
    )(*operands)


def _adam_math(w, g, m, v):
    bc1 = 1.0 - ADAM_B1 ** ADAM_STEP
    bc2 = 1.0 - ADAM_B2 ** ADAM_STEP
    m = ADAM_B1 * m + (1.0 - ADAM_B1) * g
    v = ADAM_B2 * v + (1.0 - ADAM_B2) * (g * g)
    return -ADAM_LR * ((m / bc1) / (jnp.sqrt(v / bc2) + ADAM_EPS) + ADAM_WD * w), m, v


def _adam(w, g, m, v, *, name):
    R, C = w.shape
    tr = _tile(R, 256)

    def body(w_ref, g_ref, m_ref, v_ref, d_ref, mo_ref, vo_ref):
        d_ref[...], mo_ref[...], vo_ref[...] = _adam_math(w_ref[...], g_ref[...], m_ref[...], v_ref[...])

    spec = pl.BlockSpec((tr, C), lambda i: (i, 0))
    return pl.pallas_call(
        body, name=name, grid=(R // tr,), in_specs=[spec] * 4, out_specs=[spec] * 3,
        out_shape=[jax.ShapeDtypeStruct((R, C), F32)] * 3,
        compiler_params=_cp(("parallel",)),
    )(w, g, m, v)


def _adam_halves(w, own, got, m, v, core, *, name):
    L, R, C = w.shape
    rh = R // 2
    tr = _tile(rh, 256)
    nt = rh // tr

    def body(c_ref, w_ref, own_ref, got_ref, m_ref, v_ref, g_ref, d_ref, mo_ref, vo_ref):
        g = jnp.where(pl.program_id(1) == c_ref[0], own_ref[...], got_ref[...])
        g_ref[...] = g
        d_ref[...], mo_ref[...], vo_ref[...] = _adam_math(w_ref[...], g, m_ref[...], v_ref[...])

    full = pl.BlockSpec((None, tr, C), lambda l, h, t, c_ref: (l, h * nt + t, 0))
    half = pl.BlockSpec((None, tr, C), lambda l, h, t, c_ref: (l, t, 0))
    return pl.pallas_call(
        body, name=name,
        grid_spec=pltpu.PrefetchScalarGridSpec(
            num_scalar_prefetch=1, grid=(L, 2, nt), in_specs=[full, half, half, full, full], out_specs=[full] * 4),
        out_shape=[jax.ShapeDtypeStruct((L, R, C), F32)] * 4,
        compiler_params=_cp(("parallel", "parallel", "parallel")),
    )(core, w, own, got, m, v)


def _remote(src, dst, send, recv, dev):
    return pltpu.make_async_remote_copy(src_ref=src, dst_ref=dst, send_sem=send, recv_sem=recv,
                                        device_id=dev, device_id_type=MESH)


def _place_w(shard, pos, layer, *, axis, name):
    _, R, C = shard.shape
    tr = _tile(R, 512, 16)
    nt = R // tr
    if axis == 2:
        out_shape = (1, R, N_CHIPS * C)
        out_map = lambda t, q: (0, t, q[0])
    else:
        out_shape = (1, N_CHIPS * R, C)
        out_map = lambda t, q: (0, q[0] * nt + t, 0)

    def body(q_ref, s_ref, o_ref):
        o_ref[...] = s_ref[...].astype(_WIRE)

    return pl.pallas_call(
        body, name=name,
        grid_spec=pltpu.PrefetchScalarGridSpec(
            num_scalar_prefetch=1, grid=(nt,),
            in_specs=[pl.BlockSpec((None, tr, C), lambda t, q: (layer, t, 0))],
            out_specs=pl.BlockSpec((None, tr, C), out_map)),
        out_shape=jax.ShapeDtypeStruct(out_shape, _WIRE),
        compiler_params=_cp(("parallel",)),
    )(pos, shard)


def _ag_window(ref, kind, px, py, h):
    axis, perm = kind
    q = 2 * px + py
    if perm:
        q = _perm_idx(q)
    if axis == 2:
        R, C = ref.shape[1], ref.shape[2] // N_CHIPS
        rh = R // 2
        return ref.at[:, pl.ds(pl.multiple_of(h * rh, 16), rh), pl.ds(pl.multiple_of(q * C, LANES), C)]
    R = ref.shape[1] // N_CHIPS
    rh = R // 2
    return ref.at[:, pl.ds(pl.multiple_of(q * R + h * rh, 16), rh), :]


def _ag_ici_copies(refs, kinds, send, recv):
    x, y, c = lax.axis_index("x"), lax.axis_index("y"), lax.axis_index("c")
    chips = [(1 - x, y), (x, 1 - y), (1 - x, 1 - y)]
    sends, recvs = [], []
    for a, (ref, kind) in enumerate(zip(refs, kinds)):
        own = _ag_window(ref, kind, x, y, c)
        for i, (px, py) in enumerate(chips):
            k = 3 * a + i
            sends.append(_remote(own, own, send.at[k], recv.at[k], (px, py, c)))
            recvs.append(_remote(own, _ag_window(ref, kind, px, py, c), send.at[k], recv.at[k], (px, py, c)))
    return sends, recvs


def _ag_start(arrs, kinds, after, *, name):
    n = len(arrs)

    def body(*refs):
        in_refs = refs[:n]
        send, recv = refs[n + len(after)], refs[n + len(after) + 1]
        token = refs[-1]
        sends, _ = _ag_ici_copies(in_refs, kinds, send, recv)
        for cp in sends:
            cp.start()
        token[...] = jnp.zeros_like(token)

    sems = pltpu.SemaphoreType.DMA((3 * n,))
    out = pl.pallas_call(
        body, name=name,
        out_shape=(sems, sems) + tuple(pltpu.HBM(a.shape, a.dtype) for a in arrs)
        + (jax.ShapeDtypeStruct((SUBLANES, LANES), F32),),
        in_specs=(HBM,) * n + (ANY,) * len(after),
        out_specs=(SEMS, SEMS) + (HBM,) * n + (pl.BlockSpec(memory_space=pltpu.VMEM),),
        input_output_aliases={a: 2 + a for a in range(n)},
        compiler_params=pltpu.CompilerParams(has_side_effects=EFFECT),
    )(*[pltpu.with_memory_space_constraint(a, pltpu.HBM) for a in arrs], *after)
    return out[0], out[1], list(out[2:2 + n]), out[-1]


def _ag_wait(send, recv, arrs, kinds, after, *, name):
    n = len(arrs)

    def body(*refs):
        in_refs = refs[:n]
        send, recv = refs[n], refs[n + 1]
        sends, recvs = _ag_ici_copies(in_refs, kinds, send, recv)
        for cp in sends:
            cp.wait_send()
        for cp in recvs:
            cp.wait_recv()

    out = pl.pallas_call(
        body, name=name,
        out_shape=tuple(pltpu.HBM(a.shape, a.dtype) for a in arrs),
        in_specs=(HBM,) * n + (SEMS, SEMS) + (ANY,) * len(after), out_specs=(HBM,) * n,
        input_output_aliases={a: a for a in range(n)},
        compiler_params=pltpu.CompilerParams(has_side_effects=EFFECT),
    )(*arrs, send, recv, *after)
    return list(out)


def _ag_forward(arrs, kinds, *, name):
    n = len(arrs)

    def body(*refs):
        o_refs, send, recv = refs[n:2 * n], refs[2 * n], refs[2 * n + 1]
        x, y, c = lax.axis_index("x"), lax.axis_index("y"), lax.axis_index("c")
        chips = [(1 - x, y), (x, 1 - y), (1 - x, 1 - y)]
        sib = (x, y, 1 - c)
        sends, recvs = [], []
        for a, (ref, kind) in enumerate(zip(o_refs, kinds)):
            for i, (px, py) in enumerate(chips):
                k = 3 * a + i
                got = _ag_window(ref, kind, px, py, c)
                cp = _remote(got, got, send.at[k], recv.at[k], sib)
                cp.start()
                sends.append(cp)
                recvs.append(_remote(got, _ag_window(ref, kind, px, py, 1 - c), send.at[k], recv.at[k], sib))
        for cp in recvs:
            cp.wait_recv()
        for cp in sends:
            cp.wait_send()

    out = pl.pallas_call(
        body, name=name, in_specs=[ANY] * n, out_specs=[ANY] * n,
        out_shape=[jax.ShapeDtypeStruct(a.shape, a.dtype) for a in arrs],
        input_output_aliases={a: a for a in range(n)},
        scratch_shapes=[pltpu.SemaphoreType.DMA((3 * n,)), pltpu.SemaphoreType.DMA((3 * n,))],
    )(*arrs)
    return list(out)


def _flip(x, y, c, f):
    return ((1 - x) if f & 4 else x, (1 - y) if f & 2 else y, (1 - c) if f & 1 else c)


def _rs_copies(g_ref, land_ref, send, recv):
    x, y, c = lax.axis_index("x"), lax.axis_index("y"), lax.axis_index("c")
    cps = []
    for f in range(1, N_DEV):
        tx, ty, tcx = _flip(x, y, c, f)
        cps.append(_remote(g_ref.at[4 * tx + 2 * ty + tcx], land_ref.at[f - 1], send.at[f - 1], recv.at[f - 1],
                           (tx, ty, tcx)))
    return cps


def _rs_start(g, *, name):
    _, pr, pc = g.shape
    land_shape = (N_DEV - 1, pr, pc)

    def body(g_ref, land_ref, send, recv, g_thru, land_thru, token):
        for cp in _rs_copies(g_ref, land_ref, send, recv):
            cp.start()
        token[...] = jnp.zeros_like(token)

    sems = pltpu.SemaphoreType.DMA((N_DEV - 1,))
    return pl.pallas_call(
        body, name=name,
        out_shape=(sems, sems, pltpu.HBM(g.shape, g.dtype), pltpu.HBM(land_shape, g.dtype),
                   jax.ShapeDtypeStruct((SUBLANES, LANES), F32)),
        in_specs=(HBM, HBM), out_specs=(SEMS, SEMS, HBM, HBM, pl.BlockSpec(memory_space=pltpu.VMEM)),
        input_output_aliases={0: 2, 1: 3},
        compiler_params=pltpu.CompilerParams(has_side_effects=EFFECT),
    )(pltpu.with_memory_space_constraint(g, pltpu.HBM),
      pltpu.with_memory_space_constraint(lax.empty(land_shape, g.dtype), pltpu.HBM))


def _rs_wait(send, recv, g_thru, land_thru, after, *, name):
    def body(g_ref, land_ref, send, recv, after_ref, g_out, land_out):
        cps = _rs_copies(g_ref, land_ref, send, recv)
        for cp in cps:
            cp.wait_send()
        for cp in cps:
            cp.wait_recv()

    return pl.pallas_call(
        body, name=name,
        out_shape=(pltpu.HBM(g_thru.shape, g_thru.dtype), pltpu.HBM(land_thru.shape, land_thru.dtype)),
        in_specs=(HBM, HBM, SEMS, SEMS, ANY), out_specs=(HBM, HBM), input_output_aliases={0: 0, 1: 1},
        compiler_params=pltpu.CompilerParams(has_side_effects=EFFECT),
    )(g_thru, land_thru, send, recv, after)


def _pair_exchange(own, *, name):
    def body(own_ref, got_ref, send, recv):
        x, y, c = lax.axis_index("x"), lax.axis_index("y"), lax.axis_index("c")
        cp = _remote(own_ref, got_ref, send, recv, (x, y, 1 - c))
        cp.start()
        cp.wait_recv()
        cp.wait_send()

    return pl.pallas_call(
        body, name=name, in_specs=[ANY], out_specs=ANY, out_shape=jax.ShapeDtypeStruct(own.shape, own.dtype),
        scratch_shapes=[pltpu.SemaphoreType.DMA, pltpu.SemaphoreType.DMA],
    )(own)


def _allreduce_flat(vec, *, name):
    n = vec.shape[0]
    unit = N_DEV * SUBLANES * LANES
    npad = -(-n // unit) * unit
    rows = npad // (N_DEV * LANES)
    xin = jnp.pad(vec, (0, npad - n)).reshape(N_DEV, rows, LANES)

    def body(x_ref, y_ref, a_ref, send_a, recv_a, send_b, recv_b):
        x, y, c = lax.axis_index("x"), lax.axis_index("y"), lax.axis_index("c")
        me = 4 * x + 2 * y + c
        a_ref[me] = x_ref[me]
        sends, recvs = [], []
        for f in range(1, N_DEV):
            dev = _flip(x, y, c, f)
            t = 4 * dev[0] + 2 * dev[1] + dev[2]
            cp = _remote(x_ref.at[t], a_ref.at[me], send_a.at[f - 1], recv_a.at[f - 1], dev)
            cp.start()
            sends.append(cp)
            recvs.append(_remote(x_ref.at[me], a_ref.at[t], send_a.at[f - 1], recv_a.at[f - 1], dev))
        for cp in recvs:
            cp.wait_recv()
        for cp in sends:
            cp.wait_send()
        acc = a_ref[0]
        for s in range(1, N_DEV):
            acc = acc + a_ref[s]
        y_ref[me] = acc
        sends, recvs = [], []
        for f in range(1, N_DEV):
            dev = _flip(x, y, c, f)
            t = 4 * dev[0] + 2 * dev[1] + dev[2]
            cp = _remote(y_ref.at[me], y_ref.at[me], send_b.at[f - 1], recv_b.at[f - 1], dev)
            cp.start()
            sends.append(cp)
            recvs.append(_remote(y_ref.at[me], y_ref.at[t], send_b.at[f - 1], recv_b.at[f - 1], dev))
        for cp in recvs:
            cp.wait_recv()
        for cp in sends:
            cp.wait_send()

    vm = pl.BlockSpec(memory_space=pltpu.VMEM)
    out = pl.pallas_call(
        body, name=name, in_specs=[vm], out_specs=vm,
        out_shape=jax.ShapeDtypeStruct((N_DEV, rows, LANES), F32),
        scratch_shapes=[pltpu.VMEM((N_DEV, rows, LANES), F32)] + [pltpu.SemaphoreType.DMA((N_DEV - 1,))] * 4,
        compiler_params=_cp(),
    )(xin)
    return out.reshape(npad)[:n]


def _perm_cols(v, blocks=N_CHIPS):
    lead, n = v.shape[:-1], v.shape[-1]
    return v.reshape(lead + (blocks, n // blocks))[..., PERM, :].reshape(lead + (n,))


def _pack(arrs):
    return jnp.concatenate([a.reshape(-1).astype(F32) for a in arrs])


def _unpack(flat, shapes):
    out, pos = [], 0
    for s in shapes:
        n = 1
        for d in s:
            n *= d
        out.append(flat[pos:pos + n].reshape(s))
        pos += n
    return out


def kernel(x, conv_w_in, conv_b_in, conv_w_dw, conv_b_dw, conv_ln_g, conv_ln_b, conv_w_out, conv_b_out, gmlp_w_in, gmlp_b_in, gmlp_ln_g, gmlp_ln_b, gmlp_w_s, gmlp_b_s, gmlp_w_out, gmlp_b_out, ffn_w_up, ffn_b_up, ffn_w_dw, ffn_b_dw, ffn_w_down, ffn_b_down, norm1_g, norm1_b, norm2_g, norm2_b, loss_target, m_conv_w_in, m_conv_b_in, m_conv_w_dw, m_conv_b_dw, m_conv_ln_g, m_conv_ln_b, m_conv_w_out, m_conv_b_out, m_gmlp_w_in, m_gmlp_b_in, m_gmlp_ln_g, m_gmlp_ln_b, m_gmlp_w_s, m_gmlp_b_s, m_gmlp_w_out, m_gmlp_b_out, m_ffn_w_up, m_ffn_b_up, m_ffn_w_dw, m_ffn_b_dw, m_ffn_w_down, m_ffn_b_down, m_norm1_g, m_norm1_b, m_norm2_g, m_norm2_b, v_conv_w_in, v_conv_b_in, v_conv_w_dw, v_conv_b_dw, v_conv_ln_g, v_conv_ln_b, v_conv_w_out, v_conv_b_out, v_gmlp_w_in, v_gmlp_b_in, v_gmlp_ln_g, v_gmlp_ln_b, v_gmlp_w_s, v_gmlp_b_s, v_gmlp_w_out, v_gmlp_b_out, v_ffn_w_up, v_ffn_b_up, v_ffn_w_dw, v_ffn_b_dw, v_ffn_w_down, v_ffn_b_down, v_norm1_g, v_norm1_b, v_norm2_g, v_norm2_b):
    P = dict(locals())
    WEIGHTS = ['conv_w_in', 'conv_b_in', 'conv_w_dw', 'conv_b_dw', 'conv_ln_g', 'conv_ln_b', 'conv_w_out',
               'conv_b_out', 'gmlp_w_in', 'gmlp_b_in', 'gmlp_ln_g', 'gmlp_ln_b', 'gmlp_w_s', 'gmlp_b_s',
               'gmlp_w_out', 'gmlp_b_out', 'ffn_w_up', 'ffn_b_up', 'ffn_w_dw', 'ffn_b_dw', 'ffn_w_down',
               'ffn_b_down', 'norm1_g', 'norm1_b', 'norm2_g', 'norm2_b']
    BIG = ['conv_w_in', 'conv_w_out', 'gmlp_w_in', 'gmlp_w_out', 'ffn_w_up', 'ffn_w_down']
    SMALL_SHARDED = {'conv_w_dw': 2, 'gmlp_b_in': 1, 'gmlp_ln_g': 1, 'gmlp_ln_b': 1, 'gmlp_b_out': 1, 'ffn_w_dw': 2}

    B, S, D = x.shape
    T = B * S
    depth = norm1_g.shape[0]
    alpha = (2.0 * depth) ** 0.25
    C = conv_w_out.shape[-1]
    F2 = ffn_b_up.shape[-1]
    G, L = gmlp_w_s.shape[1], gmlp_w_s.shape[2]
    xi, yi, ci = lax.axis_index("x"), lax.axis_index("y"), lax.axis_index("c")
    shard = 2 * xi + yi

    i32 = lambda v: jnp.reshape(v, (1,)).astype(jnp.int32)
    pos_plain, pos_perm = i32(shard), i32(_perm_idx(shard))
    me_id, core_id = i32(4 * xi + 2 * yi + ci), i32(ci)

    groups = []
    for i in range(depth):
        mix = 'conv' if i % 2 == 0 else 'gmlp'
        groups.append((f"{mix}{i // 2}", [(mix + '_w_in', i // 2, 2, True), (mix + '_w_out', i // 2, 1, False)]))
        groups.append((f"ffn{i}", [('ffn_w_up', i, 2, True), ('ffn_w_down', i, 1, False)]))
    sm_names = list(SMALL_SHARDED)
    sm_shapes = [P[n].shape for n in sm_names]
    mine = _pack([P[n] for n in sm_names]) * (ci == 0).astype(F32)
    buf = jnp.zeros((N_CHIPS, mine.shape[0]), F32)
    buf = lax.dynamic_update_slice(buf, mine[None], (shard, 0))
    gathered = _allreduce_flat(buf.reshape(-1), name="ag_small").reshape(N_CHIPS, -1)

    started, order = {}, [gathered]
    for gname, members in groups:
        placed = [_place_w(P[n], pos_perm if perm else pos_plain, l, axis=axis, name=f"place_{n}_{l}")
                  for n, l, axis, perm in members]
        kinds = [(axis, perm) for _, _, axis, perm in members]
        send, recv, arrs, token = _ag_start(placed, kinds, order, name=f"ag_start_{gname}")
        order = [token]
        started[gname] = (send, recv, arrs, kinds, [(n, l) for n, l, _, _ in members])
    wts = {}

    def arrive(gname, after):
        send, recv, arrs, kinds, keys = started[gname]
        arrs = _ag_wait(send, recv, arrs, kinds, after, name=f"ag_wait_{gname}")
        arrs = _ag_forward(arrs, kinds, name=f"ag_fwd_{gname}")
        wts.update(zip(keys, arrs))

    full = {}
    for n, parts in zip(sm_names, zip(*[_unpack(gathered[k], sm_shapes) for k in range(N_CHIPS)])):
        full[n] = jnp.concatenate(parts, axis=SMALL_SHARDED[n])
    for n in WEIGHTS:
        if n not in BIG and n not in full:
            full[n] = P[n]

    assert G * L == C, "a gMLP group must be as wide as a chunk is long"

    def row(v):
        return v.reshape(1, -1)

    def pad_rows(v, r):
        return jnp.pad(v, ((0, r - v.shape[0]), (0, 0)))

    xf = x.reshape(T, D)
    saved = []
    cur, cur_b = xf, xf.astype(_MXU)
    for i in range(depth):
        j = i // 2
        sv = {'x': cur, 'xb': cur_b}
        arrive(groups[2 * i][0], order if i == 0 else [cur_b])
        if i % 2 == 0:
            b_in = row(_perm_cols(full['conv_b_in'][j]))
            h1 = _mm(cur_b, wts['conv_w_in', j], bl=0, bias=b_in, tm=_tile(T, 512), tn=_tile(2 * C, 1024, LANES),
                     tk=D, name=f"conv_in_{j}", n_outer=True)
            wdw = pad_rows(full['conv_w_dw'][j], CONV_TAPS_PAD)
            dwo = _conv_fwd(h1, wdw, row(full['conv_b_dw'][j]), B=B, S=S, name=f"conv_dw_{j}")
            s_act, xhc, rsc = _ln_silu_fwd(dwo, row(full['conv_ln_g'][j]), row(full['conv_ln_b'][j]),
                                           name=f"conv_ln_{j}")
            sv.update(h1=h1, wdw=wdw, act=s_act, xhc=xhc, rsc=rsc)
            y1 = _mm_res_ln(s_act, wts['conv_w_out', j], 0, row(full['conv_b_out'][j]), cur, alpha, row(norm1_g[i]),
                            row(norm1_b[i]), name=f"conv_out_ln_{j}")
        else:
            b_in = row(_perm_cols(full['gmlp_b_in'][j]))
            pre = _mm(cur_b, wts['gmlp_w_in', j], bl=0, bias=b_in, tm=_tile(T, 512), tn=_tile(2 * C, 1024, LANES),
                      tk=D, name=f"gmlp_in_{j}", n_outer=True)
            bsb = jnp.repeat(gmlp_b_s[j].T, L, axis=1)
            us, xhv, rsv = _gmlp_gate_fwd(pre, row(full['gmlp_ln_g'][j]), row(full['gmlp_ln_b'][j]), gmlp_w_s[j],
                                          bsb, name=f"gmlp_gate_{j}")
            sv.update(pre=pre, bsb=bsb, act=us, xhv=xhv, rsv=rsv)
            y1 = _mm_res_ln(us, wts['gmlp_w_out', j], 0, row(full['gmlp_b_out'][j]), cur, alpha, row(norm1_g[i]),
                            row(norm1_b[i]), name=f"gmlp_out_ln_{j}")
        x1, x1b, xh1, rs1 = y1
        arrive(groups[2 * i + 1][0], [x1b])
        wdw3 = pad_rows(_perm_cols(full['ffn_w_dw'][i]), SUBLANES)
        bdw3 = row(_perm_cols(ffn_b_dw[i]))
        hs, hcs, f_act = _ffn_up_fwd(x1b, wts['ffn_w_up', i], 0, row(_perm_cols(ffn_b_up[i])), wdw3, bdw3, S=S,
                                     name=f"ffn_up_{i}")
        x2, x2b, xh2, rs2 = _mm_res_ln(f_act, wts['ffn_w_down', i], 0, row(ffn_b_down[i]), x1, alpha, row(norm2_g[i]),
                                       row(norm2_b[i]), name=f"ffn_down_ln_{i}")
        sv.update(x1=x1, x1b=x1b, xh1=xh1, rs1=rs1, hs=hs, hcs=hcs, f=f_act, wdw3=wdw3, xh2=xh2, rs2=rs2)
        saved.append(sv)
        cur, cur_b = x2, x2b

    sg = {n: [None] * full[n].shape[0] for n in WEIGHTS if n not in BIG}
    inflight = {n: [None] * P[n].shape[0] for n in BIG}
    deps = []
    tgt = loss_target.reshape(T, D)
    dcur = None
    loss_part = None
    tk_t = _tile(T, 2048)

    def wgrad(n, l, a_, b_, **kw):
        g = _mm(a_, b_, ta=True, out_dtype=_WIRE, tk=tk_t, name=f"{n}_dw_{l}", deps=deps, **kw)
        send, recv, g_thru, land, token = _rs_start(g, name=f"rs_start_{n}_{l}")
        inflight[n][l] = (send, recv, g_thru, land)
        deps.append(token)

    for i in reversed(range(depth)):
        j = i // 2
        sv = saved[i]
        if i == depth - 1:
            dz2, dz2b, dg, db, cs, loss_part = _ln_bwd(cur, sv['xh2'], sv['rs2'], row(norm2_g[i]), target=tgt,
                                                       name=f"ln2_bwd_head_{i}")
        else:
            dz2, dz2b, dg, db, cs = dcur
        sg['norm2_g'][i], sg['norm2_b'][i], sg['ffn_b_down'][i] = dg.sum(0), db.sum(0), cs.sum(0)
        Fh = F2 // 2
        wgrad('ffn_w_down', i, sv['f'], dz2b, tm=Fh // 2, tn=_tile(D, 1024, LANES), pieces=('row',))
        dh, csu, dwd, dbd = _ffn_bwd(dz2b, wts['ffn_w_down', i], 0, sv['hs'], sv['hcs'], sv['wdw3'], S=S,
                                     name=f"ffn_bwd_{i}")
        sg['ffn_b_up'][i] = _perm_cols(csu.sum(0))
        sg['ffn_w_dw'][i] = _perm_cols(dwd.sum(1))
        sg['ffn_b_dw'][i] = _perm_cols(dbd.sum(0))
        wgrad('ffn_w_up', i, sv['x1b'], dh, tm=D, tn=F2 // N_CHIPS, pieces=('col', True))
        dz1, dz1b, dg, db, cs = _mm_ln_bwd(dh, wts['ffn_w_up', i], dz2, alpha, sv['xh1'], sv['rs1'], row(norm1_g[i]),
                                           name=f"ffn_dx_{i}", deps=deps)
        sg['norm1_g'][i], sg['norm1_b'][i] = dg.sum(0), db.sum(0)
        if i % 2 == 0:
            sg['conv_b_out'][j] = cs.sum(0)
            wgrad('conv_w_out', j, sv['act'], dz1b, tm=_tile(C, 1024), tn=_tile(D, 1024, LANES), pieces=('row',))
            ds = _mm(dz1b, wts['conv_w_out', j], bl=0, tb=True, tm=_tile(T, 512), tn=_tile(C, 1024, LANES), tk=_tile(D, 1024, LANES),
                     name=f"conv_ds_{j}", deps=deps)
            ddw, dg, db = _ln_silu_bwd(ds, sv['xhc'], sv['rsc'], row(full['conv_ln_g'][j]),
                                       row(full['conv_ln_b'][j]), name=f"conv_ln_bwd_{j}")
            sg['conv_ln_g'][j], sg['conv_ln_b'][j] = dg.sum(0), db.sum(0)
            dglu, dwk, dbk = _conv_bwd(ddw, sv['h1'], sv['wdw'], B=B, S=S, name=f"conv_dw_bwd_{j}")
            sg['conv_w_dw'][j] = dwk.sum(1)[:conv_w_dw.shape[1]]
            sg['conv_b_dw'][j] = dbk.sum(0)
            dh1, csi = _glu_bwd(dglu, sv['h1'], name=f"conv_glu_bwd_{j}")
            sg['conv_b_in'][j] = _perm_cols(csi.sum(0))
            fam = 'conv_w_in'
        else:
            sg['gmlp_b_out'][j] = cs.sum(0)
            wgrad('gmlp_w_out', j, sv['act'], dz1b, tm=_tile(C, 1024), tn=_tile(D, 1024, LANES), pieces=('row',))
            dus = _mm(dz1b, wts['gmlp_w_out', j], bl=0, tb=True, tm=_tile(T, 512), tn=_tile(C, 1024, LANES),
                      tk=_tile(D, 1024, LANES), name=f"gmlp_dus_{j}", deps=deps)
            dh1, dg, db, csi, dws, dbs = _gmlp_gate_bwd(dus, sv['pre'], sv['xhv'], sv['rsv'], row(full['gmlp_ln_g'][j]),
                                                        row(full['gmlp_ln_b'][j]), gmlp_w_s[j], sv['bsb'],
                                                        name=f"gmlp_gate_bwd_{j}")
            sg['gmlp_ln_g'][j], sg['gmlp_ln_b'][j] = dg.sum(0), db.sum(0)
            sg['gmlp_b_in'][j] = _perm_cols(csi.sum(0))
            sg['gmlp_w_s'][j] = dws
            sg['gmlp_b_s'][j] = dbs.reshape(L, G, L).sum(-1).T
            fam = 'gmlp_w_in'
        wgrad(fam, j, sv['xb'], dh1, tm=D, tn=(2 * C) // N_CHIPS, pieces=('col', True))
        if i > 0:
            below = saved[i - 1]
            dcur = _mm_ln_bwd(dh1, wts[fam, j], dz1, alpha, below['xh2'], below['rs2'], row(norm2_g[i - 1]),
                              name=f"{fam}_dx_{j}", deps=deps)
        else:
            dcur = _mm(dh1, wts[fam, j], bl=0, tb=True, res=dz1, res_scale=alpha, tm=_tile(T, 512),
                       tn=_tile(D, 1024, LANES), tk=2 * C, name=f"{fam}_dx_{j}", deps=deps)
    grad_x = dcur.reshape(B, S, D)

    small_names = [n for n in WEIGHTS if n not in BIG]
    small_full = [jnp.stack(sg[n]) for n in small_names]
    flat = _pack(small_full + [loss_part])
    red = _allreduce_flat(flat, name="ar_small")
    red_parts = _unpack(red, [a.shape for a in small_full] + [loss_part.shape])
    loss = (0.5 / D) * jnp.sum(red_parts[-1])
    grads = {}
    for n, g in zip(small_names, red_parts[:-1]):
        if n in SMALL_SHARDED:
            ax = SMALL_SHARDED[n]
            width = P[n].shape[ax]
            g = lax.dynamic_slice_in_dim(g, shard * width, width, axis=ax)
        grads[n] = g

    big_out = {}
    for n in ['ffn_w_down', 'ffn_w_up', 'gmlp_w_out', 'gmlp_w_in', 'conv_w_out', 'conv_w_in']:
        own = None
        n_layers = len(inflight[n])
        for l in reversed(range(n_layers)):
            send, recv, g_thru, land = inflight[n][l]
            pc_, r = _rs_wait(send, recv, g_thru, land, dcur, name=f"rs_wait_{n}_{l}")
            own = _sum_pieces(pc_, r, me_id, l, own, n_layers, name=f"sum_{n}_{l}")
        got = _pair_exchange(own, name=f"px_{n}")
        big_out[n] = _adam_halves(P[n], own, got, P['m_' + n], P['v_' + n], core_id, name=f"adam_{n}")

    shapes = [P[n].shape for n in small_names]
    n_small = sum(functools.reduce(lambda p_, d_: p_ * d_, s_, 1) for s_ in shapes)
    unit = SUBLANES * LANES
    npad = -(-n_small // unit) * unit

    def flat2d(arrs, fill=0.0):
        v = _pack(arrs)
        return jnp.pad(v, (0, npad - n_small), constant_values=fill).reshape(-1, LANES)

    dl, mo, vo = _adam(flat2d([P[n] for n in small_names]), flat2d([grads[n] for n in small_names]),
                       flat2d([P['m_' + n] for n in small_names]),
                       flat2d([P['v_' + n] for n in small_names], fill=1.0), name="adam_small")
    small_out = {n: [grads[n], None, None, None] for n in small_names}
    for k, t in enumerate((dl, mo, vo)):
        for n, a in zip(small_names, _unpack(t.reshape(-1), shapes)):
            small_out[n][k + 1] = a

    outs = [loss, grad_x]
    for k in range(4):
        for n in WEIGHTS:
            outs.append(big_out[n][k] if n in BIG else small_out[n][k])
    return tuple(outs)
```

```python
import functools

import jax
import jax.numpy as jnp
from jax import lax
from jax.experimental import pallas as pl
from jax.experimental.pallas import tpu as pltpu

F32 = jnp.float32
_MXU = jnp.bfloat16
_WIRE = jnp.bfloat16
_HDT = jnp.bfloat16
LN_EPS = 1e-5
ADAM_LR, ADAM_B1, ADAM_B2, ADAM_EPS, ADAM_WD, ADAM_STEP = 0.001, 0.9, 0.999, 1e-08, 0.01, 10
N_CHIPS = 4
N_DEV = 8
LANES = 128
SUBLANES = 8
CONV_TAPS_PAD = 32
VMEM_LIMIT = 56 << 20
MESH = pl.DeviceIdType.MESH
ANY = pl.BlockSpec(memory_space=pl.ANY)
HBM = pl.BlockSpec(memory_space=pltpu.HBM)
SEMS = pl.BlockSpec(memory_space=pltpu.SEMAPHORE)
EFFECT = pltpu.SideEffectType.DATAFLOW_SIDE_EFFECTING
PERM = (0, 2, 1, 3)


def _cp(sem=None):
    return pltpu.CompilerParams(dimension_semantics=sem, vmem_limit_bytes=VMEM_LIMIT)


def _tile(dim, pref, mult=SUBLANES):
    if dim <= pref:
        return dim
    t = (pref // mult) * mult
    while t > mult and dim % t:
        t -= mult
    assert dim % t == 0, (dim, pref, mult)
    return t


def _perm_idx(q):
    return (q % 2) * 2 + q // 2


def _fold8(t):
    r, n = t.shape
    return t.reshape(r // SUBLANES, SUBLANES, n).sum(axis=0)


def _ln_rows(z, g, b):
    mu = jnp.mean(z, axis=-1, keepdims=True)
    xc = z - mu
    var = jnp.mean(xc * xc, axis=-1, keepdims=True)
    rstd = lax.rsqrt(var + LN_EPS)
    xh = xc * rstd
    return xh * g + b, xh, rstd


def _ln_bwd_rows(dy, xh, rstd, g):
    dxh = dy * g
    m1 = jnp.mean(dxh, axis=-1, keepdims=True)
    m2 = jnp.mean(dxh * xh, axis=-1, keepdims=True)
    return rstd * (dxh - m1 - xh * m2)


def _sigmoid(v):
    return 1.0 / (1.0 + jnp.exp(-v))


def _gelu_parts(p):
    cdf = 0.5 * (1.0 + lax.erf(p * 0.7071067811865476))
    pdf = jnp.exp(-0.5 * p * p) * 0.3989422804014327
    return p * cdf, cdf + p * pdf


def _shift_down(prev8, t, s):
    ext = jnp.concatenate([prev8, t], axis=0)
    return pltpu.roll(ext, s, 0)[SUBLANES:]


def _shift_up(t, next8, s):
    n = t.shape[0]
    ext = jnp.concatenate([t, next8], axis=0)
    return pltpu.roll(ext, n + SUBLANES - s, 0)[:n]


def _mm(a, b, *, ta=False, tb=False, bl=None, bias=None, res=None, res_scale=1.0, out_dtype=F32,
        tm, tn, tk, name, pieces=None, deps=None, n_outer=False):
    M, K = (a.shape[1], a.shape[0]) if ta else a.shape
    bs = b.shape[1:] if bl is not None else b.shape
    N, Kb = (bs[0], bs[1]) if tb else (bs[1], bs[0])
    assert K == Kb and M % tm == 0 and N % tn == 0 and K % tk == 0, (a.shape, b.shape, tm, tn, tk)
    gm, gn, gk = M // tm, N // tn, K // tk

    def spec(block, imap):
        if n_outer:
            return pl.BlockSpec(block, lambda j, i, k: imap(i, j, k))
        return pl.BlockSpec(block, imap)

    a_spec = spec((tk, tm), lambda i, j, k: (k, i)) if ta else spec((tm, tk), lambda i, j, k: (i, k))
    bblk = (tn, tk) if tb else (tk, tn)
    bmap = (lambda i, j, k: (j, k)) if tb else (lambda i, j, k: (k, j))
    if bl is not None:
        b_spec = spec((None,) + bblk, lambda i, j, k: (bl,) + bmap(i, j, k))
    else:
        b_spec = spec(bblk, bmap)
    in_specs, operands = [a_spec, b_spec], [a, b]
    if bias is not None:
        in_specs.append(spec((1, tn), lambda i, j, k: (0, j)))
        operands.append(bias)
    if res is not None:
        in_specs.append(spec((tm, tn), lambda i, j, k: (i, j)))
        operands.append(res)
    n_dep = len(deps) if deps else 0
    if n_dep:
        in_specs += [ANY] * n_dep
        operands += deps
        del deps[:]
    if pieces is None:
        out_shape = jax.ShapeDtypeStruct((M, N), out_dtype)
        out_spec = spec((tm, tn), lambda i, j, k: (i, j))
        ppb = pr = None
    elif pieces[0] == 'col':
        pr, pc = M // 2, N // N_CHIPS
        assert tm % pr == 0 and pc % tn == 0
        ppb, per = tm // pr, pc // tn
        perm = pieces[1]
        out_shape = jax.ShapeDtypeStruct((N_DEV, pr, pc), out_dtype)
        out_spec = spec(
            (ppb, pr, tn),
            lambda i, j, k: ((2 * (_perm_idx(j // per) if perm else j // per)) // ppb + i, 0, j % per))
    else:
        pr = M // N_DEV
        assert tm % pr == 0
        ppb = tm // pr
        out_shape = jax.ShapeDtypeStruct((N_DEV, pr, N), out_dtype)
        out_spec = spec((ppb, pr, tn), lambda i, j, k: (i, 0, j))
    dims = (((0 if ta else 1,), (1 if tb else 0,)), ((), ()))

    def body(*refs):
        a_ref, b_ref = refs[0], refs[1]
        pos = 2
        bias_ref = res_ref = None
        if bias is not None:
            bias_ref = refs[pos]
            pos += 1
        if res is not None:
            res_ref = refs[pos]
            pos += 1
        pos += n_dep
        o_ref = refs[pos]

        def finish(r):
            if bias_ref is not None:
                r = r + bias_ref[...]
            if res_ref is not None:
                r = r + res_scale * res_ref[...]
            if pieces is not None:
                r = r.reshape(ppb, pr, tn)
            o_ref[...] = r.astype(out_dtype)

        part = lax.dot_general(a_ref[...].astype(_MXU), b_ref[...].astype(_MXU), dims, preferred_element_type=F32)
        if gk == 1:
            finish(part)
            return
        acc_ref = refs[pos + 1]
        k = pl.program_id(2)

        @pl.when(k == 0)
        def _():
            acc_ref[...] = part

        @pl.when((k > 0) & (k < gk - 1))
        def _():
            acc_ref[...] += part

        @pl.when(k == gk - 1)
        def _():
            finish(acc_ref[...] + part)

    return pl.pallas_call(
        body, name=name, grid=(gn, gm, gk) if n_outer else (gm, gn, gk), in_specs=in_specs, out_specs=out_spec,
        out_shape=out_shape, scratch_shapes=[pltpu.VMEM((tm, tn), F32)] if gk > 1 else [],
        compiler_params=_cp(("parallel", "parallel", "arbitrary")),
    )(*operands)


def _mm_ln_bwd(a, w, res, res_scale, xh, rstd, g, *, name, deps=None):
    T, K = a.shape
    D = w.shape[1]
    tm = _tile(T, 512)
    n_dep = len(deps) if deps else 0

    def body(a_ref, w_ref, res_ref, xh_ref, rs_ref, g_ref, *rest):
        dz_ref, dzb_ref, dg_ref, db_ref, cs_ref = rest[n_dep:]

        @pl.when(pl.program_id(0) == 0)
        def _():
            dg_ref[...] = jnp.zeros_like(dg_ref)
            db_ref[...] = jnp.zeros_like(db_ref)
            cs_ref[...] = jnp.zeros_like(cs_ref)

        d = lax.dot_general(a_ref[...].astype(_MXU), w_ref[...].astype(_MXU), (((1,), (1,)), ((), ())),
                            preferred_element_type=F32) + res_scale * res_ref[...]
        xh = xh_ref[...]
        dz = _ln_bwd_rows(d, xh, rs_ref[...], g_ref[...])
        dz_ref[...] = dz
        dzb_ref[...] = dz.astype(_MXU)
        dg_ref[...] += _fold8(d * xh)
        db_ref[...] += _fold8(d)
        cs_ref[...] += _fold8(dz)

    row = lambda i: (i, 0)
    fixed = lambda i: (0, 0)
    tile = pl.BlockSpec((tm, D), row)
    part = pl.BlockSpec((SUBLANES, D), fixed)
    operands = [a, w, res, xh, rstd, g] + (list(deps) if deps else [])
    if deps:
        del deps[:]
    return pl.pallas_call(
        body, name=name, grid=(T // tm,),
        in_specs=[pl.BlockSpec((tm, K), row),
                  pl.BlockSpec((None, D, K), lambda i: (0, 0, 0), pipeline_mode=pl.Buffered(1)),
                  tile, tile, pl.BlockSpec((tm, 1), row), pl.BlockSpec((1, D), fixed)] + [ANY] * n_dep,
        out_specs=[tile, tile, part, part, part],
        out_shape=[jax.ShapeDtypeStruct((T, D), F32), jax.ShapeDtypeStruct((T, D), _MXU)]
        + [jax.ShapeDtypeStruct((SUBLANES, D), F32)] * 3,
        compiler_params=_cp(("arbitrary",)),
    )(*operands)


def _mm_res_ln(a, w, wl, bias, res, alpha, g, b, *, name):
    T, K = a.shape
    D = w.shape[-1]
    tm = _tile(T, 256)

    def body(a_ref, w_ref, bias_ref, res_ref, g_ref, b_ref, y_ref, yb_ref, xh_ref, rs_ref):
        z = jnp.dot(a_ref[...].astype(_MXU), w_ref[...].astype(_MXU), preferred_element_type=F32)
        z = z + bias_ref[...] + alpha * res_ref[...]
        y, xh, rstd = _ln_rows(z, g_ref[...], b_ref[...])
        y_ref[...] = y
        yb_ref[...] = y.astype(_MXU)
        xh_ref[...] = xh
        rs_ref[...] = rstd

    row = lambda i: (i, 0)
    vec = pl.BlockSpec((1, D), lambda i: (0, 0))
    return pl.pallas_call(
        body, name=name, grid=(T // tm,),
        in_specs=[pl.BlockSpec((tm, K), row), pl.BlockSpec((None, K, D), lambda i: (wl, 0, 0)), vec,
                  pl.BlockSpec((tm, D), row), vec, vec],
        out_specs=[pl.BlockSpec((tm, D), row), pl.BlockSpec((tm, D), row), pl.BlockSpec((tm, D), row),
                   pl.BlockSpec((tm, 1), row)],
        out_shape=[jax.ShapeDtypeStruct((T, D), F32), jax.ShapeDtypeStruct((T, D), _MXU),
                   jax.ShapeDtypeStruct((T, D), F32), jax.ShapeDtypeStruct((T, 1), F32)],
        compiler_params=_cp(("parallel",)),
    )(a, w, bias, res, g, b)


def _ln_bwd(dy, xh, rstd, g, *, name, target=None):
    T, D = dy.shape
    tm = _tile(T, 256)
    head = target is not None

    def body(*refs):
        if head:
            dy_ref, t_ref, xh_ref, rs_ref, g_ref, dz_ref, dzb_ref, dg_ref, db_ref, cs_ref, ls_ref = refs
        else:
            dy_ref, xh_ref, rs_ref, g_ref, dz_ref, dzb_ref, dg_ref, db_ref, cs_ref = refs
        i = pl.program_id(0)

        @pl.when(i == 0)
        def _():
            dg_ref[...] = jnp.zeros_like(dg_ref)
            db_ref[...] = jnp.zeros_like(db_ref)
            cs_ref[...] = jnp.zeros_like(cs_ref)
            if head:
                ls_ref[...] = jnp.zeros_like(ls_ref)

        d = dy_ref[...]
        if head:
            err = d - t_ref[...]
            ls_ref[...] += _fold8(err * err)
            d = err * (1.0 / D)
        xh = xh_ref[...]
        dz = _ln_bwd_rows(d, xh, rs_ref[...], g_ref[...])
        dz_ref[...] = dz
        dzb_ref[...] = dz.astype(_MXU)
        dg_ref[...] += _fold8(d * xh)
        db_ref[...] += _fold8(d)
        cs_ref[...] += _fold8(dz)

    row = lambda i: (i, 0)
    fixed = lambda i: (0, 0)
    tile = pl.BlockSpec((tm, D), row)
    part = pl.BlockSpec((SUBLANES, D), fixed)
    in_specs = [tile] + ([tile] if head else []) + [tile, pl.BlockSpec((tm, 1), row), pl.BlockSpec((1, D), fixed)]
    n_part = 4 if head else 3
    operands = [dy] + ([target] if head else []) + [xh, rstd, g]
    return pl.pallas_call(
        body, name=name, grid=(T // tm,), in_specs=in_specs,
        out_specs=[tile, tile] + [part] * n_part,
        out_shape=[jax.ShapeDtypeStruct((T, D), F32), jax.ShapeDtypeStruct((T, D), _MXU)]
        + [jax.ShapeDtypeStruct((SUBLANES, D), F32)] * n_part,
        compiler_params=_cp(("arbitrary",)),
    )(*operands)


def _conv_cols(C, tc):
    per = (C // 2) // tc
    return per, (lambda j: (j // per) * (2 * per) + j % per)


def _glu_shifted(a_ref, g_ref, p_ref, S):
    u = a_ref[...] * _sigmoid(g_ref[...])
    rows = lax.broadcasted_iota(jnp.int32, u.shape, 0)
    for r in range(SUBLANES):
        p_ref[r, 0:CONV_TAPS_PAD, :] = jnp.zeros((CONV_TAPS_PAD, u.shape[1]), F32)
        p_ref[r, CONV_TAPS_PAD:CONV_TAPS_PAD + S, :] = u if r == 0 else jnp.where(rows >= r, pltpu.roll(u, r, 0), 0.0)


def _conv_fwd(h1, w_dw, b_dw, *, B, S, name):
    C = w_dw.shape[1]
    taps = CONV_TAPS_PAD - 1
    tc = LANES
    ch = _tile(S, 128)
    per, col_a = _conv_cols(C, tc)

    def body(a_ref, g_ref, w_ref, b_ref, o_ref, p_ref):
        _glu_shifted(a_ref, g_ref, p_ref, S)

        def chunk(ci, carry):
            base = pl.multiple_of(ci * ch, ch)
            acc = jnp.zeros((ch, tc), F32) + b_ref[...]
            for k in range(taps):
                q, r = divmod(taps - 1 - k, SUBLANES)
                start = pl.multiple_of(base + (CONV_TAPS_PAD - SUBLANES * q), SUBLANES)
                acc = acc + w_ref[pl.ds(k, 1), :] * p_ref[r, pl.ds(start, ch), :]
            o_ref[pl.ds(base, ch), :] = acc
            return carry

        lax.fori_loop(0, S // ch, chunk, 0)

    return pl.pallas_call(
        body, name=name, grid=(B, C // tc),
        in_specs=[pl.BlockSpec((S, tc), lambda b, j: (b, col_a(j))),
                  pl.BlockSpec((S, tc), lambda b, j: (b, col_a(j) + per)),
                  pl.BlockSpec((CONV_TAPS_PAD, tc), lambda b, j: (0, j)),
                  pl.BlockSpec((1, tc), lambda b, j: (0, j))],
        out_specs=pl.BlockSpec((S, tc), lambda b, j: (b, j)),
        out_shape=jax.ShapeDtypeStruct((B * S, C), F32),
        scratch_shapes=[pltpu.VMEM((SUBLANES, S + CONV_TAPS_PAD, tc), F32)],
        compiler_params=_cp(("parallel", "parallel")),
    )(h1, h1, w_dw, b_dw)


def _conv_bwd(dd, h1, w_dw, *, B, S, name):
    C = w_dw.shape[1]
    taps = CONV_TAPS_PAD - 1
    tc = LANES
    ch = _tile(S, 128)
    per, col_a = _conv_cols(C, tc)

    def body(d_ref, a_ref, g_ref, w_ref, du_ref, dw_ref, db_ref, p_ref, q_ref):
        b = pl.program_id(1)

        @pl.when(b == 0)
        def _():
            dw_ref[...] = jnp.zeros_like(dw_ref)
            db_ref[...] = jnp.zeros_like(db_ref)

        _glu_shifted(a_ref, g_ref, p_ref, S)
        d = d_ref[...]
        rows = lax.broadcasted_iota(jnp.int32, d.shape, 0)
        for r in range(SUBLANES):
            q_ref[r, S:S + CONV_TAPS_PAD, :] = jnp.zeros((CONV_TAPS_PAD, tc), F32)
            q_ref[r, 0:S, :] = d if r == 0 else jnp.where(rows < S - r, pltpu.roll(d, S - r, 0), 0.0)
        db_ref[...] += _fold8(d)

        def chunk(ci, carry):
            base = pl.multiple_of(ci * ch, ch)
            dch = d_ref[pl.ds(base, ch), :]
            acc = jnp.zeros((ch, tc), F32)
            for k in range(taps):
                q, r = divmod(taps - 1 - k, SUBLANES)
                up = pl.multiple_of(base + SUBLANES * q, SUBLANES)
                acc = acc + w_ref[pl.ds(k, 1), :] * q_ref[r, pl.ds(up, ch), :]
                down = pl.multiple_of(base + (CONV_TAPS_PAD - SUBLANES * q), SUBLANES)
                dw_ref[k] += _fold8(dch * p_ref[r, pl.ds(down, ch), :])
            du_ref[pl.ds(base, ch), :] = acc
            return carry

        lax.fori_loop(0, S // ch, chunk, 0)

    return pl.pallas_call(
        body, name=name, grid=(C // tc, B),
        in_specs=[pl.BlockSpec((S, tc), lambda j, b: (b, j)),
                  pl.BlockSpec((S, tc), lambda j, b: (b, col_a(j))),
                  pl.BlockSpec((S, tc), lambda j, b: (b, col_a(j) + per)),
                  pl.BlockSpec((CONV_TAPS_PAD, tc), lambda j, b: (0, j))],
        out_specs=[pl.BlockSpec((S, tc), lambda j, b: (b, j)),
                   pl.BlockSpec((CONV_TAPS_PAD, SUBLANES, tc), lambda j, b: (0, 0, j)),
                   pl.BlockSpec((SUBLANES, tc), lambda j, b: (0, j))],
        out_shape=[jax.ShapeDtypeStruct((B * S, C), F32),
                   jax.ShapeDtypeStruct((CONV_TAPS_PAD, SUBLANES, C), F32),
                   jax.ShapeDtypeStruct((SUBLANES, C), F32)],
        scratch_shapes=[pltpu.VMEM((SUBLANES, S + CONV_TAPS_PAD, tc), F32),
                        pltpu.VMEM((SUBLANES, S + CONV_TAPS_PAD, tc), F32)],
        compiler_params=_cp(("parallel", "arbitrary")),
    )(dd, h1, h1, w_dw)


def _ln_silu_fwd(v, g, b, *, name):
    T, C = v.shape
    tm = _tile(T, 512)

    def body(v_ref, g_ref, b_ref, s_ref, xh_ref, rs_ref):
        y, xh, rstd = _ln_rows(v_ref[...], g_ref[...], b_ref[...])
        s_ref[...] = (y * _sigmoid(y)).astype(_MXU)
        xh_ref[...] = xh
        rs_ref[...] = rstd

    row = lambda i: (i, 0)
    vec = pl.BlockSpec((1, C), lambda i: (0, 0))
    return pl.pallas_call(
        body, name=name, grid=(T // tm,),
        in_specs=[pl.BlockSpec((tm, C), row), vec, vec],
        out_specs=[pl.BlockSpec((tm, C), row), pl.BlockSpec((tm, C), row), pl.BlockSpec((tm, 1), row)],
        out_shape=[jax.ShapeDtypeStruct((T, C), _MXU), jax.ShapeDtypeStruct((T, C), F32),
                   jax.ShapeDtypeStruct((T, 1), F32)],
        compiler_params=_cp(("parallel",)),
    )(v, g, b)


def _ln_silu_bwd(ds, xh, rstd, g, b, *, name):
    T, C = ds.shape
    tm = _tile(T, 256)

    def body(ds_ref, xh_ref, rs_ref, g_ref, b_ref, dv_ref, dg_ref, db_ref):
        @pl.when(pl.program_id(0) == 0)
        def _():
            dg_ref[...] = jnp.zeros_like(dg_ref)
            db_ref[...] = jnp.zeros_like(db_ref)

        xh = xh_ref[...]
        gam = g_ref[...]
        y = xh * gam + b_ref[...]
        sig = _sigmoid(y)
        dln = ds_ref[...] * (sig * (1.0 + y * (1.0 - sig)))
        dv_ref[...] = _ln_bwd_rows(dln, xh, rs_ref[...], gam)
        dg_ref[...] += _fold8(dln * xh)
        db_ref[...] += _fold8(dln)

    row = lambda i: (i, 0)
    fixed = lambda i: (0, 0)
    vec = pl.BlockSpec((1, C), fixed)
    part = pl.BlockSpec((SUBLANES, C), fixed)
    return pl.pallas_call(
        body, name=name, grid=(T // tm,),
        in_specs=[pl.BlockSpec((tm, C), row), pl.BlockSpec((tm, C), row), pl.BlockSpec((tm, 1), row), vec, vec],
        out_specs=[pl.BlockSpec((tm, C), row), part, part],
        out_shape=[jax.ShapeDtypeStruct((T, C), F32)] + [jax.ShapeDtypeStruct((SUBLANES, C), F32)] * 2,
        compiler_params=_cp(("arbitrary",)),
    )(ds, xh, rstd, g, b)


def _glu_bwd(du, h1, *, name):
    T, C = du.shape
    il = C // 2
    tm = _tile(T, 256)

    def body(du_ref, h_ref, dh_ref, cs_ref):
        @pl.when(pl.program_id(0) == 0)
        def _():
            cs_ref[...] = jnp.zeros_like(cs_ref)

        for hb in range(2):
            a = h_ref[:, 2 * hb * il:(2 * hb + 1) * il]
            gate = h_ref[:, (2 * hb + 1) * il:(2 * hb + 2) * il]
            d = du_ref[:, hb * il:(hb + 1) * il]
            sig = _sigmoid(gate)
            da = d * sig
            dgate = d * a * sig * (1.0 - sig)
            dh_ref[:, 2 * hb * il:(2 * hb + 1) * il] = da.astype(_MXU)
            dh_ref[:, (2 * hb + 1) * il:(2 * hb + 2) * il] = dgate.astype(_MXU)
            cs_ref[:, 2 * hb * il:(2 * hb + 1) * il] += _fold8(da)
            cs_ref[:, (2 * hb + 1) * il:(2 * hb + 2) * il] += _fold8(dgate)

    row = lambda i: (i, 0)
    return pl.pallas_call(
        body, name=name, grid=(T // tm,),
        in_specs=[pl.BlockSpec((tm, C), row), pl.BlockSpec((tm, 2 * C), row)],
        out_specs=[pl.BlockSpec((tm, 2 * C), row), pl.BlockSpec((SUBLANES, 2 * C), lambda i: (0, 0))],
        out_shape=[jax.ShapeDtypeStruct((T, 2 * C), _MXU), jax.ShapeDtypeStruct((SUBLANES, 2 * C), F32)],
        compiler_params=_cp(("arbitrary",)),
    )(du, h1)


def _tril_mask(n):
    return lax.broadcasted_iota(jnp.int32, (n, n), 0) >= lax.broadcasted_iota(jnp.int32, (n, n), 1)


def _split_uv(t, il):
    u = jnp.concatenate([t[:, 0:il], t[:, 2 * il:3 * il]], axis=1)
    v = jnp.concatenate([t[:, il:2 * il], t[:, 3 * il:4 * il]], axis=1)
    return u, v


def _gmlp_gate_fwd(p, g, b, w_s, bsb, *, name):
    T, C2 = p.shape
    C = C2 // 2
    il = C // 2
    G, L, _ = w_s.shape
    assert G * L == C
    tm = _tile(T, 2 * L, L)

    def body(p_ref, g_ref, b_ref, ws_ref, bs_ref, us_ref, xh_ref, rs_ref, vn_ref, u_ref):
        z, _ = _gelu_parts(p_ref[...])
        u, v = _split_uv(z, il)
        vn, xh, rstd = _ln_rows(v, g_ref[...], b_ref[...])
        xh_ref[...] = xh
        rs_ref[...] = rstd
        vn_ref[...] = vn.astype(_MXU)
        u_ref[...] = u
        mask = _tril_mask(L)
        for gi in range(G):
            wc = jnp.where(mask, ws_ref[gi], 0.0).astype(_MXU)
            cols = slice(gi * L, (gi + 1) * L)
            for c in range(tm // L):
                rows = slice(c * L, (c + 1) * L)
                s = jnp.dot(wc, vn_ref[rows, cols], preferred_element_type=F32) + bs_ref[:, cols]
                us_ref[rows, cols] = (u_ref[rows, cols] * s).astype(_MXU)

    row = lambda i: (i, 0)
    fixed = lambda i: (0, 0)
    return pl.pallas_call(
        body, name=name, grid=(T // tm,),
        in_specs=[pl.BlockSpec((tm, C2), row), pl.BlockSpec((1, C), fixed), pl.BlockSpec((1, C), fixed),
                  pl.BlockSpec((G, L, L), lambda i: (0, 0, 0)), pl.BlockSpec((L, C), fixed)],
        out_specs=[pl.BlockSpec((tm, C), row), pl.BlockSpec((tm, C), row), pl.BlockSpec((tm, 1), row)],
        out_shape=[jax.ShapeDtypeStruct((T, C), _MXU), jax.ShapeDtypeStruct((T, C), F32),
                   jax.ShapeDtypeStruct((T, 1), F32)],
        scratch_shapes=[pltpu.VMEM((tm, C), _MXU), pltpu.VMEM((tm, C), F32)],
        compiler_params=_cp(("parallel",)),
    )(p, g, b, w_s, bsb)


def _gmlp_gate_bwd(dus, p, xh, rstd, g, b, w_s, bsb, *, name):
    T, C2 = p.shape
    C = C2 // 2
    il = C // 2
    G, L, _ = w_s.shape
    tm = _tile(T, 2 * L, L)

    def body(dus_ref, p_ref, xh_ref, rs_ref, g_ref, b_ref, ws_ref, bs_ref,
             dp_ref, dg_ref, db_ref, cs_ref, dws_ref, dbs_ref, vn_ref, u_ref, dvn_ref, du_ref):
        @pl.when(pl.program_id(0) == 0)
        def _():
            dg_ref[...] = jnp.zeros_like(dg_ref)
            db_ref[...] = jnp.zeros_like(db_ref)
            cs_ref[...] = jnp.zeros_like(cs_ref)
            dws_ref[...] = jnp.zeros_like(dws_ref)
            dbs_ref[...] = jnp.zeros_like(dbs_ref)

        z, gp = _gelu_parts(p_ref[...])
        u, _ = _split_uv(z, il)
        xh = xh_ref[...]
        gam = g_ref[...]
        vn_ref[...] = (xh * gam + b_ref[...]).astype(_MXU)
        u_ref[...] = u
        mask = _tril_mask(L)
        for gi in range(G):
            wc = jnp.where(mask, ws_ref[gi], 0.0).astype(_MXU)
            cols = slice(gi * L, (gi + 1) * L)
            for c in range(tm // L):
                rows = slice(c * L, (c + 1) * L)
                vnb = vn_ref[rows, cols]
                s = jnp.dot(wc, vnb, preferred_element_type=F32) + bs_ref[:, cols]
                d = dus_ref[rows, cols]
                du_ref[rows, cols] = d * s
                ds = d * u_ref[rows, cols]
                dbs_ref[:, cols] += ds
                dsb = ds.astype(_MXU)
                dw = lax.dot_general(dsb, vnb, (((1,), (1,)), ((), ())), preferred_element_type=F32)
                dws_ref[gi] += jnp.where(mask, dw, 0.0)
                dvn_ref[rows, cols] = lax.dot_general(wc, dsb, (((0,), (0,)), ((), ())), preferred_element_type=F32)
        dvn = dvn_ref[...]
        dg_ref[...] += _fold8(dvn * xh)
        db_ref[...] += _fold8(dvn)
        dv = _ln_bwd_rows(dvn, xh, rs_ref[...], gam)
        du = du_ref[...]
        for hb in range(2):
            for part, src in ((0, du), (1, dv)):
                lo = (2 * hb + part) * il
                dp = src[:, hb * il:(hb + 1) * il] * gp[:, lo:lo + il]
                dp_ref[:, lo:lo + il] = dp.astype(_MXU)
                cs_ref[:, lo:lo + il] += _fold8(dp)

    row = lambda i: (i, 0)
    fixed = lambda i: (0, 0)
    part_c = pl.BlockSpec((SUBLANES, C), fixed)
    return pl.pallas_call(
        body, name=name, grid=(T // tm,),
        in_specs=[pl.BlockSpec((tm, C), row), pl.BlockSpec((tm, C2), row), pl.BlockSpec((tm, C), row),
                  pl.BlockSpec((tm, 1), row), pl.BlockSpec((1, C), fixed), pl.BlockSpec((1, C), fixed),
                  pl.BlockSpec((G, L, L), lambda i: (0, 0, 0)), pl.BlockSpec((L, C), fixed)],
        out_specs=[pl.BlockSpec((tm, C2), row), part_c, part_c, pl.BlockSpec((SUBLANES, C2), fixed),
                   pl.BlockSpec((G, L, L), lambda i: (0, 0, 0)), pl.BlockSpec((L, C), fixed)],
        out_shape=[jax.ShapeDtypeStruct((T, C2), _MXU), jax.ShapeDtypeStruct((SUBLANES, C), F32),
                   jax.ShapeDtypeStruct((SUBLANES, C), F32), jax.ShapeDtypeStruct((SUBLANES, C2), F32),
                   jax.ShapeDtypeStruct((G, L, L), F32), jax.ShapeDtypeStruct((L, C), F32)],
        scratch_shapes=[pltpu.VMEM((tm, C), _MXU), pltpu.VMEM((tm, C), F32), pltpu.VMEM((tm, C), F32),
                        pltpu.VMEM((tm, C), F32)],
        compiler_params=_cp(("arbitrary",)),
    )(dus, p, xh, rstd, g, b, w_s, bsb)


def _ffn_conv(h, prev8, w_ref, b_ref):
    h1 = _shift_down(prev8, h, 1)
    h2 = _shift_down(prev8, h, 2)
    return w_ref[pl.ds(2, 1), :] * h + w_ref[pl.ds(1, 1), :] * h1 + w_ref[pl.ds(0, 1), :] * h2 + b_ref[...]


def _resident(block, imap):
    return pl.BlockSpec(block, imap, pipeline_mode=pl.Buffered(1))


def _ffn_fwd_half(j, xb, w_up, w_down, b_up, w_dw, b_dw, *, S, name, prev=None, tail=None):
    T, D = xb.shape
    N = w_up.shape[-1]
    tn = N // N_CHIPS
    tm = _tile(S, 256)
    spt = S // tm
    last = prev is not None
    alpha = tail[1] if last else None

    def body(*refs):
        x_ref, wu_ref, wd_ref, bu_ref, wc_ref, bc_ref = refs[:6]
        if last:
            yp_ref, res_ref, bd_ref, g_ref, b_ref = refs[9:14]
            h_ref, hc_ref, f_ref, y_ref, yb_ref, xh_ref, rs_ref, carry_ref = refs[14:22]
        else:
            h_ref, hc_ref, f_ref, yp_ref, carry_ref = refs[6:11]

        @pl.when(pl.program_id(0) % spt == 0)
        def _():
            carry_ref[...] = jnp.zeros_like(carry_ref)

        h = jnp.dot(x_ref[...].astype(_MXU), wu_ref[...].astype(_MXU), preferred_element_type=F32) + bu_ref[...]
        hq = h.astype(_HDT)
        h_ref[...] = hq
        h = hq.astype(F32)
        hc = _ffn_conv(h, carry_ref[...], wc_ref, bc_ref)
        hc_ref[...] = hc.astype(_HDT)
        carry_ref[...] = h[tm - SUBLANES:tm]
        gte = hc[:, :tn]
        f = (gte * _sigmoid(gte) * hc[:, tn:]).astype(_MXU)
        f_ref[...] = f
        y = jnp.dot(f, wd_ref[...].astype(_MXU), preferred_element_type=F32)
        if not last:
            yp_ref[...] = y
            return
        z = y + yp_ref[...] + bd_ref[...] + alpha * res_ref[...]
        out, xh, rstd = _ln_rows(z, g_ref[...], b_ref[...])
        y_ref[...] = out
        yb_ref[...] = out.astype(_MXU)
        xh_ref[...] = xh
        rs_ref[...] = rstd

    row = lambda i: (i, 0)
    pair = lambda i: (0, j)
    vec = pl.BlockSpec((1, D), lambda i: (0, 0))
    tile = pl.BlockSpec((tm, D), row)
    in_specs = [tile, _resident((None, D, 2 * tn), lambda i: (0, 0, j)), _resident((None, tn, D), lambda i: (0, j, 0)),
                pl.BlockSpec((1, 2 * tn), pair), pl.BlockSpec((SUBLANES, 2 * tn), pair), pl.BlockSpec((1, 2 * tn), pair)]
    operands = [xb, w_up, w_down, b_up, w_dw, b_dw]
    wide = pl.BlockSpec((tm, 2 * tn), lambda i: (i, j))
    out_specs = [wide, wide, pl.BlockSpec((tm, tn), lambda i: (i, j))]
    out_shape = [jax.ShapeDtypeStruct((T, N), _HDT), jax.ShapeDtypeStruct((T, N), _HDT),
                 jax.ShapeDtypeStruct((T, N // 2), _MXU)]
    aliases = {}
    if last:
        res, _, b_down, g, b = tail
        in_specs += [ANY, ANY, ANY, tile, tile, vec, vec, vec]
        operands += list(prev) + [res, b_down, g, b]
        aliases = {6: 0, 7: 1, 8: 2}
        out_specs += [tile, tile, tile, pl.BlockSpec((tm, 1), row)]
        out_shape += [jax.ShapeDtypeStruct((T, D), F32), jax.ShapeDtypeStruct((T, D), _MXU),
                      jax.ShapeDtypeStruct((T, D), F32), jax.ShapeDtypeStruct((T, 1), F32)]
    else:
        out_specs.append(tile)
        out_shape.append(jax.ShapeDtypeStruct((T, D), F32))
    return pl.pallas_call(
        body, name=name, grid=(T // tm,), in_specs=in_specs, out_specs=out_specs, out_shape=out_shape,
        input_output_aliases=aliases, scratch_shapes=[pltpu.VMEM((SUBLANES, 2 * tn), F32)],
        compiler_params=_cp(("arbitrary",)),
    )(*operands)


def _ffn_bwd_half(j, dzb, w_down, w_up, hs, hcs, w_dw, *, S, name, dz=None, alpha=None, prev=None, ln=None):
    T, D = dzb.shape
    N = hs.shape[1]
    tn = N // N_CHIPS
    tm = _tile(S, 256)
    spt = S // tm
    nt = T // tm
    last = prev is not None

    def body(*refs):
        dz_ref, wd_ref, wu_ref, h_ref, hc_ref, wc_ref = refs[:6]
        if last:
            dxp_ref, xh_ref, rs_ref, g_ref = refs[7:11]
            dh_ref, cs_ref, dw_ref, db_ref, dz1_ref, dz1b_ref, dg1_ref, db1_ref, cs1_ref, carry_ref = refs[11:21]
        else:
            dzf_ref = refs[6]
            dh_ref, cs_ref, dw_ref, db_ref, dxp_ref, carry_ref = refs[7:13]
        i = pl.program_id(0)
        ii = nt - 1 - i

        @pl.when(i == 0)
        def _():
            cs_ref[...] = jnp.zeros_like(cs_ref)
            dw_ref[...] = jnp.zeros_like(dw_ref)
            db_ref[...] = jnp.zeros_like(db_ref)
            if last:
                dg1_ref[...] = jnp.zeros_like(dg1_ref)
                db1_ref[...] = jnp.zeros_like(db1_ref)
                cs1_ref[...] = jnp.zeros_like(cs1_ref)

        df = lax.dot_general(dz_ref[...].astype(_MXU), wd_ref[...].astype(_MXU), (((1,), (1,)), ((), ())),
                             preferred_element_type=F32)
        h = h_ref[...].astype(F32)
        gte, val = hc_ref[:, :tn].astype(F32), hc_ref[:, tn:].astype(F32)
        sig = _sigmoid(gte)
        dval = df * (gte * sig)
        dg = df * val * (sig * (1.0 + gte * (1.0 - sig)))
        dhc = jnp.concatenate([dg, dval], axis=1)
        nxt = jnp.where((ii + 1) % spt == 0, 0.0, carry_ref[...])
        d1 = _shift_up(dhc, nxt, 1)
        d2 = _shift_up(dhc, nxt, 2)
        carry_ref[...] = dhc[0:SUBLANES]
        db_ref[...] += _fold8(dhc)
        dw_ref[2] += _fold8(dhc * h)
        dw_ref[1] += _fold8(d1 * h)
        dw_ref[0] += _fold8(d2 * h)
        dh = wc_ref[pl.ds(2, 1), :] * dhc + wc_ref[pl.ds(1, 1), :] * d1 + wc_ref[pl.ds(0, 1), :] * d2
        cs_ref[...] += _fold8(dh)
        dhb = dh.astype(_MXU)
        dh_ref[...] = dhb
        dx = lax.dot_general(dhb, wu_ref[...].astype(_MXU), (((1,), (1,)), ((), ())), preferred_element_type=F32)
        if not last:
            dxp_ref[...] = dx + alpha * dzf_ref[...]
            return
        d = dx + dxp_ref[...]
        xh = xh_ref[...]
        dz1 = _ln_bwd_rows(d, xh, rs_ref[...], g_ref[...])
        dz1_ref[...] = dz1
        dz1b_ref[...] = dz1.astype(_MXU)
        dg1_ref[...] += _fold8(d * xh)
        db1_ref[...] += _fold8(d)
        cs1_ref[...] += _fold8(dz1)

    rev = lambda i: (nt - 1 - i, 0)
    fixed = lambda i: (0, 0)
    pair = lambda i: (0, j)
    tile = pl.BlockSpec((tm, D), rev)
    wide = pl.BlockSpec((tm, 2 * tn), lambda i: (nt - 1 - i, j))
    part = pl.BlockSpec((SUBLANES, 2 * tn), fixed)
    in_specs = [tile, _resident((None, tn, D), lambda i: (0, j, 0)), _resident((None, D, 2 * tn), lambda i: (0, 0, j)),
                wide, wide, pl.BlockSpec((SUBLANES, 2 * tn), pair)]
    operands = [dzb, w_down, w_up, hs, hcs, w_dw]
    out_specs = [wide, part, pl.BlockSpec((3, SUBLANES, 2 * tn), lambda i: (0, 0, 0)), part]
    out_shape = [jax.ShapeDtypeStruct((T, N), _MXU), jax.ShapeDtypeStruct((SUBLANES, 2 * tn), F32),
                 jax.ShapeDtypeStruct((3, SUBLANES, 2 * tn), F32), jax.ShapeDtypeStruct((SUBLANES, 2 * tn), F32)]
    aliases = {}
    if last:
        xh, rstd, g = ln
        in_specs += [ANY, tile, tile, pl.BlockSpec((tm, 1), rev), pl.BlockSpec((1, D), fixed)]
        operands += [prev[0], prev[1], xh, rstd, g]
        aliases = {6: 0}
        out_specs += [tile, tile] + [pl.BlockSpec((SUBLANES, D), fixed)] * 3
        out_shape += [jax.ShapeDtypeStruct((T, D), F32), jax.ShapeDtypeStruct((T, D), _MXU)] \
            + [jax.ShapeDtypeStruct((SUBLANES, D), F32)] * 3
    else:
        in_specs.append(tile)
        operands.append(dz)
        out_specs.append(tile)
        out_shape.append(jax.ShapeDtypeStruct((T, D), F32))
    return pl.pallas_call(
        body, name=name, grid=(nt,), in_specs=in_specs, out_specs=out_specs, out_shape=out_shape,
        input_output_aliases=aliases, scratch_shapes=[pltpu.VMEM((SUBLANES, 2 * tn), F32)],
        compiler_params=_cp(("arbitrary",)),
    )(*operands)


def _sum_pieces(g, r, me, layer, acc, n_layers, *, name):
    _, pr, pc = g.shape
    tr = _tile(pr, 128)

    def body(me_ref, g_ref, r_ref, *rest):
        o_ref = rest[-1]
        total = g_ref[...].astype(F32)
        for s in range(N_DEV - 1):
            total = total + r_ref[s].astype(F32)
        o_ref[...] = total

    in_specs = [pl.BlockSpec((None, tr, pc), lambda i, me_ref: (me_ref[0], i, 0)),
                pl.BlockSpec((N_DEV - 1, tr, pc), lambda i, me_ref: (0, i, 0))]
    operands = [me, g, r]
    aliases = {}
    if acc is not None:
        in_specs.append(ANY)
        operands.append(acc)
        aliases = {3: 0}
    return pl.pallas_call(
        body, name=name,
        grid_spec=pltpu.PrefetchScalarGridSpec(
            num_scalar_prefetch=1, grid=(pr // tr,), in_specs=in_specs,
            out_specs=pl.BlockSpec((None, tr, pc), lambda i, me_ref: (layer, i, 0))),
        out_shape=jax.ShapeDtypeStruct((n_layers, pr, pc), F32),
        input_output_aliases=aliases,
        compiler_params=_cp(("parallel",)),
    )(*operands)


def _adam_math(w, g, m, v):
    bc1 = 1.0 - ADAM_B1 ** ADAM_STEP
    bc2 = 1.0 - ADAM_B2 ** ADAM_STEP
    m = ADAM_B1 * m + (1.0 - ADAM_B1) * g
    v = ADAM_B2 * v + (1.0 - ADAM_B2) * (g * g)
    return -ADAM_LR * ((m / bc1) / (jnp.sqrt(v / bc2) + ADAM_EPS) + ADAM_WD * w), m, v


def _adam(w, g, m, v, *, name):
    R, C = w.shape
    tr = _tile(R, 256)

    def body(w_ref, g_ref, m_ref, v_ref, d_ref, mo_ref, vo_ref):
        d_ref[...], mo_ref[...], vo_ref[...] = _adam_math(w_ref[...], g_ref[...], m_ref[...], v_ref[...])

    spec = pl.BlockSpec((tr, C), lambda i: (i, 0))
    return pl.pallas_call(
        body, name=name, grid=(R // tr,), in_specs=[spec] * 4, out_specs=[spec] * 3,
        out_shape=[jax.ShapeDtypeStruct((R, C), F32)] * 3,
        compiler_params=_cp(("parallel",)),
    )(w, g, m, v)


def _adam_halves(w, own, got, m, v, core, *, name):
    L, R, C = w.shape
    rh = R // 2
    tr = _tile(rh, 256)
    nt = rh // tr

    def body(c_ref, w_ref, own_ref, got_ref, m_ref, v_ref, g_ref, d_ref, mo_ref, vo_ref):
        g = jnp.where(pl.program_id(1) == c_ref[0], own_ref[...], got_ref[...])
        g_ref[...] = g
        d_ref[...], mo_ref[...], vo_ref[...] = _adam_math(w_ref[...], g, m_ref[...], v_ref[...])

    full = pl.BlockSpec((None, tr, C), lambda l, h, t, c_ref: (l, h * nt + t, 0))
    half = pl.BlockSpec((None, tr, C), lambda l, h, t, c_ref: (l, t, 0))
    return pl.pallas_call(
        body, name=name,
        grid_spec=pltpu.PrefetchScalarGridSpec(
            num_scalar_prefetch=1, grid=(L, 2, nt), in_specs=[full, half, half, full, full], out_specs=[full] * 4),
        out_shape=[jax.ShapeDtypeStruct((L, R, C), F32)] * 4,
        compiler_params=_cp(("parallel", "parallel", "parallel")),
    )(core, w, own, got, m, v)


def _remote(src, dst, send, recv, dev):
    return pltpu.make_async_remote_copy(src_ref=src, dst_ref=dst, send_sem=send, recv_sem=recv,
                                        device_id=dev, device_id_type=MESH)


def _place_w(shard, pos, layer, *, axis, name):
    _, R, C = shard.shape
    tr = _tile(R, 512, 16)
    nt = R // tr
    if axis == 2:
        out_shape = (1, R, N_CHIPS * C)
        out_map = lambda t, q: (0, t, q[0])
    else:
        out_shape = (1, N_CHIPS * R, C)
        out_map = lambda t, q: (0, q[0] * nt + t, 0)

    def body(q_ref, s_ref, o_ref):
        o_ref[...] = s_ref[...].astype(_WIRE)

    return pl.pallas_call(
        body, name=name,
        grid_spec=pltpu.PrefetchScalarGridSpec(
            num_scalar_prefetch=1, grid=(nt,),
            in_specs=[pl.BlockSpec((None, tr, C), lambda t, q: (layer, t, 0))],
            out_specs=pl.BlockSpec((None, tr, C), out_map)),
        out_shape=jax.ShapeDtypeStruct(out_shape, _WIRE),
        compiler_params=_cp(("parallel",)),
    )(pos, shard)


def _ag_window(ref, kind, px, py, h):
    axis, perm = kind
    q = 2 * px + py
    if perm:
        q = _perm_idx(q)
    if axis == 2:
        R, C = ref.shape[1], ref.shape[2] // N_CHIPS
        rh = R // 2
        return ref.at[:, pl.ds(pl.multiple_of(h * rh, 16), rh), pl.ds(pl.multiple_of(q * C, LANES), C)]
    R = ref.shape[1] // N_CHIPS
    rh = R // 2
    return ref.at[:, pl.ds(pl.multiple_of(q * R + h * rh, 16), rh), :]


def _ag_ici_copies(refs, kinds, send, recv):
    x, y, c = lax.axis_index("x"), lax.axis_index("y"), lax.axis_index("c")
    chips = [(1 - x, y), (x, 1 - y), (1 - x, 1 - y)]
    sends, recvs = [], []
    for a, (ref, kind) in enumerate(zip(refs, kinds)):
        own = _ag_window(ref, kind, x, y, c)
        for i, (px, py) in enumerate(chips):
            k = 3 * a + i
            sends.append(_remote(own, own, send.at[k], recv.at[k], (px, py, c)))
            recvs.append(_remote(own, _ag_window(ref, kind, px, py, c), send.at[k], recv.at[k], (px, py, c)))
    return sends, recvs


def _ag_start(arrs, kinds, after, *, name):
    n = len(arrs)

    def body(*refs):
        in_refs = refs[:n]
        send, recv = refs[n + len(after)], refs[n + len(after) + 1]
        token = refs[-1]
        sends, _ = _ag_ici_copies(in_refs, kinds, send, recv)
        for cp in sends:
            cp.start()
        token[...] = jnp.zeros_like(token)

    sems = pltpu.SemaphoreType.DMA((3 * n,))
    out = pl.pallas_call(
        body, name=name,
        out_shape=(sems, sems) + tuple(pltpu.HBM(a.shape, a.dtype) for a in arrs)
        + (jax.ShapeDtypeStruct((SUBLANES, LANES), F32),),
        in_specs=(HBM,) * n + (ANY,) * len(after),
        out_specs=(SEMS, SEMS) + (HBM,) * n + (pl.BlockSpec(memory_space=pltpu.VMEM),),
        input_output_aliases={a: 2 + a for a in range(n)},
        compiler_params=pltpu.CompilerParams(has_side_effects=EFFECT),
    )(*[pltpu.with_memory_space_constraint(a, pltpu.HBM) for a in arrs], *after)
    return out[0], out[1], list(out[2:2 + n]), out[-1]


def _ag_wait(send, recv, arrs, kinds, after, *, name):
    n = len(arrs)

    def body(*refs):
        in_refs = refs[:n]
        send, recv = refs[n], refs[n + 1]
        sends, recvs = _ag_ici_copies(in_refs, kinds, send, recv)
        for cp in sends:
            cp.wait_send()
        for cp in recvs:
            cp.wait_recv()

    out = pl.pallas_call(
        body, name=name,
        out_shape=tuple(pltpu.HBM(a.shape, a.dtype) for a in arrs),
        in_specs=(HBM,) * n + (SEMS, SEMS) + (ANY,) * len(after), out_specs=(HBM,) * n,
        input_output_aliases={a: a for a in range(n)},
        compiler_params=pltpu.CompilerParams(has_side_effects=EFFECT),
    )(*arrs, send, recv, *after)
    return list(out)


def _ag_forward(arrs, kinds, *, name):
    n = len(arrs)

    def body(*refs):
        o_refs, send, recv = refs[n:2 * n], refs[2 * n], refs[2 * n + 1]
        x, y, c = lax.axis_index("x"), lax.axis_index("y"), lax.axis_index("c")
        chips = [(1 - x, y), (x, 1 - y), (1 - x, 1 - y)]
        sib = (x, y, 1 - c)
        sends, recvs = [], []
        for a, (ref, kind) in enumerate(zip(o_refs, kinds)):
            for i, (px, py) in enumerate(chips):
                k = 3 * a + i
                got = _ag_window(ref, kind, px, py, c)
                cp = _remote(got, got, send.at[k], recv.at[k], sib)
                cp.start()
                sends.append(cp)
                recvs.append(_remote(got, _ag_window(ref, kind, px, py, 1 - c), send.at[k], recv.at[k], sib))
        for cp in recvs:
            cp.wait_recv()
        for cp in sends:
            cp.wait_send()

    out = pl.pallas_call(
        body, name=name, in_specs=[ANY] * n, out_specs=[ANY] * n,
        out_shape=[jax.ShapeDtypeStruct(a.shape, a.dtype) for a in arrs],
        input_output_aliases={a: a for a in range(n)},
        scratch_shapes=[pltpu.SemaphoreType.DMA((3 * n,)), pltpu.SemaphoreType.DMA((3 * n,))],
    )(*arrs)
    return list(out)


def _flip(x, y, c, f):
    return ((1 - x) if f & 4 else x, (1 - y) if f & 2 else y, (1 - c) if f & 1 else c)


def _rs_copies(g_ref, land_ref, send, recv):
    x, y, c = lax.axis_index("x"), lax.axis_index("y"), lax.axis_index("c")
    cps = []
    for f in range(1, N_DEV):
        tx, ty, tcx = _flip(x, y, c, f)
        cps.append(_remote(g_ref.at[4 * tx + 2 * ty + tcx], land_ref.at[f - 1], send.at[f - 1], recv.at[f - 1],
                           (tx, ty, tcx)))
    return cps


def _rs_start(g, *, name):
    _, pr, pc = g.shape
    land_shape = (N_DEV - 1, pr, pc)

    def body(g_ref, land_ref, send, recv, g_thru, land_thru, token):
        for cp in _rs_copies(g_ref, land_ref, send, recv):
            cp.start()
        token[...] = jnp.zeros_like(token)

    sems = pltpu.SemaphoreType.DMA((N_DEV - 1,))
    return pl.pallas_call(
        body, name=name,
        out_shape=(sems, sems, pltpu.HBM(g.shape, g.dtype), pltpu.HBM(land_shape, g.dtype),
                   jax.ShapeDtypeStruct((SUBLANES, LANES), F32)),
        in_specs=(HBM, HBM), out_specs=(SEMS, SEMS, HBM, HBM, pl.BlockSpec(memory_space=pltpu.VMEM)),
        input_output_aliases={0: 2, 1: 3},
        compiler_params=pltpu.CompilerParams(has_side_effects=EFFECT),
    )(pltpu.with_memory_space_constraint(g, pltpu.HBM),
      pltpu.with_memory_space_constraint(lax.empty(land_shape, g.dtype), pltpu.HBM))


def _rs_wait(send, recv, g_thru, land_thru, after, *, name):
    def body(g_ref, land_ref, send, recv, after_ref, g_out, land_out):
        cps = _rs_copies(g_ref, land_ref, send, recv)
        for cp in cps:
            cp.wait_send()
        for cp in cps:
            cp.wait_recv()

    return pl.pallas_call(
        body, name=name,
        out_shape=(pltpu.HBM(g_thru.shape, g_thru.dtype), pltpu.HBM(land_thru.shape, land_thru.dtype)),
        in_specs=(HBM, HBM, SEMS, SEMS, ANY), out_specs=(HBM, HBM), input_output_aliases={0: 0, 1: 1},
        compiler_params=pltpu.CompilerParams(has_side_effects=EFFECT),
    )(g_thru, land_thru, send, recv, after)


def _pair_exchange(own, *, name):
    def body(own_ref, got_ref, send, recv):
        x, y, c = lax.axis_index("x"), lax.axis_index("y"), lax.axis_index("c")
        cp = _remote(own_ref, got_ref, send, recv, (x, y, 1 - c))
        cp.start()
        cp.wait_recv()
        cp.wait_send()

    return pl.pallas_call(
        body, name=name, in_specs=[ANY], out_specs=ANY, out_shape=jax.ShapeDtypeStruct(own.shape, own.dtype),
        scratch_shapes=[pltpu.SemaphoreType.DMA, pltpu.SemaphoreType.DMA],
    )(own)


def _allreduce_flat(vec, *, name):
    n = vec.shape[0]
    unit = N_DEV * SUBLANES * LANES
    npad = -(-n // unit) * unit
    rows = npad // (N_DEV * LANES)
    xin = jnp.pad(vec, (0, npad - n)).reshape(N_DEV, rows, LANES)

    def body(x_ref, y_ref, a_ref, send_a, recv_a, send_b, recv_b):
        x, y, c = lax.axis_index("x"), lax.axis_index("y"), lax.axis_index("c")
        me = 4 * x + 2 * y + c
        a_ref[me] = x_ref[me]
        sends, recvs = [], []
        for f in range(1, N_DEV):
            dev = _flip(x, y, c, f)
            t = 4 * dev[0] + 2 * dev[1] + dev[2]
            cp = _remote(x_ref.at[t], a_ref.at[me], send_a.at[f - 1], recv_a.at[f - 1], dev)
            cp.start()
            sends.append(cp)
            recvs.append(_remote(x_ref.at[me], a_ref.at[t], send_a.at[f - 1], recv_a.at[f - 1], dev))
        for cp in recvs:
            cp.wait_recv()
        for cp in sends:
            cp.wait_send()
        acc = a_ref[0]
        for s in range(1, N_DEV):
            acc = acc + a_ref[s]
        y_ref[me] = acc
        sends, recvs = [], []
        for f in range(1, N_DEV):
            dev = _flip(x, y, c, f)
            t = 4 * dev[0] + 2 * dev[1] + dev[2]
            cp = _remote(y_ref.at[me], y_ref.at[me], send_b.at[f - 1], recv_b.at[f - 1], dev)
            cp.start()
            sends.append(cp)
            recvs.append(_remote(y_ref.at[me], y_ref.at[t], send_b.at[f - 1], recv_b.at[f - 1], dev))
        for cp in recvs:
            cp.wait_recv()
        for cp in sends:
            cp.wait_send()

    vm = pl.BlockSpec(memory_space=pltpu.VMEM)
    out = pl.pallas_call(
        body, name=name, in_specs=[vm], out_specs=vm,
        out_shape=jax.ShapeDtypeStruct((N_DEV, rows, LANES), F32),
        scratch_shapes=[pltpu.VMEM((N_DEV, rows, LANES), F32)] + [pltpu.SemaphoreType.DMA((N_DEV - 1,))] * 4,
        compiler_params=_cp(),
    )(xin)
    return out.reshape(npad)[:n]


def _perm_cols(v, blocks=N_CHIPS):
    lead, n = v.shape[:-1], v.shape[-1]
    return v.reshape(lead + (blocks, n // blocks))[..., PERM, :].reshape(lead + (n,))


def _pack(arrs):
    return jnp.concatenate([a.reshape(-1).astype(F32) for a in arrs])


def _unpack(flat, shapes):
    out, pos = [], 0
    for s in shapes:
        n = 1
        for d in s:
            n *= d
        out.append(flat[pos:pos + n].reshape(s))
        pos += n
    return out


def kernel(x, conv_w_in, conv_b_in, conv_w_dw, conv_b_dw, conv_ln_g, conv_ln_b, conv_w_out, conv_b_out, gmlp_w_in, gmlp_b_in, gmlp_ln_g, gmlp_ln_b, gmlp_w_s, gmlp_b_s, gmlp_w_out, gmlp_b_out, ffn_w_up, ffn_b_up, ffn_w_dw, ffn_b_dw, ffn_w_down, ffn_b_down, norm1_g, norm1_b, norm2_g, norm2_b, loss_target, m_conv_w_in, m_conv_b_in, m_conv_w_dw, m_conv_b_dw, m_conv_ln_g, m_conv_ln_b, m_conv_w_out, m_conv_b_out, m_gmlp_w_in, m_gmlp_b_in, m_gmlp_ln_g, m_gmlp_ln_b, m_gmlp_w_s, m_gmlp_b_s, m_gmlp_w_out, m_gmlp_b_out, m_ffn_w_up, m_ffn_b_up, m_ffn_w_dw, m_ffn_b_dw, m_ffn_w_down, m_ffn_b_down, m_norm1_g, m_norm1_b, m_norm2_g, m_norm2_b, v_conv_w_in, v_conv_b_in, v_conv_w_dw, v_conv_b_dw, v_conv_ln_g, v_conv_ln_b, v_conv_w_out, v_conv_b_out, v_gmlp_w_in, v_gmlp_b_in, v_gmlp_ln_g, v_gmlp_ln_b, v_gmlp_w_s, v_gmlp_b_s, v_gmlp_w_out, v_gmlp_b_out, v_ffn_w_up, v_ffn_b_up, v_ffn_w_dw, v_ffn_b_dw, v_ffn_w_down, v_ffn_b_down, v_norm1_g, v_norm1_b, v_norm2_g, v_norm2_b):
    P = dict(locals())
    WEIGHTS = ['conv_w_in', 'conv_b_in', 'conv_w_dw', 'conv_b_dw', 'conv_ln_g', 'conv_ln_b', 'conv_w_out',
               'conv_b_out', 'gmlp_w_in', 'gmlp_b_in', 'gmlp_ln_g', 'gmlp_ln_b', 'gmlp_w_s', 'gmlp_b_s',
               'gmlp_w_out', 'gmlp_b_out', 'ffn_w_up', 'ffn_b_up', 'ffn_w_dw', 'ffn_b_dw', 'ffn_w_down',
               'ffn_b_down', 'norm1_g', 'norm1_b', 'norm2_g', 'norm2_b']
    BIG = ['conv_w_in', 'conv_w_out', 'gmlp_w_in', 'gmlp_w_out', 'ffn_w_up', 'ffn_w_down']
    SMALL_SHARDED = {'conv_w_dw': 2, 'gmlp_b_in': 1, 'gmlp_ln_g': 1, 'gmlp_ln_b': 1, 'gmlp_b_out': 1, 'ffn_w_dw': 2}

    B, S, D = x.shape
    T = B * S
    depth = norm1_g.shape[0]
    alpha = (2.0 * depth) ** 0.25
    C = conv_w_out.shape[-1]
    F2 = ffn_b_up.shape[-1]
    G, L = gmlp_w_s.shape[1], gmlp_w_s.shape[2]
    xi, yi, ci = lax.axis_index("x"), lax.axis_index("y"), lax.axis_index("c")
    shard = 2 * xi + yi

    i32 = lambda v: jnp.reshape(v, (1,)).astype(jnp.int32)
    pos_plain, pos_perm = i32(shard), i32(_perm_idx(shard))
    me_id, core_id = i32(4 * xi + 2 * yi + ci), i32(ci)

    groups = []
    for i in range(depth):
        mix = 'conv' if i % 2 == 0 else 'gmlp'
        groups.append((f"{mix}{i // 2}", [(mix + '_w_in', i // 2, 2, True), (mix + '_w_out', i // 2, 1, False)]))
        groups.append((f"ffn{i}", [('ffn_w_up', i, 2, True), ('ffn_w_down', i, 1, False)]))
    sm_names = list(SMALL_SHARDED)
    sm_shapes = [P[n].shape for n in sm_names]
    mine = _pack([P[n] for n in sm_names]) * (ci == 0).astype(F32)
    buf = jnp.zeros((N_CHIPS, mine.shape[0]), F32)
    buf = lax.dynamic_update_slice(buf, mine[None], (shard, 0))
    gathered = _allreduce_flat(buf.reshape(-1), name="ag_small").reshape(N_CHIPS, -1)

    started, order = {}, [gathered]
    for gname, members in groups:
        placed = [_place_w(P[n], pos_perm if perm else pos_plain, l, axis=axis, name=f"place_{n}_{l}")
                  for n, l, axis, perm in members]
        kinds = [(axis, perm) for _, _, axis, perm in members]
        send, recv, arrs, token = _ag_start(placed, kinds, order, name=f"ag_start_{gname}")
        order = [token]
        started[gname] = (send, recv, arrs, kinds, [(n, l) for n, l, _, _ in members])
    wts = {}

    def arrive(gname, after):
        send, recv, arrs, kinds, keys = started[gname]
        arrs = _ag_wait(send, recv, arrs, kinds, after, name=f"ag_wait_{gname}")
        arrs = _ag_forward(arrs, kinds, name=f"ag_fwd_{gname}")
        wts.update(zip(keys, arrs))

    full = {}
    for n, parts in zip(sm_names, zip(*[_unpack(gathered[k], sm_shapes) for k in range(N_CHIPS)])):
        full[n] = jnp.concatenate(parts, axis=SMALL_SHARDED[n])
    for n in WEIGHTS:
        if n not in BIG and n not in full:
            full[n] = P[n]

    assert G * L == C, "a gMLP group must be as wide as a chunk is long"

    def row(v):
        return v.reshape(1, -1)

    def pad_rows(v, r):
        return jnp.pad(v, ((0, r - v.shape[0]), (0, 0)))

    xf = x.reshape(T, D)
    saved = []
    cur, cur_b = xf, xf.astype(_MXU)
    for i in range(depth):
        j = i // 2
        sv = {'x': cur, 'xb': cur_b}
        arrive(groups[2 * i][0], order if i == 0 else [cur_b])
        if i % 2 == 0:
            b_in = row(_perm_cols(full['conv_b_in'][j]))
            h1 = _mm(cur_b, wts['conv_w_in', j], bl=0, bias=b_in, tm=_tile(T, 512), tn=_tile(2 * C, 1024, LANES),
                     tk=D, name=f"conv_in_{j}", n_outer=True)
            wdw = pad_rows(full['conv_w_dw'][j], CONV_TAPS_PAD)
            dwo = _conv_fwd(h1, wdw, row(full['conv_b_dw'][j]), B=B, S=S, name=f"conv_dw_{j}")
            s_act, xhc, rsc = _ln_silu_fwd(dwo, row(full['conv_ln_g'][j]), row(full['conv_ln_b'][j]),
                                           name=f"conv_ln_{j}")
            sv.update(h1=h1, wdw=wdw, act=s_act, xhc=xhc, rsc=rsc)
            y1 = _mm_res_ln(s_act, wts['conv_w_out', j], 0, row(full['conv_b_out'][j]), cur, alpha, row(norm1_g[i]),
                            row(norm1_b[i]), name=f"conv_out_ln_{j}")
        else:
            b_in = row(_perm_cols(full['gmlp_b_in'][j]))
            pre = _mm(cur_b, wts['gmlp_w_in', j], bl=0, bias=b_in, tm=_tile(T, 512), tn=_tile(2 * C, 1024, LANES),
                      tk=D, name=f"gmlp_in_{j}", n_outer=True)
            bsb = jnp.repeat(gmlp_b_s[j].T, L, axis=1)
            us, xhv, rsv = _gmlp_gate_fwd(pre, row(full['gmlp_ln_g'][j]), row(full['gmlp_ln_b'][j]), gmlp_w_s[j],
                                          bsb, name=f"gmlp_gate_{j}")
            sv.update(pre=pre, bsb=bsb, act=us, xhv=xhv, rsv=rsv)
            y1 = _mm_res_ln(us, wts['gmlp_w_out', j], 0, row(full['gmlp_b_out'][j]), cur, alpha, row(norm1_g[i]),
                            row(norm1_b[i]), name=f"gmlp_out_ln_{j}")
        x1, x1b, xh1, rs1 = y1
        arrive(groups[2 * i + 1][0], [x1b])
        wdw3 = pad_rows(_perm_cols(full['ffn_w_dw'][i]), SUBLANES)
        bdw3 = row(_perm_cols(ffn_b_dw[i]))
        ffn_in = (x1b, wts['ffn_w_up', i], wts['ffn_w_down', i], row(_perm_cols(ffn_b_up[i])), wdw3, bdw3)
        first = _ffn_fwd_half(0, *ffn_in, S=S, name=f"ffn_fwd_a_{i}")
        hs, hcs, f_act, x2, x2b, xh2, rs2 = _ffn_fwd_half(
            1, *ffn_in, S=S, name=f"ffn_fwd_b_{i}", prev=first,
            tail=(x1, alpha, row(ffn_b_down[i]), row(norm2_g[i]), row(norm2_b[i])))
        sv.update(x1=x1, x1b=x1b, xh1=xh1, rs1=rs1, hs=hs, hcs=hcs, f=f_act, wdw3=wdw3, xh2=xh2, rs2=rs2)
        saved.append(sv)
        cur, cur_b = x2, x2b

    sg = {n: [None] * full[n].shape[0] for n in WEIGHTS if n not in BIG}
    inflight = {n: [None] * P[n].shape[0] for n in BIG}
    deps = []
    tgt = loss_target.reshape(T, D)
    dcur = None
    loss_part = None
    tk_t = _tile(T, 2048)

    def wgrad(n, l, a_, b_, **kw):
        g = _mm(a_, b_, ta=True, out_dtype=_WIRE, tk=tk_t, name=f"{n}_dw_{l}", deps=deps, **kw)
        send, recv, g_thru, land, token = _rs_start(g, name=f"rs_start_{n}_{l}")
        inflight[n][l] = (send, recv, g_thru, land)
        deps.append(token)

    for i in reversed(range(depth)):
        j = i // 2
        sv = saved[i]
        if i == depth - 1:
            dz2, dz2b, dg, db, cs, loss_part = _ln_bwd(cur, sv['xh2'], sv['rs2'], row(norm2_g[i]), target=tgt,
                                                       name=f"ln2_bwd_head_{i}")
        else:
            dz2, dz2b, dg, db, cs = dcur
        sg['norm2_g'][i], sg['norm2_b'][i], sg['ffn_b_down'][i] = dg.sum(0), db.sum(0), cs.sum(0)
        Fh = F2 // 2
        wgrad('ffn_w_down', i, sv['f'], dz2b, tm=Fh // 2, tn=_tile(D, 1024, LANES), pieces=('row',))
        ffn_in = (dz2b, wts['ffn_w_down', i], wts['ffn_w_up', i], sv['hs'], sv['hcs'], sv['wdw3'])
        dh0, csu0, dwd0, dbd0, dxp = _ffn_bwd_half(0, *ffn_in, S=S, name=f"ffn_bwd_a_{i}", dz=dz2, alpha=alpha)
        dh, csu1, dwd1, dbd1, dz1, dz1b, dg, db, cs = _ffn_bwd_half(
            1, *ffn_in, S=S, name=f"ffn_bwd_b_{i}", prev=(dh0, dxp), ln=(sv['xh1'], sv['rs1'], row(norm1_g[i])))
        sg['ffn_b_up'][i] = _perm_cols(jnp.concatenate([csu0.sum(0), csu1.sum(0)], axis=-1))
        sg['ffn_w_dw'][i] = _perm_cols(jnp.concatenate([dwd0.sum(1), dwd1.sum(1)], axis=-1))
        sg['ffn_b_dw'][i] = _perm_cols(jnp.concatenate([dbd0.sum(0), dbd1.sum(0)], axis=-1))
        wgrad('ffn_w_up', i, sv['x1b'], dh, tm=D, tn=F2 // N_CHIPS, pieces=('col', True))
        sg['norm1_g'][i], sg['norm1_b'][i] = dg.sum(0), db.sum(0)
        if i % 2 == 0:
            sg['conv_b_out'][j] = cs.sum(0)
            wgrad('conv_w_out', j, sv['act'], dz1b, tm=_tile(C, 1024), tn=_tile(D, 1024, LANES), pieces=('row',))
            ds = _mm(dz1b, wts['conv_w_out', j], bl=0, tb=True, tm=_tile(T, 512), tn=_tile(C, 1024, LANES), tk=_tile(D, 1024, LANES),
                     name=f"conv_ds_{j}", deps=deps)
            ddw, dg, db = _ln_silu_bwd(ds, sv['xhc'], sv['rsc'], row(full['conv_ln_g'][j]),
                                       row(full['conv_ln_b'][j]), name=f"conv_ln_bwd_{j}")
            sg['conv_ln_g'][j], sg['conv_ln_b'][j] = dg.sum(0), db.sum(0)
            dglu, dwk, dbk = _conv_bwd(ddw, sv['h1'], sv['wdw'], B=B, S=S, name=f"conv_dw_bwd_{j}")
            sg['conv_w_dw'][j] = dwk.sum(1)[:conv_w_dw.shape[1]]
            sg['conv_b_dw'][j] = dbk.sum(0)
            dh1, csi = _glu_bwd(dglu, sv['h1'], name=f"conv_glu_bwd_{j}")
            sg['conv_b_in'][j] = _perm_cols(csi.sum(0))
            fam = 'conv_w_in'
        else:
            sg['gmlp_b_out'][j] = cs.sum(0)
            wgrad('gmlp_w_out', j, sv['act'], dz1b, tm=_tile(C, 1024), tn=_tile(D, 1024, LANES), pieces=('row',))
            dus = _mm(dz1b, wts['gmlp_w_out', j], bl=0, tb=True, tm=_tile(T, 512), tn=_tile(C, 1024, LANES),
                      tk=_tile(D, 1024, LANES), name=f"gmlp_dus_{j}", deps=deps)
            dh1, dg, db, csi, dws, dbs = _gmlp_gate_bwd(dus, sv['pre'], sv['xhv'], sv['rsv'], row(full['gmlp_ln_g'][j]),
                                                        row(full['gmlp_ln_b'][j]), gmlp_w_s[j], sv['bsb'],
                                                        name=f"gmlp_gate_bwd_{j}")
            sg['gmlp_ln_g'][j], sg['gmlp_ln_b'][j] = dg.sum(0), db.sum(0)
            sg['gmlp_b_in'][j] = _perm_cols(csi.sum(0))
            sg['gmlp_w_s'][j] = dws
            sg['gmlp_b_s'][j] = dbs.reshape(L, G, L).sum(-1).T
            fam = 'gmlp_w_in'
        wgrad(fam, j, sv['xb'], dh1, tm=D, tn=(2 * C) // N_CHIPS, pieces=('col', True))
        if i > 0:
            below = saved[i - 1]
            dcur = _mm_ln_bwd(dh1, wts[fam, j], dz1, alpha, below['xh2'], below['rs2'], row(norm2_g[i - 1]),
                              name=f"{fam}_dx_{j}", deps=deps)
        else:
            dcur = _mm(dh1, wts[fam, j], bl=0, tb=True, res=dz1, res_scale=alpha, tm=_tile(T, 512),
                       tn=_tile(D, 1024, LANES), tk=2 * C, name=f"{fam}_dx_{j}", deps=deps)
    grad_x = dcur.reshape(B, S, D)

    small_names = [n for n in WEIGHTS if n not in BIG]
    small_full = [jnp.stack(sg[n]) for n in small_names]
    flat = _pack(small_full + [loss_part])
    red = _allreduce_flat(flat, name="ar_small")
    red_parts = _unpack(red, [a.shape for a in small_full] + [loss_part.shape])
    loss = (0.5 / D) * jnp.sum(red_parts[-1])
    grads = {}
    for n, g in zip(small_names, red_parts[:-1]):
        if n in SMALL_SHARDED:
            ax = SMALL_SHARDED[n]
            width = P[n].shape[ax]
            g = lax.dynamic_slice_in_dim(g, shard * width, width, axis=ax)
        grads[n] = g

    big_out = {}
    for n in ['ffn_w_down', 'ffn_w_up', 'gmlp_w_out', 'gmlp_w_in', 'conv_w_out', 'conv_w_in']:
        own = None
        n_layers = len(inflight[n])
        for l in reversed(range(n_layers)):
            send, recv, g_thru, land = inflight[n][l]
            pc_, r = _rs_wait(send, recv, g_thru, land, dcur, name=f"rs_wait_{n}_{l}")
            own = _sum_pieces(pc_, r, me_id, l, own, n_layers, name=f"sum_{n}_{l}")
        got = _pair_exchange(own, name=f"px_{n}")
        big_out[n] = _adam_halves(P[n], own, got, P['m_' + n], P['v_' + n], core_id, name=f"adam_{n}")

    shapes = [P[n].shape for n in small_names]
    n_small = sum(functools.reduce(lambda p_, d_: p_ * d_, s_, 1) for s_ in shapes)
    unit = SUBLANES * LANES
    npad = -(-n_small // unit) * unit

    def flat2d(arrs, fill=0.0):
        v = _pack(arrs)
        return jnp.pad(v, (0, npad - n_small), constant_values=fill).reshape(-1, LANES)

    dl, mo, vo = _adam(flat2d([P[n] for n in small_names]), flat2d([grads[n] for n in small_names]),
                       flat2d([P['m_' + n] for n in small_names]),
                       flat2d([P['v_' + n] for n in small_names], fill=1.0), name="adam_small")
    small_out = {n: [grads[n], None, None, None] for n in small_names}
    for k, t in enumerate((dl, mo, vo)):
        for n, a in zip(small_names, _unpack(t.reshape(-1), shapes)):
            small_out[n][k + 1] = a

    outs = [loss, grad_x]
    for k in range(4):
        for n in WEIGHTS:
            outs.append(big_out[n][k] if n in BIG else small_out[n][k])
    return tuple(outs)
```

```python
import functools

import jax
import jax.numpy as jnp
from jax import lax
from jax.experimental import pallas as pl
from jax.experimental.pallas import tpu as pltpu

F32 = jnp.float32
_MXU = jnp.bfloat16
_WIRE = jnp.bfloat16
_HDT = jnp.bfloat16
_ADT = jnp.bfloat16
LN_EPS = 1e-5
ADAM_LR, ADAM_B1, ADAM_B2, ADAM_EPS, ADAM_WD, ADAM_STEP = 0.001, 0.9, 0.999, 1e-08, 0.01, 10
N_CHIPS = 4
N_DEV = 8
LANES = 128
SUBLANES = 8
CONV_TAPS_PAD = 32
VMEM_LIMIT = 56 << 20
MESH = pl.DeviceIdType.MESH
ANY = pl.BlockSpec(memory_space=pl.ANY)
HBM = pl.BlockSpec(memory_space=pltpu.HBM)
SEMS = pl.BlockSpec(memory_space=pltpu.SEMAPHORE)
EFFECT = pltpu.SideEffectType.DATAFLOW_SIDE_EFFECTING
PERM = (0, 2, 1, 3)


def _cp(sem=None):
    return pltpu.CompilerParams(dimension_semantics=sem, vmem_limit_bytes=VMEM_LIMIT)


def _tile(dim, pref, mult=SUBLANES):
    if dim <= pref:
        return dim
    t = (pref // mult) * mult
    while t > mult and dim % t:
        t -= mult
    assert dim % t == 0, (dim, pref, mult)
    return t


def _perm_idx(q):
    return (q % 2) * 2 + q // 2


def _fold8(t):
    r, n = t.shape
    return t.reshape(r // SUBLANES, SUBLANES, n).sum(axis=0)


def _ln_rows(z, g, b):
    mu = jnp.mean(z, axis=-1, keepdims=True)
    xc = z - mu
    var = jnp.mean(xc * xc, axis=-1, keepdims=True)
    rstd = lax.rsqrt(var + LN_EPS)
    xh = xc * rstd
    return xh * g + b, xh, rstd


def _ln_bwd_rows(dy, xh, rstd, g):
    dxh = dy * g
    m1 = jnp.mean(dxh, axis=-1, keepdims=True)
    m2 = jnp.mean(dxh * xh, axis=-1, keepdims=True)
    return rstd * (dxh - m1 - xh * m2)


def _sigmoid(v):
    return 1.0 / (1.0 + jnp.exp(-v))


def _gelu_parts(p):
    cdf = 0.5 * (1.0 + lax.erf(p * 0.7071067811865476))
    pdf = jnp.exp(-0.5 * p * p) * 0.3989422804014327
    return p * cdf, cdf + p * pdf


def _shift_down(prev8, t, s):
    ext = jnp.concatenate([prev8, t], axis=0)
    return pltpu.roll(ext, s, 0)[SUBLANES:]


def _shift_up(t, next8, s):
    n = t.shape[0]
    ext = jnp.concatenate([t, next8], axis=0)
    return pltpu.roll(ext, n + SUBLANES - s, 0)[:n]


def _mm(a, b, *, ta=False, tb=False, bl=None, bias=None, res=None, res_scale=1.0, out_dtype=F32,
        tm, tn, tk, name, pieces=None, deps=None, n_outer=False):
    M, K = (a.shape[1], a.shape[0]) if ta else a.shape
    bs = b.shape[1:] if bl is not None else b.shape
    N, Kb = (bs[0], bs[1]) if tb else (bs[1], bs[0])
    assert K == Kb and M % tm == 0 and N % tn == 0 and K % tk == 0, (a.shape, b.shape, tm, tn, tk)
    gm, gn, gk = M // tm, N // tn, K // tk

    def spec(block, imap):
        if n_outer:
            return pl.BlockSpec(block, lambda j, i, k: imap(i, j, k))
        return pl.BlockSpec(block, imap)

    a_spec = spec((tk, tm), lambda i, j, k: (k, i)) if ta else spec((tm, tk), lambda i, j, k: (i, k))
    bblk = (tn, tk) if tb else (tk, tn)
    bmap = (lambda i, j, k: (j, k)) if tb else (lambda i, j, k: (k, j))
    if bl is not None:
        b_spec = spec((None,) + bblk, lambda i, j, k: (bl,) + bmap(i, j, k))
    else:
        b_spec = spec(bblk, bmap)
    in_specs, operands = [a_spec, b_spec], [a, b]
    if bias is not None:
        in_specs.append(spec((1, tn), lambda i, j, k: (0, j)))
        operands.append(bias)
    if res is not None:
        in_specs.append(spec((tm, tn), lambda i, j, k: (i, j)))
        operands.append(res)
    n_dep = len(deps) if deps else 0
    if n_dep:
        in_specs += [ANY] * n_dep
        operands += deps
        del deps[:]
    if pieces is None:
        out_shape = jax.ShapeDtypeStruct((M, N), out_dtype)
        out_spec = spec((tm, tn), lambda i, j, k: (i, j))
        ppb = pr = None
    elif pieces[0] == 'col':
        pr, pc = M // 2, N // N_CHIPS
        assert tm % pr == 0 and pc % tn == 0
        ppb, per = tm // pr, pc // tn
        perm = pieces[1]
        out_shape = jax.ShapeDtypeStruct((N_DEV, pr, pc), out_dtype)
        out_spec = spec(
            (ppb, pr, tn),
            lambda i, j, k: ((2 * (_perm_idx(j // per) if perm else j // per)) // ppb + i, 0, j % per))
    else:
        pr = M // N_DEV
        assert tm % pr == 0
        ppb = tm // pr
        out_shape = jax.ShapeDtypeStruct((N_DEV, pr, N), out_dtype)
        out_spec = spec((ppb, pr, tn), lambda i, j, k: (i, 0, j))
    dims = (((0 if ta else 1,), (1 if tb else 0,)), ((), ()))

    def body(*refs):
        a_ref, b_ref = refs[0], refs[1]
        pos = 2
        bias_ref = res_ref = None
        if bias is not None:
            bias_ref = refs[pos]
            pos += 1
        if res is not None:
            res_ref = refs[pos]
            pos += 1
        pos += n_dep
        o_ref = refs[pos]

        def finish(r):
            if bias_ref is not None:
                r = r + bias_ref[...]
            if res_ref is not None:
                r = r + res_scale * res_ref[...]
            if pieces is not None:
                r = r.reshape(ppb, pr, tn)
            o_ref[...] = r.astype(out_dtype)

        part = lax.dot_general(a_ref[...].astype(_MXU), b_ref[...].astype(_MXU), dims, preferred_element_type=F32)
        if gk == 1:
            finish(part)
            return
        acc_ref = refs[pos + 1]
        k = pl.program_id(2)

        @pl.when(k == 0)
        def _():
            acc_ref[...] = part

        @pl.when((k > 0) & (k < gk - 1))
        def _():
            acc_ref[...] += part

        @pl.when(k == gk - 1)
        def _():
            finish(acc_ref[...] + part)

    return pl.pallas_call(
        body, name=name, grid=(gn, gm, gk) if n_outer else (gm, gn, gk), in_specs=in_specs, out_specs=out_spec,
        out_shape=out_shape, scratch_shapes=[pltpu.VMEM((tm, tn), F32)] if gk > 1 else [],
        compiler_params=_cp(("parallel", "parallel", "arbitrary")),
    )(*operands)


def _mm_ln_bwd(a, w, res, res_scale, xh, rstd, g, *, name, deps=None):
    T, K = a.shape
    D = w.shape[1]
    tm = _tile(T, 512)
    n_dep = len(deps) if deps else 0

    def body(a_ref, w_ref, res_ref, xh_ref, rs_ref, g_ref, *rest):
        dz_ref, dzb_ref, dg_ref, db_ref, cs_ref = rest[n_dep:]

        @pl.when(pl.program_id(0) == 0)
        def _():
            dg_ref[...] = jnp.zeros_like(dg_ref)
            db_ref[...] = jnp.zeros_like(db_ref)
            cs_ref[...] = jnp.zeros_like(cs_ref)

        d = lax.dot_general(a_ref[...].astype(_MXU), w_ref[...].astype(_MXU), (((1,), (1,)), ((), ())),
                            preferred_element_type=F32) + res_scale * res_ref[...]
        xh = xh_ref[...]
        dz = _ln_bwd_rows(d, xh, rs_ref[...], g_ref[...])
        dz_ref[...] = dz
        dzb_ref[...] = dz.astype(_MXU)
        dg_ref[...] += _fold8(d * xh)
        db_ref[...] += _fold8(d)
        cs_ref[...] += _fold8(dz)

    row = lambda i: (i, 0)
    fixed = lambda i: (0, 0)
    tile = pl.BlockSpec((tm, D), row)
    part = pl.BlockSpec((SUBLANES, D), fixed)
    operands = [a, w, res, xh, rstd, g] + (list(deps) if deps else [])
    if deps:
        del deps[:]
    return pl.pallas_call(
        body, name=name, grid=(T // tm,),
        in_specs=[pl.BlockSpec((tm, K), row),
                  pl.BlockSpec((None, D, K), lambda i: (0, 0, 0), pipeline_mode=pl.Buffered(1)),
                  tile, tile, pl.BlockSpec((tm, 1), row), pl.BlockSpec((1, D), fixed)] + [ANY] * n_dep,
        out_specs=[tile, tile, part, part, part],
        out_shape=[jax.ShapeDtypeStruct((T, D), F32), jax.ShapeDtypeStruct((T, D), _MXU)]
        + [jax.ShapeDtypeStruct((SUBLANES, D), F32)] * 3,
        compiler_params=_cp(("arbitrary",)),
    )(*operands)


def _mm_res_ln(a, w, wl, bias, res, alpha, g, b, *, name):
    T, K = a.shape
    D = w.shape[-1]
    tm = _tile(T, 256)

    def body(a_ref, w_ref, bias_ref, res_ref, g_ref, b_ref, y_ref, yb_ref, xh_ref, rs_ref):
        z = jnp.dot(a_ref[...].astype(_MXU), w_ref[...].astype(_MXU), preferred_element_type=F32)
        z = z + bias_ref[...] + alpha * res_ref[...]
        y, xh, rstd = _ln_rows(z, g_ref[...], b_ref[...])
        y_ref[...] = y
        yb_ref[...] = y.astype(_MXU)
        xh_ref[...] = xh
        rs_ref[...] = rstd

    row = lambda i: (i, 0)
    vec = pl.BlockSpec((1, D), lambda i: (0, 0))
    return pl.pallas_call(
        body, name=name, grid=(T // tm,),
        in_specs=[pl.BlockSpec((tm, K), row), pl.BlockSpec((None, K, D), lambda i: (wl, 0, 0)), vec,
                  pl.BlockSpec((tm, D), row), vec, vec],
        out_specs=[pl.BlockSpec((tm, D), row), pl.BlockSpec((tm, D), row), pl.BlockSpec((tm, D), row),
                   pl.BlockSpec((tm, 1), row)],
        out_shape=[jax.ShapeDtypeStruct((T, D), F32), jax.ShapeDtypeStruct((T, D), _MXU),
                   jax.ShapeDtypeStruct((T, D), F32), jax.ShapeDtypeStruct((T, 1), F32)],
        compiler_params=_cp(("parallel",)),
    )(a, w, bias, res, g, b)


def _ln_bwd(dy, xh, rstd, g, *, name, target=None):
    T, D = dy.shape
    tm = _tile(T, 256)
    head = target is not None

    def body(*refs):
        if head:
            dy_ref, t_ref, xh_ref, rs_ref, g_ref, dz_ref, dzb_ref, dg_ref, db_ref, cs_ref, ls_ref = refs
        else:
            dy_ref, xh_ref, rs_ref, g_ref, dz_ref, dzb_ref, dg_ref, db_ref, cs_ref = refs
        i = pl.program_id(0)

        @pl.when(i == 0)
        def _():
            dg_ref[...] = jnp.zeros_like(dg_ref)
            db_ref[...] = jnp.zeros_like(db_ref)
            cs_ref[...] = jnp.zeros_like(cs_ref)
            if head:
                ls_ref[...] = jnp.zeros_like(ls_ref)

        d = dy_ref[...]
        if head:
            err = d - t_ref[...]
            ls_ref[...] += _fold8(err * err)
            d = err * (1.0 / D)
        xh = xh_ref[...]
        dz = _ln_bwd_rows(d, xh, rs_ref[...], g_ref[...])
        dz_ref[...] = dz
        dzb_ref[...] = dz.astype(_MXU)
        dg_ref[...] += _fold8(d * xh)
        db_ref[...] += _fold8(d)
        cs_ref[...] += _fold8(dz)

    row = lambda i: (i, 0)
    fixed = lambda i: (0, 0)
    tile = pl.BlockSpec((tm, D), row)
    part = pl.BlockSpec((SUBLANES, D), fixed)
    in_specs = [tile] + ([tile] if head else []) + [tile, pl.BlockSpec((tm, 1), row), pl.BlockSpec((1, D), fixed)]
    n_part = 4 if head else 3
    operands = [dy] + ([target] if head else []) + [xh, rstd, g]
    return pl.pallas_call(
        body, name=name, grid=(T // tm,), in_specs=in_specs,
        out_specs=[tile, tile] + [part] * n_part,
        out_shape=[jax.ShapeDtypeStruct((T, D), F32), jax.ShapeDtypeStruct((T, D), _MXU)]
        + [jax.ShapeDtypeStruct((SUBLANES, D), F32)] * n_part,
        compiler_params=_cp(("arbitrary",)),
    )(*operands)


def _conv_cols(C, tc):
    per = (C // 2) // tc
    return per, (lambda j: (j // per) * (2 * per) + j % per)


def _glu_shifted(a_ref, g_ref, p_ref, S):
    u = a_ref[...].astype(F32) * _sigmoid(g_ref[...].astype(F32))
    rows = lax.broadcasted_iota(jnp.int32, u.shape, 0)
    for r in range(SUBLANES):
        p_ref[r, 0:CONV_TAPS_PAD, :] = jnp.zeros((CONV_TAPS_PAD, u.shape[1]), F32)
        p_ref[r, CONV_TAPS_PAD:CONV_TAPS_PAD + S, :] = u if r == 0 else jnp.where(rows >= r, pltpu.roll(u, r, 0), 0.0)


def _conv_fwd(h1, w_dw, b_dw, *, B, S, name):
    C = w_dw.shape[1]
    taps = CONV_TAPS_PAD - 1
    tc = LANES
    ch = _tile(S, 128)
    per, col_a = _conv_cols(C, tc)

    def body(a_ref, g_ref, w_ref, b_ref, o_ref, p_ref):
        _glu_shifted(a_ref, g_ref, p_ref, S)

        def chunk(ci, carry):
            base = pl.multiple_of(ci * ch, ch)
            acc = jnp.zeros((ch, tc), F32) + b_ref[...]
            for k in range(taps):
                q, r = divmod(taps - 1 - k, SUBLANES)
                start = pl.multiple_of(base + (CONV_TAPS_PAD - SUBLANES * q), SUBLANES)
                acc = acc + w_ref[pl.ds(k, 1), :] * p_ref[r, pl.ds(start, ch), :]
            o_ref[pl.ds(base, ch), :] = acc
            return carry

        lax.fori_loop(0, S // ch, chunk, 0)

    return pl.pallas_call(
        body, name=name, grid=(B, C // tc),
        in_specs=[pl.BlockSpec((S, tc), lambda b, j: (b, col_a(j))),
                  pl.BlockSpec((S, tc), lambda b, j: (b, col_a(j) + per)),
                  pl.BlockSpec((CONV_TAPS_PAD, tc), lambda b, j: (0, j)),
                  pl.BlockSpec((1, tc), lambda b, j: (0, j))],
        out_specs=pl.BlockSpec((S, tc), lambda b, j: (b, j)),
        out_shape=jax.ShapeDtypeStruct((B * S, C), F32),
        scratch_shapes=[pltpu.VMEM((SUBLANES, S + CONV_TAPS_PAD, tc), F32)],
        compiler_params=_cp(("parallel", "parallel")),
    )(h1, h1, w_dw, b_dw)


def _conv_bwd(dd, h1, w_dw, *, B, S, name):
    C = w_dw.shape[1]
    taps = CONV_TAPS_PAD - 1
    tc = LANES
    ch = _tile(S, 128)
    per, col_a = _conv_cols(C, tc)

    def body(d_ref, a_ref, g_ref, w_ref, du_ref, dw_ref, db_ref, p_ref, q_ref):
        b = pl.program_id(1)

        @pl.when(b == 0)
        def _():
            dw_ref[...] = jnp.zeros_like(dw_ref)
            db_ref[...] = jnp.zeros_like(db_ref)

        _glu_shifted(a_ref, g_ref, p_ref, S)
        d = d_ref[...]
        rows = lax.broadcasted_iota(jnp.int32, d.shape, 0)
        for r in range(SUBLANES):
            q_ref[r, S:S + CONV_TAPS_PAD, :] = jnp.zeros((CONV_TAPS_PAD, tc), F32)
            q_ref[r, 0:S, :] = d if r == 0 else jnp.where(rows < S - r, pltpu.roll(d, S - r, 0), 0.0)
        db_ref[...] += _fold8(d)

        def chunk(ci, carry):
            base = pl.multiple_of(ci * ch, ch)
            dch = d_ref[pl.ds(base, ch), :]
            acc = jnp.zeros((ch, tc), F32)
            for k in range(taps):
                q, r = divmod(taps - 1 - k, SUBLANES)
                up = pl.multiple_of(base + SUBLANES * q, SUBLANES)
                acc = acc + w_ref[pl.ds(k, 1), :] * q_ref[r, pl.ds(up, ch), :]
                down = pl.multiple_of(base + (CONV_TAPS_PAD - SUBLANES * q), SUBLANES)
                dw_ref[k] += _fold8(dch * p_ref[r, pl.ds(down, ch), :])
            du_ref[pl.ds(base, ch), :] = acc
            return carry

        lax.fori_loop(0, S // ch, chunk, 0)

    return pl.pallas_call(
        body, name=name, grid=(C // tc, B),
        in_specs=[pl.BlockSpec((S, tc), lambda j, b: (b, j)),
                  pl.BlockSpec((S, tc), lambda j, b: (b, col_a(j))),
                  pl.BlockSpec((S, tc), lambda j, b: (b, col_a(j) + per)),
                  pl.BlockSpec((CONV_TAPS_PAD, tc), lambda j, b: (0, j))],
        out_specs=[pl.BlockSpec((S, tc), lambda j, b: (b, j)),
                   pl.BlockSpec((CONV_TAPS_PAD, SUBLANES, tc), lambda j, b: (0, 0, j)),
                   pl.BlockSpec((SUBLANES, tc), lambda j, b: (0, j))],
        out_shape=[jax.ShapeDtypeStruct((B * S, C), F32),
                   jax.ShapeDtypeStruct((CONV_TAPS_PAD, SUBLANES, C), F32),
                   jax.ShapeDtypeStruct((SUBLANES, C), F32)],
        scratch_shapes=[pltpu.VMEM((SUBLANES, S + CONV_TAPS_PAD, tc), F32),
                        pltpu.VMEM((SUBLANES, S + CONV_TAPS_PAD, tc), F32)],
        compiler_params=_cp(("parallel", "arbitrary")),
    )(dd, h1, h1, w_dw)


def _ln_silu_fwd(v, g, b, *, name):
    T, C = v.shape
    tm = _tile(T, 512)

    def body(v_ref, g_ref, b_ref, s_ref, xh_ref, rs_ref):
        y, xh, rstd = _ln_rows(v_ref[...], g_ref[...], b_ref[...])
        s_ref[...] = (y * _sigmoid(y)).astype(_MXU)
        xh_ref[...] = xh
        rs_ref[...] = rstd

    row = lambda i: (i, 0)
    vec = pl.BlockSpec((1, C), lambda i: (0, 0))
    return pl.pallas_call(
        body, name=name, grid=(T // tm,),
        in_specs=[pl.BlockSpec((tm, C), row), vec, vec],
        out_specs=[pl.BlockSpec((tm, C), row), pl.BlockSpec((tm, C), row), pl.BlockSpec((tm, 1), row)],
        out_shape=[jax.ShapeDtypeStruct((T, C), _MXU), jax.ShapeDtypeStruct((T, C), F32),
                   jax.ShapeDtypeStruct((T, 1), F32)],
        compiler_params=_cp(("parallel",)),
    )(v, g, b)


def _ln_silu_bwd(dzb, w, xh, rstd, g, b, *, name):
    T, D = dzb.shape
    C = w.shape[1]
    tm = _tile(T, 512)

    def body(dz_ref, w_ref, xh_ref, rs_ref, g_ref, b_ref, dv_ref, dg_ref, db_ref):
        @pl.when(pl.program_id(0) == 0)
        def _():
            dg_ref[...] = jnp.zeros_like(dg_ref)
            db_ref[...] = jnp.zeros_like(db_ref)

        ds = lax.dot_general(dz_ref[...].astype(_MXU), w_ref[...].astype(_MXU), (((1,), (1,)), ((), ())),
                             preferred_element_type=F32)
        xh = xh_ref[...]
        gam = g_ref[...]
        y = xh * gam + b_ref[...]
        sig = _sigmoid(y)
        dln = ds * (sig * (1.0 + y * (1.0 - sig)))
        dv_ref[...] = _ln_bwd_rows(dln, xh, rs_ref[...], gam)
        dg_ref[...] += _fold8(dln * xh)
        db_ref[...] += _fold8(dln)

    row = lambda i: (i, 0)
    fixed = lambda i: (0, 0)
    vec = pl.BlockSpec((1, C), fixed)
    part = pl.BlockSpec((SUBLANES, C), fixed)
    return pl.pallas_call(
        body, name=name, grid=(T // tm,),
        in_specs=[pl.BlockSpec((tm, D), row), _resident((None, C, D), lambda i: (0, 0, 0)),
                  pl.BlockSpec((tm, C), row), pl.BlockSpec((tm, 1), row), vec, vec],
        out_specs=[pl.BlockSpec((tm, C), row), part, part],
        out_shape=[jax.ShapeDtypeStruct((T, C), F32)] + [jax.ShapeDtypeStruct((SUBLANES, C), F32)] * 2,
        compiler_params=_cp(("arbitrary",)),
    )(dzb, w, xh, rstd, g, b)


def _glu_bwd(du, h1, *, name):
    T, C = du.shape
    il = C // 2
    tm = _tile(T, 256)

    def body(du_ref, h_ref, dh_ref, cs_ref):
        @pl.when(pl.program_id(0) == 0)
        def _():
            cs_ref[...] = jnp.zeros_like(cs_ref)

        for hb in range(2):
            a = h_ref[:, 2 * hb * il:(2 * hb + 1) * il].astype(F32)
            gate = h_ref[:, (2 * hb + 1) * il:(2 * hb + 2) * il].astype(F32)
            d = du_ref[:, hb * il:(hb + 1) * il]
            sig = _sigmoid(gate)
            da = d * sig
            dgate = d * a * sig * (1.0 - sig)
            dh_ref[:, 2 * hb * il:(2 * hb + 1) * il] = da.astype(_MXU)
            dh_ref[:, (2 * hb + 1) * il:(2 * hb + 2) * il] = dgate.astype(_MXU)
            cs_ref[:, 2 * hb * il:(2 * hb + 1) * il] += _fold8(da)
            cs_ref[:, (2 * hb + 1) * il:(2 * hb + 2) * il] += _fold8(dgate)

    row = lambda i: (i, 0)
    return pl.pallas_call(
        body, name=name, grid=(T // tm,),
        in_specs=[pl.BlockSpec((tm, C), row), pl.BlockSpec((tm, 2 * C), row)],
        out_specs=[pl.BlockSpec((tm, 2 * C), row), pl.BlockSpec((SUBLANES, 2 * C), lambda i: (0, 0))],
        out_shape=[jax.ShapeDtypeStruct((T, 2 * C), _MXU), jax.ShapeDtypeStruct((SUBLANES, 2 * C), F32)],
        compiler_params=_cp(("arbitrary",)),
    )(du, h1)


def _tril_mask(n):
    return lax.broadcasted_iota(jnp.int32, (n, n), 0) >= lax.broadcasted_iota(jnp.int32, (n, n), 1)


def _split_uv(t, il):
    u = jnp.concatenate([t[:, 0:il], t[:, 2 * il:3 * il]], axis=1)
    v = jnp.concatenate([t[:, il:2 * il], t[:, 3 * il:4 * il]], axis=1)
    return u, v


def _gmlp_gate_fwd(p, g, b, w_s, bsb, *, name):
    T, C2 = p.shape
    C = C2 // 2
    il = C // 2
    G, L, _ = w_s.shape
    assert G * L == C
    tm = _tile(T, 2 * L, L)

    def body(p_ref, g_ref, b_ref, ws_ref, bs_ref, us_ref, xh_ref, rs_ref, vn_ref, u_ref):
        z, _ = _gelu_parts(p_ref[...].astype(F32))
        u, v = _split_uv(z, il)
        vn, xh, rstd = _ln_rows(v, g_ref[...], b_ref[...])
        xh_ref[...] = xh
        rs_ref[...] = rstd
        vn_ref[...] = vn.astype(_MXU)
        u_ref[...] = u
        mask = _tril_mask(L)
        for gi in range(G):
            wc = jnp.where(mask, ws_ref[gi], 0.0).astype(_MXU)
            cols = slice(gi * L, (gi + 1) * L)
            for c in range(tm // L):
                rows = slice(c * L, (c + 1) * L)
                s = jnp.dot(wc, vn_ref[rows, cols], preferred_element_type=F32) + bs_ref[:, cols]
                us_ref[rows, cols] = (u_ref[rows, cols] * s).astype(_MXU)

    row = lambda i: (i, 0)
    fixed = lambda i: (0, 0)
    return pl.pallas_call(
        body, name=name, grid=(T // tm,),
        in_specs=[pl.BlockSpec((tm, C2), row), pl.BlockSpec((1, C), fixed), pl.BlockSpec((1, C), fixed),
                  pl.BlockSpec((G, L, L), lambda i: (0, 0, 0)), pl.BlockSpec((L, C), fixed)],
        out_specs=[pl.BlockSpec((tm, C), row), pl.BlockSpec((tm, C), row), pl.BlockSpec((tm, 1), row)],
        out_shape=[jax.ShapeDtypeStruct((T, C), _MXU), jax.ShapeDtypeStruct((T, C), F32),
                   jax.ShapeDtypeStruct((T, 1), F32)],
        scratch_shapes=[pltpu.VMEM((tm, C), _MXU), pltpu.VMEM((tm, C), F32)],
        compiler_params=_cp(("parallel",)),
    )(p, g, b, w_s, bsb)


def _gmlp_gate_bwd(dzb, w_out, p, xh, rstd, g, b, w_s, bsb, *, name):
    T, C2 = p.shape
    D = dzb.shape[1]
    C = C2 // 2
    il = C // 2
    G, L, _ = w_s.shape
    tm = _tile(T, 2 * L, L)

    def body(dz_ref, wo_ref, p_ref, xh_ref, rs_ref, g_ref, b_ref, ws_ref, bs_ref,
             dp_ref, dg_ref, db_ref, cs_ref, dws_ref, dbs_ref, vn_ref, u_ref, dvn_ref, du_ref, dus_ref):
        @pl.when(pl.program_id(0) == 0)
        def _():
            dg_ref[...] = jnp.zeros_like(dg_ref)
            db_ref[...] = jnp.zeros_like(db_ref)
            cs_ref[...] = jnp.zeros_like(cs_ref)
            dws_ref[...] = jnp.zeros_like(dws_ref)
            dbs_ref[...] = jnp.zeros_like(dbs_ref)

        dus_ref[...] = lax.dot_general(dz_ref[...].astype(_MXU), wo_ref[...].astype(_MXU), (((1,), (1,)), ((), ())),
                                       preferred_element_type=F32)
        z, gp = _gelu_parts(p_ref[...].astype(F32))
        u, _ = _split_uv(z, il)
        xh = xh_ref[...]
        gam = g_ref[...]
        vn_ref[...] = (xh * gam + b_ref[...]).astype(_MXU)
        u_ref[...] = u
        mask = _tril_mask(L)
        for gi in range(G):
            wc = jnp.where(mask, ws_ref[gi], 0.0).astype(_MXU)
            cols = slice(gi * L, (gi + 1) * L)
            for c in range(tm // L):
                rows = slice(c * L, (c + 1) * L)
                vnb = vn_ref[rows, cols]
                s = jnp.dot(wc, vnb, preferred_element_type=F32) + bs_ref[:, cols]
                d = dus_ref[rows, cols]
                du_ref[rows, cols] = d * s
                ds = d * u_ref[rows, cols]
                dbs_ref[:, cols] += ds
                dsb = ds.astype(_MXU)
                dw = lax.dot_general(dsb, vnb, (((1,), (1,)), ((), ())), preferred_element_type=F32)
                dws_ref[gi] += jnp.where(mask, dw, 0.0)
                dvn_ref[rows, cols] = lax.dot_general(wc, dsb, (((0,), (0,)), ((), ())), preferred_element_type=F32)
        dvn = dvn_ref[...]
        dg_ref[...] += _fold8(dvn * xh)
        db_ref[...] += _fold8(dvn)
        dv = _ln_bwd_rows(dvn, xh, rs_ref[...], gam)
        du = du_ref[...]
        for hb in range(2):
            for part, src in ((0, du), (1, dv)):
                lo = (2 * hb + part) * il
                dp = src[:, hb * il:(hb + 1) * il] * gp[:, lo:lo + il]
                dp_ref[:, lo:lo + il] = dp.astype(_MXU)
                cs_ref[:, lo:lo + il] += _fold8(dp)

    row = lambda i: (i, 0)
    fixed = lambda i: (0, 0)
    part_c = pl.BlockSpec((SUBLANES, C), fixed)
    return pl.pallas_call(
        body, name=name, grid=(T // tm,),
        in_specs=[pl.BlockSpec((tm, D), row), _resident((None, C, D), lambda i: (0, 0, 0)),
                  pl.BlockSpec((tm, C2), row), pl.BlockSpec((tm, C), row),
                  pl.BlockSpec((tm, 1), row), pl.BlockSpec((1, C), fixed), pl.BlockSpec((1, C), fixed),
                  pl.BlockSpec((G, L, L), lambda i: (0, 0, 0)), pl.BlockSpec((L, C), fixed)],
        out_specs=[pl.BlockSpec((tm, C2), row), part_c, part_c, pl.BlockSpec((SUBLANES, C2), fixed),
                   pl.BlockSpec((G, L, L), lambda i: (0, 0, 0)), pl.BlockSpec((L, C), fixed)],
        out_shape=[jax.ShapeDtypeStruct((T, C2), _MXU), jax.ShapeDtypeStruct((SUBLANES, C), F32),
                   jax.ShapeDtypeStruct((SUBLANES, C), F32), jax.ShapeDtypeStruct((SUBLANES, C2), F32),
                   jax.ShapeDtypeStruct((G, L, L), F32), jax.ShapeDtypeStruct((L, C), F32)],
        scratch_shapes=[pltpu.VMEM((tm, C), _MXU), pltpu.VMEM((tm, C), F32), pltpu.VMEM((tm, C), F32),
                        pltpu.VMEM((tm, C), F32), pltpu.VMEM((tm, C), F32)],
        compiler_params=_cp(("arbitrary",)),
    )(dzb, w_out, p, xh, rstd, g, b, w_s, bsb)


def _ffn_conv(h, prev8, w_ref, b_ref):
    h1 = _shift_down(prev8, h, 1)
    h2 = _shift_down(prev8, h, 2)
    return w_ref[pl.ds(2, 1), :] * h + w_ref[pl.ds(1, 1), :] * h1 + w_ref[pl.ds(0, 1), :] * h2 + b_ref[...]


def _resident(block, imap):
    return pl.BlockSpec(block, imap, pipeline_mode=pl.Buffered(1))


def _ffn_fwd_half(j, xb, w_up, w_down, b_up, w_dw, b_dw, *, S, name, prev=None, tail=None):
    T, D = xb.shape
    N = w_up.shape[-1]
    tn = N // N_CHIPS
    tm = _tile(S, 256)
    spt = S // tm
    last = prev is not None
    alpha = tail[1] if last else None

    def body(*refs):
        x_ref, wu_ref, wd_ref, bu_ref, wc_ref, bc_ref = refs[:6]
        if last:
            yp_ref, res_ref, bd_ref, g_ref, b_ref = refs[9:14]
            h_ref, hc_ref, f_ref, y_ref, yb_ref, xh_ref, rs_ref, carry_ref = refs[14:22]
        else:
            h_ref, hc_ref, f_ref, yp_ref, carry_ref = refs[6:11]

        @pl.when(pl.program_id(0) % spt == 0)
        def _():
            carry_ref[...] = jnp.zeros_like(carry_ref)

        h = jnp.dot(x_ref[...].astype(_MXU), wu_ref[...].astype(_MXU), preferred_element_type=F32) + bu_ref[...]
        hq = h.astype(_HDT)
        h_ref[...] = hq
        h = hq.astype(F32)
        hc = _ffn_conv(h, carry_ref[...], wc_ref, bc_ref)
        hc_ref[...] = hc.astype(_HDT)
        carry_ref[...] = h[tm - SUBLANES:tm]
        gte = hc[:, :tn]
        f = (gte * _sigmoid(gte) * hc[:, tn:]).astype(_MXU)
        f_ref[...] = f
        y = jnp.dot(f, wd_ref[...].astype(_MXU), preferred_element_type=F32)
        if not last:
            yp_ref[...] = y
            return
        z = y + yp_ref[...] + bd_ref[...] + alpha * res_ref[...]
        out, xh, rstd = _ln_rows(z, g_ref[...], b_ref[...])
        y_ref[...] = out
        yb_ref[...] = out.astype(_MXU)
        xh_ref[...] = xh
        rs_ref[...] = rstd

    row = lambda i: (i, 0)
    pair = lambda i: (0, j)
    vec = pl.BlockSpec((1, D), lambda i: (0, 0))
    tile = pl.BlockSpec((tm, D), row)
    in_specs = [tile, _resident((None, D, 2 * tn), lambda i: (0, 0, j)), _resident((None, tn, D), lambda i: (0, j, 0)),
                pl.BlockSpec((1, 2 * tn), pair), pl.BlockSpec((SUBLANES, 2 * tn), pair), pl.BlockSpec((1, 2 * tn), pair)]
    operands = [xb, w_up, w_down, b_up, w_dw, b_dw]
    wide = pl.BlockSpec((tm, 2 * tn), lambda i: (i, j))
    out_specs = [wide, wide, pl.BlockSpec((tm, tn), lambda i: (i, j))]
    out_shape = [jax.ShapeDtypeStruct((T, N), _HDT), jax.ShapeDtypeStruct((T, N), _HDT),
                 jax.ShapeDtypeStruct((T, N // 2), _MXU)]
    aliases = {}
    if last:
        res, _, b_down, g, b = tail
        in_specs += [ANY, ANY, ANY, tile, tile, vec, vec, vec]
        operands += list(prev) + [res, b_down, g, b]
        aliases = {6: 0, 7: 1, 8: 2}
        out_specs += [tile, tile, tile, pl.BlockSpec((tm, 1), row)]
        out_shape += [jax.ShapeDtypeStruct((T, D), F32), jax.ShapeDtypeStruct((T, D), _MXU),
                      jax.ShapeDtypeStruct((T, D), F32), jax.ShapeDtypeStruct((T, 1), F32)]
    else:
        out_specs.append(tile)
        out_shape.append(jax.ShapeDtypeStruct((T, D), F32))
    return pl.pallas_call(
        body, name=name, grid=(T // tm,), in_specs=in_specs, out_specs=out_specs, out_shape=out_shape,
        input_output_aliases=aliases, scratch_shapes=[pltpu.VMEM((SUBLANES, 2 * tn), F32)],
        compiler_params=_cp(("arbitrary",)),
    )(*operands)


def _ffn_bwd_half(j, dzb, w_down, w_up, hs, hcs, w_dw, *, S, name, dz=None, alpha=None, prev=None, ln=None):
    T, D = dzb.shape
    N = hs.shape[1]
    tn = N // N_CHIPS
    tm = _tile(S, 256)
    spt = S // tm
    nt = T // tm
    last = prev is not None

    def body(*refs):
        dz_ref, wd_ref, wu_ref, h_ref, hc_ref, wc_ref = refs[:6]
        if last:
            dxp_ref, xh_ref, rs_ref, g_ref = refs[7:11]
            dh_ref, cs_ref, dw_ref, db_ref, dz1_ref, dz1b_ref, dg1_ref, db1_ref, cs1_ref, carry_ref = refs[11:21]
        else:
            dzf_ref = refs[6]
            dh_ref, cs_ref, dw_ref, db_ref, dxp_ref, carry_ref = refs[7:13]
        i = pl.program_id(0)
        ii = nt - 1 - i

        @pl.when(i == 0)
        def _():
            cs_ref[...] = jnp.zeros_like(cs_ref)
            dw_ref[...] = jnp.zeros_like(dw_ref)
            db_ref[...] = jnp.zeros_like(db_ref)
            if last:
                dg1_ref[...] = jnp.zeros_like(dg1_ref)
                db1_ref[...] = jnp.zeros_like(db1_ref)
                cs1_ref[...] = jnp.zeros_like(cs1_ref)

        df = lax.dot_general(dz_ref[...].astype(_MXU), wd_ref[...].astype(_MXU), (((1,), (1,)), ((), ())),
                             preferred_element_type=F32)
        h = h_ref[...].astype(F32)
        gte, val = hc_ref[:, :tn].astype(F32), hc_ref[:, tn:].astype(F32)
        sig = _sigmoid(gte)
        dval = df * (gte * sig)
        dg = df * val * (sig * (1.0 + gte * (1.0 - sig)))
        dhc = jnp.concatenate([dg, dval], axis=1)
        nxt = jnp.where((ii + 1) % spt == 0, 0.0, carry_ref[...])
        d1 = _shift_up(dhc, nxt, 1)
        d2 = _shift_up(dhc, nxt, 2)
        carry_ref[...] = dhc[0:SUBLANES]
        db_ref[...] += _fold8(dhc)
        dw_ref[2] += _fold8(dhc * h)
        dw_ref[1] += _fold8(d1 * h)
        dw_ref[0] += _fold8(d2 * h)
        dh = wc_ref[pl.ds(2, 1), :] * dhc + wc_ref[pl.ds(1, 1), :] * d1 + wc_ref[pl.ds(0, 1), :] * d2
        cs_ref[...] += _fold8(dh)
        dhb = dh.astype(_MXU)
        dh_ref[...] = dhb
        dx = lax.dot_general(dhb, wu_ref[...].astype(_MXU), (((1,), (1,)), ((), ())), preferred_element_type=F32)
        if not last:
            dxp_ref[...] = dx + alpha * dzf_ref[...]
            return
        d = dx + dxp_ref[...]
        xh = xh_ref[...]
        dz1 = _ln_bwd_rows(d, xh, rs_ref[...], g_ref[...])
        dz1_ref[...] = dz1
        dz1b_ref[...] = dz1.astype(_MXU)
        dg1_ref[...] += _fold8(d * xh)
        db1_ref[...] += _fold8(d)
        cs1_ref[...] += _fold8(dz1)

    rev = lambda i: (nt - 1 - i, 0)
    fixed = lambda i: (0, 0)
    pair = lambda i: (0, j)
    tile = pl.BlockSpec((tm, D), rev)
    wide = pl.BlockSpec((tm, 2 * tn), lambda i: (nt - 1 - i, j))
    part = pl.BlockSpec((SUBLANES, 2 * tn), fixed)
    in_specs = [tile, _resident((None, tn, D), lambda i: (0, j, 0)), _resident((None, D, 2 * tn), lambda i: (0, 0, j)),
                wide, wide, pl.BlockSpec((SUBLANES, 2 * tn), pair)]
    operands = [dzb, w_down, w_up, hs, hcs, w_dw]
    out_specs = [wide, part, pl.BlockSpec((3, SUBLANES, 2 * tn), lambda i: (0, 0, 0)), part]
    out_shape = [jax.ShapeDtypeStruct((T, N), _MXU), jax.ShapeDtypeStruct((SUBLANES, 2 * tn), F32),
                 jax.ShapeDtypeStruct((3, SUBLANES, 2 * tn), F32), jax.ShapeDtypeStruct((SUBLANES, 2 * tn), F32)]
    aliases = {}
    if last:
        xh, rstd, g = ln
        in_specs += [ANY, tile, tile, pl.BlockSpec((tm, 1), rev), pl.BlockSpec((1, D), fixed)]
        operands += [prev[0], prev[1], xh, rstd, g]
        aliases = {6: 0}
        out_specs += [tile, tile] + [pl.BlockSpec((SUBLANES, D), fixed)] * 3
        out_shape += [jax.ShapeDtypeStruct((T, D), F32), jax.ShapeDtypeStruct((T, D), _MXU)] \
            + [jax.ShapeDtypeStruct((SUBLANES, D), F32)] * 3
    else:
        in_specs.append(tile)
        operands.append(dz)
        out_specs.append(tile)
        out_shape.append(jax.ShapeDtypeStruct((T, D), F32))
    return pl.pallas_call(
        body, name=name, grid=(nt,), in_specs=in_specs, out_specs=out_specs, out_shape=out_shape,
        input_output_aliases=aliases, scratch_shapes=[pltpu.VMEM((SUBLANES, 2 * tn), F32)],
        compiler_params=_cp(("arbitrary",)),
    )(*operands)


def _sum_pieces(g, r, me, layer, acc, n_layers, *, name):
    _, pr, pc = g.shape
    tr = _tile(pr, 128)

    def body(me_ref, g_ref, r_ref, *rest):
        o_ref = rest[-1]
        total = g_ref[...].astype(F32)
        for s in range(N_DEV - 1):
            total = total + r_ref[s].astype(F32)
        o_ref[...] = total

    in_specs = [pl.BlockSpec((None, tr, pc), lambda i, me_ref: (me_ref[0], i, 0)),
                pl.BlockSpec((N_DEV - 1, tr, pc), lambda i, me_ref: (0, i, 0))]
    operands = [me, g, r]
    aliases = {}
    if acc is not None:
        in_specs.append(ANY)
        operands.append(acc)
        aliases = {3: 0}
    return pl.pallas_call(
        body, name=name,
        grid_spec=pltpu.PrefetchScalarGridSpec(
            num_scalar_prefetch=1, grid=(pr // tr,), in_specs=in_specs,
            out_specs=pl.BlockSpec((None, tr, pc), lambda i, me_ref: (layer, i, 0))),
        out_shape=jax.ShapeDtypeStruct((n_layers, pr, pc), F32),
        input_output_aliases=aliases,
        compiler_params=_cp(("parallel",)),
    )(*operands)


def _adam_math(w, g, m, v):
    bc1 = 1.0 - ADAM_B1 ** ADAM_STEP
    bc2 = 1.0 - ADAM_B2 ** ADAM_STEP
    m = ADAM_B1 * m + (1.0 - ADAM_B1) * g
    v = ADAM_B2 * v + (1.0 - ADAM_B2) * (g * g)
    return -ADAM_LR * ((m / bc1) / (jnp.sqrt(v / bc2) + ADAM_EPS) + ADAM_WD * w), m, v


def _adam(w, g, m, v, *, name):
    R, C = w.shape
    tr = _tile(R, 256)

    def body(w_ref, g_ref, m_ref, v_ref, d_ref, mo_ref, vo_ref):
        d_ref[...], mo_ref[...], vo_ref[...] = _adam_math(w_ref[...], g_ref[...], m_ref[...], v_ref[...])

    spec = pl.BlockSpec((tr, C), lambda i: (i, 0))
    return pl.pallas_call(
        body, name=name, grid=(R // tr,), in_specs=[spec] * 4, out_specs=[spec] * 3,
        out_shape=[jax.ShapeDtypeStruct((R, C), F32)] * 3,
        compiler_params=_cp(("parallel",)),
    )(w, g, m, v)


def _adam_halves(w, own, got, m, v, core, *, name):
    L, R, C = w.shape
    rh = R // 2
    tr = _tile(rh, 256)
    nt = rh // tr

    def body(c_ref, w_ref, own_ref, got_ref, m_ref, v_ref, g_ref, d_ref, mo_ref, vo_ref):
        g = jnp.where(pl.program_id(1) == c_ref[0], own_ref[...], got_ref[...])
        g_ref[...] = g
        d_ref[...], mo_ref[...], vo_ref[...] = _adam_math(w_ref[...], g, m_ref[...], v_ref[...])

    full = pl.BlockSpec((None, tr, C), lambda l, h, t, c_ref: (l, h * nt + t, 0))
    half = pl.BlockSpec((None, tr, C), lambda l, h, t, c_ref: (l, t, 0))
    return pl.pallas_call(
        body, name=name,
        grid_spec=pltpu.PrefetchScalarGridSpec(
            num_scalar_prefetch=1, grid=(L, 2, nt), in_specs=[full, half, half, full, full], out_specs=[full] * 4),
        out_shape=[jax.ShapeDtypeStruct((L, R, C), F32)] * 4,
        compiler_params=_cp(("parallel", "parallel", "parallel")),
    )(core, w, own, got, m, v)


def _remote(src, dst, send, recv, dev):
    return pltpu.make_async_remote_copy(src_ref=src, dst_ref=dst, send_sem=send, recv_sem=recv,
                                        device_id=dev, device_id_type=MESH)


def _place_w(shard, pos, layer, *, axis, name):
    _, R, C = shard.shape
    tr = _tile(R, 512, 16)
    nt = R // tr
    if axis == 2:
        out_shape = (1, R, N_CHIPS * C)
        out_map = lambda t, q: (0, t, q[0])
    else:
        out_shape = (1, N_CHIPS * R, C)
        out_map = lambda t, q: (0, q[0] * nt + t, 0)

    def body(q_ref, s_ref, o_ref):
        o_ref[...] = s_ref[...].astype(_WIRE)

    return pl.pallas_call(
        body, name=name,
        grid_spec=pltpu.PrefetchScalarGridSpec(
            num_scalar_prefetch=1, grid=(nt,),
            in_specs=[pl.BlockSpec((None, tr, C), lambda t, q: (layer, t, 0))],
            out_specs=pl.BlockSpec((None, tr, C), out_map)),
        out_shape=jax.ShapeDtypeStruct(out_shape, _WIRE),
        compiler_params=_cp(("parallel",)),
    )(pos, shard)


def _ag_window(ref, kind, px, py, h):
    axis, perm = kind
    q = 2 * px + py
    if perm:
        q = _perm_idx(q)
    if axis == 2:
        R, C = ref.shape[1], ref.shape[2] // N_CHIPS
        rh = R // 2
        return ref.at[:, pl.ds(pl.multiple_of(h * rh, 16), rh), pl.ds(pl.multiple_of(q * C, LANES), C)]
    R = ref.shape[1] // N_CHIPS
    rh = R // 2
    return ref.at[:, pl.ds(pl.multiple_of(q * R + h * rh, 16), rh), :]


def _ag_ici_copies(refs, kinds, send, recv):
    x, y, c = lax.axis_index("x"), lax.axis_index("y"), lax.axis_index("c")
    chips = [(1 - x, y), (x, 1 - y), (1 - x, 1 - y)]
    sends, recvs = [], []
    for a, (ref, kind) in enumerate(zip(refs, kinds)):
        own = _ag_window(ref, kind, x, y, c)
        for i, (px, py) in enumerate(chips):
            k = 3 * a + i
            sends.append(_remote(own, own, send.at[k], recv.at[k], (px, py, c)))
            recvs.append(_remote(own, _ag_window(ref, kind, px, py, c), send.at[k], recv.at[k], (px, py, c)))
    return sends, recvs


def _ag_start(arrs, kinds, after, *, name):
    n = len(arrs)

    def body(*refs):
        in_refs = refs[:n]
        send, recv = refs[n + len(after)], refs[n + len(after) + 1]
        token = refs[-1]
        sends, _ = _ag_ici_copies(in_refs, kinds, send, recv)
        for cp in sends:
            cp.start()
        token[...] = jnp.zeros_like(token)

    sems = pltpu.SemaphoreType.DMA((3 * n,))
    out = pl.pallas_call(
        body, name=name,
        out_shape=(sems, sems) + tuple(pltpu.HBM(a.shape, a.dtype) for a in arrs)
        + (jax.ShapeDtypeStruct((SUBLANES, LANES), F32),),
        in_specs=(HBM,) * n + (ANY,) * len(after),
        out_specs=(SEMS, SEMS) + (HBM,) * n + (pl.BlockSpec(memory_space=pltpu.VMEM),),
        input_output_aliases={a: 2 + a for a in range(n)},
        compiler_params=pltpu.CompilerParams(has_side_effects=EFFECT),
    )(*[pltpu.with_memory_space_constraint(a, pltpu.HBM) for a in arrs], *after)
    return out[0], out[1], list(out[2:2 + n]), out[-1]


def _ag_wait(send, recv, arrs, kinds, after, *, name):
    n = len(arrs)

    def body(*refs):
        in_refs = refs[:n]
        send, recv = refs[n], refs[n + 1]
        sends, recvs = _ag_ici_copies(in_refs, kinds, send, recv)
        for cp in sends:
            cp.wait_send()
        for cp in recvs:
            cp.wait_recv()

    out = pl.pallas_call(
        body, name=name,
        out_shape=tuple(pltpu.HBM(a.shape, a.dtype) for a in arrs),
        in_specs=(HBM,) * n + (SEMS, SEMS) + (ANY,) * len(after), out_specs=(HBM,) * n,
        input_output_aliases={a: a for a in range(n)},
        compiler_params=pltpu.CompilerParams(has_side_effects=EFFECT),
    )(*arrs, send, recv, *after)
    return list(out)


def _ag_forward(arrs, kinds, *, name):
    n = len(arrs)

    def body(*refs):
        o_refs, send, recv = refs[n:2 * n], refs[2 * n], refs[2 * n + 1]
        x, y, c = lax.axis_index("x"), lax.axis_index("y"), lax.axis_index("c")
        chips = [(1 - x, y), (x, 1 - y), (1 - x, 1 - y)]
        sib = (x, y, 1 - c)
        sends, recvs = [], []
        for a, (ref, kind) in enumerate(zip(o_refs, kinds)):
            for i, (px, py) in enumerate(chips):
                k = 3 * a + i
                got = _ag_window(ref, kind, px, py, c)
                cp = _remote(got, got, send.at[k], recv.at[k], sib)
                cp.start()
                sends.append(cp)
                recvs.append(_remote(got, _ag_window(ref, kind, px, py, 1 - c), send.at[k], recv.at[k], sib))
        for cp in recvs:
            cp.wait_recv()
        for cp in sends:
            cp.wait_send()

    out = pl.pallas_call(
        body, name=name, in_specs=[ANY] * n, out_specs=[ANY] * n,
        out_shape=[jax.ShapeDtypeStruct(a.shape, a.dtype) for a in arrs],
        input_output_aliases={a: a for a in range(n)},
        scratch_shapes=[pltpu.SemaphoreType.DMA((3 * n,)), pltpu.SemaphoreType.DMA((3 * n,))],
    )(*arrs)
    return list(out)


def _flip(x, y, c, f):
    return ((1 - x) if f & 4 else x, (1 - y) if f & 2 else y, (1 - c) if f & 1 else c)


def _rs_copies(g_ref, land_ref, send, recv):
    x, y, c = lax.axis_index("x"), lax.axis_index("y"), lax.axis_index("c")
    cps = []
    for f in range(1, N_DEV):
        tx, ty, tcx = _flip(x, y, c, f)
        cps.append(_remote(g_ref.at[4 * tx + 2 * ty + tcx], land_ref.at[f - 1], send.at[f - 1], recv.at[f - 1],
                           (tx, ty, tcx)))
    return cps


def _rs_start(g, *, name):
    _, pr, pc = g.shape
    land_shape = (N_DEV - 1, pr, pc)

    def body(g_ref, land_ref, send, recv, g_thru, land_thru, token):
        for cp in _rs_copies(g_ref, land_ref, send, recv):
            cp.start()
        token[...] = jnp.zeros_like(token)

    sems = pltpu.SemaphoreType.DMA((N_DEV - 1,))
    return pl.pallas_call(
        body, name=name,
        out_shape=(sems, sems, pltpu.HBM(g.shape, g.dtype), pltpu.HBM(land_shape, g.dtype),
                   jax.ShapeDtypeStruct((SUBLANES, LANES), F32)),
        in_specs=(HBM, HBM), out_specs=(SEMS, SEMS, HBM, HBM, pl.BlockSpec(memory_space=pltpu.VMEM)),
        input_output_aliases={0: 2, 1: 3},
        compiler_params=pltpu.CompilerParams(has_side_effects=EFFECT),
    )(pltpu.with_memory_space_constraint(g, pltpu.HBM),
      pltpu.with_memory_space_constraint(lax.empty(land_shape, g.dtype), pltpu.HBM))


def _rs_wait(send, recv, g_thru, land_thru, after, *, name):
    def body(g_ref, land_ref, send, recv, after_ref, g_out, land_out):
        cps = _rs_copies(g_ref, land_ref, send, recv)
        for cp in cps:
            cp.wait_send()
        for cp in cps:
            cp.wait_recv()

    return pl.pallas_call(
        body, name=name,
        out_shape=(pltpu.HBM(g_thru.shape, g_thru.dtype), pltpu.HBM(land_thru.shape, land_thru.dtype)),
        in_specs=(HBM, HBM, SEMS, SEMS, ANY), out_specs=(HBM, HBM), input_output_aliases={0: 0, 1: 1},
        compiler_params=pltpu.CompilerParams(has_side_effects=EFFECT),
    )(g_thru, land_thru, send, recv, after)


def _pair_exchange(own, *, name):
    def body(own_ref, got_ref, send, recv):
        x, y, c = lax.axis_index("x"), lax.axis_index("y"), lax.axis_index("c")
        cp = _remote(own_ref, got_ref, send, recv, (x, y, 1 - c))
        cp.start()
        cp.wait_recv()
        cp.wait_send()

    return pl.pallas_call(
        body, name=name, in_specs=[ANY], out_specs=ANY, out_shape=jax.ShapeDtypeStruct(own.shape, own.dtype),
        scratch_shapes=[pltpu.SemaphoreType.DMA, pltpu.SemaphoreType.DMA],
    )(own)


def _allreduce_flat(vec, *, name):
    n = vec.shape[0]
    unit = N_DEV * SUBLANES * LANES
    npad = -(-n // unit) * unit
    rows = npad // (N_DEV * LANES)
    xin = jnp.pad(vec, (0, npad - n)).reshape(N_DEV, rows, LANES)

    def body(x_ref, y_ref, a_ref, send_a, recv_a, send_b, recv_b):
        x, y, c = lax.axis_index("x"), lax.axis_index("y"), lax.axis_index("c")
        me = 4 * x + 2 * y + c
        a_ref[me] = x_ref[me]
        sends, recvs = [], []
        for f in range(1, N_DEV):
            dev = _flip(x, y, c, f)
            t = 4 * dev[0] + 2 * dev[1] + dev[2]
            cp = _remote(x_ref.at[t], a_ref.at[me], send_a.at[f - 1], recv_a.at[f - 1], dev)
            cp.start()
            sends.append(cp)
            recvs.append(_remote(x_ref.at[me], a_ref.at[t], send_a.at[f - 1], recv_a.at[f - 1], dev))
        for cp in recvs:
            cp.wait_recv()
        for cp in sends:
            cp.wait_send()
        acc = a_ref[0]
        for s in range(1, N_DEV):
            acc = acc + a_ref[s]
        y_ref[me] = acc
        sends, recvs = [], []
        for f in range(1, N_DEV):
            dev = _flip(x, y, c, f)
            t = 4 * dev[0] + 2 * dev[1] + dev[2]
            cp = _remote(y_ref.at[me], y_ref.at[me], send_b.at[f - 1], recv_b.at[f - 1], dev)
            cp.start()
            sends.append(cp)
            recvs.append(_remote(y_ref.at[me], y_ref.at[t], send_b.at[f - 1], recv_b.at[f - 1], dev))
        for cp in recvs:
            cp.wait_recv()
        for cp in sends:
            cp.wait_send()

    vm = pl.BlockSpec(memory_space=pltpu.VMEM)
    out = pl.pallas_call(
        body, name=name, in_specs=[vm], out_specs=vm,
        out_shape=jax.ShapeDtypeStruct((N_DEV, rows, LANES), F32),
        scratch_shapes=[pltpu.VMEM((N_DEV, rows, LANES), F32)] + [pltpu.SemaphoreType.DMA((N_DEV - 1,))] * 4,
        compiler_params=_cp(),
    )(xin)
    return out.reshape(npad)[:n]


def _perm_cols(v, blocks=N_CHIPS):
    lead, n = v.shape[:-1], v.shape[-1]
    return v.reshape(lead + (blocks, n // blocks))[..., PERM, :].reshape(lead + (n,))


def _pack(arrs):
    return jnp.concatenate([a.reshape(-1).astype(F32) for a in arrs])


def _unpack(flat, shapes):
    out, pos = [], 0
    for s in shapes:
        n = 1
        for d in s:
            n *= d
        out.append(flat[pos:pos + n].reshape(s))
        pos += n
    return out


def kernel(x, conv_w_in, conv_b_in, conv_w_dw, conv_b_dw, conv_ln_g, conv_ln_b, conv_w_out, conv_b_out, gmlp_w_in, gmlp_b_in, gmlp_ln_g, gmlp_ln_b, gmlp_w_s, gmlp_b_s, gmlp_w_out, gmlp_b_out, ffn_w_up, ffn_b_up, ffn_w_dw, ffn_b_dw, ffn_w_down, ffn_b_down, norm1_g, norm1_b, norm2_g, norm2_b, loss_target, m_conv_w_in, m_conv_b_in, m_conv_w_dw, m_conv_b_dw, m_conv_ln_g, m_conv_ln_b, m_conv_w_out, m_conv_b_out, m_gmlp_w_in, m_gmlp_b_in, m_gmlp_ln_g, m_gmlp_ln_b, m_gmlp_w_s, m_gmlp_b_s, m_gmlp_w_out, m_gmlp_b_out, m_ffn_w_up, m_ffn_b_up, m_ffn_w_dw, m_ffn_b_dw, m_ffn_w_down, m_ffn_b_down, m_norm1_g, m_norm1_b, m_norm2_g, m_norm2_b, v_conv_w_in, v_conv_b_in, v_conv_w_dw, v_conv_b_dw, v_conv_ln_g, v_conv_ln_b, v_conv_w_out, v_conv_b_out, v_gmlp_w_in, v_gmlp_b_in, v_gmlp_ln_g, v_gmlp_ln_b, v_gmlp_w_s, v_gmlp_b_s, v_gmlp_w_out, v_gmlp_b_out, v_ffn_w_up, v_ffn_b_up, v_ffn_w_dw, v_ffn_b_dw, v_ffn_w_down, v_ffn_b_down, v_norm1_g, v_norm1_b, v_norm2_g, v_norm2_b):
    P = dict(locals())
    WEIGHTS = ['conv_w_in', 'conv_b_in', 'conv_w_dw', 'conv_b_dw', 'conv_ln_g', 'conv_ln_b', 'conv_w_out',
               'conv_b_out', 'gmlp_w_in', 'gmlp_b_in', 'gmlp_ln_g', 'gmlp_ln_b', 'gmlp_w_s', 'gmlp_b_s',
               'gmlp_w_out', 'gmlp_b_out', 'ffn_w_up', 'ffn_b_up', 'ffn_w_dw', 'ffn_b_dw', 'ffn_w_down',
               'ffn_b_down', 'norm1_g', 'norm1_b', 'norm2_g', 'norm2_b']
    BIG = ['conv_w_in', 'conv_w_out', 'gmlp_w_in', 'gmlp_w_out', 'ffn_w_up', 'ffn_w_down']
    SMALL_SHARDED = {'conv_w_dw': 2, 'gmlp_b_in': 1, 'gmlp_ln_g': 1, 'gmlp_ln_b': 1, 'gmlp_b_out': 1, 'ffn_w_dw': 2}

    B, S, D = x.shape
    T = B * S
    depth = norm1_g.shape[0]
    alpha = (2.0 * depth) ** 0.25
    C = conv_w_out.shape[-1]
    F2 = ffn_b_up.shape[-1]
    G, L = gmlp_w_s.shape[1], gmlp_w_s.shape[2]
    xi, yi, ci = lax.axis_index("x"), lax.axis_index("y"), lax.axis_index("c")
    shard = 2 * xi + yi

    i32 = lambda v: jnp.reshape(v, (1,)).astype(jnp.int32)
    pos_plain, pos_perm = i32(shard), i32(_perm_idx(shard))
    me_id, core_id = i32(4 * xi + 2 * yi + ci), i32(ci)

    groups = []
    for i in range(depth):
        mix = 'conv' if i % 2 == 0 else 'gmlp'
        groups.append((f"{mix}{i // 2}", [(mix + '_w_in', i // 2, 2, True), (mix + '_w_out', i // 2, 1, False)]))
        groups.append((f"ffn{i}", [('ffn_w_up', i, 2, True), ('ffn_w_down', i, 1, False)]))
    sm_names = list(SMALL_SHARDED)
    sm_shapes = [P[n].shape for n in sm_names]
    mine = _pack([P[n] for n in sm_names]) * (ci == 0).astype(F32)
    buf = jnp.zeros((N_CHIPS, mine.shape[0]), F32)
    buf = lax.dynamic_update_slice(buf, mine[None], (shard, 0))
    gathered = _allreduce_flat(buf.reshape(-1), name="ag_small").reshape(N_CHIPS, -1)

    started, order = {}, [gathered]
    for gname, members in groups:
        placed = [_place_w(P[n], pos_perm if perm else pos_plain, l, axis=axis, name=f"place_{n}_{l}")
                  for n, l, axis, perm in members]
        kinds = [(axis, perm) for _, _, axis, perm in members]
        send, recv, arrs, token = _ag_start(placed, kinds, order, name=f"ag_start_{gname}")
        order = [token]
        started[gname] = (send, recv, arrs, kinds, [(n, l) for n, l, _, _ in members])
    wts = {}

    def arrive(gname, after):
        send, recv, arrs, kinds, keys = started[gname]
        arrs = _ag_wait(send, recv, arrs, kinds, after, name=f"ag_wait_{gname}")
        arrs = _ag_forward(arrs, kinds, name=f"ag_fwd_{gname}")
        wts.update(zip(keys, arrs))

    full = {}
    for n, parts in zip(sm_names, zip(*[_unpack(gathered[k], sm_shapes) for k in range(N_CHIPS)])):
        full[n] = jnp.concatenate(parts, axis=SMALL_SHARDED[n])
    for n in WEIGHTS:
        if n not in BIG and n not in full:
            full[n] = P[n]

    assert G * L == C, "a gMLP group must be as wide as a chunk is long"

    def row(v):
        return v.reshape(1, -1)

    def pad_rows(v, r):
        return jnp.pad(v, ((0, r - v.shape[0]), (0, 0)))

    xf = x.reshape(T, D)
    saved = []
    cur, cur_b = xf, xf.astype(_MXU)
    for i in range(depth):
        j = i // 2
        sv = {'x': cur, 'xb': cur_b}
        arrive(groups[2 * i][0], order if i == 0 else [cur_b])
        if i % 2 == 0:
            b_in = row(_perm_cols(full['conv_b_in'][j]))
            h1 = _mm(cur_b, wts['conv_w_in', j], bl=0, bias=b_in, tm=_tile(T, 512), tn=_tile(2 * C, 1024, LANES),
                     tk=D, name=f"conv_in_{j}", n_outer=True, out_dtype=_ADT)
            wdw = pad_rows(full['conv_w_dw'][j], CONV_TAPS_PAD)
            dwo = _conv_fwd(h1, wdw, row(full['conv_b_dw'][j]), B=B, S=S, name=f"conv_dw_{j}")
            s_act, xhc, rsc = _ln_silu_fwd(dwo, row(full['conv_ln_g'][j]), row(full['conv_ln_b'][j]),
                                           name=f"conv_ln_{j}")
            sv.update(h1=h1, wdw=wdw, act=s_act, xhc=xhc, rsc=rsc)
            y1 = _mm_res_ln(s_act, wts['conv_w_out', j], 0, row(full['conv_b_out'][j]), cur, alpha, row(norm1_g[i]),
                            row(norm1_b[i]), name=f"conv_out_ln_{j}")
        else:
            b_in = row(_perm_cols(full['gmlp_b_in'][j]))
            pre = _mm(cur_b, wts['gmlp_w_in', j], bl=0, bias=b_in, tm=_tile(T, 512), tn=_tile(2 * C, 1024, LANES),
                      tk=D, name=f"gmlp_in_{j}", n_outer=True, out_dtype=_ADT)
            bsb = jnp.repeat(gmlp_b_s[j].T, L, axis=1)
            us, xhv, rsv = _gmlp_gate_fwd(pre, row(full['gmlp_ln_g'][j]), row(full['gmlp_ln_b'][j]), gmlp_w_s[j],
                                          bsb, name=f"gmlp_gate_{j}")
            sv.update(pre=pre, bsb=bsb, act=us, xhv=xhv, rsv=rsv)
            y1 = _mm_res_ln(us, wts['gmlp_w_out', j], 0, row(full['gmlp_b_out'][j]), cur, alpha, row(norm1_g[i]),
                            row(norm1_b[i]), name=f"gmlp_out_ln_{j}")
        x1, x1b, xh1, rs1 = y1
        arrive(groups[2 * i + 1][0], [x1b])
        wdw3 = pad_rows(_perm_cols(full['ffn_w_dw'][i]), SUBLANES)
        bdw3 = row(_perm_cols(ffn_b_dw[i]))
        ffn_in = (x1b, wts['ffn_w_up', i], wts['ffn_w_down', i], row(_perm_cols(ffn_b_up[i])), wdw3, bdw3)
        first = _ffn_fwd_half(0, *ffn_in, S=S, name=f"ffn_fwd_a_{i}")
        hs, hcs, f_act, x2, x2b, xh2, rs2 = _ffn_fwd_half(
            1, *ffn_in, S=S, name=f"ffn_fwd_b_{i}", prev=first,
            tail=(x1, alpha, row(ffn_b_down[i]), row(norm2_g[i]), row(norm2_b[i])))
        sv.update(x1=x1, x1b=x1b, xh1=xh1, rs1=rs1, hs=hs, hcs=hcs, f=f_act, wdw3=wdw3, xh2=xh2, rs2=rs2)
        saved.append(sv)
        cur, cur_b = x2, x2b

    sg = {n: [None] * full[n].shape[0] for n in WEIGHTS if n not in BIG}
    inflight = {n: [None] * P[n].shape[0] for n in BIG}
    deps = []
    tgt = loss_target.reshape(T, D)
    dcur = None
    loss_part = None
    tk_t = _tile(T, 2048)

    def wgrad(n, l, a_, b_, **kw):
        g = _mm(a_, b_, ta=True, out_dtype=_WIRE, tk=tk_t, name=f"{n}_dw_{l}", deps=deps, **kw)
        send, recv, g_thru, land, token = _rs_start(g, name=f"rs_start_{n}_{l}")
        inflight[n][l] = (send, recv, g_thru, land)
        deps.append(token)

    for i in reversed(range(depth)):
        j = i // 2
        sv = saved[i]
        if i == depth - 1:
            dz2, dz2b, dg, db, cs, loss_part = _ln_bwd(cur, sv['xh2'], sv['rs2'], row(norm2_g[i]), target=tgt,
                                                       name=f"ln2_bwd_head_{i}")
        else:
            dz2, dz2b, dg, db, cs = dcur
        sg['norm2_g'][i], sg['norm2_b'][i], sg['ffn_b_down'][i] = dg.sum(0), db.sum(0), cs.sum(0)
        Fh = F2 // 2
        wgrad('ffn_w_down', i, sv['f'], dz2b, tm=Fh // 2, tn=_tile(D, 1024, LANES), pieces=('row',))
        ffn_in = (dz2b, wts['ffn_w_down', i], wts['ffn_w_up', i], sv['hs'], sv['hcs'], sv['wdw3'])
        dh0, csu0, dwd0, dbd0, dxp = _ffn_bwd_half(0, *ffn_in, S=S, name=f"ffn_bwd_a_{i}", dz=dz2, alpha=alpha)
        dh, csu1, dwd1, dbd1, dz1, dz1b, dg, db, cs = _ffn_bwd_half(
            1, *ffn_in, S=S, name=f"ffn_bwd_b_{i}", prev=(dh0, dxp), ln=(sv['xh1'], sv['rs1'], row(norm1_g[i])))
        sg['ffn_b_up'][i] = _perm_cols(jnp.concatenate([csu0.sum(0), csu1.sum(0)], axis=-1))
        sg['ffn_w_dw'][i] = _perm_cols(jnp.concatenate([dwd0.sum(1), dwd1.sum(1)], axis=-1))
        sg['ffn_b_dw'][i] = _perm_cols(jnp.concatenate([dbd0.sum(0), dbd1.sum(0)], axis=-1))
        wgrad('ffn_w_up', i, sv['x1b'], dh, tm=D, tn=F2 // N_CHIPS, pieces=('col', True))
        sg['norm1_g'][i], sg['norm1_b'][i] = dg.sum(0), db.sum(0)
        if i % 2 == 0:
            sg['conv_b_out'][j] = cs.sum(0)
            wgrad('conv_w_out', j, sv['act'], dz1b, tm=_tile(C, 1024), tn=_tile(D, 1024, LANES), pieces=('row',))
            ddw, dg, db = _ln_silu_bwd(dz1b, wts['conv_w_out', j], sv['xhc'], sv['rsc'], row(full['conv_ln_g'][j]),
                                       row(full['conv_ln_b'][j]), name=f"conv_ln_bwd_{j}")
            sg['conv_ln_g'][j], sg['conv_ln_b'][j] = dg.sum(0), db.sum(0)
            dglu, dwk, dbk = _conv_bwd(ddw, sv['h1'], sv['wdw'], B=B, S=S, name=f"conv_dw_bwd_{j}")
            sg['conv_w_dw'][j] = dwk.sum(1)[:conv_w_dw.shape[1]]
            sg['conv_b_dw'][j] = dbk.sum(0)
            dh1, csi = _glu_bwd(dglu, sv['h1'], name=f"conv_glu_bwd_{j}")
            sg['conv_b_in'][j] = _perm_cols(csi.sum(0))
            fam = 'conv_w_in'
        else:
            sg['gmlp_b_out'][j] = cs.sum(0)
            wgrad('gmlp_w_out', j, sv['act'], dz1b, tm=_tile(C, 1024), tn=_tile(D, 1024, LANES), pieces=('row',))
            dh1, dg, db, csi, dws, dbs = _gmlp_gate_bwd(dz1b, wts['gmlp_w_out', j], sv['pre'], sv['xhv'], sv['rsv'],
                                                        row(full['gmlp_ln_g'][j]), row(full['gmlp_ln_b'][j]),
                                                        gmlp_w_s[j], sv['bsb'], name=f"gmlp_gate_bwd_{j}")
            sg['gmlp_ln_g'][j], sg['gmlp_ln_b'][j] = dg.sum(0), db.sum(0)
            sg['gmlp_b_in'][j] = _perm_cols(csi.sum(0))
            sg['gmlp_w_s'][j] = dws
            sg['gmlp_b_s'][j] = dbs.reshape(L, G, L).sum(-1).T
            fam = 'gmlp_w_in'
        wgrad(fam, j, sv['xb'], dh1, tm=D, tn=(2 * C) // N_CHIPS, pieces=('col', True))
        if i > 0:
            below = saved[i - 1]
            dcur = _mm_ln_bwd(dh1, wts[fam, j], dz1, alpha, below['xh2'], below['rs2'], row(norm2_g[i - 1]),
                              name=f"{fam}_dx_{j}", deps=deps)
        else:
            dcur = _mm(dh1, wts[fam, j], bl=0, tb=True, res=dz1, res_scale=alpha, tm=_tile(T, 512),
                       tn=_tile(D, 1024, LANES), tk=2 * C, name=f"{fam}_dx_{j}", deps=deps)
    grad_x = dcur.reshape(B, S, D)

    small_names = [n for n in WEIGHTS if n not in BIG]
    small_full = [jnp.stack(sg[n]) for n in small_names]
    flat = _pack(small_full + [loss_part])
    red = _allreduce_flat(flat, name="ar_small")
    red_parts = _unpack(red, [a.shape for a in small_full] + [loss_part.shape])
    loss = (0.5 / D) * jnp.sum(red_parts[-1])
    grads = {}
    for n, g in zip(small_names, red_parts[:-1]):
        if n in SMALL_SHARDED:
            ax = SMALL_SHARDED[n]
            width = P[n].shape[ax]
            g = lax.dynamic_slice_in_dim(g, shard * width, width, axis=ax)
        grads[n] = g

    big_out = {}
    for n in ['ffn_w_down', 'ffn_w_up', 'gmlp_w_out', 'gmlp_w_in', 'conv_w_out', 'conv_w_in']:
        own = None
        n_layers = len(inflight[n])
        for l in reversed(range(n_layers)):
            send, recv, g_thru, land = inflight[n][l]
            pc_, r = _rs_wait(send, recv, g_thru, land, dcur, name=f"rs_wait_{n}_{l}")
            own = _sum_pieces(pc_, r, me_id, l, own, n_layers, name=f"sum_{n}_{l}")
        got = _pair_exchange(own, name=f"px_{n}")
        big_out[n] = _adam_halves(P[n], own, got, P['m_' + n], P['v_' + n], core_id, name=f"adam_{n}")

    shapes = [P[n].shape for n in small_names]
    n_small = sum(functools.reduce(lambda p_, d_: p_ * d_, s_, 1) for s_ in shapes)
    unit = SUBLANES * LANES
    npad = -(-n_small // unit) * unit

    def flat2d(arrs, fill=0.0):
        v = _pack(arrs)
        return jnp.pad(v, (0, npad - n_small), constant_values=fill).reshape(-1, LANES)

    dl, mo, vo = _adam(flat2d([P[n] for n in small_names]), flat2d([grads[n] for n in small_names]),
                       flat2d([P['m_' + n] for n in small_names]),
                       flat2d([P['v_' + n] for n in small_names], fill=1.0), name="adam_small")
    small_out = {n: [grads[n], None, None, None] for n in small_names}
    for k, t in enumerate((dl, mo, vo)):
        for n, a in zip(small_names, _unpack(t.reshape(-1), shapes)):
            small_out[n][k + 1] = a

    outs = [loss, grad_x]
    for k in range(4):
        for n in WEIGHTS:
            outs.append(big_out[n][k] if n in BIG else small_out[n][k])
    return tuple(outs)
```

```python
import functools

import jax
import jax.numpy as jnp
from jax import lax
from jax.experimental import pallas as pl
from jax.experimental.pallas import tpu as pltpu

F32 = jnp.float32
_MXU = jnp.bfloat16
_WIRE = jnp.bfloat16
_HDT = jnp.bfloat16
_ADT = jnp.bfloat16
LN_EPS = 1e-5
ADAM_LR, ADAM_B1, ADAM_B2, ADAM_EPS, ADAM_WD, ADAM_STEP = 0.001, 0.9, 0.999, 1e-08, 0.01, 10
N_CHIPS = 4
N_DEV = 8
LANES = 128
SUBLANES = 8
CONV_TAPS_PAD = 32
VMEM_LIMIT = 56 << 20
MESH = pl.DeviceIdType.MESH
ANY = pl.BlockSpec(memory_space=pl.ANY)
HBM = pl.BlockSpec(memory_space=pltpu.HBM)
SEMS = pl.BlockSpec(memory_space=pltpu.SEMAPHORE)
EFFECT = pltpu.SideEffectType.DATAFLOW_SIDE_EFFECTING
PERM = (0, 2, 1, 3)


def _cp(sem=None):
    return pltpu.CompilerParams(dimension_semantics=sem, vmem_limit_bytes=VMEM_LIMIT)


def _tile(dim, pref, mult=SUBLANES):
    if dim <= pref:
        return dim
    t = (pref // mult) * mult
    while t > mult and dim % t:
        t -= mult
    assert dim % t == 0, (dim, pref, mult)
    return t


def _perm_idx(q):
    return (q % 2) * 2 + q // 2


def _fold8(t):
    r, n = t.shape
    return t.reshape(r // SUBLANES, SUBLANES, n).sum(axis=0)


def _ln_rows(z, g, b):
    mu = jnp.mean(z, axis=-1, keepdims=True)
    xc = z - mu
    var = jnp.mean(xc * xc, axis=-1, keepdims=True)
    rstd = lax.rsqrt(var + LN_EPS)
    xh = xc * rstd
    return xh * g + b, xh, rstd


def _ln_bwd_rows(dy, xh, rstd, g):
    dxh = dy * g
    m1 = jnp.mean(dxh, axis=-1, keepdims=True)
    m2 = jnp.mean(dxh * xh, axis=-1, keepdims=True)
    return rstd * (dxh - m1 - xh * m2)


def _sigmoid(v):
    return 1.0 / (1.0 + jnp.exp(-v))


def _gelu_parts(p):
    cdf = 0.5 * (1.0 + lax.erf(p * 0.7071067811865476))
    pdf = jnp.exp(-0.5 * p * p) * 0.3989422804014327
    return p * cdf, cdf + p * pdf


def _shift_down(prev8, t, s):
    ext = jnp.concatenate([prev8, t], axis=0)
    return pltpu.roll(ext, s, 0)[SUBLANES:]


def _shift_up(t, next8, s):
    n = t.shape[0]
    ext = jnp.concatenate([t, next8], axis=0)
    return pltpu.roll(ext, n + SUBLANES - s, 0)[:n]


def _mm(a, b, *, ta=False, tb=False, bl=None, bias=None, res=None, res_scale=1.0, out_dtype=F32,
        tm, tn, tk, name, pieces=None, deps=None, n_outer=False):
    M, K = (a.shape[1], a.shape[0]) if ta else a.shape
    bs = b.shape[1:] if bl is not None else b.shape
    N, Kb = (bs[0], bs[1]) if tb else (bs[1], bs[0])
    assert K == Kb and M % tm == 0 and N % tn == 0 and K % tk == 0, (a.shape, b.shape, tm, tn, tk)
    gm, gn, gk = M // tm, N // tn, K // tk

    def spec(block, imap):
        if n_outer:
            return pl.BlockSpec(block, lambda j, i, k: imap(i, j, k))
        return pl.BlockSpec(block, imap)

    a_spec = spec((tk, tm), lambda i, j, k: (k, i)) if ta else spec((tm, tk), lambda i, j, k: (i, k))
    bblk = (tn, tk) if tb else (tk, tn)
    bmap = (lambda i, j, k: (j, k)) if tb else (lambda i, j, k: (k, j))
    if bl is not None:
        b_spec = spec((None,) + bblk, lambda i, j, k: (bl,) + bmap(i, j, k))
    else:
        b_spec = spec(bblk, bmap)
    in_specs, operands = [a_spec, b_spec], [a, b]
    if bias is not None:
        in_specs.append(spec((1, tn), lambda i, j, k: (0, j)))
        operands.append(bias)
    if res is not None:
        in_specs.append(spec((tm, tn), lambda i, j, k: (i, j)))
        operands.append(res)
    n_dep = len(deps) if deps else 0
    if n_dep:
        in_specs += [ANY] * n_dep
        operands += deps
        del deps[:]
    if pieces is None:
        out_shape = jax.ShapeDtypeStruct((M, N), out_dtype)
        out_spec = spec((tm, tn), lambda i, j, k: (i, j))
        ppb = pr = None
    elif pieces[0] == 'col':
        pr, pc = M // 2, N // N_CHIPS
        assert tm % pr == 0 and pc % tn == 0
        ppb, per = tm // pr, pc // tn
        perm = pieces[1]
        out_shape = jax.ShapeDtypeStruct((N_DEV, pr, pc), out_dtype)
        out_spec = spec(
            (ppb, pr, tn),
            lambda i, j, k: ((2 * (_perm_idx(j // per) if perm else j // per)) // ppb + i, 0, j % per))
    else:
        pr = M // N_DEV
        assert tm % pr == 0
        ppb = tm // pr
        out_shape = jax.ShapeDtypeStruct((N_DEV, pr, N), out_dtype)
        out_spec = spec((ppb, pr, tn), lambda i, j, k: (i, 0, j))
    dims = (((0 if ta else 1,), (1 if tb else 0,)), ((), ()))

    def body(*refs):
        a_ref, b_ref = refs[0], refs[1]
        pos = 2
        bias_ref = res_ref = None
        if bias is not None:
            bias_ref = refs[pos]
            pos += 1
        if res is not None:
            res_ref = refs[pos]
            pos += 1
        pos += n_dep
        o_ref = refs[pos]

        def finish(r):
            if bias_ref is not None:
                r = r + bias_ref[...]
            if res_ref is not None:
                r = r + res_scale * res_ref[...]
            if pieces is not None:
                r = r.reshape(ppb, pr, tn)
            o_ref[...] = r.astype(out_dtype)

        part = lax.dot_general(a_ref[...].astype(_MXU), b_ref[...].astype(_MXU), dims, preferred_element_type=F32)
        if gk == 1:
            finish(part)
            return
        acc_ref = refs[pos + 1]
        k = pl.program_id(2)

        @pl.when(k == 0)
        def _():
            acc_ref[...] = part

        @pl.when((k > 0) & (k < gk - 1))
        def _():
            acc_ref[...] += part

        @pl.when(k == gk - 1)
        def _():
            finish(acc_ref[...] + part)

    return pl.pallas_call(
        body, name=name, grid=(gn, gm, gk) if n_outer else (gm, gn, gk), in_specs=in_specs, out_specs=out_spec,
        out_shape=out_shape, scratch_shapes=[pltpu.VMEM((tm, tn), F32)] if gk > 1 else [],
        compiler_params=_cp(("parallel", "parallel", "arbitrary")),
    )(*operands)


def _mm_ln_bwd(a, w, res, res_scale, xh, rstd, g, *, name, deps=None):
    T, K = a.shape
    D = w.shape[1]
    tm = _tile(T, 512)
    n_dep = len(deps) if deps else 0

    def body(a_ref, w_ref, res_ref, xh_ref, rs_ref, g_ref, *rest):
        dz_ref, dzb_ref, dg_ref, db_ref, cs_ref = rest[n_dep:]

        @pl.when(pl.program_id(0) == 0)
        def _():
            dg_ref[...] = jnp.zeros_like(dg_ref)
            db_ref[...] = jnp.zeros_like(db_ref)
            cs_ref[...] = jnp.zeros_like(cs_ref)

        d = lax.dot_general(a_ref[...].astype(_MXU), w_ref[...].astype(_MXU), (((1,), (1,)), ((), ())),
                            preferred_element_type=F32) + res_scale * res_ref[...]
        xh = xh_ref[...]
        dz = _ln_bwd_rows(d, xh, rs_ref[...], g_ref[...])
        dz_ref[...] = dz
        dzb_ref[...] = dz.astype(_MXU)
        dg_ref[...] += _fold8(d * xh)
        db_ref[...] += _fold8(d)
        cs_ref[...] += _fold8(dz)

    row = lambda i: (i, 0)
    fixed = lambda i: (0, 0)
    tile = pl.BlockSpec((tm, D), row)
    part = pl.BlockSpec((SUBLANES, D), fixed)
    operands = [a, w, res, xh, rstd, g] + (list(deps) if deps else [])
    if deps:
        del deps[:]
    return pl.pallas_call(
        body, name=name, grid=(T // tm,),
        in_specs=[pl.BlockSpec((tm, K), row),
                  pl.BlockSpec((None, D, K), lambda i: (0, 0, 0), pipeline_mode=pl.Buffered(1)),
                  tile, tile, pl.BlockSpec((tm, 1), row), pl.BlockSpec((1, D), fixed)] + [ANY] * n_dep,
        out_specs=[tile, tile, part, part, part],
        out_shape=[jax.ShapeDtypeStruct((T, D), F32), jax.ShapeDtypeStruct((T, D), _MXU)]
        + [jax.ShapeDtypeStruct((SUBLANES, D), F32)] * 3,
        compiler_params=_cp(("arbitrary",)),
    )(*operands)


def _out_ln(act, wo_ref, bias_ref, res_ref, alpha, g_ref, b_ref, y_ref, yb_ref, xh_ref, rs_ref):
    z = jnp.dot(act, wo_ref[...].astype(_MXU), preferred_element_type=F32) + bias_ref[...] + alpha * res_ref[...]
    y, xh, rstd = _ln_rows(z, g_ref[...], b_ref[...])
    y_ref[...] = y
    yb_ref[...] = y.astype(_MXU)
    xh_ref[...] = xh
    rs_ref[...] = rstd


def _conv_tail_fwd(v, gc, bc, w, bias, res, alpha, g, b, *, name):
    T, C = v.shape
    D = w.shape[-1]
    tm = _tile(T, 256)

    def body(v_ref, gc_ref, bc_ref, w_ref, bias_ref, res_ref, g_ref, b_ref,
             s_ref, xhc_ref, rsc_ref, y_ref, yb_ref, xh_ref, rs_ref):
        yv, xhc, rsc = _ln_rows(v_ref[...], gc_ref[...], bc_ref[...])
        s = (yv * _sigmoid(yv)).astype(_MXU)
        s_ref[...] = s
        xhc_ref[...] = xhc
        rsc_ref[...] = rsc
        _out_ln(s, w_ref, bias_ref, res_ref, alpha, g_ref, b_ref, y_ref, yb_ref, xh_ref, rs_ref)

    row = lambda i: (i, 0)
    fixed = lambda i: (0, 0)
    vc, vd = pl.BlockSpec((1, C), fixed), pl.BlockSpec((1, D), fixed)
    tc_, td = pl.BlockSpec((tm, C), row), pl.BlockSpec((tm, D), row)
    one = pl.BlockSpec((tm, 1), row)
    return pl.pallas_call(
        body, name=name, grid=(T // tm,),
        in_specs=[tc_, vc, vc, _resident((None, C, D), lambda i: (0, 0, 0)), vd, td, vd, vd],
        out_specs=[tc_, tc_, one, td, td, td, one],
        out_shape=[jax.ShapeDtypeStruct((T, C), _MXU), jax.ShapeDtypeStruct((T, C), F32),
                   jax.ShapeDtypeStruct((T, 1), F32), jax.ShapeDtypeStruct((T, D), F32),
                   jax.ShapeDtypeStruct((T, D), _MXU), jax.ShapeDtypeStruct((T, D), F32),
                   jax.ShapeDtypeStruct((T, 1), F32)],
        compiler_params=_cp(("parallel",)),
    )(v, gc, bc, w, bias, res, g, b)


def _ln_bwd(dy, xh, rstd, g, *, name, target=None):
    T, D = dy.shape
    tm = _tile(T, 256)
    head = target is not None

    def body(*refs):
        if head:
            dy_ref, t_ref, xh_ref, rs_ref, g_ref, dz_ref, dzb_ref, dg_ref, db_ref, cs_ref, ls_ref = refs
        else:
            dy_ref, xh_ref, rs_ref, g_ref, dz_ref, dzb_ref, dg_ref, db_ref, cs_ref = refs
        i = pl.program_id(0)

        @pl.when(i == 0)
        def _():
            dg_ref[...] = jnp.zeros_like(dg_ref)
            db_ref[...] = jnp.zeros_like(db_ref)
            cs_ref[...] = jnp.zeros_like(cs_ref)
            if head:
                ls_ref[...] = jnp.zeros_like(ls_ref)

        d = dy_ref[...]
        if head:
            err = d - t_ref[...]
            ls_ref[...] += _fold8(err * err)
            d = err * (1.0 / D)
        xh = xh_ref[...]
        dz = _ln_bwd_rows(d, xh, rs_ref[...], g_ref[...])
        dz_ref[...] = dz
        dzb_ref[...] = dz.astype(_MXU)
        dg_ref[...] += _fold8(d * xh)
        db_ref[...] += _fold8(d)
        cs_ref[...] += _fold8(dz)

    row = lambda i: (i, 0)
    fixed = lambda i: (0, 0)
    tile = pl.BlockSpec((tm, D), row)
    part = pl.BlockSpec((SUBLANES, D), fixed)
    in_specs = [tile] + ([tile] if head else []) + [tile, pl.BlockSpec((tm, 1), row), pl.BlockSpec((1, D), fixed)]
    n_part = 4 if head else 3
    operands = [dy] + ([target] if head else []) + [xh, rstd, g]
    return pl.pallas_call(
        body, name=name, grid=(T // tm,), in_specs=in_specs,
        out_specs=[tile, tile] + [part] * n_part,
        out_shape=[jax.ShapeDtypeStruct((T, D), F32), jax.ShapeDtypeStruct((T, D), _MXU)]
        + [jax.ShapeDtypeStruct((SUBLANES, D), F32)] * n_part,
        compiler_params=_cp(("arbitrary",)),
    )(*operands)


def _conv_cols(C, tc):
    per = (C // 2) // tc
    return per, (lambda j: (j // per) * (2 * per) + j % per)


def _glu_shifted(a_ref, g_ref, p_ref, S):
    u = a_ref[...].astype(F32) * _sigmoid(g_ref[...].astype(F32))
    rows = lax.broadcasted_iota(jnp.int32, u.shape, 0)
    for r in range(SUBLANES):
        p_ref[r, 0:CONV_TAPS_PAD, :] = jnp.zeros((CONV_TAPS_PAD, u.shape[1]), F32)
        p_ref[r, CONV_TAPS_PAD:CONV_TAPS_PAD + S, :] = u if r == 0 else jnp.where(rows >= r, pltpu.roll(u, r, 0), 0.0)


def _conv_fwd(h1, w_dw, b_dw, *, B, S, name):
    C = w_dw.shape[1]
    taps = CONV_TAPS_PAD - 1
    tc = LANES
    ch = _tile(S, 128)
    per, col_a = _conv_cols(C, tc)

    def body(a_ref, g_ref, w_ref, b_ref, o_ref, p_ref):
        _glu_shifted(a_ref, g_ref, p_ref, S)

        def chunk(ci, carry):
            base = pl.multiple_of(ci * ch, ch)
            acc = jnp.zeros((ch, tc), F32) + b_ref[...]
            for k in range(taps):
                q, r = divmod(taps - 1 - k, SUBLANES)
                start = pl.multiple_of(base + (CONV_TAPS_PAD - SUBLANES * q), SUBLANES)
                acc = acc + w_ref[pl.ds(k, 1), :] * p_ref[r, pl.ds(start, ch), :]
            o_ref[pl.ds(base, ch), :] = acc
            return carry

        lax.fori_loop(0, S // ch, chunk, 0)

    return pl.pallas_call(
        body, name=name, grid=(B, C // tc),
        in_specs=[pl.BlockSpec((S, tc), lambda b, j: (b, col_a(j))),
                  pl.BlockSpec((S, tc), lambda b, j: (b, col_a(j) + per)),
                  pl.BlockSpec((CONV_TAPS_PAD, tc), lambda b, j: (0, j)),
                  pl.BlockSpec((1, tc), lambda b, j: (0, j))],
        out_specs=pl.BlockSpec((S, tc), lambda b, j: (b, j)),
        out_shape=jax.ShapeDtypeStruct((B * S, C), F32),
        scratch_shapes=[pltpu.VMEM((SUBLANES, S + CONV_TAPS_PAD, tc), F32)],
        compiler_params=_cp(("parallel", "parallel")),
    )(h1, h1, w_dw, b_dw)


def _conv_bwd(dd, h1, w_dw, *, B, S, name):
    C = w_dw.shape[1]
    taps = CONV_TAPS_PAD - 1
    tc = LANES
    ch = _tile(S, 128)
    per, col_a = _conv_cols(C, tc)

    def body(d_ref, a_ref, g_ref, w_ref, du_ref, dw_ref, db_ref, p_ref, q_ref):
        b = pl.program_id(1)

        @pl.when(b == 0)
        def _():
            dw_ref[...] = jnp.zeros_like(dw_ref)
            db_ref[...] = jnp.zeros_like(db_ref)

        _glu_shifted(a_ref, g_ref, p_ref, S)
        d = d_ref[...]
        rows = lax.broadcasted_iota(jnp.int32, d.shape, 0)
        for r in range(SUBLANES):
            q_ref[r, S:S + CONV_TAPS_PAD, :] = jnp.zeros((CONV_TAPS_PAD, tc), F32)
            q_ref[r, 0:S, :] = d if r == 0 else jnp.where(rows < S - r, pltpu.roll(d, S - r, 0), 0.0)
        db_ref[...] += _fold8(d)

        def chunk(ci, carry):
            base = pl.multiple_of(ci * ch, ch)
            dch = d_ref[pl.ds(base, ch), :]
            acc = jnp.zeros((ch, tc), F32)
            for k in range(taps):
                q, r = divmod(taps - 1 - k, SUBLANES)
                up = pl.multiple_of(base + SUBLANES * q, SUBLANES)
                acc = acc + w_ref[pl.ds(k, 1), :] * q_ref[r, pl.ds(up, ch), :]
                down = pl.multiple_of(base + (CONV_TAPS_PAD - SUBLANES * q), SUBLANES)
                dw_ref[k] += _fold8(dch * p_ref[r, pl.ds(down, ch), :])
            du_ref[pl.ds(base, ch), :] = acc
            return carry

        lax.fori_loop(0, S // ch, chunk, 0)

    return pl.pallas_call(
        body, name=name, grid=(C // tc, B),
        in_specs=[pl.BlockSpec((S, tc), lambda j, b: (b, j)),
                  pl.BlockSpec((S, tc), lambda j, b: (b, col_a(j))),
                  pl.BlockSpec((S, tc), lambda j, b: (b, col_a(j) + per)),
                  pl.BlockSpec((CONV_TAPS_PAD, tc), lambda j, b: (0, j))],
        out_specs=[pl.BlockSpec((S, tc), lambda j, b: (b, j)),
                   pl.BlockSpec((CONV_TAPS_PAD, SUBLANES, tc), lambda j, b: (0, 0, j)),
                   pl.BlockSpec((SUBLANES, tc), lambda j, b: (0, j))],
        out_shape=[jax.ShapeDtypeStruct((B * S, C), F32),
                   jax.ShapeDtypeStruct((CONV_TAPS_PAD, SUBLANES, C), F32),
                   jax.ShapeDtypeStruct((SUBLANES, C), F32)],
        scratch_shapes=[pltpu.VMEM((SUBLANES, S + CONV_TAPS_PAD, tc), F32),
                        pltpu.VMEM((SUBLANES, S + CONV_TAPS_PAD, tc), F32)],
        compiler_params=_cp(("parallel", "arbitrary")),
    )(dd, h1, h1, w_dw)


def _ln_silu_bwd(dzb, w, xh, rstd, g, b, *, name):
    T, D = dzb.shape
    C = w.shape[1]
    tm = _tile(T, 512)

    def body(dz_ref, w_ref, xh_ref, rs_ref, g_ref, b_ref, dv_ref, dg_ref, db_ref):
        @pl.when(pl.program_id(0) == 0)
        def _():
            dg_ref[...] = jnp.zeros_like(dg_ref)
            db_ref[...] = jnp.zeros_like(db_ref)

        ds = lax.dot_general(dz_ref[...].astype(_MXU), w_ref[...].astype(_MXU), (((1,), (1,)), ((), ())),
                             preferred_element_type=F32)
        xh = xh_ref[...]
        gam = g_ref[...]
        y = xh * gam + b_ref[...]
        sig = _sigmoid(y)
        dln = ds * (sig * (1.0 + y * (1.0 - sig)))
        dv_ref[...] = _ln_bwd_rows(dln, xh, rs_ref[...], gam)
        dg_ref[...] += _fold8(dln * xh)
        db_ref[...] += _fold8(dln)

    row = lambda i: (i, 0)
    fixed = lambda i: (0, 0)
    vec = pl.BlockSpec((1, C), fixed)
    part = pl.BlockSpec((SUBLANES, C), fixed)
    return pl.pallas_call(
        body, name=name, grid=(T // tm,),
        in_specs=[pl.BlockSpec((tm, D), row), _resident((None, C, D), lambda i: (0, 0, 0)),
                  pl.BlockSpec((tm, C), row), pl.BlockSpec((tm, 1), row), vec, vec],
        out_specs=[pl.BlockSpec((tm, C), row), part, part],
        out_shape=[jax.ShapeDtypeStruct((T, C), F32)] + [jax.ShapeDtypeStruct((SUBLANES, C), F32)] * 2,
        compiler_params=_cp(("arbitrary",)),
    )(dzb, w, xh, rstd, g, b)


def _glu_bwd(du, h1, *, name):
    T, C = du.shape
    il = C // 2
    tm = _tile(T, 256)

    def body(du_ref, h_ref, dh_ref, cs_ref):
        @pl.when(pl.program_id(0) == 0)
        def _():
            cs_ref[...] = jnp.zeros_like(cs_ref)

        for hb in range(2):
            a = h_ref[:, 2 * hb * il:(2 * hb + 1) * il].astype(F32)
            gate = h_ref[:, (2 * hb + 1) * il:(2 * hb + 2) * il].astype(F32)
            d = du_ref[:, hb * il:(hb + 1) * il]
            sig = _sigmoid(gate)
            da = d * sig
            dgate = d * a * sig * (1.0 - sig)
            dh_ref[:, 2 * hb * il:(2 * hb + 1) * il] = da.astype(_MXU)
            dh_ref[:, (2 * hb + 1) * il:(2 * hb + 2) * il] = dgate.astype(_MXU)
            cs_ref[:, 2 * hb * il:(2 * hb + 1) * il] += _fold8(da)
            cs_ref[:, (2 * hb + 1) * il:(2 * hb + 2) * il] += _fold8(dgate)

    row = lambda i: (i, 0)
    return pl.pallas_call(
        body, name=name, grid=(T // tm,),
        in_specs=[pl.BlockSpec((tm, C), row), pl.BlockSpec((tm, 2 * C), row)],
        out_specs=[pl.BlockSpec((tm, 2 * C), row), pl.BlockSpec((SUBLANES, 2 * C), lambda i: (0, 0))],
        out_shape=[jax.ShapeDtypeStruct((T, 2 * C), _MXU), jax.ShapeDtypeStruct((SUBLANES, 2 * C), F32)],
        compiler_params=_cp(("arbitrary",)),
    )(du, h1)


def _tril_mask(n):
    return lax.broadcasted_iota(jnp.int32, (n, n), 0) >= lax.broadcasted_iota(jnp.int32, (n, n), 1)


def _split_uv(t, il):
    u = jnp.concatenate([t[:, 0:il], t[:, 2 * il:3 * il]], axis=1)
    v = jnp.concatenate([t[:, il:2 * il], t[:, 3 * il:4 * il]], axis=1)
    return u, v


def _gmlp_gate_fwd(p, g, b, w_s, bsb, w_out, bias, res, alpha, g1, b1, *, name):
    T, C2 = p.shape
    C = C2 // 2
    D = w_out.shape[-1]
    il = C // 2
    G, L, _ = w_s.shape
    assert G * L == C
    tm = _tile(T, 2 * L, L)

    def body(p_ref, g_ref, b_ref, ws_ref, bs_ref, wo_ref, bias_ref, res_ref, g1_ref, b1_ref,
             us_ref, xh_ref, rs_ref, y_ref, yb_ref, xh1_ref, rs1_ref, vn_ref, u_ref):
        z, _ = _gelu_parts(p_ref[...].astype(F32))
        u, v = _split_uv(z, il)
        vn, xh, rstd = _ln_rows(v, g_ref[...], b_ref[...])
        xh_ref[...] = xh
        rs_ref[...] = rstd
        vn_ref[...] = vn.astype(_MXU)
        u_ref[...] = u
        mask = _tril_mask(L)
        for gi in range(G):
            wc = jnp.where(mask, ws_ref[gi], 0.0).astype(_MXU)
            cols = slice(gi * L, (gi + 1) * L)
            for c in range(tm // L):
                rows = slice(c * L, (c + 1) * L)
                s = jnp.dot(wc, vn_ref[rows, cols], preferred_element_type=F32) + bs_ref[:, cols]
                us_ref[rows, cols] = (u_ref[rows, cols] * s).astype(_MXU)
        _out_ln(us_ref[...], wo_ref, bias_ref, res_ref, alpha, g1_ref, b1_ref, y_ref, yb_ref, xh1_ref, rs1_ref)

    row = lambda i: (i, 0)
    fixed = lambda i: (0, 0)
    vd, td, one = pl.BlockSpec((1, D), fixed), pl.BlockSpec((tm, D), row), pl.BlockSpec((tm, 1), row)
    return pl.pallas_call(
        body, name=name, grid=(T // tm,),
        in_specs=[pl.BlockSpec((tm, C2), row), pl.BlockSpec((1, C), fixed), pl.BlockSpec((1, C), fixed),
                  pl.BlockSpec((G, L, L), lambda i: (0, 0, 0)), pl.BlockSpec((L, C), fixed),
                  _resident((None, C, D), lambda i: (0, 0, 0)), vd, td, vd, vd],
        out_specs=[pl.BlockSpec((tm, C), row), pl.BlockSpec((tm, C), row), one, td, td, td, one],
        out_shape=[jax.ShapeDtypeStruct((T, C), _MXU), jax.ShapeDtypeStruct((T, C), F32),
                   jax.ShapeDtypeStruct((T, 1), F32), jax.ShapeDtypeStruct((T, D), F32),
                   jax.ShapeDtypeStruct((T, D), _MXU), jax.ShapeDtypeStruct((T, D), F32),
                   jax.ShapeDtypeStruct((T, 1), F32)],
        scratch_shapes=[pltpu.VMEM((tm, C), _MXU), pltpu.VMEM((tm, C), F32)],
        compiler_params=_cp(("parallel",)),
    )(p, g, b, w_s, bsb, w_out, bias, res, g1, b1)


def _gmlp_gate_bwd(dzb, w_out, p, xh, rstd, g, b, w_s, bsb, *, name):
    T, C2 = p.shape
    D = dzb.shape[1]
    C = C2 // 2
    il = C // 2
    G, L, _ = w_s.shape
    tm = _tile(T, 2 * L, L)

    def body(dz_ref, wo_ref, p_ref, xh_ref, rs_ref, g_ref, b_ref, ws_ref, bs_ref,
             dp_ref, dg_ref, db_ref, cs_ref, dws_ref, dbs_ref, vn_ref, u_ref, dvn_ref, du_ref, dus_ref):
        @pl.when(pl.program_id(0) == 0)
        def _():
            dg_ref[...] = jnp.zeros_like(dg_ref)
            db_ref[...] = jnp.zeros_like(db_ref)
            cs_ref[...] = jnp.zeros_like(cs_ref)
            dws_ref[...] = jnp.zeros_like(dws_ref)
            dbs_ref[...] = jnp.zeros_like(dbs_ref)

        dus_ref[...] = lax.dot_general(dz_ref[...].astype(_MXU), wo_ref[...].astype(_MXU), (((1,), (1,)), ((), ())),
                                       preferred_element_type=F32)
        z, gp = _gelu_parts(p_ref[...].astype(F32))
        u, _ = _split_uv(z, il)
        xh = xh_ref[...]
        gam = g_ref[...]
        vn_ref[...] = (xh * gam + b_ref[...]).astype(_MXU)
        u_ref[...] = u
        mask = _tril_mask(L)
        for gi in range(G):
            wc = jnp.where(mask, ws_ref[gi], 0.0).astype(_MXU)
            cols = slice(gi * L, (gi + 1) * L)
            for c in range(tm // L):
                rows = slice(c * L, (c + 1) * L)
                vnb = vn_ref[rows, cols]
                s = jnp.dot(wc, vnb, preferred_element_type=F32) + bs_ref[:, cols]
                d = dus_ref[rows, cols]
                du_ref[rows, cols] = d * s
                ds = d * u_ref[rows, cols]
                dbs_ref[:, cols] += ds
                dsb = ds.astype(_MXU)
                dw = lax.dot_general(dsb, vnb, (((1,), (1,)), ((), ())), preferred_element_type=F32)
                dws_ref[gi] += jnp.where(mask, dw, 0.0)
                dvn_ref[rows, cols] = lax.dot_general(wc, dsb, (((0,), (0,)), ((), ())), preferred_element_type=F32)
        dvn = dvn_ref[...]
        dg_ref[...] += _fold8(dvn * xh)
        db_ref[...] += _fold8(dvn)
        dv = _ln_bwd_rows(dvn, xh, rs_ref[...], gam)
        du = du_ref[...]
        for hb in range(2):
            for part, src in ((0, du), (1, dv)):
                lo = (2 * hb + part) * il
                dp = src[:, hb * il:(hb + 1) * il] * gp[:, lo:lo + il]
                dp_ref[:, lo:lo + il] = dp.astype(_MXU)
                cs_ref[:, lo:lo + il] += _fold8(dp)

    row = lambda i: (i, 0)
    fixed = lambda i: (0, 0)
    part_c = pl.BlockSpec((SUBLANES, C), fixed)
    return pl.pallas_call(
        body, name=name, grid=(T // tm,),
        in_specs=[pl.BlockSpec((tm, D), row), _resident((None, C, D), lambda i: (0, 0, 0)),
                  pl.BlockSpec((tm, C2), row), pl.BlockSpec((tm, C), row),
                  pl.BlockSpec((tm, 1), row), pl.BlockSpec((1, C), fixed), pl.BlockSpec((1, C), fixed),
                  pl.BlockSpec((G, L, L), lambda i: (0, 0, 0)), pl.BlockSpec((L, C), fixed)],
        out_specs=[pl.BlockSpec((tm, C2), row), part_c, part_c, pl.BlockSpec((SUBLANES, C2), fixed),
                   pl.BlockSpec((G, L, L), lambda i: (0, 0, 0)), pl.BlockSpec((L, C), fixed)],
        out_shape=[jax.ShapeDtypeStruct((T, C2), _MXU), jax.ShapeDtypeStruct((SUBLANES, C), F32),
                   jax.ShapeDtypeStruct((SUBLANES, C), F32), jax.ShapeDtypeStruct((SUBLANES, C2), F32),
                   jax.ShapeDtypeStruct((G, L, L), F32), jax.ShapeDtypeStruct((L, C), F32)],
        scratch_shapes=[pltpu.VMEM((tm, C), _MXU), pltpu.VMEM((tm, C), F32), pltpu.VMEM((tm, C), F32),
                        pltpu.VMEM((tm, C), F32), pltpu.VMEM((tm, C), F32)],
        compiler_params=_cp(("arbitrary",)),
    )(dzb, w_out, p, xh, rstd, g, b, w_s, bsb)


def _ffn_conv(h, prev8, w_ref, b_ref):
    h1 = _shift_down(prev8, h, 1)
    h2 = _shift_down(prev8, h, 2)
    return w_ref[pl.ds(2, 1), :] * h + w_ref[pl.ds(1, 1), :] * h1 + w_ref[pl.ds(0, 1), :] * h2 + b_ref[...]


def _resident(block, imap):
    return pl.BlockSpec(block, imap, pipeline_mode=pl.Buffered(1))


def _ffn_fwd_half(j, xb, w_up, w_down, b_up, w_dw, b_dw, *, S, name, prev=None, tail=None):
    T, D = xb.shape
    N = w_up.shape[-1]
    tn = N // N_CHIPS
    tm = _tile(S, 256)
    spt = S // tm
    last = prev is not None
    alpha = tail[1] if last else None

    def body(*refs):
        x_ref, wu_ref, wd_ref, bu_ref, wc_ref, bc_ref = refs[:6]
        if last:
            yp_ref, res_ref, bd_ref, g_ref, b_ref = refs[9:14]
            h_ref, hc_ref, f_ref, y_ref, yb_ref, xh_ref, rs_ref, carry_ref = refs[14:22]
        else:
            h_ref, hc_ref, f_ref, yp_ref, carry_ref = refs[6:11]

        @pl.when(pl.program_id(0) % spt == 0)
        def _():
            carry_ref[...] = jnp.zeros_like(carry_ref)

        h = jnp.dot(x_ref[...].astype(_MXU), wu_ref[...].astype(_MXU), preferred_element_type=F32) + bu_ref[...]
        hq = h.astype(_HDT)
        h_ref[...] = hq
        h = hq.astype(F32)
        hc = _ffn_conv(h, carry_ref[...], wc_ref, bc_ref)
        hc_ref[...] = hc.astype(_HDT)
        carry_ref[...] = h[tm - SUBLANES:tm]
        gte = hc[:, :tn]
        f = (gte * _sigmoid(gte) * hc[:, tn:]).astype(_MXU)
        f_ref[...] = f
        y = jnp.dot(f, wd_ref[...].astype(_MXU), preferred_element_type=F32)
        if not last:
            yp_ref[...] = y
            return
        z = y + yp_ref[...] + bd_ref[...] + alpha * res_ref[...]
        out, xh, rstd = _ln_rows(z, g_ref[...], b_ref[...])
        y_ref[...] = out
        yb_ref[...] = out.astype(_MXU)
        xh_ref[...] = xh
        rs_ref[...] = rstd

    row = lambda i: (i, 0)
    pair = lambda i: (0, j)
    vec = pl.BlockSpec((1, D), lambda i: (0, 0))
    tile = pl.BlockSpec((tm, D), row)
    in_specs = [tile, _resident((None, D, 2 * tn), lambda i: (0, 0, j)), _resident((None, tn, D), lambda i: (0, j, 0)),
                pl.BlockSpec((1, 2 * tn), pair), pl.BlockSpec((SUBLANES, 2 * tn), pair), pl.BlockSpec((1, 2 * tn), pair)]
    operands = [xb, w_up, w_down, b_up, w_dw, b_dw]
    wide = pl.BlockSpec((tm, 2 * tn), lambda i: (i, j))
    out_specs = [wide, wide, pl.BlockSpec((tm, tn), lambda i: (i, j))]
    out_shape = [jax.ShapeDtypeStruct((T, N), _HDT), jax.ShapeDtypeStruct((T, N), _HDT),
                 jax.ShapeDtypeStruct((T, N // 2), _MXU)]
    aliases = {}
    if last:
        res, _, b_down, g, b = tail
        in_specs += [ANY, ANY, ANY, tile, tile, vec, vec, vec]
        operands += list(prev) + [res, b_down, g, b]
        aliases = {6: 0, 7: 1, 8: 2}
        out_specs += [tile, tile, tile, pl.BlockSpec((tm, 1), row)]
        out_shape += [jax.ShapeDtypeStruct((T, D), F32), jax.ShapeDtypeStruct((T, D), _MXU),
                      jax.ShapeDtypeStruct((T, D), F32), jax.ShapeDtypeStruct((T, 1), F32)]
    else:
        out_specs.append(tile)
        out_shape.append(jax.ShapeDtypeStruct((T, D), F32))
    return pl.pallas_call(
        body, name=name, grid=(T // tm,), in_specs=in_specs, out_specs=out_specs, out_shape=out_shape,
        input_output_aliases=aliases, scratch_shapes=[pltpu.VMEM((SUBLANES, 2 * tn), F32)],
        compiler_params=_cp(("arbitrary",)),
    )(*operands)


def _ffn_bwd_half(j, dzb, w_down, w_up, hs, hcs, w_dw, *, S, name, dz=None, alpha=None, prev=None, ln=None):
    T, D = dzb.shape
    N = hs.shape[1]
    tn = N // N_CHIPS
    tm = _tile(S, 256)
    spt = S // tm
    nt = T // tm
    last = prev is not None

    def body(*refs):
        dz_ref, wd_ref, wu_ref, h_ref, hc_ref, wc_ref = refs[:6]
        if last:
            dxp_ref, xh_ref, rs_ref, g_ref = refs[7:11]
            dh_ref, cs_ref, dw_ref, db_ref, dz1_ref, dz1b_ref, dg1_ref, db1_ref, cs1_ref, carry_ref = refs[11:21]
        else:
            dzf_ref = refs[6]
            dh_ref, cs_ref, dw_ref, db_ref, dxp_ref, carry_ref = refs[7:13]
        i = pl.program_id(0)
        ii = nt - 1 - i

        @pl.when(i == 0)
        def _():
            cs_ref[...] = jnp.zeros_like(cs_ref)
            dw_ref[...] = jnp.zeros_like(dw_ref)
            db_ref[...] = jnp.zeros_like(db_ref)
            if last:
                dg1_ref[...] = jnp.zeros_like(dg1_ref)
                db1_ref[...] = jnp.zeros_like(db1_ref)
                cs1_ref[...] = jnp.zeros_like(cs1_ref)

        df = lax.dot_general(dz_ref[...].astype(_MXU), wd_ref[...].astype(_MXU), (((1,), (1,)), ((), ())),
                             preferred_element_type=F32)
        h = h_ref[...].astype(F32)
        gte, val = hc_ref[:, :tn].astype(F32), hc_ref[:, tn:].astype(F32)
        sig = _sigmoid(gte)
        dval = df * (gte * sig)
        dg = df * val * (sig * (1.0 + gte * (1.0 - sig)))
        dhc = jnp.concatenate([dg, dval], axis=1)
        nxt = jnp.where((ii + 1) % spt == 0, 0.0, carry_ref[...])
        d1 = _shift_up(dhc, nxt, 1)
        d2 = _shift_up(dhc, nxt, 2)
        carry_ref[...] = dhc[0:SUBLANES]
        db_ref[...] += _fold8(dhc)
        dw_ref[2] += _fold8(dhc * h)
        dw_ref[1] += _fold8(d1 * h)
        dw_ref[0] += _fold8(d2 * h)
        dh = wc_ref[pl.ds(2, 1), :] * dhc + wc_ref[pl.ds(1, 1), :] * d1 + wc_ref[pl.ds(0, 1), :] * d2
        cs_ref[...] += _fold8(dh)
        dhb = dh.astype(_MXU)
        dh_ref[...] = dhb
        dx = lax.dot_general(dhb, wu_ref[...].astype(_MXU), (((1,), (1,)), ((), ())), preferred_element_type=F32)
        if not last:
            dxp_ref[...] = dx + alpha * dzf_ref[...]
            return
        d = dx + dxp_ref[...]
        xh = xh_ref[...]
        dz1 = _ln_bwd_rows(d, xh, rs_ref[...], g_ref[...])
        dz1_ref[...] = dz1
        dz1b_ref[...] = dz1.astype(_MXU)
        dg1_ref[...] += _fold8(d * xh)
        db1_ref[...] += _fold8(d)
        cs1_ref[...] += _fold8(dz1)

    rev = lambda i: (nt - 1 - i, 0)
    fixed = lambda i: (0, 0)
    pair = lambda i: (0, j)
    tile = pl.BlockSpec((tm, D), rev)
    wide = pl.BlockSpec((tm, 2 * tn), lambda i: (nt - 1 - i, j))
    part = pl.BlockSpec((SUBLANES, 2 * tn), fixed)
    in_specs = [tile, _resident((None, tn, D), lambda i: (0, j, 0)), _resident((None, D, 2 * tn), lambda i: (0, 0, j)),
                wide, wide, pl.BlockSpec((SUBLANES, 2 * tn), pair)]
    operands = [dzb, w_down, w_up, hs, hcs, w_dw]
    out_specs = [wide, part, pl.BlockSpec((3, SUBLANES, 2 * tn), lambda i: (0, 0, 0)), part]
    out_shape = [jax.ShapeDtypeStruct((T, N), _MXU), jax.ShapeDtypeStruct((SUBLANES, 2 * tn), F32),
                 jax.ShapeDtypeStruct((3, SUBLANES, 2 * tn), F32), jax.ShapeDtypeStruct((SUBLANES, 2 * tn), F32)]
    aliases = {}
    if last:
        xh, rstd, g = ln
        in_specs += [ANY, tile, tile, pl.BlockSpec((tm, 1), rev), pl.BlockSpec((1, D), fixed)]
        operands += [prev[0], prev[1], xh, rstd, g]
        aliases = {6: 0}
        out_specs += [tile, tile] + [pl.BlockSpec((SUBLANES, D), fixed)] * 3
        out_shape += [jax.ShapeDtypeStruct((T, D), F32), jax.ShapeDtypeStruct((T, D), _MXU)] \
            + [jax.ShapeDtypeStruct((SUBLANES, D), F32)] * 3
    else:
        in_specs.append(tile)
        operands.append(dz)
        out_specs.append(tile)
        out_shape.append(jax.ShapeDtypeStruct((T, D), F32))
    return pl.pallas_call(
        body, name=name, grid=(nt,), in_specs=in_specs, out_specs=out_specs, out_shape=out_shape,
        input_output_aliases=aliases, scratch_shapes=[pltpu.VMEM((SUBLANES, 2 * tn), F32)],
        compiler_params=_cp(("arbitrary",)),
    )(*operands)


def _sum_pieces(g, r, me, layer, acc, n_layers, *, name):
    _, pr, pc = g.shape
    tr = _tile(pr, 128)

    def body(me_ref, g_ref, r_ref, *rest):
        o_ref = rest[-1]
        total = g_ref[...].astype(F32)
        for s in range(N_DEV - 1):
            total = total + r_ref[s].astype(F32)
        o_ref[...] = total

    in_specs = [pl.BlockSpec((None, tr, pc), lambda i, me_ref: (me_ref[0], i, 0)),
                pl.BlockSpec((N_DEV - 1, tr, pc), lambda i, me_ref: (0, i, 0))]
    operands = [me, g, r]
    aliases = {}
    if acc is not None:
        in_specs.append(ANY)
        operands.append(acc)
        aliases = {3: 0}
    return pl.pallas_call(
        body, name=name,
        grid_spec=pltpu.PrefetchScalarGridSpec(
            num_scalar_prefetch=1, grid=(pr // tr,), in_specs=in_specs,
            out_specs=pl.BlockSpec((None, tr, pc), lambda i, me_ref: (layer, i, 0))),
        out_shape=jax.ShapeDtypeStruct((n_layers, pr, pc), F32),
        input_output_aliases=aliases,
        compiler_params=_cp(("parallel",)),
    )(*operands)


def _adam_math(w, g, m, v):
    bc1 = 1.0 - ADAM_B1 ** ADAM_STEP
    bc2 = 1.0 - ADAM_B2 ** ADAM_STEP
    m = ADAM_B1 * m + (1.0 - ADAM_B1) * g
    v = ADAM_B2 * v + (1.0 - ADAM_B2) * (g * g)
    return -ADAM_LR * ((m / bc1) / (jnp.sqrt(v / bc2) + ADAM_EPS) + ADAM_WD * w), m, v


def _adam(w, g, m, v, *, name):
    R, C = w.shape
    tr = _tile(R, 256)

    def body(w_ref, g_ref, m_ref, v_ref, d_ref, mo_ref, vo_ref):
        d_ref[...], mo_ref[...], vo_ref[...] = _adam_math(w_ref[...], g_ref[...], m_ref[...], v_ref[...])

    spec = pl.BlockSpec((tr, C), lambda i: (i, 0))
    return pl.pallas_call(
        body, name=name, grid=(R // tr,), in_specs=[spec] * 4, out_specs=[spec] * 3,
        out_shape=[jax.ShapeDtypeStruct((R, C), F32)] * 3,
        compiler_params=_cp(("parallel",)),
    )(w, g, m, v)


def _adam_halves(w, own, got, m, v, core, *, name):
    L, R, C = w.shape
    rh = R // 2
    tr = _tile(rh, 256)
    nt = rh // tr

    def body(c_ref, w_ref, own_ref, got_ref, m_ref, v_ref, g_ref, d_ref, mo_ref, vo_ref):
        g = jnp.where(pl.program_id(1) == c_ref[0], own_ref[...], got_ref[...])
        g_ref[...] = g
        d_ref[...], mo_ref[...], vo_ref[...] = _adam_math(w_ref[...], g, m_ref[...], v_ref[...])

    full = pl.BlockSpec((None, tr, C), lambda l, h, t, c_ref: (l, h * nt + t, 0))
    half = pl.BlockSpec((None, tr, C), lambda l, h, t, c_ref: (l, t, 0))
    return pl.pallas_call(
        body, name=name,
        grid_spec=pltpu.PrefetchScalarGridSpec(
            num_scalar_prefetch=1, grid=(L, 2, nt), in_specs=[full, half, half, full, full], out_specs=[full] * 4),
        out_shape=[jax.ShapeDtypeStruct((L, R, C), F32)] * 4,
        compiler_params=_cp(("parallel", "parallel", "parallel")),
    )(core, w, own, got, m, v)


def _remote(src, dst, send, recv, dev):
    return pltpu.make_async_remote_copy(src_ref=src, dst_ref=dst, send_sem=send, recv_sem=recv,
                                        device_id=dev, device_id_type=MESH)


def _place_w(shard, pos, layer, *, axis, name):
    _, R, C = shard.shape
    tr = _tile(R, 512, 16)
    nt = R // tr
    if axis == 2:
        out_shape = (1, R, N_CHIPS * C)
        out_map = lambda t, q: (0, t, q[0])
    else:
        out_shape = (1, N_CHIPS * R, C)
        out_map = lambda t, q: (0, q[0] * nt + t, 0)

    def body(q_ref, s_ref, o_ref):
        o_ref[...] = s_ref[...].astype(_WIRE)

    return pl.pallas_call(
        body, name=name,
        grid_spec=pltpu.PrefetchScalarGridSpec(
            num_scalar_prefetch=1, grid=(nt,),
            in_specs=[pl.BlockSpec((None, tr, C), lambda t, q: (layer, t, 0))],
            out_specs=pl.BlockSpec((None, tr, C), out_map)),
        out_shape=jax.ShapeDtypeStruct(out_shape, _WIRE),
        compiler_params=_cp(("parallel",)),
    )(pos, shard)


def _ag_window(ref, kind, px, py, h):
    axis, perm = kind
    q = 2 * px + py
    if perm:
        q = _perm_idx(q)
    if axis == 2:
        R, C = ref.shape[1], ref.shape[2] // N_CHIPS
        rh = R // 2
        return ref.at[:, pl.ds(pl.multiple_of(h * rh, 16), rh), pl.ds(pl.multiple_of(q * C, LANES), C)]
    R = ref.shape[1] // N_CHIPS
    rh = R // 2
    return ref.at[:, pl.ds(pl.multiple_of(q * R + h * rh, 16), rh), :]


def _ag_ici_copies(refs, kinds, send, recv):
    x, y, c = lax.axis_index("x"), lax.axis_index("y"), lax.axis_index("c")
    chips = [(1 - x, y), (x, 1 - y), (1 - x, 1 - y)]
    sends, recvs = [], []
    for a, (ref, kind) in enumerate(zip(refs, kinds)):
        own = _ag_window(ref, kind, x, y, c)
        for i, (px, py) in enumerate(chips):
            k = 3 * a + i
            sends.append(_remote(own, own, send.at[k], recv.at[k], (px, py, c)))
            recvs.append(_remote(own, _ag_window(ref, kind, px, py, c), send.at[k], recv.at[k], (px, py, c)))
    return sends, recvs


def _ag_start(arrs, kinds, after, *, name):
    n = len(arrs)

    def body(*refs):
        in_refs = refs[:n]
        send, recv = refs[n + len(after)], refs[n + len(after) + 1]
        token = refs[-1]
        sends, _ = _ag_ici_copies(in_refs, kinds, send, recv)
        for cp in sends:
            cp.start()
        token[...] = jnp.zeros_like(token)

    sems = pltpu.SemaphoreType.DMA((3 * n,))
    out = pl.pallas_call(
        body, name=name,
        out_shape=(sems, sems) + tuple(pltpu.HBM(a.shape, a.dtype) for a in arrs)
        + (jax.ShapeDtypeStruct((SUBLANES, LANES), F32),),
        in_specs=(HBM,) * n + (ANY,) * len(after),
        out_specs=(SEMS, SEMS) + (HBM,) * n + (pl.BlockSpec(memory_space=pltpu.VMEM),),
        input_output_aliases={a: 2 + a for a in range(n)},
        compiler_params=pltpu.CompilerParams(has_side_effects=EFFECT),
    )(*[pltpu.with_memory_space_constraint(a, pltpu.HBM) for a in arrs], *after)
    return out[0], out[1], list(out[2:2 + n]), out[-1]


def _ag_wait(send, recv, arrs, kinds, after, *, name):
    n = len(arrs)

    def body(*refs):
        in_refs = refs[:n]
        send, recv = refs[n], refs[n + 1]
        sends, recvs = _ag_ici_copies(in_refs, kinds, send, recv)
        for cp in sends:
            cp.wait_send()
        for cp in recvs:
            cp.wait_recv()

    out = pl.pallas_call(
        body, name=name,
        out_shape=tuple(pltpu.HBM(a.shape, a.dtype) for a in arrs),
        in_specs=(HBM,) * n + (SEMS, SEMS) + (ANY,) * len(after), out_specs=(HBM,) * n,
        input_output_aliases={a: a for a in range(n)},
        compiler_params=pltpu.CompilerParams(has_side_effects=EFFECT),
    )(*arrs, send, recv, *after)
    return list(out)


def _ag_forward(arrs, kinds, *, name):
    n = len(arrs)

    def body(*refs):
        o_refs, send, recv = refs[n:2 * n], refs[2 * n], refs[2 * n + 1]
        x, y, c = lax.axis_index("x"), lax.axis_index("y"), lax.axis_index("c")
        chips = [(1 - x, y), (x, 1 - y), (1 - x, 1 - y)]
        sib = (x, y, 1 - c)
        sends, recvs = [], []
        for a, (ref, kind) in enumerate(zip(o_refs, kinds)):
            for i, (px, py) in enumerate(chips):
                k = 3 * a + i
                got = _ag_window(ref, kind, px, py, c)
                cp = _remote(got, got, send.at[k], recv.at[k], sib)
                cp.start()
                sends.append(cp)
                recvs.append(_remote(got, _ag_window(ref, kind, px, py, 1 - c), send.at[k], recv.at[k], sib))
        for cp in recvs:
            cp.wait_recv()
        for cp in sends:
            cp.wait_send()

    out = pl.pallas_call(
        body, name=name, in_specs=[ANY] * n, out_specs=[ANY] * n,
        out_shape=[jax.ShapeDtypeStruct(a.shape, a.dtype) for a in arrs],
        input_output_aliases={a: a for a in range(n)},
        scratch_shapes=[pltpu.SemaphoreType.DMA((3 * n,)), pltpu.SemaphoreType.DMA((3 * n,))],
    )(*arrs)
    return list(out)


def _flip(x, y, c, f):
    return ((1 - x) if f & 4 else x, (1 - y) if f & 2 else y, (1 - c) if f & 1 else c)


def _rs_copies(g_ref, land_ref, send, recv):
    x, y, c = lax.axis_index("x"), lax.axis_index("y"), lax.axis_index("c")
    cps = []
    for f in range(1, N_DEV):
        tx, ty, tcx = _flip(x, y, c, f)
        cps.append(_remote(g_ref.at[4 * tx + 2 * ty + tcx], land_ref.at[f - 1], send.at[f - 1], recv.at[f - 1],
                           (tx, ty, tcx)))
    return cps


def _rs_start(g, *, name):
    _, pr, pc = g.shape
    land_shape = (N_DEV - 1, pr, pc)

    def body(g_ref, land_ref, send, recv, g_thru, land_thru, token):
        for cp in _rs_copies(g_ref, land_ref, send, recv):
            cp.start()
        token[...] = jnp.zeros_like(token)

    sems = pltpu.SemaphoreType.DMA((N_DEV - 1,))
    return pl.pallas_call(
        body, name=name,
        out_shape=(sems, sems, pltpu.HBM(g.shape, g.dtype), pltpu.HBM(land_shape, g.dtype),
                   jax.ShapeDtypeStruct((SUBLANES, LANES), F32)),
        in_specs=(HBM, HBM), out_specs=(SEMS, SEMS, HBM, HBM, pl.BlockSpec(memory_space=pltpu.VMEM)),
        input_output_aliases={0: 2, 1: 3},
        compiler_params=pltpu.CompilerParams(has_side_effects=EFFECT),
    )(pltpu.with_memory_space_constraint(g, pltpu.HBM),
      pltpu.with_memory_space_constraint(lax.empty(land_shape, g.dtype), pltpu.HBM))


def _rs_wait(send, recv, g_thru, land_thru, after, *, name):
    def body(g_ref, land_ref, send, recv, after_ref, g_out, land_out):
        cps = _rs_copies(g_ref, land_ref, send, recv)
        for cp in cps:
            cp.wait_send()
        for cp in cps:
            cp.wait_recv()

    return pl.pallas_call(
        body, name=name,
        out_shape=(pltpu.HBM(g_thru.shape, g_thru.dtype), pltpu.HBM(land_thru.shape, land_thru.dtype)),
        in_specs=(HBM, HBM, SEMS, SEMS, ANY), out_specs=(HBM, HBM), input_output_aliases={0: 0, 1: 1},
        compiler_params=pltpu.CompilerParams(has_side_effects=EFFECT),
    )(g_thru, land_thru, send, recv, after)


def _pair_exchange(own, *, name):
    def body(own_ref, got_ref, send, recv):
        x, y, c = lax.axis_index("x"), lax.axis_index("y"), lax.axis_index("c")
        cp = _remote(own_ref, got_ref, send, recv, (x, y, 1 - c))
        cp.start()
        cp.wait_recv()
        cp.wait_send()

    return pl.pallas_call(
        body, name=name, in_specs=[ANY], out_specs=ANY, out_shape=jax.ShapeDtypeStruct(own.shape, own.dtype),
        scratch_shapes=[pltpu.SemaphoreType.DMA, pltpu.SemaphoreType.DMA],
    )(own)


def _allreduce_flat(vec, *, name):
    n = vec.shape[0]
    unit = N_DEV * SUBLANES * LANES
    npad = -(-n // unit) * unit
    rows = npad // (N_DEV * LANES)
    xin = jnp.pad(vec, (0, npad - n)).reshape(N_DEV, rows, LANES)

    def body(x_ref, y_ref, a_ref, send_a, recv_a, send_b, recv_b):
        x, y, c = lax.axis_index("x"), lax.axis_index("y"), lax.axis_index("c")
        me = 4 * x + 2 * y + c
        a_ref[me] = x_ref[me]
        sends, recvs = [], []
        for f in range(1, N_DEV):
            dev = _flip(x, y, c, f)
            t = 4 * dev[0] + 2 * dev[1] + dev[2]
            cp = _remote(x_ref.at[t], a_ref.at[me], send_a.at[f - 1], recv_a.at[f - 1], dev)
            cp.start()
            sends.append(cp)
            recvs.append(_remote(x_ref.at[me], a_ref.at[t], send_a.at[f - 1], recv_a.at[f - 1], dev))
        for cp in recvs:
            cp.wait_recv()
        for cp in sends:
            cp.wait_send()
        acc = a_ref[0]
        for s in range(1, N_DEV):
            acc = acc + a_ref[s]
        y_ref[me] = acc
        sends, recvs = [], []
        for f in range(1, N_DEV):
            dev = _flip(x, y, c, f)
            t = 4 * dev[0] + 2 * dev[1] + dev[2]
            cp = _remote(y_ref.at[me], y_ref.at[me], send_b.at[f - 1], recv_b.at[f - 1], dev)
            cp.start()
            sends.append(cp)
            recvs.append(_remote(y_ref.at[me], y_ref.at[t], send_b.at[f - 1], recv_b.at[f - 1], dev))
        for cp in recvs:
            cp.wait_recv()
        for cp in sends:
            cp.wait_send()

    vm = pl.BlockSpec(memory_space=pltpu.VMEM)
    out = pl.pallas_call(
        body, name=name, in_specs=[vm], out_specs=vm,
        out_shape=jax.ShapeDtypeStruct((N_DEV, rows, LANES), F32),
        scratch_shapes=[pltpu.VMEM((N_DEV, rows, LANES), F32)] + [pltpu.SemaphoreType.DMA((N_DEV - 1,))] * 4,
        compiler_params=_cp(),
    )(xin)
    return out.reshape(npad)[:n]


def _perm_cols(v, blocks=N_CHIPS):
    lead, n = v.shape[:-1], v.shape[-1]
    return v.reshape(lead + (blocks, n // blocks))[..., PERM, :].reshape(lead + (n,))


def _pack(arrs):
    return jnp.concatenate([a.reshape(-1).astype(F32) for a in arrs])


def _unpack(flat, shapes):
    out, pos = [], 0
    for s in shapes:
        n = 1
        for d in s:
            n *= d
        out.append(flat[pos:pos + n].reshape(s))
        pos += n
    return out


def kernel(x, conv_w_in, conv_b_in, conv_w_dw, conv_b_dw, conv_ln_g, conv_ln_b, conv_w_out, conv_b_out, gmlp_w_in, gmlp_b_in, gmlp_ln_g, gmlp_ln_b, gmlp_w_s, gmlp_b_s, gmlp_w_out, gmlp_b_out, ffn_w_up, ffn_b_up, ffn_w_dw, ffn_b_dw, ffn_w_down, ffn_b_down, norm1_g, norm1_b, norm2_g, norm2_b, loss_target, m_conv_w_in, m_conv_b_in, m_conv_w_dw, m_conv_b_dw, m_conv_ln_g, m_conv_ln_b, m_conv_w_out, m_conv_b_out, m_gmlp_w_in, m_gmlp_b_in, m_gmlp_ln_g, m_gmlp_ln_b, m_gmlp_w_s, m_gmlp_b_s, m_gmlp_w_out, m_gmlp_b_out, m_ffn_w_up, m_ffn_b_up, m_ffn_w_dw, m_ffn_b_dw, m_ffn_w_down, m_ffn_b_down, m_norm1_g, m_norm1_b, m_norm2_g, m_norm2_b, v_conv_w_in, v_conv_b_in, v_conv_w_dw, v_conv_b_dw, v_conv_ln_g, v_conv_ln_b, v_conv_w_out, v_conv_b_out, v_gmlp_w_in, v_gmlp_b_in, v_gmlp_ln_g, v_gmlp_ln_b, v_gmlp_w_s, v_gmlp_b_s, v_gmlp_w_out, v_gmlp_b_out, v_ffn_w_up, v_ffn_b_up, v_ffn_w_dw, v_ffn_b_dw, v_ffn_w_down, v_ffn_b_down, v_norm1_g, v_norm1_b, v_norm2_g, v_norm2_b):
    P = dict(locals())
    WEIGHTS = ['conv_w_in', 'conv_b_in', 'conv_w_dw', 'conv_b_dw', 'conv_ln_g', 'conv_ln_b', 'conv_w_out',
               'conv_b_out', 'gmlp_w_in', 'gmlp_b_in', 'gmlp_ln_g', 'gmlp_ln_b', 'gmlp_w_s', 'gmlp_b_s',
               'gmlp_w_out', 'gmlp_b_out', 'ffn_w_up', 'ffn_b_up', 'ffn_w_dw', 'ffn_b_dw', 'ffn_w_down',
               'ffn_b_down', 'norm1_g', 'norm1_b', 'norm2_g', 'norm2_b']
    BIG = ['conv_w_in', 'conv_w_out', 'gmlp_w_in', 'gmlp_w_out', 'ffn_w_up', 'ffn_w_down']
    SMALL_SHARDED = {'conv_w_dw': 2, 'gmlp_b_in': 1, 'gmlp_ln_g': 1, 'gmlp_ln_b': 1, 'gmlp_b_out': 1, 'ffn_w_dw': 2}

    B, S, D = x.shape
    T = B * S
    depth = norm1_g.shape[0]
    alpha = (2.0 * depth) ** 0.25
    C = conv_w_out.shape[-1]
    F2 = ffn_b_up.shape[-1]
    G, L = gmlp_w_s.shape[1], gmlp_w_s.shape[2]
    xi, yi, ci = lax.axis_index("x"), lax.axis_index("y"), lax.axis_index("c")
    shard = 2 * xi + yi

    i32 = lambda v: jnp.reshape(v, (1,)).astype(jnp.int32)
    pos_plain, pos_perm = i32(shard), i32(_perm_idx(shard))
    me_id, core_id = i32(4 * xi + 2 * yi + ci), i32(ci)

    groups = []
    for i in range(depth):
        mix = 'conv' if i % 2 == 0 else 'gmlp'
        groups.append((f"{mix}{i // 2}", [(mix + '_w_in', i // 2, 2, True), (mix + '_w_out', i // 2, 1, False)]))
        groups.append((f"ffn{i}", [('ffn_w_up', i, 2, True), ('ffn_w_down', i, 1, False)]))
    sm_names = list(SMALL_SHARDED)
    sm_shapes = [P[n].shape for n in sm_names]
    mine = _pack([P[n] for n in sm_names]) * (ci == 0).astype(F32)
    buf = jnp.zeros((N_CHIPS, mine.shape[0]), F32)
    buf = lax.dynamic_update_slice(buf, mine[None], (shard, 0))
    gathered = _allreduce_flat(buf.reshape(-1), name="ag_small").reshape(N_CHIPS, -1)

    started, order = {}, [gathered]
    for gname, members in groups:
        placed = [_place_w(P[n], pos_perm if perm else pos_plain, l, axis=axis, name=f"place_{n}_{l}")
                  for n, l, axis, perm in members]
        kinds = [(axis, perm) for _, _, axis, perm in members]
        send, recv, arrs, token = _ag_start(placed, kinds, order, name=f"ag_start_{gname}")
        order = [token]
        started[gname] = (send, recv, arrs, kinds, [(n, l) for n, l, _, _ in members])
    wts = {}

    def arrive(gname, after):
        send, recv, arrs, kinds, keys = started[gname]
        arrs = _ag_wait(send, recv, arrs, kinds, after, name=f"ag_wait_{gname}")
        arrs = _ag_forward(arrs, kinds, name=f"ag_fwd_{gname}")
        wts.update(zip(keys, arrs))

    full = {}
    for n, parts in zip(sm_names, zip(*[_unpack(gathered[k], sm_shapes) for k in range(N_CHIPS)])):
        full[n] = jnp.concatenate(parts, axis=SMALL_SHARDED[n])
    for n in WEIGHTS:
        if n not in BIG and n not in full:
            full[n] = P[n]

    assert G * L == C, "a gMLP group must be as wide as a chunk is long"

    def row(v):
        return v.reshape(1, -1)

    def pad_rows(v, r):
        return jnp.pad(v, ((0, r - v.shape[0]), (0, 0)))

    xf = x.reshape(T, D)
    saved = []
    cur, cur_b = xf, xf.astype(_MXU)
    for i in range(depth):
        j = i // 2
        sv = {'x': cur, 'xb': cur_b}
        arrive(groups[2 * i][0], order if i == 0 else [cur_b])
        if i % 2 == 0:
            b_in = row(_perm_cols(full['conv_b_in'][j]))
            h1 = _mm(cur_b, wts['conv_w_in', j], bl=0, bias=b_in, tm=_tile(T, 512), tn=_tile(2 * C, 1024, LANES),
                     tk=D, name=f"conv_in_{j}", n_outer=True, out_dtype=_ADT)
            wdw = pad_rows(full['conv_w_dw'][j], CONV_TAPS_PAD)
            dwo = _conv_fwd(h1, wdw, row(full['conv_b_dw'][j]), B=B, S=S, name=f"conv_dw_{j}")
            s_act, xhc, rsc, *y1 = _conv_tail_fwd(
                dwo, row(full['conv_ln_g'][j]), row(full['conv_ln_b'][j]), wts['conv_w_out', j],
                row(full['conv_b_out'][j]), cur, alpha, row(norm1_g[i]), row(norm1_b[i]), name=f"conv_out_ln_{j}")
            sv.update(h1=h1, wdw=wdw, act=s_act, xhc=xhc, rsc=rsc)
        else:
            b_in = row(_perm_cols(full['gmlp_b_in'][j]))
            pre = _mm(cur_b, wts['gmlp_w_in', j], bl=0, bias=b_in, tm=_tile(T, 512), tn=_tile(2 * C, 1024, LANES),
                      tk=D, name=f"gmlp_in_{j}", n_outer=True, out_dtype=_ADT)
            bsb = jnp.repeat(gmlp_b_s[j].T, L, axis=1)
            us, xhv, rsv, *y1 = _gmlp_gate_fwd(
                pre, row(full['gmlp_ln_g'][j]), row(full['gmlp_ln_b'][j]), gmlp_w_s[j], bsb, wts['gmlp_w_out', j],
                row(full['gmlp_b_out'][j]), cur, alpha, row(norm1_g[i]), row(norm1_b[i]), name=f"gmlp_gate_{j}")
            sv.update(pre=pre, bsb=bsb, act=us, xhv=xhv, rsv=rsv)
        x1, x1b, xh1, rs1 = y1
        arrive(groups[2 * i + 1][0], [x1b])
        wdw3 = pad_rows(_perm_cols(full['ffn_w_dw'][i]), SUBLANES)
        bdw3 = row(_perm_cols(ffn_b_dw[i]))
        ffn_in = (x1b, wts['ffn_w_up', i], wts['ffn_w_down', i], row(_perm_cols(ffn_b_up[i])), wdw3, bdw3)
        first = _ffn_fwd_half(0, *ffn_in, S=S, name=f"ffn_fwd_a_{i}")
        hs, hcs, f_act, x2, x2b, xh2, rs2 = _ffn_fwd_half(
            1, *ffn_in, S=S, name=f"ffn_fwd_b_{i}", prev=first,
            tail=(x1, alpha, row(ffn_b_down[i]), row(norm2_g[i]), row(norm2_b[i])))
        sv.update(x1=x1, x1b=x1b, xh1=xh1, rs1=rs1, hs=hs, hcs=hcs, f=f_act, wdw3=wdw3, xh2=xh2, rs2=rs2)
        saved.append(sv)
        cur, cur_b = x2, x2b

    sg = {n: [None] * full[n].shape[0] for n in WEIGHTS if n not in BIG}
    inflight = {n: [None] * P[n].shape[0] for n in BIG}
    deps = []
    tgt = loss_target.reshape(T, D)
    dcur = None
    loss_part = None
    tk_t = _tile(T, 2048)

    def wgrad(n, l, a_, b_, **kw):
        g = _mm(a_, b_, ta=True, out_dtype=_WIRE, tk=tk_t, name=f"{n}_dw_{l}", deps=deps, **kw)
        send, recv, g_thru, land, token = _rs_start(g, name=f"rs_start_{n}_{l}")
        inflight[n][l] = (send, recv, g_thru, land)
        deps.append(token)

    for i in reversed(range(depth)):
        j = i // 2
        sv = saved[i]
        if i == depth - 1:
            dz2, dz2b, dg, db, cs, loss_part = _ln_bwd(cur, sv['xh2'], sv['rs2'], row(norm2_g[i]), target=tgt,
                                                       name=f"ln2_bwd_head_{i}")
        else:
            dz2, dz2b, dg, db, cs = dcur
        sg['norm2_g'][i], sg['norm2_b'][i], sg['ffn_b_down'][i] = dg.sum(0), db.sum(0), cs.sum(0)
        Fh = F2 // 2
        wgrad('ffn_w_down', i, sv['f'], dz2b, tm=Fh // 2, tn=_tile(D, 1024, LANES), pieces=('row',))
        ffn_in = (dz2b, wts['ffn_w_down', i], wts['ffn_w_up', i], sv['hs'], sv['hcs'], sv['wdw3'])
        dh0, csu0, dwd0, dbd0, dxp = _ffn_bwd_half(0, *ffn_in, S=S, name=f"ffn_bwd_a_{i}", dz=dz2, alpha=alpha)
        dh, csu1, dwd1, dbd1, dz1, dz1b, dg, db, cs = _ffn_bwd_half(
            1, *ffn_in, S=S, name=f"ffn_bwd_b_{i}", prev=(dh0, dxp), ln=(sv['xh1'], sv['rs1'], row(norm1_g[i])))
        sg['ffn_b_up'][i] = _perm_cols(jnp.concatenate([csu0.sum(0), csu1.sum(0)], axis=-1))
        sg['ffn_w_dw'][i] = _perm_cols(jnp.concatenate([dwd0.sum(1), dwd1.sum(1)], axis=-1))
        sg['ffn_b_dw'][i] = _perm_cols(jnp.concatenate([dbd0.sum(0), dbd1.sum(0)], axis=-1))
        wgrad('ffn_w_up', i, sv['x1b'], dh, tm=D, tn=F2 // N_CHIPS, pieces=('col', True))
        sg['norm1_g'][i], sg['norm1_b'][i] = dg.sum(0), db.sum(0)
        if i % 2 == 0:
            sg['conv_b_out'][j] = cs.sum(0)
            wgrad('conv_w_out', j, sv['act'], dz1b, tm=_tile(C, 1024), tn=_tile(D, 1024, LANES), pieces=('row',))
            ddw, dg, db = _ln_silu_bwd(dz1b, wts['conv_w_out', j], sv['xhc'], sv['rsc'], row(full['conv_ln_g'][j]),
                                       row(full['conv_ln_b'][j]), name=f"conv_ln_bwd_{j}")
            sg['conv_ln_g'][j], sg['conv_ln_b'][j] = dg.sum(0), db.sum(0)
            dglu, dwk, dbk = _conv_bwd(ddw, sv['h1'], sv['wdw'], B=B, S=S, name=f"conv_dw_bwd_{j}")
            sg['conv_w_dw'][j] = dwk.sum(1)[:conv_w_dw.shape[1]]
            sg['conv_b_dw'][j] = dbk.sum(0)
            dh1, csi = _glu_bwd(dglu, sv['h1'], name=f"conv_glu_bwd_{j}")
            sg['conv_b_in'][j] = _perm_cols(csi.sum(0))
            fam = 'conv_w_in'
        else:
            sg['gmlp_b_out'][j] = cs.sum(0)
            wgrad('gmlp_w_out', j, sv['act'], dz1b, tm=_tile(C, 1024), tn=_tile(D, 1024, LANES), pieces=('row',))
            dh1, dg, db, csi, dws, dbs = _gmlp_gate_bwd(dz1b, wts['gmlp_w_out', j], sv['pre'], sv['xhv'], sv['rsv'],
                                                        row(full['gmlp_ln_g'][j]), row(full['gmlp_ln_b'][j]),
                                                        gmlp_w_s[j], sv['bsb'], name=f"gmlp_gate_bwd_{j}")
            sg['gmlp_ln_g'][j], sg['gmlp_ln_b'][j] = dg.sum(0), db.sum(0)
            sg['gmlp_b_in'][j] = _perm_cols(csi.sum(0))
            sg['gmlp_w_s'][j] = dws
            sg['gmlp_b_s'][j] = dbs.reshape(L, G, L).sum(-1).T
            fam = 'gmlp_w_in'
        wgrad(fam, j, sv['xb'], dh1, tm=D, tn=(2 * C) // N_CHIPS, pieces=('col', True))
        if i > 0:
            below = saved[i - 1]
            dcur = _mm_ln_bwd(dh1, wts[fam, j], dz1, alpha, below['xh2'], below['rs2'], row(norm2_g[i - 1]),
                              name=f"{fam}_dx_{j}", deps=deps)
        else:
            dcur = _mm(dh1, wts[fam, j], bl=0, tb=True, res=dz1, res_scale=alpha, tm=_tile(T, 512),
                       tn=_tile(D, 1024, LANES), tk=2 * C, name=f"{fam}_dx_{j}", deps=deps)
    grad_x = dcur.reshape(B, S, D)

    small_names = [n for n in WEIGHTS if n not in BIG]
    small_full = [jnp.stack(sg[n]) for n in small_names]
    flat = _pack(small_full + [loss_part])
    red = _allreduce_flat(flat, name="ar_small")
    red_parts = _unpack(red, [a.shape for a in small_full] + [loss_part.shape])
    loss = (0.5 / D) * jnp.sum(red_parts[-1])
    grads = {}
    for n, g in zip(small_names, red_parts[:-1]):
        if n in SMALL_SHARDED:
            ax = SMALL_SHARDED[n]
            width = P[n].shape[ax]
            g = lax.dynamic_slice_in_dim(g, shard * width, width, axis=ax)
        grads[n] = g

    big_out = {}
    for n in ['ffn_w_down', 'ffn_w_up', 'gmlp_w_out', 'gmlp_w_in', 'conv_w_out', 'conv_w_in']:
        own = None
        n_layers = len(inflight[n])
        for l in reversed(range(n_layers)):
            send, recv, g_thru, land = inflight[n][l]
            pc_, r = _rs_wait(send, recv, g_thru, land, dcur, name=f"rs_wait_{n}_{l}")
            own = _sum_pieces(pc_, r, me_id, l, own, n_layers, name=f"sum_{n}_{l}")
        got = _pair_exchange(own, name=f"px_{n}")
        big_out[n] = _adam_halves(P[n], own, got, P['m_' + n], P['v_' + n], core_id, name=f"adam_{n}")

    shapes = [P[n].shape for n in small_names]
    n_small = sum(functools.reduce(lambda p_, d_: p_ * d_, s_, 1) for s_ in shapes)
    unit = SUBLANES * LANES
    npad = -(-n_small // unit) * unit

    def flat2d(arrs, fill=0.0):
        v = _pack(arrs)
        return jnp.pad(v, (0, npad - n_small), constant_values=fill).reshape(-1, LANES)

    dl, mo, vo = _adam(flat2d([P[n] for n in small_names]), flat2d([grads[n] for n in small_names]),
                       flat2d([P['m_' + n] for n in small_names]),
                       flat2d([P['v_' + n] for n in small_names], fill=1.0), name="adam_small")
    small_out = {n: [grads[n], None, None, None] for n in small_names}
    for k, t in enumerate((dl, mo, vo)):
        for n, a in zip(small_names, _unpack(t.reshape(-1), shapes)):
            small_out[n][k + 1] = a

    outs = [loss, grad_x]
    for k in range(4):
        for n in WEIGHTS:
            outs.append(big_out[n][k] if n in BIG else small_out[n][k])
    return tuple(outs)
```

```python
import functools

import jax
import jax.numpy as jnp
from jax import lax
from jax.experimental import pallas as pl
from jax.experimental.pallas import tpu as pltpu

F32 = jnp.float32
_MXU = jnp.bfloat16
_WIRE = jnp.bfloat16
_HDT = jnp.bfloat16
_ADT = jnp.bfloat16
LN_EPS = 1e-5
ADAM_LR, ADAM_B1, ADAM_B2, ADAM_EPS, ADAM_WD, ADAM_STEP = 0.001, 0.9, 0.999, 1e-08, 0.01, 10
N_CHIPS = 4
N_DEV = 8
LANES = 128
SUBLANES = 8
CONV_TAPS_PAD = 32
VMEM_LIMIT = 56 << 20
MESH = pl.DeviceIdType.MESH
ANY = pl.BlockSpec(memory_space=pl.ANY)
HBM = pl.BlockSpec(memory_space=pltpu.HBM)
SEMS = pl.BlockSpec(memory_space=pltpu.SEMAPHORE)
EFFECT = pltpu.SideEffectType.DATAFLOW_SIDE_EFFECTING
PERM = (0, 2, 1, 3)


def _cp(sem=None):
    return pltpu.CompilerParams(dimension_semantics=sem, vmem_limit_bytes=VMEM_LIMIT)


def _tile(dim, pref, mult=SUBLANES):
    if dim <= pref:
        return dim
    t = (pref // mult) * mult
    while t > mult and dim % t:
        t -= mult
    assert dim % t == 0, (dim, pref, mult)
    return t


def _perm_idx(q):
    return (q % 2) * 2 + q // 2


def _fold8(t):
    r, n = t.shape
    return t.reshape(r // SUBLANES, SUBLANES, n).sum(axis=0)


def _ln_rows(z, g, b):
    mu = jnp.mean(z, axis=-1, keepdims=True)
    xc = z - mu
    var = jnp.mean(xc * xc, axis=-1, keepdims=True)
    rstd = lax.rsqrt(var + LN_EPS)
    xh = xc * rstd
    return xh * g + b, xh, rstd


def _ln_bwd_rows(dy, xh, rstd, g):
    dxh = dy * g
    m1 = jnp.mean(dxh, axis=-1, keepdims=True)
    m2 = jnp.mean(dxh * xh, axis=-1, keepdims=True)
    return rstd * (dxh - m1 - xh * m2)


def _sigmoid(v):
    return 0.5 * jnp.tanh(0.5 * v) + 0.5


def _gelu_parts(p):
    cdf = 0.5 * (1.0 + lax.erf(p * 0.7071067811865476))
    pdf = jnp.exp(-0.5 * p * p) * 0.3989422804014327
    return p * cdf, cdf + p * pdf


def _shift_down(prev8, t, s):
    ext = jnp.concatenate([prev8, t], axis=0)
    return pltpu.roll(ext, s, 0)[SUBLANES:]


def _shift_up(t, next8, s):
    n = t.shape[0]
    ext = jnp.concatenate([t, next8], axis=0)
    return pltpu.roll(ext, n + SUBLANES - s, 0)[:n]


def _mm(a, b, *, ta=False, tb=False, bl=None, bias=None, res=None, res_scale=1.0, out_dtype=F32,
        tm, tn, tk, name, pieces=None, deps=None, n_outer=False):
    M, K = (a.shape[1], a.shape[0]) if ta else a.shape
    bs = b.shape[1:] if bl is not None else b.shape
    N, Kb = (bs[0], bs[1]) if tb else (bs[1], bs[0])
    assert K == Kb and M % tm == 0 and N % tn == 0 and K % tk == 0, (a.shape, b.shape, tm, tn, tk)
    gm, gn, gk = M // tm, N // tn, K // tk

    def spec(block, imap):
        if n_outer:
            return pl.BlockSpec(block, lambda j, i, k: imap(i, j, k))
        return pl.BlockSpec(block, imap)

    a_spec = spec((tk, tm), lambda i, j, k: (k, i)) if ta else spec((tm, tk), lambda i, j, k: (i, k))
    bblk = (tn, tk) if tb else (tk, tn)
    bmap = (lambda i, j, k: (j, k)) if tb else (lambda i, j, k: (k, j))
    if bl is not None:
        b_spec = spec((None,) + bblk, lambda i, j, k: (bl,) + bmap(i, j, k))
    else:
        b_spec = spec(bblk, bmap)
    in_specs, operands = [a_spec, b_spec], [a, b]
    if bias is not None:
        in_specs.append(spec((1, tn), lambda i, j, k: (0, j)))
        operands.append(bias)
    if res is not None:
        in_specs.append(spec((tm, tn), lambda i, j, k: (i, j)))
        operands.append(res)
    n_dep = len(deps) if deps else 0
    if n_dep:
        in_specs += [ANY] * n_dep
        operands += deps
        del deps[:]
    if pieces is None:
        out_shape = jax.ShapeDtypeStruct((M, N), out_dtype)
        out_spec = spec((tm, tn), lambda i, j, k: (i, j))
        ppb = pr = None
    elif pieces[0] == 'col':
        pr, pc = M // 2, N // N_CHIPS
        assert tm % pr == 0 and pc % tn == 0
        ppb, per = tm // pr, pc // tn
        perm = pieces[1]
        out_shape = jax.ShapeDtypeStruct((N_DEV, pr, pc), out_dtype)
        out_spec = spec(
            (ppb, pr, tn),
            lambda i, j, k: ((2 * (_perm_idx(j // per) if perm else j // per)) // ppb + i, 0, j % per))
    else:
        pr = M // N_DEV
        assert tm % pr == 0
        ppb = tm // pr
        out_shape = jax.ShapeDtypeStruct((N_DEV, pr, N), out_dtype)
        out_spec = spec((ppb, pr, tn), lambda i, j, k: (i, 0, j))
    dims = (((0 if ta else 1,), (1 if tb else 0,)), ((), ()))

    def body(*refs):
        a_ref, b_ref = refs[0], refs[1]
        pos = 2
        bias_ref = res_ref = None
        if bias is not None:
            bias_ref = refs[pos]
            pos += 1
        if res is not None:
            res_ref = refs[pos]
            pos += 1
        pos += n_dep
        o_ref = refs[pos]

        def finish(r):
            if bias_ref is not None:
                r = r + bias_ref[...]
            if res_ref is not None:
                r = r + res_scale * res_ref[...]
            if pieces is not None:
                r = r.reshape(ppb, pr, tn)
            o_ref[...] = r.astype(out_dtype)

        part = lax.dot_general(a_ref[...].astype(_MXU), b_ref[...].astype(_MXU), dims, preferred_element_type=F32)
        if gk == 1:
            finish(part)
            return
        acc_ref = refs[pos + 1]
        k = pl.program_id(2)

        @pl.when(k == 0)
        def _():
            acc_ref[...] = part

        @pl.when((k > 0) & (k < gk - 1))
        def _():
            acc_ref[...] += part

        @pl.when(k == gk - 1)
        def _():
            finish(acc_ref[...] + part)

    return pl.pallas_call(
        body, name=name, grid=(gn, gm, gk) if n_outer else (gm, gn, gk), in_specs=in_specs, out_specs=out_spec,
        out_shape=out_shape, scratch_shapes=[pltpu.VMEM((tm, tn), F32)] if gk > 1 else [],
        compiler_params=_cp(("parallel", "parallel", "arbitrary")),
    )(*operands)


def _mm_ln_bwd(a, w, res, res_scale, xh, rstd, g, *, name, deps=None):
    T, K = a.shape
    D = w.shape[1]
    tm = _tile(T, 512)
    n_dep = len(deps) if deps else 0

    def body(a_ref, w_ref, res_ref, xh_ref, rs_ref, g_ref, *rest):
        dz_ref, dzb_ref, dg_ref, db_ref, cs_ref = rest[n_dep:]

        @pl.when(pl.program_id(0) == 0)
        def _():
            dg_ref[...] = jnp.zeros_like(dg_ref)
            db_ref[...] = jnp.zeros_like(db_ref)
            cs_ref[...] = jnp.zeros_like(cs_ref)

        d = lax.dot_general(a_ref[...].astype(_MXU), w_ref[...].astype(_MXU), (((1,), (1,)), ((), ())),
                            preferred_element_type=F32) + res_scale * res_ref[...]
        xh = xh_ref[...]
        dz = _ln_bwd_rows(d, xh, rs_ref[...], g_ref[...])
        dz_ref[...] = dz
        dzb_ref[...] = dz.astype(_MXU)
        dg_ref[...] += _fold8(d * xh)
        db_ref[...] += _fold8(d)
        cs_ref[...] += _fold8(dz)

    row = lambda i: (i, 0)
    fixed = lambda i: (0, 0)
    tile = pl.BlockSpec((tm, D), row)
    part = pl.BlockSpec((SUBLANES, D), fixed)
    operands = [a, w, res, xh, rstd, g] + (list(deps) if deps else [])
    if deps:
        del deps[:]
    return pl.pallas_call(
        body, name=name, grid=(T // tm,),
        in_specs=[pl.BlockSpec((tm, K), row),
                  pl.BlockSpec((None, D, K), lambda i: (0, 0, 0), pipeline_mode=pl.Buffered(1)),
                  tile, tile, pl.BlockSpec((tm, 1), row), pl.BlockSpec((1, D), fixed)] + [ANY] * n_dep,
        out_specs=[tile, tile, part, part, part],
        out_shape=[jax.ShapeDtypeStruct((T, D), F32), jax.ShapeDtypeStruct((T, D), _MXU)]
        + [jax.ShapeDtypeStruct((SUBLANES, D), F32)] * 3,
        compiler_params=_cp(("arbitrary",)),
    )(*operands)


def _out_ln(act, wo_ref, bias_ref, res_ref, alpha, g_ref, b_ref, y_ref, yb_ref, xh_ref, rs_ref):
    z = jnp.dot(act, wo_ref[...].astype(_MXU), preferred_element_type=F32) + bias_ref[...] + alpha * res_ref[...]
    y, xh, rstd = _ln_rows(z, g_ref[...], b_ref[...])
    y_ref[...] = y
    yb_ref[...] = y.astype(_MXU)
    xh_ref[...] = xh
    rs_ref[...] = rstd


def _conv_tail_fwd(v, gc, bc, w, bias, res, alpha, g, b, *, name):
    T, C = v.shape
    D = w.shape[-1]
    tm = _tile(T, 256)

    def body(v_ref, gc_ref, bc_ref, w_ref, bias_ref, res_ref, g_ref, b_ref,
             s_ref, xhc_ref, rsc_ref, y_ref, yb_ref, xh_ref, rs_ref):
        yv, xhc, rsc = _ln_rows(v_ref[...], gc_ref[...], bc_ref[...])
        s = (yv * _sigmoid(yv)).astype(_MXU)
        s_ref[...] = s
        xhc_ref[...] = xhc
        rsc_ref[...] = rsc
        _out_ln(s, w_ref, bias_ref, res_ref, alpha, g_ref, b_ref, y_ref, yb_ref, xh_ref, rs_ref)

    row = lambda i: (i, 0)
    fixed = lambda i: (0, 0)
    vc, vd = pl.BlockSpec((1, C), fixed), pl.BlockSpec((1, D), fixed)
    tc_, td = pl.BlockSpec((tm, C), row), pl.BlockSpec((tm, D), row)
    one = pl.BlockSpec((tm, 1), row)
    return pl.pallas_call(
        body, name=name, grid=(T // tm,),
        in_specs=[tc_, vc, vc, _resident((None, C, D), lambda i: (0, 0, 0)), vd, td, vd, vd],
        out_specs=[tc_, tc_, one, td, td, td, one],
        out_shape=[jax.ShapeDtypeStruct((T, C), _MXU), jax.ShapeDtypeStruct((T, C), F32),
                   jax.ShapeDtypeStruct((T, 1), F32), jax.ShapeDtypeStruct((T, D), F32),
                   jax.ShapeDtypeStruct((T, D), _MXU), jax.ShapeDtypeStruct((T, D), F32),
                   jax.ShapeDtypeStruct((T, 1), F32)],
        compiler_params=_cp(("parallel",)),
    )(v, gc, bc, w, bias, res, g, b)


def _ln_bwd(dy, xh, rstd, g, *, name, target=None):
    T, D = dy.shape
    tm = _tile(T, 256)
    head = target is not None

    def body(*refs):
        if head:
            dy_ref, t_ref, xh_ref, rs_ref, g_ref, dz_ref, dzb_ref, dg_ref, db_ref, cs_ref, ls_ref = refs
        else:
            dy_ref, xh_ref, rs_ref, g_ref, dz_ref, dzb_ref, dg_ref, db_ref, cs_ref = refs
        i = pl.program_id(0)

        @pl.when(i == 0)
        def _():
            dg_ref[...] = jnp.zeros_like(dg_ref)
            db_ref[...] = jnp.zeros_like(db_ref)
            cs_ref[...] = jnp.zeros_like(cs_ref)
            if head:
                ls_ref[...] = jnp.zeros_like(ls_ref)

        d = dy_ref[...]
        if head:
            err = d - t_ref[...]
            ls_ref[...] += _fold8(err * err)
            d = err * (1.0 / D)
        xh = xh_ref[...]
        dz = _ln_bwd_rows(d, xh, rs_ref[...], g_ref[...])
        dz_ref[...] = dz
        dzb_ref[...] = dz.astype(_MXU)
        dg_ref[...] += _fold8(d * xh)
        db_ref[...] += _fold8(d)
        cs_ref[...] += _fold8(dz)

    row = lambda i: (i, 0)
    fixed = lambda i: (0, 0)
    tile = pl.BlockSpec((tm, D), row)
    part = pl.BlockSpec((SUBLANES, D), fixed)
    in_specs = [tile] + ([tile] if head else []) + [tile, pl.BlockSpec((tm, 1), row), pl.BlockSpec((1, D), fixed)]
    n_part = 4 if head else 3
    operands = [dy] + ([target] if head else []) + [xh, rstd, g]
    return pl.pallas_call(
        body, name=name, grid=(T // tm,), in_specs=in_specs,
        out_specs=[tile, tile] + [part] * n_part,
        out_shape=[jax.ShapeDtypeStruct((T, D), F32), jax.ShapeDtypeStruct((T, D), _MXU)]
        + [jax.ShapeDtypeStruct((SUBLANES, D), F32)] * n_part,
        compiler_params=_cp(("arbitrary",)),
    )(*operands)


def _conv_cols(C, tc):
    per = (C // 2) // tc
    return per, (lambda j: (j // per) * (2 * per) + j % per)


def _glu_shifted(a_ref, g_ref, p_ref, S):
    u = a_ref[...].astype(F32) * _sigmoid(g_ref[...].astype(F32))
    rows = lax.broadcasted_iota(jnp.int32, (SUBLANES, u.shape[1]), 0)
    lo = CONV_TAPS_PAD
    for r in range(SUBLANES):
        p_ref[r, 0:lo, :] = jnp.zeros((lo, u.shape[1]), F32)
        if r == 0:
            p_ref[r, lo:lo + S, :] = u
        else:
            rolled = pltpu.roll(u, r, 0)
            p_ref[r, lo:lo + S, :] = rolled
            p_ref[r, lo:lo + SUBLANES, :] = jnp.where(rows >= r, rolled[0:SUBLANES], 0.0)


def _conv_fwd(h1, w_dw, b_dw, *, B, S, name):
    C = w_dw.shape[1]
    taps = CONV_TAPS_PAD - 1
    tc = LANES
    ch = _tile(S, 128)
    per, col_a = _conv_cols(C, tc)

    def body(a_ref, g_ref, w_ref, b_ref, o_ref, p_ref):
        _glu_shifted(a_ref, g_ref, p_ref, S)

        def chunk(ci, carry):
            base = pl.multiple_of(ci * ch, ch)
            acc = jnp.zeros((ch, tc), F32) + b_ref[...]
            for k in range(taps):
                q, r = divmod(taps - 1 - k, SUBLANES)
                start = pl.multiple_of(base + (CONV_TAPS_PAD - SUBLANES * q), SUBLANES)
                acc = acc + w_ref[pl.ds(k, 1), :] * p_ref[r, pl.ds(start, ch), :]
            o_ref[pl.ds(base, ch), :] = acc
            return carry

        lax.fori_loop(0, S // ch, chunk, 0)

    return pl.pallas_call(
        body, name=name, grid=(B, C // tc),
        in_specs=[pl.BlockSpec((S, tc), lambda b, j: (b, col_a(j))),
                  pl.BlockSpec((S, tc), lambda b, j: (b, col_a(j) + per)),
                  pl.BlockSpec((CONV_TAPS_PAD, tc), lambda b, j: (0, j)),
                  pl.BlockSpec((1, tc), lambda b, j: (0, j))],
        out_specs=pl.BlockSpec((S, tc), lambda b, j: (b, j)),
        out_shape=jax.ShapeDtypeStruct((B * S, C), F32),
        scratch_shapes=[pltpu.VMEM((SUBLANES, S + CONV_TAPS_PAD, tc), F32)],
        compiler_params=_cp(("parallel", "parallel")),
    )(h1, h1, w_dw, b_dw)


def _conv_bwd(dd, h1, w_dw, *, B, S, name):
    C = w_dw.shape[1]
    taps = CONV_TAPS_PAD - 1
    tc = LANES
    ch = _tile(S, 128)
    per, col_a = _conv_cols(C, tc)

    def body(d_ref, a_ref, g_ref, w_ref, du_ref, dw_ref, db_ref, p_ref, q_ref):
        b = pl.program_id(1)

        @pl.when(b == 0)
        def _():
            dw_ref[...] = jnp.zeros_like(dw_ref)
            db_ref[...] = jnp.zeros_like(db_ref)

        _glu_shifted(a_ref, g_ref, p_ref, S)
        d = d_ref[...]
        rows = lax.broadcasted_iota(jnp.int32, (SUBLANES, tc), 0)
        for r in range(SUBLANES):
            q_ref[r, S:S + CONV_TAPS_PAD, :] = jnp.zeros((CONV_TAPS_PAD, tc), F32)
            if r == 0:
                q_ref[r, 0:S, :] = d
            else:
                rolled = pltpu.roll(d, S - r, 0)
                q_ref[r, 0:S, :] = rolled
                q_ref[r, S - SUBLANES:S, :] = jnp.where(rows < SUBLANES - r, rolled[S - SUBLANES:S], 0.0)
        db_ref[...] += _fold8(d)

        def chunk(ci, carry):
            base = pl.multiple_of(ci * ch, ch)
            dch = d_ref[pl.ds(base, ch), :]
            acc = jnp.zeros((ch, tc), F32)
            for k in range(taps):
                q, r = divmod(taps - 1 - k, SUBLANES)
                up = pl.multiple_of(base + SUBLANES * q, SUBLANES)
                acc = acc + w_ref[pl.ds(k, 1), :] * q_ref[r, pl.ds(up, ch), :]
                down = pl.multiple_of(base + (CONV_TAPS_PAD - SUBLANES * q), SUBLANES)
                dw_ref[k] += _fold8(dch * p_ref[r, pl.ds(down, ch), :])
            du_ref[pl.ds(base, ch), :] = acc
            return carry

        lax.fori_loop(0, S // ch, chunk, 0)

    return pl.pallas_call(
        body, name=name, grid=(C // tc, B),
        in_specs=[pl.BlockSpec((S, tc), lambda j, b: (b, j)),
                  pl.BlockSpec((S, tc), lambda j, b: (b, col_a(j))),
                  pl.BlockSpec((S, tc), lambda j, b: (b, col_a(j) + per)),
                  pl.BlockSpec((CONV_TAPS_PAD, tc), lambda j, b: (0, j))],
        out_specs=[pl.BlockSpec((S, tc), lambda j, b: (b, j)),
                   pl.BlockSpec((CONV_TAPS_PAD, SUBLANES, tc), lambda j, b: (0, 0, j)),
                   pl.BlockSpec((SUBLANES, tc), lambda j, b: (0, j))],
        out_shape=[jax.ShapeDtypeStruct((B * S, C), F32),
                   jax.ShapeDtypeStruct((CONV_TAPS_PAD, SUBLANES, C), F32),
                   jax.ShapeDtypeStruct((SUBLANES, C), F32)],
        scratch_shapes=[pltpu.VMEM((SUBLANES, S + CONV_TAPS_PAD, tc), F32),
                        pltpu.VMEM((SUBLANES, S + CONV_TAPS_PAD, tc), F32)],
        compiler_params=_cp(("parallel", "arbitrary")),
    )(dd, h1, h1, w_dw)


def _ln_silu_bwd(dzb, w, xh, rstd, g, b, *, name):
    T, D = dzb.shape
    C = w.shape[1]
    tm = _tile(T, 512)

    def body(dz_ref, w_ref, xh_ref, rs_ref, g_ref, b_ref, dv_ref, dg_ref, db_ref):
        @pl.when(pl.program_id(0) == 0)
        def _():
            dg_ref[...] = jnp.zeros_like(dg_ref)
            db_ref[...] = jnp.zeros_like(db_ref)

        ds = lax.dot_general(dz_ref[...].astype(_MXU), w_ref[...].astype(_MXU), (((1,), (1,)), ((), ())),
                             preferred_element_type=F32)
        xh = xh_ref[...]
        gam = g_ref[...]
        y = xh * gam + b_ref[...]
        sig = _sigmoid(y)
        dln = ds * (sig * (1.0 + y * (1.0 - sig)))
        dv_ref[...] = _ln_bwd_rows(dln, xh, rs_ref[...], gam)
        dg_ref[...] += _fold8(dln * xh)
        db_ref[...] += _fold8(dln)

    row = lambda i: (i, 0)
    fixed = lambda i: (0, 0)
    vec = pl.BlockSpec((1, C), fixed)
    part = pl.BlockSpec((SUBLANES, C), fixed)
    return pl.pallas_call(
        body, name=name, grid=(T // tm,),
        in_specs=[pl.BlockSpec((tm, D), row), _resident((None, C, D), lambda i: (0, 0, 0)),
                  pl.BlockSpec((tm, C), row), pl.BlockSpec((tm, 1), row), vec, vec],
        out_specs=[pl.BlockSpec((tm, C), row), part, part],
        out_shape=[jax.ShapeDtypeStruct((T, C), F32)] + [jax.ShapeDtypeStruct((SUBLANES, C), F32)] * 2,
        compiler_params=_cp(("arbitrary",)),
    )(dzb, w, xh, rstd, g, b)


def _glu_bwd(du, h1, *, name):
    T, C = du.shape
    il = C // 2
    tm = _tile(T, 256)

    def body(du_ref, h_ref, dh_ref, cs_ref):
        @pl.when(pl.program_id(0) == 0)
        def _():
            cs_ref[...] = jnp.zeros_like(cs_ref)

        for hb in range(2):
            a = h_ref[:, 2 * hb * il:(2 * hb + 1) * il].astype(F32)
            gate = h_ref[:, (2 * hb + 1) * il:(2 * hb + 2) * il].astype(F32)
            d = du_ref[:, hb * il:(hb + 1) * il]
            sig = _sigmoid(gate)
            da = d * sig
            dgate = d * a * sig * (1.0 - sig)
            dh_ref[:, 2 * hb * il:(2 * hb + 1) * il] = da.astype(_MXU)
            dh_ref[:, (2 * hb + 1) * il:(2 * hb + 2) * il] = dgate.astype(_MXU)
            cs_ref[:, 2 * hb * il:(2 * hb + 1) * il] += _fold8(da)
            cs_ref[:, (2 * hb + 1) * il:(2 * hb + 2) * il] += _fold8(dgate)

    row = lambda i: (i, 0)
    return pl.pallas_call(
        body, name=name, grid=(T // tm,),
        in_specs=[pl.BlockSpec((tm, C), row), pl.BlockSpec((tm, 2 * C), row)],
        out_specs=[pl.BlockSpec((tm, 2 * C), row), pl.BlockSpec((SUBLANES, 2 * C), lambda i: (0, 0))],
        out_shape=[jax.ShapeDtypeStruct((T, 2 * C), _MXU), jax.ShapeDtypeStruct((SUBLANES, 2 * C), F32)],
        compiler_params=_cp(("arbitrary",)),
    )(du, h1)


def _tril_mask(n):
    return lax.broadcasted_iota(jnp.int32, (n, n), 0) >= lax.broadcasted_iota(jnp.int32, (n, n), 1)


def _split_uv(t, il):
    u = jnp.concatenate([t[:, 0:il], t[:, 2 * il:3 * il]], axis=1)
    v = jnp.concatenate([t[:, il:2 * il], t[:, 3 * il:4 * il]], axis=1)
    return u, v


def _gmlp_gate_fwd(p, g, b, w_s, bsb, w_out, bias, res, alpha, g1, b1, *, name):
    T, C2 = p.shape
    C = C2 // 2
    D = w_out.shape[-1]
    il = C // 2
    G, L, _ = w_s.shape
    assert G * L == C
    tm = _tile(T, 2 * L, L)

    def body(p_ref, g_ref, b_ref, ws_ref, bs_ref, wo_ref, bias_ref, res_ref, g1_ref, b1_ref,
             us_ref, xh_ref, rs_ref, y_ref, yb_ref, xh1_ref, rs1_ref, vn_ref, u_ref):
        z, _ = _gelu_parts(p_ref[...].astype(F32))
        u, v = _split_uv(z, il)
        vn, xh, rstd = _ln_rows(v, g_ref[...], b_ref[...])
        xh_ref[...] = xh
        rs_ref[...] = rstd
        vn_ref[...] = vn.astype(_MXU)
        u_ref[...] = u
        mask = _tril_mask(L)
        for gi in range(G):
            wc = jnp.where(mask, ws_ref[gi], 0.0).astype(_MXU)
            cols = slice(gi * L, (gi + 1) * L)
            for c in range(tm // L):
                rows = slice(c * L, (c + 1) * L)
                s = jnp.dot(wc, vn_ref[rows, cols], preferred_element_type=F32) + bs_ref[:, cols]
                us_ref[rows, cols] = (u_ref[rows, cols] * s).astype(_MXU)
        _out_ln(us_ref[...], wo_ref, bias_ref, res_ref, alpha, g1_ref, b1_ref, y_ref, yb_ref, xh1_ref, rs1_ref)

    row = lambda i: (i, 0)
    fixed = lambda i: (0, 0)
    vd, td, one = pl.BlockSpec((1, D), fixed), pl.BlockSpec((tm, D), row), pl.BlockSpec((tm, 1), row)
    return pl.pallas_call(
        body, name=name, grid=(T // tm,),
        in_specs=[pl.BlockSpec((tm, C2), row), pl.BlockSpec((1, C), fixed), pl.BlockSpec((1, C), fixed),
                  pl.BlockSpec((G, L, L), lambda i: (0, 0, 0)), pl.BlockSpec((L, C), fixed),
                  _resident((None, C, D), lambda i: (0, 0, 0)), vd, td, vd, vd],
        out_specs=[pl.BlockSpec((tm, C), row), pl.BlockSpec((tm, C), row), one, td, td, td, one],
        out_shape=[jax.ShapeDtypeStruct((T, C), _MXU), jax.ShapeDtypeStruct((T, C), F32),
                   jax.ShapeDtypeStruct((T, 1), F32), jax.ShapeDtypeStruct((T, D), F32),
                   jax.ShapeDtypeStruct((T, D), _MXU), jax.ShapeDtypeStruct((T, D), F32),
                   jax.ShapeDtypeStruct((T, 1), F32)],
        scratch_shapes=[pltpu.VMEM((tm, C), _MXU), pltpu.VMEM((tm, C), F32)],
        compiler_params=_cp(("parallel",)),
    )(p, g, b, w_s, bsb, w_out, bias, res, g1, b1)


def _gmlp_gate_bwd(dzb, w_out, p, xh, rstd, g, b, w_s, bsb, *, name):
    T, C2 = p.shape
    D = dzb.shape[1]
    C = C2 // 2
    il = C // 2
    G, L, _ = w_s.shape
    tm = _tile(T, 2 * L, L)

    def body(dz_ref, wo_ref, p_ref, xh_ref, rs_ref, g_ref, b_ref, ws_ref, bs_ref,
             dp_ref, dg_ref, db_ref, cs_ref, dws_ref, dbs_ref, vn_ref, u_ref, dvn_ref, du_ref, dus_ref):
        @pl.when(pl.program_id(0) == 0)
        def _():
            dg_ref[...] = jnp.zeros_like(dg_ref)
            db_ref[...] = jnp.zeros_like(db_ref)
            cs_ref[...] = jnp.zeros_like(cs_ref)
            dws_ref[...] = jnp.zeros_like(dws_ref)
            dbs_ref[...] = jnp.zeros_like(dbs_ref)

        dus_ref[...] = lax.dot_general(dz_ref[...].astype(_MXU), wo_ref[...].astype(_MXU), (((1,), (1,)), ((), ())),
                                       preferred_element_type=F32)
        z, gp = _gelu_parts(p_ref[...].astype(F32))
        u, _ = _split_uv(z, il)
        xh = xh_ref[...]
        gam = g_ref[...]
        vn_ref[...] = (xh * gam + b_ref[...]).astype(_MXU)
        u_ref[...] = u
        mask = _tril_mask(L)
        for gi in range(G):
            wc = jnp.where(mask, ws_ref[gi], 0.0).astype(_MXU)
            cols = slice(gi * L, (gi + 1) * L)
            for c in range(tm // L):
                rows = slice(c * L, (c + 1) * L)
                vnb = vn_ref[rows, cols]
                s = jnp.dot(wc, vnb, preferred_element_type=F32) + bs_ref[:, cols]
                d = dus_ref[rows, cols]
                du_ref[rows, cols] = d * s
                ds = d * u_ref[rows, cols]
                dbs_ref[:, cols] += ds
                dsb = ds.astype(_MXU)
                dw = lax.dot_general(dsb, vnb, (((1,), (1,)), ((), ())), preferred_element_type=F32)
                dws_ref[gi] += jnp.where(mask, dw, 0.0)
                dvn_ref[rows, cols] = lax.dot_general(wc, dsb, (((0,), (0,)), ((), ())), preferred_element_type=F32)
        dvn = dvn_ref[...]
        dg_ref[...] += _fold8(dvn * xh)
        db_ref[...] += _fold8(dvn)
        dv = _ln_bwd_rows(dvn, xh, rs_ref[...], gam)
        du = du_ref[...]
        for hb in range(2):
            for part, src in ((0, du), (1, dv)):
                lo = (2 * hb + part) * il
                dp = src[:, hb * il:(hb + 1) * il] * gp[:, lo:lo + il]
                dp_ref[:, lo:lo + il] = dp.astype(_MXU)
                cs_ref[:, lo:lo + il] += _fold8(dp)

    row = lambda i: (i, 0)
    fixed = lambda i: (0, 0)
    part_c = pl.BlockSpec((SUBLANES, C), fixed)
    return pl.pallas_call(
        body, name=name, grid=(T // tm,),
        in_specs=[pl.BlockSpec((tm, D), row), _resident((None, C, D), lambda i: (0, 0, 0)),
                  pl.BlockSpec((tm, C2), row), pl.BlockSpec((tm, C), row),
                  pl.BlockSpec((tm, 1), row), pl.BlockSpec((1, C), fixed), pl.BlockSpec((1, C), fixed),
                  pl.BlockSpec((G, L, L), lambda i: (0, 0, 0)), pl.BlockSpec((L, C), fixed)],
        out_specs=[pl.BlockSpec((tm, C2), row), part_c, part_c, pl.BlockSpec((SUBLANES, C2), fixed),
                   pl.BlockSpec((G, L, L), lambda i: (0, 0, 0)), pl.BlockSpec((L, C), fixed)],
        out_shape=[jax.ShapeDtypeStruct((T, C2), _MXU), jax.ShapeDtypeStruct((SUBLANES, C), F32),
                   jax.ShapeDtypeStruct((SUBLANES, C), F32), jax.ShapeDtypeStruct((SUBLANES, C2), F32),
                   jax.ShapeDtypeStruct((G, L, L), F32), jax.ShapeDtypeStruct((L, C), F32)],
        scratch_shapes=[pltpu.VMEM((tm, C), _MXU), pltpu.VMEM((tm, C), F32), pltpu.VMEM((tm, C), F32),
                        pltpu.VMEM((tm, C), F32), pltpu.VMEM((tm, C), F32)],
        compiler_params=_cp(("arbitrary",)),
    )(dzb, w_out, p, xh, rstd, g, b, w_s, bsb)


def _ffn_conv(h, prev8, w_ref, b_ref):
    h1 = _shift_down(prev8, h, 1)
    h2 = _shift_down(prev8, h, 2)
    return w_ref[pl.ds(2, 1), :] * h + w_ref[pl.ds(1, 1), :] * h1 + w_ref[pl.ds(0, 1), :] * h2 + b_ref[...]


def _resident(block, imap):
    return pl.BlockSpec(block, imap, pipeline_mode=pl.Buffered(1))


def _ffn_fwd_half(j, xb, w_up, w_down, b_up, w_dw, b_dw, *, S, name, prev=None, tail=None):
    T, D = xb.shape
    N = w_up.shape[-1]
    tn = N // N_CHIPS
    tm = _tile(S, 256)
    spt = S // tm
    last = prev is not None
    alpha = tail[1] if last else None

    def body(*refs):
        x_ref, wu_ref, wd_ref, bu_ref, wc_ref, bc_ref = refs[:6]
        if last:
            yp_ref, res_ref, bd_ref, g_ref, b_ref = refs[9:14]
            h_ref, hc_ref, f_ref, y_ref, yb_ref, xh_ref, rs_ref, carry_ref = refs[14:22]
        else:
            h_ref, hc_ref, f_ref, yp_ref, carry_ref = refs[6:11]

        @pl.when(pl.program_id(0) % spt == 0)
        def _():
            carry_ref[...] = jnp.zeros_like(carry_ref)

        h = jnp.dot(x_ref[...].astype(_MXU), wu_ref[...].astype(_MXU), preferred_element_type=F32) + bu_ref[...]
        h_ref[...] = h.astype(_HDT)
        hc = _ffn_conv(h, carry_ref[...], wc_ref, bc_ref)
        hc_ref[...] = hc.astype(_HDT)
        carry_ref[...] = h[tm - SUBLANES:tm]
        gte = hc[:, :tn]
        f = (gte * _sigmoid(gte) * hc[:, tn:]).astype(_MXU)
        f_ref[...] = f
        y = jnp.dot(f, wd_ref[...].astype(_MXU), preferred_element_type=F32)
        if not last:
            yp_ref[...] = y
            return
        z = y + yp_ref[...] + bd_ref[...] + alpha * res_ref[...]
        out, xh, rstd = _ln_rows(z, g_ref[...], b_ref[...])
        y_ref[...] = out
        yb_ref[...] = out.astype(_MXU)
        xh_ref[...] = xh
        rs_ref[...] = rstd

    row = lambda i: (i, 0)
    pair = lambda i: (0, j)
    vec = pl.BlockSpec((1, D), lambda i: (0, 0))
    tile = pl.BlockSpec((tm, D), row)
    in_specs = [tile, _resident((None, D, 2 * tn), lambda i: (0, 0, j)), _resident((None, tn, D), lambda i: (0, j, 0)),
                pl.BlockSpec((1, 2 * tn), pair), pl.BlockSpec((SUBLANES, 2 * tn), pair), pl.BlockSpec((1, 2 * tn), pair)]
    operands = [xb, w_up, w_down, b_up, w_dw, b_dw]
    wide = pl.BlockSpec((tm, 2 * tn), lambda i: (i, j))
    out_specs = [wide, wide, pl.BlockSpec((tm, tn), lambda i: (i, j))]
    out_shape = [jax.ShapeDtypeStruct((T, N), _HDT), jax.ShapeDtypeStruct((T, N), _HDT),
                 jax.ShapeDtypeStruct((T, N // 2), _MXU)]
    aliases = {}
    if last:
        res, _, b_down, g, b = tail
        in_specs += [ANY, ANY, ANY, tile, tile, vec, vec, vec]
        operands += list(prev) + [res, b_down, g, b]
        aliases = {6: 0, 7: 1, 8: 2}
        out_specs += [tile, tile, tile, pl.BlockSpec((tm, 1), row)]
        out_shape += [jax.ShapeDtypeStruct((T, D), F32), jax.ShapeDtypeStruct((T, D), _MXU),
                      jax.ShapeDtypeStruct((T, D), F32), jax.ShapeDtypeStruct((T, 1), F32)]
    else:
        out_specs.append(tile)
        out_shape.append(jax.ShapeDtypeStruct((T, D), F32))
    return pl.pallas_call(
        body, name=name, grid=(T // tm,), in_specs=in_specs, out_specs=out_specs, out_shape=out_shape,
        input_output_aliases=aliases, scratch_shapes=[pltpu.VMEM((SUBLANES, 2 * tn), F32)],
        compiler_params=_cp(("arbitrary",)),
    )(*operands)


def _ffn_bwd_half(j, dzb, w_down, w_up, hs, hcs, w_dw, *, S, name, dz=None, alpha=None, prev=None, ln=None):
    T, D = dzb.shape
    N = hs.shape[1]
    tn = N // N_CHIPS
    tm = _tile(S, 256)
    spt = S // tm
    nt = T // tm
    last = prev is not None

    def body(*refs):
        dz_ref, wd_ref, wu_ref, h_ref, hc_ref, wc_ref = refs[:6]
        if last:
            dxp_ref, xh_ref, rs_ref, g_ref = refs[7:11]
            dh_ref, cs_ref, dw_ref, db_ref, dz1_ref, dz1b_ref, dg1_ref, db1_ref, cs1_ref, carry_ref = refs[11:21]
        else:
            dzf_ref = refs[6]
            dh_ref, cs_ref, dw_ref, db_ref, dxp_ref, carry_ref = refs[7:13]
        i = pl.program_id(0)
        ii = nt - 1 - i

        @pl.when(i == 0)
        def _():
            cs_ref[...] = jnp.zeros_like(cs_ref)
            dw_ref[...] = jnp.zeros_like(dw_ref)
            db_ref[...] = jnp.zeros_like(db_ref)
            if last:
                dg1_ref[...] = jnp.zeros_like(dg1_ref)
                db1_ref[...] = jnp.zeros_like(db1_ref)
                cs1_ref[...] = jnp.zeros_like(cs1_ref)

        df = lax.dot_general(dz_ref[...].astype(_MXU), wd_ref[...].astype(_MXU), (((1,), (1,)), ((), ())),
                             preferred_element_type=F32)
        h = h_ref[...].astype(F32)
        gte, val = hc_ref[:, :tn].astype(F32), hc_ref[:, tn:].astype(F32)
        sig = _sigmoid(gte)
        dval = df * (gte * sig)
        dg = df * val * (sig * (1.0 + gte * (1.0 - sig)))
        dhc = jnp.concatenate([dg, dval], axis=1)
        nxt = jnp.where((ii + 1) % spt == 0, 0.0, carry_ref[...])
        d1 = _shift_up(dhc, nxt, 1)
        d2 = _shift_up(dhc, nxt, 2)
        carry_ref[...] = dhc[0:SUBLANES]
        db_ref[...] += _fold8(dhc)
        dw_ref[2] += _fold8(dhc * h)
        dw_ref[1] += _fold8(d1 * h)
        dw_ref[0] += _fold8(d2 * h)
        dh = wc_ref[pl.ds(2, 1), :] * dhc + wc_ref[pl.ds(1, 1), :] * d1 + wc_ref[pl.ds(0, 1), :] * d2
        cs_ref[...] += _fold8(dh)
        dhb = dh.astype(_MXU)
        dh_ref[...] = dhb
        dx = lax.dot_general(dhb, wu_ref[...].astype(_MXU), (((1,), (1,)), ((), ())), preferred_element_type=F32)
        if not last:
            dxp_ref[...] = dx + alpha * dzf_ref[...]
            return
        d = dx + dxp_ref[...]
        xh = xh_ref[...]
        dz1 = _ln_bwd_rows(d, xh, rs_ref[...], g_ref[...])
        dz1_ref[...] = dz1
        dz1b_ref[...] = dz1.astype(_MXU)
        dg1_ref[...] += _fold8(d * xh)
        db1_ref[...] += _fold8(d)
        cs1_ref[...] += _fold8(dz1)

    rev = lambda i: (nt - 1 - i, 0)
    fixed = lambda i: (0, 0)
    pair = lambda i: (0, j)
    tile = pl.BlockSpec((tm, D), rev)
    wide = pl.BlockSpec((tm, 2 * tn), lambda i: (nt - 1 - i, j))
    part = pl.BlockSpec((SUBLANES, 2 * tn), fixed)
    in_specs = [tile, _resident((None, tn, D), lambda i: (0, j, 0)), _resident((None, D, 2 * tn), lambda i: (0, 0, j)),
                wide, wide, pl.BlockSpec((SUBLANES, 2 * tn), pair)]
    operands = [dzb, w_down, w_up, hs, hcs, w_dw]
    out_specs = [wide, part, pl.BlockSpec((3, SUBLANES, 2 * tn), lambda i: (0, 0, 0)), part]
    out_shape = [jax.ShapeDtypeStruct((T, N), _MXU), jax.ShapeDtypeStruct((SUBLANES, 2 * tn), F32),
                 jax.ShapeDtypeStruct((3, SUBLANES, 2 * tn), F32), jax.ShapeDtypeStruct((SUBLANES, 2 * tn), F32)]
    aliases = {}
    if last:
        xh, rstd, g = ln
        in_specs += [ANY, tile, tile, pl.BlockSpec((tm, 1), rev), pl.BlockSpec((1, D), fixed)]
        operands += [prev[0], prev[1], xh, rstd, g]
        aliases = {6: 0}
        out_specs += [tile, tile] + [pl.BlockSpec((SUBLANES, D), fixed)] * 3
        out_shape += [jax.ShapeDtypeStruct((T, D), F32), jax.ShapeDtypeStruct((T, D), _MXU)] \
            + [jax.ShapeDtypeStruct((SUBLANES, D), F32)] * 3
    else:
        in_specs.append(tile)
        operands.append(dz)
        out_specs.append(tile)
        out_shape.append(jax.ShapeDtypeStruct((T, D), F32))
    return pl.pallas_call(
        body, name=name, grid=(nt,), in_specs=in_specs, out_specs=out_specs, out_shape=out_shape,
        input_output_aliases=aliases, scratch_shapes=[pltpu.VMEM((SUBLANES, 2 * tn), F32)],
        compiler_params=_cp(("arbitrary",)),
    )(*operands)


def _sum_pieces(g, r, me, layer, acc, n_layers, *, name):
    _, pr, pc = g.shape
    tr = _tile(pr, 128)

    def body(me_ref, g_ref, r_ref, *rest):
        o_ref = rest[-1]
        total = g_ref[...].astype(F32)
        for s in range(N_DEV - 1):
            total = total + r_ref[s].astype(F32)
        o_ref[...] = total

    in_specs = [pl.BlockSpec((None, tr, pc), lambda i, me_ref: (me_ref[0], i, 0)),
                pl.BlockSpec((N_DEV - 1, tr, pc), lambda i, me_ref: (0, i, 0))]
    operands = [me, g, r]
    aliases = {}
    if acc is not None:
        in_specs.append(ANY)
        operands.append(acc)
        aliases = {3: 0}
    return pl.pallas_call(
        body, name=name,
        grid_spec=pltpu.PrefetchScalarGridSpec(
            num_scalar_prefetch=1, grid=(pr // tr,), in_specs=in_specs,
            out_specs=pl.BlockSpec((None, tr, pc), lambda i, me_ref: (layer, i, 0))),
        out_shape=jax.ShapeDtypeStruct((n_layers, pr, pc), F32),
        input_output_aliases=aliases,
        compiler_params=_cp(("parallel",)),
    )(*operands)


def _adam_math(w, g, m, v):
    bc1 = 1.0 - ADAM_B1 ** ADAM_STEP
    bc2 = 1.0 - ADAM_B2 ** ADAM_STEP
    m = ADAM_B1 * m + (1.0 - ADAM_B1) * g
    v = ADAM_B2 * v + (1.0 - ADAM_B2) * (g * g)
    return -ADAM_LR * ((m / bc1) / (jnp.sqrt(v / bc2) + ADAM_EPS) + ADAM_WD * w), m, v


def _adam(w, g, m, v, *, name):
    R, C = w.shape
    tr = _tile(R, 256)

    def body(w_ref, g_ref, m_ref, v_ref, d_ref, mo_ref, vo_ref):
        d_ref[...], mo_ref[...], vo_ref[...] = _adam_math(w_ref[...], g_ref[...], m_ref[...], v_ref[...])

    spec = pl.BlockSpec((tr, C), lambda i: (i, 0))
    return pl.pallas_call(
        body, name=name, grid=(R // tr,), in_specs=[spec] * 4, out_specs=[spec] * 3,
        out_shape=[jax.ShapeDtypeStruct((R, C), F32)] * 3,
        compiler_params=_cp(("parallel",)),
    )(w, g, m, v)


def _adam_halves(w, own, got, m, v, core, *, name):
    L, R, C = w.shape
    rh = R // 2
    tr = _tile(rh, 256)
    nt = rh // tr

    def body(c_ref, w_ref, own_ref, got_ref, m_ref, v_ref, g_ref, d_ref, mo_ref, vo_ref):
        g = jnp.where(pl.program_id(1) == c_ref[0], own_ref[...], got_ref[...])
        g_ref[...] = g
        d_ref[...], mo_ref[...], vo_ref[...] = _adam_math(w_ref[...], g, m_ref[...], v_ref[...])

    full = pl.BlockSpec((None, tr, C), lambda l, h, t, c_ref: (l, h * nt + t, 0))
    half = pl.BlockSpec((None, tr, C), lambda l, h, t, c_ref: (l, t, 0))
    return pl.pallas_call(
        body, name=name,
        grid_spec=pltpu.PrefetchScalarGridSpec(
            num_scalar_prefetch=1, grid=(L, 2, nt), in_specs=[full, half, half, full, full], out_specs=[full] * 4),
        out_shape=[jax.ShapeDtypeStruct((L, R, C), F32)] * 4,
        compiler_params=_cp(("parallel", "parallel", "parallel")),
    )(core, w, own, got, m, v)


def _remote(src, dst, send, recv, dev):
    return pltpu.make_async_remote_copy(src_ref=src, dst_ref=dst, send_sem=send, recv_sem=recv,
                                        device_id=dev, device_id_type=MESH)


def _place_w(shard, pos, layer, *, axis, name):
    _, R, C = shard.shape
    tr = _tile(R, 512, 16)
    nt = R // tr
    if axis == 2:
        out_shape = (1, R, N_CHIPS * C)
        out_map = lambda t, q: (0, t, q[0])
    else:
        out_shape = (1, N_CHIPS * R, C)
        out_map = lambda t, q: (0, q[0] * nt + t, 0)

    def body(q_ref, s_ref, o_ref):
        o_ref[...] = s_ref[...].astype(_WIRE)

    return pl.pallas_call(
        body, name=name,
        grid_spec=pltpu.PrefetchScalarGridSpec(
            num_scalar_prefetch=1, grid=(nt,),
            in_specs=[pl.BlockSpec((None, tr, C), lambda t, q: (layer, t, 0))],
            out_specs=pl.BlockSpec((None, tr, C), out_map)),
        out_shape=jax.ShapeDtypeStruct(out_shape, _WIRE),
        compiler_params=_cp(("parallel",)),
    )(pos, shard)


def _ag_window(ref, kind, px, py, h):
    axis, perm = kind
    q = 2 * px + py
    if perm:
        q = _perm_idx(q)
    if axis == 2:
        R, C = ref.shape[1], ref.shape[2] // N_CHIPS
        rh = R // 2
        return ref.at[:, pl.ds(pl.multiple_of(h * rh, 16), rh), pl.ds(pl.multiple_of(q * C, LANES), C)]
    R = ref.shape[1] // N_CHIPS
    rh = R // 2
    return ref.at[:, pl.ds(pl.multiple_of(q * R + h * rh, 16), rh), :]


def _ag_ici_copies(refs, kinds, send, recv):
    x, y, c = lax.axis_index("x"), lax.axis_index("y"), lax.axis_index("c")
    chips = [(1 - x, y), (x, 1 - y), (1 - x, 1 - y)]
    sends, recvs = [], []
    for a, (ref, kind) in enumerate(zip(refs, kinds)):
        own = _ag_window(ref, kind, x, y, c)
        for i, (px, py) in enumerate(chips):
            k = 3 * a + i
            sends.append(_remote(own, own, send.at[k], recv.at[k], (px, py, c)))
            recvs.append(_remote(own, _ag_window(ref, kind, px, py, c), send.at[k], recv.at[k], (px, py, c)))
    return sends, recvs


def _ag_start(arrs, kinds, after, *, name):
    n = len(arrs)

    def body(*refs):
        in_refs = refs[:n]
        send, recv = refs[n + len(after)], refs[n + len(after) + 1]
        token = refs[-1]
        sends, _ = _ag_ici_copies(in_refs, kinds, send, recv)
        for cp in sends:
            cp.start()
        token[...] = jnp.zeros_like(token)

    sems = pltpu.SemaphoreType.DMA((3 * n,))
    out = pl.pallas_call(
        body, name=name,
        out_shape=(sems, sems) + tuple(pltpu.HBM(a.shape, a.dtype) for a in arrs)
        + (jax.ShapeDtypeStruct((SUBLANES, LANES), F32),),
        in_specs=(HBM,) * n + (ANY,) * len(after),
        out_specs=(SEMS, SEMS) + (HBM,) * n + (pl.BlockSpec(memory_space=pltpu.VMEM),),
        input_output_aliases={a: 2 + a for a in range(n)},
        compiler_params=pltpu.CompilerParams(has_side_effects=EFFECT),
    )(*[pltpu.with_memory_space_constraint(a, pltpu.HBM) for a in arrs], *after)
    return out[0], out[1], list(out[2:2 + n]), out[-1]


def _ag_wait(send, recv, arrs, kinds, after, *, name):
    n = len(arrs)

    def body(*refs):
        in_refs = refs[:n]
        send, recv = refs[n], refs[n + 1]
        sends, recvs = _ag_ici_copies(in_refs, kinds, send, recv)
        for cp in sends:
            cp.wait_send()
        for cp in recvs:
            cp.wait_recv()

    out = pl.pallas_call(
        body, name=name,
        out_shape=tuple(pltpu.HBM(a.shape, a.dtype) for a in arrs),
        in_specs=(HBM,) * n + (SEMS, SEMS) + (ANY,) * len(after), out_specs=(HBM,) * n,
        input_output_aliases={a: a for a in range(n)},
        compiler_params=pltpu.CompilerParams(has_side_effects=EFFECT),
    )(*arrs, send, recv, *after)
    return list(out)


def _ag_forward(arrs, kinds, *, name):
    n = len(arrs)

    def body(*refs):
        o_refs, send, recv = refs[n:2 * n], refs[2 * n], refs[2 * n + 1]
        x, y, c = lax.axis_index("x"), lax.axis_index("y"), lax.axis_index("c")
        chips = [(1 - x, y), (x, 1 - y), (1 - x, 1 - y)]
        sib = (x, y, 1 - c)
        sends, recvs = [], []
        for a, (ref, kind) in enumerate(zip(o_refs, kinds)):
            for i, (px, py) in enumerate(chips):
                k = 3 * a + i
                got = _ag_window(ref, kind, px, py, c)
                cp = _remote(got, got, send.at[k], recv.at[k], sib)
                cp.start()
                sends.append(cp)
                recvs.append(_remote(got, _ag_window(ref, kind, px, py, 1 - c), send.at[k], recv.at[k], sib))
        for cp in recvs:
            cp.wait_recv()
        for cp in sends:
            cp.wait_send()

    out = pl.pallas_call(
        body, name=name, in_specs=[ANY] * n, out_specs=[ANY] * n,
        out_shape=[jax.ShapeDtypeStruct(a.shape, a.dtype) for a in arrs],
        input_output_aliases={a: a for a in range(n)},
        scratch_shapes=[pltpu.SemaphoreType.DMA((3 * n,)), pltpu.SemaphoreType.DMA((3 * n,))],
    )(*arrs)
    return list(out)


def _flip(x, y, c, f):
    return ((1 - x) if f & 4 else x, (1 - y) if f & 2 else y, (1 - c) if f & 1 else c)


def _rs_copies(g_refs, land_refs, send, recv):
    x, y, c = lax.axis_index("x"), lax.axis_index("y"), lax.axis_index("c")
    cps = []
    for a, (g_ref, land_ref) in enumerate(zip(g_refs, land_refs)):
        for f in range(1, N_DEV):
            tx, ty, tcx = _flip(x, y, c, f)
            k = (N_DEV - 1) * a + f - 1
            cps.append(_remote(g_ref.at[4 * tx + 2 * ty + tcx], land_ref.at[f - 1], send.at[k], recv.at[k],
                               (tx, ty, tcx)))
    return cps


def _rs_start(gs, *, name):
    n = len(gs)
    lands = [lax.empty((N_DEV - 1,) + g.shape[1:], g.dtype) for g in gs]

    def body(*refs):
        send, recv, token = refs[2 * n], refs[2 * n + 1], refs[-1]
        for cp in _rs_copies(refs[:n], refs[n:2 * n], send, recv):
            cp.start()
        token[...] = jnp.zeros_like(token)

    sems = pltpu.SemaphoreType.DMA(((N_DEV - 1) * n,))
    thru = [pltpu.HBM(t.shape, t.dtype) for t in gs + lands]
    out = pl.pallas_call(
        body, name=name,
        out_shape=(sems, sems, *thru, jax.ShapeDtypeStruct((SUBLANES, LANES), F32)),
        in_specs=(HBM,) * (2 * n), out_specs=(SEMS, SEMS) + (HBM,) * (2 * n) + (pl.BlockSpec(memory_space=pltpu.VMEM),),
        input_output_aliases={a: 2 + a for a in range(2 * n)},
        compiler_params=pltpu.CompilerParams(has_side_effects=EFFECT),
    )(*[pltpu.with_memory_space_constraint(t, pltpu.HBM) for t in gs + lands])
    return out[0], out[1], list(out[2:2 + n]), list(out[2 + n:2 + 2 * n]), out[-1]


def _rs_wait(send, recv, gs, lands, after, *, name):
    n = len(gs)

    def body(*refs):
        cps = _rs_copies(refs[:n], refs[n:2 * n], refs[2 * n], refs[2 * n + 1])
        for cp in cps:
            cp.wait_send()
        for cp in cps:
            cp.wait_recv()

    out = pl.pallas_call(
        body, name=name,
        out_shape=tuple(pltpu.HBM(t.shape, t.dtype) for t in gs + lands),
        in_specs=(HBM,) * (2 * n) + (SEMS, SEMS, ANY), out_specs=(HBM,) * (2 * n),
        input_output_aliases={a: a for a in range(2 * n)},
        compiler_params=pltpu.CompilerParams(has_side_effects=EFFECT),
    )(*gs, *lands, send, recv, after)
    return list(out[:n]), list(out[n:])


def _pair_exchange(own, *, name):
    def body(own_ref, got_ref, send, recv):
        x, y, c = lax.axis_index("x"), lax.axis_index("y"), lax.axis_index("c")
        cp = _remote(own_ref, got_ref, send, recv, (x, y, 1 - c))
        cp.start()
        cp.wait_recv()
        cp.wait_send()

    return pl.pallas_call(
        body, name=name, in_specs=[ANY], out_specs=ANY, out_shape=jax.ShapeDtypeStruct(own.shape, own.dtype),
        scratch_shapes=[pltpu.SemaphoreType.DMA, pltpu.SemaphoreType.DMA],
    )(own)


def _allreduce_flat(vec, *, name):
    n = vec.shape[0]
    unit = N_DEV * SUBLANES * LANES
    npad = -(-n // unit) * unit
    rows = npad // (N_DEV * LANES)
    xin = jnp.pad(vec, (0, npad - n)).reshape(N_DEV, rows, LANES)

    def body(x_ref, y_ref, a_ref, send_a, recv_a, send_b, recv_b):
        x, y, c = lax.axis_index("x"), lax.axis_index("y"), lax.axis_index("c")
        me = 4 * x + 2 * y + c
        a_ref[me] = x_ref[me]
        sends, recvs = [], []
        for f in range(1, N_DEV):
            dev = _flip(x, y, c, f)
            t = 4 * dev[0] + 2 * dev[1] + dev[2]
            cp = _remote(x_ref.at[t], a_ref.at[me], send_a.at[f - 1], recv_a.at[f - 1], dev)
            cp.start()
            sends.append(cp)
            recvs.append(_remote(x_ref.at[me], a_ref.at[t], send_a.at[f - 1], recv_a.at[f - 1], dev))
        for cp in recvs:
            cp.wait_recv()
        for cp in sends:
            cp.wait_send()
        acc = a_ref[0]
        for s in range(1, N_DEV):
            acc = acc + a_ref[s]
        y_ref[me] = acc
        sends, recvs = [], []
        for f in range(1, N_DEV):
            dev = _flip(x, y, c, f)
            t = 4 * dev[0] + 2 * dev[1] + dev[2]
            cp = _remote(y_ref.at[me], y_ref.at[me], send_b.at[f - 1], recv_b.at[f - 1], dev)
            cp.start()
            sends.append(cp)
            recvs.append(_remote(y_ref.at[me], y_ref.at[t], send_b.at[f - 1], recv_b.at[f - 1], dev))
        for cp in recvs:
            cp.wait_recv()
        for cp in sends:
            cp.wait_send()

    vm = pl.BlockSpec(memory_space=pltpu.VMEM)
    out = pl.pallas_call(
        body, name=name, in_specs=[vm], out_specs=vm,
        out_shape=jax.ShapeDtypeStruct((N_DEV, rows, LANES), F32),
        scratch_shapes=[pltpu.VMEM((N_DEV, rows, LANES), F32)] + [pltpu.SemaphoreType.DMA((N_DEV - 1,))] * 4,
        compiler_params=_cp(),
    )(xin)
    return out.reshape(npad)[:n]


def _perm_cols(v, blocks=N_CHIPS):
    lead, n = v.shape[:-1], v.shape[-1]
    return v.reshape(lead + (blocks, n // blocks))[..., PERM, :].reshape(lead + (n,))


def _pack(arrs):
    return jnp.concatenate([a.reshape(-1).astype(F32) for a in arrs])


def _unpack(flat, shapes):
    out, pos = [], 0
    for s in shapes:
        n = 1
        for d in s:
            n *= d
        out.append(flat[pos:pos + n].reshape(s))
        pos += n
    return out


def kernel(x, conv_w_in, conv_b_in, conv_w_dw, conv_b_dw, conv_ln_g, conv_ln_b, conv_w_out, conv_b_out, gmlp_w_in, gmlp_b_in, gmlp_ln_g, gmlp_ln_b, gmlp_w_s, gmlp_b_s, gmlp_w_out, gmlp_b_out, ffn_w_up, ffn_b_up, ffn_w_dw, ffn_b_dw, ffn_w_down, ffn_b_down, norm1_g, norm1_b, norm2_g, norm2_b, loss_target, m_conv_w_in, m_conv_b_in, m_conv_w_dw, m_conv_b_dw, m_conv_ln_g, m_conv_ln_b, m_conv_w_out, m_conv_b_out, m_gmlp_w_in, m_gmlp_b_in, m_gmlp_ln_g, m_gmlp_ln_b, m_gmlp_w_s, m_gmlp_b_s, m_gmlp_w_out, m_gmlp_b_out, m_ffn_w_up, m_ffn_b_up, m_ffn_w_dw, m_ffn_b_dw, m_ffn_w_down, m_ffn_b_down, m_norm1_g, m_norm1_b, m_norm2_g, m_norm2_b, v_conv_w_in, v_conv_b_in, v_conv_w_dw, v_conv_b_dw, v_conv_ln_g, v_conv_ln_b, v_conv_w_out, v_conv_b_out, v_gmlp_w_in, v_gmlp_b_in, v_gmlp_ln_g, v_gmlp_ln_b, v_gmlp_w_s, v_gmlp_b_s, v_gmlp_w_out, v_gmlp_b_out, v_ffn_w_up, v_ffn_b_up, v_ffn_w_dw, v_ffn_b_dw, v_ffn_w_down, v_ffn_b_down, v_norm1_g, v_norm1_b, v_norm2_g, v_norm2_b):
    P = dict(locals())
    WEIGHTS = ['conv_w_in', 'conv_b_in', 'conv_w_dw', 'conv_b_dw', 'conv_ln_g', 'conv_ln_b', 'conv_w_out',
               'conv_b_out', 'gmlp_w_in', 'gmlp_b_in', 'gmlp_ln_g', 'gmlp_ln_b', 'gmlp_w_s', 'gmlp_b_s',
               'gmlp_w_out', 'gmlp_b_out', 'ffn_w_up', 'ffn_b_up', 'ffn_w_dw', 'ffn_b_dw', 'ffn_w_down',
               'ffn_b_down', 'norm1_g', 'norm1_b', 'norm2_g', 'norm2_b']
    BIG = ['conv_w_in', 'conv_w_out', 'gmlp_w_in', 'gmlp_w_out', 'ffn_w_up', 'ffn_w_down']
    SMALL_SHARDED = {'conv_w_dw': 2, 'gmlp_b_in': 1, 'gmlp_ln_g': 1, 'gmlp_ln_b': 1, 'gmlp_b_out': 1, 'ffn_w_dw': 2}

    B, S, D = x.shape
    T = B * S
    depth = norm1_g.shape[0]
    alpha = (2.0 * depth) ** 0.25
    C = conv_w_out.shape[-1]
    F2 = ffn_b_up.shape[-1]
    G, L = gmlp_w_s.shape[1], gmlp_w_s.shape[2]
    xi, yi, ci = lax.axis_index("x"), lax.axis_index("y"), lax.axis_index("c")
    shard = 2 * xi + yi

    i32 = lambda v: jnp.reshape(v, (1,)).astype(jnp.int32)
    pos_plain, pos_perm = i32(shard), i32(_perm_idx(shard))
    me_id, core_id = i32(4 * xi + 2 * yi + ci), i32(ci)

    groups = []
    for i in range(depth):
        mix = 'conv' if i % 2 == 0 else 'gmlp'
        groups.append((f"{mix}{i // 2}", [(mix + '_w_in', i // 2, 2, True), (mix + '_w_out', i // 2, 1, False)]))
        groups.append((f"ffn{i}", [('ffn_w_up', i, 2, True), ('ffn_w_down', i, 1, False)]))
    sm_names = list(SMALL_SHARDED)
    sm_shapes = [P[n].shape for n in sm_names]
    mine = _pack([P[n] for n in sm_names]) * (ci == 0).astype(F32)
    buf = jnp.zeros((N_CHIPS, mine.shape[0]), F32)
    buf = lax.dynamic_update_slice(buf, mine[None], (shard, 0))
    gathered = _allreduce_flat(buf.reshape(-1), name="ag_small").reshape(N_CHIPS, -1)

    started, order = {}, [gathered]
    for gname, members in groups:
        placed = [_place_w(P[n], pos_perm if perm else pos_plain, l, axis=axis, name=f"place_{n}_{l}")
                  for n, l, axis, perm in members]
        kinds = [(axis, perm) for _, _, axis, perm in members]
        send, recv, arrs, token = _ag_start(placed, kinds, order, name=f"ag_start_{gname}")
        order = [token]
        started[gname] = (send, recv, arrs, kinds, [(n, l) for n, l, _, _ in members])
    wts = {}

    def arrive(gname, after):
        send, recv, arrs, kinds, keys = started[gname]
        arrs = _ag_wait(send, recv, arrs, kinds, after, name=f"ag_wait_{gname}")
        arrs = _ag_forward(arrs, kinds, name=f"ag_fwd_{gname}")
        wts.update(zip(keys, arrs))

    full = {}
    for n, parts in zip(sm_names, zip(*[_unpack(gathered[k], sm_shapes) for k in range(N_CHIPS)])):
        full[n] = jnp.concatenate(parts, axis=SMALL_SHARDED[n])
    for n in WEIGHTS:
        if n not in BIG and n not in full:
            full[n] = P[n]

    assert G * L == C, "a gMLP group must be as wide as a chunk is long"

    def row(v):
        return v.reshape(1, -1)

    def pad_rows(v, r):
        return jnp.pad(v, ((0, r - v.shape[0]), (0, 0)))

    xf = x.reshape(T, D)
    saved = []
    cur, cur_b = xf, xf.astype(_MXU)
    for i in range(depth):
        j = i // 2
        sv = {'x': cur, 'xb': cur_b}
        arrive(groups[2 * i][0], order if i == 0 else [cur_b])
        if i % 2 == 0:
            b_in = row(_perm_cols(full['conv_b_in'][j]))
            h1 = _mm(cur_b, wts['conv_w_in', j], bl=0, bias=b_in, tm=_tile(T, 512), tn=_tile(2 * C, 1024, LANES),
                     tk=D, name=f"conv_in_{j}", n_outer=True, out_dtype=_ADT)
            wdw = pad_rows(full['conv_w_dw'][j], CONV_TAPS_PAD)
            dwo = _conv_fwd(h1, wdw, row(full['conv_b_dw'][j]), B=B, S=S, name=f"conv_dw_{j}")
            s_act, xhc, rsc, *y1 = _conv_tail_fwd(
                dwo, row(full['conv_ln_g'][j]), row(full['conv_ln_b'][j]), wts['conv_w_out', j],
                row(full['conv_b_out'][j]), cur, alpha, row(norm1_g[i]), row(norm1_b[i]), name=f"conv_out_ln_{j}")
            sv.update(h1=h1, wdw=wdw, act=s_act, xhc=xhc, rsc=rsc)
        else:
            b_in = row(_perm_cols(full['gmlp_b_in'][j]))
            pre = _mm(cur_b, wts['gmlp_w_in', j], bl=0, bias=b_in, tm=_tile(T, 512), tn=_tile(2 * C, 1024, LANES),
                      tk=D, name=f"gmlp_in_{j}", n_outer=True, out_dtype=_ADT)
            bsb = jnp.repeat(gmlp_b_s[j].T, L, axis=1)
            us, xhv, rsv, *y1 = _gmlp_gate_fwd(
                pre, row(full['gmlp_ln_g'][j]), row(full['gmlp_ln_b'][j]), gmlp_w_s[j], bsb, wts['gmlp_w_out', j],
                row(full['gmlp_b_out'][j]), cur, alpha, row(norm1_g[i]), row(norm1_b[i]), name=f"gmlp_gate_{j}")
            sv.update(pre=pre, bsb=bsb, act=us, xhv=xhv, rsv=rsv)
        x1, x1b, xh1, rs1 = y1
        arrive(groups[2 * i + 1][0], [x1b])
        wdw3 = pad_rows(_perm_cols(full['ffn_w_dw'][i]), SUBLANES)
        bdw3 = row(_perm_cols(ffn_b_dw[i]))
        ffn_in = (x1b, wts['ffn_w_up', i], wts['ffn_w_down', i], row(_perm_cols(ffn_b_up[i])), wdw3, bdw3)
        first = _ffn_fwd_half(0, *ffn_in, S=S, name=f"ffn_fwd_a_{i}")
        hs, hcs, f_act, x2, x2b, xh2, rs2 = _ffn_fwd_half(
            1, *ffn_in, S=S, name=f"ffn_fwd_b_{i}", prev=first,
            tail=(x1, alpha, row(ffn_b_down[i]), row(norm2_g[i]), row(norm2_b[i])))
        sv.update(x1=x1, x1b=x1b, xh1=xh1, rs1=rs1, hs=hs, hcs=hcs, f=f_act, wdw3=wdw3, xh2=xh2, rs2=rs2)
        saved.append(sv)
        cur, cur_b = x2, x2b

    sg = {n: [None] * full[n].shape[0] for n in WEIGHTS if n not in BIG}
    inflight = {n: [None] * P[n].shape[0] for n in BIG}
    deps = []
    tgt = loss_target.reshape(T, D)
    dcur = None
    loss_part = None
    tk_t = _tile(T, 2048)

    ready = []

    def wgrad(n, l, a_, b_, **kw):
        ready.append((n, l, _mm(a_, b_, ta=True, out_dtype=_WIRE, tk=tk_t, name=f"{n}_dw_{l}", deps=deps, **kw)))

    def launch(gname):
        send, recv, gs, lands, token = _rs_start([g for _, _, g in ready], name=f"rs_start_{gname}")
        group = {'name': gname, 'flight': (send, recv, gs, lands), 'landed': None}
        for a, (n, l, _) in enumerate(ready):
            inflight[n][l] = (group, a)
        del ready[:]
        deps.append(token)

    def landed(n, l):
        group, a = inflight[n][l]
        if group['landed'] is None:
            group['landed'] = _rs_wait(*group['flight'], dcur, name=f"rs_wait_{group['name']}")
        return group['landed'][0][a], group['landed'][1][a]

    for i in reversed(range(depth)):
        j = i // 2
        sv = saved[i]
        if i == depth - 1:
            dz2, dz2b, dg, db, cs, loss_part = _ln_bwd(cur, sv['xh2'], sv['rs2'], row(norm2_g[i]), target=tgt,
                                                       name=f"ln2_bwd_head_{i}")
        else:
            dz2, dz2b, dg, db, cs = dcur
        sg['norm2_g'][i], sg['norm2_b'][i], sg['ffn_b_down'][i] = dg.sum(0), db.sum(0), cs.sum(0)
        Fh = F2 // 2
        wgrad('ffn_w_down', i, sv['f'], dz2b, tm=Fh // 2, tn=_tile(D, 1024, LANES), pieces=('row',))
        ffn_in = (dz2b, wts['ffn_w_down', i], wts['ffn_w_up', i], sv['hs'], sv['hcs'], sv['wdw3'])
        dh0, csu0, dwd0, dbd0, dxp = _ffn_bwd_half(0, *ffn_in, S=S, name=f"ffn_bwd_a_{i}", dz=dz2, alpha=alpha)
        dh, csu1, dwd1, dbd1, dz1, dz1b, dg, db, cs = _ffn_bwd_half(
            1, *ffn_in, S=S, name=f"ffn_bwd_b_{i}", prev=(dh0, dxp), ln=(sv['xh1'], sv['rs1'], row(norm1_g[i])))
        sg['ffn_b_up'][i] = _perm_cols(jnp.concatenate([csu0.sum(0), csu1.sum(0)], axis=-1))
        sg['ffn_w_dw'][i] = _perm_cols(jnp.concatenate([dwd0.sum(1), dwd1.sum(1)], axis=-1))
        sg['ffn_b_dw'][i] = _perm_cols(jnp.concatenate([dbd0.sum(0), dbd1.sum(0)], axis=-1))
        wgrad('ffn_w_up', i, sv['x1b'], dh, tm=D, tn=F2 // N_CHIPS, pieces=('col', True))
        launch(f"ffn{i}")
        sg['norm1_g'][i], sg['norm1_b'][i] = dg.sum(0), db.sum(0)
        if i % 2 == 0:
            sg['conv_b_out'][j] = cs.sum(0)
            wgrad('conv_w_out', j, sv['act'], dz1b, tm=_tile(C, 1024), tn=_tile(D, 1024, LANES), pieces=('row',))
            ddw, dg, db = _ln_silu_bwd(dz1b, wts['conv_w_out', j], sv['xhc'], sv['rsc'], row(full['conv_ln_g'][j]),
                                       row(full['conv_ln_b'][j]), name=f"conv_ln_bwd_{j}")
            sg['conv_ln_g'][j], sg['conv_ln_b'][j] = dg.sum(0), db.sum(0)
            dglu, dwk, dbk = _conv_bwd(ddw, sv['h1'], sv['wdw'], B=B, S=S, name=f"conv_dw_bwd_{j}")
            sg['conv_w_dw'][j] = dwk.sum(1)[:conv_w_dw.shape[1]]
            sg['conv_b_dw'][j] = dbk.sum(0)
            dh1, csi = _glu_bwd(dglu, sv['h1'], name=f"conv_glu_bwd_{j}")
            sg['conv_b_in'][j] = _perm_cols(csi.sum(0))
            fam = 'conv_w_in'
        else:
            sg['gmlp_b_out'][j] = cs.sum(0)
            wgrad('gmlp_w_out', j, sv['act'], dz1b, tm=_tile(C, 1024), tn=_tile(D, 1024, LANES), pieces=('row',))
            dh1, dg, db, csi, dws, dbs = _gmlp_gate_bwd(dz1b, wts['gmlp_w_out', j], sv['pre'], sv['xhv'], sv['rsv'],
                                                        row(full['gmlp_ln_g'][j]), row(full['gmlp_ln_b'][j]),
                                                        gmlp_w_s[j], sv['bsb'], name=f"gmlp_gate_bwd_{j}")
            sg['gmlp_ln_g'][j], sg['gmlp_ln_b'][j] = dg.sum(0), db.sum(0)
            sg['gmlp_b_in'][j] = _perm_cols(csi.sum(0))
            sg['gmlp_w_s'][j] = dws
            sg['gmlp_b_s'][j] = dbs.reshape(L, G, L).sum(-1).T
            fam = 'gmlp_w_in'
        wgrad(fam, j, sv['xb'], dh1, tm=D, tn=(2 * C) // N_CHIPS, pieces=('col', True))
        launch(fam[:-5] + str(j))
        if i > 0:
            below = saved[i - 1]
            dcur = _mm_ln_bwd(dh1, wts[fam, j], dz1, alpha, below['xh2'], below['rs2'], row(norm2_g[i - 1]),
                              name=f"{fam}_dx_{j}", deps=deps)
        else:
            dcur = _mm(dh1, wts[fam, j], bl=0, tb=True, res=dz1, res_scale=alpha, tm=_tile(T, 512),
                       tn=_tile(D, 1024, LANES), tk=2 * C, name=f"{fam}_dx_{j}", deps=deps)
    grad_x = dcur.reshape(B, S, D)

    small_names = [n for n in WEIGHTS if n not in BIG]
    small_full = [jnp.stack(sg[n]) for n in small_names]
    flat = _pack(small_full + [loss_part])
    red = _allreduce_flat(flat, name="ar_small")
    red_parts = _unpack(red, [a.shape for a in small_full] + [loss_part.shape])
    loss = (0.5 / D) * jnp.sum(red_parts[-1])
    grads = {}
    for n, g in zip(small_names, red_parts[:-1]):
        if n in SMALL_SHARDED:
            ax = SMALL_SHARDED[n]
            width = P[n].shape[ax]
            g = lax.dynamic_slice_in_dim(g, shard * width, width, axis=ax)
        grads[n] = g

    big_out = {}
    for n in ['ffn_w_down', 'ffn_w_up', 'gmlp_w_out', 'gmlp_w_in', 'conv_w_out', 'conv_w_in']:
        own = None
        n_layers = len(inflight[n])
        for l in reversed(range(n_layers)):
            pc_, r = landed(n, l)
            own = _sum_pieces(pc_, r, me_id, l, own, n_layers, name=f"sum_{n}_{l}")
        got = _pair_exchange(own, name=f"px_{n}")
        big_out[n] = _adam_halves(P[n], own, got, P['m_' + n], P['v_' + n], core_id, name=f"adam_{n}")

    shapes = [P[n].shape for n in small_names]
    n_small = sum(functools.reduce(lambda p_, d_: p_ * d_, s_, 1) for s_ in shapes)
    unit = SUBLANES * LANES
    npad = -(-n_small // unit) * unit

    def flat2d(arrs, fill=0.0):
        v = _pack(arrs)
        return jnp.pad(v, (0, npad - n_small), constant_values=fill).reshape(-1, LANES)

    dl, mo, vo = _adam(flat2d([P[n] for n in small_names]), flat2d([grads[n] for n in small_names]),
                       flat2d([P['m_' + n] for n in small_names]),
                       flat2d([P['v_' + n] for n in small_names], fill=1.0), name="adam_small")
    small_out = {n: [grads[n], None, None, None] for n in small_names}
    for k, t in enumerate((dl, mo, vo)):
        for n, a in zip(small_names, _unpack(t.reshape(-1), shapes)):
            small_out[n][k + 1] = a

    outs = [loss, grad_x]
    for k in range(4):
        for n in WEIGHTS:
            outs.append(big_out[n][k] if n in BIG else small_out[n][k])
    return tuple(outs)
```

```python
import functools

import jax
import jax.numpy as jnp
from jax import lax
from jax.experimental import pallas as pl
from jax.experimental.pallas import tpu as pltpu

F32 = jnp.float32
_MXU = jnp.bfloat16
_WIRE = jnp.bfloat16
_HDT = jnp.bfloat16
_ADT = jnp.bfloat16
LN_EPS = 1e-5
ADAM_LR, ADAM_B1, ADAM_B2, ADAM_EPS, ADAM_WD, ADAM_STEP = 0.001, 0.9, 0.999, 1e-08, 0.01, 10
N_CHIPS = 4
N_DEV = 8
LANES = 128
SUBLANES = 8
CONV_TAPS_PAD = 32
VMEM_LIMIT = 56 << 20
MESH = pl.DeviceIdType.MESH
ANY = pl.BlockSpec(memory_space=pl.ANY)
HBM = pl.BlockSpec(memory_space=pltpu.HBM)
SEMS = pl.BlockSpec(memory_space=pltpu.SEMAPHORE)
EFFECT = pltpu.SideEffectType.DATAFLOW_SIDE_EFFECTING
PERM = (0, 2, 1, 3)


def _cp(sem=None):
    return pltpu.CompilerParams(dimension_semantics=sem, vmem_limit_bytes=VMEM_LIMIT)


def _tile(dim, pref, mult=SUBLANES):
    if dim <= pref:
        return dim
    t = (pref // mult) * mult
    while t > mult and dim % t:
        t -= mult
    assert dim % t == 0, (dim, pref, mult)
    return t


def _perm_idx(q):
    return (q % 2) * 2 + q // 2


def _fold8(t):
    r, n = t.shape
    return t.reshape(r // SUBLANES, SUBLANES, n).sum(axis=0)


def _ln_rows(z, g, b):
    mu = jnp.mean(z, axis=-1, keepdims=True)
    xc = z - mu
    var = jnp.mean(xc * xc, axis=-1, keepdims=True)
    rstd = lax.rsqrt(var + LN_EPS)
    xh = xc * rstd
    return xh * g + b, xh, rstd


def _ln_bwd_rows(dy, xh, rstd, g):
    dxh = dy * g
    m1 = jnp.mean(dxh, axis=-1, keepdims=True)
    m2 = jnp.mean(dxh * xh, axis=-1, keepdims=True)
    return rstd * (dxh - m1 - xh * m2)


def _sigmoid(v):
    return 0.5 * jnp.tanh(0.5 * v) + 0.5


def _gelu_parts(p):
    cdf = 0.5 * (1.0 + lax.erf(p * 0.7071067811865476))
    pdf = jnp.exp(-0.5 * p * p) * 0.3989422804014327
    return p * cdf, cdf + p * pdf


def _shift_down(prev8, t, s):
    ext = jnp.concatenate([prev8, t], axis=0)
    return pltpu.roll(ext, s, 0)[SUBLANES:]


def _shift_up(t, next8, s):
    n = t.shape[0]
    ext = jnp.concatenate([t, next8], axis=0)
    return pltpu.roll(ext, n + SUBLANES - s, 0)[:n]


def _mm(a, b, *, ta=False, tb=False, bl=None, bias=None, res=None, res_scale=1.0, out_dtype=F32,
        tm, tn, tk, name, pieces=None, deps=None, n_outer=False):
    M, K = (a.shape[1], a.shape[0]) if ta else a.shape
    bs = b.shape[1:] if bl is not None else b.shape
    N, Kb = (bs[0], bs[1]) if tb else (bs[1], bs[0])
    assert K == Kb and M % tm == 0 and N % tn == 0 and K % tk == 0, (a.shape, b.shape, tm, tn, tk)
    gm, gn, gk = M // tm, N // tn, K // tk

    def spec(block, imap):
        if n_outer:
            return pl.BlockSpec(block, lambda j, i, k: imap(i, j, k))
        return pl.BlockSpec(block, imap)

    a_spec = spec((tk, tm), lambda i, j, k: (k, i)) if ta else spec((tm, tk), lambda i, j, k: (i, k))
    bblk = (tn, tk) if tb else (tk, tn)
    bmap = (lambda i, j, k: (j, k)) if tb else (lambda i, j, k: (k, j))
    if bl is not None:
        b_spec = spec((None,) + bblk, lambda i, j, k: (bl,) + bmap(i, j, k))
    else:
        b_spec = spec(bblk, bmap)
    in_specs, operands = [a_spec, b_spec], [a, b]
    if bias is not None:
        in_specs.append(spec((1, tn), lambda i, j, k: (0, j)))
        operands.append(bias)
    if res is not None:
        in_specs.append(spec((tm, tn), lambda i, j, k: (i, j)))
        operands.append(res)
    n_dep = len(deps) if deps else 0
    if n_dep:
        in_specs += [ANY] * n_dep
        operands += deps
        del deps[:]
    if pieces is None:
        out_shape = jax.ShapeDtypeStruct((M, N), out_dtype)
        out_spec = spec((tm, tn), lambda i, j, k: (i, j))
        ppb = pr = None
    elif pieces[0] == 'col':
        pr, pc = M // 2, N // N_CHIPS
        assert tm % pr == 0 and pc % tn == 0
        ppb, per = tm // pr, pc // tn
        perm = pieces[1]
        out_shape = jax.ShapeDtypeStruct((N_DEV, pr, pc), out_dtype)
        out_spec = spec(
            (ppb, pr, tn),
            lambda i, j, k: ((2 * (_perm_idx(j // per) if perm else j // per)) // ppb + i, 0, j % per))
    else:
        pr = M // N_DEV
        assert tm % pr == 0
        ppb = tm // pr
        out_shape = jax.ShapeDtypeStruct((N_DEV, pr, N), out_dtype)
        out_spec = spec((ppb, pr, tn), lambda i, j, k: (i, 0, j))
    dims = (((0 if ta else 1,), (1 if tb else 0,)), ((), ()))

    def body(*refs):
        a_ref, b_ref = refs[0], refs[1]
        pos = 2
        bias_ref = res_ref = None
        if bias is not None:
            bias_ref = refs[pos]
            pos += 1
        if res is not None:
            res_ref = refs[pos]
            pos += 1
        pos += n_dep
        o_ref = refs[pos]

        def finish(r):
            if bias_ref is not None:
                r = r + bias_ref[...]
            if res_ref is not None:
                r = r + res_scale * res_ref[...]
            if pieces is not None:
                r = r.reshape(ppb, pr, tn)
            o_ref[...] = r.astype(out_dtype)

        part = lax.dot_general(a_ref[...].astype(_MXU), b_ref[...].astype(_MXU), dims, preferred_element_type=F32)
        if gk == 1:
            finish(part)
            return
        acc_ref = refs[pos + 1]
        k = pl.program_id(2)

        @pl.when(k == 0)
        def _():
            acc_ref[...] = part

        @pl.when((k > 0) & (k < gk - 1))
        def _():
            acc_ref[...] += part

        @pl.when(k == gk - 1)
        def _():
            finish(acc_ref[...] + part)

    return pl.pallas_call(
        body, name=name, grid=(gn, gm, gk) if n_outer else (gm, gn, gk), in_specs=in_specs, out_specs=out_spec,
        out_shape=out_shape, scratch_shapes=[pltpu.VMEM((tm, tn), F32)] if gk > 1 else [],
        compiler_params=_cp(("parallel", "parallel", "arbitrary")),
    )(*operands)


def _mm_ln_bwd(a, w, res, res_scale, xh, rstd, g, *, name, deps=None):
    T, K = a.shape
    D = w.shape[1]
    tm = _tile(T, 512)
    n_dep = len(deps) if deps else 0

    def body(a_ref, w_ref, res_ref, xh_ref, rs_ref, g_ref, *rest):
        dz_ref, dzb_ref, dg_ref, db_ref, cs_ref = rest[n_dep:]

        @pl.when(pl.program_id(0) == 0)
        def _():
            dg_ref[...] = jnp.zeros_like(dg_ref)
            db_ref[...] = jnp.zeros_like(db_ref)
            cs_ref[...] = jnp.zeros_like(cs_ref)

        d = lax.dot_general(a_ref[...].astype(_MXU), w_ref[...].astype(_MXU), (((1,), (1,)), ((), ())),
                            preferred_element_type=F32) + res_scale * res_ref[...]
        xh = xh_ref[...]
        dz = _ln_bwd_rows(d, xh, rs_ref[...], g_ref[...])
        dz_ref[...] = dz
        dzb_ref[...] = dz.astype(_MXU)
        dg_ref[...] += _fold8(d * xh)
        db_ref[...] += _fold8(d)
        cs_ref[...] += _fold8(dz)

    row = lambda i: (i, 0)
    fixed = lambda i: (0, 0)
    tile = pl.BlockSpec((tm, D), row)
    part = pl.BlockSpec((SUBLANES, D), fixed)
    operands = [a, w, res, xh, rstd, g] + (list(deps) if deps else [])
    if deps:
        del deps[:]
    return pl.pallas_call(
        body, name=name, grid=(T // tm,),
        in_specs=[pl.BlockSpec((tm, K), row),
                  pl.BlockSpec((None, D, K), lambda i: (0, 0, 0), pipeline_mode=pl.Buffered(1)),
                  tile, tile, pl.BlockSpec((tm, 1), row), pl.BlockSpec((1, D), fixed)] + [ANY] * n_dep,
        out_specs=[tile, tile, part, part, part],
        out_shape=[jax.ShapeDtypeStruct((T, D), F32), jax.ShapeDtypeStruct((T, D), _MXU)]
        + [jax.ShapeDtypeStruct((SUBLANES, D), F32)] * 3,
        compiler_params=_cp(("arbitrary",)),
    )(*operands)


def _out_ln(act, wo_ref, bias_ref, res_ref, alpha, g_ref, b_ref, y_ref, yb_ref, xh_ref, rs_ref):
    z = jnp.dot(act, wo_ref[...].astype(_MXU), preferred_element_type=F32) + bias_ref[...] + alpha * res_ref[...]
    y, xh, rstd = _ln_rows(z, g_ref[...], b_ref[...])
    y_ref[...] = y
    yb_ref[...] = y.astype(_MXU)
    xh_ref[...] = xh
    rs_ref[...] = rstd


def _conv_tail_fwd(v, gc, bc, w, bias, res, alpha, g, b, *, name):
    T, C = v.shape
    D = w.shape[-1]
    tm = _tile(T, 256)

    def body(v_ref, gc_ref, bc_ref, w_ref, bias_ref, res_ref, g_ref, b_ref,
             s_ref, xhc_ref, rsc_ref, y_ref, yb_ref, xh_ref, rs_ref):
        yv, xhc, rsc = _ln_rows(v_ref[...], gc_ref[...], bc_ref[...])
        s = (yv * _sigmoid(yv)).astype(_MXU)
        s_ref[...] = s
        xhc_ref[...] = xhc
        rsc_ref[...] = rsc
        _out_ln(s, w_ref, bias_ref, res_ref, alpha, g_ref, b_ref, y_ref, yb_ref, xh_ref, rs_ref)

    row = lambda i: (i, 0)
    fixed = lambda i: (0, 0)
    vc, vd = pl.BlockSpec((1, C), fixed), pl.BlockSpec((1, D), fixed)
    tc_, td = pl.BlockSpec((tm, C), row), pl.BlockSpec((tm, D), row)
    one = pl.BlockSpec((tm, 1), row)
    return pl.pallas_call(
        body, name=name, grid=(T // tm,),
        in_specs=[tc_, vc, vc, _resident((None, C, D), lambda i: (0, 0, 0)), vd, td, vd, vd],
        out_specs=[tc_, tc_, one, td, td, td, one],
        out_shape=[jax.ShapeDtypeStruct((T, C), _MXU), jax.ShapeDtypeStruct((T, C), F32),
                   jax.ShapeDtypeStruct((T, 1), F32), jax.ShapeDtypeStruct((T, D), F32),
                   jax.ShapeDtypeStruct((T, D), _MXU), jax.ShapeDtypeStruct((T, D), F32),
                   jax.ShapeDtypeStruct((T, 1), F32)],
        compiler_params=_cp(("parallel",)),
    )(v, gc, bc, w, bias, res, g, b)


def _ln_bwd(dy, xh, rstd, g, *, name, target=None):
    T, D = dy.shape
    tm = _tile(T, 256)
    head = target is not None

    def body(*refs):
        if head:
            dy_ref, t_ref, xh_ref, rs_ref, g_ref, dz_ref, dzb_ref, dg_ref, db_ref, cs_ref, ls_ref = refs
        else:
            dy_ref, xh_ref, rs_ref, g_ref, dz_ref, dzb_ref, dg_ref, db_ref, cs_ref = refs
        i = pl.program_id(0)

        @pl.when(i == 0)
        def _():
            dg_ref[...] = jnp.zeros_like(dg_ref)
            db_ref[...] = jnp.zeros_like(db_ref)
            cs_ref[...] = jnp.zeros_like(cs_ref)
            if head:
                ls_ref[...] = jnp.zeros_like(ls_ref)

        d = dy_ref[...]
        if head:
            err = d - t_ref[...]
            ls_ref[...] += _fold8(err * err)
            d = err * (1.0 / D)
        xh = xh_ref[...]
        dz = _ln_bwd_rows(d, xh, rs_ref[...], g_ref[...])
        dz_ref[...] = dz
        dzb_ref[...] = dz.astype(_MXU)
        dg_ref[...] += _fold8(d * xh)
        db_ref[...] += _fold8(d)
        cs_ref[...] += _fold8(dz)

    row = lambda i: (i, 0)
    fixed = lambda i: (0, 0)
    tile = pl.BlockSpec((tm, D), row)
    part = pl.BlockSpec((SUBLANES, D), fixed)
    in_specs = [tile] + ([tile] if head else []) + [tile, pl.BlockSpec((tm, 1), row), pl.BlockSpec((1, D), fixed)]
    n_part = 4 if head else 3
    operands = [dy] + ([target] if head else []) + [xh, rstd, g]
    return pl.pallas_call(
        body, name=name, grid=(T // tm,), in_specs=in_specs,
        out_specs=[tile, tile] + [part] * n_part,
        out_shape=[jax.ShapeDtypeStruct((T, D), F32), jax.ShapeDtypeStruct((T, D), _MXU)]
        + [jax.ShapeDtypeStruct((SUBLANES, D), F32)] * n_part,
        compiler_params=_cp(("arbitrary",)),
    )(*operands)


def _conv_cols(C, tc):
    per = (C // 2) // tc
    return per, (lambda j: (j // per) * (2 * per) + j % per)


def _glu_shifted(a_ref, g_ref, p_ref, S):
    u = a_ref[...].astype(F32) * _sigmoid(g_ref[...].astype(F32))
    rows = lax.broadcasted_iota(jnp.int32, (SUBLANES, u.shape[1]), 0)
    lo = CONV_TAPS_PAD
    for r in range(SUBLANES):
        p_ref[r, 0:lo, :] = jnp.zeros((lo, u.shape[1]), F32)
        if r == 0:
            p_ref[r, lo:lo + S, :] = u
        else:
            rolled = pltpu.roll(u, r, 0)
            p_ref[r, lo:lo + S, :] = rolled
            p_ref[r, lo:lo + SUBLANES, :] = jnp.where(rows >= r, rolled[0:SUBLANES], 0.0)


def _conv_fwd(h1, w_dw, b_dw, *, B, S, name):
    C = w_dw.shape[1]
    taps = CONV_TAPS_PAD - 1
    tc = LANES
    ch = _tile(S, 128)
    per, col_a = _conv_cols(C, tc)

    def body(a_ref, g_ref, w_ref, b_ref, o_ref, p_ref):
        _glu_shifted(a_ref, g_ref, p_ref, S)

        def chunk(ci, carry):
            base = pl.multiple_of(ci * ch, ch)
            acc = jnp.zeros((ch, tc), F32) + b_ref[...]
            for k in range(taps):
                q, r = divmod(taps - 1 - k, SUBLANES)
                start = pl.multiple_of(base + (CONV_TAPS_PAD - SUBLANES * q), SUBLANES)
                acc = acc + w_ref[pl.ds(k, 1), :] * p_ref[r, pl.ds(start, ch), :]
            o_ref[pl.ds(base, ch), :] = acc
            return carry

        lax.fori_loop(0, S // ch, chunk, 0)

    return pl.pallas_call(
        body, name=name, grid=(B, C // tc),
        in_specs=[pl.BlockSpec((S, tc), lambda b, j: (b, col_a(j))),
                  pl.BlockSpec((S, tc), lambda b, j: (b, col_a(j) + per)),
                  pl.BlockSpec((CONV_TAPS_PAD, tc), lambda b, j: (0, j)),
                  pl.BlockSpec((1, tc), lambda b, j: (0, j))],
        out_specs=pl.BlockSpec((S, tc), lambda b, j: (b, j)),
        out_shape=jax.ShapeDtypeStruct((B * S, C), F32),
        scratch_shapes=[pltpu.VMEM((SUBLANES, S + CONV_TAPS_PAD, tc), F32)],
        compiler_params=_cp(("parallel", "parallel")),
    )(h1, h1, w_dw, b_dw)


def _conv_bwd(dd, h1, w_dw, *, B, S, name):
    C = w_dw.shape[1]
    taps = CONV_TAPS_PAD - 1
    tc = LANES
    ch = _tile(S, 128)
    per, col_a = _conv_cols(C, tc)

    def body(d_ref, a_ref, g_ref, w_ref, du_ref, dw_ref, db_ref, p_ref, q_ref):
        b = pl.program_id(1)

        @pl.when(b == 0)
        def _():
            dw_ref[...] = jnp.zeros_like(dw_ref)
            db_ref[...] = jnp.zeros_like(db_ref)

        _glu_shifted(a_ref, g_ref, p_ref, S)
        d = d_ref[...]
        rows = lax.broadcasted_iota(jnp.int32, (SUBLANES, tc), 0)
        for r in range(SUBLANES):
            q_ref[r, S:S + CONV_TAPS_PAD, :] = jnp.zeros((CONV_TAPS_PAD, tc), F32)
            if r == 0:
                q_ref[r, 0:S, :] = d
            else:
                rolled = pltpu.roll(d, S - r, 0)
                q_ref[r, 0:S, :] = rolled
                q_ref[r, S - SUBLANES:S, :] = jnp.where(rows < SUBLANES - r, rolled[S - SUBLANES:S], 0.0)
        db_ref[...] += _fold8(d)

        def chunk(ci, carry):
            base = pl.multiple_of(ci * ch, ch)
            dch = d_ref[pl.ds(base, ch), :]
            acc = jnp.zeros((ch, tc), F32)
            for k in range(taps):
                q, r = divmod(taps - 1 - k, SUBLANES)
                up = pl.multiple_of(base + SUBLANES * q, SUBLANES)
                acc = acc + w_ref[pl.ds(k, 1), :] * q_ref[r, pl.ds(up, ch), :]
                down = pl.multiple_of(base + (CONV_TAPS_PAD - SUBLANES * q), SUBLANES)
                dw_ref[k] += _fold8(dch * p_ref[r, pl.ds(down, ch), :])
            du_ref[pl.ds(base, ch), :] = acc
            return carry

        lax.fori_loop(0, S // ch, chunk, 0)

    return pl.pallas_call(
        body, name=name, grid=(C // tc, B),
        in_specs=[pl.BlockSpec((S, tc), lambda j, b: (b, j)),
                  pl.BlockSpec((S, tc), lambda j, b: (b, col_a(j))),
                  pl.BlockSpec((S, tc), lambda j, b: (b, col_a(j) + per)),
                  pl.BlockSpec((CONV_TAPS_PAD, tc), lambda j, b: (0, j))],
        out_specs=[pl.BlockSpec((S, tc), lambda j, b: (b, j)),
                   pl.BlockSpec((CONV_TAPS_PAD, SUBLANES, tc), lambda j, b: (0, 0, j)),
                   pl.BlockSpec((SUBLANES, tc), lambda j, b: (0, j))],
        out_shape=[jax.ShapeDtypeStruct((B * S, C), F32),
                   jax.ShapeDtypeStruct((CONV_TAPS_PAD, SUBLANES, C), F32),
                   jax.ShapeDtypeStruct((SUBLANES, C), F32)],
        scratch_shapes=[pltpu.VMEM((SUBLANES, S + CONV_TAPS_PAD, tc), F32),
                        pltpu.VMEM((SUBLANES, S + CONV_TAPS_PAD, tc), F32)],
        compiler_params=_cp(("parallel", "arbitrary")),
    )(dd, h1, h1, w_dw)


def _ln_silu_bwd(dzb, w, xh, rstd, g, b, *, name):
    T, D = dzb.shape
    C = w.shape[1]
    tm = _tile(T, 512)

    def body(dz_ref, w_ref, xh_ref, rs_ref, g_ref, b_ref, dv_ref, dg_ref, db_ref):
        @pl.when(pl.program_id(0) == 0)
        def _():
            dg_ref[...] = jnp.zeros_like(dg_ref)
            db_ref[...] = jnp.zeros_like(db_ref)

        ds = lax.dot_general(dz_ref[...].astype(_MXU), w_ref[...].astype(_MXU), (((1,), (1,)), ((), ())),
                             preferred_element_type=F32)
        xh = xh_ref[...]
        gam = g_ref[...]
        y = xh * gam + b_ref[...]
        sig = _sigmoid(y)
        dln = ds * (sig * (1.0 + y * (1.0 - sig)))
        dv_ref[...] = _ln_bwd_rows(dln, xh, rs_ref[...], gam)
        dg_ref[...] += _fold8(dln * xh)
        db_ref[...] += _fold8(dln)

    row = lambda i: (i, 0)
    fixed = lambda i: (0, 0)
    vec = pl.BlockSpec((1, C), fixed)
    part = pl.BlockSpec((SUBLANES, C), fixed)
    return pl.pallas_call(
        body, name=name, grid=(T // tm,),
        in_specs=[pl.BlockSpec((tm, D), row), _resident((None, C, D), lambda i: (0, 0, 0)),
                  pl.BlockSpec((tm, C), row), pl.BlockSpec((tm, 1), row), vec, vec],
        out_specs=[pl.BlockSpec((tm, C), row), part, part],
        out_shape=[jax.ShapeDtypeStruct((T, C), F32)] + [jax.ShapeDtypeStruct((SUBLANES, C), F32)] * 2,
        compiler_params=_cp(("arbitrary",)),
    )(dzb, w, xh, rstd, g, b)


def _glu_bwd(du, h1, *, name):
    T, C = du.shape
    il = C // 2
    tm = _tile(T, 256)

    def body(du_ref, h_ref, dh_ref, cs_ref):
        @pl.when(pl.program_id(0) == 0)
        def _():
            cs_ref[...] = jnp.zeros_like(cs_ref)

        for hb in range(2):
            a = h_ref[:, 2 * hb * il:(2 * hb + 1) * il].astype(F32)
            gate = h_ref[:, (2 * hb + 1) * il:(2 * hb + 2) * il].astype(F32)
            d = du_ref[:, hb * il:(hb + 1) * il]
            sig = _sigmoid(gate)
            da = d * sig
            dgate = d * a * sig * (1.0 - sig)
            dh_ref[:, 2 * hb * il:(2 * hb + 1) * il] = da.astype(_MXU)
            dh_ref[:, (2 * hb + 1) * il:(2 * hb + 2) * il] = dgate.astype(_MXU)
            cs_ref[:, 2 * hb * il:(2 * hb + 1) * il] += _fold8(da)
            cs_ref[:, (2 * hb + 1) * il:(2 * hb + 2) * il] += _fold8(dgate)

    row = lambda i: (i, 0)
    return pl.pallas_call(
        body, name=name, grid=(T // tm,),
        in_specs=[pl.BlockSpec((tm, C), row), pl.BlockSpec((tm, 2 * C), row)],
        out_specs=[pl.BlockSpec((tm, 2 * C), row), pl.BlockSpec((SUBLANES, 2 * C), lambda i: (0, 0))],
        out_shape=[jax.ShapeDtypeStruct((T, 2 * C), _MXU), jax.ShapeDtypeStruct((SUBLANES, 2 * C), F32)],
        compiler_params=_cp(("arbitrary",)),
    )(du, h1)


def _tril_mask(n):
    return lax.broadcasted_iota(jnp.int32, (n, n), 0) >= lax.broadcasted_iota(jnp.int32, (n, n), 1)


def _split_uv(t, il):
    u = jnp.concatenate([t[:, 0:il], t[:, 2 * il:3 * il]], axis=1)
    v = jnp.concatenate([t[:, il:2 * il], t[:, 3 * il:4 * il]], axis=1)
    return u, v


def _gmlp_gate_fwd(p, g, b, w_s, bsb, w_out, bias, res, alpha, g1, b1, *, name):
    T, C2 = p.shape
    C = C2 // 2
    D = w_out.shape[-1]
    il = C // 2
    G, L, _ = w_s.shape
    assert G * L == C
    tm = _tile(T, 2 * L, L)

    def body(p_ref, g_ref, b_ref, ws_ref, bs_ref, wo_ref, bias_ref, res_ref, g1_ref, b1_ref,
             us_ref, xh_ref, rs_ref, y_ref, yb_ref, xh1_ref, rs1_ref, vn_ref, u_ref):
        z, _ = _gelu_parts(p_ref[...].astype(F32))
        u, v = _split_uv(z, il)
        vn, xh, rstd = _ln_rows(v, g_ref[...], b_ref[...])
        xh_ref[...] = xh
        rs_ref[...] = rstd
        vn_ref[...] = vn.astype(_MXU)
        u_ref[...] = u
        mask = _tril_mask(L)
        for gi in range(G):
            wc = jnp.where(mask, ws_ref[gi], 0.0).astype(_MXU)
            cols = slice(gi * L, (gi + 1) * L)
            for c in range(tm // L):
                rows = slice(c * L, (c + 1) * L)
                s = jnp.dot(wc, vn_ref[rows, cols], preferred_element_type=F32) + bs_ref[:, cols]
                us_ref[rows, cols] = (u_ref[rows, cols] * s).astype(_MXU)
        _out_ln(us_ref[...], wo_ref, bias_ref, res_ref, alpha, g1_ref, b1_ref, y_ref, yb_ref, xh1_ref, rs1_ref)

    row = lambda i: (i, 0)
    fixed = lambda i: (0, 0)
    vd, td, one = pl.BlockSpec((1, D), fixed), pl.BlockSpec((tm, D), row), pl.BlockSpec((tm, 1), row)
    return pl.pallas_call(
        body, name=name, grid=(T // tm,),
        in_specs=[pl.BlockSpec((tm, C2), row), pl.BlockSpec((1, C), fixed), pl.BlockSpec((1, C), fixed),
                  pl.BlockSpec((G, L, L), lambda i: (0, 0, 0)), pl.BlockSpec((L, C), fixed),
                  _resident((None, C, D), lambda i: (0, 0, 0)), vd, td, vd, vd],
        out_specs=[pl.BlockSpec((tm, C), row), pl.BlockSpec((tm, C), row), one, td, td, td, one],
        out_shape=[jax.ShapeDtypeStruct((T, C), _MXU), jax.ShapeDtypeStruct((T, C), F32),
                   jax.ShapeDtypeStruct((T, 1), F32), jax.ShapeDtypeStruct((T, D), F32),
                   jax.ShapeDtypeStruct((T, D), _MXU), jax.ShapeDtypeStruct((T, D), F32),
                   jax.ShapeDtypeStruct((T, 1), F32)],
        scratch_shapes=[pltpu.VMEM((tm, C), _MXU), pltpu.VMEM((tm, C), F32)],
        compiler_params=_cp(("parallel",)),
    )(p, g, b, w_s, bsb, w_out, bias, res, g1, b1)


def _gmlp_gate_bwd(dzb, w_out, p, xh, rstd, g, b, w_s, bsb, *, name):
    T, C2 = p.shape
    D = dzb.shape[1]
    C = C2 // 2
    il = C // 2
    G, L, _ = w_s.shape
    tm = _tile(T, 2 * L, L)

    def body(dz_ref, wo_ref, p_ref, xh_ref, rs_ref, g_ref, b_ref, ws_ref, bs_ref,
             dp_ref, dg_ref, db_ref, cs_ref, dws_ref, dbs_ref, vn_ref, u_ref, dvn_ref, du_ref, dus_ref):
        @pl.when(pl.program_id(0) == 0)
        def _():
            dg_ref[...] = jnp.zeros_like(dg_ref)
            db_ref[...] = jnp.zeros_like(db_ref)
            cs_ref[...] = jnp.zeros_like(cs_ref)
            dws_ref[...] = jnp.zeros_like(dws_ref)
            dbs_ref[...] = jnp.zeros_like(dbs_ref)

        dus_ref[...] = lax.dot_general(dz_ref[...].astype(_MXU), wo_ref[...].astype(_MXU), (((1,), (1,)), ((), ())),
                                       preferred_element_type=F32)
        z, gp = _gelu_parts(p_ref[...].astype(F32))
        u, _ = _split_uv(z, il)
        xh = xh_ref[...]
        gam = g_ref[...]
        vn_ref[...] = (xh * gam + b_ref[...]).astype(_MXU)
        u_ref[...] = u
        mask = _tril_mask(L)
        for gi in range(G):
            wc = jnp.where(mask, ws_ref[gi], 0.0).astype(_MXU)
            cols = slice(gi * L, (gi + 1) * L)
            for c in range(tm // L):
                rows = slice(c * L, (c + 1) * L)
                vnb = vn_ref[rows, cols]
                s = jnp.dot(wc, vnb, preferred_element_type=F32) + bs_ref[:, cols]
                d = dus_ref[rows, cols]
                du_ref[rows, cols] = d * s
                ds = d * u_ref[rows, cols]
                dbs_ref[:, cols] += ds
                dsb = ds.astype(_MXU)
                dw = lax.dot_general(dsb, vnb, (((1,), (1,)), ((), ())), preferred_element_type=F32)
                dws_ref[gi] += jnp.where(mask, dw, 0.0)
                dvn_ref[rows, cols] = lax.dot_general(wc, dsb, (((0,), (0,)), ((), ())), preferred_element_type=F32)
        dvn = dvn_ref[...]
        dg_ref[...] += _fold8(dvn * xh)
        db_ref[...] += _fold8(dvn)
        dv = _ln_bwd_rows(dvn, xh, rs_ref[...], gam)
        du = du_ref[...]
        for hb in range(2):
            for part, src in ((0, du), (1, dv)):
                lo = (2 * hb + part) * il
                dp = src[:, hb * il:(hb + 1) * il] * gp[:, lo:lo + il]
                dp_ref[:, lo:lo + il] = dp.astype(_MXU)
                cs_ref[:, lo:lo + il] += _fold8(dp)

    row = lambda i: (i, 0)
    fixed = lambda i: (0, 0)
    part_c = pl.BlockSpec((SUBLANES, C), fixed)
    return pl.pallas_call(
        body, name=name, grid=(T // tm,),
        in_specs=[pl.BlockSpec((tm, D), row), _resident((None, C, D), lambda i: (0, 0, 0)),
                  pl.BlockSpec((tm, C2), row), pl.BlockSpec((tm, C), row),
                  pl.BlockSpec((tm, 1), row), pl.BlockSpec((1, C), fixed), pl.BlockSpec((1, C), fixed),
                  pl.BlockSpec((G, L, L), lambda i: (0, 0, 0)), pl.BlockSpec((L, C), fixed)],
        out_specs=[pl.BlockSpec((tm, C2), row), part_c, part_c, pl.BlockSpec((SUBLANES, C2), fixed),
                   pl.BlockSpec((G, L, L), lambda i: (0, 0, 0)), pl.BlockSpec((L, C), fixed)],
        out_shape=[jax.ShapeDtypeStruct((T, C2), _MXU), jax.ShapeDtypeStruct((SUBLANES, C), F32),
                   jax.ShapeDtypeStruct((SUBLANES, C), F32), jax.ShapeDtypeStruct((SUBLANES, C2), F32),
                   jax.ShapeDtypeStruct((G, L, L), F32), jax.ShapeDtypeStruct((L, C), F32)],
        scratch_shapes=[pltpu.VMEM((tm, C), _MXU), pltpu.VMEM((tm, C), F32), pltpu.VMEM((tm, C), F32),
                        pltpu.VMEM((tm, C), F32), pltpu.VMEM((tm, C), F32)],
        compiler_params=_cp(("arbitrary",)),
    )(dzb, w_out, p, xh, rstd, g, b, w_s, bsb)


def _ffn_conv(h, prev8, w_ref, b_ref):
    h1 = _shift_down(prev8, h, 1)
    h2 = _shift_down(prev8, h, 2)
    return w_ref[pl.ds(2, 1), :] * h + w_ref[pl.ds(1, 1), :] * h1 + w_ref[pl.ds(0, 1), :] * h2 + b_ref[...]


def _resident(block, imap):
    return pl.BlockSpec(block, imap, pipeline_mode=pl.Buffered(1))


def _ffn_fwd_half(j, xb, w_up, w_down, b_up, w_dw, b_dw, *, S, name, prev=None, tail=None):
    T, D = xb.shape
    N = w_up.shape[-1]
    tn = N // N_CHIPS
    tm = _tile(S, 256)
    spt = S // tm
    last = prev is not None
    alpha = tail[1] if last else None

    def body(*refs):
        x_ref, wu_ref, wd_ref, bu_ref, wc_ref, bc_ref = refs[:6]
        if last:
            yp_ref, res_ref, bd_ref, g_ref, b_ref = refs[9:14]
            h_ref, hc_ref, f_ref, y_ref, yb_ref, xh_ref, rs_ref, carry_ref = refs[14:22]
        else:
            h_ref, hc_ref, f_ref, yp_ref, carry_ref = refs[6:11]

        @pl.when(pl.program_id(0) % spt == 0)
        def _():
            carry_ref[...] = jnp.zeros_like(carry_ref)

        h = jnp.dot(x_ref[...].astype(_MXU), wu_ref[...].astype(_MXU), preferred_element_type=F32) + bu_ref[...]
        h_ref[...] = h.astype(_HDT)
        hc = _ffn_conv(h, carry_ref[...], wc_ref, bc_ref)
        hc_ref[...] = hc.astype(_HDT)
        carry_ref[...] = h[tm - SUBLANES:tm]
        gte = hc[:, :tn]
        f = (gte * _sigmoid(gte) * hc[:, tn:]).astype(_MXU)
        f_ref[...] = f
        y = jnp.dot(f, wd_ref[...].astype(_MXU), preferred_element_type=F32)
        if not last:
            yp_ref[...] = y
            return
        z = y + yp_ref[...] + bd_ref[...] + alpha * res_ref[...]
        out, xh, rstd = _ln_rows(z, g_ref[...], b_ref[...])
        y_ref[...] = out
        yb_ref[...] = out.astype(_MXU)
        xh_ref[...] = xh
        rs_ref[...] = rstd

    row = lambda i: (i, 0)
    pair = lambda i: (0, j)
    vec = pl.BlockSpec((1, D), lambda i: (0, 0))
    tile = pl.BlockSpec((tm, D), row)
    in_specs = [tile, _resident((None, D, 2 * tn), lambda i: (0, 0, j)), _resident((None, tn, D), lambda i: (0, j, 0)),
                pl.BlockSpec((1, 2 * tn), pair), pl.BlockSpec((SUBLANES, 2 * tn), pair), pl.BlockSpec((1, 2 * tn), pair)]
    operands = [xb, w_up, w_down, b_up, w_dw, b_dw]
    wide = pl.BlockSpec((tm, 2 * tn), lambda i: (i, j))
    out_specs = [wide, wide, pl.BlockSpec((tm, tn), lambda i: (i, j))]
    out_shape = [jax.ShapeDtypeStruct((T, N), _HDT), jax.ShapeDtypeStruct((T, N), _HDT),
                 jax.ShapeDtypeStruct((T, N // 2), _MXU)]
    aliases = {}
    if last:
        res, _, b_down, g, b = tail
        in_specs += [ANY, ANY, ANY, tile, tile, vec, vec, vec]
        operands += list(prev) + [res, b_down, g, b]
        aliases = {6: 0, 7: 1, 8: 2}
        out_specs += [tile, tile, tile, pl.BlockSpec((tm, 1), row)]
        out_shape += [jax.ShapeDtypeStruct((T, D), F32), jax.ShapeDtypeStruct((T, D), _MXU),
                      jax.ShapeDtypeStruct((T, D), F32), jax.ShapeDtypeStruct((T, 1), F32)]
    else:
        out_specs.append(tile)
        out_shape.append(jax.ShapeDtypeStruct((T, D), F32))
    return pl.pallas_call(
        body, name=name, grid=(T // tm,), in_specs=in_specs, out_specs=out_specs, out_shape=out_shape,
        input_output_aliases=aliases, scratch_shapes=[pltpu.VMEM((SUBLANES, 2 * tn), F32)],
        compiler_params=_cp(("arbitrary",)),
    )(*operands)


def _ffn_bwd_half(j, dzb, w_down, w_up, hs, hcs, w_dw, *, S, name, dz=None, alpha=None, prev=None, ln=None):
    T, D = dzb.shape
    N = hs.shape[1]
    tn = N // N_CHIPS
    tm = _tile(S, 256)
    spt = S // tm
    nt = T // tm
    last = prev is not None

    def body(*refs):
        dz_ref, wd_ref, wu_ref, h_ref, hc_ref, wc_ref = refs[:6]
        if last:
            dxp_ref, xh_ref, rs_ref, g_ref = refs[7:11]
            dh_ref, cs_ref, dw_ref, db_ref, dz1_ref, dz1b_ref, dg1_ref, db1_ref, cs1_ref, carry_ref = refs[11:21]
        else:
            dzf_ref = refs[6]
            dh_ref, cs_ref, dw_ref, db_ref, dxp_ref, carry_ref = refs[7:13]
        i = pl.program_id(0)
        ii = nt - 1 - i

        @pl.when(i == 0)
        def _():
            cs_ref[...] = jnp.zeros_like(cs_ref)
            dw_ref[...] = jnp.zeros_like(dw_ref)
            db_ref[...] = jnp.zeros_like(db_ref)
            if last:
                dg1_ref[...] = jnp.zeros_like(dg1_ref)
                db1_ref[...] = jnp.zeros_like(db1_ref)
                cs1_ref[...] = jnp.zeros_like(cs1_ref)

        df = lax.dot_general(dz_ref[...].astype(_MXU), wd_ref[...].astype(_MXU), (((1,), (1,)), ((), ())),
                             preferred_element_type=F32)
        h = h_ref[...].astype(F32)
        gte, val = hc_ref[:, :tn].astype(F32), hc_ref[:, tn:].astype(F32)
        sig = _sigmoid(gte)
        dval = df * (gte * sig)
        dg = df * val * (sig * (1.0 + gte * (1.0 - sig)))
        dhc = jnp.concatenate([dg, dval], axis=1)
        nxt = jnp.where((ii + 1) % spt == 0, 0.0, carry_ref[...])
        d1 = _shift_up(dhc, nxt, 1)
        d2 = _shift_up(dhc, nxt, 2)
        carry_ref[...] = dhc[0:SUBLANES]
        db_ref[...] += _fold8(dhc)
        dw_ref[2] += _fold8(dhc * h)
        dw_ref[1] += _fold8(d1 * h)
        dw_ref[0] += _fold8(d2 * h)
        dh = wc_ref[pl.ds(2, 1), :] * dhc + wc_ref[pl.ds(1, 1), :] * d1 + wc_ref[pl.ds(0, 1), :] * d2
        cs_ref[...] += _fold8(dh)
        dhb = dh.astype(_MXU)
        dh_ref[...] = dhb
        dx = lax.dot_general(dhb, wu_ref[...].astype(_MXU), (((1,), (1,)), ((), ())), preferred_element_type=F32)
        if not last:
            dxp_ref[...] = dx + alpha * dzf_ref[...]
            return
        d = dx + dxp_ref[...]
        xh = xh_ref[...]
        dz1 = _ln_bwd_rows(d, xh, rs_ref[...], g_ref[...])
        dz1_ref[...] = dz1
        dz1b_ref[...] = dz1.astype(_MXU)
        dg1_ref[...] += _fold8(d * xh)
        db1_ref[...] += _fold8(d)
        cs1_ref[...] += _fold8(dz1)

    rev = lambda i: (nt - 1 - i, 0)
    fixed = lambda i: (0, 0)
    pair = lambda i: (0, j)
    tile = pl.BlockSpec((tm, D), rev)
    wide = pl.BlockSpec((tm, 2 * tn), lambda i: (nt - 1 - i, j))
    part = pl.BlockSpec((SUBLANES, 2 * tn), fixed)
    in_specs = [tile, _resident((None, tn, D), lambda i: (0, j, 0)), _resident((None, D, 2 * tn), lambda i: (0, 0, j)),
                wide, wide, pl.BlockSpec((SUBLANES, 2 * tn), pair)]
    operands = [dzb, w_down, w_up, hs, hcs, w_dw]
    out_specs = [wide, part, pl.BlockSpec((3, SUBLANES, 2 * tn), lambda i: (0, 0, 0)), part]
    out_shape = [jax.ShapeDtypeStruct((T, N), _MXU), jax.ShapeDtypeStruct((SUBLANES, 2 * tn), F32),
                 jax.ShapeDtypeStruct((3, SUBLANES, 2 * tn), F32), jax.ShapeDtypeStruct((SUBLANES, 2 * tn), F32)]
    aliases = {}
    if last:
        xh, rstd, g = ln
        in_specs += [ANY, tile, tile, pl.BlockSpec((tm, 1), rev), pl.BlockSpec((1, D), fixed)]
        operands += [prev[0], prev[1], xh, rstd, g]
        aliases = {6: 0}
        out_specs += [tile, tile] + [pl.BlockSpec((SUBLANES, D), fixed)] * 3
        out_shape += [jax.ShapeDtypeStruct((T, D), F32), jax.ShapeDtypeStruct((T, D), _MXU)] \
            + [jax.ShapeDtypeStruct((SUBLANES, D), F32)] * 3
    else:
        in_specs.append(tile)
        operands.append(dz)
        out_specs.append(tile)
        out_shape.append(jax.ShapeDtypeStruct((T, D), F32))
    return pl.pallas_call(
        body, name=name, grid=(nt,), in_specs=in_specs, out_specs=out_specs, out_shape=out_shape,
        input_output_aliases=aliases, scratch_shapes=[pltpu.VMEM((SUBLANES, 2 * tn), F32)],
        compiler_params=_cp(("arbitrary",)),
    )(*operands)


def _sum_pieces(g, r, me, layer, acc, n_layers, *, name):
    _, pr, pc = g.shape
    tr = _tile(pr, 128)

    def body(me_ref, g_ref, r_ref, *rest):
        o_ref = rest[-1]
        total = g_ref[...].astype(F32)
        for s in range(N_DEV - 1):
            total = total + r_ref[s].astype(F32)
        o_ref[...] = total

    in_specs = [pl.BlockSpec((None, tr, pc), lambda i, me_ref: (me_ref[0], i, 0)),
                pl.BlockSpec((N_DEV - 1, tr, pc), lambda i, me_ref: (0, i, 0))]
    operands = [me, g, r]
    aliases = {}
    if acc is not None:
        in_specs.append(ANY)
        operands.append(acc)
        aliases = {3: 0}
    return pl.pallas_call(
        body, name=name,
        grid_spec=pltpu.PrefetchScalarGridSpec(
            num_scalar_prefetch=1, grid=(pr // tr,), in_specs=in_specs,
            out_specs=pl.BlockSpec((None, tr, pc), lambda i, me_ref: (layer, i, 0))),
        out_shape=jax.ShapeDtypeStruct((n_layers, pr, pc), F32),
        input_output_aliases=aliases,
        compiler_params=_cp(("parallel",)),
    )(*operands)


def _adam_math(w, g, m, v):
    bc1 = 1.0 - ADAM_B1 ** ADAM_STEP
    bc2 = 1.0 - ADAM_B2 ** ADAM_STEP
    m = ADAM_B1 * m + (1.0 - ADAM_B1) * g
    v = ADAM_B2 * v + (1.0 - ADAM_B2) * (g * g)
    return -ADAM_LR * ((m / bc1) / (jnp.sqrt(v / bc2) + ADAM_EPS) + ADAM_WD * w), m, v


def _adam(w, g, m, v, *, name):
    R, C = w.shape
    tr = _tile(R, 256)

    def body(w_ref, g_ref, m_ref, v_ref, d_ref, mo_ref, vo_ref):
        d_ref[...], mo_ref[...], vo_ref[...] = _adam_math(w_ref[...], g_ref[...], m_ref[...], v_ref[...])

    spec = pl.BlockSpec((tr, C), lambda i: (i, 0))
    return pl.pallas_call(
        body, name=name, grid=(R // tr,), in_specs=[spec] * 4, out_specs=[spec] * 3,
        out_shape=[jax.ShapeDtypeStruct((R, C), F32)] * 3,
        compiler_params=_cp(("parallel",)),
    )(w, g, m, v)


def _adam_halves(w, own, got, m, v, core, *, name):
    L, R, C = w.shape
    rh = R // 2
    tr = _tile(rh, 256)
    nt = rh // tr

    def body(c_ref, w_ref, own_ref, got_ref, m_ref, v_ref, g_ref, d_ref, mo_ref, vo_ref):
        g = jnp.where(pl.program_id(1) == c_ref[0], own_ref[...], got_ref[...])
        g_ref[...] = g
        d_ref[...], mo_ref[...], vo_ref[...] = _adam_math(w_ref[...], g, m_ref[...], v_ref[...])

    full = pl.BlockSpec((None, tr, C), lambda l, h, t, c_ref: (l, h * nt + t, 0))
    half = pl.BlockSpec((None, tr, C), lambda l, h, t, c_ref: (l, t, 0))
    return pl.pallas_call(
        body, name=name,
        grid_spec=pltpu.PrefetchScalarGridSpec(
            num_scalar_prefetch=1, grid=(L, 2, nt), in_specs=[full, half, half, full, full], out_specs=[full] * 4),
        out_shape=[jax.ShapeDtypeStruct((L, R, C), F32)] * 4,
        compiler_params=_cp(("parallel", "parallel", "parallel")),
    )(core, w, own, got, m, v)


def _remote(src, dst, send, recv, dev):
    return pltpu.make_async_remote_copy(src_ref=src, dst_ref=dst, send_sem=send, recv_sem=recv,
                                        device_id=dev, device_id_type=MESH)


def _place_w(shard, pos, layer, *, axis, name):
    _, R, C = shard.shape
    tr = _tile(R, 512, 16)
    nt = R // tr
    if axis == 2:
        out_shape = (1, R, N_CHIPS * C)
        out_map = lambda t, q: (0, t, q[0])
    else:
        out_shape = (1, N_CHIPS * R, C)
        out_map = lambda t, q: (0, q[0] * nt + t, 0)

    def body(q_ref, s_ref, o_ref):
        o_ref[...] = s_ref[...].astype(_WIRE)

    return pl.pallas_call(
        body, name=name,
        grid_spec=pltpu.PrefetchScalarGridSpec(
            num_scalar_prefetch=1, grid=(nt,),
            in_specs=[pl.BlockSpec((None, tr, C), lambda t, q: (layer, t, 0))],
            out_specs=pl.BlockSpec((None, tr, C), out_map)),
        out_shape=jax.ShapeDtypeStruct(out_shape, _WIRE),
        compiler_params=_cp(("parallel",)),
    )(pos, shard)


def _ag_window(ref, kind, px, py, h):
    axis, perm = kind
    q = 2 * px + py
    if perm:
        q = _perm_idx(q)
    if axis == 2:
        R, C = ref.shape[1], ref.shape[2] // N_CHIPS
        rh = R // 2
        return ref.at[:, pl.ds(pl.multiple_of(h * rh, 16), rh), pl.ds(pl.multiple_of(q * C, LANES), C)]
    R = ref.shape[1] // N_CHIPS
    rh = R // 2
    return ref.at[:, pl.ds(pl.multiple_of(q * R + h * rh, 16), rh), :]


def _ag_ici_copies(refs, kinds, send, recv):
    x, y, c = lax.axis_index("x"), lax.axis_index("y"), lax.axis_index("c")
    chips = [(1 - x, y), (x, 1 - y), (1 - x, 1 - y)]
    sends, recvs = [], []
    for a, (ref, kind) in enumerate(zip(refs, kinds)):
        own = _ag_window(ref, kind, x, y, c)
        for i, (px, py) in enumerate(chips):
            k = 3 * a + i
            sends.append(_remote(own, own, send.at[k], recv.at[k], (px, py, c)))
            recvs.append(_remote(own, _ag_window(ref, kind, px, py, c), send.at[k], recv.at[k], (px, py, c)))
    return sends, recvs


def _ag_start(arrs, kinds, after, *, name):
    n = len(arrs)

    def body(*refs):
        in_refs = refs[:n]
        send, recv = refs[n + len(after)], refs[n + len(after) + 1]
        token = refs[-1]
        sends, _ = _ag_ici_copies(in_refs, kinds, send, recv)
        for cp in sends:
            cp.start()
        token[...] = jnp.zeros_like(token)

    sems = pltpu.SemaphoreType.DMA((3 * n,))
    out = pl.pallas_call(
        body, name=name,
        out_shape=(sems, sems) + tuple(pltpu.HBM(a.shape, a.dtype) for a in arrs)
        + (jax.ShapeDtypeStruct((SUBLANES, LANES), F32),),
        in_specs=(HBM,) * n + (ANY,) * len(after),
        out_specs=(SEMS, SEMS) + (HBM,) * n + (pl.BlockSpec(memory_space=pltpu.VMEM),),
        input_output_aliases={a: 2 + a for a in range(n)},
        compiler_params=pltpu.CompilerParams(has_side_effects=EFFECT),
    )(*[pltpu.with_memory_space_constraint(a, pltpu.HBM) for a in arrs], *after)
    return out[0], out[1], list(out[2:2 + n]), out[-1]


def _ag_wait(send, recv, arrs, kinds, after, *, name):
    n = len(arrs)

    def body(*refs):
        in_refs = refs[:n]
        send, recv = refs[n], refs[n + 1]
        sends, recvs = _ag_ici_copies(in_refs, kinds, send, recv)
        for cp in sends:
            cp.wait_send()
        for cp in recvs:
            cp.wait_recv()

    out = pl.pallas_call(
        body, name=name,
        out_shape=tuple(pltpu.HBM(a.shape, a.dtype) for a in arrs),
        in_specs=(HBM,) * n + (SEMS, SEMS) + (ANY,) * len(after), out_specs=(HBM,) * n,
        input_output_aliases={a: a for a in range(n)},
        compiler_params=pltpu.CompilerParams(has_side_effects=EFFECT),
    )(*arrs, send, recv, *after)
    return list(out)


def _ag_forward(arrs, kinds, *, name):
    n = len(arrs)

    def body(*refs):
        o_refs, send, recv = refs[n:2 * n], refs[2 * n], refs[2 * n + 1]
        x, y, c = lax.axis_index("x"), lax.axis_index("y"), lax.axis_index("c")
        chips = [(1 - x, y), (x, 1 - y), (1 - x, 1 - y)]
        sib = (x, y, 1 - c)
        sends, recvs = [], []
        for a, (ref, kind) in enumerate(zip(o_refs, kinds)):
            for i, (px, py) in enumerate(chips):
                k = 3 * a + i
                got = _ag_window(ref, kind, px, py, c)
                cp = _remote(got, got, send.at[k], recv.at[k], sib)
                cp.start()
                sends.append(cp)
                recvs.append(_remote(got, _ag_window(ref, kind, px, py, 1 - c), send.at[k], recv.at[k], sib))
        for cp in recvs:
            cp.wait_recv()
        for cp in sends:
            cp.wait_send()

    out = pl.pallas_call(
        body, name=name, in_specs=[ANY] * n, out_specs=[ANY] * n,
        out_shape=[jax.ShapeDtypeStruct(a.shape, a.dtype) for a in arrs],
        input_output_aliases={a: a for a in range(n)},
        scratch_shapes=[pltpu.SemaphoreType.DMA((3 * n,)), pltpu.SemaphoreType.DMA((3 * n,))],
    )(*arrs)
    return list(out)


def _flip(x, y, c, f):
    return ((1 - x) if f & 4 else x, (1 - y) if f & 2 else y, (1 - c) if f & 1 else c)


def _rs_copies(g_refs, land_refs, send, recv):
    x, y, c = lax.axis_index("x"), lax.axis_index("y"), lax.axis_index("c")
    cps = []
    for a, (g_ref, land_ref) in enumerate(zip(g_refs, land_refs)):
        for f in range(1, N_DEV):
            tx, ty, tcx = _flip(x, y, c, f)
            k = (N_DEV - 1) * a + f - 1
            cps.append(_remote(g_ref.at[4 * tx + 2 * ty + tcx], land_ref.at[f - 1], send.at[k], recv.at[k],
                               (tx, ty, tcx)))
    return cps


def _rs_start(gs, *, name):
    n = len(gs)
    lands = [lax.empty((N_DEV - 1,) + g.shape[1:], g.dtype) for g in gs]

    def body(*refs):
        send, recv, token = refs[2 * n], refs[2 * n + 1], refs[-1]
        for cp in _rs_copies(refs[:n], refs[n:2 * n], send, recv):
            cp.start()
        token[...] = jnp.zeros_like(token)

    sems = pltpu.SemaphoreType.DMA(((N_DEV - 1) * n,))
    thru = [pltpu.HBM(t.shape, t.dtype) for t in gs + lands]
    out = pl.pallas_call(
        body, name=name,
        out_shape=(sems, sems, *thru, jax.ShapeDtypeStruct((SUBLANES, LANES), F32)),
        in_specs=(HBM,) * (2 * n), out_specs=(SEMS, SEMS) + (HBM,) * (2 * n) + (pl.BlockSpec(memory_space=pltpu.VMEM),),
        input_output_aliases={a: 2 + a for a in range(2 * n)},
        compiler_params=pltpu.CompilerParams(has_side_effects=EFFECT),
    )(*[pltpu.with_memory_space_constraint(t, pltpu.HBM) for t in gs + lands])
    return out[0], out[1], list(out[2:2 + n]), list(out[2 + n:2 + 2 * n]), out[-1]


def _rs_wait(send, recv, gs, lands, after, *, name):
    n = len(gs)

    def body(*refs):
        cps = _rs_copies(refs[:n], refs[n:2 * n], refs[2 * n], refs[2 * n + 1])
        for cp in cps:
            cp.wait_send()
        for cp in cps:
            cp.wait_recv()

    out = pl.pallas_call(
        body, name=name,
        out_shape=tuple(pltpu.HBM(t.shape, t.dtype) for t in gs + lands),
        in_specs=(HBM,) * (2 * n) + (SEMS, SEMS, ANY), out_specs=(HBM,) * (2 * n),
        input_output_aliases={a: a for a in range(2 * n)},
        compiler_params=pltpu.CompilerParams(has_side_effects=EFFECT),
    )(*gs, *lands, send, recv, after)
    return list(out[:n]), list(out[n:])


def _pair_exchange(own, *, name):
    def body(own_ref, got_ref, send, recv):
        x, y, c = lax.axis_index("x"), lax.axis_index("y"), lax.axis_index("c")
        cp = _remote(own_ref, got_ref, send, recv, (x, y, 1 - c))
        cp.start()
        cp.wait_recv()
        cp.wait_send()

    return pl.pallas_call(
        body, name=name, in_specs=[ANY], out_specs=ANY, out_shape=jax.ShapeDtypeStruct(own.shape, own.dtype),
        scratch_shapes=[pltpu.SemaphoreType.DMA, pltpu.SemaphoreType.DMA],
    )(own)


def _allreduce_flat(vec, *, name):
    n = vec.shape[0]
    unit = N_DEV * SUBLANES * LANES
    npad = -(-n // unit) * unit
    rows = npad // (N_DEV * LANES)
    xin = jnp.pad(vec, (0, npad - n)).reshape(N_DEV, rows, LANES)

    def body(x_ref, y_ref, a_ref, send_a, recv_a, send_b, recv_b):
        x, y, c = lax.axis_index("x"), lax.axis_index("y"), lax.axis_index("c")
        me = 4 * x + 2 * y + c
        a_ref[me] = x_ref[me]
        sends, recvs = [], []
        for f in range(1, N_DEV):
            dev = _flip(x, y, c, f)
            t = 4 * dev[0] + 2 * dev[1] + dev[2]
            cp = _remote(x_ref.at[t], a_ref.at[me], send_a.at[f - 1], recv_a.at[f - 1], dev)
            cp.start()
            sends.append(cp)
            recvs.append(_remote(x_ref.at[me], a_ref.at[t], send_a.at[f - 1], recv_a.at[f - 1], dev))
        for cp in recvs:
            cp.wait_recv()
        for cp in sends:
            cp.wait_send()
        acc = a_ref[0]
        for s in range(1, N_DEV):
            acc = acc + a_ref[s]
        y_ref[me] = acc
        sends, recvs = [], []
        for f in range(1, N_DEV):
            dev = _flip(x, y, c, f)
            t = 4 * dev[0] + 2 * dev[1] + dev[2]
            cp = _remote(y_ref.at[me], y_ref.at[me], send_b.at[f - 1], recv_b.at[f - 1], dev)
            cp.start()
            sends.append(cp)
            recvs.append(_remote(y_ref.at[me], y_ref.at[t], send_b.at[f - 1], recv_b.at[f - 1], dev))
        for cp in recvs:
            cp.wait_recv()
        for cp in sends:
            cp.wait_send()

    vm = pl.BlockSpec(memory_space=pltpu.VMEM)
    out = pl.pallas_call(
        body, name=name, in_specs=[vm], out_specs=vm,
        out_shape=jax.ShapeDtypeStruct((N_DEV, rows, LANES), F32),
        scratch_shapes=[pltpu.VMEM((N_DEV, rows, LANES), F32)] + [pltpu.SemaphoreType.DMA((N_DEV - 1,))] * 4,
        compiler_params=_cp(),
    )(xin)
    return out.reshape(npad)[:n]


def _perm_cols(v, blocks=N_CHIPS):
    lead, n = v.shape[:-1], v.shape[-1]
    return v.reshape(lead + (blocks, n // blocks))[..., PERM, :].reshape(lead + (n,))


def _pack(arrs):
    return jnp.concatenate([a.reshape(-1).astype(F32) for a in arrs])


def _unpack(flat, shapes):
    out, pos = [], 0
    for s in shapes:
        n = 1
        for d in s:
            n *= d
        out.append(flat[pos:pos + n].reshape(s))
        pos += n
    return out


def kernel(x, conv_w_in, conv_b_in, conv_w_dw, conv_b_dw, conv_ln_g, conv_ln_b, conv_w_out, conv_b_out, gmlp_w_in, gmlp_b_in, gmlp_ln_g, gmlp_ln_b, gmlp_w_s, gmlp_b_s, gmlp_w_out, gmlp_b_out, ffn_w_up, ffn_b_up, ffn_w_dw, ffn_b_dw, ffn_w_down, ffn_b_down, norm1_g, norm1_b, norm2_g, norm2_b, loss_target, m_conv_w_in, m_conv_b_in, m_conv_w_dw, m_conv_b_dw, m_conv_ln_g, m_conv_ln_b, m_conv_w_out, m_conv_b_out, m_gmlp_w_in, m_gmlp_b_in, m_gmlp_ln_g, m_gmlp_ln_b, m_gmlp_w_s, m_gmlp_b_s, m_gmlp_w_out, m_gmlp_b_out, m_ffn_w_up, m_ffn_b_up, m_ffn_w_dw, m_ffn_b_dw, m_ffn_w_down, m_ffn_b_down, m_norm1_g, m_norm1_b, m_norm2_g, m_norm2_b, v_conv_w_in, v_conv_b_in, v_conv_w_dw, v_conv_b_dw, v_conv_ln_g, v_conv_ln_b, v_conv_w_out, v_conv_b_out, v_gmlp_w_in, v_gmlp_b_in, v_gmlp_ln_g, v_gmlp_ln_b, v_gmlp_w_s, v_gmlp_b_s, v_gmlp_w_out, v_gmlp_b_out, v_ffn_w_up, v_ffn_b_up, v_ffn_w_dw, v_ffn_b_dw, v_ffn_w_down, v_ffn_b_down, v_norm1_g, v_norm1_b, v_norm2_g, v_norm2_b):
    P = dict(locals())
    WEIGHTS = ['conv_w_in', 'conv_b_in', 'conv_w_dw', 'conv_b_dw', 'conv_ln_g', 'conv_ln_b', 'conv_w_out',
               'conv_b_out', 'gmlp_w_in', 'gmlp_b_in', 'gmlp_ln_g', 'gmlp_ln_b', 'gmlp_w_s', 'gmlp_b_s',
               'gmlp_w_out', 'gmlp_b_out', 'ffn_w_up', 'ffn_b_up', 'ffn_w_dw', 'ffn_b_dw', 'ffn_w_down',
               'ffn_b_down', 'norm1_g', 'norm1_b', 'norm2_g', 'norm2_b']
    BIG = ['conv_w_in', 'conv_w_out', 'gmlp_w_in', 'gmlp_w_out', 'ffn_w_up', 'ffn_w_down']
    SMALL_SHARDED = {'conv_w_dw': 2, 'gmlp_b_in': 1, 'gmlp_ln_g': 1, 'gmlp_ln_b': 1, 'gmlp_b_out': 1, 'ffn_w_dw': 2}

    B, S, D = x.shape
    T = B * S
    depth = norm1_g.shape[0]
    alpha = (2.0 * depth) ** 0.25
    C = conv_w_out.shape[-1]
    F2 = ffn_b_up.shape[-1]
    G, L = gmlp_w_s.shape[1], gmlp_w_s.shape[2]
    xi, yi, ci = lax.axis_index("x"), lax.axis_index("y"), lax.axis_index("c")
    shard = 2 * xi + yi

    i32 = lambda v: jnp.reshape(v, (1,)).astype(jnp.int32)
    pos_plain, pos_perm = i32(shard), i32(_perm_idx(shard))
    me_id, core_id = i32(4 * xi + 2 * yi + ci), i32(ci)

    groups = []
    for i in range(depth):
        mix = 'conv' if i % 2 == 0 else 'gmlp'
        groups.append((f"{mix}{i // 2}", [(mix + '_w_in', i // 2, 2, True), (mix + '_w_out', i // 2, 1, False)]))
        groups.append((f"ffn{i}", [('ffn_w_up', i, 2, True), ('ffn_w_down', i, 1, False)]))
    sm_names = list(SMALL_SHARDED)
    sm_shapes = [P[n].shape for n in sm_names]
    mine = _pack([P[n] for n in sm_names]) * (ci == 0).astype(F32)
    buf = jnp.zeros((N_CHIPS, mine.shape[0]), F32)
    buf = lax.dynamic_update_slice(buf, mine[None], (shard, 0))
    gathered = _allreduce_flat(buf.reshape(-1), name="ag_small").reshape(N_CHIPS, -1)

    started, order = {}, [gathered]
    for gname, members in groups:
        placed = [_place_w(P[n], pos_perm if perm else pos_plain, l, axis=axis, name=f"place_{n}_{l}")
                  for n, l, axis, perm in members]
        kinds = [(axis, perm) for _, _, axis, perm in members]
        send, recv, arrs, token = _ag_start(placed, kinds, order, name=f"ag_start_{gname}")
        order = [token]
        started[gname] = (send, recv, arrs, kinds, [(n, l) for n, l, _, _ in members])
    wts = {}

    def arrive(gname, after):
        send, recv, arrs, kinds, keys = started[gname]
        arrs = _ag_wait(send, recv, arrs, kinds, after, name=f"ag_wait_{gname}")
        arrs = _ag_forward(arrs, kinds, name=f"ag_fwd_{gname}")
        wts.update(zip(keys, arrs))

    full = {}
    for n, parts in zip(sm_names, zip(*[_unpack(gathered[k], sm_shapes) for k in range(N_CHIPS)])):
        full[n] = jnp.concatenate(parts, axis=SMALL_SHARDED[n])
    for n in WEIGHTS:
        if n not in BIG and n not in full:
            full[n] = P[n]

    assert G * L == C, "a gMLP group must be as wide as a chunk is long"

    def row(v):
        return v.reshape(1, -1)

    def pad_rows(v, r):
        return jnp.pad(v, ((0, r - v.shape[0]), (0, 0)))

    xf = x.reshape(T, D)
    saved = []
    cur, cur_b = xf, xf.astype(_MXU)
    for i in range(depth):
        j = i // 2
        sv = {'x': cur, 'xb': cur_b}
        arrive(groups[2 * i][0], order if i == 0 else [cur_b])
        if i % 2 == 0:
            b_in = row(_perm_cols(full['conv_b_in'][j]))
            h1 = _mm(cur_b, wts['conv_w_in', j], bl=0, bias=b_in, tm=_tile(T, 512), tn=_tile(2 * C, 1024, LANES),
                     tk=D, name=f"conv_in_{j}", n_outer=True, out_dtype=_ADT)
            wdw = pad_rows(full['conv_w_dw'][j], CONV_TAPS_PAD)
            dwo = _conv_fwd(h1, wdw, row(full['conv_b_dw'][j]), B=B, S=S, name=f"conv_dw_{j}")
            s_act, xhc, rsc, *y1 = _conv_tail_fwd(
                dwo, row(full['conv_ln_g'][j]), row(full['conv_ln_b'][j]), wts['conv_w_out', j],
                row(full['conv_b_out'][j]), cur, alpha, row(norm1_g[i]), row(norm1_b[i]), name=f"conv_out_ln_{j}")
            sv.update(h1=h1, wdw=wdw, act=s_act, xhc=xhc, rsc=rsc)
        else:
            b_in = row(_perm_cols(full['gmlp_b_in'][j]))
            pre = _mm(cur_b, wts['gmlp_w_in', j], bl=0, bias=b_in, tm=_tile(T, 512), tn=_tile(2 * C, 1024, LANES),
                      tk=D, name=f"gmlp_in_{j}", n_outer=True, out_dtype=_ADT)
            bsb = jnp.repeat(gmlp_b_s[j].T, L, axis=1)
            us, xhv, rsv, *y1 = _gmlp_gate_fwd(
                pre, row(full['gmlp_ln_g'][j]), row(full['gmlp_ln_b'][j]), gmlp_w_s[j], bsb, wts['gmlp_w_out', j],
                row(full['gmlp_b_out'][j]), cur, alpha, row(norm1_g[i]), row(norm1_b[i]), name=f"gmlp_gate_{j}")
            sv.update(pre=pre, bsb=bsb, act=us, xhv=xhv, rsv=rsv)
        x1, x1b, xh1, rs1 = y1
        arrive(groups[2 * i + 1][0], [x1b])
        wdw3 = pad_rows(_perm_cols(full['ffn_w_dw'][i]), SUBLANES)
        bdw3 = row(_perm_cols(ffn_b_dw[i]))
        ffn_in = (x1b, wts['ffn_w_up', i], wts['ffn_w_down', i], row(_perm_cols(ffn_b_up[i])), wdw3, bdw3)
        first = _ffn_fwd_half(0, *ffn_in, S=S, name=f"ffn_fwd_a_{i}")
        hs, hcs, f_act, x2, x2b, xh2, rs2 = _ffn_fwd_half(
            1, *ffn_in, S=S, name=f"ffn_fwd_b_{i}", prev=first,
            tail=(x1, alpha, row(ffn_b_down[i]), row(norm2_g[i]), row(norm2_b[i])))
        sv.update(x1=x1, x1b=x1b, xh1=xh1, rs1=rs1, hs=hs, hcs=hcs, f=f_act, wdw3=wdw3, xh2=xh2, rs2=rs2)
        saved.append(sv)
        cur, cur_b = x2, x2b

    sg = {n: [None] * full[n].shape[0] for n in WEIGHTS if n not in BIG}
    inflight = {n: [None] * P[n].shape[0] for n in BIG}
    deps = []
    tgt = loss_target.reshape(T, D)
    dcur = None
    loss_part = None
    tk_t = _tile(T, 2048)

    ready = []

    def wgrad(n, l, a_, b_, **kw):
        ready.append((n, l, _mm(a_, b_, ta=True, out_dtype=_WIRE, tk=tk_t, name=f"{n}_dw_{l}", deps=deps, **kw)))
        launch(f"{n}_{l}")

    def launch(gname):
        send, recv, gs, lands, token = _rs_start([g for _, _, g in ready], name=f"rs_start_{gname}")
        group = {'name': gname, 'flight': (send, recv, gs, lands), 'landed': None}
        for a, (n, l, _) in enumerate(ready):
            inflight[n][l] = (group, a)
        del ready[:]
        deps.append(token)

    def landed(n, l):
        group, a = inflight[n][l]
        if group['landed'] is None:
            group['landed'] = _rs_wait(*group['flight'], dcur, name=f"rs_wait_{group['name']}")
        return group['landed'][0][a], group['landed'][1][a]

    for i in reversed(range(depth)):
        j = i // 2
        sv = saved[i]
        if i == depth - 1:
            dz2, dz2b, dg, db, cs, loss_part = _ln_bwd(cur, sv['xh2'], sv['rs2'], row(norm2_g[i]), target=tgt,
                                                       name=f"ln2_bwd_head_{i}")
        else:
            dz2, dz2b, dg, db, cs = dcur
        sg['norm2_g'][i], sg['norm2_b'][i], sg['ffn_b_down'][i] = dg.sum(0), db.sum(0), cs.sum(0)
        Fh = F2 // 2
        wgrad('ffn_w_down', i, sv['f'], dz2b, tm=Fh // 2, tn=_tile(D, 1024, LANES), pieces=('row',))
        ffn_in = (dz2b, wts['ffn_w_down', i], wts['ffn_w_up', i], sv['hs'], sv['hcs'], sv['wdw3'])
        dh0, csu0, dwd0, dbd0, dxp = _ffn_bwd_half(0, *ffn_in, S=S, name=f"ffn_bwd_a_{i}", dz=dz2, alpha=alpha)
        dh, csu1, dwd1, dbd1, dz1, dz1b, dg, db, cs = _ffn_bwd_half(
            1, *ffn_in, S=S, name=f"ffn_bwd_b_{i}", prev=(dh0, dxp), ln=(sv['xh1'], sv['rs1'], row(norm1_g[i])))
        sg['ffn_b_up'][i] = _perm_cols(jnp.concatenate([csu0.sum(0), csu1.sum(0)], axis=-1))
        sg['ffn_w_dw'][i] = _perm_cols(jnp.concatenate([dwd0.sum(1), dwd1.sum(1)], axis=-1))
        sg['ffn_b_dw'][i] = _perm_cols(jnp.concatenate([dbd0.sum(0), dbd1.sum(0)], axis=-1))
        wgrad('ffn_w_up', i, sv['x1b'], dh, tm=D, tn=F2 // N_CHIPS, pieces=('col', True))
        sg['norm1_g'][i], sg['norm1_b'][i] = dg.sum(0), db.sum(0)
        if i % 2 == 0:
            sg['conv_b_out'][j] = cs.sum(0)
            wgrad('conv_w_out', j, sv['act'], dz1b, tm=_tile(C, 1024), tn=_tile(D, 1024, LANES), pieces=('row',))
            ddw, dg, db = _ln_silu_bwd(dz1b, wts['conv_w_out', j], sv['xhc'], sv['rsc'], row(full['conv_ln_g'][j]),
                                       row(full['conv_ln_b'][j]), name=f"conv_ln_bwd_{j}")
            sg['conv_ln_g'][j], sg['conv_ln_b'][j] = dg.sum(0), db.sum(0)
            dglu, dwk, dbk = _conv_bwd(ddw, sv['h1'], sv['wdw'], B=B, S=S, name=f"conv_dw_bwd_{j}")
            sg['conv_w_dw'][j] = dwk.sum(1)[:conv_w_dw.shape[1]]
            sg['conv_b_dw'][j] = dbk.sum(0)
            dh1, csi = _glu_bwd(dglu, sv['h1'], name=f"conv_glu_bwd_{j}")
            sg['conv_b_in'][j] = _perm_cols(csi.sum(0))
            fam = 'conv_w_in'
        else:
            sg['gmlp_b_out'][j] = cs.sum(0)
            wgrad('gmlp_w_out', j, sv['act'], dz1b, tm=_tile(C, 1024), tn=_tile(D, 1024, LANES), pieces=('row',))
            dh1, dg, db, csi, dws, dbs = _gmlp_gate_bwd(dz1b, wts['gmlp_w_out', j], sv['pre'], sv['xhv'], sv['rsv'],
                                                        row(full['gmlp_ln_g'][j]), row(full['gmlp_ln_b'][j]),
                                                        gmlp_w_s[j], sv['bsb'], name=f"gmlp_gate_bwd_{j}")
            sg['gmlp_ln_g'][j], sg['gmlp_ln_b'][j] = dg.sum(0), db.sum(0)
            sg['gmlp_b_in'][j] = _perm_cols(csi.sum(0))
            sg['gmlp_w_s'][j] = dws
            sg['gmlp_b_s'][j] = dbs.reshape(L, G, L).sum(-1).T
            fam = 'gmlp_w_in'
        wgrad(fam, j, sv['xb'], dh1, tm=D, tn=(2 * C) // N_CHIPS, pieces=('col', True))
        if i > 0:
            below = saved[i - 1]
            dcur = _mm_ln_bwd(dh1, wts[fam, j], dz1, alpha, below['xh2'], below['rs2'], row(norm2_g[i - 1]),
                              name=f"{fam}_dx_{j}", deps=deps)
        else:
            dcur = _mm(dh1, wts[fam, j], bl=0, tb=True, res=dz1, res_scale=alpha, tm=_tile(T, 512),
                       tn=_tile(D, 1024, LANES), tk=2 * C, name=f"{fam}_dx_{j}", deps=deps)
    grad_x = dcur.reshape(B, S, D)

    small_names = [n for n in WEIGHTS if n not in BIG]
    small_full = [jnp.stack(sg[n]) for n in small_names]
    flat = _pack(small_full + [loss_part])
    red = _allreduce_flat(flat, name="ar_small")
    red_parts = _unpack(red, [a.shape for a in small_full] + [loss_part.shape])
    loss = (0.5 / D) * jnp.sum(red_parts[-1])
    grads = {}
    for n, g in zip(small_names, red_parts[:-1]):
        if n in SMALL_SHARDED:
            ax = SMALL_SHARDED[n]
            width = P[n].shape[ax]
            g = lax.dynamic_slice_in_dim(g, shard * width, width, axis=ax)
        grads[n] = g

    big_out = {}
    for n in ['ffn_w_down', 'ffn_w_up', 'gmlp_w_out', 'gmlp_w_in', 'conv_w_out', 'conv_w_in']:
        own = None
        n_layers = len(inflight[n])
        for l in reversed(range(n_layers)):
            pc_, r = landed(n, l)
            own = _sum_pieces(pc_, r, me_id, l, own, n_layers, name=f"sum_{n}_{l}")
        got = _pair_exchange(own, name=f"px_{n}")
        big_out[n] = _adam_halves(P[n], own, got, P['m_' + n], P['v_' + n], core_id, name=f"adam_{n}")

    shapes = [P[n].shape for n in small_names]
    n_small = sum(functools.reduce(lambda p_, d_: p_ * d_, s_, 1) for s_ in shapes)
    unit = SUBLANES * LANES
    npad = -(-n_small // unit) * unit

    def flat2d(arrs, fill=0.0):
        v = _pack(arrs)
        return jnp.pad(v, (0, npad - n_small), constant_values=fill).reshape(-1, LANES)

    dl, mo, vo = _adam(flat2d([P[n] for n in small_names]), flat2d([grads[n] for n in small_names]),
                       flat2d([P['m_' + n] for n in small_names]),
                       flat2d([P['v_' + n] for n in small_names], fill=1.0), name="adam_small")
    small_out = {n: [grads[n], None, None, None] for n in small_names}
    for k, t in enumerate((dl, mo, vo)):
        for n, a in zip(small_names, _unpack(t.reshape(-1), shapes)):
            small_out[n][k + 1] = a

    outs = [loss, grad_x]
    for k in range(4):
        for n in WEIGHTS:
            outs.append(big_out[n][k] if n in BIG else small_out[n][k])
    return tuple(outs)
```

```python
import functools

import jax
import jax.numpy as jnp
from jax import lax
from jax.experimental import pallas as pl
from jax.experimental.pallas import tpu as pltpu

F32 = jnp.float32
_MXU = jnp.bfloat16
_WIRE = jnp.bfloat16
_HDT = jnp.bfloat16
_ADT = jnp.bfloat16
LN_EPS = 1e-5
ADAM_LR, ADAM_B1, ADAM_B2, ADAM_EPS, ADAM_WD, ADAM_STEP = 0.001, 0.9, 0.999, 1e-08, 0.01, 10
N_CHIPS = 4
N_DEV = 8
LANES = 128
SUBLANES = 8
CONV_TAPS_PAD = 32
VMEM_LIMIT = 56 << 20
MESH = pl.DeviceIdType.MESH
ANY = pl.BlockSpec(memory_space=pl.ANY)
HBM = pl.BlockSpec(memory_space=pltpu.HBM)
SEMS = pl.BlockSpec(memory_space=pltpu.SEMAPHORE)
EFFECT = pltpu.SideEffectType.DATAFLOW_SIDE_EFFECTING
PERM = (0, 2, 1, 3)


def _cp(sem=None):
    return pltpu.CompilerParams(dimension_semantics=sem, vmem_limit_bytes=VMEM_LIMIT)


def _tile(dim, pref, mult=SUBLANES):
    if dim <= pref:
        return dim
    t = (pref // mult) * mult
    while t > mult and dim % t:
        t -= mult
    assert dim % t == 0, (dim, pref, mult)
    return t


def _perm_idx(q):
    return (q % 2) * 2 + q // 2


def _fold8(t):
    r, n = t.shape
    return t.reshape(r // SUBLANES, SUBLANES, n).sum(axis=0)


def _ln_rows(z, g, b):
    mu = jnp.mean(z, axis=-1, keepdims=True)
    xc = z - mu
    var = jnp.mean(xc * xc, axis=-1, keepdims=True)
    rstd = lax.rsqrt(var + LN_EPS)
    xh = xc * rstd
    return xh * g + b, xh, rstd


def _ln_bwd_rows(dy, xh, rstd, g):
    dxh = dy * g
    m1 = jnp.mean(dxh, axis=-1, keepdims=True)
    m2 = jnp.mean(dxh * xh, axis=-1, keepdims=True)
    return rstd * (dxh - m1 - xh * m2)


def _sigmoid(v):
    return 0.5 * jnp.tanh(0.5 * v) + 0.5


def _gelu_parts(p):
    cdf = 0.5 * (1.0 + lax.erf(p * 0.7071067811865476))
    pdf = jnp.exp(-0.5 * p * p) * 0.3989422804014327
    return p * cdf, cdf + p * pdf


def _shift_down(prev8, t, s):
    ext = jnp.concatenate([prev8, t], axis=0)
    return pltpu.roll(ext, s, 0)[SUBLANES:]


def _shift_up(t, next8, s):
    n = t.shape[0]
    ext = jnp.concatenate([t, next8], axis=0)
    return pltpu.roll(ext, n + SUBLANES - s, 0)[:n]


def _mm(a, b, *, ta=False, tb=False, bl=None, bias=None, res=None, res_scale=1.0, out_dtype=F32,
        tm, tn, tk, name, pieces=None, deps=None, n_outer=False):
    M, K = (a.shape[1], a.shape[0]) if ta else a.shape
    bs = b.shape[1:] if bl is not None else b.shape
    N, Kb = (bs[0], bs[1]) if tb else (bs[1], bs[0])
    assert K == Kb and M % tm == 0 and N % tn == 0 and K % tk == 0, (a.shape, b.shape, tm, tn, tk)
    gm, gn, gk = M // tm, N // tn, K // tk

    def spec(block, imap):
        if n_outer:
            return pl.BlockSpec(block, lambda j, i, k: imap(i, j, k))
        return pl.BlockSpec(block, imap)

    a_spec = spec((tk, tm), lambda i, j, k: (k, i)) if ta else spec((tm, tk), lambda i, j, k: (i, k))
    bblk = (tn, tk) if tb else (tk, tn)
    bmap = (lambda i, j, k: (j, k)) if tb else (lambda i, j, k: (k, j))
    if bl is not None:
        b_spec = spec((None,) + bblk, lambda i, j, k: (bl,) + bmap(i, j, k))
    else:
        b_spec = spec(bblk, bmap)
    in_specs, operands = [a_spec, b_spec], [a, b]
    if bias is not None:
        in_specs.append(spec((1, tn), lambda i, j, k: (0, j)))
        operands.append(bias)
    if res is not None:
        in_specs.append(spec((tm, tn), lambda i, j, k: (i, j)))
        operands.append(res)
    n_dep = len(deps) if deps else 0
    if n_dep:
        in_specs += [ANY] * n_dep
        operands += deps
        del deps[:]
    if pieces is None:
        out_shape = jax.ShapeDtypeStruct((M, N), out_dtype)
        out_spec = spec((tm, tn), lambda i, j, k: (i, j))
        ppb = pr = None
    elif pieces[0] == 'col':
        pr, pc = M // 2, N // N_CHIPS
        assert tm % pr == 0 and pc % tn == 0
        ppb, per = tm // pr, pc // tn
        perm = pieces[1]
        out_shape = jax.ShapeDtypeStruct((N_DEV, pr, pc), out_dtype)
        out_spec = spec(
            (ppb, pr, tn),
            lambda i, j, k: ((2 * (_perm_idx(j // per) if perm else j // per)) // ppb + i, 0, j % per))
    else:
        pr = M // N_DEV
        assert tm % pr == 0
        ppb = tm // pr
        out_shape = jax.ShapeDtypeStruct((N_DEV, pr, N), out_dtype)
        out_spec = spec((ppb, pr, tn), lambda i, j, k: (i, 0, j))
    dims = (((0 if ta else 1,), (1 if tb else 0,)), ((), ()))

    def body(*refs):
        a_ref, b_ref = refs[0], refs[1]
        pos = 2
        bias_ref = res_ref = None
        if bias is not None:
            bias_ref = refs[pos]
            pos += 1
        if res is not None:
            res_ref = refs[pos]
            pos += 1
        pos += n_dep
        o_ref = refs[pos]

        def finish(r):
            if bias_ref is not None:
                r = r + bias_ref[...]
            if res_ref is not None:
                r = r + res_scale * res_ref[...]
            if pieces is not None:
                r = r.reshape(ppb, pr, tn)
            o_ref[...] = r.astype(out_dtype)

        part = lax.dot_general(a_ref[...].astype(_MXU), b_ref[...].astype(_MXU), dims, preferred_element_type=F32)
        if gk == 1:
            finish(part)
            return
        acc_ref = refs[pos + 1]
        k = pl.program_id(2)

        @pl.when(k == 0)
        def _():
            acc_ref[...] = part

        @pl.when((k > 0) & (k < gk - 1))
        def _():
            acc_ref[...] += part

        @pl.when(k == gk - 1)
        def _():
            finish(acc_ref[...] + part)

    return pl.pallas_call(
        body, name=name, grid=(gn, gm, gk) if n_outer else (gm, gn, gk), in_specs=in_specs, out_specs=out_spec,
        out_shape=out_shape, scratch_shapes=[pltpu.VMEM((tm, tn), F32)] if gk > 1 else [],
        compiler_params=_cp(("parallel", "parallel", "arbitrary")),
    )(*operands)


def _mm_ln_bwd(a, w, res, res_scale, xh, rstd, g, *, name, deps=None):
    T, K = a.shape
    D = w.shape[1]
    tm = _tile(T, 512)
    n_dep = len(deps) if deps else 0

    def body(a_ref, w_ref, res_ref, xh_ref, rs_ref, g_ref, *rest):
        dz_ref, dzb_ref, dg_ref, db_ref, cs_ref = rest[n_dep:]

        @pl.when(pl.program_id(0) == 0)
        def _():
            dg_ref[...] = jnp.zeros_like(dg_ref)
            db_ref[...] = jnp.zeros_like(db_ref)
            cs_ref[...] = jnp.zeros_like(cs_ref)

        d = lax.dot_general(a_ref[...].astype(_MXU), w_ref[...].astype(_MXU), (((1,), (1,)), ((), ())),
                            preferred_element_type=F32) + res_scale * res_ref[...]
        xh = xh_ref[...]
        dz = _ln_bwd_rows(d, xh, rs_ref[...], g_ref[...])
        dz_ref[...] = dz
        dzb_ref[...] = dz.astype(_MXU)
        dg_ref[...] += _fold8(d * xh)
        db_ref[...] += _fold8(d)
        cs_ref[...] += _fold8(dz)

    row = lambda i: (i, 0)
    fixed = lambda i: (0, 0)
    tile = pl.BlockSpec((tm, D), row)
    part = pl.BlockSpec((SUBLANES, D), fixed)
    operands = [a, w, res, xh, rstd, g] + (list(deps) if deps else [])
    if deps:
        del deps[:]
    return pl.pallas_call(
        body, name=name, grid=(T // tm,),
        in_specs=[pl.BlockSpec((tm, K), row),
                  pl.BlockSpec((None, D, K), lambda i: (0, 0, 0), pipeline_mode=pl.Buffered(1)),
                  tile, tile, pl.BlockSpec((tm, 1), row), pl.BlockSpec((1, D), fixed)] + [ANY] * n_dep,
        out_specs=[tile, tile, part, part, part],
        out_shape=[jax.ShapeDtypeStruct((T, D), F32), jax.ShapeDtypeStruct((T, D), _MXU)]
        + [jax.ShapeDtypeStruct((SUBLANES, D), F32)] * 3,
        compiler_params=_cp(("arbitrary",)),
    )(*operands)


def _out_ln(act, wo_ref, bias_ref, res_ref, alpha, g_ref, b_ref, y_ref, yb_ref, xh_ref, rs_ref):
    z = jnp.dot(act, wo_ref[...].astype(_MXU), preferred_element_type=F32) + bias_ref[...] + alpha * res_ref[...]
    y, xh, rstd = _ln_rows(z, g_ref[...], b_ref[...])
    y_ref[...] = y
    yb_ref[...] = y.astype(_MXU)
    xh_ref[...] = xh
    rs_ref[...] = rstd


def _conv_tail_fwd(v, gc, bc, w, bias, res, alpha, g, b, *, name):
    T, C = v.shape
    D = w.shape[-1]
    tm = _tile(T, 512)

    def body(v_ref, gc_ref, bc_ref, w_ref, bias_ref, res_ref, g_ref, b_ref,
             s_ref, xhc_ref, rsc_ref, y_ref, yb_ref, xh_ref, rs_ref):
        yv, xhc, rsc = _ln_rows(v_ref[...], gc_ref[...], bc_ref[...])
        s = (yv * _sigmoid(yv)).astype(_MXU)
        s_ref[...] = s
        xhc_ref[...] = xhc
        rsc_ref[...] = rsc
        _out_ln(s, w_ref, bias_ref, res_ref, alpha, g_ref, b_ref, y_ref, yb_ref, xh_ref, rs_ref)

    row = lambda i: (i, 0)
    fixed = lambda i: (0, 0)
    vc, vd = pl.BlockSpec((1, C), fixed), pl.BlockSpec((1, D), fixed)
    tc_, td = pl.BlockSpec((tm, C), row), pl.BlockSpec((tm, D), row)
    one = pl.BlockSpec((tm, 1), row)
    return pl.pallas_call(
        body, name=name, grid=(T // tm,),
        in_specs=[tc_, vc, vc, _resident((None, C, D), lambda i: (0, 0, 0)), vd, td, vd, vd],
        out_specs=[tc_, tc_, one, td, td, td, one],
        out_shape=[jax.ShapeDtypeStruct((T, C), _MXU), jax.ShapeDtypeStruct((T, C), F32),
                   jax.ShapeDtypeStruct((T, 1), F32), jax.ShapeDtypeStruct((T, D), F32),
                   jax.ShapeDtypeStruct((T, D), _MXU), jax.ShapeDtypeStruct((T, D), F32),
                   jax.ShapeDtypeStruct((T, 1), F32)],
        compiler_params=_cp(("parallel",)),
    )(v, gc, bc, w, bias, res, g, b)


def _ln_bwd(dy, xh, rstd, g, *, name, target=None):
    T, D = dy.shape
    tm = _tile(T, 512)
    head = target is not None

    def body(*refs):
        if head:
            dy_ref, t_ref, xh_ref, rs_ref, g_ref, dz_ref, dzb_ref, dg_ref, db_ref, cs_ref, ls_ref = refs
        else:
            dy_ref, xh_ref, rs_ref, g_ref, dz_ref, dzb_ref, dg_ref, db_ref, cs_ref = refs
        i = pl.program_id(0)

        @pl.when(i == 0)
        def _():
            dg_ref[...] = jnp.zeros_like(dg_ref)
            db_ref[...] = jnp.zeros_like(db_ref)
            cs_ref[...] = jnp.zeros_like(cs_ref)
            if head:
                ls_ref[...] = jnp.zeros_like(ls_ref)

        d = dy_ref[...]
        if head:
            err = d - t_ref[...]
            ls_ref[...] += _fold8(err * err)
            d = err * (1.0 / D)
        xh = xh_ref[...]
        dz = _ln_bwd_rows(d, xh, rs_ref[...], g_ref[...])
        dz_ref[...] = dz
        dzb_ref[...] = dz.astype(_MXU)
        dg_ref[...] += _fold8(d * xh)
        db_ref[...] += _fold8(d)
        cs_ref[...] += _fold8(dz)

    row = lambda i: (i, 0)
    fixed = lambda i: (0, 0)
    tile = pl.BlockSpec((tm, D), row)
    part = pl.BlockSpec((SUBLANES, D), fixed)
    in_specs = [tile] + ([tile] if head else []) + [tile, pl.BlockSpec((tm, 1), row), pl.BlockSpec((1, D), fixed)]
    n_part = 4 if head else 3
    operands = [dy] + ([target] if head else []) + [xh, rstd, g]
    return pl.pallas_call(
        body, name=name, grid=(T // tm,), in_specs=in_specs,
        out_specs=[tile, tile] + [part] * n_part,
        out_shape=[jax.ShapeDtypeStruct((T, D), F32), jax.ShapeDtypeStruct((T, D), _MXU)]
        + [jax.ShapeDtypeStruct((SUBLANES, D), F32)] * n_part,
        compiler_params=_cp(("arbitrary",)),
    )(*operands)


def _conv_cols(C, tc):
    per = (C // 2) // tc
    return per, (lambda j: (j // per) * (2 * per) + j % per)


def _glu_shifted(a_ref, g_ref, p_ref, S):
    u = a_ref[...].astype(F32) * _sigmoid(g_ref[...].astype(F32))
    rows = lax.broadcasted_iota(jnp.int32, (SUBLANES, u.shape[1]), 0)
    lo = CONV_TAPS_PAD
    for r in range(SUBLANES):
        p_ref[r, 0:lo, :] = jnp.zeros((lo, u.shape[1]), F32)
        if r == 0:
            p_ref[r, lo:lo + S, :] = u
        else:
            rolled = pltpu.roll(u, r, 0)
            p_ref[r, lo:lo + S, :] = rolled
            p_ref[r, lo:lo + SUBLANES, :] = jnp.where(rows >= r, rolled[0:SUBLANES], 0.0)


def _conv_fwd(h1, w_dw, b_dw, *, B, S, name):
    C = w_dw.shape[1]
    taps = CONV_TAPS_PAD - 1
    tc = LANES
    ch = _tile(S, 128)
    per, col_a = _conv_cols(C, tc)

    def body(a_ref, g_ref, w_ref, b_ref, o_ref, p_ref):
        _glu_shifted(a_ref, g_ref, p_ref, S)

        def chunk(ci, carry):
            base = pl.multiple_of(ci * ch, ch)
            acc = jnp.zeros((ch, tc), F32) + b_ref[...]
            for k in range(taps):
                q, r = divmod(taps - 1 - k, SUBLANES)
                start = pl.multiple_of(base + (CONV_TAPS_PAD - SUBLANES * q), SUBLANES)
                acc = acc + w_ref[pl.ds(k, 1), :] * p_ref[r, pl.ds(start, ch), :]
            o_ref[pl.ds(base, ch), :] = acc
            return carry

        lax.fori_loop(0, S // ch, chunk, 0)

    return pl.pallas_call(
        body, name=name, grid=(B, C // tc),
        in_specs=[pl.BlockSpec((S, tc), lambda b, j: (b, col_a(j))),
                  pl.BlockSpec((S, tc), lambda b, j: (b, col_a(j) + per)),
                  pl.BlockSpec((CONV_TAPS_PAD, tc), lambda b, j: (0, j)),
                  pl.BlockSpec((1, tc), lambda b, j: (0, j))],
        out_specs=pl.BlockSpec((S, tc), lambda b, j: (b, j)),
        out_shape=jax.ShapeDtypeStruct((B * S, C), F32),
        scratch_shapes=[pltpu.VMEM((SUBLANES, S + CONV_TAPS_PAD, tc), F32)],
        compiler_params=_cp(("parallel", "parallel")),
    )(h1, h1, w_dw, b_dw)


def _conv_bwd(dd, h1, w_dw, *, B, S, name):
    C = w_dw.shape[1]
    taps = CONV_TAPS_PAD - 1
    tc = LANES
    ch = _tile(S, 128)
    per, col_a = _conv_cols(C, tc)

    def body(d_ref, a_ref, g_ref, w_ref, du_ref, dw_ref, db_ref, p_ref, q_ref):
        b = pl.program_id(1)

        @pl.when(b == 0)
        def _():
            dw_ref[...] = jnp.zeros_like(dw_ref)
            db_ref[...] = jnp.zeros_like(db_ref)

        _glu_shifted(a_ref, g_ref, p_ref, S)
        d = d_ref[...]
        rows = lax.broadcasted_iota(jnp.int32, (SUBLANES, tc), 0)
        for r in range(SUBLANES):
            q_ref[r, S:S + CONV_TAPS_PAD, :] = jnp.zeros((CONV_TAPS_PAD, tc), F32)
            if r == 0:
                q_ref[r, 0:S, :] = d
            else:
                rolled = pltpu.roll(d, S - r, 0)
                q_ref[r, 0:S, :] = rolled
                q_ref[r, S - SUBLANES:S, :] = jnp.where(rows < SUBLANES - r, rolled[S - SUBLANES:S], 0.0)
        db_ref[...] += _fold8(d)

        def chunk(ci, carry):
            base = pl.multiple_of(ci * ch, ch)
            dch = d_ref[pl.ds(base, ch), :]
            acc = jnp.zeros((ch, tc), F32)
            for k in range(taps):
                q, r = divmod(taps - 1 - k, SUBLANES)
                up = pl.multiple_of(base + SUBLANES * q, SUBLANES)
                acc = acc + w_ref[pl.ds(k, 1), :] * q_ref[r, pl.ds(up, ch), :]
                down = pl.multiple_of(base + (CONV_TAPS_PAD - SUBLANES * q), SUBLANES)
                dw_ref[k] += _fold8(dch * p_ref[r, pl.ds(down, ch), :])
            du_ref[pl.ds(base, ch), :] = acc
            return carry

        lax.fori_loop(0, S // ch, chunk, 0)

    return pl.pallas_call(
        body, name=name, grid=(C // tc, B),
        in_specs=[pl.BlockSpec((S, tc), lambda j, b: (b, j)),
                  pl.BlockSpec((S, tc), lambda j, b: (b, col_a(j))),
                  pl.BlockSpec((S, tc), lambda j, b: (b, col_a(j) + per)),
                  pl.BlockSpec((CONV_TAPS_PAD, tc), lambda j, b: (0, j))],
        out_specs=[pl.BlockSpec((S, tc), lambda j, b: (b, j)),
                   pl.BlockSpec((CONV_TAPS_PAD, SUBLANES, tc), lambda j, b: (0, 0, j)),
                   pl.BlockSpec((SUBLANES, tc), lambda j, b: (0, j))],
        out_shape=[jax.ShapeDtypeStruct((B * S, C), F32),
                   jax.ShapeDtypeStruct((CONV_TAPS_PAD, SUBLANES, C), F32),
                   jax.ShapeDtypeStruct((SUBLANES, C), F32)],
        scratch_shapes=[pltpu.VMEM((SUBLANES, S + CONV_TAPS_PAD, tc), F32),
                        pltpu.VMEM((SUBLANES, S + CONV_TAPS_PAD, tc), F32)],
        compiler_params=_cp(("parallel", "arbitrary")),
    )(dd, h1, h1, w_dw)


def _ln_silu_bwd(dzb, w, xh, rstd, g, b, *, name):
    T, D = dzb.shape
    C = w.shape[1]
    tm = _tile(T, 512)

    def body(dz_ref, w_ref, xh_ref, rs_ref, g_ref, b_ref, dv_ref, dg_ref, db_ref):
        @pl.when(pl.program_id(0) == 0)
        def _():
            dg_ref[...] = jnp.zeros_like(dg_ref)
            db_ref[...] = jnp.zeros_like(db_ref)

        ds = lax.dot_general(dz_ref[...].astype(_MXU), w_ref[...].astype(_MXU), (((1,), (1,)), ((), ())),
                             preferred_element_type=F32)
        xh = xh_ref[...]
        gam = g_ref[...]
        y = xh * gam + b_ref[...]
        sig = _sigmoid(y)
        dln = ds * (sig * (1.0 + y * (1.0 - sig)))
        dv_ref[...] = _ln_bwd_rows(dln, xh, rs_ref[...], gam)
        dg_ref[...] += _fold8(dln * xh)
        db_ref[...] += _fold8(dln)

    row = lambda i: (i, 0)
    fixed = lambda i: (0, 0)
    vec = pl.BlockSpec((1, C), fixed)
    part = pl.BlockSpec((SUBLANES, C), fixed)
    return pl.pallas_call(
        body, name=name, grid=(T // tm,),
        in_specs=[pl.BlockSpec((tm, D), row), _resident((None, C, D), lambda i: (0, 0, 0)),
                  pl.BlockSpec((tm, C), row), pl.BlockSpec((tm, 1), row), vec, vec],
        out_specs=[pl.BlockSpec((tm, C), row), part, part],
        out_shape=[jax.ShapeDtypeStruct((T, C), F32)] + [jax.ShapeDtypeStruct((SUBLANES, C), F32)] * 2,
        compiler_params=_cp(("arbitrary",)),
    )(dzb, w, xh, rstd, g, b)


def _glu_bwd(du, h1, *, name):
    T, C = du.shape
    il = C // 2
    tm = _tile(T, 512)

    def body(du_ref, h_ref, dh_ref, cs_ref):
        @pl.when(pl.program_id(0) == 0)
        def _():
            cs_ref[...] = jnp.zeros_like(cs_ref)

        for hb in range(2):
            a = h_ref[:, 2 * hb * il:(2 * hb + 1) * il].astype(F32)
            gate = h_ref[:, (2 * hb + 1) * il:(2 * hb + 2) * il].astype(F32)
            d = du_ref[:, hb * il:(hb + 1) * il]
            sig = _sigmoid(gate)
            da = d * sig
            dgate = d * a * sig * (1.0 - sig)
            dh_ref[:, 2 * hb * il:(2 * hb + 1) * il] = da.astype(_MXU)
            dh_ref[:, (2 * hb + 1) * il:(2 * hb + 2) * il] = dgate.astype(_MXU)
            cs_ref[:, 2 * hb * il:(2 * hb + 1) * il] += _fold8(da)
            cs_ref[:, (2 * hb + 1) * il:(2 * hb + 2) * il] += _fold8(dgate)

    row = lambda i: (i, 0)
    return pl.pallas_call(
        body, name=name, grid=(T // tm,),
        in_specs=[pl.BlockSpec((tm, C), row), pl.BlockSpec((tm, 2 * C), row)],
        out_specs=[pl.BlockSpec((tm, 2 * C), row), pl.BlockSpec((SUBLANES, 2 * C), lambda i: (0, 0))],
        out_shape=[jax.ShapeDtypeStruct((T, 2 * C), _MXU), jax.ShapeDtypeStruct((SUBLANES, 2 * C), F32)],
        compiler_params=_cp(("arbitrary",)),
    )(du, h1)


def _tril_mask(n):
    return lax.broadcasted_iota(jnp.int32, (n, n), 0) >= lax.broadcasted_iota(jnp.int32, (n, n), 1)


def _split_uv(t, il):
    u = jnp.concatenate([t[:, 0:il], t[:, 2 * il:3 * il]], axis=1)
    v = jnp.concatenate([t[:, il:2 * il], t[:, 3 * il:4 * il]], axis=1)
    return u, v


def _gmlp_gate_fwd(p, g, b, w_s, bsb, w_out, bias, res, alpha, g1, b1, *, name):
    T, C2 = p.shape
    C = C2 // 2
    D = w_out.shape[-1]
    il = C // 2
    G, L, _ = w_s.shape
    assert G * L == C
    tm = _tile(T, 4 * L, L)

    def body(p_ref, g_ref, b_ref, ws_ref, bs_ref, wo_ref, bias_ref, res_ref, g1_ref, b1_ref,
             us_ref, xh_ref, rs_ref, y_ref, yb_ref, xh1_ref, rs1_ref, vn_ref, u_ref):
        z, _ = _gelu_parts(p_ref[...].astype(F32))
        u, v = _split_uv(z, il)
        vn, xh, rstd = _ln_rows(v, g_ref[...], b_ref[...])
        xh_ref[...] = xh
        rs_ref[...] = rstd
        vn_ref[...] = vn.astype(_MXU)
        u_ref[...] = u
        mask = _tril_mask(L)
        for gi in range(G):
            wc = jnp.where(mask, ws_ref[gi], 0.0).astype(_MXU)
            cols = slice(gi * L, (gi + 1) * L)
            for c in range(tm // L):
                rows = slice(c * L, (c + 1) * L)
                s = jnp.dot(wc, vn_ref[rows, cols], preferred_element_type=F32) + bs_ref[:, cols]
                us_ref[rows, cols] = (u_ref[rows, cols] * s).astype(_MXU)
        _out_ln(us_ref[...], wo_ref, bias_ref, res_ref, alpha, g1_ref, b1_ref, y_ref, yb_ref, xh1_ref, rs1_ref)

    row = lambda i: (i, 0)
    fixed = lambda i: (0, 0)
    vd, td, one = pl.BlockSpec((1, D), fixed), pl.BlockSpec((tm, D), row), pl.BlockSpec((tm, 1), row)
    return pl.pallas_call(
        body, name=name, grid=(T // tm,),
        in_specs=[pl.BlockSpec((tm, C2), row), pl.BlockSpec((1, C), fixed), pl.BlockSpec((1, C), fixed),
                  pl.BlockSpec((G, L, L), lambda i: (0, 0, 0)), pl.BlockSpec((L, C), fixed),
                  _resident((None, C, D), lambda i: (0, 0, 0)), vd, td, vd, vd],
        out_specs=[pl.BlockSpec((tm, C), row), pl.BlockSpec((tm, C), row), one, td, td, td, one],
        out_shape=[jax.ShapeDtypeStruct((T, C), _MXU), jax.ShapeDtypeStruct((T, C), F32),
                   jax.ShapeDtypeStruct((T, 1), F32), jax.ShapeDtypeStruct((T, D), F32),
                   jax.ShapeDtypeStruct((T, D), _MXU), jax.ShapeDtypeStruct((T, D), F32),
                   jax.ShapeDtypeStruct((T, 1), F32)],
        scratch_shapes=[pltpu.VMEM((tm, C), _MXU), pltpu.VMEM((tm, C), F32)],
        compiler_params=_cp(("parallel",)),
    )(p, g, b, w_s, bsb, w_out, bias, res, g1, b1)


def _gmlp_gate_bwd(dzb, w_out, p, xh, rstd, g, b, w_s, bsb, *, name):
    T, C2 = p.shape
    D = dzb.shape[1]
    C = C2 // 2
    il = C // 2
    G, L, _ = w_s.shape
    tm = _tile(T, 4 * L, L)

    def body(dz_ref, wo_ref, p_ref, xh_ref, rs_ref, g_ref, b_ref, ws_ref, bs_ref,
             dp_ref, dg_ref, db_ref, cs_ref, dws_ref, dbs_ref, vn_ref, u_ref, dvn_ref, du_ref, dus_ref):
        @pl.when(pl.program_id(0) == 0)
        def _():
            dg_ref[...] = jnp.zeros_like(dg_ref)
            db_ref[...] = jnp.zeros_like(db_ref)
            cs_ref[...] = jnp.zeros_like(cs_ref)
            dws_ref[...] = jnp.zeros_like(dws_ref)
            dbs_ref[...] = jnp.zeros_like(dbs_ref)

        dus_ref[...] = lax.dot_general(dz_ref[...].astype(_MXU), wo_ref[...].astype(_MXU), (((1,), (1,)), ((), ())),
                                       preferred_element_type=F32)
        z, gp = _gelu_parts(p_ref[...].astype(F32))
        u, _ = _split_uv(z, il)
        xh = xh_ref[...]
        gam = g_ref[...]
        vn_ref[...] = (xh * gam + b_ref[...]).astype(_MXU)
        u_ref[...] = u
        mask = _tril_mask(L)
        for gi in range(G):
            wc = jnp.where(mask, ws_ref[gi], 0.0).astype(_MXU)
            cols = slice(gi * L, (gi + 1) * L)
            for c in range(tm // L):
                rows = slice(c * L, (c + 1) * L)
                vnb = vn_ref[rows, cols]
                s = jnp.dot(wc, vnb, preferred_element_type=F32) + bs_ref[:, cols]
                d = dus_ref[rows, cols]
                du_ref[rows, cols] = d * s
                ds = d * u_ref[rows, cols]
                dbs_ref[:, cols] += ds
                dsb = ds.astype(_MXU)
                dw = lax.dot_general(dsb, vnb, (((1,), (1,)), ((), ())), preferred_element_type=F32)
                dws_ref[gi] += jnp.where(mask, dw, 0.0)
                dvn_ref[rows, cols] = lax.dot_general(wc, dsb, (((0,), (0,)), ((), ())), preferred_element_type=F32)
        dvn = dvn_ref[...]
        dg_ref[...] += _fold8(dvn * xh)
        db_ref[...] += _fold8(dvn)
        dv = _ln_bwd_rows(dvn, xh, rs_ref[...], gam)
        du = du_ref[...]
        for hb in range(2):
            for part, src in ((0, du), (1, dv)):
                lo = (2 * hb + part) * il
                dp = src[:, hb * il:(hb + 1) * il] * gp[:, lo:lo + il]
                dp_ref[:, lo:lo + il] = dp.astype(_MXU)
                cs_ref[:, lo:lo + il] += _fold8(dp)

    row = lambda i: (i, 0)
    fixed = lambda i: (0, 0)
    part_c = pl.BlockSpec((SUBLANES, C), fixed)
    return pl.pallas_call(
        body, name=name, grid=(T // tm,),
        in_specs=[pl.BlockSpec((tm, D), row), _resident((None, C, D), lambda i: (0, 0, 0)),
                  pl.BlockSpec((tm, C2), row), pl.BlockSpec((tm, C), row),
                  pl.BlockSpec((tm, 1), row), pl.BlockSpec((1, C), fixed), pl.BlockSpec((1, C), fixed),
                  pl.BlockSpec((G, L, L), lambda i: (0, 0, 0)), pl.BlockSpec((L, C), fixed)],
        out_specs=[pl.BlockSpec((tm, C2), row), part_c, part_c, pl.BlockSpec((SUBLANES, C2), fixed),
                   pl.BlockSpec((G, L, L), lambda i: (0, 0, 0)), pl.BlockSpec((L, C), fixed)],
        out_shape=[jax.ShapeDtypeStruct((T, C2), _MXU), jax.ShapeDtypeStruct((SUBLANES, C), F32),
                   jax.ShapeDtypeStruct((SUBLANES, C), F32), jax.ShapeDtypeStruct((SUBLANES, C2), F32),
                   jax.ShapeDtypeStruct((G, L, L), F32), jax.ShapeDtypeStruct((L, C), F32)],
        scratch_shapes=[pltpu.VMEM((tm, C), _MXU), pltpu.VMEM((tm, C), F32), pltpu.VMEM((tm, C), F32),
                        pltpu.VMEM((tm, C), F32), pltpu.VMEM((tm, C), F32)],
        compiler_params=_cp(("arbitrary",)),
    )(dzb, w_out, p, xh, rstd, g, b, w_s, bsb)


def _ffn_conv(h, prev8, w_ref, b_ref):
    h1 = _shift_down(prev8, h, 1)
    h2 = _shift_down(prev8, h, 2)
    return w_ref[pl.ds(2, 1), :] * h + w_ref[pl.ds(1, 1), :] * h1 + w_ref[pl.ds(0, 1), :] * h2 + b_ref[...]


def _resident(block, imap):
    return pl.BlockSpec(block, imap, pipeline_mode=pl.Buffered(1))


def _ffn_fwd_half(j, xb, w_up, w_down, b_up, w_dw, b_dw, *, S, name, prev=None, tail=None):
    T, D = xb.shape
    N = w_up.shape[-1]
    tn = N // N_CHIPS
    tm = _tile(S, 256)
    spt = S // tm
    last = prev is not None
    alpha = tail[1] if last else None

    def body(*refs):
        x_ref, wu_ref, wd_ref, bu_ref, wc_ref, bc_ref = refs[:6]
        if last:
            yp_ref, res_ref, bd_ref, g_ref, b_ref = refs[9:14]
            h_ref, hc_ref, f_ref, y_ref, yb_ref, xh_ref, rs_ref, carry_ref = refs[14:22]
        else:
            h_ref, hc_ref, f_ref, yp_ref, carry_ref = refs[6:11]

        @pl.when(pl.program_id(0) % spt == 0)
        def _():
            carry_ref[...] = jnp.zeros_like(carry_ref)

        h = jnp.dot(x_ref[...].astype(_MXU), wu_ref[...].astype(_MXU), preferred_element_type=F32) + bu_ref[...]
        h_ref[...] = h.astype(_HDT)
        hc = _ffn_conv(h, carry_ref[...], wc_ref, bc_ref)
        hc_ref[...] = hc.astype(_HDT)
        carry_ref[...] = h[tm - SUBLANES:tm]
        gte = hc[:, :tn]
        f = (gte * _sigmoid(gte) * hc[:, tn:]).astype(_MXU)
        f_ref[...] = f
        y = jnp.dot(f, wd_ref[...].astype(_MXU), preferred_element_type=F32)
        if not last:
            yp_ref[...] = y
            return
        z = y + yp_ref[...] + bd_ref[...] + alpha * res_ref[...]
        out, xh, rstd = _ln_rows(z, g_ref[...], b_ref[...])
        y_ref[...] = out
        yb_ref[...] = out.astype(_MXU)
        xh_ref[...] = xh
        rs_ref[...] = rstd

    row = lambda i: (i, 0)
    pair = lambda i: (0, j)
    vec = pl.BlockSpec((1, D), lambda i: (0, 0))
    tile = pl.BlockSpec((tm, D), row)
    in_specs = [tile, _resident((None, D, 2 * tn), lambda i: (0, 0, j)), _resident((None, tn, D), lambda i: (0, j, 0)),
                pl.BlockSpec((1, 2 * tn), pair), pl.BlockSpec((SUBLANES, 2 * tn), pair), pl.BlockSpec((1, 2 * tn), pair)]
    operands = [xb, w_up, w_down, b_up, w_dw, b_dw]
    wide = pl.BlockSpec((tm, 2 * tn), lambda i: (i, j))
    out_specs = [wide, wide, pl.BlockSpec((tm, tn), lambda i: (i, j))]
    out_shape = [jax.ShapeDtypeStruct((T, N), _HDT), jax.ShapeDtypeStruct((T, N), _HDT),
                 jax.ShapeDtypeStruct((T, N // 2), _MXU)]
    aliases = {}
    if last:
        res, _, b_down, g, b = tail
        in_specs += [ANY, ANY, ANY, tile, tile, vec, vec, vec]
        operands += list(prev) + [res, b_down, g, b]
        aliases = {6: 0, 7: 1, 8: 2}
        out_specs += [tile, tile, tile, pl.BlockSpec((tm, 1), row)]
        out_shape += [jax.ShapeDtypeStruct((T, D), F32), jax.ShapeDtypeStruct((T, D), _MXU),
                      jax.ShapeDtypeStruct((T, D), F32), jax.ShapeDtypeStruct((T, 1), F32)]
    else:
        out_specs.append(tile)
        out_shape.append(jax.ShapeDtypeStruct((T, D), F32))
    return pl.pallas_call(
        body, name=name, grid=(T // tm,), in_specs=in_specs, out_specs=out_specs, out_shape=out_shape,
        input_output_aliases=aliases, scratch_shapes=[pltpu.VMEM((SUBLANES, 2 * tn), F32)],
        compiler_params=_cp(("arbitrary",)),
    )(*operands)


def _ffn_bwd_half(j, dzb, w_down, w_up, hs, hcs, w_dw, *, S, name, dz=None, alpha=None, prev=None, ln=None):
    T, D = dzb.shape
    N = hs.shape[1]
    tn = N // N_CHIPS
    tm = _tile(S, 256)
    spt = S // tm
    nt = T // tm
    last = prev is not None

    def body(*refs):
        dz_ref, wd_ref, wu_ref, h_ref, hc_ref, wc_ref = refs[:6]
        if last:
            dxp_ref, xh_ref, rs_ref, g_ref = refs[7:11]
            dh_ref, cs_ref, dw_ref, db_ref, dz1_ref, dz1b_ref, dg1_ref, db1_ref, cs1_ref, carry_ref = refs[11:21]
        else:
            dzf_ref = refs[6]
            dh_ref, cs_ref, dw_ref, db_ref, dxp_ref, carry_ref = refs[7:13]
        i = pl.program_id(0)
        ii = nt - 1 - i

        @pl.when(i == 0)
        def _():
            cs_ref[...] = jnp.zeros_like(cs_ref)
            dw_ref[...] = jnp.zeros_like(dw_ref)
            db_ref[...] = jnp.zeros_like(db_ref)
            if last:
                dg1_ref[...] = jnp.zeros_like(dg1_ref)
                db1_ref[...] = jnp.zeros_like(db1_ref)
                cs1_ref[...] = jnp.zeros_like(cs1_ref)

        df = lax.dot_general(dz_ref[...].astype(_MXU), wd_ref[...].astype(_MXU), (((1,), (1,)), ((), ())),
                             preferred_element_type=F32)
        h = h_ref[...].astype(F32)
        gte, val = hc_ref[:, :tn].astype(F32), hc_ref[:, tn:].astype(F32)
        sig = _sigmoid(gte)
        dval = df * (gte * sig)
        dg = df * val * (sig * (1.0 + gte * (1.0 - sig)))
        dhc = jnp.concatenate([dg, dval], axis=1)
        nxt = jnp.where((ii + 1) % spt == 0, 0.0, carry_ref[...])
        d1 = _shift_up(dhc, nxt, 1)
        d2 = _shift_up(dhc, nxt, 2)
        carry_ref[...] = dhc[0:SUBLANES]
        db_ref[...] += _fold8(dhc)
        dw_ref[2] += _fold8(dhc * h)
        dw_ref[1] += _fold8(d1 * h)
        dw_ref[0] += _fold8(d2 * h)
        dh = wc_ref[pl.ds(2, 1), :] * dhc + wc_ref[pl.ds(1, 1), :] * d1 + wc_ref[pl.ds(0, 1), :] * d2
        cs_ref[...] += _fold8(dh)
        dhb = dh.astype(_MXU)
        dh_ref[...] = dhb
        dx = lax.dot_general(dhb, wu_ref[...].astype(_MXU), (((1,), (1,)), ((), ())), preferred_element_type=F32)
        if not last:
            dxp_ref[...] = dx + alpha * dzf_ref[...]
            return
        d = dx + dxp_ref[...]
        xh = xh_ref[...]
        dz1 = _ln_bwd_rows(d, xh, rs_ref[...], g_ref[...])
        dz1_ref[...] = dz1
        dz1b_ref[...] = dz1.astype(_MXU)
        dg1_ref[...] += _fold8(d * xh)
        db1_ref[...] += _fold8(d)
        cs1_ref[...] += _fold8(dz1)

    rev = lambda i: (nt - 1 - i, 0)
    fixed = lambda i: (0, 0)
    pair = lambda i: (0, j)
    tile = pl.BlockSpec((tm, D), rev)
    wide = pl.BlockSpec((tm, 2 * tn), lambda i: (nt - 1 - i, j))
    part = pl.BlockSpec((SUBLANES, 2 * tn), fixed)
    in_specs = [tile, _resident((None, tn, D), lambda i: (0, j, 0)), _resident((None, D, 2 * tn), lambda i: (0, 0, j)),
                wide, wide, pl.BlockSpec((SUBLANES, 2 * tn), pair)]
    operands = [dzb, w_down, w_up, hs, hcs, w_dw]
    out_specs = [wide, part, pl.BlockSpec((3, SUBLANES, 2 * tn), lambda i: (0, 0, 0)), part]
    out_shape = [jax.ShapeDtypeStruct((T, N), _MXU), jax.ShapeDtypeStruct((SUBLANES, 2 * tn), F32),
                 jax.ShapeDtypeStruct((3, SUBLANES, 2 * tn), F32), jax.ShapeDtypeStruct((SUBLANES, 2 * tn), F32)]
    aliases = {}
    if last:
        xh, rstd, g = ln
        in_specs += [ANY, tile, tile, pl.BlockSpec((tm, 1), rev), pl.BlockSpec((1, D), fixed)]
        operands += [prev[0], prev[1], xh, rstd, g]
        aliases = {6: 0}
        out_specs += [tile, tile] + [pl.BlockSpec((SUBLANES, D), fixed)] * 3
        out_shape += [jax.ShapeDtypeStruct((T, D), F32), jax.ShapeDtypeStruct((T, D), _MXU)] \
            + [jax.ShapeDtypeStruct((SUBLANES, D), F32)] * 3
    else:
        in_specs.append(tile)
        operands.append(dz)
        out_specs.append(tile)
        out_shape.append(jax.ShapeDtypeStruct((T, D), F32))
    return pl.pallas_call(
        body, name=name, grid=(nt,), in_specs=in_specs, out_specs=out_specs, out_shape=out_shape,
        input_output_aliases=aliases, scratch_shapes=[pltpu.VMEM((SUBLANES, 2 * tn), F32)],
        compiler_params=_cp(("arbitrary",)),
    )(*operands)


def _sum_pieces(gs, rs, me, *, name):
    n = len(gs)
    _, pr, pc = gs[0].shape
    tr = _tile(pr, 128)

    def body(me_ref, *refs):
        o_ref = refs[2 * n]
        for l in range(n):
            total = refs[l][...].astype(F32)
            for s in range(N_DEV - 1):
                total = total + refs[n + l][s].astype(F32)
            o_ref[l] = total

    own = pl.BlockSpec((None, tr, pc), lambda i, me_ref: (me_ref[0], i, 0))
    got = pl.BlockSpec((N_DEV - 1, tr, pc), lambda i, me_ref: (0, i, 0))
    return pl.pallas_call(
        body, name=name,
        grid_spec=pltpu.PrefetchScalarGridSpec(
            num_scalar_prefetch=1, grid=(pr // tr,), in_specs=[own] * n + [got] * n,
            out_specs=pl.BlockSpec((n, tr, pc), lambda i, me_ref: (0, i, 0))),
        out_shape=jax.ShapeDtypeStruct((n, pr, pc), F32),
        compiler_params=_cp(("parallel",)),
    )(me, *gs, *rs)


def _adam_math(w, g, m, v):
    bc1 = 1.0 - ADAM_B1 ** ADAM_STEP
    bc2 = 1.0 - ADAM_B2 ** ADAM_STEP
    m = ADAM_B1 * m + (1.0 - ADAM_B1) * g
    v = ADAM_B2 * v + (1.0 - ADAM_B2) * (g * g)
    return -ADAM_LR * ((m / bc1) / (jnp.sqrt(v / bc2) + ADAM_EPS) + ADAM_WD * w), m, v


def _adam(w, g, m, v, *, name):
    R, C = w.shape
    tr = _tile(R, 256)

    def body(w_ref, g_ref, m_ref, v_ref, d_ref, mo_ref, vo_ref):
        d_ref[...], mo_ref[...], vo_ref[...] = _adam_math(w_ref[...], g_ref[...], m_ref[...], v_ref[...])

    spec = pl.BlockSpec((tr, C), lambda i: (i, 0))
    return pl.pallas_call(
        body, name=name, grid=(R // tr,), in_specs=[spec] * 4, out_specs=[spec] * 3,
        out_shape=[jax.ShapeDtypeStruct((R, C), F32)] * 3,
        compiler_params=_cp(("parallel",)),
    )(w, g, m, v)


def _adam_halves(w, own, got, m, v, core, *, name):
    L, R, C = w.shape
    rh = R // 2
    tr = _tile(rh, 256)
    nt = rh // tr

    def body(c_ref, w_ref, own_ref, got_ref, m_ref, v_ref, g_ref, d_ref, mo_ref, vo_ref):
        g = jnp.where(pl.program_id(1) == c_ref[0], own_ref[...], got_ref[...])
        g_ref[...] = g
        d_ref[...], mo_ref[...], vo_ref[...] = _adam_math(w_ref[...], g, m_ref[...], v_ref[...])

    full = pl.BlockSpec((None, tr, C), lambda l, h, t, c_ref: (l, h * nt + t, 0))
    half = pl.BlockSpec((None, tr, C), lambda l, h, t, c_ref: (l, t, 0))
    return pl.pallas_call(
        body, name=name,
        grid_spec=pltpu.PrefetchScalarGridSpec(
            num_scalar_prefetch=1, grid=(L, 2, nt), in_specs=[full, half, half, full, full], out_specs=[full] * 4),
        out_shape=[jax.ShapeDtypeStruct((L, R, C), F32)] * 4,
        compiler_params=_cp(("parallel", "parallel", "parallel")),
    )(core, w, own, got, m, v)


def _remote(src, dst, send, recv, dev):
    return pltpu.make_async_remote_copy(src_ref=src, dst_ref=dst, send_sem=send, recv_sem=recv,
                                        device_id=dev, device_id_type=MESH)


def _place_w(shard, pos, layer, *, axis, name):
    _, R, C = shard.shape
    tr = _tile(R, 512, 16)
    nt = R // tr
    if axis == 2:
        out_shape = (1, R, N_CHIPS * C)
        out_map = lambda t, q: (0, t, q[0])
    else:
        out_shape = (1, N_CHIPS * R, C)
        out_map = lambda t, q: (0, q[0] * nt + t, 0)

    def body(q_ref, s_ref, o_ref):
        o_ref[...] = s_ref[...].astype(_WIRE)

    return pl.pallas_call(
        body, name=name,
        grid_spec=pltpu.PrefetchScalarGridSpec(
            num_scalar_prefetch=1, grid=(nt,),
            in_specs=[pl.BlockSpec((None, tr, C), lambda t, q: (layer, t, 0))],
            out_specs=pl.BlockSpec((None, tr, C), out_map)),
        out_shape=jax.ShapeDtypeStruct(out_shape, _WIRE),
        compiler_params=_cp(("parallel",)),
    )(pos, shard)


def _ag_window(ref, kind, px, py, h):
    axis, perm = kind
    q = 2 * px + py
    if perm:
        q = _perm_idx(q)
    if axis == 2:
        R, C = ref.shape[1], ref.shape[2] // N_CHIPS
        rh = R // 2
        return ref.at[:, pl.ds(pl.multiple_of(h * rh, 16), rh), pl.ds(pl.multiple_of(q * C, LANES), C)]
    R = ref.shape[1] // N_CHIPS
    rh = R // 2
    return ref.at[:, pl.ds(pl.multiple_of(q * R + h * rh, 16), rh), :]


def _ag_ici_copies(refs, kinds, send, recv):
    x, y, c = lax.axis_index("x"), lax.axis_index("y"), lax.axis_index("c")
    chips = [(1 - x, y), (x, 1 - y), (1 - x, 1 - y)]
    sends, recvs = [], []
    for a, (ref, kind) in enumerate(zip(refs, kinds)):
        own = _ag_window(ref, kind, x, y, c)
        for i, (px, py) in enumerate(chips):
            k = 3 * a + i
            sends.append(_remote(own, own, send.at[k], recv.at[k], (px, py, c)))
            recvs.append(_remote(own, _ag_window(ref, kind, px, py, c), send.at[k], recv.at[k], (px, py, c)))
    return sends, recvs


def _ag_start(arrs, kinds, after, *, name):
    n = len(arrs)

    def body(*refs):
        in_refs = refs[:n]
        send, recv = refs[n + len(after)], refs[n + len(after) + 1]
        token = refs[-1]
        sends, _ = _ag_ici_copies(in_refs, kinds, send, recv)
        for cp in sends:
            cp.start()
        token[...] = jnp.zeros_like(token)

    sems = pltpu.SemaphoreType.DMA((3 * n,))
    out = pl.pallas_call(
        body, name=name,
        out_shape=(sems, sems) + tuple(pltpu.HBM(a.shape, a.dtype) for a in arrs)
        + (jax.ShapeDtypeStruct((SUBLANES, LANES), F32),),
        in_specs=(HBM,) * n + (ANY,) * len(after),
        out_specs=(SEMS, SEMS) + (HBM,) * n + (pl.BlockSpec(memory_space=pltpu.VMEM),),
        input_output_aliases={a: 2 + a for a in range(n)},
        compiler_params=pltpu.CompilerParams(has_side_effects=EFFECT),
    )(*[pltpu.with_memory_space_constraint(a, pltpu.HBM) for a in arrs], *after)
    return out[0], out[1], list(out[2:2 + n]), out[-1]


def _ag_wait(send, recv, arrs, kinds, after, *, name):
    n = len(arrs)

    def body(*refs):
        in_refs = refs[:n]
        send, recv = refs[n], refs[n + 1]
        sends, recvs = _ag_ici_copies(in_refs, kinds, send, recv)
        for cp in sends:
            cp.wait_send()
        for cp in recvs:
            cp.wait_recv()

    out = pl.pallas_call(
        body, name=name,
        out_shape=tuple(pltpu.HBM(a.shape, a.dtype) for a in arrs),
        in_specs=(HBM,) * n + (SEMS, SEMS) + (ANY,) * len(after), out_specs=(HBM,) * n,
        input_output_aliases={a: a for a in range(n)},
        compiler_params=pltpu.CompilerParams(has_side_effects=EFFECT),
    )(*arrs, send, recv, *after)
    return list(out)


def _ag_forward(arrs, kinds, *, name):
    n = len(arrs)

    def body(*refs):
        o_refs, send, recv = refs[n:2 * n], refs[2 * n], refs[2 * n + 1]
        x, y, c = lax.axis_index("x"), lax.axis_index("y"), lax.axis_index("c")
        chips = [(1 - x, y), (x, 1 - y), (1 - x, 1 - y)]
        sib = (x, y, 1 - c)
        sends, recvs = [], []
        for a, (ref, kind) in enumerate(zip(o_refs, kinds)):
            for i, (px, py) in enumerate(chips):
                k = 3 * a + i
                got = _ag_window(ref, kind, px, py, c)
                cp = _remote(got, got, send.at[k], recv.at[k], sib)
                cp.start()
                sends.append(cp)
                recvs.append(_remote(got, _ag_window(ref, kind, px, py, 1 - c), send.at[k], recv.at[k], sib))
        for cp in recvs:
            cp.wait_recv()
        for cp in sends:
            cp.wait_send()

    out = pl.pallas_call(
        body, name=name, in_specs=[ANY] * n, out_specs=[ANY] * n,
        out_shape=[jax.ShapeDtypeStruct(a.shape, a.dtype) for a in arrs],
        input_output_aliases={a: a for a in range(n)},
        scratch_shapes=[pltpu.SemaphoreType.DMA((3 * n,)), pltpu.SemaphoreType.DMA((3 * n,))],
    )(*arrs)
    return list(out)


def _flip(x, y, c, f):
    return ((1 - x) if f & 4 else x, (1 - y) if f & 2 else y, (1 - c) if f & 1 else c)


def _rs_copies(g_refs, land_refs, send, recv):
    x, y, c = lax.axis_index("x"), lax.axis_index("y"), lax.axis_index("c")
    cps = []
    for a, (g_ref, land_ref) in enumerate(zip(g_refs, land_refs)):
        for f in range(1, N_DEV):
            tx, ty, tcx = _flip(x, y, c, f)
            k = (N_DEV - 1) * a + f - 1
            cps.append(_remote(g_ref.at[4 * tx + 2 * ty + tcx], land_ref.at[f - 1], send.at[k], recv.at[k],
                               (tx, ty, tcx)))
    return cps


def _rs_start(gs, *, name):
    n = len(gs)
    lands = [lax.empty((N_DEV - 1,) + g.shape[1:], g.dtype) for g in gs]

    def body(*refs):
        send, recv, token = refs[2 * n], refs[2 * n + 1], refs[-1]
        for cp in _rs_copies(refs[:n], refs[n:2 * n], send, recv):
            cp.start()
        token[...] = jnp.zeros_like(token)

    sems = pltpu.SemaphoreType.DMA(((N_DEV - 1) * n,))
    thru = [pltpu.HBM(t.shape, t.dtype) for t in gs + lands]
    out = pl.pallas_call(
        body, name=name,
        out_shape=(sems, sems, *thru, jax.ShapeDtypeStruct((SUBLANES, LANES), F32)),
        in_specs=(HBM,) * (2 * n), out_specs=(SEMS, SEMS) + (HBM,) * (2 * n) + (pl.BlockSpec(memory_space=pltpu.VMEM),),
        input_output_aliases={a: 2 + a for a in range(2 * n)},
        compiler_params=pltpu.CompilerParams(has_side_effects=EFFECT),
    )(*[pltpu.with_memory_space_constraint(t, pltpu.HBM) for t in gs + lands])
    return out[0], out[1], list(out[2:2 + n]), list(out[2 + n:2 + 2 * n]), out[-1]


def _rs_wait(send, recv, gs, lands, after, *, name):
    n = len(gs)

    def body(*refs):
        cps = _rs_copies(refs[:n], refs[n:2 * n], refs[2 * n], refs[2 * n + 1])
        for cp in cps:
            cp.wait_send()
        for cp in cps:
            cp.wait_recv()

    out = pl.pallas_call(
        body, name=name,
        out_shape=tuple(pltpu.HBM(t.shape, t.dtype) for t in gs + lands),
        in_specs=(HBM,) * (2 * n) + (SEMS, SEMS, ANY), out_specs=(HBM,) * (2 * n),
        input_output_aliases={a: a for a in range(2 * n)},
        compiler_params=pltpu.CompilerParams(has_side_effects=EFFECT),
    )(*gs, *lands, send, recv, after)
    return list(out[:n]), list(out[n:])


def _pair_exchange(own, *, name):
    def body(own_ref, got_ref, send, recv):
        x, y, c = lax.axis_index("x"), lax.axis_index("y"), lax.axis_index("c")
        cp = _remote(own_ref, got_ref, send, recv, (x, y, 1 - c))
        cp.start()
        cp.wait_recv()
        cp.wait_send()

    return pl.pallas_call(
        body, name=name, in_specs=[ANY], out_specs=ANY, out_shape=jax.ShapeDtypeStruct(own.shape, own.dtype),
        scratch_shapes=[pltpu.SemaphoreType.DMA, pltpu.SemaphoreType.DMA],
    )(own)


def _allreduce_flat(vec, *, name):
    n = vec.shape[0]
    unit = N_DEV * SUBLANES * LANES
    npad = -(-n // unit) * unit
    rows = npad // (N_DEV * LANES)
    xin = jnp.pad(vec, (0, npad - n)).reshape(N_DEV, rows, LANES)

    def body(x_ref, y_ref, a_ref, send_a, recv_a, send_b, recv_b):
        x, y, c = lax.axis_index("x"), lax.axis_index("y"), lax.axis_index("c")
        me = 4 * x + 2 * y + c
        a_ref[me] = x_ref[me]
        sends, recvs = [], []
        for f in range(1, N_DEV):
            dev = _flip(x, y, c, f)
            t = 4 * dev[0] + 2 * dev[1] + dev[2]
            cp = _remote(x_ref.at[t], a_ref.at[me], send_a.at[f - 1], recv_a.at[f - 1], dev)
            cp.start()
            sends.append(cp)
            recvs.append(_remote(x_ref.at[me], a_ref.at[t], send_a.at[f - 1], recv_a.at[f - 1], dev))
        for cp in recvs:
            cp.wait_recv()
        for cp in sends:
            cp.wait_send()
        acc = a_ref[0]
        for s in range(1, N_DEV):
            acc = acc + a_ref[s]
        y_ref[me] = acc
        sends, recvs = [], []
        for f in range(1, N_DEV):
            dev = _flip(x, y, c, f)
            t = 4 * dev[0] + 2 * dev[1] + dev[2]
            cp = _remote(y_ref.at[me], y_ref.at[me], send_b.at[f - 1], recv_b.at[f - 1], dev)
            cp.start()
            sends.append(cp)
            recvs.append(_remote(y_ref.at[me], y_ref.at[t], send_b.at[f - 1], recv_b.at[f - 1], dev))
        for cp in recvs:
            cp.wait_recv()
        for cp in sends:
            cp.wait_send()

    vm = pl.BlockSpec(memory_space=pltpu.VMEM)
    out = pl.pallas_call(
        body, name=name, in_specs=[vm], out_specs=vm,
        out_shape=jax.ShapeDtypeStruct((N_DEV, rows, LANES), F32),
        scratch_shapes=[pltpu.VMEM((N_DEV, rows, LANES), F32)] + [pltpu.SemaphoreType.DMA((N_DEV - 1,))] * 4,
        compiler_params=_cp(),
    )(xin)
    return out.reshape(npad)[:n]


def _perm_cols(v, blocks=N_CHIPS):
    lead, n = v.shape[:-1], v.shape[-1]
    return v.reshape(lead + (blocks, n // blocks))[..., PERM, :].reshape(lead + (n,))


def _pack(arrs):
    return jnp.concatenate([a.reshape(-1).astype(F32) for a in arrs])


def _unpack(flat, shapes):
    out, pos = [], 0
    for s in shapes:
        n = 1
        for d in s:
            n *= d
        out.append(flat[pos:pos + n].reshape(s))
        pos += n
    return out


def kernel(x, conv_w_in, conv_b_in, conv_w_dw, conv_b_dw, conv_ln_g, conv_ln_b, conv_w_out, conv_b_out, gmlp_w_in, gmlp_b_in, gmlp_ln_g, gmlp_ln_b, gmlp_w_s, gmlp_b_s, gmlp_w_out, gmlp_b_out, ffn_w_up, ffn_b_up, ffn_w_dw, ffn_b_dw, ffn_w_down, ffn_b_down, norm1_g, norm1_b, norm2_g, norm2_b, loss_target, m_conv_w_in, m_conv_b_in, m_conv_w_dw, m_conv_b_dw, m_conv_ln_g, m_conv_ln_b, m_conv_w_out, m_conv_b_out, m_gmlp_w_in, m_gmlp_b_in, m_gmlp_ln_g, m_gmlp_ln_b, m_gmlp_w_s, m_gmlp_b_s, m_gmlp_w_out, m_gmlp_b_out, m_ffn_w_up, m_ffn_b_up, m_ffn_w_dw, m_ffn_b_dw, m_ffn_w_down, m_ffn_b_down, m_norm1_g, m_norm1_b, m_norm2_g, m_norm2_b, v_conv_w_in, v_conv_b_in, v_conv_w_dw, v_conv_b_dw, v_conv_ln_g, v_conv_ln_b, v_conv_w_out, v_conv_b_out, v_gmlp_w_in, v_gmlp_b_in, v_gmlp_ln_g, v_gmlp_ln_b, v_gmlp_w_s, v_gmlp_b_s, v_gmlp_w_out, v_gmlp_b_out, v_ffn_w_up, v_ffn_b_up, v_ffn_w_dw, v_ffn_b_dw, v_ffn_w_down, v_ffn_b_down, v_norm1_g, v_norm1_b, v_norm2_g, v_norm2_b):
    P = dict(locals())
    WEIGHTS = ['conv_w_in', 'conv_b_in', 'conv_w_dw', 'conv_b_dw', 'conv_ln_g', 'conv_ln_b', 'conv_w_out',
               'conv_b_out', 'gmlp_w_in', 'gmlp_b_in', 'gmlp_ln_g', 'gmlp_ln_b', 'gmlp_w_s', 'gmlp_b_s',
               'gmlp_w_out', 'gmlp_b_out', 'ffn_w_up', 'ffn_b_up', 'ffn_w_dw', 'ffn_b_dw', 'ffn_w_down',
               'ffn_b_down', 'norm1_g', 'norm1_b', 'norm2_g', 'norm2_b']
    BIG = ['conv_w_in', 'conv_w_out', 'gmlp_w_in', 'gmlp_w_out', 'ffn_w_up', 'ffn_w_down']
    SMALL_SHARDED = {'conv_w_dw': 2, 'gmlp_b_in': 1, 'gmlp_ln_g': 1, 'gmlp_ln_b': 1, 'gmlp_b_out': 1, 'ffn_w_dw': 2}

    B, S, D = x.shape
    T = B * S
    depth = norm1_g.shape[0]
    alpha = (2.0 * depth) ** 0.25
    C = conv_w_out.shape[-1]
    F2 = ffn_b_up.shape[-1]
    G, L = gmlp_w_s.shape[1], gmlp_w_s.shape[2]
    xi, yi, ci = lax.axis_index("x"), lax.axis_index("y"), lax.axis_index("c")
    shard = 2 * xi + yi

    i32 = lambda v: jnp.reshape(v, (1,)).astype(jnp.int32)
    pos_plain, pos_perm = i32(shard), i32(_perm_idx(shard))
    me_id, core_id = i32(4 * xi + 2 * yi + ci), i32(ci)

    groups = []
    for i in range(depth):
        mix = 'conv' if i % 2 == 0 else 'gmlp'
        groups.append((f"{mix}{i // 2}", [(mix + '_w_in', i // 2, 2, True), (mix + '_w_out', i // 2, 1, False)]))
        groups.append((f"ffn{i}", [('ffn_w_up', i, 2, True), ('ffn_w_down', i, 1, False)]))
    sm_names = list(SMALL_SHARDED)
    sm_shapes = [P[n].shape for n in sm_names]
    mine = _pack([P[n] for n in sm_names]) * (ci == 0).astype(F32)
    buf = jnp.zeros((N_CHIPS, mine.shape[0]), F32)
    buf = lax.dynamic_update_slice(buf, mine[None], (shard, 0))
    gathered = _allreduce_flat(buf.reshape(-1), name="ag_small").reshape(N_CHIPS, -1)

    started, order = {}, [gathered]
    for gname, members in groups:
        placed = [_place_w(P[n], pos_perm if perm else pos_plain, l, axis=axis, name=f"place_{n}_{l}")
                  for n, l, axis, perm in members]
        kinds = [(axis, perm) for _, _, axis, perm in members]
        send, recv, arrs, token = _ag_start(placed, kinds, order, name=f"ag_start_{gname}")
        order = [token]
        started[gname] = (send, recv, arrs, kinds, [(n, l) for n, l, _, _ in members])
    wts = {}

    def arrive(gname, after):
        send, recv, arrs, kinds, keys = started[gname]
        arrs = _ag_wait(send, recv, arrs, kinds, after, name=f"ag_wait_{gname}")
        arrs = _ag_forward(arrs, kinds, name=f"ag_fwd_{gname}")
        wts.update(zip(keys, arrs))

    full = {}
    for n, parts in zip(sm_names, zip(*[_unpack(gathered[k], sm_shapes) for k in range(N_CHIPS)])):
        full[n] = jnp.concatenate(parts, axis=SMALL_SHARDED[n])
    for n in WEIGHTS:
        if n not in BIG and n not in full:
            full[n] = P[n]

    assert G * L == C, "a gMLP group must be as wide as a chunk is long"

    def row(v):
        return v.reshape(1, -1)

    def pad_rows(v, r):
        return jnp.pad(v, ((0, r - v.shape[0]), (0, 0)))

    xf = x.reshape(T, D)
    saved = []
    cur, cur_b = xf, xf.astype(_MXU)
    for i in range(depth):
        j = i // 2
        sv = {'x': cur, 'xb': cur_b}
        arrive(groups[2 * i][0], order if i == 0 else [cur_b])
        if i % 2 == 0:
            b_in = row(_perm_cols(full['conv_b_in'][j]))
            h1 = _mm(cur_b, wts['conv_w_in', j], bl=0, bias=b_in, tm=_tile(T, 512), tn=_tile(2 * C, 1024, LANES),
                     tk=D, name=f"conv_in_{j}", n_outer=True, out_dtype=_ADT)
            wdw = pad_rows(full['conv_w_dw'][j], CONV_TAPS_PAD)
            dwo = _conv_fwd(h1, wdw, row(full['conv_b_dw'][j]), B=B, S=S, name=f"conv_dw_{j}")
            s_act, xhc, rsc, *y1 = _conv_tail_fwd(
                dwo, row(full['conv_ln_g'][j]), row(full['conv_ln_b'][j]), wts['conv_w_out', j],
                row(full['conv_b_out'][j]), cur, alpha, row(norm1_g[i]), row(norm1_b[i]), name=f"conv_out_ln_{j}")
            sv.update(h1=h1, wdw=wdw, act=s_act, xhc=xhc, rsc=rsc)
        else:
            b_in = row(_perm_cols(full['gmlp_b_in'][j]))
            pre = _mm(cur_b, wts['gmlp_w_in', j], bl=0, bias=b_in, tm=_tile(T, 512), tn=_tile(2 * C, 1024, LANES),
                      tk=D, name=f"gmlp_in_{j}", n_outer=True, out_dtype=_ADT)
            bsb = jnp.repeat(gmlp_b_s[j].T, L, axis=1)
            us, xhv, rsv, *y1 = _gmlp_gate_fwd(
                pre, row(full['gmlp_ln_g'][j]), row(full['gmlp_ln_b'][j]), gmlp_w_s[j], bsb, wts['gmlp_w_out', j],
                row(full['gmlp_b_out'][j]), cur, alpha, row(norm1_g[i]), row(norm1_b[i]), name=f"gmlp_gate_{j}")
            sv.update(pre=pre, bsb=bsb, act=us, xhv=xhv, rsv=rsv)
        x1, x1b, xh1, rs1 = y1
        arrive(groups[2 * i + 1][0], [x1b])
        wdw3 = pad_rows(_perm_cols(full['ffn_w_dw'][i]), SUBLANES)
        bdw3 = row(_perm_cols(ffn_b_dw[i]))
        ffn_in = (x1b, wts['ffn_w_up', i], wts['ffn_w_down', i], row(_perm_cols(ffn_b_up[i])), wdw3, bdw3)
        first = _ffn_fwd_half(0, *ffn_in, S=S, name=f"ffn_fwd_a_{i}")
        hs, hcs, f_act, x2, x2b, xh2, rs2 = _ffn_fwd_half(
            1, *ffn_in, S=S, name=f"ffn_fwd_b_{i}", prev=first,
            tail=(x1, alpha, row(ffn_b_down[i]), row(norm2_g[i]), row(norm2_b[i])))
        sv.update(x1=x1, x1b=x1b, xh1=xh1, rs1=rs1, hs=hs, hcs=hcs, f=f_act, wdw3=wdw3, xh2=xh2, rs2=rs2)
        saved.append(sv)
        cur, cur_b = x2, x2b

    sg = {n: [None] * full[n].shape[0] for n in WEIGHTS if n not in BIG}
    inflight = {n: [None] * P[n].shape[0] for n in BIG}
    deps = []
    tgt = loss_target.reshape(T, D)
    dcur = None
    loss_part = None
    tk_t = _tile(T, 2048)

    ready = []

    def wgrad(n, l, a_, b_, **kw):
        ready.append((n, l, _mm(a_, b_, ta=True, out_dtype=_WIRE, tk=tk_t, name=f"{n}_dw_{l}", deps=deps, **kw)))
        launch(f"{n}_{l}")

    def launch(gname):
        send, recv, gs, lands, token = _rs_start([g for _, _, g in ready], name=f"rs_start_{gname}")
        group = {'name': gname, 'flight': (send, recv, gs, lands), 'landed': None}
        for a, (n, l, _) in enumerate(ready):
            inflight[n][l] = (group, a)
        del ready[:]
        deps.append(token)

    def landed(n, l):
        group, a = inflight[n][l]
        if group['landed'] is None:
            group['landed'] = _rs_wait(*group['flight'], dcur, name=f"rs_wait_{group['name']}")
        return group['landed'][0][a], group['landed'][1][a]

    for i in reversed(range(depth)):
        j = i // 2
        sv = saved[i]
        if i == depth - 1:
            dz2, dz2b, dg, db, cs, loss_part = _ln_bwd(cur, sv['xh2'], sv['rs2'], row(norm2_g[i]), target=tgt,
                                                       name=f"ln2_bwd_head_{i}")
        else:
            dz2, dz2b, dg, db, cs = dcur
        sg['norm2_g'][i], sg['norm2_b'][i], sg['ffn_b_down'][i] = dg.sum(0), db.sum(0), cs.sum(0)
        Fh = F2 // 2
        wgrad('ffn_w_down', i, sv['f'], dz2b, tm=Fh // 2, tn=_tile(D, 1024, LANES), pieces=('row',))
        ffn_in = (dz2b, wts['ffn_w_down', i], wts['ffn_w_up', i], sv['hs'], sv['hcs'], sv['wdw3'])
        dh0, csu0, dwd0, dbd0, dxp = _ffn_bwd_half(0, *ffn_in, S=S, name=f"ffn_bwd_a_{i}", dz=dz2, alpha=alpha)
        dh, csu1, dwd1, dbd1, dz1, dz1b, dg, db, cs = _ffn_bwd_half(
            1, *ffn_in, S=S, name=f"ffn_bwd_b_{i}", prev=(dh0, dxp), ln=(sv['xh1'], sv['rs1'], row(norm1_g[i])))
        sg['ffn_b_up'][i] = _perm_cols(jnp.concatenate([csu0.sum(0), csu1.sum(0)], axis=-1))
        sg['ffn_w_dw'][i] = _perm_cols(jnp.concatenate([dwd0.sum(1), dwd1.sum(1)], axis=-1))
        sg['ffn_b_dw'][i] = _perm_cols(jnp.concatenate([dbd0.sum(0), dbd1.sum(0)], axis=-1))
        wgrad('ffn_w_up', i, sv['x1b'], dh, tm=D, tn=F2 // N_CHIPS, pieces=('col', True))
        sg['norm1_g'][i], sg['norm1_b'][i] = dg.sum(0), db.sum(0)
        if i % 2 == 0:
            sg['conv_b_out'][j] = cs.sum(0)
            wgrad('conv_w_out', j, sv['act'], dz1b, tm=_tile(C, 1024), tn=_tile(D, 1024, LANES), pieces=('row',))
            ddw, dg, db = _ln_silu_bwd(dz1b, wts['conv_w_out', j], sv['xhc'], sv['rsc'], row(full['conv_ln_g'][j]),
                                       row(full['conv_ln_b'][j]), name=f"conv_ln_bwd_{j}")
            sg['conv_ln_g'][j], sg['conv_ln_b'][j] = dg.sum(0), db.sum(0)
            dglu, dwk, dbk = _conv_bwd(ddw, sv['h1'], sv['wdw'], B=B, S=S, name=f"conv_dw_bwd_{j}")
            sg['conv_w_dw'][j] = dwk.sum(1)[:conv_w_dw.shape[1]]
            sg['conv_b_dw'][j] = dbk.sum(0)
            dh1, csi = _glu_bwd(dglu, sv['h1'], name=f"conv_glu_bwd_{j}")
            sg['conv_b_in'][j] = _perm_cols(csi.sum(0))
            fam = 'conv_w_in'
        else:
            sg['gmlp_b_out'][j] = cs.sum(0)
            wgrad('gmlp_w_out', j, sv['act'], dz1b, tm=_tile(C, 1024), tn=_tile(D, 1024, LANES), pieces=('row',))
            dh1, dg, db, csi, dws, dbs = _gmlp_gate_bwd(dz1b, wts['gmlp_w_out', j], sv['pre'], sv['xhv'], sv['rsv'],
                                                        row(full['gmlp_ln_g'][j]), row(full['gmlp_ln_b'][j]),
                                                        gmlp_w_s[j], sv['bsb'], name=f"gmlp_gate_bwd_{j}")
            sg['gmlp_ln_g'][j], sg['gmlp_ln_b'][j] = dg.sum(0), db.sum(0)
            sg['gmlp_b_in'][j] = _perm_cols(csi.sum(0))
            sg['gmlp_w_s'][j] = dws
            sg['gmlp_b_s'][j] = dbs.reshape(L, G, L).sum(-1).T
            fam = 'gmlp_w_in'
        wgrad(fam, j, sv['xb'], dh1, tm=D, tn=(2 * C) // N_CHIPS, pieces=('col', True))
        if i > 0:
            below = saved[i - 1]
            dcur = _mm_ln_bwd(dh1, wts[fam, j], dz1, alpha, below['xh2'], below['rs2'], row(norm2_g[i - 1]),
                              name=f"{fam}_dx_{j}", deps=deps)
        else:
            dcur = _mm(dh1, wts[fam, j], bl=0, tb=True, res=dz1, res_scale=alpha, tm=_tile(T, 512),
                       tn=_tile(D, 1024, LANES), tk=2 * C, name=f"{fam}_dx_{j}", deps=deps)
    grad_x = dcur.reshape(B, S, D)

    small_names = [n for n in WEIGHTS if n not in BIG]
    small_full = [jnp.stack(sg[n]) for n in small_names]
    flat = _pack(small_full + [loss_part])
    red = _allreduce_flat(flat, name="ar_small")
    red_parts = _unpack(red, [a.shape for a in small_full] + [loss_part.shape])
    loss = (0.5 / D) * jnp.sum(red_parts[-1])
    grads = {}
    for n, g in zip(small_names, red_parts[:-1]):
        if n in SMALL_SHARDED:
            ax = SMALL_SHARDED[n]
            width = P[n].shape[ax]
            g = lax.dynamic_slice_in_dim(g, shard * width, width, axis=ax)
        grads[n] = g

    big_out = {}
    for n in ['ffn_w_down', 'ffn_w_up', 'gmlp_w_out', 'gmlp_w_in', 'conv_w_out', 'conv_w_in']:
        both = [landed(n, l) for l in range(len(inflight[n]))]
        own = _sum_pieces([g for g, _ in both], [r for _, r in both], me_id, name=f"sum_{n}")
        got = _pair_exchange(own, name=f"px_{n}")
        big_out[n] = _adam_halves(P[n], own, got, P['m_' + n], P['v_' + n], core_id, name=f"adam_{n}")

    shapes = [P[n].shape for n in small_names]
    n_small = sum(functools.reduce(lambda p_, d_: p_ * d_, s_, 1) for s_ in shapes)
    unit = SUBLANES * LANES
    npad = -(-n_small // unit) * unit

    def flat2d(arrs, fill=0.0):
        v = _pack(arrs)
        return jnp.pad(v, (0, npad - n_small), constant_values=fill).reshape(-1, LANES)

    dl, mo, vo = _adam(flat2d([P[n] for n in small_names]), flat2d([grads[n] for n in small_names]),
                       flat2d([P['m_' + n] for n in small_names]),
                       flat2d([P['v_' + n] for n in small_names], fill=1.0), name="adam_small")
    small_out = {n: [grads[n], None, None, None] for n in small_names}
    for k, t in enumerate((dl, mo, vo)):
        for n, a in zip(small_names, _unpack(t.reshape(-1), shapes)):
            small_out[n][k + 1] = a

    outs = [loss, grad_x]
    for k in range(4):
        for n in WEIGHTS:
            outs.append(big_out[n][k] if n in BIG else small_out[n][k])
    return tuple(outs)
```

```python
import functools

import jax
import jax.numpy as jnp
from jax import lax
from jax.experimental import pallas as pl
from jax.experimental.pallas import tpu as pltpu

F32 = jnp.float32
_MXU = jnp.bfloat16
_WIRE = jnp.bfloat16
_HDT = jnp.bfloat16
_ADT = jnp.bfloat16
LN_EPS = 1e-5
ADAM_LR, ADAM_B1, ADAM_B2, ADAM_EPS, ADAM_WD, ADAM_STEP = 0.001, 0.9, 0.999, 1e-08, 0.01, 10
N_CHIPS = 4
N_DEV = 8
LANES = 128
SUBLANES = 8
CONV_TAPS_PAD = 32
VMEM_LIMIT = 56 << 20
MESH = pl.DeviceIdType.MESH
ANY = pl.BlockSpec(memory_space=pl.ANY)
HBM = pl.BlockSpec(memory_space=pltpu.HBM)
SEMS = pl.BlockSpec(memory_space=pltpu.SEMAPHORE)
EFFECT = pltpu.SideEffectType.DATAFLOW_SIDE_EFFECTING
PERM = (0, 2, 1, 3)


def _cp(sem=None):
    return pltpu.CompilerParams(dimension_semantics=sem, vmem_limit_bytes=VMEM_LIMIT)


def _tile(dim, pref, mult=SUBLANES):
    if dim <= pref:
        return dim
    t = (pref // mult) * mult
    while t > mult and dim % t:
        t -= mult
    assert dim % t == 0, (dim, pref, mult)
    return t


def _perm_idx(q):
    return (q % 2) * 2 + q // 2


def _fold8(t):
    r, n = t.shape
    return t.reshape(r // SUBLANES, SUBLANES, n).sum(axis=0)


def _ln_rows(z, g, b):
    mu = jnp.mean(z, axis=-1, keepdims=True)
    xc = z - mu
    var = jnp.mean(xc * xc, axis=-1, keepdims=True)
    rstd = lax.rsqrt(var + LN_EPS)
    xh = xc * rstd
    return xh * g + b, xh, rstd


def _ln_bwd_rows(dy, xh, rstd, g):
    dxh = dy * g
    m1 = jnp.mean(dxh, axis=-1, keepdims=True)
    m2 = jnp.mean(dxh * xh, axis=-1, keepdims=True)
    return rstd * (dxh - m1 - xh * m2)


def _sigmoid(v):
    return 0.5 * jnp.tanh(0.5 * v) + 0.5


def _gelu_parts(p):
    cdf = 0.5 * (1.0 + lax.erf(p * 0.7071067811865476))
    pdf = jnp.exp(-0.5 * p * p) * 0.3989422804014327
    return p * cdf, cdf + p * pdf


def _shift_down(prev8, t, s):
    ext = jnp.concatenate([prev8, t], axis=0)
    return pltpu.roll(ext, s, 0)[SUBLANES:]


def _shift_up(t, next8, s):
    n = t.shape[0]
    ext = jnp.concatenate([t, next8], axis=0)
    return pltpu.roll(ext, n + SUBLANES - s, 0)[:n]


def _mm(a, b, *, ta=False, tb=False, bl=None, bias=None, res=None, res_scale=1.0, out_dtype=F32,
        tm, tn, tk, name, pieces=None, deps=None, n_outer=False):
    M, K = (a.shape[1], a.shape[0]) if ta else a.shape
    bs = b.shape[1:] if bl is not None else b.shape
    N, Kb = (bs[0], bs[1]) if tb else (bs[1], bs[0])
    assert K == Kb and M % tm == 0 and N % tn == 0 and K % tk == 0, (a.shape, b.shape, tm, tn, tk)
    gm, gn, gk = M // tm, N // tn, K // tk

    def spec(block, imap):
        if n_outer:
            return pl.BlockSpec(block, lambda j, i, k: imap(i, j, k))
        return pl.BlockSpec(block, imap)

    a_spec = spec((tk, tm), lambda i, j, k: (k, i)) if ta else spec((tm, tk), lambda i, j, k: (i, k))
    bblk = (tn, tk) if tb else (tk, tn)
    bmap = (lambda i, j, k: (j, k)) if tb else (lambda i, j, k: (k, j))
    if bl is not None:
        b_spec = spec((None,) + bblk, lambda i, j, k: (bl,) + bmap(i, j, k))
    else:
        b_spec = spec(bblk, bmap)
    in_specs, operands = [a_spec, b_spec], [a, b]
    if bias is not None:
        in_specs.append(spec((1, tn), lambda i, j, k: (0, j)))
        operands.append(bias)
    if res is not None:
        in_specs.append(spec((tm, tn), lambda i, j, k: (i, j)))
        operands.append(res)
    n_dep = len(deps) if deps else 0
    if n_dep:
        in_specs += [ANY] * n_dep
        operands += deps
        del deps[:]
    if pieces is None:
        out_shape = jax.ShapeDtypeStruct((M, N), out_dtype)
        out_spec = spec((tm, tn), lambda i, j, k: (i, j))
        ppb = pr = None
    elif pieces[0] == 'col':
        pr, pc = M // 2, N // N_CHIPS
        assert tm % pr == 0 and pc % tn == 0
        ppb, per = tm // pr, pc // tn
        perm = pieces[1]
        out_shape = jax.ShapeDtypeStruct((N_DEV, pr, pc), out_dtype)
        out_spec = spec(
            (ppb, pr, tn),
            lambda i, j, k: ((2 * (_perm_idx(j // per) if perm else j // per)) // ppb + i, 0, j % per))
    else:
        pr = M // N_DEV
        assert tm % pr == 0
        ppb = tm // pr
        out_shape = jax.ShapeDtypeStruct((N_DEV, pr, N), out_dtype)
        out_spec = spec((ppb, pr, tn), lambda i, j, k: (i, 0, j))
    dims = (((0 if ta else 1,), (1 if tb else 0,)), ((), ()))

    def body(*refs):
        a_ref, b_ref = refs[0], refs[1]
        pos = 2
        bias_ref = res_ref = None
        if bias is not None:
            bias_ref = refs[pos]
            pos += 1
        if res is not None:
            res_ref = refs[pos]
            pos += 1
        pos += n_dep
        o_ref = refs[pos]

        def finish(r):
            if bias_ref is not None:
                r = r + bias_ref[...]
            if res_ref is not None:
                r = r + res_scale * res_ref[...]
            if pieces is not None:
                r = r.reshape(ppb, pr, tn)
            o_ref[...] = r.astype(out_dtype)

        part = lax.dot_general(a_ref[...].astype(_MXU), b_ref[...].astype(_MXU), dims, preferred_element_type=F32)
        if gk == 1:
            finish(part)
            return
        acc_ref = refs[pos + 1]
        k = pl.program_id(2)

        @pl.when(k == 0)
        def _():
            acc_ref[...] = part

        @pl.when((k > 0) & (k < gk - 1))
        def _():
            acc_ref[...] += part

        @pl.when(k == gk - 1)
        def _():
            finish(acc_ref[...] + part)

    return pl.pallas_call(
        body, name=name, grid=(gn, gm, gk) if n_outer else (gm, gn, gk), in_specs=in_specs, out_specs=out_spec,
        out_shape=out_shape, scratch_shapes=[pltpu.VMEM((tm, tn), F32)] if gk > 1 else [],
        compiler_params=_cp(("parallel", "parallel", "arbitrary")),
    )(*operands)


def _mm_ln_bwd(a, w, res, res_scale, xh, rstd, g, *, name, deps=None):
    T, K = a.shape
    D = w.shape[1]
    tm = _tile(T, 512)
    n_dep = len(deps) if deps else 0

    def body(a_ref, w_ref, res_ref, xh_ref, rs_ref, g_ref, *rest):
        dz_ref, dzb_ref, dg_ref, db_ref, cs_ref = rest[n_dep:]

        @pl.when(pl.program_id(0) == 0)
        def _():
            dg_ref[...] = jnp.zeros_like(dg_ref)
            db_ref[...] = jnp.zeros_like(db_ref)
            cs_ref[...] = jnp.zeros_like(cs_ref)

        d = lax.dot_general(a_ref[...].astype(_MXU), w_ref[...].astype(_MXU), (((1,), (1,)), ((), ())),
                            preferred_element_type=F32) + res_scale * res_ref[...]
        xh = xh_ref[...]
        dz = _ln_bwd_rows(d, xh, rs_ref[...], g_ref[...])
        dz_ref[...] = dz
        dzb_ref[...] = dz.astype(_MXU)
        dg_ref[...] += _fold8(d * xh)
        db_ref[...] += _fold8(d)
        cs_ref[...] += _fold8(dz)

    row = lambda i: (i, 0)
    fixed = lambda i: (0, 0)
    tile = pl.BlockSpec((tm, D), row)
    part = pl.BlockSpec((SUBLANES, D), fixed)
    operands = [a, w, res, xh, rstd, g] + (list(deps) if deps else [])
    if deps:
        del deps[:]
    return pl.pallas_call(
        body, name=name, grid=(T // tm,),
        in_specs=[pl.BlockSpec((tm, K), row),
                  pl.BlockSpec((None, D, K), lambda i: (0, 0, 0), pipeline_mode=pl.Buffered(1)),
                  tile, tile, pl.BlockSpec((tm, 1), row), pl.BlockSpec((1, D), fixed)] + [ANY] * n_dep,
        out_specs=[tile, tile, part, part, part],
        out_shape=[jax.ShapeDtypeStruct((T, D), F32), jax.ShapeDtypeStruct((T, D), _MXU)]
        + [jax.ShapeDtypeStruct((SUBLANES, D), F32)] * 3,
        compiler_params=_cp(("arbitrary",)),
    )(*operands)


def _out_ln(act, wo_ref, bias_ref, res_ref, alpha, g_ref, b_ref, y_ref, yb_ref, xh_ref, rs_ref):
    z = jnp.dot(act, wo_ref[...].astype(_MXU), preferred_element_type=F32) + bias_ref[...] + alpha * res_ref[...]
    y, xh, rstd = _ln_rows(z, g_ref[...], b_ref[...])
    y_ref[...] = y
    yb_ref[...] = y.astype(_MXU)
    xh_ref[...] = xh
    rs_ref[...] = rstd


def _conv_tail_fwd(v, gc, bc, w, bias, res, alpha, g, b, *, name):
    T, C = v.shape
    D = w.shape[-1]
    tm = _tile(T, 512)

    def body(v_ref, gc_ref, bc_ref, w_ref, bias_ref, res_ref, g_ref, b_ref,
             s_ref, xhc_ref, rsc_ref, y_ref, yb_ref, xh_ref, rs_ref):
        yv, xhc, rsc = _ln_rows(v_ref[...], gc_ref[...], bc_ref[...])
        s = (yv * _sigmoid(yv)).astype(_MXU)
        s_ref[...] = s
        xhc_ref[...] = xhc
        rsc_ref[...] = rsc
        _out_ln(s, w_ref, bias_ref, res_ref, alpha, g_ref, b_ref, y_ref, yb_ref, xh_ref, rs_ref)

    row = lambda i: (i, 0)
    fixed = lambda i: (0, 0)
    vc, vd = pl.BlockSpec((1, C), fixed), pl.BlockSpec((1, D), fixed)
    tc_, td = pl.BlockSpec((tm, C), row), pl.BlockSpec((tm, D), row)
    one = pl.BlockSpec((tm, 1), row)
    return pl.pallas_call(
        body, name=name, grid=(T // tm,),
        in_specs=[tc_, vc, vc, _resident((None, C, D), lambda i: (0, 0, 0)), vd, td, vd, vd],
        out_specs=[tc_, tc_, one, td, td, td, one],
        out_shape=[jax.ShapeDtypeStruct((T, C), _MXU), jax.ShapeDtypeStruct((T, C), F32),
                   jax.ShapeDtypeStruct((T, 1), F32), jax.ShapeDtypeStruct((T, D), F32),
                   jax.ShapeDtypeStruct((T, D), _MXU), jax.ShapeDtypeStruct((T, D), F32),
                   jax.ShapeDtypeStruct((T, 1), F32)],
        compiler_params=_cp(("parallel",)),
    )(v, gc, bc, w, bias, res, g, b)


def _conv_cols(C, tc):
    per = (C // 2) // tc
    return per, (lambda j: (j // per) * (2 * per) + j % per)


def _glu_shifted(a_ref, g_ref, p_ref, S):
    u = a_ref[...].astype(F32) * _sigmoid(g_ref[...].astype(F32))
    rows = lax.broadcasted_iota(jnp.int32, (SUBLANES, u.shape[1]), 0)
    lo = CONV_TAPS_PAD
    for r in range(SUBLANES):
        p_ref[r, 0:lo, :] = jnp.zeros((lo, u.shape[1]), F32)
        if r == 0:
            p_ref[r, lo:lo + S, :] = u
        else:
            rolled = pltpu.roll(u, r, 0)
            p_ref[r, lo:lo + S, :] = rolled
            p_ref[r, lo:lo + SUBLANES, :] = jnp.where(rows >= r, rolled[0:SUBLANES], 0.0)


def _conv_fwd(h1, w_dw, b_dw, *, B, S, name):
    C = w_dw.shape[1]
    taps = CONV_TAPS_PAD - 1
    tc = LANES
    ch = _tile(S, 128)
    per, col_a = _conv_cols(C, tc)

    def body(a_ref, g_ref, w_ref, b_ref, o_ref, p_ref):
        _glu_shifted(a_ref, g_ref, p_ref, S)

        def chunk(ci, carry):
            base = pl.multiple_of(ci * ch, ch)
            acc = jnp.zeros((ch, tc), F32) + b_ref[...]
            for k in range(taps):
                q, r = divmod(taps - 1 - k, SUBLANES)
                start = pl.multiple_of(base + (CONV_TAPS_PAD - SUBLANES * q), SUBLANES)
                acc = acc + w_ref[pl.ds(k, 1), :] * p_ref[r, pl.ds(start, ch), :]
            o_ref[pl.ds(base, ch), :] = acc
            return carry

        lax.fori_loop(0, S // ch, chunk, 0)

    return pl.pallas_call(
        body, name=name, grid=(B, C // tc),
        in_specs=[pl.BlockSpec((S, tc), lambda b, j: (b, col_a(j))),
                  pl.BlockSpec((S, tc), lambda b, j: (b, col_a(j) + per)),
                  pl.BlockSpec((CONV_TAPS_PAD, tc), lambda b, j: (0, j)),
                  pl.BlockSpec((1, tc), lambda b, j: (0, j))],
        out_specs=pl.BlockSpec((S, tc), lambda b, j: (b, j)),
        out_shape=jax.ShapeDtypeStruct((B * S, C), F32),
        scratch_shapes=[pltpu.VMEM((SUBLANES, S + CONV_TAPS_PAD, tc), F32)],
        compiler_params=_cp(("parallel", "parallel")),
    )(h1, h1, w_dw, b_dw)


def _conv_bwd(dd, h1, w_dw, *, B, S, name):
    C = w_dw.shape[1]
    taps = CONV_TAPS_PAD - 1
    tc = LANES
    ch = _tile(S, 128)
    per, col_a = _conv_cols(C, tc)

    def body(d_ref, a_ref, g_ref, w_ref, du_ref, dw_ref, db_ref, p_ref, q_ref):
        b = pl.program_id(1)

        @pl.when(b == 0)
        def _():
            dw_ref[...] = jnp.zeros_like(dw_ref)
            db_ref[...] = jnp.zeros_like(db_ref)

        _glu_shifted(a_ref, g_ref, p_ref, S)
        d = d_ref[...]
        rows = lax.broadcasted_iota(jnp.int32, (SUBLANES, tc), 0)
        for r in range(SUBLANES):
            q_ref[r, S:S + CONV_TAPS_PAD, :] = jnp.zeros((CONV_TAPS_PAD, tc), F32)
            if r == 0:
                q_ref[r, 0:S, :] = d
            else:
                rolled = pltpu.roll(d, S - r, 0)
                q_ref[r, 0:S, :] = rolled
                q_ref[r, S - SUBLANES:S, :] = jnp.where(rows < SUBLANES - r, rolled[S - SUBLANES:S], 0.0)
        db_ref[...] += _fold8(d)

        def chunk(ci, carry):
            base = pl.multiple_of(ci * ch, ch)
            dch = d_ref[pl.ds(base, ch), :]
            acc = jnp.zeros((ch, tc), F32)
            for k in range(taps):
                q, r = divmod(taps - 1 - k, SUBLANES)
                up = pl.multiple_of(base + SUBLANES * q, SUBLANES)
                acc = acc + w_ref[pl.ds(k, 1), :] * q_ref[r, pl.ds(up, ch), :]
                down = pl.multiple_of(base + (CONV_TAPS_PAD - SUBLANES * q), SUBLANES)
                dw_ref[k] += _fold8(dch * p_ref[r, pl.ds(down, ch), :])
            du_ref[pl.ds(base, ch), :] = acc
            return carry

        lax.fori_loop(0, S // ch, chunk, 0)

    return pl.pallas_call(
        body, name=name, grid=(C // tc, B),
        in_specs=[pl.BlockSpec((S, tc), lambda j, b: (b, j)),
                  pl.BlockSpec((S, tc), lambda j, b: (b, col_a(j))),
                  pl.BlockSpec((S, tc), lambda j, b: (b, col_a(j) + per)),
                  pl.BlockSpec((CONV_TAPS_PAD, tc), lambda j, b: (0, j))],
        out_specs=[pl.BlockSpec((S, tc), lambda j, b: (b, j)),
                   pl.BlockSpec((CONV_TAPS_PAD, SUBLANES, tc), lambda j, b: (0, 0, j)),
                   pl.BlockSpec((SUBLANES, tc), lambda j, b: (0, j))],
        out_shape=[jax.ShapeDtypeStruct((B * S, C), F32),
                   jax.ShapeDtypeStruct((CONV_TAPS_PAD, SUBLANES, C), F32),
                   jax.ShapeDtypeStruct((SUBLANES, C), F32)],
        scratch_shapes=[pltpu.VMEM((SUBLANES, S + CONV_TAPS_PAD, tc), F32),
                        pltpu.VMEM((SUBLANES, S + CONV_TAPS_PAD, tc), F32)],
        compiler_params=_cp(("parallel", "arbitrary")),
    )(dd, h1, h1, w_dw)


def _ln_silu_bwd(dzb, w, xh, rstd, g, b, *, name):
    T, D = dzb.shape
    C = w.shape[1]
    tm = _tile(T, 512)

    def body(dz_ref, w_ref, xh_ref, rs_ref, g_ref, b_ref, dv_ref, dg_ref, db_ref):
        @pl.when(pl.program_id(0) == 0)
        def _():
            dg_ref[...] = jnp.zeros_like(dg_ref)
            db_ref[...] = jnp.zeros_like(db_ref)

        ds = lax.dot_general(dz_ref[...].astype(_MXU), w_ref[...].astype(_MXU), (((1,), (1,)), ((), ())),
                             preferred_element_type=F32)
        xh = xh_ref[...]
        gam = g_ref[...]
        y = xh * gam + b_ref[...]
        sig = _sigmoid(y)
        dln = ds * (sig * (1.0 + y * (1.0 - sig)))
        dv_ref[...] = _ln_bwd_rows(dln, xh, rs_ref[...], gam)
        dg_ref[...] += _fold8(dln * xh)
        db_ref[...] += _fold8(dln)

    row = lambda i: (i, 0)
    fixed = lambda i: (0, 0)
    vec = pl.BlockSpec((1, C), fixed)
    part = pl.BlockSpec((SUBLANES, C), fixed)
    return pl.pallas_call(
        body, name=name, grid=(T // tm,),
        in_specs=[pl.BlockSpec((tm, D), row), _resident((None, C, D), lambda i: (0, 0, 0)),
                  pl.BlockSpec((tm, C), row), pl.BlockSpec((tm, 1), row), vec, vec],
        out_specs=[pl.BlockSpec((tm, C), row), part, part],
        out_shape=[jax.ShapeDtypeStruct((T, C), F32)] + [jax.ShapeDtypeStruct((SUBLANES, C), F32)] * 2,
        compiler_params=_cp(("arbitrary",)),
    )(dzb, w, xh, rstd, g, b)


def _glu_bwd(du, h1, *, name):
    T, C = du.shape
    il = C // 2
    tm = _tile(T, 512)

    def body(du_ref, h_ref, dh_ref, cs_ref):
        @pl.when(pl.program_id(0) == 0)
        def _():
            cs_ref[...] = jnp.zeros_like(cs_ref)

        for hb in range(2):
            a = h_ref[:, 2 * hb * il:(2 * hb + 1) * il].astype(F32)
            gate = h_ref[:, (2 * hb + 1) * il:(2 * hb + 2) * il].astype(F32)
            d = du_ref[:, hb * il:(hb + 1) * il]
            sig = _sigmoid(gate)
            da = d * sig
            dgate = d * a * sig * (1.0 - sig)
            dh_ref[:, 2 * hb * il:(2 * hb + 1) * il] = da.astype(_MXU)
            dh_ref[:, (2 * hb + 1) * il:(2 * hb + 2) * il] = dgate.astype(_MXU)
            cs_ref[:, 2 * hb * il:(2 * hb + 1) * il] += _fold8(da)
            cs_ref[:, (2 * hb + 1) * il:(2 * hb + 2) * il] += _fold8(dgate)

    row = lambda i: (i, 0)
    return pl.pallas_call(
        body, name=name, grid=(T // tm,),
        in_specs=[pl.BlockSpec((tm, C), row), pl.BlockSpec((tm, 2 * C), row)],
        out_specs=[pl.BlockSpec((tm, 2 * C), row), pl.BlockSpec((SUBLANES, 2 * C), lambda i: (0, 0))],
        out_shape=[jax.ShapeDtypeStruct((T, 2 * C), _MXU), jax.ShapeDtypeStruct((SUBLANES, 2 * C), F32)],
        compiler_params=_cp(("arbitrary",)),
    )(du, h1)


def _tril_mask(n):
    return lax.broadcasted_iota(jnp.int32, (n, n), 0) >= lax.broadcasted_iota(jnp.int32, (n, n), 1)


def _split_uv(t, il):
    u = jnp.concatenate([t[:, 0:il], t[:, 2 * il:3 * il]], axis=1)
    v = jnp.concatenate([t[:, il:2 * il], t[:, 3 * il:4 * il]], axis=1)
    return u, v


def _gmlp_gate_fwd(p, g, b, w_s, bsb, w_out, bias, res, alpha, g1, b1, *, name):
    T, C2 = p.shape
    C = C2 // 2
    D = w_out.shape[-1]
    il = C // 2
    G, L, _ = w_s.shape
    assert G * L == C
    tm = _tile(T, 4 * L, L)

    def body(p_ref, g_ref, b_ref, ws_ref, bs_ref, wo_ref, bias_ref, res_ref, g1_ref, b1_ref,
             us_ref, xh_ref, rs_ref, y_ref, yb_ref, xh1_ref, rs1_ref, vn_ref, u_ref):
        z, _ = _gelu_parts(p_ref[...].astype(F32))
        u, v = _split_uv(z, il)
        vn, xh, rstd = _ln_rows(v, g_ref[...], b_ref[...])
        xh_ref[...] = xh
        rs_ref[...] = rstd
        vn_ref[...] = vn.astype(_MXU)
        u_ref[...] = u
        mask = _tril_mask(L)
        for gi in range(G):
            wc = jnp.where(mask, ws_ref[gi], 0.0).astype(_MXU)
            cols = slice(gi * L, (gi + 1) * L)
            for c in range(tm // L):
                rows = slice(c * L, (c + 1) * L)
                s = jnp.dot(wc, vn_ref[rows, cols], preferred_element_type=F32) + bs_ref[:, cols]
                us_ref[rows, cols] = (u_ref[rows, cols] * s).astype(_MXU)
        _out_ln(us_ref[...], wo_ref, bias_ref, res_ref, alpha, g1_ref, b1_ref, y_ref, yb_ref, xh1_ref, rs1_ref)

    row = lambda i: (i, 0)
    fixed = lambda i: (0, 0)
    vd, td, one = pl.BlockSpec((1, D), fixed), pl.BlockSpec((tm, D), row), pl.BlockSpec((tm, 1), row)
    return pl.pallas_call(
        body, name=name, grid=(T // tm,),
        in_specs=[pl.BlockSpec((tm, C2), row), pl.BlockSpec((1, C), fixed), pl.BlockSpec((1, C), fixed),
                  pl.BlockSpec((G, L, L), lambda i: (0, 0, 0)), pl.BlockSpec((L, C), fixed),
                  _resident((None, C, D), lambda i: (0, 0, 0)), vd, td, vd, vd],
        out_specs=[pl.BlockSpec((tm, C), row), pl.BlockSpec((tm, C), row), one, td, td, td, one],
        out_shape=[jax.ShapeDtypeStruct((T, C), _MXU), jax.ShapeDtypeStruct((T, C), F32),
                   jax.ShapeDtypeStruct((T, 1), F32), jax.ShapeDtypeStruct((T, D), F32),
                   jax.ShapeDtypeStruct((T, D), _MXU), jax.ShapeDtypeStruct((T, D), F32),
                   jax.ShapeDtypeStruct((T, 1), F32)],
        scratch_shapes=[pltpu.VMEM((tm, C), _MXU), pltpu.VMEM((tm, C), F32)],
        compiler_params=_cp(("parallel",)),
    )(p, g, b, w_s, bsb, w_out, bias, res, g1, b1)


def _gmlp_gate_bwd(dzb, w_out, p, xh, rstd, g, b, w_s, bsb, *, name):
    T, C2 = p.shape
    D = dzb.shape[1]
    C = C2 // 2
    il = C // 2
    G, L, _ = w_s.shape
    tm = _tile(T, 4 * L, L)

    def body(dz_ref, wo_ref, p_ref, xh_ref, rs_ref, g_ref, b_ref, ws_ref, bs_ref,
             dp_ref, dg_ref, db_ref, cs_ref, dws_ref, dbs_ref, vn_ref, u_ref, dvn_ref, du_ref, dus_ref):
        @pl.when(pl.program_id(0) == 0)
        def _():
            dg_ref[...] = jnp.zeros_like(dg_ref)
            db_ref[...] = jnp.zeros_like(db_ref)
            cs_ref[...] = jnp.zeros_like(cs_ref)
            dws_ref[...] = jnp.zeros_like(dws_ref)
            dbs_ref[...] = jnp.zeros_like(dbs_ref)

        dus_ref[...] = lax.dot_general(dz_ref[...].astype(_MXU), wo_ref[...].astype(_MXU), (((1,), (1,)), ((), ())),
                                       preferred_element_type=F32)
        z, gp = _gelu_parts(p_ref[...].astype(F32))
        u, _ = _split_uv(z, il)
        xh = xh_ref[...]
        gam = g_ref[...]
        vn_ref[...] = (xh * gam + b_ref[...]).astype(_MXU)
        u_ref[...] = u
        mask = _tril_mask(L)
        for gi in range(G):
            wc = jnp.where(mask, ws_ref[gi], 0.0).astype(_MXU)
            cols = slice(gi * L, (gi + 1) * L)
            for c in range(tm // L):
                rows = slice(c * L, (c + 1) * L)
                vnb = vn_ref[rows, cols]
                s = jnp.dot(wc, vnb, preferred_element_type=F32) + bs_ref[:, cols]
                d = dus_ref[rows, cols]
                du_ref[rows, cols] = d * s
                ds = d * u_ref[rows, cols]
                dbs_ref[:, cols] += ds
                dsb = ds.astype(_MXU)
                dw = lax.dot_general(dsb, vnb, (((1,), (1,)), ((), ())), preferred_element_type=F32)
                dws_ref[gi] += jnp.where(mask, dw, 0.0)
                dvn_ref[rows, cols] = lax.dot_general(wc, dsb, (((0,), (0,)), ((), ())), preferred_element_type=F32)
        dvn = dvn_ref[...]
        dg_ref[...] += _fold8(dvn * xh)
        db_ref[...] += _fold8(dvn)
        dv = _ln_bwd_rows(dvn, xh, rs_ref[...], gam)
        du = du_ref[...]
        for hb in range(2):
            for part, src in ((0, du), (1, dv)):
                lo = (2 * hb + part) * il
                dp = src[:, hb * il:(hb + 1) * il] * gp[:, lo:lo + il]
                dp_ref[:, lo:lo + il] = dp.astype(_MXU)
                cs_ref[:, lo:lo + il] += _fold8(dp)

    row = lambda i: (i, 0)
    fixed = lambda i: (0, 0)
    part_c = pl.BlockSpec((SUBLANES, C), fixed)
    return pl.pallas_call(
        body, name=name, grid=(T // tm,),
        in_specs=[pl.BlockSpec((tm, D), row), _resident((None, C, D), lambda i: (0, 0, 0)),
                  pl.BlockSpec((tm, C2), row), pl.BlockSpec((tm, C), row),
                  pl.BlockSpec((tm, 1), row), pl.BlockSpec((1, C), fixed), pl.BlockSpec((1, C), fixed),
                  pl.BlockSpec((G, L, L), lambda i: (0, 0, 0)), pl.BlockSpec((L, C), fixed)],
        out_specs=[pl.BlockSpec((tm, C2), row), part_c, part_c, pl.BlockSpec((SUBLANES, C2), fixed),
                   pl.BlockSpec((G, L, L), lambda i: (0, 0, 0)), pl.BlockSpec((L, C), fixed)],
        out_shape=[jax.ShapeDtypeStruct((T, C2), _MXU), jax.ShapeDtypeStruct((SUBLANES, C), F32),
                   jax.ShapeDtypeStruct((SUBLANES, C), F32), jax.ShapeDtypeStruct((SUBLANES, C2), F32),
                   jax.ShapeDtypeStruct((G, L, L), F32), jax.ShapeDtypeStruct((L, C), F32)],
        scratch_shapes=[pltpu.VMEM((tm, C), _MXU), pltpu.VMEM((tm, C), F32), pltpu.VMEM((tm, C), F32),
                        pltpu.VMEM((tm, C), F32), pltpu.VMEM((tm, C), F32)],
        compiler_params=_cp(("arbitrary",)),
    )(dzb, w_out, p, xh, rstd, g, b, w_s, bsb)


def _ffn_conv(h, prev8, w_ref, b_ref):
    h1 = _shift_down(prev8, h, 1)
    h2 = _shift_down(prev8, h, 2)
    return w_ref[pl.ds(2, 1), :] * h + w_ref[pl.ds(1, 1), :] * h1 + w_ref[pl.ds(0, 1), :] * h2 + b_ref[...]


def _resident(block, imap):
    return pl.BlockSpec(block, imap, pipeline_mode=pl.Buffered(1))


def _ffn_fwd_half(j, xb, w_up, w_down, b_up, w_dw, b_dw, *, S, name, prev=None, tail=None, head=None):
    T, D = xb.shape
    N = w_up.shape[-1]
    tn = N // N_CHIPS
    tm = _tile(S, 256)
    spt = S // tm
    last = prev is not None
    alpha = tail[1] if last else None

    def body(*refs):
        x_ref, wu_ref, wd_ref, bu_ref, wc_ref, bc_ref = refs[:6]
        if last:
            yp_ref, res_ref, bd_ref, g_ref, b_ref = refs[9:14]
            o = 14 if head is None else 15
            h_ref, hc_ref, f_ref, y_ref, yb_ref, xh_ref, rs_ref = refs[o:o + 7]
            carry_ref = refs[-1]
        else:
            h_ref, hc_ref, f_ref, yp_ref, carry_ref = refs[6:11]

        @pl.when(pl.program_id(0) % spt == 0)
        def _():
            carry_ref[...] = jnp.zeros_like(carry_ref)

        h = jnp.dot(x_ref[...].astype(_MXU), wu_ref[...].astype(_MXU), preferred_element_type=F32) + bu_ref[...]
        h_ref[...] = h.astype(_HDT)
        hc = _ffn_conv(h, carry_ref[...], wc_ref, bc_ref)
        hc_ref[...] = hc.astype(_HDT)
        carry_ref[...] = h[tm - SUBLANES:tm]
        gte = hc[:, :tn]
        f = (gte * _sigmoid(gte) * hc[:, tn:]).astype(_MXU)
        f_ref[...] = f
        y = jnp.dot(f, wd_ref[...].astype(_MXU), preferred_element_type=F32)
        if not last:
            yp_ref[...] = y
            return
        z = y + yp_ref[...] + bd_ref[...] + alpha * res_ref[...]
        out, xh, rstd = _ln_rows(z, g_ref[...], b_ref[...])
        if head is None:
            y_ref[...] = out
            yb_ref[...] = out.astype(_MXU)
            xh_ref[...] = xh
            rs_ref[...] = rstd
            return
        t_ref, cs_ref, ls_ref = refs[14], refs[o + 7], refs[o + 8]

        @pl.when(pl.program_id(0) == 0)
        def _():
            for acc in (xh_ref, rs_ref, cs_ref, ls_ref):
                acc[...] = jnp.zeros_like(acc)

        err = out - t_ref[...]
        d = err * (1.0 / D)
        dz = _ln_bwd_rows(d, xh, rstd, g_ref[...])
        y_ref[...] = dz
        yb_ref[...] = dz.astype(_MXU)
        xh_ref[...] += _fold8(d * xh)
        rs_ref[...] += _fold8(d)
        cs_ref[...] += _fold8(dz)
        ls_ref[...] += _fold8(err * err)

    row = lambda i: (i, 0)
    pair = lambda i: (0, j)
    vec = pl.BlockSpec((1, D), lambda i: (0, 0))
    tile = pl.BlockSpec((tm, D), row)
    in_specs = [tile, _resident((None, D, 2 * tn), lambda i: (0, 0, j)), _resident((None, tn, D), lambda i: (0, j, 0)),
                pl.BlockSpec((1, 2 * tn), pair), pl.BlockSpec((SUBLANES, 2 * tn), pair), pl.BlockSpec((1, 2 * tn), pair)]
    operands = [xb, w_up, w_down, b_up, w_dw, b_dw]
    wide = pl.BlockSpec((tm, 2 * tn), lambda i: (i, j))
    out_specs = [wide, wide, pl.BlockSpec((tm, tn), lambda i: (i, j))]
    out_shape = [jax.ShapeDtypeStruct((T, N), _HDT), jax.ShapeDtypeStruct((T, N), _HDT),
                 jax.ShapeDtypeStruct((T, N // 2), _MXU)]
    aliases = {}
    if last:
        res, _, b_down, g, b = tail
        in_specs += [ANY, ANY, ANY, tile, tile, vec, vec, vec]
        operands += list(prev) + [res, b_down, g, b]
        aliases = {6: 0, 7: 1, 8: 2}
        if head is None:
            out_specs += [tile, tile, tile, pl.BlockSpec((tm, 1), row)]
            out_shape += [jax.ShapeDtypeStruct((T, D), F32), jax.ShapeDtypeStruct((T, D), _MXU),
                          jax.ShapeDtypeStruct((T, D), F32), jax.ShapeDtypeStruct((T, 1), F32)]
        else:
            in_specs.append(tile)
            operands.append(head)
            part = pl.BlockSpec((SUBLANES, D), lambda i: (0, 0))
            out_specs += [tile, tile, part, part, part, part]
            out_shape += [jax.ShapeDtypeStruct((T, D), F32), jax.ShapeDtypeStruct((T, D), _MXU)] \
                + [jax.ShapeDtypeStruct((SUBLANES, D), F32)] * 4
    else:
        out_specs.append(tile)
        out_shape.append(jax.ShapeDtypeStruct((T, D), F32))
    return pl.pallas_call(
        body, name=name, grid=(T // tm,), in_specs=in_specs, out_specs=out_specs, out_shape=out_shape,
        input_output_aliases=aliases, scratch_shapes=[pltpu.VMEM((SUBLANES, 2 * tn), F32)],
        compiler_params=_cp(("arbitrary",)),
    )(*operands)


def _ffn_bwd_half(j, dzb, w_down, w_up, hs, hcs, w_dw, *, S, name, dz=None, alpha=None, prev=None, ln=None):
    T, D = dzb.shape
    N = hs.shape[1]
    tn = N // N_CHIPS
    tm = _tile(S, 256)
    spt = S // tm
    nt = T // tm
    last = prev is not None

    def body(*refs):
        dz_ref, wd_ref, wu_ref, h_ref, hc_ref, wc_ref = refs[:6]
        if last:
            dxp_ref, xh_ref, rs_ref, g_ref = refs[7:11]
            dh_ref, cs_ref, dw_ref, db_ref, dz1_ref, dz1b_ref, dg1_ref, db1_ref, cs1_ref, carry_ref = refs[11:21]
        else:
            dzf_ref = refs[6]
            dh_ref, cs_ref, dw_ref, db_ref, dxp_ref, carry_ref = refs[7:13]
        i = pl.program_id(0)
        ii = nt - 1 - i

        @pl.when(i == 0)
        def _():
            cs_ref[...] = jnp.zeros_like(cs_ref)
            dw_ref[...] = jnp.zeros_like(dw_ref)
            db_ref[...] = jnp.zeros_like(db_ref)
            if last:
                dg1_ref[...] = jnp.zeros_like(dg1_ref)
                db1_ref[...] = jnp.zeros_like(db1_ref)
                cs1_ref[...] = jnp.zeros_like(cs1_ref)

        df = lax.dot_general(dz_ref[...].astype(_MXU), wd_ref[...].astype(_MXU), (((1,), (1,)), ((), ())),
                             preferred_element_type=F32)
        h = h_ref[...].astype(F32)
        gte, val = hc_ref[:, :tn].astype(F32), hc_ref[:, tn:].astype(F32)
        sig = _sigmoid(gte)
        dval = df * (gte * sig)
        dg = df * val * (sig * (1.0 + gte * (1.0 - sig)))
        dhc = jnp.concatenate([dg, dval], axis=1)
        nxt = jnp.where((ii + 1) % spt == 0, 0.0, carry_ref[...])
        d1 = _shift_up(dhc, nxt, 1)
        d2 = _shift_up(dhc, nxt, 2)
        carry_ref[...] = dhc[0:SUBLANES]
        db_ref[...] += _fold8(dhc)
        dw_ref[2] += _fold8(dhc * h)
        dw_ref[1] += _fold8(d1 * h)
        dw_ref[0] += _fold8(d2 * h)
        dh = wc_ref[pl.ds(2, 1), :] * dhc + wc_ref[pl.ds(1, 1), :] * d1 + wc_ref[pl.ds(0, 1), :] * d2
        cs_ref[...] += _fold8(dh)
        dhb = dh.astype(_MXU)
        dh_ref[...] = dhb
        dx = lax.dot_general(dhb, wu_ref[...].astype(_MXU), (((1,), (1,)), ((), ())), preferred_element_type=F32)
        if not last:
            dxp_ref[...] = dx + alpha * dzf_ref[...]
            return
        d = dx + dxp_ref[...]
        xh = xh_ref[...]
        dz1 = _ln_bwd_rows(d, xh, rs_ref[...], g_ref[...])
        dz1_ref[...] = dz1
        dz1b_ref[...] = dz1.astype(_MXU)
        dg1_ref[...] += _fold8(d * xh)
        db1_ref[...] += _fold8(d)
        cs1_ref[...] += _fold8(dz1)

    rev = lambda i: (nt - 1 - i, 0)
    fixed = lambda i: (0, 0)
    pair = lambda i: (0, j)
    tile = pl.BlockSpec((tm, D), rev)
    wide = pl.BlockSpec((tm, 2 * tn), lambda i: (nt - 1 - i, j))
    part = pl.BlockSpec((SUBLANES, 2 * tn), fixed)
    in_specs = [tile, _resident((None, tn, D), lambda i: (0, j, 0)), _resident((None, D, 2 * tn), lambda i: (0, 0, j)),
                wide, wide, pl.BlockSpec((SUBLANES, 2 * tn), pair)]
    operands = [dzb, w_down, w_up, hs, hcs, w_dw]
    out_specs = [wide, part, pl.BlockSpec((3, SUBLANES, 2 * tn), lambda i: (0, 0, 0)), part]
    out_shape = [jax.ShapeDtypeStruct((T, N), _MXU), jax.ShapeDtypeStruct((SUBLANES, 2 * tn), F32),
                 jax.ShapeDtypeStruct((3, SUBLANES, 2 * tn), F32), jax.ShapeDtypeStruct((SUBLANES, 2 * tn), F32)]
    aliases = {}
    if last:
        xh, rstd, g = ln
        in_specs += [ANY, tile, tile, pl.BlockSpec((tm, 1), rev), pl.BlockSpec((1, D), fixed)]
        operands += [prev[0], prev[1], xh, rstd, g]
        aliases = {6: 0}
        out_specs += [tile, tile] + [pl.BlockSpec((SUBLANES, D), fixed)] * 3
        out_shape += [jax.ShapeDtypeStruct((T, D), F32), jax.ShapeDtypeStruct((T, D), _MXU)] \
            + [jax.ShapeDtypeStruct((SUBLANES, D), F32)] * 3
    else:
        in_specs.append(tile)
        operands.append(dz)
        out_specs.append(tile)
        out_shape.append(jax.ShapeDtypeStruct((T, D), F32))
    return pl.pallas_call(
        body, name=name, grid=(nt,), in_specs=in_specs, out_specs=out_specs, out_shape=out_shape,
        input_output_aliases=aliases, scratch_shapes=[pltpu.VMEM((SUBLANES, 2 * tn), F32)],
        compiler_params=_cp(("arbitrary",)),
    )(*operands)


def _sum_pieces(gs, rs, me, *, name):
    n = len(gs)
    _, pr, pc = gs[0].shape
    tr = _tile(pr, 128)

    def body(me_ref, *refs):
        o_ref = refs[2 * n]
        for l in range(n):
            total = refs[l][...].astype(F32)
            for s in range(N_DEV - 1):
                total = total + refs[n + l][s].astype(F32)
            o_ref[l] = total

    own = pl.BlockSpec((None, tr, pc), lambda i, me_ref: (me_ref[0], i, 0))
    got = pl.BlockSpec((N_DEV - 1, tr, pc), lambda i, me_ref: (0, i, 0))
    return pl.pallas_call(
        body, name=name,
        grid_spec=pltpu.PrefetchScalarGridSpec(
            num_scalar_prefetch=1, grid=(pr // tr,), in_specs=[own] * n + [got] * n,
            out_specs=pl.BlockSpec((n, tr, pc), lambda i, me_ref: (0, i, 0))),
        out_shape=jax.ShapeDtypeStruct((n, pr, pc), F32),
        compiler_params=_cp(("parallel",)),
    )(me, *gs, *rs)


def _adam_math(w, g, m, v):
    bc1 = 1.0 - ADAM_B1 ** ADAM_STEP
    bc2 = 1.0 - ADAM_B2 ** ADAM_STEP
    m = ADAM_B1 * m + (1.0 - ADAM_B1) * g
    v = ADAM_B2 * v + (1.0 - ADAM_B2) * (g * g)
    return -ADAM_LR * ((m / bc1) / (jnp.sqrt(v / bc2) + ADAM_EPS) + ADAM_WD * w), m, v


def _adam(w, g, m, v, *, name):
    R, C = w.shape
    tr = _tile(R, 256)

    def body(w_ref, g_ref, m_ref, v_ref, d_ref, mo_ref, vo_ref):
        d_ref[...], mo_ref[...], vo_ref[...] = _adam_math(w_ref[...], g_ref[...], m_ref[...], v_ref[...])

    spec = pl.BlockSpec((tr, C), lambda i: (i, 0))
    return pl.pallas_call(
        body, name=name, grid=(R // tr,), in_specs=[spec] * 4, out_specs=[spec] * 3,
        out_shape=[jax.ShapeDtypeStruct((R, C), F32)] * 3,
        compiler_params=_cp(("parallel",)),
    )(w, g, m, v)


def _adam_halves(w, own, got, m, v, core, *, name):
    L, R, C = w.shape
    rh = R // 2
    tr = _tile(rh, 256)
    nt = rh // tr

    def body(c_ref, w_ref, own_ref, got_ref, m_ref, v_ref, g_ref, d_ref, mo_ref, vo_ref):
        g = jnp.where(pl.program_id(1) == c_ref[0], own_ref[...], got_ref[...])
        g_ref[...] = g
        d_ref[...], mo_ref[...], vo_ref[...] = _adam_math(w_ref[...], g, m_ref[...], v_ref[...])

    full = pl.BlockSpec((None, tr, C), lambda l, h, t, c_ref: (l, h * nt + t, 0))
    half = pl.BlockSpec((None, tr, C), lambda l, h, t, c_ref: (l, t, 0))
    return pl.pallas_call(
        body, name=name,
        grid_spec=pltpu.PrefetchScalarGridSpec(
            num_scalar_prefetch=1, grid=(L, 2, nt), in_specs=[full, half, half, full, full], out_specs=[full] * 4),
        out_shape=[jax.ShapeDtypeStruct((L, R, C), F32)] * 4,
        compiler_params=_cp(("parallel", "parallel", "parallel")),
    )(core, w, own, got, m, v)


def _remote(src, dst, send, recv, dev):
    return pltpu.make_async_remote_copy(src_ref=src, dst_ref=dst, send_sem=send, recv_sem=recv,
                                        device_id=dev, device_id_type=MESH)


def _place_w(shard, pos, layer, *, axis, name):
    _, R, C = shard.shape
    tr = _tile(R, 512, 16)
    nt = R // tr
    if axis == 2:
        out_shape = (1, R, N_CHIPS * C)
        out_map = lambda t, q: (0, t, q[0])
    else:
        out_shape = (1, N_CHIPS * R, C)
        out_map = lambda t, q: (0, q[0] * nt + t, 0)

    def body(q_ref, s_ref, o_ref):
        o_ref[...] = s_ref[...].astype(_WIRE)

    return pl.pallas_call(
        body, name=name,
        grid_spec=pltpu.PrefetchScalarGridSpec(
            num_scalar_prefetch=1, grid=(nt,),
            in_specs=[pl.BlockSpec((None, tr, C), lambda t, q: (layer, t, 0))],
            out_specs=pl.BlockSpec((None, tr, C), out_map)),
        out_shape=jax.ShapeDtypeStruct(out_shape, _WIRE),
        compiler_params=_cp(("parallel",)),
    )(pos, shard)


def _ag_window(ref, kind, px, py, h):
    axis, perm = kind
    q = 2 * px + py
    if perm:
        q = _perm_idx(q)
    if axis == 2:
        R, C = ref.shape[1], ref.shape[2] // N_CHIPS
        rh = R // 2
        return ref.at[:, pl.ds(pl.multiple_of(h * rh, 16), rh), pl.ds(pl.multiple_of(q * C, LANES), C)]
    R = ref.shape[1] // N_CHIPS
    rh = R // 2
    return ref.at[:, pl.ds(pl.multiple_of(q * R + h * rh, 16), rh), :]


def _ag_ici_copies(refs, kinds, send, recv):
    x, y, c = lax.axis_index("x"), lax.axis_index("y"), lax.axis_index("c")
    chips = [(1 - x, y), (x, 1 - y), (1 - x, 1 - y)]
    sends, recvs = [], []
    for a, (ref, kind) in enumerate(zip(refs, kinds)):
        own = _ag_window(ref, kind, x, y, c)
        for i, (px, py) in enumerate(chips):
            k = 3 * a + i
            sends.append(_remote(own, own, send.at[k], recv.at[k], (px, py, c)))
            recvs.append(_remote(own, _ag_window(ref, kind, px, py, c), send.at[k], recv.at[k], (px, py, c)))
    return sends, recvs


def _ag_start(arrs, kinds, after, *, name):
    n = len(arrs)

    def body(*refs):
        in_refs = refs[:n]
        send, recv = refs[n + len(after)], refs[n + len(after) + 1]
        token = refs[-1]
        sends, _ = _ag_ici_copies(in_refs, kinds, send, recv)
        for cp in sends:
            cp.start()
        token[...] = jnp.zeros_like(token)

    sems = pltpu.SemaphoreType.DMA((3 * n,))
    out = pl.pallas_call(
        body, name=name,
        out_shape=(sems, sems) + tuple(pltpu.HBM(a.shape, a.dtype) for a in arrs)
        + (jax.ShapeDtypeStruct((SUBLANES, LANES), F32),),
        in_specs=(HBM,) * n + (ANY,) * len(after),
        out_specs=(SEMS, SEMS) + (HBM,) * n + (pl.BlockSpec(memory_space=pltpu.VMEM),),
        input_output_aliases={a: 2 + a for a in range(n)},
        compiler_params=pltpu.CompilerParams(has_side_effects=EFFECT),
    )(*[pltpu.with_memory_space_constraint(a, pltpu.HBM) for a in arrs], *after)
    return out[0], out[1], list(out[2:2 + n]), out[-1]


def _ag_wait(send, recv, arrs, kinds, after, *, name):
    n = len(arrs)

    def body(*refs):
        in_refs = refs[:n]
        send, recv = refs[n], refs[n + 1]
        sends, recvs = _ag_ici_copies(in_refs, kinds, send, recv)
        for cp in sends:
            cp.wait_send()
        for cp in recvs:
            cp.wait_recv()

    out = pl.pallas_call(
        body, name=name,
        out_shape=tuple(pltpu.HBM(a.shape, a.dtype) for a in arrs),
        in_specs=(HBM,) * n + (SEMS, SEMS) + (ANY,) * len(after), out_specs=(HBM,) * n,
        input_output_aliases={a: a for a in range(n)},
        compiler_params=pltpu.CompilerParams(has_side_effects=EFFECT),
    )(*arrs, send, recv, *after)
    return list(out)


def _ag_forward(arrs, kinds, *, name):
    n = len(arrs)

    def body(*refs):
        o_refs, send, recv = refs[n:2 * n], refs[2 * n], refs[2 * n + 1]
        x, y, c = lax.axis_index("x"), lax.axis_index("y"), lax.axis_index("c")
        chips = [(1 - x, y), (x, 1 - y), (1 - x, 1 - y)]
        sib = (x, y, 1 - c)
        sends, recvs = [], []
        for a, (ref, kind) in enumerate(zip(o_refs, kinds)):
            for i, (px, py) in enumerate(chips):
                k = 3 * a + i
                got = _ag_window(ref, kind, px, py, c)
                cp = _remote(got, got, send.at[k], recv.at[k], sib)
                cp.start()
                sends.append(cp)
                recvs.append(_remote(got, _ag_window(ref, kind, px, py, 1 - c), send.at[k], recv.at[k], sib))
        for cp in recvs:
            cp.wait_recv()
        for cp in sends:
            cp.wait_send()

    out = pl.pallas_call(
        body, name=name, in_specs=[ANY] * n, out_specs=[ANY] * n,
        out_shape=[jax.ShapeDtypeStruct(a.shape, a.dtype) for a in arrs],
        input_output_aliases={a: a for a in range(n)},
        scratch_shapes=[pltpu.SemaphoreType.DMA((3 * n,)), pltpu.SemaphoreType.DMA((3 * n,))],
    )(*arrs)
    return list(out)


def _flip(x, y, c, f):
    return ((1 - x) if f & 4 else x, (1 - y) if f & 2 else y, (1 - c) if f & 1 else c)


def _rs_copies(g_refs, land_refs, send, recv):
    x, y, c = lax.axis_index("x"), lax.axis_index("y"), lax.axis_index("c")
    cps = []
    for a, (g_ref, land_ref) in enumerate(zip(g_refs, land_refs)):
        for f in range(1, N_DEV):
            tx, ty, tcx = _flip(x, y, c, f)
            k = (N_DEV - 1) * a + f - 1
            cps.append(_remote(g_ref.at[4 * tx + 2 * ty + tcx], land_ref.at[f - 1], send.at[k], recv.at[k],
                               (tx, ty, tcx)))
    return cps


def _rs_start(gs, *, name):
    n = len(gs)
    lands = [lax.empty((N_DEV - 1,) + g.shape[1:], g.dtype) for g in gs]

    def body(*refs):
        send, recv, token = refs[2 * n], refs[2 * n + 1], refs[-1]
        for cp in _rs_copies(refs[:n], refs[n:2 * n], send, recv):
            cp.start()
        token[...] = jnp.zeros_like(token)

    sems = pltpu.SemaphoreType.DMA(((N_DEV - 1) * n,))
    thru = [pltpu.HBM(t.shape, t.dtype) for t in gs + lands]
    out = pl.pallas_call(
        body, name=name,
        out_shape=(sems, sems, *thru, jax.ShapeDtypeStruct((SUBLANES, LANES), F32)),
        in_specs=(HBM,) * (2 * n), out_specs=(SEMS, SEMS) + (HBM,) * (2 * n) + (pl.BlockSpec(memory_space=pltpu.VMEM),),
        input_output_aliases={a: 2 + a for a in range(2 * n)},
        compiler_params=pltpu.CompilerParams(has_side_effects=EFFECT),
    )(*[pltpu.with_memory_space_constraint(t, pltpu.HBM) for t in gs + lands])
    return out[0], out[1], list(out[2:2 + n]), list(out[2 + n:2 + 2 * n]), out[-1]


def _rs_wait(send, recv, gs, lands, after, *, name):
    n = len(gs)

    def body(*refs):
        cps = _rs_copies(refs[:n], refs[n:2 * n], refs[2 * n], refs[2 * n + 1])
        for cp in cps:
            cp.wait_send()
        for cp in cps:
            cp.wait_recv()

    out = pl.pallas_call(
        body, name=name,
        out_shape=tuple(pltpu.HBM(t.shape, t.dtype) for t in gs + lands),
        in_specs=(HBM,) * (2 * n) + (SEMS, SEMS, ANY), out_specs=(HBM,) * (2 * n),
        input_output_aliases={a: a for a in range(2 * n)},
        compiler_params=pltpu.CompilerParams(has_side_effects=EFFECT),
    )(*gs, *lands, send, recv, after)
    return list(out[:n]), list(out[n:])


def _pair_exchange(owns, *, name):
    n = len(owns)

    def body(*refs):
        send, recv = refs[2 * n], refs[2 * n + 1]
        x, y, c = lax.axis_index("x"), lax.axis_index("y"), lax.axis_index("c")
        cps = [_remote(refs[a], refs[n + a], send.at[a], recv.at[a], (x, y, 1 - c)) for a in range(n)]
        for cp in cps:
            cp.start()
        for cp in cps:
            cp.wait_recv()
        for cp in cps:
            cp.wait_send()

    return pl.pallas_call(
        body, name=name, in_specs=[ANY] * n, out_specs=[ANY] * n,
        out_shape=[jax.ShapeDtypeStruct(o.shape, o.dtype) for o in owns],
        scratch_shapes=[pltpu.SemaphoreType.DMA((n,)), pltpu.SemaphoreType.DMA((n,))],
    )(*owns)


def _allreduce_flat(vec, *, name):
    n = vec.shape[0]
    unit = N_DEV * SUBLANES * LANES
    npad = -(-n // unit) * unit
    rows = npad // (N_DEV * LANES)
    xin = jnp.pad(vec, (0, npad - n)).reshape(N_DEV, rows, LANES)

    def body(x_ref, y_ref, a_ref, send_a, recv_a, send_b, recv_b):
        x, y, c = lax.axis_index("x"), lax.axis_index("y"), lax.axis_index("c")
        me = 4 * x + 2 * y + c
        a_ref[me] = x_ref[me]
        sends, recvs = [], []
        for f in range(1, N_DEV):
            dev = _flip(x, y, c, f)
            t = 4 * dev[0] + 2 * dev[1] + dev[2]
            cp = _remote(x_ref.at[t], a_ref.at[me], send_a.at[f - 1], recv_a.at[f - 1], dev)
            cp.start()
            sends.append(cp)
            recvs.append(_remote(x_ref.at[me], a_ref.at[t], send_a.at[f - 1], recv_a.at[f - 1], dev))
        for cp in recvs:
            cp.wait_recv()
        for cp in sends:
            cp.wait_send()
        acc = a_ref[0]
        for s in range(1, N_DEV):
            acc = acc + a_ref[s]
        y_ref[me] = acc
        sends, recvs = [], []
        for f in range(1, N_DEV):
            dev = _flip(x, y, c, f)
            t = 4 * dev[0] + 2 * dev[1] + dev[2]
            cp = _remote(y_ref.at[me], y_ref.at[me], send_b.at[f - 1], recv_b.at[f - 1], dev)
            cp.start()
            sends.append(cp)
            recvs.append(_remote(y_ref.at[me], y_ref.at[t], send_b.at[f - 1], recv_b.at[f - 1], dev))
        for cp in recvs:
            cp.wait_recv()
        for cp in sends:
            cp.wait_send()

    vm = pl.BlockSpec(memory_space=pltpu.VMEM)
    out = pl.pallas_call(
        body, name=name, in_specs=[vm], out_specs=vm,
        out_shape=jax.ShapeDtypeStruct((N_DEV, rows, LANES), F32),
        scratch_shapes=[pltpu.VMEM((N_DEV, rows, LANES), F32)] + [pltpu.SemaphoreType.DMA((N_DEV - 1,))] * 4,
        compiler_params=_cp(),
    )(xin)
    return out.reshape(npad)[:n]


def _perm_cols(v, blocks=N_CHIPS):
    lead, n = v.shape[:-1], v.shape[-1]
    return v.reshape(lead + (blocks, n // blocks))[..., PERM, :].reshape(lead + (n,))


def _pack(arrs):
    return jnp.concatenate([a.reshape(-1).astype(F32) for a in arrs])


def _unpack(flat, shapes):
    out, pos = [], 0
    for s in shapes:
        n = 1
        for d in s:
            n *= d
        out.append(flat[pos:pos + n].reshape(s))
        pos += n
    return out


def kernel(x, conv_w_in, conv_b_in, conv_w_dw, conv_b_dw, conv_ln_g, conv_ln_b, conv_w_out, conv_b_out, gmlp_w_in, gmlp_b_in, gmlp_ln_g, gmlp_ln_b, gmlp_w_s, gmlp_b_s, gmlp_w_out, gmlp_b_out, ffn_w_up, ffn_b_up, ffn_w_dw, ffn_b_dw, ffn_w_down, ffn_b_down, norm1_g, norm1_b, norm2_g, norm2_b, loss_target, m_conv_w_in, m_conv_b_in, m_conv_w_dw, m_conv_b_dw, m_conv_ln_g, m_conv_ln_b, m_conv_w_out, m_conv_b_out, m_gmlp_w_in, m_gmlp_b_in, m_gmlp_ln_g, m_gmlp_ln_b, m_gmlp_w_s, m_gmlp_b_s, m_gmlp_w_out, m_gmlp_b_out, m_ffn_w_up, m_ffn_b_up, m_ffn_w_dw, m_ffn_b_dw, m_ffn_w_down, m_ffn_b_down, m_norm1_g, m_norm1_b, m_norm2_g, m_norm2_b, v_conv_w_in, v_conv_b_in, v_conv_w_dw, v_conv_b_dw, v_conv_ln_g, v_conv_ln_b, v_conv_w_out, v_conv_b_out, v_gmlp_w_in, v_gmlp_b_in, v_gmlp_ln_g, v_gmlp_ln_b, v_gmlp_w_s, v_gmlp_b_s, v_gmlp_w_out, v_gmlp_b_out, v_ffn_w_up, v_ffn_b_up, v_ffn_w_dw, v_ffn_b_dw, v_ffn_w_down, v_ffn_b_down, v_norm1_g, v_norm1_b, v_norm2_g, v_norm2_b):
    P = dict(locals())
    WEIGHTS = ['conv_w_in', 'conv_b_in', 'conv_w_dw', 'conv_b_dw', 'conv_ln_g', 'conv_ln_b', 'conv_w_out',
               'conv_b_out', 'gmlp_w_in', 'gmlp_b_in', 'gmlp_ln_g', 'gmlp_ln_b', 'gmlp_w_s', 'gmlp_b_s',
               'gmlp_w_out', 'gmlp_b_out', 'ffn_w_up', 'ffn_b_up', 'ffn_w_dw', 'ffn_b_dw', 'ffn_w_down',
               'ffn_b_down', 'norm1_g', 'norm1_b', 'norm2_g', 'norm2_b']
    BIG = ['conv_w_in', 'conv_w_out', 'gmlp_w_in', 'gmlp_w_out', 'ffn_w_up', 'ffn_w_down']
    SMALL_SHARDED = {'conv_w_dw': 2, 'gmlp_b_in': 1, 'gmlp_ln_g': 1, 'gmlp_ln_b': 1, 'gmlp_b_out': 1, 'ffn_w_dw': 2}

    B, S, D = x.shape
    T = B * S
    depth = norm1_g.shape[0]
    alpha = (2.0 * depth) ** 0.25
    C = conv_w_out.shape[-1]
    F2 = ffn_b_up.shape[-1]
    G, L = gmlp_w_s.shape[1], gmlp_w_s.shape[2]
    xi, yi, ci = lax.axis_index("x"), lax.axis_index("y"), lax.axis_index("c")
    shard = 2 * xi + yi

    i32 = lambda v: jnp.reshape(v, (1,)).astype(jnp.int32)
    pos_plain, pos_perm = i32(shard), i32(_perm_idx(shard))
    me_id, core_id = i32(4 * xi + 2 * yi + ci), i32(ci)

    groups = []
    for i in range(depth):
        mix = 'conv' if i % 2 == 0 else 'gmlp'
        groups.append((f"{mix}{i // 2}", [(mix + '_w_in', i // 2, 2, True), (mix + '_w_out', i // 2, 1, False)]))
        groups.append((f"ffn{i}", [('ffn_w_up', i, 2, True), ('ffn_w_down', i, 1, False)]))
    sm_names = list(SMALL_SHARDED)
    sm_shapes = [P[n].shape for n in sm_names]
    mine = _pack([P[n] for n in sm_names]) * (ci == 0).astype(F32)
    buf = jnp.zeros((N_CHIPS, mine.shape[0]), F32)
    buf = lax.dynamic_update_slice(buf, mine[None], (shard, 0))
    gathered = _allreduce_flat(buf.reshape(-1), name="ag_small").reshape(N_CHIPS, -1)

    started, order = {}, [gathered]
    for gname, members in groups:
        placed = [_place_w(P[n], pos_perm if perm else pos_plain, l, axis=axis, name=f"place_{n}_{l}")
                  for n, l, axis, perm in members]
        kinds = [(axis, perm) for _, _, axis, perm in members]
        send, recv, arrs, token = _ag_start(placed, kinds, order, name=f"ag_start_{gname}")
        order = [token]
        started[gname] = (send, recv, arrs, kinds, [(n, l) for n, l, _, _ in members])
    wts = {}

    def arrive(gname, after):
        send, recv, arrs, kinds, keys = started[gname]
        arrs = _ag_wait(send, recv, arrs, kinds, after, name=f"ag_wait_{gname}")
        arrs = _ag_forward(arrs, kinds, name=f"ag_fwd_{gname}")
        wts.update(zip(keys, arrs))

    full = {}
    for n, parts in zip(sm_names, zip(*[_unpack(gathered[k], sm_shapes) for k in range(N_CHIPS)])):
        full[n] = jnp.concatenate(parts, axis=SMALL_SHARDED[n])
    for n in WEIGHTS:
        if n not in BIG and n not in full:
            full[n] = P[n]

    assert G * L == C, "a gMLP group must be as wide as a chunk is long"

    def row(v):
        return v.reshape(1, -1)

    def pad_rows(v, r):
        return jnp.pad(v, ((0, r - v.shape[0]), (0, 0)))

    xf = x.reshape(T, D)
    saved = []
    cur, cur_b = xf, xf.astype(_MXU)
    for i in range(depth):
        j = i // 2
        sv = {'x': cur, 'xb': cur_b}
        arrive(groups[2 * i][0], order if i == 0 else [cur_b])
        if i % 2 == 0:
            b_in = row(_perm_cols(full['conv_b_in'][j]))
            h1 = _mm(cur_b, wts['conv_w_in', j], bl=0, bias=b_in, tm=_tile(T, 512), tn=_tile(2 * C, 1024, LANES),
                     tk=D, name=f"conv_in_{j}", n_outer=True, out_dtype=_ADT)
            wdw = pad_rows(full['conv_w_dw'][j], CONV_TAPS_PAD)
            dwo = _conv_fwd(h1, wdw, row(full['conv_b_dw'][j]), B=B, S=S, name=f"conv_dw_{j}")
            s_act, xhc, rsc, *y1 = _conv_tail_fwd(
                dwo, row(full['conv_ln_g'][j]), row(full['conv_ln_b'][j]), wts['conv_w_out', j],
                row(full['conv_b_out'][j]), cur, alpha, row(norm1_g[i]), row(norm1_b[i]), name=f"conv_out_ln_{j}")
            sv.update(h1=h1, wdw=wdw, act=s_act, xhc=xhc, rsc=rsc)
        else:
            b_in = row(_perm_cols(full['gmlp_b_in'][j]))
            pre = _mm(cur_b, wts['gmlp_w_in', j], bl=0, bias=b_in, tm=_tile(T, 512), tn=_tile(2 * C, 1024, LANES),
                      tk=D, name=f"gmlp_in_{j}", n_outer=True, out_dtype=_ADT)
            bsb = jnp.repeat(gmlp_b_s[j].T, L, axis=1)
            us, xhv, rsv, *y1 = _gmlp_gate_fwd(
                pre, row(full['gmlp_ln_g'][j]), row(full['gmlp_ln_b'][j]), gmlp_w_s[j], bsb, wts['gmlp_w_out', j],
                row(full['gmlp_b_out'][j]), cur, alpha, row(norm1_g[i]), row(norm1_b[i]), name=f"gmlp_gate_{j}")
            sv.update(pre=pre, bsb=bsb, act=us, xhv=xhv, rsv=rsv)
        x1, x1b, xh1, rs1 = y1
        arrive(groups[2 * i + 1][0], [x1b])
        wdw3 = pad_rows(_perm_cols(full['ffn_w_dw'][i]), SUBLANES)
        bdw3 = row(_perm_cols(ffn_b_dw[i]))
        ffn_in = (x1b, wts['ffn_w_up', i], wts['ffn_w_down', i], row(_perm_cols(ffn_b_up[i])), wdw3, bdw3)
        first = _ffn_fwd_half(0, *ffn_in, S=S, name=f"ffn_fwd_a_{i}")
        ffn_tail = (x1, alpha, row(ffn_b_down[i]), row(norm2_g[i]), row(norm2_b[i]))
        sv.update(x1=x1, x1b=x1b, xh1=xh1, rs1=rs1, wdw3=wdw3)
        if i < depth - 1:
            hs, hcs, f_act, cur, cur_b, xh2, rs2 = _ffn_fwd_half(1, *ffn_in, S=S, name=f"ffn_fwd_b_{i}", prev=first,
                                                                 tail=ffn_tail)
            sv.update(xh2=xh2, rs2=rs2)
        else:
            hs, hcs, f_act, *sv['head'] = _ffn_fwd_half(1, *ffn_in, S=S, name=f"ffn_fwd_b_{i}", prev=first,
                                                         tail=ffn_tail, head=loss_target.reshape(T, D))
        sv.update(hs=hs, hcs=hcs, f=f_act)
        saved.append(sv)

    sg = {n: [None] * full[n].shape[0] for n in WEIGHTS if n not in BIG}
    inflight = {n: [None] * P[n].shape[0] for n in BIG}
    deps = []
    dcur = None
    loss_part = None
    tk_t = _tile(T, 2048)

    ready = []

    def wgrad(n, l, a_, b_, **kw):
        ready.append((n, l, _mm(a_, b_, ta=True, out_dtype=_WIRE, tk=tk_t, name=f"{n}_dw_{l}", deps=deps, **kw)))
        launch(f"{n}_{l}")

    def launch(gname):
        send, recv, gs, lands, token = _rs_start([g for _, _, g in ready], name=f"rs_start_{gname}")
        group = {'name': gname, 'flight': (send, recv, gs, lands), 'landed': None}
        for a, (n, l, _) in enumerate(ready):
            inflight[n][l] = (group, a)
        del ready[:]
        deps.append(token)

    def landed(n, l):
        group, a = inflight[n][l]
        if group['landed'] is None:
            group['landed'] = _rs_wait(*group['flight'], dcur, name=f"rs_wait_{group['name']}")
        return group['landed'][0][a], group['landed'][1][a]

    for i in reversed(range(depth)):
        j = i // 2
        sv = saved[i]
        if i == depth - 1:
            dz2, dz2b, dg, db, cs, loss_part = sv['head']
        else:
            dz2, dz2b, dg, db, cs = dcur
        sg['norm2_g'][i], sg['norm2_b'][i], sg['ffn_b_down'][i] = dg.sum(0), db.sum(0), cs.sum(0)
        Fh = F2 // 2
        wgrad('ffn_w_down', i, sv['f'], dz2b, tm=Fh // 2, tn=_tile(D, 1024, LANES), pieces=('row',))
        ffn_in = (dz2b, wts['ffn_w_down', i], wts['ffn_w_up', i], sv['hs'], sv['hcs'], sv['wdw3'])
        dh0, csu0, dwd0, dbd0, dxp = _ffn_bwd_half(0, *ffn_in, S=S, name=f"ffn_bwd_a_{i}", dz=dz2, alpha=alpha)
        dh, csu1, dwd1, dbd1, dz1, dz1b, dg, db, cs = _ffn_bwd_half(
            1, *ffn_in, S=S, name=f"ffn_bwd_b_{i}", prev=(dh0, dxp), ln=(sv['xh1'], sv['rs1'], row(norm1_g[i])))
        sg['ffn_b_up'][i] = _perm_cols(jnp.concatenate([csu0.sum(0), csu1.sum(0)], axis=-1))
        sg['ffn_w_dw'][i] = _perm_cols(jnp.concatenate([dwd0.sum(1), dwd1.sum(1)], axis=-1))
        sg['ffn_b_dw'][i] = _perm_cols(jnp.concatenate([dbd0.sum(0), dbd1.sum(0)], axis=-1))
        wgrad('ffn_w_up', i, sv['x1b'], dh, tm=D, tn=F2 // N_CHIPS, pieces=('col', True))
        sg['norm1_g'][i], sg['norm1_b'][i] = dg.sum(0), db.sum(0)
        if i % 2 == 0:
            sg['conv_b_out'][j] = cs.sum(0)
            wgrad('conv_w_out', j, sv['act'], dz1b, tm=_tile(C, 1024), tn=_tile(D, 1024, LANES), pieces=('row',))
            ddw, dg, db = _ln_silu_bwd(dz1b, wts['conv_w_out', j], sv['xhc'], sv['rsc'], row(full['conv_ln_g'][j]),
                                       row(full['conv_ln_b'][j]), name=f"conv_ln_bwd_{j}")
            sg['conv_ln_g'][j], sg['conv_ln_b'][j] = dg.sum(0), db.sum(0)
            dglu, dwk, dbk = _conv_bwd(ddw, sv['h1'], sv['wdw'], B=B, S=S, name=f"conv_dw_bwd_{j}")
            sg['conv_w_dw'][j] = dwk.sum(1)[:conv_w_dw.shape[1]]
            sg['conv_b_dw'][j] = dbk.sum(0)
            dh1, csi = _glu_bwd(dglu, sv['h1'], name=f"conv_glu_bwd_{j}")
            sg['conv_b_in'][j] = _perm_cols(csi.sum(0))
            fam = 'conv_w_in'
        else:
            sg['gmlp_b_out'][j] = cs.sum(0)
            wgrad('gmlp_w_out', j, sv['act'], dz1b, tm=_tile(C, 1024), tn=_tile(D, 1024, LANES), pieces=('row',))
            dh1, dg, db, csi, dws, dbs = _gmlp_gate_bwd(dz1b, wts['gmlp_w_out', j], sv['pre'], sv['xhv'], sv['rsv'],
                                                        row(full['gmlp_ln_g'][j]), row(full['gmlp_ln_b'][j]),
                                                        gmlp_w_s[j], sv['bsb'], name=f"gmlp_gate_bwd_{j}")
            sg['gmlp_ln_g'][j], sg['gmlp_ln_b'][j] = dg.sum(0), db.sum(0)
            sg['gmlp_b_in'][j] = _perm_cols(csi.sum(0))
            sg['gmlp_w_s'][j] = dws
            sg['gmlp_b_s'][j] = dbs.reshape(L, G, L).sum(-1).T
            fam = 'gmlp_w_in'
        wgrad(fam, j, sv['xb'], dh1, tm=D, tn=(2 * C) // N_CHIPS, pieces=('col', True))
        if i > 0:
            below = saved[i - 1]
            dcur = _mm_ln_bwd(dh1, wts[fam, j], dz1, alpha, below['xh2'], below['rs2'], row(norm2_g[i - 1]),
                              name=f"{fam}_dx_{j}", deps=deps)
        else:
            dcur = _mm(dh1, wts[fam, j], bl=0, tb=True, res=dz1, res_scale=alpha, tm=_tile(T, 512),
                       tn=_tile(D, 1024, LANES), tk=2 * C, name=f"{fam}_dx_{j}", deps=deps)
    grad_x = dcur.reshape(B, S, D)

    small_names = [n for n in WEIGHTS if n not in BIG]
    small_full = [jnp.stack(sg[n]) for n in small_names]
    flat = _pack(small_full + [loss_part])
    red = _allreduce_flat(flat, name="ar_small")
    red_parts = _unpack(red, [a.shape for a in small_full] + [loss_part.shape])
    loss = (0.5 / D) * jnp.sum(red_parts[-1])
    grads = {}
    for n, g in zip(small_names, red_parts[:-1]):
        if n in SMALL_SHARDED:
            ax = SMALL_SHARDED[n]
            width = P[n].shape[ax]
            g = lax.dynamic_slice_in_dim(g, shard * width, width, axis=ax)
        grads[n] = g

    owns = []
    for n in BIG:
        both = [landed(n, l) for l in range(len(inflight[n]))]
        owns.append(_sum_pieces([g for g, _ in both], [r for _, r in both], me_id, name=f"sum_{n}"))
    gots = _pair_exchange(owns, name="px_big")
    big_out = {n: _adam_halves(P[n], own, got, P['m_' + n], P['v_' + n], core_id, name=f"adam_{n}")
               for n, own, got in zip(BIG, owns, gots)}

    shapes = [P[n].shape for n in small_names]
    n_small = sum(functools.reduce(lambda p_, d_: p_ * d_, s_, 1) for s_ in shapes)
    unit = SUBLANES * LANES
    npad = -(-n_small // unit) * unit

    def flat2d(arrs, fill=0.0):
        v = _pack(arrs)
        return jnp.pad(v, (0, npad - n_small), constant_values=fill).reshape(-1, LANES)

    dl, mo, vo = _adam(flat2d([P[n] for n in small_names]), flat2d([grads[n] for n in small_names]),
                       flat2d([P['m_' + n] for n in small_names]),
                       flat2d([P['v_' + n] for n in small_names], fill=1.0), name="adam_small")
    small_out = {n: [grads[n], None, None, None] for n in small_names}
    for k, t in enumerate((dl, mo, vo)):
        for n, a in zip(small_names, _unpack(t.reshape(-1), shapes)):
            small_out[n][k + 1] = a

    outs = [loss, grad_x]
    for k in range(4):
        for n in WEIGHTS:
            outs.append(big_out[n][k] if n in BIG else small_out[n][k])
    return tuple(outs)
```

```python
import functools

import jax
import jax.numpy as jnp
from jax import lax
from jax.experimental import pallas as pl
from jax.experimental.pallas import tpu as pltpu

F32 = jnp.float32
_MXU = jnp.bfloat16
_WIRE = jnp.bfloat16
_HDT = jnp.bfloat16
_ADT = jnp.bfloat16
LN_EPS = 1e-5
ADAM_LR, ADAM_B1, ADAM_B2, ADAM_EPS, ADAM_WD, ADAM_STEP = 0.001, 0.9, 0.999, 1e-08, 0.01, 10
N_CHIPS = 4
N_DEV = 8
LANES = 128
SUBLANES = 8
CONV_TAPS_PAD = 32
VMEM_LIMIT = 56 << 20
MESH = pl.DeviceIdType.MESH
ANY = pl.BlockSpec(memory_space=pl.ANY)
HBM = pl.BlockSpec(memory_space=pltpu.HBM)
SEMS = pl.BlockSpec(memory_space=pltpu.SEMAPHORE)
EFFECT = pltpu.SideEffectType.DATAFLOW_SIDE_EFFECTING
PERM = (0, 2, 1, 3)


def _cp(sem=None):
    return pltpu.CompilerParams(dimension_semantics=sem, vmem_limit_bytes=VMEM_LIMIT)


def _tile(dim, pref, mult=SUBLANES):
    if dim <= pref:
        return dim
    t = (pref // mult) * mult
    while t > mult and dim % t:
        t -= mult
    assert dim % t == 0, (dim, pref, mult)
    return t


def _perm_idx(q):
    return (q % 2) * 2 + q // 2


def _fold8(t):
    r, n = t.shape
    return t.reshape(r // SUBLANES, SUBLANES, n).sum(axis=0)


def _ln_rows(z, g, b):
    mu = jnp.mean(z, axis=-1, keepdims=True)
    xc = z - mu
    var = jnp.mean(xc * xc, axis=-1, keepdims=True)
    rstd = lax.rsqrt(var + LN_EPS)
    xh = xc * rstd
    return xh * g + b, xh, rstd


def _ln_bwd_rows(dy, xh, rstd, g):
    dxh = dy * g
    m1 = jnp.mean(dxh, axis=-1, keepdims=True)
    m2 = jnp.mean(dxh * xh, axis=-1, keepdims=True)
    return rstd * (dxh - m1 - xh * m2)


def _sigmoid(v):
    return 0.5 * jnp.tanh(0.5 * v) + 0.5


def _gelu_parts(p):
    cdf = 0.5 * (1.0 + lax.erf(p * 0.7071067811865476))
    pdf = jnp.exp(-0.5 * p * p) * 0.3989422804014327
    return p * cdf, cdf + p * pdf


def _shift_down(prev8, t, s):
    ext = jnp.concatenate([prev8, t], axis=0)
    return pltpu.roll(ext, s, 0)[SUBLANES:]


def _shift_up(t, next8, s):
    n = t.shape[0]
    ext = jnp.concatenate([t, next8], axis=0)
    return pltpu.roll(ext, n + SUBLANES - s, 0)[:n]


def _mm(a, b, *, ta=False, tb=False, bl=None, bias=None, res=None, res_scale=1.0, out_dtype=F32,
        tm, tn, tk, name, pieces=None, deps=None, n_outer=False):
    M, K = (a.shape[1], a.shape[0]) if ta else a.shape
    bs = b.shape[1:] if bl is not None else b.shape
    N, Kb = (bs[0], bs[1]) if tb else (bs[1], bs[0])
    assert K == Kb and M % tm == 0 and N % tn == 0 and K % tk == 0, (a.shape, b.shape, tm, tn, tk)
    gm, gn, gk = M // tm, N // tn, K // tk

    def spec(block, imap):
        if n_outer:
            return pl.BlockSpec(block, lambda j, i, k: imap(i, j, k))
        return pl.BlockSpec(block, imap)

    a_spec = spec((tk, tm), lambda i, j, k: (k, i)) if ta else spec((tm, tk), lambda i, j, k: (i, k))
    bblk = (tn, tk) if tb else (tk, tn)
    bmap = (lambda i, j, k: (j, k)) if tb else (lambda i, j, k: (k, j))
    if bl is not None:
        b_spec = spec((None,) + bblk, lambda i, j, k: (bl,) + bmap(i, j, k))
    else:
        b_spec = spec(bblk, bmap)
    in_specs, operands = [a_spec, b_spec], [a, b]
    if bias is not None:
        in_specs.append(spec((1, tn), lambda i, j, k: (0, j)))
        operands.append(bias)
    if res is not None:
        in_specs.append(spec((tm, tn), lambda i, j, k: (i, j)))
        operands.append(res)
    n_dep = len(deps) if deps else 0
    if n_dep:
        in_specs += [ANY] * n_dep
        operands += deps
        del deps[:]
    if pieces is None:
        out_shape = jax.ShapeDtypeStruct((M, N), out_dtype)
        out_spec = spec((tm, tn), lambda i, j, k: (i, j))
        ppb = pr = None
    elif pieces[0] == 'col':
        pr, pc = M // 2, N // N_CHIPS
        assert tm % pr == 0 and pc % tn == 0
        ppb, per = tm // pr, pc // tn
        perm = pieces[1]
        out_shape = jax.ShapeDtypeStruct((N_DEV, pr, pc), out_dtype)
        out_spec = spec(
            (ppb, pr, tn),
            lambda i, j, k: ((2 * (_perm_idx(j // per) if perm else j // per)) // ppb + i, 0, j % per))
    else:
        pr = M // N_DEV
        assert tm % pr == 0
        ppb = tm // pr
        out_shape = jax.ShapeDtypeStruct((N_DEV, pr, N), out_dtype)
        out_spec = spec((ppb, pr, tn), lambda i, j, k: (i, 0, j))
    dims = (((0 if ta else 1,), (1 if tb else 0,)), ((), ()))

    def body(*refs):
        a_ref, b_ref = refs[0], refs[1]
        pos = 2
        bias_ref = res_ref = None
        if bias is not None:
            bias_ref = refs[pos]
            pos += 1
        if res is not None:
            res_ref = refs[pos]
            pos += 1
        pos += n_dep
        o_ref = refs[pos]

        def finish(r):
            if bias_ref is not None:
                r = r + bias_ref[...]
            if res_ref is not None:
                r = r + res_scale * res_ref[...]
            if pieces is not None:
                r = r.reshape(ppb, pr, tn)
            o_ref[...] = r.astype(out_dtype)

        part = lax.dot_general(a_ref[...].astype(_MXU), b_ref[...].astype(_MXU), dims, preferred_element_type=F32)
        if gk == 1:
            finish(part)
            return
        acc_ref = refs[pos + 1]
        k = pl.program_id(2)

        @pl.when(k == 0)
        def _():
            acc_ref[...] = part

        @pl.when((k > 0) & (k < gk - 1))
        def _():
            acc_ref[...] += part

        @pl.when(k == gk - 1)
        def _():
            finish(acc_ref[...] + part)

    return pl.pallas_call(
        body, name=name, grid=(gn, gm, gk) if n_outer else (gm, gn, gk), in_specs=in_specs, out_specs=out_spec,
        out_shape=out_shape, scratch_shapes=[pltpu.VMEM((tm, tn), F32)] if gk > 1 else [],
        compiler_params=_cp(("parallel", "parallel", "arbitrary")),
    )(*operands)


def _mm_ln_bwd(a, w, res, res_scale, xh, rstd, g, *, name, deps=None):
    T, K = a.shape
    D = w.shape[1]
    tm = _tile(T, 512)
    n_dep = len(deps) if deps else 0

    def body(a_ref, w_ref, res_ref, xh_ref, rs_ref, g_ref, *rest):
        dz_ref, dzb_ref, dg_ref, db_ref, cs_ref = rest[n_dep:]

        @pl.when(pl.program_id(0) == 0)
        def _():
            dg_ref[...] = jnp.zeros_like(dg_ref)
            db_ref[...] = jnp.zeros_like(db_ref)
            cs_ref[...] = jnp.zeros_like(cs_ref)

        d = lax.dot_general(a_ref[...].astype(_MXU), w_ref[...].astype(_MXU), (((1,), (1,)), ((), ())),
                            preferred_element_type=F32) + res_scale * res_ref[...]
        xh = xh_ref[...]
        dz = _ln_bwd_rows(d, xh, rs_ref[...], g_ref[...])
        dz_ref[...] = dz
        dzb_ref[...] = dz.astype(_MXU)
        dg_ref[...] += _fold8(d * xh)
        db_ref[...] += _fold8(d)
        cs_ref[...] += _fold8(dz)

    row = lambda i: (i, 0)
    fixed = lambda i: (0, 0)
    tile = pl.BlockSpec((tm, D), row)
    part = pl.BlockSpec((SUBLANES, D), fixed)
    operands = [a, w, res, xh, rstd, g] + (list(deps) if deps else [])
    if deps:
        del deps[:]
    return pl.pallas_call(
        body, name=name, grid=(T // tm,),
        in_specs=[pl.BlockSpec((tm, K), row),
                  pl.BlockSpec((None, D, K), lambda i: (0, 0, 0), pipeline_mode=pl.Buffered(1)),
                  tile, tile, pl.BlockSpec((tm, 1), row), pl.BlockSpec((1, D), fixed)] + [ANY] * n_dep,
        out_specs=[tile, tile, part, part, part],
        out_shape=[jax.ShapeDtypeStruct((T, D), F32), jax.ShapeDtypeStruct((T, D), _MXU)]
        + [jax.ShapeDtypeStruct((SUBLANES, D), F32)] * 3,
        compiler_params=_cp(("arbitrary",)),
    )(*operands)


def _out_ln(act, wo_ref, bias_ref, res_ref, alpha, g_ref, b_ref, y_ref, yb_ref, xh_ref, rs_ref):
    z = jnp.dot(act, wo_ref[...].astype(_MXU), preferred_element_type=F32) + bias_ref[...] + alpha * res_ref[...]
    y, xh, rstd = _ln_rows(z, g_ref[...], b_ref[...])
    y_ref[...] = y
    yb_ref[...] = y.astype(_MXU)
    xh_ref[...] = xh
    rs_ref[...] = rstd


def _conv_tail_fwd(v, gc, bc, w, bias, res, alpha, g, b, *, name):
    T, C = v.shape
    D = w.shape[-1]
    tm = _tile(T, 512)

    def body(v_ref, gc_ref, bc_ref, w_ref, bias_ref, res_ref, g_ref, b_ref,
             s_ref, xhc_ref, rsc_ref, y_ref, yb_ref, xh_ref, rs_ref):
        yv, xhc, rsc = _ln_rows(v_ref[...], gc_ref[...], bc_ref[...])
        s = (yv * _sigmoid(yv)).astype(_MXU)
        s_ref[...] = s
        xhc_ref[...] = xhc
        rsc_ref[...] = rsc
        _out_ln(s, w_ref, bias_ref, res_ref, alpha, g_ref, b_ref, y_ref, yb_ref, xh_ref, rs_ref)

    row = lambda i: (i, 0)
    fixed = lambda i: (0, 0)
    vc, vd = pl.BlockSpec((1, C), fixed), pl.BlockSpec((1, D), fixed)
    tc_, td = pl.BlockSpec((tm, C), row), pl.BlockSpec((tm, D), row)
    one = pl.BlockSpec((tm, 1), row)
    return pl.pallas_call(
        body, name=name, grid=(T // tm,),
        in_specs=[tc_, vc, vc, _resident((None, C, D), lambda i: (0, 0, 0)), vd, td, vd, vd],
        out_specs=[tc_, tc_, one, td, td, td, one],
        out_shape=[jax.ShapeDtypeStruct((T, C), _MXU), jax.ShapeDtypeStruct((T, C), F32),
                   jax.ShapeDtypeStruct((T, 1), F32), jax.ShapeDtypeStruct((T, D), F32),
                   jax.ShapeDtypeStruct((T, D), _MXU), jax.ShapeDtypeStruct((T, D), F32),
                   jax.ShapeDtypeStruct((T, 1), F32)],
        compiler_params=_cp(("parallel",)),
    )(v, gc, bc, w, bias, res, g, b)


def _conv_cols(C, tc):
    per = (C // 2) // tc
    return per, (lambda j: (j // per) * (2 * per) + j % per)


def _glu_shifted(a_ref, g_ref, p_ref, S):
    u = a_ref[...].astype(F32) * _sigmoid(g_ref[...].astype(F32))
    rows = lax.broadcasted_iota(jnp.int32, (SUBLANES, u.shape[1]), 0)
    lo = CONV_TAPS_PAD
    for r in range(SUBLANES):
        p_ref[r, 0:lo, :] = jnp.zeros((lo, u.shape[1]), F32)
        if r == 0:
            p_ref[r, lo:lo + S, :] = u
        else:
            rolled = pltpu.roll(u, r, 0)
            p_ref[r, lo:lo + S, :] = rolled
            p_ref[r, lo:lo + SUBLANES, :] = jnp.where(rows >= r, rolled[0:SUBLANES], 0.0)


def _conv_fwd(h1, w_dw, b_dw, *, B, S, name):
    C = w_dw.shape[1]
    taps = CONV_TAPS_PAD - 1
    tc = LANES
    ch = _tile(S, 128)
    per, col_a = _conv_cols(C, tc)

    def body(a_ref, g_ref, w_ref, b_ref, o_ref, p_ref):
        _glu_shifted(a_ref, g_ref, p_ref, S)

        def chunk(ci, carry):
            base = pl.multiple_of(ci * ch, ch)
            acc = jnp.zeros((ch, tc), F32) + b_ref[...]
            for k in range(taps):
                q, r = divmod(taps - 1 - k, SUBLANES)
                start = pl.multiple_of(base + (CONV_TAPS_PAD - SUBLANES * q), SUBLANES)
                acc = acc + w_ref[pl.ds(k, 1), :] * p_ref[r, pl.ds(start, ch), :]
            o_ref[pl.ds(base, ch), :] = acc
            return carry

        lax.fori_loop(0, S // ch, chunk, 0)

    return pl.pallas_call(
        body, name=name, grid=(B, C // tc),
        in_specs=[pl.BlockSpec((S, tc), lambda b, j: (b, col_a(j))),
                  pl.BlockSpec((S, tc), lambda b, j: (b, col_a(j) + per)),
                  pl.BlockSpec((CONV_TAPS_PAD, tc), lambda b, j: (0, j)),
                  pl.BlockSpec((1, tc), lambda b, j: (0, j))],
        out_specs=pl.BlockSpec((S, tc), lambda b, j: (b, j)),
        out_shape=jax.ShapeDtypeStruct((B * S, C), F32),
        scratch_shapes=[pltpu.VMEM((SUBLANES, S + CONV_TAPS_PAD, tc), F32)],
        compiler_params=_cp(("parallel", "parallel")),
    )(h1, h1, w_dw, b_dw)


def _conv_bwd(dd, h1, w_dw, *, B, S, name):
    C = w_dw.shape[1]
    taps = CONV_TAPS_PAD - 1
    tc = LANES
    ch = _tile(S, 128)
    per, col_a = _conv_cols(C, tc)

    def body(d_ref, a_ref, g_ref, w_ref, du_ref, dw_ref, db_ref, p_ref, q_ref):
        b = pl.program_id(1)

        @pl.when(b == 0)
        def _():
            dw_ref[...] = jnp.zeros_like(dw_ref)
            db_ref[...] = jnp.zeros_like(db_ref)

        _glu_shifted(a_ref, g_ref, p_ref, S)
        d = d_ref[...]
        rows = lax.broadcasted_iota(jnp.int32, (SUBLANES, tc), 0)
        for r in range(SUBLANES):
            q_ref[r, S:S + CONV_TAPS_PAD, :] = jnp.zeros((CONV_TAPS_PAD, tc), F32)
            if r == 0:
                q_ref[r, 0:S, :] = d
            else:
                rolled = pltpu.roll(d, S - r, 0)
                q_ref[r, 0:S, :] = rolled
                q_ref[r, S - SUBLANES:S, :] = jnp.where(rows < SUBLANES - r, rolled[S - SUBLANES:S], 0.0)
        db_ref[...] += _fold8(d)

        def chunk(ci, carry):
            base = pl.multiple_of(ci * ch, ch)
            dch = d_ref[pl.ds(base, ch), :]
            acc = jnp.zeros((ch, tc), F32)
            for k in range(taps):
                q, r = divmod(taps - 1 - k, SUBLANES)
                up = pl.multiple_of(base + SUBLANES * q, SUBLANES)
                acc = acc + w_ref[pl.ds(k, 1), :] * q_ref[r, pl.ds(up, ch), :]
                down = pl.multiple_of(base + (CONV_TAPS_PAD - SUBLANES * q), SUBLANES)
                dw_ref[k] += _fold8(dch * p_ref[r, pl.ds(down, ch), :])
            du_ref[pl.ds(base, ch), :] = acc
            return carry

        lax.fori_loop(0, S // ch, chunk, 0)

    return pl.pallas_call(
        body, name=name, grid=(C // tc, B),
        in_specs=[pl.BlockSpec((S, tc), lambda j, b: (b, j)),
                  pl.BlockSpec((S, tc), lambda j, b: (b, col_a(j))),
                  pl.BlockSpec((S, tc), lambda j, b: (b, col_a(j) + per)),
                  pl.BlockSpec((CONV_TAPS_PAD, tc), lambda j, b: (0, j))],
        out_specs=[pl.BlockSpec((S, tc), lambda j, b: (b, j)),
                   pl.BlockSpec((CONV_TAPS_PAD, SUBLANES, tc), lambda j, b: (0, 0, j)),
                   pl.BlockSpec((SUBLANES, tc), lambda j, b: (0, j))],
        out_shape=[jax.ShapeDtypeStruct((B * S, C), F32),
                   jax.ShapeDtypeStruct((CONV_TAPS_PAD, SUBLANES, C), F32),
                   jax.ShapeDtypeStruct((SUBLANES, C), F32)],
        scratch_shapes=[pltpu.VMEM((SUBLANES, S + CONV_TAPS_PAD, tc), F32),
                        pltpu.VMEM((SUBLANES, S + CONV_TAPS_PAD, tc), F32)],
        compiler_params=_cp(("parallel", "arbitrary")),
    )(dd, h1, h1, w_dw)


def _ln_silu_bwd(dzb, w, xh, rstd, g, b, *, name):
    T, D = dzb.shape
    C = w.shape[1]
    tm = _tile(T, 512)

    def body(dz_ref, w_ref, xh_ref, rs_ref, g_ref, b_ref, dv_ref, dg_ref, db_ref):
        @pl.when(pl.program_id(0) == 0)
        def _():
            dg_ref[...] = jnp.zeros_like(dg_ref)
            db_ref[...] = jnp.zeros_like(db_ref)

        ds = lax.dot_general(dz_ref[...].astype(_MXU), w_ref[...].astype(_MXU), (((1,), (1,)), ((), ())),
                             preferred_element_type=F32)
        xh = xh_ref[...]
        gam = g_ref[...]
        y = xh * gam + b_ref[...]
        sig = _sigmoid(y)
        dln = ds * (sig * (1.0 + y * (1.0 - sig)))
        dv_ref[...] = _ln_bwd_rows(dln, xh, rs_ref[...], gam)
        dg_ref[...] += _fold8(dln * xh)
        db_ref[...] += _fold8(dln)

    row = lambda i: (i, 0)
    fixed = lambda i: (0, 0)
    vec = pl.BlockSpec((1, C), fixed)
    part = pl.BlockSpec((SUBLANES, C), fixed)
    return pl.pallas_call(
        body, name=name, grid=(T // tm,),
        in_specs=[pl.BlockSpec((tm, D), row), _resident((None, C, D), lambda i: (0, 0, 0)),
                  pl.BlockSpec((tm, C), row), pl.BlockSpec((tm, 1), row), vec, vec],
        out_specs=[pl.BlockSpec((tm, C), row), part, part],
        out_shape=[jax.ShapeDtypeStruct((T, C), F32)] + [jax.ShapeDtypeStruct((SUBLANES, C), F32)] * 2,
        compiler_params=_cp(("arbitrary",)),
    )(dzb, w, xh, rstd, g, b)


def _glu_bwd(du, h1, *, name):
    T, C = du.shape
    il = C // 2
    tm = _tile(T, 512)

    def body(du_ref, h_ref, dh_ref, cs_ref):
        @pl.when(pl.program_id(0) == 0)
        def _():
            cs_ref[...] = jnp.zeros_like(cs_ref)

        for hb in range(2):
            a = h_ref[:, 2 * hb * il:(2 * hb + 1) * il].astype(F32)
            gate = h_ref[:, (2 * hb + 1) * il:(2 * hb + 2) * il].astype(F32)
            d = du_ref[:, hb * il:(hb + 1) * il]
            sig = _sigmoid(gate)
            da = d * sig
            dgate = d * a * sig * (1.0 - sig)
            dh_ref[:, 2 * hb * il:(2 * hb + 1) * il] = da.astype(_MXU)
            dh_ref[:, (2 * hb + 1) * il:(2 * hb + 2) * il] = dgate.astype(_MXU)
            cs_ref[:, 2 * hb * il:(2 * hb + 1) * il] += _fold8(da)
            cs_ref[:, (2 * hb + 1) * il:(2 * hb + 2) * il] += _fold8(dgate)

    row = lambda i: (i, 0)
    return pl.pallas_call(
        body, name=name, grid=(T // tm,),
        in_specs=[pl.BlockSpec((tm, C), row), pl.BlockSpec((tm, 2 * C), row)],
        out_specs=[pl.BlockSpec((tm, 2 * C), row), pl.BlockSpec((SUBLANES, 2 * C), lambda i: (0, 0))],
        out_shape=[jax.ShapeDtypeStruct((T, 2 * C), _MXU), jax.ShapeDtypeStruct((SUBLANES, 2 * C), F32)],
        compiler_params=_cp(("arbitrary",)),
    )(du, h1)


def _tril_mask(n):
    return lax.broadcasted_iota(jnp.int32, (n, n), 0) >= lax.broadcasted_iota(jnp.int32, (n, n), 1)


def _split_uv(t, il):
    u = jnp.concatenate([t[:, 0:il], t[:, 2 * il:3 * il]], axis=1)
    v = jnp.concatenate([t[:, il:2 * il], t[:, 3 * il:4 * il]], axis=1)
    return u, v


def _gmlp_gate_fwd(p, g, b, w_s, bsb, w_out, bias, res, alpha, g1, b1, *, name):
    T, C2 = p.shape
    C = C2 // 2
    D = w_out.shape[-1]
    il = C // 2
    G, L, _ = w_s.shape
    assert G * L == C
    tm = _tile(T, 4 * L, L)

    def body(p_ref, g_ref, b_ref, ws_ref, bs_ref, wo_ref, bias_ref, res_ref, g1_ref, b1_ref,
             us_ref, xh_ref, rs_ref, y_ref, yb_ref, xh1_ref, rs1_ref, vn_ref, u_ref):
        z, _ = _gelu_parts(p_ref[...].astype(F32))
        u, v = _split_uv(z, il)
        vn, xh, rstd = _ln_rows(v, g_ref[...], b_ref[...])
        xh_ref[...] = xh
        rs_ref[...] = rstd
        vn_ref[...] = vn.astype(_MXU)
        u_ref[...] = u
        mask = _tril_mask(L)
        for gi in range(G):
            wc = jnp.where(mask, ws_ref[gi], 0.0).astype(_MXU)
            cols = slice(gi * L, (gi + 1) * L)
            for c in range(tm // L):
                rows = slice(c * L, (c + 1) * L)
                s = jnp.dot(wc, vn_ref[rows, cols], preferred_element_type=F32) + bs_ref[:, cols]
                us_ref[rows, cols] = (u_ref[rows, cols] * s).astype(_MXU)
        _out_ln(us_ref[...], wo_ref, bias_ref, res_ref, alpha, g1_ref, b1_ref, y_ref, yb_ref, xh1_ref, rs1_ref)

    row = lambda i: (i, 0)
    fixed = lambda i: (0, 0)
    vd, td, one = pl.BlockSpec((1, D), fixed), pl.BlockSpec((tm, D), row), pl.BlockSpec((tm, 1), row)
    return pl.pallas_call(
        body, name=name, grid=(T // tm,),
        in_specs=[pl.BlockSpec((tm, C2), row), pl.BlockSpec((1, C), fixed), pl.BlockSpec((1, C), fixed),
                  pl.BlockSpec((G, L, L), lambda i: (0, 0, 0)), pl.BlockSpec((L, C), fixed),
                  _resident((None, C, D), lambda i: (0, 0, 0)), vd, td, vd, vd],
        out_specs=[pl.BlockSpec((tm, C), row), pl.BlockSpec((tm, C), row), one, td, td, td, one],
        out_shape=[jax.ShapeDtypeStruct((T, C), _MXU), jax.ShapeDtypeStruct((T, C), F32),
                   jax.ShapeDtypeStruct((T, 1), F32), jax.ShapeDtypeStruct((T, D), F32),
                   jax.ShapeDtypeStruct((T, D), _MXU), jax.ShapeDtypeStruct((T, D), F32),
                   jax.ShapeDtypeStruct((T, 1), F32)],
        scratch_shapes=[pltpu.VMEM((tm, C), _MXU), pltpu.VMEM((tm, C), F32)],
        compiler_params=_cp(("parallel",)),
    )(p, g, b, w_s, bsb, w_out, bias, res, g1, b1)


def _gmlp_gate_bwd(dzb, w_out, p, xh, rstd, g, b, w_s, bsb, *, name):
    T, C2 = p.shape
    D = dzb.shape[1]
    C = C2 // 2
    il = C // 2
    G, L, _ = w_s.shape
    tm = _tile(T, 4 * L, L)

    def body(dz_ref, wo_ref, p_ref, xh_ref, rs_ref, g_ref, b_ref, ws_ref, bs_ref,
             dp_ref, dg_ref, db_ref, cs_ref, dws_ref, dbs_ref, vn_ref, u_ref, dvn_ref, du_ref, dus_ref):
        @pl.when(pl.program_id(0) == 0)
        def _():
            dg_ref[...] = jnp.zeros_like(dg_ref)
            db_ref[...] = jnp.zeros_like(db_ref)
            cs_ref[...] = jnp.zeros_like(cs_ref)
            dws_ref[...] = jnp.zeros_like(dws_ref)
            dbs_ref[...] = jnp.zeros_like(dbs_ref)

        dus_ref[...] = lax.dot_general(dz_ref[...].astype(_MXU), wo_ref[...].astype(_MXU), (((1,), (1,)), ((), ())),
                                       preferred_element_type=F32)
        z, gp = _gelu_parts(p_ref[...].astype(F32))
        u, _ = _split_uv(z, il)
        xh = xh_ref[...]
        gam = g_ref[...]
        vn_ref[...] = (xh * gam + b_ref[...]).astype(_MXU)
        u_ref[...] = u
        mask = _tril_mask(L)
        for gi in range(G):
            wc = jnp.where(mask, ws_ref[gi], 0.0).astype(_MXU)
            cols = slice(gi * L, (gi + 1) * L)
            for c in range(tm // L):
                rows = slice(c * L, (c + 1) * L)
                vnb = vn_ref[rows, cols]
                s = jnp.dot(wc, vnb, preferred_element_type=F32) + bs_ref[:, cols]
                d = dus_ref[rows, cols]
                du_ref[rows, cols] = d * s
                ds = d * u_ref[rows, cols]
                dbs_ref[:, cols] += ds
                dsb = ds.astype(_MXU)
                dw = lax.dot_general(dsb, vnb, (((1,), (1,)), ((), ())), preferred_element_type=F32)
                dws_ref[gi] += jnp.where(mask, dw, 0.0)
                dvn_ref[rows, cols] = lax.dot_general(wc, dsb, (((0,), (0,)), ((), ())), preferred_element_type=F32)
        dvn = dvn_ref[...]
        dg_ref[...] += _fold8(dvn * xh)
        db_ref[...] += _fold8(dvn)
        dv = _ln_bwd_rows(dvn, xh, rs_ref[...], gam)
        du = du_ref[...]
        for hb in range(2):
            for part, src in ((0, du), (1, dv)):
                lo = (2 * hb + part) * il
                dp = src[:, hb * il:(hb + 1) * il] * gp[:, lo:lo + il]
                dp_ref[:, lo:lo + il] = dp.astype(_MXU)
                cs_ref[:, lo:lo + il] += _fold8(dp)

    row = lambda i: (i, 0)
    fixed = lambda i: (0, 0)
    part_c = pl.BlockSpec((SUBLANES, C), fixed)
    return pl.pallas_call(
        body, name=name, grid=(T // tm,),
        in_specs=[pl.BlockSpec((tm, D), row), _resident((None, C, D), lambda i: (0, 0, 0)),
                  pl.BlockSpec((tm, C2), row), pl.BlockSpec((tm, C), row),
                  pl.BlockSpec((tm, 1), row), pl.BlockSpec((1, C), fixed), pl.BlockSpec((1, C), fixed),
                  pl.BlockSpec((G, L, L), lambda i: (0, 0, 0)), pl.BlockSpec((L, C), fixed)],
        out_specs=[pl.BlockSpec((tm, C2), row), part_c, part_c, pl.BlockSpec((SUBLANES, C2), fixed),
                   pl.BlockSpec((G, L, L), lambda i: (0, 0, 0)), pl.BlockSpec((L, C), fixed)],
        out_shape=[jax.ShapeDtypeStruct((T, C2), _MXU), jax.ShapeDtypeStruct((SUBLANES, C), F32),
                   jax.ShapeDtypeStruct((SUBLANES, C), F32), jax.ShapeDtypeStruct((SUBLANES, C2), F32),
                   jax.ShapeDtypeStruct((G, L, L), F32), jax.ShapeDtypeStruct((L, C), F32)],
        scratch_shapes=[pltpu.VMEM((tm, C), _MXU), pltpu.VMEM((tm, C), F32), pltpu.VMEM((tm, C), F32),
                        pltpu.VMEM((tm, C), F32), pltpu.VMEM((tm, C), F32)],
        compiler_params=_cp(("arbitrary",)),
    )(dzb, w_out, p, xh, rstd, g, b, w_s, bsb)


def _ffn_conv(h, prev8, w_ref, b_ref):
    h1 = _shift_down(prev8, h, 1)
    h2 = _shift_down(prev8, h, 2)
    return w_ref[pl.ds(2, 1), :] * h + w_ref[pl.ds(1, 1), :] * h1 + w_ref[pl.ds(0, 1), :] * h2 + b_ref[...]


def _resident(block, imap):
    return pl.BlockSpec(block, imap, pipeline_mode=pl.Buffered(1))


def _ffn_fwd_half(j, xb, w_up, w_down, b_up, w_dw, b_dw, *, S, name, prev=None, tail=None, head=None):
    T, D = xb.shape
    N = w_up.shape[-1]
    tn = N // N_CHIPS
    tm = _tile(S, 256)
    spt = S // tm
    last = prev is not None
    alpha = tail[1] if last else None

    def body(*refs):
        x_ref, wu_ref, wd_ref, bu_ref, wc_ref, bc_ref = refs[:6]
        if last:
            yp_ref, res_ref, bd_ref, g_ref, b_ref = refs[9:14]
            o = 14 if head is None else 15
            h_ref, hc_ref, f_ref, y_ref, yb_ref, xh_ref, rs_ref = refs[o:o + 7]
            carry_ref = refs[-1]
        else:
            h_ref, hc_ref, f_ref, yp_ref, carry_ref = refs[6:11]

        @pl.when(pl.program_id(0) % spt == 0)
        def _():
            carry_ref[...] = jnp.zeros_like(carry_ref)

        h = jnp.dot(x_ref[...].astype(_MXU), wu_ref[...].astype(_MXU), preferred_element_type=F32) + bu_ref[...]
        h_ref[...] = h.astype(_HDT)
        hc = _ffn_conv(h, carry_ref[...], wc_ref, bc_ref)
        hc_ref[...] = hc.astype(_HDT)
        carry_ref[...] = h[tm - SUBLANES:tm]
        gte = hc[:, :tn]
        f = (gte * _sigmoid(gte) * hc[:, tn:]).astype(_MXU)
        f_ref[...] = f
        y = jnp.dot(f, wd_ref[...].astype(_MXU), preferred_element_type=F32)
        if not last:
            yp_ref[...] = y
            return
        z = y + yp_ref[...] + bd_ref[...] + alpha * res_ref[...]
        out, xh, rstd = _ln_rows(z, g_ref[...], b_ref[...])
        if head is None:
            y_ref[...] = out
            yb_ref[...] = out.astype(_MXU)
            xh_ref[...] = xh
            rs_ref[...] = rstd
            return
        t_ref, cs_ref, ls_ref = refs[14], refs[o + 7], refs[o + 8]

        @pl.when(pl.program_id(0) == 0)
        def _():
            for acc in (xh_ref, rs_ref, cs_ref, ls_ref):
                acc[...] = jnp.zeros_like(acc)

        err = out - t_ref[...]
        d = err * (1.0 / D)
        dz = _ln_bwd_rows(d, xh, rstd, g_ref[...])
        y_ref[...] = dz
        yb_ref[...] = dz.astype(_MXU)
        xh_ref[...] += _fold8(d * xh)
        rs_ref[...] += _fold8(d)
        cs_ref[...] += _fold8(dz)
        ls_ref[...] += _fold8(err * err)

    row = lambda i: (i, 0)
    pair = lambda i: (0, j)
    vec = pl.BlockSpec((1, D), lambda i: (0, 0))
    tile = pl.BlockSpec((tm, D), row)
    in_specs = [tile, _resident((None, D, 2 * tn), lambda i: (0, 0, j)), _resident((None, tn, D), lambda i: (0, j, 0)),
                pl.BlockSpec((1, 2 * tn), pair), pl.BlockSpec((SUBLANES, 2 * tn), pair), pl.BlockSpec((1, 2 * tn), pair)]
    operands = [xb, w_up, w_down, b_up, w_dw, b_dw]
    wide = pl.BlockSpec((tm, 2 * tn), lambda i: (i, j))
    out_specs = [wide, wide, pl.BlockSpec((tm, tn), lambda i: (i, j))]
    out_shape = [jax.ShapeDtypeStruct((T, N), _HDT), jax.ShapeDtypeStruct((T, N), _HDT),
                 jax.ShapeDtypeStruct((T, N // 2), _MXU)]
    aliases = {}
    if last:
        res, _, b_down, g, b = tail
        in_specs += [ANY, ANY, ANY, tile, tile, vec, vec, vec]
        operands += list(prev) + [res, b_down, g, b]
        aliases = {6: 0, 7: 1, 8: 2}
        if head is None:
            out_specs += [tile, tile, tile, pl.BlockSpec((tm, 1), row)]
            out_shape += [jax.ShapeDtypeStruct((T, D), F32), jax.ShapeDtypeStruct((T, D), _MXU),
                          jax.ShapeDtypeStruct((T, D), F32), jax.ShapeDtypeStruct((T, 1), F32)]
        else:
            in_specs.append(tile)
            operands.append(head)
            part = pl.BlockSpec((SUBLANES, D), lambda i: (0, 0))
            out_specs += [tile, tile, part, part, part, part]
            out_shape += [jax.ShapeDtypeStruct((T, D), F32), jax.ShapeDtypeStruct((T, D), _MXU)] \
                + [jax.ShapeDtypeStruct((SUBLANES, D), F32)] * 4
    else:
        out_specs.append(tile)
        out_shape.append(jax.ShapeDtypeStruct((T, D), F32))
    return pl.pallas_call(
        body, name=name, grid=(T // tm,), in_specs=in_specs, out_specs=out_specs, out_shape=out_shape,
        input_output_aliases=aliases, scratch_shapes=[pltpu.VMEM((SUBLANES, 2 * tn), F32)],
        compiler_params=_cp(("arbitrary",)),
    )(*operands)


def _ffn_bwd_half(j, dzb, w_down, w_up, hs, hcs, w_dw, *, S, name, dz=None, alpha=None, prev=None, ln=None):
    T, D = dzb.shape
    N = hs.shape[1]
    tn = N // N_CHIPS
    tm = _tile(S, 256)
    spt = S // tm
    nt = T // tm
    last = prev is not None

    def body(*refs):
        dz_ref, wd_ref, wu_ref, h_ref, hc_ref, wc_ref = refs[:6]
        if last:
            dxp_ref, xh_ref, rs_ref, g_ref = refs[7:11]
            dh_ref, cs_ref, dw_ref, db_ref, dz1_ref, dz1b_ref, dg1_ref, db1_ref, cs1_ref, carry_ref = refs[11:21]
        else:
            dzf_ref = refs[6]
            dh_ref, cs_ref, dw_ref, db_ref, dxp_ref, carry_ref = refs[7:13]
        i = pl.program_id(0)
        ii = nt - 1 - i

        @pl.when(i == 0)
        def _():
            cs_ref[...] = jnp.zeros_like(cs_ref)
            dw_ref[...] = jnp.zeros_like(dw_ref)
            db_ref[...] = jnp.zeros_like(db_ref)
            if last:
                dg1_ref[...] = jnp.zeros_like(dg1_ref)
                db1_ref[...] = jnp.zeros_like(db1_ref)
                cs1_ref[...] = jnp.zeros_like(cs1_ref)

        df = lax.dot_general(dz_ref[...].astype(_MXU), wd_ref[...].astype(_MXU), (((1,), (1,)), ((), ())),
                             preferred_element_type=F32)
        h = h_ref[...].astype(F32)
        gte, val = hc_ref[:, :tn].astype(F32), hc_ref[:, tn:].astype(F32)
        sig = _sigmoid(gte)
        dval = df * (gte * sig)
        dg = df * val * (sig * (1.0 + gte * (1.0 - sig)))
        dhc = jnp.concatenate([dg, dval], axis=1)
        nxt = jnp.where((ii + 1) % spt == 0, 0.0, carry_ref[...])
        d1 = _shift_up(dhc, nxt, 1)
        d2 = _shift_up(dhc, nxt, 2)
        carry_ref[...] = dhc[0:SUBLANES]
        db_ref[...] += _fold8(dhc)
        dw_ref[2] += _fold8(dhc * h)
        dw_ref[1] += _fold8(d1 * h)
        dw_ref[0] += _fold8(d2 * h)
        dh = wc_ref[pl.ds(2, 1), :] * dhc + wc_ref[pl.ds(1, 1), :] * d1 + wc_ref[pl.ds(0, 1), :] * d2
        cs_ref[...] += _fold8(dh)
        dhb = dh.astype(_MXU)
        dh_ref[...] = dhb
        dx = lax.dot_general(dhb, wu_ref[...].astype(_MXU), (((1,), (1,)), ((), ())), preferred_element_type=F32)
        if not last:
            dxp_ref[...] = dx + alpha * dzf_ref[...]
            return
        d = dx + dxp_ref[...]
        xh = xh_ref[...]
        dz1 = _ln_bwd_rows(d, xh, rs_ref[...], g_ref[...])
        dz1_ref[...] = dz1
        dz1b_ref[...] = dz1.astype(_MXU)
        dg1_ref[...] += _fold8(d * xh)
        db1_ref[...] += _fold8(d)
        cs1_ref[...] += _fold8(dz1)

    rev = lambda i: (nt - 1 - i, 0)
    fixed = lambda i: (0, 0)
    pair = lambda i: (0, j)
    tile = pl.BlockSpec((tm, D), rev)
    wide = pl.BlockSpec((tm, 2 * tn), lambda i: (nt - 1 - i, j))
    part = pl.BlockSpec((SUBLANES, 2 * tn), fixed)
    in_specs = [tile, _resident((None, tn, D), lambda i: (0, j, 0)), _resident((None, D, 2 * tn), lambda i: (0, 0, j)),
                wide, wide, pl.BlockSpec((SUBLANES, 2 * tn), pair)]
    operands = [dzb, w_down, w_up, hs, hcs, w_dw]
    out_specs = [wide, part, pl.BlockSpec((3, SUBLANES, 2 * tn), lambda i: (0, 0, 0)), part]
    out_shape = [jax.ShapeDtypeStruct((T, N), _MXU), jax.ShapeDtypeStruct((SUBLANES, 2 * tn), F32),
                 jax.ShapeDtypeStruct((3, SUBLANES, 2 * tn), F32), jax.ShapeDtypeStruct((SUBLANES, 2 * tn), F32)]
    aliases = {}
    if last:
        xh, rstd, g = ln
        in_specs += [ANY, tile, tile, pl.BlockSpec((tm, 1), rev), pl.BlockSpec((1, D), fixed)]
        operands += [prev[0], prev[1], xh, rstd, g]
        aliases = {6: 0}
        out_specs += [tile, tile] + [pl.BlockSpec((SUBLANES, D), fixed)] * 3
        out_shape += [jax.ShapeDtypeStruct((T, D), F32), jax.ShapeDtypeStruct((T, D), _MXU)] \
            + [jax.ShapeDtypeStruct((SUBLANES, D), F32)] * 3
    else:
        in_specs.append(tile)
        operands.append(dz)
        out_specs.append(tile)
        out_shape.append(jax.ShapeDtypeStruct((T, D), F32))
    return pl.pallas_call(
        body, name=name, grid=(nt,), in_specs=in_specs, out_specs=out_specs, out_shape=out_shape,
        input_output_aliases=aliases, scratch_shapes=[pltpu.VMEM((SUBLANES, 2 * tn), F32)],
        compiler_params=_cp(("arbitrary",)),
    )(*operands)


def _sum_pieces(gs, rs, me, *, name):
    n = len(gs)
    _, pr, pc = gs[0].shape
    tr = _tile(pr, 128)

    def body(me_ref, *refs):
        o_ref = refs[2 * n]
        for l in range(n):
            total = refs[l][...].astype(F32)
            for s in range(N_DEV - 1):
                total = total + refs[n + l][s].astype(F32)
            o_ref[l] = total

    own = pl.BlockSpec((None, tr, pc), lambda i, me_ref: (me_ref[0], i, 0))
    got = pl.BlockSpec((N_DEV - 1, tr, pc), lambda i, me_ref: (0, i, 0))
    return pl.pallas_call(
        body, name=name,
        grid_spec=pltpu.PrefetchScalarGridSpec(
            num_scalar_prefetch=1, grid=(pr // tr,), in_specs=[own] * n + [got] * n,
            out_specs=pl.BlockSpec((n, tr, pc), lambda i, me_ref: (0, i, 0))),
        out_shape=jax.ShapeDtypeStruct((n, pr, pc), F32),
        compiler_params=_cp(("parallel",)),
    )(me, *gs, *rs)


def _adam_math(w, g, m, v):
    bc1 = 1.0 - ADAM_B1 ** ADAM_STEP
    bc2 = 1.0 - ADAM_B2 ** ADAM_STEP
    m = ADAM_B1 * m + (1.0 - ADAM_B1) * g
    v = ADAM_B2 * v + (1.0 - ADAM_B2) * (g * g)
    return -ADAM_LR * ((m / bc1) / (jnp.sqrt(v / bc2) + ADAM_EPS) + ADAM_WD * w), m, v


def _adam(w, g, m, v, *, name):
    R, C = w.shape
    tr = _tile(R, 256)

    def body(w_ref, g_ref, m_ref, v_ref, d_ref, mo_ref, vo_ref):
        d_ref[...], mo_ref[...], vo_ref[...] = _adam_math(w_ref[...], g_ref[...], m_ref[...], v_ref[...])

    spec = pl.BlockSpec((tr, C), lambda i: (i, 0))
    return pl.pallas_call(
        body, name=name, grid=(R // tr,), in_specs=[spec] * 4, out_specs=[spec] * 3,
        out_shape=[jax.ShapeDtypeStruct((R, C), F32)] * 3,
        compiler_params=_cp(("parallel",)),
    )(w, g, m, v)


def _adam_halves(w, own, got, m, v, core, *, name):
    L, R, C = w.shape
    rh = R // 2
    tr = _tile(rh, 256)
    nt = rh // tr

    def body(c_ref, w_ref, own_ref, got_ref, m_ref, v_ref, g_ref, d_ref, mo_ref, vo_ref):
        g = jnp.where(pl.program_id(1) == c_ref[0], own_ref[...], got_ref[...])
        g_ref[...] = g
        d_ref[...], mo_ref[...], vo_ref[...] = _adam_math(w_ref[...], g, m_ref[...], v_ref[...])

    full = pl.BlockSpec((None, tr, C), lambda l, h, t, c_ref: (l, h * nt + t, 0))
    half = pl.BlockSpec((None, tr, C), lambda l, h, t, c_ref: (l, t, 0))
    return pl.pallas_call(
        body, name=name,
        grid_spec=pltpu.PrefetchScalarGridSpec(
            num_scalar_prefetch=1, grid=(L, 2, nt), in_specs=[full, half, half, full, full], out_specs=[full] * 4),
        out_shape=[jax.ShapeDtypeStruct((L, R, C), F32)] * 4,
        compiler_params=_cp(("parallel", "parallel", "parallel")),
    )(core, w, own, got, m, v)


def _remote(src, dst, send, recv, dev):
    return pltpu.make_async_remote_copy(src_ref=src, dst_ref=dst, send_sem=send, recv_sem=recv,
                                        device_id=dev, device_id_type=MESH)


def _place_w(shard, pos, layer, *, axis, name):
    _, R, C = shard.shape
    tr = _tile(R, 512, 16)
    nt = R // tr
    if axis == 2:
        out_shape = (1, R, N_CHIPS * C)
        out_map = lambda t, q: (0, t, q[0])
    else:
        out_shape = (1, N_CHIPS * R, C)
        out_map = lambda t, q: (0, q[0] * nt + t, 0)

    def body(q_ref, s_ref, o_ref):
        o_ref[...] = s_ref[...].astype(_WIRE)

    return pl.pallas_call(
        body, name=name,
        grid_spec=pltpu.PrefetchScalarGridSpec(
            num_scalar_prefetch=1, grid=(nt,),
            in_specs=[pl.BlockSpec((None, tr, C), lambda t, q: (layer, t, 0))],
            out_specs=pl.BlockSpec((None, tr, C), out_map)),
        out_shape=jax.ShapeDtypeStruct(out_shape, _WIRE),
        compiler_params=_cp(("parallel",)),
    )(pos, shard)


def _ag_window(ref, kind, px, py, h):
    axis, perm = kind
    q = 2 * px + py
    if perm:
        q = _perm_idx(q)
    if axis == 2:
        R, C = ref.shape[1], ref.shape[2] // N_CHIPS
        rh = R // 2
        return ref.at[:, pl.ds(pl.multiple_of(h * rh, 16), rh), pl.ds(pl.multiple_of(q * C, LANES), C)]
    R = ref.shape[1] // N_CHIPS
    rh = R // 2
    return ref.at[:, pl.ds(pl.multiple_of(q * R + h * rh, 16), rh), :]


def _ag_ici_copies(refs, kinds, send, recv):
    x, y, c = lax.axis_index("x"), lax.axis_index("y"), lax.axis_index("c")
    chips = [(1 - x, y), (x, 1 - y), (1 - x, 1 - y)]
    sends, recvs = [], []
    for a, (ref, kind) in enumerate(zip(refs, kinds)):
        own = _ag_window(ref, kind, x, y, c)
        for i, (px, py) in enumerate(chips):
            k = 3 * a + i
            sends.append(_remote(own, own, send.at[k], recv.at[k], (px, py, c)))
            recvs.append(_remote(own, _ag_window(ref, kind, px, py, c), send.at[k], recv.at[k], (px, py, c)))
    return sends, recvs


def _ag_start(arrs, kinds, after, *, name):
    n = len(arrs)

    def body(*refs):
        in_refs = refs[:n]
        send, recv = refs[n + len(after)], refs[n + len(after) + 1]
        token = refs[-1]
        sends, _ = _ag_ici_copies(in_refs, kinds, send, recv)
        for cp in sends:
            cp.start()
        token[...] = jnp.zeros_like(token)

    sems = pltpu.SemaphoreType.DMA((3 * n,))
    out = pl.pallas_call(
        body, name=name,
        out_shape=(sems, sems) + tuple(pltpu.HBM(a.shape, a.dtype) for a in arrs)
        + (jax.ShapeDtypeStruct((SUBLANES, LANES), F32),),
        in_specs=(HBM,) * n + (ANY,) * len(after),
        out_specs=(SEMS, SEMS) + (HBM,) * n + (pl.BlockSpec(memory_space=pltpu.VMEM),),
        input_output_aliases={a: 2 + a for a in range(n)},
        compiler_params=pltpu.CompilerParams(has_side_effects=EFFECT),
    )(*[pltpu.with_memory_space_constraint(a, pltpu.HBM) for a in arrs], *after)
    return out[0], out[1], list(out[2:2 + n]), out[-1]


def _ag_wait(send, recv, arrs, kinds, after, *, name):
    n = len(arrs)

    def body(*refs):
        in_refs = refs[:n]
        send, recv = refs[n], refs[n + 1]
        sends, recvs = _ag_ici_copies(in_refs, kinds, send, recv)
        for cp in sends:
            cp.wait_send()
        for cp in recvs:
            cp.wait_recv()

    out = pl.pallas_call(
        body, name=name,
        out_shape=tuple(pltpu.HBM(a.shape, a.dtype) for a in arrs),
        in_specs=(HBM,) * n + (SEMS, SEMS) + (ANY,) * len(after), out_specs=(HBM,) * n,
        input_output_aliases={a: a for a in range(n)},
        compiler_params=pltpu.CompilerParams(has_side_effects=EFFECT),
    )(*arrs, send, recv, *after)
    return list(out)


def _ag_forward(arrs, kinds, *, name):
    n = len(arrs)

    def body(*refs):
        o_refs, send, recv = refs[n:2 * n], refs[2 * n], refs[2 * n + 1]
        x, y, c = lax.axis_index("x"), lax.axis_index("y"), lax.axis_index("c")
        chips = [(1 - x, y), (x, 1 - y), (1 - x, 1 - y)]
        sib = (x, y, 1 - c)
        sends, recvs = [], []
        for a, (ref, kind) in enumerate(zip(o_refs, kinds)):
            for i, (px, py) in enumerate(chips):
                k = 3 * a + i
                got = _ag_window(ref, kind, px, py, c)
                cp = _remote(got, got, send.at[k], recv.at[k], sib)
                cp.start()
                sends.append(cp)
                recvs.append(_remote(got, _ag_window(ref, kind, px, py, 1 - c), send.at[k], recv.at[k], sib))
        for cp in recvs:
            cp.wait_recv()
        for cp in sends:
            cp.wait_send()

    out = pl.pallas_call(
        body, name=name, in_specs=[ANY] * n, out_specs=[ANY] * n,
        out_shape=[jax.ShapeDtypeStruct(a.shape, a.dtype) for a in arrs],
        input_output_aliases={a: a for a in range(n)},
        scratch_shapes=[pltpu.SemaphoreType.DMA((3 * n,)), pltpu.SemaphoreType.DMA((3 * n,))],
    )(*arrs)
    return list(out)


def _flip(x, y, c, f):
    return ((1 - x) if f & 4 else x, (1 - y) if f & 2 else y, (1 - c) if f & 1 else c)


def _rs_copies(g_refs, land_refs, send, recv):
    x, y, c = lax.axis_index("x"), lax.axis_index("y"), lax.axis_index("c")
    cps = []
    for a, (g_ref, land_ref) in enumerate(zip(g_refs, land_refs)):
        for f in range(1, N_DEV):
            tx, ty, tcx = _flip(x, y, c, f)
            k = (N_DEV - 1) * a + f - 1
            cps.append(_remote(g_ref.at[4 * tx + 2 * ty + tcx], land_ref.at[f - 1], send.at[k], recv.at[k],
                               (tx, ty, tcx)))
    return cps


def _rs_start(gs, *, name):
    n = len(gs)
    lands = [lax.empty((N_DEV - 1,) + g.shape[1:], g.dtype) for g in gs]

    def body(*refs):
        send, recv, token = refs[2 * n], refs[2 * n + 1], refs[-1]
        for cp in _rs_copies(refs[:n], refs[n:2 * n], send, recv):
            cp.start()
        token[...] = jnp.zeros_like(token)

    sems = pltpu.SemaphoreType.DMA(((N_DEV - 1) * n,))
    thru = [pltpu.HBM(t.shape, t.dtype) for t in gs + lands]
    out = pl.pallas_call(
        body, name=name,
        out_shape=(sems, sems, *thru, jax.ShapeDtypeStruct((SUBLANES, LANES), F32)),
        in_specs=(HBM,) * (2 * n), out_specs=(SEMS, SEMS) + (HBM,) * (2 * n) + (pl.BlockSpec(memory_space=pltpu.VMEM),),
        input_output_aliases={a: 2 + a for a in range(2 * n)},
        compiler_params=pltpu.CompilerParams(has_side_effects=EFFECT),
    )(*[pltpu.with_memory_space_constraint(t, pltpu.HBM) for t in gs + lands])
    return out[0], out[1], list(out[2:2 + n]), list(out[2 + n:2 + 2 * n]), out[-1]


def _rs_wait(send, recv, gs, lands, after, *, name):
    n = len(gs)

    def body(*refs):
        cps = _rs_copies(refs[:n], refs[n:2 * n], refs[2 * n], refs[2 * n + 1])
        for cp in cps:
            cp.wait_send()
        for cp in cps:
            cp.wait_recv()

    out = pl.pallas_call(
        body, name=name,
        out_shape=tuple(pltpu.HBM(t.shape, t.dtype) for t in gs + lands),
        in_specs=(HBM,) * (2 * n) + (SEMS, SEMS, ANY), out_specs=(HBM,) * (2 * n),
        input_output_aliases={a: a for a in range(2 * n)},
        compiler_params=pltpu.CompilerParams(has_side_effects=EFFECT),
    )(*gs, *lands, send, recv, after)
    return list(out[:n]), list(out[n:])


def _pair_exchange(owns, *, name):
    n = len(owns)

    def body(*refs):
        send, recv = refs[2 * n], refs[2 * n + 1]
        x, y, c = lax.axis_index("x"), lax.axis_index("y"), lax.axis_index("c")
        cps = [_remote(refs[a], refs[n + a], send.at[a], recv.at[a], (x, y, 1 - c)) for a in range(n)]
        for cp in cps:
            cp.start()
        for cp in cps:
            cp.wait_recv()
        for cp in cps:
            cp.wait_send()

    return pl.pallas_call(
        body, name=name, in_specs=[ANY] * n, out_specs=[ANY] * n,
        out_shape=[jax.ShapeDtypeStruct(o.shape, o.dtype) for o in owns],
        scratch_shapes=[pltpu.SemaphoreType.DMA((n,)), pltpu.SemaphoreType.DMA((n,))],
    )(*owns)


def _allreduce_flat(vec, *, name):
    n = vec.shape[0]
    unit = N_DEV * SUBLANES * LANES
    npad = -(-n // unit) * unit
    rows = npad // (N_DEV * LANES)
    xin = jnp.pad(vec, (0, npad - n)).reshape(N_DEV, rows, LANES)

    def body(x_ref, y_ref, a_ref, send_a, recv_a, send_b, recv_b):
        x, y, c = lax.axis_index("x"), lax.axis_index("y"), lax.axis_index("c")
        me = 4 * x + 2 * y + c
        a_ref[me] = x_ref[me]
        sends, recvs = [], []
        for f in range(1, N_DEV):
            dev = _flip(x, y, c, f)
            t = 4 * dev[0] + 2 * dev[1] + dev[2]
            cp = _remote(x_ref.at[t], a_ref.at[me], send_a.at[f - 1], recv_a.at[f - 1], dev)
            cp.start()
            sends.append(cp)
            recvs.append(_remote(x_ref.at[me], a_ref.at[t], send_a.at[f - 1], recv_a.at[f - 1], dev))
        for cp in recvs:
            cp.wait_recv()
        for cp in sends:
            cp.wait_send()
        acc = a_ref[0]
        for s in range(1, N_DEV):
            acc = acc + a_ref[s]
        y_ref[me] = acc
        sends, recvs = [], []
        for f in range(1, N_DEV):
            dev = _flip(x, y, c, f)
            t = 4 * dev[0] + 2 * dev[1] + dev[2]
            cp = _remote(y_ref.at[me], y_ref.at[me], send_b.at[f - 1], recv_b.at[f - 1], dev)
            cp.start()
            sends.append(cp)
            recvs.append(_remote(y_ref.at[me], y_ref.at[t], send_b.at[f - 1], recv_b.at[f - 1], dev))
        for cp in recvs:
            cp.wait_recv()
        for cp in sends:
            cp.wait_send()

    vm = pl.BlockSpec(memory_space=pltpu.VMEM)
    out = pl.pallas_call(
        body, name=name, in_specs=[vm], out_specs=vm,
        out_shape=jax.ShapeDtypeStruct((N_DEV, rows, LANES), F32),
        scratch_shapes=[pltpu.VMEM((N_DEV, rows, LANES), F32)] + [pltpu.SemaphoreType.DMA((N_DEV - 1,))] * 4,
        compiler_params=_cp(),
    )(xin)
    return out.reshape(npad)[:n]


def _perm_cols(v, blocks=N_CHIPS):
    lead, n = v.shape[:-1], v.shape[-1]
    return v.reshape(lead + (blocks, n // blocks))[..., PERM, :].reshape(lead + (n,))


def _pack(arrs):
    return jnp.concatenate([a.reshape(-1).astype(F32) for a in arrs])


def _unpack(flat, shapes):
    out, pos = [], 0
    for s in shapes:
        n = 1
        for d in s:
            n *= d
        out.append(flat[pos:pos + n].reshape(s))
        pos += n
    return out


def kernel(x, conv_w_in, conv_b_in, conv_w_dw, conv_b_dw, conv_ln_g, conv_ln_b, conv_w_out, conv_b_out, gmlp_w_in, gmlp_b_in, gmlp_ln_g, gmlp_ln_b, gmlp_w_s, gmlp_b_s, gmlp_w_out, gmlp_b_out, ffn_w_up, ffn_b_up, ffn_w_dw, ffn_b_dw, ffn_w_down, ffn_b_down, norm1_g, norm1_b, norm2_g, norm2_b, loss_target, m_conv_w_in, m_conv_b_in, m_conv_w_dw, m_conv_b_dw, m_conv_ln_g, m_conv_ln_b, m_conv_w_out, m_conv_b_out, m_gmlp_w_in, m_gmlp_b_in, m_gmlp_ln_g, m_gmlp_ln_b, m_gmlp_w_s, m_gmlp_b_s, m_gmlp_w_out, m_gmlp_b_out, m_ffn_w_up, m_ffn_b_up, m_ffn_w_dw, m_ffn_b_dw, m_ffn_w_down, m_ffn_b_down, m_norm1_g, m_norm1_b, m_norm2_g, m_norm2_b, v_conv_w_in, v_conv_b_in, v_conv_w_dw, v_conv_b_dw, v_conv_ln_g, v_conv_ln_b, v_conv_w_out, v_conv_b_out, v_gmlp_w_in, v_gmlp_b_in, v_gmlp_ln_g, v_gmlp_ln_b, v_gmlp_w_s, v_gmlp_b_s, v_gmlp_w_out, v_gmlp_b_out, v_ffn_w_up, v_ffn_b_up, v_ffn_w_dw, v_ffn_b_dw, v_ffn_w_down, v_ffn_b_down, v_norm1_g, v_norm1_b, v_norm2_g, v_norm2_b):
    P = dict(locals())
    WEIGHTS = ['conv_w_in', 'conv_b_in', 'conv_w_dw', 'conv_b_dw', 'conv_ln_g', 'conv_ln_b', 'conv_w_out',
               'conv_b_out', 'gmlp_w_in', 'gmlp_b_in', 'gmlp_ln_g', 'gmlp_ln_b', 'gmlp_w_s', 'gmlp_b_s',
               'gmlp_w_out', 'gmlp_b_out', 'ffn_w_up', 'ffn_b_up', 'ffn_w_dw', 'ffn_b_dw', 'ffn_w_down',
               'ffn_b_down', 'norm1_g', 'norm1_b', 'norm2_g', 'norm2_b']
    BIG = ['conv_w_in', 'conv_w_out', 'gmlp_w_in', 'gmlp_w_out', 'ffn_w_up', 'ffn_w_down']
    SMALL_SHARDED = {'conv_w_dw': 2, 'gmlp_b_in': 1, 'gmlp_ln_g': 1, 'gmlp_ln_b': 1, 'gmlp_b_out': 1, 'ffn_w_dw': 2}

    B, S, D = x.shape
    T = B * S
    depth = norm1_g.shape[0]
    alpha = (2.0 * depth) ** 0.25
    C = conv_w_out.shape[-1]
    F2 = ffn_b_up.shape[-1]
    G, L = gmlp_w_s.shape[1], gmlp_w_s.shape[2]
    xi, yi, ci = lax.axis_index("x"), lax.axis_index("y"), lax.axis_index("c")
    shard = 2 * xi + yi

    i32 = lambda v: jnp.reshape(v, (1,)).astype(jnp.int32)
    pos_plain, pos_perm = i32(shard), i32(_perm_idx(shard))
    me_id, core_id = i32(4 * xi + 2 * yi + ci), i32(ci)

    groups = []
    for i in range(depth):
        mix = 'conv' if i % 2 == 0 else 'gmlp'
        groups.append((f"{mix}{i // 2}", [(mix + '_w_in', i // 2, 2, True), (mix + '_w_out', i // 2, 1, False)]))
        groups.append((f"ffn{i}", [('ffn_w_up', i, 2, True), ('ffn_w_down', i, 1, False)]))
    sm_names = list(SMALL_SHARDED)
    sm_shapes = [P[n].shape for n in sm_names]
    mine = _pack([P[n] for n in sm_names]) * (ci == 0).astype(F32)
    buf = jnp.zeros((N_CHIPS, mine.shape[0]), F32)
    buf = lax.dynamic_update_slice(buf, mine[None], (shard, 0))
    gathered = _allreduce_flat(buf.reshape(-1), name="ag_small").reshape(N_CHIPS, -1)

    started, order = {}, [gathered]
    for gname, members in groups:
        placed = [_place_w(P[n], pos_perm if perm else pos_plain, l, axis=axis, name=f"place_{n}_{l}")
                  for n, l, axis, perm in members]
        kinds = [(axis, perm) for _, _, axis, perm in members]
        send, recv, arrs, token = _ag_start(placed, kinds, order, name=f"ag_start_{gname}")
        order = [token]
        started[gname] = (send, recv, arrs, kinds, [(n, l) for n, l, _, _ in members])
    wts = {}

    def arrive(gname, after):
        send, recv, arrs, kinds, keys = started[gname]
        arrs = _ag_wait(send, recv, arrs, kinds, after, name=f"ag_wait_{gname}")
        arrs = _ag_forward(arrs, kinds, name=f"ag_fwd_{gname}")
        wts.update(zip(keys, arrs))

    full = {}
    for n, parts in zip(sm_names, zip(*[_unpack(gathered[k], sm_shapes) for k in range(N_CHIPS)])):
        full[n] = jnp.concatenate(parts, axis=SMALL_SHARDED[n])
    for n in WEIGHTS:
        if n not in BIG and n not in full:
            full[n] = P[n]

    assert G * L == C, "a gMLP group must be as wide as a chunk is long"

    def row(v):
        return v.reshape(1, -1)

    def pad_rows(v, r):
        return jnp.pad(v, ((0, r - v.shape[0]), (0, 0)))

    xf = x.reshape(T, D)
    saved = []
    cur, cur_b = xf, xf.astype(_MXU)
    for i in range(depth):
        j = i // 2
        sv = {'x': cur, 'xb': cur_b}
        arrive(groups[2 * i][0], order if i == 0 else [cur_b])
        if i % 2 == 0:
            b_in = row(_perm_cols(full['conv_b_in'][j]))
            h1 = _mm(cur_b, wts['conv_w_in', j], bl=0, bias=b_in, tm=_tile(T, 512), tn=_tile(2 * C, 1024, LANES),
                     tk=D, name=f"conv_in_{j}", n_outer=True, out_dtype=_ADT)
            wdw = pad_rows(full['conv_w_dw'][j], CONV_TAPS_PAD)
            dwo = _conv_fwd(h1, wdw, row(full['conv_b_dw'][j]), B=B, S=S, name=f"conv_dw_{j}")
            s_act, xhc, rsc, *y1 = _conv_tail_fwd(
                dwo, row(full['conv_ln_g'][j]), row(full['conv_ln_b'][j]), wts['conv_w_out', j],
                row(full['conv_b_out'][j]), cur, alpha, row(norm1_g[i]), row(norm1_b[i]), name=f"conv_out_ln_{j}")
            sv.update(h1=h1, wdw=wdw, act=s_act, xhc=xhc, rsc=rsc)
        else:
            b_in = row(_perm_cols(full['gmlp_b_in'][j]))
            pre = _mm(cur_b, wts['gmlp_w_in', j], bl=0, bias=b_in, tm=_tile(T, 512), tn=_tile(2 * C, 1024, LANES),
                      tk=D, name=f"gmlp_in_{j}", n_outer=True, out_dtype=_ADT)
            bsb = jnp.repeat(gmlp_b_s[j].T, L, axis=1)
            us, xhv, rsv, *y1 = _gmlp_gate_fwd(
                pre, row(full['gmlp_ln_g'][j]), row(full['gmlp_ln_b'][j]), gmlp_w_s[j], bsb, wts['gmlp_w_out', j],
                row(full['gmlp_b_out'][j]), cur, alpha, row(norm1_g[i]), row(norm1_b[i]), name=f"gmlp_gate_{j}")
            sv.update(pre=pre, bsb=bsb, act=us, xhv=xhv, rsv=rsv)
        x1, x1b, xh1, rs1 = y1
        arrive(groups[2 * i + 1][0], [x1b])
        wdw3 = pad_rows(_perm_cols(full['ffn_w_dw'][i]), SUBLANES)
        bdw3 = row(_perm_cols(ffn_b_dw[i]))
        ffn_in = (x1b, wts['ffn_w_up', i], wts['ffn_w_down', i], row(_perm_cols(ffn_b_up[i])), wdw3, bdw3)
        first = _ffn_fwd_half(0, *ffn_in, S=S, name=f"ffn_fwd_a_{i}")
        ffn_tail = (x1, alpha, row(ffn_b_down[i]), row(norm2_g[i]), row(norm2_b[i]))
        sv.update(x1=x1, x1b=x1b, xh1=xh1, rs1=rs1, wdw3=wdw3)
        if i < depth - 1:
            hs, hcs, f_act, cur, cur_b, xh2, rs2 = _ffn_fwd_half(1, *ffn_in, S=S, name=f"ffn_fwd_b_{i}", prev=first,
                                                                 tail=ffn_tail)
            sv.update(xh2=xh2, rs2=rs2)
        else:
            hs, hcs, f_act, *sv['head'] = _ffn_fwd_half(1, *ffn_in, S=S, name=f"ffn_fwd_b_{i}", prev=first,
                                                         tail=ffn_tail, head=loss_target.reshape(T, D))
        sv.update(hs=hs, hcs=hcs, f=f_act)
        saved.append(sv)

    sg = {n: [None] * full[n].shape[0] for n in WEIGHTS if n not in BIG}
    inflight = {n: [None] * P[n].shape[0] for n in BIG}
    deps = []
    dcur = None
    loss_part = None
    tk_t = _tile(T, 2048)

    ready = []

    def wgrad(n, l, a_, b_, **kw):
        ready.append((n, l, _mm(a_, b_, ta=True, out_dtype=_WIRE, tk=tk_t, name=f"{n}_dw_{l}", deps=deps, **kw)))
        launch(f"{n}_{l}")

    def launch(gname):
        send, recv, gs, lands, token = _rs_start([g for _, _, g in ready], name=f"rs_start_{gname}")
        group = {'name': gname, 'flight': (send, recv, gs, lands), 'landed': None}
        for a, (n, l, _) in enumerate(ready):
            inflight[n][l] = (group, a)
        del ready[:]
        deps.append(token)

    def landed(n, l):
        group, a = inflight[n][l]
        if group['landed'] is None:
            group['landed'] = _rs_wait(*group['flight'], dcur, name=f"rs_wait_{group['name']}")
        return group['landed'][0][a], group['landed'][1][a]

    for i in reversed(range(depth)):
        j = i // 2
        sv = saved[i]
        if i == depth - 1:
            dz2, dz2b, dg, db, cs, loss_part = sv['head']
        else:
            dz2, dz2b, dg, db, cs = dcur
        sg['norm2_g'][i], sg['norm2_b'][i], sg['ffn_b_down'][i] = dg.sum(0), db.sum(0), cs.sum(0)
        Fh = F2 // 2
        wgrad('ffn_w_down', i, sv['f'], dz2b, tm=Fh // 2, tn=_tile(D, 1024, LANES), pieces=('row',))
        ffn_in = (dz2b, wts['ffn_w_down', i], wts['ffn_w_up', i], sv['hs'], sv['hcs'], sv['wdw3'])
        dh0, csu0, dwd0, dbd0, dxp = _ffn_bwd_half(0, *ffn_in, S=S, name=f"ffn_bwd_a_{i}", dz=dz2, alpha=alpha)
        dh, csu1, dwd1, dbd1, dz1, dz1b, dg, db, cs = _ffn_bwd_half(
            1, *ffn_in, S=S, name=f"ffn_bwd_b_{i}", prev=(dh0, dxp), ln=(sv['xh1'], sv['rs1'], row(norm1_g[i])))
        sg['ffn_b_up'][i] = _perm_cols(jnp.concatenate([csu0.sum(0), csu1.sum(0)], axis=-1))
        sg['ffn_w_dw'][i] = _perm_cols(jnp.concatenate([dwd0.sum(1), dwd1.sum(1)], axis=-1))
        sg['ffn_b_dw'][i] = _perm_cols(jnp.concatenate([dbd0.sum(0), dbd1.sum(0)], axis=-1))
        wgrad('ffn_w_up', i, sv['x1b'], dh, tm=D, tn=F2 // N_CHIPS, pieces=('col', True))
        sg['norm1_g'][i], sg['norm1_b'][i] = dg.sum(0), db.sum(0)
        if i % 2 == 0:
            sg['conv_b_out'][j] = cs.sum(0)
            wgrad('conv_w_out', j, sv['act'], dz1b, tm=_tile(C, 1024), tn=_tile(D, 1024, LANES), pieces=('row',))
            ddw, dg, db = _ln_silu_bwd(dz1b, wts['conv_w_out', j], sv['xhc'], sv['rsc'], row(full['conv_ln_g'][j]),
                                       row(full['conv_ln_b'][j]), name=f"conv_ln_bwd_{j}")
            sg['conv_ln_g'][j], sg['conv_ln_b'][j] = dg.sum(0), db.sum(0)
            dglu, dwk, dbk = _conv_bwd(ddw, sv['h1'], sv['wdw'], B=B, S=S, name=f"conv_dw_bwd_{j}")
            sg['conv_w_dw'][j] = dwk.sum(1)[:conv_w_dw.shape[1]]
            sg['conv_b_dw'][j] = dbk.sum(0)
            dh1, csi = _glu_bwd(dglu, sv['h1'], name=f"conv_glu_bwd_{j}")
            sg['conv_b_in'][j] = _perm_cols(csi.sum(0))
            fam = 'conv_w_in'
        else:
            sg['gmlp_b_out'][j] = cs.sum(0)
            wgrad('gmlp_w_out', j, sv['act'], dz1b, tm=_tile(C, 1024), tn=_tile(D, 1024, LANES), pieces=('row',))
            dh1, dg, db, csi, dws, dbs = _gmlp_gate_bwd(dz1b, wts['gmlp_w_out', j], sv['pre'], sv['xhv'], sv['rsv'],
                                                        row(full['gmlp_ln_g'][j]), row(full['gmlp_ln_b'][j]),
                                                        gmlp_w_s[j], sv['bsb'], name=f"gmlp_gate_bwd_{j}")
            sg['gmlp_ln_g'][j], sg['gmlp_ln_b'][j] = dg.sum(0), db.sum(0)
            sg['gmlp_b_in'][j] = _perm_cols(csi.sum(0))
            sg['gmlp_w_s'][j] = dws
            sg['gmlp_b_s'][j] = dbs.reshape(L, G, L).sum(-1).T
            fam = 'gmlp_w_in'
        wgrad(fam, j, sv['xb'], dh1, tm=D, tn=(2 * C) // N_CHIPS, pieces=('col', True))
        if i > 0:
            below = saved[i - 1]
            dcur = _mm_ln_bwd(dh1, wts[fam, j], dz1, alpha, below['xh2'], below['rs2'], row(norm2_g[i - 1]),
                              name=f"{fam}_dx_{j}", deps=deps)
        else:
            dcur = _mm(dh1, wts[fam, j], bl=0, tb=True, res=dz1, res_scale=alpha, tm=_tile(T, 512),
                       tn=_tile(D, 1024, LANES), tk=2 * C, name=f"{fam}_dx_{j}", deps=deps)
    grad_x = dcur.reshape(B, S, D)

    small_names = [n for n in WEIGHTS if n not in BIG]
    small_full = [jnp.stack(sg[n]) for n in small_names]
    flat = _pack(small_full + [loss_part])
    red = _allreduce_flat(flat, name="ar_small")
    red_parts = _unpack(red, [a.shape for a in small_full] + [loss_part.shape])
    loss = (0.5 / D) * jnp.sum(red_parts[-1])
    grads = {}
    for n, g in zip(small_names, red_parts[:-1]):
        if n in SMALL_SHARDED:
            ax = SMALL_SHARDED[n]
            width = P[n].shape[ax]
            g = lax.dynamic_slice_in_dim(g, shard * width, width, axis=ax)
        grads[n] = g

    big_out = {}
    for n in ['ffn_w_down', 'ffn_w_up', 'gmlp_w_out', 'gmlp_w_in', 'conv_w_out', 'conv_w_in']:
        both = [landed(n, l) for l in range(len(inflight[n]))]
        own = _sum_pieces([g for g, _ in both], [r for _, r in both], me_id, name=f"sum_{n}")
        got, = _pair_exchange([own], name=f"px_{n}")
        big_out[n] = _adam_halves(P[n], own, got, P['m_' + n], P['v_' + n], core_id, name=f"adam_{n}")

    shapes = [P[n].shape for n in small_names]
    n_small = sum(functools.reduce(lambda p_, d_: p_ * d_, s_, 1) for s_ in shapes)
    unit = SUBLANES * LANES
    npad = -(-n_small // unit) * unit

    def flat2d(arrs, fill=0.0):
        v = _pack(arrs)
        return jnp.pad(v, (0, npad - n_small), constant_values=fill).reshape(-1, LANES)

    dl, mo, vo = _adam(flat2d([P[n] for n in small_names]), flat2d([grads[n] for n in small_names]),
                       flat2d([P['m_' + n] for n in small_names]),
                       flat2d([P['v_' + n] for n in small_names], fill=1.0), name="adam_small")
    small_out = {n: [grads[n], None, None, None] for n in small_names}
    for k, t in enumerate((dl, mo, vo)):
        for n, a in zip(small_names, _unpack(t.reshape(-1), shapes)):
            small_out[n][k + 1] = a

    outs = [loss, grad_x]
    for k in range(4):
        for n in WEIGHTS:
            outs.append(big_out[n][k] if n in BIG else small_out[n][k])
    return tuple(outs)
```

```python
import functools

import jax
import jax.numpy as jnp
from jax import lax
from jax.experimental import pallas as pl
from jax.experimental.pallas import tpu as pltpu

F32 = jnp.float32
_MXU = jnp.bfloat16
_WIRE = jnp.bfloat16
_HDT = jnp.bfloat16
_ADT = jnp.bfloat16
LN_EPS = 1e-5
ADAM_LR, ADAM_B1, ADAM_B2, ADAM_EPS, ADAM_WD, ADAM_STEP = 0.001, 0.9, 0.999, 1e-08, 0.01, 10
N_CHIPS = 4
N_DEV = 8
LANES = 128
SUBLANES = 8
CONV_TAPS_PAD = 32
VMEM_LIMIT = 56 << 20
MESH = pl.DeviceIdType.MESH
ANY = pl.BlockSpec(memory_space=pl.ANY)
HBM = pl.BlockSpec(memory_space=pltpu.HBM)
SEMS = pl.BlockSpec(memory_space=pltpu.SEMAPHORE)
EFFECT = pltpu.SideEffectType.DATAFLOW_SIDE_EFFECTING
PERM = (0, 2, 1, 3)


def _cp(sem=None):
    return pltpu.CompilerParams(dimension_semantics=sem, vmem_limit_bytes=VMEM_LIMIT)


def _tile(dim, pref, mult=SUBLANES):
    if dim <= pref:
        return dim
    t = (pref // mult) * mult
    while t > mult and dim % t:
        t -= mult
    assert dim % t == 0, (dim, pref, mult)
    return t


def _perm_idx(q):
    return (q % 2) * 2 + q // 2


def _fold8(t):
    r, n = t.shape
    return t.reshape(r // SUBLANES, SUBLANES, n).sum(axis=0)


def _ln_rows(z, g, b):
    mu = jnp.mean(z, axis=-1, keepdims=True)
    xc = z - mu
    var = jnp.mean(xc * xc, axis=-1, keepdims=True)
    rstd = lax.rsqrt(var + LN_EPS)
    xh = xc * rstd
    return xh * g + b, xh, rstd


def _ln_bwd_rows(dy, xh, rstd, g):
    dxh = dy * g
    m1 = jnp.mean(dxh, axis=-1, keepdims=True)
    m2 = jnp.mean(dxh * xh, axis=-1, keepdims=True)
    return rstd * (dxh - m1 - xh * m2)


def _sigmoid(v):
    return 0.5 * jnp.tanh(0.5 * v) + 0.5


def _gelu_parts(p):
    cdf = 0.5 * (1.0 + lax.erf(p * 0.7071067811865476))
    pdf = jnp.exp(-0.5 * p * p) * 0.3989422804014327
    return p * cdf, cdf + p * pdf


def _shift_down(prev8, t, s):
    ext = jnp.concatenate([prev8, t], axis=0)
    return pltpu.roll(ext, s, 0)[SUBLANES:]


def _shift_up(t, next8, s):
    n = t.shape[0]
    ext = jnp.concatenate([t, next8], axis=0)
    return pltpu.roll(ext, n + SUBLANES - s, 0)[:n]


def _mm(a, b, *, ta=False, tb=False, bl=None, bias=None, res=None, res_scale=1.0, out_dtype=F32,
        tm, tn, tk, name, pieces=None, deps=None, n_outer=False):
    M, K = (a.shape[1], a.shape[0]) if ta else a.shape
    bs = b.shape[1:] if bl is not None else b.shape
    N, Kb = (bs[0], bs[1]) if tb else (bs[1], bs[0])
    assert K == Kb and M % tm == 0 and N % tn == 0 and K % tk == 0, (a.shape, b.shape, tm, tn, tk)
    gm, gn, gk = M // tm, N // tn, K // tk

    def spec(block, imap):
        if n_outer:
            return pl.BlockSpec(block, lambda j, i, k: imap(i, j, k))
        return pl.BlockSpec(block, imap)

    a_spec = spec((tk, tm), lambda i, j, k: (k, i)) if ta else spec((tm, tk), lambda i, j, k: (i, k))
    bblk = (tn, tk) if tb else (tk, tn)
    bmap = (lambda i, j, k: (j, k)) if tb else (lambda i, j, k: (k, j))
    if bl is not None:
        b_spec = spec((None,) + bblk, lambda i, j, k: (bl,) + bmap(i, j, k))
    else:
        b_spec = spec(bblk, bmap)
    in_specs, operands = [a_spec, b_spec], [a, b]
    if bias is not None:
        in_specs.append(spec((1, tn), lambda i, j, k: (0, j)))
        operands.append(bias)
    if res is not None:
        in_specs.append(spec((tm, tn), lambda i, j, k: (i, j)))
        operands.append(res)
    n_dep = len(deps) if deps else 0
    if n_dep:
        in_specs += [ANY] * n_dep
        operands += deps
        del deps[:]
    if pieces is None:
        out_shape = jax.ShapeDtypeStruct((M, N), out_dtype)
        out_spec = spec((tm, tn), lambda i, j, k: (i, j))
        ppb = pr = None
    elif pieces[0] == 'col':
        pr, pc = M // 2, N // N_CHIPS
        assert tm % pr == 0 and pc % tn == 0
        ppb, per = tm // pr, pc // tn
        perm = pieces[1]
        out_shape = jax.ShapeDtypeStruct((N_DEV, pr, pc), out_dtype)
        out_spec = spec(
            (ppb, pr, tn),
            lambda i, j, k: ((2 * (_perm_idx(j // per) if perm else j // per)) // ppb + i, 0, j % per))
    else:
        pr = M // N_DEV
        assert tm % pr == 0
        ppb = tm // pr
        out_shape = jax.ShapeDtypeStruct((N_DEV, pr, N), out_dtype)
        out_spec = spec((ppb, pr, tn), lambda i, j, k: (i, 0, j))
    dims = (((0 if ta else 1,), (1 if tb else 0,)), ((), ()))

    def body(*refs):
        a_ref, b_ref = refs[0], refs[1]
        pos = 2
        bias_ref = res_ref = None
        if bias is not None:
            bias_ref = refs[pos]
            pos += 1
        if res is not None:
            res_ref = refs[pos]
            pos += 1
        pos += n_dep
        o_ref = refs[pos]

        def finish(r):
            if bias_ref is not None:
                r = r + bias_ref[...]
            if res_ref is not None:
                r = r + res_scale * res_ref[...]
            if pieces is not None:
                r = r.reshape(ppb, pr, tn)
            o_ref[...] = r.astype(out_dtype)

        part = lax.dot_general(a_ref[...].astype(_MXU), b_ref[...].astype(_MXU), dims, preferred_element_type=F32)
        if gk == 1:
            finish(part)
            return
        acc_ref = refs[pos + 1]
        k = pl.program_id(2)

        @pl.when(k == 0)
        def _():
            acc_ref[...] = part

        @pl.when((k > 0) & (k < gk - 1))
        def _():
            acc_ref[...] += part

        @pl.when(k == gk - 1)
        def _():
            finish(acc_ref[...] + part)

    return pl.pallas_call(
        body, name=name, grid=(gn, gm, gk) if n_outer else (gm, gn, gk), in_specs=in_specs, out_specs=out_spec,
        out_shape=out_shape, scratch_shapes=[pltpu.VMEM((tm, tn), F32)] if gk > 1 else [],
        compiler_params=_cp(("parallel", "parallel", "arbitrary")),
    )(*operands)


def _mm_ln_bwd(a, w, res, res_scale, xh, rstd, g, *, name, deps=None):
    T, K = a.shape
    D = w.shape[1]
    tm = _tile(T, 512)
    n_dep = len(deps) if deps else 0

    def body(a_ref, w_ref, res_ref, xh_ref, rs_ref, g_ref, *rest):
        dz_ref, dzb_ref, dg_ref, db_ref, cs_ref = rest[n_dep:]

        @pl.when(pl.program_id(0) == 0)
        def _():
            dg_ref[...] = jnp.zeros_like(dg_ref)
            db_ref[...] = jnp.zeros_like(db_ref)
            cs_ref[...] = jnp.zeros_like(cs_ref)

        d = lax.dot_general(a_ref[...].astype(_MXU), w_ref[...].astype(_MXU), (((1,), (1,)), ((), ())),
                            preferred_element_type=F32) + res_scale * res_ref[...]
        xh = xh_ref[...]
        dz = _ln_bwd_rows(d, xh, rs_ref[...], g_ref[...])
        dz_ref[...] = dz
        dzb_ref[...] = dz.astype(_MXU)
        dg_ref[...] += _fold8(d * xh)
        db_ref[...] += _fold8(d)
        cs_ref[...] += _fold8(dz)

    row = lambda i: (i, 0)
    fixed = lambda i: (0, 0)
    tile = pl.BlockSpec((tm, D), row)
    part = pl.BlockSpec((SUBLANES, D), fixed)
    operands = [a, w, res, xh, rstd, g] + (list(deps) if deps else [])
    if deps:
        del deps[:]
    return pl.pallas_call(
        body, name=name, grid=(T // tm,),
        in_specs=[pl.BlockSpec((tm, K), row),
                  pl.BlockSpec((None, D, K), lambda i: (0, 0, 0), pipeline_mode=pl.Buffered(1)),
                  tile, tile, pl.BlockSpec((tm, 1), row), pl.BlockSpec((1, D), fixed)] + [ANY] * n_dep,
        out_specs=[tile, tile, part, part, part],
        out_shape=[jax.ShapeDtypeStruct((T, D), F32), jax.ShapeDtypeStruct((T, D), _MXU)]
        + [jax.ShapeDtypeStruct((SUBLANES, D), F32)] * 3,
        compiler_params=_cp(("arbitrary",)),
    )(*operands)


def _out_ln(act, wo_ref, bias_ref, res_ref, alpha, g_ref, b_ref, y_ref, yb_ref, xh_ref, rs_ref):
    z = jnp.dot(act, wo_ref[...].astype(_MXU), preferred_element_type=F32) + bias_ref[...] + alpha * res_ref[...]
    y, xh, rstd = _ln_rows(z, g_ref[...], b_ref[...])
    y_ref[...] = y
    yb_ref[...] = y.astype(_MXU)
    xh_ref[...] = xh
    rs_ref[...] = rstd


def _conv_tail_fwd(v, gc, bc, w, bias, res, alpha, g, b, *, name):
    T, C = v.shape
    D = w.shape[-1]
    tm = _tile(T, 512)

    def body(v_ref, gc_ref, bc_ref, w_ref, bias_ref, res_ref, g_ref, b_ref,
             s_ref, xhc_ref, rsc_ref, y_ref, yb_ref, xh_ref, rs_ref):
        yv, xhc, rsc = _ln_rows(v_ref[...], gc_ref[...], bc_ref[...])
        s = (yv * _sigmoid(yv)).astype(_MXU)
        s_ref[...] = s
        xhc_ref[...] = xhc
        rsc_ref[...] = rsc
        _out_ln(s, w_ref, bias_ref, res_ref, alpha, g_ref, b_ref, y_ref, yb_ref, xh_ref, rs_ref)

    row = lambda i: (i, 0)
    fixed = lambda i: (0, 0)
    vc, vd = pl.BlockSpec((1, C), fixed), pl.BlockSpec((1, D), fixed)
    tc_, td = pl.BlockSpec((tm, C), row), pl.BlockSpec((tm, D), row)
    one = pl.BlockSpec((tm, 1), row)
    return pl.pallas_call(
        body, name=name, grid=(T // tm,),
        in_specs=[tc_, vc, vc, _resident((None, C, D), lambda i: (0, 0, 0)), vd, td, vd, vd],
        out_specs=[tc_, tc_, one, td, td, td, one],
        out_shape=[jax.ShapeDtypeStruct((T, C), _MXU), jax.ShapeDtypeStruct((T, C), F32),
                   jax.ShapeDtypeStruct((T, 1), F32), jax.ShapeDtypeStruct((T, D), F32),
                   jax.ShapeDtypeStruct((T, D), _MXU), jax.ShapeDtypeStruct((T, D), F32),
                   jax.ShapeDtypeStruct((T, 1), F32)],
        compiler_params=_cp(("parallel",)),
    )(v, gc, bc, w, bias, res, g, b)


def _conv_cols(C, tc):
    per = (C // 2) // tc
    return per, (lambda j: (j // per) * (2 * per) + j % per)


def _glu_shifted(a_ref, g_ref, p_ref, S):
    u = a_ref[...].astype(F32) * _sigmoid(g_ref[...].astype(F32))
    rows = lax.broadcasted_iota(jnp.int32, (SUBLANES, u.shape[1]), 0)
    lo = CONV_TAPS_PAD
    for r in range(SUBLANES):
        p_ref[r, 0:lo, :] = jnp.zeros((lo, u.shape[1]), F32)
        if r == 0:
            p_ref[r, lo:lo + S, :] = u
        else:
            rolled = pltpu.roll(u, r, 0)
            p_ref[r, lo:lo + S, :] = rolled
            p_ref[r, lo:lo + SUBLANES, :] = jnp.where(rows >= r, rolled[0:SUBLANES], 0.0)


def _conv_fwd(h1, w_dw, b_dw, *, B, S, name):
    C = w_dw.shape[1]
    taps = CONV_TAPS_PAD - 1
    tc = LANES
    ch = _tile(S, 128)
    per, col_a = _conv_cols(C, tc)

    def body(a_ref, g_ref, w_ref, b_ref, o_ref, p_ref):
        _glu_shifted(a_ref, g_ref, p_ref, S)

        def chunk(ci, carry):
            base = pl.multiple_of(ci * ch, ch)
            acc = jnp.zeros((ch, tc), F32) + b_ref[...]
            for k in range(taps):
                q, r = divmod(taps - 1 - k, SUBLANES)
                start = pl.multiple_of(base + (CONV_TAPS_PAD - SUBLANES * q), SUBLANES)
                acc = acc + w_ref[pl.ds(k, 1), :] * p_ref[r, pl.ds(start, ch), :]
            o_ref[pl.ds(base, ch), :] = acc
            return carry

        lax.fori_loop(0, S // ch, chunk, 0)

    return pl.pallas_call(
        body, name=name, grid=(B, C // tc),
        in_specs=[pl.BlockSpec((S, tc), lambda b, j: (b, col_a(j))),
                  pl.BlockSpec((S, tc), lambda b, j: (b, col_a(j) + per)),
                  pl.BlockSpec((CONV_TAPS_PAD, tc), lambda b, j: (0, j)),
                  pl.BlockSpec((1, tc), lambda b, j: (0, j))],
        out_specs=pl.BlockSpec((S, tc), lambda b, j: (b, j)),
        out_shape=jax.ShapeDtypeStruct((B * S, C), F32),
        scratch_shapes=[pltpu.VMEM((SUBLANES, S + CONV_TAPS_PAD, tc), F32)],
        compiler_params=_cp(("parallel", "parallel")),
    )(h1, h1, w_dw, b_dw)


def _conv_bwd(dd, h1, w_dw, *, B, S, name):
    C = w_dw.shape[1]
    taps = CONV_TAPS_PAD - 1
    tc = LANES
    ch = _tile(S, 128)
    per, col_a = _conv_cols(C, tc)

    def body(d_ref, a_ref, g_ref, w_ref, du_ref, dw_ref, db_ref, p_ref, q_ref):
        b = pl.program_id(1)

        @pl.when(b == 0)
        def _():
            dw_ref[...] = jnp.zeros_like(dw_ref)
            db_ref[...] = jnp.zeros_like(db_ref)

        _glu_shifted(a_ref, g_ref, p_ref, S)
        d = d_ref[...]
        rows = lax.broadcasted_iota(jnp.int32, (SUBLANES, tc), 0)
        for r in range(SUBLANES):
            q_ref[r, S:S + CONV_TAPS_PAD, :] = jnp.zeros((CONV_TAPS_PAD, tc), F32)
            if r == 0:
                q_ref[r, 0:S, :] = d
            else:
                rolled = pltpu.roll(d, S - r, 0)
                q_ref[r, 0:S, :] = rolled
                q_ref[r, S - SUBLANES:S, :] = jnp.where(rows < SUBLANES - r, rolled[S - SUBLANES:S], 0.0)
        db_ref[...] += _fold8(d)

        def chunk(ci, carry):
            base = pl.multiple_of(ci * ch, ch)
            dch = d_ref[pl.ds(base, ch), :]
            acc = jnp.zeros((ch, tc), F32)
            for k in range(taps):
                q, r = divmod(taps - 1 - k, SUBLANES)
                up = pl.multiple_of(base + SUBLANES * q, SUBLANES)
                acc = acc + w_ref[pl.ds(k, 1), :] * q_ref[r, pl.ds(up, ch), :]
                down = pl.multiple_of(base + (CONV_TAPS_PAD - SUBLANES * q), SUBLANES)
                dw_ref[k] += _fold8(dch * p_ref[r, pl.ds(down, ch), :])
            du_ref[pl.ds(base, ch), :] = acc
            return carry

        lax.fori_loop(0, S // ch, chunk, 0)

    return pl.pallas_call(
        body, name=name, grid=(C // tc, B),
        in_specs=[pl.BlockSpec((S, tc), lambda j, b: (b, j)),
                  pl.BlockSpec((S, tc), lambda j, b: (b, col_a(j))),
                  pl.BlockSpec((S, tc), lambda j, b: (b, col_a(j) + per)),
                  pl.BlockSpec((CONV_TAPS_PAD, tc), lambda j, b: (0, j))],
        out_specs=[pl.BlockSpec((S, tc), lambda j, b: (b, j)),
                   pl.BlockSpec((CONV_TAPS_PAD, SUBLANES, tc), lambda j, b: (0, 0, j)),
                   pl.BlockSpec((SUBLANES, tc), lambda j, b: (0, j))],
        out_shape=[jax.ShapeDtypeStruct((B * S, C), F32),
                   jax.ShapeDtypeStruct((CONV_TAPS_PAD, SUBLANES, C), F32),
                   jax.ShapeDtypeStruct((SUBLANES, C), F32)],
        scratch_shapes=[pltpu.VMEM((SUBLANES, S + CONV_TAPS_PAD, tc), F32),
                        pltpu.VMEM((SUBLANES, S + CONV_TAPS_PAD, tc), F32)],
        compiler_params=_cp(("parallel", "arbitrary")),
    )(dd, h1, h1, w_dw)


def _ln_silu_bwd(dzb, w, xh, rstd, g, b, *, name):
    T, D = dzb.shape
    C = w.shape[1]
    tm = _tile(T, 512)

    def body(dz_ref, w_ref, xh_ref, rs_ref, g_ref, b_ref, dv_ref, dg_ref, db_ref):
        @pl.when(pl.program_id(0) == 0)
        def _():
            dg_ref[...] = jnp.zeros_like(dg_ref)
            db_ref[...] = jnp.zeros_like(db_ref)

        ds = lax.dot_general(dz_ref[...].astype(_MXU), w_ref[...].astype(_MXU), (((1,), (1,)), ((), ())),
                             preferred_element_type=F32)
        xh = xh_ref[...]
        gam = g_ref[...]
        y = xh * gam + b_ref[...]
        sig = _sigmoid(y)
        dln = ds * (sig * (1.0 + y * (1.0 - sig)))
        dv_ref[...] = _ln_bwd_rows(dln, xh, rs_ref[...], gam)
        dg_ref[...] += _fold8(dln * xh)
        db_ref[...] += _fold8(dln)

    row = lambda i: (i, 0)
    fixed = lambda i: (0, 0)
    vec = pl.BlockSpec((1, C), fixed)
    part = pl.BlockSpec((SUBLANES, C), fixed)
    return pl.pallas_call(
        body, name=name, grid=(T // tm,),
        in_specs=[pl.BlockSpec((tm, D), row), _resident((None, C, D), lambda i: (0, 0, 0)),
                  pl.BlockSpec((tm, C), row), pl.BlockSpec((tm, 1), row), vec, vec],
        out_specs=[pl.BlockSpec((tm, C), row), part, part],
        out_shape=[jax.ShapeDtypeStruct((T, C), F32)] + [jax.ShapeDtypeStruct((SUBLANES, C), F32)] * 2,
        compiler_params=_cp(("arbitrary",)),
    )(dzb, w, xh, rstd, g, b)


def _glu_bwd(du, h1, *, name):
    T, C = du.shape
    il = C // 2
    tm = _tile(T, 512)

    def body(du_ref, h_ref, dh_ref, cs_ref):
        @pl.when(pl.program_id(0) == 0)
        def _():
            cs_ref[...] = jnp.zeros_like(cs_ref)

        for hb in range(2):
            a = h_ref[:, 2 * hb * il:(2 * hb + 1) * il].astype(F32)
            gate = h_ref[:, (2 * hb + 1) * il:(2 * hb + 2) * il].astype(F32)
            d = du_ref[:, hb * il:(hb + 1) * il]
            sig = _sigmoid(gate)
            da = d * sig
            dgate = d * a * sig * (1.0 - sig)
            dh_ref[:, 2 * hb * il:(2 * hb + 1) * il] = da.astype(_MXU)
            dh_ref[:, (2 * hb + 1) * il:(2 * hb + 2) * il] = dgate.astype(_MXU)
            cs_ref[:, 2 * hb * il:(2 * hb + 1) * il] += _fold8(da)
            cs_ref[:, (2 * hb + 1) * il:(2 * hb + 2) * il] += _fold8(dgate)

    row = lambda i: (i, 0)
    return pl.pallas_call(
        body, name=name, grid=(T // tm,),
        in_specs=[pl.BlockSpec((tm, C), row), pl.BlockSpec((tm, 2 * C), row)],
        out_specs=[pl.BlockSpec((tm, 2 * C), row), pl.BlockSpec((SUBLANES, 2 * C), lambda i: (0, 0))],
        out_shape=[jax.ShapeDtypeStruct((T, 2 * C), _MXU), jax.ShapeDtypeStruct((SUBLANES, 2 * C), F32)],
        compiler_params=_cp(("arbitrary",)),
    )(du, h1)


def _tril_mask(n):
    return lax.broadcasted_iota(jnp.int32, (n, n), 0) >= lax.broadcasted_iota(jnp.int32, (n, n), 1)


def _split_uv(t, il):
    u = jnp.concatenate([t[:, 0:il], t[:, 2 * il:3 * il]], axis=1)
    v = jnp.concatenate([t[:, il:2 * il], t[:, 3 * il:4 * il]], axis=1)
    return u, v


def _gmlp_gate_fwd(p, g, b, w_s, bsb, w_out, bias, res, alpha, g1, b1, *, name):
    T, C2 = p.shape
    C = C2 // 2
    D = w_out.shape[-1]
    il = C // 2
    G, L, _ = w_s.shape
    assert G * L == C
    tm = _tile(T, 4 * L, L)

    def body(p_ref, g_ref, b_ref, ws_ref, bs_ref, wo_ref, bias_ref, res_ref, g1_ref, b1_ref,
             us_ref, xh_ref, rs_ref, y_ref, yb_ref, xh1_ref, rs1_ref, vn_ref, u_ref):
        z, _ = _gelu_parts(p_ref[...].astype(F32))
        u, v = _split_uv(z, il)
        vn, xh, rstd = _ln_rows(v, g_ref[...], b_ref[...])
        xh_ref[...] = xh
        rs_ref[...] = rstd
        vn_ref[...] = vn.astype(_MXU)
        u_ref[...] = u
        mask = _tril_mask(L)
        for gi in range(G):
            wc = jnp.where(mask, ws_ref[gi], 0.0).astype(_MXU)
            cols = slice(gi * L, (gi + 1) * L)
            for c in range(tm // L):
                rows = slice(c * L, (c + 1) * L)
                s = jnp.dot(wc, vn_ref[rows, cols], preferred_element_type=F32) + bs_ref[:, cols]
                us_ref[rows, cols] = (u_ref[rows, cols] * s).astype(_MXU)
        _out_ln(us_ref[...], wo_ref, bias_ref, res_ref, alpha, g1_ref, b1_ref, y_ref, yb_ref, xh1_ref, rs1_ref)

    row = lambda i: (i, 0)
    fixed = lambda i: (0, 0)
    vd, td, one = pl.BlockSpec((1, D), fixed), pl.BlockSpec((tm, D), row), pl.BlockSpec((tm, 1), row)
    return pl.pallas_call(
        body, name=name, grid=(T // tm,),
        in_specs=[pl.BlockSpec((tm, C2), row), pl.BlockSpec((1, C), fixed), pl.BlockSpec((1, C), fixed),
                  pl.BlockSpec((G, L, L), lambda i: (0, 0, 0)), pl.BlockSpec((L, C), fixed),
                  _resident((None, C, D), lambda i: (0, 0, 0)), vd, td, vd, vd],
        out_specs=[pl.BlockSpec((tm, C), row), pl.BlockSpec((tm, C), row), one, td, td, td, one],
        out_shape=[jax.ShapeDtypeStruct((T, C), _MXU), jax.ShapeDtypeStruct((T, C), F32),
                   jax.ShapeDtypeStruct((T, 1), F32), jax.ShapeDtypeStruct((T, D), F32),
                   jax.ShapeDtypeStruct((T, D), _MXU), jax.ShapeDtypeStruct((T, D), F32),
                   jax.ShapeDtypeStruct((T, 1), F32)],
        scratch_shapes=[pltpu.VMEM((tm, C), _MXU), pltpu.VMEM((tm, C), F32)],
        compiler_params=_cp(("parallel",)),
    )(p, g, b, w_s, bsb, w_out, bias, res, g1, b1)


def _gmlp_gate_bwd(dzb, w_out, p, xh, rstd, g, b, w_s, bsb, *, name):
    T, C2 = p.shape
    D = dzb.shape[1]
    C = C2 // 2
    il = C // 2
    G, L, _ = w_s.shape
    tm = _tile(T, 4 * L, L)

    def body(dz_ref, wo_ref, p_ref, xh_ref, rs_ref, g_ref, b_ref, ws_ref, bs_ref,
             dp_ref, dg_ref, db_ref, cs_ref, dws_ref, dbs_ref, vn_ref, u_ref, dvn_ref, du_ref, dus_ref):
        @pl.when(pl.program_id(0) == 0)
        def _():
            dg_ref[...] = jnp.zeros_like(dg_ref)
            db_ref[...] = jnp.zeros_like(db_ref)
            cs_ref[...] = jnp.zeros_like(cs_ref)
            dws_ref[...] = jnp.zeros_like(dws_ref)
            dbs_ref[...] = jnp.zeros_like(dbs_ref)

        dus_ref[...] = lax.dot_general(dz_ref[...].astype(_MXU), wo_ref[...].astype(_MXU), (((1,), (1,)), ((), ())),
                                       preferred_element_type=F32)
        z, gp = _gelu_parts(p_ref[...].astype(F32))
        u, _ = _split_uv(z, il)
        xh = xh_ref[...]
        gam = g_ref[...]
        vn_ref[...] = (xh * gam + b_ref[...]).astype(_MXU)
        u_ref[...] = u
        mask = _tril_mask(L)
        for gi in range(G):
            wc = jnp.where(mask, ws_ref[gi], 0.0).astype(_MXU)
            cols = slice(gi * L, (gi + 1) * L)
            for c in range(tm // L):
                rows = slice(c * L, (c + 1) * L)
                vnb = vn_ref[rows, cols]
                s = jnp.dot(wc, vnb, preferred_element_type=F32) + bs_ref[:, cols]
                d = dus_ref[rows, cols]
                du_ref[rows, cols] = d * s
                ds = d * u_ref[rows, cols]
                dbs_ref[:, cols] += ds
                dsb = ds.astype(_MXU)
                dw = lax.dot_general(dsb, vnb, (((1,), (1,)), ((), ())), preferred_element_type=F32)
                dws_ref[gi] += jnp.where(mask, dw, 0.0)
                dvn_ref[rows, cols] = lax.dot_general(wc, dsb, (((0,), (0,)), ((), ())), preferred_element_type=F32)
        dvn = dvn_ref[...]
        dg_ref[...] += _fold8(dvn * xh)
        db_ref[...] += _fold8(dvn)
        dv = _ln_bwd_rows(dvn, xh, rs_ref[...], gam)
        du = du_ref[...]
        for hb in range(2):
            for part, src in ((0, du), (1, dv)):
                lo = (2 * hb + part) * il
                dp = src[:, hb * il:(hb + 1) * il] * gp[:, lo:lo + il]
                dp_ref[:, lo:lo + il] = dp.astype(_MXU)
                cs_ref[:, lo:lo + il] += _fold8(dp)

    row = lambda i: (i, 0)
    fixed = lambda i: (0, 0)
    part_c = pl.BlockSpec((SUBLANES, C), fixed)
    return pl.pallas_call(
        body, name=name, grid=(T // tm,),
        in_specs=[pl.BlockSpec((tm, D), row), _resident((None, C, D), lambda i: (0, 0, 0)),
                  pl.BlockSpec((tm, C2), row), pl.BlockSpec((tm, C), row),
                  pl.BlockSpec((tm, 1), row), pl.BlockSpec((1, C), fixed), pl.BlockSpec((1, C), fixed),
                  pl.BlockSpec((G, L, L), lambda i: (0, 0, 0)), pl.BlockSpec((L, C), fixed)],
        out_specs=[pl.BlockSpec((tm, C2), row), part_c, part_c, pl.BlockSpec((SUBLANES, C2), fixed),
                   pl.BlockSpec((G, L, L), lambda i: (0, 0, 0)), pl.BlockSpec((L, C), fixed)],
        out_shape=[jax.ShapeDtypeStruct((T, C2), _MXU), jax.ShapeDtypeStruct((SUBLANES, C), F32),
                   jax.ShapeDtypeStruct((SUBLANES, C), F32), jax.ShapeDtypeStruct((SUBLANES, C2), F32),
                   jax.ShapeDtypeStruct((G, L, L), F32), jax.ShapeDtypeStruct((L, C), F32)],
        scratch_shapes=[pltpu.VMEM((tm, C), _MXU), pltpu.VMEM((tm, C), F32), pltpu.VMEM((tm, C), F32),
                        pltpu.VMEM((tm, C), F32), pltpu.VMEM((tm, C), F32)],
        compiler_params=_cp(("arbitrary",)),
    )(dzb, w_out, p, xh, rstd, g, b, w_s, bsb)


def _ffn_conv(h, prev8, w_ref, b_ref):
    h1 = _shift_down(prev8, h, 1)
    h2 = _shift_down(prev8, h, 2)
    return w_ref[pl.ds(2, 1), :] * h + w_ref[pl.ds(1, 1), :] * h1 + w_ref[pl.ds(0, 1), :] * h2 + b_ref[...]


def _resident(block, imap):
    return pl.BlockSpec(block, imap, pipeline_mode=pl.Buffered(1))


def _ffn_fwd_half(j, xb, w_up, w_down, b_up, w_dw, b_dw, *, S, name, prev=None, tail=None, head=None):
    T, D = xb.shape
    N = w_up.shape[-1]
    tn = N // N_CHIPS
    tm = _tile(S, 256)
    spt = S // tm
    last = prev is not None
    alpha = tail[1] if last else None

    def body(*refs):
        x_ref, wu_ref, wd_ref, bu_ref, wc_ref, bc_ref = refs[:6]
        if last:
            yp_ref, res_ref, bd_ref, g_ref, b_ref = refs[9:14]
            o = 14 if head is None else 15
            h_ref, hc_ref, f_ref, y_ref, yb_ref, xh_ref, rs_ref = refs[o:o + 7]
            carry_ref = refs[-1]
        else:
            h_ref, hc_ref, f_ref, yp_ref, carry_ref = refs[6:11]

        @pl.when(pl.program_id(0) % spt == 0)
        def _():
            carry_ref[...] = jnp.zeros_like(carry_ref)

        h = jnp.dot(x_ref[...].astype(_MXU), wu_ref[...].astype(_MXU), preferred_element_type=F32) + bu_ref[...]
        h_ref[...] = h.astype(_HDT)
        hc = _ffn_conv(h, carry_ref[...], wc_ref, bc_ref)
        hc_ref[...] = hc.astype(_HDT)
        carry_ref[...] = h[tm - SUBLANES:tm]
        gte = hc[:, :tn]
        f = (gte * _sigmoid(gte) * hc[:, tn:]).astype(_MXU)
        f_ref[...] = f
        y = jnp.dot(f, wd_ref[...].astype(_MXU), preferred_element_type=F32)
        if not last:
            yp_ref[...] = y
            return
        z = y + yp_ref[...] + bd_ref[...] + alpha * res_ref[...]
        out, xh, rstd = _ln_rows(z, g_ref[...], b_ref[...])
        if head is None:
            y_ref[...] = out
            yb_ref[...] = out.astype(_MXU)
            xh_ref[...] = xh
            rs_ref[...] = rstd
            return
        t_ref, cs_ref, ls_ref = refs[14], refs[o + 7], refs[o + 8]

        @pl.when(pl.program_id(0) == 0)
        def _():
            for acc in (xh_ref, rs_ref, cs_ref, ls_ref):
                acc[...] = jnp.zeros_like(acc)

        err = out - t_ref[...]
        d = err * (1.0 / D)
        dz = _ln_bwd_rows(d, xh, rstd, g_ref[...])
        y_ref[...] = dz
        yb_ref[...] = dz.astype(_MXU)
        xh_ref[...] += _fold8(d * xh)
        rs_ref[...] += _fold8(d)
        cs_ref[...] += _fold8(dz)
        ls_ref[...] += _fold8(err * err)

    row = lambda i: (i, 0)
    pair = lambda i: (0, j)
    vec = pl.BlockSpec((1, D), lambda i: (0, 0))
    tile = pl.BlockSpec((tm, D), row)
    in_specs = [tile, _resident((None, D, 2 * tn), lambda i: (0, 0, j)), _resident((None, tn, D), lambda i: (0, j, 0)),
                pl.BlockSpec((1, 2 * tn), pair), pl.BlockSpec((SUBLANES, 2 * tn), pair), pl.BlockSpec((1, 2 * tn), pair)]
    operands = [xb, w_up, w_down, b_up, w_dw, b_dw]
    wide = pl.BlockSpec((tm, 2 * tn), lambda i: (i, j))
    out_specs = [wide, wide, pl.BlockSpec((tm, tn), lambda i: (i, j))]
    out_shape = [jax.ShapeDtypeStruct((T, N), _HDT), jax.ShapeDtypeStruct((T, N), _HDT),
                 jax.ShapeDtypeStruct((T, N // 2), _MXU)]
    aliases = {}
    if last:
        res, _, b_down, g, b = tail
        in_specs += [ANY, ANY, ANY, tile, tile, vec, vec, vec]
        operands += list(prev) + [res, b_down, g, b]
        aliases = {6: 0, 7: 1, 8: 2}
        if head is None:
            out_specs += [tile, tile, tile, pl.BlockSpec((tm, 1), row)]
            out_shape += [jax.ShapeDtypeStruct((T, D), F32), jax.ShapeDtypeStruct((T, D), _MXU),
                          jax.ShapeDtypeStruct((T, D), F32), jax.ShapeDtypeStruct((T, 1), F32)]
        else:
            in_specs.append(tile)
            operands.append(head)
            part = pl.BlockSpec((SUBLANES, D), lambda i: (0, 0))
            out_specs += [tile, tile, part, part, part, part]
            out_shape += [jax.ShapeDtypeStruct((T, D), F32), jax.ShapeDtypeStruct((T, D), _MXU)] \
                + [jax.ShapeDtypeStruct((SUBLANES, D), F32)] * 4
    else:
        out_specs.append(tile)
        out_shape.append(jax.ShapeDtypeStruct((T, D), F32))
    return pl.pallas_call(
        body, name=name, grid=(T // tm,), in_specs=in_specs, out_specs=out_specs, out_shape=out_shape,
        input_output_aliases=aliases, scratch_shapes=[pltpu.VMEM((SUBLANES, 2 * tn), F32)],
        compiler_params=_cp(("arbitrary",)),
    )(*operands)


def _ffn_bwd_half(j, dzb, w_down, w_up, hs, hcs, w_dw, *, S, name, dz=None, alpha=None, prev=None, ln=None):
    T, D = dzb.shape
    N = hs.shape[1]
    tn = N // N_CHIPS
    tm = _tile(S, 256)
    spt = S // tm
    nt = T // tm
    last = prev is not None

    def body(*refs):
        dz_ref, wd_ref, wu_ref, h_ref, hc_ref, wc_ref = refs[:6]
        if last:
            dxp_ref, xh_ref, rs_ref, g_ref = refs[7:11]
            dh_ref, cs_ref, dw_ref, db_ref, dz1_ref, dz1b_ref, dg1_ref, db1_ref, cs1_ref, carry_ref = refs[11:21]
        else:
            dzf_ref = refs[6]
            dh_ref, cs_ref, dw_ref, db_ref, dxp_ref, carry_ref = refs[7:13]
        i = pl.program_id(0)
        ii = nt - 1 - i

        @pl.when(i == 0)
        def _():
            cs_ref[...] = jnp.zeros_like(cs_ref)
            dw_ref[...] = jnp.zeros_like(dw_ref)
            db_ref[...] = jnp.zeros_like(db_ref)
            if last:
                dg1_ref[...] = jnp.zeros_like(dg1_ref)
                db1_ref[...] = jnp.zeros_like(db1_ref)
                cs1_ref[...] = jnp.zeros_like(cs1_ref)

        df = lax.dot_general(dz_ref[...].astype(_MXU), wd_ref[...].astype(_MXU), (((1,), (1,)), ((), ())),
                             preferred_element_type=F32)
        h = h_ref[...].astype(F32)
        gte, val = hc_ref[:, :tn].astype(F32), hc_ref[:, tn:].astype(F32)
        sig = _sigmoid(gte)
        dval = df * (gte * sig)
        dg = df * val * (sig * (1.0 + gte * (1.0 - sig)))
        dhc = jnp.concatenate([dg, dval], axis=1)
        nxt = jnp.where((ii + 1) % spt == 0, 0.0, carry_ref[...])
        d1 = _shift_up(dhc, nxt, 1)
        d2 = _shift_up(dhc, nxt, 2)
        carry_ref[...] = dhc[0:SUBLANES]
        db_ref[...] += _fold8(dhc)
        dw_ref[2] += _fold8(dhc * h)
        dw_ref[1] += _fold8(d1 * h)
        dw_ref[0] += _fold8(d2 * h)
        dh = wc_ref[pl.ds(2, 1), :] * dhc + wc_ref[pl.ds(1, 1), :] * d1 + wc_ref[pl.ds(0, 1), :] * d2
        cs_ref[...] += _fold8(dh)
        dhb = dh.astype(_MXU)
        dh_ref[...] = dhb
        dx = lax.dot_general(dhb, wu_ref[...].astype(_MXU), (((1,), (1,)), ((), ())), preferred_element_type=F32)
        if not last:
            dxp_ref[...] = dx + alpha * dzf_ref[...]
            return
        d = dx + dxp_ref[...]
        xh = xh_ref[...]
        dz1 = _ln_bwd_rows(d, xh, rs_ref[...], g_ref[...])
        dz1_ref[...] = dz1
        dz1b_ref[...] = dz1.astype(_MXU)
        dg1_ref[...] += _fold8(d * xh)
        db1_ref[...] += _fold8(d)
        cs1_ref[...] += _fold8(dz1)

    rev = lambda i: (nt - 1 - i, 0)
    fixed = lambda i: (0, 0)
    pair = lambda i: (0, j)
    tile = pl.BlockSpec((tm, D), rev)
    wide = pl.BlockSpec((tm, 2 * tn), lambda i: (nt - 1 - i, j))
    part = pl.BlockSpec((SUBLANES, 2 * tn), fixed)
    in_specs = [tile, _resident((None, tn, D), lambda i: (0, j, 0)), _resident((None, D, 2 * tn), lambda i: (0, 0, j)),
                wide, wide, pl.BlockSpec((SUBLANES, 2 * tn), pair)]
    operands = [dzb, w_down, w_up, hs, hcs, w_dw]
    out_specs = [wide, part, pl.BlockSpec((3, SUBLANES, 2 * tn), lambda i: (0, 0, 0)), part]
    out_shape = [jax.ShapeDtypeStruct((T, N), _MXU), jax.ShapeDtypeStruct((SUBLANES, 2 * tn), F32),
                 jax.ShapeDtypeStruct((3, SUBLANES, 2 * tn), F32), jax.ShapeDtypeStruct((SUBLANES, 2 * tn), F32)]
    aliases = {}
    if last:
        xh, rstd, g = ln
        in_specs += [ANY, tile, tile, pl.BlockSpec((tm, 1), rev), pl.BlockSpec((1, D), fixed)]
        operands += [prev[0], prev[1], xh, rstd, g]
        aliases = {6: 0}
        out_specs += [tile, tile] + [pl.BlockSpec((SUBLANES, D), fixed)] * 3
        out_shape += [jax.ShapeDtypeStruct((T, D), F32), jax.ShapeDtypeStruct((T, D), _MXU)] \
            + [jax.ShapeDtypeStruct((SUBLANES, D), F32)] * 3
    else:
        in_specs.append(tile)
        operands.append(dz)
        out_specs.append(tile)
        out_shape.append(jax.ShapeDtypeStruct((T, D), F32))
    return pl.pallas_call(
        body, name=name, grid=(nt,), in_specs=in_specs, out_specs=out_specs, out_shape=out_shape,
        input_output_aliases=aliases, scratch_shapes=[pltpu.VMEM((SUBLANES, 2 * tn), F32)],
        compiler_params=_cp(("arbitrary",)),
    )(*operands)


def _sum_pieces(gs, rs, me, *, name):
    n = len(gs)
    _, pr, pc = gs[0].shape
    tr = _tile(pr, 128)

    def body(me_ref, *refs):
        o_ref = refs[2 * n]
        for l in range(n):
            total = refs[l][...].astype(F32)
            for s in range(N_DEV - 1):
                total = total + refs[n + l][s].astype(F32)
            o_ref[l] = total

    own = pl.BlockSpec((None, tr, pc), lambda i, me_ref: (me_ref[0], i, 0))
    got = pl.BlockSpec((N_DEV - 1, tr, pc), lambda i, me_ref: (0, i, 0))
    return pl.pallas_call(
        body, name=name,
        grid_spec=pltpu.PrefetchScalarGridSpec(
            num_scalar_prefetch=1, grid=(pr // tr,), in_specs=[own] * n + [got] * n,
            out_specs=pl.BlockSpec((n, tr, pc), lambda i, me_ref: (0, i, 0))),
        out_shape=jax.ShapeDtypeStruct((n, pr, pc), F32),
        compiler_params=_cp(("parallel",)),
    )(me, *gs, *rs)


def _adam_math(w, g, m, v):
    bc1 = 1.0 - ADAM_B1 ** ADAM_STEP
    bc2 = 1.0 - ADAM_B2 ** ADAM_STEP
    m = ADAM_B1 * m + (1.0 - ADAM_B1) * g
    v = ADAM_B2 * v + (1.0 - ADAM_B2) * (g * g)
    return -ADAM_LR * ((m / bc1) / (jnp.sqrt(v / bc2) + ADAM_EPS) + ADAM_WD * w), m, v


def _adam(w, g, m, v, *, name):
    R, C = w.shape
    tr = _tile(R, 256)

    def body(w_ref, g_ref, m_ref, v_ref, d_ref, mo_ref, vo_ref):
        d_ref[...], mo_ref[...], vo_ref[...] = _adam_math(w_ref[...], g_ref[...], m_ref[...], v_ref[...])

    spec = pl.BlockSpec((tr, C), lambda i: (i, 0))
    return pl.pallas_call(
        body, name=name, grid=(R // tr,), in_specs=[spec] * 4, out_specs=[spec] * 3,
        out_shape=[jax.ShapeDtypeStruct((R, C), F32)] * 3,
        compiler_params=_cp(("parallel",)),
    )(w, g, m, v)


def _adam_halves(w, own, got, m, v, core, *, name):
    L, R, C = w.shape
    rh = R // 2
    tr = _tile(rh, 256)
    nt = rh // tr

    def body(c_ref, w_ref, own_ref, got_ref, m_ref, v_ref, g_ref, d_ref, mo_ref, vo_ref):
        g = jnp.where(pl.program_id(1) == c_ref[0], own_ref[...], got_ref[...])
        g_ref[...] = g
        d_ref[...], mo_ref[...], vo_ref[...] = _adam_math(w_ref[...], g, m_ref[...], v_ref[...])

    full = pl.BlockSpec((None, tr, C), lambda l, h, t, c_ref: (l, h * nt + t, 0))
    half = pl.BlockSpec((None, tr, C), lambda l, h, t, c_ref: (l, t, 0))
    return pl.pallas_call(
        body, name=name,
        grid_spec=pltpu.PrefetchScalarGridSpec(
            num_scalar_prefetch=1, grid=(L, 2, nt), in_specs=[full, half, half, full, full], out_specs=[full] * 4),
        out_shape=[jax.ShapeDtypeStruct((L, R, C), F32)] * 4,
        compiler_params=_cp(("parallel", "parallel", "parallel")),
    )(core, w, own, got, m, v)


def _remote(src, dst, send, recv, dev):
    return pltpu.make_async_remote_copy(src_ref=src, dst_ref=dst, send_sem=send, recv_sem=recv,
                                        device_id=dev, device_id_type=MESH)


def _place_w(shard, pos, layer, *, axis, name):
    _, R, C = shard.shape
    tr = _tile(R, 512, 16)
    nt = R // tr
    if axis == 2:
        out_shape = (1, R, N_CHIPS * C)
        out_map = lambda t, q: (0, t, q[0])
    else:
        out_shape = (1, N_CHIPS * R, C)
        out_map = lambda t, q: (0, q[0] * nt + t, 0)

    def body(q_ref, s_ref, o_ref):
        o_ref[...] = s_ref[...].astype(_WIRE)

    return pl.pallas_call(
        body, name=name,
        grid_spec=pltpu.PrefetchScalarGridSpec(
            num_scalar_prefetch=1, grid=(nt,),
            in_specs=[pl.BlockSpec((None, tr, C), lambda t, q: (layer, t, 0))],
            out_specs=pl.BlockSpec((None, tr, C), out_map)),
        out_shape=jax.ShapeDtypeStruct(out_shape, _WIRE),
        compiler_params=_cp(("parallel",)),
    )(pos, shard)


def _ag_window(ref, kind, px, py, h):
    axis, perm = kind
    q = 2 * px + py
    if perm:
        q = _perm_idx(q)
    if axis == 2:
        R, C = ref.shape[1], ref.shape[2] // N_CHIPS
        rh = R // 2
        return ref.at[:, pl.ds(pl.multiple_of(h * rh, 16), rh), pl.ds(pl.multiple_of(q * C, LANES), C)]
    R = ref.shape[1] // N_CHIPS
    rh = R // 2
    return ref.at[:, pl.ds(pl.multiple_of(q * R + h * rh, 16), rh), :]


def _ag_ici_copies(refs, kinds, send, recv):
    x, y, c = lax.axis_index("x"), lax.axis_index("y"), lax.axis_index("c")
    chips = [(1 - x, y), (x, 1 - y), (1 - x, 1 - y)]
    sends, recvs = [], []
    for a, (ref, kind) in enumerate(zip(refs, kinds)):
        own = _ag_window(ref, kind, x, y, c)
        for i, (px, py) in enumerate(chips):
            k = 3 * a + i
            sends.append(_remote(own, own, send.at[k], recv.at[k], (px, py, c)))
            recvs.append(_remote(own, _ag_window(ref, kind, px, py, c), send.at[k], recv.at[k], (px, py, c)))
    return sends, recvs


def _ag_start(arrs, kinds, after, *, name):
    n = len(arrs)

    def body(*refs):
        in_refs = refs[:n]
        send, recv = refs[n + len(after)], refs[n + len(after) + 1]
        token = refs[-1]
        sends, _ = _ag_ici_copies(in_refs, kinds, send, recv)
        for cp in sends:
            cp.start()
        token[...] = jnp.zeros_like(token)

    sems = pltpu.SemaphoreType.DMA((3 * n,))
    out = pl.pallas_call(
        body, name=name,
        out_shape=(sems, sems) + tuple(pltpu.HBM(a.shape, a.dtype) for a in arrs)
        + (jax.ShapeDtypeStruct((SUBLANES, LANES), F32),),
        in_specs=(HBM,) * n + (ANY,) * len(after),
        out_specs=(SEMS, SEMS) + (HBM,) * n + (pl.BlockSpec(memory_space=pltpu.VMEM),),
        input_output_aliases={a: 2 + a for a in range(n)},
        compiler_params=pltpu.CompilerParams(has_side_effects=EFFECT),
    )(*[pltpu.with_memory_space_constraint(a, pltpu.HBM) for a in arrs], *after)
    return out[0], out[1], list(out[2:2 + n]), out[-1]


def _ag_wait(send, recv, arrs, kinds, after, *, name):
    n = len(arrs)

    def body(*refs):
        in_refs = refs[:n]
        send, recv = refs[n], refs[n + 1]
        sends, recvs = _ag_ici_copies(in_refs, kinds, send, recv)
        for cp in sends:
            cp.wait_send()
        for cp in recvs:
            cp.wait_recv()

    out = pl.pallas_call(
        body, name=name,
        out_shape=tuple(pltpu.HBM(a.shape, a.dtype) for a in arrs),
        in_specs=(HBM,) * n + (SEMS, SEMS) + (ANY,) * len(after), out_specs=(HBM,) * n,
        input_output_aliases={a: a for a in range(n)},
        compiler_params=pltpu.CompilerParams(has_side_effects=EFFECT),
    )(*arrs, send, recv, *after)
    return list(out)


def _ag_forward(arrs, kinds, *, name):
    n = len(arrs)

    def body(*refs):
        o_refs, send, recv = refs[n:2 * n], refs[2 * n], refs[2 * n + 1]
        x, y, c = lax.axis_index("x"), lax.axis_index("y"), lax.axis_index("c")
        chips = [(1 - x, y), (x, 1 - y), (1 - x, 1 - y)]
        sib = (x, y, 1 - c)
        sends, recvs = [], []
        for a, (ref, kind) in enumerate(zip(o_refs, kinds)):
            for i, (px, py) in enumerate(chips):
                k = 3 * a + i
                got = _ag_window(ref, kind, px, py, c)
                cp = _remote(got, got, send.at[k], recv.at[k], sib)
                cp.start()
                sends.append(cp)
                recvs.append(_remote(got, _ag_window(ref, kind, px, py, 1 - c), send.at[k], recv.at[k], sib))
        for cp in recvs:
            cp.wait_recv()
        for cp in sends:
            cp.wait_send()

    out = pl.pallas_call(
        body, name=name, in_specs=[ANY] * n, out_specs=[ANY] * n,
        out_shape=[jax.ShapeDtypeStruct(a.shape, a.dtype) for a in arrs],
        input_output_aliases={a: a for a in range(n)},
        scratch_shapes=[pltpu.SemaphoreType.DMA((3 * n,)), pltpu.SemaphoreType.DMA((3 * n,))],
    )(*arrs)
    return list(out)


def _flip(x, y, c, f):
    return ((1 - x) if f & 4 else x, (1 - y) if f & 2 else y, (1 - c) if f & 1 else c)


def _rs_copies(g_refs, land_refs, send, recv):
    x, y, c = lax.axis_index("x"), lax.axis_index("y"), lax.axis_index("c")
    cps = []
    for a, (g_ref, land_ref) in enumerate(zip(g_refs, land_refs)):
        for f in range(1, N_DEV):
            tx, ty, tcx = _flip(x, y, c, f)
            k = (N_DEV - 1) * a + f - 1
            cps.append(_remote(g_ref.at[4 * tx + 2 * ty + tcx], land_ref.at[f - 1], send.at[k], recv.at[k],
                               (tx, ty, tcx)))
    return cps


def _rs_start(gs, *, name):
    n = len(gs)
    lands = [lax.empty((N_DEV - 1,) + g.shape[1:], g.dtype) for g in gs]

    def body(*refs):
        send, recv, token = refs[2 * n], refs[2 * n + 1], refs[-1]
        for cp in _rs_copies(refs[:n], refs[n:2 * n], send, recv):
            cp.start()
        token[...] = jnp.zeros_like(token)

    sems = pltpu.SemaphoreType.DMA(((N_DEV - 1) * n,))
    thru = [pltpu.HBM(t.shape, t.dtype) for t in gs + lands]
    out = pl.pallas_call(
        body, name=name,
        out_shape=(sems, sems, *thru, jax.ShapeDtypeStruct((SUBLANES, LANES), F32)),
        in_specs=(HBM,) * (2 * n), out_specs=(SEMS, SEMS) + (HBM,) * (2 * n) + (pl.BlockSpec(memory_space=pltpu.VMEM),),
        input_output_aliases={a: 2 + a for a in range(2 * n)},
        compiler_params=pltpu.CompilerParams(has_side_effects=EFFECT),
    )(*[pltpu.with_memory_space_constraint(t, pltpu.HBM) for t in gs + lands])
    return out[0], out[1], list(out[2:2 + n]), list(out[2 + n:2 + 2 * n]), out[-1]


def _rs_wait(send, recv, gs, lands, after, *, name):
    n = len(gs)

    def body(*refs):
        cps = _rs_copies(refs[:n], refs[n:2 * n], refs[2 * n], refs[2 * n + 1])
        for cp in cps:
            cp.wait_send()
        for cp in cps:
            cp.wait_recv()

    out = pl.pallas_call(
        body, name=name,
        out_shape=tuple(pltpu.HBM(t.shape, t.dtype) for t in gs + lands),
        in_specs=(HBM,) * (2 * n) + (SEMS, SEMS, ANY), out_specs=(HBM,) * (2 * n),
        input_output_aliases={a: a for a in range(2 * n)},
        compiler_params=pltpu.CompilerParams(has_side_effects=EFFECT),
    )(*gs, *lands, send, recv, after)
    return list(out[:n]), list(out[n:])


def _pair_exchange(owns, *, name):
    n = len(owns)

    def body(*refs):
        send, recv = refs[2 * n], refs[2 * n + 1]
        x, y, c = lax.axis_index("x"), lax.axis_index("y"), lax.axis_index("c")
        cps = [_remote(refs[a], refs[n + a], send.at[a], recv.at[a], (x, y, 1 - c)) for a in range(n)]
        for cp in cps:
            cp.start()
        for cp in cps:
            cp.wait_recv()
        for cp in cps:
            cp.wait_send()

    return pl.pallas_call(
        body, name=name, in_specs=[ANY] * n, out_specs=[ANY] * n,
        out_shape=[jax.ShapeDtypeStruct(o.shape, o.dtype) for o in owns],
        scratch_shapes=[pltpu.SemaphoreType.DMA((n,)), pltpu.SemaphoreType.DMA((n,))],
    )(*owns)


def _allreduce_flat(vec, *, name):
    n = vec.shape[0]
    unit = N_DEV * SUBLANES * LANES
    npad = -(-n // unit) * unit
    rows = npad // (N_DEV * LANES)
    xin = jnp.pad(vec, (0, npad - n)).reshape(N_DEV, rows, LANES)

    def body(x_ref, y_ref, a_ref, send_a, recv_a, send_b, recv_b):
        x, y, c = lax.axis_index("x"), lax.axis_index("y"), lax.axis_index("c")
        me = 4 * x + 2 * y + c
        a_ref[me] = x_ref[me]
        sends, recvs = [], []
        for f in range(1, N_DEV):
            dev = _flip(x, y, c, f)
            t = 4 * dev[0] + 2 * dev[1] + dev[2]
            cp = _remote(x_ref.at[t], a_ref.at[me], send_a.at[f - 1], recv_a.at[f - 1], dev)
            cp.start()
            sends.append(cp)
            recvs.append(_remote(x_ref.at[me], a_ref.at[t], send_a.at[f - 1], recv_a.at[f - 1], dev))
        for cp in recvs:
            cp.wait_recv()
        for cp in sends:
            cp.wait_send()
        acc = a_ref[0]
        for s in range(1, N_DEV):
            acc = acc + a_ref[s]
        y_ref[me] = acc
        sends, recvs = [], []
        for f in range(1, N_DEV):
            dev = _flip(x, y, c, f)
            t = 4 * dev[0] + 2 * dev[1] + dev[2]
            cp = _remote(y_ref.at[me], y_ref.at[me], send_b.at[f - 1], recv_b.at[f - 1], dev)
            cp.start()
            sends.append(cp)
            recvs.append(_remote(y_ref.at[me], y_ref.at[t], send_b.at[f - 1], recv_b.at[f - 1], dev))
        for cp in recvs:
            cp.wait_recv()
        for cp in sends:
            cp.wait_send()

    vm = pl.BlockSpec(memory_space=pltpu.VMEM)
    out = pl.pallas_call(
        body, name=name, in_specs=[vm], out_specs=vm,
        out_shape=jax.ShapeDtypeStruct((N_DEV, rows, LANES), F32),
        scratch_shapes=[pltpu.VMEM((N_DEV, rows, LANES), F32)] + [pltpu.SemaphoreType.DMA((N_DEV - 1,))] * 4,
        compiler_params=_cp(),
    )(xin)
    return out.reshape(npad)[:n]


def _perm_cols(v, blocks=N_CHIPS):
    w = v.shape[-1] // blocks
    return jnp.concatenate([v[..., q * w:(q + 1) * w] for q in PERM], axis=-1)


def _pack(arrs):
    return jnp.concatenate([a.reshape(-1).astype(F32) for a in arrs])


def _unpack(flat, shapes):
    out, pos = [], 0
    for s in shapes:
        n = 1
        for d in s:
            n *= d
        out.append(flat[pos:pos + n].reshape(s))
        pos += n
    return out


def kernel(x, conv_w_in, conv_b_in, conv_w_dw, conv_b_dw, conv_ln_g, conv_ln_b, conv_w_out, conv_b_out, gmlp_w_in, gmlp_b_in, gmlp_ln_g, gmlp_ln_b, gmlp_w_s, gmlp_b_s, gmlp_w_out, gmlp_b_out, ffn_w_up, ffn_b_up, ffn_w_dw, ffn_b_dw, ffn_w_down, ffn_b_down, norm1_g, norm1_b, norm2_g, norm2_b, loss_target, m_conv_w_in, m_conv_b_in, m_conv_w_dw, m_conv_b_dw, m_conv_ln_g, m_conv_ln_b, m_conv_w_out, m_conv_b_out, m_gmlp_w_in, m_gmlp_b_in, m_gmlp_ln_g, m_gmlp_ln_b, m_gmlp_w_s, m_gmlp_b_s, m_gmlp_w_out, m_gmlp_b_out, m_ffn_w_up, m_ffn_b_up, m_ffn_w_dw, m_ffn_b_dw, m_ffn_w_down, m_ffn_b_down, m_norm1_g, m_norm1_b, m_norm2_g, m_norm2_b, v_conv_w_in, v_conv_b_in, v_conv_w_dw, v_conv_b_dw, v_conv_ln_g, v_conv_ln_b, v_conv_w_out, v_conv_b_out, v_gmlp_w_in, v_gmlp_b_in, v_gmlp_ln_g, v_gmlp_ln_b, v_gmlp_w_s, v_gmlp_b_s, v_gmlp_w_out, v_gmlp_b_out, v_ffn_w_up, v_ffn_b_up, v_ffn_w_dw, v_ffn_b_dw, v_ffn_w_down, v_ffn_b_down, v_norm1_g, v_norm1_b, v_norm2_g, v_norm2_b):
    P = dict(locals())
    WEIGHTS = ['conv_w_in', 'conv_b_in', 'conv_w_dw', 'conv_b_dw', 'conv_ln_g', 'conv_ln_b', 'conv_w_out',
               'conv_b_out', 'gmlp_w_in', 'gmlp_b_in', 'gmlp_ln_g', 'gmlp_ln_b', 'gmlp_w_s', 'gmlp_b_s',
               'gmlp_w_out', 'gmlp_b_out', 'ffn_w_up', 'ffn_b_up', 'ffn_w_dw', 'ffn_b_dw', 'ffn_w_down',
               'ffn_b_down', 'norm1_g', 'norm1_b', 'norm2_g', 'norm2_b']
    BIG = ['conv_w_in', 'conv_w_out', 'gmlp_w_in', 'gmlp_w_out', 'ffn_w_up', 'ffn_w_down']
    SMALL_SHARDED = {'conv_w_dw': 2, 'gmlp_b_in': 1, 'gmlp_ln_g': 1, 'gmlp_ln_b': 1, 'gmlp_b_out': 1, 'ffn_w_dw': 2}

    B, S, D = x.shape
    T = B * S
    depth = norm1_g.shape[0]
    alpha = (2.0 * depth) ** 0.25
    C = conv_w_out.shape[-1]
    F2 = ffn_b_up.shape[-1]
    G, L = gmlp_w_s.shape[1], gmlp_w_s.shape[2]
    xi, yi, ci = lax.axis_index("x"), lax.axis_index("y"), lax.axis_index("c")
    shard = 2 * xi + yi

    i32 = lambda v: jnp.reshape(v, (1,)).astype(jnp.int32)
    pos_plain, pos_perm = i32(shard), i32(_perm_idx(shard))
    me_id, core_id = i32(4 * xi + 2 * yi + ci), i32(ci)

    groups = []
    for i in range(depth):
        mix = 'conv' if i % 2 == 0 else 'gmlp'
        groups.append((f"{mix}{i // 2}", [(mix + '_w_in', i // 2, 2, True), (mix + '_w_out', i // 2, 1, False)]))
        groups.append((f"ffn{i}", [('ffn_w_up', i, 2, True), ('ffn_w_down', i, 1, False)]))
    sm_names = list(SMALL_SHARDED)
    sm_shapes = [P[n].shape for n in sm_names]
    mine = _pack([P[n] for n in sm_names]) * (ci == 0).astype(F32)
    buf = jnp.zeros((N_CHIPS, mine.shape[0]), F32)
    buf = lax.dynamic_update_slice(buf, mine[None], (shard, 0))
    gathered = _allreduce_flat(buf.reshape(-1), name="ag_small").reshape(N_CHIPS, -1)

    started, order = {}, [gathered]
    for gname, members in groups:
        placed = [_place_w(P[n], pos_perm if perm else pos_plain, l, axis=axis, name=f"place_{n}_{l}")
                  for n, l, axis, perm in members]
        kinds = [(axis, perm) for _, _, axis, perm in members]
        send, recv, arrs, token = _ag_start(placed, kinds, order, name=f"ag_start_{gname}")
        order = [token]
        started[gname] = (send, recv, arrs, kinds, [(n, l) for n, l, _, _ in members])
    wts = {}

    def arrive(gname, after):
        send, recv, arrs, kinds, keys = started[gname]
        arrs = _ag_wait(send, recv, arrs, kinds, after, name=f"ag_wait_{gname}")
        arrs = _ag_forward(arrs, kinds, name=f"ag_fwd_{gname}")
        wts.update(zip(keys, arrs))

    full = {}
    for n, parts in zip(sm_names, zip(*[_unpack(gathered[k], sm_shapes) for k in range(N_CHIPS)])):
        full[n] = jnp.concatenate(parts, axis=SMALL_SHARDED[n])
    for n in WEIGHTS:
        if n not in BIG and n not in full:
            full[n] = P[n]

    assert G * L == C, "a gMLP group must be as wide as a chunk is long"

    def row(v):
        return v.reshape(1, -1)

    def pad_rows(v, r):
        return jnp.pad(v, ((0, r - v.shape[0]), (0, 0)))

    xf = x.reshape(T, D)
    saved = []
    cur, cur_b = xf, xf.astype(_MXU)
    for i in range(depth):
        j = i // 2
        sv = {'x': cur, 'xb': cur_b}
        arrive(groups[2 * i][0], order if i == 0 else [cur_b])
        if i % 2 == 0:
            b_in = row(_perm_cols(full['conv_b_in'][j]))
            h1 = _mm(cur_b, wts['conv_w_in', j], bl=0, bias=b_in, tm=_tile(T, 512), tn=_tile(2 * C, 1024, LANES),
                     tk=D, name=f"conv_in_{j}", n_outer=True, out_dtype=_ADT)
            wdw = pad_rows(full['conv_w_dw'][j], CONV_TAPS_PAD)
            dwo = _conv_fwd(h1, wdw, row(full['conv_b_dw'][j]), B=B, S=S, name=f"conv_dw_{j}")
            s_act, xhc, rsc, *y1 = _conv_tail_fwd(
                dwo, row(full['conv_ln_g'][j]), row(full['conv_ln_b'][j]), wts['conv_w_out', j],
                row(full['conv_b_out'][j]), cur, alpha, row(norm1_g[i]), row(norm1_b[i]), name=f"conv_out_ln_{j}")
            sv.update(h1=h1, wdw=wdw, act=s_act, xhc=xhc, rsc=rsc)
        else:
            b_in = row(_perm_cols(full['gmlp_b_in'][j]))
            pre = _mm(cur_b, wts['gmlp_w_in', j], bl=0, bias=b_in, tm=_tile(T, 512), tn=_tile(2 * C, 1024, LANES),
                      tk=D, name=f"gmlp_in_{j}", n_outer=True, out_dtype=_ADT)
            bsb = jnp.repeat(gmlp_b_s[j].T, L, axis=1)
            us, xhv, rsv, *y1 = _gmlp_gate_fwd(
                pre, row(full['gmlp_ln_g'][j]), row(full['gmlp_ln_b'][j]), gmlp_w_s[j], bsb, wts['gmlp_w_out', j],
                row(full['gmlp_b_out'][j]), cur, alpha, row(norm1_g[i]), row(norm1_b[i]), name=f"gmlp_gate_{j}")
            sv.update(pre=pre, bsb=bsb, act=us, xhv=xhv, rsv=rsv)
        x1, x1b, xh1, rs1 = y1
        arrive(groups[2 * i + 1][0], [x1b])
        wdw3 = pad_rows(_perm_cols(full['ffn_w_dw'][i]), SUBLANES)
        bdw3 = row(_perm_cols(ffn_b_dw[i]))
        ffn_in = (x1b, wts['ffn_w_up', i], wts['ffn_w_down', i], row(_perm_cols(ffn_b_up[i])), wdw3, bdw3)
        first = _ffn_fwd_half(0, *ffn_in, S=S, name=f"ffn_fwd_a_{i}")
        ffn_tail = (x1, alpha, row(ffn_b_down[i]), row(norm2_g[i]), row(norm2_b[i]))
        sv.update(x1=x1, x1b=x1b, xh1=xh1, rs1=rs1, wdw3=wdw3)
        if i < depth - 1:
            hs, hcs, f_act, cur, cur_b, xh2, rs2 = _ffn_fwd_half(1, *ffn_in, S=S, name=f"ffn_fwd_b_{i}", prev=first,
                                                                 tail=ffn_tail)
            sv.update(xh2=xh2, rs2=rs2)
        else:
            hs, hcs, f_act, *sv['head'] = _ffn_fwd_half(1, *ffn_in, S=S, name=f"ffn_fwd_b_{i}", prev=first,
                                                         tail=ffn_tail, head=loss_target.reshape(T, D))
        sv.update(hs=hs, hcs=hcs, f=f_act)
        saved.append(sv)

    sg = {n: [None] * full[n].shape[0] for n in WEIGHTS if n not in BIG}
    inflight = {n: [None] * P[n].shape[0] for n in BIG}
    deps = []
    dcur = None
    loss_part = None
    tk_t = _tile(T, 2048)

    ready = []

    def wgrad(n, l, a_, b_, **kw):
        tk = T if n.endswith('w_in') else tk_t
        ready.append((n, l, _mm(a_, b_, ta=True, out_dtype=_WIRE, tk=tk, name=f"{n}_dw_{l}", deps=deps, **kw)))
        launch(f"{n}_{l}")

    def launch(gname):
        send, recv, gs, lands, token = _rs_start([g for _, _, g in ready], name=f"rs_start_{gname}")
        group = {'name': gname, 'flight': (send, recv, gs, lands), 'landed': None}
        for a, (n, l, _) in enumerate(ready):
            inflight[n][l] = (group, a)
        del ready[:]
        deps.append(token)

    def landed(n, l):
        group, a = inflight[n][l]
        if group['landed'] is None:
            group['landed'] = _rs_wait(*group['flight'], dcur, name=f"rs_wait_{group['name']}")
        return group['landed'][0][a], group['landed'][1][a]

    for i in reversed(range(depth)):
        j = i // 2
        sv = saved[i]
        if i == depth - 1:
            dz2, dz2b, dg, db, cs, loss_part = sv['head']
        else:
            dz2, dz2b, dg, db, cs = dcur
        sg['norm2_g'][i], sg['norm2_b'][i], sg['ffn_b_down'][i] = dg.sum(0), db.sum(0), cs.sum(0)
        Fh = F2 // 2
        wgrad('ffn_w_down', i, sv['f'], dz2b, tm=Fh // 2, tn=_tile(D, 1024, LANES), pieces=('row',))
        ffn_in = (dz2b, wts['ffn_w_down', i], wts['ffn_w_up', i], sv['hs'], sv['hcs'], sv['wdw3'])
        dh0, csu0, dwd0, dbd0, dxp = _ffn_bwd_half(0, *ffn_in, S=S, name=f"ffn_bwd_a_{i}", dz=dz2, alpha=alpha)
        dh, csu1, dwd1, dbd1, dz1, dz1b, dg, db, cs = _ffn_bwd_half(
            1, *ffn_in, S=S, name=f"ffn_bwd_b_{i}", prev=(dh0, dxp), ln=(sv['xh1'], sv['rs1'], row(norm1_g[i])))
        sg['ffn_b_up'][i] = _perm_cols(jnp.concatenate([csu0.sum(0), csu1.sum(0)], axis=-1))
        sg['ffn_w_dw'][i] = _perm_cols(jnp.concatenate([dwd0.sum(1), dwd1.sum(1)], axis=-1))
        sg['ffn_b_dw'][i] = _perm_cols(jnp.concatenate([dbd0.sum(0), dbd1.sum(0)], axis=-1))
        wgrad('ffn_w_up', i, sv['x1b'], dh, tm=D, tn=F2 // N_CHIPS, pieces=('col', True))
        sg['norm1_g'][i], sg['norm1_b'][i] = dg.sum(0), db.sum(0)
        if i % 2 == 0:
            sg['conv_b_out'][j] = cs.sum(0)
            wgrad('conv_w_out', j, sv['act'], dz1b, tm=_tile(C, 1024), tn=_tile(D, 1024, LANES), pieces=('row',))
            ddw, dg, db = _ln_silu_bwd(dz1b, wts['conv_w_out', j], sv['xhc'], sv['rsc'], row(full['conv_ln_g'][j]),
                                       row(full['conv_ln_b'][j]), name=f"conv_ln_bwd_{j}")
            sg['conv_ln_g'][j], sg['conv_ln_b'][j] = dg.sum(0), db.sum(0)
            dglu, dwk, dbk = _conv_bwd(ddw, sv['h1'], sv['wdw'], B=B, S=S, name=f"conv_dw_bwd_{j}")
            sg['conv_w_dw'][j] = dwk.sum(1)[:conv_w_dw.shape[1]]
            sg['conv_b_dw'][j] = dbk.sum(0)
            dh1, csi = _glu_bwd(dglu, sv['h1'], name=f"conv_glu_bwd_{j}")
            sg['conv_b_in'][j] = _perm_cols(csi.sum(0))
            fam = 'conv_w_in'
        else:
            sg['gmlp_b_out'][j] = cs.sum(0)
            wgrad('gmlp_w_out', j, sv['act'], dz1b, tm=_tile(C, 1024), tn=_tile(D, 1024, LANES), pieces=('row',))
            dh1, dg, db, csi, dws, dbs = _gmlp_gate_bwd(dz1b, wts['gmlp_w_out', j], sv['pre'], sv['xhv'], sv['rsv'],
                                                        row(full['gmlp_ln_g'][j]), row(full['gmlp_ln_b'][j]),
                                                        gmlp_w_s[j], sv['bsb'], name=f"gmlp_gate_bwd_{j}")
            sg['gmlp_ln_g'][j], sg['gmlp_ln_b'][j] = dg.sum(0), db.sum(0)
            sg['gmlp_b_in'][j] = _perm_cols(csi.sum(0))
            sg['gmlp_w_s'][j] = dws
            sg['gmlp_b_s'][j] = dbs.reshape(L, G, L).sum(-1).T
            fam = 'gmlp_w_in'
        wgrad(fam, j, sv['xb'], dh1, tm=D, tn=(2 * C) // N_CHIPS, pieces=('col', True))
        if i > 0:
            below = saved[i - 1]
            dcur = _mm_ln_bwd(dh1, wts[fam, j], dz1, alpha, below['xh2'], below['rs2'], row(norm2_g[i - 1]),
                              name=f"{fam}_dx_{j}", deps=deps)
        else:
            dcur = _mm(dh1, wts[fam, j], bl=0, tb=True, res=dz1, res_scale=alpha, tm=_tile(T, 512),
                       tn=_tile(D, 1024, LANES), tk=2 * C, name=f"{fam}_dx_{j}", deps=deps)
    grad_x = dcur.reshape(B, S, D)

    small_names = [n for n in WEIGHTS if n not in BIG]
    small_full = [jnp.stack(sg[n]) for n in small_names]
    flat = _pack(small_full + [loss_part])
    red = _allreduce_flat(flat, name="ar_small")
    red_parts = _unpack(red, [a.shape for a in small_full] + [loss_part.shape])
    loss = (0.5 / D) * jnp.sum(red_parts[-1])
    grads = {}
    for n, g in zip(small_names, red_parts[:-1]):
        if n in SMALL_SHARDED:
            ax = SMALL_SHARDED[n]
            width = P[n].shape[ax]
            g = lax.dynamic_slice_in_dim(g, shard * width, width, axis=ax)
        grads[n] = g

    big_out = {}
    for n in ['ffn_w_down', 'ffn_w_up', 'gmlp_w_out', 'gmlp_w_in', 'conv_w_out', 'conv_w_in']:
        both = [landed(n, l) for l in range(len(inflight[n]))]
        own = _sum_pieces([g for g, _ in both], [r for _, r in both], me_id, name=f"sum_{n}")
        got, = _pair_exchange([own], name=f"px_{n}")
        big_out[n] = _adam_halves(P[n], own, got, P['m_' + n], P['v_' + n], core_id, name=f"adam_{n}")

    shapes = [P[n].shape for n in small_names]
    n_small = sum(functools.reduce(lambda p_, d_: p_ * d_, s_, 1) for s_ in shapes)
    unit = SUBLANES * LANES
    npad = -(-n_small // unit) * unit

    def flat2d(arrs, fill=0.0):
        v = _pack(arrs)
        return jnp.pad(v, (0, npad - n_small), constant_values=fill).reshape(-1, LANES)

    dl, mo, vo = _adam(flat2d([P[n] for n in small_names]), flat2d([grads[n] for n in small_names]),
                       flat2d([P['m_' + n] for n in small_names]),
                       flat2d([P['v_' + n] for n in small_names], fill=1.0), name="adam_small")
    small_out = {n: [grads[n], None, None, None] for n in small_names}
    for k, t in enumerate((dl, mo, vo)):
        for n, a in zip(small_names, _unpack(t.reshape(-1), shapes)):
            small_out[n][k + 1] = a

    outs = [loss, grad_x]
    for k in range(4):
        for n in WEIGHTS:
            outs.append(big_out[n][k] if n in BIG else small_out[n][k])
    return tuple(outs)
```

```python
import functools

import jax
import jax.numpy as jnp
from jax import lax
from jax.experimental import pallas as pl
from jax.experimental.pallas import tpu as pltpu

F32 = jnp.float32
_MXU = jnp.bfloat16
_WIRE = jnp.bfloat16
_HDT = jnp.bfloat16
_ADT = jnp.bfloat16
_XDT = jnp.bfloat16
LN_EPS = 1e-5
ADAM_LR, ADAM_B1, ADAM_B2, ADAM_EPS, ADAM_WD, ADAM_STEP = 0.001, 0.9, 0.999, 1e-08, 0.01, 10
N_CHIPS = 4
N_DEV = 8
LANES = 128
SUBLANES = 8
CONV_TAPS_PAD = 32
VMEM_LIMIT = 56 << 20
MESH = pl.DeviceIdType.MESH
ANY = pl.BlockSpec(memory_space=pl.ANY)
HBM = pl.BlockSpec(memory_space=pltpu.HBM)
SEMS = pl.BlockSpec(memory_space=pltpu.SEMAPHORE)
EFFECT = pltpu.SideEffectType.DATAFLOW_SIDE_EFFECTING
PERM = (0, 2, 1, 3)


def _cp(sem=None):
    return pltpu.CompilerParams(dimension_semantics=sem, vmem_limit_bytes=VMEM_LIMIT)


def _tile(dim, pref, mult=SUBLANES):
    if dim <= pref:
        return dim
    t = (pref // mult) * mult
    while t > mult and dim % t:
        t -= mult
    assert dim % t == 0, (dim, pref, mult)
    return t


def _perm_idx(q):
    return (q % 2) * 2 + q // 2


def _fold8(t):
    r, n = t.shape
    return t.reshape(r // SUBLANES, SUBLANES, n).sum(axis=0)


def _ln_rows(z, g, b):
    mu = jnp.mean(z, axis=-1, keepdims=True)
    xc = z - mu
    var = jnp.mean(xc * xc, axis=-1, keepdims=True)
    rstd = lax.rsqrt(var + LN_EPS)
    xh = xc * rstd
    return xh * g + b, xh, rstd


def _ln_bwd_rows(dy, xh, rstd, g):
    dxh = dy * g
    m1 = jnp.mean(dxh, axis=-1, keepdims=True)
    m2 = jnp.mean(dxh * xh, axis=-1, keepdims=True)
    return rstd * (dxh - m1 - xh * m2)


def _sigmoid(v):
    return 0.5 * jnp.tanh(0.5 * v) + 0.5


def _gelu_parts(p):
    cdf = 0.5 * (1.0 + lax.erf(p * 0.7071067811865476))
    pdf = jnp.exp(-0.5 * p * p) * 0.3989422804014327
    return p * cdf, cdf + p * pdf


def _shift_down(prev8, t, s):
    ext = jnp.concatenate([prev8, t], axis=0)
    return pltpu.roll(ext, s, 0)[SUBLANES:]


def _shift_up(t, next8, s):
    n = t.shape[0]
    ext = jnp.concatenate([t, next8], axis=0)
    return pltpu.roll(ext, n + SUBLANES - s, 0)[:n]


def _mm(a, b, *, ta=False, tb=False, bl=None, bias=None, res=None, res_scale=1.0, out_dtype=F32,
        tm, tn, tk, name, pieces=None, deps=None, n_outer=False):
    M, K = (a.shape[1], a.shape[0]) if ta else a.shape
    bs = b.shape[1:] if bl is not None else b.shape
    N, Kb = (bs[0], bs[1]) if tb else (bs[1], bs[0])
    assert K == Kb and M % tm == 0 and N % tn == 0 and K % tk == 0, (a.shape, b.shape, tm, tn, tk)
    gm, gn, gk = M // tm, N // tn, K // tk

    def spec(block, imap):
        if n_outer:
            return pl.BlockSpec(block, lambda j, i, k: imap(i, j, k))
        return pl.BlockSpec(block, imap)

    a_spec = spec((tk, tm), lambda i, j, k: (k, i)) if ta else spec((tm, tk), lambda i, j, k: (i, k))
    bblk = (tn, tk) if tb else (tk, tn)
    bmap = (lambda i, j, k: (j, k)) if tb else (lambda i, j, k: (k, j))
    if bl is not None:
        b_spec = spec((None,) + bblk, lambda i, j, k: (bl,) + bmap(i, j, k))
    else:
        b_spec = spec(bblk, bmap)
    in_specs, operands = [a_spec, b_spec], [a, b]
    if bias is not None:
        in_specs.append(spec((1, tn), lambda i, j, k: (0, j)))
        operands.append(bias)
    if res is not None:
        in_specs.append(spec((tm, tn), lambda i, j, k: (i, j)))
        operands.append(res)
    n_dep = len(deps) if deps else 0
    if n_dep:
        in_specs += [ANY] * n_dep
        operands += deps
        del deps[:]
    if pieces is None:
        out_shape = jax.ShapeDtypeStruct((M, N), out_dtype)
        out_spec = spec((tm, tn), lambda i, j, k: (i, j))
        ppb = pr = None
    elif pieces[0] == 'col':
        pr, pc = M // 2, N // N_CHIPS
        assert tm % pr == 0 and pc % tn == 0
        ppb, per = tm // pr, pc // tn
        perm = pieces[1]
        out_shape = jax.ShapeDtypeStruct((N_DEV, pr, pc), out_dtype)
        out_spec = spec(
            (ppb, pr, tn),
            lambda i, j, k: ((2 * (_perm_idx(j // per) if perm else j // per)) // ppb + i, 0, j % per))
    else:
        pr = M // N_DEV
        assert tm % pr == 0
        ppb = tm // pr
        out_shape = jax.ShapeDtypeStruct((N_DEV, pr, N), out_dtype)
        out_spec = spec((ppb, pr, tn), lambda i, j, k: (i, 0, j))
    dims = (((0 if ta else 1,), (1 if tb else 0,)), ((), ()))

    def body(*refs):
        a_ref, b_ref = refs[0], refs[1]
        pos = 2
        bias_ref = res_ref = None
        if bias is not None:
            bias_ref = refs[pos]
            pos += 1
        if res is not None:
            res_ref = refs[pos]
            pos += 1
        pos += n_dep
        o_ref = refs[pos]

        def finish(r):
            if bias_ref is not None:
                r = r + bias_ref[...]
            if res_ref is not None:
                r = r + res_scale * res_ref[...]
            if pieces is not None:
                r = r.reshape(ppb, pr, tn)
            o_ref[...] = r.astype(out_dtype)

        part = lax.dot_general(a_ref[...].astype(_MXU), b_ref[...].astype(_MXU), dims, preferred_element_type=F32)
        if gk == 1:
            finish(part)
            return
        acc_ref = refs[pos + 1]
        k = pl.program_id(2)

        @pl.when(k == 0)
        def _():
            acc_ref[...] = part

        @pl.when((k > 0) & (k < gk - 1))
        def _():
            acc_ref[...] += part

        @pl.when(k == gk - 1)
        def _():
            finish(acc_ref[...] + part)

    return pl.pallas_call(
        body, name=name, grid=(gn, gm, gk) if n_outer else (gm, gn, gk), in_specs=in_specs, out_specs=out_spec,
        out_shape=out_shape, scratch_shapes=[pltpu.VMEM((tm, tn), F32)] if gk > 1 else [],
        compiler_params=_cp(("parallel", "parallel", "arbitrary")),
    )(*operands)


def _mm_ln_bwd(a, w, res, res_scale, xh, rstd, g, *, name, deps=None):
    T, K = a.shape
    D = w.shape[1]
    tm = _tile(T, 512)
    n_dep = len(deps) if deps else 0

    def body(a_ref, w_ref, res_ref, xh_ref, rs_ref, g_ref, *rest):
        dz_ref, dzb_ref, dg_ref, db_ref, cs_ref = rest[n_dep:]

        @pl.when(pl.program_id(0) == 0)
        def _():
            dg_ref[...] = jnp.zeros_like(dg_ref)
            db_ref[...] = jnp.zeros_like(db_ref)
            cs_ref[...] = jnp.zeros_like(cs_ref)

        d = lax.dot_general(a_ref[...].astype(_MXU), w_ref[...].astype(_MXU), (((1,), (1,)), ((), ())),
                            preferred_element_type=F32) + res_scale * res_ref[...]
        xh = xh_ref[...].astype(F32)
        dz = _ln_bwd_rows(d, xh, rs_ref[...], g_ref[...])
        dz_ref[...] = dz
        dzb_ref[...] = dz.astype(_MXU)
        dg_ref[...] += _fold8(d * xh)
        db_ref[...] += _fold8(d)
        cs_ref[...] += _fold8(dz)

    row = lambda i: (i, 0)
    fixed = lambda i: (0, 0)
    tile = pl.BlockSpec((tm, D), row)
    part = pl.BlockSpec((SUBLANES, D), fixed)
    operands = [a, w, res, xh, rstd, g] + (list(deps) if deps else [])
    if deps:
        del deps[:]
    return pl.pallas_call(
        body, name=name, grid=(T // tm,),
        in_specs=[pl.BlockSpec((tm, K), row),
                  pl.BlockSpec((None, D, K), lambda i: (0, 0, 0), pipeline_mode=pl.Buffered(1)),
                  tile, tile, pl.BlockSpec((tm, 1), row), pl.BlockSpec((1, D), fixed)] + [ANY] * n_dep,
        out_specs=[tile, tile, part, part, part],
        out_shape=[jax.ShapeDtypeStruct((T, D), F32), jax.ShapeDtypeStruct((T, D), _MXU)]
        + [jax.ShapeDtypeStruct((SUBLANES, D), F32)] * 3,
        compiler_params=_cp(("arbitrary",)),
    )(*operands)


def _out_ln(act, wo_ref, bias_ref, res_ref, alpha, g_ref, b_ref, y_ref, yb_ref, xh_ref, rs_ref):
    z = jnp.dot(act, wo_ref[...].astype(_MXU), preferred_element_type=F32) + bias_ref[...] + alpha * res_ref[...]
    y, xh, rstd = _ln_rows(z, g_ref[...], b_ref[...])
    y_ref[...] = y
    yb_ref[...] = y.astype(_MXU)
    xh_ref[...] = xh.astype(_XDT)
    rs_ref[...] = rstd


def _conv_tail_fwd(v, gc, bc, w, bias, res, alpha, g, b, *, name):
    T, C = v.shape
    D = w.shape[-1]
    tm = _tile(T, 512)

    def body(v_ref, gc_ref, bc_ref, w_ref, bias_ref, res_ref, g_ref, b_ref,
             s_ref, xhc_ref, rsc_ref, y_ref, yb_ref, xh_ref, rs_ref):
        yv, xhc, rsc = _ln_rows(v_ref[...], gc_ref[...], bc_ref[...])
        s = (yv * _sigmoid(yv)).astype(_MXU)
        s_ref[...] = s
        xhc_ref[...] = xhc.astype(_XDT)
        rsc_ref[...] = rsc
        _out_ln(s, w_ref, bias_ref, res_ref, alpha, g_ref, b_ref, y_ref, yb_ref, xh_ref, rs_ref)

    row = lambda i: (i, 0)
    fixed = lambda i: (0, 0)
    vc, vd = pl.BlockSpec((1, C), fixed), pl.BlockSpec((1, D), fixed)
    tc_, td = pl.BlockSpec((tm, C), row), pl.BlockSpec((tm, D), row)
    one = pl.BlockSpec((tm, 1), row)
    return pl.pallas_call(
        body, name=name, grid=(T // tm,),
        in_specs=[tc_, vc, vc, _resident((None, C, D), lambda i: (0, 0, 0)), vd, td, vd, vd],
        out_specs=[tc_, tc_, one, td, td, td, one],
        out_shape=[jax.ShapeDtypeStruct((T, C), _MXU), jax.ShapeDtypeStruct((T, C), _XDT),
                   jax.ShapeDtypeStruct((T, 1), F32), jax.ShapeDtypeStruct((T, D), F32),
                   jax.ShapeDtypeStruct((T, D), _MXU), jax.ShapeDtypeStruct((T, D), _XDT),
                   jax.ShapeDtypeStruct((T, 1), F32)],
        compiler_params=_cp(("parallel",)),
    )(v, gc, bc, w, bias, res, g, b)


def _conv_cols(C, tc):
    per = (C // 2) // tc
    return per, (lambda j: (j // per) * (2 * per) + j % per)


def _glu_shifted(a_ref, g_ref, p_ref, S):
    u = a_ref[...].astype(F32) * _sigmoid(g_ref[...].astype(F32))
    rows = lax.broadcasted_iota(jnp.int32, (SUBLANES, u.shape[1]), 0)
    lo = CONV_TAPS_PAD
    for r in range(SUBLANES):
        p_ref[r, 0:lo, :] = jnp.zeros((lo, u.shape[1]), F32)
        if r == 0:
            p_ref[r, lo:lo + S, :] = u
        else:
            rolled = pltpu.roll(u, r, 0)
            p_ref[r, lo:lo + S, :] = rolled
            p_ref[r, lo:lo + SUBLANES, :] = jnp.where(rows >= r, rolled[0:SUBLANES], 0.0)


def _conv_fwd(h1, w_dw, b_dw, *, B, S, name):
    C = w_dw.shape[1]
    taps = CONV_TAPS_PAD - 1
    tc = LANES
    ch = _tile(S, 128)
    per, col_a = _conv_cols(C, tc)

    def body(a_ref, g_ref, w_ref, b_ref, o_ref, p_ref):
        _glu_shifted(a_ref, g_ref, p_ref, S)

        def chunk(ci, carry):
            base = pl.multiple_of(ci * ch, ch)
            acc = jnp.zeros((ch, tc), F32) + b_ref[...]
            for k in range(taps):
                q, r = divmod(taps - 1 - k, SUBLANES)
                start = pl.multiple_of(base + (CONV_TAPS_PAD - SUBLANES * q), SUBLANES)
                acc = acc + w_ref[pl.ds(k, 1), :] * p_ref[r, pl.ds(start, ch), :]
            o_ref[pl.ds(base, ch), :] = acc
            return carry

        lax.fori_loop(0, S // ch, chunk, 0)

    return pl.pallas_call(
        body, name=name, grid=(B, C // tc),
        in_specs=[pl.BlockSpec((S, tc), lambda b, j: (b, col_a(j))),
                  pl.BlockSpec((S, tc), lambda b, j: (b, col_a(j) + per)),
                  pl.BlockSpec((CONV_TAPS_PAD, tc), lambda b, j: (0, j)),
                  pl.BlockSpec((1, tc), lambda b, j: (0, j))],
        out_specs=pl.BlockSpec((S, tc), lambda b, j: (b, j)),
        out_shape=jax.ShapeDtypeStruct((B * S, C), F32),
        scratch_shapes=[pltpu.VMEM((SUBLANES, S + CONV_TAPS_PAD, tc), F32)],
        compiler_params=_cp(("parallel", "parallel")),
    )(h1, h1, w_dw, b_dw)


def _conv_bwd(dd, h1, w_dw, *, B, S, name):
    C = w_dw.shape[1]
    taps = CONV_TAPS_PAD - 1
    tc = LANES
    ch = _tile(S, 128)
    per, col_a = _conv_cols(C, tc)

    def body(d_ref, a_ref, g_ref, w_ref, du_ref, dw_ref, db_ref, p_ref, q_ref):
        b = pl.program_id(1)

        @pl.when(b == 0)
        def _():
            dw_ref[...] = jnp.zeros_like(dw_ref)
            db_ref[...] = jnp.zeros_like(db_ref)

        _glu_shifted(a_ref, g_ref, p_ref, S)
        d = d_ref[...]
        rows = lax.broadcasted_iota(jnp.int32, (SUBLANES, tc), 0)
        for r in range(SUBLANES):
            q_ref[r, S:S + CONV_TAPS_PAD, :] = jnp.zeros((CONV_TAPS_PAD, tc), F32)
            if r == 0:
                q_ref[r, 0:S, :] = d
            else:
                rolled = pltpu.roll(d, S - r, 0)
                q_ref[r, 0:S, :] = rolled
                q_ref[r, S - SUBLANES:S, :] = jnp.where(rows < SUBLANES - r, rolled[S - SUBLANES:S], 0.0)
        db_ref[...] += _fold8(d)

        def chunk(ci, carry):
            base = pl.multiple_of(ci * ch, ch)
            dch = d_ref[pl.ds(base, ch), :]
            acc = jnp.zeros((ch, tc), F32)
            for k in range(taps):
                q, r = divmod(taps - 1 - k, SUBLANES)
                up = pl.multiple_of(base + SUBLANES * q, SUBLANES)
                acc = acc + w_ref[pl.ds(k, 1), :] * q_ref[r, pl.ds(up, ch), :]
                down = pl.multiple_of(base + (CONV_TAPS_PAD - SUBLANES * q), SUBLANES)
                dw_ref[k] += _fold8(dch * p_ref[r, pl.ds(down, ch), :])
            du_ref[pl.ds(base, ch), :] = acc
            return carry

        lax.fori_loop(0, S // ch, chunk, 0)

    return pl.pallas_call(
        body, name=name, grid=(C // tc, B),
        in_specs=[pl.BlockSpec((S, tc), lambda j, b: (b, j)),
                  pl.BlockSpec((S, tc), lambda j, b: (b, col_a(j))),
                  pl.BlockSpec((S, tc), lambda j, b: (b, col_a(j) + per)),
                  pl.BlockSpec((CONV_TAPS_PAD, tc), lambda j, b: (0, j))],
        out_specs=[pl.BlockSpec((S, tc), lambda j, b: (b, j)),
                   pl.BlockSpec((CONV_TAPS_PAD, SUBLANES, tc), lambda j, b: (0, 0, j)),
                   pl.BlockSpec((SUBLANES, tc), lambda j, b: (0, j))],
        out_shape=[jax.ShapeDtypeStruct((B * S, C), F32),
                   jax.ShapeDtypeStruct((CONV_TAPS_PAD, SUBLANES, C), F32),
                   jax.ShapeDtypeStruct((SUBLANES, C), F32)],
        scratch_shapes=[pltpu.VMEM((SUBLANES, S + CONV_TAPS_PAD, tc), F32),
                        pltpu.VMEM((SUBLANES, S + CONV_TAPS_PAD, tc), F32)],
        compiler_params=_cp(("parallel", "arbitrary")),
    )(dd, h1, h1, w_dw)


def _ln_silu_bwd(dzb, w, xh, rstd, g, b, *, name):
    T, D = dzb.shape
    C = w.shape[1]
    tm = _tile(T, 512)

    def body(dz_ref, w_ref, xh_ref, rs_ref, g_ref, b_ref, dv_ref, dg_ref, db_ref):
        @pl.when(pl.program_id(0) == 0)
        def _():
            dg_ref[...] = jnp.zeros_like(dg_ref)
            db_ref[...] = jnp.zeros_like(db_ref)

        ds = lax.dot_general(dz_ref[...].astype(_MXU), w_ref[...].astype(_MXU), (((1,), (1,)), ((), ())),
                             preferred_element_type=F32)
        xh = xh_ref[...].astype(F32)
        gam = g_ref[...]
        y = xh * gam + b_ref[...]
        sig = _sigmoid(y)
        dln = ds * (sig * (1.0 + y * (1.0 - sig)))
        dv_ref[...] = _ln_bwd_rows(dln, xh, rs_ref[...], gam)
        dg_ref[...] += _fold8(dln * xh)
        db_ref[...] += _fold8(dln)

    row = lambda i: (i, 0)
    fixed = lambda i: (0, 0)
    vec = pl.BlockSpec((1, C), fixed)
    part = pl.BlockSpec((SUBLANES, C), fixed)
    return pl.pallas_call(
        body, name=name, grid=(T // tm,),
        in_specs=[pl.BlockSpec((tm, D), row), _resident((None, C, D), lambda i: (0, 0, 0)),
                  pl.BlockSpec((tm, C), row), pl.BlockSpec((tm, 1), row), vec, vec],
        out_specs=[pl.BlockSpec((tm, C), row), part, part],
        out_shape=[jax.ShapeDtypeStruct((T, C), F32)] + [jax.ShapeDtypeStruct((SUBLANES, C), F32)] * 2,
        compiler_params=_cp(("arbitrary",)),
    )(dzb, w, xh, rstd, g, b)


def _glu_bwd(du, h1, *, name):
    T, C = du.shape
    il = C // 2
    tm = _tile(T, 512)

    def body(du_ref, h_ref, dh_ref, cs_ref):
        @pl.when(pl.program_id(0) == 0)
        def _():
            cs_ref[...] = jnp.zeros_like(cs_ref)

        for hb in range(2):
            a = h_ref[:, 2 * hb * il:(2 * hb + 1) * il].astype(F32)
            gate = h_ref[:, (2 * hb + 1) * il:(2 * hb + 2) * il].astype(F32)
            d = du_ref[:, hb * il:(hb + 1) * il]
            sig = _sigmoid(gate)
            da = d * sig
            dgate = d * a * sig * (1.0 - sig)
            dh_ref[:, 2 * hb * il:(2 * hb + 1) * il] = da.astype(_MXU)
            dh_ref[:, (2 * hb + 1) * il:(2 * hb + 2) * il] = dgate.astype(_MXU)
            cs_ref[:, 2 * hb * il:(2 * hb + 1) * il] += _fold8(da)
            cs_ref[:, (2 * hb + 1) * il:(2 * hb + 2) * il] += _fold8(dgate)

    row = lambda i: (i, 0)
    return pl.pallas_call(
        body, name=name, grid=(T // tm,),
        in_specs=[pl.BlockSpec((tm, C), row), pl.BlockSpec((tm, 2 * C), row)],
        out_specs=[pl.BlockSpec((tm, 2 * C), row), pl.BlockSpec((SUBLANES, 2 * C), lambda i: (0, 0))],
        out_shape=[jax.ShapeDtypeStruct((T, 2 * C), _MXU), jax.ShapeDtypeStruct((SUBLANES, 2 * C), F32)],
        compiler_params=_cp(("arbitrary",)),
    )(du, h1)


def _tril_mask(n):
    return lax.broadcasted_iota(jnp.int32, (n, n), 0) >= lax.broadcasted_iota(jnp.int32, (n, n), 1)


def _split_uv(t, il):
    u = jnp.concatenate([t[:, 0:il], t[:, 2 * il:3 * il]], axis=1)
    v = jnp.concatenate([t[:, il:2 * il], t[:, 3 * il:4 * il]], axis=1)
    return u, v


def _gmlp_gate_fwd(p, g, b, w_s, bsb, w_out, bias, res, alpha, g1, b1, *, name):
    T, C2 = p.shape
    C = C2 // 2
    D = w_out.shape[-1]
    il = C // 2
    G, L, _ = w_s.shape
    assert G * L == C
    tm = _tile(T, 4 * L, L)

    def body(p_ref, g_ref, b_ref, ws_ref, bs_ref, wo_ref, bias_ref, res_ref, g1_ref, b1_ref,
             us_ref, xh_ref, rs_ref, y_ref, yb_ref, xh1_ref, rs1_ref, vn_ref, u_ref):
        z, _ = _gelu_parts(p_ref[...].astype(F32))
        u, v = _split_uv(z, il)
        vn, xh, rstd = _ln_rows(v, g_ref[...], b_ref[...])
        xh_ref[...] = xh.astype(_XDT)
        rs_ref[...] = rstd
        vn_ref[...] = vn.astype(_MXU)
        u_ref[...] = u
        mask = _tril_mask(L)
        for gi in range(G):
            wc = jnp.where(mask, ws_ref[gi], 0.0).astype(_MXU)
            cols = slice(gi * L, (gi + 1) * L)
            for c in range(tm // L):
                rows = slice(c * L, (c + 1) * L)
                s = jnp.dot(wc, vn_ref[rows, cols], preferred_element_type=F32) + bs_ref[:, cols]
                us_ref[rows, cols] = (u_ref[rows, cols] * s).astype(_MXU)
        _out_ln(us_ref[...], wo_ref, bias_ref, res_ref, alpha, g1_ref, b1_ref, y_ref, yb_ref, xh1_ref, rs1_ref)

    row = lambda i: (i, 0)
    fixed = lambda i: (0, 0)
    vd, td, one = pl.BlockSpec((1, D), fixed), pl.BlockSpec((tm, D), row), pl.BlockSpec((tm, 1), row)
    return pl.pallas_call(
        body, name=name, grid=(T // tm,),
        in_specs=[pl.BlockSpec((tm, C2), row), pl.BlockSpec((1, C), fixed), pl.BlockSpec((1, C), fixed),
                  pl.BlockSpec((G, L, L), lambda i: (0, 0, 0)), pl.BlockSpec((L, C), fixed),
                  _resident((None, C, D), lambda i: (0, 0, 0)), vd, td, vd, vd],
        out_specs=[pl.BlockSpec((tm, C), row), pl.BlockSpec((tm, C), row), one, td, td, td, one],
        out_shape=[jax.ShapeDtypeStruct((T, C), _MXU), jax.ShapeDtypeStruct((T, C), _XDT),
                   jax.ShapeDtypeStruct((T, 1), F32), jax.ShapeDtypeStruct((T, D), F32),
                   jax.ShapeDtypeStruct((T, D), _MXU), jax.ShapeDtypeStruct((T, D), _XDT),
                   jax.ShapeDtypeStruct((T, 1), F32)],
        scratch_shapes=[pltpu.VMEM((tm, C), _MXU), pltpu.VMEM((tm, C), F32)],
        compiler_params=_cp(("parallel",)),
    )(p, g, b, w_s, bsb, w_out, bias, res, g1, b1)


def _gmlp_gate_bwd(dzb, w_out, p, xh, rstd, g, b, w_s, bsb, *, name):
    T, C2 = p.shape
    D = dzb.shape[1]
    C = C2 // 2
    il = C // 2
    G, L, _ = w_s.shape
    tm = _tile(T, 4 * L, L)

    def body(dz_ref, wo_ref, p_ref, xh_ref, rs_ref, g_ref, b_ref, ws_ref, bs_ref,
             dp_ref, dg_ref, db_ref, cs_ref, dws_ref, dbs_ref, vn_ref, u_ref, dvn_ref, du_ref, dus_ref):
        @pl.when(pl.program_id(0) == 0)
        def _():
            dg_ref[...] = jnp.zeros_like(dg_ref)
            db_ref[...] = jnp.zeros_like(db_ref)
            cs_ref[...] = jnp.zeros_like(cs_ref)
            dws_ref[...] = jnp.zeros_like(dws_ref)
            dbs_ref[...] = jnp.zeros_like(dbs_ref)

        dus_ref[...] = lax.dot_general(dz_ref[...].astype(_MXU), wo_ref[...].astype(_MXU), (((1,), (1,)), ((), ())),
                                       preferred_element_type=F32)
        z, gp = _gelu_parts(p_ref[...].astype(F32))
        u, _ = _split_uv(z, il)
        xh = xh_ref[...].astype(F32)
        gam = g_ref[...]
        vn_ref[...] = (xh * gam + b_ref[...]).astype(_MXU)
        u_ref[...] = u
        mask = _tril_mask(L)
        for gi in range(G):
            wc = jnp.where(mask, ws_ref[gi], 0.0).astype(_MXU)
            cols = slice(gi * L, (gi + 1) * L)
            for c in range(tm // L):
                rows = slice(c * L, (c + 1) * L)
                vnb = vn_ref[rows, cols]
                s = jnp.dot(wc, vnb, preferred_element_type=F32) + bs_ref[:, cols]
                d = dus_ref[rows, cols]
                du_ref[rows, cols] = d * s
                ds = d * u_ref[rows, cols]
                dbs_ref[:, cols] += ds
                dsb = ds.astype(_MXU)
                dw = lax.dot_general(dsb, vnb, (((1,), (1,)), ((), ())), preferred_element_type=F32)
                dws_ref[gi] += jnp.where(mask, dw, 0.0)
                dvn_ref[rows, cols] = lax.dot_general(wc, dsb, (((0,), (0,)), ((), ())), preferred_element_type=F32)
        dvn = dvn_ref[...]
        dg_ref[...] += _fold8(dvn * xh)
        db_ref[...] += _fold8(dvn)
        dv = _ln_bwd_rows(dvn, xh, rs_ref[...], gam)
        du = du_ref[...]
        for hb in range(2):
            for part, src in ((0, du), (1, dv)):
                lo = (2 * hb + part) * il
                dp = src[:, hb * il:(hb + 1) * il] * gp[:, lo:lo + il]
                dp_ref[:, lo:lo + il] = dp.astype(_MXU)
                cs_ref[:, lo:lo + il] += _fold8(dp)

    row = lambda i: (i, 0)
    fixed = lambda i: (0, 0)
    part_c = pl.BlockSpec((SUBLANES, C), fixed)
    return pl.pallas_call(
        body, name=name, grid=(T // tm,),
        in_specs=[pl.BlockSpec((tm, D), row), _resident((None, C, D), lambda i: (0, 0, 0)),
                  pl.BlockSpec((tm, C2), row), pl.BlockSpec((tm, C), row),
                  pl.BlockSpec((tm, 1), row), pl.BlockSpec((1, C), fixed), pl.BlockSpec((1, C), fixed),
                  pl.BlockSpec((G, L, L), lambda i: (0, 0, 0)), pl.BlockSpec((L, C), fixed)],
        out_specs=[pl.BlockSpec((tm, C2), row), part_c, part_c, pl.BlockSpec((SUBLANES, C2), fixed),
                   pl.BlockSpec((G, L, L), lambda i: (0, 0, 0)), pl.BlockSpec((L, C), fixed)],
        out_shape=[jax.ShapeDtypeStruct((T, C2), _MXU), jax.ShapeDtypeStruct((SUBLANES, C), F32),
                   jax.ShapeDtypeStruct((SUBLANES, C), F32), jax.ShapeDtypeStruct((SUBLANES, C2), F32),
                   jax.ShapeDtypeStruct((G, L, L), F32), jax.ShapeDtypeStruct((L, C), F32)],
        scratch_shapes=[pltpu.VMEM((tm, C), _MXU), pltpu.VMEM((tm, C), F32), pltpu.VMEM((tm, C), F32),
                        pltpu.VMEM((tm, C), F32), pltpu.VMEM((tm, C), F32)],
        compiler_params=_cp(("arbitrary",)),
    )(dzb, w_out, p, xh, rstd, g, b, w_s, bsb)


def _ffn_conv(h, prev8, w_ref, b_ref):
    h1 = _shift_down(prev8, h, 1)
    h2 = _shift_down(prev8, h, 2)
    return w_ref[pl.ds(2, 1), :] * h + w_ref[pl.ds(1, 1), :] * h1 + w_ref[pl.ds(0, 1), :] * h2 + b_ref[...]


def _resident(block, imap):
    return pl.BlockSpec(block, imap, pipeline_mode=pl.Buffered(1))


def _ffn_fwd_half(j, xb, w_up, w_down, b_up, w_dw, b_dw, *, S, name, prev=None, tail=None, head=None):
    T, D = xb.shape
    N = w_up.shape[-1]
    tn = N // N_CHIPS
    tm = _tile(S, 256)
    spt = S // tm
    last = prev is not None
    alpha = tail[1] if last else None

    def body(*refs):
        x_ref, wu_ref, wd_ref, bu_ref, wc_ref, bc_ref = refs[:6]
        if last:
            yp_ref, res_ref, bd_ref, g_ref, b_ref = refs[9:14]
            o = 14 if head is None else 15
            h_ref, hc_ref, f_ref, y_ref, yb_ref, xh_ref, rs_ref = refs[o:o + 7]
            carry_ref = refs[-1]
        else:
            h_ref, hc_ref, f_ref, yp_ref, carry_ref = refs[6:11]

        @pl.when(pl.program_id(0) % spt == 0)
        def _():
            carry_ref[...] = jnp.zeros_like(carry_ref)

        h = jnp.dot(x_ref[...].astype(_MXU), wu_ref[...].astype(_MXU), preferred_element_type=F32) + bu_ref[...]
        h_ref[...] = h.astype(_HDT)
        hc = _ffn_conv(h, carry_ref[...], wc_ref, bc_ref)
        hc_ref[...] = hc.astype(_HDT)
        carry_ref[...] = h[tm - SUBLANES:tm]
        gte = hc[:, :tn]
        f = (gte * _sigmoid(gte) * hc[:, tn:]).astype(_MXU)
        f_ref[...] = f
        y = jnp.dot(f, wd_ref[...].astype(_MXU), preferred_element_type=F32)
        if not last:
            yp_ref[...] = y
            return
        z = y + yp_ref[...] + bd_ref[...] + alpha * res_ref[...]
        out, xh, rstd = _ln_rows(z, g_ref[...], b_ref[...])
        if head is None:
            y_ref[...] = out
            yb_ref[...] = out.astype(_MXU)
            xh_ref[...] = xh.astype(_XDT)
            rs_ref[...] = rstd
            return
        t_ref, cs_ref, ls_ref = refs[14], refs[o + 7], refs[o + 8]

        @pl.when(pl.program_id(0) == 0)
        def _():
            for acc in (xh_ref, rs_ref, cs_ref, ls_ref):
                acc[...] = jnp.zeros_like(acc)

        err = out - t_ref[...]
        d = err * (1.0 / D)
        dz = _ln_bwd_rows(d, xh, rstd, g_ref[...])
        y_ref[...] = dz
        yb_ref[...] = dz.astype(_MXU)
        xh_ref[...] += _fold8(d * xh)
        rs_ref[...] += _fold8(d)
        cs_ref[...] += _fold8(dz)
        ls_ref[...] += _fold8(err * err)

    row = lambda i: (i, 0)
    pair = lambda i: (0, j)
    vec = pl.BlockSpec((1, D), lambda i: (0, 0))
    tile = pl.BlockSpec((tm, D), row)
    in_specs = [tile, _resident((None, D, 2 * tn), lambda i: (0, 0, j)), _resident((None, tn, D), lambda i: (0, j, 0)),
                pl.BlockSpec((1, 2 * tn), pair), pl.BlockSpec((SUBLANES, 2 * tn), pair), pl.BlockSpec((1, 2 * tn), pair)]
    operands = [xb, w_up, w_down, b_up, w_dw, b_dw]
    wide = pl.BlockSpec((tm, 2 * tn), lambda i: (i, j))
    out_specs = [wide, wide, pl.BlockSpec((tm, tn), lambda i: (i, j))]
    out_shape = [jax.ShapeDtypeStruct((T, N), _HDT), jax.ShapeDtypeStruct((T, N), _HDT),
                 jax.ShapeDtypeStruct((T, N // 2), _MXU)]
    aliases = {}
    if last:
        res, _, b_down, g, b = tail
        in_specs += [ANY, ANY, ANY, tile, tile, vec, vec, vec]
        operands += list(prev) + [res, b_down, g, b]
        aliases = {6: 0, 7: 1, 8: 2}
        if head is None:
            out_specs += [tile, tile, tile, pl.BlockSpec((tm, 1), row)]
            out_shape += [jax.ShapeDtypeStruct((T, D), F32), jax.ShapeDtypeStruct((T, D), _MXU),
                          jax.ShapeDtypeStruct((T, D), _XDT), jax.ShapeDtypeStruct((T, 1), F32)]
        else:
            in_specs.append(tile)
            operands.append(head)
            part = pl.BlockSpec((SUBLANES, D), lambda i: (0, 0))
            out_specs += [tile, tile, part, part, part, part]
            out_shape += [jax.ShapeDtypeStruct((T, D), F32), jax.ShapeDtypeStruct((T, D), _MXU)] \
                + [jax.ShapeDtypeStruct((SUBLANES, D), F32)] * 4
    else:
        out_specs.append(tile)
        out_shape.append(jax.ShapeDtypeStruct((T, D), F32))
    return pl.pallas_call(
        body, name=name, grid=(T // tm,), in_specs=in_specs, out_specs=out_specs, out_shape=out_shape,
        input_output_aliases=aliases, scratch_shapes=[pltpu.VMEM((SUBLANES, 2 * tn), F32)],
        compiler_params=_cp(("arbitrary",)),
    )(*operands)


def _ffn_bwd_half(j, dzb, w_down, w_up, hs, hcs, w_dw, *, S, name, dz=None, alpha=None, prev=None, ln=None):
    T, D = dzb.shape
    N = hs.shape[1]
    tn = N // N_CHIPS
    tm = _tile(S, 256)
    spt = S // tm
    nt = T // tm
    last = prev is not None

    def body(*refs):
        dz_ref, wd_ref, wu_ref, h_ref, hc_ref, wc_ref = refs[:6]
        if last:
            dxp_ref, xh_ref, rs_ref, g_ref = refs[7:11]
            dh_ref, cs_ref, dw_ref, db_ref, dz1_ref, dz1b_ref, dg1_ref, db1_ref, cs1_ref, carry_ref = refs[11:21]
        else:
            dzf_ref = refs[6]
            dh_ref, cs_ref, dw_ref, db_ref, dxp_ref, carry_ref = refs[7:13]
        i = pl.program_id(0)
        ii = nt - 1 - i

        @pl.when(i == 0)
        def _():
            cs_ref[...] = jnp.zeros_like(cs_ref)
            dw_ref[...] = jnp.zeros_like(dw_ref)
            db_ref[...] = jnp.zeros_like(db_ref)
            if last:
                dg1_ref[...] = jnp.zeros_like(dg1_ref)
                db1_ref[...] = jnp.zeros_like(db1_ref)
                cs1_ref[...] = jnp.zeros_like(cs1_ref)

        df = lax.dot_general(dz_ref[...].astype(_MXU), wd_ref[...].astype(_MXU), (((1,), (1,)), ((), ())),
                             preferred_element_type=F32)
        h = h_ref[...].astype(F32)
        gte, val = hc_ref[:, :tn].astype(F32), hc_ref[:, tn:].astype(F32)
        sig = _sigmoid(gte)
        dval = df * (gte * sig)
        dg = df * val * (sig * (1.0 + gte * (1.0 - sig)))
        dhc = jnp.concatenate([dg, dval], axis=1)
        nxt = jnp.where((ii + 1) % spt == 0, 0.0, carry_ref[...])
        d1 = _shift_up(dhc, nxt, 1)
        d2 = _shift_up(dhc, nxt, 2)
        carry_ref[...] = dhc[0:SUBLANES]
        db_ref[...] += _fold8(dhc)
        dw_ref[2] += _fold8(dhc * h)
        dw_ref[1] += _fold8(d1 * h)
        dw_ref[0] += _fold8(d2 * h)
        dh = wc_ref[pl.ds(2, 1), :] * dhc + wc_ref[pl.ds(1, 1), :] * d1 + wc_ref[pl.ds(0, 1), :] * d2
        cs_ref[...] += _fold8(dh)
        dhb = dh.astype(_MXU)
        dh_ref[...] = dhb
        dx = lax.dot_general(dhb, wu_ref[...].astype(_MXU), (((1,), (1,)), ((), ())), preferred_element_type=F32)
        if not last:
            dxp_ref[...] = dx + alpha * dzf_ref[...]
            return
        d = dx + dxp_ref[...]
        xh = xh_ref[...].astype(F32)
        dz1 = _ln_bwd_rows(d, xh, rs_ref[...], g_ref[...])
        dz1_ref[...] = dz1
        dz1b_ref[...] = dz1.astype(_MXU)
        dg1_ref[...] += _fold8(d * xh)
        db1_ref[...] += _fold8(d)
        cs1_ref[...] += _fold8(dz1)

    rev = lambda i: (nt - 1 - i, 0)
    fixed = lambda i: (0, 0)
    pair = lambda i: (0, j)
    tile = pl.BlockSpec((tm, D), rev)
    wide = pl.BlockSpec((tm, 2 * tn), lambda i: (nt - 1 - i, j))
    part = pl.BlockSpec((SUBLANES, 2 * tn), fixed)
    in_specs = [tile, _resident((None, tn, D), lambda i: (0, j, 0)), _resident((None, D, 2 * tn), lambda i: (0, 0, j)),
                wide, wide, pl.BlockSpec((SUBLANES, 2 * tn), pair)]
    operands = [dzb, w_down, w_up, hs, hcs, w_dw]
    out_specs = [wide, part, pl.BlockSpec((3, SUBLANES, 2 * tn), lambda i: (0, 0, 0)), part]
    out_shape = [jax.ShapeDtypeStruct((T, N), _MXU), jax.ShapeDtypeStruct((SUBLANES, 2 * tn), F32),
                 jax.ShapeDtypeStruct((3, SUBLANES, 2 * tn), F32), jax.ShapeDtypeStruct((SUBLANES, 2 * tn), F32)]
    aliases = {}
    if last:
        xh, rstd, g = ln
        in_specs += [ANY, tile, tile, pl.BlockSpec((tm, 1), rev), pl.BlockSpec((1, D), fixed)]
        operands += [prev[0], prev[1], xh, rstd, g]
        aliases = {6: 0}
        out_specs += [tile, tile] + [pl.BlockSpec((SUBLANES, D), fixed)] * 3
        out_shape += [jax.ShapeDtypeStruct((T, D), F32), jax.ShapeDtypeStruct((T, D), _MXU)] \
            + [jax.ShapeDtypeStruct((SUBLANES, D), F32)] * 3
    else:
        in_specs.append(tile)
        operands.append(dz)
        out_specs.append(tile)
        out_shape.append(jax.ShapeDtypeStruct((T, D), F32))
    return pl.pallas_call(
        body, name=name, grid=(nt,), in_specs=in_specs, out_specs=out_specs, out_shape=out_shape,
        input_output_aliases=aliases, scratch_shapes=[pltpu.VMEM((SUBLANES, 2 * tn), F32)],
        compiler_params=_cp(("arbitrary",)),
    )(*operands)


def _sum_pieces(gs, rs, me, *, name):
    n = len(gs)
    _, pr, pc = gs[0].shape
    tr = _tile(pr, 128)

    def body(me_ref, *refs):
        o_ref = refs[2 * n]
        for l in range(n):
            total = refs[l][...].astype(F32)
            for s in range(N_DEV - 1):
                total = total + refs[n + l][s].astype(F32)
            o_ref[l] = total

    own = pl.BlockSpec((None, tr, pc), lambda i, me_ref: (me_ref[0], i, 0))
    got = pl.BlockSpec((N_DEV - 1, tr, pc), lambda i, me_ref: (0, i, 0))
    return pl.pallas_call(
        body, name=name,
        grid_spec=pltpu.PrefetchScalarGridSpec(
            num_scalar_prefetch=1, grid=(pr // tr,), in_specs=[own] * n + [got] * n,
            out_specs=pl.BlockSpec((n, tr, pc), lambda i, me_ref: (0, i, 0))),
        out_shape=jax.ShapeDtypeStruct((n, pr, pc), F32),
        compiler_params=_cp(("parallel",)),
    )(me, *gs, *rs)


def _adam_math(w, g, m, v):
    bc1 = 1.0 - ADAM_B1 ** ADAM_STEP
    bc2 = 1.0 - ADAM_B2 ** ADAM_STEP
    m = ADAM_B1 * m + (1.0 - ADAM_B1) * g
    v = ADAM_B2 * v + (1.0 - ADAM_B2) * (g * g)
    return -ADAM_LR * ((m / bc1) / (jnp.sqrt(v / bc2) + ADAM_EPS) + ADAM_WD * w), m, v


def _adam(w, g, m, v, *, name):
    R, C = w.shape
    tr = _tile(R, 256)

    def body(w_ref, g_ref, m_ref, v_ref, d_ref, mo_ref, vo_ref):
        d_ref[...], mo_ref[...], vo_ref[...] = _adam_math(w_ref[...], g_ref[...], m_ref[...], v_ref[...])

    spec = pl.BlockSpec((tr, C), lambda i: (i, 0))
    return pl.pallas_call(
        body, name=name, grid=(R // tr,), in_specs=[spec] * 4, out_specs=[spec] * 3,
        out_shape=[jax.ShapeDtypeStruct((R, C), F32)] * 3,
        compiler_params=_cp(("parallel",)),
    )(w, g, m, v)


def _adam_halves(w, own, got, m, v, core, *, name):
    L, R, C = w.shape
    rh = R // 2
    tr = _tile(rh, 256)
    nt = rh // tr

    def body(c_ref, w_ref, own_ref, got_ref, m_ref, v_ref, g_ref, d_ref, mo_ref, vo_ref):
        g = jnp.where(pl.program_id(1) == c_ref[0], own_ref[...], got_ref[...])
        g_ref[...] = g
        d_ref[...], mo_ref[...], vo_ref[...] = _adam_math(w_ref[...], g, m_ref[...], v_ref[...])

    full = pl.BlockSpec((None, tr, C), lambda l, h, t, c_ref: (l, h * nt + t, 0))
    half = pl.BlockSpec((None, tr, C), lambda l, h, t, c_ref: (l, t, 0))
    return pl.pallas_call(
        body, name=name,
        grid_spec=pltpu.PrefetchScalarGridSpec(
            num_scalar_prefetch=1, grid=(L, 2, nt), in_specs=[full, half, half, full, full], out_specs=[full] * 4),
        out_shape=[jax.ShapeDtypeStruct((L, R, C), F32)] * 4,
        compiler_params=_cp(("parallel", "parallel", "parallel")),
    )(core, w, own, got, m, v)


def _remote(src, dst, send, recv, dev):
    return pltpu.make_async_remote_copy(src_ref=src, dst_ref=dst, send_sem=send, recv_sem=recv,
                                        device_id=dev, device_id_type=MESH)


def _place_w(shard, pos, layer, *, axis, name):
    _, R, C = shard.shape
    tr = _tile(R, 512, 16)
    nt = R // tr
    if axis == 2:
        out_shape = (1, R, N_CHIPS * C)
        out_map = lambda t, q: (0, t, q[0])
    else:
        out_shape = (1, N_CHIPS * R, C)
        out_map = lambda t, q: (0, q[0] * nt + t, 0)

    def body(q_ref, s_ref, o_ref):
        o_ref[...] = s_ref[...].astype(_WIRE)

    return pl.pallas_call(
        body, name=name,
        grid_spec=pltpu.PrefetchScalarGridSpec(
            num_scalar_prefetch=1, grid=(nt,),
            in_specs=[pl.BlockSpec((None, tr, C), lambda t, q: (layer, t, 0))],
            out_specs=pl.BlockSpec((None, tr, C), out_map)),
        out_shape=jax.ShapeDtypeStruct(out_shape, _WIRE),
        compiler_params=_cp(("parallel",)),
    )(pos, shard)


def _ag_window(ref, kind, px, py, h):
    axis, perm = kind
    q = 2 * px + py
    if perm:
        q = _perm_idx(q)
    if axis == 2:
        R, C = ref.shape[1], ref.shape[2] // N_CHIPS
        rh = R // 2
        return ref.at[:, pl.ds(pl.multiple_of(h * rh, 16), rh), pl.ds(pl.multiple_of(q * C, LANES), C)]
    R = ref.shape[1] // N_CHIPS
    rh = R // 2
    return ref.at[:, pl.ds(pl.multiple_of(q * R + h * rh, 16), rh), :]


def _ag_ici_copies(refs, kinds, send, recv):
    x, y, c = lax.axis_index("x"), lax.axis_index("y"), lax.axis_index("c")
    chips = [(1 - x, y), (x, 1 - y), (1 - x, 1 - y)]
    sends, recvs = [], []
    for a, (ref, kind) in enumerate(zip(refs, kinds)):
        own = _ag_window(ref, kind, x, y, c)
        for i, (px, py) in enumerate(chips):
            k = 3 * a + i
            sends.append(_remote(own, own, send.at[k], recv.at[k], (px, py, c)))
            recvs.append(_remote(own, _ag_window(ref, kind, px, py, c), send.at[k], recv.at[k], (px, py, c)))
    return sends, recvs


def _ag_start(arrs, kinds, after, *, name):
    n = len(arrs)

    def body(*refs):
        in_refs = refs[:n]
        send, recv = refs[n + len(after)], refs[n + len(after) + 1]
        token = refs[-1]
        sends, _ = _ag_ici_copies(in_refs, kinds, send, recv)
        for cp in sends:
            cp.start()
        token[...] = jnp.zeros_like(token)

    sems = pltpu.SemaphoreType.DMA((3 * n,))
    out = pl.pallas_call(
        body, name=name,
        out_shape=(sems, sems) + tuple(pltpu.HBM(a.shape, a.dtype) for a in arrs)
        + (jax.ShapeDtypeStruct((SUBLANES, LANES), F32),),
        in_specs=(HBM,) * n + (ANY,) * len(after),
        out_specs=(SEMS, SEMS) + (HBM,) * n + (pl.BlockSpec(memory_space=pltpu.VMEM),),
        input_output_aliases={a: 2 + a for a in range(n)},
        compiler_params=pltpu.CompilerParams(has_side_effects=EFFECT),
    )(*[pltpu.with_memory_space_constraint(a, pltpu.HBM) for a in arrs], *after)
    return out[0], out[1], list(out[2:2 + n]), out[-1]


def _ag_wait(send, recv, arrs, kinds, after, *, name):
    n = len(arrs)

    def body(*refs):
        in_refs = refs[:n]
        send, recv = refs[n], refs[n + 1]
        sends, recvs = _ag_ici_copies(in_refs, kinds, send, recv)
        for cp in sends:
            cp.wait_send()
        for cp in recvs:
            cp.wait_recv()

    out = pl.pallas_call(
        body, name=name,
        out_shape=tuple(pltpu.HBM(a.shape, a.dtype) for a in arrs),
        in_specs=(HBM,) * n + (SEMS, SEMS) + (ANY,) * len(after), out_specs=(HBM,) * n,
        input_output_aliases={a: a for a in range(n)},
        compiler_params=pltpu.CompilerParams(has_side_effects=EFFECT),
    )(*arrs, send, recv, *after)
    return list(out)


def _ag_forward(arrs, kinds, *, name):
    n = len(arrs)

    def body(*refs):
        o_refs, send, recv = refs[n:2 * n], refs[2 * n], refs[2 * n + 1]
        x, y, c = lax.axis_index("x"), lax.axis_index("y"), lax.axis_index("c")
        chips = [(1 - x, y), (x, 1 - y), (1 - x, 1 - y)]
        sib = (x, y, 1 - c)
        sends, recvs = [], []
        for a, (ref, kind) in enumerate(zip(o_refs, kinds)):
            for i, (px, py) in enumerate(chips):
                k = 3 * a + i
                got = _ag_window(ref, kind, px, py, c)
                cp = _remote(got, got, send.at[k], recv.at[k], sib)
                cp.start()
                sends.append(cp)
                recvs.append(_remote(got, _ag_window(ref, kind, px, py, 1 - c), send.at[k], recv.at[k], sib))
        for cp in recvs:
            cp.wait_recv()
        for cp in sends:
            cp.wait_send()

    out = pl.pallas_call(
        body, name=name, in_specs=[ANY] * n, out_specs=[ANY] * n,
        out_shape=[jax.ShapeDtypeStruct(a.shape, a.dtype) for a in arrs],
        input_output_aliases={a: a for a in range(n)},
        scratch_shapes=[pltpu.SemaphoreType.DMA((3 * n,)), pltpu.SemaphoreType.DMA((3 * n,))],
    )(*arrs)
    return list(out)


def _flip(x, y, c, f):
    return ((1 - x) if f & 4 else x, (1 - y) if f & 2 else y, (1 - c) if f & 1 else c)


def _rs_copies(g_refs, land_refs, send, recv):
    x, y, c = lax.axis_index("x"), lax.axis_index("y"), lax.axis_index("c")
    cps = []
    for a, (g_ref, land_ref) in enumerate(zip(g_refs, land_refs)):
        for f in range(1, N_DEV):
            tx, ty, tcx = _flip(x, y, c, f)
            k = (N_DEV - 1) * a + f - 1
            cps.append(_remote(g_ref.at[4 * tx + 2 * ty + tcx], land_ref.at[f - 1], send.at[k], recv.at[k],
                               (tx, ty, tcx)))
    return cps


def _rs_start(gs, *, name):
    n = len(gs)
    lands = [lax.empty((N_DEV - 1,) + g.shape[1:], g.dtype) for g in gs]

    def body(*refs):
        send, recv, token = refs[2 * n], refs[2 * n + 1], refs[-1]
        for cp in _rs_copies(refs[:n], refs[n:2 * n], send, recv):
            cp.start()
        token[...] = jnp.zeros_like(token)

    sems = pltpu.SemaphoreType.DMA(((N_DEV - 1) * n,))
    thru = [pltpu.HBM(t.shape, t.dtype) for t in gs + lands]
    out = pl.pallas_call(
        body, name=name,
        out_shape=(sems, sems, *thru, jax.ShapeDtypeStruct((SUBLANES, LANES), F32)),
        in_specs=(HBM,) * (2 * n), out_specs=(SEMS, SEMS) + (HBM,) * (2 * n) + (pl.BlockSpec(memory_space=pltpu.VMEM),),
        input_output_aliases={a: 2 + a for a in range(2 * n)},
        compiler_params=pltpu.CompilerParams(has_side_effects=EFFECT),
    )(*[pltpu.with_memory_space_constraint(t, pltpu.HBM) for t in gs + lands])
    return out[0], out[1], list(out[2:2 + n]), list(out[2 + n:2 + 2 * n]), out[-1]


def _rs_wait(send, recv, gs, lands, after, *, name):
    n = len(gs)

    def body(*refs):
        cps = _rs_copies(refs[:n], refs[n:2 * n], refs[2 * n], refs[2 * n + 1])
        for cp in cps:
            cp.wait_send()
        for cp in cps:
            cp.wait_recv()

    out = pl.pallas_call(
        body, name=name,
        out_shape=tuple(pltpu.HBM(t.shape, t.dtype) for t in gs + lands),
        in_specs=(HBM,) * (2 * n) + (SEMS, SEMS, ANY), out_specs=(HBM,) * (2 * n),
        input_output_aliases={a: a for a in range(2 * n)},
        compiler_params=pltpu.CompilerParams(has_side_effects=EFFECT),
    )(*gs, *lands, send, recv, after)
    return list(out[:n]), list(out[n:])


def _pair_exchange(owns, *, name):
    n = len(owns)

    def body(*refs):
        send, recv = refs[2 * n], refs[2 * n + 1]
        x, y, c = lax.axis_index("x"), lax.axis_index("y"), lax.axis_index("c")
        cps = [_remote(refs[a], refs[n + a], send.at[a], recv.at[a], (x, y, 1 - c)) for a in range(n)]
        for cp in cps:
            cp.start()
        for cp in cps:
            cp.wait_recv()
        for cp in cps:
            cp.wait_send()

    return pl.pallas_call(
        body, name=name, in_specs=[ANY] * n, out_specs=[ANY] * n,
        out_shape=[jax.ShapeDtypeStruct(o.shape, o.dtype) for o in owns],
        scratch_shapes=[pltpu.SemaphoreType.DMA((n,)), pltpu.SemaphoreType.DMA((n,))],
    )(*owns)


def _allreduce_flat(vec, *, name):
    n = vec.shape[0]
    unit = N_DEV * SUBLANES * LANES
    npad = -(-n // unit) * unit
    rows = npad // (N_DEV * LANES)
    xin = jnp.pad(vec, (0, npad - n)).reshape(N_DEV, rows, LANES)

    def body(x_ref, y_ref, a_ref, send_a, recv_a, send_b, recv_b):
        x, y, c = lax.axis_index("x"), lax.axis_index("y"), lax.axis_index("c")
        me = 4 * x + 2 * y + c
        a_ref[me] = x_ref[me]
        sends, recvs = [], []
        for f in range(1, N_DEV):
            dev = _flip(x, y, c, f)
            t = 4 * dev[0] + 2 * dev[1] + dev[2]
            cp = _remote(x_ref.at[t], a_ref.at[me], send_a.at[f - 1], recv_a.at[f - 1], dev)
            cp.start()
            sends.append(cp)
            recvs.append(_remote(x_ref.at[me], a_ref.at[t], send_a.at[f - 1], recv_a.at[f - 1], dev))
        for cp in recvs:
            cp.wait_recv()
        for cp in sends:
            cp.wait_send()
        acc = a_ref[0]
        for s in range(1, N_DEV):
            acc = acc + a_ref[s]
        y_ref[me] = acc
        sends, recvs = [], []
        for f in range(1, N_DEV):
            dev = _flip(x, y, c, f)
            t = 4 * dev[0] + 2 * dev[1] + dev[2]
            cp = _remote(y_ref.at[me], y_ref.at[me], send_b.at[f - 1], recv_b.at[f - 1], dev)
            cp.start()
            sends.append(cp)
            recvs.append(_remote(y_ref.at[me], y_ref.at[t], send_b.at[f - 1], recv_b.at[f - 1], dev))
        for cp in recvs:
            cp.wait_recv()
        for cp in sends:
            cp.wait_send()

    vm = pl.BlockSpec(memory_space=pltpu.VMEM)
    out = pl.pallas_call(
        body, name=name, in_specs=[vm], out_specs=vm,
        out_shape=jax.ShapeDtypeStruct((N_DEV, rows, LANES), F32),
        scratch_shapes=[pltpu.VMEM((N_DEV, rows, LANES), F32)] + [pltpu.SemaphoreType.DMA((N_DEV - 1,))] * 4,
        compiler_params=_cp(),
    )(xin)
    return out.reshape(npad)[:n]


def _perm_cols(v, blocks=N_CHIPS):
    w = v.shape[-1] // blocks
    return jnp.concatenate([v[..., q * w:(q + 1) * w] for q in PERM], axis=-1)


def _pack(arrs):
    return jnp.concatenate([a.reshape(-1).astype(F32) for a in arrs])


def _unpack(flat, shapes):
    out, pos = [], 0
    for s in shapes:
        n = 1
        for d in s:
            n *= d
        out.append(flat[pos:pos + n].reshape(s))
        pos += n
    return out


def kernel(x, conv_w_in, conv_b_in, conv_w_dw, conv_b_dw, conv_ln_g, conv_ln_b, conv_w_out, conv_b_out, gmlp_w_in, gmlp_b_in, gmlp_ln_g, gmlp_ln_b, gmlp_w_s, gmlp_b_s, gmlp_w_out, gmlp_b_out, ffn_w_up, ffn_b_up, ffn_w_dw, ffn_b_dw, ffn_w_down, ffn_b_down, norm1_g, norm1_b, norm2_g, norm2_b, loss_target, m_conv_w_in, m_conv_b_in, m_conv_w_dw, m_conv_b_dw, m_conv_ln_g, m_conv_ln_b, m_conv_w_out, m_conv_b_out, m_gmlp_w_in, m_gmlp_b_in, m_gmlp_ln_g, m_gmlp_ln_b, m_gmlp_w_s, m_gmlp_b_s, m_gmlp_w_out, m_gmlp_b_out, m_ffn_w_up, m_ffn_b_up, m_ffn_w_dw, m_ffn_b_dw, m_ffn_w_down, m_ffn_b_down, m_norm1_g, m_norm1_b, m_norm2_g, m_norm2_b, v_conv_w_in, v_conv_b_in, v_conv_w_dw, v_conv_b_dw, v_conv_ln_g, v_conv_ln_b, v_conv_w_out, v_conv_b_out, v_gmlp_w_in, v_gmlp_b_in, v_gmlp_ln_g, v_gmlp_ln_b, v_gmlp_w_s, v_gmlp_b_s, v_gmlp_w_out, v_gmlp_b_out, v_ffn_w_up, v_ffn_b_up, v_ffn_w_dw, v_ffn_b_dw, v_ffn_w_down, v_ffn_b_down, v_norm1_g, v_norm1_b, v_norm2_g, v_norm2_b):
    P = dict(locals())
    WEIGHTS = ['conv_w_in', 'conv_b_in', 'conv_w_dw', 'conv_b_dw', 'conv_ln_g', 'conv_ln_b', 'conv_w_out',
               'conv_b_out', 'gmlp_w_in', 'gmlp_b_in', 'gmlp_ln_g', 'gmlp_ln_b', 'gmlp_w_s', 'gmlp_b_s',
               'gmlp_w_out', 'gmlp_b_out', 'ffn_w_up', 'ffn_b_up', 'ffn_w_dw', 'ffn_b_dw', 'ffn_w_down',
               'ffn_b_down', 'norm1_g', 'norm1_b', 'norm2_g', 'norm2_b']
    BIG = ['conv_w_in', 'conv_w_out', 'gmlp_w_in', 'gmlp_w_out', 'ffn_w_up', 'ffn_w_down']
    SMALL_SHARDED = {'conv_w_dw': 2, 'gmlp_b_in': 1, 'gmlp_ln_g': 1, 'gmlp_ln_b': 1, 'gmlp_b_out': 1, 'ffn_w_dw': 2}

    B, S, D = x.shape
    T = B * S
    depth = norm1_g.shape[0]
    alpha = (2.0 * depth) ** 0.25
    C = conv_w_out.shape[-1]
    F2 = ffn_b_up.shape[-1]
    G, L = gmlp_w_s.shape[1], gmlp_w_s.shape[2]
    xi, yi, ci = lax.axis_index("x"), lax.axis_index("y"), lax.axis_index("c")
    shard = 2 * xi + yi

    i32 = lambda v: jnp.reshape(v, (1,)).astype(jnp.int32)
    pos_plain, pos_perm = i32(shard), i32(_perm_idx(shard))
    me_id, core_id = i32(4 * xi + 2 * yi + ci), i32(ci)

    groups = []
    for i in range(depth):
        mix = 'conv' if i % 2 == 0 else 'gmlp'
        groups.append((f"{mix}{i // 2}", [(mix + '_w_in', i // 2, 2, True), (mix + '_w_out', i // 2, 1, False)]))
        groups.append((f"ffn{i}", [('ffn_w_up', i, 2, True), ('ffn_w_down', i, 1, False)]))
    sm_names = list(SMALL_SHARDED)
    sm_shapes = [P[n].shape for n in sm_names]
    mine = _pack([P[n] for n in sm_names]) * (ci == 0).astype(F32)
    buf = jnp.zeros((N_CHIPS, mine.shape[0]), F32)
    buf = lax.dynamic_update_slice(buf, mine[None], (shard, 0))
    gathered = _allreduce_flat(buf.reshape(-1), name="ag_small").reshape(N_CHIPS, -1)

    started, order = {}, [gathered]
    for gname, members in groups:
        placed = [_place_w(P[n], pos_perm if perm else pos_plain, l, axis=axis, name=f"place_{n}_{l}")
                  for n, l, axis, perm in members]
        kinds = [(axis, perm) for _, _, axis, perm in members]
        send, recv, arrs, token = _ag_start(placed, kinds, order, name=f"ag_start_{gname}")
        order = [token]
        started[gname] = (send, recv, arrs, kinds, [(n, l) for n, l, _, _ in members])
    wts = {}

    def arrive(gname, after):
        send, recv, arrs, kinds, keys = started[gname]
        arrs = _ag_wait(send, recv, arrs, kinds, after, name=f"ag_wait_{gname}")
        arrs = _ag_forward(arrs, kinds, name=f"ag_fwd_{gname}")
        wts.update(zip(keys, arrs))

    full = {}
    for n, parts in zip(sm_names, zip(*[_unpack(gathered[k], sm_shapes) for k in range(N_CHIPS)])):
        full[n] = jnp.concatenate(parts, axis=SMALL_SHARDED[n])
    for n in WEIGHTS:
        if n not in BIG and n not in full:
            full[n] = P[n]

    assert G * L == C, "a gMLP group must be as wide as a chunk is long"

    def row(v):
        return v.reshape(1, -1)

    def pad_rows(v, r):
        return jnp.pad(v, ((0, r - v.shape[0]), (0, 0)))

    xf = x.reshape(T, D)
    saved = []
    cur, cur_b = xf, xf.astype(_MXU)
    for i in range(depth):
        j = i // 2
        sv = {'x': cur, 'xb': cur_b}
        arrive(groups[2 * i][0], order if i == 0 else [cur_b])
        if i % 2 == 0:
            b_in = row(_perm_cols(full['conv_b_in'][j]))
            h1 = _mm(cur_b, wts['conv_w_in', j], bl=0, bias=b_in, tm=_tile(T, 1024), tn=_tile(2 * C, 1024, LANES),
                     tk=D, name=f"conv_in_{j}", n_outer=True, out_dtype=_ADT)
            wdw = pad_rows(full['conv_w_dw'][j], CONV_TAPS_PAD)
            dwo = _conv_fwd(h1, wdw, row(full['conv_b_dw'][j]), B=B, S=S, name=f"conv_dw_{j}")
            s_act, xhc, rsc, *y1 = _conv_tail_fwd(
                dwo, row(full['conv_ln_g'][j]), row(full['conv_ln_b'][j]), wts['conv_w_out', j],
                row(full['conv_b_out'][j]), cur, alpha, row(norm1_g[i]), row(norm1_b[i]), name=f"conv_out_ln_{j}")
            sv.update(h1=h1, wdw=wdw, act=s_act, xhc=xhc, rsc=rsc)
        else:
            b_in = row(_perm_cols(full['gmlp_b_in'][j]))
            pre = _mm(cur_b, wts['gmlp_w_in', j], bl=0, bias=b_in, tm=_tile(T, 1024), tn=_tile(2 * C, 1024, LANES),
                      tk=D, name=f"gmlp_in_{j}", n_outer=True, out_dtype=_ADT)
            bsb = jnp.repeat(gmlp_b_s[j].T, L, axis=1)
            us, xhv, rsv, *y1 = _gmlp_gate_fwd(
                pre, row(full['gmlp_ln_g'][j]), row(full['gmlp_ln_b'][j]), gmlp_w_s[j], bsb, wts['gmlp_w_out', j],
                row(full['gmlp_b_out'][j]), cur, alpha, row(norm1_g[i]), row(norm1_b[i]), name=f"gmlp_gate_{j}")
            sv.update(pre=pre, bsb=bsb, act=us, xhv=xhv, rsv=rsv)
        x1, x1b, xh1, rs1 = y1
        arrive(groups[2 * i + 1][0], [x1b])
        wdw3 = pad_rows(_perm_cols(full['ffn_w_dw'][i]), SUBLANES)
        bdw3 = row(_perm_cols(ffn_b_dw[i]))
        ffn_in = (x1b, wts['ffn_w_up', i], wts['ffn_w_down', i], row(_perm_cols(ffn_b_up[i])), wdw3, bdw3)
        first = _ffn_fwd_half(0, *ffn_in, S=S, name=f"ffn_fwd_a_{i}")
        ffn_tail = (x1, alpha, row(ffn_b_down[i]), row(norm2_g[i]), row(norm2_b[i]))
        sv.update(x1=x1, x1b=x1b, xh1=xh1, rs1=rs1, wdw3=wdw3)
        if i < depth - 1:
            hs, hcs, f_act, cur, cur_b, xh2, rs2 = _ffn_fwd_half(1, *ffn_in, S=S, name=f"ffn_fwd_b_{i}", prev=first,
                                                                 tail=ffn_tail)
            sv.update(xh2=xh2, rs2=rs2)
        else:
            hs, hcs, f_act, *sv['head'] = _ffn_fwd_half(1, *ffn_in, S=S, name=f"ffn_fwd_b_{i}", prev=first,
                                                         tail=ffn_tail, head=loss_target.reshape(T, D))
        sv.update(hs=hs, hcs=hcs, f=f_act)
        saved.append(sv)

    sg = {n: [None] * full[n].shape[0] for n in WEIGHTS if n not in BIG}
    inflight = {n: [None] * P[n].shape[0] for n in BIG}
    deps = []
    dcur = None
    loss_part = None
    tk_t = _tile(T, 2048)

    ready = []

    def wgrad(n, l, a_, b_, **kw):
        tk = T if n.endswith('w_in') else tk_t
        ready.append((n, l, _mm(a_, b_, ta=True, out_dtype=_WIRE, tk=tk, name=f"{n}_dw_{l}", deps=deps, **kw)))
        launch(f"{n}_{l}")

    def launch(gname):
        send, recv, gs, lands, token = _rs_start([g for _, _, g in ready], name=f"rs_start_{gname}")
        group = {'name': gname, 'flight': (send, recv, gs, lands), 'landed': None}
        for a, (n, l, _) in enumerate(ready):
            inflight[n][l] = (group, a)
        del ready[:]
        deps.append(token)

    def landed(n, l):
        group, a = inflight[n][l]
        if group['landed'] is None:
            group['landed'] = _rs_wait(*group['flight'], dcur, name=f"rs_wait_{group['name']}")
        return group['landed'][0][a], group['landed'][1][a]

    for i in reversed(range(depth)):
        j = i // 2
        sv = saved[i]
        if i == depth - 1:
            dz2, dz2b, dg, db, cs, loss_part = sv['head']
        else:
            dz2, dz2b, dg, db, cs = dcur
        sg['norm2_g'][i], sg['norm2_b'][i], sg['ffn_b_down'][i] = dg.sum(0), db.sum(0), cs.sum(0)
        Fh = F2 // 2
        wgrad('ffn_w_down', i, sv['f'], dz2b, tm=Fh // 2, tn=_tile(D, 1024, LANES), pieces=('row',))
        ffn_in = (dz2b, wts['ffn_w_down', i], wts['ffn_w_up', i], sv['hs'], sv['hcs'], sv['wdw3'])
        dh0, csu0, dwd0, dbd0, dxp = _ffn_bwd_half(0, *ffn_in, S=S, name=f"ffn_bwd_a_{i}", dz=dz2, alpha=alpha)
        dh, csu1, dwd1, dbd1, dz1, dz1b, dg, db, cs = _ffn_bwd_half(
            1, *ffn_in, S=S, name=f"ffn_bwd_b_{i}", prev=(dh0, dxp), ln=(sv['xh1'], sv['rs1'], row(norm1_g[i])))
        sg['ffn_b_up'][i] = _perm_cols(jnp.concatenate([csu0.sum(0), csu1.sum(0)], axis=-1))
        sg['ffn_w_dw'][i] = _perm_cols(jnp.concatenate([dwd0.sum(1), dwd1.sum(1)], axis=-1))
        sg['ffn_b_dw'][i] = _perm_cols(jnp.concatenate([dbd0.sum(0), dbd1.sum(0)], axis=-1))
        wgrad('ffn_w_up', i, sv['x1b'], dh, tm=D, tn=F2 // N_CHIPS, pieces=('col', True))
        sg['norm1_g'][i], sg['norm1_b'][i] = dg.sum(0), db.sum(0)
        if i % 2 == 0:
            sg['conv_b_out'][j] = cs.sum(0)
            wgrad('conv_w_out', j, sv['act'], dz1b, tm=_tile(C, 1024), tn=_tile(D, 1024, LANES), pieces=('row',))
            ddw, dg, db = _ln_silu_bwd(dz1b, wts['conv_w_out', j], sv['xhc'], sv['rsc'], row(full['conv_ln_g'][j]),
                                       row(full['conv_ln_b'][j]), name=f"conv_ln_bwd_{j}")
            sg['conv_ln_g'][j], sg['conv_ln_b'][j] = dg.sum(0), db.sum(0)
            dglu, dwk, dbk = _conv_bwd(ddw, sv['h1'], sv['wdw'], B=B, S=S, name=f"conv_dw_bwd_{j}")
            sg['conv_w_dw'][j] = dwk.sum(1)[:conv_w_dw.shape[1]]
            sg['conv_b_dw'][j] = dbk.sum(0)
            dh1, csi = _glu_bwd(dglu, sv['h1'], name=f"conv_glu_bwd_{j}")
            sg['conv_b_in'][j] = _perm_cols(csi.sum(0))
            fam = 'conv_w_in'
        else:
            sg['gmlp_b_out'][j] = cs.sum(0)
            wgrad('gmlp_w_out', j, sv['act'], dz1b, tm=_tile(C, 1024), tn=_tile(D, 1024, LANES), pieces=('row',))
            dh1, dg, db, csi, dws, dbs = _gmlp_gate_bwd(dz1b, wts['gmlp_w_out', j], sv['pre'], sv['xhv'], sv['rsv'],
                                                        row(full['gmlp_ln_g'][j]), row(full['gmlp_ln_b'][j]),
                                                        gmlp_w_s[j], sv['bsb'], name=f"gmlp_gate_bwd_{j}")
            sg['gmlp_ln_g'][j], sg['gmlp_ln_b'][j] = dg.sum(0), db.sum(0)
            sg['gmlp_b_in'][j] = _perm_cols(csi.sum(0))
            sg['gmlp_w_s'][j] = dws
            sg['gmlp_b_s'][j] = dbs.reshape(L, G, L).sum(-1).T
            fam = 'gmlp_w_in'
        wgrad(fam, j, sv['xb'], dh1, tm=D, tn=(2 * C) // N_CHIPS, pieces=('col', True))
        if i > 0:
            below = saved[i - 1]
            dcur = _mm_ln_bwd(dh1, wts[fam, j], dz1, alpha, below['xh2'], below['rs2'], row(norm2_g[i - 1]),
                              name=f"{fam}_dx_{j}", deps=deps)
        else:
            dcur = _mm(dh1, wts[fam, j], bl=0, tb=True, res=dz1, res_scale=alpha, tm=_tile(T, 512),
                       tn=_tile(D, 1024, LANES), tk=2 * C, name=f"{fam}_dx_{j}", deps=deps)
    grad_x = dcur.reshape(B, S, D)

    small_names = [n for n in WEIGHTS if n not in BIG]
    small_full = [jnp.stack(sg[n]) for n in small_names]
    flat = _pack(small_full + [loss_part])
    red = _allreduce_flat(flat, name="ar_small")
    red_parts = _unpack(red, [a.shape for a in small_full] + [loss_part.shape])
    loss = (0.5 / D) * jnp.sum(red_parts[-1])
    grads = {}
    for n, g in zip(small_names, red_parts[:-1]):
        if n in SMALL_SHARDED:
            ax = SMALL_SHARDED[n]
            width = P[n].shape[ax]
            g = lax.dynamic_slice_in_dim(g, shard * width, width, axis=ax)
        grads[n] = g

    big_out = {}
    for n in ['ffn_w_down', 'ffn_w_up', 'gmlp_w_out', 'gmlp_w_in', 'conv_w_out', 'conv_w_in']:
        both = [landed(n, l) for l in range(len(inflight[n]))]
        own = _sum_pieces([g for g, _ in both], [r for _, r in both], me_id, name=f"sum_{n}")
        got, = _pair_exchange([own], name=f"px_{n}")
        big_out[n] = _adam_halves(P[n], own, got, P['m_' + n], P['v_' + n], core_id, name=f"adam_{n}")

    shapes = [P[n].shape for n in small_names]
    n_small = sum(functools.reduce(lambda p_, d_: p_ * d_, s_, 1) for s_ in shapes)
    unit = SUBLANES * LANES
    npad = -(-n_small // unit) * unit

    def flat2d(arrs, fill=0.0):
        v = _pack(arrs)
        return jnp.pad(v, (0, npad - n_small), constant_values=fill).reshape(-1, LANES)

    dl, mo, vo = _adam(flat2d([P[n] for n in small_names]), flat2d([grads[n] for n in small_names]),
                       flat2d([P['m_' + n] for n in small_names]),
                       flat2d([P['v_' + n] for n in small_names], fill=1.0), name="adam_small")
    small_out = {n: [grads[n], None, None, None] for n in small_names}
    for k, t in enumerate((dl, mo, vo)):
        for n, a in zip(small_names, _unpack(t.reshape(-1), shapes)):
            small_out[n][k + 1] = a

    outs = [loss, grad_x]
    for k in range(4):
        for n in WEIGHTS:
            outs.append(big_out[n][k] if n in BIG else small_out[n][k])
    return tuple(outs)
```

```python
import functools

import jax
import jax.numpy as jnp
from jax import lax
from jax.experimental import pallas as pl
from jax.experimental.pallas import tpu as pltpu

F32 = jnp.float32
_MXU = jnp.bfloat16
_WIRE = jnp.bfloat16
_HDT = jnp.bfloat16
_ADT = jnp.bfloat16
_XDT = jnp.bfloat16
LN_EPS = 1e-5
ADAM_LR, ADAM_B1, ADAM_B2, ADAM_EPS, ADAM_WD, ADAM_STEP = 0.001, 0.9, 0.999, 1e-08, 0.01, 10
N_CHIPS = 4
N_DEV = 8
LANES = 128
SUBLANES = 8
CONV_TAPS_PAD = 32
VMEM_LIMIT = 56 << 20
MESH = pl.DeviceIdType.MESH
ANY = pl.BlockSpec(memory_space=pl.ANY)
HBM = pl.BlockSpec(memory_space=pltpu.HBM)
SEMS = pl.BlockSpec(memory_space=pltpu.SEMAPHORE)
EFFECT = pltpu.SideEffectType.DATAFLOW_SIDE_EFFECTING
PERM = (0, 2, 1, 3)


def _cp(sem=None):
    return pltpu.CompilerParams(dimension_semantics=sem, vmem_limit_bytes=VMEM_LIMIT)


def _tile(dim, pref, mult=SUBLANES):
    if dim <= pref:
        return dim
    t = (pref // mult) * mult
    while t > mult and dim % t:
        t -= mult
    assert dim % t == 0, (dim, pref, mult)
    return t


def _perm_idx(q):
    return (q % 2) * 2 + q // 2


def _fold8(t):
    r, n = t.shape
    return t.reshape(r // SUBLANES, SUBLANES, n).sum(axis=0)


def _ln_rows(z, g, b):
    mu = jnp.mean(z, axis=-1, keepdims=True)
    xc = z - mu
    var = jnp.mean(xc * xc, axis=-1, keepdims=True)
    rstd = lax.rsqrt(var + LN_EPS)
    xh = xc * rstd
    return xh * g + b, xh, rstd


def _ln_bwd_rows(dy, xh, rstd, g):
    dxh = dy * g
    m1 = jnp.mean(dxh, axis=-1, keepdims=True)
    m2 = jnp.mean(dxh * xh, axis=-1, keepdims=True)
    return rstd * (dxh - m1 - xh * m2)


def _sigmoid(v):
    return 0.5 * jnp.tanh(0.5 * v) + 0.5


def _gelu_parts(p):
    cdf = 0.5 * (1.0 + lax.erf(p * 0.7071067811865476))
    pdf = jnp.exp(-0.5 * p * p) * 0.3989422804014327
    return p * cdf, cdf + p * pdf


def _shift_down(prev8, t, s):
    ext = jnp.concatenate([prev8, t], axis=0)
    return pltpu.roll(ext, s, 0)[SUBLANES:]


def _shift_up(t, next8, s):
    n = t.shape[0]
    ext = jnp.concatenate([t, next8], axis=0)
    return pltpu.roll(ext, n + SUBLANES - s, 0)[:n]


def _mm(a, b, *, ta=False, tb=False, bl=None, bias=None, res=None, res_scale=1.0, out_dtype=F32,
        tm, tn, tk, name, pieces=None, deps=None, n_outer=False):
    M, K = (a.shape[1], a.shape[0]) if ta else a.shape
    bs = b.shape[1:] if bl is not None else b.shape
    N, Kb = (bs[0], bs[1]) if tb else (bs[1], bs[0])
    assert K == Kb and M % tm == 0 and N % tn == 0 and K % tk == 0, (a.shape, b.shape, tm, tn, tk)
    gm, gn, gk = M // tm, N // tn, K // tk

    def spec(block, imap):
        if n_outer:
            return pl.BlockSpec(block, lambda j, i, k: imap(i, j, k))
        return pl.BlockSpec(block, imap)

    a_spec = spec((tk, tm), lambda i, j, k: (k, i)) if ta else spec((tm, tk), lambda i, j, k: (i, k))
    bblk = (tn, tk) if tb else (tk, tn)
    bmap = (lambda i, j, k: (j, k)) if tb else (lambda i, j, k: (k, j))
    if bl is not None:
        b_spec = spec((None,) + bblk, lambda i, j, k: (bl,) + bmap(i, j, k))
    else:
        b_spec = spec(bblk, bmap)
    in_specs, operands = [a_spec, b_spec], [a, b]
    if bias is not None:
        in_specs.append(spec((1, tn), lambda i, j, k: (0, j)))
        operands.append(bias)
    if res is not None:
        in_specs.append(spec((tm, tn), lambda i, j, k: (i, j)))
        operands.append(res)
    n_dep = len(deps) if deps else 0
    if n_dep:
        in_specs += [ANY] * n_dep
        operands += deps
        del deps[:]
    if pieces is None:
        out_shape = jax.ShapeDtypeStruct((M, N), out_dtype)
        out_spec = spec((tm, tn), lambda i, j, k: (i, j))
        ppb = pr = None
    elif pieces[0] == 'col':
        pr, pc = M // 2, N // N_CHIPS
        assert tm % pr == 0 and pc % tn == 0
        ppb, per = tm // pr, pc // tn
        perm = pieces[1]
        out_shape = jax.ShapeDtypeStruct((N_DEV, pr, pc), out_dtype)
        out_spec = spec(
            (ppb, pr, tn),
            lambda i, j, k: ((2 * (_perm_idx(j // per) if perm else j // per)) // ppb + i, 0, j % per))
    else:
        pr = M // N_DEV
        assert tm % pr == 0
        ppb = tm // pr
        out_shape = jax.ShapeDtypeStruct((N_DEV, pr, N), out_dtype)
        out_spec = spec((ppb, pr, tn), lambda i, j, k: (i, 0, j))
    dims = (((0 if ta else 1,), (1 if tb else 0,)), ((), ()))

    def body(*refs):
        a_ref, b_ref = refs[0], refs[1]
        pos = 2
        bias_ref = res_ref = None
        if bias is not None:
            bias_ref = refs[pos]
            pos += 1
        if res is not None:
            res_ref = refs[pos]
            pos += 1
        pos += n_dep
        o_ref = refs[pos]

        def finish(r):
            if bias_ref is not None:
                r = r + bias_ref[...]
            if res_ref is not None:
                r = r + res_scale * res_ref[...]
            if pieces is not None:
                r = r.reshape(ppb, pr, tn)
            o_ref[...] = r.astype(out_dtype)

        part = lax.dot_general(a_ref[...].astype(_MXU), b_ref[...].astype(_MXU), dims, preferred_element_type=F32)
        if gk == 1:
            finish(part)
            return
        acc_ref = refs[pos + 1]
        k = pl.program_id(2)

        @pl.when(k == 0)
        def _():
            acc_ref[...] = part

        @pl.when((k > 0) & (k < gk - 1))
        def _():
            acc_ref[...] += part

        @pl.when(k == gk - 1)
        def _():
            finish(acc_ref[...] + part)

    return pl.pallas_call(
        body, name=name, grid=(gn, gm, gk) if n_outer else (gm, gn, gk), in_specs=in_specs, out_specs=out_spec,
        out_shape=out_shape, scratch_shapes=[pltpu.VMEM((tm, tn), F32)] if gk > 1 else [],
        compiler_params=_cp(("parallel", "parallel", "arbitrary")),
    )(*operands)


def _mm_ln_bwd(a, w, res, res_scale, xh, rstd, g, *, name, deps=None):
    T, K = a.shape
    D = w.shape[1]
    tm = _tile(T, 512)
    n_dep = len(deps) if deps else 0

    def body(a_ref, w_ref, res_ref, xh_ref, rs_ref, g_ref, *rest):
        dz_ref, dzb_ref, dg_ref, db_ref, cs_ref = rest[n_dep:]

        @pl.when(pl.program_id(0) == 0)
        def _():
            dg_ref[...] = jnp.zeros_like(dg_ref)
            db_ref[...] = jnp.zeros_like(db_ref)
            cs_ref[...] = jnp.zeros_like(cs_ref)

        d = lax.dot_general(a_ref[...].astype(_MXU), w_ref[...].astype(_MXU), (((1,), (1,)), ((), ())),
                            preferred_element_type=F32) + res_scale * res_ref[...]
        xh = xh_ref[...].astype(F32)
        dz = _ln_bwd_rows(d, xh, rs_ref[...], g_ref[...])
        dz_ref[...] = dz
        dzb_ref[...] = dz.astype(_MXU)
        dg_ref[...] += _fold8(d * xh)
        db_ref[...] += _fold8(d)
        cs_ref[...] += _fold8(dz)

    row = lambda i: (i, 0)
    fixed = lambda i: (0, 0)
    tile = pl.BlockSpec((tm, D), row)
    part = pl.BlockSpec((SUBLANES, D), fixed)
    operands = [a, w, res, xh, rstd, g] + (list(deps) if deps else [])
    if deps:
        del deps[:]
    return pl.pallas_call(
        body, name=name, grid=(T // tm,),
        in_specs=[pl.BlockSpec((tm, K), row),
                  pl.BlockSpec((None, D, K), lambda i: (0, 0, 0), pipeline_mode=pl.Buffered(1)),
                  tile, tile, pl.BlockSpec((tm, 1), row), pl.BlockSpec((1, D), fixed)] + [ANY] * n_dep,
        out_specs=[tile, tile, part, part, part],
        out_shape=[jax.ShapeDtypeStruct((T, D), F32), jax.ShapeDtypeStruct((T, D), _MXU)]
        + [jax.ShapeDtypeStruct((SUBLANES, D), F32)] * 3,
        compiler_params=_cp(("arbitrary",)),
    )(*operands)


def _out_ln(act, wo_ref, bias_ref, res_ref, alpha, g_ref, b_ref, y_ref, yb_ref, xh_ref, rs_ref):
    z = jnp.dot(act, wo_ref[...].astype(_MXU), preferred_element_type=F32) + bias_ref[...] + alpha * res_ref[...]
    y, xh, rstd = _ln_rows(z, g_ref[...], b_ref[...])
    y_ref[...] = y
    yb_ref[...] = y.astype(_MXU)
    xh_ref[...] = xh.astype(_XDT)
    rs_ref[...] = rstd


def _conv_tail_fwd(v, gc, bc, w, bias, res, alpha, g, b, *, name):
    T, C = v.shape
    D = w.shape[-1]
    tm = _tile(T, 512)

    def body(v_ref, gc_ref, bc_ref, w_ref, bias_ref, res_ref, g_ref, b_ref,
             s_ref, xhc_ref, rsc_ref, y_ref, yb_ref, xh_ref, rs_ref):
        yv, xhc, rsc = _ln_rows(v_ref[...], gc_ref[...], bc_ref[...])
        s = (yv * _sigmoid(yv)).astype(_MXU)
        s_ref[...] = s
        xhc_ref[...] = xhc.astype(_XDT)
        rsc_ref[...] = rsc
        _out_ln(s, w_ref, bias_ref, res_ref, alpha, g_ref, b_ref, y_ref, yb_ref, xh_ref, rs_ref)

    row = lambda i: (i, 0)
    fixed = lambda i: (0, 0)
    vc, vd = pl.BlockSpec((1, C), fixed), pl.BlockSpec((1, D), fixed)
    tc_, td = pl.BlockSpec((tm, C), row), pl.BlockSpec((tm, D), row)
    one = pl.BlockSpec((tm, 1), row)
    return pl.pallas_call(
        body, name=name, grid=(T // tm,),
        in_specs=[tc_, vc, vc, _resident((None, C, D), lambda i: (0, 0, 0)), vd, td, vd, vd],
        out_specs=[tc_, tc_, one, td, td, td, one],
        out_shape=[jax.ShapeDtypeStruct((T, C), _MXU), jax.ShapeDtypeStruct((T, C), _XDT),
                   jax.ShapeDtypeStruct((T, 1), F32), jax.ShapeDtypeStruct((T, D), F32),
                   jax.ShapeDtypeStruct((T, D), _MXU), jax.ShapeDtypeStruct((T, D), _XDT),
                   jax.ShapeDtypeStruct((T, 1), F32)],
        compiler_params=_cp(("parallel",)),
    )(v, gc, bc, w, bias, res, g, b)


def _conv_cols(C, tc):
    per = (C // 2) // tc
    return per, (lambda j: (j // per) * (2 * per) + j % per)


def _glu_shifted(a_ref, g_ref, p_ref, S):
    u = a_ref[...].astype(F32) * _sigmoid(g_ref[...].astype(F32))
    rows = lax.broadcasted_iota(jnp.int32, (SUBLANES, u.shape[1]), 0)
    lo = CONV_TAPS_PAD
    for r in range(SUBLANES):
        p_ref[r, 0:lo, :] = jnp.zeros((lo, u.shape[1]), F32)
        if r == 0:
            p_ref[r, lo:lo + S, :] = u
        else:
            rolled = pltpu.roll(u, r, 0)
            p_ref[r, lo:lo + S, :] = rolled
            p_ref[r, lo:lo + SUBLANES, :] = jnp.where(rows >= r, rolled[0:SUBLANES], 0.0)


def _conv_fwd(h1, w_dw, b_dw, *, B, S, name):
    C = w_dw.shape[1]
    taps = CONV_TAPS_PAD - 1
    tc = LANES
    ch = _tile(S, 128)
    per, col_a = _conv_cols(C, tc)

    def body(a_ref, g_ref, w_ref, b_ref, o_ref, p_ref):
        _glu_shifted(a_ref, g_ref, p_ref, S)

        def chunk(ci, carry):
            base = pl.multiple_of(ci * ch, ch)
            acc = jnp.zeros((ch, tc), F32) + b_ref[...]
            for k in range(taps):
                q, r = divmod(taps - 1 - k, SUBLANES)
                start = pl.multiple_of(base + (CONV_TAPS_PAD - SUBLANES * q), SUBLANES)
                acc = acc + w_ref[pl.ds(k, 1), :] * p_ref[r, pl.ds(start, ch), :]
            o_ref[pl.ds(base, ch), :] = acc
            return carry

        lax.fori_loop(0, S // ch, chunk, 0)

    return pl.pallas_call(
        body, name=name, grid=(B, C // tc),
        in_specs=[pl.BlockSpec((S, tc), lambda b, j: (b, col_a(j))),
                  pl.BlockSpec((S, tc), lambda b, j: (b, col_a(j) + per)),
                  pl.BlockSpec((CONV_TAPS_PAD, tc), lambda b, j: (0, j)),
                  pl.BlockSpec((1, tc), lambda b, j: (0, j))],
        out_specs=pl.BlockSpec((S, tc), lambda b, j: (b, j)),
        out_shape=jax.ShapeDtypeStruct((B * S, C), F32),
        scratch_shapes=[pltpu.VMEM((SUBLANES, S + CONV_TAPS_PAD, tc), F32)],
        compiler_params=_cp(("parallel", "parallel")),
    )(h1, h1, w_dw, b_dw)


def _conv_bwd(dd, h1, w_dw, *, B, S, name):
    C = w_dw.shape[1]
    taps = CONV_TAPS_PAD - 1
    tc = LANES
    ch = _tile(S, 128)
    per, col_a = _conv_cols(C, tc)

    def body(d_ref, a_ref, g_ref, w_ref, du_ref, dw_ref, db_ref, p_ref, q_ref):
        b = pl.program_id(1)

        @pl.when(b == 0)
        def _():
            dw_ref[...] = jnp.zeros_like(dw_ref)
            db_ref[...] = jnp.zeros_like(db_ref)

        _glu_shifted(a_ref, g_ref, p_ref, S)
        d = d_ref[...]
        rows = lax.broadcasted_iota(jnp.int32, (SUBLANES, tc), 0)
        for r in range(SUBLANES):
            q_ref[r, S:S + CONV_TAPS_PAD, :] = jnp.zeros((CONV_TAPS_PAD, tc), F32)
            if r == 0:
                q_ref[r, 0:S, :] = d
            else:
                rolled = pltpu.roll(d, S - r, 0)
                q_ref[r, 0:S, :] = rolled
                q_ref[r, S - SUBLANES:S, :] = jnp.where(rows < SUBLANES - r, rolled[S - SUBLANES:S], 0.0)
        db_ref[...] += _fold8(d)

        def chunk(ci, carry):
            base = pl.multiple_of(ci * ch, ch)
            dch = d_ref[pl.ds(base, ch), :]
            acc = jnp.zeros((ch, tc), F32)
            for k in range(taps):
                q, r = divmod(taps - 1 - k, SUBLANES)
                up = pl.multiple_of(base + SUBLANES * q, SUBLANES)
                acc = acc + w_ref[pl.ds(k, 1), :] * q_ref[r, pl.ds(up, ch), :]
                down = pl.multiple_of(base + (CONV_TAPS_PAD - SUBLANES * q), SUBLANES)
                dw_ref[k] += _fold8(dch * p_ref[r, pl.ds(down, ch), :])
            du_ref[pl.ds(base, ch), :] = acc
            return carry

        lax.fori_loop(0, S // ch, chunk, 0)

    return pl.pallas_call(
        body, name=name, grid=(C // tc, B),
        in_specs=[pl.BlockSpec((S, tc), lambda j, b: (b, j)),
                  pl.BlockSpec((S, tc), lambda j, b: (b, col_a(j))),
                  pl.BlockSpec((S, tc), lambda j, b: (b, col_a(j) + per)),
                  pl.BlockSpec((CONV_TAPS_PAD, tc), lambda j, b: (0, j))],
        out_specs=[pl.BlockSpec((S, tc), lambda j, b: (b, j)),
                   pl.BlockSpec((CONV_TAPS_PAD, SUBLANES, tc), lambda j, b: (0, 0, j)),
                   pl.BlockSpec((SUBLANES, tc), lambda j, b: (0, j))],
        out_shape=[jax.ShapeDtypeStruct((B * S, C), F32),
                   jax.ShapeDtypeStruct((CONV_TAPS_PAD, SUBLANES, C), F32),
                   jax.ShapeDtypeStruct((SUBLANES, C), F32)],
        scratch_shapes=[pltpu.VMEM((SUBLANES, S + CONV_TAPS_PAD, tc), F32),
                        pltpu.VMEM((SUBLANES, S + CONV_TAPS_PAD, tc), F32)],
        compiler_params=_cp(("parallel", "arbitrary")),
    )(dd, h1, h1, w_dw)


def _ln_silu_bwd(dzb, w, xh, rstd, g, b, *, name):
    T, D = dzb.shape
    C = w.shape[1]
    tm = _tile(T, 512)

    def body(dz_ref, w_ref, xh_ref, rs_ref, g_ref, b_ref, dv_ref, dg_ref, db_ref):
        @pl.when(pl.program_id(0) == 0)
        def _():
            dg_ref[...] = jnp.zeros_like(dg_ref)
            db_ref[...] = jnp.zeros_like(db_ref)

        ds = lax.dot_general(dz_ref[...].astype(_MXU), w_ref[...].astype(_MXU), (((1,), (1,)), ((), ())),
                             preferred_element_type=F32)
        xh = xh_ref[...].astype(F32)
        gam = g_ref[...]
        y = xh * gam + b_ref[...]
        sig = _sigmoid(y)
        dln = ds * (sig * (1.0 + y * (1.0 - sig)))
        dv_ref[...] = _ln_bwd_rows(dln, xh, rs_ref[...], gam)
        dg_ref[...] += _fold8(dln * xh)
        db_ref[...] += _fold8(dln)

    row = lambda i: (i, 0)
    fixed = lambda i: (0, 0)
    vec = pl.BlockSpec((1, C), fixed)
    part = pl.BlockSpec((SUBLANES, C), fixed)
    return pl.pallas_call(
        body, name=name, grid=(T // tm,),
        in_specs=[pl.BlockSpec((tm, D), row), _resident((None, C, D), lambda i: (0, 0, 0)),
                  pl.BlockSpec((tm, C), row), pl.BlockSpec((tm, 1), row), vec, vec],
        out_specs=[pl.BlockSpec((tm, C), row), part, part],
        out_shape=[jax.ShapeDtypeStruct((T, C), F32)] + [jax.ShapeDtypeStruct((SUBLANES, C), F32)] * 2,
        compiler_params=_cp(("arbitrary",)),
    )(dzb, w, xh, rstd, g, b)


def _glu_bwd(du, h1, *, name):
    T, C = du.shape
    il = C // 2
    tm = _tile(T, 512)

    def body(du_ref, h_ref, dh_ref, cs_ref):
        @pl.when(pl.program_id(0) == 0)
        def _():
            cs_ref[...] = jnp.zeros_like(cs_ref)

        for hb in range(2):
            a = h_ref[:, 2 * hb * il:(2 * hb + 1) * il].astype(F32)
            gate = h_ref[:, (2 * hb + 1) * il:(2 * hb + 2) * il].astype(F32)
            d = du_ref[:, hb * il:(hb + 1) * il]
            sig = _sigmoid(gate)
            da = d * sig
            dgate = d * a * sig * (1.0 - sig)
            dh_ref[:, 2 * hb * il:(2 * hb + 1) * il] = da.astype(_MXU)
            dh_ref[:, (2 * hb + 1) * il:(2 * hb + 2) * il] = dgate.astype(_MXU)
            cs_ref[:, 2 * hb * il:(2 * hb + 1) * il] += _fold8(da)
            cs_ref[:, (2 * hb + 1) * il:(2 * hb + 2) * il] += _fold8(dgate)

    row = lambda i: (i, 0)
    return pl.pallas_call(
        body, name=name, grid=(T // tm,),
        in_specs=[pl.BlockSpec((tm, C), row), pl.BlockSpec((tm, 2 * C), row)],
        out_specs=[pl.BlockSpec((tm, 2 * C), row), pl.BlockSpec((SUBLANES, 2 * C), lambda i: (0, 0))],
        out_shape=[jax.ShapeDtypeStruct((T, 2 * C), _MXU), jax.ShapeDtypeStruct((SUBLANES, 2 * C), F32)],
        compiler_params=_cp(("arbitrary",)),
    )(du, h1)


def _tril_mask(n):
    return lax.broadcasted_iota(jnp.int32, (n, n), 0) >= lax.broadcasted_iota(jnp.int32, (n, n), 1)


def _split_uv(t, il):
    u = jnp.concatenate([t[:, 0:il], t[:, 2 * il:3 * il]], axis=1)
    v = jnp.concatenate([t[:, il:2 * il], t[:, 3 * il:4 * il]], axis=1)
    return u, v


def _gmlp_gate_fwd(p, g, b, w_s, bsb, w_out, bias, res, alpha, g1, b1, *, name):
    T, C2 = p.shape
    C = C2 // 2
    D = w_out.shape[-1]
    il = C // 2
    G, L, _ = w_s.shape
    assert G * L == C
    tm = _tile(T, 4 * L, L)

    def body(p_ref, g_ref, b_ref, ws_ref, bs_ref, wo_ref, bias_ref, res_ref, g1_ref, b1_ref,
             us_ref, xh_ref, rs_ref, y_ref, yb_ref, xh1_ref, rs1_ref, vn_ref, u_ref):
        z, _ = _gelu_parts(p_ref[...].astype(F32))
        u, v = _split_uv(z, il)
        vn, xh, rstd = _ln_rows(v, g_ref[...], b_ref[...])
        xh_ref[...] = xh.astype(_XDT)
        rs_ref[...] = rstd
        vn_ref[...] = vn.astype(_MXU)
        u_ref[...] = u
        mask = _tril_mask(L)
        for gi in range(G):
            wc = jnp.where(mask, ws_ref[gi], 0.0).astype(_MXU)
            cols = slice(gi * L, (gi + 1) * L)
            for c in range(tm // L):
                rows = slice(c * L, (c + 1) * L)
                s = jnp.dot(wc, vn_ref[rows, cols], preferred_element_type=F32) + bs_ref[:, cols]
                us_ref[rows, cols] = (u_ref[rows, cols] * s).astype(_MXU)
        _out_ln(us_ref[...], wo_ref, bias_ref, res_ref, alpha, g1_ref, b1_ref, y_ref, yb_ref, xh1_ref, rs1_ref)

    row = lambda i: (i, 0)
    fixed = lambda i: (0, 0)
    vd, td, one = pl.BlockSpec((1, D), fixed), pl.BlockSpec((tm, D), row), pl.BlockSpec((tm, 1), row)
    return pl.pallas_call(
        body, name=name, grid=(T // tm,),
        in_specs=[pl.BlockSpec((tm, C2), row), pl.BlockSpec((1, C), fixed), pl.BlockSpec((1, C), fixed),
                  pl.BlockSpec((G, L, L), lambda i: (0, 0, 0)), pl.BlockSpec((L, C), fixed),
                  _resident((None, C, D), lambda i: (0, 0, 0)), vd, td, vd, vd],
        out_specs=[pl.BlockSpec((tm, C), row), pl.BlockSpec((tm, C), row), one, td, td, td, one],
        out_shape=[jax.ShapeDtypeStruct((T, C), _MXU), jax.ShapeDtypeStruct((T, C), _XDT),
                   jax.ShapeDtypeStruct((T, 1), F32), jax.ShapeDtypeStruct((T, D), F32),
                   jax.ShapeDtypeStruct((T, D), _MXU), jax.ShapeDtypeStruct((T, D), _XDT),
                   jax.ShapeDtypeStruct((T, 1), F32)],
        scratch_shapes=[pltpu.VMEM((tm, C), _MXU), pltpu.VMEM((tm, C), F32)],
        compiler_params=_cp(("parallel",)),
    )(p, g, b, w_s, bsb, w_out, bias, res, g1, b1)


def _gmlp_gate_bwd(dzb, w_out, p, xh, rstd, g, b, w_s, bsb, *, name):
    T, C2 = p.shape
    D = dzb.shape[1]
    C = C2 // 2
    il = C // 2
    G, L, _ = w_s.shape
    tm = _tile(T, 4 * L, L)

    def body(dz_ref, wo_ref, p_ref, xh_ref, rs_ref, g_ref, b_ref, ws_ref, bs_ref,
             dp_ref, dg_ref, db_ref, cs_ref, dws_ref, dbs_ref, vn_ref, u_ref, dvn_ref, du_ref, dus_ref):
        @pl.when(pl.program_id(0) == 0)
        def _():
            dg_ref[...] = jnp.zeros_like(dg_ref)
            db_ref[...] = jnp.zeros_like(db_ref)
            cs_ref[...] = jnp.zeros_like(cs_ref)
            dws_ref[...] = jnp.zeros_like(dws_ref)
            dbs_ref[...] = jnp.zeros_like(dbs_ref)

        dus_ref[...] = lax.dot_general(dz_ref[...].astype(_MXU), wo_ref[...].astype(_MXU), (((1,), (1,)), ((), ())),
                                       preferred_element_type=F32)
        z, gp = _gelu_parts(p_ref[...].astype(F32))
        u, _ = _split_uv(z, il)
        xh = xh_ref[...].astype(F32)
        gam = g_ref[...]
        vn_ref[...] = (xh * gam + b_ref[...]).astype(_MXU)
        u_ref[...] = u
        mask = _tril_mask(L)
        for gi in range(G):
            wc = jnp.where(mask, ws_ref[gi], 0.0).astype(_MXU)
            cols = slice(gi * L, (gi + 1) * L)
            for c in range(tm // L):
                rows = slice(c * L, (c + 1) * L)
                vnb = vn_ref[rows, cols]
                s = jnp.dot(wc, vnb, preferred_element_type=F32) + bs_ref[:, cols]
                d = dus_ref[rows, cols]
                du_ref[rows, cols] = d * s
                ds = d * u_ref[rows, cols]
                dbs_ref[:, cols] += ds
                dsb = ds.astype(_MXU)
                dw = lax.dot_general(dsb, vnb, (((1,), (1,)), ((), ())), preferred_element_type=F32)
                dws_ref[gi] += jnp.where(mask, dw, 0.0)
                dvn_ref[rows, cols] = lax.dot_general(wc, dsb, (((0,), (0,)), ((), ())), preferred_element_type=F32)
        dvn = dvn_ref[...]
        dg_ref[...] += _fold8(dvn * xh)
        db_ref[...] += _fold8(dvn)
        dv = _ln_bwd_rows(dvn, xh, rs_ref[...], gam)
        du = du_ref[...]
        for hb in range(2):
            for part, src in ((0, du), (1, dv)):
                lo = (2 * hb + part) * il
                dp = src[:, hb * il:(hb + 1) * il] * gp[:, lo:lo + il]
                dp_ref[:, lo:lo + il] = dp.astype(_MXU)
                cs_ref[:, lo:lo + il] += _fold8(dp)

    row = lambda i: (i, 0)
    fixed = lambda i: (0, 0)
    part_c = pl.BlockSpec((SUBLANES, C), fixed)
    return pl.pallas_call(
        body, name=name, grid=(T // tm,),
        in_specs=[pl.BlockSpec((tm, D), row), _resident((None, C, D), lambda i: (0, 0, 0)),
                  pl.BlockSpec((tm, C2), row), pl.BlockSpec((tm, C), row),
                  pl.BlockSpec((tm, 1), row), pl.BlockSpec((1, C), fixed), pl.BlockSpec((1, C), fixed),
                  pl.BlockSpec((G, L, L), lambda i: (0, 0, 0)), pl.BlockSpec((L, C), fixed)],
        out_specs=[pl.BlockSpec((tm, C2), row), part_c, part_c, pl.BlockSpec((SUBLANES, C2), fixed),
                   pl.BlockSpec((G, L, L), lambda i: (0, 0, 0)), pl.BlockSpec((L, C), fixed)],
        out_shape=[jax.ShapeDtypeStruct((T, C2), _MXU), jax.ShapeDtypeStruct((SUBLANES, C), F32),
                   jax.ShapeDtypeStruct((SUBLANES, C), F32), jax.ShapeDtypeStruct((SUBLANES, C2), F32),
                   jax.ShapeDtypeStruct((G, L, L), F32), jax.ShapeDtypeStruct((L, C), F32)],
        scratch_shapes=[pltpu.VMEM((tm, C), _MXU), pltpu.VMEM((tm, C), F32), pltpu.VMEM((tm, C), F32),
                        pltpu.VMEM((tm, C), F32), pltpu.VMEM((tm, C), F32)],
        compiler_params=_cp(("arbitrary",)),
    )(dzb, w_out, p, xh, rstd, g, b, w_s, bsb)


def _ffn_conv(h, prev8, w_ref, b_ref):
    h1 = _shift_down(prev8, h, 1)
    h2 = _shift_down(prev8, h, 2)
    return w_ref[pl.ds(2, 1), :] * h + w_ref[pl.ds(1, 1), :] * h1 + w_ref[pl.ds(0, 1), :] * h2 + b_ref[...]


def _resident(block, imap):
    return pl.BlockSpec(block, imap, pipeline_mode=pl.Buffered(1))


def _ffn_fwd_half(j, xb, w_up, w_down, b_up, w_dw, b_dw, *, S, name, prev=None, tail=None, head=None):
    T, D = xb.shape
    N = w_up.shape[-1]
    tn = N // N_CHIPS
    tm = _tile(S, 256)
    spt = S // tm
    last = prev is not None
    alpha = tail[1] if last else None

    def body(*refs):
        x_ref, wu_ref, wd_ref, bu_ref, wc_ref, bc_ref = refs[:6]
        if last:
            yp_ref, res_ref, bd_ref, g_ref, b_ref = refs[9:14]
            o = 14 if head is None else 15
            h_ref, hc_ref, f_ref, y_ref, yb_ref, xh_ref, rs_ref = refs[o:o + 7]
            carry_ref = refs[-1]
        else:
            h_ref, hc_ref, f_ref, yp_ref, carry_ref = refs[6:11]

        @pl.when(pl.program_id(0) % spt == 0)
        def _():
            carry_ref[...] = jnp.zeros_like(carry_ref)

        h = jnp.dot(x_ref[...].astype(_MXU), wu_ref[...].astype(_MXU), preferred_element_type=F32) + bu_ref[...]
        h_ref[...] = h.astype(_HDT)
        hc = _ffn_conv(h, carry_ref[...], wc_ref, bc_ref)
        hc_ref[...] = hc.astype(_HDT)
        carry_ref[...] = h[tm - SUBLANES:tm]
        gte = hc[:, :tn]
        f = (gte * _sigmoid(gte) * hc[:, tn:]).astype(_MXU)
        f_ref[...] = f
        y = jnp.dot(f, wd_ref[...].astype(_MXU), preferred_element_type=F32)
        if not last:
            yp_ref[...] = y
            return
        z = y + yp_ref[...] + bd_ref[...] + alpha * res_ref[...]
        out, xh, rstd = _ln_rows(z, g_ref[...], b_ref[...])
        if head is None:
            y_ref[...] = out
            yb_ref[...] = out.astype(_MXU)
            xh_ref[...] = xh.astype(_XDT)
            rs_ref[...] = rstd
            return
        t_ref, cs_ref, ls_ref = refs[14], refs[o + 7], refs[o + 8]

        @pl.when(pl.program_id(0) == 0)
        def _():
            for acc in (xh_ref, rs_ref, cs_ref, ls_ref):
                acc[...] = jnp.zeros_like(acc)

        err = out - t_ref[...]
        d = err * (1.0 / D)
        dz = _ln_bwd_rows(d, xh, rstd, g_ref[...])
        y_ref[...] = dz
        yb_ref[...] = dz.astype(_MXU)
        xh_ref[...] += _fold8(d * xh)
        rs_ref[...] += _fold8(d)
        cs_ref[...] += _fold8(dz)
        ls_ref[...] += _fold8(err * err)

    row = lambda i: (i, 0)
    pair = lambda i: (0, j)
    vec = pl.BlockSpec((1, D), lambda i: (0, 0))
    tile = pl.BlockSpec((tm, D), row)
    in_specs = [tile, _resident((None, D, 2 * tn), lambda i: (0, 0, j)), _resident((None, tn, D), lambda i: (0, j, 0)),
                pl.BlockSpec((1, 2 * tn), pair), pl.BlockSpec((SUBLANES, 2 * tn), pair), pl.BlockSpec((1, 2 * tn), pair)]
    operands = [xb, w_up, w_down, b_up, w_dw, b_dw]
    wide = pl.BlockSpec((tm, 2 * tn), lambda i: (i, j))
    out_specs = [wide, wide, pl.BlockSpec((tm, tn), lambda i: (i, j))]
    out_shape = [jax.ShapeDtypeStruct((T, N), _HDT), jax.ShapeDtypeStruct((T, N), _HDT),
                 jax.ShapeDtypeStruct((T, N // 2), _MXU)]
    aliases = {}
    if last:
        res, _, b_down, g, b = tail
        in_specs += [ANY, ANY, ANY, tile, tile, vec, vec, vec]
        operands += list(prev) + [res, b_down, g, b]
        aliases = {6: 0, 7: 1, 8: 2}
        if head is None:
            out_specs += [tile, tile, tile, pl.BlockSpec((tm, 1), row)]
            out_shape += [jax.ShapeDtypeStruct((T, D), F32), jax.ShapeDtypeStruct((T, D), _MXU),
                          jax.ShapeDtypeStruct((T, D), _XDT), jax.ShapeDtypeStruct((T, 1), F32)]
        else:
            in_specs.append(tile)
            operands.append(head)
            part = pl.BlockSpec((SUBLANES, D), lambda i: (0, 0))
            out_specs += [tile, tile, part, part, part, part]
            out_shape += [jax.ShapeDtypeStruct((T, D), F32), jax.ShapeDtypeStruct((T, D), _MXU)] \
                + [jax.ShapeDtypeStruct((SUBLANES, D), F32)] * 4
    else:
        out_specs.append(tile)
        out_shape.append(jax.ShapeDtypeStruct((T, D), F32))
    return pl.pallas_call(
        body, name=name, grid=(T // tm,), in_specs=in_specs, out_specs=out_specs, out_shape=out_shape,
        input_output_aliases=aliases, scratch_shapes=[pltpu.VMEM((SUBLANES, 2 * tn), F32)],
        compiler_params=_cp(("arbitrary",)),
    )(*operands)


def _ffn_bwd_half(j, dzb, w_down, w_up, hs, hcs, w_dw, *, S, name, dz=None, alpha=None, prev=None, ln=None):
    T, D = dzb.shape
    N = hs.shape[1]
    tn = N // N_CHIPS
    tm = _tile(S, 256)
    spt = S // tm
    nt = T // tm
    last = prev is not None

    def body(*refs):
        dz_ref, wd_ref, wu_ref, h_ref, hc_ref, wc_ref = refs[:6]
        if last:
            dxp_ref, xh_ref, rs_ref, g_ref = refs[7:11]
            dh_ref, cs_ref, dw_ref, db_ref, dz1_ref, dz1b_ref, dg1_ref, db1_ref, cs1_ref, carry_ref = refs[11:21]
        else:
            dzf_ref = refs[6]
            dh_ref, cs_ref, dw_ref, db_ref, dxp_ref, carry_ref = refs[7:13]
        i = pl.program_id(0)
        ii = nt - 1 - i

        @pl.when(i == 0)
        def _():
            cs_ref[...] = jnp.zeros_like(cs_ref)
            dw_ref[...] = jnp.zeros_like(dw_ref)
            db_ref[...] = jnp.zeros_like(db_ref)
            if last:
                dg1_ref[...] = jnp.zeros_like(dg1_ref)
                db1_ref[...] = jnp.zeros_like(db1_ref)
                cs1_ref[...] = jnp.zeros_like(cs1_ref)

        df = lax.dot_general(dz_ref[...].astype(_MXU), wd_ref[...].astype(_MXU), (((1,), (1,)), ((), ())),
                             preferred_element_type=F32)
        h = h_ref[...].astype(F32)
        gte, val = hc_ref[:, :tn].astype(F32), hc_ref[:, tn:].astype(F32)
        sig = _sigmoid(gte)
        dval = df * (gte * sig)
        dg = df * val * (sig * (1.0 + gte * (1.0 - sig)))
        dhc = jnp.concatenate([dg, dval], axis=1)
        nxt = jnp.where((ii + 1) % spt == 0, 0.0, carry_ref[...])
        d1 = _shift_up(dhc, nxt, 1)
        d2 = _shift_up(dhc, nxt, 2)
        carry_ref[...] = dhc[0:SUBLANES]
        db_ref[...] += _fold8(dhc)
        dw_ref[2] += _fold8(dhc * h)
        dw_ref[1] += _fold8(d1 * h)
        dw_ref[0] += _fold8(d2 * h)
        dh = wc_ref[pl.ds(2, 1), :] * dhc + wc_ref[pl.ds(1, 1), :] * d1 + wc_ref[pl.ds(0, 1), :] * d2
        cs_ref[...] += _fold8(dh)
        dhb = dh.astype(_MXU)
        dh_ref[...] = dhb
        dx = lax.dot_general(dhb, wu_ref[...].astype(_MXU), (((1,), (1,)), ((), ())), preferred_element_type=F32)
        if not last:
            dxp_ref[...] = dx + alpha * dzf_ref[...]
            return
        d = dx + dxp_ref[...]
        xh = xh_ref[...].astype(F32)
        dz1 = _ln_bwd_rows(d, xh, rs_ref[...], g_ref[...])
        dz1_ref[...] = dz1
        dz1b_ref[...] = dz1.astype(_MXU)
        dg1_ref[...] += _fold8(d * xh)
        db1_ref[...] += _fold8(d)
        cs1_ref[...] += _fold8(dz1)

    rev = lambda i: (nt - 1 - i, 0)
    fixed = lambda i: (0, 0)
    pair = lambda i: (0, j)
    tile = pl.BlockSpec((tm, D), rev)
    wide = pl.BlockSpec((tm, 2 * tn), lambda i: (nt - 1 - i, j))
    part = pl.BlockSpec((SUBLANES, 2 * tn), fixed)
    in_specs = [tile, _resident((None, tn, D), lambda i: (0, j, 0)), _resident((None, D, 2 * tn), lambda i: (0, 0, j)),
                wide, wide, pl.BlockSpec((SUBLANES, 2 * tn), pair)]
    operands = [dzb, w_down, w_up, hs, hcs, w_dw]
    out_specs = [wide, part, pl.BlockSpec((3, SUBLANES, 2 * tn), lambda i: (0, 0, 0)), part]
    out_shape = [jax.ShapeDtypeStruct((T, N), _MXU), jax.ShapeDtypeStruct((SUBLANES, 2 * tn), F32),
                 jax.ShapeDtypeStruct((3, SUBLANES, 2 * tn), F32), jax.ShapeDtypeStruct((SUBLANES, 2 * tn), F32)]
    aliases = {}
    if last:
        xh, rstd, g = ln
        in_specs += [ANY, tile, tile, pl.BlockSpec((tm, 1), rev), pl.BlockSpec((1, D), fixed)]
        operands += [prev[0], prev[1], xh, rstd, g]
        aliases = {6: 0}
        out_specs += [tile, tile] + [pl.BlockSpec((SUBLANES, D), fixed)] * 3
        out_shape += [jax.ShapeDtypeStruct((T, D), F32), jax.ShapeDtypeStruct((T, D), _MXU)] \
            + [jax.ShapeDtypeStruct((SUBLANES, D), F32)] * 3
    else:
        in_specs.append(tile)
        operands.append(dz)
        out_specs.append(tile)
        out_shape.append(jax.ShapeDtypeStruct((T, D), F32))
    return pl.pallas_call(
        body, name=name, grid=(nt,), in_specs=in_specs, out_specs=out_specs, out_shape=out_shape,
        input_output_aliases=aliases, scratch_shapes=[pltpu.VMEM((SUBLANES, 2 * tn), F32)],
        compiler_params=_cp(("arbitrary",)),
    )(*operands)


def _sum_pieces(gs, rs, me, *, name):
    n = len(gs)
    _, pr, pc = gs[0].shape
    tr = _tile(pr, 128)

    def body(me_ref, *refs):
        o_ref = refs[2 * n]
        for l in range(n):
            total = refs[l][...].astype(F32)
            for s in range(N_DEV - 1):
                total = total + refs[n + l][s].astype(F32)
            o_ref[l] = total

    own = pl.BlockSpec((None, tr, pc), lambda i, me_ref: (me_ref[0], i, 0))
    got = pl.BlockSpec((N_DEV - 1, tr, pc), lambda i, me_ref: (0, i, 0))
    return pl.pallas_call(
        body, name=name,
        grid_spec=pltpu.PrefetchScalarGridSpec(
            num_scalar_prefetch=1, grid=(pr // tr,), in_specs=[own] * n + [got] * n,
            out_specs=pl.BlockSpec((n, tr, pc), lambda i, me_ref: (0, i, 0))),
        out_shape=jax.ShapeDtypeStruct((n, pr, pc), F32),
        compiler_params=_cp(("parallel",)),
    )(me, *gs, *rs)


def _adam_math(w, g, m, v):
    bc1 = 1.0 - ADAM_B1 ** ADAM_STEP
    bc2 = 1.0 - ADAM_B2 ** ADAM_STEP
    m = ADAM_B1 * m + (1.0 - ADAM_B1) * g
    v = ADAM_B2 * v + (1.0 - ADAM_B2) * (g * g)
    return -ADAM_LR * ((m / bc1) / (jnp.sqrt(v / bc2) + ADAM_EPS) + ADAM_WD * w), m, v


def _adam(w, g, m, v, *, name):
    R, C = w.shape
    tr = _tile(R, 256)

    def body(w_ref, g_ref, m_ref, v_ref, d_ref, mo_ref, vo_ref):
        d_ref[...], mo_ref[...], vo_ref[...] = _adam_math(w_ref[...], g_ref[...], m_ref[...], v_ref[...])

    spec = pl.BlockSpec((tr, C), lambda i: (i, 0))
    return pl.pallas_call(
        body, name=name, grid=(R // tr,), in_specs=[spec] * 4, out_specs=[spec] * 3,
        out_shape=[jax.ShapeDtypeStruct((R, C), F32)] * 3,
        compiler_params=_cp(("parallel",)),
    )(w, g, m, v)


def _adam_halves(w, own, got, m, v, core, *, name):
    L, R, C = w.shape
    rh = R // 2
    tr = _tile(rh, 256)
    nt = rh // tr

    def body(c_ref, w_ref, own_ref, got_ref, m_ref, v_ref, g_ref, d_ref, mo_ref, vo_ref):
        g = jnp.where(pl.program_id(1) == c_ref[0], own_ref[...], got_ref[...])
        g_ref[...] = g
        d_ref[...], mo_ref[...], vo_ref[...] = _adam_math(w_ref[...], g, m_ref[...], v_ref[...])

    full = pl.BlockSpec((None, tr, C), lambda l, h, t, c_ref: (l, h * nt + t, 0))
    half = pl.BlockSpec((None, tr, C), lambda l, h, t, c_ref: (l, t, 0))
    return pl.pallas_call(
        body, name=name,
        grid_spec=pltpu.PrefetchScalarGridSpec(
            num_scalar_prefetch=1, grid=(L, 2, nt), in_specs=[full, half, half, full, full], out_specs=[full] * 4),
        out_shape=[jax.ShapeDtypeStruct((L, R, C), F32)] * 4,
        compiler_params=_cp(("parallel", "parallel", "parallel")),
    )(core, w, own, got, m, v)


def _remote(src, dst, send, recv, dev):
    return pltpu.make_async_remote_copy(src_ref=src, dst_ref=dst, send_sem=send, recv_sem=recv,
                                        device_id=dev, device_id_type=MESH)


def _place_w(shard, pos, layer, *, axis, name):
    _, R, C = shard.shape
    tr = _tile(R, 512, 16)
    nt = R // tr
    if axis == 2:
        out_shape = (1, R, N_CHIPS * C)
        out_map = lambda t, q: (0, t, q[0])
    else:
        out_shape = (1, N_CHIPS * R, C)
        out_map = lambda t, q: (0, q[0] * nt + t, 0)

    def body(q_ref, s_ref, o_ref):
        o_ref[...] = s_ref[...].astype(_WIRE)

    return pl.pallas_call(
        body, name=name,
        grid_spec=pltpu.PrefetchScalarGridSpec(
            num_scalar_prefetch=1, grid=(nt,),
            in_specs=[pl.BlockSpec((None, tr, C), lambda t, q: (layer, t, 0))],
            out_specs=pl.BlockSpec((None, tr, C), out_map)),
        out_shape=jax.ShapeDtypeStruct(out_shape, _WIRE),
        compiler_params=_cp(("parallel",)),
    )(pos, shard)


def _ag_window(ref, kind, px, py, h):
    axis, perm = kind
    q = 2 * px + py
    if perm:
        q = _perm_idx(q)
    if axis == 2:
        R, C = ref.shape[1], ref.shape[2] // N_CHIPS
        rh = R // 2
        return ref.at[:, pl.ds(pl.multiple_of(h * rh, 16), rh), pl.ds(pl.multiple_of(q * C, LANES), C)]
    R = ref.shape[1] // N_CHIPS
    rh = R // 2
    return ref.at[:, pl.ds(pl.multiple_of(q * R + h * rh, 16), rh), :]


def _ag_ici_copies(refs, kinds, send, recv):
    x, y, c = lax.axis_index("x"), lax.axis_index("y"), lax.axis_index("c")
    chips = [(1 - x, y), (x, 1 - y), (1 - x, 1 - y)]
    sends, recvs = [], []
    for a, (ref, kind) in enumerate(zip(refs, kinds)):
        own = _ag_window(ref, kind, x, y, c)
        for i, (px, py) in enumerate(chips):
            k = 3 * a + i
            sends.append(_remote(own, own, send.at[k], recv.at[k], (px, py, c)))
            recvs.append(_remote(own, _ag_window(ref, kind, px, py, c), send.at[k], recv.at[k], (px, py, c)))
    return sends, recvs


def _ag_start(arrs, kinds, after, *, name):
    n = len(arrs)

    def body(*refs):
        in_refs = refs[:n]
        send, recv = refs[n + len(after)], refs[n + len(after) + 1]
        token = refs[-1]
        sends, _ = _ag_ici_copies(in_refs, kinds, send, recv)
        for cp in sends:
            cp.start()
        token[...] = jnp.zeros_like(token)

    sems = pltpu.SemaphoreType.DMA((3 * n,))
    out = pl.pallas_call(
        body, name=name,
        out_shape=(sems, sems) + tuple(pltpu.HBM(a.shape, a.dtype) for a in arrs)
        + (jax.ShapeDtypeStruct((SUBLANES, LANES), F32),),
        in_specs=(HBM,) * n + (ANY,) * len(after),
        out_specs=(SEMS, SEMS) + (HBM,) * n + (pl.BlockSpec(memory_space=pltpu.VMEM),),
        input_output_aliases={a: 2 + a for a in range(n)},
        compiler_params=pltpu.CompilerParams(has_side_effects=EFFECT),
    )(*[pltpu.with_memory_space_constraint(a, pltpu.HBM) for a in arrs], *after)
    return out[0], out[1], list(out[2:2 + n]), out[-1]


def _ag_wait(send, recv, arrs, kinds, after, *, name):
    n = len(arrs)

    def body(*refs):
        in_refs = refs[:n]
        send, recv = refs[n], refs[n + 1]
        sends, recvs = _ag_ici_copies(in_refs, kinds, send, recv)
        for cp in sends:
            cp.wait_send()
        for cp in recvs:
            cp.wait_recv()

    out = pl.pallas_call(
        body, name=name,
        out_shape=tuple(pltpu.HBM(a.shape, a.dtype) for a in arrs),
        in_specs=(HBM,) * n + (SEMS, SEMS) + (ANY,) * len(after), out_specs=(HBM,) * n,
        input_output_aliases={a: a for a in range(n)},
        compiler_params=pltpu.CompilerParams(has_side_effects=EFFECT),
    )(*arrs, send, recv, *after)
    return list(out)


def _ag_forward(arrs, kinds, *, name):
    n = len(arrs)

    def body(*refs):
        o_refs, send, recv = refs[n:2 * n], refs[2 * n], refs[2 * n + 1]
        x, y, c = lax.axis_index("x"), lax.axis_index("y"), lax.axis_index("c")
        chips = [(1 - x, y), (x, 1 - y), (1 - x, 1 - y)]
        sib = (x, y, 1 - c)
        sends, recvs = [], []
        for a, (ref, kind) in enumerate(zip(o_refs, kinds)):
            for i, (px, py) in enumerate(chips):
                k = 3 * a + i
                got = _ag_window(ref, kind, px, py, c)
                cp = _remote(got, got, send.at[k], recv.at[k], sib)
                cp.start()
                sends.append(cp)
                recvs.append(_remote(got, _ag_window(ref, kind, px, py, 1 - c), send.at[k], recv.at[k], sib))
        for cp in recvs:
            cp.wait_recv()
        for cp in sends:
            cp.wait_send()

    out = pl.pallas_call(
        body, name=name, in_specs=[ANY] * n, out_specs=[ANY] * n,
        out_shape=[jax.ShapeDtypeStruct(a.shape, a.dtype) for a in arrs],
        input_output_aliases={a: a for a in range(n)},
        scratch_shapes=[pltpu.SemaphoreType.DMA((3 * n,)), pltpu.SemaphoreType.DMA((3 * n,))],
    )(*arrs)
    return list(out)


def _flip(x, y, c, f):
    return ((1 - x) if f & 4 else x, (1 - y) if f & 2 else y, (1 - c) if f & 1 else c)


def _rs_copies(g_refs, land_refs, send, recv):
    x, y, c = lax.axis_index("x"), lax.axis_index("y"), lax.axis_index("c")
    cps = []
    for a, (g_ref, land_ref) in enumerate(zip(g_refs, land_refs)):
        for f in range(1, N_DEV):
            tx, ty, tcx = _flip(x, y, c, f)
            k = (N_DEV - 1) * a + f - 1
            cps.append(_remote(g_ref.at[4 * tx + 2 * ty + tcx], land_ref.at[f - 1], send.at[k], recv.at[k],
                               (tx, ty, tcx)))
    return cps


def _rs_start(gs, *, name):
    n = len(gs)
    lands = [lax.empty((N_DEV - 1,) + g.shape[1:], g.dtype) for g in gs]

    def body(*refs):
        send, recv, token = refs[2 * n], refs[2 * n + 1], refs[-1]
        for cp in _rs_copies(refs[:n], refs[n:2 * n], send, recv):
            cp.start()
        token[...] = jnp.zeros_like(token)

    sems = pltpu.SemaphoreType.DMA(((N_DEV - 1) * n,))
    thru = [pltpu.HBM(t.shape, t.dtype) for t in gs + lands]
    out = pl.pallas_call(
        body, name=name,
        out_shape=(sems, sems, *thru, jax.ShapeDtypeStruct((SUBLANES, LANES), F32)),
        in_specs=(HBM,) * (2 * n), out_specs=(SEMS, SEMS) + (HBM,) * (2 * n) + (pl.BlockSpec(memory_space=pltpu.VMEM),),
        input_output_aliases={a: 2 + a for a in range(2 * n)},
        compiler_params=pltpu.CompilerParams(has_side_effects=EFFECT),
    )(*[pltpu.with_memory_space_constraint(t, pltpu.HBM) for t in gs + lands])
    return out[0], out[1], list(out[2:2 + n]), list(out[2 + n:2 + 2 * n]), out[-1]


def _rs_wait(send, recv, gs, lands, after, *, name):
    n = len(gs)

    def body(*refs):
        cps = _rs_copies(refs[:n], refs[n:2 * n], refs[2 * n], refs[2 * n + 1])
        for cp in cps:
            cp.wait_send()
        for cp in cps:
            cp.wait_recv()

    out = pl.pallas_call(
        body, name=name,
        out_shape=tuple(pltpu.HBM(t.shape, t.dtype) for t in gs + lands),
        in_specs=(HBM,) * (2 * n) + (SEMS, SEMS, ANY), out_specs=(HBM,) * (2 * n),
        input_output_aliases={a: a for a in range(2 * n)},
        compiler_params=pltpu.CompilerParams(has_side_effects=EFFECT),
    )(*gs, *lands, send, recv, after)
    return list(out[:n]), list(out[n:])


def _pair_exchange(owns, *, name):
    n = len(owns)

    def body(*refs):
        send, recv = refs[2 * n], refs[2 * n + 1]
        x, y, c = lax.axis_index("x"), lax.axis_index("y"), lax.axis_index("c")
        cps = [_remote(refs[a], refs[n + a], send.at[a], recv.at[a], (x, y, 1 - c)) for a in range(n)]
        for cp in cps:
            cp.start()
        for cp in cps:
            cp.wait_recv()
        for cp in cps:
            cp.wait_send()

    return pl.pallas_call(
        body, name=name, in_specs=[ANY] * n, out_specs=[ANY] * n,
        out_shape=[jax.ShapeDtypeStruct(o.shape, o.dtype) for o in owns],
        scratch_shapes=[pltpu.SemaphoreType.DMA((n,)), pltpu.SemaphoreType.DMA((n,))],
    )(*owns)


def _allreduce_flat(vec, *, name):
    n = vec.shape[0]
    unit = N_DEV * SUBLANES * LANES
    npad = -(-n // unit) * unit
    rows = npad // (N_DEV * LANES)
    xin = jnp.pad(vec, (0, npad - n)).reshape(N_DEV, rows, LANES)

    def body(x_ref, y_ref, a_ref, send_a, recv_a, send_b, recv_b):
        x, y, c = lax.axis_index("x"), lax.axis_index("y"), lax.axis_index("c")
        me = 4 * x + 2 * y + c
        a_ref[me] = x_ref[me]
        sends, recvs = [], []
        for f in range(1, N_DEV):
            dev = _flip(x, y, c, f)
            t = 4 * dev[0] + 2 * dev[1] + dev[2]
            cp = _remote(x_ref.at[t], a_ref.at[me], send_a.at[f - 1], recv_a.at[f - 1], dev)
            cp.start()
            sends.append(cp)
            recvs.append(_remote(x_ref.at[me], a_ref.at[t], send_a.at[f - 1], recv_a.at[f - 1], dev))
        for cp in recvs:
            cp.wait_recv()
        for cp in sends:
            cp.wait_send()
        acc = a_ref[0]
        for s in range(1, N_DEV):
            acc = acc + a_ref[s]
        y_ref[me] = acc
        sends, recvs = [], []
        for f in range(1, N_DEV):
            dev = _flip(x, y, c, f)
            t = 4 * dev[0] + 2 * dev[1] + dev[2]
            cp = _remote(y_ref.at[me], y_ref.at[me], send_b.at[f - 1], recv_b.at[f - 1], dev)
            cp.start()
            sends.append(cp)
            recvs.append(_remote(y_ref.at[me], y_ref.at[t], send_b.at[f - 1], recv_b.at[f - 1], dev))
        for cp in recvs:
            cp.wait_recv()
        for cp in sends:
            cp.wait_send()

    vm = pl.BlockSpec(memory_space=pltpu.VMEM)
    out = pl.pallas_call(
        body, name=name, in_specs=[vm], out_specs=vm,
        out_shape=jax.ShapeDtypeStruct((N_DEV, rows, LANES), F32),
        scratch_shapes=[pltpu.VMEM((N_DEV, rows, LANES), F32)] + [pltpu.SemaphoreType.DMA((N_DEV - 1,))] * 4,
        compiler_params=_cp(),
    )(xin)
    return out.reshape(npad)[:n]


def _perm_cols(v, blocks=N_CHIPS):
    w = v.shape[-1] // blocks
    return jnp.concatenate([v[..., q * w:(q + 1) * w] for q in PERM], axis=-1)


def _pack(arrs):
    return jnp.concatenate([a.reshape(-1).astype(F32) for a in arrs])


def _unpack(flat, shapes):
    out, pos = [], 0
    for s in shapes:
        n = 1
        for d in s:
            n *= d
        out.append(flat[pos:pos + n].reshape(s))
        pos += n
    return out


def kernel(x, conv_w_in, conv_b_in, conv_w_dw, conv_b_dw, conv_ln_g, conv_ln_b, conv_w_out, conv_b_out, gmlp_w_in, gmlp_b_in, gmlp_ln_g, gmlp_ln_b, gmlp_w_s, gmlp_b_s, gmlp_w_out, gmlp_b_out, ffn_w_up, ffn_b_up, ffn_w_dw, ffn_b_dw, ffn_w_down, ffn_b_down, norm1_g, norm1_b, norm2_g, norm2_b, loss_target, m_conv_w_in, m_conv_b_in, m_conv_w_dw, m_conv_b_dw, m_conv_ln_g, m_conv_ln_b, m_conv_w_out, m_conv_b_out, m_gmlp_w_in, m_gmlp_b_in, m_gmlp_ln_g, m_gmlp_ln_b, m_gmlp_w_s, m_gmlp_b_s, m_gmlp_w_out, m_gmlp_b_out, m_ffn_w_up, m_ffn_b_up, m_ffn_w_dw, m_ffn_b_dw, m_ffn_w_down, m_ffn_b_down, m_norm1_g, m_norm1_b, m_norm2_g, m_norm2_b, v_conv_w_in, v_conv_b_in, v_conv_w_dw, v_conv_b_dw, v_conv_ln_g, v_conv_ln_b, v_conv_w_out, v_conv_b_out, v_gmlp_w_in, v_gmlp_b_in, v_gmlp_ln_g, v_gmlp_ln_b, v_gmlp_w_s, v_gmlp_b_s, v_gmlp_w_out, v_gmlp_b_out, v_ffn_w_up, v_ffn_b_up, v_ffn_w_dw, v_ffn_b_dw, v_ffn_w_down, v_ffn_b_down, v_norm1_g, v_norm1_b, v_norm2_g, v_norm2_b):
    P = dict(locals())
    WEIGHTS = ['conv_w_in', 'conv_b_in', 'conv_w_dw', 'conv_b_dw', 'conv_ln_g', 'conv_ln_b', 'conv_w_out',
               'conv_b_out', 'gmlp_w_in', 'gmlp_b_in', 'gmlp_ln_g', 'gmlp_ln_b', 'gmlp_w_s', 'gmlp_b_s',
               'gmlp_w_out', 'gmlp_b_out', 'ffn_w_up', 'ffn_b_up', 'ffn_w_dw', 'ffn_b_dw', 'ffn_w_down',
               'ffn_b_down', 'norm1_g', 'norm1_b', 'norm2_g', 'norm2_b']
    BIG = ['conv_w_in', 'conv_w_out', 'gmlp_w_in', 'gmlp_w_out', 'ffn_w_up', 'ffn_w_down']
    SMALL_SHARDED = {'conv_w_dw': 2, 'gmlp_b_in': 1, 'gmlp_ln_g': 1, 'gmlp_ln_b': 1, 'gmlp_b_out': 1, 'ffn_w_dw': 2}

    B, S, D = x.shape
    T = B * S
    depth = norm1_g.shape[0]
    alpha = (2.0 * depth) ** 0.25
    C = conv_w_out.shape[-1]
    F2 = ffn_b_up.shape[-1]
    G, L = gmlp_w_s.shape[1], gmlp_w_s.shape[2]
    xi, yi, ci = lax.axis_index("x"), lax.axis_index("y"), lax.axis_index("c")
    shard = 2 * xi + yi

    i32 = lambda v: jnp.reshape(v, (1,)).astype(jnp.int32)
    pos_plain, pos_perm = i32(shard), i32(_perm_idx(shard))
    me_id, core_id = i32(4 * xi + 2 * yi + ci), i32(ci)

    groups = []
    for i in range(depth):
        mix = 'conv' if i % 2 == 0 else 'gmlp'
        groups.append((f"{mix}{i // 2}", [(mix + '_w_in', i // 2, 2, True), (mix + '_w_out', i // 2, 1, False)]))
        groups.append((f"ffn{i}", [('ffn_w_up', i, 2, True), ('ffn_w_down', i, 1, False)]))
    sm_names = list(SMALL_SHARDED)
    sm_shapes = [P[n].shape for n in sm_names]
    mine = _pack([P[n] for n in sm_names]) * (ci == 0).astype(F32)
    buf = jnp.zeros((N_CHIPS, mine.shape[0]), F32)
    buf = lax.dynamic_update_slice(buf, mine[None], (shard, 0))
    gathered = _allreduce_flat(buf.reshape(-1), name="ag_small").reshape(N_CHIPS, -1)

    started, order = {}, [gathered]
    for gname, members in groups:
        placed = [_place_w(P[n], pos_perm if perm else pos_plain, l, axis=axis, name=f"place_{n}_{l}")
                  for n, l, axis, perm in members]
        kinds = [(axis, perm) for _, _, axis, perm in members]
        send, recv, arrs, token = _ag_start(placed, kinds, order, name=f"ag_start_{gname}")
        order = [token]
        started[gname] = (send, recv, arrs, kinds, [(n, l) for n, l, _, _ in members])
    wts = {}

    def arrive(gname, after):
        send, recv, arrs, kinds, keys = started[gname]
        arrs = _ag_wait(send, recv, arrs, kinds, after, name=f"ag_wait_{gname}")
        arrs = _ag_forward(arrs, kinds, name=f"ag_fwd_{gname}")
        wts.update(zip(keys, arrs))

    full = {}
    for n, parts in zip(sm_names, zip(*[_unpack(gathered[k], sm_shapes) for k in range(N_CHIPS)])):
        full[n] = jnp.concatenate(parts, axis=SMALL_SHARDED[n])
    for n in WEIGHTS:
        if n not in BIG and n not in full:
            full[n] = P[n]

    assert G * L == C, "a gMLP group must be as wide as a chunk is long"

    def row(v):
        return v.reshape(1, -1)

    def pad_rows(v, r):
        return jnp.pad(v, ((0, r - v.shape[0]), (0, 0)))

    xf = x.reshape(T, D)
    saved = []
    cur, cur_b = xf, xf.astype(_MXU)
    for i in range(depth):
        j = i // 2
        sv = {'x': cur, 'xb': cur_b}
        arrive(groups[2 * i][0], order if i == 0 else [cur_b])
        if i % 2 == 0:
            b_in = row(_perm_cols(full['conv_b_in'][j]))
            h1 = _mm(cur_b, wts['conv_w_in', j], bl=0, bias=b_in, tm=_tile(T, 1024), tn=_tile(2 * C, 1024, LANES),
                     tk=D, name=f"conv_in_{j}", n_outer=True, out_dtype=_ADT)
            wdw = pad_rows(full['conv_w_dw'][j], CONV_TAPS_PAD)
            dwo = _conv_fwd(h1, wdw, row(full['conv_b_dw'][j]), B=B, S=S, name=f"conv_dw_{j}")
            s_act, xhc, rsc, *y1 = _conv_tail_fwd(
                dwo, row(full['conv_ln_g'][j]), row(full['conv_ln_b'][j]), wts['conv_w_out', j],
                row(full['conv_b_out'][j]), cur, alpha, row(norm1_g[i]), row(norm1_b[i]), name=f"conv_out_ln_{j}")
            sv.update(h1=h1, wdw=wdw, act=s_act, xhc=xhc, rsc=rsc)
        else:
            b_in = row(_perm_cols(full['gmlp_b_in'][j]))
            pre = _mm(cur_b, wts['gmlp_w_in', j], bl=0, bias=b_in, tm=_tile(T, 1024), tn=_tile(2 * C, 1024, LANES),
                      tk=D, name=f"gmlp_in_{j}", n_outer=True, out_dtype=_ADT)
            bsb = jnp.repeat(gmlp_b_s[j].T, L, axis=1)
            us, xhv, rsv, *y1 = _gmlp_gate_fwd(
                pre, row(full['gmlp_ln_g'][j]), row(full['gmlp_ln_b'][j]), gmlp_w_s[j], bsb, wts['gmlp_w_out', j],
                row(full['gmlp_b_out'][j]), cur, alpha, row(norm1_g[i]), row(norm1_b[i]), name=f"gmlp_gate_{j}")
            sv.update(pre=pre, bsb=bsb, act=us, xhv=xhv, rsv=rsv)
        x1, x1b, xh1, rs1 = y1
        arrive(groups[2 * i + 1][0], [x1b])
        wdw3 = pad_rows(_perm_cols(full['ffn_w_dw'][i]), SUBLANES)
        bdw3 = row(_perm_cols(ffn_b_dw[i]))
        ffn_in = (x1b, wts['ffn_w_up', i], wts['ffn_w_down', i], row(_perm_cols(ffn_b_up[i])), wdw3, bdw3)
        first = _ffn_fwd_half(0, *ffn_in, S=S, name=f"ffn_fwd_a_{i}")
        ffn_tail = (x1, alpha, row(ffn_b_down[i]), row(norm2_g[i]), row(norm2_b[i]))
        sv.update(x1=x1, x1b=x1b, xh1=xh1, rs1=rs1, wdw3=wdw3)
        if i < depth - 1:
            hs, hcs, f_act, cur, cur_b, xh2, rs2 = _ffn_fwd_half(1, *ffn_in, S=S, name=f"ffn_fwd_b_{i}", prev=first,
                                                                 tail=ffn_tail)
            sv.update(xh2=xh2, rs2=rs2)
        else:
            hs, hcs, f_act, *sv['head'] = _ffn_fwd_half(1, *ffn_in, S=S, name=f"ffn_fwd_b_{i}", prev=first,
                                                         tail=ffn_tail, head=loss_target.reshape(T, D))
        sv.update(hs=hs, hcs=hcs, f=f_act)
        saved.append(sv)

    sg = {n: [None] * full[n].shape[0] for n in WEIGHTS if n not in BIG}
    inflight = {n: [None] * P[n].shape[0] for n in BIG}
    deps = []
    dcur = None
    loss_part = None
    tk_t = _tile(T, 2048)

    ready = []

    def wgrad(n, l, a_, b_, **kw):
        tk = tk_t if n.endswith('w_out') or n.endswith('w_down') else T
        ready.append((n, l, _mm(a_, b_, ta=True, out_dtype=_WIRE, tk=tk, name=f"{n}_dw_{l}", deps=deps, **kw)))
        launch(f"{n}_{l}")

    def launch(gname):
        send, recv, gs, lands, token = _rs_start([g for _, _, g in ready], name=f"rs_start_{gname}")
        group = {'name': gname, 'flight': (send, recv, gs, lands), 'landed': None}
        for a, (n, l, _) in enumerate(ready):
            inflight[n][l] = (group, a)
        del ready[:]
        deps.append(token)

    def landed(n, l):
        group, a = inflight[n][l]
        if group['landed'] is None:
            group['landed'] = _rs_wait(*group['flight'], dcur, name=f"rs_wait_{group['name']}")
        return group['landed'][0][a], group['landed'][1][a]

    for i in reversed(range(depth)):
        j = i // 2
        sv = saved[i]
        if i == depth - 1:
            dz2, dz2b, dg, db, cs, loss_part = sv['head']
        else:
            dz2, dz2b, dg, db, cs = dcur
        sg['norm2_g'][i], sg['norm2_b'][i], sg['ffn_b_down'][i] = dg.sum(0), db.sum(0), cs.sum(0)
        Fh = F2 // 2
        wgrad('ffn_w_down', i, sv['f'], dz2b, tm=Fh // 2, tn=_tile(D, 1024, LANES), pieces=('row',))
        ffn_in = (dz2b, wts['ffn_w_down', i], wts['ffn_w_up', i], sv['hs'], sv['hcs'], sv['wdw3'])
        dh0, csu0, dwd0, dbd0, dxp = _ffn_bwd_half(0, *ffn_in, S=S, name=f"ffn_bwd_a_{i}", dz=dz2, alpha=alpha)
        dh, csu1, dwd1, dbd1, dz1, dz1b, dg, db, cs = _ffn_bwd_half(
            1, *ffn_in, S=S, name=f"ffn_bwd_b_{i}", prev=(dh0, dxp), ln=(sv['xh1'], sv['rs1'], row(norm1_g[i])))
        sg['ffn_b_up'][i] = _perm_cols(jnp.concatenate([csu0.sum(0), csu1.sum(0)], axis=-1))
        sg['ffn_w_dw'][i] = _perm_cols(jnp.concatenate([dwd0.sum(1), dwd1.sum(1)], axis=-1))
        sg['ffn_b_dw'][i] = _perm_cols(jnp.concatenate([dbd0.sum(0), dbd1.sum(0)], axis=-1))
        wgrad('ffn_w_up', i, sv['x1b'], dh, tm=D // 2, tn=F2 // N_CHIPS, pieces=('col', True))
        sg['norm1_g'][i], sg['norm1_b'][i] = dg.sum(0), db.sum(0)
        if i % 2 == 0:
            sg['conv_b_out'][j] = cs.sum(0)
            wgrad('conv_w_out', j, sv['act'], dz1b, tm=_tile(C, 1024), tn=_tile(D, 1024, LANES), pieces=('row',))
            ddw, dg, db = _ln_silu_bwd(dz1b, wts['conv_w_out', j], sv['xhc'], sv['rsc'], row(full['conv_ln_g'][j]),
                                       row(full['conv_ln_b'][j]), name=f"conv_ln_bwd_{j}")
            sg['conv_ln_g'][j], sg['conv_ln_b'][j] = dg.sum(0), db.sum(0)
            dglu, dwk, dbk = _conv_bwd(ddw, sv['h1'], sv['wdw'], B=B, S=S, name=f"conv_dw_bwd_{j}")
            sg['conv_w_dw'][j] = dwk.sum(1)[:conv_w_dw.shape[1]]
            sg['conv_b_dw'][j] = dbk.sum(0)
            dh1, csi = _glu_bwd(dglu, sv['h1'], name=f"conv_glu_bwd_{j}")
            sg['conv_b_in'][j] = _perm_cols(csi.sum(0))
            fam = 'conv_w_in'
        else:
            sg['gmlp_b_out'][j] = cs.sum(0)
            wgrad('gmlp_w_out', j, sv['act'], dz1b, tm=_tile(C, 1024), tn=_tile(D, 1024, LANES), pieces=('row',))
            dh1, dg, db, csi, dws, dbs = _gmlp_gate_bwd(dz1b, wts['gmlp_w_out', j], sv['pre'], sv['xhv'], sv['rsv'],
                                                        row(full['gmlp_ln_g'][j]), row(full['gmlp_ln_b'][j]),
                                                        gmlp_w_s[j], sv['bsb'], name=f"gmlp_gate_bwd_{j}")
            sg['gmlp_ln_g'][j], sg['gmlp_ln_b'][j] = dg.sum(0), db.sum(0)
            sg['gmlp_b_in'][j] = _perm_cols(csi.sum(0))
            sg['gmlp_w_s'][j] = dws
            sg['gmlp_b_s'][j] = dbs.reshape(L, G, L).sum(-1).T
            fam = 'gmlp_w_in'
        wgrad(fam, j, sv['xb'], dh1, tm=D, tn=(2 * C) // N_CHIPS, pieces=('col', True))
        if i > 0:
            below = saved[i - 1]
            dcur = _mm_ln_bwd(dh1, wts[fam, j], dz1, alpha, below['xh2'], below['rs2'], row(norm2_g[i - 1]),
                              name=f"{fam}_dx_{j}", deps=deps)
        else:
            dcur = _mm(dh1, wts[fam, j], bl=0, tb=True, res=dz1, res_scale=alpha, tm=_tile(T, 512),
                       tn=_tile(D, 1024, LANES), tk=2 * C, name=f"{fam}_dx_{j}", deps=deps)
    grad_x = dcur.reshape(B, S, D)

    small_names = [n for n in WEIGHTS if n not in BIG]
    small_full = [jnp.stack(sg[n]) for n in small_names]
    flat = _pack(small_full + [loss_part])
    red = _allreduce_flat(flat, name="ar_small")
    red_parts = _unpack(red, [a.shape for a in small_full] + [loss_part.shape])
    loss = (0.5 / D) * jnp.sum(red_parts[-1])
    grads = {}
    for n, g in zip(small_names, red_parts[:-1]):
        if n in SMALL_SHARDED:
            ax = SMALL_SHARDED[n]
            width = P[n].shape[ax]
            g = lax.dynamic_slice_in_dim(g, shard * width, width, axis=ax)
        grads[n] = g

    big_out = {}
    for n in ['ffn_w_down', 'ffn_w_up', 'gmlp_w_out', 'gmlp_w_in', 'conv_w_out', 'conv_w_in']:
        both = [landed(n, l) for l in range(len(inflight[n]))]
        own = _sum_pieces([g for g, _ in both], [r for _, r in both], me_id, name=f"sum_{n}")
        got, = _pair_exchange([own], name=f"px_{n}")
        big_out[n] = _adam_halves(P[n], own, got, P['m_' + n], P['v_' + n], core_id, name=f"adam_{n}")

    shapes = [P[n].shape for n in small_names]
    n_small = sum(functools.reduce(lambda p_, d_: p_ * d_, s_, 1) for s_ in shapes)
    unit = SUBLANES * LANES
    npad = -(-n_small // unit) * unit

    def flat2d(arrs, fill=0.0):
        v = _pack(arrs)
        return jnp.pad(v, (0, npad - n_small), constant_values=fill).reshape(-1, LANES)

    dl, mo, vo = _adam(flat2d([P[n] for n in small_names]), flat2d([grads[n] for n in small_names]),
                       flat2d([P['m_' + n] for n in small_names]),
                       flat2d([P['v_' + n] for n in small_names], fill=1.0), name="adam_small")
    small_out = {n: [grads[n], None, None, None] for n in small_names}
    for k, t in enumerate((dl, mo, vo)):
        for n, a in zip(small_names, _unpack(t.reshape(-1), shapes)):
            small_out[n][k + 1] = a

    outs = [loss, grad_x]
    for k in range(4):
        for n in WEIGHTS:
            outs.append(big_out[n][k] if n in BIG else small_out[n][k])
    return tuple(outs)
```

```python
import functools

import jax
import jax.numpy as jnp
from jax import lax
from jax.experimental import pallas as pl
from jax.experimental.pallas import tpu as pltpu

F32 = jnp.float32
_MXU = jnp.bfloat16
_WIRE = jnp.bfloat16
_HDT = jnp.bfloat16
_ADT = jnp.bfloat16
_XDT = jnp.bfloat16
LN_EPS = 1e-5
ADAM_LR, ADAM_B1, ADAM_B2, ADAM_EPS, ADAM_WD, ADAM_STEP = 0.001, 0.9, 0.999, 1e-08, 0.01, 10
N_CHIPS = 4
N_DEV = 8
LANES = 128
SUBLANES = 8
CONV_TAPS_PAD = 32
VMEM_LIMIT = 56 << 20
MESH = pl.DeviceIdType.MESH
ANY = pl.BlockSpec(memory_space=pl.ANY)
HBM = pl.BlockSpec(memory_space=pltpu.HBM)
SEMS = pl.BlockSpec(memory_space=pltpu.SEMAPHORE)
EFFECT = pltpu.SideEffectType.DATAFLOW_SIDE_EFFECTING
PERM = (0, 2, 1, 3)


def _cp(sem=None):
    return pltpu.CompilerParams(dimension_semantics=sem, vmem_limit_bytes=VMEM_LIMIT)


def _tile(dim, pref, mult=SUBLANES):
    if dim <= pref:
        return dim
    t = (pref // mult) * mult
    while t > mult and dim % t:
        t -= mult
    assert dim % t == 0, (dim, pref, mult)
    return t


def _perm_idx(q):
    return (q % 2) * 2 + q // 2


def _fold8(t):
    r, n = t.shape
    return t.reshape(r // SUBLANES, SUBLANES, n).sum(axis=0)


def _ln_rows(z, g, b):
    mu = jnp.mean(z, axis=-1, keepdims=True)
    xc = z - mu
    var = jnp.mean(xc * xc, axis=-1, keepdims=True)
    rstd = lax.rsqrt(var + LN_EPS)
    xh = xc * rstd
    return xh * g + b, xh, rstd


def _ln_bwd_rows(dy, xh, rstd, g):
    dxh = dy * g
    m1 = jnp.mean(dxh, axis=-1, keepdims=True)
    m2 = jnp.mean(dxh * xh, axis=-1, keepdims=True)
    return rstd * (dxh - m1 - xh * m2)


def _sigmoid(v):
    return 0.5 * jnp.tanh(0.5 * v) + 0.5


def _gelu_parts(p):
    cdf = 0.5 * (1.0 + lax.erf(p * 0.7071067811865476))
    pdf = jnp.exp(-0.5 * p * p) * 0.3989422804014327
    return p * cdf, cdf + p * pdf


def _shift_down(prev8, t, s):
    ext = jnp.concatenate([prev8, t], axis=0)
    return pltpu.roll(ext, s, 0)[SUBLANES:]


def _shift_up(t, next8, s):
    n = t.shape[0]
    ext = jnp.concatenate([t, next8], axis=0)
    return pltpu.roll(ext, n + SUBLANES - s, 0)[:n]


def _mm(a, b, *, ta=False, tb=False, bl=None, bias=None, res=None, res_scale=1.0, out_dtype=F32,
        tm, tn, tk, name, pieces=None, deps=None, n_outer=False):
    M, K = (a.shape[1], a.shape[0]) if ta else a.shape
    bs = b.shape[1:] if bl is not None else b.shape
    N, Kb = (bs[0], bs[1]) if tb else (bs[1], bs[0])
    assert K == Kb and M % tm == 0 and N % tn == 0 and K % tk == 0, (a.shape, b.shape, tm, tn, tk)
    gm, gn, gk = M // tm, N // tn, K // tk

    def spec(block, imap):
        if n_outer:
            return pl.BlockSpec(block, lambda j, i, k: imap(i, j, k))
        return pl.BlockSpec(block, imap)

    a_spec = spec((tk, tm), lambda i, j, k: (k, i)) if ta else spec((tm, tk), lambda i, j, k: (i, k))
    bblk = (tn, tk) if tb else (tk, tn)
    bmap = (lambda i, j, k: (j, k)) if tb else (lambda i, j, k: (k, j))
    if bl is not None:
        b_spec = spec((None,) + bblk, lambda i, j, k: (bl,) + bmap(i, j, k))
    else:
        b_spec = spec(bblk, bmap)
    in_specs, operands = [a_spec, b_spec], [a, b]
    if bias is not None:
        in_specs.append(spec((1, tn), lambda i, j, k: (0, j)))
        operands.append(bias)
    if res is not None:
        in_specs.append(spec((tm, tn), lambda i, j, k: (i, j)))
        operands.append(res)
    n_dep = len(deps) if deps else 0
    if n_dep:
        in_specs += [ANY] * n_dep
        operands += deps
        del deps[:]
    if pieces is None:
        out_shape = jax.ShapeDtypeStruct((M, N), out_dtype)
        out_spec = spec((tm, tn), lambda i, j, k: (i, j))
        ppb = pr = None
    elif pieces[0] == 'col':
        pr, pc = M // 2, N // N_CHIPS
        assert tm % pr == 0 and pc % tn == 0
        ppb, per = tm // pr, pc // tn
        perm = pieces[1]
        out_shape = jax.ShapeDtypeStruct((N_DEV, pr, pc), out_dtype)
        out_spec = spec(
            (ppb, pr, tn),
            lambda i, j, k: ((2 * (_perm_idx(j // per) if perm else j // per)) // ppb + i, 0, j % per))
    else:
        pr = M // N_DEV
        assert tm % pr == 0
        ppb = tm // pr
        out_shape = jax.ShapeDtypeStruct((N_DEV, pr, N), out_dtype)
        out_spec = spec((ppb, pr, tn), lambda i, j, k: (i, 0, j))
    dims = (((0 if ta else 1,), (1 if tb else 0,)), ((), ()))

    def body(*refs):
        a_ref, b_ref = refs[0], refs[1]
        pos = 2
        bias_ref = res_ref = None
        if bias is not None:
            bias_ref = refs[pos]
            pos += 1
        if res is not None:
            res_ref = refs[pos]
            pos += 1
        pos += n_dep
        o_ref = refs[pos]

        def finish(r):
            if bias_ref is not None:
                r = r + bias_ref[...]
            if res_ref is not None:
                r = r + res_scale * res_ref[...]
            if pieces is not None:
                r = r.reshape(ppb, pr, tn)
            o_ref[...] = r.astype(out_dtype)

        part = lax.dot_general(a_ref[...].astype(_MXU), b_ref[...].astype(_MXU), dims, preferred_element_type=F32)
        if gk == 1:
            finish(part)
            return
        acc_ref = refs[pos + 1]
        k = pl.program_id(2)

        @pl.when(k == 0)
        def _():
            acc_ref[...] = part

        @pl.when((k > 0) & (k < gk - 1))
        def _():
            acc_ref[...] += part

        @pl.when(k == gk - 1)
        def _():
            finish(acc_ref[...] + part)

    return pl.pallas_call(
        body, name=name, grid=(gn, gm, gk) if n_outer else (gm, gn, gk), in_specs=in_specs, out_specs=out_spec,
        out_shape=out_shape, scratch_shapes=[pltpu.VMEM((tm, tn), F32)] if gk > 1 else [],
        compiler_params=_cp(("parallel", "parallel", "arbitrary")),
    )(*operands)


def _mm_ln_bwd(a, w, res, res_scale, xh, rstd, g, *, name, deps=None):
    T, K = a.shape
    D = w.shape[1]
    tm = _tile(T, 512)
    n_dep = len(deps) if deps else 0

    def body(a_ref, w_ref, res_ref, xh_ref, rs_ref, g_ref, *rest):
        dz_ref, dzb_ref, dg_ref, db_ref, cs_ref = rest[n_dep:]

        @pl.when(pl.program_id(0) == 0)
        def _():
            dg_ref[...] = jnp.zeros_like(dg_ref)
            db_ref[...] = jnp.zeros_like(db_ref)
            cs_ref[...] = jnp.zeros_like(cs_ref)

        d = lax.dot_general(a_ref[...].astype(_MXU), w_ref[...].astype(_MXU), (((1,), (1,)), ((), ())),
                            preferred_element_type=F32) + res_scale * res_ref[...]
        xh = xh_ref[...].astype(F32)
        dz = _ln_bwd_rows(d, xh, rs_ref[...], g_ref[...])
        dz_ref[...] = dz
        dzb_ref[...] = dz.astype(_MXU)
        dg_ref[...] += _fold8(d * xh)
        db_ref[...] += _fold8(d)
        cs_ref[...] += _fold8(dz)

    row = lambda i: (i, 0)
    fixed = lambda i: (0, 0)
    tile = pl.BlockSpec((tm, D), row)
    part = pl.BlockSpec((SUBLANES, D), fixed)
    operands = [a, w, res, xh, rstd, g] + (list(deps) if deps else [])
    if deps:
        del deps[:]
    return pl.pallas_call(
        body, name=name, grid=(T // tm,),
        in_specs=[pl.BlockSpec((tm, K), row),
                  pl.BlockSpec((None, D, K), lambda i: (0, 0, 0), pipeline_mode=pl.Buffered(1)),
                  tile, tile, pl.BlockSpec((tm, 1), row), pl.BlockSpec((1, D), fixed)] + [ANY] * n_dep,
        out_specs=[tile, tile, part, part, part],
        out_shape=[jax.ShapeDtypeStruct((T, D), F32), jax.ShapeDtypeStruct((T, D), _MXU)]
        + [jax.ShapeDtypeStruct((SUBLANES, D), F32)] * 3,
        compiler_params=_cp(("arbitrary",)),
    )(*operands)


def _out_ln(act, wo_ref, bias_ref, res_ref, alpha, g_ref, b_ref, y_ref, yb_ref, xh_ref, rs_ref):
    z = jnp.dot(act, wo_ref[...].astype(_MXU), preferred_element_type=F32) + bias_ref[...] + alpha * res_ref[...]
    y, xh, rstd = _ln_rows(z, g_ref[...], b_ref[...])
    y_ref[...] = y
    yb_ref[...] = y.astype(_MXU)
    xh_ref[...] = xh.astype(_XDT)
    rs_ref[...] = rstd


def _conv_tail_fwd(v, gc, bc, w, bias, res, alpha, g, b, *, name):
    T, C = v.shape
    D = w.shape[-1]
    tm = _tile(T, 512)

    def body(v_ref, gc_ref, bc_ref, w_ref, bias_ref, res_ref, g_ref, b_ref,
             s_ref, xhc_ref, rsc_ref, y_ref, yb_ref, xh_ref, rs_ref):
        yv, xhc, rsc = _ln_rows(v_ref[...], gc_ref[...], bc_ref[...])
        s = (yv * _sigmoid(yv)).astype(_MXU)
        s_ref[...] = s
        xhc_ref[...] = xhc.astype(_XDT)
        rsc_ref[...] = rsc
        _out_ln(s, w_ref, bias_ref, res_ref, alpha, g_ref, b_ref, y_ref, yb_ref, xh_ref, rs_ref)

    row = lambda i: (i, 0)
    fixed = lambda i: (0, 0)
    vc, vd = pl.BlockSpec((1, C), fixed), pl.BlockSpec((1, D), fixed)
    tc_, td = pl.BlockSpec((tm, C), row), pl.BlockSpec((tm, D), row)
    one = pl.BlockSpec((tm, 1), row)
    return pl.pallas_call(
        body, name=name, grid=(T // tm,),
        in_specs=[tc_, vc, vc, _resident((None, C, D), lambda i: (0, 0, 0)), vd, td, vd, vd],
        out_specs=[tc_, tc_, one, td, td, td, one],
        out_shape=[jax.ShapeDtypeStruct((T, C), _MXU), jax.ShapeDtypeStruct((T, C), _XDT),
                   jax.ShapeDtypeStruct((T, 1), F32), jax.ShapeDtypeStruct((T, D), F32),
                   jax.ShapeDtypeStruct((T, D), _MXU), jax.ShapeDtypeStruct((T, D), _XDT),
                   jax.ShapeDtypeStruct((T, 1), F32)],
        compiler_params=_cp(("parallel",)),
    )(v, gc, bc, w, bias, res, g, b)


def _conv_cols(C, tc):
    per = (C // 2) // tc
    return per, (lambda j: (j // per) * (2 * per) + j % per)


def _glu_shifted(a_ref, g_ref, p_ref, S):
    u = a_ref[...].astype(F32) * _sigmoid(g_ref[...].astype(F32))
    rows = lax.broadcasted_iota(jnp.int32, (SUBLANES, u.shape[1]), 0)
    lo = CONV_TAPS_PAD
    for r in range(SUBLANES):
        p_ref[r, 0:lo, :] = jnp.zeros((lo, u.shape[1]), F32)
        if r == 0:
            p_ref[r, lo:lo + S, :] = u
        else:
            rolled = pltpu.roll(u, r, 0)
            p_ref[r, lo:lo + S, :] = rolled
            p_ref[r, lo:lo + SUBLANES, :] = jnp.where(rows >= r, rolled[0:SUBLANES], 0.0)


def _conv_fwd(h1, w_dw, b_dw, *, B, S, name):
    C = w_dw.shape[1]
    taps = CONV_TAPS_PAD - 1
    tc = LANES
    ch = _tile(S, 128)
    per, col_a = _conv_cols(C, tc)

    def body(a_ref, g_ref, w_ref, b_ref, o_ref, p_ref):
        _glu_shifted(a_ref, g_ref, p_ref, S)

        def chunk(ci, carry):
            base = pl.multiple_of(ci * ch, ch)
            acc = jnp.zeros((ch, tc), F32) + b_ref[...]
            for k in range(taps):
                q, r = divmod(taps - 1 - k, SUBLANES)
                start = pl.multiple_of(base + (CONV_TAPS_PAD - SUBLANES * q), SUBLANES)
                acc = acc + w_ref[pl.ds(k, 1), :] * p_ref[r, pl.ds(start, ch), :]
            o_ref[pl.ds(base, ch), :] = acc
            return carry

        lax.fori_loop(0, S // ch, chunk, 0)

    return pl.pallas_call(
        body, name=name, grid=(B, C // tc),
        in_specs=[pl.BlockSpec((S, tc), lambda b, j: (b, col_a(j))),
                  pl.BlockSpec((S, tc), lambda b, j: (b, col_a(j) + per)),
                  pl.BlockSpec((CONV_TAPS_PAD, tc), lambda b, j: (0, j)),
                  pl.BlockSpec((1, tc), lambda b, j: (0, j))],
        out_specs=pl.BlockSpec((S, tc), lambda b, j: (b, j)),
        out_shape=jax.ShapeDtypeStruct((B * S, C), F32),
        scratch_shapes=[pltpu.VMEM((SUBLANES, S + CONV_TAPS_PAD, tc), F32)],
        compiler_params=_cp(("parallel", "parallel")),
    )(h1, h1, w_dw, b_dw)


def _conv_bwd(dd, h1, w_dw, *, B, S, name):
    C = w_dw.shape[1]
    taps = CONV_TAPS_PAD - 1
    tc = LANES
    ch = _tile(S, 128)
    per, col_a = _conv_cols(C, tc)

    def body(d_ref, a_ref, g_ref, w_ref, du_ref, dw_ref, db_ref, p_ref, q_ref):
        b = pl.program_id(1)

        @pl.when(b == 0)
        def _():
            dw_ref[...] = jnp.zeros_like(dw_ref)
            db_ref[...] = jnp.zeros_like(db_ref)

        _glu_shifted(a_ref, g_ref, p_ref, S)
        d = d_ref[...]
        rows = lax.broadcasted_iota(jnp.int32, (SUBLANES, tc), 0)
        for r in range(SUBLANES):
            q_ref[r, S:S + CONV_TAPS_PAD, :] = jnp.zeros((CONV_TAPS_PAD, tc), F32)
            if r == 0:
                q_ref[r, 0:S, :] = d
            else:
                rolled = pltpu.roll(d, S - r, 0)
                q_ref[r, 0:S, :] = rolled
                q_ref[r, S - SUBLANES:S, :] = jnp.where(rows < SUBLANES - r, rolled[S - SUBLANES:S], 0.0)
        db_ref[...] += _fold8(d)

        def chunk(ci, carry):
            base = pl.multiple_of(ci * ch, ch)
            dch = d_ref[pl.ds(base, ch), :]
            acc = jnp.zeros((ch, tc), F32)
            for k in range(taps):
                q, r = divmod(taps - 1 - k, SUBLANES)
                up = pl.multiple_of(base + SUBLANES * q, SUBLANES)
                acc = acc + w_ref[pl.ds(k, 1), :] * q_ref[r, pl.ds(up, ch), :]
                down = pl.multiple_of(base + (CONV_TAPS_PAD - SUBLANES * q), SUBLANES)
                dw_ref[k] += _fold8(dch * p_ref[r, pl.ds(down, ch), :])
            du_ref[pl.ds(base, ch), :] = acc
            return carry

        lax.fori_loop(0, S // ch, chunk, 0)

    return pl.pallas_call(
        body, name=name, grid=(C // tc, B),
        in_specs=[pl.BlockSpec((S, tc), lambda j, b: (b, j)),
                  pl.BlockSpec((S, tc), lambda j, b: (b, col_a(j))),
                  pl.BlockSpec((S, tc), lambda j, b: (b, col_a(j) + per)),
                  pl.BlockSpec((CONV_TAPS_PAD, tc), lambda j, b: (0, j))],
        out_specs=[pl.BlockSpec((S, tc), lambda j, b: (b, j)),
                   pl.BlockSpec((CONV_TAPS_PAD, SUBLANES, tc), lambda j, b: (0, 0, j)),
                   pl.BlockSpec((SUBLANES, tc), lambda j, b: (0, j))],
        out_shape=[jax.ShapeDtypeStruct((B * S, C), F32),
                   jax.ShapeDtypeStruct((CONV_TAPS_PAD, SUBLANES, C), F32),
                   jax.ShapeDtypeStruct((SUBLANES, C), F32)],
        scratch_shapes=[pltpu.VMEM((SUBLANES, S + CONV_TAPS_PAD, tc), F32),
                        pltpu.VMEM((SUBLANES, S + CONV_TAPS_PAD, tc), F32)],
        compiler_params=_cp(("parallel", "arbitrary")),
    )(dd, h1, h1, w_dw)


def _ln_silu_bwd(dzb, w, xh, rstd, g, b, *, name):
    T, D = dzb.shape
    C = w.shape[1]
    tm = _tile(T, 512)

    def body(dz_ref, w_ref, xh_ref, rs_ref, g_ref, b_ref, dv_ref, dg_ref, db_ref):
        @pl.when(pl.program_id(0) == 0)
        def _():
            dg_ref[...] = jnp.zeros_like(dg_ref)
            db_ref[...] = jnp.zeros_like(db_ref)

        ds = lax.dot_general(dz_ref[...].astype(_MXU), w_ref[...].astype(_MXU), (((1,), (1,)), ((), ())),
                             preferred_element_type=F32)
        xh = xh_ref[...].astype(F32)
        gam = g_ref[...]
        y = xh * gam + b_ref[...]
        sig = _sigmoid(y)
        dln = ds * (sig * (1.0 + y * (1.0 - sig)))
        dv_ref[...] = _ln_bwd_rows(dln, xh, rs_ref[...], gam)
        dg_ref[...] += _fold8(dln * xh)
        db_ref[...] += _fold8(dln)

    row = lambda i: (i, 0)
    fixed = lambda i: (0, 0)
    vec = pl.BlockSpec((1, C), fixed)
    part = pl.BlockSpec((SUBLANES, C), fixed)
    return pl.pallas_call(
        body, name=name, grid=(T // tm,),
        in_specs=[pl.BlockSpec((tm, D), row), _resident((None, C, D), lambda i: (0, 0, 0)),
                  pl.BlockSpec((tm, C), row), pl.BlockSpec((tm, 1), row), vec, vec],
        out_specs=[pl.BlockSpec((tm, C), row), part, part],
        out_shape=[jax.ShapeDtypeStruct((T, C), F32)] + [jax.ShapeDtypeStruct((SUBLANES, C), F32)] * 2,
        compiler_params=_cp(("arbitrary",)),
    )(dzb, w, xh, rstd, g, b)


def _glu_bwd(du, h1, *, name):
    T, C = du.shape
    il = C // 2
    tm = _tile(T, 512)

    def body(du_ref, h_ref, dh_ref, cs_ref):
        @pl.when(pl.program_id(0) == 0)
        def _():
            cs_ref[...] = jnp.zeros_like(cs_ref)

        for hb in range(2):
            a = h_ref[:, 2 * hb * il:(2 * hb + 1) * il].astype(F32)
            gate = h_ref[:, (2 * hb + 1) * il:(2 * hb + 2) * il].astype(F32)
            d = du_ref[:, hb * il:(hb + 1) * il]
            sig = _sigmoid(gate)
            da = d * sig
            dgate = d * a * sig * (1.0 - sig)
            dh_ref[:, 2 * hb * il:(2 * hb + 1) * il] = da.astype(_MXU)
            dh_ref[:, (2 * hb + 1) * il:(2 * hb + 2) * il] = dgate.astype(_MXU)
            cs_ref[:, 2 * hb * il:(2 * hb + 1) * il] += _fold8(da)
            cs_ref[:, (2 * hb + 1) * il:(2 * hb + 2) * il] += _fold8(dgate)

    row = lambda i: (i, 0)
    return pl.pallas_call(
        body, name=name, grid=(T // tm,),
        in_specs=[pl.BlockSpec((tm, C), row), pl.BlockSpec((tm, 2 * C), row)],
        out_specs=[pl.BlockSpec((tm, 2 * C), row), pl.BlockSpec((SUBLANES, 2 * C), lambda i: (0, 0))],
        out_shape=[jax.ShapeDtypeStruct((T, 2 * C), _MXU), jax.ShapeDtypeStruct((SUBLANES, 2 * C), F32)],
        compiler_params=_cp(("arbitrary",)),
    )(du, h1)


def _tril_mask(n):
    return lax.broadcasted_iota(jnp.int32, (n, n), 0) >= lax.broadcasted_iota(jnp.int32, (n, n), 1)


def _split_uv(t, il):
    u = jnp.concatenate([t[:, 0:il], t[:, 2 * il:3 * il]], axis=1)
    v = jnp.concatenate([t[:, il:2 * il], t[:, 3 * il:4 * il]], axis=1)
    return u, v


def _gmlp_gate_fwd(p, g, b, w_s, bsb, w_out, bias, res, alpha, g1, b1, *, name):
    T, C2 = p.shape
    C = C2 // 2
    D = w_out.shape[-1]
    il = C // 2
    G, L, _ = w_s.shape
    assert G * L == C
    tm = _tile(T, 4 * L, L)

    def body(p_ref, g_ref, b_ref, ws_ref, bs_ref, wo_ref, bias_ref, res_ref, g1_ref, b1_ref,
             us_ref, xh_ref, rs_ref, y_ref, yb_ref, xh1_ref, rs1_ref, vn_ref, u_ref):
        z, _ = _gelu_parts(p_ref[...].astype(F32))
        u, v = _split_uv(z, il)
        vn, xh, rstd = _ln_rows(v, g_ref[...], b_ref[...])
        xh_ref[...] = xh.astype(_XDT)
        rs_ref[...] = rstd
        vn_ref[...] = vn.astype(_MXU)
        u_ref[...] = u
        mask = _tril_mask(L)
        for gi in range(G):
            wc = jnp.where(mask, ws_ref[gi], 0.0).astype(_MXU)
            cols = slice(gi * L, (gi + 1) * L)
            for c in range(tm // L):
                rows = slice(c * L, (c + 1) * L)
                s = jnp.dot(wc, vn_ref[rows, cols], preferred_element_type=F32) + bs_ref[:, cols]
                us_ref[rows, cols] = (u_ref[rows, cols] * s).astype(_MXU)
        _out_ln(us_ref[...], wo_ref, bias_ref, res_ref, alpha, g1_ref, b1_ref, y_ref, yb_ref, xh1_ref, rs1_ref)

    row = lambda i: (i, 0)
    fixed = lambda i: (0, 0)
    vd, td, one = pl.BlockSpec((1, D), fixed), pl.BlockSpec((tm, D), row), pl.BlockSpec((tm, 1), row)
    return pl.pallas_call(
        body, name=name, grid=(T // tm,),
        in_specs=[pl.BlockSpec((tm, C2), row), pl.BlockSpec((1, C), fixed), pl.BlockSpec((1, C), fixed),
                  pl.BlockSpec((G, L, L), lambda i: (0, 0, 0)), pl.BlockSpec((L, C), fixed),
                  _resident((None, C, D), lambda i: (0, 0, 0)), vd, td, vd, vd],
        out_specs=[pl.BlockSpec((tm, C), row), pl.BlockSpec((tm, C), row), one, td, td, td, one],
        out_shape=[jax.ShapeDtypeStruct((T, C), _MXU), jax.ShapeDtypeStruct((T, C), _XDT),
                   jax.ShapeDtypeStruct((T, 1), F32), jax.ShapeDtypeStruct((T, D), F32),
                   jax.ShapeDtypeStruct((T, D), _MXU), jax.ShapeDtypeStruct((T, D), _XDT),
                   jax.ShapeDtypeStruct((T, 1), F32)],
        scratch_shapes=[pltpu.VMEM((tm, C), _MXU), pltpu.VMEM((tm, C), F32)],
        compiler_params=_cp(("parallel",)),
    )(p, g, b, w_s, bsb, w_out, bias, res, g1, b1)


def _gmlp_gate_bwd(dzb, w_out, p, xh, rstd, g, b, w_s, bsb, *, name):
    T, C2 = p.shape
    D = dzb.shape[1]
    C = C2 // 2
    il = C // 2
    G, L, _ = w_s.shape
    tm = _tile(T, 4 * L, L)

    def body(dz_ref, wo_ref, p_ref, xh_ref, rs_ref, g_ref, b_ref, ws_ref, bs_ref,
             dp_ref, dg_ref, db_ref, cs_ref, dws_ref, dbs_ref, vn_ref, u_ref, dvn_ref, du_ref, dus_ref):
        @pl.when(pl.program_id(0) == 0)
        def _():
            dg_ref[...] = jnp.zeros_like(dg_ref)
            db_ref[...] = jnp.zeros_like(db_ref)
            cs_ref[...] = jnp.zeros_like(cs_ref)
            dws_ref[...] = jnp.zeros_like(dws_ref)
            dbs_ref[...] = jnp.zeros_like(dbs_ref)

        dus_ref[...] = lax.dot_general(dz_ref[...].astype(_MXU), wo_ref[...].astype(_MXU), (((1,), (1,)), ((), ())),
                                       preferred_element_type=F32)
        z, gp = _gelu_parts(p_ref[...].astype(F32))
        u, _ = _split_uv(z, il)
        xh = xh_ref[...].astype(F32)
        gam = g_ref[...]
        vn_ref[...] = (xh * gam + b_ref[...]).astype(_MXU)
        u_ref[...] = u
        mask = _tril_mask(L)
        for gi in range(G):
            wc = jnp.where(mask, ws_ref[gi], 0.0).astype(_MXU)
            cols = slice(gi * L, (gi + 1) * L)
            for c in range(tm // L):
                rows = slice(c * L, (c + 1) * L)
                vnb = vn_ref[rows, cols]
                s = jnp.dot(wc, vnb, preferred_element_type=F32) + bs_ref[:, cols]
                d = dus_ref[rows, cols]
                du_ref[rows, cols] = d * s
                ds = d * u_ref[rows, cols]
                dbs_ref[:, cols] += ds
                dsb = ds.astype(_MXU)
                dw = lax.dot_general(dsb, vnb, (((1,), (1,)), ((), ())), preferred_element_type=F32)
                dws_ref[gi] += jnp.where(mask, dw, 0.0)
                dvn_ref[rows, cols] = lax.dot_general(wc, dsb, (((0,), (0,)), ((), ())), preferred_element_type=F32)
        dvn = dvn_ref[...]
        dg_ref[...] += _fold8(dvn * xh)
        db_ref[...] += _fold8(dvn)
        dv = _ln_bwd_rows(dvn, xh, rs_ref[...], gam)
        du = du_ref[...]
        for hb in range(2):
            for part, src in ((0, du), (1, dv)):
                lo = (2 * hb + part) * il
                dp = src[:, hb * il:(hb + 1) * il] * gp[:, lo:lo + il]
                dp_ref[:, lo:lo + il] = dp.astype(_MXU)
                cs_ref[:, lo:lo + il] += _fold8(dp)

    row = lambda i: (i, 0)
    fixed = lambda i: (0, 0)
    part_c = pl.BlockSpec((SUBLANES, C), fixed)
    return pl.pallas_call(
        body, name=name, grid=(T // tm,),
        in_specs=[pl.BlockSpec((tm, D), row), _resident((None, C, D), lambda i: (0, 0, 0)),
                  pl.BlockSpec((tm, C2), row), pl.BlockSpec((tm, C), row),
                  pl.BlockSpec((tm, 1), row), pl.BlockSpec((1, C), fixed), pl.BlockSpec((1, C), fixed),
                  pl.BlockSpec((G, L, L), lambda i: (0, 0, 0)), pl.BlockSpec((L, C), fixed)],
        out_specs=[pl.BlockSpec((tm, C2), row), part_c, part_c, pl.BlockSpec((SUBLANES, C2), fixed),
                   pl.BlockSpec((G, L, L), lambda i: (0, 0, 0)), pl.BlockSpec((L, C), fixed)],
        out_shape=[jax.ShapeDtypeStruct((T, C2), _MXU), jax.ShapeDtypeStruct((SUBLANES, C), F32),
                   jax.ShapeDtypeStruct((SUBLANES, C), F32), jax.ShapeDtypeStruct((SUBLANES, C2), F32),
                   jax.ShapeDtypeStruct((G, L, L), F32), jax.ShapeDtypeStruct((L, C), F32)],
        scratch_shapes=[pltpu.VMEM((tm, C), _MXU), pltpu.VMEM((tm, C), F32), pltpu.VMEM((tm, C), F32),
                        pltpu.VMEM((tm, C), F32), pltpu.VMEM((tm, C), F32)],
        compiler_params=_cp(("arbitrary",)),
    )(dzb, w_out, p, xh, rstd, g, b, w_s, bsb)


def _ffn_conv(h, prev8, w_ref, b_ref):
    h1 = _shift_down(prev8, h, 1)
    h2 = _shift_down(prev8, h, 2)
    return w_ref[pl.ds(2, 1), :] * h + w_ref[pl.ds(1, 1), :] * h1 + w_ref[pl.ds(0, 1), :] * h2 + b_ref[...]


def _resident(block, imap):
    return pl.BlockSpec(block, imap, pipeline_mode=pl.Buffered(1))


def _ffn_fwd_half(j, xb, w_up, w_down, b_up, w_dw, b_dw, *, S, name, prev=None, tail=None, head=None):
    T, D = xb.shape
    N = w_up.shape[-1]
    tn = N // N_CHIPS
    tm = _tile(S, 256)
    spt = S // tm
    last = prev is not None
    alpha = tail[1] if last else None

    def body(*refs):
        x_ref, wu_ref, wd_ref, bu_ref, wc_ref, bc_ref = refs[:6]
        if last:
            yp_ref, res_ref, bd_ref, g_ref, b_ref = refs[9:14]
            o = 14 if head is None else 15
            h_ref, hc_ref, f_ref, y_ref, yb_ref, xh_ref, rs_ref = refs[o:o + 7]
            carry_ref = refs[-1]
        else:
            h_ref, hc_ref, f_ref, yp_ref, carry_ref = refs[6:11]

        @pl.when(pl.program_id(0) % spt == 0)
        def _():
            carry_ref[...] = jnp.zeros_like(carry_ref)

        h = jnp.dot(x_ref[...].astype(_MXU), wu_ref[...].astype(_MXU), preferred_element_type=F32) + bu_ref[...]
        h_ref[...] = h.astype(_HDT)
        hc = _ffn_conv(h, carry_ref[...], wc_ref, bc_ref)
        hc_ref[...] = hc.astype(_HDT)
        carry_ref[...] = h[tm - SUBLANES:tm]
        gte = hc[:, :tn]
        f = (gte * _sigmoid(gte) * hc[:, tn:]).astype(_MXU)
        f_ref[...] = f
        y = jnp.dot(f, wd_ref[...].astype(_MXU), preferred_element_type=F32)
        if not last:
            yp_ref[...] = y
            return
        z = y + yp_ref[...] + bd_ref[...] + alpha * res_ref[...]
        out, xh, rstd = _ln_rows(z, g_ref[...], b_ref[...])
        if head is None:
            y_ref[...] = out
            yb_ref[...] = out.astype(_MXU)
            xh_ref[...] = xh.astype(_XDT)
            rs_ref[...] = rstd
            return
        t_ref, cs_ref, ls_ref = refs[14], refs[o + 7], refs[o + 8]

        @pl.when(pl.program_id(0) == 0)
        def _():
            for acc in (xh_ref, rs_ref, cs_ref, ls_ref):
                acc[...] = jnp.zeros_like(acc)

        err = out - t_ref[...]
        d = err * (1.0 / D)
        dz = _ln_bwd_rows(d, xh, rstd, g_ref[...])
        y_ref[...] = dz
        yb_ref[...] = dz.astype(_MXU)
        xh_ref[...] += _fold8(d * xh)
        rs_ref[...] += _fold8(d)
        cs_ref[...] += _fold8(dz)
        ls_ref[...] += _fold8(err * err)

    row = lambda i: (i, 0)
    pair = lambda i: (0, j)
    vec = pl.BlockSpec((1, D), lambda i: (0, 0))
    tile = pl.BlockSpec((tm, D), row)
    in_specs = [tile, _resident((None, D, 2 * tn), lambda i: (0, 0, j)), _resident((None, tn, D), lambda i: (0, j, 0)),
                pl.BlockSpec((1, 2 * tn), pair), pl.BlockSpec((SUBLANES, 2 * tn), pair), pl.BlockSpec((1, 2 * tn), pair)]
    operands = [xb, w_up, w_down, b_up, w_dw, b_dw]
    wide = pl.BlockSpec((tm, 2 * tn), lambda i: (i, j))
    out_specs = [wide, wide, pl.BlockSpec((tm, tn), lambda i: (i, j))]
    out_shape = [jax.ShapeDtypeStruct((T, N), _HDT), jax.ShapeDtypeStruct((T, N), _HDT),
                 jax.ShapeDtypeStruct((T, N // 2), _MXU)]
    aliases = {}
    if last:
        res, _, b_down, g, b = tail
        in_specs += [ANY, ANY, ANY, tile, tile, vec, vec, vec]
        operands += list(prev) + [res, b_down, g, b]
        aliases = {6: 0, 7: 1, 8: 2}
        if head is None:
            out_specs += [tile, tile, tile, pl.BlockSpec((tm, 1), row)]
            out_shape += [jax.ShapeDtypeStruct((T, D), F32), jax.ShapeDtypeStruct((T, D), _MXU),
                          jax.ShapeDtypeStruct((T, D), _XDT), jax.ShapeDtypeStruct((T, 1), F32)]
        else:
            in_specs.append(tile)
            operands.append(head)
            part = pl.BlockSpec((SUBLANES, D), lambda i: (0, 0))
            out_specs += [tile, tile, part, part, part, part]
            out_shape += [jax.ShapeDtypeStruct((T, D), F32), jax.ShapeDtypeStruct((T, D), _MXU)] \
                + [jax.ShapeDtypeStruct((SUBLANES, D), F32)] * 4
    else:
        out_specs.append(tile)
        out_shape.append(jax.ShapeDtypeStruct((T, D), F32))
    return pl.pallas_call(
        body, name=name, grid=(T // tm,), in_specs=in_specs, out_specs=out_specs, out_shape=out_shape,
        input_output_aliases=aliases, scratch_shapes=[pltpu.VMEM((SUBLANES, 2 * tn), F32)],
        compiler_params=_cp(("arbitrary",)),
    )(*operands)


def _ffn_bwd_half(j, dzb, w_down, w_up, hs, hcs, w_dw, *, S, name, dz=None, alpha=None, prev=None, ln=None):
    T, D = dzb.shape
    N = hs.shape[1]
    tn = N // N_CHIPS
    tm = _tile(S, 256)
    spt = S // tm
    nt = T // tm
    last = prev is not None

    def body(*refs):
        dz_ref, wd_ref, wu_ref, h_ref, hc_ref, wc_ref = refs[:6]
        if last:
            dxp_ref, xh_ref, rs_ref, g_ref = refs[7:11]
            dh_ref, cs_ref, dw_ref, db_ref, dz1_ref, dz1b_ref, dg1_ref, db1_ref, cs1_ref, carry_ref = refs[11:21]
        else:
            dzf_ref = refs[6]
            dh_ref, cs_ref, dw_ref, db_ref, dxp_ref, carry_ref = refs[7:13]
        i = pl.program_id(0)
        ii = nt - 1 - i

        @pl.when(i == 0)
        def _():
            cs_ref[...] = jnp.zeros_like(cs_ref)
            dw_ref[...] = jnp.zeros_like(dw_ref)
            db_ref[...] = jnp.zeros_like(db_ref)
            if last:
                dg1_ref[...] = jnp.zeros_like(dg1_ref)
                db1_ref[...] = jnp.zeros_like(db1_ref)
                cs1_ref[...] = jnp.zeros_like(cs1_ref)

        df = lax.dot_general(dz_ref[...].astype(_MXU), wd_ref[...].astype(_MXU), (((1,), (1,)), ((), ())),
                             preferred_element_type=F32)
        h = h_ref[...].astype(F32)
        gte, val = hc_ref[:, :tn].astype(F32), hc_ref[:, tn:].astype(F32)
        sig = _sigmoid(gte)
        dval = df * (gte * sig)
        dg = df * val * (sig * (1.0 + gte * (1.0 - sig)))
        dhc = jnp.concatenate([dg, dval], axis=1)
        nxt = jnp.where((ii + 1) % spt == 0, 0.0, carry_ref[...])
        d1 = _shift_up(dhc, nxt, 1)
        d2 = _shift_up(dhc, nxt, 2)
        carry_ref[...] = dhc[0:SUBLANES]
        db_ref[...] += _fold8(dhc)
        dw_ref[2] += _fold8(dhc * h)
        dw_ref[1] += _fold8(d1 * h)
        dw_ref[0] += _fold8(d2 * h)
        dh = wc_ref[pl.ds(2, 1), :] * dhc + wc_ref[pl.ds(1, 1), :] * d1 + wc_ref[pl.ds(0, 1), :] * d2
        cs_ref[...] += _fold8(dh)
        dhb = dh.astype(_MXU)
        dh_ref[...] = dhb
        dx = lax.dot_general(dhb, wu_ref[...].astype(_MXU), (((1,), (1,)), ((), ())), preferred_element_type=F32)
        if not last:
            dxp_ref[...] = dx + alpha * dzf_ref[...]
            return
        d = dx + dxp_ref[...]
        xh = xh_ref[...].astype(F32)
        dz1 = _ln_bwd_rows(d, xh, rs_ref[...], g_ref[...])
        dz1_ref[...] = dz1
        dz1b_ref[...] = dz1.astype(_MXU)
        dg1_ref[...] += _fold8(d * xh)
        db1_ref[...] += _fold8(d)
        cs1_ref[...] += _fold8(dz1)

    rev = lambda i: (nt - 1 - i, 0)
    fixed = lambda i: (0, 0)
    pair = lambda i: (0, j)
    tile = pl.BlockSpec((tm, D), rev)
    wide = pl.BlockSpec((tm, 2 * tn), lambda i: (nt - 1 - i, j))
    part = pl.BlockSpec((SUBLANES, 2 * tn), fixed)
    in_specs = [tile, _resident((None, tn, D), lambda i: (0, j, 0)), _resident((None, D, 2 * tn), lambda i: (0, 0, j)),
                wide, wide, pl.BlockSpec((SUBLANES, 2 * tn), pair)]
    operands = [dzb, w_down, w_up, hs, hcs, w_dw]
    out_specs = [wide, part, pl.BlockSpec((3, SUBLANES, 2 * tn), lambda i: (0, 0, 0)), part]
    out_shape = [jax.ShapeDtypeStruct((T, N), _MXU), jax.ShapeDtypeStruct((SUBLANES, 2 * tn), F32),
                 jax.ShapeDtypeStruct((3, SUBLANES, 2 * tn), F32), jax.ShapeDtypeStruct((SUBLANES, 2 * tn), F32)]
    aliases = {}
    if last:
        xh, rstd, g = ln
        in_specs += [ANY, tile, tile, pl.BlockSpec((tm, 1), rev), pl.BlockSpec((1, D), fixed)]
        operands += [prev[0], prev[1], xh, rstd, g]
        aliases = {6: 0}
        out_specs += [tile, tile] + [pl.BlockSpec((SUBLANES, D), fixed)] * 3
        out_shape += [jax.ShapeDtypeStruct((T, D), F32), jax.ShapeDtypeStruct((T, D), _MXU)] \
            + [jax.ShapeDtypeStruct((SUBLANES, D), F32)] * 3
    else:
        in_specs.append(tile)
        operands.append(dz)
        out_specs.append(tile)
        out_shape.append(jax.ShapeDtypeStruct((T, D), F32))
    return pl.pallas_call(
        body, name=name, grid=(nt,), in_specs=in_specs, out_specs=out_specs, out_shape=out_shape,
        input_output_aliases=aliases, scratch_shapes=[pltpu.VMEM((SUBLANES, 2 * tn), F32)],
        compiler_params=_cp(("arbitrary",)),
    )(*operands)


def _sum_pieces(gs, rs, me, *, name):
    n = len(gs)
    _, pr, pc = gs[0].shape
    tr = _tile(pr, 128)

    def body(me_ref, *refs):
        o_ref = refs[2 * n]
        for l in range(n):
            total = refs[l][...].astype(F32)
            for s in range(N_DEV - 1):
                total = total + refs[n + l][s].astype(F32)
            o_ref[l] = total

    own = pl.BlockSpec((None, tr, pc), lambda i, me_ref: (me_ref[0], i, 0))
    got = pl.BlockSpec((N_DEV - 1, tr, pc), lambda i, me_ref: (0, i, 0))
    return pl.pallas_call(
        body, name=name,
        grid_spec=pltpu.PrefetchScalarGridSpec(
            num_scalar_prefetch=1, grid=(pr // tr,), in_specs=[own] * n + [got] * n,
            out_specs=pl.BlockSpec((n, tr, pc), lambda i, me_ref: (0, i, 0))),
        out_shape=jax.ShapeDtypeStruct((n, pr, pc), F32),
        compiler_params=_cp(("parallel",)),
    )(me, *gs, *rs)


def _adam_math(w, g, m, v):
    bc1 = 1.0 - ADAM_B1 ** ADAM_STEP
    bc2 = 1.0 - ADAM_B2 ** ADAM_STEP
    m = ADAM_B1 * m + (1.0 - ADAM_B1) * g
    v = ADAM_B2 * v + (1.0 - ADAM_B2) * (g * g)
    return -ADAM_LR * ((m / bc1) / (jnp.sqrt(v / bc2) + ADAM_EPS) + ADAM_WD * w), m, v


def _adam(w, g, m, v, *, name):
    R, C = w.shape
    tr = _tile(R, 256)

    def body(w_ref, g_ref, m_ref, v_ref, d_ref, mo_ref, vo_ref):
        d_ref[...], mo_ref[...], vo_ref[...] = _adam_math(w_ref[...], g_ref[...], m_ref[...], v_ref[...])

    spec = pl.BlockSpec((tr, C), lambda i: (i, 0))
    return pl.pallas_call(
        body, name=name, grid=(R // tr,), in_specs=[spec] * 4, out_specs=[spec] * 3,
        out_shape=[jax.ShapeDtypeStruct((R, C), F32)] * 3,
        compiler_params=_cp(("parallel",)),
    )(w, g, m, v)


def _adam_halves(w, own, got, m, v, core, *, name):
    L, R, C = w.shape
    rh = R // 2
    tr = _tile(rh, 256)
    nt = rh // tr

    def body(c_ref, w_ref, own_ref, got_ref, m_ref, v_ref, g_ref, d_ref, mo_ref, vo_ref):
        g = jnp.where(pl.program_id(1) == c_ref[0], own_ref[...], got_ref[...])
        g_ref[...] = g
        d_ref[...], mo_ref[...], vo_ref[...] = _adam_math(w_ref[...], g, m_ref[...], v_ref[...])

    full = pl.BlockSpec((None, tr, C), lambda l, h, t, c_ref: (l, h * nt + t, 0))
    half = pl.BlockSpec((None, tr, C), lambda l, h, t, c_ref: (l, t, 0))
    return pl.pallas_call(
        body, name=name,
        grid_spec=pltpu.PrefetchScalarGridSpec(
            num_scalar_prefetch=1, grid=(L, 2, nt), in_specs=[full, half, half, full, full], out_specs=[full] * 4),
        out_shape=[jax.ShapeDtypeStruct((L, R, C), F32)] * 4,
        compiler_params=_cp(("parallel", "parallel", "parallel")),
    )(core, w, own, got, m, v)


def _remote(src, dst, send, recv, dev):
    return pltpu.make_async_remote_copy(src_ref=src, dst_ref=dst, send_sem=send, recv_sem=recv,
                                        device_id=dev, device_id_type=MESH)


def _place_w(shard, pos, layer, *, axis, name):
    _, R, C = shard.shape
    tr = _tile(R, 512, 16)
    nt = R // tr
    if axis == 2:
        out_shape = (1, R, N_CHIPS * C)
        out_map = lambda t, q: (0, t, q[0])
    else:
        out_shape = (1, N_CHIPS * R, C)
        out_map = lambda t, q: (0, q[0] * nt + t, 0)

    def body(q_ref, s_ref, o_ref):
        o_ref[...] = s_ref[...].astype(_WIRE)

    return pl.pallas_call(
        body, name=name,
        grid_spec=pltpu.PrefetchScalarGridSpec(
            num_scalar_prefetch=1, grid=(nt,),
            in_specs=[pl.BlockSpec((None, tr, C), lambda t, q: (layer, t, 0))],
            out_specs=pl.BlockSpec((None, tr, C), out_map)),
        out_shape=jax.ShapeDtypeStruct(out_shape, _WIRE),
        compiler_params=_cp(("parallel",)),
    )(pos, shard)


def _ag_window(ref, kind, px, py, h):
    axis, perm = kind
    q = 2 * px + py
    if perm:
        q = _perm_idx(q)
    if axis == 2:
        R, C = ref.shape[1], ref.shape[2] // N_CHIPS
        rh = R // 2
        return ref.at[:, pl.ds(pl.multiple_of(h * rh, 16), rh), pl.ds(pl.multiple_of(q * C, LANES), C)]
    R = ref.shape[1] // N_CHIPS
    rh = R // 2
    return ref.at[:, pl.ds(pl.multiple_of(q * R + h * rh, 16), rh), :]


def _ag_ici_copies(refs, kinds, send, recv):
    x, y, c = lax.axis_index("x"), lax.axis_index("y"), lax.axis_index("c")
    chips = [(1 - x, y), (x, 1 - y), (1 - x, 1 - y)]
    sends, recvs = [], []
    for a, (ref, kind) in enumerate(zip(refs, kinds)):
        own = _ag_window(ref, kind, x, y, c)
        for i, (px, py) in enumerate(chips):
            k = 3 * a + i
            sends.append(_remote(own, own, send.at[k], recv.at[k], (px, py, c)))
            recvs.append(_remote(own, _ag_window(ref, kind, px, py, c), send.at[k], recv.at[k], (px, py, c)))
    return sends, recvs


def _ag_start(arrs, kinds, after, *, name, copies=_ag_ici_copies):
    n = len(arrs)

    def body(*refs):
        in_refs = refs[:n]
        send, recv = refs[n + len(after)], refs[n + len(after) + 1]
        token = refs[-1]
        sends, _ = copies(in_refs, kinds, send, recv)
        for cp in sends:
            cp.start()
        token[...] = jnp.zeros_like(token)

    sems = pltpu.SemaphoreType.DMA((3 * n,))
    out = pl.pallas_call(
        body, name=name,
        out_shape=(sems, sems) + tuple(pltpu.HBM(a.shape, a.dtype) for a in arrs)
        + (jax.ShapeDtypeStruct((SUBLANES, LANES), F32),),
        in_specs=(HBM,) * n + (ANY,) * len(after),
        out_specs=(SEMS, SEMS) + (HBM,) * n + (pl.BlockSpec(memory_space=pltpu.VMEM),),
        input_output_aliases={a: 2 + a for a in range(n)},
        compiler_params=pltpu.CompilerParams(has_side_effects=EFFECT),
    )(*[pltpu.with_memory_space_constraint(a, pltpu.HBM) for a in arrs], *after)
    return out[0], out[1], list(out[2:2 + n]), out[-1]


def _ag_wait(send, recv, arrs, kinds, after, *, name, copies=_ag_ici_copies):
    n = len(arrs)

    def body(*refs):
        in_refs = refs[:n]
        send, recv = refs[n], refs[n + 1]
        sends, recvs = copies(in_refs, kinds, send, recv)
        for cp in sends:
            cp.wait_send()
        for cp in recvs:
            cp.wait_recv()

    out = pl.pallas_call(
        body, name=name,
        out_shape=tuple(pltpu.HBM(a.shape, a.dtype) for a in arrs),
        in_specs=(HBM,) * n + (SEMS, SEMS) + (ANY,) * len(after), out_specs=(HBM,) * n,
        input_output_aliases={a: a for a in range(n)},
        compiler_params=pltpu.CompilerParams(has_side_effects=EFFECT),
    )(*arrs, send, recv, *after)
    return list(out)


def _ag_d2d_copies(refs, kinds, send, recv):
    x, y, c = lax.axis_index("x"), lax.axis_index("y"), lax.axis_index("c")
    chips = [(1 - x, y), (x, 1 - y), (1 - x, 1 - y)]
    sib = (x, y, 1 - c)
    sends, recvs = [], []
    for a, (ref, kind) in enumerate(zip(refs, kinds)):
        for i, (px, py) in enumerate(chips):
            k = 3 * a + i
            got = _ag_window(ref, kind, px, py, c)
            sends.append(_remote(got, got, send.at[k], recv.at[k], sib))
            recvs.append(_remote(got, _ag_window(ref, kind, px, py, 1 - c), send.at[k], recv.at[k], sib))
    return sends, recvs


def _flip(x, y, c, f):
    return ((1 - x) if f & 4 else x, (1 - y) if f & 2 else y, (1 - c) if f & 1 else c)


def _rs_copies(g_refs, land_refs, send, recv):
    x, y, c = lax.axis_index("x"), lax.axis_index("y"), lax.axis_index("c")
    cps = []
    for a, (g_ref, land_ref) in enumerate(zip(g_refs, land_refs)):
        for f in range(1, N_DEV):
            tx, ty, tcx = _flip(x, y, c, f)
            k = (N_DEV - 1) * a + f - 1
            cps.append(_remote(g_ref.at[4 * tx + 2 * ty + tcx], land_ref.at[f - 1], send.at[k], recv.at[k],
                               (tx, ty, tcx)))
    return cps


def _rs_start(gs, *, name):
    n = len(gs)
    lands = [lax.empty((N_DEV - 1,) + g.shape[1:], g.dtype) for g in gs]

    def body(*refs):
        send, recv, token = refs[2 * n], refs[2 * n + 1], refs[-1]
        for cp in _rs_copies(refs[:n], refs[n:2 * n], send, recv):
            cp.start()
        token[...] = jnp.zeros_like(token)

    sems = pltpu.SemaphoreType.DMA(((N_DEV - 1) * n,))
    thru = [pltpu.HBM(t.shape, t.dtype) for t in gs + lands]
    out = pl.pallas_call(
        body, name=name,
        out_shape=(sems, sems, *thru, jax.ShapeDtypeStruct((SUBLANES, LANES), F32)),
        in_specs=(HBM,) * (2 * n), out_specs=(SEMS, SEMS) + (HBM,) * (2 * n) + (pl.BlockSpec(memory_space=pltpu.VMEM),),
        input_output_aliases={a: 2 + a for a in range(2 * n)},
        compiler_params=pltpu.CompilerParams(has_side_effects=EFFECT),
    )(*[pltpu.with_memory_space_constraint(t, pltpu.HBM) for t in gs + lands])
    return out[0], out[1], list(out[2:2 + n]), list(out[2 + n:2 + 2 * n]), out[-1]


def _rs_wait(send, recv, gs, lands, after, *, name):
    n = len(gs)

    def body(*refs):
        cps = _rs_copies(refs[:n], refs[n:2 * n], refs[2 * n], refs[2 * n + 1])
        for cp in cps:
            cp.wait_send()
        for cp in cps:
            cp.wait_recv()

    out = pl.pallas_call(
        body, name=name,
        out_shape=tuple(pltpu.HBM(t.shape, t.dtype) for t in gs + lands),
        in_specs=(HBM,) * (2 * n) + (SEMS, SEMS, ANY), out_specs=(HBM,) * (2 * n),
        input_output_aliases={a: a for a in range(2 * n)},
        compiler_params=pltpu.CompilerParams(has_side_effects=EFFECT),
    )(*gs, *lands, send, recv, after)
    return list(out[:n]), list(out[n:])


def _pair_exchange(owns, *, name):
    n = len(owns)

    def body(*refs):
        send, recv = refs[2 * n], refs[2 * n + 1]
        x, y, c = lax.axis_index("x"), lax.axis_index("y"), lax.axis_index("c")
        cps = [_remote(refs[a], refs[n + a], send.at[a], recv.at[a], (x, y, 1 - c)) for a in range(n)]
        for cp in cps:
            cp.start()
        for cp in cps:
            cp.wait_recv()
        for cp in cps:
            cp.wait_send()

    return pl.pallas_call(
        body, name=name, in_specs=[ANY] * n, out_specs=[ANY] * n,
        out_shape=[jax.ShapeDtypeStruct(o.shape, o.dtype) for o in owns],
        scratch_shapes=[pltpu.SemaphoreType.DMA((n,)), pltpu.SemaphoreType.DMA((n,))],
    )(*owns)


def _allreduce_flat(vec, *, name):
    n = vec.shape[0]
    unit = N_DEV * SUBLANES * LANES
    npad = -(-n // unit) * unit
    rows = npad // (N_DEV * LANES)
    xin = jnp.pad(vec, (0, npad - n)).reshape(N_DEV, rows, LANES)

    def body(x_ref, y_ref, a_ref, send_a, recv_a, send_b, recv_b):
        x, y, c = lax.axis_index("x"), lax.axis_index("y"), lax.axis_index("c")
        me = 4 * x + 2 * y + c
        a_ref[me] = x_ref[me]
        sends, recvs = [], []
        for f in range(1, N_DEV):
            dev = _flip(x, y, c, f)
            t = 4 * dev[0] + 2 * dev[1] + dev[2]
            cp = _remote(x_ref.at[t], a_ref.at[me], send_a.at[f - 1], recv_a.at[f - 1], dev)
            cp.start()
            sends.append(cp)
            recvs.append(_remote(x_ref.at[me], a_ref.at[t], send_a.at[f - 1], recv_a.at[f - 1], dev))
        for cp in recvs:
            cp.wait_recv()
        for cp in sends:
            cp.wait_send()
        acc = a_ref[0]
        for s in range(1, N_DEV):
            acc = acc + a_ref[s]
        y_ref[me] = acc
        sends, recvs = [], []
        for f in range(1, N_DEV):
            dev = _flip(x, y, c, f)
            t = 4 * dev[0] + 2 * dev[1] + dev[2]
            cp = _remote(y_ref.at[me], y_ref.at[me], send_b.at[f - 1], recv_b.at[f - 1], dev)
            cp.start()
            sends.append(cp)
            recvs.append(_remote(y_ref.at[me], y_ref.at[t], send_b.at[f - 1], recv_b.at[f - 1], dev))
        for cp in recvs:
            cp.wait_recv()
        for cp in sends:
            cp.wait_send()

    vm = pl.BlockSpec(memory_space=pltpu.VMEM)
    out = pl.pallas_call(
        body, name=name, in_specs=[vm], out_specs=vm,
        out_shape=jax.ShapeDtypeStruct((N_DEV, rows, LANES), F32),
        scratch_shapes=[pltpu.VMEM((N_DEV, rows, LANES), F32)] + [pltpu.SemaphoreType.DMA((N_DEV - 1,))] * 4,
        compiler_params=_cp(),
    )(xin)
    return out.reshape(npad)[:n]


def _perm_cols(v, blocks=N_CHIPS):
    w = v.shape[-1] // blocks
    return jnp.concatenate([v[..., q * w:(q + 1) * w] for q in PERM], axis=-1)


def _pack(arrs):
    return jnp.concatenate([a.reshape(-1).astype(F32) for a in arrs])


def _unpack(flat, shapes):
    out, pos = [], 0
    for s in shapes:
        n = 1
        for d in s:
            n *= d
        out.append(flat[pos:pos + n].reshape(s))
        pos += n
    return out


def kernel(x, conv_w_in, conv_b_in, conv_w_dw, conv_b_dw, conv_ln_g, conv_ln_b, conv_w_out, conv_b_out, gmlp_w_in, gmlp_b_in, gmlp_ln_g, gmlp_ln_b, gmlp_w_s, gmlp_b_s, gmlp_w_out, gmlp_b_out, ffn_w_up, ffn_b_up, ffn_w_dw, ffn_b_dw, ffn_w_down, ffn_b_down, norm1_g, norm1_b, norm2_g, norm2_b, loss_target, m_conv_w_in, m_conv_b_in, m_conv_w_dw, m_conv_b_dw, m_conv_ln_g, m_conv_ln_b, m_conv_w_out, m_conv_b_out, m_gmlp_w_in, m_gmlp_b_in, m_gmlp_ln_g, m_gmlp_ln_b, m_gmlp_w_s, m_gmlp_b_s, m_gmlp_w_out, m_gmlp_b_out, m_ffn_w_up, m_ffn_b_up, m_ffn_w_dw, m_ffn_b_dw, m_ffn_w_down, m_ffn_b_down, m_norm1_g, m_norm1_b, m_norm2_g, m_norm2_b, v_conv_w_in, v_conv_b_in, v_conv_w_dw, v_conv_b_dw, v_conv_ln_g, v_conv_ln_b, v_conv_w_out, v_conv_b_out, v_gmlp_w_in, v_gmlp_b_in, v_gmlp_ln_g, v_gmlp_ln_b, v_gmlp_w_s, v_gmlp_b_s, v_gmlp_w_out, v_gmlp_b_out, v_ffn_w_up, v_ffn_b_up, v_ffn_w_dw, v_ffn_b_dw, v_ffn_w_down, v_ffn_b_down, v_norm1_g, v_norm1_b, v_norm2_g, v_norm2_b):
    P = dict(locals())
    WEIGHTS = ['conv_w_in', 'conv_b_in', 'conv_w_dw', 'conv_b_dw', 'conv_ln_g', 'conv_ln_b', 'conv_w_out',
               'conv_b_out', 'gmlp_w_in', 'gmlp_b_in', 'gmlp_ln_g', 'gmlp_ln_b', 'gmlp_w_s', 'gmlp_b_s',
               'gmlp_w_out', 'gmlp_b_out', 'ffn_w_up', 'ffn_b_up', 'ffn_w_dw', 'ffn_b_dw', 'ffn_w_down',
               'ffn_b_down', 'norm1_g', 'norm1_b', 'norm2_g', 'norm2_b']
    BIG = ['conv_w_in', 'conv_w_out', 'gmlp_w_in', 'gmlp_w_out', 'ffn_w_up', 'ffn_w_down']
    SMALL_SHARDED = {'conv_w_dw': 2, 'gmlp_b_in': 1, 'gmlp_ln_g': 1, 'gmlp_ln_b': 1, 'gmlp_b_out': 1, 'ffn_w_dw': 2}

    B, S, D = x.shape
    T = B * S
    depth = norm1_g.shape[0]
    alpha = (2.0 * depth) ** 0.25
    C = conv_w_out.shape[-1]
    F2 = ffn_b_up.shape[-1]
    G, L = gmlp_w_s.shape[1], gmlp_w_s.shape[2]
    xi, yi, ci = lax.axis_index("x"), lax.axis_index("y"), lax.axis_index("c")
    shard = 2 * xi + yi

    i32 = lambda v: jnp.reshape(v, (1,)).astype(jnp.int32)
    pos_plain, pos_perm = i32(shard), i32(_perm_idx(shard))
    me_id, core_id = i32(4 * xi + 2 * yi + ci), i32(ci)

    groups = []
    for i in range(depth):
        mix = 'conv' if i % 2 == 0 else 'gmlp'
        groups.append((f"{mix}{i // 2}", [(mix + '_w_in', i // 2, 2, True), (mix + '_w_out', i // 2, 1, False)]))
        groups.append((f"ffn{i}", [('ffn_w_up', i, 2, True), ('ffn_w_down', i, 1, False)]))
    sm_names = list(SMALL_SHARDED)
    sm_shapes = [P[n].shape for n in sm_names]
    mine = _pack([P[n] for n in sm_names]) * (ci == 0).astype(F32)
    buf = jnp.zeros((N_CHIPS, mine.shape[0]), F32)
    buf = lax.dynamic_update_slice(buf, mine[None], (shard, 0))
    gathered = _allreduce_flat(buf.reshape(-1), name="ag_small").reshape(N_CHIPS, -1)

    started, order = {}, [gathered]
    for gname, members in groups:
        placed = [_place_w(P[n], pos_perm if perm else pos_plain, l, axis=axis, name=f"place_{n}_{l}")
                  for n, l, axis, perm in members]
        kinds = [(axis, perm) for _, _, axis, perm in members]
        send, recv, arrs, token = _ag_start(placed, kinds, order, name=f"ag_start_{gname}")
        order = [token]
        started[gname] = (send, recv, arrs, kinds, [(n, l) for n, l, _, _ in members])
    wts = {}

    def landed_ici(gname, after):
        send, recv, arrs, kinds, keys = started[gname]
        arrs = _ag_wait(send, recv, arrs, kinds, after, name=f"ag_wait_{gname}")
        send, recv, arrs, _ = _ag_start(arrs, kinds, [], name=f"ag_fwd_start_{gname}", copies=_ag_d2d_copies)
        started[gname] = (send, recv, arrs, kinds, keys)

    def arrive(gname, after):
        send, recv, arrs, kinds, keys = started[gname]
        arrs = _ag_wait(send, recv, arrs, kinds, after, name=f"ag_fwd_wait_{gname}", copies=_ag_d2d_copies)
        wts.update(zip(keys, arrs))

    full = {}
    for n, parts in zip(sm_names, zip(*[_unpack(gathered[k], sm_shapes) for k in range(N_CHIPS)])):
        full[n] = jnp.concatenate(parts, axis=SMALL_SHARDED[n])
    for n in WEIGHTS:
        if n not in BIG and n not in full:
            full[n] = P[n]

    assert G * L == C, "a gMLP group must be as wide as a chunk is long"

    def row(v):
        return v.reshape(1, -1)

    def pad_rows(v, r):
        return jnp.pad(v, ((0, r - v.shape[0]), (0, 0)))

    xf = x.reshape(T, D)
    saved = []
    cur, cur_b = xf, xf.astype(_MXU)
    for i in range(depth):
        j = i // 2
        sv = {'x': cur, 'xb': cur_b}
        if i == 0:
            landed_ici(groups[0][0], order)
        arrive(groups[2 * i][0], [] if i == 0 else [cur_b])
        if i % 2 == 0:
            b_in = row(_perm_cols(full['conv_b_in'][j]))
            h1 = _mm(cur_b, wts['conv_w_in', j], bl=0, bias=b_in, tm=_tile(T, 1024), tn=_tile(2 * C, 1024, LANES),
                     tk=D, name=f"conv_in_{j}", n_outer=True, out_dtype=_ADT)
            wdw = pad_rows(full['conv_w_dw'][j], CONV_TAPS_PAD)
            dwo = _conv_fwd(h1, wdw, row(full['conv_b_dw'][j]), B=B, S=S, name=f"conv_dw_{j}")
            landed_ici(groups[2 * i + 1][0], [dwo])
            s_act, xhc, rsc, *y1 = _conv_tail_fwd(
                dwo, row(full['conv_ln_g'][j]), row(full['conv_ln_b'][j]), wts['conv_w_out', j],
                row(full['conv_b_out'][j]), cur, alpha, row(norm1_g[i]), row(norm1_b[i]), name=f"conv_out_ln_{j}")
            sv.update(h1=h1, wdw=wdw, act=s_act, xhc=xhc, rsc=rsc)
        else:
            b_in = row(_perm_cols(full['gmlp_b_in'][j]))
            pre = _mm(cur_b, wts['gmlp_w_in', j], bl=0, bias=b_in, tm=_tile(T, 1024), tn=_tile(2 * C, 1024, LANES),
                      tk=D, name=f"gmlp_in_{j}", n_outer=True, out_dtype=_ADT)
            bsb = jnp.repeat(gmlp_b_s[j].T, L, axis=1)
            landed_ici(groups[2 * i + 1][0], [pre])
            us, xhv, rsv, *y1 = _gmlp_gate_fwd(
                pre, row(full['gmlp_ln_g'][j]), row(full['gmlp_ln_b'][j]), gmlp_w_s[j], bsb, wts['gmlp_w_out', j],
                row(full['gmlp_b_out'][j]), cur, alpha, row(norm1_g[i]), row(norm1_b[i]), name=f"gmlp_gate_{j}")
            sv.update(pre=pre, bsb=bsb, act=us, xhv=xhv, rsv=rsv)
        x1, x1b, xh1, rs1 = y1
        arrive(groups[2 * i + 1][0], [x1b])
        wdw3 = pad_rows(_perm_cols(full['ffn_w_dw'][i]), SUBLANES)
        bdw3 = row(_perm_cols(ffn_b_dw[i]))
        ffn_in = (x1b, wts['ffn_w_up', i], wts['ffn_w_down', i], row(_perm_cols(ffn_b_up[i])), wdw3, bdw3)
        first = _ffn_fwd_half(0, *ffn_in, S=S, name=f"ffn_fwd_a_{i}")
        if i < depth - 1:
            landed_ici(groups[2 * i + 2][0], [first[3]])
        ffn_tail = (x1, alpha, row(ffn_b_down[i]), row(norm2_g[i]), row(norm2_b[i]))
        sv.update(x1=x1, x1b=x1b, xh1=xh1, rs1=rs1, wdw3=wdw3)
        if i < depth - 1:
            hs, hcs, f_act, cur, cur_b, xh2, rs2 = _ffn_fwd_half(1, *ffn_in, S=S, name=f"ffn_fwd_b_{i}", prev=first,
                                                                 tail=ffn_tail)
            sv.update(xh2=xh2, rs2=rs2)
        else:
            hs, hcs, f_act, *sv['head'] = _ffn_fwd_half(1, *ffn_in, S=S, name=f"ffn_fwd_b_{i}", prev=first,
                                                         tail=ffn_tail, head=loss_target.reshape(T, D))
        sv.update(hs=hs, hcs=hcs, f=f_act)
        saved.append(sv)

    sg = {n: [None] * full[n].shape[0] for n in WEIGHTS if n not in BIG}
    inflight = {n: [None] * P[n].shape[0] for n in BIG}
    deps = []
    dcur = None
    loss_part = None
    tk_t = _tile(T, 2048)

    ready = []

    def wgrad(n, l, a_, b_, **kw):
        tk = T if n.endswith('w_in') else tk_t
        ready.append((n, l, _mm(a_, b_, ta=True, out_dtype=_WIRE, tk=tk, name=f"{n}_dw_{l}", deps=deps, **kw)))
        launch(f"{n}_{l}")

    def launch(gname):
        send, recv, gs, lands, token = _rs_start([g for _, _, g in ready], name=f"rs_start_{gname}")
        group = {'name': gname, 'flight': (send, recv, gs, lands), 'landed': None}
        for a, (n, l, _) in enumerate(ready):
            inflight[n][l] = (group, a)
        del ready[:]
        deps.append(token)

    def landed(n, l):
        group, a = inflight[n][l]
        if group['landed'] is None:
            group['landed'] = _rs_wait(*group['flight'], dcur, name=f"rs_wait_{group['name']}")
        return group['landed'][0][a], group['landed'][1][a]

    for i in reversed(range(depth)):
        j = i // 2
        sv = saved[i]
        if i == depth - 1:
            dz2, dz2b, dg, db, cs, loss_part = sv['head']
        else:
            dz2, dz2b, dg, db, cs = dcur
        sg['norm2_g'][i], sg['norm2_b'][i], sg['ffn_b_down'][i] = dg.sum(0), db.sum(0), cs.sum(0)
        Fh = F2 // 2
        wgrad('ffn_w_down', i, sv['f'], dz2b, tm=Fh // 2, tn=_tile(D, 1024, LANES), pieces=('row',))
        ffn_in = (dz2b, wts['ffn_w_down', i], wts['ffn_w_up', i], sv['hs'], sv['hcs'], sv['wdw3'])
        dh0, csu0, dwd0, dbd0, dxp = _ffn_bwd_half(0, *ffn_in, S=S, name=f"ffn_bwd_a_{i}", dz=dz2, alpha=alpha)
        dh, csu1, dwd1, dbd1, dz1, dz1b, dg, db, cs = _ffn_bwd_half(
            1, *ffn_in, S=S, name=f"ffn_bwd_b_{i}", prev=(dh0, dxp), ln=(sv['xh1'], sv['rs1'], row(norm1_g[i])))
        sg['ffn_b_up'][i] = _perm_cols(jnp.concatenate([csu0.sum(0), csu1.sum(0)], axis=-1))
        sg['ffn_w_dw'][i] = _perm_cols(jnp.concatenate([dwd0.sum(1), dwd1.sum(1)], axis=-1))
        sg['ffn_b_dw'][i] = _perm_cols(jnp.concatenate([dbd0.sum(0), dbd1.sum(0)], axis=-1))
        wgrad('ffn_w_up', i, sv['x1b'], dh, tm=D, tn=F2 // N_CHIPS, pieces=('col', True))
        sg['norm1_g'][i], sg['norm1_b'][i] = dg.sum(0), db.sum(0)
        if i % 2 == 0:
            sg['conv_b_out'][j] = cs.sum(0)
            wgrad('conv_w_out', j, sv['act'], dz1b, tm=_tile(C, 1024), tn=_tile(D, 1024, LANES), pieces=('row',))
            ddw, dg, db = _ln_silu_bwd(dz1b, wts['conv_w_out', j], sv['xhc'], sv['rsc'], row(full['conv_ln_g'][j]),
                                       row(full['conv_ln_b'][j]), name=f"conv_ln_bwd_{j}")
            sg['conv_ln_g'][j], sg['conv_ln_b'][j] = dg.sum(0), db.sum(0)
            dglu, dwk, dbk = _conv_bwd(ddw, sv['h1'], sv['wdw'], B=B, S=S, name=f"conv_dw_bwd_{j}")
            sg['conv_w_dw'][j] = dwk.sum(1)[:conv_w_dw.shape[1]]
            sg['conv_b_dw'][j] = dbk.sum(0)
            dh1, csi = _glu_bwd(dglu, sv['h1'], name=f"conv_glu_bwd_{j}")
            sg['conv_b_in'][j] = _perm_cols(csi.sum(0))
            fam = 'conv_w_in'
        else:
            sg['gmlp_b_out'][j] = cs.sum(0)
            wgrad('gmlp_w_out', j, sv['act'], dz1b, tm=_tile(C, 1024), tn=_tile(D, 1024, LANES), pieces=('row',))
            dh1, dg, db, csi, dws, dbs = _gmlp_gate_bwd(dz1b, wts['gmlp_w_out', j], sv['pre'], sv['xhv'], sv['rsv'],
                                                        row(full['gmlp_ln_g'][j]), row(full['gmlp_ln_b'][j]),
                                                        gmlp_w_s[j], sv['bsb'], name=f"gmlp_gate_bwd_{j}")
            sg['gmlp_ln_g'][j], sg['gmlp_ln_b'][j] = dg.sum(0), db.sum(0)
            sg['gmlp_b_in'][j] = _perm_cols(csi.sum(0))
            sg['gmlp_w_s'][j] = dws
            sg['gmlp_b_s'][j] = dbs.reshape(L, G, L).sum(-1).T
            fam = 'gmlp_w_in'
        wgrad(fam, j, sv['xb'], dh1, tm=D, tn=(2 * C) // N_CHIPS, pieces=('col', True))
        if i > 0:
            below = saved[i - 1]
            dcur = _mm_ln_bwd(dh1, wts[fam, j], dz1, alpha, below['xh2'], below['rs2'], row(norm2_g[i - 1]),
                              name=f"{fam}_dx_{j}", deps=deps)
        else:
            dcur = _mm(dh1, wts[fam, j], bl=0, tb=True, res=dz1, res_scale=alpha, tm=_tile(T, 512),
                       tn=_tile(D, 1024, LANES), tk=2 * C, name=f"{fam}_dx_{j}", deps=deps)
    grad_x = dcur.reshape(B, S, D)

    small_names = [n for n in WEIGHTS if n not in BIG]
    small_full = [jnp.stack(sg[n]) for n in small_names]
    flat = _pack(small_full + [loss_part])
    red = _allreduce_flat(flat, name="ar_small")
    red_parts = _unpack(red, [a.shape for a in small_full] + [loss_part.shape])
    loss = (0.5 / D) * jnp.sum(red_parts[-1])
    grads = {}
    for n, g in zip(small_names, red_parts[:-1]):
        if n in SMALL_SHARDED:
            ax = SMALL_SHARDED[n]
            width = P[n].shape[ax]
            g = lax.dynamic_slice_in_dim(g, shard * width, width, axis=ax)
        grads[n] = g

    big_out = {}
    for n in ['ffn_w_down', 'ffn_w_up', 'gmlp_w_out', 'gmlp_w_in', 'conv_w_out', 'conv_w_in']:
        both = [landed(n, l) for l in range(len(inflight[n]))]
        own = _sum_pieces([g for g, _ in both], [r for _, r in both], me_id, name=f"sum_{n}")
        got, = _pair_exchange([own], name=f"px_{n}")
        big_out[n] = _adam_halves(P[n], own, got, P['m_' + n], P['v_' + n], core_id, name=f"adam_{n}")

    shapes = [P[n].shape for n in small_names]
    n_small = sum(functools.reduce(lambda p_, d_: p_ * d_, s_, 1) for s_ in shapes)
    unit = SUBLANES * LANES
    npad = -(-n_small // unit) * unit

    def flat2d(arrs, fill=0.0):
        v = _pack(arrs)
        return jnp.pad(v, (0, npad - n_small), constant_values=fill).reshape(-1, LANES)

    dl, mo, vo = _adam(flat2d([P[n] for n in small_names]), flat2d([grads[n] for n in small_names]),
                       flat2d([P['m_' + n] for n in small_names]),
                       flat2d([P['v_' + n] for n in small_names], fill=1.0), name="adam_small")
    small_out = {n: [grads[n], None, None, None] for n in small_names}
    for k, t in enumerate((dl, mo, vo)):
        for n, a in zip(small_names, _unpack(t.reshape(-1), shapes)):
            small_out[n][k + 1] = a

    outs = [loss, grad_x]
    for k in range(4):
        for n in WEIGHTS:
            outs.append(big_out[n][k] if n in BIG else small_out[n][k])
    return tuple(outs)
```

```python
import functools

import jax
import jax.numpy as jnp
from jax import lax
from jax.experimental import pallas as pl
from jax.experimental.pallas import tpu as pltpu

F32 = jnp.float32
_MXU = jnp.bfloat16
_WIRE = jnp.bfloat16
_HDT = jnp.bfloat16
_ADT = jnp.bfloat16
_XDT = jnp.bfloat16
LN_EPS = 1e-5
ADAM_LR, ADAM_B1, ADAM_B2, ADAM_EPS, ADAM_WD, ADAM_STEP = 0.001, 0.9, 0.999, 1e-08, 0.01, 10
N_CHIPS = 4
N_DEV = 8
LANES = 128
SUBLANES = 8
CONV_TAPS_PAD = 32
VMEM_LIMIT = 56 << 20
MESH = pl.DeviceIdType.MESH
ANY = pl.BlockSpec(memory_space=pl.ANY)
HBM = pl.BlockSpec(memory_space=pltpu.HBM)
SEMS = pl.BlockSpec(memory_space=pltpu.SEMAPHORE)
EFFECT = pltpu.SideEffectType.DATAFLOW_SIDE_EFFECTING
PERM = (0, 2, 1, 3)


def _cp(sem=None):
    return pltpu.CompilerParams(dimension_semantics=sem, vmem_limit_bytes=VMEM_LIMIT)


def _tile(dim, pref, mult=SUBLANES):
    if dim <= pref:
        return dim
    t = (pref // mult) * mult
    while t > mult and dim % t:
        t -= mult
    assert dim % t == 0, (dim, pref, mult)
    return t


def _perm_idx(q):
    return (q % 2) * 2 + q // 2


def _fold8(t):
    r, n = t.shape
    return t.reshape(r // SUBLANES, SUBLANES, n).sum(axis=0)


def _ln_rows(z, g, b):
    mu = jnp.mean(z, axis=-1, keepdims=True)
    xc = z - mu
    var = jnp.mean(xc * xc, axis=-1, keepdims=True)
    rstd = lax.rsqrt(var + LN_EPS)
    xh = xc * rstd
    return xh * g + b, xh, rstd


def _ln_bwd_rows(dy, xh, rstd, g):
    dxh = dy * g
    m1 = jnp.mean(dxh, axis=-1, keepdims=True)
    m2 = jnp.mean(dxh * xh, axis=-1, keepdims=True)
    return rstd * (dxh - m1 - xh * m2)


def _sigmoid(v):
    return 0.5 * jnp.tanh(0.5 * v) + 0.5


def _gelu_parts(p):
    cdf = 0.5 * (1.0 + lax.erf(p * 0.7071067811865476))
    pdf = jnp.exp(-0.5 * p * p) * 0.3989422804014327
    return p * cdf, cdf + p * pdf


def _shift_down(prev8, t, s):
    ext = jnp.concatenate([prev8, t], axis=0)
    return pltpu.roll(ext, s, 0)[SUBLANES:]


def _shift_up(t, next8, s):
    n = t.shape[0]
    ext = jnp.concatenate([t, next8], axis=0)
    return pltpu.roll(ext, n + SUBLANES - s, 0)[:n]


def _mm(a, b, *, ta=False, tb=False, bl=None, bias=None, res=None, res_scale=1.0, out_dtype=F32,
        tm, tn, tk, name, pieces=None, deps=None, n_outer=False):
    M, K = (a.shape[1], a.shape[0]) if ta else a.shape
    bs = b.shape[1:] if bl is not None else b.shape
    N, Kb = (bs[0], bs[1]) if tb else (bs[1], bs[0])
    assert K == Kb and M % tm == 0 and N % tn == 0 and K % tk == 0, (a.shape, b.shape, tm, tn, tk)
    gm, gn, gk = M // tm, N // tn, K // tk

    def spec(block, imap):
        if n_outer:
            return pl.BlockSpec(block, lambda j, i, k: imap(i, j, k))
        return pl.BlockSpec(block, imap)

    a_spec = spec((tk, tm), lambda i, j, k: (k, i)) if ta else spec((tm, tk), lambda i, j, k: (i, k))
    bblk = (tn, tk) if tb else (tk, tn)
    bmap = (lambda i, j, k: (j, k)) if tb else (lambda i, j, k: (k, j))
    if bl is not None:
        b_spec = spec((None,) + bblk, lambda i, j, k: (bl,) + bmap(i, j, k))
    else:
        b_spec = spec(bblk, bmap)
    in_specs, operands = [a_spec, b_spec], [a, b]
    if bias is not None:
        in_specs.append(spec((1, tn), lambda i, j, k: (0, j)))
        operands.append(bias)
    if res is not None:
        in_specs.append(spec((tm, tn), lambda i, j, k: (i, j)))
        operands.append(res)
    n_dep = len(deps) if deps else 0
    if n_dep:
        in_specs += [ANY] * n_dep
        operands += deps
        del deps[:]
    if pieces is None:
        out_shape = jax.ShapeDtypeStruct((M, N), out_dtype)
        out_spec = spec((tm, tn), lambda i, j, k: (i, j))
        ppb = pr = None
    elif pieces[0] == 'col':
        pr, pc = M // 2, N // N_CHIPS
        assert tm % pr == 0 and pc % tn == 0
        ppb, per = tm // pr, pc // tn
        perm = pieces[1]
        out_shape = jax.ShapeDtypeStruct((N_DEV, pr, pc), out_dtype)
        out_spec = spec(
            (ppb, pr, tn),
            lambda i, j, k: ((2 * (_perm_idx(j // per) if perm else j // per)) // ppb + i, 0, j % per))
    else:
        pr = M // N_DEV
        assert tm % pr == 0
        ppb = tm // pr
        out_shape = jax.ShapeDtypeStruct((N_DEV, pr, N), out_dtype)
        out_spec = spec((ppb, pr, tn), lambda i, j, k: (i, 0, j))
    dims = (((0 if ta else 1,), (1 if tb else 0,)), ((), ()))

    def body(*refs):
        a_ref, b_ref = refs[0], refs[1]
        pos = 2
        bias_ref = res_ref = None
        if bias is not None:
            bias_ref = refs[pos]
            pos += 1
        if res is not None:
            res_ref = refs[pos]
            pos += 1
        pos += n_dep
        o_ref = refs[pos]

        def finish(r):
            if bias_ref is not None:
                r = r + bias_ref[...]
            if res_ref is not None:
                r = r + res_scale * res_ref[...]
            if pieces is not None:
                r = r.reshape(ppb, pr, tn)
            o_ref[...] = r.astype(out_dtype)

        part = lax.dot_general(a_ref[...].astype(_MXU), b_ref[...].astype(_MXU), dims, preferred_element_type=F32)
        if gk == 1:
            finish(part)
            return
        acc_ref = refs[pos + 1]
        k = pl.program_id(2)

        @pl.when(k == 0)
        def _():
            acc_ref[...] = part

        @pl.when((k > 0) & (k < gk - 1))
        def _():
            acc_ref[...] += part

        @pl.when(k == gk - 1)
        def _():
            finish(acc_ref[...] + part)

    return pl.pallas_call(
        body, name=name, grid=(gn, gm, gk) if n_outer else (gm, gn, gk), in_specs=in_specs, out_specs=out_spec,
        out_shape=out_shape, scratch_shapes=[pltpu.VMEM((tm, tn), F32)] if gk > 1 else [],
        compiler_params=_cp(("parallel", "parallel", "arbitrary")),
    )(*operands)


def _mm_ln_bwd(a, w, res, res_scale, xh, rstd, g, *, name, deps=None):
    T, K = a.shape
    D = w.shape[1]
    tm = _tile(T, 512)
    n_dep = len(deps) if deps else 0

    def body(a_ref, w_ref, res_ref, xh_ref, rs_ref, g_ref, *rest):
        dz_ref, dzb_ref, dg_ref, db_ref, cs_ref = rest[n_dep:]

        @pl.when(pl.program_id(0) == 0)
        def _():
            dg_ref[...] = jnp.zeros_like(dg_ref)
            db_ref[...] = jnp.zeros_like(db_ref)
            cs_ref[...] = jnp.zeros_like(cs_ref)

        d = lax.dot_general(a_ref[...].astype(_MXU), w_ref[...].astype(_MXU), (((1,), (1,)), ((), ())),
                            preferred_element_type=F32) + res_scale * res_ref[...]
        xh = xh_ref[...].astype(F32)
        dz = _ln_bwd_rows(d, xh, rs_ref[...], g_ref[...])
        dz_ref[...] = dz
        dzb_ref[...] = dz.astype(_MXU)
        dg_ref[...] += _fold8(d * xh)
        db_ref[...] += _fold8(d)
        cs_ref[...] += _fold8(dz)

    row = lambda i: (i, 0)
    fixed = lambda i: (0, 0)
    tile = pl.BlockSpec((tm, D), row)
    part = pl.BlockSpec((SUBLANES, D), fixed)
    operands = [a, w, res, xh, rstd, g] + (list(deps) if deps else [])
    if deps:
        del deps[:]
    return pl.pallas_call(
        body, name=name, grid=(T // tm,),
        in_specs=[pl.BlockSpec((tm, K), row),
                  pl.BlockSpec((None, D, K), lambda i: (0, 0, 0), pipeline_mode=pl.Buffered(1)),
                  tile, tile, pl.BlockSpec((tm, 1), row), pl.BlockSpec((1, D), fixed)] + [ANY] * n_dep,
        out_specs=[tile, tile, part, part, part],
        out_shape=[jax.ShapeDtypeStruct((T, D), F32), jax.ShapeDtypeStruct((T, D), _MXU)]
        + [jax.ShapeDtypeStruct((SUBLANES, D), F32)] * 3,
        compiler_params=_cp(("arbitrary",)),
    )(*operands)


def _out_ln(act, wo_ref, bias_ref, res_ref, alpha, g_ref, b_ref, y_ref, yb_ref, xh_ref, rs_ref):
    z = jnp.dot(act, wo_ref[...].astype(_MXU), preferred_element_type=F32) + bias_ref[...] + alpha * res_ref[...]
    y, xh, rstd = _ln_rows(z, g_ref[...], b_ref[...])
    y_ref[...] = y
    yb_ref[...] = y.astype(_MXU)
    xh_ref[...] = xh.astype(_XDT)
    rs_ref[...] = rstd


def _conv_tail_fwd(v, gc, bc, w, bias, res, alpha, g, b, *, name):
    T, C = v.shape
    D = w.shape[-1]
    tm = _tile(T, 512)

    def body(v_ref, gc_ref, bc_ref, w_ref, bias_ref, res_ref, g_ref, b_ref,
             s_ref, xhc_ref, rsc_ref, y_ref, yb_ref, xh_ref, rs_ref):
        yv, xhc, rsc = _ln_rows(v_ref[...], gc_ref[...], bc_ref[...])
        s = (yv * _sigmoid(yv)).astype(_MXU)
        s_ref[...] = s
        xhc_ref[...] = xhc.astype(_XDT)
        rsc_ref[...] = rsc
        _out_ln(s, w_ref, bias_ref, res_ref, alpha, g_ref, b_ref, y_ref, yb_ref, xh_ref, rs_ref)

    row = lambda i: (i, 0)
    fixed = lambda i: (0, 0)
    vc, vd = pl.BlockSpec((1, C), fixed), pl.BlockSpec((1, D), fixed)
    tc_, td = pl.BlockSpec((tm, C), row), pl.BlockSpec((tm, D), row)
    one = pl.BlockSpec((tm, 1), row)
    return pl.pallas_call(
        body, name=name, grid=(T // tm,),
        in_specs=[tc_, vc, vc, _resident((None, C, D), lambda i: (0, 0, 0)), vd, td, vd, vd],
        out_specs=[tc_, tc_, one, td, td, td, one],
        out_shape=[jax.ShapeDtypeStruct((T, C), _MXU), jax.ShapeDtypeStruct((T, C), _XDT),
                   jax.ShapeDtypeStruct((T, 1), F32), jax.ShapeDtypeStruct((T, D), F32),
                   jax.ShapeDtypeStruct((T, D), _MXU), jax.ShapeDtypeStruct((T, D), _XDT),
                   jax.ShapeDtypeStruct((T, 1), F32)],
        compiler_params=_cp(("parallel",)),
    )(v, gc, bc, w, bias, res, g, b)


def _conv_cols(C, tc):
    per = (C // 2) // tc
    return per, (lambda j: (j // per) * (2 * per) + j % per)


def _glu_shifted(a_ref, g_ref, p_ref, S):
    u = a_ref[...].astype(F32) * _sigmoid(g_ref[...].astype(F32))
    rows = lax.broadcasted_iota(jnp.int32, (SUBLANES, u.shape[1]), 0)
    lo = CONV_TAPS_PAD
    for r in range(SUBLANES):
        p_ref[r, 0:lo, :] = jnp.zeros((lo, u.shape[1]), F32)
        if r == 0:
            p_ref[r, lo:lo + S, :] = u
        else:
            rolled = pltpu.roll(u, r, 0)
            p_ref[r, lo:lo + S, :] = rolled
            p_ref[r, lo:lo + SUBLANES, :] = jnp.where(rows >= r, rolled[0:SUBLANES], 0.0)


def _conv_fwd(h1, w_dw, b_dw, *, B, S, name):
    C = w_dw.shape[1]
    taps = CONV_TAPS_PAD - 1
    tc = LANES
    ch = _tile(S, 128)
    per, col_a = _conv_cols(C, tc)

    def body(a_ref, g_ref, w_ref, b_ref, o_ref, p_ref):
        _glu_shifted(a_ref, g_ref, p_ref, S)

        def chunk(ci, carry):
            base = pl.multiple_of(ci * ch, ch)
            acc = jnp.zeros((ch, tc), F32) + b_ref[...]
            for k in range(taps):
                q, r = divmod(taps - 1 - k, SUBLANES)
                start = pl.multiple_of(base + (CONV_TAPS_PAD - SUBLANES * q), SUBLANES)
                acc = acc + w_ref[pl.ds(k, 1), :] * p_ref[r, pl.ds(start, ch), :]
            o_ref[pl.ds(base, ch), :] = acc
            return carry

        lax.fori_loop(0, S // ch, chunk, 0)

    return pl.pallas_call(
        body, name=name, grid=(B, C // tc),
        in_specs=[pl.BlockSpec((S, tc), lambda b, j: (b, col_a(j))),
                  pl.BlockSpec((S, tc), lambda b, j: (b, col_a(j) + per)),
                  pl.BlockSpec((CONV_TAPS_PAD, tc), lambda b, j: (0, j)),
                  pl.BlockSpec((1, tc), lambda b, j: (0, j))],
        out_specs=pl.BlockSpec((S, tc), lambda b, j: (b, j)),
        out_shape=jax.ShapeDtypeStruct((B * S, C), F32),
        scratch_shapes=[pltpu.VMEM((SUBLANES, S + CONV_TAPS_PAD, tc), F32)],
        compiler_params=_cp(("parallel", "parallel")),
    )(h1, h1, w_dw, b_dw)


def _conv_bwd(dd, h1, w_dw, *, B, S, name):
    C = w_dw.shape[1]
    taps = CONV_TAPS_PAD - 1
    tc = LANES
    ch = _tile(S, 128)
    per, col_a = _conv_cols(C, tc)

    def body(d_ref, a_ref, g_ref, w_ref, du_ref, dw_ref, db_ref, p_ref, q_ref):
        b = pl.program_id(1)

        @pl.when(b == 0)
        def _():
            dw_ref[...] = jnp.zeros_like(dw_ref)
            db_ref[...] = jnp.zeros_like(db_ref)

        _glu_shifted(a_ref, g_ref, p_ref, S)
        d = d_ref[...]
        rows = lax.broadcasted_iota(jnp.int32, (SUBLANES, tc), 0)
        for r in range(SUBLANES):
            q_ref[r, S:S + CONV_TAPS_PAD, :] = jnp.zeros((CONV_TAPS_PAD, tc), F32)
            if r == 0:
                q_ref[r, 0:S, :] = d
            else:
                rolled = pltpu.roll(d, S - r, 0)
                q_ref[r, 0:S, :] = rolled
                q_ref[r, S - SUBLANES:S, :] = jnp.where(rows < SUBLANES - r, rolled[S - SUBLANES:S], 0.0)
        db_ref[...] += _fold8(d)

        def chunk(ci, carry):
            base = pl.multiple_of(ci * ch, ch)
            dch = d_ref[pl.ds(base, ch), :]
            acc = jnp.zeros((ch, tc), F32)
            for k in range(taps):
                q, r = divmod(taps - 1 - k, SUBLANES)
                up = pl.multiple_of(base + SUBLANES * q, SUBLANES)
                acc = acc + w_ref[pl.ds(k, 1), :] * q_ref[r, pl.ds(up, ch), :]
                down = pl.multiple_of(base + (CONV_TAPS_PAD - SUBLANES * q), SUBLANES)
                dw_ref[k] += _fold8(dch * p_ref[r, pl.ds(down, ch), :])
            du_ref[pl.ds(base, ch), :] = acc
            return carry

        lax.fori_loop(0, S // ch, chunk, 0)

    return pl.pallas_call(
        body, name=name, grid=(C // tc, B),
        in_specs=[pl.BlockSpec((S, tc), lambda j, b: (b, j)),
                  pl.BlockSpec((S, tc), lambda j, b: (b, col_a(j))),
                  pl.BlockSpec((S, tc), lambda j, b: (b, col_a(j) + per)),
                  pl.BlockSpec((CONV_TAPS_PAD, tc), lambda j, b: (0, j))],
        out_specs=[pl.BlockSpec((S, tc), lambda j, b: (b, j)),
                   pl.BlockSpec((CONV_TAPS_PAD, SUBLANES, tc), lambda j, b: (0, 0, j)),
                   pl.BlockSpec((SUBLANES, tc), lambda j, b: (0, j))],
        out_shape=[jax.ShapeDtypeStruct((B * S, C), F32),
                   jax.ShapeDtypeStruct((CONV_TAPS_PAD, SUBLANES, C), F32),
                   jax.ShapeDtypeStruct((SUBLANES, C), F32)],
        scratch_shapes=[pltpu.VMEM((SUBLANES, S + CONV_TAPS_PAD, tc), F32),
                        pltpu.VMEM((SUBLANES, S + CONV_TAPS_PAD, tc), F32)],
        compiler_params=_cp(("parallel", "arbitrary")),
    )(dd, h1, h1, w_dw)


def _ln_silu_bwd(dzb, w, xh, rstd, g, b, *, name):
    T, D = dzb.shape
    C = w.shape[1]
    tm = _tile(T, 512)

    def body(dz_ref, w_ref, xh_ref, rs_ref, g_ref, b_ref, dv_ref, dg_ref, db_ref):
        @pl.when(pl.program_id(0) == 0)
        def _():
            dg_ref[...] = jnp.zeros_like(dg_ref)
            db_ref[...] = jnp.zeros_like(db_ref)

        ds = lax.dot_general(dz_ref[...].astype(_MXU), w_ref[...].astype(_MXU), (((1,), (1,)), ((), ())),
                             preferred_element_type=F32)
        xh = xh_ref[...].astype(F32)
        gam = g_ref[...]
        y = xh * gam + b_ref[...]
        sig = _sigmoid(y)
        dln = ds * (sig * (1.0 + y * (1.0 - sig)))
        dv_ref[...] = _ln_bwd_rows(dln, xh, rs_ref[...], gam)
        dg_ref[...] += _fold8(dln * xh)
        db_ref[...] += _fold8(dln)

    row = lambda i: (i, 0)
    fixed = lambda i: (0, 0)
    vec = pl.BlockSpec((1, C), fixed)
    part = pl.BlockSpec((SUBLANES, C), fixed)
    return pl.pallas_call(
        body, name=name, grid=(T // tm,),
        in_specs=[pl.BlockSpec((tm, D), row), _resident((None, C, D), lambda i: (0, 0, 0)),
                  pl.BlockSpec((tm, C), row), pl.BlockSpec((tm, 1), row), vec, vec],
        out_specs=[pl.BlockSpec((tm, C), row), part, part],
        out_shape=[jax.ShapeDtypeStruct((T, C), F32)] + [jax.ShapeDtypeStruct((SUBLANES, C), F32)] * 2,
        compiler_params=_cp(("arbitrary",)),
    )(dzb, w, xh, rstd, g, b)


def _glu_bwd(du, h1, *, name):
    T, C = du.shape
    il = C // 2
    tm = _tile(T, 512)

    def body(du_ref, h_ref, dh_ref, cs_ref):
        @pl.when(pl.program_id(0) == 0)
        def _():
            cs_ref[...] = jnp.zeros_like(cs_ref)

        for hb in range(2):
            a = h_ref[:, 2 * hb * il:(2 * hb + 1) * il].astype(F32)
            gate = h_ref[:, (2 * hb + 1) * il:(2 * hb + 2) * il].astype(F32)
            d = du_ref[:, hb * il:(hb + 1) * il]
            sig = _sigmoid(gate)
            da = d * sig
            dgate = d * a * sig * (1.0 - sig)
            dh_ref[:, 2 * hb * il:(2 * hb + 1) * il] = da.astype(_MXU)
            dh_ref[:, (2 * hb + 1) * il:(2 * hb + 2) * il] = dgate.astype(_MXU)
            cs_ref[:, 2 * hb * il:(2 * hb + 1) * il] += _fold8(da)
            cs_ref[:, (2 * hb + 1) * il:(2 * hb + 2) * il] += _fold8(dgate)

    row = lambda i: (i, 0)
    return pl.pallas_call(
        body, name=name, grid=(T // tm,),
        in_specs=[pl.BlockSpec((tm, C), row), pl.BlockSpec((tm, 2 * C), row)],
        out_specs=[pl.BlockSpec((tm, 2 * C), row), pl.BlockSpec((SUBLANES, 2 * C), lambda i: (0, 0))],
        out_shape=[jax.ShapeDtypeStruct((T, 2 * C), _MXU), jax.ShapeDtypeStruct((SUBLANES, 2 * C), F32)],
        compiler_params=_cp(("arbitrary",)),
    )(du, h1)


def _tril_mask(n):
    return lax.broadcasted_iota(jnp.int32, (n, n), 0) >= lax.broadcasted_iota(jnp.int32, (n, n), 1)


def _split_uv(t, il):
    u = jnp.concatenate([t[:, 0:il], t[:, 2 * il:3 * il]], axis=1)
    v = jnp.concatenate([t[:, il:2 * il], t[:, 3 * il:4 * il]], axis=1)
    return u, v


def _gmlp_gate_fwd(p, g, b, w_s, bsb, w_out, bias, res, alpha, g1, b1, *, name):
    T, C2 = p.shape
    C = C2 // 2
    D = w_out.shape[-1]
    il = C // 2
    G, L, _ = w_s.shape
    assert G * L == C
    tm = _tile(T, 4 * L, L)

    def body(p_ref, g_ref, b_ref, ws_ref, bs_ref, wo_ref, bias_ref, res_ref, g1_ref, b1_ref,
             us_ref, xh_ref, rs_ref, y_ref, yb_ref, xh1_ref, rs1_ref, vn_ref, u_ref):
        z, _ = _gelu_parts(p_ref[...].astype(F32))
        u, v = _split_uv(z, il)
        vn, xh, rstd = _ln_rows(v, g_ref[...], b_ref[...])
        xh_ref[...] = xh.astype(_XDT)
        rs_ref[...] = rstd
        vn_ref[...] = vn.astype(_MXU)
        u_ref[...] = u
        mask = _tril_mask(L)
        for gi in range(G):
            wc = jnp.where(mask, ws_ref[gi], 0.0).astype(_MXU)
            cols = slice(gi * L, (gi + 1) * L)
            for c in range(tm // L):
                rows = slice(c * L, (c + 1) * L)
                s = jnp.dot(wc, vn_ref[rows, cols], preferred_element_type=F32) + bs_ref[:, cols]
                us_ref[rows, cols] = (u_ref[rows, cols] * s).astype(_MXU)
        _out_ln(us_ref[...], wo_ref, bias_ref, res_ref, alpha, g1_ref, b1_ref, y_ref, yb_ref, xh1_ref, rs1_ref)

    row = lambda i: (i, 0)
    fixed = lambda i: (0, 0)
    vd, td, one = pl.BlockSpec((1, D), fixed), pl.BlockSpec((tm, D), row), pl.BlockSpec((tm, 1), row)
    return pl.pallas_call(
        body, name=name, grid=(T // tm,),
        in_specs=[pl.BlockSpec((tm, C2), row), pl.BlockSpec((1, C), fixed), pl.BlockSpec((1, C), fixed),
                  pl.BlockSpec((G, L, L), lambda i: (0, 0, 0)), pl.BlockSpec((L, C), fixed),
                  _resident((None, C, D), lambda i: (0, 0, 0)), vd, td, vd, vd],
        out_specs=[pl.BlockSpec((tm, C), row), pl.BlockSpec((tm, C), row), one, td, td, td, one],
        out_shape=[jax.ShapeDtypeStruct((T, C), _MXU), jax.ShapeDtypeStruct((T, C), _XDT),
                   jax.ShapeDtypeStruct((T, 1), F32), jax.ShapeDtypeStruct((T, D), F32),
                   jax.ShapeDtypeStruct((T, D), _MXU), jax.ShapeDtypeStruct((T, D), _XDT),
                   jax.ShapeDtypeStruct((T, 1), F32)],
        scratch_shapes=[pltpu.VMEM((tm, C), _MXU), pltpu.VMEM((tm, C), F32)],
        compiler_params=_cp(("parallel",)),
    )(p, g, b, w_s, bsb, w_out, bias, res, g1, b1)


def _gmlp_gate_bwd(dzb, w_out, p, xh, rstd, g, b, w_s, bsb, *, name):
    T, C2 = p.shape
    D = dzb.shape[1]
    C = C2 // 2
    il = C // 2
    G, L, _ = w_s.shape
    tm = _tile(T, 4 * L, L)

    def body(dz_ref, wo_ref, p_ref, xh_ref, rs_ref, g_ref, b_ref, ws_ref, bs_ref,
             dp_ref, dg_ref, db_ref, cs_ref, dws_ref, dbs_ref, vn_ref, u_ref, dvn_ref, du_ref, dus_ref):
        @pl.when(pl.program_id(0) == 0)
        def _():
            dg_ref[...] = jnp.zeros_like(dg_ref)
            db_ref[...] = jnp.zeros_like(db_ref)
            cs_ref[...] = jnp.zeros_like(cs_ref)
            dws_ref[...] = jnp.zeros_like(dws_ref)
            dbs_ref[...] = jnp.zeros_like(dbs_ref)

        dus_ref[...] = lax.dot_general(dz_ref[...].astype(_MXU), wo_ref[...].astype(_MXU), (((1,), (1,)), ((), ())),
                                       preferred_element_type=F32)
        z, gp = _gelu_parts(p_ref[...].astype(F32))
        u, _ = _split_uv(z, il)
        xh = xh_ref[...].astype(F32)
        gam = g_ref[...]
        vn_ref[...] = (xh * gam + b_ref[...]).astype(_MXU)
        u_ref[...] = u
        mask = _tril_mask(L)
        for gi in range(G):
            wc = jnp.where(mask, ws_ref[gi], 0.0).astype(_MXU)
            cols = slice(gi * L, (gi + 1) * L)
            for c in range(tm // L):
                rows = slice(c * L, (c + 1) * L)
                vnb = vn_ref[rows, cols]
                s = jnp.dot(wc, vnb, preferred_element_type=F32) + bs_ref[:, cols]
                d = dus_ref[rows, cols]
                du_ref[rows, cols] = d * s
                ds = d * u_ref[rows, cols]
                dbs_ref[:, cols] += ds
                dsb = ds.astype(_MXU)
                dw = lax.dot_general(dsb, vnb, (((1,), (1,)), ((), ())), preferred_element_type=F32)
                dws_ref[gi] += jnp.where(mask, dw, 0.0)
                dvn_ref[rows, cols] = lax.dot_general(wc, dsb, (((0,), (0,)), ((), ())), preferred_element_type=F32)
        dvn = dvn_ref[...]
        dg_ref[...] += _fold8(dvn * xh)
        db_ref[...] += _fold8(dvn)
        dv = _ln_bwd_rows(dvn, xh, rs_ref[...], gam)
        du = du_ref[...]
        for hb in range(2):
            for part, src in ((0, du), (1, dv)):
                lo = (2 * hb + part) * il
                dp = src[:, hb * il:(hb + 1) * il] * gp[:, lo:lo + il]
                dp_ref[:, lo:lo + il] = dp.astype(_MXU)
                cs_ref[:, lo:lo + il] += _fold8(dp)

    row = lambda i: (i, 0)
    fixed = lambda i: (0, 0)
    part_c = pl.BlockSpec((SUBLANES, C), fixed)
    return pl.pallas_call(
        body, name=name, grid=(T // tm,),
        in_specs=[pl.BlockSpec((tm, D), row), _resident((None, C, D), lambda i: (0, 0, 0)),
                  pl.BlockSpec((tm, C2), row), pl.BlockSpec((tm, C), row),
                  pl.BlockSpec((tm, 1), row), pl.BlockSpec((1, C), fixed), pl.BlockSpec((1, C), fixed),
                  pl.BlockSpec((G, L, L), lambda i: (0, 0, 0)), pl.BlockSpec((L, C), fixed)],
        out_specs=[pl.BlockSpec((tm, C2), row), part_c, part_c, pl.BlockSpec((SUBLANES, C2), fixed),
                   pl.BlockSpec((G, L, L), lambda i: (0, 0, 0)), pl.BlockSpec((L, C), fixed)],
        out_shape=[jax.ShapeDtypeStruct((T, C2), _MXU), jax.ShapeDtypeStruct((SUBLANES, C), F32),
                   jax.ShapeDtypeStruct((SUBLANES, C), F32), jax.ShapeDtypeStruct((SUBLANES, C2), F32),
                   jax.ShapeDtypeStruct((G, L, L), F32), jax.ShapeDtypeStruct((L, C), F32)],
        scratch_shapes=[pltpu.VMEM((tm, C), _MXU), pltpu.VMEM((tm, C), F32), pltpu.VMEM((tm, C), F32),
                        pltpu.VMEM((tm, C), F32), pltpu.VMEM((tm, C), F32)],
        compiler_params=_cp(("arbitrary",)),
    )(dzb, w_out, p, xh, rstd, g, b, w_s, bsb)


def _ffn_conv(h, prev8, w_ref, b_ref):
    h1 = _shift_down(prev8, h, 1)
    h2 = _shift_down(prev8, h, 2)
    return w_ref[pl.ds(2, 1), :] * h + w_ref[pl.ds(1, 1), :] * h1 + w_ref[pl.ds(0, 1), :] * h2 + b_ref[...]


def _resident(block, imap):
    return pl.BlockSpec(block, imap, pipeline_mode=pl.Buffered(1))


def _ffn_fwd_half(j, xb, w_up, w_down, b_up, w_dw, b_dw, *, S, name, prev=None, tail=None, head=None):
    T, D = xb.shape
    N = w_up.shape[-1]
    tn = N // N_CHIPS
    tm = _tile(S, 256)
    spt = S // tm
    last = prev is not None
    alpha = tail[1] if last else None

    def body(*refs):
        x_ref, wu_ref, wd_ref, bu_ref, wc_ref, bc_ref = refs[:6]
        if last:
            yp_ref, res_ref, bd_ref, g_ref, b_ref = refs[9:14]
            o = 14 if head is None else 15
            h_ref, hc_ref, f_ref, y_ref, yb_ref, xh_ref, rs_ref = refs[o:o + 7]
            carry_ref = refs[-1]
        else:
            h_ref, hc_ref, f_ref, yp_ref, carry_ref = refs[6:11]

        @pl.when(pl.program_id(0) % spt == 0)
        def _():
            carry_ref[...] = jnp.zeros_like(carry_ref)

        h = jnp.dot(x_ref[...].astype(_MXU), wu_ref[...].astype(_MXU), preferred_element_type=F32) + bu_ref[...]
        h_ref[...] = h.astype(_HDT)
        hc = _ffn_conv(h, carry_ref[...], wc_ref, bc_ref)
        hc_ref[...] = hc.astype(_HDT)
        carry_ref[...] = h[tm - SUBLANES:tm]
        gte = hc[:, :tn]
        f = (gte * _sigmoid(gte) * hc[:, tn:]).astype(_MXU)
        f_ref[...] = f
        y = jnp.dot(f, wd_ref[...].astype(_MXU), preferred_element_type=F32)
        if not last:
            yp_ref[...] = y
            return
        z = y + yp_ref[...] + bd_ref[...] + alpha * res_ref[...]
        out, xh, rstd = _ln_rows(z, g_ref[...], b_ref[...])
        if head is None:
            y_ref[...] = out
            yb_ref[...] = out.astype(_MXU)
            xh_ref[...] = xh.astype(_XDT)
            rs_ref[...] = rstd
            return
        t_ref, cs_ref, ls_ref = refs[14], refs[o + 7], refs[o + 8]

        @pl.when(pl.program_id(0) == 0)
        def _():
            for acc in (xh_ref, rs_ref, cs_ref, ls_ref):
                acc[...] = jnp.zeros_like(acc)

        err = out - t_ref[...]
        d = err * (1.0 / D)
        dz = _ln_bwd_rows(d, xh, rstd, g_ref[...])
        y_ref[...] = dz
        yb_ref[...] = dz.astype(_MXU)
        xh_ref[...] += _fold8(d * xh)
        rs_ref[...] += _fold8(d)
        cs_ref[...] += _fold8(dz)
        ls_ref[...] += _fold8(err * err)

    row = lambda i: (i, 0)
    pair = lambda i: (0, j)
    vec = pl.BlockSpec((1, D), lambda i: (0, 0))
    tile = pl.BlockSpec((tm, D), row)
    in_specs = [tile, _resident((None, D, 2 * tn), lambda i: (0, 0, j)), _resident((None, tn, D), lambda i: (0, j, 0)),
                pl.BlockSpec((1, 2 * tn), pair), pl.BlockSpec((SUBLANES, 2 * tn), pair), pl.BlockSpec((1, 2 * tn), pair)]
    operands = [xb, w_up, w_down, b_up, w_dw, b_dw]
    wide = pl.BlockSpec((tm, 2 * tn), lambda i: (i, j))
    out_specs = [wide, wide, pl.BlockSpec((tm, tn), lambda i: (i, j))]
    out_shape = [jax.ShapeDtypeStruct((T, N), _HDT), jax.ShapeDtypeStruct((T, N), _HDT),
                 jax.ShapeDtypeStruct((T, N // 2), _MXU)]
    aliases = {}
    if last:
        res, _, b_down, g, b = tail
        in_specs += [ANY, ANY, ANY, tile, tile, vec, vec, vec]
        operands += list(prev) + [res, b_down, g, b]
        aliases = {6: 0, 7: 1, 8: 2}
        if head is None:
            out_specs += [tile, tile, tile, pl.BlockSpec((tm, 1), row)]
            out_shape += [jax.ShapeDtypeStruct((T, D), F32), jax.ShapeDtypeStruct((T, D), _MXU),
                          jax.ShapeDtypeStruct((T, D), _XDT), jax.ShapeDtypeStruct((T, 1), F32)]
        else:
            in_specs.append(tile)
            operands.append(head)
            part = pl.BlockSpec((SUBLANES, D), lambda i: (0, 0))
            out_specs += [tile, tile, part, part, part, part]
            out_shape += [jax.ShapeDtypeStruct((T, D), F32), jax.ShapeDtypeStruct((T, D), _MXU)] \
                + [jax.ShapeDtypeStruct((SUBLANES, D), F32)] * 4
    else:
        out_specs.append(tile)
        out_shape.append(jax.ShapeDtypeStruct((T, D), F32))
    return pl.pallas_call(
        body, name=name, grid=(T // tm,), in_specs=in_specs, out_specs=out_specs, out_shape=out_shape,
        input_output_aliases=aliases, scratch_shapes=[pltpu.VMEM((SUBLANES, 2 * tn), F32)],
        compiler_params=_cp(("arbitrary",)),
    )(*operands)


def _ffn_bwd_half(j, dzb, w_down, w_up, hs, hcs, w_dw, *, S, name, dz=None, alpha=None, prev=None, ln=None):
    T, D = dzb.shape
    N = hs.shape[1]
    tn = N // N_CHIPS
    tm = _tile(S, 256)
    spt = S // tm
    nt = T // tm
    last = prev is not None

    def body(*refs):
        dz_ref, wd_ref, wu_ref, h_ref, hc_ref, wc_ref = refs[:6]
        if last:
            dxp_ref, xh_ref, rs_ref, g_ref = refs[7:11]
            dh_ref, cs_ref, dw_ref, db_ref, dz1_ref, dz1b_ref, dg1_ref, db1_ref, cs1_ref, carry_ref = refs[11:21]
        else:
            dzf_ref = refs[6]
            dh_ref, cs_ref, dw_ref, db_ref, dxp_ref, carry_ref = refs[7:13]
        i = pl.program_id(0)
        ii = nt - 1 - i

        @pl.when(i == 0)
        def _():
            cs_ref[...] = jnp.zeros_like(cs_ref)
            dw_ref[...] = jnp.zeros_like(dw_ref)
            db_ref[...] = jnp.zeros_like(db_ref)
            if last:
                dg1_ref[...] = jnp.zeros_like(dg1_ref)
                db1_ref[...] = jnp.zeros_like(db1_ref)
                cs1_ref[...] = jnp.zeros_like(cs1_ref)

        df = lax.dot_general(dz_ref[...].astype(_MXU), wd_ref[...].astype(_MXU), (((1,), (1,)), ((), ())),
                             preferred_element_type=F32)
        h = h_ref[...].astype(F32)
        gte, val = hc_ref[:, :tn].astype(F32), hc_ref[:, tn:].astype(F32)
        sig = _sigmoid(gte)
        dval = df * (gte * sig)
        dg = df * val * (sig * (1.0 + gte * (1.0 - sig)))
        dhc = jnp.concatenate([dg, dval], axis=1)
        nxt = jnp.where((ii + 1) % spt == 0, 0.0, carry_ref[...])
        d1 = _shift_up(dhc, nxt, 1)
        d2 = _shift_up(dhc, nxt, 2)
        carry_ref[...] = dhc[0:SUBLANES]
        db_ref[...] += _fold8(dhc)
        dw_ref[2] += _fold8(dhc * h)
        dw_ref[1] += _fold8(d1 * h)
        dw_ref[0] += _fold8(d2 * h)
        dh = wc_ref[pl.ds(2, 1), :] * dhc + wc_ref[pl.ds(1, 1), :] * d1 + wc_ref[pl.ds(0, 1), :] * d2
        cs_ref[...] += _fold8(dh)
        dhb = dh.astype(_MXU)
        dh_ref[...] = dhb
        dx = lax.dot_general(dhb, wu_ref[...].astype(_MXU), (((1,), (1,)), ((), ())), preferred_element_type=F32)
        if not last:
            dxp_ref[...] = dx + alpha * dzf_ref[...]
            return
        d = dx + dxp_ref[...]
        xh = xh_ref[...].astype(F32)
        dz1 = _ln_bwd_rows(d, xh, rs_ref[...], g_ref[...])
        dz1_ref[...] = dz1
        dz1b_ref[...] = dz1.astype(_MXU)
        dg1_ref[...] += _fold8(d * xh)
        db1_ref[...] += _fold8(d)
        cs1_ref[...] += _fold8(dz1)

    rev = lambda i: (nt - 1 - i, 0)
    fixed = lambda i: (0, 0)
    pair = lambda i: (0, j)
    tile = pl.BlockSpec((tm, D), rev)
    wide = pl.BlockSpec((tm, 2 * tn), lambda i: (nt - 1 - i, j))
    part = pl.BlockSpec((SUBLANES, 2 * tn), fixed)
    in_specs = [tile, _resident((None, tn, D), lambda i: (0, j, 0)), _resident((None, D, 2 * tn), lambda i: (0, 0, j)),
                wide, wide, pl.BlockSpec((SUBLANES, 2 * tn), pair)]
    operands = [dzb, w_down, w_up, hs, hcs, w_dw]
    out_specs = [wide, part, pl.BlockSpec((3, SUBLANES, 2 * tn), lambda i: (0, 0, 0)), part]
    out_shape = [jax.ShapeDtypeStruct((T, N), _MXU), jax.ShapeDtypeStruct((SUBLANES, 2 * tn), F32),
                 jax.ShapeDtypeStruct((3, SUBLANES, 2 * tn), F32), jax.ShapeDtypeStruct((SUBLANES, 2 * tn), F32)]
    aliases = {}
    if last:
        xh, rstd, g = ln
        in_specs += [ANY, tile, tile, pl.BlockSpec((tm, 1), rev), pl.BlockSpec((1, D), fixed)]
        operands += [prev[0], prev[1], xh, rstd, g]
        aliases = {6: 0}
        out_specs += [tile, tile] + [pl.BlockSpec((SUBLANES, D), fixed)] * 3
        out_shape += [jax.ShapeDtypeStruct((T, D), F32), jax.ShapeDtypeStruct((T, D), _MXU)] \
            + [jax.ShapeDtypeStruct((SUBLANES, D), F32)] * 3
    else:
        in_specs.append(tile)
        operands.append(dz)
        out_specs.append(tile)
        out_shape.append(jax.ShapeDtypeStruct((T, D), F32))
    return pl.pallas_call(
        body, name=name, grid=(nt,), in_specs=in_specs, out_specs=out_specs, out_shape=out_shape,
        input_output_aliases=aliases, scratch_shapes=[pltpu.VMEM((SUBLANES, 2 * tn), F32)],
        compiler_params=_cp(("arbitrary",)),
    )(*operands)


def _sum_pieces(gs, rs, me, *, name):
    n = len(gs)
    _, pr, pc = gs[0].shape
    tr = _tile(pr, 128)

    def body(me_ref, *refs):
        o_ref = refs[2 * n]
        for l in range(n):
            total = refs[l][...].astype(F32)
            for s in range(N_DEV - 1):
                total = total + refs[n + l][s].astype(F32)
            o_ref[l] = total

    own = pl.BlockSpec((None, tr, pc), lambda i, me_ref: (me_ref[0], i, 0))
    got = pl.BlockSpec((N_DEV - 1, tr, pc), lambda i, me_ref: (0, i, 0))
    return pl.pallas_call(
        body, name=name,
        grid_spec=pltpu.PrefetchScalarGridSpec(
            num_scalar_prefetch=1, grid=(pr // tr,), in_specs=[own] * n + [got] * n,
            out_specs=pl.BlockSpec((n, tr, pc), lambda i, me_ref: (0, i, 0))),
        out_shape=jax.ShapeDtypeStruct((n, pr, pc), F32),
        compiler_params=_cp(("parallel",)),
    )(me, *gs, *rs)


def _adam_math(w, g, m, v):
    bc1 = 1.0 - ADAM_B1 ** ADAM_STEP
    bc2 = 1.0 - ADAM_B2 ** ADAM_STEP
    m = ADAM_B1 * m + (1.0 - ADAM_B1) * g
    v = ADAM_B2 * v + (1.0 - ADAM_B2) * (g * g)
    return -ADAM_LR * ((m / bc1) / (jnp.sqrt(v / bc2) + ADAM_EPS) + ADAM_WD * w), m, v


def _adam(w, g, m, v, *, name):
    R, C = w.shape
    tr = _tile(R, 256)

    def body(w_ref, g_ref, m_ref, v_ref, d_ref, mo_ref, vo_ref):
        d_ref[...], mo_ref[...], vo_ref[...] = _adam_math(w_ref[...], g_ref[...], m_ref[...], v_ref[...])

    spec = pl.BlockSpec((tr, C), lambda i: (i, 0))
    return pl.pallas_call(
        body, name=name, grid=(R // tr,), in_specs=[spec] * 4, out_specs=[spec] * 3,
        out_shape=[jax.ShapeDtypeStruct((R, C), F32)] * 3,
        compiler_params=_cp(("parallel",)),
    )(w, g, m, v)


def _adam_halves(w, own, got, m, v, core, *, name):
    L, R, C = w.shape
    rh = R // 2
    tr = _tile(rh, 256)
    nt = rh // tr

    def body(c_ref, w_ref, own_ref, got_ref, m_ref, v_ref, g_ref, d_ref, mo_ref, vo_ref):
        g = jnp.where(pl.program_id(1) == c_ref[0], own_ref[...], got_ref[...])
        g_ref[...] = g
        d_ref[...], mo_ref[...], vo_ref[...] = _adam_math(w_ref[...], g, m_ref[...], v_ref[...])

    full = pl.BlockSpec((None, tr, C), lambda l, h, t, c_ref: (l, h * nt + t, 0))
    half = pl.BlockSpec((None, tr, C), lambda l, h, t, c_ref: (l, t, 0))
    return pl.pallas_call(
        body, name=name,
        grid_spec=pltpu.PrefetchScalarGridSpec(
            num_scalar_prefetch=1, grid=(L, 2, nt), in_specs=[full, half, half, full, full], out_specs=[full] * 4),
        out_shape=[jax.ShapeDtypeStruct((L, R, C), F32)] * 4,
        compiler_params=_cp(("parallel", "parallel", "parallel")),
    )(core, w, own, got, m, v)


def _remote(src, dst, send, recv, dev):
    return pltpu.make_async_remote_copy(src_ref=src, dst_ref=dst, send_sem=send, recv_sem=recv,
                                        device_id=dev, device_id_type=MESH)


def _place_w(shard, pos, layer, *, axis, name):
    _, R, C = shard.shape
    tr = _tile(R, 512, 16)
    nt = R // tr
    if axis == 2:
        out_shape = (1, R, N_CHIPS * C)
        out_map = lambda t, q: (0, t, q[0])
    else:
        out_shape = (1, N_CHIPS * R, C)
        out_map = lambda t, q: (0, q[0] * nt + t, 0)

    def body(q_ref, s_ref, o_ref):
        o_ref[...] = s_ref[...].astype(_WIRE)

    return pl.pallas_call(
        body, name=name,
        grid_spec=pltpu.PrefetchScalarGridSpec(
            num_scalar_prefetch=1, grid=(nt,),
            in_specs=[pl.BlockSpec((None, tr, C), lambda t, q: (layer, t, 0))],
            out_specs=pl.BlockSpec((None, tr, C), out_map)),
        out_shape=jax.ShapeDtypeStruct(out_shape, _WIRE),
        compiler_params=_cp(("parallel",)),
    )(pos, shard)


def _ag_window(ref, kind, px, py, h):
    axis, perm = kind
    q = 2 * px + py
    if perm:
        q = _perm_idx(q)
    if axis == 2:
        R, C = ref.shape[1], ref.shape[2] // N_CHIPS
        rh = R // 2
        return ref.at[:, pl.ds(pl.multiple_of(h * rh, 16), rh), pl.ds(pl.multiple_of(q * C, LANES), C)]
    R = ref.shape[1] // N_CHIPS
    rh = R // 2
    return ref.at[:, pl.ds(pl.multiple_of(q * R + h * rh, 16), rh), :]


def _ag_ici_copies(refs, kinds, send, recv):
    x, y, c = lax.axis_index("x"), lax.axis_index("y"), lax.axis_index("c")
    chips = [(1 - x, y), (x, 1 - y), (1 - x, 1 - y)]
    sends, recvs = [], []
    for a, (ref, kind) in enumerate(zip(refs, kinds)):
        own = _ag_window(ref, kind, x, y, c)
        for i, (px, py) in enumerate(chips):
            k = 3 * a + i
            sends.append(_remote(own, own, send.at[k], recv.at[k], (px, py, c)))
            recvs.append(_remote(own, _ag_window(ref, kind, px, py, c), send.at[k], recv.at[k], (px, py, c)))
    return sends, recvs


def _ag_start(arrs, kinds, after, *, name, copies=_ag_ici_copies):
    n = len(arrs)

    def body(*refs):
        in_refs = refs[:n]
        send, recv = refs[n + len(after)], refs[n + len(after) + 1]
        token = refs[-1]
        sends, _ = copies(in_refs, kinds, send, recv)
        for cp in sends:
            cp.start()
        token[...] = jnp.zeros_like(token)

    sems = pltpu.SemaphoreType.DMA((3 * n,))
    out = pl.pallas_call(
        body, name=name,
        out_shape=(sems, sems) + tuple(pltpu.HBM(a.shape, a.dtype) for a in arrs)
        + (jax.ShapeDtypeStruct((SUBLANES, LANES), F32),),
        in_specs=(HBM,) * n + (ANY,) * len(after),
        out_specs=(SEMS, SEMS) + (HBM,) * n + (pl.BlockSpec(memory_space=pltpu.VMEM),),
        input_output_aliases={a: 2 + a for a in range(n)},
        compiler_params=pltpu.CompilerParams(has_side_effects=EFFECT),
    )(*[pltpu.with_memory_space_constraint(a, pltpu.HBM) for a in arrs], *after)
    return out[0], out[1], list(out[2:2 + n]), out[-1]


def _ag_wait(send, recv, arrs, kinds, after, *, name, copies=_ag_ici_copies):
    n = len(arrs)

    def body(*refs):
        in_refs = refs[:n]
        send, recv = refs[n], refs[n + 1]
        sends, recvs = copies(in_refs, kinds, send, recv)
        for cp in sends:
            cp.wait_send()
        for cp in recvs:
            cp.wait_recv()

    out = pl.pallas_call(
        body, name=name,
        out_shape=tuple(pltpu.HBM(a.shape, a.dtype) for a in arrs),
        in_specs=(HBM,) * n + (SEMS, SEMS) + (ANY,) * len(after), out_specs=(HBM,) * n,
        input_output_aliases={a: a for a in range(n)},
        compiler_params=pltpu.CompilerParams(has_side_effects=EFFECT),
    )(*arrs, send, recv, *after)
    return list(out)


def _ag_d2d_copies(refs, kinds, send, recv):
    x, y, c = lax.axis_index("x"), lax.axis_index("y"), lax.axis_index("c")
    chips = [(1 - x, y), (x, 1 - y), (1 - x, 1 - y)]
    sib = (x, y, 1 - c)
    sends, recvs = [], []
    for a, (ref, kind) in enumerate(zip(refs, kinds)):
        for i, (px, py) in enumerate(chips):
            k = 3 * a + i
            got = _ag_window(ref, kind, px, py, c)
            sends.append(_remote(got, got, send.at[k], recv.at[k], sib))
            recvs.append(_remote(got, _ag_window(ref, kind, px, py, 1 - c), send.at[k], recv.at[k], sib))
    return sends, recvs


def _flip(x, y, c, f):
    return ((1 - x) if f & 4 else x, (1 - y) if f & 2 else y, (1 - c) if f & 1 else c)


def _rs_copies(g_refs, land_refs, send, recv):
    x, y, c = lax.axis_index("x"), lax.axis_index("y"), lax.axis_index("c")
    cps = []
    for a, (g_ref, land_ref) in enumerate(zip(g_refs, land_refs)):
        for f in range(1, N_DEV):
            tx, ty, tcx = _flip(x, y, c, f)
            k = (N_DEV - 1) * a + f - 1
            cps.append(_remote(g_ref.at[4 * tx + 2 * ty + tcx], land_ref.at[f - 1], send.at[k], recv.at[k],
                               (tx, ty, tcx)))
    return cps


def _rs_start(gs, *, name):
    n = len(gs)
    lands = [lax.empty((N_DEV - 1,) + g.shape[1:], g.dtype) for g in gs]

    def body(*refs):
        send, recv, token = refs[2 * n], refs[2 * n + 1], refs[-1]
        for cp in _rs_copies(refs[:n], refs[n:2 * n], send, recv):
            cp.start()
        token[...] = jnp.zeros_like(token)

    sems = pltpu.SemaphoreType.DMA(((N_DEV - 1) * n,))
    thru = [pltpu.HBM(t.shape, t.dtype) for t in gs + lands]
    out = pl.pallas_call(
        body, name=name,
        out_shape=(sems, sems, *thru, jax.ShapeDtypeStruct((SUBLANES, LANES), F32)),
        in_specs=(HBM,) * (2 * n), out_specs=(SEMS, SEMS) + (HBM,) * (2 * n) + (pl.BlockSpec(memory_space=pltpu.VMEM),),
        input_output_aliases={a: 2 + a for a in range(2 * n)},
        compiler_params=pltpu.CompilerParams(has_side_effects=EFFECT),
    )(*[pltpu.with_memory_space_constraint(t, pltpu.HBM) for t in gs + lands])
    return out[0], out[1], list(out[2:2 + n]), list(out[2 + n:2 + 2 * n]), out[-1]


def _rs_wait(send, recv, gs, lands, after, *, name):
    n = len(gs)

    def body(*refs):
        cps = _rs_copies(refs[:n], refs[n:2 * n], refs[2 * n], refs[2 * n + 1])
        for cp in cps:
            cp.wait_send()
        for cp in cps:
            cp.wait_recv()

    out = pl.pallas_call(
        body, name=name,
        out_shape=tuple(pltpu.HBM(t.shape, t.dtype) for t in gs + lands),
        in_specs=(HBM,) * (2 * n) + (SEMS, SEMS, ANY), out_specs=(HBM,) * (2 * n),
        input_output_aliases={a: a for a in range(2 * n)},
        compiler_params=pltpu.CompilerParams(has_side_effects=EFFECT),
    )(*gs, *lands, send, recv, after)
    return list(out[:n]), list(out[n:])


def _pair_copy(own_ref, got_ref, send, recv):
    x, y, c = lax.axis_index("x"), lax.axis_index("y"), lax.axis_index("c")
    return _remote(own_ref, got_ref, send, recv, (x, y, 1 - c))


def _pair_start(own, *, name):
    def body(own_ref, got_ref, send, recv, own_thru, got_thru):
        _pair_copy(own_ref, got_ref, send, recv).start()

    sem = pltpu.SemaphoreType.DMA(())
    return pl.pallas_call(
        body, name=name,
        out_shape=(sem, sem, pltpu.HBM(own.shape, own.dtype), pltpu.HBM(own.shape, own.dtype)),
        in_specs=(HBM, HBM), out_specs=(SEMS, SEMS, HBM, HBM), input_output_aliases={0: 2, 1: 3},
        compiler_params=pltpu.CompilerParams(has_side_effects=EFFECT),
    )(pltpu.with_memory_space_constraint(own, pltpu.HBM),
      pltpu.with_memory_space_constraint(lax.empty(own.shape, own.dtype), pltpu.HBM))


def _pair_wait(send, recv, own, got, *, name):
    def body(own_ref, got_ref, send, recv, own_out, got_out):
        cp = _pair_copy(own_ref, got_ref, send, recv)
        cp.wait_send()
        cp.wait_recv()

    return pl.pallas_call(
        body, name=name,
        out_shape=(pltpu.HBM(own.shape, own.dtype), pltpu.HBM(got.shape, got.dtype)),
        in_specs=(HBM, HBM, SEMS, SEMS), out_specs=(HBM, HBM), input_output_aliases={0: 0, 1: 1},
        compiler_params=pltpu.CompilerParams(has_side_effects=EFFECT),
    )(own, got, send, recv)


def _allreduce_flat(vec, *, name):
    n = vec.shape[0]
    unit = N_DEV * SUBLANES * LANES
    npad = -(-n // unit) * unit
    rows = npad // (N_DEV * LANES)
    xin = jnp.pad(vec, (0, npad - n)).reshape(N_DEV, rows, LANES)

    def body(x_ref, y_ref, a_ref, send_a, recv_a, send_b, recv_b):
        x, y, c = lax.axis_index("x"), lax.axis_index("y"), lax.axis_index("c")
        me = 4 * x + 2 * y + c
        a_ref[me] = x_ref[me]
        sends, recvs = [], []
        for f in range(1, N_DEV):
            dev = _flip(x, y, c, f)
            t = 4 * dev[0] + 2 * dev[1] + dev[2]
            cp = _remote(x_ref.at[t], a_ref.at[me], send_a.at[f - 1], recv_a.at[f - 1], dev)
            cp.start()
            sends.append(cp)
            recvs.append(_remote(x_ref.at[me], a_ref.at[t], send_a.at[f - 1], recv_a.at[f - 1], dev))
        for cp in recvs:
            cp.wait_recv()
        for cp in sends:
            cp.wait_send()
        acc = a_ref[0]
        for s in range(1, N_DEV):
            acc = acc + a_ref[s]
        y_ref[me] = acc
        sends, recvs = [], []
        for f in range(1, N_DEV):
            dev = _flip(x, y, c, f)
            t = 4 * dev[0] + 2 * dev[1] + dev[2]
            cp = _remote(y_ref.at[me], y_ref.at[me], send_b.at[f - 1], recv_b.at[f - 1], dev)
            cp.start()
            sends.append(cp)
            recvs.append(_remote(y_ref.at[me], y_ref.at[t], send_b.at[f - 1], recv_b.at[f - 1], dev))
        for cp in recvs:
            cp.wait_recv()
        for cp in sends:
            cp.wait_send()

    vm = pl.BlockSpec(memory_space=pltpu.VMEM)
    out = pl.pallas_call(
        body, name=name, in_specs=[vm], out_specs=vm,
        out_shape=jax.ShapeDtypeStruct((N_DEV, rows, LANES), F32),
        scratch_shapes=[pltpu.VMEM((N_DEV, rows, LANES), F32)] + [pltpu.SemaphoreType.DMA((N_DEV - 1,))] * 4,
        compiler_params=_cp(),
    )(xin)
    return out.reshape(npad)[:n]


def _perm_cols(v, blocks=N_CHIPS):
    w = v.shape[-1] // blocks
    return jnp.concatenate([v[..., q * w:(q + 1) * w] for q in PERM], axis=-1)


def _pack(arrs):
    return jnp.concatenate([a.reshape(-1).astype(F32) for a in arrs])


def _unpack(flat, shapes):
    out, pos = [], 0
    for s in shapes:
        n = 1
        for d in s:
            n *= d
        out.append(flat[pos:pos + n].reshape(s))
        pos += n
    return out


def kernel(x, conv_w_in, conv_b_in, conv_w_dw, conv_b_dw, conv_ln_g, conv_ln_b, conv_w_out, conv_b_out, gmlp_w_in, gmlp_b_in, gmlp_ln_g, gmlp_ln_b, gmlp_w_s, gmlp_b_s, gmlp_w_out, gmlp_b_out, ffn_w_up, ffn_b_up, ffn_w_dw, ffn_b_dw, ffn_w_down, ffn_b_down, norm1_g, norm1_b, norm2_g, norm2_b, loss_target, m_conv_w_in, m_conv_b_in, m_conv_w_dw, m_conv_b_dw, m_conv_ln_g, m_conv_ln_b, m_conv_w_out, m_conv_b_out, m_gmlp_w_in, m_gmlp_b_in, m_gmlp_ln_g, m_gmlp_ln_b, m_gmlp_w_s, m_gmlp_b_s, m_gmlp_w_out, m_gmlp_b_out, m_ffn_w_up, m_ffn_b_up, m_ffn_w_dw, m_ffn_b_dw, m_ffn_w_down, m_ffn_b_down, m_norm1_g, m_norm1_b, m_norm2_g, m_norm2_b, v_conv_w_in, v_conv_b_in, v_conv_w_dw, v_conv_b_dw, v_conv_ln_g, v_conv_ln_b, v_conv_w_out, v_conv_b_out, v_gmlp_w_in, v_gmlp_b_in, v_gmlp_ln_g, v_gmlp_ln_b, v_gmlp_w_s, v_gmlp_b_s, v_gmlp_w_out, v_gmlp_b_out, v_ffn_w_up, v_ffn_b_up, v_ffn_w_dw, v_ffn_b_dw, v_ffn_w_down, v_ffn_b_down, v_norm1_g, v_norm1_b, v_norm2_g, v_norm2_b):
    P = dict(locals())
    WEIGHTS = ['conv_w_in', 'conv_b_in', 'conv_w_dw', 'conv_b_dw', 'conv_ln_g', 'conv_ln_b', 'conv_w_out',
               'conv_b_out', 'gmlp_w_in', 'gmlp_b_in', 'gmlp_ln_g', 'gmlp_ln_b', 'gmlp_w_s', 'gmlp_b_s',
               'gmlp_w_out', 'gmlp_b_out', 'ffn_w_up', 'ffn_b_up', 'ffn_w_dw', 'ffn_b_dw', 'ffn_w_down',
               'ffn_b_down', 'norm1_g', 'norm1_b', 'norm2_g', 'norm2_b']
    BIG = ['conv_w_in', 'conv_w_out', 'gmlp_w_in', 'gmlp_w_out', 'ffn_w_up', 'ffn_w_down']
    SMALL_SHARDED = {'conv_w_dw': 2, 'gmlp_b_in': 1, 'gmlp_ln_g': 1, 'gmlp_ln_b': 1, 'gmlp_b_out': 1, 'ffn_w_dw': 2}

    B, S, D = x.shape
    T = B * S
    depth = norm1_g.shape[0]
    alpha = (2.0 * depth) ** 0.25
    C = conv_w_out.shape[-1]
    F2 = ffn_b_up.shape[-1]
    G, L = gmlp_w_s.shape[1], gmlp_w_s.shape[2]
    xi, yi, ci = lax.axis_index("x"), lax.axis_index("y"), lax.axis_index("c")
    shard = 2 * xi + yi

    i32 = lambda v: jnp.reshape(v, (1,)).astype(jnp.int32)
    pos_plain, pos_perm = i32(shard), i32(_perm_idx(shard))
    me_id, core_id = i32(4 * xi + 2 * yi + ci), i32(ci)

    groups = []
    for i in range(depth):
        mix = 'conv' if i % 2 == 0 else 'gmlp'
        groups.append((f"{mix}{i // 2}", [(mix + '_w_in', i // 2, 2, True), (mix + '_w_out', i // 2, 1, False)]))
        groups.append((f"ffn{i}", [('ffn_w_up', i, 2, True), ('ffn_w_down', i, 1, False)]))
    sm_names = list(SMALL_SHARDED)
    sm_shapes = [P[n].shape for n in sm_names]
    mine = _pack([P[n] for n in sm_names]) * (ci == 0).astype(F32)
    buf = jnp.zeros((N_CHIPS, mine.shape[0]), F32)
    buf = lax.dynamic_update_slice(buf, mine[None], (shard, 0))
    gathered = _allreduce_flat(buf.reshape(-1), name="ag_small").reshape(N_CHIPS, -1)

    started, order = {}, [gathered]
    for gname, members in groups:
        placed = [_place_w(P[n], pos_perm if perm else pos_plain, l, axis=axis, name=f"place_{n}_{l}")
                  for n, l, axis, perm in members]
        kinds = [(axis, perm) for _, _, axis, perm in members]
        send, recv, arrs, token = _ag_start(placed, kinds, order, name=f"ag_start_{gname}")
        order = [token]
        started[gname] = (send, recv, arrs, kinds, [(n, l) for n, l, _, _ in members])
    wts = {}

    def landed_ici(gname, after):
        send, recv, arrs, kinds, keys = started[gname]
        arrs = _ag_wait(send, recv, arrs, kinds, after, name=f"ag_wait_{gname}")
        send, recv, arrs, _ = _ag_start(arrs, kinds, [], name=f"ag_fwd_start_{gname}", copies=_ag_d2d_copies)
        started[gname] = (send, recv, arrs, kinds, keys)

    def arrive(gname, after):
        send, recv, arrs, kinds, keys = started[gname]
        arrs = _ag_wait(send, recv, arrs, kinds, after, name=f"ag_fwd_wait_{gname}", copies=_ag_d2d_copies)
        wts.update(zip(keys, arrs))

    full = {}
    for n, parts in zip(sm_names, zip(*[_unpack(gathered[k], sm_shapes) for k in range(N_CHIPS)])):
        full[n] = jnp.concatenate(parts, axis=SMALL_SHARDED[n])
    for n in WEIGHTS:
        if n not in BIG and n not in full:
            full[n] = P[n]

    assert G * L == C, "a gMLP group must be as wide as a chunk is long"

    def row(v):
        return v.reshape(1, -1)

    def pad_rows(v, r):
        return jnp.pad(v, ((0, r - v.shape[0]), (0, 0)))

    xf = x.reshape(T, D)
    saved = []
    cur, cur_b = xf, xf.astype(_MXU)
    for i in range(depth):
        j = i // 2
        sv = {'x': cur, 'xb': cur_b}
        if i == 0:
            landed_ici(groups[0][0], order)
        arrive(groups[2 * i][0], [] if i == 0 else [cur_b])
        if i % 2 == 0:
            b_in = row(_perm_cols(full['conv_b_in'][j]))
            h1 = _mm(cur_b, wts['conv_w_in', j], bl=0, bias=b_in, tm=_tile(T, 1024), tn=_tile(2 * C, 1024, LANES),
                     tk=D, name=f"conv_in_{j}", n_outer=True, out_dtype=_ADT)
            wdw = pad_rows(full['conv_w_dw'][j], CONV_TAPS_PAD)
            dwo = _conv_fwd(h1, wdw, row(full['conv_b_dw'][j]), B=B, S=S, name=f"conv_dw_{j}")
            landed_ici(groups[2 * i + 1][0], [dwo])
            s_act, xhc, rsc, *y1 = _conv_tail_fwd(
                dwo, row(full['conv_ln_g'][j]), row(full['conv_ln_b'][j]), wts['conv_w_out', j],
                row(full['conv_b_out'][j]), cur, alpha, row(norm1_g[i]), row(norm1_b[i]), name=f"conv_out_ln_{j}")
            sv.update(h1=h1, wdw=wdw, act=s_act, xhc=xhc, rsc=rsc)
        else:
            b_in = row(_perm_cols(full['gmlp_b_in'][j]))
            pre = _mm(cur_b, wts['gmlp_w_in', j], bl=0, bias=b_in, tm=_tile(T, 1024), tn=_tile(2 * C, 1024, LANES),
                      tk=D, name=f"gmlp_in_{j}", n_outer=True, out_dtype=_ADT)
            bsb = jnp.repeat(gmlp_b_s[j].T, L, axis=1)
            landed_ici(groups[2 * i + 1][0], [pre])
            us, xhv, rsv, *y1 = _gmlp_gate_fwd(
                pre, row(full['gmlp_ln_g'][j]), row(full['gmlp_ln_b'][j]), gmlp_w_s[j], bsb, wts['gmlp_w_out', j],
                row(full['gmlp_b_out'][j]), cur, alpha, row(norm1_g[i]), row(norm1_b[i]), name=f"gmlp_gate_{j}")
            sv.update(pre=pre, bsb=bsb, act=us, xhv=xhv, rsv=rsv)
        x1, x1b, xh1, rs1 = y1
        arrive(groups[2 * i + 1][0], [x1b])
        wdw3 = pad_rows(_perm_cols(full['ffn_w_dw'][i]), SUBLANES)
        bdw3 = row(_perm_cols(ffn_b_dw[i]))
        ffn_in = (x1b, wts['ffn_w_up', i], wts['ffn_w_down', i], row(_perm_cols(ffn_b_up[i])), wdw3, bdw3)
        first = _ffn_fwd_half(0, *ffn_in, S=S, name=f"ffn_fwd_a_{i}")
        if i < depth - 1:
            landed_ici(groups[2 * i + 2][0], [first[3]])
        ffn_tail = (x1, alpha, row(ffn_b_down[i]), row(norm2_g[i]), row(norm2_b[i]))
        sv.update(x1=x1, x1b=x1b, xh1=xh1, rs1=rs1, wdw3=wdw3)
        if i < depth - 1:
            hs, hcs, f_act, cur, cur_b, xh2, rs2 = _ffn_fwd_half(1, *ffn_in, S=S, name=f"ffn_fwd_b_{i}", prev=first,
                                                                 tail=ffn_tail)
            sv.update(xh2=xh2, rs2=rs2)
        else:
            hs, hcs, f_act, *sv['head'] = _ffn_fwd_half(1, *ffn_in, S=S, name=f"ffn_fwd_b_{i}", prev=first,
                                                         tail=ffn_tail, head=loss_target.reshape(T, D))
        sv.update(hs=hs, hcs=hcs, f=f_act)
        saved.append(sv)

    sg = {n: [None] * full[n].shape[0] for n in WEIGHTS if n not in BIG}
    inflight = {n: [None] * P[n].shape[0] for n in BIG}
    deps = []
    dcur = None
    loss_part = None
    tk_t = _tile(T, 2048)

    ready = []

    def wgrad(n, l, a_, b_, **kw):
        tk = T if n.endswith('w_in') else tk_t
        ready.append((n, l, _mm(a_, b_, ta=True, out_dtype=_WIRE, tk=tk, name=f"{n}_dw_{l}", deps=deps, **kw)))
        launch(f"{n}_{l}")

    def launch(gname):
        send, recv, gs, lands, token = _rs_start([g for _, _, g in ready], name=f"rs_start_{gname}")
        group = {'name': gname, 'flight': (send, recv, gs, lands), 'landed': None}
        for a, (n, l, _) in enumerate(ready):
            inflight[n][l] = (group, a)
        del ready[:]
        deps.append(token)

    def landed(n, l):
        group, a = inflight[n][l]
        if group['landed'] is None:
            group['landed'] = _rs_wait(*group['flight'], dcur, name=f"rs_wait_{group['name']}")
        return group['landed'][0][a], group['landed'][1][a]

    for i in reversed(range(depth)):
        j = i // 2
        sv = saved[i]
        if i == depth - 1:
            dz2, dz2b, dg, db, cs, loss_part = sv['head']
        else:
            dz2, dz2b, dg, db, cs = dcur
        sg['norm2_g'][i], sg['norm2_b'][i], sg['ffn_b_down'][i] = dg.sum(0), db.sum(0), cs.sum(0)
        Fh = F2 // 2
        wgrad('ffn_w_down', i, sv['f'], dz2b, tm=Fh // 2, tn=_tile(D, 1024, LANES), pieces=('row',))
        ffn_in = (dz2b, wts['ffn_w_down', i], wts['ffn_w_up', i], sv['hs'], sv['hcs'], sv['wdw3'])
        dh0, csu0, dwd0, dbd0, dxp = _ffn_bwd_half(0, *ffn_in, S=S, name=f"ffn_bwd_a_{i}", dz=dz2, alpha=alpha)
        dh, csu1, dwd1, dbd1, dz1, dz1b, dg, db, cs = _ffn_bwd_half(
            1, *ffn_in, S=S, name=f"ffn_bwd_b_{i}", prev=(dh0, dxp), ln=(sv['xh1'], sv['rs1'], row(norm1_g[i])))
        sg['ffn_b_up'][i] = _perm_cols(jnp.concatenate([csu0.sum(0), csu1.sum(0)], axis=-1))
        sg['ffn_w_dw'][i] = _perm_cols(jnp.concatenate([dwd0.sum(1), dwd1.sum(1)], axis=-1))
        sg['ffn_b_dw'][i] = _perm_cols(jnp.concatenate([dbd0.sum(0), dbd1.sum(0)], axis=-1))
        wgrad('ffn_w_up', i, sv['x1b'], dh, tm=D, tn=F2 // N_CHIPS, pieces=('col', True))
        sg['norm1_g'][i], sg['norm1_b'][i] = dg.sum(0), db.sum(0)
        if i % 2 == 0:
            sg['conv_b_out'][j] = cs.sum(0)
            wgrad('conv_w_out', j, sv['act'], dz1b, tm=_tile(C, 1024), tn=_tile(D, 1024, LANES), pieces=('row',))
            ddw, dg, db = _ln_silu_bwd(dz1b, wts['conv_w_out', j], sv['xhc'], sv['rsc'], row(full['conv_ln_g'][j]),
                                       row(full['conv_ln_b'][j]), name=f"conv_ln_bwd_{j}")
            sg['conv_ln_g'][j], sg['conv_ln_b'][j] = dg.sum(0), db.sum(0)
            dglu, dwk, dbk = _conv_bwd(ddw, sv['h1'], sv['wdw'], B=B, S=S, name=f"conv_dw_bwd_{j}")
            sg['conv_w_dw'][j] = dwk.sum(1)[:conv_w_dw.shape[1]]
            sg['conv_b_dw'][j] = dbk.sum(0)
            dh1, csi = _glu_bwd(dglu, sv['h1'], name=f"conv_glu_bwd_{j}")
            sg['conv_b_in'][j] = _perm_cols(csi.sum(0))
            fam = 'conv_w_in'
        else:
            sg['gmlp_b_out'][j] = cs.sum(0)
            wgrad('gmlp_w_out', j, sv['act'], dz1b, tm=_tile(C, 1024), tn=_tile(D, 1024, LANES), pieces=('row',))
            dh1, dg, db, csi, dws, dbs = _gmlp_gate_bwd(dz1b, wts['gmlp_w_out', j], sv['pre'], sv['xhv'], sv['rsv'],
                                                        row(full['gmlp_ln_g'][j]), row(full['gmlp_ln_b'][j]),
                                                        gmlp_w_s[j], sv['bsb'], name=f"gmlp_gate_bwd_{j}")
            sg['gmlp_ln_g'][j], sg['gmlp_ln_b'][j] = dg.sum(0), db.sum(0)
            sg['gmlp_b_in'][j] = _perm_cols(csi.sum(0))
            sg['gmlp_w_s'][j] = dws
            sg['gmlp_b_s'][j] = dbs.reshape(L, G, L).sum(-1).T
            fam = 'gmlp_w_in'
        wgrad(fam, j, sv['xb'], dh1, tm=D, tn=(2 * C) // N_CHIPS, pieces=('col', True))
        if i > 0:
            below = saved[i - 1]
            dcur = _mm_ln_bwd(dh1, wts[fam, j], dz1, alpha, below['xh2'], below['rs2'], row(norm2_g[i - 1]),
                              name=f"{fam}_dx_{j}", deps=deps)
        else:
            dcur = _mm(dh1, wts[fam, j], bl=0, tb=True, res=dz1, res_scale=alpha, tm=_tile(T, 512),
                       tn=_tile(D, 1024, LANES), tk=2 * C, name=f"{fam}_dx_{j}", deps=deps)
    grad_x = dcur.reshape(B, S, D)

    small_names = [n for n in WEIGHTS if n not in BIG]
    small_full = [jnp.stack(sg[n]) for n in small_names]
    flat = _pack(small_full + [loss_part])
    red = _allreduce_flat(flat, name="ar_small")
    red_parts = _unpack(red, [a.shape for a in small_full] + [loss_part.shape])
    loss = (0.5 / D) * jnp.sum(red_parts[-1])
    grads = {}
    for n, g in zip(small_names, red_parts[:-1]):
        if n in SMALL_SHARDED:
            ax = SMALL_SHARDED[n]
            width = P[n].shape[ax]
            g = lax.dynamic_slice_in_dim(g, shard * width, width, axis=ax)
        grads[n] = g

    big_out = {}
    def finish(n, swap):
        own, got = _pair_wait(*swap, name=f"px_wait_{n}")
        big_out[n] = _adam_halves(P[n], own, got, P['m_' + n], P['v_' + n], core_id, name=f"adam_{n}")

    behind = None
    for n in ['ffn_w_down', 'ffn_w_up', 'gmlp_w_out', 'gmlp_w_in', 'conv_w_out', 'conv_w_in']:
        both = [landed(n, l) for l in range(len(inflight[n]))]
        own = _sum_pieces([g for g, _ in both], [r for _, r in both], me_id, name=f"sum_{n}")
        swap = _pair_start(own, name=f"px_start_{n}")
        if behind is not None:
            finish(*behind)
        behind = (n, swap)
    finish(*behind)

    shapes = [P[n].shape for n in small_names]
    n_small = sum(functools.reduce(lambda p_, d_: p_ * d_, s_, 1) for s_ in shapes)
    unit = SUBLANES * LANES
    npad = -(-n_small // unit) * unit

    def flat2d(arrs, fill=0.0):
        v = _pack(arrs)
        return jnp.pad(v, (0, npad - n_small), constant_values=fill).reshape(-1, LANES)

    dl, mo, vo = _adam(flat2d([P[n] for n in small_names]), flat2d([grads[n] for n in small_names]),
                       flat2d([P['m_' + n] for n in small_names]),
                       flat2d([P['v_' + n] for n in small_names], fill=1.0), name="adam_small")
    small_out = {n: [grads[n], None, None, None] for n in small_names}
    for k, t in enumerate((dl, mo, vo)):
        for n, a in zip(small_names, _unpack(t.reshape(-1), shapes)):
            small_out[n][k + 1] = a

    outs = [loss, grad_x]
    for k in range(4):
        for n in WEIGHTS:
            outs.append(big_out[n][k] if n in BIG else small_out[n][k])
    return tuple(outs)
```

```python
import functools

import jax
import jax.numpy as jnp
from jax import lax
from jax.experimental import pallas as pl
from jax.experimental.pallas import tpu as pltpu

F32 = jnp.float32
_MXU = jnp.bfloat16
_WIRE = jnp.bfloat16
_HDT = jnp.bfloat16
_ADT = jnp.bfloat16
_XDT = jnp.bfloat16
LN_EPS = 1e-5
ADAM_LR, ADAM_B1, ADAM_B2, ADAM_EPS, ADAM_WD, ADAM_STEP = 0.001, 0.9, 0.999, 1e-08, 0.01, 10
N_CHIPS = 4
N_DEV = 8
LANES = 128
SUBLANES = 8
CONV_TAPS_PAD = 32
VMEM_LIMIT = 56 << 20
MESH = pl.DeviceIdType.MESH
ANY = pl.BlockSpec(memory_space=pl.ANY)
HBM = pl.BlockSpec(memory_space=pltpu.HBM)
SEMS = pl.BlockSpec(memory_space=pltpu.SEMAPHORE)
EFFECT = pltpu.SideEffectType.DATAFLOW_SIDE_EFFECTING
PERM = (0, 2, 1, 3)


def _cp(sem=None):
    return pltpu.CompilerParams(dimension_semantics=sem, vmem_limit_bytes=VMEM_LIMIT)


def _tile(dim, pref, mult=SUBLANES):
    if dim <= pref:
        return dim
    t = (pref // mult) * mult
    while t > mult and dim % t:
        t -= mult
    assert dim % t == 0, (dim, pref, mult)
    return t


def _perm_idx(q):
    return (q % 2) * 2 + q // 2


def _fold8(t):
    r, n = t.shape
    return t.reshape(r // SUBLANES, SUBLANES, n).sum(axis=0)


def _ln_rows(z, g, b):
    mu = jnp.mean(z, axis=-1, keepdims=True)
    xc = z - mu
    var = jnp.mean(xc * xc, axis=-1, keepdims=True)
    rstd = lax.rsqrt(var + LN_EPS)
    xh = xc * rstd
    return xh * g + b, xh, rstd


def _ln_bwd_rows(dy, xh, rstd, g):
    dxh = dy * g
    m1 = jnp.mean(dxh, axis=-1, keepdims=True)
    m2 = jnp.mean(dxh * xh, axis=-1, keepdims=True)
    return rstd * (dxh - m1 - xh * m2)


def _sigmoid(v):
    return 0.5 * jnp.tanh(0.5 * v) + 0.5


def _gelu_parts(p):
    cdf = 0.5 * (1.0 + lax.erf(p * 0.7071067811865476))
    pdf = jnp.exp(-0.5 * p * p) * 0.3989422804014327
    return p * cdf, cdf + p * pdf


def _shift_down(prev8, t, s):
    ext = jnp.concatenate([prev8, t], axis=0)
    return pltpu.roll(ext, s, 0)[SUBLANES:]


def _shift_up(t, next8, s):
    n = t.shape[0]
    ext = jnp.concatenate([t, next8], axis=0)
    return pltpu.roll(ext, n + SUBLANES - s, 0)[:n]


def _mm(a, b, *, ta=False, tb=False, bl=None, bias=None, res=None, res_scale=1.0, out_dtype=F32,
        tm, tn, tk, name, pieces=None, deps=None, n_outer=False):
    M, K = (a.shape[1], a.shape[0]) if ta else a.shape
    bs = b.shape[1:] if bl is not None else b.shape
    N, Kb = (bs[0], bs[1]) if tb else (bs[1], bs[0])
    assert K == Kb and M % tm == 0 and N % tn == 0 and K % tk == 0, (a.shape, b.shape, tm, tn, tk)
    gm, gn, gk = M // tm, N // tn, K // tk

    def spec(block, imap):
        if n_outer:
            return pl.BlockSpec(block, lambda j, i, k: imap(i, j, k))
        return pl.BlockSpec(block, imap)

    a_spec = spec((tk, tm), lambda i, j, k: (k, i)) if ta else spec((tm, tk), lambda i, j, k: (i, k))
    bblk = (tn, tk) if tb else (tk, tn)
    bmap = (lambda i, j, k: (j, k)) if tb else (lambda i, j, k: (k, j))
    if bl is not None:
        b_spec = spec((None,) + bblk, lambda i, j, k: (bl,) + bmap(i, j, k))
    else:
        b_spec = spec(bblk, bmap)
    in_specs, operands = [a_spec, b_spec], [a, b]
    if bias is not None:
        in_specs.append(spec((1, tn), lambda i, j, k: (0, j)))
        operands.append(bias)
    if res is not None:
        in_specs.append(spec((tm, tn), lambda i, j, k: (i, j)))
        operands.append(res)
    n_dep = len(deps) if deps else 0
    if n_dep:
        in_specs += [ANY] * n_dep
        operands += deps
        del deps[:]
    if pieces is None:
        out_shape = jax.ShapeDtypeStruct((M, N), out_dtype)
        out_spec = spec((tm, tn), lambda i, j, k: (i, j))
        ppb = pr = None
    elif pieces[0] == 'col':
        pr, pc = M // 2, N // N_CHIPS
        assert tm % pr == 0 and pc % tn == 0
        ppb, per = tm // pr, pc // tn
        perm = pieces[1]
        out_shape = jax.ShapeDtypeStruct((N_DEV, pr, pc), out_dtype)
        out_spec = spec(
            (ppb, pr, tn),
            lambda i, j, k: ((2 * (_perm_idx(j // per) if perm else j // per)) // ppb + i, 0, j % per))
    else:
        pr = M // N_DEV
        assert tm % pr == 0
        ppb = tm // pr
        out_shape = jax.ShapeDtypeStruct((N_DEV, pr, N), out_dtype)
        out_spec = spec((ppb, pr, tn), lambda i, j, k: (i, 0, j))
    dims = (((0 if ta else 1,), (1 if tb else 0,)), ((), ()))

    def body(*refs):
        a_ref, b_ref = refs[0], refs[1]
        pos = 2
        bias_ref = res_ref = None
        if bias is not None:
            bias_ref = refs[pos]
            pos += 1
        if res is not None:
            res_ref = refs[pos]
            pos += 1
        pos += n_dep
        o_ref = refs[pos]

        def finish(r):
            if bias_ref is not None:
                r = r + bias_ref[...]
            if res_ref is not None:
                r = r + res_scale * res_ref[...]
            if pieces is not None:
                r = r.reshape(ppb, pr, tn)
            o_ref[...] = r.astype(out_dtype)

        part = lax.dot_general(a_ref[...].astype(_MXU), b_ref[...].astype(_MXU), dims, preferred_element_type=F32)
        if gk == 1:
            finish(part)
            return
        acc_ref = refs[pos + 1]
        k = pl.program_id(2)

        @pl.when(k == 0)
        def _():
            acc_ref[...] = part

        @pl.when((k > 0) & (k < gk - 1))
        def _():
            acc_ref[...] += part

        @pl.when(k == gk - 1)
        def _():
            finish(acc_ref[...] + part)

    return pl.pallas_call(
        body, name=name, grid=(gn, gm, gk) if n_outer else (gm, gn, gk), in_specs=in_specs, out_specs=out_spec,
        out_shape=out_shape, scratch_shapes=[pltpu.VMEM((tm, tn), F32)] if gk > 1 else [],
        compiler_params=_cp(("parallel", "parallel", "arbitrary")),
    )(*operands)


def _mm_ln_bwd(a, w, res, res_scale, xh, rstd, g, *, name, deps=None):
    T, K = a.shape
    D = w.shape[1]
    tm = _tile(T, 512)
    n_dep = len(deps) if deps else 0

    def body(a_ref, w_ref, res_ref, xh_ref, rs_ref, g_ref, *rest):
        dz_ref, dzb_ref, dg_ref, db_ref, cs_ref = rest[n_dep:]

        @pl.when(pl.program_id(0) == 0)
        def _():
            dg_ref[...] = jnp.zeros_like(dg_ref)
            db_ref[...] = jnp.zeros_like(db_ref)
            cs_ref[...] = jnp.zeros_like(cs_ref)

        d = lax.dot_general(a_ref[...].astype(_MXU), w_ref[...].astype(_MXU), (((1,), (1,)), ((), ())),
                            preferred_element_type=F32) + res_scale * res_ref[...]
        xh = xh_ref[...].astype(F32)
        dz = _ln_bwd_rows(d, xh, rs_ref[...], g_ref[...])
        dz_ref[...] = dz
        dzb_ref[...] = dz.astype(_MXU)
        dg_ref[...] += _fold8(d * xh)
        db_ref[...] += _fold8(d)
        cs_ref[...] += _fold8(dz)

    row = lambda i: (i, 0)
    fixed = lambda i: (0, 0)
    tile = pl.BlockSpec((tm, D), row)
    part = pl.BlockSpec((SUBLANES, D), fixed)
    operands = [a, w, res, xh, rstd, g] + (list(deps) if deps else [])
    if deps:
        del deps[:]
    return pl.pallas_call(
        body, name=name, grid=(T // tm,),
        in_specs=[pl.BlockSpec((tm, K), row),
                  pl.BlockSpec((None, D, K), lambda i: (0, 0, 0), pipeline_mode=pl.Buffered(1)),
                  tile, tile, pl.BlockSpec((tm, 1), row), pl.BlockSpec((1, D), fixed)] + [ANY] * n_dep,
        out_specs=[tile, tile, part, part, part],
        out_shape=[jax.ShapeDtypeStruct((T, D), F32), jax.ShapeDtypeStruct((T, D), _MXU)]
        + [jax.ShapeDtypeStruct((SUBLANES, D), F32)] * 3,
        compiler_params=_cp(("arbitrary",)),
    )(*operands)


def _out_ln(act, wo_ref, bias_ref, res_ref, alpha, g_ref, b_ref, y_ref, yb_ref, xh_ref, rs_ref):
    z = jnp.dot(act, wo_ref[...].astype(_MXU), preferred_element_type=F32) + bias_ref[...] + alpha * res_ref[...]
    y, xh, rstd = _ln_rows(z, g_ref[...], b_ref[...])
    y_ref[...] = y
    yb_ref[...] = y.astype(_MXU)
    xh_ref[...] = xh.astype(_XDT)
    rs_ref[...] = rstd


def _conv_tail_fwd(v, gc, bc, w, bias, res, alpha, g, b, *, name):
    T, C = v.shape
    D = w.shape[-1]
    tm = _tile(T, 512)

    def body(v_ref, gc_ref, bc_ref, w_ref, bias_ref, res_ref, g_ref, b_ref,
             s_ref, xhc_ref, rsc_ref, y_ref, yb_ref, xh_ref, rs_ref):
        yv, xhc, rsc = _ln_rows(v_ref[...], gc_ref[...], bc_ref[...])
        s = (yv * _sigmoid(yv)).astype(_MXU)
        s_ref[...] = s
        xhc_ref[...] = xhc.astype(_XDT)
        rsc_ref[...] = rsc
        _out_ln(s, w_ref, bias_ref, res_ref, alpha, g_ref, b_ref, y_ref, yb_ref, xh_ref, rs_ref)

    row = lambda i: (i, 0)
    fixed = lambda i: (0, 0)
    vc, vd = pl.BlockSpec((1, C), fixed), pl.BlockSpec((1, D), fixed)
    tc_, td = pl.BlockSpec((tm, C), row), pl.BlockSpec((tm, D), row)
    one = pl.BlockSpec((tm, 1), row)
    return pl.pallas_call(
        body, name=name, grid=(T // tm,),
        in_specs=[tc_, vc, vc, _resident((None, C, D), lambda i: (0, 0, 0)), vd, td, vd, vd],
        out_specs=[tc_, tc_, one, td, td, td, one],
        out_shape=[jax.ShapeDtypeStruct((T, C), _MXU), jax.ShapeDtypeStruct((T, C), _XDT),
                   jax.ShapeDtypeStruct((T, 1), F32), jax.ShapeDtypeStruct((T, D), F32),
                   jax.ShapeDtypeStruct((T, D), _MXU), jax.ShapeDtypeStruct((T, D), _XDT),
                   jax.ShapeDtypeStruct((T, 1), F32)],
        compiler_params=_cp(("parallel",)),
    )(v, gc, bc, w, bias, res, g, b)


def _conv_cols(C, tc):
    per = (C // 2) // tc
    return per, (lambda j: (j // per) * (2 * per) + j % per)


def _glu_shifted(a_ref, g_ref, p_ref, S):
    u = a_ref[...].astype(F32) * _sigmoid(g_ref[...].astype(F32))
    rows = lax.broadcasted_iota(jnp.int32, (SUBLANES, u.shape[1]), 0)
    lo = CONV_TAPS_PAD
    for r in range(SUBLANES):
        p_ref[r, 0:lo, :] = jnp.zeros((lo, u.shape[1]), F32)
        if r == 0:
            p_ref[r, lo:lo + S, :] = u
        else:
            rolled = pltpu.roll(u, r, 0)
            p_ref[r, lo:lo + S, :] = rolled
            p_ref[r, lo:lo + SUBLANES, :] = jnp.where(rows >= r, rolled[0:SUBLANES], 0.0)


def _conv_fwd(h1, w_dw, b_dw, *, B, S, name):
    C = w_dw.shape[1]
    taps = CONV_TAPS_PAD - 1
    tc = LANES
    ch = _tile(S, 128)
    per, col_a = _conv_cols(C, tc)

    def body(a_ref, g_ref, w_ref, b_ref, o_ref, p_ref):
        _glu_shifted(a_ref, g_ref, p_ref, S)

        def chunk(ci, carry):
            base = pl.multiple_of(ci * ch, ch)
            acc = jnp.zeros((ch, tc), F32) + b_ref[...]
            for k in range(taps):
                q, r = divmod(taps - 1 - k, SUBLANES)
                start = pl.multiple_of(base + (CONV_TAPS_PAD - SUBLANES * q), SUBLANES)
                acc = acc + w_ref[pl.ds(k, 1), :] * p_ref[r, pl.ds(start, ch), :]
            o_ref[pl.ds(base, ch), :] = acc
            return carry

        lax.fori_loop(0, S // ch, chunk, 0)

    return pl.pallas_call(
        body, name=name, grid=(B, C // tc),
        in_specs=[pl.BlockSpec((S, tc), lambda b, j: (b, col_a(j))),
                  pl.BlockSpec((S, tc), lambda b, j: (b, col_a(j) + per)),
                  pl.BlockSpec((CONV_TAPS_PAD, tc), lambda b, j: (0, j)),
                  pl.BlockSpec((1, tc), lambda b, j: (0, j))],
        out_specs=pl.BlockSpec((S, tc), lambda b, j: (b, j)),
        out_shape=jax.ShapeDtypeStruct((B * S, C), F32),
        scratch_shapes=[pltpu.VMEM((SUBLANES, S + CONV_TAPS_PAD, tc), F32)],
        compiler_params=_cp(("parallel", "parallel")),
    )(h1, h1, w_dw, b_dw)


def _conv_bwd(dd, h1, w_dw, *, B, S, name):
    C = w_dw.shape[1]
    taps = CONV_TAPS_PAD - 1
    tc = LANES
    ch = _tile(S, 128)
    per, col_a = _conv_cols(C, tc)

    def body(d_ref, a_ref, g_ref, w_ref, du_ref, dw_ref, db_ref, p_ref, q_ref):
        b = pl.program_id(1)

        @pl.when(b == 0)
        def _():
            dw_ref[...] = jnp.zeros_like(dw_ref)
            db_ref[...] = jnp.zeros_like(db_ref)

        _glu_shifted(a_ref, g_ref, p_ref, S)
        d = d_ref[...]
        rows = lax.broadcasted_iota(jnp.int32, (SUBLANES, tc), 0)
        for r in range(SUBLANES):
            q_ref[r, S:S + CONV_TAPS_PAD, :] = jnp.zeros((CONV_TAPS_PAD, tc), F32)
            if r == 0:
                q_ref[r, 0:S, :] = d
            else:
                rolled = pltpu.roll(d, S - r, 0)
                q_ref[r, 0:S, :] = rolled
                q_ref[r, S - SUBLANES:S, :] = jnp.where(rows < SUBLANES - r, rolled[S - SUBLANES:S], 0.0)
        db_ref[...] += _fold8(d)

        def chunk(ci, carry):
            base = pl.multiple_of(ci * ch, ch)
            dch = d_ref[pl.ds(base, ch), :]
            acc = jnp.zeros((ch, tc), F32)
            for k in range(taps):
                q, r = divmod(taps - 1 - k, SUBLANES)
                up = pl.multiple_of(base + SUBLANES * q, SUBLANES)
                acc = acc + w_ref[pl.ds(k, 1), :] * q_ref[r, pl.ds(up, ch), :]
                down = pl.multiple_of(base + (CONV_TAPS_PAD - SUBLANES * q), SUBLANES)
                dw_ref[k] += _fold8(dch * p_ref[r, pl.ds(down, ch), :])
            du_ref[pl.ds(base, ch), :] = acc
            return carry

        lax.fori_loop(0, S // ch, chunk, 0)

    return pl.pallas_call(
        body, name=name, grid=(C // tc, B),
        in_specs=[pl.BlockSpec((S, tc), lambda j, b: (b, j)),
                  pl.BlockSpec((S, tc), lambda j, b: (b, col_a(j))),
                  pl.BlockSpec((S, tc), lambda j, b: (b, col_a(j) + per)),
                  pl.BlockSpec((CONV_TAPS_PAD, tc), lambda j, b: (0, j))],
        out_specs=[pl.BlockSpec((S, tc), lambda j, b: (b, j)),
                   pl.BlockSpec((CONV_TAPS_PAD, SUBLANES, tc), lambda j, b: (0, 0, j)),
                   pl.BlockSpec((SUBLANES, tc), lambda j, b: (0, j))],
        out_shape=[jax.ShapeDtypeStruct((B * S, C), F32),
                   jax.ShapeDtypeStruct((CONV_TAPS_PAD, SUBLANES, C), F32),
                   jax.ShapeDtypeStruct((SUBLANES, C), F32)],
        scratch_shapes=[pltpu.VMEM((SUBLANES, S + CONV_TAPS_PAD, tc), F32),
                        pltpu.VMEM((SUBLANES, S + CONV_TAPS_PAD, tc), F32)],
        compiler_params=_cp(("parallel", "arbitrary")),
    )(dd, h1, h1, w_dw)


def _ln_silu_bwd(dzb, w, xh, rstd, g, b, *, name):
    T, D = dzb.shape
    C = w.shape[1]
    tm = _tile(T, 512)

    def body(dz_ref, w_ref, xh_ref, rs_ref, g_ref, b_ref, dv_ref, dg_ref, db_ref):
        @pl.when(pl.program_id(0) == 0)
        def _():
            dg_ref[...] = jnp.zeros_like(dg_ref)
            db_ref[...] = jnp.zeros_like(db_ref)

        ds = lax.dot_general(dz_ref[...].astype(_MXU), w_ref[...].astype(_MXU), (((1,), (1,)), ((), ())),
                             preferred_element_type=F32)
        xh = xh_ref[...].astype(F32)
        gam = g_ref[...]
        y = xh * gam + b_ref[...]
        sig = _sigmoid(y)
        dln = ds * (sig * (1.0 + y * (1.0 - sig)))
        dv_ref[...] = _ln_bwd_rows(dln, xh, rs_ref[...], gam)
        dg_ref[...] += _fold8(dln * xh)
        db_ref[...] += _fold8(dln)

    row = lambda i: (i, 0)
    fixed = lambda i: (0, 0)
    vec = pl.BlockSpec((1, C), fixed)
    part = pl.BlockSpec((SUBLANES, C), fixed)
    return pl.pallas_call(
        body, name=name, grid=(T // tm,),
        in_specs=[pl.BlockSpec((tm, D), row), _resident((None, C, D), lambda i: (0, 0, 0)),
                  pl.BlockSpec((tm, C), row), pl.BlockSpec((tm, 1), row), vec, vec],
        out_specs=[pl.BlockSpec((tm, C), row), part, part],
        out_shape=[jax.ShapeDtypeStruct((T, C), F32)] + [jax.ShapeDtypeStruct((SUBLANES, C), F32)] * 2,
        compiler_params=_cp(("arbitrary",)),
    )(dzb, w, xh, rstd, g, b)


def _glu_bwd(du, h1, *, name):
    T, C = du.shape
    il = C // 2
    tm = _tile(T, 512)

    def body(du_ref, h_ref, dh_ref, cs_ref):
        @pl.when(pl.program_id(0) == 0)
        def _():
            cs_ref[...] = jnp.zeros_like(cs_ref)

        for hb in range(2):
            a = h_ref[:, 2 * hb * il:(2 * hb + 1) * il].astype(F32)
            gate = h_ref[:, (2 * hb + 1) * il:(2 * hb + 2) * il].astype(F32)
            d = du_ref[:, hb * il:(hb + 1) * il]
            sig = _sigmoid(gate)
            da = d * sig
            dgate = d * a * sig * (1.0 - sig)
            dh_ref[:, 2 * hb * il:(2 * hb + 1) * il] = da.astype(_MXU)
            dh_ref[:, (2 * hb + 1) * il:(2 * hb + 2) * il] = dgate.astype(_MXU)
            cs_ref[:, 2 * hb * il:(2 * hb + 1) * il] += _fold8(da)
            cs_ref[:, (2 * hb + 1) * il:(2 * hb + 2) * il] += _fold8(dgate)

    row = lambda i: (i, 0)
    return pl.pallas_call(
        body, name=name, grid=(T // tm,),
        in_specs=[pl.BlockSpec((tm, C), row), pl.BlockSpec((tm, 2 * C), row)],
        out_specs=[pl.BlockSpec((tm, 2 * C), row), pl.BlockSpec((SUBLANES, 2 * C), lambda i: (0, 0))],
        out_shape=[jax.ShapeDtypeStruct((T, 2 * C), _MXU), jax.ShapeDtypeStruct((SUBLANES, 2 * C), F32)],
        compiler_params=_cp(("arbitrary",)),
    )(du, h1)


def _tril_mask(n):
    return lax.broadcasted_iota(jnp.int32, (n, n), 0) >= lax.broadcasted_iota(jnp.int32, (n, n), 1)


def _split_uv(t, il):
    u = jnp.concatenate([t[:, 0:il], t[:, 2 * il:3 * il]], axis=1)
    v = jnp.concatenate([t[:, il:2 * il], t[:, 3 * il:4 * il]], axis=1)
    return u, v


def _gmlp_gate_fwd(p, g, b, w_s, bsb, w_out, bias, res, alpha, g1, b1, *, name):
    T, C2 = p.shape
    C = C2 // 2
    D = w_out.shape[-1]
    il = C // 2
    G, L, _ = w_s.shape
    assert G * L == C
    tm = _tile(T, 4 * L, L)

    def body(p_ref, g_ref, b_ref, ws_ref, bs_ref, wo_ref, bias_ref, res_ref, g1_ref, b1_ref,
             us_ref, xh_ref, rs_ref, y_ref, yb_ref, xh1_ref, rs1_ref, vn_ref, u_ref):
        z, _ = _gelu_parts(p_ref[...].astype(F32))
        u, v = _split_uv(z, il)
        vn, xh, rstd = _ln_rows(v, g_ref[...], b_ref[...])
        xh_ref[...] = xh.astype(_XDT)
        rs_ref[...] = rstd
        vn_ref[...] = vn.astype(_MXU)
        u_ref[...] = u
        mask = _tril_mask(L)
        for gi in range(G):
            wc = jnp.where(mask, ws_ref[gi], 0.0).astype(_MXU)
            cols = slice(gi * L, (gi + 1) * L)
            for c in range(tm // L):
                rows = slice(c * L, (c + 1) * L)
                s = jnp.dot(wc, vn_ref[rows, cols], preferred_element_type=F32) + bs_ref[:, cols]
                us_ref[rows, cols] = (u_ref[rows, cols] * s).astype(_MXU)
        _out_ln(us_ref[...], wo_ref, bias_ref, res_ref, alpha, g1_ref, b1_ref, y_ref, yb_ref, xh1_ref, rs1_ref)

    row = lambda i: (i, 0)
    fixed = lambda i: (0, 0)
    vd, td, one = pl.BlockSpec((1, D), fixed), pl.BlockSpec((tm, D), row), pl.BlockSpec((tm, 1), row)
    return pl.pallas_call(
        body, name=name, grid=(T // tm,),
        in_specs=[pl.BlockSpec((tm, C2), row), pl.BlockSpec((1, C), fixed), pl.BlockSpec((1, C), fixed),
                  pl.BlockSpec((G, L, L), lambda i: (0, 0, 0)), pl.BlockSpec((L, C), fixed),
                  _resident((None, C, D), lambda i: (0, 0, 0)), vd, td, vd, vd],
        out_specs=[pl.BlockSpec((tm, C), row), pl.BlockSpec((tm, C), row), one, td, td, td, one],
        out_shape=[jax.ShapeDtypeStruct((T, C), _MXU), jax.ShapeDtypeStruct((T, C), _XDT),
                   jax.ShapeDtypeStruct((T, 1), F32), jax.ShapeDtypeStruct((T, D), F32),
                   jax.ShapeDtypeStruct((T, D), _MXU), jax.ShapeDtypeStruct((T, D), _XDT),
                   jax.ShapeDtypeStruct((T, 1), F32)],
        scratch_shapes=[pltpu.VMEM((tm, C), _MXU), pltpu.VMEM((tm, C), F32)],
        compiler_params=_cp(("parallel",)),
    )(p, g, b, w_s, bsb, w_out, bias, res, g1, b1)


def _gmlp_gate_bwd(dzb, w_out, p, xh, rstd, g, b, w_s, bsb, *, name):
    T, C2 = p.shape
    D = dzb.shape[1]
    C = C2 // 2
    il = C // 2
    G, L, _ = w_s.shape
    tm = _tile(T, 4 * L, L)

    def body(dz_ref, wo_ref, p_ref, xh_ref, rs_ref, g_ref, b_ref, ws_ref, bs_ref,
             dp_ref, dg_ref, db_ref, cs_ref, dws_ref, dbs_ref, vn_ref, u_ref, dvn_ref, du_ref, dus_ref):
        @pl.when(pl.program_id(0) == 0)
        def _():
            dg_ref[...] = jnp.zeros_like(dg_ref)
            db_ref[...] = jnp.zeros_like(db_ref)
            cs_ref[...] = jnp.zeros_like(cs_ref)
            dws_ref[...] = jnp.zeros_like(dws_ref)
            dbs_ref[...] = jnp.zeros_like(dbs_ref)

        dus_ref[...] = lax.dot_general(dz_ref[...].astype(_MXU), wo_ref[...].astype(_MXU), (((1,), (1,)), ((), ())),
                                       preferred_element_type=F32)
        z, gp = _gelu_parts(p_ref[...].astype(F32))
        u, _ = _split_uv(z, il)
        xh = xh_ref[...].astype(F32)
        gam = g_ref[...]
        vn_ref[...] = (xh * gam + b_ref[...]).astype(_MXU)
        u_ref[...] = u
        mask = _tril_mask(L)
        for gi in range(G):
            wc = jnp.where(mask, ws_ref[gi], 0.0).astype(_MXU)
            cols = slice(gi * L, (gi + 1) * L)
            for c in range(tm // L):
                rows = slice(c * L, (c + 1) * L)
                vnb = vn_ref[rows, cols]
                s = jnp.dot(wc, vnb, preferred_element_type=F32) + bs_ref[:, cols]
                d = dus_ref[rows, cols]
                du_ref[rows, cols] = d * s
                ds = d * u_ref[rows, cols]
                dbs_ref[:, cols] += ds
                dsb = ds.astype(_MXU)
                dw = lax.dot_general(dsb, vnb, (((1,), (1,)), ((), ())), preferred_element_type=F32)
                dws_ref[gi] += jnp.where(mask, dw, 0.0)
                dvn_ref[rows, cols] = lax.dot_general(wc, dsb, (((0,), (0,)), ((), ())), preferred_element_type=F32)
        dvn = dvn_ref[...]
        dg_ref[...] += _fold8(dvn * xh)
        db_ref[...] += _fold8(dvn)
        dv = _ln_bwd_rows(dvn, xh, rs_ref[...], gam)
        du = du_ref[...]
        for hb in range(2):
            for part, src in ((0, du), (1, dv)):
                lo = (2 * hb + part) * il
                dp = src[:, hb * il:(hb + 1) * il] * gp[:, lo:lo + il]
                dp_ref[:, lo:lo + il] = dp.astype(_MXU)
                cs_ref[:, lo:lo + il] += _fold8(dp)

    row = lambda i: (i, 0)
    fixed = lambda i: (0, 0)
    part_c = pl.BlockSpec((SUBLANES, C), fixed)
    return pl.pallas_call(
        body, name=name, grid=(T // tm,),
        in_specs=[pl.BlockSpec((tm, D), row), _resident((None, C, D), lambda i: (0, 0, 0)),
                  pl.BlockSpec((tm, C2), row), pl.BlockSpec((tm, C), row),
                  pl.BlockSpec((tm, 1), row), pl.BlockSpec((1, C), fixed), pl.BlockSpec((1, C), fixed),
                  pl.BlockSpec((G, L, L), lambda i: (0, 0, 0)), pl.BlockSpec((L, C), fixed)],
        out_specs=[pl.BlockSpec((tm, C2), row), part_c, part_c, pl.BlockSpec((SUBLANES, C2), fixed),
                   pl.BlockSpec((G, L, L), lambda i: (0, 0, 0)), pl.BlockSpec((L, C), fixed)],
        out_shape=[jax.ShapeDtypeStruct((T, C2), _MXU), jax.ShapeDtypeStruct((SUBLANES, C), F32),
                   jax.ShapeDtypeStruct((SUBLANES, C), F32), jax.ShapeDtypeStruct((SUBLANES, C2), F32),
                   jax.ShapeDtypeStruct((G, L, L), F32), jax.ShapeDtypeStruct((L, C), F32)],
        scratch_shapes=[pltpu.VMEM((tm, C), _MXU), pltpu.VMEM((tm, C), F32), pltpu.VMEM((tm, C), F32),
                        pltpu.VMEM((tm, C), F32), pltpu.VMEM((tm, C), F32)],
        compiler_params=_cp(("arbitrary",)),
    )(dzb, w_out, p, xh, rstd, g, b, w_s, bsb)


def _ffn_conv(h, prev8, w_ref, b_ref):
    h1 = _shift_down(prev8, h, 1)
    h2 = _shift_down(prev8, h, 2)
    return w_ref[pl.ds(2, 1), :] * h + w_ref[pl.ds(1, 1), :] * h1 + w_ref[pl.ds(0, 1), :] * h2 + b_ref[...]


def _resident(block, imap):
    return pl.BlockSpec(block, imap, pipeline_mode=pl.Buffered(1))


def _ffn_fwd_half(j, xb, w_up, w_down, b_up, w_dw, b_dw, *, S, name, prev=None, tail=None, head=None):
    T, D = xb.shape
    N = w_up.shape[-1]
    tn = N // N_CHIPS
    tm = _tile(S, 256)
    spt = S // tm
    last = prev is not None
    alpha = tail[1] if last else None

    def body(*refs):
        x_ref, wu_ref, wd_ref, bu_ref, wc_ref, bc_ref = refs[:6]
        if last:
            yp_ref, res_ref, bd_ref, g_ref, b_ref = refs[9:14]
            o = 14 if head is None else 15
            h_ref, hc_ref, f_ref, y_ref, yb_ref, xh_ref, rs_ref = refs[o:o + 7]
            carry_ref = refs[-1]
        else:
            h_ref, hc_ref, f_ref, yp_ref, carry_ref = refs[6:11]

        @pl.when(pl.program_id(0) % spt == 0)
        def _():
            carry_ref[...] = jnp.zeros_like(carry_ref)

        h = jnp.dot(x_ref[...].astype(_MXU), wu_ref[...].astype(_MXU), preferred_element_type=F32) + bu_ref[...]
        h_ref[...] = h.astype(_HDT)
        hc = _ffn_conv(h, carry_ref[...], wc_ref, bc_ref)
        hc_ref[...] = hc.astype(_HDT)
        carry_ref[...] = h[tm - SUBLANES:tm]
        gte = hc[:, :tn]
        f = (gte * _sigmoid(gte) * hc[:, tn:]).astype(_MXU)
        f_ref[...] = f
        y = jnp.dot(f, wd_ref[...].astype(_MXU), preferred_element_type=F32)
        if not last:
            yp_ref[...] = y
            return
        z = y + yp_ref[...] + bd_ref[...] + alpha * res_ref[...]
        out, xh, rstd = _ln_rows(z, g_ref[...], b_ref[...])
        if head is None:
            y_ref[...] = out
            yb_ref[...] = out.astype(_MXU)
            xh_ref[...] = xh.astype(_XDT)
            rs_ref[...] = rstd
            return
        t_ref, cs_ref, ls_ref = refs[14], refs[o + 7], refs[o + 8]

        @pl.when(pl.program_id(0) == 0)
        def _():
            for acc in (xh_ref, rs_ref, cs_ref, ls_ref):
                acc[...] = jnp.zeros_like(acc)

        err = out - t_ref[...]
        d = err * (1.0 / D)
        dz = _ln_bwd_rows(d, xh, rstd, g_ref[...])
        y_ref[...] = dz
        yb_ref[...] = dz.astype(_MXU)
        xh_ref[...] += _fold8(d * xh)
        rs_ref[...] += _fold8(d)
        cs_ref[...] += _fold8(dz)
        ls_ref[...] += _fold8(err * err)

    row = lambda i: (i, 0)
    pair = lambda i: (0, j)
    vec = pl.BlockSpec((1, D), lambda i: (0, 0))
    tile = pl.BlockSpec((tm, D), row)
    in_specs = [tile, _resident((None, D, 2 * tn), lambda i: (0, 0, j)), _resident((None, tn, D), lambda i: (0, j, 0)),
                pl.BlockSpec((1, 2 * tn), pair), pl.BlockSpec((SUBLANES, 2 * tn), pair), pl.BlockSpec((1, 2 * tn), pair)]
    operands = [xb, w_up, w_down, b_up, w_dw, b_dw]
    wide = pl.BlockSpec((tm, 2 * tn), lambda i: (i, j))
    out_specs = [wide, wide, pl.BlockSpec((tm, tn), lambda i: (i, j))]
    out_shape = [jax.ShapeDtypeStruct((T, N), _HDT), jax.ShapeDtypeStruct((T, N), _HDT),
                 jax.ShapeDtypeStruct((T, N // 2), _MXU)]
    aliases = {}
    if last:
        res, _, b_down, g, b = tail
        in_specs += [ANY, ANY, ANY, tile, tile, vec, vec, vec]
        operands += list(prev) + [res, b_down, g, b]
        aliases = {6: 0, 7: 1, 8: 2}
        if head is None:
            out_specs += [tile, tile, tile, pl.BlockSpec((tm, 1), row)]
            out_shape += [jax.ShapeDtypeStruct((T, D), F32), jax.ShapeDtypeStruct((T, D), _MXU),
                          jax.ShapeDtypeStruct((T, D), _XDT), jax.ShapeDtypeStruct((T, 1), F32)]
        else:
            in_specs.append(tile)
            operands.append(head)
            part = pl.BlockSpec((SUBLANES, D), lambda i: (0, 0))
            out_specs += [tile, tile, part, part, part, part]
            out_shape += [jax.ShapeDtypeStruct((T, D), F32), jax.ShapeDtypeStruct((T, D), _MXU)] \
                + [jax.ShapeDtypeStruct((SUBLANES, D), F32)] * 4
    else:
        out_specs.append(tile)
        out_shape.append(jax.ShapeDtypeStruct((T, D), F32))
    return pl.pallas_call(
        body, name=name, grid=(T // tm,), in_specs=in_specs, out_specs=out_specs, out_shape=out_shape,
        input_output_aliases=aliases, scratch_shapes=[pltpu.VMEM((SUBLANES, 2 * tn), F32)],
        compiler_params=_cp(("arbitrary",)),
    )(*operands)


def _ffn_bwd_half(j, dzb, w_down, w_up, hs, hcs, w_dw, *, S, name, dz=None, alpha=None, prev=None, ln=None):
    T, D = dzb.shape
    N = hs.shape[1]
    tn = N // N_CHIPS
    tm = _tile(S, 256)
    spt = S // tm
    nt = T // tm
    last = prev is not None

    def body(*refs):
        dz_ref, wd_ref, wu_ref, h_ref, hc_ref, wc_ref = refs[:6]
        if last:
            dxp_ref, xh_ref, rs_ref, g_ref = refs[7:11]
            dh_ref, cs_ref, dw_ref, db_ref, dz1_ref, dz1b_ref, dg1_ref, db1_ref, cs1_ref, carry_ref = refs[11:21]
        else:
            dzf_ref = refs[6]
            dh_ref, cs_ref, dw_ref, db_ref, dxp_ref, carry_ref = refs[7:13]
        i = pl.program_id(0)
        ii = nt - 1 - i

        @pl.when(i == 0)
        def _():
            cs_ref[...] = jnp.zeros_like(cs_ref)
            dw_ref[...] = jnp.zeros_like(dw_ref)
            db_ref[...] = jnp.zeros_like(db_ref)
            if last:
                dg1_ref[...] = jnp.zeros_like(dg1_ref)
                db1_ref[...] = jnp.zeros_like(db1_ref)
                cs1_ref[...] = jnp.zeros_like(cs1_ref)

        df = lax.dot_general(dz_ref[...].astype(_MXU), wd_ref[...].astype(_MXU), (((1,), (1,)), ((), ())),
                             preferred_element_type=F32)
        h = h_ref[...].astype(F32)
        gte, val = hc_ref[:, :tn].astype(F32), hc_ref[:, tn:].astype(F32)
        sig = _sigmoid(gte)
        dval = df * (gte * sig)
        dg = df * val * (sig * (1.0 + gte * (1.0 - sig)))
        dhc = jnp.concatenate([dg, dval], axis=1)
        nxt = jnp.where((ii + 1) % spt == 0, 0.0, carry_ref[...])
        d1 = _shift_up(dhc, nxt, 1)
        d2 = _shift_up(dhc, nxt, 2)
        carry_ref[...] = dhc[0:SUBLANES]
        db_ref[...] += _fold8(dhc)
        dw_ref[2] += _fold8(dhc * h)
        dw_ref[1] += _fold8(d1 * h)
        dw_ref[0] += _fold8(d2 * h)
        dh = wc_ref[pl.ds(2, 1), :] * dhc + wc_ref[pl.ds(1, 1), :] * d1 + wc_ref[pl.ds(0, 1), :] * d2
        cs_ref[...] += _fold8(dh)
        dhb = dh.astype(_MXU)
        dh_ref[...] = dhb
        dx = lax.dot_general(dhb, wu_ref[...].astype(_MXU), (((1,), (1,)), ((), ())), preferred_element_type=F32)
        if not last:
            dxp_ref[...] = dx + alpha * dzf_ref[...]
            return
        d = dx + dxp_ref[...]
        xh = xh_ref[...].astype(F32)
        dz1 = _ln_bwd_rows(d, xh, rs_ref[...], g_ref[...])
        dz1_ref[...] = dz1
        dz1b_ref[...] = dz1.astype(_MXU)
        dg1_ref[...] += _fold8(d * xh)
        db1_ref[...] += _fold8(d)
        cs1_ref[...] += _fold8(dz1)

    rev = lambda i: (nt - 1 - i, 0)
    fixed = lambda i: (0, 0)
    pair = lambda i: (0, j)
    tile = pl.BlockSpec((tm, D), rev)
    wide = pl.BlockSpec((tm, 2 * tn), lambda i: (nt - 1 - i, j))
    part = pl.BlockSpec((SUBLANES, 2 * tn), fixed)
    in_specs = [tile, _resident((None, tn, D), lambda i: (0, j, 0)), _resident((None, D, 2 * tn), lambda i: (0, 0, j)),
                wide, wide, pl.BlockSpec((SUBLANES, 2 * tn), pair)]
    operands = [dzb, w_down, w_up, hs, hcs, w_dw]
    out_specs = [wide, part, pl.BlockSpec((3, SUBLANES, 2 * tn), lambda i: (0, 0, 0)), part]
    out_shape = [jax.ShapeDtypeStruct((T, N), _MXU), jax.ShapeDtypeStruct((SUBLANES, 2 * tn), F32),
                 jax.ShapeDtypeStruct((3, SUBLANES, 2 * tn), F32), jax.ShapeDtypeStruct((SUBLANES, 2 * tn), F32)]
    aliases = {}
    if last:
        xh, rstd, g = ln
        in_specs += [ANY, tile, tile, pl.BlockSpec((tm, 1), rev), pl.BlockSpec((1, D), fixed)]
        operands += [prev[0], prev[1], xh, rstd, g]
        aliases = {6: 0}
        out_specs += [tile, tile] + [pl.BlockSpec((SUBLANES, D), fixed)] * 3
        out_shape += [jax.ShapeDtypeStruct((T, D), F32), jax.ShapeDtypeStruct((T, D), _MXU)] \
            + [jax.ShapeDtypeStruct((SUBLANES, D), F32)] * 3
    else:
        in_specs.append(tile)
        operands.append(dz)
        out_specs.append(tile)
        out_shape.append(jax.ShapeDtypeStruct((T, D), F32))
    return pl.pallas_call(
        body, name=name, grid=(nt,), in_specs=in_specs, out_specs=out_specs, out_shape=out_shape,
        input_output_aliases=aliases, scratch_shapes=[pltpu.VMEM((SUBLANES, 2 * tn), F32)],
        compiler_params=_cp(("arbitrary",)),
    )(*operands)


def _sum_pieces(gs, rs, me, *, name):
    n = len(gs)
    _, pr, pc = gs[0].shape
    tr = _tile(pr, 128)

    def body(me_ref, *refs):
        o_ref = refs[2 * n]
        for l in range(n):
            total = refs[l][...].astype(F32)
            for s in range(N_DEV - 1):
                total = total + refs[n + l][s].astype(F32)
            o_ref[l] = total

    own = pl.BlockSpec((None, tr, pc), lambda i, me_ref: (me_ref[0], i, 0))
    got = pl.BlockSpec((N_DEV - 1, tr, pc), lambda i, me_ref: (0, i, 0))
    return pl.pallas_call(
        body, name=name,
        grid_spec=pltpu.PrefetchScalarGridSpec(
            num_scalar_prefetch=1, grid=(pr // tr,), in_specs=[own] * n + [got] * n,
            out_specs=pl.BlockSpec((n, tr, pc), lambda i, me_ref: (0, i, 0))),
        out_shape=jax.ShapeDtypeStruct((n, pr, pc), F32),
        compiler_params=_cp(("parallel",)),
    )(me, *gs, *rs)


def _adam_math(w, g, m, v):
    bc1 = 1.0 - ADAM_B1 ** ADAM_STEP
    bc2 = 1.0 - ADAM_B2 ** ADAM_STEP
    m = ADAM_B1 * m + (1.0 - ADAM_B1) * g
    v = ADAM_B2 * v + (1.0 - ADAM_B2) * (g * g)
    return -ADAM_LR * ((m / bc1) / (jnp.sqrt(v / bc2) + ADAM_EPS) + ADAM_WD * w), m, v


def _adam(w, g, m, v, *, name):
    R, C = w.shape
    tr = _tile(R, 256)

    def body(w_ref, g_ref, m_ref, v_ref, d_ref, mo_ref, vo_ref):
        d_ref[...], mo_ref[...], vo_ref[...] = _adam_math(w_ref[...], g_ref[...], m_ref[...], v_ref[...])

    spec = pl.BlockSpec((tr, C), lambda i: (i, 0))
    return pl.pallas_call(
        body, name=name, grid=(R // tr,), in_specs=[spec] * 4, out_specs=[spec] * 3,
        out_shape=[jax.ShapeDtypeStruct((R, C), F32)] * 3,
        compiler_params=_cp(("parallel",)),
    )(w, g, m, v)


def _adam_halves(w, own, got, m, v, core, *, name):
    L, R, C = w.shape
    rh = R // 2
    tr = _tile(rh, 256)
    nt = rh // tr

    def body(c_ref, w_ref, own_ref, got_ref, m_ref, v_ref, g_ref, d_ref, mo_ref, vo_ref):
        g = jnp.where(pl.program_id(1) == c_ref[0], own_ref[...], got_ref[...])
        g_ref[...] = g
        d_ref[...], mo_ref[...], vo_ref[...] = _adam_math(w_ref[...], g, m_ref[...], v_ref[...])

    full = pl.BlockSpec((None, tr, C), lambda l, h, t, c_ref: (l, h * nt + t, 0))
    half = pl.BlockSpec((None, tr, C), lambda l, h, t, c_ref: (l, t, 0))
    return pl.pallas_call(
        body, name=name,
        grid_spec=pltpu.PrefetchScalarGridSpec(
            num_scalar_prefetch=1, grid=(L, 2, nt), in_specs=[full, half, half, full, full], out_specs=[full] * 4),
        out_shape=[jax.ShapeDtypeStruct((L, R, C), F32)] * 4,
        compiler_params=_cp(("parallel", "parallel", "parallel")),
    )(core, w, own, got, m, v)


def _remote(src, dst, send, recv, dev):
    return pltpu.make_async_remote_copy(src_ref=src, dst_ref=dst, send_sem=send, recv_sem=recv,
                                        device_id=dev, device_id_type=MESH)


def _place_w(shard, pos, layer, *, axis, name):
    _, R, C = shard.shape
    tr = _tile(R, 512, 16)
    nt = R // tr
    if axis == 2:
        out_shape = (1, R, N_CHIPS * C)
        out_map = lambda t, q: (0, t, q[0])
    else:
        out_shape = (1, N_CHIPS * R, C)
        out_map = lambda t, q: (0, q[0] * nt + t, 0)

    def body(q_ref, s_ref, o_ref):
        o_ref[...] = s_ref[...].astype(_WIRE)

    return pl.pallas_call(
        body, name=name,
        grid_spec=pltpu.PrefetchScalarGridSpec(
            num_scalar_prefetch=1, grid=(nt,),
            in_specs=[pl.BlockSpec((None, tr, C), lambda t, q: (layer, t, 0))],
            out_specs=pl.BlockSpec((None, tr, C), out_map)),
        out_shape=jax.ShapeDtypeStruct(out_shape, _WIRE),
        compiler_params=_cp(("parallel",)),
    )(pos, shard)


def _ag_window(ref, kind, px, py, h):
    axis, perm = kind
    q = 2 * px + py
    if perm:
        q = _perm_idx(q)
    if axis == 2:
        R, C = ref.shape[1], ref.shape[2] // N_CHIPS
        rh = R // 2
        return ref.at[:, pl.ds(pl.multiple_of(h * rh, 16), rh), pl.ds(pl.multiple_of(q * C, LANES), C)]
    R = ref.shape[1] // N_CHIPS
    rh = R // 2
    return ref.at[:, pl.ds(pl.multiple_of(q * R + h * rh, 16), rh), :]


def _ag_ici_copies(refs, kinds, send, recv):
    x, y, c = lax.axis_index("x"), lax.axis_index("y"), lax.axis_index("c")
    chips = [(1 - x, y), (x, 1 - y), (1 - x, 1 - y)]
    sends, recvs = [], []
    for a, (ref, kind) in enumerate(zip(refs, kinds)):
        own = _ag_window(ref, kind, x, y, c)
        for i, (px, py) in enumerate(chips):
            k = 3 * a + i
            sends.append(_remote(own, own, send.at[k], recv.at[k], (px, py, c)))
            recvs.append(_remote(own, _ag_window(ref, kind, px, py, c), send.at[k], recv.at[k], (px, py, c)))
    return sends, recvs


def _ag_start(arrs, kinds, after, *, name, copies=_ag_ici_copies):
    n = len(arrs)

    def body(*refs):
        in_refs = refs[:n]
        send, recv = refs[n + len(after)], refs[n + len(after) + 1]
        token = refs[-1]
        sends, _ = copies(in_refs, kinds, send, recv)
        for cp in sends:
            cp.start()
        token[...] = jnp.zeros_like(token)

    sems = pltpu.SemaphoreType.DMA((3 * n,))
    out = pl.pallas_call(
        body, name=name,
        out_shape=(sems, sems) + tuple(pltpu.HBM(a.shape, a.dtype) for a in arrs)
        + (jax.ShapeDtypeStruct((SUBLANES, LANES), F32),),
        in_specs=(HBM,) * n + (ANY,) * len(after),
        out_specs=(SEMS, SEMS) + (HBM,) * n + (pl.BlockSpec(memory_space=pltpu.VMEM),),
        input_output_aliases={a: 2 + a for a in range(n)},
        compiler_params=pltpu.CompilerParams(has_side_effects=EFFECT),
    )(*[pltpu.with_memory_space_constraint(a, pltpu.HBM) for a in arrs], *after)
    return out[0], out[1], list(out[2:2 + n]), out[-1]


def _ag_wait(send, recv, arrs, kinds, after, *, name, copies=_ag_ici_copies):
    n = len(arrs)

    def body(*refs):
        in_refs = refs[:n]
        send, recv = refs[n], refs[n + 1]
        sends, recvs = copies(in_refs, kinds, send, recv)
        for cp in sends:
            cp.wait_send()
        for cp in recvs:
            cp.wait_recv()

    out = pl.pallas_call(
        body, name=name,
        out_shape=tuple(pltpu.HBM(a.shape, a.dtype) for a in arrs),
        in_specs=(HBM,) * n + (SEMS, SEMS) + (ANY,) * len(after), out_specs=(HBM,) * n,
        input_output_aliases={a: a for a in range(n)},
        compiler_params=pltpu.CompilerParams(has_side_effects=EFFECT),
    )(*arrs, send, recv, *after)
    return list(out)


def _ag_d2d_copies(refs, kinds, send, recv):
    x, y, c = lax.axis_index("x"), lax.axis_index("y"), lax.axis_index("c")
    chips = [(1 - x, y), (x, 1 - y), (1 - x, 1 - y)]
    sib = (x, y, 1 - c)
    sends, recvs = [], []
    for a, (ref, kind) in enumerate(zip(refs, kinds)):
        for i, (px, py) in enumerate(chips):
            k = 3 * a + i
            got = _ag_window(ref, kind, px, py, c)
            sends.append(_remote(got, got, send.at[k], recv.at[k], sib))
            recvs.append(_remote(got, _ag_window(ref, kind, px, py, 1 - c), send.at[k], recv.at[k], sib))
    return sends, recvs


def _flip(x, y, c, f):
    return ((1 - x) if f & 4 else x, (1 - y) if f & 2 else y, (1 - c) if f & 1 else c)


def _rs_copies(g_refs, land_refs, send, recv):
    x, y, c = lax.axis_index("x"), lax.axis_index("y"), lax.axis_index("c")
    cps = []
    for a, (g_ref, land_ref) in enumerate(zip(g_refs, land_refs)):
        for f in range(1, N_DEV):
            tx, ty, tcx = _flip(x, y, c, f)
            k = (N_DEV - 1) * a + f - 1
            cps.append(_remote(g_ref.at[4 * tx + 2 * ty + tcx], land_ref.at[f - 1], send.at[k], recv.at[k],
                               (tx, ty, tcx)))
    return cps


def _rs_start(gs, *, name):
    n = len(gs)
    lands = [lax.empty((N_DEV - 1,) + g.shape[1:], g.dtype) for g in gs]

    def body(*refs):
        send, recv, token = refs[2 * n], refs[2 * n + 1], refs[-1]
        for cp in _rs_copies(refs[:n], refs[n:2 * n], send, recv):
            cp.start()
        token[...] = jnp.zeros_like(token)

    sems = pltpu.SemaphoreType.DMA(((N_DEV - 1) * n,))
    thru = [pltpu.HBM(t.shape, t.dtype) for t in gs + lands]
    out = pl.pallas_call(
        body, name=name,
        out_shape=(sems, sems, *thru, jax.ShapeDtypeStruct((SUBLANES, LANES), F32)),
        in_specs=(HBM,) * (2 * n), out_specs=(SEMS, SEMS) + (HBM,) * (2 * n) + (pl.BlockSpec(memory_space=pltpu.VMEM),),
        input_output_aliases={a: 2 + a for a in range(2 * n)},
        compiler_params=pltpu.CompilerParams(has_side_effects=EFFECT),
    )(*[pltpu.with_memory_space_constraint(t, pltpu.HBM) for t in gs + lands])
    return out[0], out[1], list(out[2:2 + n]), list(out[2 + n:2 + 2 * n]), out[-1]


def _rs_wait(send, recv, gs, lands, after, *, name):
    n = len(gs)

    def body(*refs):
        cps = _rs_copies(refs[:n], refs[n:2 * n], refs[2 * n], refs[2 * n + 1])
        for cp in cps:
            cp.wait_send()
        for cp in cps:
            cp.wait_recv()

    out = pl.pallas_call(
        body, name=name,
        out_shape=tuple(pltpu.HBM(t.shape, t.dtype) for t in gs + lands),
        in_specs=(HBM,) * (2 * n) + (SEMS, SEMS, ANY), out_specs=(HBM,) * (2 * n),
        input_output_aliases={a: a for a in range(2 * n)},
        compiler_params=pltpu.CompilerParams(has_side_effects=EFFECT),
    )(*gs, *lands, send, recv, after)
    return list(out[:n]), list(out[n:])


def _pair_exchange(owns, *, name):
    n = len(owns)

    def body(*refs):
        send, recv = refs[2 * n], refs[2 * n + 1]
        x, y, c = lax.axis_index("x"), lax.axis_index("y"), lax.axis_index("c")
        cps = [_remote(refs[a], refs[n + a], send.at[a], recv.at[a], (x, y, 1 - c)) for a in range(n)]
        for cp in cps:
            cp.start()
        for cp in cps:
            cp.wait_recv()
        for cp in cps:
            cp.wait_send()

    return pl.pallas_call(
        body, name=name, in_specs=[ANY] * n, out_specs=[ANY] * n,
        out_shape=[jax.ShapeDtypeStruct(o.shape, o.dtype) for o in owns],
        scratch_shapes=[pltpu.SemaphoreType.DMA((n,)), pltpu.SemaphoreType.DMA((n,))],
    )(*owns)


def _allreduce_flat(vec, *, name):
    n = vec.shape[0]
    unit = N_DEV * SUBLANES * LANES
    npad = -(-n // unit) * unit
    rows = npad // (N_DEV * LANES)
    xin = jnp.pad(vec, (0, npad - n)).reshape(N_DEV, rows, LANES)

    def body(x_ref, y_ref, a_ref, send_a, recv_a, send_b, recv_b):
        x, y, c = lax.axis_index("x"), lax.axis_index("y"), lax.axis_index("c")
        me = 4 * x + 2 * y + c
        a_ref[me] = x_ref[me]
        sends, recvs = [], []
        for f in range(1, N_DEV):
            dev = _flip(x, y, c, f)
            t = 4 * dev[0] + 2 * dev[1] + dev[2]
            cp = _remote(x_ref.at[t], a_ref.at[me], send_a.at[f - 1], recv_a.at[f - 1], dev)
            cp.start()
            sends.append(cp)
            recvs.append(_remote(x_ref.at[me], a_ref.at[t], send_a.at[f - 1], recv_a.at[f - 1], dev))
        for cp in recvs:
            cp.wait_recv()
        for cp in sends:
            cp.wait_send()
        acc = a_ref[0]
        for s in range(1, N_DEV):
            acc = acc + a_ref[s]
        y_ref[me] = acc
        sends, recvs = [], []
        for f in range(1, N_DEV):
            dev = _flip(x, y, c, f)
            t = 4 * dev[0] + 2 * dev[1] + dev[2]
            cp = _remote(y_ref.at[me], y_ref.at[me], send_b.at[f - 1], recv_b.at[f - 1], dev)
            cp.start()
            sends.append(cp)
            recvs.append(_remote(y_ref.at[me], y_ref.at[t], send_b.at[f - 1], recv_b.at[f - 1], dev))
        for cp in recvs:
            cp.wait_recv()
        for cp in sends:
            cp.wait_send()

    vm = pl.BlockSpec(memory_space=pltpu.VMEM)
    out = pl.pallas_call(
        body, name=name, in_specs=[vm], out_specs=vm,
        out_shape=jax.ShapeDtypeStruct((N_DEV, rows, LANES), F32),
        scratch_shapes=[pltpu.VMEM((N_DEV, rows, LANES), F32)] + [pltpu.SemaphoreType.DMA((N_DEV - 1,))] * 4,
        compiler_params=_cp(),
    )(xin)
    return out.reshape(npad)[:n]


def _perm_cols(v, blocks=N_CHIPS):
    w = v.shape[-1] // blocks
    return jnp.concatenate([v[..., q * w:(q + 1) * w] for q in PERM], axis=-1)


def _pack(arrs):
    return jnp.concatenate([a.reshape(-1).astype(F32) for a in arrs])


def _unpack(flat, shapes):
    out, pos = [], 0
    for s in shapes:
        n = 1
        for d in s:
            n *= d
        out.append(flat[pos:pos + n].reshape(s))
        pos += n
    return out


def kernel(x, conv_w_in, conv_b_in, conv_w_dw, conv_b_dw, conv_ln_g, conv_ln_b, conv_w_out, conv_b_out, gmlp_w_in, gmlp_b_in, gmlp_ln_g, gmlp_ln_b, gmlp_w_s, gmlp_b_s, gmlp_w_out, gmlp_b_out, ffn_w_up, ffn_b_up, ffn_w_dw, ffn_b_dw, ffn_w_down, ffn_b_down, norm1_g, norm1_b, norm2_g, norm2_b, loss_target, m_conv_w_in, m_conv_b_in, m_conv_w_dw, m_conv_b_dw, m_conv_ln_g, m_conv_ln_b, m_conv_w_out, m_conv_b_out, m_gmlp_w_in, m_gmlp_b_in, m_gmlp_ln_g, m_gmlp_ln_b, m_gmlp_w_s, m_gmlp_b_s, m_gmlp_w_out, m_gmlp_b_out, m_ffn_w_up, m_ffn_b_up, m_ffn_w_dw, m_ffn_b_dw, m_ffn_w_down, m_ffn_b_down, m_norm1_g, m_norm1_b, m_norm2_g, m_norm2_b, v_conv_w_in, v_conv_b_in, v_conv_w_dw, v_conv_b_dw, v_conv_ln_g, v_conv_ln_b, v_conv_w_out, v_conv_b_out, v_gmlp_w_in, v_gmlp_b_in, v_gmlp_ln_g, v_gmlp_ln_b, v_gmlp_w_s, v_gmlp_b_s, v_gmlp_w_out, v_gmlp_b_out, v_ffn_w_up, v_ffn_b_up, v_ffn_w_dw, v_ffn_b_dw, v_ffn_w_down, v_ffn_b_down, v_norm1_g, v_norm1_b, v_norm2_g, v_norm2_b):
    P = dict(locals())
    WEIGHTS = ['conv_w_in', 'conv_b_in', 'conv_w_dw', 'conv_b_dw', 'conv_ln_g', 'conv_ln_b', 'conv_w_out',
               'conv_b_out', 'gmlp_w_in', 'gmlp_b_in', 'gmlp_ln_g', 'gmlp_ln_b', 'gmlp_w_s', 'gmlp_b_s',
               'gmlp_w_out', 'gmlp_b_out', 'ffn_w_up', 'ffn_b_up', 'ffn_w_dw', 'ffn_b_dw', 'ffn_w_down',
               'ffn_b_down', 'norm1_g', 'norm1_b', 'norm2_g', 'norm2_b']
    BIG = ['conv_w_in', 'conv_w_out', 'gmlp_w_in', 'gmlp_w_out', 'ffn_w_up', 'ffn_w_down']
    SMALL_SHARDED = {'conv_w_dw': 2, 'gmlp_b_in': 1, 'gmlp_ln_g': 1, 'gmlp_ln_b': 1, 'gmlp_b_out': 1, 'ffn_w_dw': 2}

    B, S, D = x.shape
    T = B * S
    depth = norm1_g.shape[0]
    alpha = (2.0 * depth) ** 0.25
    C = conv_w_out.shape[-1]
    F2 = ffn_b_up.shape[-1]
    G, L = gmlp_w_s.shape[1], gmlp_w_s.shape[2]
    xi, yi, ci = lax.axis_index("x"), lax.axis_index("y"), lax.axis_index("c")
    shard = 2 * xi + yi

    i32 = lambda v: jnp.reshape(v, (1,)).astype(jnp.int32)
    pos_plain, pos_perm = i32(shard), i32(_perm_idx(shard))
    me_id, core_id = i32(4 * xi + 2 * yi + ci), i32(ci)

    groups = []
    for i in range(depth):
        mix = 'conv' if i % 2 == 0 else 'gmlp'
        groups.append((f"{mix}{i // 2}", [(mix + '_w_in', i // 2, 2, True), (mix + '_w_out', i // 2, 1, False)]))
        groups.append((f"ffn{i}", [('ffn_w_up', i, 2, True), ('ffn_w_down', i, 1, False)]))
    sm_names = list(SMALL_SHARDED)
    sm_shapes = [P[n].shape for n in sm_names]
    mine = _pack([P[n] for n in sm_names]) * (ci == 0).astype(F32)
    buf = jnp.zeros((N_CHIPS, mine.shape[0]), F32)
    buf = lax.dynamic_update_slice(buf, mine[None], (shard, 0))

    started, order = {}, []
    for gname, members in groups:
        placed = [_place_w(P[n], pos_perm if perm else pos_plain, l, axis=axis, name=f"place_{n}_{l}")
                  for n, l, axis, perm in members]
        kinds = [(axis, perm) for _, _, axis, perm in members]
        send, recv, arrs, token = _ag_start(placed, kinds, order, name=f"ag_start_{gname}")
        order = [token]
        started[gname] = (send, recv, arrs, kinds, [(n, l) for n, l, _, _ in members])
        if len(started) == 1:
            gathered = _allreduce_flat((buf + token[0, 0]).reshape(-1), name="ag_small").reshape(N_CHIPS, -1)
            order = [gathered]
    wts = {}

    def landed_ici(gname, after):
        send, recv, arrs, kinds, keys = started[gname]
        arrs = _ag_wait(send, recv, arrs, kinds, after, name=f"ag_wait_{gname}")
        send, recv, arrs, _ = _ag_start(arrs, kinds, [], name=f"ag_fwd_start_{gname}", copies=_ag_d2d_copies)
        started[gname] = (send, recv, arrs, kinds, keys)

    def arrive(gname, after):
        send, recv, arrs, kinds, keys = started[gname]
        arrs = _ag_wait(send, recv, arrs, kinds, after, name=f"ag_fwd_wait_{gname}", copies=_ag_d2d_copies)
        wts.update(zip(keys, arrs))

    full = {}
    for n, parts in zip(sm_names, zip(*[_unpack(gathered[k], sm_shapes) for k in range(N_CHIPS)])):
        full[n] = jnp.concatenate(parts, axis=SMALL_SHARDED[n])
    for n in WEIGHTS:
        if n not in BIG and n not in full:
            full[n] = P[n]

    assert G * L == C, "a gMLP group must be as wide as a chunk is long"

    def row(v):
        return v.reshape(1, -1)

    def pad_rows(v, r):
        return jnp.pad(v, ((0, r - v.shape[0]), (0, 0)))

    xf = x.reshape(T, D)
    saved = []
    cur, cur_b = xf, xf.astype(_MXU)
    for i in range(depth):
        j = i // 2
        sv = {'x': cur, 'xb': cur_b}
        if i == 0:
            landed_ici(groups[0][0], order)
        arrive(groups[2 * i][0], [] if i == 0 else [cur_b])
        if i % 2 == 0:
            b_in = row(_perm_cols(full['conv_b_in'][j]))
            h1 = _mm(cur_b, wts['conv_w_in', j], bl=0, bias=b_in, tm=_tile(T, 1024), tn=_tile(2 * C, 1024, LANES),
                     tk=D, name=f"conv_in_{j}", n_outer=True, out_dtype=_ADT)
            wdw = pad_rows(full['conv_w_dw'][j], CONV_TAPS_PAD)
            dwo = _conv_fwd(h1, wdw, row(full['conv_b_dw'][j]), B=B, S=S, name=f"conv_dw_{j}")
            landed_ici(groups[2 * i + 1][0], [dwo])
            s_act, xhc, rsc, *y1 = _conv_tail_fwd(
                dwo, row(full['conv_ln_g'][j]), row(full['conv_ln_b'][j]), wts['conv_w_out', j],
                row(full['conv_b_out'][j]), cur, alpha, row(norm1_g[i]), row(norm1_b[i]), name=f"conv_out_ln_{j}")
            sv.update(h1=h1, wdw=wdw, act=s_act, xhc=xhc, rsc=rsc)
        else:
            b_in = row(_perm_cols(full['gmlp_b_in'][j]))
            pre = _mm(cur_b, wts['gmlp_w_in', j], bl=0, bias=b_in, tm=_tile(T, 1024), tn=_tile(2 * C, 1024, LANES),
                      tk=D, name=f"gmlp_in_{j}", n_outer=True, out_dtype=_ADT)
            bsb = jnp.repeat(gmlp_b_s[j].T, L, axis=1)
            landed_ici(groups[2 * i + 1][0], [pre])
            us, xhv, rsv, *y1 = _gmlp_gate_fwd(
                pre, row(full['gmlp_ln_g'][j]), row(full['gmlp_ln_b'][j]), gmlp_w_s[j], bsb, wts['gmlp_w_out', j],
                row(full['gmlp_b_out'][j]), cur, alpha, row(norm1_g[i]), row(norm1_b[i]), name=f"gmlp_gate_{j}")
            sv.update(pre=pre, bsb=bsb, act=us, xhv=xhv, rsv=rsv)
        x1, x1b, xh1, rs1 = y1
        arrive(groups[2 * i + 1][0], [x1b])
        wdw3 = pad_rows(_perm_cols(full['ffn_w_dw'][i]), SUBLANES)
        bdw3 = row(_perm_cols(ffn_b_dw[i]))
        ffn_in = (x1b, wts['ffn_w_up', i], wts['ffn_w_down', i], row(_perm_cols(ffn_b_up[i])), wdw3, bdw3)
        first = _ffn_fwd_half(0, *ffn_in, S=S, name=f"ffn_fwd_a_{i}")
        if i < depth - 1:
            landed_ici(groups[2 * i + 2][0], [first[3]])
        ffn_tail = (x1, alpha, row(ffn_b_down[i]), row(norm2_g[i]), row(norm2_b[i]))
        sv.update(x1=x1, x1b=x1b, xh1=xh1, rs1=rs1, wdw3=wdw3)
        if i < depth - 1:
            hs, hcs, f_act, cur, cur_b, xh2, rs2 = _ffn_fwd_half(1, *ffn_in, S=S, name=f"ffn_fwd_b_{i}", prev=first,
                                                                 tail=ffn_tail)
            sv.update(xh2=xh2, rs2=rs2)
        else:
            hs, hcs, f_act, *sv['head'] = _ffn_fwd_half(1, *ffn_in, S=S, name=f"ffn_fwd_b_{i}", prev=first,
                                                         tail=ffn_tail, head=loss_target.reshape(T, D))
        sv.update(hs=hs, hcs=hcs, f=f_act)
        saved.append(sv)

    sg = {n: [None] * full[n].shape[0] for n in WEIGHTS if n not in BIG}
    inflight = {n: [None] * P[n].shape[0] for n in BIG}
    deps = []
    dcur = None
    loss_part = None
    tk_t = _tile(T, 2048)

    ready = []

    def wgrad(n, l, a_, b_, **kw):
        tk = T if n.endswith('w_in') else tk_t
        ready.append((n, l, _mm(a_, b_, ta=True, out_dtype=_WIRE, tk=tk, name=f"{n}_dw_{l}", deps=deps, **kw)))
        launch(f"{n}_{l}")

    def launch(gname):
        send, recv, gs, lands, token = _rs_start([g for _, _, g in ready], name=f"rs_start_{gname}")
        group = {'name': gname, 'flight': (send, recv, gs, lands), 'landed': None}
        for a, (n, l, _) in enumerate(ready):
            inflight[n][l] = (group, a)
        del ready[:]
        deps.append(token)

    def landed(n, l):
        group, a = inflight[n][l]
        if group['landed'] is None:
            group['landed'] = _rs_wait(*group['flight'], dcur, name=f"rs_wait_{group['name']}")
        return group['landed'][0][a], group['landed'][1][a]

    for i in reversed(range(depth)):
        j = i // 2
        sv = saved[i]
        if i == depth - 1:
            dz2, dz2b, dg, db, cs, loss_part = sv['head']
        else:
            dz2, dz2b, dg, db, cs = dcur
        sg['norm2_g'][i], sg['norm2_b'][i], sg['ffn_b_down'][i] = dg.sum(0), db.sum(0), cs.sum(0)
        Fh = F2 // 2
        wgrad('ffn_w_down', i, sv['f'], dz2b, tm=Fh // 2, tn=_tile(D, 1024, LANES), pieces=('row',))
        ffn_in = (dz2b, wts['ffn_w_down', i], wts['ffn_w_up', i], sv['hs'], sv['hcs'], sv['wdw3'])
        dh0, csu0, dwd0, dbd0, dxp = _ffn_bwd_half(0, *ffn_in, S=S, name=f"ffn_bwd_a_{i}", dz=dz2, alpha=alpha)
        dh, csu1, dwd1, dbd1, dz1, dz1b, dg, db, cs = _ffn_bwd_half(
            1, *ffn_in, S=S, name=f"ffn_bwd_b_{i}", prev=(dh0, dxp), ln=(sv['xh1'], sv['rs1'], row(norm1_g[i])))
        sg['ffn_b_up'][i] = _perm_cols(jnp.concatenate([csu0.sum(0), csu1.sum(0)], axis=-1))
        sg['ffn_w_dw'][i] = _perm_cols(jnp.concatenate([dwd0.sum(1), dwd1.sum(1)], axis=-1))
        sg['ffn_b_dw'][i] = _perm_cols(jnp.concatenate([dbd0.sum(0), dbd1.sum(0)], axis=-1))
        wgrad('ffn_w_up', i, sv['x1b'], dh, tm=D, tn=F2 // N_CHIPS, pieces=('col', True))
        sg['norm1_g'][i], sg['norm1_b'][i] = dg.sum(0), db.sum(0)
        if i % 2 == 0:
            sg['conv_b_out'][j] = cs.sum(0)
            wgrad('conv_w_out', j, sv['act'], dz1b, tm=_tile(C, 1024), tn=_tile(D, 1024, LANES), pieces=('row',))
            ddw, dg, db = _ln_silu_bwd(dz1b, wts['conv_w_out', j], sv['xhc'], sv['rsc'], row(full['conv_ln_g'][j]),
                                       row(full['conv_ln_b'][j]), name=f"conv_ln_bwd_{j}")
            sg['conv_ln_g'][j], sg['conv_ln_b'][j] = dg.sum(0), db.sum(0)
            dglu, dwk, dbk = _conv_bwd(ddw, sv['h1'], sv['wdw'], B=B, S=S, name=f"conv_dw_bwd_{j}")
            sg['conv_w_dw'][j] = dwk.sum(1)[:conv_w_dw.shape[1]]
            sg['conv_b_dw'][j] = dbk.sum(0)
            dh1, csi = _glu_bwd(dglu, sv['h1'], name=f"conv_glu_bwd_{j}")
            sg['conv_b_in'][j] = _perm_cols(csi.sum(0))
            fam = 'conv_w_in'
        else:
            sg['gmlp_b_out'][j] = cs.sum(0)
            wgrad('gmlp_w_out', j, sv['act'], dz1b, tm=_tile(C, 1024), tn=_tile(D, 1024, LANES), pieces=('row',))
            dh1, dg, db, csi, dws, dbs = _gmlp_gate_bwd(dz1b, wts['gmlp_w_out', j], sv['pre'], sv['xhv'], sv['rsv'],
                                                        row(full['gmlp_ln_g'][j]), row(full['gmlp_ln_b'][j]),
                                                        gmlp_w_s[j], sv['bsb'], name=f"gmlp_gate_bwd_{j}")
            sg['gmlp_ln_g'][j], sg['gmlp_ln_b'][j] = dg.sum(0), db.sum(0)
            sg['gmlp_b_in'][j] = _perm_cols(csi.sum(0))
            sg['gmlp_w_s'][j] = dws
            sg['gmlp_b_s'][j] = dbs.reshape(L, G, L).sum(-1).T
            fam = 'gmlp_w_in'
        wgrad(fam, j, sv['xb'], dh1, tm=D, tn=(2 * C) // N_CHIPS, pieces=('col', True))
        if i > 0:
            below = saved[i - 1]
            dcur = _mm_ln_bwd(dh1, wts[fam, j], dz1, alpha, below['xh2'], below['rs2'], row(norm2_g[i - 1]),
                              name=f"{fam}_dx_{j}", deps=deps)
        else:
            dcur = _mm(dh1, wts[fam, j], bl=0, tb=True, res=dz1, res_scale=alpha, tm=_tile(T, 512),
                       tn=_tile(D, 1024, LANES), tk=2 * C, name=f"{fam}_dx_{j}", deps=deps)
    grad_x = dcur.reshape(B, S, D)

    small_names = [n for n in WEIGHTS if n not in BIG]
    small_full = [jnp.stack(sg[n]) for n in small_names]
    flat = _pack(small_full + [loss_part])
    red = _allreduce_flat(flat, name="ar_small")
    red_parts = _unpack(red, [a.shape for a in small_full] + [loss_part.shape])
    loss = (0.5 / D) * jnp.sum(red_parts[-1])
    grads = {}
    for n, g in zip(small_names, red_parts[:-1]):
        if n in SMALL_SHARDED:
            ax = SMALL_SHARDED[n]
            width = P[n].shape[ax]
            g = lax.dynamic_slice_in_dim(g, shard * width, width, axis=ax)
        grads[n] = g

    big_out = {}
    for n in ['ffn_w_down', 'ffn_w_up', 'gmlp_w_out', 'gmlp_w_in', 'conv_w_out', 'conv_w_in']:
        both = [landed(n, l) for l in range(len(inflight[n]))]
        own = _sum_pieces([g for g, _ in both], [r for _, r in both], me_id, name=f"sum_{n}")
        got, = _pair_exchange([own], name=f"px_{n}")
        big_out[n] = _adam_halves(P[n], own, got, P['m_' + n], P['v_' + n], core_id, name=f"adam_{n}")

    shapes = [P[n].shape for n in small_names]
    n_small = sum(functools.reduce(lambda p_, d_: p_ * d_, s_, 1) for s_ in shapes)
    unit = SUBLANES * LANES
    npad = -(-n_small // unit) * unit

    def flat2d(arrs, fill=0.0):
        v = _pack(arrs)
        return jnp.pad(v, (0, npad - n_small), constant_values=fill).reshape(-1, LANES)

    dl, mo, vo = _adam(flat2d([P[n] for n in small_names]), flat2d([grads[n] for n in small_names]),
                       flat2d([P['m_' + n] for n in small_names]),
                       flat2d([P['v_' + n] for n in small_names], fill=1.0), name="adam_small")
    small_out = {n: [grads[n], None, None, None] for n in small_names}
    for k, t in enumerate((dl, mo, vo)):
        for n, a in zip(small_names, _unpack(t.reshape(-1), shapes)):
            small_out[n][k + 1] = a

    outs = [loss, grad_x]
    for k in range(4):
        for n in WEIGHTS:
            outs.append(big_out[n][k] if n in BIG else small_out[n][k])
    return tuple(outs)
```

```python
import functools

import jax
import jax.numpy as jnp
from jax import lax
from jax.experimental import pallas as pl
from jax.experimental.pallas import tpu as pltpu

F32 = jnp.float32
_MXU = jnp.bfloat16
_WIRE = jnp.bfloat16
_HDT = jnp.bfloat16
_ADT = jnp.bfloat16
_XDT = jnp.bfloat16
LN_EPS = 1e-5
ADAM_LR, ADAM_B1, ADAM_B2, ADAM_EPS, ADAM_WD, ADAM_STEP = 0.001, 0.9, 0.999, 1e-08, 0.01, 10
N_CHIPS = 4
N_DEV = 8
LANES = 128
SUBLANES = 8
CONV_TAPS_PAD = 32
VMEM_LIMIT = 56 << 20
MESH = pl.DeviceIdType.MESH
ANY = pl.BlockSpec(memory_space=pl.ANY)
HBM = pl.BlockSpec(memory_space=pltpu.HBM)
SEMS = pl.BlockSpec(memory_space=pltpu.SEMAPHORE)
EFFECT = pltpu.SideEffectType.DATAFLOW_SIDE_EFFECTING
PERM = (0, 2, 1, 3)


def _cp(sem=None):
    return pltpu.CompilerParams(dimension_semantics=sem, vmem_limit_bytes=VMEM_LIMIT)


def _tile(dim, pref, mult=SUBLANES):
    if dim <= pref:
        return dim
    t = (pref // mult) * mult
    while t > mult and dim % t:
        t -= mult
    assert dim % t == 0, (dim, pref, mult)
    return t


def _perm_idx(q):
    return (q % 2) * 2 + q // 2


def _fold8(t):
    r, n = t.shape
    return t.reshape(r // SUBLANES, SUBLANES, n).sum(axis=0)


def _ln_rows(z, g, b):
    mu = jnp.mean(z, axis=-1, keepdims=True)
    xc = z - mu
    var = jnp.mean(xc * xc, axis=-1, keepdims=True)
    rstd = lax.rsqrt(var + LN_EPS)
    xh = xc * rstd
    return xh * g + b, xh, rstd


def _ln_bwd_rows(dy, xh, rstd, g):
    dxh = dy * g
    m1 = jnp.mean(dxh, axis=-1, keepdims=True)
    m2 = jnp.mean(dxh * xh, axis=-1, keepdims=True)
    return rstd * (dxh - m1 - xh * m2)


def _sigmoid(v):
    return 0.5 * jnp.tanh(0.5 * v) + 0.5


def _gelu_parts(p):
    cdf = 0.5 * (1.0 + lax.erf(p * 0.7071067811865476))
    pdf = jnp.exp(-0.5 * p * p) * 0.3989422804014327
    return p * cdf, cdf + p * pdf


def _shift_down(prev8, t, s):
    ext = jnp.concatenate([prev8, t], axis=0)
    return pltpu.roll(ext, s, 0)[SUBLANES:]


def _shift_up(t, next8, s):
    n = t.shape[0]
    ext = jnp.concatenate([t, next8], axis=0)
    return pltpu.roll(ext, n + SUBLANES - s, 0)[:n]


def _mm(a, b, *, ta=False, tb=False, bl=None, bias=None, res=None, res_scale=1.0, out_dtype=F32,
        tm, tn, tk, name, pieces=None, deps=None, n_outer=False):
    M, K = (a.shape[1], a.shape[0]) if ta else a.shape
    bs = b.shape[1:] if bl is not None else b.shape
    N, Kb = (bs[0], bs[1]) if tb else (bs[1], bs[0])
    assert K == Kb and M % tm == 0 and N % tn == 0 and K % tk == 0, (a.shape, b.shape, tm, tn, tk)
    gm, gn, gk = M // tm, N // tn, K // tk

    def spec(block, imap):
        if n_outer:
            return pl.BlockSpec(block, lambda j, i, k: imap(i, j, k))
        return pl.BlockSpec(block, imap)

    a_spec = spec((tk, tm), lambda i, j, k: (k, i)) if ta else spec((tm, tk), lambda i, j, k: (i, k))
    bblk = (tn, tk) if tb else (tk, tn)
    bmap = (lambda i, j, k: (j, k)) if tb else (lambda i, j, k: (k, j))
    if bl is not None:
        b_spec = spec((None,) + bblk, lambda i, j, k: (bl,) + bmap(i, j, k))
    else:
        b_spec = spec(bblk, bmap)
    in_specs, operands = [a_spec, b_spec], [a, b]
    if bias is not None:
        in_specs.append(spec((1, tn), lambda i, j, k: (0, j)))
        operands.append(bias)
    if res is not None:
        in_specs.append(spec((tm, tn), lambda i, j, k: (i, j)))
        operands.append(res)
    n_dep = len(deps) if deps else 0
    if n_dep:
        in_specs += [ANY] * n_dep
        operands += deps
        del deps[:]
    if pieces is None:
        out_shape = jax.ShapeDtypeStruct((M, N), out_dtype)
        out_spec = spec((tm, tn), lambda i, j, k: (i, j))
        ppb = pr = None
    elif pieces[0] == 'col':
        pr, pc = M // 2, N // N_CHIPS
        assert tm % pr == 0 and pc % tn == 0
        ppb, per = tm // pr, pc // tn
        perm = pieces[1]
        out_shape = jax.ShapeDtypeStruct((N_DEV, pr, pc), out_dtype)
        out_spec = spec(
            (ppb, pr, tn),
            lambda i, j, k: ((2 * (_perm_idx(j // per) if perm else j // per)) // ppb + i, 0, j % per))
    else:
        pr = M // N_DEV
        assert tm % pr == 0
        ppb = tm // pr
        out_shape = jax.ShapeDtypeStruct((N_DEV, pr, N), out_dtype)
        out_spec = spec((ppb, pr, tn), lambda i, j, k: (i, 0, j))
    dims = (((0 if ta else 1,), (1 if tb else 0,)), ((), ()))

    def body(*refs):
        a_ref, b_ref = refs[0], refs[1]
        pos = 2
        bias_ref = res_ref = None
        if bias is not None:
            bias_ref = refs[pos]
            pos += 1
        if res is not None:
            res_ref = refs[pos]
            pos += 1
        pos += n_dep
        o_ref = refs[pos]

        def finish(r):
            if bias_ref is not None:
                r = r + bias_ref[...]
            if res_ref is not None:
                r = r + res_scale * res_ref[...]
            if pieces is not None:
                r = r.reshape(ppb, pr, tn)
            o_ref[...] = r.astype(out_dtype)

        part = lax.dot_general(a_ref[...].astype(_MXU), b_ref[...].astype(_MXU), dims, preferred_element_type=F32)
        if gk == 1:
            finish(part)
            return
        acc_ref = refs[pos + 1]
        k = pl.program_id(2)

        @pl.when(k == 0)
        def _():
            acc_ref[...] = part

        @pl.when((k > 0) & (k < gk - 1))
        def _():
            acc_ref[...] += part

        @pl.when(k == gk - 1)
        def _():
            finish(acc_ref[...] + part)

    return pl.pallas_call(
        body, name=name, grid=(gn, gm, gk) if n_outer else (gm, gn, gk), in_specs=in_specs, out_specs=out_spec,
        out_shape=out_shape, scratch_shapes=[pltpu.VMEM((tm, tn), F32)] if gk > 1 else [],
        compiler_params=_cp(("parallel", "parallel", "arbitrary")),
    )(*operands)


def _mm_ln_bwd(a, w, res, res_scale, xh, rstd, g, *, name, deps=None):
    T, K = a.shape
    D = w.shape[1]
    tm = _tile(T, 512)
    n_dep = len(deps) if deps else 0

    def body(a_ref, w_ref, res_ref, xh_ref, rs_ref, g_ref, *rest):
        dz_ref, dzb_ref, dg_ref, db_ref, cs_ref = rest[n_dep:]

        @pl.when(pl.program_id(0) == 0)
        def _():
            dg_ref[...] = jnp.zeros_like(dg_ref)
            db_ref[...] = jnp.zeros_like(db_ref)
            cs_ref[...] = jnp.zeros_like(cs_ref)

        d = lax.dot_general(a_ref[...].astype(_MXU), w_ref[...].astype(_MXU), (((1,), (1,)), ((), ())),
                            preferred_element_type=F32) + res_scale * res_ref[...]
        xh = xh_ref[...].astype(F32)
        dz = _ln_bwd_rows(d, xh, rs_ref[...], g_ref[...])
        dz_ref[...] = dz
        dzb_ref[...] = dz.astype(_MXU)
        dg_ref[...] += _fold8(d * xh)
        db_ref[...] += _fold8(d)
        cs_ref[...] += _fold8(dz)

    row = lambda i: (i, 0)
    fixed = lambda i: (0, 0)
    tile = pl.BlockSpec((tm, D), row)
    part = pl.BlockSpec((SUBLANES, D), fixed)
    operands = [a, w, res, xh, rstd, g] + (list(deps) if deps else [])
    if deps:
        del deps[:]
    return pl.pallas_call(
        body, name=name, grid=(T // tm,),
        in_specs=[pl.BlockSpec((tm, K), row),
                  pl.BlockSpec((None, D, K), lambda i: (0, 0, 0), pipeline_mode=pl.Buffered(1)),
                  tile, tile, pl.BlockSpec((tm, 1), row), pl.BlockSpec((1, D), fixed)] + [ANY] * n_dep,
        out_specs=[tile, tile, part, part, part],
        out_shape=[jax.ShapeDtypeStruct((T, D), F32), jax.ShapeDtypeStruct((T, D), _MXU)]
        + [jax.ShapeDtypeStruct((SUBLANES, D), F32)] * 3,
        compiler_params=_cp(("arbitrary",)),
    )(*operands)


def _out_ln(act, wo_ref, bias_ref, res_ref, alpha, g_ref, b_ref, y_ref, yb_ref, xh_ref, rs_ref):
    z = jnp.dot(act, wo_ref[...].astype(_MXU), preferred_element_type=F32) + bias_ref[...] + alpha * res_ref[...]
    y, xh, rstd = _ln_rows(z, g_ref[...], b_ref[...])
    y_ref[...] = y
    yb_ref[...] = y.astype(_MXU)
    xh_ref[...] = xh.astype(_XDT)
    rs_ref[...] = rstd


def _conv_tail_fwd(v, gc, bc, w, bias, res, alpha, g, b, *, name):
    T, C = v.shape
    D = w.shape[-1]
    tm = _tile(T, 512)

    def body(v_ref, gc_ref, bc_ref, w_ref, bias_ref, res_ref, g_ref, b_ref,
             s_ref, xhc_ref, rsc_ref, y_ref, yb_ref, xh_ref, rs_ref):
        yv, xhc, rsc = _ln_rows(v_ref[...], gc_ref[...], bc_ref[...])
        s = (yv * _sigmoid(yv)).astype(_MXU)
        s_ref[...] = s
        xhc_ref[...] = xhc.astype(_XDT)
        rsc_ref[...] = rsc
        _out_ln(s, w_ref, bias_ref, res_ref, alpha, g_ref, b_ref, y_ref, yb_ref, xh_ref, rs_ref)

    row = lambda i: (i, 0)
    fixed = lambda i: (0, 0)
    vc, vd = pl.BlockSpec((1, C), fixed), pl.BlockSpec((1, D), fixed)
    tc_, td = pl.BlockSpec((tm, C), row), pl.BlockSpec((tm, D), row)
    one = pl.BlockSpec((tm, 1), row)
    return pl.pallas_call(
        body, name=name, grid=(T // tm,),
        in_specs=[tc_, vc, vc, _resident((None, C, D), lambda i: (0, 0, 0)), vd, td, vd, vd],
        out_specs=[tc_, tc_, one, td, td, td, one],
        out_shape=[jax.ShapeDtypeStruct((T, C), _MXU), jax.ShapeDtypeStruct((T, C), _XDT),
                   jax.ShapeDtypeStruct((T, 1), F32), jax.ShapeDtypeStruct((T, D), F32),
                   jax.ShapeDtypeStruct((T, D), _MXU), jax.ShapeDtypeStruct((T, D), _XDT),
                   jax.ShapeDtypeStruct((T, 1), F32)],
        compiler_params=_cp(("parallel",)),
    )(v, gc, bc, w, bias, res, g, b)


def _conv_cols(C, tc):
    per = (C // 2) // tc
    return per, (lambda j: (j // per) * (2 * per) + j % per)


def _glu_shifted(a_ref, g_ref, p_ref, S):
    u = a_ref[...].astype(F32) * _sigmoid(g_ref[...].astype(F32))
    rows = lax.broadcasted_iota(jnp.int32, (SUBLANES, u.shape[1]), 0)
    lo = CONV_TAPS_PAD
    for r in range(SUBLANES):
        p_ref[r, 0:lo, :] = jnp.zeros((lo, u.shape[1]), F32)
        if r == 0:
            p_ref[r, lo:lo + S, :] = u
        else:
            rolled = pltpu.roll(u, r, 0)
            p_ref[r, lo:lo + S, :] = rolled
            p_ref[r, lo:lo + SUBLANES, :] = jnp.where(rows >= r, rolled[0:SUBLANES], 0.0)


def _conv_fwd(h1, w_dw, b_dw, *, B, S, name):
    C = w_dw.shape[1]
    taps = CONV_TAPS_PAD - 1
    tc = LANES
    ch = _tile(S, 128)
    per, col_a = _conv_cols(C, tc)

    def body(a_ref, g_ref, w_ref, b_ref, o_ref, p_ref):
        _glu_shifted(a_ref, g_ref, p_ref, S)

        def chunk(ci, carry):
            base = pl.multiple_of(ci * ch, ch)
            acc = jnp.zeros((ch, tc), F32) + b_ref[...]
            for k in range(taps):
                q, r = divmod(taps - 1 - k, SUBLANES)
                start = pl.multiple_of(base + (CONV_TAPS_PAD - SUBLANES * q), SUBLANES)
                acc = acc + w_ref[pl.ds(k, 1), :] * p_ref[r, pl.ds(start, ch), :]
            o_ref[pl.ds(base, ch), :] = acc
            return carry

        lax.fori_loop(0, S // ch, chunk, 0)

    return pl.pallas_call(
        body, name=name, grid=(B, C // tc),
        in_specs=[pl.BlockSpec((S, tc), lambda b, j: (b, col_a(j))),
                  pl.BlockSpec((S, tc), lambda b, j: (b, col_a(j) + per)),
                  pl.BlockSpec((CONV_TAPS_PAD, tc), lambda b, j: (0, j)),
                  pl.BlockSpec((1, tc), lambda b, j: (0, j))],
        out_specs=pl.BlockSpec((S, tc), lambda b, j: (b, j)),
        out_shape=jax.ShapeDtypeStruct((B * S, C), F32),
        scratch_shapes=[pltpu.VMEM((SUBLANES, S + CONV_TAPS_PAD, tc), F32)],
        compiler_params=_cp(("parallel", "parallel")),
    )(h1, h1, w_dw, b_dw)


def _conv_bwd(dd, h1, w_dw, *, B, S, name):
    C = w_dw.shape[1]
    taps = CONV_TAPS_PAD - 1
    tc = LANES
    ch = _tile(S, 128)
    per, col_a = _conv_cols(C, tc)

    def body(d_ref, a_ref, g_ref, w_ref, du_ref, dw_ref, db_ref, p_ref, q_ref):
        b = pl.program_id(1)

        @pl.when(b == 0)
        def _():
            dw_ref[...] = jnp.zeros_like(dw_ref)
            db_ref[...] = jnp.zeros_like(db_ref)

        _glu_shifted(a_ref, g_ref, p_ref, S)
        d = d_ref[...]
        rows = lax.broadcasted_iota(jnp.int32, (SUBLANES, tc), 0)
        for r in range(SUBLANES):
            q_ref[r, S:S + CONV_TAPS_PAD, :] = jnp.zeros((CONV_TAPS_PAD, tc), F32)
            if r == 0:
                q_ref[r, 0:S, :] = d
            else:
                rolled = pltpu.roll(d, S - r, 0)
                q_ref[r, 0:S, :] = rolled
                q_ref[r, S - SUBLANES:S, :] = jnp.where(rows < SUBLANES - r, rolled[S - SUBLANES:S], 0.0)
        db_ref[...] += _fold8(d)

        def chunk(ci, carry):
            base = pl.multiple_of(ci * ch, ch)
            dch = d_ref[pl.ds(base, ch), :]
            acc = jnp.zeros((ch, tc), F32)
            for k in range(taps):
                q, r = divmod(taps - 1 - k, SUBLANES)
                up = pl.multiple_of(base + SUBLANES * q, SUBLANES)
                acc = acc + w_ref[pl.ds(k, 1), :] * q_ref[r, pl.ds(up, ch), :]
                down = pl.multiple_of(base + (CONV_TAPS_PAD - SUBLANES * q), SUBLANES)
                dw_ref[k] += _fold8(dch * p_ref[r, pl.ds(down, ch), :])
            du_ref[pl.ds(base, ch), :] = acc
            return carry

        lax.fori_loop(0, S // ch, chunk, 0)

    return pl.pallas_call(
        body, name=name, grid=(C // tc, B),
        in_specs=[pl.BlockSpec((S, tc), lambda j, b: (b, j)),
                  pl.BlockSpec((S, tc), lambda j, b: (b, col_a(j))),
                  pl.BlockSpec((S, tc), lambda j, b: (b, col_a(j) + per)),
                  pl.BlockSpec((CONV_TAPS_PAD, tc), lambda j, b: (0, j))],
        out_specs=[pl.BlockSpec((S, tc), lambda j, b: (b, j)),
                   pl.BlockSpec((CONV_TAPS_PAD, SUBLANES, tc), lambda j, b: (0, 0, j)),
                   pl.BlockSpec((SUBLANES, tc), lambda j, b: (0, j))],
        out_shape=[jax.ShapeDtypeStruct((B * S, C), F32),
                   jax.ShapeDtypeStruct((CONV_TAPS_PAD, SUBLANES, C), F32),
                   jax.ShapeDtypeStruct((SUBLANES, C), F32)],
        scratch_shapes=[pltpu.VMEM((SUBLANES, S + CONV_TAPS_PAD, tc), F32),
                        pltpu.VMEM((SUBLANES, S + CONV_TAPS_PAD, tc), F32)],
        compiler_params=_cp(("parallel", "arbitrary")),
    )(dd, h1, h1, w_dw)


def _ln_silu_bwd(dzb, w, xh, rstd, g, b, *, name):
    T, D = dzb.shape
    C = w.shape[1]
    tm = _tile(T, 512)

    def body(dz_ref, w_ref, xh_ref, rs_ref, g_ref, b_ref, dv_ref, dg_ref, db_ref):
        @pl.when(pl.program_id(0) == 0)
        def _():
            dg_ref[...] = jnp.zeros_like(dg_ref)
            db_ref[...] = jnp.zeros_like(db_ref)

        ds = lax.dot_general(dz_ref[...].astype(_MXU), w_ref[...].astype(_MXU), (((1,), (1,)), ((), ())),
                             preferred_element_type=F32)
        xh = xh_ref[...].astype(F32)
        gam = g_ref[...]
        y = xh * gam + b_ref[...]
        sig = _sigmoid(y)
        dln = ds * (sig * (1.0 + y * (1.0 - sig)))
        dv_ref[...] = _ln_bwd_rows(dln, xh, rs_ref[...], gam)
        dg_ref[...] += _fold8(dln * xh)
        db_ref[...] += _fold8(dln)

    row = lambda i: (i, 0)
    fixed = lambda i: (0, 0)
    vec = pl.BlockSpec((1, C), fixed)
    part = pl.BlockSpec((SUBLANES, C), fixed)
    return pl.pallas_call(
        body, name=name, grid=(T // tm,),
        in_specs=[pl.BlockSpec((tm, D), row), _resident((None, C, D), lambda i: (0, 0, 0)),
                  pl.BlockSpec((tm, C), row), pl.BlockSpec((tm, 1), row), vec, vec],
        out_specs=[pl.BlockSpec((tm, C), row), part, part],
        out_shape=[jax.ShapeDtypeStruct((T, C), F32)] + [jax.ShapeDtypeStruct((SUBLANES, C), F32)] * 2,
        compiler_params=_cp(("arbitrary",)),
    )(dzb, w, xh, rstd, g, b)


def _glu_bwd(du, h1, *, name):
    T, C = du.shape
    il = C // 2
    tm = _tile(T, 512)

    def body(du_ref, h_ref, dh_ref, cs_ref):
        @pl.when(pl.program_id(0) == 0)
        def _():
            cs_ref[...] = jnp.zeros_like(cs_ref)

        for hb in range(2):
            a = h_ref[:, 2 * hb * il:(2 * hb + 1) * il].astype(F32)
            gate = h_ref[:, (2 * hb + 1) * il:(2 * hb + 2) * il].astype(F32)
            d = du_ref[:, hb * il:(hb + 1) * il]
            sig = _sigmoid(gate)
            da = d * sig
            dgate = d * a * sig * (1.0 - sig)
            dh_ref[:, 2 * hb * il:(2 * hb + 1) * il] = da.astype(_MXU)
            dh_ref[:, (2 * hb + 1) * il:(2 * hb + 2) * il] = dgate.astype(_MXU)
            cs_ref[:, 2 * hb * il:(2 * hb + 1) * il] += _fold8(da)
            cs_ref[:, (2 * hb + 1) * il:(2 * hb + 2) * il] += _fold8(dgate)

    row = lambda i: (i, 0)
    return pl.pallas_call(
        body, name=name, grid=(T // tm,),
        in_specs=[pl.BlockSpec((tm, C), row), pl.BlockSpec((tm, 2 * C), row)],
        out_specs=[pl.BlockSpec((tm, 2 * C), row), pl.BlockSpec((SUBLANES, 2 * C), lambda i: (0, 0))],
        out_shape=[jax.ShapeDtypeStruct((T, 2 * C), _MXU), jax.ShapeDtypeStruct((SUBLANES, 2 * C), F32)],
        compiler_params=_cp(("arbitrary",)),
    )(du, h1)


def _tril_mask(n):
    return lax.broadcasted_iota(jnp.int32, (n, n), 0) >= lax.broadcasted_iota(jnp.int32, (n, n), 1)


def _split_uv(t, il):
    u = jnp.concatenate([t[:, 0:il], t[:, 2 * il:3 * il]], axis=1)
    v = jnp.concatenate([t[:, il:2 * il], t[:, 3 * il:4 * il]], axis=1)
    return u, v


def _gmlp_gate_fwd(p, g, b, w_s, bsb, w_out, bias, res, alpha, g1, b1, *, name):
    T, C2 = p.shape
    C = C2 // 2
    D = w_out.shape[-1]
    il = C // 2
    G, L, _ = w_s.shape
    assert G * L == C
    tm = _tile(T, 4 * L, L)

    def body(p_ref, g_ref, b_ref, ws_ref, bs_ref, wo_ref, bias_ref, res_ref, g1_ref, b1_ref,
             us_ref, xh_ref, rs_ref, y_ref, yb_ref, xh1_ref, rs1_ref, vn_ref, u_ref):
        z, _ = _gelu_parts(p_ref[...].astype(F32))
        u, v = _split_uv(z, il)
        vn, xh, rstd = _ln_rows(v, g_ref[...], b_ref[...])
        xh_ref[...] = xh.astype(_XDT)
        rs_ref[...] = rstd
        vn_ref[...] = vn.astype(_MXU)
        u_ref[...] = u
        mask = _tril_mask(L)
        for gi in range(G):
            wc = jnp.where(mask, ws_ref[gi], 0.0).astype(_MXU)
            cols = slice(gi * L, (gi + 1) * L)
            for c in range(tm // L):
                rows = slice(c * L, (c + 1) * L)
                s = jnp.dot(wc, vn_ref[rows, cols], preferred_element_type=F32) + bs_ref[:, cols]
                us_ref[rows, cols] = (u_ref[rows, cols] * s).astype(_MXU)
        _out_ln(us_ref[...], wo_ref, bias_ref, res_ref, alpha, g1_ref, b1_ref, y_ref, yb_ref, xh1_ref, rs1_ref)

    row = lambda i: (i, 0)
    fixed = lambda i: (0, 0)
    vd, td, one = pl.BlockSpec((1, D), fixed), pl.BlockSpec((tm, D), row), pl.BlockSpec((tm, 1), row)
    return pl.pallas_call(
        body, name=name, grid=(T // tm,),
        in_specs=[pl.BlockSpec((tm, C2), row), pl.BlockSpec((1, C), fixed), pl.BlockSpec((1, C), fixed),
                  pl.BlockSpec((G, L, L), lambda i: (0, 0, 0)), pl.BlockSpec((L, C), fixed),
                  _resident((None, C, D), lambda i: (0, 0, 0)), vd, td, vd, vd],
        out_specs=[pl.BlockSpec((tm, C), row), pl.BlockSpec((tm, C), row), one, td, td, td, one],
        out_shape=[jax.ShapeDtypeStruct((T, C), _MXU), jax.ShapeDtypeStruct((T, C), _XDT),
                   jax.ShapeDtypeStruct((T, 1), F32), jax.ShapeDtypeStruct((T, D), F32),
                   jax.ShapeDtypeStruct((T, D), _MXU), jax.ShapeDtypeStruct((T, D), _XDT),
                   jax.ShapeDtypeStruct((T, 1), F32)],
        scratch_shapes=[pltpu.VMEM((tm, C), _MXU), pltpu.VMEM((tm, C), F32)],
        compiler_params=_cp(("parallel",)),
    )(p, g, b, w_s, bsb, w_out, bias, res, g1, b1)


def _gmlp_gate_bwd(dzb, w_out, p, xh, rstd, g, b, w_s, bsb, *, name):
    T, C2 = p.shape
    D = dzb.shape[1]
    C = C2 // 2
    il = C // 2
    G, L, _ = w_s.shape
    tm = _tile(T, 4 * L, L)

    def body(dz_ref, wo_ref, p_ref, xh_ref, rs_ref, g_ref, b_ref, ws_ref, bs_ref,
             dp_ref, dg_ref, db_ref, cs_ref, dws_ref, dbs_ref, vn_ref, u_ref, dvn_ref, du_ref, dus_ref):
        @pl.when(pl.program_id(0) == 0)
        def _():
            dg_ref[...] = jnp.zeros_like(dg_ref)
            db_ref[...] = jnp.zeros_like(db_ref)
            cs_ref[...] = jnp.zeros_like(cs_ref)
            dws_ref[...] = jnp.zeros_like(dws_ref)
            dbs_ref[...] = jnp.zeros_like(dbs_ref)

        dus_ref[...] = lax.dot_general(dz_ref[...].astype(_MXU), wo_ref[...].astype(_MXU), (((1,), (1,)), ((), ())),
                                       preferred_element_type=F32)
        z, gp = _gelu_parts(p_ref[...].astype(F32))
        u, _ = _split_uv(z, il)
        xh = xh_ref[...].astype(F32)
        gam = g_ref[...]
        vn_ref[...] = (xh * gam + b_ref[...]).astype(_MXU)
        u_ref[...] = u
        mask = _tril_mask(L)
        for gi in range(G):
            wc = jnp.where(mask, ws_ref[gi], 0.0).astype(_MXU)
            cols = slice(gi * L, (gi + 1) * L)
            for c in range(tm // L):
                rows = slice(c * L, (c + 1) * L)
                vnb = vn_ref[rows, cols]
                s = jnp.dot(wc, vnb, preferred_element_type=F32) + bs_ref[:, cols]
                d = dus_ref[rows, cols]
                du_ref[rows, cols] = d * s
                ds = d * u_ref[rows, cols]
                dbs_ref[:, cols] += ds
                dsb = ds.astype(_MXU)
                dw = lax.dot_general(dsb, vnb, (((1,), (1,)), ((), ())), preferred_element_type=F32)
                dws_ref[gi] += jnp.where(mask, dw, 0.0)
                dvn_ref[rows, cols] = lax.dot_general(wc, dsb, (((0,), (0,)), ((), ())), preferred_element_type=F32)
        dvn = dvn_ref[...]
        dg_ref[...] += _fold8(dvn * xh)
        db_ref[...] += _fold8(dvn)
        dv = _ln_bwd_rows(dvn, xh, rs_ref[...], gam)
        du = du_ref[...]
        for hb in range(2):
            for part, src in ((0, du), (1, dv)):
                lo = (2 * hb + part) * il
                dp = src[:, hb * il:(hb + 1) * il] * gp[:, lo:lo + il]
                dp_ref[:, lo:lo + il] = dp.astype(_MXU)
                cs_ref[:, lo:lo + il] += _fold8(dp)

    row = lambda i: (i, 0)
    fixed = lambda i: (0, 0)
    part_c = pl.BlockSpec((SUBLANES, C), fixed)
    return pl.pallas_call(
        body, name=name, grid=(T // tm,),
        in_specs=[pl.BlockSpec((tm, D), row), _resident((None, C, D), lambda i: (0, 0, 0)),
                  pl.BlockSpec((tm, C2), row), pl.BlockSpec((tm, C), row),
                  pl.BlockSpec((tm, 1), row), pl.BlockSpec((1, C), fixed), pl.BlockSpec((1, C), fixed),
                  pl.BlockSpec((G, L, L), lambda i: (0, 0, 0)), pl.BlockSpec((L, C), fixed)],
        out_specs=[pl.BlockSpec((tm, C2), row), part_c, part_c, pl.BlockSpec((SUBLANES, C2), fixed),
                   pl.BlockSpec((G, L, L), lambda i: (0, 0, 0)), pl.BlockSpec((L, C), fixed)],
        out_shape=[jax.ShapeDtypeStruct((T, C2), _MXU), jax.ShapeDtypeStruct((SUBLANES, C), F32),
                   jax.ShapeDtypeStruct((SUBLANES, C), F32), jax.ShapeDtypeStruct((SUBLANES, C2), F32),
                   jax.ShapeDtypeStruct((G, L, L), F32), jax.ShapeDtypeStruct((L, C), F32)],
        scratch_shapes=[pltpu.VMEM((tm, C), _MXU), pltpu.VMEM((tm, C), F32), pltpu.VMEM((tm, C), F32),
                        pltpu.VMEM((tm, C), F32), pltpu.VMEM((tm, C), F32)],
        compiler_params=_cp(("arbitrary",)),
    )(dzb, w_out, p, xh, rstd, g, b, w_s, bsb)


def _ffn_conv(h, prev8, w_ref, b_ref):
    h1 = _shift_down(prev8, h, 1)
    h2 = _shift_down(prev8, h, 2)
    return w_ref[pl.ds(2, 1), :] * h + w_ref[pl.ds(1, 1), :] * h1 + w_ref[pl.ds(0, 1), :] * h2 + b_ref[...]


def _resident(block, imap):
    return pl.BlockSpec(block, imap, pipeline_mode=pl.Buffered(1))


def _ffn_fwd_half(j, xb, w_up, w_down, b_up, w_dw, b_dw, *, S, name, prev=None, tail=None, head=None):
    T, D = xb.shape
    N = w_up.shape[-1]
    tn = N // N_CHIPS
    tm = _tile(S, 256)
    spt = S // tm
    last = prev is not None
    alpha = tail[1] if last else None

    def body(*refs):
        x_ref, wu_ref, wd_ref, bu_ref, wc_ref, bc_ref = refs[:6]
        if last:
            yp_ref, res_ref, bd_ref, g_ref, b_ref = refs[9:14]
            o = 14 if head is None else 15
            h_ref, hc_ref, f_ref, y_ref, yb_ref, xh_ref, rs_ref = refs[o:o + 7]
            carry_ref = refs[-1]
        else:
            h_ref, hc_ref, f_ref, yp_ref, carry_ref = refs[6:11]

        @pl.when(pl.program_id(0) % spt == 0)
        def _():
            carry_ref[...] = jnp.zeros_like(carry_ref)

        h = jnp.dot(x_ref[...].astype(_MXU), wu_ref[...].astype(_MXU), preferred_element_type=F32) + bu_ref[...]
        h_ref[...] = h.astype(_HDT)
        prev = carry_ref[...]
        for s in range(tn // LANES):
            pair = []
            for lo in (s * LANES, tn + s * LANES):
                hs_, p8 = h[:, lo:lo + LANES], prev[:, lo:lo + LANES]
                hc = (wc_ref[pl.ds(2, 1), lo:lo + LANES] * hs_
                      + wc_ref[pl.ds(1, 1), lo:lo + LANES] * _shift_down(p8, hs_, 1)
                      + wc_ref[pl.ds(0, 1), lo:lo + LANES] * _shift_down(p8, hs_, 2) + bc_ref[:, lo:lo + LANES])
                hc_ref[:, lo:lo + LANES] = hc.astype(_HDT)
                pair.append(hc)
            gte, val = pair
            f_ref[:, s * LANES:(s + 1) * LANES] = (gte * _sigmoid(gte) * val).astype(_MXU)
        carry_ref[...] = h[tm - SUBLANES:tm]
        y = jnp.dot(f_ref[...], wd_ref[...].astype(_MXU), preferred_element_type=F32)
        if not last:
            yp_ref[...] = y
            return
        z = y + yp_ref[...] + bd_ref[...] + alpha * res_ref[...]
        out, xh, rstd = _ln_rows(z, g_ref[...], b_ref[...])
        if head is None:
            y_ref[...] = out
            yb_ref[...] = out.astype(_MXU)
            xh_ref[...] = xh.astype(_XDT)
            rs_ref[...] = rstd
            return
        t_ref, cs_ref, ls_ref = refs[14], refs[o + 7], refs[o + 8]

        @pl.when(pl.program_id(0) == 0)
        def _():
            for acc in (xh_ref, rs_ref, cs_ref, ls_ref):
                acc[...] = jnp.zeros_like(acc)

        err = out - t_ref[...]
        d = err * (1.0 / D)
        dz = _ln_bwd_rows(d, xh, rstd, g_ref[...])
        y_ref[...] = dz
        yb_ref[...] = dz.astype(_MXU)
        xh_ref[...] += _fold8(d * xh)
        rs_ref[...] += _fold8(d)
        cs_ref[...] += _fold8(dz)
        ls_ref[...] += _fold8(err * err)

    row = lambda i: (i, 0)
    pair = lambda i: (0, j)
    vec = pl.BlockSpec((1, D), lambda i: (0, 0))
    tile = pl.BlockSpec((tm, D), row)
    in_specs = [tile, _resident((None, D, 2 * tn), lambda i: (0, 0, j)), _resident((None, tn, D), lambda i: (0, j, 0)),
                pl.BlockSpec((1, 2 * tn), pair), pl.BlockSpec((SUBLANES, 2 * tn), pair), pl.BlockSpec((1, 2 * tn), pair)]
    operands = [xb, w_up, w_down, b_up, w_dw, b_dw]
    wide = pl.BlockSpec((tm, 2 * tn), lambda i: (i, j))
    out_specs = [wide, wide, pl.BlockSpec((tm, tn), lambda i: (i, j))]
    out_shape = [jax.ShapeDtypeStruct((T, N), _HDT), jax.ShapeDtypeStruct((T, N), _HDT),
                 jax.ShapeDtypeStruct((T, N // 2), _MXU)]
    aliases = {}
    if last:
        res, _, b_down, g, b = tail
        in_specs += [ANY, ANY, ANY, tile, tile, vec, vec, vec]
        operands += list(prev) + [res, b_down, g, b]
        aliases = {6: 0, 7: 1, 8: 2}
        if head is None:
            out_specs += [tile, tile, tile, pl.BlockSpec((tm, 1), row)]
            out_shape += [jax.ShapeDtypeStruct((T, D), F32), jax.ShapeDtypeStruct((T, D), _MXU),
                          jax.ShapeDtypeStruct((T, D), _XDT), jax.ShapeDtypeStruct((T, 1), F32)]
        else:
            in_specs.append(tile)
            operands.append(head)
            part = pl.BlockSpec((SUBLANES, D), lambda i: (0, 0))
            out_specs += [tile, tile, part, part, part, part]
            out_shape += [jax.ShapeDtypeStruct((T, D), F32), jax.ShapeDtypeStruct((T, D), _MXU)] \
                + [jax.ShapeDtypeStruct((SUBLANES, D), F32)] * 4
    else:
        out_specs.append(tile)
        out_shape.append(jax.ShapeDtypeStruct((T, D), F32))
    return pl.pallas_call(
        body, name=name, grid=(T // tm,), in_specs=in_specs, out_specs=out_specs, out_shape=out_shape,
        input_output_aliases=aliases, scratch_shapes=[pltpu.VMEM((SUBLANES, 2 * tn), F32)],
        compiler_params=_cp(("arbitrary",)),
    )(*operands)


def _ffn_bwd_half(j, dzb, w_down, w_up, hs, hcs, w_dw, *, S, name, dz=None, alpha=None, prev=None, ln=None):
    T, D = dzb.shape
    N = hs.shape[1]
    tn = N // N_CHIPS
    tm = _tile(S, 256)
    spt = S // tm
    nt = T // tm
    last = prev is not None

    def body(*refs):
        dz_ref, wd_ref, wu_ref, h_ref, hc_ref, wc_ref = refs[:6]
        if last:
            dxp_ref, xh_ref, rs_ref, g_ref = refs[7:11]
            dh_ref, cs_ref, dw_ref, db_ref, dz1_ref, dz1b_ref, dg1_ref, db1_ref, cs1_ref, carry_ref = refs[11:21]
        else:
            dzf_ref = refs[6]
            dh_ref, cs_ref, dw_ref, db_ref, dxp_ref, carry_ref = refs[7:13]
        i = pl.program_id(0)
        ii = nt - 1 - i

        @pl.when(i == 0)
        def _():
            cs_ref[...] = jnp.zeros_like(cs_ref)
            dw_ref[...] = jnp.zeros_like(dw_ref)
            db_ref[...] = jnp.zeros_like(db_ref)
            if last:
                dg1_ref[...] = jnp.zeros_like(dg1_ref)
                db1_ref[...] = jnp.zeros_like(db1_ref)
                cs1_ref[...] = jnp.zeros_like(cs1_ref)

        df = lax.dot_general(dz_ref[...].astype(_MXU), wd_ref[...].astype(_MXU), (((1,), (1,)), ((), ())),
                             preferred_element_type=F32)
        h = h_ref[...].astype(F32)
        gte, val = hc_ref[:, :tn].astype(F32), hc_ref[:, tn:].astype(F32)
        sig = _sigmoid(gte)
        dval = df * (gte * sig)
        dg = df * val * (sig * (1.0 + gte * (1.0 - sig)))
        dhc = jnp.concatenate([dg, dval], axis=1)
        nxt = jnp.where((ii + 1) % spt == 0, 0.0, carry_ref[...])
        d1 = _shift_up(dhc, nxt, 1)
        d2 = _shift_up(dhc, nxt, 2)
        carry_ref[...] = dhc[0:SUBLANES]
        db_ref[...] += _fold8(dhc)
        dw_ref[2] += _fold8(dhc * h)
        dw_ref[1] += _fold8(d1 * h)
        dw_ref[0] += _fold8(d2 * h)
        dh = wc_ref[pl.ds(2, 1), :] * dhc + wc_ref[pl.ds(1, 1), :] * d1 + wc_ref[pl.ds(0, 1), :] * d2
        cs_ref[...] += _fold8(dh)
        dhb = dh.astype(_MXU)
        dh_ref[...] = dhb
        dx = lax.dot_general(dhb, wu_ref[...].astype(_MXU), (((1,), (1,)), ((), ())), preferred_element_type=F32)
        if not last:
            dxp_ref[...] = dx + alpha * dzf_ref[...]
            return
        d = dx + dxp_ref[...]
        xh = xh_ref[...].astype(F32)
        dz1 = _ln_bwd_rows(d, xh, rs_ref[...], g_ref[...])
        dz1_ref[...] = dz1
        dz1b_ref[...] = dz1.astype(_MXU)
        dg1_ref[...] += _fold8(d * xh)
        db1_ref[...] += _fold8(d)
        cs1_ref[...] += _fold8(dz1)

    rev = lambda i: (nt - 1 - i, 0)
    fixed = lambda i: (0, 0)
    pair = lambda i: (0, j)
    tile = pl.BlockSpec((tm, D), rev)
    wide = pl.BlockSpec((tm, 2 * tn), lambda i: (nt - 1 - i, j))
    part = pl.BlockSpec((SUBLANES, 2 * tn), fixed)
    in_specs = [tile, _resident((None, tn, D), lambda i: (0, j, 0)), _resident((None, D, 2 * tn), lambda i: (0, 0, j)),
                wide, wide, pl.BlockSpec((SUBLANES, 2 * tn), pair)]
    operands = [dzb, w_down, w_up, hs, hcs, w_dw]
    out_specs = [wide, part, pl.BlockSpec((3, SUBLANES, 2 * tn), lambda i: (0, 0, 0)), part]
    out_shape = [jax.ShapeDtypeStruct((T, N), _MXU), jax.ShapeDtypeStruct((SUBLANES, 2 * tn), F32),
                 jax.ShapeDtypeStruct((3, SUBLANES, 2 * tn), F32), jax.ShapeDtypeStruct((SUBLANES, 2 * tn), F32)]
    aliases = {}
    if last:
        xh, rstd, g = ln
        in_specs += [ANY, tile, tile, pl.BlockSpec((tm, 1), rev), pl.BlockSpec((1, D), fixed)]
        operands += [prev[0], prev[1], xh, rstd, g]
        aliases = {6: 0}
        out_specs += [tile, tile] + [pl.BlockSpec((SUBLANES, D), fixed)] * 3
        out_shape += [jax.ShapeDtypeStruct((T, D), F32), jax.ShapeDtypeStruct((T, D), _MXU)] \
            + [jax.ShapeDtypeStruct((SUBLANES, D), F32)] * 3
    else:
        in_specs.append(tile)
        operands.append(dz)
        out_specs.append(tile)
        out_shape.append(jax.ShapeDtypeStruct((T, D), F32))
    return pl.pallas_call(
        body, name=name, grid=(nt,), in_specs=in_specs, out_specs=out_specs, out_shape=out_shape,
        input_output_aliases=aliases, scratch_shapes=[pltpu.VMEM((SUBLANES, 2 * tn), F32)],
        compiler_params=_cp(("arbitrary",)),
    )(*operands)


def _sum_pieces(gs, rs, me, *, name):
    n = len(gs)
    _, pr, pc = gs[0].shape
    tr = _tile(pr, 128)

    def body(me_ref, *refs):
        o_ref = refs[2 * n]
        for l in range(n):
            total = refs[l][...].astype(F32)
            for s in range(N_DEV - 1):
                total = total + refs[n + l][s].astype(F32)
            o_ref[l] = total

    own = pl.BlockSpec((None, tr, pc), lambda i, me_ref: (me_ref[0], i, 0))
    got = pl.BlockSpec((N_DEV - 1, tr, pc), lambda i, me_ref: (0, i, 0))
    return pl.pallas_call(
        body, name=name,
        grid_spec=pltpu.PrefetchScalarGridSpec(
            num_scalar_prefetch=1, grid=(pr // tr,), in_specs=[own] * n + [got] * n,
            out_specs=pl.BlockSpec((n, tr, pc), lambda i, me_ref: (0, i, 0))),
        out_shape=jax.ShapeDtypeStruct((n, pr, pc), F32),
        compiler_params=_cp(("parallel",)),
    )(me, *gs, *rs)


def _adam_math(w, g, m, v):
    bc1 = 1.0 - ADAM_B1 ** ADAM_STEP
    bc2 = 1.0 - ADAM_B2 ** ADAM_STEP
    m = ADAM_B1 * m + (1.0 - ADAM_B1) * g
    v = ADAM_B2 * v + (1.0 - ADAM_B2) * (g * g)
    return -ADAM_LR * ((m / bc1) / (jnp.sqrt(v / bc2) + ADAM_EPS) + ADAM_WD * w), m, v


def _adam(w, g, m, v, *, name):
    R, C = w.shape
    tr = _tile(R, 256)

    def body(w_ref, g_ref, m_ref, v_ref, d_ref, mo_ref, vo_ref):
        d_ref[...], mo_ref[...], vo_ref[...] = _adam_math(w_ref[...], g_ref[...], m_ref[...], v_ref[...])

    spec = pl.BlockSpec((tr, C), lambda i: (i, 0))
    return pl.pallas_call(
        body, name=name, grid=(R // tr,), in_specs=[spec] * 4, out_specs=[spec] * 3,
        out_shape=[jax.ShapeDtypeStruct((R, C), F32)] * 3,
        compiler_params=_cp(("parallel",)),
    )(w, g, m, v)


def _adam_halves(w, own, got, m, v, core, *, name):
    L, R, C = w.shape
    rh = R // 2
    tr = _tile(rh, 256)
    nt = rh // tr

    def body(c_ref, w_ref, own_ref, got_ref, m_ref, v_ref, g_ref, d_ref, mo_ref, vo_ref):
        g = jnp.where(pl.program_id(1) == c_ref[0], own_ref[...], got_ref[...])
        g_ref[...] = g
        d_ref[...], mo_ref[...], vo_ref[...] = _adam_math(w_ref[...], g, m_ref[...], v_ref[...])

    full = pl.BlockSpec((None, tr, C), lambda l, h, t, c_ref: (l, h * nt + t, 0))
    half = pl.BlockSpec((None, tr, C), lambda l, h, t, c_ref: (l, t, 0))
    return pl.pallas_call(
        body, name=name,
        grid_spec=pltpu.PrefetchScalarGridSpec(
            num_scalar_prefetch=1, grid=(L, 2, nt), in_specs=[full, half, half, full, full], out_specs=[full] * 4),
        out_shape=[jax.ShapeDtypeStruct((L, R, C), F32)] * 4,
        compiler_params=_cp(("parallel", "parallel", "parallel")),
    )(core, w, own, got, m, v)


def _remote(src, dst, send, recv, dev):
    return pltpu.make_async_remote_copy(src_ref=src, dst_ref=dst, send_sem=send, recv_sem=recv,
                                        device_id=dev, device_id_type=MESH)


def _place_w(shard, pos, layer, *, axis, name):
    _, R, C = shard.shape
    tr = _tile(R, 512, 16)
    nt = R // tr
    if axis == 2:
        out_shape = (1, R, N_CHIPS * C)
        out_map = lambda t, q: (0, t, q[0])
    else:
        out_shape = (1, N_CHIPS * R, C)
        out_map = lambda t, q: (0, q[0] * nt + t, 0)

    def body(q_ref, s_ref, o_ref):
        o_ref[...] = s_ref[...].astype(_WIRE)

    return pl.pallas_call(
        body, name=name,
        grid_spec=pltpu.PrefetchScalarGridSpec(
            num_scalar_prefetch=1, grid=(nt,),
            in_specs=[pl.BlockSpec((None, tr, C), lambda t, q: (layer, t, 0))],
            out_specs=pl.BlockSpec((None, tr, C), out_map)),
        out_shape=jax.ShapeDtypeStruct(out_shape, _WIRE),
        compiler_params=_cp(("parallel",)),
    )(pos, shard)


def _ag_window(ref, kind, px, py, h):
    axis, perm = kind
    q = 2 * px + py
    if perm:
        q = _perm_idx(q)
    if axis == 2:
        R, C = ref.shape[1], ref.shape[2] // N_CHIPS
        rh = R // 2
        return ref.at[:, pl.ds(pl.multiple_of(h * rh, 16), rh), pl.ds(pl.multiple_of(q * C, LANES), C)]
    R = ref.shape[1] // N_CHIPS
    rh = R // 2
    return ref.at[:, pl.ds(pl.multiple_of(q * R + h * rh, 16), rh), :]


def _ag_ici_copies(refs, kinds, send, recv):
    x, y, c = lax.axis_index("x"), lax.axis_index("y"), lax.axis_index("c")
    chips = [(1 - x, y), (x, 1 - y), (1 - x, 1 - y)]
    sends, recvs = [], []
    for a, (ref, kind) in enumerate(zip(refs, kinds)):
        own = _ag_window(ref, kind, x, y, c)
        for i, (px, py) in enumerate(chips):
            k = 3 * a + i
            sends.append(_remote(own, own, send.at[k], recv.at[k], (px, py, c)))
            recvs.append(_remote(own, _ag_window(ref, kind, px, py, c), send.at[k], recv.at[k], (px, py, c)))
    return sends, recvs


def _ag_start(arrs, kinds, after, *, name, copies=_ag_ici_copies):
    n = len(arrs)

    def body(*refs):
        in_refs = refs[:n]
        send, recv = refs[n + len(after)], refs[n + len(after) + 1]
        token = refs[-1]
        sends, _ = copies(in_refs, kinds, send, recv)
        for cp in sends:
            cp.start()
        token[...] = jnp.zeros_like(token)

    sems = pltpu.SemaphoreType.DMA((3 * n,))
    out = pl.pallas_call(
        body, name=name,
        out_shape=(sems, sems) + tuple(pltpu.HBM(a.shape, a.dtype) for a in arrs)
        + (jax.ShapeDtypeStruct((SUBLANES, LANES), F32),),
        in_specs=(HBM,) * n + (ANY,) * len(after),
        out_specs=(SEMS, SEMS) + (HBM,) * n + (pl.BlockSpec(memory_space=pltpu.VMEM),),
        input_output_aliases={a: 2 + a for a in range(n)},
        compiler_params=pltpu.CompilerParams(has_side_effects=EFFECT),
    )(*[pltpu.with_memory_space_constraint(a, pltpu.HBM) for a in arrs], *after)
    return out[0], out[1], list(out[2:2 + n]), out[-1]


def _ag_wait(send, recv, arrs, kinds, after, *, name, copies=_ag_ici_copies):
    n = len(arrs)

    def body(*refs):
        in_refs = refs[:n]
        send, recv = refs[n], refs[n + 1]
        sends, recvs = copies(in_refs, kinds, send, recv)
        for cp in sends:
            cp.wait_send()
        for cp in recvs:
            cp.wait_recv()

    out = pl.pallas_call(
        body, name=name,
        out_shape=tuple(pltpu.HBM(a.shape, a.dtype) for a in arrs),
        in_specs=(HBM,) * n + (SEMS, SEMS) + (ANY,) * len(after), out_specs=(HBM,) * n,
        input_output_aliases={a: a for a in range(n)},
        compiler_params=pltpu.CompilerParams(has_side_effects=EFFECT),
    )(*arrs, send, recv, *after)
    return list(out)


def _ag_d2d_copies(refs, kinds, send, recv):
    x, y, c = lax.axis_index("x"), lax.axis_index("y"), lax.axis_index("c")
    chips = [(1 - x, y), (x, 1 - y), (1 - x, 1 - y)]
    sib = (x, y, 1 - c)
    sends, recvs = [], []
    for a, (ref, kind) in enumerate(zip(refs, kinds)):
        for i, (px, py) in enumerate(chips):
            k = 3 * a + i
            got = _ag_window(ref, kind, px, py, c)
            sends.append(_remote(got, got, send.at[k], recv.at[k], sib))
            recvs.append(_remote(got, _ag_window(ref, kind, px, py, 1 - c), send.at[k], recv.at[k], sib))
    return sends, recvs


def _flip(x, y, c, f):
    return ((1 - x) if f & 4 else x, (1 - y) if f & 2 else y, (1 - c) if f & 1 else c)


def _rs_copies(g_refs, land_refs, send, recv):
    x, y, c = lax.axis_index("x"), lax.axis_index("y"), lax.axis_index("c")
    cps = []
    for a, (g_ref, land_ref) in enumerate(zip(g_refs, land_refs)):
        for f in range(1, N_DEV):
            tx, ty, tcx = _flip(x, y, c, f)
            k = (N_DEV - 1) * a + f - 1
            cps.append(_remote(g_ref.at[4 * tx + 2 * ty + tcx], land_ref.at[f - 1], send.at[k], recv.at[k],
                               (tx, ty, tcx)))
    return cps


def _rs_start(gs, *, name):
    n = len(gs)
    lands = [lax.empty((N_DEV - 1,) + g.shape[1:], g.dtype) for g in gs]

    def body(*refs):
        send, recv, token = refs[2 * n], refs[2 * n + 1], refs[-1]
        for cp in _rs_copies(refs[:n], refs[n:2 * n], send, recv):
            cp.start()
        token[...] = jnp.zeros_like(token)

    sems = pltpu.SemaphoreType.DMA(((N_DEV - 1) * n,))
    thru = [pltpu.HBM(t.shape, t.dtype) for t in gs + lands]
    out = pl.pallas_call(
        body, name=name,
        out_shape=(sems, sems, *thru, jax.ShapeDtypeStruct((SUBLANES, LANES), F32)),
        in_specs=(HBM,) * (2 * n), out_specs=(SEMS, SEMS) + (HBM,) * (2 * n) + (pl.BlockSpec(memory_space=pltpu.VMEM),),
        input_output_aliases={a: 2 + a for a in range(2 * n)},
        compiler_params=pltpu.CompilerParams(has_side_effects=EFFECT),
    )(*[pltpu.with_memory_space_constraint(t, pltpu.HBM) for t in gs + lands])
    return out[0], out[1], list(out[2:2 + n]), list(out[2 + n:2 + 2 * n]), out[-1]


def _rs_wait(send, recv, gs, lands, after, *, name):
    n = len(gs)

    def body(*refs):
        cps = _rs_copies(refs[:n], refs[n:2 * n], refs[2 * n], refs[2 * n + 1])
        for cp in cps:
            cp.wait_send()
        for cp in cps:
            cp.wait_recv()

    out = pl.pallas_call(
        body, name=name,
        out_shape=tuple(pltpu.HBM(t.shape, t.dtype) for t in gs + lands),
        in_specs=(HBM,) * (2 * n) + (SEMS, SEMS, ANY), out_specs=(HBM,) * (2 * n),
        input_output_aliases={a: a for a in range(2 * n)},
        compiler_params=pltpu.CompilerParams(has_side_effects=EFFECT),
    )(*gs, *lands, send, recv, after)
    return list(out[:n]), list(out[n:])


def _pair_exchange(owns, *, name):
    n = len(owns)

    def body(*refs):
        send, recv = refs[2 * n], refs[2 * n + 1]
        x, y, c = lax.axis_index("x"), lax.axis_index("y"), lax.axis_index("c")
        cps = [_remote(refs[a], refs[n + a], send.at[a], recv.at[a], (x, y, 1 - c)) for a in range(n)]
        for cp in cps:
            cp.start()
        for cp in cps:
            cp.wait_recv()
        for cp in cps:
            cp.wait_send()

    return pl.pallas_call(
        body, name=name, in_specs=[ANY] * n, out_specs=[ANY] * n,
        out_shape=[jax.ShapeDtypeStruct(o.shape, o.dtype) for o in owns],
        scratch_shapes=[pltpu.SemaphoreType.DMA((n,)), pltpu.SemaphoreType.DMA((n,))],
    )(*owns)


def _allreduce_flat(vec, *, name):
    n = vec.shape[0]
    unit = N_DEV * SUBLANES * LANES
    npad = -(-n // unit) * unit
    rows = npad // (N_DEV * LANES)
    xin = jnp.pad(vec, (0, npad - n)).reshape(N_DEV, rows, LANES)

    def body(x_ref, y_ref, a_ref, send_a, recv_a, send_b, recv_b):
        x, y, c = lax.axis_index("x"), lax.axis_index("y"), lax.axis_index("c")
        me = 4 * x + 2 * y + c
        a_ref[me] = x_ref[me]
        sends, recvs = [], []
        for f in range(1, N_DEV):
            dev = _flip(x, y, c, f)
            t = 4 * dev[0] + 2 * dev[1] + dev[2]
            cp = _remote(x_ref.at[t], a_ref.at[me], send_a.at[f - 1], recv_a.at[f - 1], dev)
            cp.start()
            sends.append(cp)
            recvs.append(_remote(x_ref.at[me], a_ref.at[t], send_a.at[f - 1], recv_a.at[f - 1], dev))
        for cp in recvs:
            cp.wait_recv()
        for cp in sends:
            cp.wait_send()
        acc = a_ref[0]
        for s in range(1, N_DEV):
            acc = acc + a_ref[s]
        y_ref[me] = acc
        sends, recvs = [], []
        for f in range(1, N_DEV):
            dev = _flip(x, y, c, f)
            t = 4 * dev[0] + 2 * dev[1] + dev[2]
            cp = _remote(y_ref.at[me], y_ref.at[me], send_b.at[f - 1], recv_b.at[f - 1], dev)
            cp.start()
            sends.append(cp)
            recvs.append(_remote(y_ref.at[me], y_ref.at[t], send_b.at[f - 1], recv_b.at[f - 1], dev))
        for cp in recvs:
            cp.wait_recv()
        for cp in sends:
            cp.wait_send()

    vm = pl.BlockSpec(memory_space=pltpu.VMEM)
    out = pl.pallas_call(
        body, name=name, in_specs=[vm], out_specs=vm,
        out_shape=jax.ShapeDtypeStruct((N_DEV, rows, LANES), F32),
        scratch_shapes=[pltpu.VMEM((N_DEV, rows, LANES), F32)] + [pltpu.SemaphoreType.DMA((N_DEV - 1,))] * 4,
        compiler_params=_cp(),
    )(xin)
    return out.reshape(npad)[:n]


def _perm_cols(v, blocks=N_CHIPS):
    w = v.shape[-1] // blocks
    return jnp.concatenate([v[..., q * w:(q + 1) * w] for q in PERM], axis=-1)


def _pack(arrs):
    return jnp.concatenate([a.reshape(-1).astype(F32) for a in arrs])


def _unpack(flat, shapes):
    out, pos = [], 0
    for s in shapes:
        n = 1
        for d in s:
            n *= d
        out.append(flat[pos:pos + n].reshape(s))
        pos += n
    return out


def kernel(x, conv_w_in, conv_b_in, conv_w_dw, conv_b_dw, conv_ln_g, conv_ln_b, conv_w_out, conv_b_out, gmlp_w_in, gmlp_b_in, gmlp_ln_g, gmlp_ln_b, gmlp_w_s, gmlp_b_s, gmlp_w_out, gmlp_b_out, ffn_w_up, ffn_b_up, ffn_w_dw, ffn_b_dw, ffn_w_down, ffn_b_down, norm1_g, norm1_b, norm2_g, norm2_b, loss_target, m_conv_w_in, m_conv_b_in, m_conv_w_dw, m_conv_b_dw, m_conv_ln_g, m_conv_ln_b, m_conv_w_out, m_conv_b_out, m_gmlp_w_in, m_gmlp_b_in, m_gmlp_ln_g, m_gmlp_ln_b, m_gmlp_w_s, m_gmlp_b_s, m_gmlp_w_out, m_gmlp_b_out, m_ffn_w_up, m_ffn_b_up, m_ffn_w_dw, m_ffn_b_dw, m_ffn_w_down, m_ffn_b_down, m_norm1_g, m_norm1_b, m_norm2_g, m_norm2_b, v_conv_w_in, v_conv_b_in, v_conv_w_dw, v_conv_b_dw, v_conv_ln_g, v_conv_ln_b, v_conv_w_out, v_conv_b_out, v_gmlp_w_in, v_gmlp_b_in, v_gmlp_ln_g, v_gmlp_ln_b, v_gmlp_w_s, v_gmlp_b_s, v_gmlp_w_out, v_gmlp_b_out, v_ffn_w_up, v_ffn_b_up, v_ffn_w_dw, v_ffn_b_dw, v_ffn_w_down, v_ffn_b_down, v_norm1_g, v_norm1_b, v_norm2_g, v_norm2_b):
    P = dict(locals())
    WEIGHTS = ['conv_w_in', 'conv_b_in', 'conv_w_dw', 'conv_b_dw', 'conv_ln_g', 'conv_ln_b', 'conv_w_out',
               'conv_b_out', 'gmlp_w_in', 'gmlp_b_in', 'gmlp_ln_g', 'gmlp_ln_b', 'gmlp_w_s', 'gmlp_b_s',
               'gmlp_w_out', 'gmlp_b_out', 'ffn_w_up', 'ffn_b_up', 'ffn_w_dw', 'ffn_b_dw', 'ffn_w_down',
               'ffn_b_down', 'norm1_g', 'norm1_b', 'norm2_g', 'norm2_b']
    BIG = ['conv_w_in', 'conv_w_out', 'gmlp_w_in', 'gmlp_w_out', 'ffn_w_up', 'ffn_w_down']
    SMALL_SHARDED = {'conv_w_dw': 2, 'gmlp_b_in': 1, 'gmlp_ln_g': 1, 'gmlp_ln_b': 1, 'gmlp_b_out': 1, 'ffn_w_dw': 2}

    B, S, D = x.shape
    T = B * S
    depth = norm1_g.shape[0]
    alpha = (2.0 * depth) ** 0.25
    C = conv_w_out.shape[-1]
    F2 = ffn_b_up.shape[-1]
    G, L = gmlp_w_s.shape[1], gmlp_w_s.shape[2]
    xi, yi, ci = lax.axis_index("x"), lax.axis_index("y"), lax.axis_index("c")
    shard = 2 * xi + yi

    i32 = lambda v: jnp.reshape(v, (1,)).astype(jnp.int32)
    pos_plain, pos_perm = i32(shard), i32(_perm_idx(shard))
    me_id, core_id = i32(4 * xi + 2 * yi + ci), i32(ci)

    groups = []
    for i in range(depth):
        mix = 'conv' if i % 2 == 0 else 'gmlp'
        groups.append((f"{mix}{i // 2}", [(mix + '_w_in', i // 2, 2, True), (mix + '_w_out', i // 2, 1, False)]))
        groups.append((f"ffn{i}", [('ffn_w_up', i, 2, True), ('ffn_w_down', i, 1, False)]))
    sm_names = list(SMALL_SHARDED)
    sm_shapes = [P[n].shape for n in sm_names]
    mine = _pack([P[n] for n in sm_names]) * (ci == 0).astype(F32)
    buf = jnp.zeros((N_CHIPS, mine.shape[0]), F32)
    buf = lax.dynamic_update_slice(buf, mine[None], (shard, 0))
    gathered = _allreduce_flat(buf.reshape(-1), name="ag_small").reshape(N_CHIPS, -1)

    started, order = {}, [gathered]
    for gname, members in groups:
        placed = [_place_w(P[n], pos_perm if perm else pos_plain, l, axis=axis, name=f"place_{n}_{l}")
                  for n, l, axis, perm in members]
        kinds = [(axis, perm) for _, _, axis, perm in members]
        send, recv, arrs, token = _ag_start(placed, kinds, order, name=f"ag_start_{gname}")
        order = [token]
        started[gname] = (send, recv, arrs, kinds, [(n, l) for n, l, _, _ in members])
    wts = {}

    def landed_ici(gname, after):
        send, recv, arrs, kinds, keys = started[gname]
        arrs = _ag_wait(send, recv, arrs, kinds, after, name=f"ag_wait_{gname}")
        send, recv, arrs, _ = _ag_start(arrs, kinds, [], name=f"ag_fwd_start_{gname}", copies=_ag_d2d_copies)
        started[gname] = (send, recv, arrs, kinds, keys)

    def arrive(gname, after):
        send, recv, arrs, kinds, keys = started[gname]
        arrs = _ag_wait(send, recv, arrs, kinds, after, name=f"ag_fwd_wait_{gname}", copies=_ag_d2d_copies)
        wts.update(zip(keys, arrs))

    full = {}
    for n, parts in zip(sm_names, zip(*[_unpack(gathered[k], sm_shapes) for k in range(N_CHIPS)])):
        full[n] = jnp.concatenate(parts, axis=SMALL_SHARDED[n])
    for n in WEIGHTS:
        if n not in BIG and n not in full:
            full[n] = P[n]

    assert G * L == C, "a gMLP group must be as wide as a chunk is long"

    def row(v):
        return v.reshape(1, -1)

    def pad_rows(v, r):
        return jnp.pad(v, ((0, r - v.shape[0]), (0, 0)))

    xf = x.reshape(T, D)
    saved = []
    cur, cur_b = xf, xf.astype(_MXU)
    for i in range(depth):
        j = i // 2
        sv = {'x': cur, 'xb': cur_b}
        if i == 0:
            landed_ici(groups[0][0], order)
        arrive(groups[2 * i][0], [] if i == 0 else [cur_b])
        if i % 2 == 0:
            b_in = row(_perm_cols(full['conv_b_in'][j]))
            h1 = _mm(cur_b, wts['conv_w_in', j], bl=0, bias=b_in, tm=_tile(T, 1024), tn=_tile(2 * C, 1024, LANES),
                     tk=D, name=f"conv_in_{j}", n_outer=True, out_dtype=_ADT)
            wdw = pad_rows(full['conv_w_dw'][j], CONV_TAPS_PAD)
            dwo = _conv_fwd(h1, wdw, row(full['conv_b_dw'][j]), B=B, S=S, name=f"conv_dw_{j}")
            landed_ici(groups[2 * i + 1][0], [dwo])
            s_act, xhc, rsc, *y1 = _conv_tail_fwd(
                dwo, row(full['conv_ln_g'][j]), row(full['conv_ln_b'][j]), wts['conv_w_out', j],
                row(full['conv_b_out'][j]), cur, alpha, row(norm1_g[i]), row(norm1_b[i]), name=f"conv_out_ln_{j}")
            sv.update(h1=h1, wdw=wdw, act=s_act, xhc=xhc, rsc=rsc)
        else:
            b_in = row(_perm_cols(full['gmlp_b_in'][j]))
            pre = _mm(cur_b, wts['gmlp_w_in', j], bl=0, bias=b_in, tm=_tile(T, 1024), tn=_tile(2 * C, 1024, LANES),
                      tk=D, name=f"gmlp_in_{j}", n_outer=True, out_dtype=_ADT)
            bsb = jnp.repeat(gmlp_b_s[j].T, L, axis=1)
            landed_ici(groups[2 * i + 1][0], [pre])
            us, xhv, rsv, *y1 = _gmlp_gate_fwd(
                pre, row(full['gmlp_ln_g'][j]), row(full['gmlp_ln_b'][j]), gmlp_w_s[j], bsb, wts['gmlp_w_out', j],
                row(full['gmlp_b_out'][j]), cur, alpha, row(norm1_g[i]), row(norm1_b[i]), name=f"gmlp_gate_{j}")
            sv.update(pre=pre, bsb=bsb, act=us, xhv=xhv, rsv=rsv)
        x1, x1b, xh1, rs1 = y1
        arrive(groups[2 * i + 1][0], [x1b])
        wdw3 = pad_rows(_perm_cols(full['ffn_w_dw'][i]), SUBLANES)
        bdw3 = row(_perm_cols(ffn_b_dw[i]))
        ffn_in = (x1b, wts['ffn_w_up', i], wts['ffn_w_down', i], row(_perm_cols(ffn_b_up[i])), wdw3, bdw3)
        first = _ffn_fwd_half(0, *ffn_in, S=S, name=f"ffn_fwd_a_{i}")
        if i < depth - 1:
            landed_ici(groups[2 * i + 2][0], [first[3]])
        ffn_tail = (x1, alpha, row(ffn_b_down[i]), row(norm2_g[i]), row(norm2_b[i]))
        sv.update(x1=x1, x1b=x1b, xh1=xh1, rs1=rs1, wdw3=wdw3)
        if i < depth - 1:
            hs, hcs, f_act, cur, cur_b, xh2, rs2 = _ffn_fwd_half(1, *ffn_in, S=S, name=f"ffn_fwd_b_{i}", prev=first,
                                                                 tail=ffn_tail)
            sv.update(xh2=xh2, rs2=rs2)
        else:
            hs, hcs, f_act, *sv['head'] = _ffn_fwd_half(1, *ffn_in, S=S, name=f"ffn_fwd_b_{i}", prev=first,
                                                         tail=ffn_tail, head=loss_target.reshape(T, D))
        sv.update(hs=hs, hcs=hcs, f=f_act)
        saved.append(sv)

    sg = {n: [None] * full[n].shape[0] for n in WEIGHTS if n not in BIG}
    inflight = {n: [None] * P[n].shape[0] for n in BIG}
    deps = []
    dcur = None
    loss_part = None
    tk_t = _tile(T, 2048)

    ready = []

    def wgrad(n, l, a_, b_, **kw):
        tk = T if n.endswith('w_in') else tk_t
        ready.append((n, l, _mm(a_, b_, ta=True, out_dtype=_WIRE, tk=tk, name=f"{n}_dw_{l}", deps=deps, **kw)))
        launch(f"{n}_{l}")

    def launch(gname):
        send, recv, gs, lands, token = _rs_start([g for _, _, g in ready], name=f"rs_start_{gname}")
        group = {'name': gname, 'flight': (send, recv, gs, lands), 'landed': None}
        for a, (n, l, _) in enumerate(ready):
            inflight[n][l] = (group, a)
        del ready[:]
        deps.append(token)

    def landed(n, l):
        group, a = inflight[n][l]
        if group['landed'] is None:
            group['landed'] = _rs_wait(*group['flight'], dcur, name=f"rs_wait_{group['name']}")
        return group['landed'][0][a], group['landed'][1][a]

    for i in reversed(range(depth)):
        j = i // 2
        sv = saved[i]
        if i == depth - 1:
            dz2, dz2b, dg, db, cs, loss_part = sv['head']
        else:
            dz2, dz2b, dg, db, cs = dcur
        sg['norm2_g'][i], sg['norm2_b'][i], sg['ffn_b_down'][i] = dg.sum(0), db.sum(0), cs.sum(0)
        Fh = F2 // 2
        wgrad('ffn_w_down', i, sv['f'], dz2b, tm=Fh // 2, tn=_tile(D, 1024, LANES), pieces=('row',))
        ffn_in = (dz2b, wts['ffn_w_down', i], wts['ffn_w_up', i], sv['hs'], sv['hcs'], sv['wdw3'])
        dh0, csu0, dwd0, dbd0, dxp = _ffn_bwd_half(0, *ffn_in, S=S, name=f"ffn_bwd_a_{i}", dz=dz2, alpha=alpha)
        dh, csu1, dwd1, dbd1, dz1, dz1b, dg, db, cs = _ffn_bwd_half(
            1, *ffn_in, S=S, name=f"ffn_bwd_b_{i}", prev=(dh0, dxp), ln=(sv['xh1'], sv['rs1'], row(norm1_g[i])))
        sg['ffn_b_up'][i] = _perm_cols(jnp.concatenate([csu0.sum(0), csu1.sum(0)], axis=-1))
        sg['ffn_w_dw'][i] = _perm_cols(jnp.concatenate([dwd0.sum(1), dwd1.sum(1)], axis=-1))
        sg['ffn_b_dw'][i] = _perm_cols(jnp.concatenate([dbd0.sum(0), dbd1.sum(0)], axis=-1))
        wgrad('ffn_w_up', i, sv['x1b'], dh, tm=D, tn=F2 // N_CHIPS, pieces=('col', True))
        sg['norm1_g'][i], sg['norm1_b'][i] = dg.sum(0), db.sum(0)
        if i % 2 == 0:
            sg['conv_b_out'][j] = cs.sum(0)
            wgrad('conv_w_out', j, sv['act'], dz1b, tm=_tile(C, 1024), tn=_tile(D, 1024, LANES), pieces=('row',))
            ddw, dg, db = _ln_silu_bwd(dz1b, wts['conv_w_out', j], sv['xhc'], sv['rsc'], row(full['conv_ln_g'][j]),
                                       row(full['conv_ln_b'][j]), name=f"conv_ln_bwd_{j}")
            sg['conv_ln_g'][j], sg['conv_ln_b'][j] = dg.sum(0), db.sum(0)
            dglu, dwk, dbk = _conv_bwd(ddw, sv['h1'], sv['wdw'], B=B, S=S, name=f"conv_dw_bwd_{j}")
            sg['conv_w_dw'][j] = dwk.sum(1)[:conv_w_dw.shape[1]]
            sg['conv_b_dw'][j] = dbk.sum(0)
            dh1, csi = _glu_bwd(dglu, sv['h1'], name=f"conv_glu_bwd_{j}")
            sg['conv_b_in'][j] = _perm_cols(csi.sum(0))
            fam = 'conv_w_in'
        else:
            sg['gmlp_b_out'][j] = cs.sum(0)
            wgrad('gmlp_w_out', j, sv['act'], dz1b, tm=_tile(C, 1024), tn=_tile(D, 1024, LANES), pieces=('row',))
            dh1, dg, db, csi, dws, dbs = _gmlp_gate_bwd(dz1b, wts['gmlp_w_out', j], sv['pre'], sv['xhv'], sv['rsv'],
                                                        row(full['gmlp_ln_g'][j]), row(full['gmlp_ln_b'][j]),
                                                        gmlp_w_s[j], sv['bsb'], name=f"gmlp_gate_bwd_{j}")
            sg['gmlp_ln_g'][j], sg['gmlp_ln_b'][j] = dg.sum(0), db.sum(0)
            sg['gmlp_b_in'][j] = _perm_cols(csi.sum(0))
            sg['gmlp_w_s'][j] = dws
            sg['gmlp_b_s'][j] = dbs.reshape(L, G, L).sum(-1).T
            fam = 'gmlp_w_in'
        wgrad(fam, j, sv['xb'], dh1, tm=D, tn=(2 * C) // N_CHIPS, pieces=('col', True))
        if i > 0:
            below = saved[i - 1]
            dcur = _mm_ln_bwd(dh1, wts[fam, j], dz1, alpha, below['xh2'], below['rs2'], row(norm2_g[i - 1]),
                              name=f"{fam}_dx_{j}", deps=deps)
        else:
            dcur = _mm(dh1, wts[fam, j], bl=0, tb=True, res=dz1, res_scale=alpha, tm=_tile(T, 512),
                       tn=_tile(D, 1024, LANES), tk=2 * C, name=f"{fam}_dx_{j}", deps=deps)
    grad_x = dcur.reshape(B, S, D)

    small_names = [n for n in WEIGHTS if n not in BIG]
    small_full = [jnp.stack(sg[n]) for n in small_names]
    flat = _pack(small_full + [loss_part])
    red = _allreduce_flat(flat, name="ar_small")
    red_parts = _unpack(red, [a.shape for a in small_full] + [loss_part.shape])
    loss = (0.5 / D) * jnp.sum(red_parts[-1])
    grads = {}
    for n, g in zip(small_names, red_parts[:-1]):
        if n in SMALL_SHARDED:
            ax = SMALL_SHARDED[n]
            width = P[n].shape[ax]
            g = lax.dynamic_slice_in_dim(g, shard * width, width, axis=ax)
        grads[n] = g

    big_out = {}
    for n in ['ffn_w_down', 'ffn_w_up', 'gmlp_w_out', 'gmlp_w_in', 'conv_w_out', 'conv_w_in']:
        both = [landed(n, l) for l in range(len(inflight[n]))]
        own = _sum_pieces([g for g, _ in both], [r for _, r in both], me_id, name=f"sum_{n}")
        got, = _pair_exchange([own], name=f"px_{n}")
        big_out[n] = _adam_halves(P[n], own, got, P['m_' + n], P['v_' + n], core_id, name=f"adam_{n}")

    shapes = [P[n].shape for n in small_names]
    n_small = sum(functools.reduce(lambda p_, d_: p_ * d_, s_, 1) for s_ in shapes)
    unit = SUBLANES * LANES
    npad = -(-n_small // unit) * unit

    def flat2d(arrs, fill=0.0):
        v = _pack(arrs)
        return jnp.pad(v, (0, npad - n_small), constant_values=fill).reshape(-1, LANES)

    dl, mo, vo = _adam(flat2d([P[n] for n in small_names]), flat2d([grads[n] for n in small_names]),
                       flat2d([P['m_' + n] for n in small_names]),
                       flat2d([P['v_' + n] for n in small_names], fill=1.0), name="adam_small")
    small_out = {n: [grads[n], None, None, None] for n in small_names}
    for k, t in enumerate((dl, mo, vo)):
        for n, a in zip(small_names, _unpack(t.reshape(-1), shapes)):
            small_out[n][k + 1] = a

    outs = [loss, grad_x]
    for k in range(4):
        for n in WEIGHTS:
            outs.append(big_out[n][k] if n in BIG else small_out[n][k])
    return tuple(outs)
```
